```python
import jax, jax.numpy as jnp
from jax import lax
import numpy as np

D_MODEL = 1024
BATCH = 8
SEQ = 4096
DEPTH = 2

N_MIXERS = 2
N_ATTN_LAYERS = (DEPTH + 1) // 2
N_SGU_LAYERS = DEPTH // 2

HEAD_DIM = 64
N_Q_HEADS = D_MODEL // HEAD_DIM
N_KV_HEADS = N_Q_HEADS // 4
GQA_GROUP = N_Q_HEADS // N_KV_HEADS
WINDOW = 128
Q_BLOCK = WINDOW
ROPE_THETA = 10000.0
Q_WIDTH = N_Q_HEADS * HEAD_DIM
KV_WIDTH = N_KV_HEADS * HEAD_DIM
QKV_WIDTH = Q_WIDTH + 2 * KV_WIDTH

SGU_WIDTH = D_MODEL
SGU_GROUPS = 8
SGU_GROUP_DIM = SGU_WIDTH // SGU_GROUPS
SGU_CHUNK = 128

D_FF = ((8 * D_MODEL + 3 * 256 - 1) // (3 * 256)) * 256

EPS = 1e-6

kernel_name = "hybrid_swa_sink_gqa_chunked_sgu_swiglu"


def rmsnorm(x, g):
    xf = x.astype(jnp.float32)
    y = xf * lax.rsqrt(jnp.mean(xf * xf, axis=-1, keepdims=True) + EPS)
    return (y * g.astype(jnp.float32)).astype(x.dtype)


def layernorm(x, g, b):
    xf = x.astype(jnp.float32)
    mu = jnp.mean(xf, axis=-1, keepdims=True)
    var = jnp.mean(jnp.square(xf - mu), axis=-1, keepdims=True)
    y = (xf - mu) * lax.rsqrt(var + EPS)
    return (y * g.astype(jnp.float32) + b.astype(jnp.float32)).astype(x.dtype)


def rope(x, pos):
    half = HEAD_DIM // 2
    inv_freq = ROPE_THETA ** (-(jnp.arange(half, dtype=jnp.float32) * 2.0) / HEAD_DIM)
    ang = pos.astype(jnp.float32)[:, None] * inv_freq[None, :]
    cos = jnp.cos(ang)[None, :, None, :].astype(x.dtype)
    sin = jnp.sin(ang)[None, :, None, :].astype(x.dtype)
    x1, x2 = x[..., :half], x[..., half:]
    return jnp.concatenate([x1 * cos - x2 * sin, x2 * cos + x1 * sin], axis=-1)


def swa_sink_attention(h, w_qkv, b_qkv, sinks, w_o, b_o):
    B, S, _ = h.shape
    nb = S // Q_BLOCK
    T = Q_BLOCK
    qkv = h @ w_qkv + b_qkv
    q = qkv[..., :Q_WIDTH].reshape(B, S, N_Q_HEADS, HEAD_DIM)
    k = qkv[..., Q_WIDTH:Q_WIDTH + KV_WIDTH].reshape(B, S, N_KV_HEADS, HEAD_DIM)
    v = qkv[..., Q_WIDTH + KV_WIDTH:].reshape(B, S, N_KV_HEADS, HEAD_DIM)
    pos = jnp.arange(S, dtype=jnp.int32)
    q = rope(q, pos)
    k = rope(k, pos)
    qb = q.reshape(B, nb, T, N_KV_HEADS, GQA_GROUP, HEAD_DIM)
    kb = k.reshape(B, nb, T, N_KV_HEADS, HEAD_DIM)
    vb = v.reshape(B, nb, T, N_KV_HEADS, HEAD_DIM)
    pad = jnp.zeros_like(kb[:, :1])
    kk = jnp.concatenate([jnp.concatenate([pad, kb[:, :-1]], axis=1), kb], axis=2)
    vv = jnp.concatenate([jnp.concatenate([pad, vb[:, :-1]], axis=1), vb], axis=2)
    scale = HEAD_DIM ** -0.5
    s = jnp.einsum('bnqhgd,bnkhd->bnhgqk', qb, kk).astype(jnp.float32) * scale
    qpos = jnp.arange(T)[:, None] + T
    kpos = jnp.arange(2 * T)[None, :]
    band = (kpos <= qpos) & (qpos - kpos < WINDOW)
    blk = jnp.arange(nb)[:, None, None]
    valid = band[None] & ((blk * T + kpos[None] - T) >= 0)
    s = jnp.where(valid[None, :, None, None], s, -jnp.inf)
    sink = sinks.astype(jnp.float32).reshape(1, 1, N_KV_HEADS, GQA_GROUP, 1, 1)
    m = jnp.maximum(jnp.max(s, axis=-1, keepdims=True), sink)
    p = jnp.exp(s - m)
    denom = jnp.sum(p, axis=-1, keepdims=True) + jnp.exp(sink - m)
    probs = (p / denom).astype(vv.dtype)
    o = jnp.einsum('bnhgqk,bnkhd->bnqhgd', probs, vv).reshape(B, S, Q_WIDTH)
    return o @ w_o + b_o


def chunked_sgu(h, w_in, ln_g, ln_b, w_spatial, b_spatial, w_out):
    B, S, _ = h.shape
    nc = S // SGU_CHUNK
    z = jax.nn.gelu(h @ w_in)
    u, v = z[..., :SGU_WIDTH], z[..., SGU_WIDTH:]
    v = layernorm(v, ln_g, ln_b)
    vg = v.reshape(B, nc, SGU_CHUNK, SGU_GROUPS, SGU_GROUP_DIM)
    causal = jnp.tril(jnp.ones((SGU_CHUNK, SGU_CHUNK), dtype=w_spatial.dtype))
    ws = w_spatial * causal[None]
    mixed = jnp.einsum('gts,bnsgd->bntgd', ws, vg) + b_spatial.T[None, None, :, :, None]
    y = u * mixed.reshape(B, S, SGU_WIDTH)
    return y @ w_out


def swiglu(h, w_gate_up, w_down):
    gu = h @ w_gate_up
    return (jax.nn.silu(gu[..., :D_FF]) * gu[..., D_FF:]) @ w_down


def _fwd_setup_inputs(seed: int = 0) -> dict:
    key = jax.random.key(seed)
    ks = jax.random.split(key, 20)
    f32 = jnp.float32

    def nrm(k, shape, scale):
        return jax.random.normal(k, shape, f32) * scale

    def gain(k, shape):
        return 1.0 + 0.02 * jax.random.normal(k, shape, f32)

    NA, NS = N_ATTN_LAYERS, N_SGU_LAYERS
    return {
        "x": jax.random.normal(ks[0], (BATCH, SEQ, D_MODEL), f32),
        "norm_mix_pre": gain(ks[1], (DEPTH, D_MODEL)),
        "norm_mix_post": gain(ks[2], (DEPTH, D_MODEL)),
        "norm_ffn_pre": gain(ks[3], (DEPTH, D_MODEL)),
        "norm_ffn_post": gain(ks[4], (DEPTH, D_MODEL)),
        "attn_w_qkv": nrm(ks[5], (NA, D_MODEL, QKV_WIDTH), D_MODEL ** -0.5),
        "attn_b_qkv": nrm(ks[6], (NA, QKV_WIDTH), 0.02),
        "attn_sinks": nrm(ks[7], (NA, N_Q_HEADS), 1.0),
        "attn_w_o": nrm(ks[8], (NA, Q_WIDTH, D_MODEL), Q_WIDTH ** -0.5),
        "attn_b_o": nrm(ks[9], (NA, D_MODEL), 0.02),
        "sgu_w_in": nrm(ks[10], (NS, D_MODEL, 2 * SGU_WIDTH), D_MODEL ** -0.5),
        "sgu_ln_g": gain(ks[11], (NS, SGU_WIDTH)),
        "sgu_ln_b": nrm(ks[12], (NS, SGU_WIDTH), 0.02),
        "sgu_w_spatial": nrm(ks[13], (NS, SGU_GROUPS, SGU_CHUNK, SGU_CHUNK), SGU_CHUNK ** -0.5),
        "sgu_b_spatial": 1.0 + nrm(ks[14], (NS, SGU_GROUPS, SGU_CHUNK), 0.02),
        "sgu_w_out": nrm(ks[15], (NS, SGU_WIDTH, D_MODEL), SGU_WIDTH ** -0.5),
        "ffn_w_gate_up": nrm(ks[16], (DEPTH, D_MODEL, 2 * D_FF), D_MODEL ** -0.5),
        "ffn_w_down": nrm(ks[17], (DEPTH, D_FF, D_MODEL), D_FF ** -0.5),
    }


def _fwd_reference(x, norm_mix_pre, norm_mix_post, norm_ffn_pre, norm_ffn_post,
              attn_w_qkv, attn_b_qkv, attn_sinks, attn_w_o, attn_b_o,
              sgu_w_in, sgu_ln_g, sgu_ln_b, sgu_w_spatial, sgu_b_spatial, sgu_w_out,
              ffn_w_gate_up, ffn_w_down):
    for i in range(DEPTH):
        h = rmsnorm(x, norm_mix_pre[i])
        j = i // N_MIXERS
        if i % N_MIXERS == 0:
            m = swa_sink_attention(h, attn_w_qkv[j], attn_b_qkv[j], attn_sinks[j],
                                   attn_w_o[j], attn_b_o[j])
        else:
            m = chunked_sgu(h, sgu_w_in[j], sgu_ln_g[j], sgu_ln_b[j],
                            sgu_w_spatial[j], sgu_b_spatial[j], sgu_w_out[j])
        x = x + rmsnorm(m, norm_mix_post[i])
        h = rmsnorm(x, norm_ffn_pre[i])
        x = x + rmsnorm(swiglu(h, ffn_w_gate_up[i], ffn_w_down[i]), norm_ffn_post[i])
    return x


import jax as _jax
import jax.numpy as _jnp

TWIN_FORMAT = 'train_step'
FWD_PARAMS = ['x', 'norm_mix_pre', 'norm_mix_post', 'norm_ffn_pre', 'norm_ffn_post', 'attn_w_qkv', 'attn_b_qkv', 'attn_sinks', 'attn_w_o', 'attn_b_o', 'sgu_w_in', 'sgu_ln_g', 'sgu_ln_b', 'sgu_w_spatial', 'sgu_b_spatial', 'sgu_w_out', 'ffn_w_gate_up', 'ffn_w_down']
TWIN_WEIGHTS = ['norm_mix_pre', 'norm_mix_post', 'norm_ffn_pre', 'norm_ffn_post', 'attn_w_qkv', 'attn_b_qkv', 'attn_sinks', 'attn_w_o', 'attn_b_o', 'sgu_w_in', 'sgu_ln_g', 'sgu_ln_b', 'sgu_w_spatial', 'sgu_b_spatial', 'sgu_w_out', 'ffn_w_gate_up', 'ffn_w_down']
TWIN_DIFF_INPUT = 'x'
TWIN_INPUTS = ['x', 'norm_mix_pre', 'norm_mix_post', 'norm_ffn_pre', 'norm_ffn_post', 'attn_w_qkv', 'attn_b_qkv', 'attn_sinks', 'attn_w_o', 'attn_b_o', 'sgu_w_in', 'sgu_ln_g', 'sgu_ln_b', 'sgu_w_spatial', 'sgu_b_spatial', 'sgu_w_out', 'ffn_w_gate_up', 'ffn_w_down', 'loss_target', 'm_norm_mix_pre', 'm_norm_mix_post', 'm_norm_ffn_pre', 'm_norm_ffn_post', 'm_attn_w_qkv', 'm_attn_b_qkv', 'm_attn_sinks', 'm_attn_w_o', 'm_attn_b_o', 'm_sgu_w_in', 'm_sgu_ln_g', 'm_sgu_ln_b', 'm_sgu_w_spatial', 'm_sgu_b_spatial', 'm_sgu_w_out', 'm_ffn_w_gate_up', 'm_ffn_w_down', 'v_norm_mix_pre', 'v_norm_mix_post', 'v_norm_ffn_pre', 'v_norm_ffn_post', 'v_attn_w_qkv', 'v_attn_b_qkv', 'v_attn_sinks', 'v_attn_w_o', 'v_attn_b_o', 'v_sgu_w_in', 'v_sgu_ln_g', 'v_sgu_ln_b', 'v_sgu_w_spatial', 'v_sgu_b_spatial', 'v_sgu_w_out', 'v_ffn_w_gate_up', 'v_ffn_w_down']
TWIN_OUTPUTS = ['loss', 'grad_x', 'grad_norm_mix_pre', 'grad_norm_mix_post', 'grad_norm_ffn_pre', 'grad_norm_ffn_post', 'grad_attn_w_qkv', 'grad_attn_b_qkv', 'grad_attn_sinks', 'grad_attn_w_o', 'grad_attn_b_o', 'grad_sgu_w_in', 'grad_sgu_ln_g', 'grad_sgu_ln_b', 'grad_sgu_w_spatial', 'grad_sgu_b_spatial', 'grad_sgu_w_out', 'grad_ffn_w_gate_up', 'grad_ffn_w_down', 'delta_norm_mix_pre', 'delta_norm_mix_post', 'delta_norm_ffn_pre', 'delta_norm_ffn_post', 'delta_attn_w_qkv', 'delta_attn_b_qkv', 'delta_attn_sinks', 'delta_attn_w_o', 'delta_attn_b_o', 'delta_sgu_w_in', 'delta_sgu_ln_g', 'delta_sgu_ln_b', 'delta_sgu_w_spatial', 'delta_sgu_b_spatial', 'delta_sgu_w_out', 'delta_ffn_w_gate_up', 'delta_ffn_w_down', 'new_m_norm_mix_pre', 'new_m_norm_mix_post', 'new_m_norm_ffn_pre', 'new_m_norm_ffn_post', 'new_m_attn_w_qkv', 'new_m_attn_b_qkv', 'new_m_attn_sinks', 'new_m_attn_w_o', 'new_m_attn_b_o', 'new_m_sgu_w_in', 'new_m_sgu_ln_g', 'new_m_sgu_ln_b', 'new_m_sgu_w_spatial', 'new_m_sgu_b_spatial', 'new_m_sgu_w_out', 'new_m_ffn_w_gate_up', 'new_m_ffn_w_down', 'new_v_norm_mix_pre', 'new_v_norm_mix_post', 'new_v_norm_ffn_pre', 'new_v_norm_ffn_post', 'new_v_attn_w_qkv', 'new_v_attn_b_qkv', 'new_v_attn_sinks', 'new_v_attn_w_o', 'new_v_attn_b_o', 'new_v_sgu_w_in', 'new_v_sgu_ln_g', 'new_v_sgu_ln_b', 'new_v_sgu_w_spatial', 'new_v_sgu_b_spatial', 'new_v_sgu_w_out', 'new_v_ffn_w_gate_up', 'new_v_ffn_w_down']
TWIN_LEAF_KINDS = {'loss': 'loss', 'grad_x': 'grad_x', 'grad_norm_mix_pre': 'grad_w', 'grad_norm_mix_post': 'grad_w', 'grad_norm_ffn_pre': 'grad_w', 'grad_norm_ffn_post': 'grad_w', 'grad_attn_w_qkv': 'grad_w', 'grad_attn_b_qkv': 'grad_w', 'grad_attn_sinks': 'grad_w', 'grad_attn_w_o': 'grad_w', 'grad_attn_b_o': 'grad_w', 'grad_sgu_w_in': 'grad_w', 'grad_sgu_ln_g': 'grad_w', 'grad_sgu_ln_b': 'grad_w', 'grad_sgu_w_spatial': 'grad_w', 'grad_sgu_b_spatial': 'grad_w', 'grad_sgu_w_out': 'grad_w', 'grad_ffn_w_gate_up': 'grad_w', 'grad_ffn_w_down': 'grad_w', 'delta_norm_mix_pre': 'delta_w', 'delta_norm_mix_post': 'delta_w', 'delta_norm_ffn_pre': 'delta_w', 'delta_norm_ffn_post': 'delta_w', 'delta_attn_w_qkv': 'delta_w', 'delta_attn_b_qkv': 'delta_w', 'delta_attn_sinks': 'delta_w', 'delta_attn_w_o': 'delta_w', 'delta_attn_b_o': 'delta_w', 'delta_sgu_w_in': 'delta_w', 'delta_sgu_ln_g': 'delta_w', 'delta_sgu_ln_b': 'delta_w', 'delta_sgu_w_spatial': 'delta_w', 'delta_sgu_b_spatial': 'delta_w', 'delta_sgu_w_out': 'delta_w', 'delta_ffn_w_gate_up': 'delta_w', 'delta_ffn_w_down': 'delta_w', 'new_m_norm_mix_pre': 'new_m', 'new_m_norm_mix_post': 'new_m', 'new_m_norm_ffn_pre': 'new_m', 'new_m_norm_ffn_post': 'new_m', 'new_m_attn_w_qkv': 'new_m', 'new_m_attn_b_qkv': 'new_m', 'new_m_attn_sinks': 'new_m', 'new_m_attn_w_o': 'new_m', 'new_m_attn_b_o': 'new_m', 'new_m_sgu_w_in': 'new_m', 'new_m_sgu_ln_g': 'new_m', 'new_m_sgu_ln_b': 'new_m', 'new_m_sgu_w_spatial': 'new_m', 'new_m_sgu_b_spatial': 'new_m', 'new_m_sgu_w_out': 'new_m', 'new_m_ffn_w_gate_up': 'new_m', 'new_m_ffn_w_down': 'new_m', 'new_v_norm_mix_pre': 'new_v', 'new_v_norm_mix_post': 'new_v', 'new_v_norm_ffn_pre': 'new_v', 'new_v_norm_ffn_post': 'new_v', 'new_v_attn_w_qkv': 'new_v', 'new_v_attn_b_qkv': 'new_v', 'new_v_attn_sinks': 'new_v', 'new_v_attn_w_o': 'new_v', 'new_v_attn_b_o': 'new_v', 'new_v_sgu_w_in': 'new_v', 'new_v_sgu_ln_g': 'new_v', 'new_v_sgu_ln_b': 'new_v', 'new_v_sgu_w_spatial': 'new_v', 'new_v_sgu_b_spatial': 'new_v', 'new_v_sgu_w_out': 'new_v', 'new_v_ffn_w_gate_up': 'new_v', 'new_v_ffn_w_down': 'new_v'}


def _forward(args):
    return _fwd_reference(*[args[k] for k in FWD_PARAMS])


def _output_shape():
    out = _jax.eval_shape(lambda: _forward(_fwd_setup_inputs(0)))
    return out.shape, out.dtype

N_MICROBATCH = 1
ADAM_LR = 0.001
ADAM_B1 = 0.9
ADAM_B2 = 0.999
ADAM_EPS = 1e-08
ADAM_WD = 0.01
ADAM_STEP = 10
PER_EXAMPLE_BATCH_AXIS = {'x': 0, 'loss_target': 0}
SHARED_INPUTS = []
_WEIGHT_DTYPES = {'norm_mix_pre': _jnp.float32, 'norm_mix_post': _jnp.float32, 'norm_ffn_pre': _jnp.float32, 'norm_ffn_post': _jnp.float32, 'attn_w_qkv': _jnp.float32, 'attn_b_qkv': _jnp.float32, 'attn_sinks': _jnp.float32, 'attn_w_o': _jnp.float32, 'attn_b_o': _jnp.float32, 'sgu_w_in': _jnp.float32, 'sgu_ln_g': _jnp.float32, 'sgu_ln_b': _jnp.float32, 'sgu_w_spatial': _jnp.float32, 'sgu_b_spatial': _jnp.float32, 'sgu_w_out': _jnp.float32, 'ffn_w_gate_up': _jnp.float32, 'ffn_w_down': _jnp.float32}
MOMENT_SCALE = {'norm_mix_pre': 1.465604e+00, 'norm_mix_post': 3.239841e+01, 'norm_ffn_pre': 1.334425e+00, 'norm_ffn_post': 3.185005e+01, 'attn_w_qkv': 1.381761e+00, 'attn_b_qkv': 7.548039e+01, 'attn_sinks': 3.560787e-01, 'attn_w_o': 1.695147e+00, 'attn_b_o': 9.257379e+01, 'sgu_w_in': 7.802341e-01, 'sgu_ln_g': 2.785417e-01, 'sgu_ln_b': 2.861572e-01, 'sgu_w_spatial': 2.730813e-01, 'sgu_b_spatial': 4.179603e-01, 'sgu_w_out': 2.510838e+00, 'ffn_w_gate_up': 5.326565e-01, 'ffn_w_down': 1.055254e+00}


def _to_microbatches(a, axis):
    t = _jnp.moveaxis(a, axis, 0)
    t = t.reshape((N_MICROBATCH, t.shape[0] // N_MICROBATCH) + t.shape[1:])
    return _jnp.moveaxis(t, 1, axis + 1)


def setup_inputs(seed: int = 0) -> dict:
    inp = _fwd_setup_inputs(seed)
    key = _jax.random.fold_in(_jax.random.key(seed), 7919)
    shape, _ = _output_shape()
    out = dict(inp)
    out["loss_target"] = _jax.random.normal(_jax.random.fold_in(key, 0), shape, _jnp.float32)
    for i, name in enumerate(TWIN_WEIGHTS):
        w = inp[name].astype(_jnp.float32)
        if MOMENT_SCALE is None:
            s = _jnp.sqrt(_jnp.mean(_jnp.square(w)) + 1e-30)
        else:
            s = MOMENT_SCALE[name]
        km, kv = _jax.random.split(_jax.random.fold_in(key, i + 1))
        out[name] = w
        out["m_" + name] = s * _jax.random.normal(km, w.shape, _jnp.float32)
        out["v_" + name] = (s * s) * _jax.random.uniform(kv, w.shape, _jnp.float32, 0.5, 1.5)
    if N_MICROBATCH > 1:
        for name, axis in PER_EXAMPLE_BATCH_AXIS.items():
            out[name] = _to_microbatches(out[name], axis)
    return {'x': out['x'], 'norm_mix_pre': out['norm_mix_pre'], 'norm_mix_post': out['norm_mix_post'], 'norm_ffn_pre': out['norm_ffn_pre'], 'norm_ffn_post': out['norm_ffn_post'], 'attn_w_qkv': out['attn_w_qkv'], 'attn_b_qkv': out['attn_b_qkv'], 'attn_sinks': out['attn_sinks'], 'attn_w_o': out['attn_w_o'], 'attn_b_o': out['attn_b_o'], 'sgu_w_in': out['sgu_w_in'], 'sgu_ln_g': out['sgu_ln_g'], 'sgu_ln_b': out['sgu_ln_b'], 'sgu_w_spatial': out['sgu_w_spatial'], 'sgu_b_spatial': out['sgu_b_spatial'], 'sgu_w_out': out['sgu_w_out'], 'ffn_w_gate_up': out['ffn_w_gate_up'], 'ffn_w_down': out['ffn_w_down'], 'loss_target': out['loss_target'], 'm_norm_mix_pre': out['m_norm_mix_pre'], 'm_norm_mix_post': out['m_norm_mix_post'], 'm_norm_ffn_pre': out['m_norm_ffn_pre'], 'm_norm_ffn_post': out['m_norm_ffn_post'], 'm_attn_w_qkv': out['m_attn_w_qkv'], 'm_attn_b_qkv': out['m_attn_b_qkv'], 'm_attn_sinks': out['m_attn_sinks'], 'm_attn_w_o': out['m_attn_w_o'], 'm_attn_b_o': out['m_attn_b_o'], 'm_sgu_w_in': out['m_sgu_w_in'], 'm_sgu_ln_g': out['m_sgu_ln_g'], 'm_sgu_ln_b': out['m_sgu_ln_b'], 'm_sgu_w_spatial': out['m_sgu_w_spatial'], 'm_sgu_b_spatial': out['m_sgu_b_spatial'], 'm_sgu_w_out': out['m_sgu_w_out'], 'm_ffn_w_gate_up': out['m_ffn_w_gate_up'], 'm_ffn_w_down': out['m_ffn_w_down'], 'v_norm_mix_pre': out['v_norm_mix_pre'], 'v_norm_mix_post': out['v_norm_mix_post'], 'v_norm_ffn_pre': out['v_norm_ffn_pre'], 'v_norm_ffn_post': out['v_norm_ffn_post'], 'v_attn_w_qkv': out['v_attn_w_qkv'], 'v_attn_b_qkv': out['v_attn_b_qkv'], 'v_attn_sinks': out['v_attn_sinks'], 'v_attn_w_o': out['v_attn_w_o'], 'v_attn_b_o': out['v_attn_b_o'], 'v_sgu_w_in': out['v_sgu_w_in'], 'v_sgu_ln_g': out['v_sgu_ln_g'], 'v_sgu_ln_b': out['v_sgu_ln_b'], 'v_sgu_w_spatial': out['v_sgu_w_spatial'], 'v_sgu_b_spatial': out['v_sgu_b_spatial'], 'v_sgu_w_out': out['v_sgu_w_out'], 'v_ffn_w_gate_up': out['v_ffn_w_gate_up'], 'v_ffn_w_down': out['v_ffn_w_down']}


def _loss(weights, diff, rest, loss_target):
    with _jax.named_scope("forward"):
        args = {**rest, TWIN_DIFF_INPUT: diff, **{k: w.astype(_WEIGHT_DTYPES[k]) for k, w in weights.items()}}
        y = _forward(args)
    with _jax.named_scope("loss_head"):
        err = _jnp.square(y.astype(_jnp.float32) - loss_target)
        return 0.5 * _jnp.sum(_jnp.mean(err, axis=-1)) if err.ndim else 0.5 * err


def _adamw(w, g, m, v):
    m = ADAM_B1 * m + (1.0 - ADAM_B1) * g
    v = ADAM_B2 * v + (1.0 - ADAM_B2) * _jnp.square(g)
    m_hat = m / (1.0 - ADAM_B1 ** ADAM_STEP)
    v_hat = v / (1.0 - ADAM_B2 ** ADAM_STEP)
    delta = -ADAM_LR * (m_hat / (_jnp.sqrt(v_hat) + ADAM_EPS) + ADAM_WD * w)
    return delta, m, v


def reference(x, norm_mix_pre, norm_mix_post, norm_ffn_pre, norm_ffn_post, attn_w_qkv, attn_b_qkv, attn_sinks, attn_w_o, attn_b_o, sgu_w_in, sgu_ln_g, sgu_ln_b, sgu_w_spatial, sgu_b_spatial, sgu_w_out, ffn_w_gate_up, ffn_w_down, loss_target, m_norm_mix_pre, m_norm_mix_post, m_norm_ffn_pre, m_norm_ffn_post, m_attn_w_qkv, m_attn_b_qkv, m_attn_sinks, m_attn_w_o, m_attn_b_o, m_sgu_w_in, m_sgu_ln_g, m_sgu_ln_b, m_sgu_w_spatial, m_sgu_b_spatial, m_sgu_w_out, m_ffn_w_gate_up, m_ffn_w_down, v_norm_mix_pre, v_norm_mix_post, v_norm_ffn_pre, v_norm_ffn_post, v_attn_w_qkv, v_attn_b_qkv, v_attn_sinks, v_attn_w_o, v_attn_b_o, v_sgu_w_in, v_sgu_ln_g, v_sgu_ln_b, v_sgu_w_spatial, v_sgu_b_spatial, v_sgu_w_out, v_ffn_w_gate_up, v_ffn_w_down):
    given = dict(x=x, norm_mix_pre=norm_mix_pre, norm_mix_post=norm_mix_post, norm_ffn_pre=norm_ffn_pre, norm_ffn_post=norm_ffn_post, attn_w_qkv=attn_w_qkv, attn_b_qkv=attn_b_qkv, attn_sinks=attn_sinks, attn_w_o=attn_w_o, attn_b_o=attn_b_o, sgu_w_in=sgu_w_in, sgu_ln_g=sgu_ln_g, sgu_ln_b=sgu_ln_b, sgu_w_spatial=sgu_w_spatial, sgu_b_spatial=sgu_b_spatial, sgu_w_out=sgu_w_out, ffn_w_gate_up=ffn_w_gate_up, ffn_w_down=ffn_w_down, loss_target=loss_target, m_norm_mix_pre=m_norm_mix_pre, m_norm_mix_post=m_norm_mix_post, m_norm_ffn_pre=m_norm_ffn_pre, m_norm_ffn_post=m_norm_ffn_post, m_attn_w_qkv=m_attn_w_qkv, m_attn_b_qkv=m_attn_b_qkv, m_attn_sinks=m_attn_sinks, m_attn_w_o=m_attn_w_o, m_attn_b_o=m_attn_b_o, m_sgu_w_in=m_sgu_w_in, m_sgu_ln_g=m_sgu_ln_g, m_sgu_ln_b=m_sgu_ln_b, m_sgu_w_spatial=m_sgu_w_spatial, m_sgu_b_spatial=m_sgu_b_spatial, m_sgu_w_out=m_sgu_w_out, m_ffn_w_gate_up=m_ffn_w_gate_up, m_ffn_w_down=m_ffn_w_down, v_norm_mix_pre=v_norm_mix_pre, v_norm_mix_post=v_norm_mix_post, v_norm_ffn_pre=v_norm_ffn_pre, v_norm_ffn_post=v_norm_ffn_post, v_attn_w_qkv=v_attn_w_qkv, v_attn_b_qkv=v_attn_b_qkv, v_attn_sinks=v_attn_sinks, v_attn_w_o=v_attn_w_o, v_attn_b_o=v_attn_b_o, v_sgu_w_in=v_sgu_w_in, v_sgu_ln_g=v_sgu_ln_g, v_sgu_ln_b=v_sgu_ln_b, v_sgu_w_spatial=v_sgu_w_spatial, v_sgu_b_spatial=v_sgu_b_spatial, v_sgu_w_out=v_sgu_w_out, v_ffn_w_gate_up=v_ffn_w_gate_up, v_ffn_w_down=v_ffn_w_down)
    weights = {n: given[n] for n in TWIN_WEIGHTS}
    shared = {n: given[n] for n in SHARED_INPUTS}
    per_example = {n: given[n] for n in ['x']}
    grad_fn = _jax.value_and_grad(_loss, argnums=(0, 1))

    def one_microbatch(ex, loss_target):
        ex = dict(ex)
        diff = ex.pop(TWIN_DIFF_INPUT)
        return grad_fn(weights, diff, {**shared, **ex}, loss_target)

    if N_MICROBATCH == 1:
        loss, (grad_w, grad_x) = one_microbatch(per_example, given["loss_target"])
    else:
        def body(carry, xs):
            loss_sum, grad_sum = carry
            l_k, (gw_k, gx_k) = one_microbatch(xs[0], xs[1])
            with _jax.named_scope("update"):
                return (loss_sum + l_k, _jax.tree.map(_jnp.add, grad_sum, gw_k)), gx_k

        init = (_jnp.zeros((), _jnp.float32), _jax.tree.map(_jnp.zeros_like, weights))
        (loss, grad_w), grad_x = _jax.lax.scan(body, init, (per_example, given["loss_target"]))
    with _jax.named_scope("update"):
        delta_w, new_m, new_v = {}, {}, {}
        for n in TWIN_WEIGHTS:
            delta_w[n], new_m[n], new_v[n] = _adamw(weights[n], grad_w[n], given["m_" + n], given["v_" + n])
    return (loss, grad_x, *[grad_w[n] for n in TWIN_WEIGHTS], *[delta_w[n] for n in TWIN_WEIGHTS],
            *[new_m[n] for n in TWIN_WEIGHTS], *[new_v[n] for n in TWIN_WEIGHTS])
```

```python
import functools

import jax
import jax.numpy as jnp
from jax import lax
from jax.experimental import pallas as pl
from jax.experimental.pallas import tpu as pltpu

F32 = jnp.float32
BF16 = jnp.bfloat16
I32 = jnp.int32

D_MODEL = 1024
HEAD_DIM = 64
N_Q_HEADS = 16
N_KV_HEADS = 4
GQA_GROUP = 4
WINDOW = 128
Q_WIDTH = 1024
KV_WIDTH = 256
QKV_WIDTH = 1536
ROPE_THETA = 10000.0
SGU_GROUPS = 8
SGU_CHUNK = 128
D_FF = 2816
FF_HALF = D_FF // 2
EPS = 1e-6
N_CHIPS = 4
LANES = 128

ADAM_LR = 0.001
ADAM_B1 = 0.9
ADAM_B2 = 0.999
ADAM_EPS = 1e-08
ADAM_WD = 0.01
ADAM_STEP = 10

VMEM_LIMIT = 52 * 1024 * 1024
MESH = pl.DeviceIdType.MESH
NEG = -1e30
NT_DIMS = (((1,), (1,)), ((), ()))
TN_DIMS = (((0,), (0,)), ((), ()))
NN_DIMS = (((1,), (0,)), ((), ()))


def _cp(*sem):
    return pltpu.CompilerParams(dimension_semantics=sem, vmem_limit_bytes=VMEM_LIMIT)


def _row_tile(s, want):
    return want if s % want == 0 else s


def _mm_call(*, grid, in_specs, out_spec, out_shape, dims, nk, kaxis, acc_shape, name, operands):
    out_dtype = out_shape.dtype

    def body(a_ref, b_ref, o_ref, *scratch):
        p = lax.dot_general(a_ref[...].astype(BF16), b_ref[...].astype(BF16), dims, preferred_element_type=F32)
        if nk == 1:
            o_ref[...] = p.astype(out_dtype)
        else:
            acc = scratch[0]
            kk = pl.program_id(kaxis)

            @pl.when(kk == 0)
            def _():
                acc[...] = p

            @pl.when(kk > 0)
            def _():
                acc[...] += p

            @pl.when(kk == nk - 1)
            def _():
                o_ref[...] = acc[...].astype(out_dtype)

    sem = ["parallel"] * len(grid)
    if nk > 1:
        sem[kaxis] = "arbitrary"
    return pl.pallas_call(
        body, grid=grid, in_specs=in_specs, out_specs=out_spec, out_shape=out_shape,
        scratch_shapes=[pltpu.VMEM(acc_shape, F32)] if nk > 1 else [],
        compiler_params=_cp(*sem), name=name)(*operands)


def mm_nn(a, w, *, out_dtype, name, tm=512, tn=512):
    m, k = a.shape
    tm = _row_tile(m, tm)
    if w.ndim == 3:
        ns = w.shape[2]
        grid = (N_CHIPS, m // tm)
        w_spec = pl.BlockSpec((None, k, ns), lambda j, i: (j, 0, 0))
        o_spec = pl.BlockSpec((tm, ns), lambda j, i: (i, j))
        n = N_CHIPS * ns
    else:
        n = w.shape[1]
        grid = (n // tn, m // tm)
        w_spec = pl.BlockSpec((k, tn), lambda j, i: (0, j))
        o_spec = pl.BlockSpec((tm, tn), lambda j, i: (i, j))
    return _mm_call(grid=grid, in_specs=[pl.BlockSpec((tm, k), lambda j, i: (i, 0)), w_spec], out_spec=o_spec,
                    out_shape=jax.ShapeDtypeStruct((m, n), out_dtype), dims=NN_DIMS, nk=1, kaxis=0, acc_shape=None,
                    name=name, operands=(a, w))


def mm_nt(a, w, *, out_dtype, name, tm=512, tn=512):
    if w.ndim == 2:
        m, n = a.shape
        kout = w.shape[0]
        tm = _row_tile(m, tm)
        return _mm_call(grid=(kout // tn, m // tm),
                        in_specs=[pl.BlockSpec((tm, n), lambda j, i: (i, 0)), pl.BlockSpec((tn, n), lambda j, i: (j, 0))],
                        out_spec=pl.BlockSpec((tm, tn), lambda j, i: (i, j)),
                        out_shape=jax.ShapeDtypeStruct((m, kout), out_dtype), dims=NT_DIMS, nk=1, kaxis=0,
                        acc_shape=None, name=name, operands=(a, w))
    _, kout, ns = w.shape
    if a.ndim == 3:
        m = a.shape[1]
        tm = _row_tile(m, tm)
        a_spec = pl.BlockSpec((None, tm, ns), lambda i, kk: (kk // 2, i, kk % 2))
    else:
        m = a.shape[0]
        tm = _row_tile(m, tm)
        a_spec = pl.BlockSpec((tm, ns), lambda i, kk: (i, kk))
    return _mm_call(grid=(m // tm, N_CHIPS),
                    in_specs=[a_spec, pl.BlockSpec((None, kout, ns), lambda i, kk: (kk, 0, 0))],
                    out_spec=pl.BlockSpec((tm, kout), lambda i, kk: (i, 0)),
                    out_shape=jax.ShapeDtypeStruct((m, kout), out_dtype), dims=NT_DIMS, nk=N_CHIPS, kaxis=1,
                    acc_shape=(tm, kout), name=name, operands=(a, w))


def mm_tn(a, b, *, shard_major, name, tm, tn, tk=512, out_dtype=BF16):
    s, m = a.shape
    tk = _row_tile(s, tk)
    if b.ndim == 3:
        n = 2 * b.shape[2]
        b_spec = pl.BlockSpec((None, tk, tn), lambda i, j, kk: (j // 2, kk, j % 2))
    else:
        n = b.shape[1]
        b_spec = pl.BlockSpec((tk, tn), lambda i, j, kk: (kk, j))
    if shard_major:
        assert tn == n // N_CHIPS
        o_spec = pl.BlockSpec((None, tm, tn), lambda i, j, kk: (j, i, 0))
        o_shape = jax.ShapeDtypeStruct((N_CHIPS, m, tn), out_dtype)
    else:
        o_spec = pl.BlockSpec((tm, tn), lambda i, j, kk: (i, j))
        o_shape = jax.ShapeDtypeStruct((m, n), out_dtype)
    return _mm_call(grid=(m // tm, n // tn, s // tk),
                    in_specs=[pl.BlockSpec((tk, tm), lambda i, j, kk: (kk, i)), b_spec], out_spec=o_spec,
                    out_shape=o_shape, dims=TN_DIMS, nk=s // tk, kaxis=2, acc_shape=(tm, tn), name=name, operands=(a, b))


def _rstd(x):
    return lax.rsqrt(jnp.mean(x * x, axis=-1, keepdims=True) + EPS)


def _rms_bwd(dy, x, g):
    r = _rstd(x)
    xhat = x * r
    gy = dy * g
    dx = r * (gy - xhat * jnp.mean(gy * xhat, axis=-1, keepdims=True))
    return dx, jnp.sum(dy * xhat, axis=0, keepdims=True)


def _accum(ref, val, first):
    @pl.when(first)
    def _():
        ref[...] = val

    @pl.when(jnp.logical_not(first))
    def _():
        ref[...] += val


def _row_spec(tm, width):
    return pl.BlockSpec((tm, width), lambda i: (i, 0))


def _vec_spec(width):
    return pl.BlockSpec((1, width), lambda i: (0, 0))


def prenorm(x, g, *, name, tm=256):
    s = x.shape[0]
    tm = _row_tile(s, tm)

    def body(x_ref, g_ref, h_ref):
        xv = x_ref[...]
        h_ref[...] = (xv * _rstd(xv) * g_ref[...]).astype(BF16)

    return pl.pallas_call(
        body, grid=(s // tm,), in_specs=[_row_spec(tm, D_MODEL), _vec_spec(D_MODEL)], out_specs=_row_spec(tm, D_MODEL),
        out_shape=jax.ShapeDtypeStruct((s, D_MODEL), BF16), compiler_params=_cp("parallel"), name=name)(x, g)


def residual_norm(x, m, bias, g_post, g_next, *, name, tm=256):
    s = x.shape[0]
    tm = _row_tile(s, tm)

    def body(x_ref, m_ref, b_ref, gp_ref, gn_ref, xo_ref, h_ref):
        mv = m_ref[...] + b_ref[...]
        xn = x_ref[...] + mv * _rstd(mv) * gp_ref[...]
        xo_ref[...] = xn
        h_ref[...] = (xn * _rstd(xn) * gn_ref[...]).astype(BF16)

    return pl.pallas_call(
        body, grid=(s // tm,),
        in_specs=[_row_spec(tm, D_MODEL), _row_spec(tm, D_MODEL), _vec_spec(D_MODEL), _vec_spec(D_MODEL), _vec_spec(D_MODEL)],
        out_specs=[_row_spec(tm, D_MODEL), _row_spec(tm, D_MODEL)],
        out_shape=[jax.ShapeDtypeStruct((s, D_MODEL), F32), jax.ShapeDtypeStruct((s, D_MODEL), BF16)],
        compiler_params=_cp("parallel"), name=name)(x, m, bias, g_post, g_next)


def loss_head(x, f, g_post, target, *, name, tm=256):
    s = x.shape[0]
    tm = _row_tile(s, tm)

    def body(x_ref, f_ref, g_ref, t_ref, dx_ref, df_ref, dg_ref, loss_ref):
        first = pl.program_id(0) == 0
        fv = f_ref[...]
        g = g_ref[...]
        err = x_ref[...] + fv * _rstd(fv) * g - t_ref[...]
        dx = err * (1.0 / D_MODEL)
        dx_ref[...] = dx
        df, dg = _rms_bwd(dx, fv, g)
        df_ref[...] = df.astype(BF16)
        _accum(dg_ref, dg, first)
        part = jnp.sum(jnp.sum(err * err, axis=-1, keepdims=True), axis=0, keepdims=True) * (0.5 / D_MODEL)
        _accum(loss_ref, jnp.broadcast_to(part, (8, LANES)), first)

    return pl.pallas_call(
        body, grid=(s // tm,),
        in_specs=[_row_spec(tm, D_MODEL), _row_spec(tm, D_MODEL), _vec_spec(D_MODEL), _row_spec(tm, D_MODEL)],
        out_specs=[_row_spec(tm, D_MODEL), _row_spec(tm, D_MODEL), _vec_spec(D_MODEL), pl.BlockSpec((8, LANES), lambda i: (0, 0))],
        out_shape=[jax.ShapeDtypeStruct((s, D_MODEL), F32), jax.ShapeDtypeStruct((s, D_MODEL), BF16),
                   jax.ShapeDtypeStruct((1, D_MODEL), F32), jax.ShapeDtypeStruct((8, LANES), F32)],
        compiler_params=_cp("arbitrary"), name=name)(x, f, g_post, target)


def norm_bwd_pair(dres, dh, x, g_pre, m, bias, g_post, *, name, tm=256):
    s = x.shape[0]
    tm = _row_tile(s, tm)

    def body(dres_ref, dh_ref, x_ref, gpre_ref, m_ref, b_ref, gpost_ref, dx_ref, dm_ref, dgpre_ref, dgpost_ref, db_ref):
        first = pl.program_id(0) == 0
        d1, dgpre = _rms_bwd(dh_ref[...], x_ref[...], gpre_ref[...])
        dx = dres_ref[...] + d1
        dx_ref[...] = dx
        dm, dgpost = _rms_bwd(dx, m_ref[...] + b_ref[...], gpost_ref[...])
        dm_ref[...] = dm.astype(BF16)
        _accum(dgpre_ref, dgpre, first)
        _accum(dgpost_ref, dgpost, first)
        _accum(db_ref, jnp.sum(dm, axis=0, keepdims=True), first)

    row, vec = _row_spec(tm, D_MODEL), _vec_spec(D_MODEL)
    vshape = jax.ShapeDtypeStruct((1, D_MODEL), F32)
    return pl.pallas_call(
        body, grid=(s // tm,), in_specs=[row, row, row, vec, row, vec, vec], out_specs=[row, row, vec, vec, vec],
        out_shape=[jax.ShapeDtypeStruct((s, D_MODEL), F32), jax.ShapeDtypeStruct((s, D_MODEL), BF16), vshape, vshape, vshape],
        compiler_params=_cp("arbitrary"), name=name)(dres, dh, x, g_pre, m, bias, g_post)


def norm_bwd_last(dres, dh, x, g_pre, *, name, tm=256):
    s = x.shape[0]
    tm = _row_tile(s, tm)

    def body(dres_ref, dh_ref, x_ref, g_ref, dx_ref, dg_ref):
        d1, dg = _rms_bwd(dh_ref[...], x_ref[...], g_ref[...])
        dx_ref[...] = dres_ref[...] + d1
        _accum(dg_ref, dg, pl.program_id(0) == 0)

    row, vec = _row_spec(tm, D_MODEL), _vec_spec(D_MODEL)
    return pl.pallas_call(
        body, grid=(s // tm,), in_specs=[row, row, row, vec], out_specs=[row, vec],
        out_shape=[jax.ShapeDtypeStruct((s, D_MODEL), F32), jax.ShapeDtypeStruct((1, D_MODEL), F32)],
        compiler_params=_cp("arbitrary"), name=name)(dres, dh, x, g_pre)


def _rope_tables(s):
    half = HEAD_DIM // 2
    inv_freq = ROPE_THETA ** (-(jnp.arange(half, dtype=F32) * 2.0) / HEAD_DIM)
    ang = jnp.arange(s, dtype=I32).astype(F32)[:, None] * inv_freq[None, :]
    cos, sin = jnp.cos(ang), jnp.sin(ang)
    return jnp.tile(cos, (1, 4)), jnp.concatenate([-sin, sin, -sin, sin], axis=1)


def _swap_halves(x):
    lane = lax.broadcasted_iota(I32, x.shape, 1)
    return jnp.where((lane & (HEAD_DIM - 1)) < HEAD_DIM // 2, pltpu.roll(x, LANES - 32, 1), pltpu.roll(x, 32, 1))


N_ROPE_BLOCKS = (Q_WIDTH + KV_WIDTH) // LANES


def rope_fwd(qkv, bias, cos, sin, *, name, tm=256):
    s = qkv.shape[0]
    tm = _row_tile(s, tm)

    def body(x_ref, b_ref, c_ref, s_ref, q_ref, k_ref, v_ref):
        cosv, sinv = c_ref[...], s_ref[...]
        for blk in range(QKV_WIDTH // LANES):
            cols = slice(blk * LANES, (blk + 1) * LANES)
            xb = x_ref[:, cols] + b_ref[:, cols]
            if blk < N_ROPE_BLOCKS:
                xb = xb * cosv + _swap_halves(xb) * sinv
            for e in range(2):
                head = 2 * blk + e
                piece = xb[:, e * HEAD_DIM:(e + 1) * HEAD_DIM].astype(BF16)
                if head < N_Q_HEADS:
                    q_ref[head] = piece
                elif head < N_Q_HEADS + N_KV_HEADS:
                    k_ref[head - N_Q_HEADS] = piece
                else:
                    v_ref[head - N_Q_HEADS - N_KV_HEADS] = piece

    def hm(nh):
        return pl.BlockSpec((nh, tm, HEAD_DIM), lambda i: (0, i, 0))

    return pl.pallas_call(
        body, grid=(s // tm,),
        in_specs=[_row_spec(tm, QKV_WIDTH), _vec_spec(QKV_WIDTH), _row_spec(tm, LANES), _row_spec(tm, LANES)],
        out_specs=[hm(N_Q_HEADS), hm(N_KV_HEADS), hm(N_KV_HEADS)],
        out_shape=[jax.ShapeDtypeStruct((N_Q_HEADS, s, HEAD_DIM), BF16), jax.ShapeDtypeStruct((N_KV_HEADS, s, HEAD_DIM), BF16),
                   jax.ShapeDtypeStruct((N_KV_HEADS, s, HEAD_DIM), BF16)],
        compiler_params=_cp("parallel"), name=name)(qkv, bias, cos, sin)


def rope_bwd(dq, dkc, dkp, dvc, dvp, cos, sin, *, name):
    s = dq.shape[1]
    tm = WINDOW
    nb = s // tm

    def body(dq_ref, dkc_ref, dkp_ref, dvc_ref, dvp_ref, c_ref, s_ref, o_ref, db_ref, buf):
        i = pl.program_id(0)
        has_next = (i < nb - 1).astype(F32)
        for head in range(N_Q_HEADS):
            buf[:, head * HEAD_DIM:(head + 1) * HEAD_DIM] = dq_ref[head].astype(F32)
        for head in range(N_KV_HEADS):
            kcols = slice(Q_WIDTH + head * HEAD_DIM, Q_WIDTH + (head + 1) * HEAD_DIM)
            vcols = slice(Q_WIDTH + KV_WIDTH + head * HEAD_DIM, Q_WIDTH + KV_WIDTH + (head + 1) * HEAD_DIM)
            buf[:, kcols] = dkc_ref[head].astype(F32) + has_next * dkp_ref[head].astype(F32)
            buf[:, vcols] = dvc_ref[head].astype(F32) + has_next * dvp_ref[head].astype(F32)
        cosv, sinv = c_ref[...], s_ref[...]
        for blk in range(QKV_WIDTH // LANES):
            cols = slice(blk * LANES, (blk + 1) * LANES)
            g = buf[:, cols]
            if blk < N_ROPE_BLOCKS:
                g = g * cosv + _swap_halves(g * sinv)
            o_ref[:, cols] = g.astype(BF16)
            buf[:, cols] = g
        _accum(db_ref, jnp.sum(buf[...], axis=0, keepdims=True), i == 0)

    def hm(nh, shift):
        if shift:
            return pl.BlockSpec((nh, tm, HEAD_DIM), lambda i: (0, jnp.minimum(i + 1, nb - 1), 0))
        return pl.BlockSpec((nh, tm, HEAD_DIM), lambda i: (0, i, 0))

    return pl.pallas_call(
        body, grid=(nb,),
        in_specs=[hm(N_Q_HEADS, False), hm(N_KV_HEADS, False), hm(N_KV_HEADS, True), hm(N_KV_HEADS, False), hm(N_KV_HEADS, True),
                  _row_spec(tm, LANES), _row_spec(tm, LANES)],
        out_specs=[_row_spec(tm, QKV_WIDTH), _vec_spec(QKV_WIDTH)],
        out_shape=[jax.ShapeDtypeStruct((s, QKV_WIDTH), BF16), jax.ShapeDtypeStruct((1, QKV_WIDTH), F32)],
        scratch_shapes=[pltpu.VMEM((tm, QKV_WIDTH), F32)],
        compiler_params=_cp("arbitrary"), name=name)(dq, dkc, dkp, dvc, dvp, cos, sin)


def heads_to_rows(o, *, name, tm=256):
    s = o.shape[1]
    tm = _row_tile(s, tm)

    def body(o_ref, r_ref):
        for head in range(N_Q_HEADS):
            r_ref[:, head * HEAD_DIM:(head + 1) * HEAD_DIM] = o_ref[head]

    return pl.pallas_call(
        body, grid=(s // tm,), in_specs=[pl.BlockSpec((N_Q_HEADS, tm, HEAD_DIM), lambda i: (0, i, 0))],
        out_specs=_row_spec(tm, Q_WIDTH), out_shape=jax.ShapeDtypeStruct((s, Q_WIDTH), BF16),
        compiler_params=_cp("parallel"), name=name)(o)


def rows_to_heads(x, *, name, tm=256):
    s = x.shape[0]
    tm = _row_tile(s, tm)

    def body(x_ref, o_ref):
        xv = x_ref[...].astype(F32)
        for head in range(N_Q_HEADS):
            o_ref[head] = xv[:, head * HEAD_DIM:(head + 1) * HEAD_DIM].astype(BF16)

    return pl.pallas_call(
        body, grid=(s // tm,), in_specs=[_row_spec(tm, Q_WIDTH)],
        out_specs=pl.BlockSpec((N_Q_HEADS, tm, HEAD_DIM), lambda i: (0, i, 0)),
        out_shape=jax.ShapeDtypeStruct((N_Q_HEADS, s, HEAD_DIM), BF16), compiler_params=_cp("parallel"), name=name)(x)


ROWS = GQA_GROUP * WINDOW


def _attn_probs(q, kc, kp, sink, n):
    scale = HEAD_DIM ** -0.5
    sc = lax.dot_general(q, kc, NT_DIMS, preferred_element_type=F32) * scale
    sp = lax.dot_general(q, kp, NT_DIMS, preferred_element_type=F32) * scale
    qpos = lax.broadcasted_iota(I32, (ROWS, WINDOW), 0) & (WINDOW - 1)
    kpos = lax.broadcasted_iota(I32, (ROWS, WINDOW), 1)
    sc = jnp.where(kpos <= qpos, sc, NEG)
    sp = jnp.where(jnp.logical_and(kpos > qpos, n > 0), sp, NEG)
    m = jnp.maximum(jnp.maximum(jnp.max(sc, axis=-1, keepdims=True), jnp.max(sp, axis=-1, keepdims=True)), sink)
    ec, ep, es = jnp.exp(sc - m), jnp.exp(sp - m), jnp.exp(sink - m)
    inv = 1.0 / (jnp.sum(ec, axis=-1, keepdims=True) + jnp.sum(ep, axis=-1, keepdims=True) + es)
    return ec * inv, ep * inv, es * inv


def _attn_specs(s):
    q_spec = pl.BlockSpec((None, GQA_GROUP, WINDOW, HEAD_DIM), lambda h, n: (h, 0, n, 0))
    cur = pl.BlockSpec((None, WINDOW, HEAD_DIM), lambda h, n: (h, n, 0))
    prev = pl.BlockSpec((None, WINDOW, HEAD_DIM), lambda h, n: (h, jnp.maximum(n - 1, 0), 0))
    sink = pl.BlockSpec((None, ROWS, LANES), lambda h, n: (h, 0, 0))
    return q_spec, cur, prev, sink


def attn_fwd(q, k, v, sink_rows, *, name):
    s = k.shape[1]

    def body(q_ref, kc_ref, kp_ref, vc_ref, vp_ref, sink_ref, o_ref):
        qv = q_ref[...].reshape(ROWS, HEAD_DIM)
        pc, pp, _ = _attn_probs(qv, kc_ref[...], kp_ref[...], sink_ref[...], pl.program_id(1))
        o = jnp.dot(pc.astype(BF16), vc_ref[...], preferred_element_type=F32)
        o += jnp.dot(pp.astype(BF16), vp_ref[...], preferred_element_type=F32)
        o_ref[...] = o.reshape(GQA_GROUP, WINDOW, HEAD_DIM).astype(BF16)

    q_spec, cur, prev, sink = _attn_specs(s)
    return pl.pallas_call(
        body, grid=(N_KV_HEADS, s // WINDOW), in_specs=[q_spec, cur, prev, cur, prev, sink], out_specs=q_spec,
        out_shape=jax.ShapeDtypeStruct(q.shape, BF16), compiler_params=_cp("parallel", "parallel"), name=name)(q, k, k, v, v, sink_rows)


def attn_bwd(q, k, v, sink_rows, do, *, name):
    s = k.shape[1]

    def body(q_ref, kc_ref, kp_ref, vc_ref, vp_ref, sink_ref, do_ref, dq_ref, dkc_ref, dkp_ref, dvc_ref, dvp_ref, dsink_ref):
        n = pl.program_id(1)
        qv = q_ref[...].reshape(ROWS, HEAD_DIM)
        dov = do_ref[...].reshape(ROWS, HEAD_DIM)
        kc, kp, vc, vp = kc_ref[...], kp_ref[...], vc_ref[...], vp_ref[...]
        pc, pp, ps = _attn_probs(qv, kc, kp, sink_ref[...], n)
        dpc = lax.dot_general(dov, vc, NT_DIMS, preferred_element_type=F32)
        dpp = lax.dot_general(dov, vp, NT_DIMS, preferred_element_type=F32)
        delta = jnp.sum(pc * dpc, axis=-1, keepdims=True) + jnp.sum(pp * dpp, axis=-1, keepdims=True)
        scale = HEAD_DIM ** -0.5
        dsc = (pc * (dpc - delta) * scale).astype(BF16)
        dsp = (pp * (dpp - delta) * scale).astype(BF16)
        dq = jnp.dot(dsc, kc, preferred_element_type=F32) + jnp.dot(dsp, kp, preferred_element_type=F32)
        dq_ref[...] = dq.reshape(GQA_GROUP, WINDOW, HEAD_DIM).astype(BF16)
        dkc_ref[...] = lax.dot_general(dsc, qv, TN_DIMS, preferred_element_type=F32).astype(BF16)
        dkp_ref[...] = lax.dot_general(dsp, qv, TN_DIMS, preferred_element_type=F32).astype(BF16)
        dvc_ref[...] = lax.dot_general(pc.astype(BF16), dov, TN_DIMS, preferred_element_type=F32).astype(BF16)
        dvp_ref[...] = lax.dot_general(pp.astype(BF16), dov, TN_DIMS, preferred_element_type=F32).astype(BF16)
        dsink = -(ps * delta)
        for g in range(GQA_GROUP):
            part = jnp.broadcast_to(jnp.sum(dsink[g * WINDOW:(g + 1) * WINDOW], axis=0, keepdims=True), (8, LANES))

            @pl.when(n == 0)
            def _():
                dsink_ref[g] = part

            @pl.when(n > 0)
            def _():
                dsink_ref[g] += part

    q_spec, cur, prev, sink = _attn_specs(s)
    kv_shape = jax.ShapeDtypeStruct(k.shape, BF16)
    return pl.pallas_call(
        body, grid=(N_KV_HEADS, s // WINDOW), in_specs=[q_spec, cur, prev, cur, prev, sink, q_spec],
        out_specs=[q_spec, cur, cur, cur, cur, pl.BlockSpec((None, GQA_GROUP, 8, LANES), lambda h, n: (h, 0, 0, 0))],
        out_shape=[jax.ShapeDtypeStruct(q.shape, BF16), kv_shape, kv_shape, kv_shape, kv_shape,
                   jax.ShapeDtypeStruct((N_KV_HEADS, GQA_GROUP, 8, LANES), F32)],
        compiler_params=_cp("parallel", "arbitrary"), name=name)(q, k, k, v, v, sink_rows, do)


GELU_C = 0.7978845608028654
GELU_A = 0.044715


def _gelu(x):
    return 0.5 * x * (1.0 + jnp.tanh(GELU_C * (x + GELU_A * x * x * x)))


def _gelu_grad(x):
    t = jnp.tanh(GELU_C * (x + GELU_A * x * x * x))
    return 0.5 * (1.0 + t) + 0.5 * x * (1.0 - t * t) * GELU_C * (1.0 + 3.0 * GELU_A * x * x)


def _tril_bf16(w):
    row = lax.broadcasted_iota(I32, (SGU_CHUNK, SGU_CHUNK), 0)
    col = lax.broadcasted_iota(I32, (SGU_CHUNK, SGU_CHUNK), 1)
    return jnp.where(row >= col, w, 0.0).astype(BF16)


def _sgu_norm(vg, g, b):
    mu = jnp.mean(vg, axis=-1, keepdims=True)
    cen = vg - mu
    rstd = lax.rsqrt(jnp.mean(cen * cen, axis=-1, keepdims=True) + EPS)
    xhat = cen * rstd
    return xhat, rstd, xhat * g + b


def sgu_fwd(z, ln_g, ln_b, w_sp, b_sp, *, name, tm=256):
    s = z.shape[0]
    tm = _row_tile(s, tm)

    def body(z_ref, g_ref, b_ref, w_ref, bs_ref, y_ref):
        u = _gelu(z_ref[:, :D_MODEL])
        _, _, vn = _sgu_norm(_gelu(z_ref[:, D_MODEL:]), g_ref[...], b_ref[...])
        vn = vn.astype(BF16)
        for grp in range(SGU_GROUPS):
            w = _tril_bf16(w_ref[grp])
            cols = slice(grp * LANES, (grp + 1) * LANES)
            for ch in range(tm // SGU_CHUNK):
                rows = slice(ch * SGU_CHUNK, (ch + 1) * SGU_CHUNK)
                mixed = jnp.dot(w, vn[rows, cols], preferred_element_type=F32) + bs_ref[grp]
                y_ref[rows, cols] = (u[rows, cols] * mixed).astype(BF16)

    full3 = pl.BlockSpec((SGU_GROUPS, SGU_CHUNK, SGU_CHUNK), lambda i: (0, 0, 0))
    return pl.pallas_call(
        body, grid=(s // tm,), in_specs=[_row_spec(tm, 2 * D_MODEL), _vec_spec(D_MODEL), _vec_spec(D_MODEL), full3, full3],
        out_specs=_row_spec(tm, D_MODEL), out_shape=jax.ShapeDtypeStruct((s, D_MODEL), BF16),
        compiler_params=_cp("parallel"), name=name)(z, ln_g, ln_b, w_sp, b_sp)


def sgu_bwd(z, dy, ln_g, ln_b, w_sp, b_sp, *, name, tm=256):
    s = z.shape[0]
    tm = _row_tile(s, tm)

    def body(z_ref, dy_ref, g_ref, b_ref, w_ref, bs_ref, dz_ref, dw_ref, dbs_ref, dg_ref, db_ref, dvn_buf):
        first = pl.program_id(0) == 0
        zu, zv = z_ref[:, :D_MODEL], z_ref[:, D_MODEL:]
        u = _gelu(zu)
        xhat, rstd, vn = _sgu_norm(_gelu(zv), g_ref[...], b_ref[...])
        vn = vn.astype(BF16)
        dyv = dy_ref[...]
        dmixed = dyv * u
        row = lax.broadcasted_iota(I32, (SGU_CHUNK, SGU_CHUNK), 0)
        col = lax.broadcasted_iota(I32, (SGU_CHUNK, SGU_CHUNK), 1)
        for grp in range(SGU_GROUPS):
            w = _tril_bf16(w_ref[grp])
            cols = slice(grp * LANES, (grp + 1) * LANES)
            dw = jnp.zeros((SGU_CHUNK, SGU_CHUNK), F32)
            dbs = jnp.zeros((SGU_CHUNK, 1), F32)
            for ch in range(tm // SGU_CHUNK):
                rows = slice(ch * SGU_CHUNK, (ch + 1) * SGU_CHUNK)
                vblk = vn[rows, cols]
                mixed = jnp.dot(w, vblk, preferred_element_type=F32) + bs_ref[grp]
                dz_ref[rows, cols] = (dyv[rows, cols] * mixed * _gelu_grad(zu[rows, cols])).astype(BF16)
                dm = dmixed[rows, cols]
                dmb = dm.astype(BF16)
                dvn_buf[rows, cols] = lax.dot_general(w, dmb, TN_DIMS, preferred_element_type=F32)
                dw += lax.dot_general(dmb, vblk, NT_DIMS, preferred_element_type=F32)
                dbs += jnp.sum(dm, axis=-1, keepdims=True)
            dw = jnp.where(row >= col, dw, 0.0)
            dbs = jnp.broadcast_to(dbs, (SGU_CHUNK, SGU_CHUNK))

            @pl.when(first)
            def _():
                dw_ref[grp] = dw
                dbs_ref[grp] = dbs

            @pl.when(jnp.logical_not(first))
            def _():
                dw_ref[grp] += dw
                dbs_ref[grp] += dbs

        dvn = dvn_buf[...]
        dxhat = dvn * g_ref[...]
        dvg = rstd * (dxhat - jnp.mean(dxhat, axis=-1, keepdims=True) - xhat * jnp.mean(dxhat * xhat, axis=-1, keepdims=True))
        dz_ref[:, D_MODEL:] = (dvg * _gelu_grad(zv)).astype(BF16)
        _accum(dg_ref, jnp.sum(dvn * xhat, axis=0, keepdims=True), first)
        _accum(db_ref, jnp.sum(dvn, axis=0, keepdims=True), first)

    full3 = pl.BlockSpec((SGU_GROUPS, SGU_CHUNK, SGU_CHUNK), lambda i: (0, 0, 0))
    s3 = jax.ShapeDtypeStruct((SGU_GROUPS, SGU_CHUNK, SGU_CHUNK), F32)
    vshape = jax.ShapeDtypeStruct((1, D_MODEL), F32)
    return pl.pallas_call(
        body, grid=(s // tm,),
        in_specs=[_row_spec(tm, 2 * D_MODEL), _row_spec(tm, D_MODEL), _vec_spec(D_MODEL), _vec_spec(D_MODEL), full3, full3],
        out_specs=[_row_spec(tm, 2 * D_MODEL), full3, full3, _vec_spec(D_MODEL), _vec_spec(D_MODEL)],
        out_shape=[jax.ShapeDtypeStruct((s, 2 * D_MODEL), BF16), s3, s3, vshape, vshape],
        scratch_shapes=[pltpu.VMEM((tm, D_MODEL), F32)],
        compiler_params=_cp("arbitrary"), name=name)(z, dy, ln_g, ln_b, w_sp, b_sp)


def _sigmoid(x):
    return 1.0 / (1.0 + jnp.exp(-x))


def ffn_up(h, w_gu, *, name, tm=512):
    s = h.shape[0]
    tm = _row_tile(s, tm)

    def body(h_ref, wg_ref, wu_ref, gu_ref, a_ref):
        hv = h_ref[...]
        g = jnp.dot(hv, wg_ref[...], preferred_element_type=F32)
        u = jnp.dot(hv, wu_ref[...], preferred_element_type=F32)
        gu_ref[0] = g.astype(BF16)
        gu_ref[1] = u.astype(BF16)
        a_ref[...] = (g * _sigmoid(g) * u).astype(BF16)

    return pl.pallas_call(
        body, grid=(2, s // tm),
        in_specs=[pl.BlockSpec((tm, D_MODEL), lambda j, i: (i, 0)),
                  pl.BlockSpec((None, D_MODEL, FF_HALF), lambda j, i: (j, 0, 0)),
                  pl.BlockSpec((None, D_MODEL, FF_HALF), lambda j, i: (j + 2, 0, 0))],
        out_specs=[pl.BlockSpec((2, tm, FF_HALF), lambda j, i: (0, i, j)), pl.BlockSpec((tm, FF_HALF), lambda j, i: (i, j))],
        out_shape=[jax.ShapeDtypeStruct((2, s, D_FF), BF16), jax.ShapeDtypeStruct((s, D_FF), BF16)],
        compiler_params=_cp("parallel", "parallel"), name=name)(h, w_gu, w_gu)


def ffn_dact(df, w_d, gu, *, name, tm=512):
    s = df.shape[0]
    tm = _row_tile(s, tm)

    def body(df_ref, w_ref, gu_ref, o_ref):
        da = lax.dot_general(df_ref[...], w_ref[...], NT_DIMS, preferred_element_type=F32)
        g = gu_ref[0].astype(F32)
        u = gu_ref[1].astype(F32)
        sig = _sigmoid(g)
        o_ref[0] = (da * u * sig * (1.0 + g * (1.0 - sig))).astype(BF16)
        o_ref[1] = (da * g * sig).astype(BF16)

    planes = pl.BlockSpec((2, tm, FF_HALF), lambda j, i: (0, i, j))
    return pl.pallas_call(
        body, grid=(2, s // tm),
        in_specs=[pl.BlockSpec((tm, D_MODEL), lambda j, i: (i, 0)), pl.BlockSpec((FF_HALF, D_MODEL), lambda j, i: (j, 0)), planes],
        out_specs=planes, out_shape=jax.ShapeDtypeStruct((2, s, D_FF), BF16),
        compiler_params=_cp("parallel", "parallel"), name=name)(df, w_d, gu)


def _lead_spec(shape, tr):
    if len(shape) == 3:
        return pl.BlockSpec((None, tr, shape[2]), lambda l, r: (l, r, 0))
    return pl.BlockSpec((tr, shape[1]), lambda l, r: (r, 0))


def _weight_tile(rows):
    for tr in (512, 352, 256, 128):
        if rows % tr == 0:
            return tr
    return rows


def cast_bf16(w, layer, *, name):
    _, r, c = w.shape
    tr = _weight_tile(r)

    def body(w_ref, o_ref):
        o_ref[...] = w_ref[...].astype(BF16)

    return pl.pallas_call(
        body, grid=(r // tr,), in_specs=[pl.BlockSpec((None, tr, c), lambda i: (layer, i, 0))],
        out_specs=pl.BlockSpec((tr, c), lambda i: (i, 0)), out_shape=jax.ShapeDtypeStruct((r, c), BF16),
        compiler_params=_cp("parallel"), name=name)(w)


def _adamw_math(w, g, m, v):
    m = ADAM_B1 * m + (1.0 - ADAM_B1) * g
    v = ADAM_B2 * v + (1.0 - ADAM_B2) * (g * g)
    m_hat = m / (1.0 - ADAM_B1 ** ADAM_STEP)
    v_hat = v / (1.0 - ADAM_B2 ** ADAM_STEP)
    delta = -ADAM_LR * (m_hat / (jnp.sqrt(v_hat) + ADAM_EPS) + ADAM_WD * w)
    return delta, m, v


def adamw(w, g, m, v, *, name):
    nl, r, c = w.shape
    tr = _weight_tile(r)

    def body(w_ref, g_ref, m_ref, v_ref, go_ref, d_ref, mo_ref, vo_ref):
        gv = g_ref[...]
        go_ref[...] = gv
        d_ref[...], mo_ref[...], vo_ref[...] = _adamw_math(w_ref[...], gv, m_ref[...], v_ref[...])

    spec = _lead_spec(w.shape, tr)
    shape = jax.ShapeDtypeStruct(w.shape, F32)
    return pl.pallas_call(
        body, grid=(nl, r // tr), in_specs=[spec] * 4, out_specs=[spec] * 4, out_shape=[shape] * 4,
        compiler_params=_cp("parallel", "parallel"), name=name)(w, g, m, v)


def adamw_small(ws, gs, ms, vs, *, name):
    n = len(ws)

    def body(*refs):
        ins, outs = refs[:4 * n], refs[4 * n:]
        for t in range(n):
            gv = ins[n + t][...]
            outs[t][...] = gv
            outs[n + t][...], outs[2 * n + t][...], outs[3 * n + t][...] = _adamw_math(
                ins[t][...], gv, ins[2 * n + t][...], ins[3 * n + t][...])

    shapes = [jax.ShapeDtypeStruct(w.shape, F32) for w in ws]
    res = pl.pallas_call(body, out_shape=shapes * 4, name=name)(*ws, *gs, *ms, *vs)
    return res[:n], res[n:2 * n], res[2 * n:3 * n], res[3 * n:]


ANY = pl.BlockSpec(memory_space=pl.ANY)


def _place():
    return lax.axis_index("x"), lax.axis_index("y"), lax.axis_index("c")


def _partner(x, y, k):
    return (1 - x if k >> 1 else x), (1 - y if k & 1 else y)


def _half(rows, sel, align):
    return pl.ds(pl.multiple_of(sel * (rows // 2), align), rows // 2)


def _align(dtype):
    return 16 if dtype == BF16 else 8


def all_gather_shards(shards, *, name):
    nt = len(shards)

    def body(*refs):
        ins, outs = refs[:nt], refs[nt:2 * nt]
        ici_send, ici_recv, d2d_send, d2d_recv, local_sem = refs[2 * nt:]
        x, y, c = _place()
        j = 2 * x + y
        pending = []
        for t in range(nt):
            cp = pltpu.make_async_copy(ins[t], outs[t].at[j], local_sem.at[t])
            cp.start()
            pending.append(cp)

        def mine(t, sel):
            rows = ins[t].shape[0]
            return _half(rows, sel, _align(ins[t].dtype))

        def ici(t, k):
            px, py = _partner(x, y, k)
            return pltpu.make_async_remote_copy(
                src_ref=ins[t].at[mine(t, c)], dst_ref=outs[t].at[j, mine(t, c)], send_sem=ici_send.at[3 * t + k - 1],
                recv_sem=ici_recv.at[3 * t + k - 1], device_id=(px, py, c), device_id_type=MESH)

        def landed(t, k, sel):
            px, py = _partner(x, y, k)
            return outs[t].at[2 * px + py, mine(t, sel)]

        def d2d(t, k, sel):
            return pltpu.make_async_remote_copy(
                src_ref=landed(t, k, sel), dst_ref=landed(t, k, sel), send_sem=d2d_send.at[3 * t + k - 1],
                recv_sem=d2d_recv.at[3 * t + k - 1], device_id=(x, y, 1 - c), device_id_type=MESH)

        sends = []
        for t in range(nt):
            for k in (1, 2, 3):
                cp = ici(t, k)
                cp.start()
                sends.append(cp)
        for t in range(nt):
            for k in (1, 2, 3):
                pltpu.make_async_remote_copy(
                    src_ref=landed(t, k, c), dst_ref=landed(t, k, c), send_sem=ici_send.at[3 * t + k - 1],
                    recv_sem=ici_recv.at[3 * t + k - 1], device_id=(x, y, c), device_id_type=MESH).wait_recv()
                cp = d2d(t, k, c)
                cp.start()
                sends.append(cp)
        for t in range(nt):
            for k in (1, 2, 3):
                d2d(t, k, 1 - c).wait_recv()
        for cp in sends:
            cp.wait_send()
        for cp in pending:
            cp.wait()

    return pl.pallas_call(
        body, in_specs=[ANY] * nt, out_specs=[ANY] * nt,
        out_shape=[jax.ShapeDtypeStruct((N_CHIPS,) + w.shape, w.dtype) for w in shards],
        scratch_shapes=[pltpu.SemaphoreType.DMA((3 * nt,))] * 4 + [pltpu.SemaphoreType.DMA((nt,))], name=name)(*shards)


def pair_exchange(grads, *, name):
    nt = len(grads)

    def body(*refs):
        ins, outs = refs[:nt], refs[nt:2 * nt]
        send_sem, recv_sem = refs[2 * nt:]
        x, y, c = _place()

        def copy(t, sel):
            rows = ins[t].shape[1]
            return pltpu.make_async_remote_copy(
                src_ref=ins[t].at[:, _half(rows, sel, _align(ins[t].dtype))], dst_ref=outs[t], send_sem=send_sem.at[t],
                recv_sem=recv_sem.at[t], device_id=(x, y, 1 - c), device_id_type=MESH)

        cps = [copy(t, 1 - c) for t in range(nt)]
        for cp in cps:
            cp.start()
        for cp in cps:
            cp.wait_recv()
        for cp in cps:
            cp.wait_send()

    return pl.pallas_call(
        body, in_specs=[ANY] * nt, out_specs=[ANY] * nt,
        out_shape=[jax.ShapeDtypeStruct((N_CHIPS, g.shape[1] // 2, g.shape[2]), g.dtype) for g in grads],
        scratch_shapes=[pltpu.SemaphoreType.DMA((nt,))] * 2, name=name)(*grads)


def chip_scatter(parts, *, name):
    nt = len(parts)

    def body(*refs):
        ins, outs = refs[:nt], refs[nt:2 * nt]
        send_sem, recv_sem = refs[2 * nt:]
        x, y, c = _place()

        def copy(t, k):
            px, py = _partner(x, y, k)
            return pltpu.make_async_remote_copy(
                src_ref=ins[t].at[2 * px + py], dst_ref=outs[t].at[k - 1], send_sem=send_sem.at[3 * t + k - 1],
                recv_sem=recv_sem.at[3 * t + k - 1], device_id=(px, py, c), device_id_type=MESH)

        cps = [copy(t, k) for t in range(nt) for k in (1, 2, 3)]
        for cp in cps:
            cp.start()
        for cp in cps:
            cp.wait_recv()
        for cp in cps:
            cp.wait_send()

    return pl.pallas_call(
        body, in_specs=[ANY] * nt, out_specs=[ANY] * nt,
        out_shape=[jax.ShapeDtypeStruct((3,) + p.shape[1:], p.dtype) for p in parts],
        scratch_shapes=[pltpu.SemaphoreType.DMA((3 * nt,))] * 2, name=name)(*parts)


def pair_broadcast(halves, dests, out_shapes, *, name):
    nt = len(halves)
    no = len(out_shapes)

    def body(*refs):
        ins, outs = refs[:nt], refs[nt:nt + no]
        send_sem, recv_sem, local_sem = refs[nt + no:]
        x, y, c = _place()

        def rows_of(t, sel):
            oi, layer = dests[t]
            h = ins[t].shape[0]
            ref = outs[oi] if layer is None else outs[oi].at[layer]
            return ref.at[_half(2 * h, sel, 8)]

        def remote(t, sel):
            return pltpu.make_async_remote_copy(
                src_ref=ins[t], dst_ref=rows_of(t, sel), send_sem=send_sem.at[t], recv_sem=recv_sem.at[t],
                device_id=(x, y, 1 - c), device_id_type=MESH)

        local = [pltpu.make_async_copy(ins[t], rows_of(t, c), local_sem.at[t]) for t in range(nt)]
        sends = [remote(t, c) for t in range(nt)]
        for cp in local + sends:
            cp.start()
        for t in range(nt):
            remote(t, 1 - c).wait_recv()
        for cp in sends:
            cp.wait_send()
        for cp in local:
            cp.wait()

    return pl.pallas_call(
        body, in_specs=[ANY] * nt, out_specs=[ANY] * no, out_shape=[jax.ShapeDtypeStruct(s, F32) for s in out_shapes],
        scratch_shapes=[pltpu.SemaphoreType.DMA((nt,))] * 3, name=name)(*halves)


def pair_add(g, r1, c_arr, *, name):
    _, rows, cdim = g.shape
    h = rows // 2

    def body(c_ref, g_ref, r_ref, o_ref):
        o_ref[...] = (g_ref[...].astype(F32) + r_ref[...].astype(F32)).astype(o_ref.dtype)

    return pl.pallas_call(
        body,
        grid_spec=pltpu.PrefetchScalarGridSpec(
            num_scalar_prefetch=1, grid=(N_CHIPS,),
            in_specs=[pl.BlockSpec((None, h, cdim), lambda s, c_ref: (s, c_ref[0], 0)),
                      pl.BlockSpec((None, h, cdim), lambda s, c_ref: (s, 0, 0))],
            out_specs=pl.BlockSpec((None, h, cdim), lambda s, c_ref: (s, 0, 0))),
        out_shape=jax.ShapeDtypeStruct((N_CHIPS, h, cdim), g.dtype), compiler_params=_cp("parallel"), name=name)(c_arr, g, r1)


def final_add(g, r1, r2, jc_arr, *, name):
    _, rows, cdim = g.shape
    h = rows // 2

    def body(jc_ref, g_ref, r1_ref, r2_ref, o_ref):
        acc = g_ref[...].astype(F32) + r1_ref[...].astype(F32)
        for k in range(3):
            acc = acc + r2_ref[k].astype(F32)
        o_ref[...] = acc

    return pl.pallas_call(
        body,
        grid_spec=pltpu.PrefetchScalarGridSpec(
            num_scalar_prefetch=1, grid=(1,),
            in_specs=[pl.BlockSpec((None, h, cdim), lambda i, jc: (jc[0], jc[1], 0)),
                      pl.BlockSpec((None, h, cdim), lambda i, jc: (jc[0], 0, 0)),
                      pl.BlockSpec((3, h, cdim), lambda i, jc: (0, 0, 0))],
            out_specs=pl.BlockSpec((h, cdim), lambda i, jc: (0, 0))),
        out_shape=jax.ShapeDtypeStruct((h, cdim), F32), compiler_params=_cp("arbitrary"), name=name)(jc_arr, g, r1, r2)


SLAB_ROWS = 192


def _pad_rows(a, rows=8):
    return jnp.pad(a, ((0, rows - a.shape[0]), (0, 0)))


def _pack_small(norm_grads, db_qkv, db_o, dsinks, db_sp, dln_g, dln_b, dw_sp):
    parts = [
        jnp.concatenate(norm_grads, axis=0),
        _pad_rows(jnp.pad(db_qkv, ((0, 0), (0, 2 * D_MODEL - QKV_WIDTH))).reshape(2, D_MODEL)),
        _pad_rows(db_o),
        _pad_rows(jnp.pad(dsinks.reshape(1, N_Q_HEADS), ((0, 0), (0, D_MODEL - N_Q_HEADS)))),
        _pad_rows(db_sp.reshape(1, D_MODEL)),
        _pad_rows(jnp.concatenate([dln_g, dln_b], axis=0)),
        dw_sp.reshape(SGU_CHUNK, D_MODEL),
    ]
    slab = jnp.concatenate(parts, axis=0)
    return jnp.pad(slab, ((0, SLAB_ROWS - slab.shape[0]), (0, 0))).reshape(N_CHIPS, SLAB_ROWS // N_CHIPS, D_MODEL)


def _unpack_small(slab, j):
    slab = slab.reshape(SLAB_ROWS, D_MODEL)
    norms = [slab[2 * i:2 * i + 2] for i in range(4)]
    db_qkv = slab[8:10].reshape(1, 2 * D_MODEL)[:, :QKV_WIDTH]
    db_o = slab[16:17]
    dsinks = slab[24:25, :N_Q_HEADS]
    db_sp = slab[32:33].reshape(SGU_GROUPS, SGU_CHUNK)
    width = D_MODEL // N_CHIPS
    dln_g = lax.dynamic_slice(slab[40:41], (0, j * width), (1, width))
    dln_b = lax.dynamic_slice(slab[41:42], (0, j * width), (1, width))
    dw_sp = slab[48:48 + SGU_CHUNK].reshape(SGU_GROUPS * SGU_CHUNK, SGU_CHUNK)
    return norms, db_qkv, db_o, dsinks, db_sp, dln_g, dln_b, dw_sp


def _ffn_backward(df, h, gu, a, w_gu, w_d, tag):
    dgu = ffn_dact(df, w_d, gu, name=f"ffn_dact_{tag}")
    dw_d = mm_tn(a, df, shard_major=False, tm=FF_HALF, tn=D_MODEL, name=f"dw_down_{tag}")
    dw_gu = mm_tn(h, dgu, shard_major=True, tm=D_MODEL, tn=FF_HALF, name=f"dw_gate_up_{tag}")
    dh = mm_nt(dgu, w_gu, out_dtype=F32, tm=1024, name=f"dh_ffn_{tag}")
    return dh, dw_gu, dw_d


def kernel(x, norm_mix_pre, norm_mix_post, norm_ffn_pre, norm_ffn_post, attn_w_qkv, attn_b_qkv, attn_sinks, attn_w_o, attn_b_o, sgu_w_in, sgu_ln_g, sgu_ln_b, sgu_w_spatial, sgu_b_spatial, sgu_w_out, ffn_w_gate_up, ffn_w_down, loss_target, m_norm_mix_pre, m_norm_mix_post, m_norm_ffn_pre, m_norm_ffn_post, m_attn_w_qkv, m_attn_b_qkv, m_attn_sinks, m_attn_w_o, m_attn_b_o, m_sgu_w_in, m_sgu_ln_g, m_sgu_ln_b, m_sgu_w_spatial, m_sgu_b_spatial, m_sgu_w_out, m_ffn_w_gate_up, m_ffn_w_down, v_norm_mix_pre, v_norm_mix_post, v_norm_ffn_pre, v_norm_ffn_post, v_attn_w_qkv, v_attn_b_qkv, v_attn_sinks, v_attn_w_o, v_attn_b_o, v_sgu_w_in, v_sgu_ln_g, v_sgu_ln_b, v_sgu_w_spatial, v_sgu_b_spatial, v_sgu_w_out, v_ffn_w_gate_up, v_ffn_w_down):
    s = x.shape[1]
    x0 = x.reshape(s, D_MODEL)
    target = loss_target.reshape(s, D_MODEL)
    mx, my, mc = lax.axis_index("x"), lax.axis_index("y"), lax.axis_index("c")
    chip = 2 * mx + my
    c_arr = jnp.reshape(mc, (1,)).astype(I32)
    jc_arr = jnp.stack([chip, mc]).astype(I32)
    zero_bias = jnp.zeros((1, D_MODEL), F32)

    def gain(p, i):
        return p[i:i + 1]

    big = [attn_w_qkv, attn_w_o, sgu_w_in, sgu_w_out, ffn_w_gate_up, ffn_w_gate_up, ffn_w_down, ffn_w_down]
    layers = [0, 0, 0, 0, 0, 1, 0, 1]
    tags = ["qkv", "wo", "win", "wout", "wgu0", "wgu1", "wd0", "wd1"]
    shards = [cast_bf16(w, l, name=f"cast_{t}") for w, l, t in zip(big, layers, tags)]
    ln_pack = _pad_rows(jnp.concatenate([sgu_ln_g, sgu_ln_b], axis=0), 16)
    w_qkv, w_o, w_in, w_out, w_gu0, w_gu1, w_d0, w_d1, ln_full = all_gather_shards(shards + [ln_pack], name="gather_weights")
    ln_g = ln_full[:, 0, :].reshape(1, D_MODEL)
    ln_b = ln_full[:, 1, :].reshape(1, D_MODEL)
    w_o = w_o.reshape(Q_WIDTH, D_MODEL)
    w_out = w_out.reshape(D_MODEL, D_MODEL)
    w_d0 = w_d0.reshape(D_FF, D_MODEL)
    w_d1 = w_d1.reshape(D_FF, D_MODEL)

    cos, sin = _rope_tables(s)
    sink_rows = jnp.broadcast_to(
        jnp.repeat(attn_sinks.reshape(N_KV_HEADS, GQA_GROUP), WINDOW, axis=1)[:, :, None], (N_KV_HEADS, ROWS, LANES))
    w_sp = sgu_w_spatial.reshape(SGU_GROUPS, SGU_CHUNK, SGU_CHUNK)
    b_sp = jnp.broadcast_to(sgu_b_spatial.reshape(SGU_GROUPS, SGU_CHUNK)[:, :, None], (SGU_GROUPS, SGU_CHUNK, LANES))

    h0 = prenorm(x0, gain(norm_mix_pre, 0), name="prenorm_0")
    qkv = mm_nn(h0, w_qkv, out_dtype=F32, name="qkv_proj")
    q, k, v = rope_fwd(qkv, attn_b_qkv, cos, sin, name="rope_fwd")
    q = q.reshape(N_KV_HEADS, GQA_GROUP, s, HEAD_DIM)
    o_heads = attn_fwd(q, k, v, sink_rows, name="attn_fwd")
    o = heads_to_rows(o_heads.reshape(N_Q_HEADS, s, HEAD_DIM), name="heads_to_rows")
    m0 = mm_nn(o, w_o, out_dtype=F32, name="attn_out_proj")
    x1, h1 = residual_norm(x0, m0, attn_b_o, gain(norm_mix_post, 0), gain(norm_ffn_pre, 0), name="residual_norm_0a")
    gu0, a0 = ffn_up(h1, w_gu0, name="ffn_up_0")
    f0 = mm_nn(a0, w_d0, out_dtype=F32, name="ffn_down_0")
    x2, h2 = residual_norm(x1, f0, zero_bias, gain(norm_ffn_post, 0), gain(norm_mix_pre, 1), name="residual_norm_0b")
    z = mm_nn(h2, w_in, out_dtype=F32, name="sgu_in_proj")
    y = sgu_fwd(z, ln_g, ln_b, w_sp, b_sp, name="sgu_fwd")
    m1 = mm_nn(y, w_out, out_dtype=F32, name="sgu_out_proj")
    x3, h3 = residual_norm(x2, m1, zero_bias, gain(norm_mix_post, 1), gain(norm_ffn_pre, 1), name="residual_norm_1a")
    gu1, a1 = ffn_up(h3, w_gu1, name="ffn_up_1")
    f1 = mm_nn(a1, w_d1, out_dtype=F32, name="ffn_down_1")
    dx4, df1, dg_fpost1, loss_part = loss_head(x3, f1, gain(norm_ffn_post, 1), target, name="loss_head")
    loss = lax.psum(loss_part[0, 0], ("x", "y", "c"))

    dh3, dw_gu1, dw_d1 = _ffn_backward(df1, h3, gu1, a1, w_gu1, w_d1, "1")
    dx3, dm1, dg_fpre1, dg_mpost1, _ = norm_bwd_pair(
        dx4, dh3, x3, gain(norm_ffn_pre, 1), m1, zero_bias, gain(norm_mix_post, 1), name="norm_bwd_1a")
    dy = mm_nt(dm1, w_out, out_dtype=F32, name="dy_sgu")
    dw_out = mm_tn(y, dm1, shard_major=False, tm=D_MODEL, tn=D_MODEL, name="dw_sgu_out")
    dz, dw_sp, db_sp, dln_g, dln_b = sgu_bwd(z, dy, ln_g, ln_b, w_sp, b_sp, name="sgu_bwd")
    dw_in = mm_tn(h2, dz, shard_major=True, tm=D_MODEL, tn=2 * D_MODEL // N_CHIPS, name="dw_sgu_in")
    dh2 = mm_nt(dz, w_in, out_dtype=F32, tm=1024, name="dh_sgu")
    dx2, df0, dg_mpre1, dg_fpost0, _ = norm_bwd_pair(
        dx3, dh2, x2, gain(norm_mix_pre, 1), f0, zero_bias, gain(norm_ffn_post, 0), name="norm_bwd_0b")
    dh1, dw_gu0, dw_d0 = _ffn_backward(df0, h1, gu0, a0, w_gu0, w_d0, "0")
    dx1, dm0, dg_fpre0, dg_mpost0, db_o = norm_bwd_pair(
        dx2, dh1, x1, gain(norm_ffn_pre, 0), m0, attn_b_o, gain(norm_mix_post, 0), name="norm_bwd_0a")
    do = mm_nt(dm0, w_o, out_dtype=BF16, name="do_attn")
    dw_o = mm_tn(o, dm0, shard_major=False, tm=D_MODEL, tn=D_MODEL, name="dw_attn_out")
    do_heads = rows_to_heads(do, name="rows_to_heads").reshape(N_KV_HEADS, GQA_GROUP, s, HEAD_DIM)
    dq, dkc, dkp, dvc, dvp, dsink = attn_bwd(q, k, v, sink_rows, do_heads, name="attn_bwd")
    dqkv, db_qkv = rope_bwd(dq.reshape(N_Q_HEADS, s, HEAD_DIM), dkc, dkp, dvc, dvp, cos, sin, name="rope_bwd")
    dw_qkv = mm_tn(h0, dqkv, shard_major=True, tm=D_MODEL, tn=QKV_WIDTH // N_CHIPS, name="dw_qkv")
    dh0 = mm_nt(dqkv, w_qkv, out_dtype=F32, tm=1024, name="dh_attn")
    grad_x, dg_mpre0 = norm_bwd_last(dx1, dh0, x0, gain(norm_mix_pre, 0), name="norm_bwd_in")

    norm_grads = [jnp.concatenate(p, axis=0) for p in
                  ((dg_mpre0, dg_mpre1), (dg_mpost0, dg_mpost1), (dg_fpre0, dg_fpre1), (dg_fpost0, dg_fpost1))]
    slab = _pack_small(norm_grads, db_qkv, db_o, dsink[:, :, 0, 0], db_sp[:, :, 0], dln_g, dln_b, dw_sp)
    grads = [dw_qkv, dw_o.reshape(N_CHIPS, Q_WIDTH // N_CHIPS, D_MODEL), dw_in,
             dw_out.reshape(N_CHIPS, D_MODEL // N_CHIPS, D_MODEL), dw_gu0, dw_gu1,
             dw_d0.reshape(N_CHIPS, D_FF // N_CHIPS, D_MODEL), dw_d1.reshape(N_CHIPS, D_FF // N_CHIPS, D_MODEL), slab]
    gtags = tags + ["small"]
    from_sibling = pair_exchange(grads, name="pair_exchange")
    pair_sums = [pair_add(g, r, c_arr, name=f"pair_add_{t}") for g, r, t in zip(grads, from_sibling, gtags)]
    from_chips = chip_scatter(pair_sums, name="chip_scatter")
    halves = [final_add(g, r1, r2, jc_arr, name=f"final_add_{t}")
              for g, r1, r2, t in zip(grads, from_sibling, from_chips, gtags)]
    dests = [(0, None), (1, None), (2, None), (3, None), (4, 0), (4, 1), (5, 0), (5, 1), (6, None)]
    out_shapes = [attn_w_qkv.shape[1:], attn_w_o.shape[1:], sgu_w_in.shape[1:], sgu_w_out.shape[1:],
                  ffn_w_gate_up.shape, ffn_w_down.shape, (SLAB_ROWS // N_CHIPS, D_MODEL)]
    g_qkv, g_wo, g_win, g_wout, g_wgu, g_wd, slab_shard = pair_broadcast(halves, dests, out_shapes, name="pair_broadcast")
    (slab_full,) = all_gather_shards([slab_shard], name="gather_small")
    g_norms, g_bqkv, g_bo, g_sinks, g_bsp, g_lng, g_lnb, g_wsp = _unpack_small(slab_full, chip)

    def big_update(w, g, m, v, tag):
        return adamw(w, g.reshape(w.shape), m, v, name=f"adamw_{tag}")

    upd = {
        "attn_w_qkv": big_update(attn_w_qkv, g_qkv, m_attn_w_qkv, v_attn_w_qkv, "qkv"),
        "attn_w_o": big_update(attn_w_o, g_wo, m_attn_w_o, v_attn_w_o, "wo"),
        "sgu_w_in": big_update(sgu_w_in, g_win, m_sgu_w_in, v_sgu_w_in, "win"),
        "sgu_w_out": big_update(sgu_w_out, g_wout, m_sgu_w_out, v_sgu_w_out, "wout"),
        "ffn_w_gate_up": big_update(ffn_w_gate_up, g_wgu, m_ffn_w_gate_up, v_ffn_w_gate_up, "wgu"),
        "ffn_w_down": big_update(ffn_w_down, g_wd, m_ffn_w_down, v_ffn_w_down, "wd"),
    }
    small_names = ["norm_mix_pre", "norm_mix_post", "norm_ffn_pre", "norm_ffn_post", "attn_b_qkv", "attn_sinks", "attn_b_o",
                   "sgu_ln_g", "sgu_ln_b", "sgu_w_spatial", "sgu_b_spatial"]
    small_w = [norm_mix_pre, norm_mix_post, norm_ffn_pre, norm_ffn_post, attn_b_qkv, attn_sinks, attn_b_o, sgu_ln_g, sgu_ln_b,
               sgu_w_spatial, sgu_b_spatial]
    small_m = [m_norm_mix_pre, m_norm_mix_post, m_norm_ffn_pre, m_norm_ffn_post, m_attn_b_qkv, m_attn_sinks, m_attn_b_o,
               m_sgu_ln_g, m_sgu_ln_b, m_sgu_w_spatial, m_sgu_b_spatial]
    small_v = [v_norm_mix_pre, v_norm_mix_post, v_norm_ffn_pre, v_norm_ffn_post, v_attn_b_qkv, v_attn_sinks, v_attn_b_o,
               v_sgu_ln_g, v_sgu_ln_b, v_sgu_w_spatial, v_sgu_b_spatial]
    small_g = g_norms + [g_bqkv, g_sinks, g_bo, g_lng, g_lnb, g_wsp, g_bsp]

    def flat2(a):
        return a.reshape(-1, a.shape[-1])

    res = adamw_small([flat2(a) for a in small_w], [flat2(a) for a in small_g], [flat2(a) for a in small_m],
                      [flat2(a) for a in small_v], name="adamw_small")
    for i, nm in enumerate(small_names):
        upd[nm] = tuple(r[i].reshape(small_w[i].shape) for r in res)

    order = ["norm_mix_pre", "norm_mix_post", "norm_ffn_pre", "norm_ffn_post", "attn_w_qkv", "attn_b_qkv", "attn_sinks",
             "attn_w_o", "attn_b_o", "sgu_w_in", "sgu_ln_g", "sgu_ln_b", "sgu_w_spatial", "sgu_b_spatial", "sgu_w_out",
             "ffn_w_gate_up", "ffn_w_down"]
    outs = [loss, grad_x.reshape(1, s, D_MODEL)]
    for part in range(4):
        outs += [upd[nm][part] for nm in order]
    return tuple(outs)
```

```python
import types

import jax
import jax.numpy as jnp
from jax import lax
from jax.experimental import pallas as pl
from jax.experimental.pallas import tpu as pltpu

F32 = jnp.float32
BF16 = jnp.bfloat16
I32 = jnp.int32

D_MODEL = 1024
HEAD_DIM = 64
N_Q_HEADS = 16
N_KV_HEADS = 4
GQA_GROUP = 4
WINDOW = 128
Q_WIDTH = 1024
KV_WIDTH = 256
QKV_WIDTH = 1536
ROPE_THETA = 10000.0
SGU_GROUPS = 8
SGU_CHUNK = 128
D_FF = 2816
FF_HALF = D_FF // 2
EPS = 1e-6
N_CHIPS = 4
LANES = 128

ADAM_LR = 0.001
ADAM_B1 = 0.9
ADAM_B2 = 0.999
ADAM_EPS = 1e-08
ADAM_WD = 0.01
ADAM_STEP = 10

VMEM_LIMIT = 52 * 1024 * 1024
MESH = pl.DeviceIdType.MESH
NEG = -1e30
NT_DIMS = (((1,), (1,)), ((), ()))
TN_DIMS = (((0,), (0,)), ((), ()))
NN_DIMS = (((1,), (0,)), ((), ()))
ANY = pl.BlockSpec(memory_space=pl.ANY)


def _row_tile(s, want):
    return want if s % want == 0 else s


def _call(body, *, name, grid=(), in_specs=(), out_specs=(), out_shape=(), scratch_shapes=(), operands=(), prefetch=(),
          aliases=None, riders=(), sem=None):
    n_pre, n_in, n_out, n_scr = len(prefetch), len(operands), len(out_shape), len(scratch_shapes)
    in_specs, out_specs, out_shape = list(in_specs), list(out_specs), list(out_shape)
    operands, scratch_shapes = list(operands), list(scratch_shapes)
    io_alias = {n_pre + i: o for i, o in (aliases or {}).items()}
    for r in riders:
        base_in, base_out = n_pre + len(operands), len(out_shape)
        operands += list(r.inputs)
        in_specs += [ANY] * len(r.inputs)
        for pos, i in enumerate(r.aliased):
            io_alias[base_in + i] = base_out + pos
            out_shape.append(jax.ShapeDtypeStruct(r.inputs[i].shape, r.inputs[i].dtype))
        out_shape += list(r.fresh)
        out_specs += [ANY] * (len(r.aliased) + len(r.fresh))
        scratch_shapes += [pltpu.SemaphoreType.DMA((r.nsem,)), pltpu.SemaphoreType.DMA((r.nsem,))]

    def wrapped(*refs):
        pre, p = refs[:n_pre], n_pre
        core_in, p = refs[p:p + n_in], p + n_in
        r_in = []
        for r in riders:
            r_in.append(refs[p:p + len(r.inputs)])
            p += len(r.inputs)
        core_out, p = refs[p:p + n_out], p + n_out
        r_out = []
        for r in riders:
            k = len(r.aliased) + len(r.fresh)
            r_out.append(refs[p:p + k])
            p += k
        core_scr, p = refs[p:p + n_scr], p + n_scr
        r_sem = [refs[p + 2 * i:p + 2 * i + 2] for i in range(len(riders))]

        def edge(at_last, fns):
            def run():
                for i, r in enumerate(riders):
                    getattr(r, fns)(r_in[i], r_out[i], r_sem[i][0], r_sem[i][1])
            if not riders:
                return
            if not grid:
                run()
                return
            cond = None
            for d, n in enumerate(grid):
                c = pl.program_id(d) == (n - 1 if at_last else 0)
                cond = c if cond is None else jnp.logical_and(cond, c)
            pl.when(cond)(run)

        edge(False, "start")
        if body is not None:
            body(*pre, *core_in, *core_out, *core_scr)
        edge(True, "finish")

    if sem is None or riders:
        sem = ("arbitrary",) * len(grid)
    kwargs = dict(out_shape=out_shape, input_output_aliases=io_alias, name=name)
    if grid:
        kwargs["compiler_params"] = pltpu.CompilerParams(dimension_semantics=sem, vmem_limit_bytes=VMEM_LIMIT)
    if n_pre:
        kwargs["grid_spec"] = pltpu.PrefetchScalarGridSpec(
            num_scalar_prefetch=n_pre, grid=grid, in_specs=in_specs, out_specs=out_specs, scratch_shapes=scratch_shapes)
    else:
        kwargs.update(grid=grid, in_specs=in_specs, out_specs=out_specs, scratch_shapes=scratch_shapes)
    res = pl.pallas_call(wrapped, **kwargs)(*prefetch, *operands)
    core, rest, rider_res = list(res[:n_out]), list(res[n_out:]), []
    for r in riders:
        k = len(r.aliased) + len(r.fresh)
        rider_res.append(rest[:k])
        rest = rest[k:]
    return core, rider_res


def _mm_call(*, grid, in_specs, out_spec, out_shape, dims, nk, kaxis, acc_shape, name, operands, riders=()):
    out_dtype = out_shape.dtype

    def body(a_ref, b_ref, o_ref, *scratch):
        p = lax.dot_general(a_ref[...].astype(BF16), b_ref[...].astype(BF16), dims, preferred_element_type=F32)
        if nk == 1:
            o_ref[...] = p.astype(out_dtype)
        else:
            acc = scratch[0]
            kk = pl.program_id(kaxis)

            @pl.when(kk == 0)
            def _():
                acc[...] = p

            @pl.when(kk > 0)
            def _():
                acc[...] += p

            @pl.when(kk == nk - 1)
            def _():
                o_ref[...] = acc[...].astype(out_dtype)

    sem = ["parallel"] * len(grid)
    if nk > 1:
        sem[kaxis] = "arbitrary"
    (out,), rider_res = _call(
        body, grid=grid, in_specs=in_specs, out_specs=[out_spec], out_shape=[out_shape],
        scratch_shapes=[pltpu.VMEM(acc_shape, F32)] if nk > 1 else [], operands=operands, name=name, riders=riders,
        sem=tuple(sem))
    return (out, rider_res) if riders else out


def mm_nn(a, w, *, out_dtype, name, tm=512, tn=512, riders=()):
    m, k = a.shape
    tm = _row_tile(m, tm)
    if w.ndim == 3:
        ns = w.shape[2]
        grid = (N_CHIPS, m // tm)
        w_spec = pl.BlockSpec((None, k, ns), lambda j, i: (j, 0, 0))
        o_spec = pl.BlockSpec((tm, ns), lambda j, i: (i, j))
        n = N_CHIPS * ns
    else:
        n = w.shape[1]
        grid = (n // tn, m // tm)
        w_spec = pl.BlockSpec((k, tn), lambda j, i: (0, j))
        o_spec = pl.BlockSpec((tm, tn), lambda j, i: (i, j))
    return _mm_call(grid=grid, in_specs=[pl.BlockSpec((tm, k), lambda j, i: (i, 0)), w_spec], out_spec=o_spec,
                    out_shape=jax.ShapeDtypeStruct((m, n), out_dtype), dims=NN_DIMS, nk=1, kaxis=0, acc_shape=None,
                    name=name, operands=(a, w), riders=riders)


def mm_nt(a, w, *, out_dtype, name, tm=512, tn=512, riders=()):
    if w.ndim == 2:
        m, n = a.shape
        kout = w.shape[0]
        tm = _row_tile(m, tm)
        return _mm_call(grid=(kout // tn, m // tm),
                        in_specs=[pl.BlockSpec((tm, n), lambda j, i: (i, 0)), pl.BlockSpec((tn, n), lambda j, i: (j, 0))],
                        out_spec=pl.BlockSpec((tm, tn), lambda j, i: (i, j)),
                        out_shape=jax.ShapeDtypeStruct((m, kout), out_dtype), dims=NT_DIMS, nk=1, kaxis=0,
                        acc_shape=None, name=name, operands=(a, w), riders=riders)
    _, kout, ns = w.shape
    if a.ndim == 3:
        m = a.shape[1]
        tm = _row_tile(m, tm)
        a_spec = pl.BlockSpec((None, tm, ns), lambda i, kk: (kk // 2, i, kk % 2))
    else:
        m = a.shape[0]
        tm = _row_tile(m, tm)
        a_spec = pl.BlockSpec((tm, ns), lambda i, kk: (i, kk))
    return _mm_call(grid=(m // tm, N_CHIPS),
                    in_specs=[a_spec, pl.BlockSpec((None, kout, ns), lambda i, kk: (kk, 0, 0))],
                    out_spec=pl.BlockSpec((tm, kout), lambda i, kk: (i, 0)),
                    out_shape=jax.ShapeDtypeStruct((m, kout), out_dtype), dims=NT_DIMS, nk=N_CHIPS, kaxis=1,
                    acc_shape=(tm, kout), name=name, operands=(a, w), riders=riders)


def mm_tn(a, b, *, shard_major, name, tm, tn, tk=512, out_dtype=BF16, riders=()):
    s, m = a.shape
    tk = _row_tile(s, tk)
    if b.ndim == 3:
        n = 2 * b.shape[2]
        b_spec = pl.BlockSpec((None, tk, tn), lambda i, j, kk: (j // 2, kk, j % 2))
    else:
        n = b.shape[1]
        b_spec = pl.BlockSpec((tk, tn), lambda i, j, kk: (kk, j))
    if shard_major:
        assert tn == n // N_CHIPS
        o_spec = pl.BlockSpec((None, tm, tn), lambda i, j, kk: (j, i, 0))
        o_shape = jax.ShapeDtypeStruct((N_CHIPS, m, tn), out_dtype)
    else:
        o_spec = pl.BlockSpec((tm, tn), lambda i, j, kk: (i, j))
        o_shape = jax.ShapeDtypeStruct((m, n), out_dtype)
    return _mm_call(grid=(m // tm, n // tn, s // tk),
                    in_specs=[pl.BlockSpec((tk, tm), lambda i, j, kk: (kk, i)), b_spec], out_spec=o_spec,
                    out_shape=o_shape, dims=TN_DIMS, nk=s // tk, kaxis=2, acc_shape=(tm, tn), name=name, operands=(a, b),
                    riders=riders)


def _rstd(x):
    return lax.rsqrt(jnp.mean(x * x, axis=-1, keepdims=True) + EPS)


def _rms_bwd(dy, x, g):
    r = _rstd(x)
    xhat = x * r
    gy = dy * g
    dx = r * (gy - xhat * jnp.mean(gy * xhat, axis=-1, keepdims=True))
    return dx, jnp.sum(dy * xhat, axis=0, keepdims=True)


def _accum(ref, val, first):
    @pl.when(first)
    def _():
        ref[...] = val

    @pl.when(jnp.logical_not(first))
    def _():
        ref[...] += val


def _row_spec(tm, width):
    return pl.BlockSpec((tm, width), lambda i: (i, 0))


def _vec_spec(width):
    return pl.BlockSpec((1, width), lambda i: (0, 0))


def _ret(core, rider_res, riders):
    core = core[0] if len(core) == 1 else core
    return (core, rider_res) if riders else core


def prenorm(x, g, *, name, tm=256, riders=()):
    s = x.shape[0]
    tm = _row_tile(s, tm)

    def body(x_ref, g_ref, h_ref):
        xv = x_ref[...]
        h_ref[...] = (xv * _rstd(xv) * g_ref[...]).astype(BF16)

    core, rr = _call(
        body, grid=(s // tm,), in_specs=[_row_spec(tm, D_MODEL), _vec_spec(D_MODEL)], out_specs=[_row_spec(tm, D_MODEL)],
        out_shape=[jax.ShapeDtypeStruct((s, D_MODEL), BF16)], operands=(x, g), sem=("parallel",), name=name, riders=riders)
    return _ret(core, rr, riders)


def residual_norm(x, m, bias, g_post, g_next, *, name, tm=256, riders=()):
    s = x.shape[0]
    tm = _row_tile(s, tm)

    def body(x_ref, m_ref, b_ref, gp_ref, gn_ref, xo_ref, h_ref):
        mv = m_ref[...] + b_ref[...]
        xn = x_ref[...] + mv * _rstd(mv) * gp_ref[...]
        xo_ref[...] = xn
        h_ref[...] = (xn * _rstd(xn) * gn_ref[...]).astype(BF16)

    core, rr = _call(
        body, grid=(s // tm,),
        in_specs=[_row_spec(tm, D_MODEL), _row_spec(tm, D_MODEL), _vec_spec(D_MODEL), _vec_spec(D_MODEL), _vec_spec(D_MODEL)],
        out_specs=[_row_spec(tm, D_MODEL), _row_spec(tm, D_MODEL)],
        out_shape=[jax.ShapeDtypeStruct((s, D_MODEL), F32), jax.ShapeDtypeStruct((s, D_MODEL), BF16)],
        operands=(x, m, bias, g_post, g_next), sem=("parallel",), name=name, riders=riders)
    return _ret(core, rr, riders)


def loss_head(x, f, g_post, target, *, name, tm=256, riders=()):
    s = x.shape[0]
    tm = _row_tile(s, tm)

    def body(x_ref, f_ref, g_ref, t_ref, dx_ref, df_ref, dg_ref, loss_ref):
        first = pl.program_id(0) == 0
        fv = f_ref[...]
        g = g_ref[...]
        err = x_ref[...] + fv * _rstd(fv) * g - t_ref[...]
        dx = err * (1.0 / D_MODEL)
        dx_ref[...] = dx
        df, dg = _rms_bwd(dx, fv, g)
        df_ref[...] = df.astype(BF16)
        _accum(dg_ref, dg, first)
        part = jnp.sum(jnp.sum(err * err, axis=-1, keepdims=True), axis=0, keepdims=True) * (0.5 / D_MODEL)
        _accum(loss_ref, jnp.broadcast_to(part, (8, LANES)), first)

    core, rr = _call(
        body, grid=(s // tm,),
        in_specs=[_row_spec(tm, D_MODEL), _row_spec(tm, D_MODEL), _vec_spec(D_MODEL), _row_spec(tm, D_MODEL)],
        out_specs=[_row_spec(tm, D_MODEL), _row_spec(tm, D_MODEL), _vec_spec(D_MODEL), pl.BlockSpec((8, LANES), lambda i: (0, 0))],
        out_shape=[jax.ShapeDtypeStruct((s, D_MODEL), F32), jax.ShapeDtypeStruct((s, D_MODEL), BF16),
                   jax.ShapeDtypeStruct((1, D_MODEL), F32), jax.ShapeDtypeStruct((8, LANES), F32)],
        operands=(x, f, g_post, target), name=name, riders=riders)
    return _ret(core, rr, riders)


def norm_bwd_pair(dres, dh, x, g_pre, m, bias, g_post, *, name, tm=256, riders=()):
    s = x.shape[0]
    tm = _row_tile(s, tm)

    def body(dres_ref, dh_ref, x_ref, gpre_ref, m_ref, b_ref, gpost_ref, dx_ref, dm_ref, dgpre_ref, dgpost_ref, db_ref):
        first = pl.program_id(0) == 0
        d1, dgpre = _rms_bwd(dh_ref[...], x_ref[...], gpre_ref[...])
        dx = dres_ref[...] + d1
        dx_ref[...] = dx
        dm, dgpost = _rms_bwd(dx, m_ref[...] + b_ref[...], gpost_ref[...])
        dm_ref[...] = dm.astype(BF16)
        _accum(dgpre_ref, dgpre, first)
        _accum(dgpost_ref, dgpost, first)
        _accum(db_ref, jnp.sum(dm, axis=0, keepdims=True), first)

    row, vec = _row_spec(tm, D_MODEL), _vec_spec(D_MODEL)
    vshape = jax.ShapeDtypeStruct((1, D_MODEL), F32)
    core, rr = _call(
        body, grid=(s // tm,), in_specs=[row, row, row, vec, row, vec, vec], out_specs=[row, row, vec, vec, vec],
        out_shape=[jax.ShapeDtypeStruct((s, D_MODEL), F32), jax.ShapeDtypeStruct((s, D_MODEL), BF16), vshape, vshape, vshape],
        operands=(dres, dh, x, g_pre, m, bias, g_post), name=name, riders=riders)
    return _ret(core, rr, riders)


def norm_bwd_last(dres, dh, x, g_pre, *, name, tm=256, riders=()):
    s = x.shape[0]
    tm = _row_tile(s, tm)

    def body(dres_ref, dh_ref, x_ref, g_ref, dx_ref, dg_ref):
        d1, dg = _rms_bwd(dh_ref[...], x_ref[...], g_ref[...])
        dx_ref[...] = dres_ref[...] + d1
        _accum(dg_ref, dg, pl.program_id(0) == 0)

    row, vec = _row_spec(tm, D_MODEL), _vec_spec(D_MODEL)
    core, rr = _call(
        body, grid=(s // tm,), in_specs=[row, row, row, vec], out_specs=[row, vec],
        out_shape=[jax.ShapeDtypeStruct((s, D_MODEL), F32), jax.ShapeDtypeStruct((1, D_MODEL), F32)],
        operands=(dres, dh, x, g_pre), name=name, riders=riders)
    return _ret(core, rr, riders)


def _rope_tables(s):
    half = HEAD_DIM // 2
    inv_freq = ROPE_THETA ** (-(jnp.arange(half, dtype=F32) * 2.0) / HEAD_DIM)
    ang = jnp.arange(s, dtype=I32).astype(F32)[:, None] * inv_freq[None, :]
    cos, sin = jnp.cos(ang), jnp.sin(ang)
    return jnp.tile(cos, (1, 4)), jnp.concatenate([-sin, sin, -sin, sin], axis=1)


def _swap_halves(x):
    lane = lax.broadcasted_iota(I32, x.shape, 1)
    return jnp.where((lane & (HEAD_DIM - 1)) < HEAD_DIM // 2, pltpu.roll(x, LANES - 32, 1), pltpu.roll(x, 32, 1))


N_ROPE_BLOCKS = (Q_WIDTH + KV_WIDTH) // LANES


def rope_fwd(qkv, bias, cos, sin, *, name, tm=256, riders=()):
    s = qkv.shape[0]
    tm = _row_tile(s, tm)

    def body(x_ref, b_ref, c_ref, s_ref, q_ref, k_ref, v_ref):
        cosv, sinv = c_ref[...], s_ref[...]
        for blk in range(QKV_WIDTH // LANES):
            cols = slice(blk * LANES, (blk + 1) * LANES)
            xb = x_ref[:, cols] + b_ref[:, cols]
            if blk < N_ROPE_BLOCKS:
                xb = xb * cosv + _swap_halves(xb) * sinv
            for e in range(2):
                head = 2 * blk + e
                piece = xb[:, e * HEAD_DIM:(e + 1) * HEAD_DIM].astype(BF16)
                if head < N_Q_HEADS:
                    q_ref[head] = piece
                elif head < N_Q_HEADS + N_KV_HEADS:
                    k_ref[head - N_Q_HEADS] = piece
                else:
                    v_ref[head - N_Q_HEADS - N_KV_HEADS] = piece

    def hm(nh):
        return pl.BlockSpec((nh, tm, HEAD_DIM), lambda i: (0, i, 0))

    core, rr = _call(
        body, grid=(s // tm,),
        in_specs=[_row_spec(tm, QKV_WIDTH), _vec_spec(QKV_WIDTH), _row_spec(tm, LANES), _row_spec(tm, LANES)],
        out_specs=[hm(N_Q_HEADS), hm(N_KV_HEADS), hm(N_KV_HEADS)],
        out_shape=[jax.ShapeDtypeStruct((N_Q_HEADS, s, HEAD_DIM), BF16), jax.ShapeDtypeStruct((N_KV_HEADS, s, HEAD_DIM), BF16),
                   jax.ShapeDtypeStruct((N_KV_HEADS, s, HEAD_DIM), BF16)],
        operands=(qkv, bias, cos, sin), sem=("parallel",), name=name, riders=riders)
    return _ret(core, rr, riders)


def rope_bwd(dq, dkc, dkp, dvc, dvp, cos, sin, *, name, riders=()):
    s = dq.shape[1]
    tm = WINDOW
    nb = s // tm

    def body(dq_ref, dkc_ref, dkp_ref, dvc_ref, dvp_ref, c_ref, s_ref, o_ref, db_ref, buf):
        i = pl.program_id(0)
        has_next = (i < nb - 1).astype(F32)
        for head in range(N_Q_HEADS):
            buf[:, head * HEAD_DIM:(head + 1) * HEAD_DIM] = dq_ref[head].astype(F32)
        for head in range(N_KV_HEADS):
            kcols = slice(Q_WIDTH + head * HEAD_DIM, Q_WIDTH + (head + 1) * HEAD_DIM)
            vcols = slice(Q_WIDTH + KV_WIDTH + head * HEAD_DIM, Q_WIDTH + KV_WIDTH + (head + 1) * HEAD_DIM)
            buf[:, kcols] = dkc_ref[head].astype(F32) + has_next * dkp_ref[head].astype(F32)
            buf[:, vcols] = dvc_ref[head].astype(F32) + has_next * dvp_ref[head].astype(F32)
        cosv, sinv = c_ref[...], s_ref[...]
        for blk in range(QKV_WIDTH // LANES):
            cols = slice(blk * LANES, (blk + 1) * LANES)
            g = buf[:, cols]
            if blk < N_ROPE_BLOCKS:
                g = g * cosv + _swap_halves(g * sinv)
            o_ref[:, cols] = g.astype(BF16)
            buf[:, cols] = g
        _accum(db_ref, jnp.sum(buf[...], axis=0, keepdims=True), i == 0)

    def hm(nh, shift):
        if shift:
            return pl.BlockSpec((nh, tm, HEAD_DIM), lambda i: (0, jnp.minimum(i + 1, nb - 1), 0))
        return pl.BlockSpec((nh, tm, HEAD_DIM), lambda i: (0, i, 0))

    core, rr = _call(
        body, grid=(nb,),
        in_specs=[hm(N_Q_HEADS, False), hm(N_KV_HEADS, False), hm(N_KV_HEADS, True), hm(N_KV_HEADS, False), hm(N_KV_HEADS, True),
                  _row_spec(tm, LANES), _row_spec(tm, LANES)],
        out_specs=[_row_spec(tm, QKV_WIDTH), _vec_spec(QKV_WIDTH)],
        out_shape=[jax.ShapeDtypeStruct((s, QKV_WIDTH), BF16), jax.ShapeDtypeStruct((1, QKV_WIDTH), F32)],
        scratch_shapes=[pltpu.VMEM((tm, QKV_WIDTH), F32)], operands=(dq, dkc, dkp, dvc, dvp, cos, sin), name=name, riders=riders)
    return _ret(core, rr, riders)


def heads_to_rows(o, *, name, tm=256, riders=()):
    s = o.shape[1]
    tm = _row_tile(s, tm)

    def body(o_ref, r_ref):
        for head in range(N_Q_HEADS):
            r_ref[:, head * HEAD_DIM:(head + 1) * HEAD_DIM] = o_ref[head]

    core, rr = _call(
        body, grid=(s // tm,), in_specs=[pl.BlockSpec((N_Q_HEADS, tm, HEAD_DIM), lambda i: (0, i, 0))],
        out_specs=[_row_spec(tm, Q_WIDTH)], out_shape=[jax.ShapeDtypeStruct((s, Q_WIDTH), BF16)], operands=(o,),
        sem=("parallel",), name=name, riders=riders)
    return _ret(core, rr, riders)


def rows_to_heads(x, *, name, tm=256, riders=()):
    s = x.shape[0]
    tm = _row_tile(s, tm)

    def body(x_ref, o_ref):
        xv = x_ref[...].astype(F32)
        for head in range(N_Q_HEADS):
            o_ref[head] = xv[:, head * HEAD_DIM:(head + 1) * HEAD_DIM].astype(BF16)

    core, rr = _call(
        body, grid=(s // tm,), in_specs=[_row_spec(tm, Q_WIDTH)],
        out_specs=[pl.BlockSpec((N_Q_HEADS, tm, HEAD_DIM), lambda i: (0, i, 0))],
        out_shape=[jax.ShapeDtypeStruct((N_Q_HEADS, s, HEAD_DIM), BF16)], operands=(x,), sem=("parallel",), name=name,
        riders=riders)
    return _ret(core, rr, riders)


ROWS = GQA_GROUP * WINDOW


def _attn_probs(q, kc, kp, sink, n):
    scale = HEAD_DIM ** -0.5
    sc = lax.dot_general(q, kc, NT_DIMS, preferred_element_type=F32) * scale
    sp = lax.dot_general(q, kp, NT_DIMS, preferred_element_type=F32) * scale
    qpos = lax.broadcasted_iota(I32, (ROWS, WINDOW), 0) & (WINDOW - 1)
    kpos = lax.broadcasted_iota(I32, (ROWS, WINDOW), 1)
    sc = jnp.where(kpos <= qpos, sc, NEG)
    sp = jnp.where(jnp.logical_and(kpos > qpos, n > 0), sp, NEG)
    m = jnp.maximum(jnp.maximum(jnp.max(sc, axis=-1, keepdims=True), jnp.max(sp, axis=-1, keepdims=True)), sink)
    ec, ep, es = jnp.exp(sc - m), jnp.exp(sp - m), jnp.exp(sink - m)
    inv = 1.0 / (jnp.sum(ec, axis=-1, keepdims=True) + jnp.sum(ep, axis=-1, keepdims=True) + es)
    return ec * inv, ep * inv, es * inv


def _attn_specs(s):
    q_spec = pl.BlockSpec((None, GQA_GROUP, WINDOW, HEAD_DIM), lambda h, n: (h, 0, n, 0))
    cur = pl.BlockSpec((None, WINDOW, HEAD_DIM), lambda h, n: (h, n, 0))
    prev = pl.BlockSpec((None, WINDOW, HEAD_DIM), lambda h, n: (h, jnp.maximum(n - 1, 0), 0))
    sink = pl.BlockSpec((None, ROWS, LANES), lambda h, n: (h, 0, 0))
    return q_spec, cur, prev, sink


def attn_fwd(q, k, v, sink_rows, *, name, riders=()):
    s = k.shape[1]

    def body(q_ref, kc_ref, kp_ref, vc_ref, vp_ref, sink_ref, o_ref):
        qv = q_ref[...].reshape(ROWS, HEAD_DIM)
        pc, pp, _ = _attn_probs(qv, kc_ref[...], kp_ref[...], sink_ref[...], pl.program_id(1))
        o = jnp.dot(pc.astype(BF16), vc_ref[...], preferred_element_type=F32)
        o += jnp.dot(pp.astype(BF16), vp_ref[...], preferred_element_type=F32)
        o_ref[...] = o.reshape(GQA_GROUP, WINDOW, HEAD_DIM).astype(BF16)

    q_spec, cur, prev, sink = _attn_specs(s)
    core, rr = _call(
        body, grid=(N_KV_HEADS, s // WINDOW), in_specs=[q_spec, cur, prev, cur, prev, sink], out_specs=[q_spec],
        out_shape=[jax.ShapeDtypeStruct(q.shape, BF16)], operands=(q, k, k, v, v, sink_rows), sem=("parallel", "parallel"),
        name=name, riders=riders)
    return _ret(core, rr, riders)


def attn_bwd(q, k, v, sink_rows, do, *, name, riders=()):
    s = k.shape[1]

    def body(q_ref, kc_ref, kp_ref, vc_ref, vp_ref, sink_ref, do_ref, dq_ref, dkc_ref, dkp_ref, dvc_ref, dvp_ref, dsink_ref):
        n = pl.program_id(1)
        qv = q_ref[...].reshape(ROWS, HEAD_DIM)
        dov = do_ref[...].reshape(ROWS, HEAD_DIM)
        kc, kp, vc, vp = kc_ref[...], kp_ref[...], vc_ref[...], vp_ref[...]
        pc, pp, ps = _attn_probs(qv, kc, kp, sink_ref[...], n)
        dpc = lax.dot_general(dov, vc, NT_DIMS, preferred_element_type=F32)
        dpp = lax.dot_general(dov, vp, NT_DIMS, preferred_element_type=F32)
        delta = jnp.sum(pc * dpc, axis=-1, keepdims=True) + jnp.sum(pp * dpp, axis=-1, keepdims=True)
        scale = HEAD_DIM ** -0.5
        dsc = (pc * (dpc - delta) * scale).astype(BF16)
        dsp = (pp * (dpp - delta) * scale).astype(BF16)
        dq = jnp.dot(dsc, kc, preferred_element_type=F32) + jnp.dot(dsp, kp, preferred_element_type=F32)
        dq_ref[...] = dq.reshape(GQA_GROUP, WINDOW, HEAD_DIM).astype(BF16)
        dkc_ref[...] = lax.dot_general(dsc, qv, TN_DIMS, preferred_element_type=F32).astype(BF16)
        dkp_ref[...] = lax.dot_general(dsp, qv, TN_DIMS, preferred_element_type=F32).astype(BF16)
        dvc_ref[...] = lax.dot_general(pc.astype(BF16), dov, TN_DIMS, preferred_element_type=F32).astype(BF16)
        dvp_ref[...] = lax.dot_general(pp.astype(BF16), dov, TN_DIMS, preferred_element_type=F32).astype(BF16)
        dsink = -(ps * delta)
        for g in range(GQA_GROUP):
            part = jnp.broadcast_to(jnp.sum(dsink[g * WINDOW:(g + 1) * WINDOW], axis=0, keepdims=True), (8, LANES))

            @pl.when(n == 0)
            def _():
                dsink_ref[g] = part

            @pl.when(n > 0)
            def _():
                dsink_ref[g] += part

    q_spec, cur, prev, sink = _attn_specs(s)
    kv_shape = jax.ShapeDtypeStruct(k.shape, BF16)
    core, rr = _call(
        body, grid=(N_KV_HEADS, s // WINDOW), in_specs=[q_spec, cur, prev, cur, prev, sink, q_spec],
        out_specs=[q_spec, cur, cur, cur, cur, pl.BlockSpec((None, GQA_GROUP, 8, LANES), lambda h, n: (h, 0, 0, 0))],
        out_shape=[jax.ShapeDtypeStruct(q.shape, BF16), kv_shape, kv_shape, kv_shape, kv_shape,
                   jax.ShapeDtypeStruct((N_KV_HEADS, GQA_GROUP, 8, LANES), F32)],
        operands=(q, k, k, v, v, sink_rows, do), sem=("parallel", "arbitrary"), name=name, riders=riders)
    return _ret(core, rr, riders)


GELU_C = 0.7978845608028654
GELU_A = 0.044715


def _gelu(x):
    return 0.5 * x * (1.0 + jnp.tanh(GELU_C * (x + GELU_A * x * x * x)))


def _gelu_grad(x):
    t = jnp.tanh(GELU_C * (x + GELU_A * x * x * x))
    return 0.5 * (1.0 + t) + 0.5 * x * (1.0 - t * t) * GELU_C * (1.0 + 3.0 * GELU_A * x * x)


def _tril_bf16(w):
    row = lax.broadcasted_iota(I32, (SGU_CHUNK, SGU_CHUNK), 0)
    col = lax.broadcasted_iota(I32, (SGU_CHUNK, SGU_CHUNK), 1)
    return jnp.where(row >= col, w, 0.0).astype(BF16)


def _sgu_norm(vg, g, b):
    mu = jnp.mean(vg, axis=-1, keepdims=True)
    cen = vg - mu
    rstd = lax.rsqrt(jnp.mean(cen * cen, axis=-1, keepdims=True) + EPS)
    xhat = cen * rstd
    return xhat, rstd, xhat * g + b


def sgu_fwd(z, ln_g, ln_b, w_sp, b_sp, *, name, tm=256, riders=()):
    s = z.shape[0]
    tm = _row_tile(s, tm)

    def body(z_ref, g_ref, b_ref, w_ref, bs_ref, y_ref):
        u = _gelu(z_ref[:, :D_MODEL])
        _, _, vn = _sgu_norm(_gelu(z_ref[:, D_MODEL:]), g_ref[...], b_ref[...])
        vn = vn.astype(BF16)
        for grp in range(SGU_GROUPS):
            w = _tril_bf16(w_ref[grp])
            cols = slice(grp * LANES, (grp + 1) * LANES)
            for ch in range(tm // SGU_CHUNK):
                rows = slice(ch * SGU_CHUNK, (ch + 1) * SGU_CHUNK)
                mixed = jnp.dot(w, vn[rows, cols], preferred_element_type=F32) + bs_ref[grp]
                y_ref[rows, cols] = (u[rows, cols] * mixed).astype(BF16)

    full3 = pl.BlockSpec((SGU_GROUPS, SGU_CHUNK, SGU_CHUNK), lambda i: (0, 0, 0))
    core, rr = _call(
        body, grid=(s // tm,), in_specs=[_row_spec(tm, 2 * D_MODEL), _vec_spec(D_MODEL), _vec_spec(D_MODEL), full3, full3],
        out_specs=[_row_spec(tm, D_MODEL)], out_shape=[jax.ShapeDtypeStruct((s, D_MODEL), BF16)],
        operands=(z, ln_g, ln_b, w_sp, b_sp), sem=("parallel",), name=name, riders=riders)
    return _ret(core, rr, riders)


def sgu_bwd(z, dy, ln_g, ln_b, w_sp, b_sp, *, name, tm=256, riders=()):
    s = z.shape[0]
    tm = _row_tile(s, tm)

    def body(z_ref, dy_ref, g_ref, b_ref, w_ref, bs_ref, dz_ref, dw_ref, dbs_ref, dg_ref, db_ref, dvn_buf):
        first = pl.program_id(0) == 0
        zu, zv = z_ref[:, :D_MODEL], z_ref[:, D_MODEL:]
        u = _gelu(zu)
        xhat, rstd, vn = _sgu_norm(_gelu(zv), g_ref[...], b_ref[...])
        vn = vn.astype(BF16)
        dyv = dy_ref[...]
        dmixed = dyv * u
        row = lax.broadcasted_iota(I32, (SGU_CHUNK, SGU_CHUNK), 0)
        col = lax.broadcasted_iota(I32, (SGU_CHUNK, SGU_CHUNK), 1)
        for grp in range(SGU_GROUPS):
            w = _tril_bf16(w_ref[grp])
            cols = slice(grp * LANES, (grp + 1) * LANES)
            dw = jnp.zeros((SGU_CHUNK, SGU_CHUNK), F32)
            dbs = jnp.zeros((SGU_CHUNK, 1), F32)
            for ch in range(tm // SGU_CHUNK):
                rows = slice(ch * SGU_CHUNK, (ch + 1) * SGU_CHUNK)
                vblk = vn[rows, cols]
                mixed = jnp.dot(w, vblk, preferred_element_type=F32) + bs_ref[grp]
                dz_ref[rows, cols] = (dyv[rows, cols] * mixed * _gelu_grad(zu[rows, cols])).astype(BF16)
                dm = dmixed[rows, cols]
                dmb = dm.astype(BF16)
                dvn_buf[rows, cols] = lax.dot_general(w, dmb, TN_DIMS, preferred_element_type=F32)
                dw += lax.dot_general(dmb, vblk, NT_DIMS, preferred_element_type=F32)
                dbs += jnp.sum(dm, axis=-1, keepdims=True)
            dw = jnp.where(row >= col, dw, 0.0)
            dbs = jnp.broadcast_to(dbs, (SGU_CHUNK, SGU_CHUNK))

            @pl.when(first)
            def _():
                dw_ref[grp] = dw
                dbs_ref[grp] = dbs

            @pl.when(jnp.logical_not(first))
            def _():
                dw_ref[grp] += dw
                dbs_ref[grp] += dbs

        dvn = dvn_buf[...]
        dxhat = dvn * g_ref[...]
        dvg = rstd * (dxhat - jnp.mean(dxhat, axis=-1, keepdims=True) - xhat * jnp.mean(dxhat * xhat, axis=-1, keepdims=True))
        dz_ref[:, D_MODEL:] = (dvg * _gelu_grad(zv)).astype(BF16)
        _accum(dg_ref, jnp.sum(dvn * xhat, axis=0, keepdims=True), first)
        _accum(db_ref, jnp.sum(dvn, axis=0, keepdims=True), first)

    full3 = pl.BlockSpec((SGU_GROUPS, SGU_CHUNK, SGU_CHUNK), lambda i: (0, 0, 0))
    s3 = jax.ShapeDtypeStruct((SGU_GROUPS, SGU_CHUNK, SGU_CHUNK), F32)
    vshape = jax.ShapeDtypeStruct((1, D_MODEL), F32)
    core, rr = _call(
        body, grid=(s // tm,),
        in_specs=[_row_spec(tm, 2 * D_MODEL), _row_spec(tm, D_MODEL), _vec_spec(D_MODEL), _vec_spec(D_MODEL), full3, full3],
        out_specs=[_row_spec(tm, 2 * D_MODEL), full3, full3, _vec_spec(D_MODEL), _vec_spec(D_MODEL)],
        out_shape=[jax.ShapeDtypeStruct((s, 2 * D_MODEL), BF16), s3, s3, vshape, vshape],
        scratch_shapes=[pltpu.VMEM((tm, D_MODEL), F32)], operands=(z, dy, ln_g, ln_b, w_sp, b_sp), name=name, riders=riders)
    return _ret(core, rr, riders)


def _sigmoid(x):
    return 1.0 / (1.0 + jnp.exp(-x))


def ffn_up(h, w_gu, *, name, tm=512, riders=()):
    s = h.shape[0]
    tm = _row_tile(s, tm)

    def body(h_ref, wg_ref, wu_ref, gu_ref, a_ref):
        hv = h_ref[...]
        g = jnp.dot(hv, wg_ref[...], preferred_element_type=F32)
        u = jnp.dot(hv, wu_ref[...], preferred_element_type=F32)
        gu_ref[0] = g.astype(BF16)
        gu_ref[1] = u.astype(BF16)
        a_ref[...] = (g * _sigmoid(g) * u).astype(BF16)

    core, rr = _call(
        body, grid=(2, s // tm),
        in_specs=[pl.BlockSpec((tm, D_MODEL), lambda j, i: (i, 0)),
                  pl.BlockSpec((None, D_MODEL, FF_HALF), lambda j, i: (j, 0, 0)),
                  pl.BlockSpec((None, D_MODEL, FF_HALF), lambda j, i: (j + 2, 0, 0))],
        out_specs=[pl.BlockSpec((2, tm, FF_HALF), lambda j, i: (0, i, j)), pl.BlockSpec((tm, FF_HALF), lambda j, i: (i, j))],
        out_shape=[jax.ShapeDtypeStruct((2, s, D_FF), BF16), jax.ShapeDtypeStruct((s, D_FF), BF16)],
        operands=(h, w_gu, w_gu), sem=("parallel", "parallel"), name=name, riders=riders)
    return _ret(core, rr, riders)


def ffn_dact(df, w_d, gu, *, name, tm=512, riders=()):
    s = df.shape[0]
    tm = _row_tile(s, tm)

    def body(df_ref, w_ref, gu_ref, o_ref):
        da = lax.dot_general(df_ref[...], w_ref[...], NT_DIMS, preferred_element_type=F32)
        g = gu_ref[0].astype(F32)
        u = gu_ref[1].astype(F32)
        sig = _sigmoid(g)
        o_ref[0] = (da * u * sig * (1.0 + g * (1.0 - sig))).astype(BF16)
        o_ref[1] = (da * g * sig).astype(BF16)

    planes = pl.BlockSpec((2, tm, FF_HALF), lambda j, i: (0, i, j))
    core, rr = _call(
        body, grid=(2, s // tm),
        in_specs=[pl.BlockSpec((tm, D_MODEL), lambda j, i: (i, 0)), pl.BlockSpec((FF_HALF, D_MODEL), lambda j, i: (j, 0)), planes],
        out_specs=[planes], out_shape=[jax.ShapeDtypeStruct((2, s, D_FF), BF16)], operands=(df, w_d, gu),
        sem=("parallel", "parallel"), name=name, riders=riders)
    return _ret(core, rr, riders)


def _weight_tile(rows):
    for tr in (512, 352, 256, 128):
        if rows % tr == 0:
            return tr
    return rows


def place_shard(w, layer, chip_arr, dtype, *, name):
    _, r, c = w.shape
    tr = _weight_tile(r)

    def body(chip_ref, w_ref, o_ref):
        o_ref[...] = w_ref[...].astype(dtype)

    (out,), _ = _call(
        body, grid=(r // tr,), prefetch=(chip_arr,),
        in_specs=[pl.BlockSpec((None, tr, c), lambda i, chip: (layer, i, 0))],
        out_specs=[pl.BlockSpec((None, tr, c), lambda i, chip: (chip[0], i, 0))],
        out_shape=[jax.ShapeDtypeStruct((N_CHIPS, r, c), dtype)], operands=(w,), sem=("parallel",), name=name)
    return out


def _adamw_math(w, g, m, v):
    m = ADAM_B1 * m + (1.0 - ADAM_B1) * g
    v = ADAM_B2 * v + (1.0 - ADAM_B2) * (g * g)
    m_hat = m / (1.0 - ADAM_B1 ** ADAM_STEP)
    v_hat = v / (1.0 - ADAM_B2 ** ADAM_STEP)
    delta = -ADAM_LR * (m_hat / (jnp.sqrt(v_hat) + ADAM_EPS) + ADAM_WD * w)
    return delta, m, v


def adamw(w, g, m, v, *, name):
    nl, r, c = w.shape
    tr = _weight_tile(r)

    def body(w_ref, g_ref, m_ref, v_ref, go_ref, d_ref, mo_ref, vo_ref):
        gv = g_ref[...]
        go_ref[...] = gv
        d_ref[...], mo_ref[...], vo_ref[...] = _adamw_math(w_ref[...], gv, m_ref[...], v_ref[...])

    spec = pl.BlockSpec((None, tr, c), lambda l, i: (l, i, 0))
    shape = jax.ShapeDtypeStruct(w.shape, F32)
    outs, _ = _call(body, grid=(nl, r // tr), in_specs=[spec] * 4, out_specs=[spec] * 4, out_shape=[shape] * 4,
                    operands=(w, g, m, v), sem=("parallel", "parallel"), name=name)
    return outs


def adamw_small(ws, gs, ms, vs, *, name):
    n = len(ws)

    def body(*refs):
        ins, outs = refs[:4 * n], refs[4 * n:]
        for t in range(n):
            gv = ins[n + t][...]
            outs[t][...] = gv
            outs[n + t][...], outs[2 * n + t][...], outs[3 * n + t][...] = _adamw_math(
                ins[t][...], gv, ins[2 * n + t][...], ins[3 * n + t][...])

    shapes = [jax.ShapeDtypeStruct(w.shape, F32) for w in ws]
    res = pl.pallas_call(body, out_shape=shapes * 4, name=name)(*ws, *gs, *ms, *vs)
    return res[:n], res[n:2 * n], res[2 * n:3 * n], res[3 * n:]


def pair_add(g, r1, c_arr, *, name):
    _, rows, cdim = g.shape
    h = rows // 2

    def body(c_ref, g_ref, r_ref, o_ref):
        o_ref[...] = (g_ref[...].astype(F32) + r_ref[...].astype(F32)).astype(o_ref.dtype)

    (out,), _ = _call(
        body, grid=(N_CHIPS,), prefetch=(c_arr,),
        in_specs=[pl.BlockSpec((None, h, cdim), lambda s, c: (s, c[0], 0)), pl.BlockSpec((None, h, cdim), lambda s, c: (s, 0, 0))],
        out_specs=[pl.BlockSpec((None, h, cdim), lambda s, c: (s, 0, 0))],
        out_shape=[jax.ShapeDtypeStruct((N_CHIPS, h, cdim), g.dtype)], operands=(g, r1), sem=("parallel",), name=name)
    return out


def final_add(g, r1, r2, jc_arr, *, dest_shape, lead, prev, name):
    _, rows, cdim = g.shape
    h = rows // 2

    def body(jc_ref, g_ref, r1_ref, r2_ref, *rest):
        o_ref = rest[-1]
        acc = g_ref[...].astype(F32) + r1_ref[...].astype(F32)
        for k in range(3):
            acc = acc + r2_ref[k].astype(F32)
        o_ref[...] = acc

    if lead is None:
        o_spec = pl.BlockSpec((h, cdim), lambda i, jc: (jc[1], 0))
    elif lead == "chip":
        o_spec = pl.BlockSpec((None, h, cdim), lambda i, jc: (jc[0], jc[1], 0))
    else:
        o_spec = pl.BlockSpec((None, h, cdim), lambda i, jc: (lead, jc[1], 0))
    in_specs = [pl.BlockSpec((None, h, cdim), lambda i, jc: (jc[0], jc[1], 0)),
                pl.BlockSpec((None, h, cdim), lambda i, jc: (jc[0], 0, 0)),
                pl.BlockSpec((3, h, cdim), lambda i, jc: (0, 0, 0))]
    operands = [g, r1, r2]
    aliases = None
    if prev is not None:
        in_specs.append(ANY)
        operands.append(prev)
        aliases = {3: 0}
    (out,), _ = _call(body, grid=(1,), prefetch=(jc_arr,), in_specs=in_specs, out_specs=[o_spec],
                      out_shape=[jax.ShapeDtypeStruct(dest_shape, F32)], operands=operands, aliases=aliases, name=name)
    return out


def _place():
    return lax.axis_index("x"), lax.axis_index("y"), lax.axis_index("c")


def _partner(x, y, k):
    return (1 - x if k >> 1 else x), (1 - y if k & 1 else y)


def _half(rows, sel, dtype):
    align = 16 if dtype == BF16 else 8
    return pl.ds(pl.multiple_of(sel * (rows // 2), align), rows // 2)


def _rider(inputs, aliased, fresh, nsem, copies, arrivals):
    def start(ins, outs, send, recv):
        for cp in copies(ins, outs, send, recv):
            cp.start()

    def finish(ins, outs, send, recv):
        for cp in arrivals(ins, outs, send, recv):
            cp.wait_recv()
        for cp in copies(ins, outs, send, recv):
            cp.wait_send()

    return types.SimpleNamespace(inputs=list(inputs), aliased=list(aliased), fresh=list(fresh), nsem=nsem, start=start,
                                 finish=finish)


def _remote(src, dst, send, recv, idx, dev):
    return pltpu.make_async_remote_copy(src_ref=src, dst_ref=dst, send_sem=send.at[idx], recv_sem=recv.at[idx],
                                        device_id=dev, device_id_type=MESH)


def gather_ici_rider(fulls):
    nt = len(fulls)

    def region(outs, t, slot, sel):
        return outs[t].at[slot, _half(fulls[t].shape[1], sel, fulls[t].dtype)]

    def copies(ins, outs, send, recv):
        x, y, c = _place()
        res = []
        for t in range(nt):
            for k in (1, 2, 3):
                px, py = _partner(x, y, k)
                mine = region(outs, t, 2 * x + y, c)
                res.append(_remote(mine, mine, send, recv, 3 * t + k - 1, (px, py, c)))
        return res

    def arrivals(ins, outs, send, recv):
        x, y, c = _place()
        res = []
        for t in range(nt):
            for k in (1, 2, 3):
                px, py = _partner(x, y, k)
                theirs = region(outs, t, 2 * px + py, c)
                res.append(_remote(theirs, theirs, send, recv, 3 * t + k - 1, (x, y, c)))
        return res

    return _rider(fulls, range(nt), [], 3 * nt, copies, arrivals)


def gather_d2d_rider(fulls):
    nt = len(fulls)

    def region(outs, t, slot, sel):
        return outs[t].at[slot, _half(fulls[t].shape[1], sel, fulls[t].dtype)]

    def both(outs, send, recv, mine):
        x, y, c = _place()
        res = []
        for t in range(nt):
            for k in (1, 2, 3):
                px, py = _partner(x, y, k)
                part = region(outs, t, 2 * px + py, c if mine else 1 - c)
                res.append(_remote(part, part, send, recv, 3 * t + k - 1, (x, y, 1 - c)))
        return res

    return _rider(fulls, range(nt), [], 3 * nt, lambda i, o, s, r: both(o, s, r, True), lambda i, o, s, r: both(o, s, r, False))


def exchange_rider(grads):
    nt = len(grads)

    def both(ins, outs, send, recv):
        x, y, c = _place()
        return [_remote(ins[t].at[:, _half(grads[t].shape[1], 1 - c, grads[t].dtype)], outs[t], send, recv, t, (x, y, 1 - c))
                for t in range(nt)]

    fresh = [jax.ShapeDtypeStruct((N_CHIPS, g.shape[1] // 2, g.shape[2]), g.dtype) for g in grads]
    return _rider(grads, [], fresh, nt, both, both)


def scatter_rider(parts):
    nt = len(parts)

    def both(ins, outs, send, recv):
        x, y, c = _place()
        res = []
        for t in range(nt):
            for k in (1, 2, 3):
                px, py = _partner(x, y, k)
                res.append(_remote(ins[t].at[2 * px + py], outs[t].at[k - 1], send, recv, 3 * t + k - 1, (px, py, c)))
        return res

    fresh = [jax.ShapeDtypeStruct((3,) + p.shape[1:], p.dtype) for p in parts]
    return _rider(parts, [], fresh, 3 * nt, both, both)


def broadcast_rider(bufs, items):
    def region(outs, item, sel):
        bi, lead = item
        ref = outs[bi]
        if lead == "chip":
            x, y, _ = _place()
            ref = ref.at[2 * x + y]
        elif lead is not None:
            ref = ref.at[lead]
        return ref.at[_half(ref.shape[0], sel, F32)]

    def both(outs, send, recv, mine):
        x, y, c = _place()
        res = []
        for i, item in enumerate(items):
            part = region(outs, item, c if mine else 1 - c)
            res.append(_remote(part, part, send, recv, i, (x, y, 1 - c)))
        return res

    return _rider(bufs, range(len(bufs)), [], len(items), lambda i, o, s, r: both(o, s, r, True),
                  lambda i, o, s, r: both(o, s, r, False))


def comm_call(riders, *, name):
    _, res = _call(None, riders=riders, name=name)
    return res


SLAB_ROWS = 192


def _pad_rows(a, rows=8):
    return jnp.pad(a, ((0, rows - a.shape[0]), (0, 0)))


def _pack_small(norm_grads, db_qkv, db_o, dsinks, db_sp, dln_g, dln_b, dw_sp):
    parts = [
        jnp.concatenate(norm_grads, axis=0),
        _pad_rows(jnp.pad(db_qkv, ((0, 0), (0, 2 * D_MODEL - QKV_WIDTH))).reshape(2, D_MODEL)),
        _pad_rows(db_o),
        _pad_rows(jnp.pad(dsinks.reshape(1, N_Q_HEADS), ((0, 0), (0, D_MODEL - N_Q_HEADS)))),
        _pad_rows(db_sp.reshape(1, D_MODEL)),
        _pad_rows(jnp.concatenate([dln_g, dln_b], axis=0)),
        dw_sp.reshape(SGU_CHUNK, D_MODEL),
    ]
    slab = jnp.concatenate(parts, axis=0)
    return jnp.pad(slab, ((0, SLAB_ROWS - slab.shape[0]), (0, 0))).reshape(N_CHIPS, SLAB_ROWS // N_CHIPS, D_MODEL)


def _unpack_small(slab, j):
    slab = slab.reshape(SLAB_ROWS, D_MODEL)
    norms = [slab[2 * i:2 * i + 2] for i in range(4)]
    db_qkv = slab[8:10].reshape(1, 2 * D_MODEL)[:, :QKV_WIDTH]
    db_o = slab[16:17]
    dsinks = slab[24:25, :N_Q_HEADS]
    db_sp = slab[32:33].reshape(SGU_GROUPS, SGU_CHUNK)
    width = D_MODEL // N_CHIPS
    dln_g = lax.dynamic_slice(slab[40:41], (0, j * width), (1, width))
    dln_b = lax.dynamic_slice(slab[41:42], (0, j * width), (1, width))
    dw_sp = slab[48:48 + SGU_CHUNK].reshape(SGU_GROUPS * SGU_CHUNK, SGU_CHUNK)
    return norms, db_qkv, db_o, dsinks, db_sp, dln_g, dln_b, dw_sp


def _ffn_backward(df, h, gu, a, w_gu, w_d, tag):
    dgu = ffn_dact(df, w_d, gu, name=f"ffn_dact_{tag}")
    dw_d = mm_tn(a, df, shard_major=False, tm=FF_HALF, tn=D_MODEL, name=f"dw_down_{tag}")
    dw_gu = mm_tn(h, dgu, shard_major=True, tm=D_MODEL, tn=FF_HALF, name=f"dw_gate_up_{tag}")
    dh = mm_nt(dgu, w_gu, out_dtype=F32, tm=1024, name=f"dh_ffn_{tag}")
    return dh, dw_gu, dw_d


def kernel(x, norm_mix_pre, norm_mix_post, norm_ffn_pre, norm_ffn_post, attn_w_qkv, attn_b_qkv, attn_sinks, attn_w_o, attn_b_o, sgu_w_in, sgu_ln_g, sgu_ln_b, sgu_w_spatial, sgu_b_spatial, sgu_w_out, ffn_w_gate_up, ffn_w_down, loss_target, m_norm_mix_pre, m_norm_mix_post, m_norm_ffn_pre, m_norm_ffn_post, m_attn_w_qkv, m_attn_b_qkv, m_attn_sinks, m_attn_w_o, m_attn_b_o, m_sgu_w_in, m_sgu_ln_g, m_sgu_ln_b, m_sgu_w_spatial, m_sgu_b_spatial, m_sgu_w_out, m_ffn_w_gate_up, m_ffn_w_down, v_norm_mix_pre, v_norm_mix_post, v_norm_ffn_pre, v_norm_ffn_post, v_attn_w_qkv, v_attn_b_qkv, v_attn_sinks, v_attn_w_o, v_attn_b_o, v_sgu_w_in, v_sgu_ln_g, v_sgu_ln_b, v_sgu_w_spatial, v_sgu_b_spatial, v_sgu_w_out, v_ffn_w_gate_up, v_ffn_w_down):
    s = x.shape[1]
    x0 = x.reshape(s, D_MODEL)
    target = loss_target.reshape(s, D_MODEL)
    mx, my, mc = lax.axis_index("x"), lax.axis_index("y"), lax.axis_index("c")
    chip = 2 * mx + my
    chip_arr = jnp.reshape(chip, (1,)).astype(I32)
    c_arr = jnp.reshape(mc, (1,)).astype(I32)
    jc_arr = jnp.stack([chip, mc]).astype(I32)
    zero_bias = jnp.zeros((1, D_MODEL), F32)

    def gain(p, i):
        return p[i:i + 1]

    big = [attn_w_qkv, attn_w_o, sgu_w_in, sgu_w_out, ffn_w_gate_up, ffn_w_gate_up, ffn_w_down, ffn_w_down]
    layers = [0, 0, 0, 0, 0, 1, 0, 1]
    tags = ["qkv", "wo", "win", "wout", "wgu0", "wgu1", "wd0", "wd1"]
    fulls = [place_shard(w, l, chip_arr, BF16, name=f"place_{t}") for w, l, t in zip(big, layers, tags)]
    ln_pack = _pad_rows(jnp.concatenate([sgu_ln_g, sgu_ln_b], axis=0), 16)[None]
    fulls.append(place_shard(ln_pack, 0, chip_arr, F32, name="place_ln"))
    (fulls,) = comm_call([gather_ici_rider(fulls)], name="gather_weights_ici")
    (fulls,) = comm_call([gather_d2d_rider(fulls)], name="gather_weights_d2d")
    w_qkv, w_o, w_in, w_out, w_gu0, w_gu1, w_d0, w_d1, ln_full = fulls
    ln_g = ln_full[:, 0, :].reshape(1, D_MODEL)
    ln_b = ln_full[:, 1, :].reshape(1, D_MODEL)
    w_o = w_o.reshape(Q_WIDTH, D_MODEL)
    w_out = w_out.reshape(D_MODEL, D_MODEL)
    w_d0 = w_d0.reshape(D_FF, D_MODEL)
    w_d1 = w_d1.reshape(D_FF, D_MODEL)

    cos, sin = _rope_tables(s)
    sink_rows = jnp.broadcast_to(
        jnp.repeat(attn_sinks.reshape(N_KV_HEADS, GQA_GROUP), WINDOW, axis=1)[:, :, None], (N_KV_HEADS, ROWS, LANES))
    w_sp = sgu_w_spatial.reshape(SGU_GROUPS, SGU_CHUNK, SGU_CHUNK)
    b_sp = jnp.broadcast_to(sgu_b_spatial.reshape(SGU_GROUPS, SGU_CHUNK)[:, :, None], (SGU_GROUPS, SGU_CHUNK, LANES))

    h0 = prenorm(x0, gain(norm_mix_pre, 0), name="prenorm_0")
    qkv = mm_nn(h0, w_qkv, out_dtype=F32, name="qkv_proj")
    q, k, v = rope_fwd(qkv, attn_b_qkv, cos, sin, name="rope_fwd")
    q = q.reshape(N_KV_HEADS, GQA_GROUP, s, HEAD_DIM)
    o_heads = attn_fwd(q, k, v, sink_rows, name="attn_fwd")
    o = heads_to_rows(o_heads.reshape(N_Q_HEADS, s, HEAD_DIM), name="heads_to_rows")
    m0 = mm_nn(o, w_o, out_dtype=F32, name="attn_out_proj")
    x1, h1 = residual_norm(x0, m0, attn_b_o, gain(norm_mix_post, 0), gain(norm_ffn_pre, 0), name="residual_norm_0a")
    gu0, a0 = ffn_up(h1, w_gu0, name="ffn_up_0")
    f0 = mm_nn(a0, w_d0, out_dtype=F32, name="ffn_down_0")
    x2, h2 = residual_norm(x1, f0, zero_bias, gain(norm_ffn_post, 0), gain(norm_mix_pre, 1), name="residual_norm_0b")
    z = mm_nn(h2, w_in, out_dtype=F32, name="sgu_in_proj")
    y = sgu_fwd(z, ln_g, ln_b, w_sp, b_sp, name="sgu_fwd")
    m1 = mm_nn(y, w_out, out_dtype=F32, name="sgu_out_proj")
    x3, h3 = residual_norm(x2, m1, zero_bias, gain(norm_mix_post, 1), gain(norm_ffn_pre, 1), name="residual_norm_1a")
    gu1, a1 = ffn_up(h3, w_gu1, name="ffn_up_1")
    f1 = mm_nn(a1, w_d1, out_dtype=F32, name="ffn_down_1")
    dx4, df1, dg_fpost1, loss_part = loss_head(x3, f1, gain(norm_ffn_post, 1), target, name="loss_head")
    loss = lax.psum(loss_part[0, 0], ("x", "y", "c"))

    dh3, dw_gu1, dw_d1 = _ffn_backward(df1, h3, gu1, a1, w_gu1, w_d1, "1")
    dx3, dm1, dg_fpre1, dg_mpost1, _ = norm_bwd_pair(
        dx4, dh3, x3, gain(norm_ffn_pre, 1), m1, zero_bias, gain(norm_mix_post, 1), name="norm_bwd_1a")
    dy = mm_nt(dm1, w_out, out_dtype=F32, name="dy_sgu")
    dw_out = mm_tn(y, dm1, shard_major=False, tm=D_MODEL, tn=D_MODEL, name="dw_sgu_out")
    dz, dw_sp, db_sp, dln_g, dln_b = sgu_bwd(z, dy, ln_g, ln_b, w_sp, b_sp, name="sgu_bwd")
    dw_in = mm_tn(h2, dz, shard_major=True, tm=D_MODEL, tn=2 * D_MODEL // N_CHIPS, name="dw_sgu_in")
    dh2 = mm_nt(dz, w_in, out_dtype=F32, tm=1024, name="dh_sgu")
    dx2, df0, dg_mpre1, dg_fpost0, _ = norm_bwd_pair(
        dx3, dh2, x2, gain(norm_mix_pre, 1), f0, zero_bias, gain(norm_ffn_post, 0), name="norm_bwd_0b")
    dh1, dw_gu0, dw_d0 = _ffn_backward(df0, h1, gu0, a0, w_gu0, w_d0, "0")
    dx1, dm0, dg_fpre0, dg_mpost0, db_o = norm_bwd_pair(
        dx2, dh1, x1, gain(norm_ffn_pre, 0), m0, attn_b_o, gain(norm_mix_post, 0), name="norm_bwd_0a")
    do = mm_nt(dm0, w_o, out_dtype=BF16, name="do_attn")
    dw_o = mm_tn(o, dm0, shard_major=False, tm=D_MODEL, tn=D_MODEL, name="dw_attn_out")
    do_heads = rows_to_heads(do, name="rows_to_heads").reshape(N_KV_HEADS, GQA_GROUP, s, HEAD_DIM)
    dq, dkc, dkp, dvc, dvp, dsink = attn_bwd(q, k, v, sink_rows, do_heads, name="attn_bwd")
    dqkv, db_qkv = rope_bwd(dq.reshape(N_Q_HEADS, s, HEAD_DIM), dkc, dkp, dvc, dvp, cos, sin, name="rope_bwd")
    dw_qkv = mm_tn(h0, dqkv, shard_major=True, tm=D_MODEL, tn=QKV_WIDTH // N_CHIPS, name="dw_qkv")
    dh0 = mm_nt(dqkv, w_qkv, out_dtype=F32, tm=1024, name="dh_attn")
    grad_x, dg_mpre0 = norm_bwd_last(dx1, dh0, x0, gain(norm_mix_pre, 0), name="norm_bwd_in")

    norm_grads = [jnp.concatenate(p, axis=0) for p in
                  ((dg_mpre0, dg_mpre1), (dg_mpost0, dg_mpost1), (dg_fpre0, dg_fpre1), (dg_fpost0, dg_fpost1))]
    slab = _pack_small(norm_grads, db_qkv, db_o, dsink[:, :, 0, 0], db_sp[:, :, 0], dln_g, dln_b, dw_sp)
    grads = [dw_qkv, dw_o.reshape(N_CHIPS, Q_WIDTH // N_CHIPS, D_MODEL), dw_in,
             dw_out.reshape(N_CHIPS, D_MODEL // N_CHIPS, D_MODEL), dw_gu0, dw_gu1,
             dw_d0.reshape(N_CHIPS, D_FF // N_CHIPS, D_MODEL), dw_d1.reshape(N_CHIPS, D_FF // N_CHIPS, D_MODEL), slab]
    gtags = tags + ["small"]
    (from_sibling,) = comm_call([exchange_rider(grads)], name="pair_exchange")
    pair_sums = [pair_add(g, r, c_arr, name=f"pair_add_{t}") for g, r, t in zip(grads, from_sibling, gtags)]
    (from_chips,) = comm_call([scatter_rider(pair_sums)], name="chip_scatter")
    dest_shapes = [attn_w_qkv.shape[1:], attn_w_o.shape[1:], sgu_w_in.shape[1:], sgu_w_out.shape[1:],
                   ffn_w_gate_up.shape, ffn_w_down.shape, (N_CHIPS, SLAB_ROWS // N_CHIPS, D_MODEL)]
    dests = [(0, None), (1, None), (2, None), (3, None), (4, 0), (4, 1), (5, 0), (5, 1), (6, "chip")]
    bufs = [None] * len(dest_shapes)
    for t, (bi, lead) in enumerate(dests):
        bufs[bi] = final_add(grads[t], from_sibling[t], from_chips[t], jc_arr, dest_shape=dest_shapes[bi], lead=lead,
                             prev=bufs[bi], name=f"final_add_{gtags[t]}")
    (bufs,) = comm_call([broadcast_rider(bufs, dests)], name="pair_broadcast")
    g_qkv, g_wo, g_win, g_wout, g_wgu, g_wd, slab_full = bufs
    (slab_full,) = comm_call([gather_ici_rider([slab_full])], name="gather_small_ici")[0]
    (slab_full,) = comm_call([gather_d2d_rider([slab_full])], name="gather_small_d2d")[0]
    g_norms, g_bqkv, g_bo, g_sinks, g_bsp, g_lng, g_lnb, g_wsp = _unpack_small(slab_full, chip)

    def big_update(w, g, m, v, tag):
        return adamw(w, g.reshape(w.shape), m, v, name=f"adamw_{tag}")

    upd = {
        "attn_w_qkv": big_update(attn_w_qkv, g_qkv, m_attn_w_qkv, v_attn_w_qkv, "qkv"),
        "attn_w_o": big_update(attn_w_o, g_wo, m_attn_w_o, v_attn_w_o, "wo"),
        "sgu_w_in": big_update(sgu_w_in, g_win, m_sgu_w_in, v_sgu_w_in, "win"),
        "sgu_w_out": big_update(sgu_w_out, g_wout, m_sgu_w_out, v_sgu_w_out, "wout"),
        "ffn_w_gate_up": big_update(ffn_w_gate_up, g_wgu, m_ffn_w_gate_up, v_ffn_w_gate_up, "wgu"),
        "ffn_w_down": big_update(ffn_w_down, g_wd, m_ffn_w_down, v_ffn_w_down, "wd"),
    }
    small_names = ["norm_mix_pre", "norm_mix_post", "norm_ffn_pre", "norm_ffn_post", "attn_b_qkv", "attn_sinks", "attn_b_o",
                   "sgu_ln_g", "sgu_ln_b", "sgu_w_spatial", "sgu_b_spatial"]
    small_w = [norm_mix_pre, norm_mix_post, norm_ffn_pre, norm_ffn_post, attn_b_qkv, attn_sinks, attn_b_o, sgu_ln_g, sgu_ln_b,
               sgu_w_spatial, sgu_b_spatial]
    small_m = [m_norm_mix_pre, m_norm_mix_post, m_norm_ffn_pre, m_norm_ffn_post, m_attn_b_qkv, m_attn_sinks, m_attn_b_o,
               m_sgu_ln_g, m_sgu_ln_b, m_sgu_w_spatial, m_sgu_b_spatial]
    small_v = [v_norm_mix_pre, v_norm_mix_post, v_norm_ffn_pre, v_norm_ffn_post, v_attn_b_qkv, v_attn_sinks, v_attn_b_o,
               v_sgu_ln_g, v_sgu_ln_b, v_sgu_w_spatial, v_sgu_b_spatial]
    small_g = g_norms + [g_bqkv, g_sinks, g_bo, g_lng, g_lnb, g_wsp, g_bsp]

    def flat2(a):
        return a.reshape(-1, a.shape[-1])

    res = adamw_small([flat2(a) for a in small_w], [flat2(a) for a in small_g], [flat2(a) for a in small_m],
                      [flat2(a) for a in small_v], name="adamw_small")
    for i, nm in enumerate(small_names):
        upd[nm] = tuple(r[i].reshape(small_w[i].shape) for r in res)

    order = ["norm_mix_pre", "norm_mix_post", "norm_ffn_pre", "norm_ffn_post", "attn_w_qkv", "attn_b_qkv", "attn_sinks",
             "attn_w_o", "attn_b_o", "sgu_w_in", "sgu_ln_g", "sgu_ln_b", "sgu_w_spatial", "sgu_b_spatial", "sgu_w_out",
             "ffn_w_gate_up", "ffn_w_down"]
    outs = [loss, grad_x.reshape(1, s, D_MODEL)]
    for part in range(4):
        outs += [upd[nm][part] for nm in order]
    return tuple(outs)
```

```python
import types

import jax
import jax.numpy as jnp
from jax import lax
from jax.experimental import pallas as pl
from jax.experimental.pallas import tpu as pltpu

F32 = jnp.float32
BF16 = jnp.bfloat16
I32 = jnp.int32

D_MODEL = 1024
HEAD_DIM = 64
N_Q_HEADS = 16
N_KV_HEADS = 4
GQA_GROUP = 4
WINDOW = 128
Q_WIDTH = 1024
KV_WIDTH = 256
QKV_WIDTH = 1536
ROPE_THETA = 10000.0
SGU_GROUPS = 8
SGU_CHUNK = 128
D_FF = 2816
FF_HALF = D_FF // 2
EPS = 1e-6
N_CHIPS = 4
LANES = 128

ADAM_LR = 0.001
ADAM_B1 = 0.9
ADAM_B2 = 0.999
ADAM_EPS = 1e-08
ADAM_WD = 0.01
ADAM_STEP = 10

VMEM_LIMIT = 52 * 1024 * 1024
MESH = pl.DeviceIdType.MESH
NEG = -1e30
NT_DIMS = (((1,), (1,)), ((), ()))
TN_DIMS = (((0,), (0,)), ((), ()))
NN_DIMS = (((1,), (0,)), ((), ()))
ANY = pl.BlockSpec(memory_space=pl.ANY)


def _row_tile(s, want):
    return want if s % want == 0 else s


def _call(body, *, name, grid=(), in_specs=(), out_specs=(), out_shape=(), scratch_shapes=(), operands=(), prefetch=(),
          aliases=None, riders=(), sem=None):
    n_pre, n_in, n_out, n_scr = len(prefetch), len(operands), len(out_shape), len(scratch_shapes)
    in_specs, out_specs, out_shape = list(in_specs), list(out_specs), list(out_shape)
    operands, scratch_shapes = list(operands), list(scratch_shapes)
    io_alias = {n_pre + i: o for i, o in (aliases or {}).items()}
    for r in riders:
        base_in, base_out = n_pre + len(operands), len(out_shape)
        operands += list(r.inputs)
        in_specs += [ANY] * len(r.inputs)
        for pos, i in enumerate(r.aliased):
            io_alias[base_in + i] = base_out + pos
            out_shape.append(jax.ShapeDtypeStruct(r.inputs[i].shape, r.inputs[i].dtype))
        out_shape += list(r.fresh)
        out_specs += [ANY] * (len(r.aliased) + len(r.fresh))
        scratch_shapes += [pltpu.SemaphoreType.DMA((r.nsem,)), pltpu.SemaphoreType.DMA((r.nsem,))]

    def wrapped(*refs):
        pre, p = refs[:n_pre], n_pre
        core_in, p = refs[p:p + n_in], p + n_in
        r_in = []
        for r in riders:
            r_in.append(refs[p:p + len(r.inputs)])
            p += len(r.inputs)
        core_out, p = refs[p:p + n_out], p + n_out
        r_out = []
        for r in riders:
            k = len(r.aliased) + len(r.fresh)
            r_out.append(refs[p:p + k])
            p += k
        core_scr, p = refs[p:p + n_scr], p + n_scr
        r_sem = [refs[p + 2 * i:p + 2 * i + 2] for i in range(len(riders))]

        def edge(at_last, fns):
            def run():
                for i, r in enumerate(riders):
                    getattr(r, fns)(r_in[i], r_out[i], r_sem[i][0], r_sem[i][1])
            if not riders:
                return
            if not grid:
                run()
                return
            cond = None
            for d, n in enumerate(grid):
                c = pl.program_id(d) == (n - 1 if at_last else 0)
                cond = c if cond is None else jnp.logical_and(cond, c)
            pl.when(cond)(run)

        edge(False, "start")
        if body is not None:
            body(*pre, *core_in, *core_out, *core_scr)
        edge(True, "finish")

    if sem is None or riders:
        sem = ("arbitrary",) * len(grid)
    kwargs = dict(out_shape=out_shape, input_output_aliases=io_alias, name=name)
    if grid:
        kwargs["compiler_params"] = pltpu.CompilerParams(dimension_semantics=sem, vmem_limit_bytes=VMEM_LIMIT)
    if n_pre:
        kwargs["grid_spec"] = pltpu.PrefetchScalarGridSpec(
            num_scalar_prefetch=n_pre, grid=grid, in_specs=in_specs, out_specs=out_specs, scratch_shapes=scratch_shapes)
    else:
        kwargs.update(grid=grid, in_specs=in_specs, out_specs=out_specs, scratch_shapes=scratch_shapes)
    res = pl.pallas_call(wrapped, **kwargs)(*prefetch, *operands)
    core, rest, rider_res = list(res[:n_out]), list(res[n_out:]), []
    for r in riders:
        k = len(r.aliased) + len(r.fresh)
        rider_res.append(rest[:k])
        rest = rest[k:]
    return core, rider_res


def _mm_call(*, grid, in_specs, out_spec, out_shape, dims, nk, kaxis, acc_shape, name, operands, riders=()):
    out_dtype = out_shape.dtype

    def body(a_ref, b_ref, o_ref, *scratch):
        p = lax.dot_general(a_ref[...].astype(BF16), b_ref[...].astype(BF16), dims, preferred_element_type=F32)
        if nk == 1:
            o_ref[...] = p.astype(out_dtype)
        else:
            acc = scratch[0]
            kk = pl.program_id(kaxis)

            @pl.when(kk == 0)
            def _():
                acc[...] = p

            @pl.when(kk > 0)
            def _():
                acc[...] += p

            @pl.when(kk == nk - 1)
            def _():
                o_ref[...] = acc[...].astype(out_dtype)

    sem = ["parallel"] * len(grid)
    if nk > 1:
        sem[kaxis] = "arbitrary"
    (out,), rider_res = _call(
        body, grid=grid, in_specs=in_specs, out_specs=[out_spec], out_shape=[out_shape],
        scratch_shapes=[pltpu.VMEM(acc_shape, F32)] if nk > 1 else [], operands=operands, name=name, riders=riders,
        sem=tuple(sem))
    return (out, rider_res) if riders else out


def mm_nn(a, w, *, out_dtype, name, tm=512, tn=512, riders=()):
    m, k = a.shape
    tm = _row_tile(m, tm)
    if w.ndim == 3:
        ns = w.shape[2]
        grid = (N_CHIPS, m // tm)
        w_spec = pl.BlockSpec((None, k, ns), lambda j, i: (j, 0, 0))
        o_spec = pl.BlockSpec((tm, ns), lambda j, i: (i, j))
        n = N_CHIPS * ns
    else:
        n = w.shape[1]
        grid = (n // tn, m // tm)
        w_spec = pl.BlockSpec((k, tn), lambda j, i: (0, j))
        o_spec = pl.BlockSpec((tm, tn), lambda j, i: (i, j))
    return _mm_call(grid=grid, in_specs=[pl.BlockSpec((tm, k), lambda j, i: (i, 0)), w_spec], out_spec=o_spec,
                    out_shape=jax.ShapeDtypeStruct((m, n), out_dtype), dims=NN_DIMS, nk=1, kaxis=0, acc_shape=None,
                    name=name, operands=(a, w), riders=riders)


def mm_nt(a, w, *, out_dtype, name, tm=512, tn=512, riders=()):
    if w.ndim == 2:
        m, n = a.shape
        kout = w.shape[0]
        tm = _row_tile(m, tm)
        return _mm_call(grid=(kout // tn, m // tm),
                        in_specs=[pl.BlockSpec((tm, n), lambda j, i: (i, 0)), pl.BlockSpec((tn, n), lambda j, i: (j, 0))],
                        out_spec=pl.BlockSpec((tm, tn), lambda j, i: (i, j)),
                        out_shape=jax.ShapeDtypeStruct((m, kout), out_dtype), dims=NT_DIMS, nk=1, kaxis=0,
                        acc_shape=None, name=name, operands=(a, w), riders=riders)
    _, kout, ns = w.shape
    if a.ndim == 3:
        m = a.shape[1]
        tm = _row_tile(m, tm)
        a_spec = pl.BlockSpec((None, tm, ns), lambda i, kk: (kk // 2, i, kk % 2))
    else:
        m = a.shape[0]
        tm = _row_tile(m, tm)
        a_spec = pl.BlockSpec((tm, ns), lambda i, kk: (i, kk))
    return _mm_call(grid=(m // tm, N_CHIPS),
                    in_specs=[a_spec, pl.BlockSpec((None, kout, ns), lambda i, kk: (kk, 0, 0))],
                    out_spec=pl.BlockSpec((tm, kout), lambda i, kk: (i, 0)),
                    out_shape=jax.ShapeDtypeStruct((m, kout), out_dtype), dims=NT_DIMS, nk=N_CHIPS, kaxis=1,
                    acc_shape=(tm, kout), name=name, operands=(a, w), riders=riders)


def mm_tn(a, b, *, shard_major, name, tm, tn, tk=512, out_dtype=BF16, riders=()):
    s, m = a.shape
    tk = _row_tile(s, tk)
    if b.ndim == 3:
        n = 2 * b.shape[2]
        b_spec = pl.BlockSpec((None, tk, tn), lambda i, j, kk: (j // 2, kk, j % 2))
    else:
        n = b.shape[1]
        b_spec = pl.BlockSpec((tk, tn), lambda i, j, kk: (kk, j))
    if shard_major:
        assert tn == n // N_CHIPS
        o_spec = pl.BlockSpec((None, tm, tn), lambda i, j, kk: (j, i, 0))
        o_shape = jax.ShapeDtypeStruct((N_CHIPS, m, tn), out_dtype)
    else:
        o_spec = pl.BlockSpec((tm, tn), lambda i, j, kk: (i, j))
        o_shape = jax.ShapeDtypeStruct((m, n), out_dtype)
    return _mm_call(grid=(m // tm, n // tn, s // tk),
                    in_specs=[pl.BlockSpec((tk, tm), lambda i, j, kk: (kk, i)), b_spec], out_spec=o_spec,
                    out_shape=o_shape, dims=TN_DIMS, nk=s // tk, kaxis=2, acc_shape=(tm, tn), name=name, operands=(a, b),
                    riders=riders)


def _rstd(x):
    return lax.rsqrt(jnp.mean(x * x, axis=-1, keepdims=True) + EPS)


def _rms_bwd(dy, x, g):
    r = _rstd(x)
    xhat = x * r
    gy = dy * g
    dx = r * (gy - xhat * jnp.mean(gy * xhat, axis=-1, keepdims=True))
    return dx, jnp.sum(dy * xhat, axis=0, keepdims=True)


def _accum(ref, val, first):
    @pl.when(first)
    def _():
        ref[...] = val

    @pl.when(jnp.logical_not(first))
    def _():
        ref[...] += val


def _row_spec(tm, width):
    return pl.BlockSpec((tm, width), lambda i: (i, 0))


def _vec_spec(width):
    return pl.BlockSpec((1, width), lambda i: (0, 0))


def _ret(core, rider_res, riders):
    core = core[0] if len(core) == 1 else core
    return (core, rider_res) if riders else core


def prenorm(x, g, *, name, tm=256, riders=()):
    s = x.shape[0]
    tm = _row_tile(s, tm)

    def body(x_ref, g_ref, h_ref):
        xv = x_ref[...]
        h_ref[...] = (xv * _rstd(xv) * g_ref[...]).astype(BF16)

    core, rr = _call(
        body, grid=(s // tm,), in_specs=[_row_spec(tm, D_MODEL), _vec_spec(D_MODEL)], out_specs=[_row_spec(tm, D_MODEL)],
        out_shape=[jax.ShapeDtypeStruct((s, D_MODEL), BF16)], operands=(x, g), sem=("parallel",), name=name, riders=riders)
    return _ret(core, rr, riders)


def residual_norm(x, m, bias, g_post, g_next, *, name, tm=256, riders=()):
    s = x.shape[0]
    tm = _row_tile(s, tm)

    def body(x_ref, m_ref, b_ref, gp_ref, gn_ref, xo_ref, h_ref):
        mv = m_ref[...] + b_ref[...]
        xn = x_ref[...] + mv * _rstd(mv) * gp_ref[...]
        xo_ref[...] = xn
        h_ref[...] = (xn * _rstd(xn) * gn_ref[...]).astype(BF16)

    core, rr = _call(
        body, grid=(s // tm,),
        in_specs=[_row_spec(tm, D_MODEL), _row_spec(tm, D_MODEL), _vec_spec(D_MODEL), _vec_spec(D_MODEL), _vec_spec(D_MODEL)],
        out_specs=[_row_spec(tm, D_MODEL), _row_spec(tm, D_MODEL)],
        out_shape=[jax.ShapeDtypeStruct((s, D_MODEL), F32), jax.ShapeDtypeStruct((s, D_MODEL), BF16)],
        operands=(x, m, bias, g_post, g_next), sem=("parallel",), name=name, riders=riders)
    return _ret(core, rr, riders)


def loss_head(x, f, g_post, target, *, name, tm=256, riders=()):
    s = x.shape[0]
    tm = _row_tile(s, tm)

    def body(x_ref, f_ref, g_ref, t_ref, dx_ref, df_ref, dg_ref, loss_ref):
        first = pl.program_id(0) == 0
        fv = f_ref[...]
        g = g_ref[...]
        err = x_ref[...] + fv * _rstd(fv) * g - t_ref[...]
        dx = err * (1.0 / D_MODEL)
        dx_ref[...] = dx
        df, dg = _rms_bwd(dx, fv, g)
        df_ref[...] = df.astype(BF16)
        _accum(dg_ref, dg, first)
        part = jnp.sum(jnp.sum(err * err, axis=-1, keepdims=True), axis=0, keepdims=True) * (0.5 / D_MODEL)
        _accum(loss_ref, jnp.broadcast_to(part, (8, LANES)), first)

    core, rr = _call(
        body, grid=(s // tm,),
        in_specs=[_row_spec(tm, D_MODEL), _row_spec(tm, D_MODEL), _vec_spec(D_MODEL), _row_spec(tm, D_MODEL)],
        out_specs=[_row_spec(tm, D_MODEL), _row_spec(tm, D_MODEL), _vec_spec(D_MODEL), pl.BlockSpec((8, LANES), lambda i: (0, 0))],
        out_shape=[jax.ShapeDtypeStruct((s, D_MODEL), F32), jax.ShapeDtypeStruct((s, D_MODEL), BF16),
                   jax.ShapeDtypeStruct((1, D_MODEL), F32), jax.ShapeDtypeStruct((8, LANES), F32)],
        operands=(x, f, g_post, target), name=name, riders=riders)
    return _ret(core, rr, riders)


def norm_bwd_pair(dres, dh, x, g_pre, m, bias, g_post, *, name, tm=256, riders=()):
    s = x.shape[0]
    tm = _row_tile(s, tm)

    def body(dres_ref, dh_ref, x_ref, gpre_ref, m_ref, b_ref, gpost_ref, dx_ref, dm_ref, dgpre_ref, dgpost_ref, db_ref):
        first = pl.program_id(0) == 0
        d1, dgpre = _rms_bwd(dh_ref[...], x_ref[...], gpre_ref[...])
        dx = dres_ref[...] + d1
        dx_ref[...] = dx
        dm, dgpost = _rms_bwd(dx, m_ref[...] + b_ref[...], gpost_ref[...])
        dm_ref[...] = dm.astype(BF16)
        _accum(dgpre_ref, dgpre, first)
        _accum(dgpost_ref, dgpost, first)
        _accum(db_ref, jnp.sum(dm, axis=0, keepdims=True), first)

    row, vec = _row_spec(tm, D_MODEL), _vec_spec(D_MODEL)
    vshape = jax.ShapeDtypeStruct((1, D_MODEL), F32)
    core, rr = _call(
        body, grid=(s // tm,), in_specs=[row, row, row, vec, row, vec, vec], out_specs=[row, row, vec, vec, vec],
        out_shape=[jax.ShapeDtypeStruct((s, D_MODEL), F32), jax.ShapeDtypeStruct((s, D_MODEL), BF16), vshape, vshape, vshape],
        operands=(dres, dh, x, g_pre, m, bias, g_post), name=name, riders=riders)
    return _ret(core, rr, riders)


def norm_bwd_last(dres, dh, x, g_pre, *, name, tm=256, riders=()):
    s = x.shape[0]
    tm = _row_tile(s, tm)

    def body(dres_ref, dh_ref, x_ref, g_ref, dx_ref, dg_ref):
        d1, dg = _rms_bwd(dh_ref[...], x_ref[...], g_ref[...])
        dx_ref[...] = dres_ref[...] + d1
        _accum(dg_ref, dg, pl.program_id(0) == 0)

    row, vec = _row_spec(tm, D_MODEL), _vec_spec(D_MODEL)
    core, rr = _call(
        body, grid=(s // tm,), in_specs=[row, row, row, vec], out_specs=[row, vec],
        out_shape=[jax.ShapeDtypeStruct((s, D_MODEL), F32), jax.ShapeDtypeStruct((1, D_MODEL), F32)],
        operands=(dres, dh, x, g_pre), name=name, riders=riders)
    return _ret(core, rr, riders)


def _rope_tables(s):
    half = HEAD_DIM // 2
    inv_freq = ROPE_THETA ** (-(jnp.arange(half, dtype=F32) * 2.0) / HEAD_DIM)
    ang = jnp.arange(s, dtype=I32).astype(F32)[:, None] * inv_freq[None, :]
    cos, sin = jnp.cos(ang), jnp.sin(ang)
    return jnp.tile(cos, (1, 4)), jnp.concatenate([-sin, sin, -sin, sin], axis=1)


def _swap_halves(x):
    lane = lax.broadcasted_iota(I32, x.shape, 1)
    return jnp.where((lane & (HEAD_DIM - 1)) < HEAD_DIM // 2, pltpu.roll(x, LANES - 32, 1), pltpu.roll(x, 32, 1))


N_ROPE_BLOCKS = (Q_WIDTH + KV_WIDTH) // LANES


def rope_fwd(qkv, bias, cos, sin, *, name, tm=256, riders=()):
    s = qkv.shape[0]
    tm = _row_tile(s, tm)

    def body(x_ref, b_ref, c_ref, s_ref, q_ref, k_ref, v_ref):
        cosv, sinv = c_ref[...], s_ref[...]
        for blk in range(QKV_WIDTH // LANES):
            cols = slice(blk * LANES, (blk + 1) * LANES)
            xb = x_ref[:, cols] + b_ref[:, cols]
            if blk < N_ROPE_BLOCKS:
                xb = xb * cosv + _swap_halves(xb) * sinv
            for e in range(2):
                head = 2 * blk + e
                piece = xb[:, e * HEAD_DIM:(e + 1) * HEAD_DIM].astype(BF16)
                if head < N_Q_HEADS:
                    q_ref[head] = piece
                elif head < N_Q_HEADS + N_KV_HEADS:
                    k_ref[head - N_Q_HEADS] = piece
                else:
                    v_ref[head - N_Q_HEADS - N_KV_HEADS] = piece

    def hm(nh):
        return pl.BlockSpec((nh, tm, HEAD_DIM), lambda i: (0, i, 0))

    core, rr = _call(
        body, grid=(s // tm,),
        in_specs=[_row_spec(tm, QKV_WIDTH), _vec_spec(QKV_WIDTH), _row_spec(tm, LANES), _row_spec(tm, LANES)],
        out_specs=[hm(N_Q_HEADS), hm(N_KV_HEADS), hm(N_KV_HEADS)],
        out_shape=[jax.ShapeDtypeStruct((N_Q_HEADS, s, HEAD_DIM), BF16), jax.ShapeDtypeStruct((N_KV_HEADS, s, HEAD_DIM), BF16),
                   jax.ShapeDtypeStruct((N_KV_HEADS, s, HEAD_DIM), BF16)],
        operands=(qkv, bias, cos, sin), sem=("parallel",), name=name, riders=riders)
    return _ret(core, rr, riders)


def rope_bwd(dq, dkc, dkp, dvc, dvp, cos, sin, *, name, riders=()):
    s = dq.shape[1]
    tm = WINDOW
    nb = s // tm

    def body(dq_ref, dkc_ref, dkp_ref, dvc_ref, dvp_ref, c_ref, s_ref, o_ref, db_ref, buf):
        i = pl.program_id(0)
        has_next = (i < nb - 1).astype(F32)
        for head in range(N_Q_HEADS):
            buf[:, head * HEAD_DIM:(head + 1) * HEAD_DIM] = dq_ref[head].astype(F32)
        for head in range(N_KV_HEADS):
            kcols = slice(Q_WIDTH + head * HEAD_DIM, Q_WIDTH + (head + 1) * HEAD_DIM)
            vcols = slice(Q_WIDTH + KV_WIDTH + head * HEAD_DIM, Q_WIDTH + KV_WIDTH + (head + 1) * HEAD_DIM)
            buf[:, kcols] = dkc_ref[head].astype(F32) + has_next * dkp_ref[head].astype(F32)
            buf[:, vcols] = dvc_ref[head].astype(F32) + has_next * dvp_ref[head].astype(F32)
        cosv, sinv = c_ref[...], s_ref[...]
        for blk in range(QKV_WIDTH // LANES):
            cols = slice(blk * LANES, (blk + 1) * LANES)
            g = buf[:, cols]
            if blk < N_ROPE_BLOCKS:
                g = g * cosv + _swap_halves(g * sinv)
            o_ref[:, cols] = g.astype(BF16)
            buf[:, cols] = g
        _accum(db_ref, jnp.sum(buf[...], axis=0, keepdims=True), i == 0)

    def hm(nh, shift):
        if shift:
            return pl.BlockSpec((nh, tm, HEAD_DIM), lambda i: (0, jnp.minimum(i + 1, nb - 1), 0))
        return pl.BlockSpec((nh, tm, HEAD_DIM), lambda i: (0, i, 0))

    core, rr = _call(
        body, grid=(nb,),
        in_specs=[hm(N_Q_HEADS, False), hm(N_KV_HEADS, False), hm(N_KV_HEADS, True), hm(N_KV_HEADS, False), hm(N_KV_HEADS, True),
                  _row_spec(tm, LANES), _row_spec(tm, LANES)],
        out_specs=[_row_spec(tm, QKV_WIDTH), _vec_spec(QKV_WIDTH)],
        out_shape=[jax.ShapeDtypeStruct((s, QKV_WIDTH), BF16), jax.ShapeDtypeStruct((1, QKV_WIDTH), F32)],
        scratch_shapes=[pltpu.VMEM((tm, QKV_WIDTH), F32)], operands=(dq, dkc, dkp, dvc, dvp, cos, sin), name=name, riders=riders)
    return _ret(core, rr, riders)


def heads_to_rows(o, *, name, tm=256, riders=()):
    s = o.shape[1]
    tm = _row_tile(s, tm)

    def body(o_ref, r_ref):
        for head in range(N_Q_HEADS):
            r_ref[:, head * HEAD_DIM:(head + 1) * HEAD_DIM] = o_ref[head]

    core, rr = _call(
        body, grid=(s // tm,), in_specs=[pl.BlockSpec((N_Q_HEADS, tm, HEAD_DIM), lambda i: (0, i, 0))],
        out_specs=[_row_spec(tm, Q_WIDTH)], out_shape=[jax.ShapeDtypeStruct((s, Q_WIDTH), BF16)], operands=(o,),
        sem=("parallel",), name=name, riders=riders)
    return _ret(core, rr, riders)


def rows_to_heads(x, *, name, tm=256, riders=()):
    s = x.shape[0]
    tm = _row_tile(s, tm)

    def body(x_ref, o_ref):
        xv = x_ref[...].astype(F32)
        for head in range(N_Q_HEADS):
            o_ref[head] = xv[:, head * HEAD_DIM:(head + 1) * HEAD_DIM].astype(BF16)

    core, rr = _call(
        body, grid=(s // tm,), in_specs=[_row_spec(tm, Q_WIDTH)],
        out_specs=[pl.BlockSpec((N_Q_HEADS, tm, HEAD_DIM), lambda i: (0, i, 0))],
        out_shape=[jax.ShapeDtypeStruct((N_Q_HEADS, s, HEAD_DIM), BF16)], operands=(x,), sem=("parallel",), name=name,
        riders=riders)
    return _ret(core, rr, riders)


ROWS = GQA_GROUP * WINDOW


def _attn_probs(q, kc, kp, sink, n):
    scale = HEAD_DIM ** -0.5
    sc = lax.dot_general(q, kc, NT_DIMS, preferred_element_type=F32) * scale
    sp = lax.dot_general(q, kp, NT_DIMS, preferred_element_type=F32) * scale
    qpos = lax.broadcasted_iota(I32, (ROWS, WINDOW), 0) & (WINDOW - 1)
    kpos = lax.broadcasted_iota(I32, (ROWS, WINDOW), 1)
    sc = jnp.where(kpos <= qpos, sc, NEG)
    sp = jnp.where(jnp.logical_and(kpos > qpos, n > 0), sp, NEG)
    m = jnp.maximum(jnp.maximum(jnp.max(sc, axis=-1, keepdims=True), jnp.max(sp, axis=-1, keepdims=True)), sink)
    ec, ep, es = jnp.exp(sc - m), jnp.exp(sp - m), jnp.exp(sink - m)
    inv = 1.0 / (jnp.sum(ec, axis=-1, keepdims=True) + jnp.sum(ep, axis=-1, keepdims=True) + es)
    return ec * inv, ep * inv, es * inv


def _attn_specs(s):
    q_spec = pl.BlockSpec((None, GQA_GROUP, WINDOW, HEAD_DIM), lambda h, n: (h, 0, n, 0))
    cur = pl.BlockSpec((None, WINDOW, HEAD_DIM), lambda h, n: (h, n, 0))
    prev = pl.BlockSpec((None, WINDOW, HEAD_DIM), lambda h, n: (h, jnp.maximum(n - 1, 0), 0))
    sink = pl.BlockSpec((None, ROWS, LANES), lambda h, n: (h, 0, 0))
    return q_spec, cur, prev, sink


def attn_fwd(q, k, v, sink_rows, *, name, riders=()):
    s = k.shape[1]

    def body(q_ref, kc_ref, kp_ref, vc_ref, vp_ref, sink_ref, o_ref):
        qv = q_ref[...].reshape(ROWS, HEAD_DIM)
        pc, pp, _ = _attn_probs(qv, kc_ref[...], kp_ref[...], sink_ref[...], pl.program_id(1))
        o = jnp.dot(pc.astype(BF16), vc_ref[...], preferred_element_type=F32)
        o += jnp.dot(pp.astype(BF16), vp_ref[...], preferred_element_type=F32)
        o_ref[...] = o.reshape(GQA_GROUP, WINDOW, HEAD_DIM).astype(BF16)

    q_spec, cur, prev, sink = _attn_specs(s)
    core, rr = _call(
        body, grid=(N_KV_HEADS, s // WINDOW), in_specs=[q_spec, cur, prev, cur, prev, sink], out_specs=[q_spec],
        out_shape=[jax.ShapeDtypeStruct(q.shape, BF16)], operands=(q, k, k, v, v, sink_rows), sem=("parallel", "parallel"),
        name=name, riders=riders)
    return _ret(core, rr, riders)


def attn_bwd(q, k, v, sink_rows, do, *, name, riders=()):
    s = k.shape[1]

    def body(q_ref, kc_ref, kp_ref, vc_ref, vp_ref, sink_ref, do_ref, dq_ref, dkc_ref, dkp_ref, dvc_ref, dvp_ref, dsink_ref):
        n = pl.program_id(1)
        qv = q_ref[...].reshape(ROWS, HEAD_DIM)
        dov = do_ref[...].reshape(ROWS, HEAD_DIM)
        kc, kp, vc, vp = kc_ref[...], kp_ref[...], vc_ref[...], vp_ref[...]
        pc, pp, ps = _attn_probs(qv, kc, kp, sink_ref[...], n)
        dpc = lax.dot_general(dov, vc, NT_DIMS, preferred_element_type=F32)
        dpp = lax.dot_general(dov, vp, NT_DIMS, preferred_element_type=F32)
        delta = jnp.sum(pc * dpc, axis=-1, keepdims=True) + jnp.sum(pp * dpp, axis=-1, keepdims=True)
        scale = HEAD_DIM ** -0.5
        dsc = (pc * (dpc - delta) * scale).astype(BF16)
        dsp = (pp * (dpp - delta) * scale).astype(BF16)
        dq = jnp.dot(dsc, kc, preferred_element_type=F32) + jnp.dot(dsp, kp, preferred_element_type=F32)
        dq_ref[...] = dq.reshape(GQA_GROUP, WINDOW, HEAD_DIM).astype(BF16)
        dkc_ref[...] = lax.dot_general(dsc, qv, TN_DIMS, preferred_element_type=F32).astype(BF16)
        dkp_ref[...] = lax.dot_general(dsp, qv, TN_DIMS, preferred_element_type=F32).astype(BF16)
        dvc_ref[...] = lax.dot_general(pc.astype(BF16), dov, TN_DIMS, preferred_element_type=F32).astype(BF16)
        dvp_ref[...] = lax.dot_general(pp.astype(BF16), dov, TN_DIMS, preferred_element_type=F32).astype(BF16)
        dsink = -(ps * delta)
        for g in range(GQA_GROUP):
            part = jnp.broadcast_to(jnp.sum(dsink[g * WINDOW:(g + 1) * WINDOW], axis=0, keepdims=True), (8, LANES))

            @pl.when(n == 0)
            def _():
                dsink_ref[g] = part

            @pl.when(n > 0)
            def _():
                dsink_ref[g] += part

    q_spec, cur, prev, sink = _attn_specs(s)
    kv_shape = jax.ShapeDtypeStruct(k.shape, BF16)
    core, rr = _call(
        body, grid=(N_KV_HEADS, s // WINDOW), in_specs=[q_spec, cur, prev, cur, prev, sink, q_spec],
        out_specs=[q_spec, cur, cur, cur, cur, pl.BlockSpec((None, GQA_GROUP, 8, LANES), lambda h, n: (h, 0, 0, 0))],
        out_shape=[jax.ShapeDtypeStruct(q.shape, BF16), kv_shape, kv_shape, kv_shape, kv_shape,
                   jax.ShapeDtypeStruct((N_KV_HEADS, GQA_GROUP, 8, LANES), F32)],
        operands=(q, k, k, v, v, sink_rows, do), sem=("parallel", "arbitrary"), name=name, riders=riders)
    return _ret(core, rr, riders)


GELU_C = 0.7978845608028654
GELU_A = 0.044715


def _gelu(x):
    return 0.5 * x * (1.0 + jnp.tanh(GELU_C * (x + GELU_A * x * x * x)))


def _gelu_grad(x):
    t = jnp.tanh(GELU_C * (x + GELU_A * x * x * x))
    return 0.5 * (1.0 + t) + 0.5 * x * (1.0 - t * t) * GELU_C * (1.0 + 3.0 * GELU_A * x * x)


def _tril_bf16(w):
    row = lax.broadcasted_iota(I32, (SGU_CHUNK, SGU_CHUNK), 0)
    col = lax.broadcasted_iota(I32, (SGU_CHUNK, SGU_CHUNK), 1)
    return jnp.where(row >= col, w, 0.0).astype(BF16)


def _sgu_norm(vg, g, b):
    mu = jnp.mean(vg, axis=-1, keepdims=True)
    cen = vg - mu
    rstd = lax.rsqrt(jnp.mean(cen * cen, axis=-1, keepdims=True) + EPS)
    xhat = cen * rstd
    return xhat, rstd, xhat * g + b


def sgu_fwd(z, ln_g, ln_b, w_sp, b_sp, *, name, tm=256, riders=()):
    s = z.shape[0]
    tm = _row_tile(s, tm)

    def body(z_ref, g_ref, b_ref, w_ref, bs_ref, y_ref):
        u = _gelu(z_ref[:, :D_MODEL])
        _, _, vn = _sgu_norm(_gelu(z_ref[:, D_MODEL:]), g_ref[...], b_ref[...])
        vn = vn.astype(BF16)
        for grp in range(SGU_GROUPS):
            w = _tril_bf16(w_ref[grp])
            cols = slice(grp * LANES, (grp + 1) * LANES)
            for ch in range(tm // SGU_CHUNK):
                rows = slice(ch * SGU_CHUNK, (ch + 1) * SGU_CHUNK)
                mixed = jnp.dot(w, vn[rows, cols], preferred_element_type=F32) + bs_ref[grp]
                y_ref[rows, cols] = (u[rows, cols] * mixed).astype(BF16)

    full3 = pl.BlockSpec((SGU_GROUPS, SGU_CHUNK, SGU_CHUNK), lambda i: (0, 0, 0))
    core, rr = _call(
        body, grid=(s // tm,), in_specs=[_row_spec(tm, 2 * D_MODEL), _vec_spec(D_MODEL), _vec_spec(D_MODEL), full3, full3],
        out_specs=[_row_spec(tm, D_MODEL)], out_shape=[jax.ShapeDtypeStruct((s, D_MODEL), BF16)],
        operands=(z, ln_g, ln_b, w_sp, b_sp), sem=("parallel",), name=name, riders=riders)
    return _ret(core, rr, riders)


def sgu_bwd(z, dy, ln_g, ln_b, w_sp, b_sp, *, name, tm=256, riders=()):
    s = z.shape[0]
    tm = _row_tile(s, tm)

    def body(z_ref, dy_ref, g_ref, b_ref, w_ref, bs_ref, dz_ref, dw_ref, dbs_ref, dg_ref, db_ref, dvn_buf):
        first = pl.program_id(0) == 0
        zu, zv = z_ref[:, :D_MODEL], z_ref[:, D_MODEL:]
        u = _gelu(zu)
        xhat, rstd, vn = _sgu_norm(_gelu(zv), g_ref[...], b_ref[...])
        vn = vn.astype(BF16)
        dyv = dy_ref[...]
        dmixed = dyv * u
        row = lax.broadcasted_iota(I32, (SGU_CHUNK, SGU_CHUNK), 0)
        col = lax.broadcasted_iota(I32, (SGU_CHUNK, SGU_CHUNK), 1)
        for grp in range(SGU_GROUPS):
            w = _tril_bf16(w_ref[grp])
            cols = slice(grp * LANES, (grp + 1) * LANES)
            dw = jnp.zeros((SGU_CHUNK, SGU_CHUNK), F32)
            dbs = jnp.zeros((SGU_CHUNK, 1), F32)
            for ch in range(tm // SGU_CHUNK):
                rows = slice(ch * SGU_CHUNK, (ch + 1) * SGU_CHUNK)
                vblk = vn[rows, cols]
                mixed = jnp.dot(w, vblk, preferred_element_type=F32) + bs_ref[grp]
                dz_ref[rows, cols] = (dyv[rows, cols] * mixed * _gelu_grad(zu[rows, cols])).astype(BF16)
                dm = dmixed[rows, cols]
                dmb = dm.astype(BF16)
                dvn_buf[rows, cols] = lax.dot_general(w, dmb, TN_DIMS, preferred_element_type=F32)
                dw += lax.dot_general(dmb, vblk, NT_DIMS, preferred_element_type=F32)
                dbs += jnp.sum(dm, axis=-1, keepdims=True)
            dw = jnp.where(row >= col, dw, 0.0)
            dbs = jnp.broadcast_to(dbs, (SGU_CHUNK, SGU_CHUNK))

            @pl.when(first)
            def _():
                dw_ref[grp] = dw
                dbs_ref[grp] = dbs

            @pl.when(jnp.logical_not(first))
            def _():
                dw_ref[grp] += dw
                dbs_ref[grp] += dbs

        dvn = dvn_buf[...]
        dxhat = dvn * g_ref[...]
        dvg = rstd * (dxhat - jnp.mean(dxhat, axis=-1, keepdims=True) - xhat * jnp.mean(dxhat * xhat, axis=-1, keepdims=True))
        dz_ref[:, D_MODEL:] = (dvg * _gelu_grad(zv)).astype(BF16)
        _accum(dg_ref, jnp.sum(dvn * xhat, axis=0, keepdims=True), first)
        _accum(db_ref, jnp.sum(dvn, axis=0, keepdims=True), first)

    full3 = pl.BlockSpec((SGU_GROUPS, SGU_CHUNK, SGU_CHUNK), lambda i: (0, 0, 0))
    s3 = jax.ShapeDtypeStruct((SGU_GROUPS, SGU_CHUNK, SGU_CHUNK), F32)
    vshape = jax.ShapeDtypeStruct((1, D_MODEL), F32)
    core, rr = _call(
        body, grid=(s // tm,),
        in_specs=[_row_spec(tm, 2 * D_MODEL), _row_spec(tm, D_MODEL), _vec_spec(D_MODEL), _vec_spec(D_MODEL), full3, full3],
        out_specs=[_row_spec(tm, 2 * D_MODEL), full3, full3, _vec_spec(D_MODEL), _vec_spec(D_MODEL)],
        out_shape=[jax.ShapeDtypeStruct((s, 2 * D_MODEL), BF16), s3, s3, vshape, vshape],
        scratch_shapes=[pltpu.VMEM((tm, D_MODEL), F32)], operands=(z, dy, ln_g, ln_b, w_sp, b_sp), name=name, riders=riders)
    return _ret(core, rr, riders)


def _sigmoid(x):
    return 1.0 / (1.0 + jnp.exp(-x))


def ffn_up(h, w_gu, *, name, tm=512, riders=()):
    s = h.shape[0]
    tm = _row_tile(s, tm)

    def body(h_ref, wg_ref, wu_ref, gu_ref, a_ref):
        hv = h_ref[...]
        g = jnp.dot(hv, wg_ref[...], preferred_element_type=F32)
        u = jnp.dot(hv, wu_ref[...], preferred_element_type=F32)
        gu_ref[0] = g.astype(BF16)
        gu_ref[1] = u.astype(BF16)
        a_ref[...] = (g * _sigmoid(g) * u).astype(BF16)

    core, rr = _call(
        body, grid=(2, s // tm),
        in_specs=[pl.BlockSpec((tm, D_MODEL), lambda j, i: (i, 0)),
                  pl.BlockSpec((None, D_MODEL, FF_HALF), lambda j, i: (j, 0, 0)),
                  pl.BlockSpec((None, D_MODEL, FF_HALF), lambda j, i: (j + 2, 0, 0))],
        out_specs=[pl.BlockSpec((2, tm, FF_HALF), lambda j, i: (0, i, j)), pl.BlockSpec((tm, FF_HALF), lambda j, i: (i, j))],
        out_shape=[jax.ShapeDtypeStruct((2, s, D_FF), BF16), jax.ShapeDtypeStruct((s, D_FF), BF16)],
        operands=(h, w_gu, w_gu), sem=("parallel", "parallel"), name=name, riders=riders)
    return _ret(core, rr, riders)


def ffn_dact(df, w_d, gu, *, name, tm=512, riders=()):
    s = df.shape[0]
    tm = _row_tile(s, tm)

    def body(df_ref, w_ref, gu_ref, o_ref):
        da = lax.dot_general(df_ref[...], w_ref[...], NT_DIMS, preferred_element_type=F32)
        g = gu_ref[0].astype(F32)
        u = gu_ref[1].astype(F32)
        sig = _sigmoid(g)
        o_ref[0] = (da * u * sig * (1.0 + g * (1.0 - sig))).astype(BF16)
        o_ref[1] = (da * g * sig).astype(BF16)

    planes = pl.BlockSpec((2, tm, FF_HALF), lambda j, i: (0, i, j))
    core, rr = _call(
        body, grid=(2, s // tm),
        in_specs=[pl.BlockSpec((tm, D_MODEL), lambda j, i: (i, 0)), pl.BlockSpec((FF_HALF, D_MODEL), lambda j, i: (j, 0)), planes],
        out_specs=[planes], out_shape=[jax.ShapeDtypeStruct((2, s, D_FF), BF16)], operands=(df, w_d, gu),
        sem=("parallel", "parallel"), name=name, riders=riders)
    return _ret(core, rr, riders)


def _weight_tile(rows):
    for tr in (512, 352, 256, 128):
        if rows % tr == 0:
            return tr
    return rows


def place_shard(w, layer, chip_arr, dtype, *, name):
    _, r, c = w.shape
    tr = _weight_tile(r)

    def body(chip_ref, w_ref, o_ref):
        o_ref[...] = w_ref[...].astype(dtype)

    (out,), _ = _call(
        body, grid=(r // tr,), prefetch=(chip_arr,),
        in_specs=[pl.BlockSpec((None, tr, c), lambda i, chip: (layer, i, 0))],
        out_specs=[pl.BlockSpec((None, tr, c), lambda i, chip: (chip[0], i, 0))],
        out_shape=[jax.ShapeDtypeStruct((N_CHIPS, r, c), dtype)], operands=(w,), sem=("parallel",), name=name)
    return out


def _adamw_math(w, g, m, v):
    m = ADAM_B1 * m + (1.0 - ADAM_B1) * g
    v = ADAM_B2 * v + (1.0 - ADAM_B2) * (g * g)
    m_hat = m / (1.0 - ADAM_B1 ** ADAM_STEP)
    v_hat = v / (1.0 - ADAM_B2 ** ADAM_STEP)
    delta = -ADAM_LR * (m_hat / (jnp.sqrt(v_hat) + ADAM_EPS) + ADAM_WD * w)
    return delta, m, v


def adamw(w, g, m, v, *, name):
    nl, r, c = w.shape
    tr = _weight_tile(r)

    def body(w_ref, g_ref, m_ref, v_ref, go_ref, d_ref, mo_ref, vo_ref):
        gv = g_ref[...]
        go_ref[...] = gv
        d_ref[...], mo_ref[...], vo_ref[...] = _adamw_math(w_ref[...], gv, m_ref[...], v_ref[...])

    spec = pl.BlockSpec((None, tr, c), lambda l, i: (l, i, 0))
    shape = jax.ShapeDtypeStruct(w.shape, F32)
    outs, _ = _call(body, grid=(nl, r // tr), in_specs=[spec] * 4, out_specs=[spec] * 4, out_shape=[shape] * 4,
                    operands=(w, g, m, v), sem=("parallel", "parallel"), name=name)
    return outs


def adamw_small(ws, gs, ms, vs, *, name):
    n = len(ws)

    def body(*refs):
        ins, outs = refs[:4 * n], refs[4 * n:]
        for t in range(n):
            gv = ins[n + t][...]
            outs[t][...] = gv
            outs[n + t][...], outs[2 * n + t][...], outs[3 * n + t][...] = _adamw_math(
                ins[t][...], gv, ins[2 * n + t][...], ins[3 * n + t][...])

    shapes = [jax.ShapeDtypeStruct(w.shape, F32) for w in ws]
    res = pl.pallas_call(body, out_shape=shapes * 4, name=name)(*ws, *gs, *ms, *vs)
    return res[:n], res[n:2 * n], res[2 * n:3 * n], res[3 * n:]


def pair_add(g, r1, c_arr, *, name):
    _, rows, cdim = g.shape
    h = rows // 2

    def body(c_ref, g_ref, r_ref, o_ref):
        o_ref[...] = (g_ref[...].astype(F32) + r_ref[...].astype(F32)).astype(o_ref.dtype)

    (out,), _ = _call(
        body, grid=(N_CHIPS,), prefetch=(c_arr,),
        in_specs=[pl.BlockSpec((None, h, cdim), lambda s, c: (s, c[0], 0)), pl.BlockSpec((None, h, cdim), lambda s, c: (s, 0, 0))],
        out_specs=[pl.BlockSpec((None, h, cdim), lambda s, c: (s, 0, 0))],
        out_shape=[jax.ShapeDtypeStruct((N_CHIPS, h, cdim), g.dtype)], operands=(g, r1), sem=("parallel",), name=name)
    return out


def final_add(g, r1, r2, jc_arr, *, dest_shape, lead, prev, name):
    _, rows, cdim = g.shape
    h = rows // 2

    def body(jc_ref, g_ref, r1_ref, r2_ref, *rest):
        o_ref = rest[-1]
        acc = g_ref[...].astype(F32) + r1_ref[...].astype(F32)
        for k in range(3):
            acc = acc + r2_ref[k].astype(F32)
        o_ref[...] = acc

    if lead is None:
        o_spec = pl.BlockSpec((h, cdim), lambda i, jc: (jc[1], 0))
    elif lead == "chip":
        o_spec = pl.BlockSpec((None, h, cdim), lambda i, jc: (jc[0], jc[1], 0))
    else:
        o_spec = pl.BlockSpec((None, h, cdim), lambda i, jc: (lead, jc[1], 0))
    in_specs = [pl.BlockSpec((None, h, cdim), lambda i, jc: (jc[0], jc[1], 0)),
                pl.BlockSpec((None, h, cdim), lambda i, jc: (jc[0], 0, 0)),
                pl.BlockSpec((3, h, cdim), lambda i, jc: (0, 0, 0))]
    operands = [g, r1, r2]
    aliases = None
    if prev is not None:
        in_specs.append(ANY)
        operands.append(prev)
        aliases = {3: 0}
    (out,), _ = _call(body, grid=(1,), prefetch=(jc_arr,), in_specs=in_specs, out_specs=[o_spec],
                      out_shape=[jax.ShapeDtypeStruct(dest_shape, F32)], operands=operands, aliases=aliases, name=name)
    return out


def _place():
    return lax.axis_index("x"), lax.axis_index("y"), lax.axis_index("c")


def _partner(x, y, k):
    return (1 - x if k >> 1 else x), (1 - y if k & 1 else y)


def _half(rows, sel, dtype):
    align = 16 if dtype == BF16 else 8
    return pl.ds(pl.multiple_of(sel * (rows // 2), align), rows // 2)


def _rider(inputs, aliased, fresh, nsem, copies, arrivals):
    def start(ins, outs, send, recv):
        for cp in copies(ins, outs, send, recv):
            cp.start()

    def finish(ins, outs, send, recv):
        for cp in arrivals(ins, outs, send, recv):
            cp.wait_recv()
        for cp in copies(ins, outs, send, recv):
            cp.wait_send()

    return types.SimpleNamespace(inputs=list(inputs), aliased=list(aliased), fresh=list(fresh), nsem=nsem, start=start,
                                 finish=finish)


def _remote(src, dst, send, recv, idx, dev):
    return pltpu.make_async_remote_copy(src_ref=src, dst_ref=dst, send_sem=send.at[idx], recv_sem=recv.at[idx],
                                        device_id=dev, device_id_type=MESH)


def gather_ici_rider(fulls):
    nt = len(fulls)

    def region(outs, t, slot, sel):
        return outs[t].at[slot, _half(fulls[t].shape[1], sel, fulls[t].dtype)]

    def copies(ins, outs, send, recv):
        x, y, c = _place()
        res = []
        for t in range(nt):
            for k in (1, 2, 3):
                px, py = _partner(x, y, k)
                mine = region(outs, t, 2 * x + y, c)
                res.append(_remote(mine, mine, send, recv, 3 * t + k - 1, (px, py, c)))
        return res

    def arrivals(ins, outs, send, recv):
        x, y, c = _place()
        res = []
        for t in range(nt):
            for k in (1, 2, 3):
                px, py = _partner(x, y, k)
                theirs = region(outs, t, 2 * px + py, c)
                res.append(_remote(theirs, theirs, send, recv, 3 * t + k - 1, (x, y, c)))
        return res

    return _rider(fulls, range(nt), [], 3 * nt, copies, arrivals)


def gather_d2d_rider(fulls):
    nt = len(fulls)

    def region(outs, t, slot, sel):
        return outs[t].at[slot, _half(fulls[t].shape[1], sel, fulls[t].dtype)]

    def both(outs, send, recv, mine):
        x, y, c = _place()
        res = []
        for t in range(nt):
            for k in (1, 2, 3):
                px, py = _partner(x, y, k)
                part = region(outs, t, 2 * px + py, c if mine else 1 - c)
                res.append(_remote(part, part, send, recv, 3 * t + k - 1, (x, y, 1 - c)))
        return res

    return _rider(fulls, range(nt), [], 3 * nt, lambda i, o, s, r: both(o, s, r, True), lambda i, o, s, r: both(o, s, r, False))


def exchange_rider(grads):
    nt = len(grads)

    def both(ins, outs, send, recv):
        x, y, c = _place()
        return [_remote(ins[t].at[:, _half(grads[t].shape[1], 1 - c, grads[t].dtype)], outs[t], send, recv, t, (x, y, 1 - c))
                for t in range(nt)]

    fresh = [jax.ShapeDtypeStruct((N_CHIPS, g.shape[1] // 2, g.shape[2]), g.dtype) for g in grads]
    return _rider(grads, [], fresh, nt, both, both)


def scatter_rider(parts):
    nt = len(parts)

    def both(ins, outs, send, recv):
        x, y, c = _place()
        res = []
        for t in range(nt):
            for k in (1, 2, 3):
                px, py = _partner(x, y, k)
                res.append(_remote(ins[t].at[2 * px + py], outs[t].at[k - 1], send, recv, 3 * t + k - 1, (px, py, c)))
        return res

    fresh = [jax.ShapeDtypeStruct((3,) + p.shape[1:], p.dtype) for p in parts]
    return _rider(parts, [], fresh, 3 * nt, both, both)


def broadcast_rider(bufs, items):
    def region(outs, item, sel):
        bi, lead = item
        ref = outs[bi]
        if lead == "chip":
            x, y, _ = _place()
            ref = ref.at[2 * x + y]
        elif lead is not None:
            ref = ref.at[lead]
        return ref.at[_half(ref.shape[0], sel, F32)]

    def both(outs, send, recv, mine):
        x, y, c = _place()
        res = []
        for i, item in enumerate(items):
            part = region(outs, item, c if mine else 1 - c)
            res.append(_remote(part, part, send, recv, i, (x, y, 1 - c)))
        return res

    return _rider(bufs, range(len(bufs)), [], len(items), lambda i, o, s, r: both(o, s, r, True),
                  lambda i, o, s, r: both(o, s, r, False))


def allcast_rider(buf):
    peers = [(k, flip) for k in range(N_CHIPS) for flip in (0, 1) if (k, flip) != (0, 0)]

    def both(outs, send, recv, mine):
        x, y, c = _place()
        res = []
        for i, (k, flip) in enumerate(peers):
            px, py = _partner(x, y, k)
            pc = 1 - c if flip else c
            slot, sel = (2 * x + y, c) if mine else (2 * px + py, pc)
            part = outs[0].at[slot, _half(buf.shape[1], sel, F32)]
            res.append(_remote(part, part, send, recv, i, (px, py, pc)))
        return res

    return _rider([buf], [0], [], len(peers), lambda i, o, s, r: both(o, s, r, True), lambda i, o, s, r: both(o, s, r, False))


def comm_call(riders, *, name):
    _, res = _call(None, riders=riders, name=name)
    return res


SLAB_ROWS = 192


def _pad_rows(a, rows=8):
    return jnp.pad(a, ((0, rows - a.shape[0]), (0, 0)))


def _pack_small(norm_grads, db_qkv, db_o, dsinks, db_sp, dln_g, dln_b, dw_sp):
    parts = [
        jnp.concatenate(norm_grads, axis=0),
        _pad_rows(jnp.pad(db_qkv, ((0, 0), (0, 2 * D_MODEL - QKV_WIDTH))).reshape(2, D_MODEL)),
        _pad_rows(db_o),
        _pad_rows(jnp.pad(dsinks.reshape(1, N_Q_HEADS), ((0, 0), (0, D_MODEL - N_Q_HEADS)))),
        _pad_rows(db_sp.reshape(1, D_MODEL)),
        _pad_rows(jnp.concatenate([dln_g, dln_b], axis=0)),
        dw_sp.reshape(SGU_CHUNK, D_MODEL),
    ]
    slab = jnp.concatenate(parts, axis=0)
    return jnp.pad(slab, ((0, SLAB_ROWS - slab.shape[0]), (0, 0))).reshape(N_CHIPS, SLAB_ROWS // N_CHIPS, D_MODEL)


def _unpack_small(slab, j):
    slab = slab.reshape(SLAB_ROWS, D_MODEL)
    norms = [slab[2 * i:2 * i + 2] for i in range(4)]
    db_qkv = slab[8:10].reshape(1, 2 * D_MODEL)[:, :QKV_WIDTH]
    db_o = slab[16:17]
    dsinks = slab[24:25, :N_Q_HEADS]
    db_sp = slab[32:33].reshape(SGU_GROUPS, SGU_CHUNK)
    width = D_MODEL // N_CHIPS
    dln_g = lax.dynamic_slice(slab[40:41], (0, j * width), (1, width))
    dln_b = lax.dynamic_slice(slab[41:42], (0, j * width), (1, width))
    dw_sp = slab[48:48 + SGU_CHUNK].reshape(SGU_GROUPS * SGU_CHUNK, SGU_CHUNK)
    return norms, db_qkv, db_o, dsinks, db_sp, dln_g, dln_b, dw_sp


class _GradReduce:
    def __init__(self, c_arr, jc_arr, dest_shapes):
        self.c_arr, self.jc_arr, self.dest_shapes = c_arr, jc_arr, dest_shapes
        self.grad, self.sibling, self.pair, self.chips, self.dest = {}, {}, {}, {}, {}

    def exchange(self, tags):
        return exchange_rider([self.grad[t] for t in tags])

    def exchanged(self, tags, res):
        for t, r in zip(tags, res):
            self.sibling[t] = r
            self.pair[t] = pair_add(self.grad[t], r, self.c_arr, name=f"pair_add_{t}")

    def scatter(self, tags):
        return scatter_rider([self.pair[t] for t in tags])

    def scattered(self, tags, res, where):
        for t, r in zip(tags, res):
            name, lead = where[t]
            self.dest[name] = final_add(self.grad[t], self.sibling[t], r, self.jc_arr, dest_shape=self.dest_shapes[name],
                                        lead=lead, prev=self.dest.get(name), name=f"final_add_{t}")

    def broadcast(self, items):
        names = []
        for n, _ in items:
            if n not in names:
                names.append(n)
        return names, broadcast_rider([self.dest[n] for n in names], [(names.index(n), lead) for n, lead in items])

    def broadcasted(self, names, res):
        for n, r in zip(names, res):
            self.dest[n] = r


def kernel(x, norm_mix_pre, norm_mix_post, norm_ffn_pre, norm_ffn_post, attn_w_qkv, attn_b_qkv, attn_sinks, attn_w_o, attn_b_o, sgu_w_in, sgu_ln_g, sgu_ln_b, sgu_w_spatial, sgu_b_spatial, sgu_w_out, ffn_w_gate_up, ffn_w_down, loss_target, m_norm_mix_pre, m_norm_mix_post, m_norm_ffn_pre, m_norm_ffn_post, m_attn_w_qkv, m_attn_b_qkv, m_attn_sinks, m_attn_w_o, m_attn_b_o, m_sgu_w_in, m_sgu_ln_g, m_sgu_ln_b, m_sgu_w_spatial, m_sgu_b_spatial, m_sgu_w_out, m_ffn_w_gate_up, m_ffn_w_down, v_norm_mix_pre, v_norm_mix_post, v_norm_ffn_pre, v_norm_ffn_post, v_attn_w_qkv, v_attn_b_qkv, v_attn_sinks, v_attn_w_o, v_attn_b_o, v_sgu_w_in, v_sgu_ln_g, v_sgu_ln_b, v_sgu_w_spatial, v_sgu_b_spatial, v_sgu_w_out, v_ffn_w_gate_up, v_ffn_w_down):
    s = x.shape[1]
    x0 = x.reshape(s, D_MODEL)
    target = loss_target.reshape(s, D_MODEL)
    mx, my, mc = lax.axis_index("x"), lax.axis_index("y"), lax.axis_index("c")
    chip = 2 * mx + my
    chip_arr = jnp.reshape(chip, (1,)).astype(I32)
    c_arr = jnp.reshape(mc, (1,)).astype(I32)
    jc_arr = jnp.stack([chip, mc]).astype(I32)
    zero_bias = jnp.zeros((1, D_MODEL), F32)

    def gain(p, i):
        return p[i:i + 1]

    big = [attn_w_qkv, attn_w_o, sgu_w_in, sgu_w_out, ffn_w_gate_up, ffn_w_gate_up, ffn_w_down, ffn_w_down]
    layers = [0, 0, 0, 0, 0, 1, 0, 1]
    tags = ["qkv", "wo", "win", "wout", "wgu0", "wgu1", "wd0", "wd1"]
    full = {t: place_shard(w, l, chip_arr, BF16, name=f"place_{t}") for w, l, t in zip(big, layers, tags)}
    ln_pack = _pad_rows(jnp.concatenate([sgu_ln_g, sgu_ln_b], axis=0), 16)[None]
    full["ln"] = place_shard(ln_pack, 0, chip_arr, F32, name="place_ln")

    def ici(*names):
        return gather_ici_rider([full[n] for n in names])

    def d2d(*names):
        return gather_d2d_rider([full[n] for n in names])

    def landed(names, res):
        for n, r in zip(names, res):
            full[n] = r

    cos, sin = _rope_tables(s)
    sink_rows = jnp.broadcast_to(
        jnp.repeat(attn_sinks.reshape(N_KV_HEADS, GQA_GROUP), WINDOW, axis=1)[:, :, None], (N_KV_HEADS, ROWS, LANES))
    w_sp = sgu_w_spatial.reshape(SGU_GROUPS, SGU_CHUNK, SGU_CHUNK)
    b_sp = jnp.broadcast_to(sgu_b_spatial.reshape(SGU_GROUPS, SGU_CHUNK)[:, :, None], (SGU_GROUPS, SGU_CHUNK, LANES))

    (res,) = comm_call([ici("qkv", "ln")], name="gather_first")
    landed(("qkv", "ln"), res)
    h0, (res,) = prenorm(x0, gain(norm_mix_pre, 0), name="prenorm_0", riders=[d2d("qkv", "ln")])
    landed(("qkv", "ln"), res)
    ln_g = full["ln"][:, 0, :].reshape(1, D_MODEL)
    ln_b = full["ln"][:, 1, :].reshape(1, D_MODEL)

    qkv, (res,) = mm_nn(h0, full["qkv"], out_dtype=F32, name="qkv_proj", riders=[ici("wo")])
    landed(("wo",), res)
    (q, k, v), (res,) = rope_fwd(qkv, attn_b_qkv, cos, sin, name="rope_fwd", riders=[d2d("wo")])
    landed(("wo",), res)
    q = q.reshape(N_KV_HEADS, GQA_GROUP, s, HEAD_DIM)
    o_heads, (res,) = attn_fwd(q, k, v, sink_rows, name="attn_fwd", riders=[ici("wgu0", "wd0")])
    landed(("wgu0", "wd0"), res)
    o, (res,) = heads_to_rows(o_heads.reshape(N_Q_HEADS, s, HEAD_DIM), name="heads_to_rows", riders=[d2d("wgu0", "wd0")])
    landed(("wgu0", "wd0"), res)
    w_o = full["wo"].reshape(Q_WIDTH, D_MODEL)
    m0, (res,) = mm_nn(o, w_o, out_dtype=F32, name="attn_out_proj", riders=[ici("win")])
    landed(("win",), res)
    (x1, h1), (res_a, res_b) = residual_norm(x0, m0, attn_b_o, gain(norm_mix_post, 0), gain(norm_ffn_pre, 0),
                                             name="residual_norm_0a", riders=[d2d("win"), ici("wout")])
    landed(("win",), res_a)
    landed(("wout",), res_b)
    (gu0, a0), (res_a, res_b) = ffn_up(h1, full["wgu0"], name="ffn_up_0", riders=[d2d("wout"), ici("wgu1")])
    landed(("wout",), res_a)
    landed(("wgu1",), res_b)
    w_d0 = full["wd0"].reshape(D_FF, D_MODEL)
    f0, (res_a, res_b) = mm_nn(a0, w_d0, out_dtype=F32, name="ffn_down_0", riders=[d2d("wgu1"), ici("wd1")])
    landed(("wgu1",), res_a)
    landed(("wd1",), res_b)
    (x2, h2), (res,) = residual_norm(x1, f0, zero_bias, gain(norm_ffn_post, 0), gain(norm_mix_pre, 1),
                                     name="residual_norm_0b", riders=[d2d("wd1")])
    landed(("wd1",), res)
    w_qkv, w_in, w_gu0, w_gu1 = full["qkv"], full["win"], full["wgu0"], full["wgu1"]
    w_out = full["wout"].reshape(D_MODEL, D_MODEL)
    w_d1 = full["wd1"].reshape(D_FF, D_MODEL)
    z = mm_nn(h2, w_in, out_dtype=F32, name="sgu_in_proj")
    y = sgu_fwd(z, ln_g, ln_b, w_sp, b_sp, name="sgu_fwd")
    m1 = mm_nn(y, w_out, out_dtype=F32, name="sgu_out_proj")
    x3, h3 = residual_norm(x2, m1, zero_bias, gain(norm_mix_post, 1), gain(norm_ffn_pre, 1), name="residual_norm_1a")
    gu1, a1 = ffn_up(h3, w_gu1, name="ffn_up_1")
    f1 = mm_nn(a1, w_d1, out_dtype=F32, name="ffn_down_1")
    dx4, df1, dg_fpost1, loss_part = loss_head(x3, f1, gain(norm_ffn_post, 1), target, name="loss_head")
    loss = lax.psum(loss_part[0, 0], ("x", "y", "c"))

    red = _GradReduce(c_arr, jc_arr, {
        "qkv": attn_w_qkv.shape[1:], "wo": attn_w_o.shape[1:], "win": sgu_w_in.shape[1:], "wout": sgu_w_out.shape[1:],
        "wgu": ffn_w_gate_up.shape, "wd": ffn_w_down.shape, "slab": (N_CHIPS, SLAB_ROWS // N_CHIPS, D_MODEL)})
    where = {"qkv": ("qkv", None), "wo": ("wo", None), "win": ("win", None), "wout": ("wout", None), "wgu0": ("wgu", 0),
             "wgu1": ("wgu", 1), "wd0": ("wd", 0), "wd1": ("wd", 1), "small": ("slab", "chip")}

    dgu1 = ffn_dact(df1, w_d1, gu1, name="ffn_dact_1")
    red.grad["wd1"] = mm_tn(a1, df1, shard_major=False, tm=FF_HALF, tn=D_MODEL, name="dw_down_1").reshape(
        N_CHIPS, D_FF // N_CHIPS, D_MODEL)
    red.grad["wgu1"], (res,) = mm_tn(h3, dgu1, shard_major=True, tm=D_MODEL, tn=FF_HALF, name="dw_gate_up_1",
                                     riders=[red.exchange(["wd1"])])
    red.exchanged(["wd1"], res)
    dh3, (res_a, res_b) = mm_nt(dgu1, w_gu1, out_dtype=F32, tm=1024, name="dh_ffn_1",
                                riders=[red.exchange(["wgu1"]), red.scatter(["wd1"])])
    red.exchanged(["wgu1"], res_a)
    red.scattered(["wd1"], res_b, where)
    names, rider = red.broadcast([("wd", 1)])
    (dx3, dm1, dg_fpre1, dg_mpost1, _), (res,) = norm_bwd_pair(
        dx4, dh3, x3, gain(norm_ffn_pre, 1), m1, zero_bias, gain(norm_mix_post, 1), name="norm_bwd_1a", riders=[rider])
    red.broadcasted(names, res)
    dy = mm_nt(dm1, w_out, out_dtype=F32, name="dy_sgu")
    red.grad["wout"] = mm_tn(y, dm1, shard_major=False, tm=D_MODEL, tn=D_MODEL, name="dw_sgu_out").reshape(
        N_CHIPS, D_MODEL // N_CHIPS, D_MODEL)
    (dz, dw_sp, db_sp, dln_g, dln_b), (res_a, res_b) = sgu_bwd(
        z, dy, ln_g, ln_b, w_sp, b_sp, name="sgu_bwd", riders=[red.scatter(["wgu1"]), red.exchange(["wout"])])
    red.scattered(["wgu1"], res_a, where)
    red.exchanged(["wout"], res_b)
    names, rider = red.broadcast([("wgu", 1)])
    red.grad["win"], (res_a, res_b) = mm_tn(h2, dz, shard_major=True, tm=D_MODEL, tn=2 * D_MODEL // N_CHIPS, name="dw_sgu_in",
                                            riders=[rider, red.scatter(["wout"])])
    red.broadcasted(names, res_a)
    red.scattered(["wout"], res_b, where)
    names, rider = red.broadcast([("wout", None)])
    dh2, (res_a, res_b) = mm_nt(dz, w_in, out_dtype=F32, tm=1024, name="dh_sgu", riders=[red.exchange(["win"]), rider])
    red.exchanged(["win"], res_a)
    red.broadcasted(names, res_b)
    (dx2, df0, dg_mpre1, dg_fpost0, _), (res,) = norm_bwd_pair(
        dx3, dh2, x2, gain(norm_mix_pre, 1), f0, zero_bias, gain(norm_ffn_post, 0), name="norm_bwd_0b",
        riders=[red.scatter(["win"])])
    red.scattered(["win"], res, where)
    names, rider = red.broadcast([("win", None)])
    dgu0, (res,) = ffn_dact(df0, w_d0, gu0, name="ffn_dact_0", riders=[rider])
    red.broadcasted(names, res)
    red.grad["wd0"] = mm_tn(a0, df0, shard_major=False, tm=FF_HALF, tn=D_MODEL, name="dw_down_0").reshape(
        N_CHIPS, D_FF // N_CHIPS, D_MODEL)
    red.grad["wgu0"], (res,) = mm_tn(h1, dgu0, shard_major=True, tm=D_MODEL, tn=FF_HALF, name="dw_gate_up_0",
                                     riders=[red.exchange(["wd0"])])
    red.exchanged(["wd0"], res)
    dh1, (res_a, res_b) = mm_nt(dgu0, w_gu0, out_dtype=F32, tm=1024, name="dh_ffn_0",
                                riders=[red.exchange(["wgu0"]), red.scatter(["wd0"])])
    red.exchanged(["wgu0"], res_a)
    red.scattered(["wd0"], res_b, where)
    names, rider = red.broadcast([("wd", 0)])
    (dx1, dm0, dg_fpre0, dg_mpost0, db_o), (res,) = norm_bwd_pair(
        dx2, dh1, x1, gain(norm_ffn_pre, 0), m0, attn_b_o, gain(norm_mix_post, 0), name="norm_bwd_0a", riders=[rider])
    red.broadcasted(names, res)
    do = mm_nt(dm0, w_o, out_dtype=BF16, name="do_attn")
    red.grad["wo"] = mm_tn(o, dm0, shard_major=False, tm=D_MODEL, tn=D_MODEL, name="dw_attn_out").reshape(
        N_CHIPS, Q_WIDTH // N_CHIPS, D_MODEL)
    do_heads, (res,) = rows_to_heads(do, name="rows_to_heads", riders=[red.exchange(["wo"])])
    red.exchanged(["wo"], res)
    do_heads = do_heads.reshape(N_KV_HEADS, GQA_GROUP, s, HEAD_DIM)
    (dq, dkc, dkp, dvc, dvp, dsink), (res,) = attn_bwd(q, k, v, sink_rows, do_heads, name="attn_bwd",
                                                       riders=[red.scatter(["wgu0", "wo"])])
    red.scattered(["wgu0", "wo"], res, where)
    names, rider = red.broadcast([("wgu", 0), ("wo", None)])
    (dqkv, db_qkv), (res,) = rope_bwd(dq.reshape(N_Q_HEADS, s, HEAD_DIM), dkc, dkp, dvc, dvp, cos, sin, name="rope_bwd",
                                      riders=[rider])
    red.broadcasted(names, res)
    red.grad["qkv"] = mm_tn(h0, dqkv, shard_major=True, tm=D_MODEL, tn=QKV_WIDTH // N_CHIPS, name="dw_qkv")
    dh0, (res,) = mm_nt(dqkv, w_qkv, out_dtype=F32, tm=1024, name="dh_attn", riders=[red.exchange(["qkv"])])
    red.exchanged(["qkv"], res)
    grad_x, dg_mpre0 = norm_bwd_last(dx1, dh0, x0, gain(norm_mix_pre, 0), name="norm_bwd_in")

    norm_grads = [jnp.concatenate(p, axis=0) for p in
                  ((dg_mpre0, dg_mpre1), (dg_mpost0, dg_mpost1), (dg_fpre0, dg_fpre1), (dg_fpost0, dg_fpost1))]
    red.grad["small"] = _pack_small(norm_grads, db_qkv, db_o, dsink[:, :, 0, 0], db_sp[:, :, 0], dln_g, dln_b, dw_sp)
    res_a, res_b = comm_call([red.scatter(["qkv"]), red.exchange(["small"])], name="tail_1")
    red.scattered(["qkv"], res_a, where)
    red.exchanged(["small"], res_b)
    names, rider = red.broadcast([("qkv", None)])
    res_a, res_b = comm_call([red.scatter(["small"]), rider], name="tail_2")
    red.scattered(["small"], res_a, where)
    red.broadcasted(names, res_b)
    ((slab_full,),) = comm_call([allcast_rider(red.dest["slab"])], name="tail_3")
    g_qkv, g_wo, g_win, g_wout, g_wgu, g_wd = (red.dest[n] for n in ("qkv", "wo", "win", "wout", "wgu", "wd"))
    g_norms, g_bqkv, g_bo, g_sinks, g_bsp, g_lng, g_lnb, g_wsp = _unpack_small(slab_full, chip)

    def big_update(w, g, m, v, tag):
        return adamw(w, g.reshape(w.shape), m, v, name=f"adamw_{tag}")

    upd = {
        "attn_w_qkv": big_update(attn_w_qkv, g_qkv, m_attn_w_qkv, v_attn_w_qkv, "qkv"),
        "attn_w_o": big_update(attn_w_o, g_wo, m_attn_w_o, v_attn_w_o, "wo"),
        "sgu_w_in": big_update(sgu_w_in, g_win, m_sgu_w_in, v_sgu_w_in, "win"),
        "sgu_w_out": big_update(sgu_w_out, g_wout, m_sgu_w_out, v_sgu_w_out, "wout"),
        "ffn_w_gate_up": big_update(ffn_w_gate_up, g_wgu, m_ffn_w_gate_up, v_ffn_w_gate_up, "wgu"),
        "ffn_w_down": big_update(ffn_w_down, g_wd, m_ffn_w_down, v_ffn_w_down, "wd"),
    }
    small_names = ["norm_mix_pre", "norm_mix_post", "norm_ffn_pre", "norm_ffn_post", "attn_b_qkv", "attn_sinks", "attn_b_o",
                   "sgu_ln_g", "sgu_ln_b", "sgu_w_spatial", "sgu_b_spatial"]
    small_w = [norm_mix_pre, norm_mix_post, norm_ffn_pre, norm_ffn_post, attn_b_qkv, attn_sinks, attn_b_o, sgu_ln_g, sgu_ln_b,
               sgu_w_spatial, sgu_b_spatial]
    small_m = [m_norm_mix_pre, m_norm_mix_post, m_norm_ffn_pre, m_norm_ffn_post, m_attn_b_qkv, m_attn_sinks, m_attn_b_o,
               m_sgu_ln_g, m_sgu_ln_b, m_sgu_w_spatial, m_sgu_b_spatial]
    small_v = [v_norm_mix_pre, v_norm_mix_post, v_norm_ffn_pre, v_norm_ffn_post, v_attn_b_qkv, v_attn_sinks, v_attn_b_o,
               v_sgu_ln_g, v_sgu_ln_b, v_sgu_w_spatial, v_sgu_b_spatial]
    small_g = g_norms + [g_bqkv, g_sinks, g_bo, g_lng, g_lnb, g_wsp, g_bsp]

    def flat2(a):
        return a.reshape(-1, a.shape[-1])

    res = adamw_small([flat2(a) for a in small_w], [flat2(a) for a in small_g], [flat2(a) for a in small_m],
                      [flat2(a) for a in small_v], name="adamw_small")
    for i, nm in enumerate(small_names):
        upd[nm] = tuple(r[i].reshape(small_w[i].shape) for r in res)

    order = ["norm_mix_pre", "norm_mix_post", "norm_ffn_pre", "norm_ffn_post", "attn_w_qkv", "attn_b_qkv", "attn_sinks",
             "attn_w_o", "attn_b_o", "sgu_w_in", "sgu_ln_g", "sgu_ln_b", "sgu_w_spatial", "sgu_b_spatial", "sgu_w_out",
             "ffn_w_gate_up", "ffn_w_down"]
    outs = [loss, grad_x.reshape(1, s, D_MODEL)]
    for part in range(4):
        outs += [upd[nm][part] for nm in order]
    return tuple(outs)
```

```python
import types

import jax
import jax.numpy as jnp
from jax import lax
from jax.experimental import pallas as pl
from jax.experimental.pallas import tpu as pltpu

F32 = jnp.float32
BF16 = jnp.bfloat16
I32 = jnp.int32

D_MODEL = 1024
HEAD_DIM = 64
N_Q_HEADS = 16
N_KV_HEADS = 4
GQA_GROUP = 4
WINDOW = 128
Q_WIDTH = 1024
KV_WIDTH = 256
QKV_WIDTH = 1536
ROPE_THETA = 10000.0
SGU_GROUPS = 8
SGU_CHUNK = 128
D_FF = 2816
FF_HALF = D_FF // 2
EPS = 1e-6
N_CHIPS = 4
LANES = 128

ADAM_LR = 0.001
ADAM_B1 = 0.9
ADAM_B2 = 0.999
ADAM_EPS = 1e-08
ADAM_WD = 0.01
ADAM_STEP = 10

VMEM_LIMIT = 52 * 1024 * 1024
MESH = pl.DeviceIdType.MESH
NEG = -1e30
NT_DIMS = (((1,), (1,)), ((), ()))
TN_DIMS = (((0,), (0,)), ((), ()))
NN_DIMS = (((1,), (0,)), ((), ()))
ANY = pl.BlockSpec(memory_space=pl.ANY)


def _row_tile(s, want):
    return want if s % want == 0 else s


def _call(body, *, name, grid=(), in_specs=(), out_specs=(), out_shape=(), scratch_shapes=(), operands=(), prefetch=(),
          aliases=None, riders=(), sem=None):
    n_pre, n_in, n_out, n_scr = len(prefetch), len(operands), len(out_shape), len(scratch_shapes)
    in_specs, out_specs, out_shape = list(in_specs), list(out_specs), list(out_shape)
    operands, scratch_shapes = list(operands), list(scratch_shapes)
    io_alias = {n_pre + i: o for i, o in (aliases or {}).items()}
    for r in riders:
        base_in, base_out = n_pre + len(operands), len(out_shape)
        operands += list(r.inputs)
        in_specs += [ANY] * len(r.inputs)
        for pos, i in enumerate(r.aliased):
            io_alias[base_in + i] = base_out + pos
            out_shape.append(jax.ShapeDtypeStruct(r.inputs[i].shape, r.inputs[i].dtype))
        out_shape += list(r.fresh)
        out_specs += [ANY] * (len(r.aliased) + len(r.fresh))
        scratch_shapes += [pltpu.SemaphoreType.DMA((r.nsem,)), pltpu.SemaphoreType.DMA((r.nsem,))]

    def wrapped(*refs):
        pre, p = refs[:n_pre], n_pre
        core_in, p = refs[p:p + n_in], p + n_in
        r_in = []
        for r in riders:
            r_in.append(refs[p:p + len(r.inputs)])
            p += len(r.inputs)
        core_out, p = refs[p:p + n_out], p + n_out
        r_out = []
        for r in riders:
            k = len(r.aliased) + len(r.fresh)
            r_out.append(refs[p:p + k])
            p += k
        core_scr, p = refs[p:p + n_scr], p + n_scr
        r_sem = [refs[p + 2 * i:p + 2 * i + 2] for i in range(len(riders))]

        def edge(at_last, fns):
            def run():
                for i, r in enumerate(riders):
                    getattr(r, fns)(r_in[i], r_out[i], r_sem[i][0], r_sem[i][1])
            if not riders:
                return
            if not grid:
                run()
                return
            cond = None
            for d, n in enumerate(grid):
                c = pl.program_id(d) == (n - 1 if at_last else 0)
                cond = c if cond is None else jnp.logical_and(cond, c)
            pl.when(cond)(run)

        edge(False, "start")
        if body is not None:
            body(*pre, *core_in, *core_out, *core_scr)
        edge(True, "finish")

    if sem is None or riders:
        sem = ("arbitrary",) * len(grid)
    kwargs = dict(out_shape=out_shape, input_output_aliases=io_alias, name=name)
    if grid:
        kwargs["compiler_params"] = pltpu.CompilerParams(dimension_semantics=sem, vmem_limit_bytes=VMEM_LIMIT)
    if n_pre:
        kwargs["grid_spec"] = pltpu.PrefetchScalarGridSpec(
            num_scalar_prefetch=n_pre, grid=grid, in_specs=in_specs, out_specs=out_specs, scratch_shapes=scratch_shapes)
    else:
        kwargs.update(grid=grid, in_specs=in_specs, out_specs=out_specs, scratch_shapes=scratch_shapes)
    res = pl.pallas_call(wrapped, **kwargs)(*prefetch, *operands)
    core, rest, rider_res = list(res[:n_out]), list(res[n_out:]), []
    for r in riders:
        k = len(r.aliased) + len(r.fresh)
        rider_res.append(rest[:k])
        rest = rest[k:]
    return core, rider_res


def _mm_call(*, grid, in_specs, out_spec, out_shape, dims, nk, kaxis, acc_shape, name, operands, riders=()):
    out_dtype = out_shape.dtype

    def body(a_ref, b_ref, o_ref, *scratch):
        p = lax.dot_general(a_ref[...].astype(BF16), b_ref[...].astype(BF16), dims, preferred_element_type=F32)
        if nk == 1:
            o_ref[...] = p.astype(out_dtype)
        else:
            acc = scratch[0]
            kk = pl.program_id(kaxis)

            @pl.when(kk == 0)
            def _():
                acc[...] = p

            @pl.when(kk > 0)
            def _():
                acc[...] += p

            @pl.when(kk == nk - 1)
            def _():
                o_ref[...] = acc[...].astype(out_dtype)

    sem = ["parallel"] * len(grid)
    if nk > 1:
        sem[kaxis] = "arbitrary"
    (out,), rider_res = _call(
        body, grid=grid, in_specs=in_specs, out_specs=[out_spec], out_shape=[out_shape],
        scratch_shapes=[pltpu.VMEM(acc_shape, F32)] if nk > 1 else [], operands=operands, name=name, riders=riders,
        sem=tuple(sem))
    return (out, rider_res) if riders else out


def mm_nn(a, w, *, out_dtype, name, tm=512, tn=512, riders=()):
    m, k = a.shape
    tm = _row_tile(m, tm)
    if w.ndim == 3:
        ns = w.shape[2]
        grid = (N_CHIPS, m // tm)
        w_spec = pl.BlockSpec((None, k, ns), lambda j, i: (j, 0, 0))
        o_spec = pl.BlockSpec((tm, ns), lambda j, i: (i, j))
        n = N_CHIPS * ns
    else:
        n = w.shape[1]
        grid = (n // tn, m // tm)
        w_spec = pl.BlockSpec((k, tn), lambda j, i: (0, j))
        o_spec = pl.BlockSpec((tm, tn), lambda j, i: (i, j))
    return _mm_call(grid=grid, in_specs=[pl.BlockSpec((tm, k), lambda j, i: (i, 0)), w_spec], out_spec=o_spec,
                    out_shape=jax.ShapeDtypeStruct((m, n), out_dtype), dims=NN_DIMS, nk=1, kaxis=0, acc_shape=None,
                    name=name, operands=(a, w), riders=riders)


def mm_nt(a, w, *, out_dtype, name, tm=512, tn=512, riders=()):
    if w.ndim == 2:
        m, n = a.shape
        kout = w.shape[0]
        tm = _row_tile(m, tm)
        return _mm_call(grid=(kout // tn, m // tm),
                        in_specs=[pl.BlockSpec((tm, n), lambda j, i: (i, 0)), pl.BlockSpec((tn, n), lambda j, i: (j, 0))],
                        out_spec=pl.BlockSpec((tm, tn), lambda j, i: (i, j)),
                        out_shape=jax.ShapeDtypeStruct((m, kout), out_dtype), dims=NT_DIMS, nk=1, kaxis=0,
                        acc_shape=None, name=name, operands=(a, w), riders=riders)
    _, kout, ns = w.shape
    planes = a.ndim == 3
    m = a.shape[1] if planes else a.shape[0]
    tm = _row_tile(m, tm)
    a_spec = pl.BlockSpec((2, tm, 2 * ns), lambda i: (0, i, 0)) if planes else pl.BlockSpec((tm, N_CHIPS * ns), lambda i: (i, 0))

    def body(a_ref, w0, w1, w2, w3, o_ref):
        acc = None
        for j, w_ref in enumerate((w0, w1, w2, w3)):
            if planes:
                a_j = a_ref[j // 2, :, (j % 2) * ns:(j % 2 + 1) * ns]
            else:
                a_j = a_ref[:, j * ns:(j + 1) * ns]
            p = lax.dot_general(a_j, w_ref[...], NT_DIMS, preferred_element_type=F32)
            acc = p if acc is None else acc + p
        o_ref[...] = acc.astype(out_dtype)

    def shard(j):
        return pl.BlockSpec((None, kout, ns), lambda i: (j, 0, 0))

    (out,), rider_res = _call(
        body, grid=(m // tm,), in_specs=[a_spec] + [shard(j) for j in range(N_CHIPS)],
        out_specs=[pl.BlockSpec((tm, kout), lambda i: (i, 0))], out_shape=[jax.ShapeDtypeStruct((m, kout), out_dtype)],
        operands=(a, w, w, w, w), sem=("parallel",), name=name, riders=riders)
    return (out, rider_res) if riders else out


def mm_tn(a, b, *, shard_major, name, tm, tn, tk=None, out_dtype=BF16, riders=()):
    s, m = a.shape
    tk = s if tk is None else _row_tile(s, tk)
    if b.ndim == 3:
        n = 2 * b.shape[2]
        b_spec = pl.BlockSpec((None, tk, tn), lambda j, i, kk: (j // 2, kk, j % 2))
    else:
        n = b.shape[1]
        b_spec = pl.BlockSpec((tk, tn), lambda j, i, kk: (kk, j))
    if shard_major:
        assert tn == n // N_CHIPS
        o_spec = pl.BlockSpec((None, tm, tn), lambda j, i, kk: (j, i, 0))
        o_shape = jax.ShapeDtypeStruct((N_CHIPS, m, tn), out_dtype)
    else:
        o_spec = pl.BlockSpec((tm, tn), lambda j, i, kk: (i, j))
        o_shape = jax.ShapeDtypeStruct((m, n), out_dtype)
    return _mm_call(grid=(n // tn, m // tm, s // tk),
                    in_specs=[pl.BlockSpec((tk, tm), lambda j, i, kk: (kk, i)), b_spec], out_spec=o_spec,
                    out_shape=o_shape, dims=TN_DIMS, nk=s // tk, kaxis=2, acc_shape=(tm, tn), name=name, operands=(a, b),
                    riders=riders)


def _rstd(x):
    return lax.rsqrt(jnp.mean(x * x, axis=-1, keepdims=True) + EPS)


def _rms_bwd(dy, x, g):
    r = _rstd(x)
    xhat = x * r
    gy = dy * g
    dx = r * (gy - xhat * jnp.mean(gy * xhat, axis=-1, keepdims=True))
    return dx, jnp.sum(dy * xhat, axis=0, keepdims=True)


def _accum(ref, val, first):
    @pl.when(first)
    def _():
        ref[...] = val

    @pl.when(jnp.logical_not(first))
    def _():
        ref[...] += val


def _row_spec(tm, width):
    return pl.BlockSpec((tm, width), lambda i: (i, 0))


def _vec_spec(width):
    return pl.BlockSpec((1, width), lambda i: (0, 0))


def _ret(core, rider_res, riders):
    core = core[0] if len(core) == 1 else core
    return (core, rider_res) if riders else core


def prenorm(x, g, *, name, tm=256, riders=()):
    s = x.shape[0]
    tm = _row_tile(s, tm)

    def body(x_ref, g_ref, h_ref):
        xv = x_ref[...]
        h_ref[...] = (xv * _rstd(xv) * g_ref[...]).astype(BF16)

    core, rr = _call(
        body, grid=(s // tm,), in_specs=[_row_spec(tm, D_MODEL), _vec_spec(D_MODEL)], out_specs=[_row_spec(tm, D_MODEL)],
        out_shape=[jax.ShapeDtypeStruct((s, D_MODEL), BF16)], operands=(x, g), sem=("parallel",), name=name, riders=riders)
    return _ret(core, rr, riders)


def residual_norm(x, m, bias, g_post, g_next, *, name, tm=256, riders=()):
    s = x.shape[0]
    tm = _row_tile(s, tm)

    def body(x_ref, m_ref, b_ref, gp_ref, gn_ref, xo_ref, h_ref):
        mv = m_ref[...] + b_ref[...]
        xn = x_ref[...] + mv * _rstd(mv) * gp_ref[...]
        xo_ref[...] = xn
        h_ref[...] = (xn * _rstd(xn) * gn_ref[...]).astype(BF16)

    core, rr = _call(
        body, grid=(s // tm,),
        in_specs=[_row_spec(tm, D_MODEL), _row_spec(tm, D_MODEL), _vec_spec(D_MODEL), _vec_spec(D_MODEL), _vec_spec(D_MODEL)],
        out_specs=[_row_spec(tm, D_MODEL), _row_spec(tm, D_MODEL)],
        out_shape=[jax.ShapeDtypeStruct((s, D_MODEL), F32), jax.ShapeDtypeStruct((s, D_MODEL), BF16)],
        operands=(x, m, bias, g_post, g_next), sem=("parallel",), name=name, riders=riders)
    return _ret(core, rr, riders)


def loss_head(x, f, g_post, target, *, name, tm=256, riders=()):
    s = x.shape[0]
    tm = _row_tile(s, tm)

    def body(x_ref, f_ref, g_ref, t_ref, dx_ref, df_ref, dg_ref, loss_ref):
        first = pl.program_id(0) == 0
        fv = f_ref[...]
        g = g_ref[...]
        err = x_ref[...] + fv * _rstd(fv) * g - t_ref[...]
        dx = err * (1.0 / D_MODEL)
        dx_ref[...] = dx
        df, dg = _rms_bwd(dx, fv, g)
        df_ref[...] = df.astype(BF16)
        _accum(dg_ref, dg, first)
        part = jnp.sum(jnp.sum(err * err, axis=-1, keepdims=True), axis=0, keepdims=True) * (0.5 / D_MODEL)
        _accum(loss_ref, jnp.broadcast_to(part, (8, LANES)), first)

    core, rr = _call(
        body, grid=(s // tm,),
        in_specs=[_row_spec(tm, D_MODEL), _row_spec(tm, D_MODEL), _vec_spec(D_MODEL), _row_spec(tm, D_MODEL)],
        out_specs=[_row_spec(tm, D_MODEL), _row_spec(tm, D_MODEL), _vec_spec(D_MODEL), pl.BlockSpec((8, LANES), lambda i: (0, 0))],
        out_shape=[jax.ShapeDtypeStruct((s, D_MODEL), F32), jax.ShapeDtypeStruct((s, D_MODEL), BF16),
                   jax.ShapeDtypeStruct((1, D_MODEL), F32), jax.ShapeDtypeStruct((8, LANES), F32)],
        operands=(x, f, g_post, target), name=name, riders=riders)
    return _ret(core, rr, riders)


def norm_bwd_pair(dres, dh, x, g_pre, m, bias, g_post, *, name, tm=256, riders=()):
    s = x.shape[0]
    tm = _row_tile(s, tm)

    def body(dres_ref, dh_ref, x_ref, gpre_ref, m_ref, b_ref, gpost_ref, dx_ref, dm_ref, dgpre_ref, dgpost_ref, db_ref):
        first = pl.program_id(0) == 0
        d1, dgpre = _rms_bwd(dh_ref[...], x_ref[...], gpre_ref[...])
        dx = dres_ref[...] + d1
        dx_ref[...] = dx
        dm, dgpost = _rms_bwd(dx, m_ref[...] + b_ref[...], gpost_ref[...])
        dm_ref[...] = dm.astype(BF16)
        _accum(dgpre_ref, dgpre, first)
        _accum(dgpost_ref, dgpost, first)
        _accum(db_ref, jnp.sum(dm, axis=0, keepdims=True), first)

    row, vec = _row_spec(tm, D_MODEL), _vec_spec(D_MODEL)
    vshape = jax.ShapeDtypeStruct((1, D_MODEL), F32)
    core, rr = _call(
        body, grid=(s // tm,), in_specs=[row, row, row, vec, row, vec, vec], out_specs=[row, row, vec, vec, vec],
        out_shape=[jax.ShapeDtypeStruct((s, D_MODEL), F32), jax.ShapeDtypeStruct((s, D_MODEL), BF16), vshape, vshape, vshape],
        operands=(dres, dh, x, g_pre, m, bias, g_post), name=name, riders=riders)
    return _ret(core, rr, riders)


def norm_bwd_last(dres, dh, x, g_pre, *, name, tm=256, riders=()):
    s = x.shape[0]
    tm = _row_tile(s, tm)

    def body(dres_ref, dh_ref, x_ref, g_ref, dx_ref, dg_ref):
        d1, dg = _rms_bwd(dh_ref[...], x_ref[...], g_ref[...])
        dx_ref[...] = dres_ref[...] + d1
        _accum(dg_ref, dg, pl.program_id(0) == 0)

    row, vec = _row_spec(tm, D_MODEL), _vec_spec(D_MODEL)
    core, rr = _call(
        body, grid=(s // tm,), in_specs=[row, row, row, vec], out_specs=[row, vec],
        out_shape=[jax.ShapeDtypeStruct((s, D_MODEL), F32), jax.ShapeDtypeStruct((1, D_MODEL), F32)],
        operands=(dres, dh, x, g_pre), name=name, riders=riders)
    return _ret(core, rr, riders)


def _rope_tables(s):
    half = HEAD_DIM // 2
    inv_freq = ROPE_THETA ** (-(jnp.arange(half, dtype=F32) * 2.0) / HEAD_DIM)
    ang = jnp.arange(s, dtype=I32).astype(F32)[:, None] * inv_freq[None, :]
    cos, sin = jnp.cos(ang), jnp.sin(ang)
    return jnp.tile(cos, (1, 4)), jnp.concatenate([-sin, sin, -sin, sin], axis=1)


def _swap_halves(x):
    lane = lax.broadcasted_iota(I32, x.shape, 1)
    return jnp.where((lane & (HEAD_DIM - 1)) < HEAD_DIM // 2, pltpu.roll(x, LANES - 32, 1), pltpu.roll(x, 32, 1))


N_ROPE_BLOCKS = (Q_WIDTH + KV_WIDTH) // LANES


def rope_fwd(qkv, bias, cos, sin, *, name, tm=256, riders=()):
    s = qkv.shape[0]
    tm = _row_tile(s, tm)

    def body(x_ref, b_ref, c_ref, s_ref, q_ref, k_ref, v_ref):
        cosv, sinv = c_ref[...], s_ref[...]
        for blk in range(QKV_WIDTH // LANES):
            cols = slice(blk * LANES, (blk + 1) * LANES)
            xb = x_ref[:, cols] + b_ref[:, cols]
            if blk < N_ROPE_BLOCKS:
                xb = xb * cosv + _swap_halves(xb) * sinv
            for e in range(2):
                head = 2 * blk + e
                piece = xb[:, e * HEAD_DIM:(e + 1) * HEAD_DIM].astype(BF16)
                if head < N_Q_HEADS:
                    q_ref[head] = piece
                elif head < N_Q_HEADS + N_KV_HEADS:
                    k_ref[head - N_Q_HEADS] = piece
                else:
                    v_ref[head - N_Q_HEADS - N_KV_HEADS] = piece

    def hm(nh):
        return pl.BlockSpec((nh, tm, HEAD_DIM), lambda i: (0, i, 0))

    core, rr = _call(
        body, grid=(s // tm,),
        in_specs=[_row_spec(tm, QKV_WIDTH), _vec_spec(QKV_WIDTH), _row_spec(tm, LANES), _row_spec(tm, LANES)],
        out_specs=[hm(N_Q_HEADS), hm(N_KV_HEADS), hm(N_KV_HEADS)],
        out_shape=[jax.ShapeDtypeStruct((N_Q_HEADS, s, HEAD_DIM), BF16), jax.ShapeDtypeStruct((N_KV_HEADS, s, HEAD_DIM), BF16),
                   jax.ShapeDtypeStruct((N_KV_HEADS, s, HEAD_DIM), BF16)],
        operands=(qkv, bias, cos, sin), sem=("parallel",), name=name, riders=riders)
    return _ret(core, rr, riders)


def rope_bwd(dq, dkc, dkp, dvc, dvp, cos, sin, *, name, riders=()):
    s = dq.shape[1]
    tm = WINDOW
    nb = s // tm

    def body(dq_ref, dkc_ref, dkp_ref, dvc_ref, dvp_ref, c_ref, s_ref, o_ref, db_ref, buf):
        i = pl.program_id(0)
        has_next = (i < nb - 1).astype(F32)
        for head in range(N_Q_HEADS):
            buf[:, head * HEAD_DIM:(head + 1) * HEAD_DIM] = dq_ref[head].astype(F32)
        for head in range(N_KV_HEADS):
            kcols = slice(Q_WIDTH + head * HEAD_DIM, Q_WIDTH + (head + 1) * HEAD_DIM)
            vcols = slice(Q_WIDTH + KV_WIDTH + head * HEAD_DIM, Q_WIDTH + KV_WIDTH + (head + 1) * HEAD_DIM)
            buf[:, kcols] = dkc_ref[head].astype(F32) + has_next * dkp_ref[head].astype(F32)
            buf[:, vcols] = dvc_ref[head].astype(F32) + has_next * dvp_ref[head].astype(F32)
        cosv, sinv = c_ref[...], s_ref[...]
        for blk in range(QKV_WIDTH // LANES):
            cols = slice(blk * LANES, (blk + 1) * LANES)
            g = buf[:, cols]
            if blk < N_ROPE_BLOCKS:
                g = g * cosv + _swap_halves(g * sinv)
            o_ref[:, cols] = g.astype(BF16)
            buf[:, cols] = g
        _accum(db_ref, jnp.sum(buf[...], axis=0, keepdims=True), i == 0)

    def hm(nh, shift):
        if shift:
            return pl.BlockSpec((nh, tm, HEAD_DIM), lambda i: (0, jnp.minimum(i + 1, nb - 1), 0))
        return pl.BlockSpec((nh, tm, HEAD_DIM), lambda i: (0, i, 0))

    core, rr = _call(
        body, grid=(nb,),
        in_specs=[hm(N_Q_HEADS, False), hm(N_KV_HEADS, False), hm(N_KV_HEADS, True), hm(N_KV_HEADS, False), hm(N_KV_HEADS, True),
                  _row_spec(tm, LANES), _row_spec(tm, LANES)],
        out_specs=[_row_spec(tm, QKV_WIDTH), _vec_spec(QKV_WIDTH)],
        out_shape=[jax.ShapeDtypeStruct((s, QKV_WIDTH), BF16), jax.ShapeDtypeStruct((1, QKV_WIDTH), F32)],
        scratch_shapes=[pltpu.VMEM((tm, QKV_WIDTH), F32)], operands=(dq, dkc, dkp, dvc, dvp, cos, sin), name=name, riders=riders)
    return _ret(core, rr, riders)


def heads_to_rows(o, *, name, tm=256, riders=()):
    s = o.shape[1]
    tm = _row_tile(s, tm)

    def body(o_ref, r_ref):
        for head in range(N_Q_HEADS):
            r_ref[:, head * HEAD_DIM:(head + 1) * HEAD_DIM] = o_ref[head]

    core, rr = _call(
        body, grid=(s // tm,), in_specs=[pl.BlockSpec((N_Q_HEADS, tm, HEAD_DIM), lambda i: (0, i, 0))],
        out_specs=[_row_spec(tm, Q_WIDTH)], out_shape=[jax.ShapeDtypeStruct((s, Q_WIDTH), BF16)], operands=(o,),
        sem=("parallel",), name=name, riders=riders)
    return _ret(core, rr, riders)


def rows_to_heads(x, *, name, tm=256, riders=()):
    s = x.shape[0]
    tm = _row_tile(s, tm)

    def body(x_ref, o_ref):
        xv = x_ref[...].astype(F32)
        for head in range(N_Q_HEADS):
            o_ref[head] = xv[:, head * HEAD_DIM:(head + 1) * HEAD_DIM].astype(BF16)

    core, rr = _call(
        body, grid=(s // tm,), in_specs=[_row_spec(tm, Q_WIDTH)],
        out_specs=[pl.BlockSpec((N_Q_HEADS, tm, HEAD_DIM), lambda i: (0, i, 0))],
        out_shape=[jax.ShapeDtypeStruct((N_Q_HEADS, s, HEAD_DIM), BF16)], operands=(x,), sem=("parallel",), name=name,
        riders=riders)
    return _ret(core, rr, riders)


ROWS = GQA_GROUP * WINDOW


def _prev_slots():
    qpos = lax.broadcasted_iota(I32, (ROWS, WINDOW), 0) & (WINDOW - 1)
    kpos = lax.broadcasted_iota(I32, (ROWS, WINDOW), 1)
    return kpos > qpos


def _attn_probs(q, kc, kp, sink, prev, has_prev):
    scale = HEAD_DIM ** -0.5
    sc = lax.dot_general(q, kc, NT_DIMS, preferred_element_type=F32)
    sp = lax.dot_general(q, kp, NT_DIMS, preferred_element_type=F32)
    sc_or_prev = jnp.where(prev, jnp.where(has_prev, sp, NEG), sc) * scale
    m = jnp.maximum(jnp.max(sc_or_prev, axis=-1, keepdims=True), sink)
    e, es = jnp.exp(sc_or_prev - m), jnp.exp(sink - m)
    inv = 1.0 / (jnp.sum(e, axis=-1, keepdims=True) + es)
    return e * inv, es * inv


def _attn_specs():
    q_spec = pl.BlockSpec((N_KV_HEADS, GQA_GROUP, WINDOW, HEAD_DIM), lambda n: (0, 0, n, 0))
    cur = pl.BlockSpec((N_KV_HEADS, WINDOW, HEAD_DIM), lambda n: (0, n, 0))
    prev = pl.BlockSpec((N_KV_HEADS, WINDOW, HEAD_DIM), lambda n: (0, jnp.maximum(n - 1, 0), 0))
    sink = pl.BlockSpec((N_KV_HEADS, ROWS, LANES), lambda n: (0, 0, 0))
    return q_spec, cur, prev, sink


def attn_fwd(q, k, v, sink_rows, *, name, riders=()):
    s = k.shape[1]

    def body(q_ref, kc_ref, kp_ref, vc_ref, vp_ref, sink_ref, o_ref):
        prev = _prev_slots()
        has_prev = pl.program_id(0) > 0
        for h in range(N_KV_HEADS):
            qv = q_ref[h].reshape(ROWS, HEAD_DIM)
            p, _ = _attn_probs(qv, kc_ref[h], kp_ref[h], sink_ref[h], prev, has_prev)
            o = jnp.dot(jnp.where(prev, 0.0, p).astype(BF16), vc_ref[h], preferred_element_type=F32)
            o += jnp.dot(jnp.where(prev, p, 0.0).astype(BF16), vp_ref[h], preferred_element_type=F32)
            o_ref[h] = o.reshape(GQA_GROUP, WINDOW, HEAD_DIM).astype(BF16)

    q_spec, cur, prev_spec, sink = _attn_specs()
    core, rr = _call(
        body, grid=(s // WINDOW,), in_specs=[q_spec, cur, prev_spec, cur, prev_spec, sink], out_specs=[q_spec],
        out_shape=[jax.ShapeDtypeStruct(q.shape, BF16)], operands=(q, k, k, v, v, sink_rows), sem=("parallel",),
        name=name, riders=riders)
    return _ret(core, rr, riders)


def attn_bwd(q, k, v, sink_rows, do, *, name, riders=()):
    s = k.shape[1]

    def body(q_ref, kc_ref, kp_ref, vc_ref, vp_ref, sink_ref, do_ref, dq_ref, dkc_ref, dkp_ref, dvc_ref, dvp_ref, dsink_ref):
        n = pl.program_id(0)
        prev = _prev_slots()
        scale = HEAD_DIM ** -0.5
        for h in range(N_KV_HEADS):
            qv = q_ref[h].reshape(ROWS, HEAD_DIM)
            dov = do_ref[h].reshape(ROWS, HEAD_DIM)
            kc, kp, vc, vp = kc_ref[h], kp_ref[h], vc_ref[h], vp_ref[h]
            p, ps = _attn_probs(qv, kc, kp, sink_ref[h], prev, n > 0)
            dpc = lax.dot_general(dov, vc, NT_DIMS, preferred_element_type=F32)
            dpp = lax.dot_general(dov, vp, NT_DIMS, preferred_element_type=F32)
            dp = jnp.where(prev, dpp, dpc)
            delta = jnp.sum(p * dp, axis=-1, keepdims=True)
            ds = p * (dp - delta) * scale
            dsc = jnp.where(prev, 0.0, ds).astype(BF16)
            dsp = jnp.where(prev, ds, 0.0).astype(BF16)
            pc = jnp.where(prev, 0.0, p).astype(BF16)
            pp = jnp.where(prev, p, 0.0).astype(BF16)
            dq = jnp.dot(dsc, kc, preferred_element_type=F32) + jnp.dot(dsp, kp, preferred_element_type=F32)
            dq_ref[h] = dq.reshape(GQA_GROUP, WINDOW, HEAD_DIM).astype(BF16)
            dkc_ref[h] = lax.dot_general(dsc, qv, TN_DIMS, preferred_element_type=F32).astype(BF16)
            dkp_ref[h] = lax.dot_general(dsp, qv, TN_DIMS, preferred_element_type=F32).astype(BF16)
            dvc_ref[h] = lax.dot_general(pc, dov, TN_DIMS, preferred_element_type=F32).astype(BF16)
            dvp_ref[h] = lax.dot_general(pp, dov, TN_DIMS, preferred_element_type=F32).astype(BF16)
            dsink = -(ps * delta)
            for g in range(GQA_GROUP):
                part = jnp.broadcast_to(jnp.sum(dsink[g * WINDOW:(g + 1) * WINDOW], axis=0, keepdims=True), (8, LANES))

                @pl.when(n == 0)
                def _():
                    dsink_ref[h, g] = part

                @pl.when(n > 0)
                def _():
                    dsink_ref[h, g] += part

    q_spec, cur, prev_spec, sink = _attn_specs()
    kv_shape = jax.ShapeDtypeStruct(k.shape, BF16)
    core, rr = _call(
        body, grid=(s // WINDOW,), in_specs=[q_spec, cur, prev_spec, cur, prev_spec, sink, q_spec],
        out_specs=[q_spec, cur, cur, cur, cur, pl.BlockSpec((N_KV_HEADS, GQA_GROUP, 8, LANES), lambda n: (0, 0, 0, 0))],
        out_shape=[jax.ShapeDtypeStruct(q.shape, BF16), kv_shape, kv_shape, kv_shape, kv_shape,
                   jax.ShapeDtypeStruct((N_KV_HEADS, GQA_GROUP, 8, LANES), F32)],
        operands=(q, k, k, v, v, sink_rows, do), sem=("arbitrary",), name=name, riders=riders)
    return _ret(core, rr, riders)


GELU_C = 0.7978845608028654
GELU_A = 0.044715


def _gelu(x):
    return 0.5 * x * (1.0 + jnp.tanh(GELU_C * (x + GELU_A * x * x * x)))


def _gelu_grad(x):
    t = jnp.tanh(GELU_C * (x + GELU_A * x * x * x))
    return 0.5 * (1.0 + t) + 0.5 * x * (1.0 - t * t) * GELU_C * (1.0 + 3.0 * GELU_A * x * x)


def _tril_bf16(w):
    row = lax.broadcasted_iota(I32, (SGU_CHUNK, SGU_CHUNK), 0)
    col = lax.broadcasted_iota(I32, (SGU_CHUNK, SGU_CHUNK), 1)
    return jnp.where(row >= col, w, 0.0).astype(BF16)


def _sgu_norm(vg, g, b):
    mu = jnp.mean(vg, axis=-1, keepdims=True)
    cen = vg - mu
    rstd = lax.rsqrt(jnp.mean(cen * cen, axis=-1, keepdims=True) + EPS)
    xhat = cen * rstd
    return xhat, rstd, xhat * g + b


def sgu_fwd(z, ln_g, ln_b, w_sp, b_sp, *, name, tm=256, riders=()):
    s = z.shape[0]
    tm = _row_tile(s, tm)

    def body(z_ref, g_ref, b_ref, w_ref, bs_ref, y_ref):
        u = _gelu(z_ref[:, :D_MODEL])
        _, _, vn = _sgu_norm(_gelu(z_ref[:, D_MODEL:]), g_ref[...], b_ref[...])
        vn = vn.astype(BF16)
        for grp in range(SGU_GROUPS):
            w = _tril_bf16(w_ref[grp])
            cols = slice(grp * LANES, (grp + 1) * LANES)
            for ch in range(tm // SGU_CHUNK):
                rows = slice(ch * SGU_CHUNK, (ch + 1) * SGU_CHUNK)
                mixed = jnp.dot(w, vn[rows, cols], preferred_element_type=F32) + bs_ref[grp]
                y_ref[rows, cols] = (u[rows, cols] * mixed).astype(BF16)

    full3 = pl.BlockSpec((SGU_GROUPS, SGU_CHUNK, SGU_CHUNK), lambda i: (0, 0, 0))
    core, rr = _call(
        body, grid=(s // tm,), in_specs=[_row_spec(tm, 2 * D_MODEL), _vec_spec(D_MODEL), _vec_spec(D_MODEL), full3, full3],
        out_specs=[_row_spec(tm, D_MODEL)], out_shape=[jax.ShapeDtypeStruct((s, D_MODEL), BF16)],
        operands=(z, ln_g, ln_b, w_sp, b_sp), sem=("parallel",), name=name, riders=riders)
    return _ret(core, rr, riders)


def sgu_bwd(z, dy, ln_g, ln_b, w_sp, b_sp, *, name, tm=256, riders=()):
    s = z.shape[0]
    tm = _row_tile(s, tm)

    def body(z_ref, dy_ref, g_ref, b_ref, w_ref, bs_ref, dz_ref, dw_ref, dbs_ref, dg_ref, db_ref, dvn_buf):
        first = pl.program_id(0) == 0
        zu, zv = z_ref[:, :D_MODEL], z_ref[:, D_MODEL:]
        u = _gelu(zu)
        xhat, rstd, vn = _sgu_norm(_gelu(zv), g_ref[...], b_ref[...])
        vn = vn.astype(BF16)
        dyv = dy_ref[...]
        dmixed = dyv * u
        row = lax.broadcasted_iota(I32, (SGU_CHUNK, SGU_CHUNK), 0)
        col = lax.broadcasted_iota(I32, (SGU_CHUNK, SGU_CHUNK), 1)
        for grp in range(SGU_GROUPS):
            w = _tril_bf16(w_ref[grp])
            cols = slice(grp * LANES, (grp + 1) * LANES)
            dw = jnp.zeros((SGU_CHUNK, SGU_CHUNK), F32)
            dbs = jnp.zeros((SGU_CHUNK, 1), F32)
            for ch in range(tm // SGU_CHUNK):
                rows = slice(ch * SGU_CHUNK, (ch + 1) * SGU_CHUNK)
                vblk = vn[rows, cols]
                mixed = jnp.dot(w, vblk, preferred_element_type=F32) + bs_ref[grp]
                dz_ref[rows, cols] = (dyv[rows, cols] * mixed * _gelu_grad(zu[rows, cols])).astype(BF16)
                dm = dmixed[rows, cols]
                dmb = dm.astype(BF16)
                dvn_buf[rows, cols] = lax.dot_general(w, dmb, TN_DIMS, preferred_element_type=F32)
                dw += lax.dot_general(dmb, vblk, NT_DIMS, preferred_element_type=F32)
                dbs += jnp.sum(dm, axis=-1, keepdims=True)
            dw = jnp.where(row >= col, dw, 0.0)
            dbs = jnp.broadcast_to(dbs, (SGU_CHUNK, SGU_CHUNK))

            @pl.when(first)
            def _():
                dw_ref[grp] = dw
                dbs_ref[grp] = dbs

            @pl.when(jnp.logical_not(first))
            def _():
                dw_ref[grp] += dw
                dbs_ref[grp] += dbs

        dvn = dvn_buf[...]
        dxhat = dvn * g_ref[...]
        dvg = rstd * (dxhat - jnp.mean(dxhat, axis=-1, keepdims=True) - xhat * jnp.mean(dxhat * xhat, axis=-1, keepdims=True))
        dz_ref[:, D_MODEL:] = (dvg * _gelu_grad(zv)).astype(BF16)
        _accum(dg_ref, jnp.sum(dvn * xhat, axis=0, keepdims=True), first)
        _accum(db_ref, jnp.sum(dvn, axis=0, keepdims=True), first)

    full3 = pl.BlockSpec((SGU_GROUPS, SGU_CHUNK, SGU_CHUNK), lambda i: (0, 0, 0))
    s3 = jax.ShapeDtypeStruct((SGU_GROUPS, SGU_CHUNK, SGU_CHUNK), F32)
    vshape = jax.ShapeDtypeStruct((1, D_MODEL), F32)
    core, rr = _call(
        body, grid=(s // tm,),
        in_specs=[_row_spec(tm, 2 * D_MODEL), _row_spec(tm, D_MODEL), _vec_spec(D_MODEL), _vec_spec(D_MODEL), full3, full3],
        out_specs=[_row_spec(tm, 2 * D_MODEL), full3, full3, _vec_spec(D_MODEL), _vec_spec(D_MODEL)],
        out_shape=[jax.ShapeDtypeStruct((s, 2 * D_MODEL), BF16), s3, s3, vshape, vshape],
        scratch_shapes=[pltpu.VMEM((tm, D_MODEL), F32)], operands=(z, dy, ln_g, ln_b, w_sp, b_sp), name=name, riders=riders)
    return _ret(core, rr, riders)


def _sigmoid(x):
    return 1.0 / (1.0 + jnp.exp(-x))


def ffn_up(h, w_gu, *, name, tm=512, riders=()):
    s = h.shape[0]
    tm = _row_tile(s, tm)

    def body(h_ref, wg_ref, wu_ref, gu_ref, a_ref):
        hv = h_ref[...]
        g = jnp.dot(hv, wg_ref[...], preferred_element_type=F32)
        u = jnp.dot(hv, wu_ref[...], preferred_element_type=F32)
        gu_ref[0] = g.astype(BF16)
        gu_ref[1] = u.astype(BF16)
        a_ref[...] = (g * _sigmoid(g) * u).astype(BF16)

    core, rr = _call(
        body, grid=(2, s // tm),
        in_specs=[pl.BlockSpec((tm, D_MODEL), lambda j, i: (i, 0)),
                  pl.BlockSpec((None, D_MODEL, FF_HALF), lambda j, i: (j, 0, 0)),
                  pl.BlockSpec((None, D_MODEL, FF_HALF), lambda j, i: (j + 2, 0, 0))],
        out_specs=[pl.BlockSpec((2, tm, FF_HALF), lambda j, i: (0, i, j)), pl.BlockSpec((tm, FF_HALF), lambda j, i: (i, j))],
        out_shape=[jax.ShapeDtypeStruct((2, s, D_FF), BF16), jax.ShapeDtypeStruct((s, D_FF), BF16)],
        operands=(h, w_gu, w_gu), sem=("parallel", "parallel"), name=name, riders=riders)
    return _ret(core, rr, riders)


def ffn_dact(df, w_d, gu, *, name, tm=512, riders=()):
    s = df.shape[0]
    tm = _row_tile(s, tm)

    def body(df_ref, w_ref, gu_ref, o_ref):
        da = lax.dot_general(df_ref[...], w_ref[...], NT_DIMS, preferred_element_type=F32)
        g = gu_ref[0].astype(F32)
        u = gu_ref[1].astype(F32)
        sig = _sigmoid(g)
        o_ref[0] = (da * u * sig * (1.0 + g * (1.0 - sig))).astype(BF16)
        o_ref[1] = (da * g * sig).astype(BF16)

    planes = pl.BlockSpec((2, tm, FF_HALF), lambda j, i: (0, i, j))
    core, rr = _call(
        body, grid=(2, s // tm),
        in_specs=[pl.BlockSpec((tm, D_MODEL), lambda j, i: (i, 0)), pl.BlockSpec((FF_HALF, D_MODEL), lambda j, i: (j, 0)), planes],
        out_specs=[planes], out_shape=[jax.ShapeDtypeStruct((2, s, D_FF), BF16)], operands=(df, w_d, gu),
        sem=("parallel", "parallel"), name=name, riders=riders)
    return _ret(core, rr, riders)


def _weight_tile(rows):
    for tr in (512, 352, 256, 128):
        if rows % tr == 0:
            return tr
    return rows


def place_shard(w, layer, chip_arr, dtype, *, name):
    _, r, c = w.shape
    tr = _weight_tile(r)

    def body(chip_ref, w_ref, o_ref):
        o_ref[...] = w_ref[...].astype(dtype)

    (out,), _ = _call(
        body, grid=(r // tr,), prefetch=(chip_arr,),
        in_specs=[pl.BlockSpec((None, tr, c), lambda i, chip: (layer, i, 0))],
        out_specs=[pl.BlockSpec((None, tr, c), lambda i, chip: (chip[0], i, 0))],
        out_shape=[jax.ShapeDtypeStruct((N_CHIPS, r, c), dtype)], operands=(w,), sem=("parallel",), name=name)
    return out


def _adamw_math(w, g, m, v):
    m = ADAM_B1 * m + (1.0 - ADAM_B1) * g
    v = ADAM_B2 * v + (1.0 - ADAM_B2) * (g * g)
    m_hat = m / (1.0 - ADAM_B1 ** ADAM_STEP)
    v_hat = v / (1.0 - ADAM_B2 ** ADAM_STEP)
    delta = -ADAM_LR * (m_hat / (jnp.sqrt(v_hat) + ADAM_EPS) + ADAM_WD * w)
    return delta, m, v


def adamw(w, g, m, v, *, name):
    nl, r, c = w.shape
    tr = _weight_tile(r)

    def body(w_ref, g_ref, m_ref, v_ref, go_ref, d_ref, mo_ref, vo_ref):
        gv = g_ref[...]
        go_ref[...] = gv
        d_ref[...], mo_ref[...], vo_ref[...] = _adamw_math(w_ref[...], gv, m_ref[...], v_ref[...])

    spec = pl.BlockSpec((None, tr, c), lambda l, i: (l, i, 0))
    shape = jax.ShapeDtypeStruct(w.shape, F32)
    outs, _ = _call(body, grid=(nl, r // tr), in_specs=[spec] * 4, out_specs=[spec] * 4, out_shape=[shape] * 4,
                    operands=(w, g, m, v), sem=("parallel", "parallel"), name=name)
    return outs


def adamw_small(ws, gs, ms, vs, *, name):
    n = len(ws)

    def body(*refs):
        ins, outs = refs[:4 * n], refs[4 * n:]
        for t in range(n):
            gv = ins[n + t][...]
            outs[t][...] = gv
            outs[n + t][...], outs[2 * n + t][...], outs[3 * n + t][...] = _adamw_math(
                ins[t][...], gv, ins[2 * n + t][...], ins[3 * n + t][...])

    shapes = [jax.ShapeDtypeStruct(w.shape, F32) for w in ws]
    res = pl.pallas_call(body, out_shape=shapes * 4, name=name)(*ws, *gs, *ms, *vs)
    return res[:n], res[n:2 * n], res[2 * n:3 * n], res[3 * n:]


def pair_add(g, r1, c_arr, *, name):
    _, rows, cdim = g.shape
    h = rows // 2

    def body(c_ref, g_ref, r_ref, o_ref):
        o_ref[...] = (g_ref[...].astype(F32) + r_ref[...].astype(F32)).astype(o_ref.dtype)

    (out,), _ = _call(
        body, grid=(N_CHIPS,), prefetch=(c_arr,),
        in_specs=[pl.BlockSpec((None, h, cdim), lambda s, c: (s, c[0], 0)), pl.BlockSpec((None, h, cdim), lambda s, c: (s, 0, 0))],
        out_specs=[pl.BlockSpec((None, h, cdim), lambda s, c: (s, 0, 0))],
        out_shape=[jax.ShapeDtypeStruct((N_CHIPS, h, cdim), g.dtype)], operands=(g, r1), sem=("parallel",), name=name)
    return out


def final_add(g, r1, r2, jc_arr, *, dest_shape, lead, prev, name):
    _, rows, cdim = g.shape
    h = rows // 2

    def body(jc_ref, g_ref, r1_ref, r2_ref, *rest):
        o_ref = rest[-1]
        acc = g_ref[...].astype(F32) + r1_ref[...].astype(F32)
        for k in range(3):
            acc = acc + r2_ref[k].astype(F32)
        o_ref[...] = acc

    if lead is None:
        o_spec = pl.BlockSpec((h, cdim), lambda i, jc: (jc[1], 0))
    elif lead == "chip":
        o_spec = pl.BlockSpec((None, h, cdim), lambda i, jc: (jc[0], jc[1], 0))
    else:
        o_spec = pl.BlockSpec((None, h, cdim), lambda i, jc: (lead, jc[1], 0))
    in_specs = [pl.BlockSpec((None, h, cdim), lambda i, jc: (jc[0], jc[1], 0)),
                pl.BlockSpec((None, h, cdim), lambda i, jc: (jc[0], 0, 0)),
                pl.BlockSpec((3, h, cdim), lambda i, jc: (0, 0, 0))]
    operands = [g, r1, r2]
    aliases = None
    if prev is not None:
        in_specs.append(ANY)
        operands.append(prev)
        aliases = {3: 0}
    (out,), _ = _call(body, grid=(1,), prefetch=(jc_arr,), in_specs=in_specs, out_specs=[o_spec],
                      out_shape=[jax.ShapeDtypeStruct(dest_shape, F32)], operands=operands, aliases=aliases, name=name)
    return out


def _place():
    return lax.axis_index("x"), lax.axis_index("y"), lax.axis_index("c")


def _partner(x, y, k):
    return (1 - x if k >> 1 else x), (1 - y if k & 1 else y)


def _half(rows, sel, dtype):
    align = 16 if dtype == BF16 else 8
    return pl.ds(pl.multiple_of(sel * (rows // 2), align), rows // 2)


def _rider(inputs, aliased, fresh, nsem, copies, arrivals):
    def start(ins, outs, send, recv):
        for cp in copies(ins, outs, send, recv):
            cp.start()

    def finish(ins, outs, send, recv):
        for cp in arrivals(ins, outs, send, recv):
            cp.wait_recv()
        for cp in copies(ins, outs, send, recv):
            cp.wait_send()

    return types.SimpleNamespace(inputs=list(inputs), aliased=list(aliased), fresh=list(fresh), nsem=nsem, start=start,
                                 finish=finish)


def _remote(src, dst, send, recv, idx, dev):
    return pltpu.make_async_remote_copy(src_ref=src, dst_ref=dst, send_sem=send.at[idx], recv_sem=recv.at[idx],
                                        device_id=dev, device_id_type=MESH)


def gather_ici_rider(fulls):
    nt = len(fulls)

    def region(outs, t, slot, sel):
        return outs[t].at[slot, _half(fulls[t].shape[1], sel, fulls[t].dtype)]

    def copies(ins, outs, send, recv):
        x, y, c = _place()
        res = []
        for t in range(nt):
            for k in (1, 2, 3):
                px, py = _partner(x, y, k)
                mine = region(outs, t, 2 * x + y, c)
                res.append(_remote(mine, mine, send, recv, 3 * t + k - 1, (px, py, c)))
        return res

    def arrivals(ins, outs, send, recv):
        x, y, c = _place()
        res = []
        for t in range(nt):
            for k in (1, 2, 3):
                px, py = _partner(x, y, k)
                theirs = region(outs, t, 2 * px + py, c)
                res.append(_remote(theirs, theirs, send, recv, 3 * t + k - 1, (x, y, c)))
        return res

    return _rider(fulls, range(nt), [], 3 * nt, copies, arrivals)


def gather_d2d_rider(fulls):
    nt = len(fulls)

    def region(outs, t, slot, sel):
        return outs[t].at[slot, _half(fulls[t].shape[1], sel, fulls[t].dtype)]

    def both(outs, send, recv, mine):
        x, y, c = _place()
        res = []
        for t in range(nt):
            for k in (1, 2, 3):
                px, py = _partner(x, y, k)
                part = region(outs, t, 2 * px + py, c if mine else 1 - c)
                res.append(_remote(part, part, send, recv, 3 * t + k - 1, (x, y, 1 - c)))
        return res

    return _rider(fulls, range(nt), [], 3 * nt, lambda i, o, s, r: both(o, s, r, True), lambda i, o, s, r: both(o, s, r, False))


def exchange_rider(grads):
    nt = len(grads)

    def both(ins, outs, send, recv):
        x, y, c = _place()
        return [_remote(ins[t].at[:, _half(grads[t].shape[1], 1 - c, grads[t].dtype)], outs[t], send, recv, t, (x, y, 1 - c))
                for t in range(nt)]

    fresh = [jax.ShapeDtypeStruct((N_CHIPS, g.shape[1] // 2, g.shape[2]), g.dtype) for g in grads]
    return _rider(grads, [], fresh, nt, both, both)


def scatter_rider(parts):
    nt = len(parts)

    def both(ins, outs, send, recv):
        x, y, c = _place()
        res = []
        for t in range(nt):
            for k in (1, 2, 3):
                px, py = _partner(x, y, k)
                res.append(_remote(ins[t].at[2 * px + py], outs[t].at[k - 1], send, recv, 3 * t + k - 1, (px, py, c)))
        return res

    fresh = [jax.ShapeDtypeStruct((3,) + p.shape[1:], p.dtype) for p in parts]
    return _rider(parts, [], fresh, 3 * nt, both, both)


def broadcast_rider(bufs, items):
    def region(outs, item, sel):
        bi, lead = item
        ref = outs[bi]
        if lead == "chip":
            x, y, _ = _place()
            ref = ref.at[2 * x + y]
        elif lead is not None:
            ref = ref.at[lead]
        return ref.at[_half(ref.shape[0], sel, F32)]

    def both(outs, send, recv, mine):
        x, y, c = _place()
        res = []
        for i, item in enumerate(items):
            part = region(outs, item, c if mine else 1 - c)
            res.append(_remote(part, part, send, recv, i, (x, y, 1 - c)))
        return res

    return _rider(bufs, range(len(bufs)), [], len(items), lambda i, o, s, r: both(o, s, r, True),
                  lambda i, o, s, r: both(o, s, r, False))


def allcast_rider(buf):
    peers = [(k, flip) for k in range(N_CHIPS) for flip in (0, 1) if (k, flip) != (0, 0)]

    def both(outs, send, recv, mine):
        x, y, c = _place()
        res = []
        for i, (k, flip) in enumerate(peers):
            px, py = _partner(x, y, k)
            pc = 1 - c if flip else c
            slot, sel = (2 * x + y, c) if mine else (2 * px + py, pc)
            part = outs[0].at[slot, _half(buf.shape[1], sel, F32)]
            res.append(_remote(part, part, send, recv, i, (px, py, pc)))
        return res

    return _rider([buf], [0], [], len(peers), lambda i, o, s, r: both(o, s, r, True), lambda i, o, s, r: both(o, s, r, False))


def comm_call(riders, *, name):
    _, res = _call(None, riders=riders, name=name)
    return res


SLAB_ROWS = 192


def _pad_rows(a, rows=8):
    return jnp.pad(a, ((0, rows - a.shape[0]), (0, 0)))


def _pack_small(norm_grads, db_qkv, db_o, dsinks, db_sp, dln_g, dln_b, dw_sp):
    parts = [
        jnp.concatenate(norm_grads, axis=0),
        _pad_rows(jnp.pad(db_qkv, ((0, 0), (0, 2 * D_MODEL - QKV_WIDTH))).reshape(2, D_MODEL)),
        _pad_rows(db_o),
        _pad_rows(jnp.pad(dsinks.reshape(1, N_Q_HEADS), ((0, 0), (0, D_MODEL - N_Q_HEADS)))),
        _pad_rows(db_sp.reshape(1, D_MODEL)),
        _pad_rows(jnp.concatenate([dln_g, dln_b], axis=0)),
        dw_sp.reshape(SGU_CHUNK, D_MODEL),
    ]
    slab = jnp.concatenate(parts, axis=0)
    return jnp.pad(slab, ((0, SLAB_ROWS - slab.shape[0]), (0, 0))).reshape(N_CHIPS, SLAB_ROWS // N_CHIPS, D_MODEL)


def _unpack_small(slab, j):
    slab = slab.reshape(SLAB_ROWS, D_MODEL)
    norms = [slab[2 * i:2 * i + 2] for i in range(4)]
    db_qkv = slab[8:10].reshape(1, 2 * D_MODEL)[:, :QKV_WIDTH]
    db_o = slab[16:17]
    dsinks = slab[24:25, :N_Q_HEADS]
    db_sp = slab[32:33].reshape(SGU_GROUPS, SGU_CHUNK)
    width = D_MODEL // N_CHIPS
    dln_g = lax.dynamic_slice(slab[40:41], (0, j * width), (1, width))
    dln_b = lax.dynamic_slice(slab[41:42], (0, j * width), (1, width))
    dw_sp = slab[48:48 + SGU_CHUNK].reshape(SGU_GROUPS * SGU_CHUNK, SGU_CHUNK)
    return norms, db_qkv, db_o, dsinks, db_sp, dln_g, dln_b, dw_sp


class _GradReduce:
    def __init__(self, c_arr, jc_arr, dest_shapes):
        self.c_arr, self.jc_arr, self.dest_shapes = c_arr, jc_arr, dest_shapes
        self.grad, self.sibling, self.pair, self.chips, self.dest = {}, {}, {}, {}, {}

    def exchange(self, tags):
        return exchange_rider([self.grad[t] for t in tags])

    def exchanged(self, tags, res):
        for t, r in zip(tags, res):
            self.sibling[t] = r
            self.pair[t] = pair_add(self.grad[t], r, self.c_arr, name=f"pair_add_{t}")

    def scatter(self, tags):
        return scatter_rider([self.pair[t] for t in tags])

    def scattered(self, tags, res, where):
        for t, r in zip(tags, res):
            name, lead = where[t]
            self.dest[name] = final_add(self.grad[t], self.sibling[t], r, self.jc_arr, dest_shape=self.dest_shapes[name],
                                        lead=lead, prev=self.dest.get(name), name=f"final_add_{t}")

    def broadcast(self, items):
        names = []
        for n, _ in items:
            if n not in names:
                names.append(n)
        return names, broadcast_rider([self.dest[n] for n in names], [(names.index(n), lead) for n, lead in items])

    def broadcasted(self, names, res):
        for n, r in zip(names, res):
            self.dest[n] = r


def kernel(x, norm_mix_pre, norm_mix_post, norm_ffn_pre, norm_ffn_post, attn_w_qkv, attn_b_qkv, attn_sinks, attn_w_o, attn_b_o, sgu_w_in, sgu_ln_g, sgu_ln_b, sgu_w_spatial, sgu_b_spatial, sgu_w_out, ffn_w_gate_up, ffn_w_down, loss_target, m_norm_mix_pre, m_norm_mix_post, m_norm_ffn_pre, m_norm_ffn_post, m_attn_w_qkv, m_attn_b_qkv, m_attn_sinks, m_attn_w_o, m_attn_b_o, m_sgu_w_in, m_sgu_ln_g, m_sgu_ln_b, m_sgu_w_spatial, m_sgu_b_spatial, m_sgu_w_out, m_ffn_w_gate_up, m_ffn_w_down, v_norm_mix_pre, v_norm_mix_post, v_norm_ffn_pre, v_norm_ffn_post, v_attn_w_qkv, v_attn_b_qkv, v_attn_sinks, v_attn_w_o, v_attn_b_o, v_sgu_w_in, v_sgu_ln_g, v_sgu_ln_b, v_sgu_w_spatial, v_sgu_b_spatial, v_sgu_w_out, v_ffn_w_gate_up, v_ffn_w_down):
    s = x.shape[1]
    x0 = x.reshape(s, D_MODEL)
    target = loss_target.reshape(s, D_MODEL)
    mx, my, mc = lax.axis_index("x"), lax.axis_index("y"), lax.axis_index("c")
    chip = 2 * mx + my
    chip_arr = jnp.reshape(chip, (1,)).astype(I32)
    c_arr = jnp.reshape(mc, (1,)).astype(I32)
    jc_arr = jnp.stack([chip, mc]).astype(I32)
    zero_bias = jnp.zeros((1, D_MODEL), F32)

    def gain(p, i):
        return p[i:i + 1]

    big = [attn_w_qkv, attn_w_o, sgu_w_in, sgu_w_out, ffn_w_gate_up, ffn_w_gate_up, ffn_w_down, ffn_w_down]
    layers = [0, 0, 0, 0, 0, 1, 0, 1]
    tags = ["qkv", "wo", "win", "wout", "wgu0", "wgu1", "wd0", "wd1"]
    full = {t: place_shard(w, l, chip_arr, BF16, name=f"place_{t}") for w, l, t in zip(big, layers, tags)}
    ln_pack = _pad_rows(jnp.concatenate([sgu_ln_g, sgu_ln_b], axis=0), 16)[None]
    full["ln"] = place_shard(ln_pack, 0, chip_arr, F32, name="place_ln")

    def ici(*names):
        return gather_ici_rider([full[n] for n in names])

    def d2d(*names):
        return gather_d2d_rider([full[n] for n in names])

    def landed(names, res):
        for n, r in zip(names, res):
            full[n] = r

    cos, sin = _rope_tables(s)
    sink_rows = jnp.broadcast_to(
        jnp.repeat(attn_sinks.reshape(N_KV_HEADS, GQA_GROUP), WINDOW, axis=1)[:, :, None], (N_KV_HEADS, ROWS, LANES))
    w_sp = sgu_w_spatial.reshape(SGU_GROUPS, SGU_CHUNK, SGU_CHUNK)
    b_sp = jnp.broadcast_to(sgu_b_spatial.reshape(SGU_GROUPS, SGU_CHUNK)[:, :, None], (SGU_GROUPS, SGU_CHUNK, LANES))

    (res,) = comm_call([ici("qkv", "ln")], name="gather_first")
    landed(("qkv", "ln"), res)
    h0, (res,) = prenorm(x0, gain(norm_mix_pre, 0), name="prenorm_0", riders=[d2d("qkv", "ln")])
    landed(("qkv", "ln"), res)
    ln_g = full["ln"][:, 0, :].reshape(1, D_MODEL)
    ln_b = full["ln"][:, 1, :].reshape(1, D_MODEL)

    qkv, (res,) = mm_nn(h0, full["qkv"], out_dtype=F32, name="qkv_proj", riders=[ici("wo")])
    landed(("wo",), res)
    (q, k, v), (res,) = rope_fwd(qkv, attn_b_qkv, cos, sin, name="rope_fwd", riders=[d2d("wo")])
    landed(("wo",), res)
    q = q.reshape(N_KV_HEADS, GQA_GROUP, s, HEAD_DIM)
    o_heads, (res,) = attn_fwd(q, k, v, sink_rows, name="attn_fwd", riders=[ici("wgu0", "wd0")])
    landed(("wgu0", "wd0"), res)
    o, (res,) = heads_to_rows(o_heads.reshape(N_Q_HEADS, s, HEAD_DIM), name="heads_to_rows", riders=[d2d("wgu0", "wd0")])
    landed(("wgu0", "wd0"), res)
    w_o = full["wo"].reshape(Q_WIDTH, D_MODEL)
    m0, (res,) = mm_nn(o, w_o, out_dtype=F32, name="attn_out_proj", riders=[ici("win")])
    landed(("win",), res)
    (x1, h1), (res_a, res_b) = residual_norm(x0, m0, attn_b_o, gain(norm_mix_post, 0), gain(norm_ffn_pre, 0),
                                             name="residual_norm_0a", riders=[d2d("win"), ici("wout")])
    landed(("win",), res_a)
    landed(("wout",), res_b)
    (gu0, a0), (res_a, res_b) = ffn_up(h1, full["wgu0"], name="ffn_up_0", riders=[d2d("wout"), ici("wgu1")])
    landed(("wout",), res_a)
    landed(("wgu1",), res_b)
    w_d0 = full["wd0"].reshape(D_FF, D_MODEL)
    f0, (res_a, res_b) = mm_nn(a0, w_d0, out_dtype=F32, name="ffn_down_0", riders=[d2d("wgu1"), ici("wd1")])
    landed(("wgu1",), res_a)
    landed(("wd1",), res_b)
    (x2, h2), (res,) = residual_norm(x1, f0, zero_bias, gain(norm_ffn_post, 0), gain(norm_mix_pre, 1),
                                     name="residual_norm_0b", riders=[d2d("wd1")])
    landed(("wd1",), res)
    w_qkv, w_in, w_gu0, w_gu1 = full["qkv"], full["win"], full["wgu0"], full["wgu1"]
    w_out = full["wout"].reshape(D_MODEL, D_MODEL)
    w_d1 = full["wd1"].reshape(D_FF, D_MODEL)
    z = mm_nn(h2, w_in, out_dtype=F32, name="sgu_in_proj")
    y = sgu_fwd(z, ln_g, ln_b, w_sp, b_sp, name="sgu_fwd")
    m1 = mm_nn(y, w_out, out_dtype=F32, name="sgu_out_proj")
    x3, h3 = residual_norm(x2, m1, zero_bias, gain(norm_mix_post, 1), gain(norm_ffn_pre, 1), name="residual_norm_1a")
    gu1, a1 = ffn_up(h3, w_gu1, name="ffn_up_1")
    f1 = mm_nn(a1, w_d1, out_dtype=F32, name="ffn_down_1")
    dx4, df1, dg_fpost1, loss_part = loss_head(x3, f1, gain(norm_ffn_post, 1), target, name="loss_head")
    loss = lax.psum(loss_part[0, 0], ("x", "y", "c"))

    red = _GradReduce(c_arr, jc_arr, {
        "qkv": attn_w_qkv.shape[1:], "wo": attn_w_o.shape[1:], "win": sgu_w_in.shape[1:], "wout": sgu_w_out.shape[1:],
        "wgu": ffn_w_gate_up.shape, "wd": ffn_w_down.shape, "slab": (N_CHIPS, SLAB_ROWS // N_CHIPS, D_MODEL)})
    where = {"qkv": ("qkv", None), "wo": ("wo", None), "win": ("win", None), "wout": ("wout", None), "wgu0": ("wgu", 0),
             "wgu1": ("wgu", 1), "wd0": ("wd", 0), "wd1": ("wd", 1), "small": ("slab", "chip")}

    dgu1 = ffn_dact(df1, w_d1, gu1, name="ffn_dact_1")
    red.grad["wd1"] = mm_tn(a1, df1, shard_major=False, tm=256, tn=D_MODEL, name="dw_down_1").reshape(
        N_CHIPS, D_FF // N_CHIPS, D_MODEL)
    red.grad["wgu1"], (res,) = mm_tn(h3, dgu1, shard_major=True, tm=512, tn=FF_HALF, name="dw_gate_up_1",
                                     riders=[red.exchange(["wd1"])])
    red.exchanged(["wd1"], res)
    dh3, (res_a, res_b) = mm_nt(dgu1, w_gu1, out_dtype=F32, tm=512, name="dh_ffn_1",
                                riders=[red.exchange(["wgu1"]), red.scatter(["wd1"])])
    red.exchanged(["wgu1"], res_a)
    red.scattered(["wd1"], res_b, where)
    names, rider = red.broadcast([("wd", 1)])
    (dx3, dm1, dg_fpre1, dg_mpost1, _), (res,) = norm_bwd_pair(
        dx4, dh3, x3, gain(norm_ffn_pre, 1), m1, zero_bias, gain(norm_mix_post, 1), name="norm_bwd_1a", riders=[rider])
    red.broadcasted(names, res)
    dy = mm_nt(dm1, w_out, out_dtype=F32, name="dy_sgu")
    red.grad["wout"] = mm_tn(y, dm1, shard_major=False, tm=512, tn=D_MODEL, name="dw_sgu_out").reshape(
        N_CHIPS, D_MODEL // N_CHIPS, D_MODEL)
    (dz, dw_sp, db_sp, dln_g, dln_b), (res_a, res_b) = sgu_bwd(
        z, dy, ln_g, ln_b, w_sp, b_sp, name="sgu_bwd", riders=[red.scatter(["wgu1"]), red.exchange(["wout"])])
    red.scattered(["wgu1"], res_a, where)
    red.exchanged(["wout"], res_b)
    names, rider = red.broadcast([("wgu", 1)])
    red.grad["win"], (res_a, res_b) = mm_tn(h2, dz, shard_major=True, tm=D_MODEL, tn=2 * D_MODEL // N_CHIPS, name="dw_sgu_in",
                                            riders=[rider, red.scatter(["wout"])])
    red.broadcasted(names, res_a)
    red.scattered(["wout"], res_b, where)
    names, rider = red.broadcast([("wout", None)])
    dh2, (res_a, res_b) = mm_nt(dz, w_in, out_dtype=F32, tm=1024, name="dh_sgu", riders=[red.exchange(["win"]), rider])
    red.exchanged(["win"], res_a)
    red.broadcasted(names, res_b)
    (dx2, df0, dg_mpre1, dg_fpost0, _), (res,) = norm_bwd_pair(
        dx3, dh2, x2, gain(norm_mix_pre, 1), f0, zero_bias, gain(norm_ffn_post, 0), name="norm_bwd_0b",
        riders=[red.scatter(["win"])])
    red.scattered(["win"], res, where)
    names, rider = red.broadcast([("win", None)])
    dgu0, (res,) = ffn_dact(df0, w_d0, gu0, name="ffn_dact_0", riders=[rider])
    red.broadcasted(names, res)
    red.grad["wd0"] = mm_tn(a0, df0, shard_major=False, tm=256, tn=D_MODEL, name="dw_down_0").reshape(
        N_CHIPS, D_FF // N_CHIPS, D_MODEL)
    red.grad["wgu0"], (res,) = mm_tn(h1, dgu0, shard_major=True, tm=512, tn=FF_HALF, name="dw_gate_up_0",
                                     riders=[red.exchange(["wd0"])])
    red.exchanged(["wd0"], res)
    dh1, (res_a, res_b) = mm_nt(dgu0, w_gu0, out_dtype=F32, tm=512, name="dh_ffn_0",
                                riders=[red.exchange(["wgu0"]), red.scatter(["wd0"])])
    red.exchanged(["wgu0"], res_a)
    red.scattered(["wd0"], res_b, where)
    names, rider = red.broadcast([("wd", 0)])
    (dx1, dm0, dg_fpre0, dg_mpost0, db_o), (res,) = norm_bwd_pair(
        dx2, dh1, x1, gain(norm_ffn_pre, 0), m0, attn_b_o, gain(norm_mix_post, 0), name="norm_bwd_0a", riders=[rider])
    red.broadcasted(names, res)
    do = mm_nt(dm0, w_o, out_dtype=BF16, name="do_attn")
    red.grad["wo"] = mm_tn(o, dm0, shard_major=False, tm=512, tn=D_MODEL, name="dw_attn_out").reshape(
        N_CHIPS, Q_WIDTH // N_CHIPS, D_MODEL)
    do_heads, (res,) = rows_to_heads(do, name="rows_to_heads", riders=[red.exchange(["wo"])])
    red.exchanged(["wo"], res)
    do_heads = do_heads.reshape(N_KV_HEADS, GQA_GROUP, s, HEAD_DIM)
    (dq, dkc, dkp, dvc, dvp, dsink), (res,) = attn_bwd(q, k, v, sink_rows, do_heads, name="attn_bwd",
                                                       riders=[red.scatter(["wgu0", "wo"])])
    red.scattered(["wgu0", "wo"], res, where)
    names, rider = red.broadcast([("wgu", 0), ("wo", None)])
    (dqkv, db_qkv), (res,) = rope_bwd(dq.reshape(N_Q_HEADS, s, HEAD_DIM), dkc, dkp, dvc, dvp, cos, sin, name="rope_bwd",
                                      riders=[rider])
    red.broadcasted(names, res)
    red.grad["qkv"] = mm_tn(h0, dqkv, shard_major=True, tm=D_MODEL, tn=QKV_WIDTH // N_CHIPS, name="dw_qkv")
    dh0, (res,) = mm_nt(dqkv, w_qkv, out_dtype=F32, tm=1024, name="dh_attn", riders=[red.exchange(["qkv"])])
    red.exchanged(["qkv"], res)
    grad_x, dg_mpre0 = norm_bwd_last(dx1, dh0, x0, gain(norm_mix_pre, 0), name="norm_bwd_in")

    norm_grads = [jnp.concatenate(p, axis=0) for p in
                  ((dg_mpre0, dg_mpre1), (dg_mpost0, dg_mpost1), (dg_fpre0, dg_fpre1), (dg_fpost0, dg_fpost1))]
    red.grad["small"] = _pack_small(norm_grads, db_qkv, db_o, dsink[:, :, 0, 0], db_sp[:, :, 0], dln_g, dln_b, dw_sp)
    res_a, res_b = comm_call([red.scatter(["qkv"]), red.exchange(["small"])], name="tail_1")
    red.scattered(["qkv"], res_a, where)
    red.exchanged(["small"], res_b)
    names, rider = red.broadcast([("qkv", None)])
    res_a, res_b = comm_call([red.scatter(["small"]), rider], name="tail_2")
    red.scattered(["small"], res_a, where)
    red.broadcasted(names, res_b)
    ((slab_full,),) = comm_call([allcast_rider(red.dest["slab"])], name="tail_3")
    g_qkv, g_wo, g_win, g_wout, g_wgu, g_wd = (red.dest[n] for n in ("qkv", "wo", "win", "wout", "wgu", "wd"))
    g_norms, g_bqkv, g_bo, g_sinks, g_bsp, g_lng, g_lnb, g_wsp = _unpack_small(slab_full, chip)

    def big_update(w, g, m, v, tag):
        return adamw(w, g.reshape(w.shape), m, v, name=f"adamw_{tag}")

    upd = {
        "attn_w_qkv": big_update(attn_w_qkv, g_qkv, m_attn_w_qkv, v_attn_w_qkv, "qkv"),
        "attn_w_o": big_update(attn_w_o, g_wo, m_attn_w_o, v_attn_w_o, "wo"),
        "sgu_w_in": big_update(sgu_w_in, g_win, m_sgu_w_in, v_sgu_w_in, "win"),
        "sgu_w_out": big_update(sgu_w_out, g_wout, m_sgu_w_out, v_sgu_w_out, "wout"),
        "ffn_w_gate_up": big_update(ffn_w_gate_up, g_wgu, m_ffn_w_gate_up, v_ffn_w_gate_up, "wgu"),
        "ffn_w_down": big_update(ffn_w_down, g_wd, m_ffn_w_down, v_ffn_w_down, "wd"),
    }
    small_names = ["norm_mix_pre", "norm_mix_post", "norm_ffn_pre", "norm_ffn_post", "attn_b_qkv", "attn_sinks", "attn_b_o",
                   "sgu_ln_g", "sgu_ln_b", "sgu_w_spatial", "sgu_b_spatial"]
    small_w = [norm_mix_pre, norm_mix_post, norm_ffn_pre, norm_ffn_post, attn_b_qkv, attn_sinks, attn_b_o, sgu_ln_g, sgu_ln_b,
               sgu_w_spatial, sgu_b_spatial]
    small_m = [m_norm_mix_pre, m_norm_mix_post, m_norm_ffn_pre, m_norm_ffn_post, m_attn_b_qkv, m_attn_sinks, m_attn_b_o,
               m_sgu_ln_g, m_sgu_ln_b, m_sgu_w_spatial, m_sgu_b_spatial]
    small_v = [v_norm_mix_pre, v_norm_mix_post, v_norm_ffn_pre, v_norm_ffn_post, v_attn_b_qkv, v_attn_sinks, v_attn_b_o,
               v_sgu_ln_g, v_sgu_ln_b, v_sgu_w_spatial, v_sgu_b_spatial]
    small_g = g_norms + [g_bqkv, g_sinks, g_bo, g_lng, g_lnb, g_wsp, g_bsp]

    def flat2(a):
        return a.reshape(-1, a.shape[-1])

    res = adamw_small([flat2(a) for a in small_w], [flat2(a) for a in small_g], [flat2(a) for a in small_m],
                      [flat2(a) for a in small_v], name="adamw_small")
    for i, nm in enumerate(small_names):
        upd[nm] = tuple(r[i].reshape(small_w[i].shape) for r in res)

    order = ["norm_mix_pre", "norm_mix_post", "norm_ffn_pre", "norm_ffn_post", "attn_w_qkv", "attn_b_qkv", "attn_sinks",
             "attn_w_o", "attn_b_o", "sgu_w_in", "sgu_ln_g", "sgu_ln_b", "sgu_w_spatial", "sgu_b_spatial", "sgu_w_out",
             "ffn_w_gate_up", "ffn_w_down"]
    outs = [loss, grad_x.reshape(1, s, D_MODEL)]
    for part in range(4):
        outs += [upd[nm][part] for nm in order]
    return tuple(outs)
```

```python
import types

import jax
import jax.numpy as jnp
from jax import lax
from jax.experimental import pallas as pl
from jax.experimental.pallas import tpu as pltpu

F32 = jnp.float32
BF16 = jnp.bfloat16
I32 = jnp.int32

D_MODEL = 1024
HEAD_DIM = 64
N_Q_HEADS = 16
N_KV_HEADS = 4
GQA_GROUP = 4
WINDOW = 128
Q_WIDTH = 1024
KV_WIDTH = 256
QKV_WIDTH = 1536
ROPE_THETA = 10000.0
SGU_GROUPS = 8
SGU_CHUNK = 128
D_FF = 2816
FF_HALF = D_FF // 2
EPS = 1e-6
N_CHIPS = 4
LANES = 128

ADAM_LR = 0.001
ADAM_B1 = 0.9
ADAM_B2 = 0.999
ADAM_EPS = 1e-08
ADAM_WD = 0.01
ADAM_STEP = 10

VMEM_LIMIT = 52 * 1024 * 1024
MESH = pl.DeviceIdType.MESH
NEG = -1e30
NT_DIMS = (((1,), (1,)), ((), ()))
TN_DIMS = (((0,), (0,)), ((), ()))
NN_DIMS = (((1,), (0,)), ((), ()))
ANY = pl.BlockSpec(memory_space=pl.ANY)


def _row_tile(s, want):
    return want if s % want == 0 else s


def _call(body, *, name, grid=(), in_specs=(), out_specs=(), out_shape=(), scratch_shapes=(), operands=(), prefetch=(),
          aliases=None, riders=(), sem=None):
    n_pre, n_in, n_out, n_scr = len(prefetch), len(operands), len(out_shape), len(scratch_shapes)
    in_specs, out_specs, out_shape = list(in_specs), list(out_specs), list(out_shape)
    operands, scratch_shapes = list(operands), list(scratch_shapes)
    io_alias = {n_pre + i: o for i, o in (aliases or {}).items()}
    for r in riders:
        base_in, base_out = n_pre + len(operands), len(out_shape)
        operands += list(r.inputs)
        in_specs += [ANY] * len(r.inputs)
        for pos, i in enumerate(r.aliased):
            io_alias[base_in + i] = base_out + pos
            out_shape.append(jax.ShapeDtypeStruct(r.inputs[i].shape, r.inputs[i].dtype))
        out_shape += list(r.fresh)
        out_specs += [ANY] * (len(r.aliased) + len(r.fresh))
        scratch_shapes += [pltpu.SemaphoreType.DMA((r.nsem,)), pltpu.SemaphoreType.DMA((r.nsem,))]

    def wrapped(*refs):
        pre, p = refs[:n_pre], n_pre
        core_in, p = refs[p:p + n_in], p + n_in
        r_in = []
        for r in riders:
            r_in.append(refs[p:p + len(r.inputs)])
            p += len(r.inputs)
        core_out, p = refs[p:p + n_out], p + n_out
        r_out = []
        for r in riders:
            k = len(r.aliased) + len(r.fresh)
            r_out.append(refs[p:p + k])
            p += k
        core_scr, p = refs[p:p + n_scr], p + n_scr
        r_sem = [refs[p + 2 * i:p + 2 * i + 2] for i in range(len(riders))]

        def edge(at_last, fns):
            def run():
                for i, r in enumerate(riders):
                    getattr(r, fns)(r_in[i], r_out[i], r_sem[i][0], r_sem[i][1])
            if not riders:
                return
            if not grid:
                run()
                return
            cond = None
            for d, n in enumerate(grid):
                c = pl.program_id(d) == (n - 1 if at_last else 0)
                cond = c if cond is None else jnp.logical_and(cond, c)
            pl.when(cond)(run)

        edge(False, "start")
        if body is not None:
            body(*pre, *core_in, *core_out, *core_scr)
        edge(True, "finish")

    if sem is None or riders:
        sem = ("arbitrary",) * len(grid)
    kwargs = dict(out_shape=out_shape, input_output_aliases=io_alias, name=name)
    if grid:
        kwargs["compiler_params"] = pltpu.CompilerParams(dimension_semantics=sem, vmem_limit_bytes=VMEM_LIMIT)
    if n_pre:
        kwargs["grid_spec"] = pltpu.PrefetchScalarGridSpec(
            num_scalar_prefetch=n_pre, grid=grid, in_specs=in_specs, out_specs=out_specs, scratch_shapes=scratch_shapes)
    else:
        kwargs.update(grid=grid, in_specs=in_specs, out_specs=out_specs, scratch_shapes=scratch_shapes)
    res = pl.pallas_call(wrapped, **kwargs)(*prefetch, *operands)
    core, rest, rider_res = list(res[:n_out]), list(res[n_out:]), []
    for r in riders:
        k = len(r.aliased) + len(r.fresh)
        rider_res.append(rest[:k])
        rest = rest[k:]
    return core, rider_res


def _mm_call(*, grid, in_specs, out_spec, out_shape, dims, nk, kaxis, acc_shape, name, operands, riders=()):
    out_dtype = out_shape.dtype

    def body(a_ref, b_ref, o_ref, *scratch):
        p = lax.dot_general(a_ref[...].astype(BF16), b_ref[...].astype(BF16), dims, preferred_element_type=F32)
        if nk == 1:
            o_ref[...] = p.astype(out_dtype)
        else:
            acc = scratch[0]
            kk = pl.program_id(kaxis)

            @pl.when(kk == 0)
            def _():
                acc[...] = p

            @pl.when(kk > 0)
            def _():
                acc[...] += p

            @pl.when(kk == nk - 1)
            def _():
                o_ref[...] = acc[...].astype(out_dtype)

    sem = ["parallel"] * len(grid)
    if nk > 1:
        sem[kaxis] = "arbitrary"
    (out,), rider_res = _call(
        body, grid=grid, in_specs=in_specs, out_specs=[out_spec], out_shape=[out_shape],
        scratch_shapes=[pltpu.VMEM(acc_shape, F32)] if nk > 1 else [], operands=operands, name=name, riders=riders,
        sem=tuple(sem))
    return (out, rider_res) if riders else out


def mm_nn(a, w, *, out_dtype, name, tm=512, tn=512, riders=()):
    m, k = a.shape
    tm = _row_tile(m, tm)
    if w.ndim == 3:
        ns = w.shape[2]
        grid = (N_CHIPS, m // tm)
        w_spec = pl.BlockSpec((None, k, ns), lambda j, i: (j, 0, 0))
        o_spec = pl.BlockSpec((tm, ns), lambda j, i: (i, j))
        n = N_CHIPS * ns
    else:
        n = w.shape[1]
        grid = (n // tn, m // tm)
        w_spec = pl.BlockSpec((k, tn), lambda j, i: (0, j))
        o_spec = pl.BlockSpec((tm, tn), lambda j, i: (i, j))
    return _mm_call(grid=grid, in_specs=[pl.BlockSpec((tm, k), lambda j, i: (i, 0)), w_spec], out_spec=o_spec,
                    out_shape=jax.ShapeDtypeStruct((m, n), out_dtype), dims=NN_DIMS, nk=1, kaxis=0, acc_shape=None,
                    name=name, operands=(a, w), riders=riders)


def mm_nt(a, w, *, out_dtype, name, tm=512, tn=512, riders=()):
    if w.ndim == 2:
        m, n = a.shape
        kout = w.shape[0]
        tm = _row_tile(m, tm)
        return _mm_call(grid=(kout // tn, m // tm),
                        in_specs=[pl.BlockSpec((tm, n), lambda j, i: (i, 0)), pl.BlockSpec((tn, n), lambda j, i: (j, 0))],
                        out_spec=pl.BlockSpec((tm, tn), lambda j, i: (i, j)),
                        out_shape=jax.ShapeDtypeStruct((m, kout), out_dtype), dims=NT_DIMS, nk=1, kaxis=0,
                        acc_shape=None, name=name, operands=(a, w), riders=riders)
    _, kout, ns = w.shape
    planes = a.ndim == 3
    m = a.shape[1] if planes else a.shape[0]
    tm = _row_tile(m, tm)
    a_spec = pl.BlockSpec((2, tm, 2 * ns), lambda i: (0, i, 0)) if planes else pl.BlockSpec((tm, N_CHIPS * ns), lambda i: (i, 0))

    def body(a_ref, w0, w1, w2, w3, o_ref):
        acc = None
        for j, w_ref in enumerate((w0, w1, w2, w3)):
            if planes:
                a_j = a_ref[j // 2, :, (j % 2) * ns:(j % 2 + 1) * ns]
            else:
                a_j = a_ref[:, j * ns:(j + 1) * ns]
            p = lax.dot_general(a_j, w_ref[...], NT_DIMS, preferred_element_type=F32)
            acc = p if acc is None else acc + p
        o_ref[...] = acc.astype(out_dtype)

    def shard(j):
        return pl.BlockSpec((None, kout, ns), lambda i: (j, 0, 0))

    (out,), rider_res = _call(
        body, grid=(m // tm,), in_specs=[a_spec] + [shard(j) for j in range(N_CHIPS)],
        out_specs=[pl.BlockSpec((tm, kout), lambda i: (i, 0))], out_shape=[jax.ShapeDtypeStruct((m, kout), out_dtype)],
        operands=(a, w, w, w, w), sem=("parallel",), name=name, riders=riders)
    return (out, rider_res) if riders else out


def mm_tn(a, b, *, shard_major, name, tm, tn, tk=None, out_dtype=BF16, riders=()):
    s, m = a.shape
    tk = s if tk is None else _row_tile(s, tk)
    if b.ndim == 3:
        n = 2 * b.shape[2]
        b_spec = pl.BlockSpec((None, tk, tn), lambda j, i, kk: (j // 2, kk, j % 2))
    else:
        n = b.shape[1]
        b_spec = pl.BlockSpec((tk, tn), lambda j, i, kk: (kk, j))
    if shard_major:
        assert tn == n // N_CHIPS
        o_spec = pl.BlockSpec((None, tm, tn), lambda j, i, kk: (j, i, 0))
        o_shape = jax.ShapeDtypeStruct((N_CHIPS, m, tn), out_dtype)
    else:
        o_spec = pl.BlockSpec((tm, tn), lambda j, i, kk: (i, j))
        o_shape = jax.ShapeDtypeStruct((m, n), out_dtype)
    return _mm_call(grid=(n // tn, m // tm, s // tk),
                    in_specs=[pl.BlockSpec((tk, tm), lambda j, i, kk: (kk, i)), b_spec], out_spec=o_spec,
                    out_shape=o_shape, dims=TN_DIMS, nk=s // tk, kaxis=2, acc_shape=(tm, tn), name=name, operands=(a, b),
                    riders=riders)


def _rstd(x):
    return lax.rsqrt(jnp.mean(x * x, axis=-1, keepdims=True) + EPS)


def _rms_bwd(dy, x, g):
    r = _rstd(x)
    xhat = x * r
    gy = dy * g
    dx = r * (gy - xhat * jnp.mean(gy * xhat, axis=-1, keepdims=True))
    return dx, jnp.sum(dy * xhat, axis=0, keepdims=True)


def _accum(ref, val, first):
    @pl.when(first)
    def _():
        ref[...] = val

    @pl.when(jnp.logical_not(first))
    def _():
        ref[...] += val


def _row_spec(tm, width):
    return pl.BlockSpec((tm, width), lambda i: (i, 0))


def _vec_spec(width):
    return pl.BlockSpec((1, width), lambda i: (0, 0))


def _ret(core, rider_res, riders):
    core = core[0] if len(core) == 1 else core
    return (core, rider_res) if riders else core


def prenorm(x, g, *, name, tm=256, riders=()):
    s = x.shape[0]
    tm = _row_tile(s, tm)

    def body(x_ref, g_ref, h_ref):
        xv = x_ref[...]
        h_ref[...] = (xv * _rstd(xv) * g_ref[...]).astype(BF16)

    core, rr = _call(
        body, grid=(s // tm,), in_specs=[_row_spec(tm, D_MODEL), _vec_spec(D_MODEL)], out_specs=[_row_spec(tm, D_MODEL)],
        out_shape=[jax.ShapeDtypeStruct((s, D_MODEL), BF16)], operands=(x, g), sem=("parallel",), name=name, riders=riders)
    return _ret(core, rr, riders)


def residual_norm(x, m, bias, g_post, g_next, *, name, tm=256, riders=()):
    s = x.shape[0]
    tm = _row_tile(s, tm)

    def body(x_ref, m_ref, b_ref, gp_ref, gn_ref, xo_ref, h_ref):
        mv = m_ref[...] + b_ref[...]
        xn = x_ref[...] + mv * _rstd(mv) * gp_ref[...]
        xo_ref[...] = xn
        h_ref[...] = (xn * _rstd(xn) * gn_ref[...]).astype(BF16)

    core, rr = _call(
        body, grid=(s // tm,),
        in_specs=[_row_spec(tm, D_MODEL), _row_spec(tm, D_MODEL), _vec_spec(D_MODEL), _vec_spec(D_MODEL), _vec_spec(D_MODEL)],
        out_specs=[_row_spec(tm, D_MODEL), _row_spec(tm, D_MODEL)],
        out_shape=[jax.ShapeDtypeStruct((s, D_MODEL), F32), jax.ShapeDtypeStruct((s, D_MODEL), BF16)],
        operands=(x, m, bias, g_post, g_next), sem=("parallel",), name=name, riders=riders)
    return _ret(core, rr, riders)


def loss_head(x, f, g_post, target, *, name, tm=256, riders=()):
    s = x.shape[0]
    tm = _row_tile(s, tm)

    def body(x_ref, f_ref, g_ref, t_ref, dx_ref, df_ref, dg_ref, loss_ref):
        first = pl.program_id(0) == 0
        fv = f_ref[...]
        g = g_ref[...]
        err = x_ref[...] + fv * _rstd(fv) * g - t_ref[...]
        dx = err * (1.0 / D_MODEL)
        dx_ref[...] = dx
        df, dg = _rms_bwd(dx, fv, g)
        df_ref[...] = df.astype(BF16)
        _accum(dg_ref, dg, first)
        part = jnp.sum(jnp.sum(err * err, axis=-1, keepdims=True), axis=0, keepdims=True) * (0.5 / D_MODEL)
        _accum(loss_ref, jnp.broadcast_to(part, (8, LANES)), first)

    core, rr = _call(
        body, grid=(s // tm,),
        in_specs=[_row_spec(tm, D_MODEL), _row_spec(tm, D_MODEL), _vec_spec(D_MODEL), _row_spec(tm, D_MODEL)],
        out_specs=[_row_spec(tm, D_MODEL), _row_spec(tm, D_MODEL), _vec_spec(D_MODEL), pl.BlockSpec((8, LANES), lambda i: (0, 0))],
        out_shape=[jax.ShapeDtypeStruct((s, D_MODEL), F32), jax.ShapeDtypeStruct((s, D_MODEL), BF16),
                   jax.ShapeDtypeStruct((1, D_MODEL), F32), jax.ShapeDtypeStruct((8, LANES), F32)],
        operands=(x, f, g_post, target), name=name, riders=riders)
    return _ret(core, rr, riders)


def norm_bwd_pair(dres, dh, x, g_pre, m, bias, g_post, *, name, tm=256, riders=()):
    s = x.shape[0]
    tm = _row_tile(s, tm)

    def body(dres_ref, dh_ref, x_ref, gpre_ref, m_ref, b_ref, gpost_ref, dx_ref, dm_ref, dgpre_ref, dgpost_ref, db_ref):
        first = pl.program_id(0) == 0
        d1, dgpre = _rms_bwd(dh_ref[...], x_ref[...], gpre_ref[...])
        dx = dres_ref[...] + d1
        dx_ref[...] = dx
        dm, dgpost = _rms_bwd(dx, m_ref[...] + b_ref[...], gpost_ref[...])
        dm_ref[...] = dm.astype(BF16)
        _accum(dgpre_ref, dgpre, first)
        _accum(dgpost_ref, dgpost, first)
        _accum(db_ref, jnp.sum(dm, axis=0, keepdims=True), first)

    row, vec = _row_spec(tm, D_MODEL), _vec_spec(D_MODEL)
    vshape = jax.ShapeDtypeStruct((1, D_MODEL), F32)
    core, rr = _call(
        body, grid=(s // tm,), in_specs=[row, row, row, vec, row, vec, vec], out_specs=[row, row, vec, vec, vec],
        out_shape=[jax.ShapeDtypeStruct((s, D_MODEL), F32), jax.ShapeDtypeStruct((s, D_MODEL), BF16), vshape, vshape, vshape],
        operands=(dres, dh, x, g_pre, m, bias, g_post), name=name, riders=riders)
    return _ret(core, rr, riders)


def norm_bwd_last(dres, dh, x, g_pre, *, name, tm=256, riders=()):
    s = x.shape[0]
    tm = _row_tile(s, tm)

    def body(dres_ref, dh_ref, x_ref, g_ref, dx_ref, dg_ref):
        d1, dg = _rms_bwd(dh_ref[...], x_ref[...], g_ref[...])
        dx_ref[...] = dres_ref[...] + d1
        _accum(dg_ref, dg, pl.program_id(0) == 0)

    row, vec = _row_spec(tm, D_MODEL), _vec_spec(D_MODEL)
    core, rr = _call(
        body, grid=(s // tm,), in_specs=[row, row, row, vec], out_specs=[row, vec],
        out_shape=[jax.ShapeDtypeStruct((s, D_MODEL), F32), jax.ShapeDtypeStruct((1, D_MODEL), F32)],
        operands=(dres, dh, x, g_pre), name=name, riders=riders)
    return _ret(core, rr, riders)


def _rope_tables(s):
    half = HEAD_DIM // 2
    inv_freq = ROPE_THETA ** (-(jnp.arange(half, dtype=F32) * 2.0) / HEAD_DIM)
    ang = jnp.arange(s, dtype=I32).astype(F32)[:, None] * inv_freq[None, :]
    cos, sin = jnp.cos(ang), jnp.sin(ang)
    return jnp.tile(cos, (1, 4)), jnp.concatenate([-sin, sin, -sin, sin], axis=1)


def _swap_halves(x):
    lane = lax.broadcasted_iota(I32, x.shape, 1)
    return jnp.where((lane & (HEAD_DIM - 1)) < HEAD_DIM // 2, pltpu.roll(x, LANES - 32, 1), pltpu.roll(x, 32, 1))


N_ROPE_BLOCKS = (Q_WIDTH + KV_WIDTH) // LANES


def qkv_proj(h, w, bias, cos, sin, *, name, tm=512, riders=()):
    s, k = h.shape
    ns = w.shape[2]
    tm = _row_tile(s, tm)

    def body(h_ref, w_ref, b_ref, c_ref, s_ref, o_ref):
        j = pl.program_id(0)
        p = jnp.dot(h_ref[...], w_ref[...], preferred_element_type=F32) + b_ref[...]
        cosv, sinv = c_ref[...], s_ref[...]
        for blk in range(ns // LANES):
            xb = p[:, blk * LANES:(blk + 1) * LANES]
            roped = xb * cosv + _swap_halves(xb) * sinv
            is_qk = j * (ns // LANES) + blk < N_ROPE_BLOCKS
            o_ref[:, blk * LANES:(blk + 1) * LANES] = jnp.where(is_qk, roped, xb).astype(BF16)

    core, rr = _call(
        body, grid=(N_CHIPS, s // tm),
        in_specs=[pl.BlockSpec((tm, k), lambda j, i: (i, 0)), pl.BlockSpec((None, k, ns), lambda j, i: (j, 0, 0)),
                  pl.BlockSpec((1, ns), lambda j, i: (0, j)), pl.BlockSpec((tm, LANES), lambda j, i: (i, 0)),
                  pl.BlockSpec((tm, LANES), lambda j, i: (i, 0))],
        out_specs=[pl.BlockSpec((tm, ns), lambda j, i: (i, j))], out_shape=[jax.ShapeDtypeStruct((s, N_CHIPS * ns), BF16)],
        operands=(h, w, bias, cos, sin), sem=("parallel", "parallel"), name=name, riders=riders)
    return _ret(core, rr, riders)


def rope_bwd(dq, dkc, dkp, dvc, dvp, cos, sin, *, name, riders=()):
    s = dq.shape[0]
    tm = WINDOW
    nb = s // tm

    def body(dq_ref, dkc_ref, dkp_ref, dvc_ref, dvp_ref, c_ref, s_ref, o_ref, db_ref):
        i = pl.program_id(0)
        has_next = (i < nb - 1).astype(F32)
        cosv, sinv = c_ref[...], s_ref[...]
        for blk in range(QKV_WIDTH // LANES):
            if blk < Q_WIDTH // LANES:
                g = dq_ref[:, blk * LANES:(blk + 1) * LANES].astype(F32)
            else:
                own, nxt = (dkc_ref, dkp_ref) if blk < N_ROPE_BLOCKS else (dvc_ref, dvp_ref)
                cols = slice((blk % 2) * LANES, (blk % 2 + 1) * LANES)
                g = own[:, cols].astype(F32) + has_next * nxt[:, cols].astype(F32)
            if blk < N_ROPE_BLOCKS:
                g = g * cosv + _swap_halves(g * sinv)
            o_ref[:, blk * LANES:(blk + 1) * LANES] = g.astype(BF16)
            part = jnp.sum(g, axis=0, keepdims=True)

            @pl.when(i == 0)
            def _():
                db_ref[:, blk * LANES:(blk + 1) * LANES] = part

            @pl.when(i > 0)
            def _():
                db_ref[:, blk * LANES:(blk + 1) * LANES] += part

    own_spec = _row_spec(tm, KV_WIDTH)
    next_spec = pl.BlockSpec((tm, KV_WIDTH), lambda i: (jnp.minimum(i + 1, nb - 1), 0))
    core, rr = _call(
        body, grid=(nb,),
        in_specs=[_row_spec(tm, Q_WIDTH), own_spec, next_spec, own_spec, next_spec, _row_spec(tm, LANES), _row_spec(tm, LANES)],
        out_specs=[_row_spec(tm, QKV_WIDTH), _vec_spec(QKV_WIDTH)],
        out_shape=[jax.ShapeDtypeStruct((s, QKV_WIDTH), BF16), jax.ShapeDtypeStruct((1, QKV_WIDTH), F32)],
        operands=(dq, dkc, dkp, dvc, dvp, cos, sin), name=name, riders=riders)
    return _ret(core, rr, riders)


ROWS = GQA_GROUP * WINDOW


def _prev_slots():
    qpos = lax.broadcasted_iota(I32, (ROWS, WINDOW), 0) & (WINDOW - 1)
    kpos = lax.broadcasted_iota(I32, (ROWS, WINDOW), 1)
    return kpos > qpos


def _head_cols(ref, head):
    return ref[:, head * HEAD_DIM:(head + 1) * HEAD_DIM]


def _stack_heads(ref, h):
    return jnp.concatenate([_head_cols(ref, GQA_GROUP * h + g) for g in range(GQA_GROUP)], axis=0)


def _band(prev_ref, cur_ref, h):
    return jnp.concatenate([_head_cols(prev_ref, h), _head_cols(cur_ref, h)], axis=0)


def _pick(prev, band):
    return jnp.where(prev, band[:, :WINDOW], band[:, WINDOW:])


def _spread(prev, x):
    return jnp.concatenate([jnp.where(prev, x, 0.0), jnp.where(prev, 0.0, x)], axis=1).astype(BF16)


def _attn_probs(q, kband, sink, prev, has_prev):
    scale = HEAD_DIM ** -0.5
    s_band = lax.dot_general(q, kband, NT_DIMS, preferred_element_type=F32)
    s = jnp.where(prev, jnp.where(has_prev, s_band[:, :WINDOW], NEG), s_band[:, WINDOW:]) * scale
    m = jnp.maximum(jnp.max(s, axis=-1, keepdims=True), sink)
    e, es = jnp.exp(s - m), jnp.exp(sink - m)
    inv = 1.0 / (jnp.sum(e, axis=-1, keepdims=True) + es)
    return e * inv, es * inv


def _attn_specs(nb):
    kcol, vcol = Q_WIDTH // KV_WIDTH, Q_WIDTH // KV_WIDTH + 1
    q_spec = pl.BlockSpec((WINDOW, Q_WIDTH), lambda n: (n, 0))
    return [q_spec,
            pl.BlockSpec((WINDOW, KV_WIDTH), lambda n: (n, kcol)),
            pl.BlockSpec((WINDOW, KV_WIDTH), lambda n: (jnp.maximum(n - 1, 0), kcol)),
            pl.BlockSpec((WINDOW, KV_WIDTH), lambda n: (n, vcol)),
            pl.BlockSpec((WINDOW, KV_WIDTH), lambda n: (jnp.maximum(n - 1, 0), vcol)),
            pl.BlockSpec((N_KV_HEADS, ROWS, LANES), lambda n: (0, 0, 0))]


def attn_fwd(qkv, sink_rows, *, name, riders=()):
    s = qkv.shape[0]

    def body(q_ref, kc_ref, kp_ref, vc_ref, vp_ref, sink_ref, o_ref):
        prev = _prev_slots()
        has_prev = pl.program_id(0) > 0
        for h in range(N_KV_HEADS):
            p, _ = _attn_probs(_stack_heads(q_ref, h), _band(kp_ref, kc_ref, h), sink_ref[h], prev, has_prev)
            o = jnp.dot(_spread(prev, p), _band(vp_ref, vc_ref, h), preferred_element_type=F32)
            for g in range(GQA_GROUP):
                head = GQA_GROUP * h + g
                o_ref[:, head * HEAD_DIM:(head + 1) * HEAD_DIM] = o[g * WINDOW:(g + 1) * WINDOW].astype(BF16)

    core, rr = _call(
        body, grid=(s // WINDOW,), in_specs=_attn_specs(s // WINDOW), out_specs=[pl.BlockSpec((WINDOW, Q_WIDTH), lambda n: (n, 0))],
        out_shape=[jax.ShapeDtypeStruct((s, Q_WIDTH), BF16)], operands=(qkv, qkv, qkv, qkv, qkv, sink_rows), sem=("parallel",),
        name=name, riders=riders)
    return _ret(core, rr, riders)


def attn_bwd(qkv, sink_rows, do, *, name, riders=()):
    s = qkv.shape[0]

    def body(q_ref, kc_ref, kp_ref, vc_ref, vp_ref, sink_ref, do_ref, dq_ref, dkc_ref, dkp_ref, dvc_ref, dvp_ref, dsink_ref):
        n = pl.program_id(0)
        prev = _prev_slots()
        scale = HEAD_DIM ** -0.5
        parts = []
        for h in range(N_KV_HEADS):
            qv, dov = _stack_heads(q_ref, h), _stack_heads(do_ref, h)
            kband, vband = _band(kp_ref, kc_ref, h), _band(vp_ref, vc_ref, h)
            p, ps = _attn_probs(qv, kband, sink_ref[h], prev, n > 0)
            dp = _pick(prev, lax.dot_general(dov, vband, NT_DIMS, preferred_element_type=F32))
            delta = jnp.sum(p * dp, axis=-1, keepdims=True)
            ds_band = _spread(prev, p * (dp - delta) * scale)
            p_band = _spread(prev, p)
            dq = jnp.dot(ds_band, kband, preferred_element_type=F32)
            dk = lax.dot_general(qv, ds_band, TN_DIMS, preferred_element_type=F32).T.astype(BF16)
            dv = lax.dot_general(dov, p_band, TN_DIMS, preferred_element_type=F32).T.astype(BF16)
            cols = slice(h * HEAD_DIM, (h + 1) * HEAD_DIM)
            dkp_ref[:, cols], dkc_ref[:, cols] = dk[:WINDOW], dk[WINDOW:]
            dvp_ref[:, cols], dvc_ref[:, cols] = dv[:WINDOW], dv[WINDOW:]
            dsink = -(ps * delta)
            for g in range(GQA_GROUP):
                head = GQA_GROUP * h + g
                dq_ref[:, head * HEAD_DIM:(head + 1) * HEAD_DIM] = dq[g * WINDOW:(g + 1) * WINDOW].astype(BF16)
                parts.append(jnp.broadcast_to(jnp.sum(dsink[g * WINDOW:(g + 1) * WINDOW], axis=0, keepdims=True), (8, LANES)))

        @pl.when(n == 0)
        def _():
            for i, part in enumerate(parts):
                dsink_ref[i // GQA_GROUP, i % GQA_GROUP] = part

        @pl.when(n > 0)
        def _():
            for i, part in enumerate(parts):
                dsink_ref[i // GQA_GROUP, i % GQA_GROUP] += part

    rows_q = pl.BlockSpec((WINDOW, Q_WIDTH), lambda n: (n, 0))
    rows_kv = pl.BlockSpec((WINDOW, KV_WIDTH), lambda n: (n, 0))
    kv_shape = jax.ShapeDtypeStruct((s, KV_WIDTH), BF16)
    core, rr = _call(
        body, grid=(s // WINDOW,), in_specs=_attn_specs(s // WINDOW) + [rows_q],
        out_specs=[rows_q, rows_kv, rows_kv, rows_kv, rows_kv,
                   pl.BlockSpec((N_KV_HEADS, GQA_GROUP, 8, LANES), lambda n: (0, 0, 0, 0))],
        out_shape=[jax.ShapeDtypeStruct((s, Q_WIDTH), BF16), kv_shape, kv_shape, kv_shape, kv_shape,
                   jax.ShapeDtypeStruct((N_KV_HEADS, GQA_GROUP, 8, LANES), F32)],
        operands=(qkv, qkv, qkv, qkv, qkv, sink_rows, do), sem=("arbitrary",), name=name, riders=riders)
    return _ret(core, rr, riders)


GELU_C = 0.7978845608028654
GELU_A = 0.044715


def _gelu(x):
    return 0.5 * x * (1.0 + jnp.tanh(GELU_C * (x + GELU_A * x * x * x)))


def _gelu_grad(x):
    t = jnp.tanh(GELU_C * (x + GELU_A * x * x * x))
    return 0.5 * (1.0 + t) + 0.5 * x * (1.0 - t * t) * GELU_C * (1.0 + 3.0 * GELU_A * x * x)


def _tril_bf16(w):
    row = lax.broadcasted_iota(I32, (SGU_CHUNK, SGU_CHUNK), 0)
    col = lax.broadcasted_iota(I32, (SGU_CHUNK, SGU_CHUNK), 1)
    return jnp.where(row >= col, w, 0.0).astype(BF16)


def _sgu_norm(vg, g, b):
    mu = jnp.mean(vg, axis=-1, keepdims=True)
    cen = vg - mu
    rstd = lax.rsqrt(jnp.mean(cen * cen, axis=-1, keepdims=True) + EPS)
    xhat = cen * rstd
    return xhat, rstd, xhat * g + b


def sgu_fwd(z, ln_g, ln_b, w_sp, b_sp, *, name, tm=256, riders=()):
    s = z.shape[0]
    tm = _row_tile(s, tm)

    def body(z_ref, g_ref, b_ref, w_ref, bs_ref, y_ref):
        u = _gelu(z_ref[:, :D_MODEL])
        _, _, vn = _sgu_norm(_gelu(z_ref[:, D_MODEL:]), g_ref[...], b_ref[...])
        vn = vn.astype(BF16)
        for grp in range(SGU_GROUPS):
            w = _tril_bf16(w_ref[grp])
            cols = slice(grp * LANES, (grp + 1) * LANES)
            for ch in range(tm // SGU_CHUNK):
                rows = slice(ch * SGU_CHUNK, (ch + 1) * SGU_CHUNK)
                mixed = jnp.dot(w, vn[rows, cols], preferred_element_type=F32) + bs_ref[grp]
                y_ref[rows, cols] = (u[rows, cols] * mixed).astype(BF16)

    full3 = pl.BlockSpec((SGU_GROUPS, SGU_CHUNK, SGU_CHUNK), lambda i: (0, 0, 0))
    core, rr = _call(
        body, grid=(s // tm,), in_specs=[_row_spec(tm, 2 * D_MODEL), _vec_spec(D_MODEL), _vec_spec(D_MODEL), full3, full3],
        out_specs=[_row_spec(tm, D_MODEL)], out_shape=[jax.ShapeDtypeStruct((s, D_MODEL), BF16)],
        operands=(z, ln_g, ln_b, w_sp, b_sp), sem=("parallel",), name=name, riders=riders)
    return _ret(core, rr, riders)


def sgu_bwd(z, dy, ln_g, ln_b, w_sp, b_sp, *, name, tm=256, riders=()):
    s = z.shape[0]
    tm = _row_tile(s, tm)

    def body(z_ref, dy_ref, g_ref, b_ref, w_ref, bs_ref, dz_ref, dw_ref, dbs_ref, dg_ref, db_ref, dvn_buf):
        first = pl.program_id(0) == 0
        zu, zv = z_ref[:, :D_MODEL], z_ref[:, D_MODEL:]
        u = _gelu(zu)
        xhat, rstd, vn = _sgu_norm(_gelu(zv), g_ref[...], b_ref[...])
        vn = vn.astype(BF16)
        dyv = dy_ref[...]
        dmixed = dyv * u
        row = lax.broadcasted_iota(I32, (SGU_CHUNK, SGU_CHUNK), 0)
        col = lax.broadcasted_iota(I32, (SGU_CHUNK, SGU_CHUNK), 1)
        for grp in range(SGU_GROUPS):
            w = _tril_bf16(w_ref[grp])
            cols = slice(grp * LANES, (grp + 1) * LANES)
            dw = jnp.zeros((SGU_CHUNK, SGU_CHUNK), F32)
            dbs = jnp.zeros((SGU_CHUNK, 1), F32)
            for ch in range(tm // SGU_CHUNK):
                rows = slice(ch * SGU_CHUNK, (ch + 1) * SGU_CHUNK)
                vblk = vn[rows, cols]
                mixed = jnp.dot(w, vblk, preferred_element_type=F32) + bs_ref[grp]
                dz_ref[rows, cols] = (dyv[rows, cols] * mixed * _gelu_grad(zu[rows, cols])).astype(BF16)
                dm = dmixed[rows, cols]
                dmb = dm.astype(BF16)
                dvn_buf[rows, cols] = lax.dot_general(w, dmb, TN_DIMS, preferred_element_type=F32)
                dw += lax.dot_general(dmb, vblk, NT_DIMS, preferred_element_type=F32)
                dbs += jnp.sum(dm, axis=-1, keepdims=True)
            dw = jnp.where(row >= col, dw, 0.0)
            dbs = jnp.broadcast_to(dbs, (SGU_CHUNK, SGU_CHUNK))

            @pl.when(first)
            def _():
                dw_ref[grp] = dw
                dbs_ref[grp] = dbs

            @pl.when(jnp.logical_not(first))
            def _():
                dw_ref[grp] += dw
                dbs_ref[grp] += dbs

        dvn = dvn_buf[...]
        dxhat = dvn * g_ref[...]
        dvg = rstd * (dxhat - jnp.mean(dxhat, axis=-1, keepdims=True) - xhat * jnp.mean(dxhat * xhat, axis=-1, keepdims=True))
        dz_ref[:, D_MODEL:] = (dvg * _gelu_grad(zv)).astype(BF16)
        _accum(dg_ref, jnp.sum(dvn * xhat, axis=0, keepdims=True), first)
        _accum(db_ref, jnp.sum(dvn, axis=0, keepdims=True), first)

    full3 = pl.BlockSpec((SGU_GROUPS, SGU_CHUNK, SGU_CHUNK), lambda i: (0, 0, 0))
    s3 = jax.ShapeDtypeStruct((SGU_GROUPS, SGU_CHUNK, SGU_CHUNK), F32)
    vshape = jax.ShapeDtypeStruct((1, D_MODEL), F32)
    core, rr = _call(
        body, grid=(s // tm,),
        in_specs=[_row_spec(tm, 2 * D_MODEL), _row_spec(tm, D_MODEL), _vec_spec(D_MODEL), _vec_spec(D_MODEL), full3, full3],
        out_specs=[_row_spec(tm, 2 * D_MODEL), full3, full3, _vec_spec(D_MODEL), _vec_spec(D_MODEL)],
        out_shape=[jax.ShapeDtypeStruct((s, 2 * D_MODEL), BF16), s3, s3, vshape, vshape],
        scratch_shapes=[pltpu.VMEM((tm, D_MODEL), F32)], operands=(z, dy, ln_g, ln_b, w_sp, b_sp), name=name, riders=riders)
    return _ret(core, rr, riders)


def _sigmoid(x):
    return 1.0 / (1.0 + jnp.exp(-x))


def ffn_up(h, w_gu, *, name, tm=512, riders=()):
    s = h.shape[0]
    tm = _row_tile(s, tm)

    def body(h_ref, wg_ref, wu_ref, gu_ref, a_ref):
        hv = h_ref[...]
        g = jnp.dot(hv, wg_ref[...], preferred_element_type=F32)
        u = jnp.dot(hv, wu_ref[...], preferred_element_type=F32)
        gu_ref[0] = g.astype(BF16)
        gu_ref[1] = u.astype(BF16)
        a_ref[...] = (g * _sigmoid(g) * u).astype(BF16)

    core, rr = _call(
        body, grid=(2, s // tm),
        in_specs=[pl.BlockSpec((tm, D_MODEL), lambda j, i: (i, 0)),
                  pl.BlockSpec((None, D_MODEL, FF_HALF), lambda j, i: (j, 0, 0)),
                  pl.BlockSpec((None, D_MODEL, FF_HALF), lambda j, i: (j + 2, 0, 0))],
        out_specs=[pl.BlockSpec((2, tm, FF_HALF), lambda j, i: (0, i, j)), pl.BlockSpec((tm, FF_HALF), lambda j, i: (i, j))],
        out_shape=[jax.ShapeDtypeStruct((2, s, D_FF), BF16), jax.ShapeDtypeStruct((s, D_FF), BF16)],
        operands=(h, w_gu, w_gu), sem=("parallel", "parallel"), name=name, riders=riders)
    return _ret(core, rr, riders)


def ffn_dact(df, w_d, gu, *, name, tm=512, riders=()):
    s = df.shape[0]
    tm = _row_tile(s, tm)

    def body(df_ref, w_ref, gu_ref, o_ref):
        da = lax.dot_general(df_ref[...], w_ref[...], NT_DIMS, preferred_element_type=F32)
        g = gu_ref[0].astype(F32)
        u = gu_ref[1].astype(F32)
        sig = _sigmoid(g)
        o_ref[0] = (da * u * sig * (1.0 + g * (1.0 - sig))).astype(BF16)
        o_ref[1] = (da * g * sig).astype(BF16)

    planes = pl.BlockSpec((2, tm, FF_HALF), lambda j, i: (0, i, j))
    core, rr = _call(
        body, grid=(2, s // tm),
        in_specs=[pl.BlockSpec((tm, D_MODEL), lambda j, i: (i, 0)), pl.BlockSpec((FF_HALF, D_MODEL), lambda j, i: (j, 0)), planes],
        out_specs=[planes], out_shape=[jax.ShapeDtypeStruct((2, s, D_FF), BF16)], operands=(df, w_d, gu),
        sem=("parallel", "parallel"), name=name, riders=riders)
    return _ret(core, rr, riders)


def _weight_tile(rows):
    for tr in (512, 352, 256, 128):
        if rows % tr == 0:
            return tr
    return rows


def place_shard(w, layer, chip_arr, dtype, *, name):
    _, r, c = w.shape
    tr = _weight_tile(r)

    def body(chip_ref, w_ref, o_ref):
        o_ref[...] = w_ref[...].astype(dtype)

    (out,), _ = _call(
        body, grid=(r // tr,), prefetch=(chip_arr,),
        in_specs=[pl.BlockSpec((None, tr, c), lambda i, chip: (layer, i, 0))],
        out_specs=[pl.BlockSpec((None, tr, c), lambda i, chip: (chip[0], i, 0))],
        out_shape=[jax.ShapeDtypeStruct((N_CHIPS, r, c), dtype)], operands=(w,), sem=("parallel",), name=name)
    return out


def _adamw_math(w, g, m, v):
    m = ADAM_B1 * m + (1.0 - ADAM_B1) * g
    v = ADAM_B2 * v + (1.0 - ADAM_B2) * (g * g)
    m_hat = m / (1.0 - ADAM_B1 ** ADAM_STEP)
    v_hat = v / (1.0 - ADAM_B2 ** ADAM_STEP)
    delta = -ADAM_LR * (m_hat / (jnp.sqrt(v_hat) + ADAM_EPS) + ADAM_WD * w)
    return delta, m, v


def adamw(w, g, m, v, *, name):
    nl, r, c = w.shape
    tr = _weight_tile(r)

    def body(w_ref, g_ref, m_ref, v_ref, go_ref, d_ref, mo_ref, vo_ref):
        gv = g_ref[...]
        go_ref[...] = gv
        d_ref[...], mo_ref[...], vo_ref[...] = _adamw_math(w_ref[...], gv, m_ref[...], v_ref[...])

    spec = pl.BlockSpec((None, tr, c), lambda l, i: (l, i, 0))
    shape = jax.ShapeDtypeStruct(w.shape, F32)
    outs, _ = _call(body, grid=(nl, r // tr), in_specs=[spec] * 4, out_specs=[spec] * 4, out_shape=[shape] * 4,
                    operands=(w, g, m, v), sem=("parallel", "parallel"), name=name)
    return outs


def adamw_small(ws, gs, ms, vs, *, name):
    n = len(ws)

    def body(*refs):
        ins, outs = refs[:4 * n], refs[4 * n:]
        for t in range(n):
            gv = ins[n + t][...]
            outs[t][...] = gv
            outs[n + t][...], outs[2 * n + t][...], outs[3 * n + t][...] = _adamw_math(
                ins[t][...], gv, ins[2 * n + t][...], ins[3 * n + t][...])

    shapes = [jax.ShapeDtypeStruct(w.shape, F32) for w in ws]
    res = pl.pallas_call(body, out_shape=shapes * 4, name=name)(*ws, *gs, *ms, *vs)
    return res[:n], res[n:2 * n], res[2 * n:3 * n], res[3 * n:]


def pair_add(g, r1, c_arr, *, name):
    _, rows, cdim = g.shape
    h = rows // 2

    def body(c_ref, g_ref, r_ref, o_ref):
        o_ref[...] = (g_ref[...].astype(F32) + r_ref[...].astype(F32)).astype(o_ref.dtype)

    (out,), _ = _call(
        body, grid=(N_CHIPS,), prefetch=(c_arr,),
        in_specs=[pl.BlockSpec((None, h, cdim), lambda s, c: (s, c[0], 0)), pl.BlockSpec((None, h, cdim), lambda s, c: (s, 0, 0))],
        out_specs=[pl.BlockSpec((None, h, cdim), lambda s, c: (s, 0, 0))],
        out_shape=[jax.ShapeDtypeStruct((N_CHIPS, h, cdim), g.dtype)], operands=(g, r1), sem=("parallel",), name=name)
    return out


def final_add(g, r1, r2, jc_arr, *, dest_shape, lead, prev, name):
    _, rows, cdim = g.shape
    h = rows // 2

    def body(jc_ref, g_ref, r1_ref, r2_ref, *rest):
        o_ref = rest[-1]
        acc = g_ref[...].astype(F32) + r1_ref[...].astype(F32)
        for k in range(3):
            acc = acc + r2_ref[k].astype(F32)
        o_ref[...] = acc

    if lead is None:
        o_spec = pl.BlockSpec((h, cdim), lambda i, jc: (jc[1], 0))
    elif lead == "chip":
        o_spec = pl.BlockSpec((None, h, cdim), lambda i, jc: (jc[0], jc[1], 0))
    else:
        o_spec = pl.BlockSpec((None, h, cdim), lambda i, jc: (lead, jc[1], 0))
    in_specs = [pl.BlockSpec((None, h, cdim), lambda i, jc: (jc[0], jc[1], 0)),
                pl.BlockSpec((None, h, cdim), lambda i, jc: (jc[0], 0, 0)),
                pl.BlockSpec((3, h, cdim), lambda i, jc: (0, 0, 0))]
    operands = [g, r1, r2]
    aliases = None
    if prev is not None:
        in_specs.append(ANY)
        operands.append(prev)
        aliases = {3: 0}
    (out,), _ = _call(body, grid=(1,), prefetch=(jc_arr,), in_specs=in_specs, out_specs=[o_spec],
                      out_shape=[jax.ShapeDtypeStruct(dest_shape, F32)], operands=operands, aliases=aliases, name=name)
    return out


def _place():
    return lax.axis_index("x"), lax.axis_index("y"), lax.axis_index("c")


def _partner(x, y, k):
    return (1 - x if k >> 1 else x), (1 - y if k & 1 else y)


def _half(rows, sel, dtype):
    align = 16 if dtype == BF16 else 8
    return pl.ds(pl.multiple_of(sel * (rows // 2), align), rows // 2)


def _rider(inputs, aliased, fresh, nsem, copies, arrivals):
    def start(ins, outs, send, recv):
        for cp in copies(ins, outs, send, recv):
            cp.start()

    def finish(ins, outs, send, recv):
        for cp in arrivals(ins, outs, send, recv):
            cp.wait_recv()
        for cp in copies(ins, outs, send, recv):
            cp.wait_send()

    return types.SimpleNamespace(inputs=list(inputs), aliased=list(aliased), fresh=list(fresh), nsem=nsem, start=start,
                                 finish=finish)


def _remote(src, dst, send, recv, idx, dev):
    return pltpu.make_async_remote_copy(src_ref=src, dst_ref=dst, send_sem=send.at[idx], recv_sem=recv.at[idx],
                                        device_id=dev, device_id_type=MESH)


def gather_ici_rider(fulls):
    nt = len(fulls)

    def region(outs, t, slot, sel):
        return outs[t].at[slot, _half(fulls[t].shape[1], sel, fulls[t].dtype)]

    def copies(ins, outs, send, recv):
        x, y, c = _place()
        res = []
        for t in range(nt):
            for k in (1, 2, 3):
                px, py = _partner(x, y, k)
                mine = region(outs, t, 2 * x + y, c)
                res.append(_remote(mine, mine, send, recv, 3 * t + k - 1, (px, py, c)))
        return res

    def arrivals(ins, outs, send, recv):
        x, y, c = _place()
        res = []
        for t in range(nt):
            for k in (1, 2, 3):
                px, py = _partner(x, y, k)
                theirs = region(outs, t, 2 * px + py, c)
                res.append(_remote(theirs, theirs, send, recv, 3 * t + k - 1, (x, y, c)))
        return res

    return _rider(fulls, range(nt), [], 3 * nt, copies, arrivals)


def gather_d2d_rider(fulls):
    nt = len(fulls)

    def region(outs, t, slot, sel):
        return outs[t].at[slot, _half(fulls[t].shape[1], sel, fulls[t].dtype)]

    def both(outs, send, recv, mine):
        x, y, c = _place()
        res = []
        for t in range(nt):
            for k in (1, 2, 3):
                px, py = _partner(x, y, k)
                part = region(outs, t, 2 * px + py, c if mine else 1 - c)
                res.append(_remote(part, part, send, recv, 3 * t + k - 1, (x, y, 1 - c)))
        return res

    return _rider(fulls, range(nt), [], 3 * nt, lambda i, o, s, r: both(o, s, r, True), lambda i, o, s, r: both(o, s, r, False))


def exchange_rider(grads):
    nt = len(grads)

    def both(ins, outs, send, recv):
        x, y, c = _place()
        return [_remote(ins[t].at[:, _half(grads[t].shape[1], 1 - c, grads[t].dtype)], outs[t], send, recv, t, (x, y, 1 - c))
                for t in range(nt)]

    fresh = [jax.ShapeDtypeStruct((N_CHIPS, g.shape[1] // 2, g.shape[2]), g.dtype) for g in grads]
    return _rider(grads, [], fresh, nt, both, both)


def scatter_rider(parts):
    nt = len(parts)

    def both(ins, outs, send, recv):
        x, y, c = _place()
        res = []
        for t in range(nt):
            for k in (1, 2, 3):
                px, py = _partner(x, y, k)
                res.append(_remote(ins[t].at[2 * px + py], outs[t].at[k - 1], send, recv, 3 * t + k - 1, (px, py, c)))
        return res

    fresh = [jax.ShapeDtypeStruct((3,) + p.shape[1:], p.dtype) for p in parts]
    return _rider(parts, [], fresh, 3 * nt, both, both)


def broadcast_rider(bufs, items):
    def region(outs, item, sel):
        bi, lead = item
        ref = outs[bi]
        if lead == "chip":
            x, y, _ = _place()
            ref = ref.at[2 * x + y]
        elif lead is not None:
            ref = ref.at[lead]
        return ref.at[_half(ref.shape[0], sel, F32)]

    def both(outs, send, recv, mine):
        x, y, c = _place()
        res = []
        for i, item in enumerate(items):
            part = region(outs, item, c if mine else 1 - c)
            res.append(_remote(part, part, send, recv, i, (x, y, 1 - c)))
        return res

    return _rider(bufs, range(len(bufs)), [], len(items), lambda i, o, s, r: both(o, s, r, True),
                  lambda i, o, s, r: both(o, s, r, False))


def allcast_rider(buf):
    peers = [(k, flip) for k in range(N_CHIPS) for flip in (0, 1) if (k, flip) != (0, 0)]

    def both(outs, send, recv, mine):
        x, y, c = _place()
        res = []
        for i, (k, flip) in enumerate(peers):
            px, py = _partner(x, y, k)
            pc = 1 - c if flip else c
            slot, sel = (2 * x + y, c) if mine else (2 * px + py, pc)
            part = outs[0].at[slot, _half(buf.shape[1], sel, F32)]
            res.append(_remote(part, part, send, recv, i, (px, py, pc)))
        return res

    return _rider([buf], [0], [], len(peers), lambda i, o, s, r: both(o, s, r, True), lambda i, o, s, r: both(o, s, r, False))


def comm_call(riders, *, name):
    _, res = _call(None, riders=riders, name=name)
    return res


SLAB_ROWS = 192


def _pad_rows(a, rows=8):
    return jnp.pad(a, ((0, rows - a.shape[0]), (0, 0)))


def _pack_small(norm_grads, db_qkv, db_o, dsinks, db_sp, dln_g, dln_b, dw_sp):
    parts = [
        jnp.concatenate(norm_grads, axis=0),
        _pad_rows(jnp.pad(db_qkv, ((0, 0), (0, 2 * D_MODEL - QKV_WIDTH))).reshape(2, D_MODEL)),
        _pad_rows(db_o),
        _pad_rows(jnp.pad(dsinks.reshape(1, N_Q_HEADS), ((0, 0), (0, D_MODEL - N_Q_HEADS)))),
        _pad_rows(db_sp.reshape(1, D_MODEL)),
        _pad_rows(jnp.concatenate([dln_g, dln_b], axis=0)),
        dw_sp.reshape(SGU_CHUNK, D_MODEL),
    ]
    slab = jnp.concatenate(parts, axis=0)
    return jnp.pad(slab, ((0, SLAB_ROWS - slab.shape[0]), (0, 0))).reshape(N_CHIPS, SLAB_ROWS // N_CHIPS, D_MODEL)


def _unpack_small(slab, j):
    slab = slab.reshape(SLAB_ROWS, D_MODEL)
    norms = [slab[2 * i:2 * i + 2] for i in range(4)]
    db_qkv = slab[8:10].reshape(1, 2 * D_MODEL)[:, :QKV_WIDTH]
    db_o = slab[16:17]
    dsinks = slab[24:25, :N_Q_HEADS]
    db_sp = slab[32:33].reshape(SGU_GROUPS, SGU_CHUNK)
    width = D_MODEL // N_CHIPS
    dln_g = lax.dynamic_slice(slab[40:41], (0, j * width), (1, width))
    dln_b = lax.dynamic_slice(slab[41:42], (0, j * width), (1, width))
    dw_sp = slab[48:48 + SGU_CHUNK].reshape(SGU_GROUPS * SGU_CHUNK, SGU_CHUNK)
    return norms, db_qkv, db_o, dsinks, db_sp, dln_g, dln_b, dw_sp


class _GradReduce:
    def __init__(self, c_arr, jc_arr, dest_shapes):
        self.c_arr, self.jc_arr, self.dest_shapes = c_arr, jc_arr, dest_shapes
        self.grad, self.sibling, self.pair, self.chips, self.dest = {}, {}, {}, {}, {}

    def exchange(self, tags):
        return exchange_rider([self.grad[t] for t in tags])

    def exchanged(self, tags, res):
        for t, r in zip(tags, res):
            self.sibling[t] = r
            self.pair[t] = pair_add(self.grad[t], r, self.c_arr, name=f"pair_add_{t}")

    def scatter(self, tags):
        return scatter_rider([self.pair[t] for t in tags])

    def scattered(self, tags, res, where):
        for t, r in zip(tags, res):
            name, lead = where[t]
            self.dest[name] = final_add(self.grad[t], self.sibling[t], r, self.jc_arr, dest_shape=self.dest_shapes[name],
                                        lead=lead, prev=self.dest.get(name), name=f"final_add_{t}")

    def broadcast(self, items):
        names = []
        for n, _ in items:
            if n not in names:
                names.append(n)
        return names, broadcast_rider([self.dest[n] for n in names], [(names.index(n), lead) for n, lead in items])

    def broadcasted(self, names, res):
        for n, r in zip(names, res):
            self.dest[n] = r


def kernel(x, norm_mix_pre, norm_mix_post, norm_ffn_pre, norm_ffn_post, attn_w_qkv, attn_b_qkv, attn_sinks, attn_w_o, attn_b_o, sgu_w_in, sgu_ln_g, sgu_ln_b, sgu_w_spatial, sgu_b_spatial, sgu_w_out, ffn_w_gate_up, ffn_w_down, loss_target, m_norm_mix_pre, m_norm_mix_post, m_norm_ffn_pre, m_norm_ffn_post, m_attn_w_qkv, m_attn_b_qkv, m_attn_sinks, m_attn_w_o, m_attn_b_o, m_sgu_w_in, m_sgu_ln_g, m_sgu_ln_b, m_sgu_w_spatial, m_sgu_b_spatial, m_sgu_w_out, m_ffn_w_gate_up, m_ffn_w_down, v_norm_mix_pre, v_norm_mix_post, v_norm_ffn_pre, v_norm_ffn_post, v_attn_w_qkv, v_attn_b_qkv, v_attn_sinks, v_attn_w_o, v_attn_b_o, v_sgu_w_in, v_sgu_ln_g, v_sgu_ln_b, v_sgu_w_spatial, v_sgu_b_spatial, v_sgu_w_out, v_ffn_w_gate_up, v_ffn_w_down):
    s = x.shape[1]
    x0 = x.reshape(s, D_MODEL)
    target = loss_target.reshape(s, D_MODEL)
    mx, my, mc = lax.axis_index("x"), lax.axis_index("y"), lax.axis_index("c")
    chip = 2 * mx + my
    chip_arr = jnp.reshape(chip, (1,)).astype(I32)
    c_arr = jnp.reshape(mc, (1,)).astype(I32)
    jc_arr = jnp.stack([chip, mc]).astype(I32)
    zero_bias = jnp.zeros((1, D_MODEL), F32)

    def gain(p, i):
        return p[i:i + 1]

    big = [attn_w_qkv, attn_w_o, sgu_w_in, sgu_w_out, ffn_w_gate_up, ffn_w_gate_up, ffn_w_down, ffn_w_down]
    layers = [0, 0, 0, 0, 0, 1, 0, 1]
    tags = ["qkv", "wo", "win", "wout", "wgu0", "wgu1", "wd0", "wd1"]
    full = {t: place_shard(w, l, chip_arr, BF16, name=f"place_{t}") for w, l, t in zip(big, layers, tags)}
    ln_pack = _pad_rows(jnp.concatenate([sgu_ln_g, sgu_ln_b], axis=0), 16)[None]
    full["ln"] = place_shard(ln_pack, 0, chip_arr, F32, name="place_ln")

    def ici(*names):
        return gather_ici_rider([full[n] for n in names])

    def d2d(*names):
        return gather_d2d_rider([full[n] for n in names])

    def landed(names, res):
        for n, r in zip(names, res):
            full[n] = r

    cos, sin = _rope_tables(s)
    sink_rows = jnp.broadcast_to(
        jnp.repeat(attn_sinks.reshape(N_KV_HEADS, GQA_GROUP), WINDOW, axis=1)[:, :, None], (N_KV_HEADS, ROWS, LANES))
    w_sp = sgu_w_spatial.reshape(SGU_GROUPS, SGU_CHUNK, SGU_CHUNK)
    b_sp = jnp.broadcast_to(sgu_b_spatial.reshape(SGU_GROUPS, SGU_CHUNK)[:, :, None], (SGU_GROUPS, SGU_CHUNK, LANES))

    (res,) = comm_call([ici("qkv", "ln")], name="gather_first")
    landed(("qkv", "ln"), res)
    h0, (res,) = prenorm(x0, gain(norm_mix_pre, 0), name="prenorm_0", riders=[d2d("qkv", "ln")])
    landed(("qkv", "ln"), res)
    ln_g = full["ln"][:, 0, :].reshape(1, D_MODEL)
    ln_b = full["ln"][:, 1, :].reshape(1, D_MODEL)

    qkv, (res,) = qkv_proj(h0, full["qkv"], attn_b_qkv, cos, sin, name="qkv_proj", riders=[ici("wo", "wd0")])
    landed(("wo", "wd0"), res)
    o, (res_a, res_b) = attn_fwd(qkv, sink_rows, name="attn_fwd", riders=[d2d("wo", "wd0"), ici("wgu0")])
    landed(("wo", "wd0"), res_a)
    landed(("wgu0",), res_b)
    w_o = full["wo"].reshape(Q_WIDTH, D_MODEL)
    m0, (res_a, res_b) = mm_nn(o, w_o, out_dtype=F32, name="attn_out_proj", riders=[d2d("wgu0"), ici("win")])
    landed(("wgu0",), res_a)
    landed(("win",), res_b)
    (x1, h1), (res_a, res_b) = residual_norm(x0, m0, attn_b_o, gain(norm_mix_post, 0), gain(norm_ffn_pre, 0),
                                             name="residual_norm_0a", riders=[d2d("win"), ici("wout")])
    landed(("win",), res_a)
    landed(("wout",), res_b)
    (gu0, a0), (res_a, res_b) = ffn_up(h1, full["wgu0"], name="ffn_up_0", riders=[d2d("wout"), ici("wgu1")])
    landed(("wout",), res_a)
    landed(("wgu1",), res_b)
    w_d0 = full["wd0"].reshape(D_FF, D_MODEL)
    f0, (res_a, res_b) = mm_nn(a0, w_d0, out_dtype=F32, name="ffn_down_0", riders=[d2d("wgu1"), ici("wd1")])
    landed(("wgu1",), res_a)
    landed(("wd1",), res_b)
    (x2, h2), (res,) = residual_norm(x1, f0, zero_bias, gain(norm_ffn_post, 0), gain(norm_mix_pre, 1),
                                     name="residual_norm_0b", riders=[d2d("wd1")])
    landed(("wd1",), res)
    w_qkv, w_in, w_gu0, w_gu1 = full["qkv"], full["win"], full["wgu0"], full["wgu1"]
    w_out = full["wout"].reshape(D_MODEL, D_MODEL)
    w_d1 = full["wd1"].reshape(D_FF, D_MODEL)
    z = mm_nn(h2, w_in, out_dtype=F32, name="sgu_in_proj")
    y = sgu_fwd(z, ln_g, ln_b, w_sp, b_sp, name="sgu_fwd")
    m1 = mm_nn(y, w_out, out_dtype=F32, name="sgu_out_proj")
    x3, h3 = residual_norm(x2, m1, zero_bias, gain(norm_mix_post, 1), gain(norm_ffn_pre, 1), name="residual_norm_1a")
    gu1, a1 = ffn_up(h3, w_gu1, name="ffn_up_1")
    f1 = mm_nn(a1, w_d1, out_dtype=F32, name="ffn_down_1")
    dx4, df1, dg_fpost1, loss_part = loss_head(x3, f1, gain(norm_ffn_post, 1), target, name="loss_head")
    loss = lax.psum(loss_part[0, 0], ("x", "y", "c"))

    red = _GradReduce(c_arr, jc_arr, {
        "qkv": attn_w_qkv.shape[1:], "wo": attn_w_o.shape[1:], "win": sgu_w_in.shape[1:], "wout": sgu_w_out.shape[1:],
        "wgu": ffn_w_gate_up.shape, "wd": ffn_w_down.shape, "slab": (N_CHIPS, SLAB_ROWS // N_CHIPS, D_MODEL)})
    where = {"qkv": ("qkv", None), "wo": ("wo", None), "win": ("win", None), "wout": ("wout", None), "wgu0": ("wgu", 0),
             "wgu1": ("wgu", 1), "wd0": ("wd", 0), "wd1": ("wd", 1), "small": ("slab", "chip")}

    dgu1 = ffn_dact(df1, w_d1, gu1, name="ffn_dact_1")
    red.grad["wd1"] = mm_tn(a1, df1, shard_major=False, tm=256, tn=D_MODEL, name="dw_down_1").reshape(
        N_CHIPS, D_FF // N_CHIPS, D_MODEL)
    red.grad["wgu1"], (res,) = mm_tn(h3, dgu1, shard_major=True, tm=512, tn=FF_HALF, name="dw_gate_up_1",
                                     riders=[red.exchange(["wd1"])])
    red.exchanged(["wd1"], res)
    dh3, (res_a, res_b) = mm_nt(dgu1, w_gu1, out_dtype=F32, tm=512, name="dh_ffn_1",
                                riders=[red.exchange(["wgu1"]), red.scatter(["wd1"])])
    red.exchanged(["wgu1"], res_a)
    red.scattered(["wd1"], res_b, where)
    names, rider = red.broadcast([("wd", 1)])
    (dx3, dm1, dg_fpre1, dg_mpost1, _), (res,) = norm_bwd_pair(
        dx4, dh3, x3, gain(norm_ffn_pre, 1), m1, zero_bias, gain(norm_mix_post, 1), name="norm_bwd_1a", riders=[rider])
    red.broadcasted(names, res)
    dy = mm_nt(dm1, w_out, out_dtype=F32, name="dy_sgu")
    red.grad["wout"] = mm_tn(y, dm1, shard_major=False, tm=512, tn=D_MODEL, name="dw_sgu_out").reshape(
        N_CHIPS, D_MODEL // N_CHIPS, D_MODEL)
    (dz, dw_sp, db_sp, dln_g, dln_b), (res_a, res_b) = sgu_bwd(
        z, dy, ln_g, ln_b, w_sp, b_sp, name="sgu_bwd", riders=[red.scatter(["wgu1"]), red.exchange(["wout"])])
    red.scattered(["wgu1"], res_a, where)
    red.exchanged(["wout"], res_b)
    names, rider = red.broadcast([("wgu", 1)])
    red.grad["win"], (res_a, res_b) = mm_tn(h2, dz, shard_major=True, tm=D_MODEL, tn=2 * D_MODEL // N_CHIPS, name="dw_sgu_in",
                                            riders=[rider, red.scatter(["wout"])])
    red.broadcasted(names, res_a)
    red.scattered(["wout"], res_b, where)
    names, rider = red.broadcast([("wout", None)])
    dh2, (res_a, res_b) = mm_nt(dz, w_in, out_dtype=F32, tm=1024, name="dh_sgu", riders=[red.exchange(["win"]), rider])
    red.exchanged(["win"], res_a)
    red.broadcasted(names, res_b)
    (dx2, df0, dg_mpre1, dg_fpost0, _), (res,) = norm_bwd_pair(
        dx3, dh2, x2, gain(norm_mix_pre, 1), f0, zero_bias, gain(norm_ffn_post, 0), name="norm_bwd_0b",
        riders=[red.scatter(["win"])])
    red.scattered(["win"], res, where)
    names, rider = red.broadcast([("win", None)])
    dgu0, (res,) = ffn_dact(df0, w_d0, gu0, name="ffn_dact_0", riders=[rider])
    red.broadcasted(names, res)
    red.grad["wd0"] = mm_tn(a0, df0, shard_major=False, tm=256, tn=D_MODEL, name="dw_down_0").reshape(
        N_CHIPS, D_FF // N_CHIPS, D_MODEL)
    red.grad["wgu0"], (res,) = mm_tn(h1, dgu0, shard_major=True, tm=512, tn=FF_HALF, name="dw_gate_up_0",
                                     riders=[red.exchange(["wd0"])])
    red.exchanged(["wd0"], res)
    dh1, (res_a, res_b) = mm_nt(dgu0, w_gu0, out_dtype=F32, tm=512, name="dh_ffn_0",
                                riders=[red.exchange(["wgu0"]), red.scatter(["wd0"])])
    red.exchanged(["wgu0"], res_a)
    red.scattered(["wd0"], res_b, where)
    names, rider = red.broadcast([("wd", 0)])
    (dx1, dm0, dg_fpre0, dg_mpost0, db_o), (res,) = norm_bwd_pair(
        dx2, dh1, x1, gain(norm_ffn_pre, 0), m0, attn_b_o, gain(norm_mix_post, 0), name="norm_bwd_0a", riders=[rider])
    red.broadcasted(names, res)
    do = mm_nt(dm0, w_o, out_dtype=BF16, name="do_attn")
    red.grad["wo"] = mm_tn(o, dm0, shard_major=False, tm=512, tn=D_MODEL, name="dw_attn_out").reshape(
        N_CHIPS, Q_WIDTH // N_CHIPS, D_MODEL)
    (dq, dkc, dkp, dvc, dvp, dsink), (res_a, res_b) = attn_bwd(
        qkv, sink_rows, do, name="attn_bwd", riders=[red.scatter(["wgu0"]), red.exchange(["wo"])])
    red.scattered(["wgu0"], res_a, where)
    red.exchanged(["wo"], res_b)
    names, rider = red.broadcast([("wgu", 0)])
    (dqkv, db_qkv), (res_a, res_b) = rope_bwd(dq, dkc, dkp, dvc, dvp, cos, sin, name="rope_bwd",
                                              riders=[rider, red.scatter(["wo"])])
    red.broadcasted(names, res_a)
    red.scattered(["wo"], res_b, where)
    names, rider = red.broadcast([("wo", None)])
    red.grad["qkv"], (res,) = mm_tn(h0, dqkv, shard_major=True, tm=D_MODEL, tn=QKV_WIDTH // N_CHIPS, name="dw_qkv",
                                    riders=[rider])
    red.broadcasted(names, res)
    dh0, (res,) = mm_nt(dqkv, w_qkv, out_dtype=F32, tm=1024, name="dh_attn", riders=[red.exchange(["qkv"])])
    red.exchanged(["qkv"], res)
    grad_x, dg_mpre0 = norm_bwd_last(dx1, dh0, x0, gain(norm_mix_pre, 0), name="norm_bwd_in")

    norm_grads = [jnp.concatenate(p, axis=0) for p in
                  ((dg_mpre0, dg_mpre1), (dg_mpost0, dg_mpost1), (dg_fpre0, dg_fpre1), (dg_fpost0, dg_fpost1))]
    red.grad["small"] = _pack_small(norm_grads, db_qkv, db_o, dsink[:, :, 0, 0], db_sp[:, :, 0], dln_g, dln_b, dw_sp)
    res_a, res_b = comm_call([red.scatter(["qkv"]), red.exchange(["small"])], name="tail_1")
    red.scattered(["qkv"], res_a, where)
    red.exchanged(["small"], res_b)
    names, rider = red.broadcast([("qkv", None)])
    res_a, res_b = comm_call([red.scatter(["small"]), rider], name="tail_2")
    red.scattered(["small"], res_a, where)
    red.broadcasted(names, res_b)
    ((slab_full,),) = comm_call([allcast_rider(red.dest["slab"])], name="tail_3")
    g_qkv, g_wo, g_win, g_wout, g_wgu, g_wd = (red.dest[n] for n in ("qkv", "wo", "win", "wout", "wgu", "wd"))
    g_norms, g_bqkv, g_bo, g_sinks, g_bsp, g_lng, g_lnb, g_wsp = _unpack_small(slab_full, chip)

    def big_update(w, g, m, v, tag):
        return adamw(w, g.reshape(w.shape), m, v, name=f"adamw_{tag}")

    upd = {
        "attn_w_qkv": big_update(attn_w_qkv, g_qkv, m_attn_w_qkv, v_attn_w_qkv, "qkv"),
        "attn_w_o": big_update(attn_w_o, g_wo, m_attn_w_o, v_attn_w_o, "wo"),
        "sgu_w_in": big_update(sgu_w_in, g_win, m_sgu_w_in, v_sgu_w_in, "win"),
        "sgu_w_out": big_update(sgu_w_out, g_wout, m_sgu_w_out, v_sgu_w_out, "wout"),
        "ffn_w_gate_up": big_update(ffn_w_gate_up, g_wgu, m_ffn_w_gate_up, v_ffn_w_gate_up, "wgu"),
        "ffn_w_down": big_update(ffn_w_down, g_wd, m_ffn_w_down, v_ffn_w_down, "wd"),
    }
    small_names = ["norm_mix_pre", "norm_mix_post", "norm_ffn_pre", "norm_ffn_post", "attn_b_qkv", "attn_sinks", "attn_b_o",
                   "sgu_ln_g", "sgu_ln_b", "sgu_w_spatial", "sgu_b_spatial"]
    small_w = [norm_mix_pre, norm_mix_post, norm_ffn_pre, norm_ffn_post, attn_b_qkv, attn_sinks, attn_b_o, sgu_ln_g, sgu_ln_b,
               sgu_w_spatial, sgu_b_spatial]
    small_m = [m_norm_mix_pre, m_norm_mix_post, m_norm_ffn_pre, m_norm_ffn_post, m_attn_b_qkv, m_attn_sinks, m_attn_b_o,
               m_sgu_ln_g, m_sgu_ln_b, m_sgu_w_spatial, m_sgu_b_spatial]
    small_v = [v_norm_mix_pre, v_norm_mix_post, v_norm_ffn_pre, v_norm_ffn_post, v_attn_b_qkv, v_attn_sinks, v_attn_b_o,
               v_sgu_ln_g, v_sgu_ln_b, v_sgu_w_spatial, v_sgu_b_spatial]
    small_g = g_norms + [g_bqkv, g_sinks, g_bo, g_lng, g_lnb, g_wsp, g_bsp]

    def flat2(a):
        return a.reshape(-1, a.shape[-1])

    res = adamw_small([flat2(a) for a in small_w], [flat2(a) for a in small_g], [flat2(a) for a in small_m],
                      [flat2(a) for a in small_v], name="adamw_small")
    for i, nm in enumerate(small_names):
        upd[nm] = tuple(r[i].reshape(small_w[i].shape) for r in res)

    order = ["norm_mix_pre", "norm_mix_post", "norm_ffn_pre", "norm_ffn_post", "attn_w_qkv", "attn_b_qkv", "attn_sinks",
             "attn_w_o", "attn_b_o", "sgu_w_in", "sgu_ln_g", "sgu_ln_b", "sgu_w_spatial", "sgu_b_spatial", "sgu_w_out",
             "ffn_w_gate_up", "ffn_w_down"]
    outs = [loss, grad_x.reshape(1, s, D_MODEL)]
    for part in range(4):
        outs += [upd[nm][part] for nm in order]
    return tuple(outs)
```

```python
import types

import jax
import jax.numpy as jnp
from jax import lax
from jax.experimental import pallas as pl
from jax.experimental.pallas import tpu as pltpu

F32 = jnp.float32
BF16 = jnp.bfloat16
I32 = jnp.int32

D_MODEL = 1024
HEAD_DIM = 64
N_Q_HEADS = 16
N_KV_HEADS = 4
GQA_GROUP = 4
WINDOW = 128
Q_WIDTH = 1024
KV_WIDTH = 256
QKV_WIDTH = 1536
ROPE_THETA = 10000.0
SGU_GROUPS = 8
SGU_CHUNK = 128
D_FF = 2816
FF_HALF = D_FF // 2
EPS = 1e-6
N_CHIPS = 4
LANES = 128

ADAM_LR = 0.001
ADAM_B1 = 0.9
ADAM_B2 = 0.999
ADAM_EPS = 1e-08
ADAM_WD = 0.01
ADAM_STEP = 10

VMEM_LIMIT = 52 * 1024 * 1024
MESH = pl.DeviceIdType.MESH
NEG = -1e30
NT_DIMS = (((1,), (1,)), ((), ()))
TN_DIMS = (((0,), (0,)), ((), ()))
NN_DIMS = (((1,), (0,)), ((), ()))
ANY = pl.BlockSpec(memory_space=pl.ANY)


def _row_tile(s, want):
    return want if s % want == 0 else s


def _call(body, *, name, grid=(), in_specs=(), out_specs=(), out_shape=(), scratch_shapes=(), operands=(), prefetch=(),
          aliases=None, riders=(), sem=None):
    n_pre, n_in, n_out, n_scr = len(prefetch), len(operands), len(out_shape), len(scratch_shapes)
    in_specs, out_specs, out_shape = list(in_specs), list(out_specs), list(out_shape)
    operands, scratch_shapes = list(operands), list(scratch_shapes)
    io_alias = {n_pre + i: o for i, o in (aliases or {}).items()}
    for r in riders:
        base_in, base_out = n_pre + len(operands), len(out_shape)
        operands += list(r.inputs)
        in_specs += [ANY] * len(r.inputs)
        for pos, i in enumerate(r.aliased):
            io_alias[base_in + i] = base_out + pos
            out_shape.append(jax.ShapeDtypeStruct(r.inputs[i].shape, r.inputs[i].dtype))
        out_shape += list(r.fresh)
        out_specs += [ANY] * (len(r.aliased) + len(r.fresh))
        scratch_shapes += [pltpu.SemaphoreType.DMA((r.nsem,)), pltpu.SemaphoreType.DMA((r.nsem,))]

    def wrapped(*refs):
        pre, p = refs[:n_pre], n_pre
        core_in, p = refs[p:p + n_in], p + n_in
        r_in = []
        for r in riders:
            r_in.append(refs[p:p + len(r.inputs)])
            p += len(r.inputs)
        core_out, p = refs[p:p + n_out], p + n_out
        r_out = []
        for r in riders:
            k = len(r.aliased) + len(r.fresh)
            r_out.append(refs[p:p + k])
            p += k
        core_scr, p = refs[p:p + n_scr], p + n_scr
        r_sem = [refs[p + 2 * i:p + 2 * i + 2] for i in range(len(riders))]

        def edge(at_last, fns):
            def run():
                for i, r in enumerate(riders):
                    getattr(r, fns)(r_in[i], r_out[i], r_sem[i][0], r_sem[i][1])
            if not riders:
                return
            if not grid:
                run()
                return
            cond = None
            for d, n in enumerate(grid):
                c = pl.program_id(d) == (n - 1 if at_last else 0)
                cond = c if cond is None else jnp.logical_and(cond, c)
            pl.when(cond)(run)

        edge(False, "start")
        if body is not None:
            body(*pre, *core_in, *core_out, *core_scr)
        edge(True, "finish")

    if sem is None or riders:
        sem = ("arbitrary",) * len(grid)
    kwargs = dict(out_shape=out_shape, input_output_aliases=io_alias, name=name)
    if grid:
        kwargs["compiler_params"] = pltpu.CompilerParams(dimension_semantics=sem, vmem_limit_bytes=VMEM_LIMIT)
    if n_pre:
        kwargs["grid_spec"] = pltpu.PrefetchScalarGridSpec(
            num_scalar_prefetch=n_pre, grid=grid, in_specs=in_specs, out_specs=out_specs, scratch_shapes=scratch_shapes)
    else:
        kwargs.update(grid=grid, in_specs=in_specs, out_specs=out_specs, scratch_shapes=scratch_shapes)
    res = pl.pallas_call(wrapped, **kwargs)(*prefetch, *operands)
    core, rest, rider_res = list(res[:n_out]), list(res[n_out:]), []
    for r in riders:
        k = len(r.aliased) + len(r.fresh)
        rider_res.append(rest[:k])
        rest = rest[k:]
    return core, rider_res


def _mm_call(*, grid, in_specs, out_spec, out_shape, dims, nk, kaxis, acc_shape, name, operands, riders=()):
    out_dtype = out_shape.dtype

    def body(a_ref, b_ref, o_ref, *scratch):
        p = lax.dot_general(a_ref[...].astype(BF16), b_ref[...].astype(BF16), dims, preferred_element_type=F32)
        if nk == 1:
            o_ref[...] = p.astype(out_dtype)
        else:
            acc = scratch[0]
            kk = pl.program_id(kaxis)

            @pl.when(kk == 0)
            def _():
                acc[...] = p

            @pl.when(kk > 0)
            def _():
                acc[...] += p

            @pl.when(kk == nk - 1)
            def _():
                o_ref[...] = acc[...].astype(out_dtype)

    sem = ["parallel"] * len(grid)
    if nk > 1:
        sem[kaxis] = "arbitrary"
    (out,), rider_res = _call(
        body, grid=grid, in_specs=in_specs, out_specs=[out_spec], out_shape=[out_shape],
        scratch_shapes=[pltpu.VMEM(acc_shape, F32)] if nk > 1 else [], operands=operands, name=name, riders=riders,
        sem=tuple(sem))
    return (out, rider_res) if riders else out


def mm_nn(a, w, *, out_dtype, name, tm=512, tn=512, riders=()):
    m, k = a.shape
    tm = _row_tile(m, tm)
    if w.ndim == 3:
        ns = w.shape[2]
        grid = (N_CHIPS, m // tm)
        w_spec = pl.BlockSpec((None, k, ns), lambda j, i: (j, 0, 0))
        o_spec = pl.BlockSpec((tm, ns), lambda j, i: (i, j))
        n = N_CHIPS * ns
    else:
        n = w.shape[1]
        grid = (n // tn, m // tm)
        w_spec = pl.BlockSpec((k, tn), lambda j, i: (0, j))
        o_spec = pl.BlockSpec((tm, tn), lambda j, i: (i, j))
    return _mm_call(grid=grid, in_specs=[pl.BlockSpec((tm, k), lambda j, i: (i, 0)), w_spec], out_spec=o_spec,
                    out_shape=jax.ShapeDtypeStruct((m, n), out_dtype), dims=NN_DIMS, nk=1, kaxis=0, acc_shape=None,
                    name=name, operands=(a, w), riders=riders)


def mm_nt(a, w, *, out_dtype, name, tm=512, tn=512, riders=()):
    if w.ndim == 2:
        m, n = a.shape
        kout = w.shape[0]
        tm = _row_tile(m, tm)
        return _mm_call(grid=(kout // tn, m // tm),
                        in_specs=[pl.BlockSpec((tm, n), lambda j, i: (i, 0)), pl.BlockSpec((tn, n), lambda j, i: (j, 0))],
                        out_spec=pl.BlockSpec((tm, tn), lambda j, i: (i, j)),
                        out_shape=jax.ShapeDtypeStruct((m, kout), out_dtype), dims=NT_DIMS, nk=1, kaxis=0,
                        acc_shape=None, name=name, operands=(a, w), riders=riders)
    _, kout, ns = w.shape
    planes = a.ndim == 3
    m = a.shape[1] if planes else a.shape[0]
    tm = _row_tile(m, tm)
    a_spec = pl.BlockSpec((2, tm, 2 * ns), lambda i: (0, i, 0)) if planes else pl.BlockSpec((tm, N_CHIPS * ns), lambda i: (i, 0))

    def body(a_ref, w0, w1, w2, w3, o_ref):
        acc = None
        for j, w_ref in enumerate((w0, w1, w2, w3)):
            if planes:
                a_j = a_ref[j // 2, :, (j % 2) * ns:(j % 2 + 1) * ns]
            else:
                a_j = a_ref[:, j * ns:(j + 1) * ns]
            p = lax.dot_general(a_j, w_ref[...], NT_DIMS, preferred_element_type=F32)
            acc = p if acc is None else acc + p
        o_ref[...] = acc.astype(out_dtype)

    def shard(j):
        return pl.BlockSpec((None, kout, ns), lambda i: (j, 0, 0))

    (out,), rider_res = _call(
        body, grid=(m // tm,), in_specs=[a_spec] + [shard(j) for j in range(N_CHIPS)],
        out_specs=[pl.BlockSpec((tm, kout), lambda i: (i, 0))], out_shape=[jax.ShapeDtypeStruct((m, kout), out_dtype)],
        operands=(a, w, w, w, w), sem=("parallel",), name=name, riders=riders)
    return (out, rider_res) if riders else out


def mm_tn(a, b, *, shard_major, name, tm, tn, tk=None, out_dtype=BF16, riders=()):
    s, m = a.shape
    tk = s if tk is None else _row_tile(s, tk)
    if b.ndim == 3:
        n = 2 * b.shape[2]
        b_spec = pl.BlockSpec((None, tk, tn), lambda j, i, kk: (j // 2, kk, j % 2))
    else:
        n = b.shape[1]
        b_spec = pl.BlockSpec((tk, tn), lambda j, i, kk: (kk, j))
    if shard_major:
        assert tn == n // N_CHIPS
        o_spec = pl.BlockSpec((None, tm, tn), lambda j, i, kk: (j, i, 0))
        o_shape = jax.ShapeDtypeStruct((N_CHIPS, m, tn), out_dtype)
    else:
        o_spec = pl.BlockSpec((tm, tn), lambda j, i, kk: (i, j))
        o_shape = jax.ShapeDtypeStruct((m, n), out_dtype)
    return _mm_call(grid=(n // tn, m // tm, s // tk),
                    in_specs=[pl.BlockSpec((tk, tm), lambda j, i, kk: (kk, i)), b_spec], out_spec=o_spec,
                    out_shape=o_shape, dims=TN_DIMS, nk=s // tk, kaxis=2, acc_shape=(tm, tn), name=name, operands=(a, b),
                    riders=riders)


def _rstd(x):
    return lax.rsqrt(jnp.mean(x * x, axis=-1, keepdims=True) + EPS)


def _rms_bwd(dy, x, g):
    r = _rstd(x)
    xhat = x * r
    gy = dy * g
    dx = r * (gy - xhat * jnp.mean(gy * xhat, axis=-1, keepdims=True))
    return dx, jnp.sum(dy * xhat, axis=0, keepdims=True)


def _accum(ref, val, first):
    @pl.when(first)
    def _():
        ref[...] = val

    @pl.when(jnp.logical_not(first))
    def _():
        ref[...] += val


def _row_spec(tm, width):
    return pl.BlockSpec((tm, width), lambda i: (i, 0))


def _vec_spec(width):
    return pl.BlockSpec((1, width), lambda i: (0, 0))


def _ret(core, rider_res, riders):
    core = core[0] if len(core) == 1 else core
    return (core, rider_res) if riders else core


def prenorm(x, g, *, name, tm=256, riders=()):
    s = x.shape[0]
    tm = _row_tile(s, tm)

    def body(x_ref, g_ref, h_ref):
        xv = x_ref[...]
        h_ref[...] = (xv * _rstd(xv) * g_ref[...]).astype(BF16)

    core, rr = _call(
        body, grid=(s // tm,), in_specs=[_row_spec(tm, D_MODEL), _vec_spec(D_MODEL)], out_specs=[_row_spec(tm, D_MODEL)],
        out_shape=[jax.ShapeDtypeStruct((s, D_MODEL), BF16)], operands=(x, g), sem=("parallel",), name=name, riders=riders)
    return _ret(core, rr, riders)


def proj_residual_norm(a, w, x, bias, g_post, g_next, *, name, tm=256, riders=()):
    s, k = a.shape
    tm = _row_tile(s, tm)

    def body(a_ref, w_ref, x_ref, b_ref, gp_ref, gn_ref, xo_ref, h_ref, m_ref):
        mv = jnp.dot(a_ref[...], w_ref[...], preferred_element_type=F32) + b_ref[...]
        m_ref[...] = mv.astype(BF16)
        xn = x_ref[...] + mv * _rstd(mv) * gp_ref[...]
        xo_ref[...] = xn
        h_ref[...] = (xn * _rstd(xn) * gn_ref[...]).astype(BF16)

    row, vec = _row_spec(tm, D_MODEL), _vec_spec(D_MODEL)
    core, rr = _call(
        body, grid=(s // tm,),
        in_specs=[_row_spec(tm, k), pl.BlockSpec((k, D_MODEL), lambda i: (0, 0)), row, vec, vec, vec], out_specs=[row, row, row],
        out_shape=[jax.ShapeDtypeStruct((s, D_MODEL), F32), jax.ShapeDtypeStruct((s, D_MODEL), BF16),
                   jax.ShapeDtypeStruct((s, D_MODEL), BF16)],
        operands=(a, w, x, bias, g_post, g_next), sem=("parallel",), name=name, riders=riders)
    return _ret(core, rr, riders)


def proj_loss_head(a, w, x, g_post, target, *, name, tm=256, riders=()):
    s, k = a.shape
    tm = _row_tile(s, tm)

    def body(a_ref, w_ref, x_ref, g_ref, t_ref, dx_ref, df_ref, dg_ref, loss_ref):
        first = pl.program_id(0) == 0
        fv = jnp.dot(a_ref[...], w_ref[...], preferred_element_type=F32)
        g = g_ref[...]
        err = x_ref[...] + fv * _rstd(fv) * g - t_ref[...]
        dx = err * (1.0 / D_MODEL)
        dx_ref[...] = dx
        df, dg = _rms_bwd(dx, fv, g)
        df_ref[...] = df.astype(BF16)
        _accum(dg_ref, dg, first)
        part = jnp.sum(jnp.sum(err * err, axis=-1, keepdims=True), axis=0, keepdims=True) * (0.5 / D_MODEL)
        _accum(loss_ref, jnp.broadcast_to(part, (8, LANES)), first)

    row, vec = _row_spec(tm, D_MODEL), _vec_spec(D_MODEL)
    core, rr = _call(
        body, grid=(s // tm,), in_specs=[_row_spec(tm, k), pl.BlockSpec((k, D_MODEL), lambda i: (0, 0)), row, vec, row],
        out_specs=[row, row, vec, pl.BlockSpec((8, LANES), lambda i: (0, 0))],
        out_shape=[jax.ShapeDtypeStruct((s, D_MODEL), F32), jax.ShapeDtypeStruct((s, D_MODEL), BF16),
                   jax.ShapeDtypeStruct((1, D_MODEL), F32), jax.ShapeDtypeStruct((8, LANES), F32)],
        operands=(a, w, x, g_post, target), name=name, riders=riders)
    return _ret(core, rr, riders)


def dh_norm_bwd_pair(a, w, dres, x, g_pre, m, g_post, *, name, tm=256, riders=()):
    _, kout, ns = w.shape
    planes = a.ndim == 3
    s = x.shape[0]
    tm = _row_tile(s, tm)
    a_spec = pl.BlockSpec((2, tm, 2 * ns), lambda i: (0, i, 0)) if planes else pl.BlockSpec((tm, N_CHIPS * ns), lambda i: (i, 0))

    def body(a_ref, w0, w1, w2, w3, dres_ref, x_ref, gpre_ref, m_ref, gpost_ref, dx_ref, dm_ref, dgpre_ref, dgpost_ref, db_ref):
        first = pl.program_id(0) == 0
        dh = None
        for j, w_ref in enumerate((w0, w1, w2, w3)):
            a_j = a_ref[j // 2, :, (j % 2) * ns:(j % 2 + 1) * ns] if planes else a_ref[:, j * ns:(j + 1) * ns]
            p = lax.dot_general(a_j, w_ref[...], NT_DIMS, preferred_element_type=F32)
            dh = p if dh is None else dh + p
        d1, dgpre = _rms_bwd(dh, x_ref[...], gpre_ref[...])
        dx = dres_ref[...] + d1
        dx_ref[...] = dx
        dm, dgpost = _rms_bwd(dx, m_ref[...].astype(F32), gpost_ref[...])
        dm_ref[...] = dm.astype(BF16)
        _accum(dgpre_ref, dgpre, first)
        _accum(dgpost_ref, dgpost, first)
        _accum(db_ref, jnp.sum(dm, axis=0, keepdims=True), first)

    def shard(j):
        return pl.BlockSpec((None, kout, ns), lambda i: (j, 0, 0))

    row, vec = _row_spec(tm, D_MODEL), _vec_spec(D_MODEL)
    vshape = jax.ShapeDtypeStruct((1, D_MODEL), F32)
    core, rr = _call(
        body, grid=(s // tm,), in_specs=[a_spec] + [shard(j) for j in range(N_CHIPS)] + [row, row, vec, row, vec],
        out_specs=[row, row, vec, vec, vec],
        out_shape=[jax.ShapeDtypeStruct((s, D_MODEL), F32), jax.ShapeDtypeStruct((s, D_MODEL), BF16), vshape, vshape, vshape],
        operands=(a, w, w, w, w, dres, x, g_pre, m, g_post), name=name, riders=riders)
    return _ret(core, rr, riders)


def norm_bwd_last(dres, dh, x, g_pre, *, name, tm=256, riders=()):
    s = x.shape[0]
    tm = _row_tile(s, tm)

    def body(dres_ref, dh_ref, x_ref, g_ref, dx_ref, dg_ref):
        d1, dg = _rms_bwd(dh_ref[...], x_ref[...], g_ref[...])
        dx_ref[...] = dres_ref[...] + d1
        _accum(dg_ref, dg, pl.program_id(0) == 0)

    row, vec = _row_spec(tm, D_MODEL), _vec_spec(D_MODEL)
    core, rr = _call(
        body, grid=(s // tm,), in_specs=[row, row, row, vec], out_specs=[row, vec],
        out_shape=[jax.ShapeDtypeStruct((s, D_MODEL), F32), jax.ShapeDtypeStruct((1, D_MODEL), F32)],
        operands=(dres, dh, x, g_pre), name=name, riders=riders)
    return _ret(core, rr, riders)


def _rope_tables(s):
    half = HEAD_DIM // 2
    inv_freq = ROPE_THETA ** (-(jnp.arange(half, dtype=F32) * 2.0) / HEAD_DIM)
    ang = jnp.arange(s, dtype=I32).astype(F32)[:, None] * inv_freq[None, :]
    cos, sin = jnp.cos(ang), jnp.sin(ang)
    return jnp.tile(cos, (1, 4)), jnp.concatenate([-sin, sin, -sin, sin], axis=1)


def _swap_halves(x):
    lane = lax.broadcasted_iota(I32, x.shape, 1)
    return jnp.where((lane & (HEAD_DIM - 1)) < HEAD_DIM // 2, pltpu.roll(x, LANES - 32, 1), pltpu.roll(x, 32, 1))


N_ROPE_BLOCKS = (Q_WIDTH + KV_WIDTH) // LANES


def qkv_proj(h, w, bias, cos, sin, *, name, tm=512, riders=()):
    s, k = h.shape
    ns = w.shape[2]
    tm = _row_tile(s, tm)

    def body(h_ref, w_ref, b_ref, c_ref, s_ref, o_ref):
        j = pl.program_id(0)
        p = jnp.dot(h_ref[...], w_ref[...], preferred_element_type=F32) + b_ref[...]
        cosv, sinv = c_ref[...], s_ref[...]
        for blk in range(ns // LANES):
            xb = p[:, blk * LANES:(blk + 1) * LANES]
            roped = xb * cosv + _swap_halves(xb) * sinv
            is_qk = j * (ns // LANES) + blk < N_ROPE_BLOCKS
            o_ref[:, blk * LANES:(blk + 1) * LANES] = jnp.where(is_qk, roped, xb).astype(BF16)

    core, rr = _call(
        body, grid=(N_CHIPS, s // tm),
        in_specs=[pl.BlockSpec((tm, k), lambda j, i: (i, 0)), pl.BlockSpec((None, k, ns), lambda j, i: (j, 0, 0)),
                  pl.BlockSpec((1, ns), lambda j, i: (0, j)), pl.BlockSpec((tm, LANES), lambda j, i: (i, 0)),
                  pl.BlockSpec((tm, LANES), lambda j, i: (i, 0))],
        out_specs=[pl.BlockSpec((tm, ns), lambda j, i: (i, j))], out_shape=[jax.ShapeDtypeStruct((s, N_CHIPS * ns), BF16)],
        operands=(h, w, bias, cos, sin), sem=("parallel", "parallel"), name=name, riders=riders)
    return _ret(core, rr, riders)


def rope_bwd(dq, dkc, dkp, dvc, dvp, cos, sin, *, name, riders=()):
    s = dq.shape[0]
    tm = WINDOW
    nb = s // tm

    def body(dq_ref, dkc_ref, dkp_ref, dvc_ref, dvp_ref, c_ref, s_ref, o_ref, db_ref):
        i = pl.program_id(0)
        has_next = (i < nb - 1).astype(F32)
        cosv, sinv = c_ref[...], s_ref[...]
        for blk in range(QKV_WIDTH // LANES):
            if blk < Q_WIDTH // LANES:
                g = dq_ref[:, blk * LANES:(blk + 1) * LANES].astype(F32)
            else:
                own, nxt = (dkc_ref, dkp_ref) if blk < N_ROPE_BLOCKS else (dvc_ref, dvp_ref)
                cols = slice((blk % 2) * LANES, (blk % 2 + 1) * LANES)
                g = own[:, cols].astype(F32) + has_next * nxt[:, cols].astype(F32)
            if blk < N_ROPE_BLOCKS:
                g = g * cosv + _swap_halves(g * sinv)
            o_ref[:, blk * LANES:(blk + 1) * LANES] = g.astype(BF16)
            part = jnp.sum(g, axis=0, keepdims=True)

            @pl.when(i == 0)
            def _():
                db_ref[:, blk * LANES:(blk + 1) * LANES] = part

            @pl.when(i > 0)
            def _():
                db_ref[:, blk * LANES:(blk + 1) * LANES] += part

    own_spec = _row_spec(tm, KV_WIDTH)
    next_spec = pl.BlockSpec((tm, KV_WIDTH), lambda i: (jnp.minimum(i + 1, nb - 1), 0))
    core, rr = _call(
        body, grid=(nb,),
        in_specs=[_row_spec(tm, Q_WIDTH), own_spec, next_spec, own_spec, next_spec, _row_spec(tm, LANES), _row_spec(tm, LANES)],
        out_specs=[_row_spec(tm, QKV_WIDTH), _vec_spec(QKV_WIDTH)],
        out_shape=[jax.ShapeDtypeStruct((s, QKV_WIDTH), BF16), jax.ShapeDtypeStruct((1, QKV_WIDTH), F32)],
        operands=(dq, dkc, dkp, dvc, dvp, cos, sin), name=name, riders=riders)
    return _ret(core, rr, riders)


ROWS = GQA_GROUP * WINDOW


def _prev_slots():
    qpos = lax.broadcasted_iota(I32, (ROWS, WINDOW), 0) & (WINDOW - 1)
    kpos = lax.broadcasted_iota(I32, (ROWS, WINDOW), 1)
    return kpos > qpos


def _head_cols(ref, head):
    return ref[:, head * HEAD_DIM:(head + 1) * HEAD_DIM]


def _stack_heads(ref, h):
    return jnp.concatenate([_head_cols(ref, GQA_GROUP * h + g) for g in range(GQA_GROUP)], axis=0)


def _band(prev_ref, cur_ref, h):
    return jnp.concatenate([_head_cols(prev_ref, h), _head_cols(cur_ref, h)], axis=0)


def _pick(prev, band):
    return jnp.where(prev, band[:, :WINDOW], band[:, WINDOW:])


def _spread(prev, x):
    return jnp.concatenate([jnp.where(prev, x, 0.0), jnp.where(prev, 0.0, x)], axis=1).astype(BF16)


def _attn_probs(q, kband, sink, prev, has_prev):
    scale = HEAD_DIM ** -0.5
    s_band = lax.dot_general(q, kband, NT_DIMS, preferred_element_type=F32)
    s = jnp.where(prev, jnp.where(has_prev, s_band[:, :WINDOW], NEG), s_band[:, WINDOW:]) * scale
    m = jnp.maximum(jnp.max(s, axis=-1, keepdims=True), sink)
    e, es = jnp.exp(s - m), jnp.exp(sink - m)
    inv = 1.0 / (jnp.sum(e, axis=-1, keepdims=True) + es)
    return e * inv, es * inv


def _attn_specs(nb):
    kcol, vcol = Q_WIDTH // KV_WIDTH, Q_WIDTH // KV_WIDTH + 1
    q_spec = pl.BlockSpec((WINDOW, Q_WIDTH), lambda n: (n, 0))
    return [q_spec,
            pl.BlockSpec((WINDOW, KV_WIDTH), lambda n: (n, kcol)),
            pl.BlockSpec((WINDOW, KV_WIDTH), lambda n: (jnp.maximum(n - 1, 0), kcol)),
            pl.BlockSpec((WINDOW, KV_WIDTH), lambda n: (n, vcol)),
            pl.BlockSpec((WINDOW, KV_WIDTH), lambda n: (jnp.maximum(n - 1, 0), vcol)),
            pl.BlockSpec((N_KV_HEADS, ROWS, LANES), lambda n: (0, 0, 0))]


def attn_fwd(qkv, sink_rows, *, name, riders=()):
    s = qkv.shape[0]

    def body(q_ref, kc_ref, kp_ref, vc_ref, vp_ref, sink_ref, o_ref):
        prev = _prev_slots()
        has_prev = pl.program_id(0) > 0
        for h in range(N_KV_HEADS):
            p, _ = _attn_probs(_stack_heads(q_ref, h), _band(kp_ref, kc_ref, h), sink_ref[h], prev, has_prev)
            o = jnp.dot(_spread(prev, p), _band(vp_ref, vc_ref, h), preferred_element_type=F32)
            for g in range(GQA_GROUP):
                head = GQA_GROUP * h + g
                o_ref[:, head * HEAD_DIM:(head + 1) * HEAD_DIM] = o[g * WINDOW:(g + 1) * WINDOW].astype(BF16)

    core, rr = _call(
        body, grid=(s // WINDOW,), in_specs=_attn_specs(s // WINDOW), out_specs=[pl.BlockSpec((WINDOW, Q_WIDTH), lambda n: (n, 0))],
        out_shape=[jax.ShapeDtypeStruct((s, Q_WIDTH), BF16)], operands=(qkv, qkv, qkv, qkv, qkv, sink_rows), sem=("parallel",),
        name=name, riders=riders)
    return _ret(core, rr, riders)


def attn_bwd(qkv, sink_rows, do, *, name, riders=()):
    s = qkv.shape[0]

    def body(q_ref, kc_ref, kp_ref, vc_ref, vp_ref, sink_ref, do_ref, dq_ref, dkc_ref, dkp_ref, dvc_ref, dvp_ref, dsink_ref):
        n = pl.program_id(0)
        prev = _prev_slots()
        scale = HEAD_DIM ** -0.5
        parts = []
        for h in range(N_KV_HEADS):
            qv, dov = _stack_heads(q_ref, h), _stack_heads(do_ref, h)
            kband, vband = _band(kp_ref, kc_ref, h), _band(vp_ref, vc_ref, h)
            p, ps = _attn_probs(qv, kband, sink_ref[h], prev, n > 0)
            dp = _pick(prev, lax.dot_general(dov, vband, NT_DIMS, preferred_element_type=F32))
            delta = jnp.sum(p * dp, axis=-1, keepdims=True)
            ds_band = _spread(prev, p * (dp - delta) * scale)
            p_band = _spread(prev, p)
            dq = jnp.dot(ds_band, kband, preferred_element_type=F32)
            dk = lax.dot_general(qv, ds_band, TN_DIMS, preferred_element_type=F32).T.astype(BF16)
            dv = lax.dot_general(dov, p_band, TN_DIMS, preferred_element_type=F32).T.astype(BF16)
            cols = slice(h * HEAD_DIM, (h + 1) * HEAD_DIM)
            dkp_ref[:, cols], dkc_ref[:, cols] = dk[:WINDOW], dk[WINDOW:]
            dvp_ref[:, cols], dvc_ref[:, cols] = dv[:WINDOW], dv[WINDOW:]
            dsink = -(ps * delta)
            for g in range(GQA_GROUP):
                head = GQA_GROUP * h + g
                dq_ref[:, head * HEAD_DIM:(head + 1) * HEAD_DIM] = dq[g * WINDOW:(g + 1) * WINDOW].astype(BF16)
                parts.append(jnp.broadcast_to(jnp.sum(dsink[g * WINDOW:(g + 1) * WINDOW], axis=0, keepdims=True), (8, LANES)))

        @pl.when(n == 0)
        def _():
            for i, part in enumerate(parts):
                dsink_ref[i // GQA_GROUP, i % GQA_GROUP] = part

        @pl.when(n > 0)
        def _():
            for i, part in enumerate(parts):
                dsink_ref[i // GQA_GROUP, i % GQA_GROUP] += part

    rows_q = pl.BlockSpec((WINDOW, Q_WIDTH), lambda n: (n, 0))
    rows_kv = pl.BlockSpec((WINDOW, KV_WIDTH), lambda n: (n, 0))
    kv_shape = jax.ShapeDtypeStruct((s, KV_WIDTH), BF16)
    core, rr = _call(
        body, grid=(s // WINDOW,), in_specs=_attn_specs(s // WINDOW) + [rows_q],
        out_specs=[rows_q, rows_kv, rows_kv, rows_kv, rows_kv,
                   pl.BlockSpec((N_KV_HEADS, GQA_GROUP, 8, LANES), lambda n: (0, 0, 0, 0))],
        out_shape=[jax.ShapeDtypeStruct((s, Q_WIDTH), BF16), kv_shape, kv_shape, kv_shape, kv_shape,
                   jax.ShapeDtypeStruct((N_KV_HEADS, GQA_GROUP, 8, LANES), F32)],
        operands=(qkv, qkv, qkv, qkv, qkv, sink_rows, do), sem=("arbitrary",), name=name, riders=riders)
    return _ret(core, rr, riders)


GELU_C = 0.7978845608028654
GELU_A = 0.044715


def _gelu(x):
    return 0.5 * x * (1.0 + jnp.tanh(GELU_C * (x + GELU_A * x * x * x)))


def _gelu_grad(x):
    t = jnp.tanh(GELU_C * (x + GELU_A * x * x * x))
    return 0.5 * (1.0 + t) + 0.5 * x * (1.0 - t * t) * GELU_C * (1.0 + 3.0 * GELU_A * x * x)


def _tril_bf16(w):
    row = lax.broadcasted_iota(I32, (SGU_CHUNK, SGU_CHUNK), 0)
    col = lax.broadcasted_iota(I32, (SGU_CHUNK, SGU_CHUNK), 1)
    return jnp.where(row >= col, w, 0.0).astype(BF16)


def _sgu_norm(vg, g, b):
    mu = jnp.mean(vg, axis=-1, keepdims=True)
    cen = vg - mu
    rstd = lax.rsqrt(jnp.mean(cen * cen, axis=-1, keepdims=True) + EPS)
    xhat = cen * rstd
    return xhat, rstd, xhat * g + b


def sgu_fwd(z, ln_g, ln_b, w_sp, b_sp, *, name, tm=256, riders=()):
    s = z.shape[0]
    tm = _row_tile(s, tm)

    def body(z_ref, g_ref, b_ref, w_ref, bs_ref, y_ref):
        u = _gelu(z_ref[:, :D_MODEL])
        _, _, vn = _sgu_norm(_gelu(z_ref[:, D_MODEL:]), g_ref[...], b_ref[...])
        vn = vn.astype(BF16)
        for grp in range(SGU_GROUPS):
            w = _tril_bf16(w_ref[grp])
            cols = slice(grp * LANES, (grp + 1) * LANES)
            for ch in range(tm // SGU_CHUNK):
                rows = slice(ch * SGU_CHUNK, (ch + 1) * SGU_CHUNK)
                mixed = jnp.dot(w, vn[rows, cols], preferred_element_type=F32) + bs_ref[grp]
                y_ref[rows, cols] = (u[rows, cols] * mixed).astype(BF16)

    full3 = pl.BlockSpec((SGU_GROUPS, SGU_CHUNK, SGU_CHUNK), lambda i: (0, 0, 0))
    core, rr = _call(
        body, grid=(s // tm,), in_specs=[_row_spec(tm, 2 * D_MODEL), _vec_spec(D_MODEL), _vec_spec(D_MODEL), full3, full3],
        out_specs=[_row_spec(tm, D_MODEL)], out_shape=[jax.ShapeDtypeStruct((s, D_MODEL), BF16)],
        operands=(z, ln_g, ln_b, w_sp, b_sp), sem=("parallel",), name=name, riders=riders)
    return _ret(core, rr, riders)


def sgu_bwd(z, dy, ln_g, ln_b, w_sp, b_sp, *, name, tm=256, riders=()):
    s = z.shape[0]
    tm = _row_tile(s, tm)

    def body(z_ref, dy_ref, g_ref, b_ref, w_ref, bs_ref, dz_ref, dw_ref, dbs_ref, dg_ref, db_ref, dvn_buf):
        first = pl.program_id(0) == 0
        zu, zv = z_ref[:, :D_MODEL], z_ref[:, D_MODEL:]
        u = _gelu(zu)
        xhat, rstd, vn = _sgu_norm(_gelu(zv), g_ref[...], b_ref[...])
        vn = vn.astype(BF16)
        dyv = dy_ref[...]
        dmixed = dyv * u
        row = lax.broadcasted_iota(I32, (SGU_CHUNK, SGU_CHUNK), 0)
        col = lax.broadcasted_iota(I32, (SGU_CHUNK, SGU_CHUNK), 1)
        for grp in range(SGU_GROUPS):
            w = _tril_bf16(w_ref[grp])
            cols = slice(grp * LANES, (grp + 1) * LANES)
            dw = jnp.zeros((SGU_CHUNK, SGU_CHUNK), F32)
            dbs = jnp.zeros((SGU_CHUNK, 1), F32)
            for ch in range(tm // SGU_CHUNK):
                rows = slice(ch * SGU_CHUNK, (ch + 1) * SGU_CHUNK)
                vblk = vn[rows, cols]
                mixed = jnp.dot(w, vblk, preferred_element_type=F32) + bs_ref[grp]
                dz_ref[rows, cols] = (dyv[rows, cols] * mixed * _gelu_grad(zu[rows, cols])).astype(BF16)
                dm = dmixed[rows, cols]
                dmb = dm.astype(BF16)
                dvn_buf[rows, cols] = lax.dot_general(w, dmb, TN_DIMS, preferred_element_type=F32)
                dw += lax.dot_general(dmb, vblk, NT_DIMS, preferred_element_type=F32)
                dbs += jnp.sum(dm, axis=-1, keepdims=True)
            dw = jnp.where(row >= col, dw, 0.0)
            dbs = jnp.broadcast_to(dbs, (SGU_CHUNK, SGU_CHUNK))

            @pl.when(first)
            def _():
                dw_ref[grp] = dw
                dbs_ref[grp] = dbs

            @pl.when(jnp.logical_not(first))
            def _():
                dw_ref[grp] += dw
                dbs_ref[grp] += dbs

        dvn = dvn_buf[...]
        dxhat = dvn * g_ref[...]
        dvg = rstd * (dxhat - jnp.mean(dxhat, axis=-1, keepdims=True) - xhat * jnp.mean(dxhat * xhat, axis=-1, keepdims=True))
        dz_ref[:, D_MODEL:] = (dvg * _gelu_grad(zv)).astype(BF16)
        _accum(dg_ref, jnp.sum(dvn * xhat, axis=0, keepdims=True), first)
        _accum(db_ref, jnp.sum(dvn, axis=0, keepdims=True), first)

    full3 = pl.BlockSpec((SGU_GROUPS, SGU_CHUNK, SGU_CHUNK), lambda i: (0, 0, 0))
    s3 = jax.ShapeDtypeStruct((SGU_GROUPS, SGU_CHUNK, SGU_CHUNK), F32)
    vshape = jax.ShapeDtypeStruct((1, D_MODEL), F32)
    core, rr = _call(
        body, grid=(s // tm,),
        in_specs=[_row_spec(tm, 2 * D_MODEL), _row_spec(tm, D_MODEL), _vec_spec(D_MODEL), _vec_spec(D_MODEL), full3, full3],
        out_specs=[_row_spec(tm, 2 * D_MODEL), full3, full3, _vec_spec(D_MODEL), _vec_spec(D_MODEL)],
        out_shape=[jax.ShapeDtypeStruct((s, 2 * D_MODEL), BF16), s3, s3, vshape, vshape],
        scratch_shapes=[pltpu.VMEM((tm, D_MODEL), F32)], operands=(z, dy, ln_g, ln_b, w_sp, b_sp), name=name, riders=riders)
    return _ret(core, rr, riders)


def _sigmoid(x):
    return 1.0 / (1.0 + jnp.exp(-x))


def ffn_up(h, w_gu, *, name, tm=512, riders=()):
    s = h.shape[0]
    tm = _row_tile(s, tm)

    def body(h_ref, wg_ref, wu_ref, gu_ref, a_ref):
        hv = h_ref[...]
        g = jnp.dot(hv, wg_ref[...], preferred_element_type=F32)
        u = jnp.dot(hv, wu_ref[...], preferred_element_type=F32)
        gu_ref[0] = g.astype(BF16)
        gu_ref[1] = u.astype(BF16)
        a_ref[...] = (g * _sigmoid(g) * u).astype(BF16)

    core, rr = _call(
        body, grid=(2, s // tm),
        in_specs=[pl.BlockSpec((tm, D_MODEL), lambda j, i: (i, 0)),
                  pl.BlockSpec((None, D_MODEL, FF_HALF), lambda j, i: (j, 0, 0)),
                  pl.BlockSpec((None, D_MODEL, FF_HALF), lambda j, i: (j + 2, 0, 0))],
        out_specs=[pl.BlockSpec((2, tm, FF_HALF), lambda j, i: (0, i, j)), pl.BlockSpec((tm, FF_HALF), lambda j, i: (i, j))],
        out_shape=[jax.ShapeDtypeStruct((2, s, D_FF), BF16), jax.ShapeDtypeStruct((s, D_FF), BF16)],
        operands=(h, w_gu, w_gu), sem=("parallel", "parallel"), name=name, riders=riders)
    return _ret(core, rr, riders)


def ffn_dact(df, w_d, gu, *, name, tm=512, riders=()):
    s = df.shape[0]
    tm = _row_tile(s, tm)

    def body(df_ref, w_ref, gu_ref, o_ref):
        da = lax.dot_general(df_ref[...], w_ref[...], NT_DIMS, preferred_element_type=F32)
        g = gu_ref[0].astype(F32)
        u = gu_ref[1].astype(F32)
        sig = _sigmoid(g)
        o_ref[0] = (da * u * sig * (1.0 + g * (1.0 - sig))).astype(BF16)
        o_ref[1] = (da * g * sig).astype(BF16)

    planes = pl.BlockSpec((2, tm, FF_HALF), lambda j, i: (0, i, j))
    core, rr = _call(
        body, grid=(2, s // tm),
        in_specs=[pl.BlockSpec((tm, D_MODEL), lambda j, i: (i, 0)), pl.BlockSpec((FF_HALF, D_MODEL), lambda j, i: (j, 0)), planes],
        out_specs=[planes], out_shape=[jax.ShapeDtypeStruct((2, s, D_FF), BF16)], operands=(df, w_d, gu),
        sem=("parallel", "parallel"), name=name, riders=riders)
    return _ret(core, rr, riders)


def _weight_tile(rows):
    for tr in (512, 352, 256, 128):
        if rows % tr == 0:
            return tr
    return rows


def place_shard(w, layer, chip_arr, dtype, *, name, riders=()):
    _, r, c = w.shape
    tr = _weight_tile(r)

    def body(chip_ref, w_ref, o_ref):
        o_ref[...] = w_ref[...].astype(dtype)

    core, rr = _call(
        body, grid=(r // tr,), prefetch=(chip_arr,),
        in_specs=[pl.BlockSpec((None, tr, c), lambda i, chip: (layer, i, 0))],
        out_specs=[pl.BlockSpec((None, tr, c), lambda i, chip: (chip[0], i, 0))],
        out_shape=[jax.ShapeDtypeStruct((N_CHIPS, r, c), dtype)], operands=(w,), sem=("parallel",), name=name, riders=riders)
    return _ret(core, rr, riders)


def _adamw_math(w, g, m, v):
    m = ADAM_B1 * m + (1.0 - ADAM_B1) * g
    v = ADAM_B2 * v + (1.0 - ADAM_B2) * (g * g)
    m_hat = m / (1.0 - ADAM_B1 ** ADAM_STEP)
    v_hat = v / (1.0 - ADAM_B2 ** ADAM_STEP)
    delta = -ADAM_LR * (m_hat / (jnp.sqrt(v_hat) + ADAM_EPS) + ADAM_WD * w)
    return delta, m, v


def adamw(w, g, m, v, *, name):
    nl, r, c = w.shape
    tr = _weight_tile(r)

    def body(w_ref, g_ref, m_ref, v_ref, go_ref, d_ref, mo_ref, vo_ref):
        gv = g_ref[...]
        go_ref[...] = gv
        d_ref[...], mo_ref[...], vo_ref[...] = _adamw_math(w_ref[...], gv, m_ref[...], v_ref[...])

    spec = pl.BlockSpec((None, tr, c), lambda l, i: (l, i, 0))
    shape = jax.ShapeDtypeStruct(w.shape, F32)
    outs, _ = _call(body, grid=(nl, r // tr), in_specs=[spec] * 4, out_specs=[spec] * 4, out_shape=[shape] * 4,
                    operands=(w, g, m, v), sem=("parallel", "parallel"), name=name)
    return outs


def adamw_small(ws, gs, ms, vs, *, name):
    n = len(ws)

    def body(*refs):
        ins, outs = refs[:4 * n], refs[4 * n:]
        for t in range(n):
            gv = ins[n + t][...]
            outs[t][...] = gv
            outs[n + t][...], outs[2 * n + t][...], outs[3 * n + t][...] = _adamw_math(
                ins[t][...], gv, ins[2 * n + t][...], ins[3 * n + t][...])

    shapes = [jax.ShapeDtypeStruct(w.shape, F32) for w in ws]
    res = pl.pallas_call(body, out_shape=shapes * 4, name=name)(*ws, *gs, *ms, *vs)
    return res[:n], res[n:2 * n], res[2 * n:3 * n], res[3 * n:]


def pair_add(g, r1, c_arr, *, name):
    _, rows, cdim = g.shape
    h = rows // 2

    def body(c_ref, g_ref, r_ref, o_ref):
        o_ref[...] = (g_ref[...].astype(F32) + r_ref[...].astype(F32)).astype(o_ref.dtype)

    (out,), _ = _call(
        body, grid=(N_CHIPS,), prefetch=(c_arr,),
        in_specs=[pl.BlockSpec((None, h, cdim), lambda s, c: (s, c[0], 0)), pl.BlockSpec((None, h, cdim), lambda s, c: (s, 0, 0))],
        out_specs=[pl.BlockSpec((None, h, cdim), lambda s, c: (s, 0, 0))],
        out_shape=[jax.ShapeDtypeStruct((N_CHIPS, h, cdim), g.dtype)], operands=(g, r1), sem=("parallel",), name=name)
    return out


def final_add(g, r1, r2, jc_arr, *, dest_shape, lead, prev, name):
    _, rows, cdim = g.shape
    h = rows // 2

    def body(jc_ref, g_ref, r1_ref, r2_ref, *rest):
        o_ref = rest[-1]
        acc = g_ref[...].astype(F32) + r1_ref[...].astype(F32)
        for k in range(3):
            acc = acc + r2_ref[k].astype(F32)
        o_ref[...] = acc

    if lead is None:
        o_spec = pl.BlockSpec((h, cdim), lambda i, jc: (jc[1], 0))
    elif lead == "chip":
        o_spec = pl.BlockSpec((None, h, cdim), lambda i, jc: (jc[0], jc[1], 0))
    else:
        o_spec = pl.BlockSpec((None, h, cdim), lambda i, jc: (lead, jc[1], 0))
    in_specs = [pl.BlockSpec((None, h, cdim), lambda i, jc: (jc[0], jc[1], 0)),
                pl.BlockSpec((None, h, cdim), lambda i, jc: (jc[0], 0, 0)),
                pl.BlockSpec((3, h, cdim), lambda i, jc: (0, 0, 0))]
    operands = [g, r1, r2]
    aliases = None
    if prev is not None:
        in_specs.append(ANY)
        operands.append(prev)
        aliases = {3: 0}
    (out,), _ = _call(body, grid=(1,), prefetch=(jc_arr,), in_specs=in_specs, out_specs=[o_spec],
                      out_shape=[jax.ShapeDtypeStruct(dest_shape, F32)], operands=operands, aliases=aliases, name=name)
    return out


def _place():
    return lax.axis_index("x"), lax.axis_index("y"), lax.axis_index("c")


def _partner(x, y, k):
    return (1 - x if k >> 1 else x), (1 - y if k & 1 else y)


def _half(rows, sel, dtype):
    align = 16 if dtype == BF16 else 8
    return pl.ds(pl.multiple_of(sel * (rows // 2), align), rows // 2)


def _rider(inputs, aliased, fresh, nsem, copies, arrivals):
    def start(ins, outs, send, recv):
        for cp in copies(ins, outs, send, recv):
            cp.start()

    def finish(ins, outs, send, recv):
        for cp in arrivals(ins, outs, send, recv):
            cp.wait_recv()
        for cp in copies(ins, outs, send, recv):
            cp.wait_send()

    return types.SimpleNamespace(inputs=list(inputs), aliased=list(aliased), fresh=list(fresh), nsem=nsem, start=start,
                                 finish=finish)


def _remote(src, dst, send, recv, idx, dev):
    return pltpu.make_async_remote_copy(src_ref=src, dst_ref=dst, send_sem=send.at[idx], recv_sem=recv.at[idx],
                                        device_id=dev, device_id_type=MESH)


def gather_ici_rider(fulls):
    nt = len(fulls)

    def region(outs, t, slot, sel):
        return outs[t].at[slot, _half(fulls[t].shape[1], sel, fulls[t].dtype)]

    def copies(ins, outs, send, recv):
        x, y, c = _place()
        res = []
        for t in range(nt):
            for k in (1, 2, 3):
                px, py = _partner(x, y, k)
                mine = region(outs, t, 2 * x + y, c)
                res.append(_remote(mine, mine, send, recv, 3 * t + k - 1, (px, py, c)))
        return res

    def arrivals(ins, outs, send, recv):
        x, y, c = _place()
        res = []
        for t in range(nt):
            for k in (1, 2, 3):
                px, py = _partner(x, y, k)
                theirs = region(outs, t, 2 * px + py, c)
                res.append(_remote(theirs, theirs, send, recv, 3 * t + k - 1, (x, y, c)))
        return res

    return _rider(fulls, range(nt), [], 3 * nt, copies, arrivals)


def gather_d2d_rider(fulls):
    nt = len(fulls)

    def region(outs, t, slot, sel):
        return outs[t].at[slot, _half(fulls[t].shape[1], sel, fulls[t].dtype)]

    def both(outs, send, recv, mine):
        x, y, c = _place()
        res = []
        for t in range(nt):
            for k in (1, 2, 3):
                px, py = _partner(x, y, k)
                part = region(outs, t, 2 * px + py, c if mine else 1 - c)
                res.append(_remote(part, part, send, recv, 3 * t + k - 1, (x, y, 1 - c)))
        return res

    return _rider(fulls, range(nt), [], 3 * nt, lambda i, o, s, r: both(o, s, r, True), lambda i, o, s, r: both(o, s, r, False))


def exchange_rider(grads):
    nt = len(grads)

    def both(ins, outs, send, recv):
        x, y, c = _place()
        return [_remote(ins[t].at[:, _half(grads[t].shape[1], 1 - c, grads[t].dtype)], outs[t], send, recv, t, (x, y, 1 - c))
                for t in range(nt)]

    fresh = [jax.ShapeDtypeStruct((N_CHIPS, g.shape[1] // 2, g.shape[2]), g.dtype) for g in grads]
    return _rider(grads, [], fresh, nt, both, both)


def scatter_rider(parts):
    nt = len(parts)

    def both(ins, outs, send, recv):
        x, y, c = _place()
        res = []
        for t in range(nt):
            for k in (1, 2, 3):
                px, py = _partner(x, y, k)
                res.append(_remote(ins[t].at[2 * px + py], outs[t].at[k - 1], send, recv, 3 * t + k - 1, (px, py, c)))
        return res

    fresh = [jax.ShapeDtypeStruct((3,) + p.shape[1:], p.dtype) for p in parts]
    return _rider(parts, [], fresh, 3 * nt, both, both)


def broadcast_rider(bufs, items):
    def region(outs, item, sel):
        bi, lead = item
        ref = outs[bi]
        if lead == "chip":
            x, y, _ = _place()
            ref = ref.at[2 * x + y]
        elif lead is not None:
            ref = ref.at[lead]
        return ref.at[_half(ref.shape[0], sel, F32)]

    def both(outs, send, recv, mine):
        x, y, c = _place()
        res = []
        for i, item in enumerate(items):
            part = region(outs, item, c if mine else 1 - c)
            res.append(_remote(part, part, send, recv, i, (x, y, 1 - c)))
        return res

    return _rider(bufs, range(len(bufs)), [], len(items), lambda i, o, s, r: both(o, s, r, True),
                  lambda i, o, s, r: both(o, s, r, False))


def allcast_rider(buf):
    peers = [(k, flip) for k in range(N_CHIPS) for flip in (0, 1) if (k, flip) != (0, 0)]

    def both(outs, send, recv, mine):
        x, y, c = _place()
        res = []
        for i, (k, flip) in enumerate(peers):
            px, py = _partner(x, y, k)
            pc = 1 - c if flip else c
            slot, sel = (2 * x + y, c) if mine else (2 * px + py, pc)
            part = outs[0].at[slot, _half(buf.shape[1], sel, F32)]
            res.append(_remote(part, part, send, recv, i, (px, py, pc)))
        return res

    return _rider([buf], [0], [], len(peers), lambda i, o, s, r: both(o, s, r, True), lambda i, o, s, r: both(o, s, r, False))


def comm_call(riders, *, name):
    _, res = _call(None, riders=riders, name=name)
    return res


SLAB_ROWS = 192


def _pad_rows(a, rows=8):
    return jnp.pad(a, ((0, rows - a.shape[0]), (0, 0)))


def _pack_small(norm_grads, db_qkv, db_o, dsinks, db_sp, dln_g, dln_b, dw_sp):
    parts = [
        jnp.concatenate(norm_grads, axis=0),
        _pad_rows(jnp.pad(db_qkv, ((0, 0), (0, 2 * D_MODEL - QKV_WIDTH))).reshape(2, D_MODEL)),
        _pad_rows(db_o),
        _pad_rows(jnp.pad(dsinks.reshape(1, N_Q_HEADS), ((0, 0), (0, D_MODEL - N_Q_HEADS)))),
        _pad_rows(db_sp.reshape(1, D_MODEL)),
        _pad_rows(jnp.concatenate([dln_g, dln_b], axis=0)),
        dw_sp.reshape(SGU_CHUNK, D_MODEL),
    ]
    slab = jnp.concatenate(parts, axis=0)
    return jnp.pad(slab, ((0, SLAB_ROWS - slab.shape[0]), (0, 0))).reshape(N_CHIPS, SLAB_ROWS // N_CHIPS, D_MODEL)


def _unpack_small(slab, j):
    slab = slab.reshape(SLAB_ROWS, D_MODEL)
    norms = [slab[2 * i:2 * i + 2] for i in range(4)]
    db_qkv = slab[8:10].reshape(1, 2 * D_MODEL)[:, :QKV_WIDTH]
    db_o = slab[16:17]
    dsinks = slab[24:25, :N_Q_HEADS]
    db_sp = slab[32:33].reshape(SGU_GROUPS, SGU_CHUNK)
    width = D_MODEL // N_CHIPS
    dln_g = lax.dynamic_slice(slab[40:41], (0, j * width), (1, width))
    dln_b = lax.dynamic_slice(slab[41:42], (0, j * width), (1, width))
    dw_sp = slab[48:48 + SGU_CHUNK].reshape(SGU_GROUPS * SGU_CHUNK, SGU_CHUNK)
    return norms, db_qkv, db_o, dsinks, db_sp, dln_g, dln_b, dw_sp


class _GradReduce:
    def __init__(self, c_arr, jc_arr, dest_shapes):
        self.c_arr, self.jc_arr, self.dest_shapes = c_arr, jc_arr, dest_shapes
        self.grad, self.sibling, self.pair, self.chips, self.dest = {}, {}, {}, {}, {}

    def exchange(self, tags):
        return exchange_rider([self.grad[t] for t in tags])

    def exchanged(self, tags, res):
        for t, r in zip(tags, res):
            self.sibling[t] = r
            self.pair[t] = pair_add(self.grad[t], r, self.c_arr, name=f"pair_add_{t}")

    def scatter(self, tags):
        return scatter_rider([self.pair[t] for t in tags])

    def scattered(self, tags, res, where):
        for t, r in zip(tags, res):
            name, lead = where[t]
            self.dest[name] = final_add(self.grad[t], self.sibling[t], r, self.jc_arr, dest_shape=self.dest_shapes[name],
                                        lead=lead, prev=self.dest.get(name), name=f"final_add_{t}")

    def broadcast(self, items):
        names = []
        for n, _ in items:
            if n not in names:
                names.append(n)
        return names, broadcast_rider([self.dest[n] for n in names], [(names.index(n), lead) for n, lead in items])

    def broadcasted(self, names, res):
        for n, r in zip(names, res):
            self.dest[n] = r


def kernel(x, norm_mix_pre, norm_mix_post, norm_ffn_pre, norm_ffn_post, attn_w_qkv, attn_b_qkv, attn_sinks, attn_w_o, attn_b_o, sgu_w_in, sgu_ln_g, sgu_ln_b, sgu_w_spatial, sgu_b_spatial, sgu_w_out, ffn_w_gate_up, ffn_w_down, loss_target, m_norm_mix_pre, m_norm_mix_post, m_norm_ffn_pre, m_norm_ffn_post, m_attn_w_qkv, m_attn_b_qkv, m_attn_sinks, m_attn_w_o, m_attn_b_o, m_sgu_w_in, m_sgu_ln_g, m_sgu_ln_b, m_sgu_w_spatial, m_sgu_b_spatial, m_sgu_w_out, m_ffn_w_gate_up, m_ffn_w_down, v_norm_mix_pre, v_norm_mix_post, v_norm_ffn_pre, v_norm_ffn_post, v_attn_w_qkv, v_attn_b_qkv, v_attn_sinks, v_attn_w_o, v_attn_b_o, v_sgu_w_in, v_sgu_ln_g, v_sgu_ln_b, v_sgu_w_spatial, v_sgu_b_spatial, v_sgu_w_out, v_ffn_w_gate_up, v_ffn_w_down):
    s = x.shape[1]
    x0 = x.reshape(s, D_MODEL)
    target = loss_target.reshape(s, D_MODEL)
    mx, my, mc = lax.axis_index("x"), lax.axis_index("y"), lax.axis_index("c")
    chip = 2 * mx + my
    chip_arr = jnp.reshape(chip, (1,)).astype(I32)
    c_arr = jnp.reshape(mc, (1,)).astype(I32)
    jc_arr = jnp.stack([chip, mc]).astype(I32)
    zero_bias = jnp.zeros((1, D_MODEL), F32)

    def gain(p, i):
        return p[i:i + 1]

    big = [attn_w_qkv, attn_w_o, sgu_w_in, sgu_w_out, ffn_w_gate_up, ffn_w_gate_up, ffn_w_down, ffn_w_down]
    layers = [0, 0, 0, 0, 0, 1, 0, 1]
    tags = ["qkv", "wo", "win", "wout", "wgu0", "wgu1", "wd0", "wd1"]
    full = {t: place_shard(w, l, chip_arr, BF16, name=f"place_{t}") for w, l, t in zip(big, layers, tags) if t != "wgu1"}
    ln_pack = _pad_rows(jnp.concatenate([sgu_ln_g, sgu_ln_b], axis=0), 16)[None]
    full["ln"] = place_shard(ln_pack, 0, chip_arr, F32, name="place_ln")

    def ici(*names):
        return gather_ici_rider([full[n] for n in names])

    def d2d(*names):
        return gather_d2d_rider([full[n] for n in names])

    def landed(names, res):
        for n, r in zip(names, res):
            full[n] = r

    cos, sin = _rope_tables(s)
    sink_rows = jnp.broadcast_to(
        jnp.repeat(attn_sinks.reshape(N_KV_HEADS, GQA_GROUP), WINDOW, axis=1)[:, :, None], (N_KV_HEADS, ROWS, LANES))
    w_sp = sgu_w_spatial.reshape(SGU_GROUPS, SGU_CHUNK, SGU_CHUNK)
    b_sp = jnp.broadcast_to(sgu_b_spatial.reshape(SGU_GROUPS, SGU_CHUNK)[:, :, None], (SGU_GROUPS, SGU_CHUNK, LANES))

    h0, (res,) = prenorm(x0, gain(norm_mix_pre, 0), name="prenorm_0", riders=[ici("qkv", "ln")])
    landed(("qkv", "ln"), res)
    full["wgu1"], (res,) = place_shard(ffn_w_gate_up, 1, chip_arr, BF16, name="place_wgu1", riders=[d2d("qkv", "ln")])
    landed(("qkv", "ln"), res)
    ln_g = full["ln"][:, 0, :].reshape(1, D_MODEL)
    ln_b = full["ln"][:, 1, :].reshape(1, D_MODEL)

    qkv, (res,) = qkv_proj(h0, full["qkv"], attn_b_qkv, cos, sin, name="qkv_proj", riders=[ici("wo", "wd0")])
    landed(("wo", "wd0"), res)
    o, (res_a, res_b) = attn_fwd(qkv, sink_rows, name="attn_fwd", riders=[d2d("wo", "wd0"), ici("wgu0")])
    landed(("wo", "wd0"), res_a)
    landed(("wgu0",), res_b)
    w_o = full["wo"].reshape(Q_WIDTH, D_MODEL)
    (x1, h1, m0), (res_a, res_b) = proj_residual_norm(
        o, w_o, x0, attn_b_o, gain(norm_mix_post, 0), gain(norm_ffn_pre, 0), name="attn_out_norm",
        riders=[d2d("wgu0"), ici("win", "wout")])
    landed(("wgu0",), res_a)
    landed(("win", "wout"), res_b)
    (gu0, a0), (res_a, res_b) = ffn_up(h1, full["wgu0"], name="ffn_up_0", riders=[d2d("win", "wout"), ici("wgu1")])
    landed(("win", "wout"), res_a)
    landed(("wgu1",), res_b)
    w_d0 = full["wd0"].reshape(D_FF, D_MODEL)
    (x2, h2, f0), (res_a, res_b) = proj_residual_norm(
        a0, w_d0, x1, zero_bias, gain(norm_ffn_post, 0), gain(norm_mix_pre, 1), name="ffn_down_norm_0",
        riders=[d2d("wgu1"), ici("wd1")])
    landed(("wgu1",), res_a)
    landed(("wd1",), res_b)
    z, (res,) = mm_nn(h2, full["win"], out_dtype=F32, name="sgu_in_proj", riders=[d2d("wd1")])
    landed(("wd1",), res)
    w_qkv, w_in, w_gu0, w_gu1 = full["qkv"], full["win"], full["wgu0"], full["wgu1"]
    w_out = full["wout"].reshape(D_MODEL, D_MODEL)
    w_d1 = full["wd1"].reshape(D_FF, D_MODEL)
    y = sgu_fwd(z, ln_g, ln_b, w_sp, b_sp, name="sgu_fwd")
    x3, h3, m1 = proj_residual_norm(y, w_out, x2, zero_bias, gain(norm_mix_post, 1), gain(norm_ffn_pre, 1), name="sgu_out_norm")
    gu1, a1 = ffn_up(h3, w_gu1, name="ffn_up_1")
    dx4, df1, dg_fpost1, loss_part = proj_loss_head(a1, w_d1, x3, gain(norm_ffn_post, 1), target, name="ffn_down_loss")
    loss = lax.psum(loss_part[0, 0], ("x", "y", "c"))

    red = _GradReduce(c_arr, jc_arr, {
        "qkv": attn_w_qkv.shape[1:], "wo": attn_w_o.shape[1:], "win": sgu_w_in.shape[1:], "wout": sgu_w_out.shape[1:],
        "wgu": ffn_w_gate_up.shape, "wd": ffn_w_down.shape, "slab": (N_CHIPS, SLAB_ROWS // N_CHIPS, D_MODEL)})
    where = {"qkv": ("qkv", None), "wo": ("wo", None), "win": ("win", None), "wout": ("wout", None), "wgu0": ("wgu", 0),
             "wgu1": ("wgu", 1), "wd0": ("wd", 0), "wd1": ("wd", 1), "small": ("slab", "chip")}

    dgu1 = ffn_dact(df1, w_d1, gu1, name="ffn_dact_1")
    red.grad["wd1"] = mm_tn(a1, df1, shard_major=False, tm=256, tn=D_MODEL, name="dw_down_1").reshape(
        N_CHIPS, D_FF // N_CHIPS, D_MODEL)
    red.grad["wgu1"], (res,) = mm_tn(h3, dgu1, shard_major=True, tm=512, tn=FF_HALF, name="dw_gate_up_1",
                                     riders=[red.exchange(["wd1"])])
    red.exchanged(["wd1"], res)
    (dx3, dm1, dg_fpre1, dg_mpost1, _), (res_a, res_b) = dh_norm_bwd_pair(
        dgu1, w_gu1, dx4, x3, gain(norm_ffn_pre, 1), m1, gain(norm_mix_post, 1), name="dh_ffn_norm_1",
        riders=[red.exchange(["wgu1"]), red.scatter(["wd1"])])
    red.exchanged(["wgu1"], res_a)
    red.scattered(["wd1"], res_b, where)
    names, rider = red.broadcast([("wd", 1)])
    dy, (res,) = mm_nt(dm1, w_out, out_dtype=F32, name="dy_sgu", riders=[rider])
    red.broadcasted(names, res)
    red.grad["wout"] = mm_tn(y, dm1, shard_major=False, tm=512, tn=D_MODEL, name="dw_sgu_out").reshape(
        N_CHIPS, D_MODEL // N_CHIPS, D_MODEL)
    (dz, dw_sp, db_sp, dln_g, dln_b), (res_a, res_b) = sgu_bwd(
        z, dy, ln_g, ln_b, w_sp, b_sp, name="sgu_bwd", riders=[red.scatter(["wgu1"]), red.exchange(["wout"])])
    red.scattered(["wgu1"], res_a, where)
    red.exchanged(["wout"], res_b)
    names, rider = red.broadcast([("wgu", 1)])
    red.grad["win"], (res_a, res_b) = mm_tn(h2, dz, shard_major=True, tm=D_MODEL, tn=2 * D_MODEL // N_CHIPS, name="dw_sgu_in",
                                            riders=[rider, red.scatter(["wout"])])
    red.broadcasted(names, res_a)
    red.scattered(["wout"], res_b, where)
    names, rider = red.broadcast([("wout", None)])
    (dx2, df0, dg_mpre1, dg_fpost0, _), (res_a, res_b) = dh_norm_bwd_pair(
        dz, w_in, dx3, x2, gain(norm_mix_pre, 1), f0, gain(norm_ffn_post, 0), name="dh_sgu_norm",
        riders=[red.exchange(["win"]), rider])
    red.exchanged(["win"], res_a)
    red.broadcasted(names, res_b)
    dgu0, (res,) = ffn_dact(df0, w_d0, gu0, name="ffn_dact_0", riders=[red.scatter(["win"])])
    red.scattered(["win"], res, where)
    names, rider = red.broadcast([("win", None)])
    dw_d0, (res,) = mm_tn(a0, df0, shard_major=False, tm=256, tn=D_MODEL, name="dw_down_0", riders=[rider])
    red.broadcasted(names, res)
    red.grad["wd0"] = dw_d0.reshape(N_CHIPS, D_FF // N_CHIPS, D_MODEL)
    red.grad["wgu0"], (res,) = mm_tn(h1, dgu0, shard_major=True, tm=512, tn=FF_HALF, name="dw_gate_up_0",
                                     riders=[red.exchange(["wd0"])])
    red.exchanged(["wd0"], res)
    (dx1, dm0, dg_fpre0, dg_mpost0, db_o), (res_a, res_b) = dh_norm_bwd_pair(
        dgu0, w_gu0, dx2, x1, gain(norm_ffn_pre, 0), m0, gain(norm_mix_post, 0), name="dh_ffn_norm_0",
        riders=[red.exchange(["wgu0"]), red.scatter(["wd0"])])
    red.exchanged(["wgu0"], res_a)
    red.scattered(["wd0"], res_b, where)
    names, rider = red.broadcast([("wd", 0)])
    do, (res,) = mm_nt(dm0, w_o, out_dtype=BF16, name="do_attn", riders=[rider])
    red.broadcasted(names, res)
    red.grad["wo"] = mm_tn(o, dm0, shard_major=False, tm=512, tn=D_MODEL, name="dw_attn_out").reshape(
        N_CHIPS, Q_WIDTH // N_CHIPS, D_MODEL)
    (dq, dkc, dkp, dvc, dvp, dsink), (res_a, res_b) = attn_bwd(
        qkv, sink_rows, do, name="attn_bwd", riders=[red.scatter(["wgu0"]), red.exchange(["wo"])])
    red.scattered(["wgu0"], res_a, where)
    red.exchanged(["wo"], res_b)
    names, rider = red.broadcast([("wgu", 0)])
    (dqkv, db_qkv), (res_a, res_b) = rope_bwd(dq, dkc, dkp, dvc, dvp, cos, sin, name="rope_bwd",
                                              riders=[rider, red.scatter(["wo"])])
    red.broadcasted(names, res_a)
    red.scattered(["wo"], res_b, where)
    names, rider = red.broadcast([("wo", None)])
    red.grad["qkv"], (res,) = mm_tn(h0, dqkv, shard_major=True, tm=D_MODEL, tn=QKV_WIDTH // N_CHIPS, name="dw_qkv",
                                    riders=[rider])
    red.broadcasted(names, res)
    dh0, (res,) = mm_nt(dqkv, w_qkv, out_dtype=F32, tm=1024, name="dh_attn", riders=[red.exchange(["qkv"])])
    red.exchanged(["qkv"], res)
    grad_x, dg_mpre0 = norm_bwd_last(dx1, dh0, x0, gain(norm_mix_pre, 0), name="norm_bwd_in")

    norm_grads = [jnp.concatenate(p, axis=0) for p in
                  ((dg_mpre0, dg_mpre1), (dg_mpost0, dg_mpost1), (dg_fpre0, dg_fpre1), (dg_fpost0, dg_fpost1))]
    red.grad["small"] = _pack_small(norm_grads, db_qkv, db_o, dsink[:, :, 0, 0], db_sp[:, :, 0], dln_g, dln_b, dw_sp)
    res_a, res_b = comm_call([red.scatter(["qkv"]), red.exchange(["small"])], name="tail_1")
    red.scattered(["qkv"], res_a, where)
    red.exchanged(["small"], res_b)
    names, rider = red.broadcast([("qkv", None)])
    res_a, res_b = comm_call([red.scatter(["small"]), rider], name="tail_2")
    red.scattered(["small"], res_a, where)
    red.broadcasted(names, res_b)
    ((slab_full,),) = comm_call([allcast_rider(red.dest["slab"])], name="tail_3")
    g_qkv, g_wo, g_win, g_wout, g_wgu, g_wd = (red.dest[n] for n in ("qkv", "wo", "win", "wout", "wgu", "wd"))
    g_norms, g_bqkv, g_bo, g_sinks, g_bsp, g_lng, g_lnb, g_wsp = _unpack_small(slab_full, chip)

    def big_update(w, g, m, v, tag):
        return adamw(w, g.reshape(w.shape), m, v, name=f"adamw_{tag}")

    upd = {
        "attn_w_qkv": big_update(attn_w_qkv, g_qkv, m_attn_w_qkv, v_attn_w_qkv, "qkv"),
        "attn_w_o": big_update(attn_w_o, g_wo, m_attn_w_o, v_attn_w_o, "wo"),
        "sgu_w_in": big_update(sgu_w_in, g_win, m_sgu_w_in, v_sgu_w_in, "win"),
        "sgu_w_out": big_update(sgu_w_out, g_wout, m_sgu_w_out, v_sgu_w_out, "wout"),
        "ffn_w_gate_up": big_update(ffn_w_gate_up, g_wgu, m_ffn_w_gate_up, v_ffn_w_gate_up, "wgu"),
        "ffn_w_down": big_update(ffn_w_down, g_wd, m_ffn_w_down, v_ffn_w_down, "wd"),
    }
    small_names = ["norm_mix_pre", "norm_mix_post", "norm_ffn_pre", "norm_ffn_post", "attn_b_qkv", "attn_sinks", "attn_b_o",
                   "sgu_ln_g", "sgu_ln_b", "sgu_w_spatial", "sgu_b_spatial"]
    small_w = [norm_mix_pre, norm_mix_post, norm_ffn_pre, norm_ffn_post, attn_b_qkv, attn_sinks, attn_b_o, sgu_ln_g, sgu_ln_b,
               sgu_w_spatial, sgu_b_spatial]
    small_m = [m_norm_mix_pre, m_norm_mix_post, m_norm_ffn_pre, m_norm_ffn_post, m_attn_b_qkv, m_attn_sinks, m_attn_b_o,
               m_sgu_ln_g, m_sgu_ln_b, m_sgu_w_spatial, m_sgu_b_spatial]
    small_v = [v_norm_mix_pre, v_norm_mix_post, v_norm_ffn_pre, v_norm_ffn_post, v_attn_b_qkv, v_attn_sinks, v_attn_b_o,
               v_sgu_ln_g, v_sgu_ln_b, v_sgu_w_spatial, v_sgu_b_spatial]
    small_g = g_norms + [g_bqkv, g_sinks, g_bo, g_lng, g_lnb, g_wsp, g_bsp]

    def flat2(a):
        return a.reshape(-1, a.shape[-1])

    res = adamw_small([flat2(a) for a in small_w], [flat2(a) for a in small_g], [flat2(a) for a in small_m],
                      [flat2(a) for a in small_v], name="adamw_small")
    for i, nm in enumerate(small_names):
        upd[nm] = tuple(r[i].reshape(small_w[i].shape) for r in res)

    order = ["norm_mix_pre", "norm_mix_post", "norm_ffn_pre", "norm_ffn_post", "attn_w_qkv", "attn_b_qkv", "attn_sinks",
             "attn_w_o", "attn_b_o", "sgu_w_in", "sgu_ln_g", "sgu_ln_b", "sgu_w_spatial", "sgu_b_spatial", "sgu_w_out",
             "ffn_w_gate_up", "ffn_w_down"]
    outs = [loss, grad_x.reshape(1, s, D_MODEL)]
    for part in range(4):
        outs += [upd[nm][part] for nm in order]
    return tuple(outs)
```

```python
import types

import jax
import jax.numpy as jnp
from jax import lax
from jax.experimental import pallas as pl
from jax.experimental.pallas import tpu as pltpu

F32 = jnp.float32
BF16 = jnp.bfloat16
I32 = jnp.int32

D_MODEL = 1024
HEAD_DIM = 64
N_Q_HEADS = 16
N_KV_HEADS = 4
GQA_GROUP = 4
WINDOW = 128
Q_WIDTH = 1024
KV_WIDTH = 256
QKV_WIDTH = 1536
ROPE_THETA = 10000.0
SGU_GROUPS = 8
SGU_CHUNK = 128
D_FF = 2816
FF_HALF = D_FF // 2
EPS = 1e-6
N_CHIPS = 4
LANES = 128

ADAM_LR = 0.001
ADAM_B1 = 0.9
ADAM_B2 = 0.999
ADAM_EPS = 1e-08
ADAM_WD = 0.01
ADAM_STEP = 10

VMEM_LIMIT = 52 * 1024 * 1024
MESH = pl.DeviceIdType.MESH
NEG = -1e30
NT_DIMS = (((1,), (1,)), ((), ()))
TN_DIMS = (((0,), (0,)), ((), ()))
NN_DIMS = (((1,), (0,)), ((), ()))
ANY = pl.BlockSpec(memory_space=pl.ANY)


def _row_tile(s, want):
    return want if s % want == 0 else s


def _call(body, *, name, grid=(), in_specs=(), out_specs=(), out_shape=(), scratch_shapes=(), operands=(), prefetch=(),
          aliases=None, riders=(), sem=None):
    n_pre, n_in, n_out, n_scr = len(prefetch), len(operands), len(out_shape), len(scratch_shapes)
    in_specs, out_specs, out_shape = list(in_specs), list(out_specs), list(out_shape)
    operands, scratch_shapes = list(operands), list(scratch_shapes)
    io_alias = {n_pre + i: o for i, o in (aliases or {}).items()}
    for r in riders:
        base_in, base_out = n_pre + len(operands), len(out_shape)
        operands += list(r.inputs)
        in_specs += [ANY] * len(r.inputs)
        for pos, i in enumerate(r.aliased):
            io_alias[base_in + i] = base_out + pos
            out_shape.append(jax.ShapeDtypeStruct(r.inputs[i].shape, r.inputs[i].dtype))
        out_shape += list(r.fresh)
        out_specs += [ANY] * (len(r.aliased) + len(r.fresh))
        scratch_shapes += [pltpu.SemaphoreType.DMA((r.nsem,)), pltpu.SemaphoreType.DMA((r.nsem,))]

    def wrapped(*refs):
        pre, p = refs[:n_pre], n_pre
        core_in, p = refs[p:p + n_in], p + n_in
        r_in = []
        for r in riders:
            r_in.append(refs[p:p + len(r.inputs)])
            p += len(r.inputs)
        core_out, p = refs[p:p + n_out], p + n_out
        r_out = []
        for r in riders:
            k = len(r.aliased) + len(r.fresh)
            r_out.append(refs[p:p + k])
            p += k
        core_scr, p = refs[p:p + n_scr], p + n_scr
        r_sem = [refs[p + 2 * i:p + 2 * i + 2] for i in range(len(riders))]

        def edge(at_last, fns):
            def run():
                for i, r in enumerate(riders):
                    getattr(r, fns)(r_in[i], r_out[i], r_sem[i][0], r_sem[i][1])
            if not riders:
                return
            if not grid:
                run()
                return
            cond = None
            for d, n in enumerate(grid):
                c = pl.program_id(d) == (n - 1 if at_last else 0)
                cond = c if cond is None else jnp.logical_and(cond, c)
            pl.when(cond)(run)

        edge(False, "start")
        if body is not None:
            body(*pre, *core_in, *core_out, *core_scr)
        edge(True, "finish")

    if sem is None or riders:
        sem = ("arbitrary",) * len(grid)
    kwargs = dict(out_shape=out_shape, input_output_aliases=io_alias, name=name)
    if grid:
        kwargs["compiler_params"] = pltpu.CompilerParams(dimension_semantics=sem, vmem_limit_bytes=VMEM_LIMIT)
    if n_pre:
        kwargs["grid_spec"] = pltpu.PrefetchScalarGridSpec(
            num_scalar_prefetch=n_pre, grid=grid, in_specs=in_specs, out_specs=out_specs, scratch_shapes=scratch_shapes)
    else:
        kwargs.update(grid=grid, in_specs=in_specs, out_specs=out_specs, scratch_shapes=scratch_shapes)
    res = pl.pallas_call(wrapped, **kwargs)(*prefetch, *operands)
    core, rest, rider_res = list(res[:n_out]), list(res[n_out:]), []
    for r in riders:
        k = len(r.aliased) + len(r.fresh)
        rider_res.append(rest[:k])
        rest = rest[k:]
    return core, rider_res


def _mm_call(*, grid, in_specs, out_spec, out_shape, dims, nk, kaxis, acc_shape, name, operands, riders=()):
    out_dtype = out_shape.dtype

    def body(a_ref, b_ref, o_ref, *scratch):
        p = lax.dot_general(a_ref[...].astype(BF16), b_ref[...].astype(BF16), dims, preferred_element_type=F32)
        if nk == 1:
            o_ref[...] = p.astype(out_dtype)
        else:
            acc = scratch[0]
            kk = pl.program_id(kaxis)

            @pl.when(kk == 0)
            def _():
                acc[...] = p

            @pl.when(kk > 0)
            def _():
                acc[...] += p

            @pl.when(kk == nk - 1)
            def _():
                o_ref[...] = acc[...].astype(out_dtype)

    sem = ["parallel"] * len(grid)
    if nk > 1:
        sem[kaxis] = "arbitrary"
    (out,), rider_res = _call(
        body, grid=grid, in_specs=in_specs, out_specs=[out_spec], out_shape=[out_shape],
        scratch_shapes=[pltpu.VMEM(acc_shape, F32)] if nk > 1 else [], operands=operands, name=name, riders=riders,
        sem=tuple(sem))
    return (out, rider_res) if riders else out


def mm_nn(a, w, *, out_dtype, name, tm=512, tn=512, riders=()):
    m, k = a.shape
    tm = _row_tile(m, tm)
    if w.ndim == 3:
        ns = w.shape[2]
        grid = (N_CHIPS, m // tm)
        w_spec = pl.BlockSpec((None, k, ns), lambda j, i: (j, 0, 0))
        o_spec = pl.BlockSpec((tm, ns), lambda j, i: (i, j))
        n = N_CHIPS * ns
    else:
        n = w.shape[1]
        grid = (n // tn, m // tm)
        w_spec = pl.BlockSpec((k, tn), lambda j, i: (0, j))
        o_spec = pl.BlockSpec((tm, tn), lambda j, i: (i, j))
    return _mm_call(grid=grid, in_specs=[pl.BlockSpec((tm, k), lambda j, i: (i, 0)), w_spec], out_spec=o_spec,
                    out_shape=jax.ShapeDtypeStruct((m, n), out_dtype), dims=NN_DIMS, nk=1, kaxis=0, acc_shape=None,
                    name=name, operands=(a, w), riders=riders)


def mm_nt(a, w, *, out_dtype, name, tm=512, tn=512, riders=()):
    if w.ndim == 2:
        m, n = a.shape
        kout = w.shape[0]
        tm = _row_tile(m, tm)
        return _mm_call(grid=(kout // tn, m // tm),
                        in_specs=[pl.BlockSpec((tm, n), lambda j, i: (i, 0)), pl.BlockSpec((tn, n), lambda j, i: (j, 0))],
                        out_spec=pl.BlockSpec((tm, tn), lambda j, i: (i, j)),
                        out_shape=jax.ShapeDtypeStruct((m, kout), out_dtype), dims=NT_DIMS, nk=1, kaxis=0,
                        acc_shape=None, name=name, operands=(a, w), riders=riders)
    _, kout, ns = w.shape
    planes = a.ndim == 3
    m = a.shape[1] if planes else a.shape[0]
    tm = _row_tile(m, tm)
    a_spec = pl.BlockSpec((2, tm, 2 * ns), lambda i: (0, i, 0)) if planes else pl.BlockSpec((tm, N_CHIPS * ns), lambda i: (i, 0))

    def body(a_ref, w0, w1, w2, w3, o_ref):
        acc = None
        for j, w_ref in enumerate((w0, w1, w2, w3)):
            if planes:
                a_j = a_ref[j // 2, :, (j % 2) * ns:(j % 2 + 1) * ns]
            else:
                a_j = a_ref[:, j * ns:(j + 1) * ns]
            p = lax.dot_general(a_j, w_ref[...], NT_DIMS, preferred_element_type=F32)
            acc = p if acc is None else acc + p
        o_ref[...] = acc.astype(out_dtype)

    def shard(j):
        return pl.BlockSpec((None, kout, ns), lambda i: (j, 0, 0))

    (out,), rider_res = _call(
        body, grid=(m // tm,), in_specs=[a_spec] + [shard(j) for j in range(N_CHIPS)],
        out_specs=[pl.BlockSpec((tm, kout), lambda i: (i, 0))], out_shape=[jax.ShapeDtypeStruct((m, kout), out_dtype)],
        operands=(a, w, w, w, w), sem=("parallel",), name=name, riders=riders)
    return (out, rider_res) if riders else out


def mm_tn(a, b, *, shard_major, name, tm, tn, tk=None, out_dtype=BF16, riders=()):
    s, m = a.shape
    tk = s if tk is None else _row_tile(s, tk)
    if b.ndim == 3:
        n = 2 * b.shape[2]
        b_spec = pl.BlockSpec((None, tk, tn), lambda j, i, kk: (j // 2, kk, j % 2))
    else:
        n = b.shape[1]
        b_spec = pl.BlockSpec((tk, tn), lambda j, i, kk: (kk, j))
    if shard_major:
        assert tn == n // N_CHIPS
        o_spec = pl.BlockSpec((None, tm, tn), lambda j, i, kk: (j, i, 0))
        o_shape = jax.ShapeDtypeStruct((N_CHIPS, m, tn), out_dtype)
    else:
        o_spec = pl.BlockSpec((tm, tn), lambda j, i, kk: (i, j))
        o_shape = jax.ShapeDtypeStruct((m, n), out_dtype)
    return _mm_call(grid=(n // tn, m // tm, s // tk),
                    in_specs=[pl.BlockSpec((tk, tm), lambda j, i, kk: (kk, i)), b_spec], out_spec=o_spec,
                    out_shape=o_shape, dims=TN_DIMS, nk=s // tk, kaxis=2, acc_shape=(tm, tn), name=name, operands=(a, b),
                    riders=riders)


def _rstd(x):
    return lax.rsqrt(jnp.mean(x * x, axis=-1, keepdims=True) + EPS)


def _rms_bwd(dy, x, g):
    r = _rstd(x)
    xhat = x * r
    gy = dy * g
    dx = r * (gy - xhat * jnp.mean(gy * xhat, axis=-1, keepdims=True))
    return dx, jnp.sum(dy * xhat, axis=0, keepdims=True)


def _accum(ref, val, first):
    @pl.when(first)
    def _():
        ref[...] = val

    @pl.when(jnp.logical_not(first))
    def _():
        ref[...] += val


def _row_spec(tm, width):
    return pl.BlockSpec((tm, width), lambda i: (i, 0))


def _vec_spec(width):
    return pl.BlockSpec((1, width), lambda i: (0, 0))


def _ret(core, rider_res, riders):
    core = core[0] if len(core) == 1 else core
    return (core, rider_res) if riders else core


def prenorm(x, g, *, name, tm=256, riders=()):
    s = x.shape[0]
    tm = _row_tile(s, tm)

    def body(x_ref, g_ref, h_ref):
        xv = x_ref[...]
        h_ref[...] = (xv * _rstd(xv) * g_ref[...]).astype(BF16)

    core, rr = _call(
        body, grid=(s // tm,), in_specs=[_row_spec(tm, D_MODEL), _vec_spec(D_MODEL)], out_specs=[_row_spec(tm, D_MODEL)],
        out_shape=[jax.ShapeDtypeStruct((s, D_MODEL), BF16)], operands=(x, g), sem=("parallel",), name=name, riders=riders)
    return _ret(core, rr, riders)


def proj_residual_norm(a, w, x, bias, g_post, g_next, *, name, tm=256, riders=()):
    s, k = a.shape
    tm = _row_tile(s, tm)

    def body(a_ref, w_ref, x_ref, b_ref, gp_ref, gn_ref, xo_ref, h_ref, m_ref):
        mv = jnp.dot(a_ref[...], w_ref[...], preferred_element_type=F32) + b_ref[...]
        m_ref[...] = mv.astype(BF16)
        xn = x_ref[...] + mv * _rstd(mv) * gp_ref[...]
        xo_ref[...] = xn
        h_ref[...] = (xn * _rstd(xn) * gn_ref[...]).astype(BF16)

    row, vec = _row_spec(tm, D_MODEL), _vec_spec(D_MODEL)
    core, rr = _call(
        body, grid=(s // tm,),
        in_specs=[_row_spec(tm, k), pl.BlockSpec((k, D_MODEL), lambda i: (0, 0)), row, vec, vec, vec], out_specs=[row, row, row],
        out_shape=[jax.ShapeDtypeStruct((s, D_MODEL), F32), jax.ShapeDtypeStruct((s, D_MODEL), BF16),
                   jax.ShapeDtypeStruct((s, D_MODEL), BF16)],
        operands=(a, w, x, bias, g_post, g_next), sem=("parallel",), name=name, riders=riders)
    return _ret(core, rr, riders)


def proj_loss_head(a, w, x, g_post, target, *, name, tm=256, riders=()):
    s, k = a.shape
    tm = _row_tile(s, tm)

    def body(a_ref, w_ref, x_ref, g_ref, t_ref, dx_ref, df_ref, dg_ref, loss_ref):
        first = pl.program_id(0) == 0
        fv = jnp.dot(a_ref[...], w_ref[...], preferred_element_type=F32)
        g = g_ref[...]
        err = x_ref[...] + fv * _rstd(fv) * g - t_ref[...]
        dx = err * (1.0 / D_MODEL)
        dx_ref[...] = dx
        df, dg = _rms_bwd(dx, fv, g)
        df_ref[...] = df.astype(BF16)
        _accum(dg_ref, dg, first)
        part = jnp.sum(jnp.sum(err * err, axis=-1, keepdims=True), axis=0, keepdims=True) * (0.5 / D_MODEL)
        _accum(loss_ref, jnp.broadcast_to(part, (8, LANES)), first)

    row, vec = _row_spec(tm, D_MODEL), _vec_spec(D_MODEL)
    core, rr = _call(
        body, grid=(s // tm,), in_specs=[_row_spec(tm, k), pl.BlockSpec((k, D_MODEL), lambda i: (0, 0)), row, vec, row],
        out_specs=[row, row, vec, pl.BlockSpec((8, LANES), lambda i: (0, 0))],
        out_shape=[jax.ShapeDtypeStruct((s, D_MODEL), F32), jax.ShapeDtypeStruct((s, D_MODEL), BF16),
                   jax.ShapeDtypeStruct((1, D_MODEL), F32), jax.ShapeDtypeStruct((8, LANES), F32)],
        operands=(a, w, x, g_post, target), name=name, riders=riders)
    return _ret(core, rr, riders)


def dh_norm_bwd_pair(a, w, dres, x, g_pre, m, g_post, *, name, tm=256, riders=()):
    _, kout, ns = w.shape
    planes = a.ndim == 3
    s = x.shape[0]
    tm = _row_tile(s, tm)
    a_spec = pl.BlockSpec((2, tm, 2 * ns), lambda i: (0, i, 0)) if planes else pl.BlockSpec((tm, N_CHIPS * ns), lambda i: (i, 0))

    def body(a_ref, w0, w1, w2, w3, dres_ref, x_ref, gpre_ref, m_ref, gpost_ref, dx_ref, dm_ref, dgpre_ref, dgpost_ref, db_ref):
        first = pl.program_id(0) == 0
        dh = None
        for j, w_ref in enumerate((w0, w1, w2, w3)):
            a_j = a_ref[j // 2, :, (j % 2) * ns:(j % 2 + 1) * ns] if planes else a_ref[:, j * ns:(j + 1) * ns]
            p = lax.dot_general(a_j, w_ref[...], NT_DIMS, preferred_element_type=F32)
            dh = p if dh is None else dh + p
        d1, dgpre = _rms_bwd(dh, x_ref[...], gpre_ref[...])
        dx = dres_ref[...] + d1
        dx_ref[...] = dx
        dm, dgpost = _rms_bwd(dx, m_ref[...].astype(F32), gpost_ref[...])
        dm_ref[...] = dm.astype(BF16)
        _accum(dgpre_ref, dgpre, first)
        _accum(dgpost_ref, dgpost, first)
        _accum(db_ref, jnp.sum(dm, axis=0, keepdims=True), first)

    def shard(j):
        return pl.BlockSpec((None, kout, ns), lambda i: (j, 0, 0))

    row, vec = _row_spec(tm, D_MODEL), _vec_spec(D_MODEL)
    vshape = jax.ShapeDtypeStruct((1, D_MODEL), F32)
    core, rr = _call(
        body, grid=(s // tm,), in_specs=[a_spec] + [shard(j) for j in range(N_CHIPS)] + [row, row, vec, row, vec],
        out_specs=[row, row, vec, vec, vec],
        out_shape=[jax.ShapeDtypeStruct((s, D_MODEL), F32), jax.ShapeDtypeStruct((s, D_MODEL), BF16), vshape, vshape, vshape],
        operands=(a, w, w, w, w, dres, x, g_pre, m, g_post), name=name, riders=riders)
    return _ret(core, rr, riders)


def norm_bwd_last(dres, dh, x, g_pre, *, name, tm=256, riders=()):
    s = x.shape[0]
    tm = _row_tile(s, tm)

    def body(dres_ref, dh_ref, x_ref, g_ref, dx_ref, dg_ref):
        d1, dg = _rms_bwd(dh_ref[...], x_ref[...], g_ref[...])
        dx_ref[...] = dres_ref[...] + d1
        _accum(dg_ref, dg, pl.program_id(0) == 0)

    row, vec = _row_spec(tm, D_MODEL), _vec_spec(D_MODEL)
    core, rr = _call(
        body, grid=(s // tm,), in_specs=[row, row, row, vec], out_specs=[row, vec],
        out_shape=[jax.ShapeDtypeStruct((s, D_MODEL), F32), jax.ShapeDtypeStruct((1, D_MODEL), F32)],
        operands=(dres, dh, x, g_pre), name=name, riders=riders)
    return _ret(core, rr, riders)


def _rope_tables(s):
    half = HEAD_DIM // 2
    inv_freq = ROPE_THETA ** (-(jnp.arange(half, dtype=F32) * 2.0) / HEAD_DIM)
    ang = jnp.arange(s, dtype=I32).astype(F32)[:, None] * inv_freq[None, :]
    cos, sin = jnp.cos(ang), jnp.sin(ang)
    return jnp.tile(cos, (1, 4)), jnp.concatenate([-sin, sin, -sin, sin], axis=1)


def _swap_halves(x):
    lane = lax.broadcasted_iota(I32, x.shape, 1)
    return jnp.where((lane & (HEAD_DIM - 1)) < HEAD_DIM // 2, pltpu.roll(x, LANES - 32, 1), pltpu.roll(x, 32, 1))


N_ROPE_BLOCKS = (Q_WIDTH + KV_WIDTH) // LANES


def qkv_proj(h, w, bias, cos, sin, *, name, tm=512, riders=()):
    s, k = h.shape
    ns = w.shape[2]
    tm = _row_tile(s, tm)

    def body(h_ref, w_ref, b_ref, c_ref, s_ref, o_ref):
        j = pl.program_id(0)
        p = jnp.dot(h_ref[...], w_ref[...], preferred_element_type=F32) + b_ref[...]
        cosv, sinv = c_ref[...], s_ref[...]
        for blk in range(ns // LANES):
            xb = p[:, blk * LANES:(blk + 1) * LANES]
            roped = xb * cosv + _swap_halves(xb) * sinv
            is_qk = j * (ns // LANES) + blk < N_ROPE_BLOCKS
            o_ref[:, blk * LANES:(blk + 1) * LANES] = jnp.where(is_qk, roped, xb).astype(BF16)

    core, rr = _call(
        body, grid=(N_CHIPS, s // tm),
        in_specs=[pl.BlockSpec((tm, k), lambda j, i: (i, 0)), pl.BlockSpec((None, k, ns), lambda j, i: (j, 0, 0)),
                  pl.BlockSpec((1, ns), lambda j, i: (0, j)), pl.BlockSpec((tm, LANES), lambda j, i: (i, 0)),
                  pl.BlockSpec((tm, LANES), lambda j, i: (i, 0))],
        out_specs=[pl.BlockSpec((tm, ns), lambda j, i: (i, j))], out_shape=[jax.ShapeDtypeStruct((s, N_CHIPS * ns), BF16)],
        operands=(h, w, bias, cos, sin), sem=("parallel", "parallel"), name=name, riders=riders)
    return _ret(core, rr, riders)


def rope_bwd(dq, dkc, dkp, dvc, dvp, cos, sin, *, name, riders=()):
    s = dq.shape[0]
    tm = WINDOW
    nb = s // tm

    def body(dq_ref, dkc_ref, dkp_ref, dvc_ref, dvp_ref, c_ref, s_ref, o_ref, db_ref):
        i = pl.program_id(0)
        has_next = (i < nb - 1).astype(F32)
        cosv, sinv = c_ref[...], s_ref[...]
        for blk in range(QKV_WIDTH // LANES):
            if blk < Q_WIDTH // LANES:
                g = dq_ref[:, blk * LANES:(blk + 1) * LANES].astype(F32)
            else:
                own, nxt = (dkc_ref, dkp_ref) if blk < N_ROPE_BLOCKS else (dvc_ref, dvp_ref)
                cols = slice((blk % 2) * LANES, (blk % 2 + 1) * LANES)
                g = own[:, cols].astype(F32) + has_next * nxt[:, cols].astype(F32)
            if blk < N_ROPE_BLOCKS:
                g = g * cosv + _swap_halves(g * sinv)
            o_ref[:, blk * LANES:(blk + 1) * LANES] = g.astype(BF16)
            part = jnp.sum(g, axis=0, keepdims=True)

            @pl.when(i == 0)
            def _():
                db_ref[:, blk * LANES:(blk + 1) * LANES] = part

            @pl.when(i > 0)
            def _():
                db_ref[:, blk * LANES:(blk + 1) * LANES] += part

    own_spec = _row_spec(tm, KV_WIDTH)
    next_spec = pl.BlockSpec((tm, KV_WIDTH), lambda i: (jnp.minimum(i + 1, nb - 1), 0))
    core, rr = _call(
        body, grid=(nb,),
        in_specs=[_row_spec(tm, Q_WIDTH), own_spec, next_spec, own_spec, next_spec, _row_spec(tm, LANES), _row_spec(tm, LANES)],
        out_specs=[_row_spec(tm, QKV_WIDTH), _vec_spec(QKV_WIDTH)],
        out_shape=[jax.ShapeDtypeStruct((s, QKV_WIDTH), BF16), jax.ShapeDtypeStruct((1, QKV_WIDTH), F32)],
        operands=(dq, dkc, dkp, dvc, dvp, cos, sin), name=name, riders=riders)
    return _ret(core, rr, riders)


ROWS = GQA_GROUP * WINDOW


def _prev_slots():
    kpos = lax.broadcasted_iota(I32, (WINDOW, ROWS), 0)
    qpos = lax.broadcasted_iota(I32, (WINDOW, ROWS), 1) & (WINDOW - 1)
    return kpos > qpos


def _head_cols(ref, head):
    return ref[:, head * HEAD_DIM:(head + 1) * HEAD_DIM]


def _stack_heads(ref, h):
    return jnp.concatenate([_head_cols(ref, GQA_GROUP * h + g) for g in range(GQA_GROUP)], axis=0)


def _band(prev_ref, cur_ref, h):
    return jnp.concatenate([_head_cols(prev_ref, h), _head_cols(cur_ref, h)], axis=0)


def _pick(prev, band):
    return jnp.where(prev, band[:WINDOW], band[WINDOW:])


def _spread(prev, x):
    return jnp.concatenate([jnp.where(prev, x, 0.0), jnp.where(prev, 0.0, x)], axis=0).astype(BF16)


def _attn_probs(q, kband, sink, prev, has_prev):
    scale = HEAD_DIM ** -0.5
    s_band = lax.dot_general(kband, q, NT_DIMS, preferred_element_type=F32)
    s = jnp.where(prev, jnp.where(has_prev, s_band[:WINDOW], NEG), s_band[WINDOW:]) * scale
    m = jnp.maximum(jnp.max(s, axis=0, keepdims=True), sink)
    e, es = jnp.exp(s - m), jnp.exp(sink - m)
    inv = 1.0 / (jnp.sum(e, axis=0, keepdims=True) + es)
    return e * inv, es * inv


def _attn_specs(nb):
    kcol, vcol = Q_WIDTH // KV_WIDTH, Q_WIDTH // KV_WIDTH + 1
    q_spec = pl.BlockSpec((WINDOW, Q_WIDTH), lambda n: (n, 0))
    return [q_spec,
            pl.BlockSpec((WINDOW, KV_WIDTH), lambda n: (n, kcol)),
            pl.BlockSpec((WINDOW, KV_WIDTH), lambda n: (jnp.maximum(n - 1, 0), kcol)),
            pl.BlockSpec((WINDOW, KV_WIDTH), lambda n: (n, vcol)),
            pl.BlockSpec((WINDOW, KV_WIDTH), lambda n: (jnp.maximum(n - 1, 0), vcol)),
            pl.BlockSpec((N_KV_HEADS, 8, ROWS), lambda n: (0, 0, 0))]


def attn_fwd(qkv, sink_rows, *, name, riders=()):
    s = qkv.shape[0]

    def body(q_ref, kc_ref, kp_ref, vc_ref, vp_ref, sink_ref, o_ref):
        prev = _prev_slots()
        has_prev = pl.program_id(0) > 0
        for h in range(N_KV_HEADS):
            p, _ = _attn_probs(_stack_heads(q_ref, h), _band(kp_ref, kc_ref, h), sink_ref[h, 0:1, :], prev, has_prev)
            o = lax.dot_general(_band(vp_ref, vc_ref, h), _spread(prev, p), TN_DIMS, preferred_element_type=F32).T
            for g in range(GQA_GROUP):
                head = GQA_GROUP * h + g
                o_ref[:, head * HEAD_DIM:(head + 1) * HEAD_DIM] = o[g * WINDOW:(g + 1) * WINDOW].astype(BF16)

    core, rr = _call(
        body, grid=(s // WINDOW,), in_specs=_attn_specs(s // WINDOW), out_specs=[pl.BlockSpec((WINDOW, Q_WIDTH), lambda n: (n, 0))],
        out_shape=[jax.ShapeDtypeStruct((s, Q_WIDTH), BF16)], operands=(qkv, qkv, qkv, qkv, qkv, sink_rows), sem=("parallel",),
        name=name, riders=riders)
    return _ret(core, rr, riders)


def attn_bwd(qkv, sink_rows, do, *, name, riders=()):
    s = qkv.shape[0]

    def body(q_ref, kc_ref, kp_ref, vc_ref, vp_ref, sink_ref, do_ref, dq_ref, dkc_ref, dkp_ref, dvc_ref, dvp_ref, dsink_ref):
        n = pl.program_id(0)
        prev = _prev_slots()
        scale = HEAD_DIM ** -0.5
        parts = []
        for h in range(N_KV_HEADS):
            qv, dov = _stack_heads(q_ref, h), _stack_heads(do_ref, h)
            kband, vband = _band(kp_ref, kc_ref, h), _band(vp_ref, vc_ref, h)
            p, ps = _attn_probs(qv, kband, sink_ref[h, 0:1, :], prev, n > 0)
            dp = _pick(prev, lax.dot_general(vband, dov, NT_DIMS, preferred_element_type=F32))
            delta = jnp.sum(p * dp, axis=0, keepdims=True)
            ds_band = _spread(prev, p * (dp - delta) * scale)
            p_band = _spread(prev, p)
            dk = jnp.dot(ds_band, qv, preferred_element_type=F32).astype(BF16)
            dv = jnp.dot(p_band, dov, preferred_element_type=F32).astype(BF16)
            dq = lax.dot_general(kband, ds_band, TN_DIMS, preferred_element_type=F32).T
            cols = slice(h * HEAD_DIM, (h + 1) * HEAD_DIM)
            dkp_ref[:, cols], dkc_ref[:, cols] = dk[:WINDOW], dk[WINDOW:]
            dvp_ref[:, cols], dvc_ref[:, cols] = dv[:WINDOW], dv[WINDOW:]
            dsink = -(ps * delta)
            for g in range(GQA_GROUP):
                head = GQA_GROUP * h + g
                dq_ref[:, head * HEAD_DIM:(head + 1) * HEAD_DIM] = dq[g * WINDOW:(g + 1) * WINDOW].astype(BF16)
                parts.append(jnp.broadcast_to(jnp.sum(dsink[:, g * WINDOW:(g + 1) * WINDOW], axis=1, keepdims=True), (8, LANES)))

        @pl.when(n == 0)
        def _():
            for i, part in enumerate(parts):
                dsink_ref[i // GQA_GROUP, i % GQA_GROUP] = part

        @pl.when(n > 0)
        def _():
            for i, part in enumerate(parts):
                dsink_ref[i // GQA_GROUP, i % GQA_GROUP] += part

    rows_q = pl.BlockSpec((WINDOW, Q_WIDTH), lambda n: (n, 0))
    rows_kv = pl.BlockSpec((WINDOW, KV_WIDTH), lambda n: (n, 0))
    kv_shape = jax.ShapeDtypeStruct((s, KV_WIDTH), BF16)
    core, rr = _call(
        body, grid=(s // WINDOW,), in_specs=_attn_specs(s // WINDOW) + [rows_q],
        out_specs=[rows_q, rows_kv, rows_kv, rows_kv, rows_kv,
                   pl.BlockSpec((N_KV_HEADS, GQA_GROUP, 8, LANES), lambda n: (0, 0, 0, 0))],
        out_shape=[jax.ShapeDtypeStruct((s, Q_WIDTH), BF16), kv_shape, kv_shape, kv_shape, kv_shape,
                   jax.ShapeDtypeStruct((N_KV_HEADS, GQA_GROUP, 8, LANES), F32)],
        operands=(qkv, qkv, qkv, qkv, qkv, sink_rows, do), sem=("arbitrary",), name=name, riders=riders)
    return _ret(core, rr, riders)


GELU_C = 0.7978845608028654
GELU_A = 0.044715


def _gelu(x):
    return 0.5 * x * (1.0 + jnp.tanh(GELU_C * (x + GELU_A * x * x * x)))


def _gelu_grad(x):
    t = jnp.tanh(GELU_C * (x + GELU_A * x * x * x))
    return 0.5 * (1.0 + t) + 0.5 * x * (1.0 - t * t) * GELU_C * (1.0 + 3.0 * GELU_A * x * x)


def _tril_bf16(w):
    row = lax.broadcasted_iota(I32, (SGU_CHUNK, SGU_CHUNK), 0)
    col = lax.broadcasted_iota(I32, (SGU_CHUNK, SGU_CHUNK), 1)
    return jnp.where(row >= col, w, 0.0).astype(BF16)


def _sgu_norm(vg, g, b):
    mu = jnp.mean(vg, axis=-1, keepdims=True)
    cen = vg - mu
    rstd = lax.rsqrt(jnp.mean(cen * cen, axis=-1, keepdims=True) + EPS)
    xhat = cen * rstd
    return xhat, rstd, xhat * g + b


def sgu_fwd(z, ln_g, ln_b, w_sp, b_sp, *, name, tm=256, riders=()):
    s = z.shape[0]
    tm = _row_tile(s, tm)

    def body(z_ref, g_ref, b_ref, w_ref, bs_ref, y_ref):
        u = _gelu(z_ref[:, :D_MODEL])
        _, _, vn = _sgu_norm(_gelu(z_ref[:, D_MODEL:]), g_ref[...], b_ref[...])
        vn = vn.astype(BF16)
        for grp in range(SGU_GROUPS):
            w = _tril_bf16(w_ref[grp])
            cols = slice(grp * LANES, (grp + 1) * LANES)
            for ch in range(tm // SGU_CHUNK):
                rows = slice(ch * SGU_CHUNK, (ch + 1) * SGU_CHUNK)
                mixed = jnp.dot(w, vn[rows, cols], preferred_element_type=F32) + bs_ref[grp]
                y_ref[rows, cols] = (u[rows, cols] * mixed).astype(BF16)

    full3 = pl.BlockSpec((SGU_GROUPS, SGU_CHUNK, SGU_CHUNK), lambda i: (0, 0, 0))
    core, rr = _call(
        body, grid=(s // tm,), in_specs=[_row_spec(tm, 2 * D_MODEL), _vec_spec(D_MODEL), _vec_spec(D_MODEL), full3, full3],
        out_specs=[_row_spec(tm, D_MODEL)], out_shape=[jax.ShapeDtypeStruct((s, D_MODEL), BF16)],
        operands=(z, ln_g, ln_b, w_sp, b_sp), sem=("parallel",), name=name, riders=riders)
    return _ret(core, rr, riders)


def sgu_bwd(z, dy, ln_g, ln_b, w_sp, b_sp, *, name, tm=256, riders=()):
    s = z.shape[0]
    tm = _row_tile(s, tm)

    def body(z_ref, dy_ref, g_ref, b_ref, w_ref, bs_ref, dz_ref, dw_ref, dbs_ref, dg_ref, db_ref, dvn_buf):
        first = pl.program_id(0) == 0
        zu, zv = z_ref[:, :D_MODEL], z_ref[:, D_MODEL:]
        u = _gelu(zu)
        xhat, rstd, vn = _sgu_norm(_gelu(zv), g_ref[...], b_ref[...])
        vn = vn.astype(BF16)
        dyv = dy_ref[...]
        dmixed = dyv * u
        row = lax.broadcasted_iota(I32, (SGU_CHUNK, SGU_CHUNK), 0)
        col = lax.broadcasted_iota(I32, (SGU_CHUNK, SGU_CHUNK), 1)
        for grp in range(SGU_GROUPS):
            w = _tril_bf16(w_ref[grp])
            cols = slice(grp * LANES, (grp + 1) * LANES)
            dw = jnp.zeros((SGU_CHUNK, SGU_CHUNK), F32)
            dbs = jnp.zeros((SGU_CHUNK, 1), F32)
            for ch in range(tm // SGU_CHUNK):
                rows = slice(ch * SGU_CHUNK, (ch + 1) * SGU_CHUNK)
                vblk = vn[rows, cols]
                mixed = jnp.dot(w, vblk, preferred_element_type=F32) + bs_ref[grp]
                dz_ref[rows, cols] = (dyv[rows, cols] * mixed * _gelu_grad(zu[rows, cols])).astype(BF16)
                dm = dmixed[rows, cols]
                dmb = dm.astype(BF16)
                dvn_buf[rows, cols] = lax.dot_general(w, dmb, TN_DIMS, preferred_element_type=F32)
                dw += lax.dot_general(dmb, vblk, NT_DIMS, preferred_element_type=F32)
                dbs += jnp.sum(dm, axis=-1, keepdims=True)
            dw = jnp.where(row >= col, dw, 0.0)
            dbs = jnp.broadcast_to(dbs, (SGU_CHUNK, SGU_CHUNK))

            @pl.when(first)
            def _():
                dw_ref[grp] = dw
                dbs_ref[grp] = dbs

            @pl.when(jnp.logical_not(first))
            def _():
                dw_ref[grp] += dw
                dbs_ref[grp] += dbs

        dvn = dvn_buf[...]
        dxhat = dvn * g_ref[...]
        dvg = rstd * (dxhat - jnp.mean(dxhat, axis=-1, keepdims=True) - xhat * jnp.mean(dxhat * xhat, axis=-1, keepdims=True))
        dz_ref[:, D_MODEL:] = (dvg * _gelu_grad(zv)).astype(BF16)
        _accum(dg_ref, jnp.sum(dvn * xhat, axis=0, keepdims=True), first)
        _accum(db_ref, jnp.sum(dvn, axis=0, keepdims=True), first)

    full3 = pl.BlockSpec((SGU_GROUPS, SGU_CHUNK, SGU_CHUNK), lambda i: (0, 0, 0))
    s3 = jax.ShapeDtypeStruct((SGU_GROUPS, SGU_CHUNK, SGU_CHUNK), F32)
    vshape = jax.ShapeDtypeStruct((1, D_MODEL), F32)
    core, rr = _call(
        body, grid=(s // tm,),
        in_specs=[_row_spec(tm, 2 * D_MODEL), _row_spec(tm, D_MODEL), _vec_spec(D_MODEL), _vec_spec(D_MODEL), full3, full3],
        out_specs=[_row_spec(tm, 2 * D_MODEL), full3, full3, _vec_spec(D_MODEL), _vec_spec(D_MODEL)],
        out_shape=[jax.ShapeDtypeStruct((s, 2 * D_MODEL), BF16), s3, s3, vshape, vshape],
        scratch_shapes=[pltpu.VMEM((tm, D_MODEL), F32)], operands=(z, dy, ln_g, ln_b, w_sp, b_sp), name=name, riders=riders)
    return _ret(core, rr, riders)


def _sigmoid(x):
    return 1.0 / (1.0 + jnp.exp(-x))


def ffn_up(h, w_gu, *, name, tm=512, riders=()):
    s = h.shape[0]
    tm = _row_tile(s, tm)

    def body(h_ref, wg_ref, wu_ref, gu_ref, a_ref):
        hv = h_ref[...]
        g = jnp.dot(hv, wg_ref[...], preferred_element_type=F32)
        u = jnp.dot(hv, wu_ref[...], preferred_element_type=F32)
        gu_ref[0] = g.astype(BF16)
        gu_ref[1] = u.astype(BF16)
        a_ref[...] = (g * _sigmoid(g) * u).astype(BF16)

    core, rr = _call(
        body, grid=(2, s // tm),
        in_specs=[pl.BlockSpec((tm, D_MODEL), lambda j, i: (i, 0)),
                  pl.BlockSpec((None, D_MODEL, FF_HALF), lambda j, i: (j, 0, 0)),
                  pl.BlockSpec((None, D_MODEL, FF_HALF), lambda j, i: (j + 2, 0, 0))],
        out_specs=[pl.BlockSpec((2, tm, FF_HALF), lambda j, i: (0, i, j)), pl.BlockSpec((tm, FF_HALF), lambda j, i: (i, j))],
        out_shape=[jax.ShapeDtypeStruct((2, s, D_FF), BF16), jax.ShapeDtypeStruct((s, D_FF), BF16)],
        operands=(h, w_gu, w_gu), sem=("parallel", "parallel"), name=name, riders=riders)
    return _ret(core, rr, riders)


def ffn_dact(df, w_d, gu, *, name, tm=512, riders=()):
    s = df.shape[0]
    tm = _row_tile(s, tm)

    def body(df_ref, w_ref, gu_ref, o_ref):
        da = lax.dot_general(df_ref[...], w_ref[...], NT_DIMS, preferred_element_type=F32)
        g = gu_ref[0].astype(F32)
        u = gu_ref[1].astype(F32)
        sig = _sigmoid(g)
        o_ref[0] = (da * u * sig * (1.0 + g * (1.0 - sig))).astype(BF16)
        o_ref[1] = (da * g * sig).astype(BF16)

    planes = pl.BlockSpec((2, tm, FF_HALF), lambda j, i: (0, i, j))
    core, rr = _call(
        body, grid=(2, s // tm),
        in_specs=[pl.BlockSpec((tm, D_MODEL), lambda j, i: (i, 0)), pl.BlockSpec((FF_HALF, D_MODEL), lambda j, i: (j, 0)), planes],
        out_specs=[planes], out_shape=[jax.ShapeDtypeStruct((2, s, D_FF), BF16)], operands=(df, w_d, gu),
        sem=("parallel", "parallel"), name=name, riders=riders)
    return _ret(core, rr, riders)


def _weight_tile(rows):
    for tr in (512, 352, 256, 128):
        if rows % tr == 0:
            return tr
    return rows


def place_shard(w, layer, chip_arr, dtype, *, name, riders=()):
    _, r, c = w.shape
    tr = _weight_tile(r)

    def body(chip_ref, w_ref, o_ref):
        o_ref[...] = w_ref[...].astype(dtype)

    core, rr = _call(
        body, grid=(r // tr,), prefetch=(chip_arr,),
        in_specs=[pl.BlockSpec((None, tr, c), lambda i, chip: (layer, i, 0))],
        out_specs=[pl.BlockSpec((None, tr, c), lambda i, chip: (chip[0], i, 0))],
        out_shape=[jax.ShapeDtypeStruct((N_CHIPS, r, c), dtype)], operands=(w,), sem=("parallel",), name=name, riders=riders)
    return _ret(core, rr, riders)


def _adamw_math(w, g, m, v):
    m = ADAM_B1 * m + (1.0 - ADAM_B1) * g
    v = ADAM_B2 * v + (1.0 - ADAM_B2) * (g * g)
    m_hat = m / (1.0 - ADAM_B1 ** ADAM_STEP)
    v_hat = v / (1.0 - ADAM_B2 ** ADAM_STEP)
    delta = -ADAM_LR * (m_hat / (jnp.sqrt(v_hat) + ADAM_EPS) + ADAM_WD * w)
    return delta, m, v


def adamw(w, g, m, v, *, name):
    nl, r, c = w.shape
    tr = _weight_tile(r)

    def body(w_ref, g_ref, m_ref, v_ref, go_ref, d_ref, mo_ref, vo_ref):
        gv = g_ref[...]
        go_ref[...] = gv
        d_ref[...], mo_ref[...], vo_ref[...] = _adamw_math(w_ref[...], gv, m_ref[...], v_ref[...])

    spec = pl.BlockSpec((None, tr, c), lambda l, i: (l, i, 0))
    shape = jax.ShapeDtypeStruct(w.shape, F32)
    outs, _ = _call(body, grid=(nl, r // tr), in_specs=[spec] * 4, out_specs=[spec] * 4, out_shape=[shape] * 4,
                    operands=(w, g, m, v), sem=("parallel", "parallel"), name=name)
    return outs


def adamw_small(ws, gs, ms, vs, *, name):
    n = len(ws)

    def body(*refs):
        ins, outs = refs[:4 * n], refs[4 * n:]
        for t in range(n):
            gv = ins[n + t][...]
            outs[t][...] = gv
            outs[n + t][...], outs[2 * n + t][...], outs[3 * n + t][...] = _adamw_math(
                ins[t][...], gv, ins[2 * n + t][...], ins[3 * n + t][...])

    shapes = [jax.ShapeDtypeStruct(w.shape, F32) for w in ws]
    res = pl.pallas_call(body, out_shape=shapes * 4, name=name)(*ws, *gs, *ms, *vs)
    return res[:n], res[n:2 * n], res[2 * n:3 * n], res[3 * n:]


def pair_add(g, r1, c_arr, *, name):
    _, rows, cdim = g.shape
    h = rows // 2

    def body(c_ref, g_ref, r_ref, o_ref):
        o_ref[...] = (g_ref[...].astype(F32) + r_ref[...].astype(F32)).astype(o_ref.dtype)

    (out,), _ = _call(
        body, grid=(N_CHIPS,), prefetch=(c_arr,),
        in_specs=[pl.BlockSpec((None, h, cdim), lambda s, c: (s, c[0], 0)), pl.BlockSpec((None, h, cdim), lambda s, c: (s, 0, 0))],
        out_specs=[pl.BlockSpec((None, h, cdim), lambda s, c: (s, 0, 0))],
        out_shape=[jax.ShapeDtypeStruct((N_CHIPS, h, cdim), g.dtype)], operands=(g, r1), sem=("parallel",), name=name)
    return out


def final_add(g, r1, r2, jc_arr, *, dest_shape, lead, prev, name):
    _, rows, cdim = g.shape
    h = rows // 2

    def body(jc_ref, g_ref, r1_ref, r2_ref, *rest):
        o_ref = rest[-1]
        acc = g_ref[...].astype(F32) + r1_ref[...].astype(F32)
        for k in range(3):
            acc = acc + r2_ref[k].astype(F32)
        o_ref[...] = acc

    if lead is None:
        o_spec = pl.BlockSpec((h, cdim), lambda i, jc: (jc[1], 0))
    elif lead == "chip":
        o_spec = pl.BlockSpec((None, h, cdim), lambda i, jc: (jc[0], jc[1], 0))
    else:
        o_spec = pl.BlockSpec((None, h, cdim), lambda i, jc: (lead, jc[1], 0))
    in_specs = [pl.BlockSpec((None, h, cdim), lambda i, jc: (jc[0], jc[1], 0)),
                pl.BlockSpec((None, h, cdim), lambda i, jc: (jc[0], 0, 0)),
                pl.BlockSpec((3, h, cdim), lambda i, jc: (0, 0, 0))]
    operands = [g, r1, r2]
    aliases = None
    if prev is not None:
        in_specs.append(ANY)
        operands.append(prev)
        aliases = {3: 0}
    (out,), _ = _call(body, grid=(1,), prefetch=(jc_arr,), in_specs=in_specs, out_specs=[o_spec],
                      out_shape=[jax.ShapeDtypeStruct(dest_shape, F32)], operands=operands, aliases=aliases, name=name)
    return out


def _place():
    return lax.axis_index("x"), lax.axis_index("y"), lax.axis_index("c")


def _partner(x, y, k):
    return (1 - x if k >> 1 else x), (1 - y if k & 1 else y)


def _half(rows, sel, dtype):
    align = 16 if dtype == BF16 else 8
    return pl.ds(pl.multiple_of(sel * (rows // 2), align), rows // 2)


def _rider(inputs, aliased, fresh, nsem, copies, arrivals):
    def start(ins, outs, send, recv):
        for cp in copies(ins, outs, send, recv):
            cp.start()

    def finish(ins, outs, send, recv):
        for cp in arrivals(ins, outs, send, recv):
            cp.wait_recv()
        for cp in copies(ins, outs, send, recv):
            cp.wait_send()

    return types.SimpleNamespace(inputs=list(inputs), aliased=list(aliased), fresh=list(fresh), nsem=nsem, start=start,
                                 finish=finish)


def _remote(src, dst, send, recv, idx, dev):
    return pltpu.make_async_remote_copy(src_ref=src, dst_ref=dst, send_sem=send.at[idx], recv_sem=recv.at[idx],
                                        device_id=dev, device_id_type=MESH)


def gather_ici_rider(fulls):
    nt = len(fulls)

    def region(outs, t, slot, sel):
        return outs[t].at[slot, _half(fulls[t].shape[1], sel, fulls[t].dtype)]

    def copies(ins, outs, send, recv):
        x, y, c = _place()
        res = []
        for t in range(nt):
            for k in (1, 2, 3):
                px, py = _partner(x, y, k)
                mine = region(outs, t, 2 * x + y, c)
                res.append(_remote(mine, mine, send, recv, 3 * t + k - 1, (px, py, c)))
        return res

    def arrivals(ins, outs, send, recv):
        x, y, c = _place()
        res = []
        for t in range(nt):
            for k in (1, 2, 3):
                px, py = _partner(x, y, k)
                theirs = region(outs, t, 2 * px + py, c)
                res.append(_remote(theirs, theirs, send, recv, 3 * t + k - 1, (x, y, c)))
        return res

    return _rider(fulls, range(nt), [], 3 * nt, copies, arrivals)


def gather_d2d_rider(fulls):
    nt = len(fulls)

    def region(outs, t, slot, sel):
        return outs[t].at[slot, _half(fulls[t].shape[1], sel, fulls[t].dtype)]

    def both(outs, send, recv, mine):
        x, y, c = _place()
        res = []
        for t in range(nt):
            for k in (1, 2, 3):
                px, py = _partner(x, y, k)
                part = region(outs, t, 2 * px + py, c if mine else 1 - c)
                res.append(_remote(part, part, send, recv, 3 * t + k - 1, (x, y, 1 - c)))
        return res

    return _rider(fulls, range(nt), [], 3 * nt, lambda i, o, s, r: both(o, s, r, True), lambda i, o, s, r: both(o, s, r, False))


def exchange_rider(grads):
    nt = len(grads)

    def both(ins, outs, send, recv):
        x, y, c = _place()
        return [_remote(ins[t].at[:, _half(grads[t].shape[1], 1 - c, grads[t].dtype)], outs[t], send, recv, t, (x, y, 1 - c))
                for t in range(nt)]

    fresh = [jax.ShapeDtypeStruct((N_CHIPS, g.shape[1] // 2, g.shape[2]), g.dtype) for g in grads]
    return _rider(grads, [], fresh, nt, both, both)


def scatter_rider(parts):
    nt = len(parts)

    def both(ins, outs, send, recv):
        x, y, c = _place()
        res = []
        for t in range(nt):
            for k in (1, 2, 3):
                px, py = _partner(x, y, k)
                res.append(_remote(ins[t].at[2 * px + py], outs[t].at[k - 1], send, recv, 3 * t + k - 1, (px, py, c)))
        return res

    fresh = [jax.ShapeDtypeStruct((3,) + p.shape[1:], p.dtype) for p in parts]
    return _rider(parts, [], fresh, 3 * nt, both, both)


def broadcast_rider(bufs, items):
    def region(outs, item, sel):
        bi, lead = item
        ref = outs[bi]
        if lead == "chip":
            x, y, _ = _place()
            ref = ref.at[2 * x + y]
        elif lead is not None:
            ref = ref.at[lead]
        return ref.at[_half(ref.shape[0], sel, F32)]

    def both(outs, send, recv, mine):
        x, y, c = _place()
        res = []
        for i, item in enumerate(items):
            part = region(outs, item, c if mine else 1 - c)
            res.append(_remote(part, part, send, recv, i, (x, y, 1 - c)))
        return res

    return _rider(bufs, range(len(bufs)), [], len(items), lambda i, o, s, r: both(o, s, r, True),
                  lambda i, o, s, r: both(o, s, r, False))


def allcast_rider(buf):
    peers = [(k, flip) for k in range(N_CHIPS) for flip in (0, 1) if (k, flip) != (0, 0)]

    def both(outs, send, recv, mine):
        x, y, c = _place()
        res = []
        for i, (k, flip) in enumerate(peers):
            px, py = _partner(x, y, k)
            pc = 1 - c if flip else c
            slot, sel = (2 * x + y, c) if mine else (2 * px + py, pc)
            part = outs[0].at[slot, _half(buf.shape[1], sel, F32)]
            res.append(_remote(part, part, send, recv, i, (px, py, pc)))
        return res

    return _rider([buf], [0], [], len(peers), lambda i, o, s, r: both(o, s, r, True), lambda i, o, s, r: both(o, s, r, False))


def comm_call(riders, *, name):
    _, res = _call(None, riders=riders, name=name)
    return res


SLAB_ROWS = 192


def _pad_rows(a, rows=8):
    return jnp.pad(a, ((0, rows - a.shape[0]), (0, 0)))


def _pack_small(norm_grads, db_qkv, db_o, dsinks, db_sp, dln_g, dln_b, dw_sp):
    parts = [
        jnp.concatenate(norm_grads, axis=0),
        _pad_rows(jnp.pad(db_qkv, ((0, 0), (0, 2 * D_MODEL - QKV_WIDTH))).reshape(2, D_MODEL)),
        _pad_rows(db_o),
        _pad_rows(jnp.pad(dsinks.reshape(1, N_Q_HEADS), ((0, 0), (0, D_MODEL - N_Q_HEADS)))),
        _pad_rows(db_sp.reshape(1, D_MODEL)),
        _pad_rows(jnp.concatenate([dln_g, dln_b], axis=0)),
        dw_sp.reshape(SGU_CHUNK, D_MODEL),
    ]
    slab = jnp.concatenate(parts, axis=0)
    return jnp.pad(slab, ((0, SLAB_ROWS - slab.shape[0]), (0, 0))).reshape(N_CHIPS, SLAB_ROWS // N_CHIPS, D_MODEL)


def _unpack_small(slab, j):
    slab = slab.reshape(SLAB_ROWS, D_MODEL)
    norms = [slab[2 * i:2 * i + 2] for i in range(4)]
    db_qkv = slab[8:10].reshape(1, 2 * D_MODEL)[:, :QKV_WIDTH]
    db_o = slab[16:17]
    dsinks = slab[24:25, :N_Q_HEADS]
    db_sp = slab[32:33].reshape(SGU_GROUPS, SGU_CHUNK)
    width = D_MODEL // N_CHIPS
    dln_g = lax.dynamic_slice(slab[40:41], (0, j * width), (1, width))
    dln_b = lax.dynamic_slice(slab[41:42], (0, j * width), (1, width))
    dw_sp = slab[48:48 + SGU_CHUNK].reshape(SGU_GROUPS * SGU_CHUNK, SGU_CHUNK)
    return norms, db_qkv, db_o, dsinks, db_sp, dln_g, dln_b, dw_sp


class _GradReduce:
    def __init__(self, c_arr, jc_arr, dest_shapes):
        self.c_arr, self.jc_arr, self.dest_shapes = c_arr, jc_arr, dest_shapes
        self.grad, self.sibling, self.pair, self.chips, self.dest = {}, {}, {}, {}, {}

    def exchange(self, tags):
        return exchange_rider([self.grad[t] for t in tags])

    def exchanged(self, tags, res):
        for t, r in zip(tags, res):
            self.sibling[t] = r
            self.pair[t] = pair_add(self.grad[t], r, self.c_arr, name=f"pair_add_{t}")

    def scatter(self, tags):
        return scatter_rider([self.pair[t] for t in tags])

    def scattered(self, tags, res, where):
        for t, r in zip(tags, res):
            name, lead = where[t]
            self.dest[name] = final_add(self.grad[t], self.sibling[t], r, self.jc_arr, dest_shape=self.dest_shapes[name],
                                        lead=lead, prev=self.dest.get(name), name=f"final_add_{t}")

    def broadcast(self, items):
        names = []
        for n, _ in items:
            if n not in names:
                names.append(n)
        return names, broadcast_rider([self.dest[n] for n in names], [(names.index(n), lead) for n, lead in items])

    def broadcasted(self, names, res):
        for n, r in zip(names, res):
            self.dest[n] = r


def kernel(x, norm_mix_pre, norm_mix_post, norm_ffn_pre, norm_ffn_post, attn_w_qkv, attn_b_qkv, attn_sinks, attn_w_o, attn_b_o, sgu_w_in, sgu_ln_g, sgu_ln_b, sgu_w_spatial, sgu_b_spatial, sgu_w_out, ffn_w_gate_up, ffn_w_down, loss_target, m_norm_mix_pre, m_norm_mix_post, m_norm_ffn_pre, m_norm_ffn_post, m_attn_w_qkv, m_attn_b_qkv, m_attn_sinks, m_attn_w_o, m_attn_b_o, m_sgu_w_in, m_sgu_ln_g, m_sgu_ln_b, m_sgu_w_spatial, m_sgu_b_spatial, m_sgu_w_out, m_ffn_w_gate_up, m_ffn_w_down, v_norm_mix_pre, v_norm_mix_post, v_norm_ffn_pre, v_norm_ffn_post, v_attn_w_qkv, v_attn_b_qkv, v_attn_sinks, v_attn_w_o, v_attn_b_o, v_sgu_w_in, v_sgu_ln_g, v_sgu_ln_b, v_sgu_w_spatial, v_sgu_b_spatial, v_sgu_w_out, v_ffn_w_gate_up, v_ffn_w_down):
    s = x.shape[1]
    x0 = x.reshape(s, D_MODEL)
    target = loss_target.reshape(s, D_MODEL)
    mx, my, mc = lax.axis_index("x"), lax.axis_index("y"), lax.axis_index("c")
    chip = 2 * mx + my
    chip_arr = jnp.reshape(chip, (1,)).astype(I32)
    c_arr = jnp.reshape(mc, (1,)).astype(I32)
    jc_arr = jnp.stack([chip, mc]).astype(I32)
    zero_bias = jnp.zeros((1, D_MODEL), F32)

    def gain(p, i):
        return p[i:i + 1]

    big = [attn_w_qkv, attn_w_o, sgu_w_in, sgu_w_out, ffn_w_gate_up, ffn_w_gate_up, ffn_w_down, ffn_w_down]
    layers = [0, 0, 0, 0, 0, 1, 0, 1]
    tags = ["qkv", "wo", "win", "wout", "wgu0", "wgu1", "wd0", "wd1"]
    full = {t: place_shard(w, l, chip_arr, BF16, name=f"place_{t}") for w, l, t in zip(big, layers, tags) if t != "wgu1"}
    ln_pack = _pad_rows(jnp.concatenate([sgu_ln_g, sgu_ln_b], axis=0), 16)[None]
    full["ln"] = place_shard(ln_pack, 0, chip_arr, F32, name="place_ln")

    def ici(*names):
        return gather_ici_rider([full[n] for n in names])

    def d2d(*names):
        return gather_d2d_rider([full[n] for n in names])

    def landed(names, res):
        for n, r in zip(names, res):
            full[n] = r

    cos, sin = _rope_tables(s)
    sink_rows = jnp.broadcast_to(
        jnp.repeat(attn_sinks.reshape(N_KV_HEADS, GQA_GROUP), WINDOW, axis=1)[:, None, :], (N_KV_HEADS, 8, ROWS))
    w_sp = sgu_w_spatial.reshape(SGU_GROUPS, SGU_CHUNK, SGU_CHUNK)
    b_sp = jnp.broadcast_to(sgu_b_spatial.reshape(SGU_GROUPS, SGU_CHUNK)[:, :, None], (SGU_GROUPS, SGU_CHUNK, LANES))

    h0, (res,) = prenorm(x0, gain(norm_mix_pre, 0), name="prenorm_0", riders=[ici("qkv", "ln")])
    landed(("qkv", "ln"), res)
    full["wgu1"], (res,) = place_shard(ffn_w_gate_up, 1, chip_arr, BF16, name="place_wgu1", riders=[d2d("qkv", "ln")])
    landed(("qkv", "ln"), res)
    ln_g = full["ln"][:, 0, :].reshape(1, D_MODEL)
    ln_b = full["ln"][:, 1, :].reshape(1, D_MODEL)

    qkv, (res,) = qkv_proj(h0, full["qkv"], attn_b_qkv, cos, sin, name="qkv_proj", riders=[ici("wo", "wd0")])
    landed(("wo", "wd0"), res)
    o, (res_a, res_b) = attn_fwd(qkv, sink_rows, name="attn_fwd", riders=[d2d("wo", "wd0"), ici("wgu0")])
    landed(("wo", "wd0"), res_a)
    landed(("wgu0",), res_b)
    w_o = full["wo"].reshape(Q_WIDTH, D_MODEL)
    (x1, h1, m0), (res_a, res_b) = proj_residual_norm(
        o, w_o, x0, attn_b_o, gain(norm_mix_post, 0), gain(norm_ffn_pre, 0), name="attn_out_norm",
        riders=[d2d("wgu0"), ici("win", "wout")])
    landed(("wgu0",), res_a)
    landed(("win", "wout"), res_b)
    (gu0, a0), (res_a, res_b) = ffn_up(h1, full["wgu0"], name="ffn_up_0", riders=[d2d("win", "wout"), ici("wgu1")])
    landed(("win", "wout"), res_a)
    landed(("wgu1",), res_b)
    w_d0 = full["wd0"].reshape(D_FF, D_MODEL)
    (x2, h2, f0), (res_a, res_b) = proj_residual_norm(
        a0, w_d0, x1, zero_bias, gain(norm_ffn_post, 0), gain(norm_mix_pre, 1), name="ffn_down_norm_0",
        riders=[d2d("wgu1"), ici("wd1")])
    landed(("wgu1",), res_a)
    landed(("wd1",), res_b)
    z, (res,) = mm_nn(h2, full["win"], out_dtype=F32, name="sgu_in_proj", riders=[d2d("wd1")])
    landed(("wd1",), res)
    w_qkv, w_in, w_gu0, w_gu1 = full["qkv"], full["win"], full["wgu0"], full["wgu1"]
    w_out = full["wout"].reshape(D_MODEL, D_MODEL)
    w_d1 = full["wd1"].reshape(D_FF, D_MODEL)
    y = sgu_fwd(z, ln_g, ln_b, w_sp, b_sp, name="sgu_fwd")
    x3, h3, m1 = proj_residual_norm(y, w_out, x2, zero_bias, gain(norm_mix_post, 1), gain(norm_ffn_pre, 1), name="sgu_out_norm")
    gu1, a1 = ffn_up(h3, w_gu1, name="ffn_up_1")
    dx4, df1, dg_fpost1, loss_part = proj_loss_head(a1, w_d1, x3, gain(norm_ffn_post, 1), target, name="ffn_down_loss")
    loss = lax.psum(loss_part[0, 0], ("x", "y", "c"))

    red = _GradReduce(c_arr, jc_arr, {
        "qkv": attn_w_qkv.shape[1:], "wo": attn_w_o.shape[1:], "win": sgu_w_in.shape[1:], "wout": sgu_w_out.shape[1:],
        "wgu": ffn_w_gate_up.shape, "wd": ffn_w_down.shape, "slab": (N_CHIPS, SLAB_ROWS // N_CHIPS, D_MODEL)})
    where = {"qkv": ("qkv", None), "wo": ("wo", None), "win": ("win", None), "wout": ("wout", None), "wgu0": ("wgu", 0),
             "wgu1": ("wgu", 1), "wd0": ("wd", 0), "wd1": ("wd", 1), "small": ("slab", "chip")}

    dgu1 = ffn_dact(df1, w_d1, gu1, name="ffn_dact_1")
    red.grad["wd1"] = mm_tn(a1, df1, shard_major=False, tm=256, tn=D_MODEL, name="dw_down_1").reshape(
        N_CHIPS, D_FF // N_CHIPS, D_MODEL)
    red.grad["wgu1"], (res,) = mm_tn(h3, dgu1, shard_major=True, tm=512, tn=FF_HALF, name="dw_gate_up_1",
                                     riders=[red.exchange(["wd1"])])
    red.exchanged(["wd1"], res)
    (dx3, dm1, dg_fpre1, dg_mpost1, _), (res_a, res_b) = dh_norm_bwd_pair(
        dgu1, w_gu1, dx4, x3, gain(norm_ffn_pre, 1), m1, gain(norm_mix_post, 1), name="dh_ffn_norm_1",
        riders=[red.exchange(["wgu1"]), red.scatter(["wd1"])])
    red.exchanged(["wgu1"], res_a)
    red.scattered(["wd1"], res_b, where)
    names, rider = red.broadcast([("wd", 1)])
    dy, (res,) = mm_nt(dm1, w_out, out_dtype=F32, name="dy_sgu", riders=[rider])
    red.broadcasted(names, res)
    red.grad["wout"] = mm_tn(y, dm1, shard_major=False, tm=512, tn=D_MODEL, name="dw_sgu_out").reshape(
        N_CHIPS, D_MODEL // N_CHIPS, D_MODEL)
    (dz, dw_sp, db_sp, dln_g, dln_b), (res_a, res_b) = sgu_bwd(
        z, dy, ln_g, ln_b, w_sp, b_sp, name="sgu_bwd", riders=[red.scatter(["wgu1"]), red.exchange(["wout"])])
    red.scattered(["wgu1"], res_a, where)
    red.exchanged(["wout"], res_b)
    names, rider = red.broadcast([("wgu", 1)])
    red.grad["win"], (res_a, res_b) = mm_tn(h2, dz, shard_major=True, tm=D_MODEL, tn=2 * D_MODEL // N_CHIPS, name="dw_sgu_in",
                                            riders=[rider, red.scatter(["wout"])])
    red.broadcasted(names, res_a)
    red.scattered(["wout"], res_b, where)
    names, rider = red.broadcast([("wout", None)])
    (dx2, df0, dg_mpre1, dg_fpost0, _), (res_a, res_b) = dh_norm_bwd_pair(
        dz, w_in, dx3, x2, gain(norm_mix_pre, 1), f0, gain(norm_ffn_post, 0), name="dh_sgu_norm",
        riders=[red.exchange(["win"]), rider])
    red.exchanged(["win"], res_a)
    red.broadcasted(names, res_b)
    dgu0, (res,) = ffn_dact(df0, w_d0, gu0, name="ffn_dact_0", riders=[red.scatter(["win"])])
    red.scattered(["win"], res, where)
    names, rider = red.broadcast([("win", None)])
    dw_d0, (res,) = mm_tn(a0, df0, shard_major=False, tm=256, tn=D_MODEL, name="dw_down_0", riders=[rider])
    red.broadcasted(names, res)
    red.grad["wd0"] = dw_d0.reshape(N_CHIPS, D_FF // N_CHIPS, D_MODEL)
    red.grad["wgu0"], (res,) = mm_tn(h1, dgu0, shard_major=True, tm=512, tn=FF_HALF, name="dw_gate_up_0",
                                     riders=[red.exchange(["wd0"])])
    red.exchanged(["wd0"], res)
    (dx1, dm0, dg_fpre0, dg_mpost0, db_o), (res_a, res_b) = dh_norm_bwd_pair(
        dgu0, w_gu0, dx2, x1, gain(norm_ffn_pre, 0), m0, gain(norm_mix_post, 0), name="dh_ffn_norm_0",
        riders=[red.exchange(["wgu0"]), red.scatter(["wd0"])])
    red.exchanged(["wgu0"], res_a)
    red.scattered(["wd0"], res_b, where)
    names, rider = red.broadcast([("wd", 0)])
    do, (res,) = mm_nt(dm0, w_o, out_dtype=BF16, name="do_attn", riders=[rider])
    red.broadcasted(names, res)
    red.grad["wo"] = mm_tn(o, dm0, shard_major=False, tm=512, tn=D_MODEL, name="dw_attn_out").reshape(
        N_CHIPS, Q_WIDTH // N_CHIPS, D_MODEL)
    (dq, dkc, dkp, dvc, dvp, dsink), (res_a, res_b) = attn_bwd(
        qkv, sink_rows, do, name="attn_bwd", riders=[red.scatter(["wgu0"]), red.exchange(["wo"])])
    red.scattered(["wgu0"], res_a, where)
    red.exchanged(["wo"], res_b)
    names, rider = red.broadcast([("wgu", 0)])
    (dqkv, db_qkv), (res_a, res_b) = rope_bwd(dq, dkc, dkp, dvc, dvp, cos, sin, name="rope_bwd",
                                              riders=[rider, red.scatter(["wo"])])
    red.broadcasted(names, res_a)
    red.scattered(["wo"], res_b, where)
    names, rider = red.broadcast([("wo", None)])
    red.grad["qkv"], (res,) = mm_tn(h0, dqkv, shard_major=True, tm=D_MODEL, tn=QKV_WIDTH // N_CHIPS, name="dw_qkv",
                                    riders=[rider])
    red.broadcasted(names, res)
    dh0, (res,) = mm_nt(dqkv, w_qkv, out_dtype=F32, tm=1024, name="dh_attn", riders=[red.exchange(["qkv"])])
    red.exchanged(["qkv"], res)
    grad_x, dg_mpre0 = norm_bwd_last(dx1, dh0, x0, gain(norm_mix_pre, 0), name="norm_bwd_in")

    norm_grads = [jnp.concatenate(p, axis=0) for p in
                  ((dg_mpre0, dg_mpre1), (dg_mpost0, dg_mpost1), (dg_fpre0, dg_fpre1), (dg_fpost0, dg_fpost1))]
    red.grad["small"] = _pack_small(norm_grads, db_qkv, db_o, dsink[:, :, 0, 0], db_sp[:, :, 0], dln_g, dln_b, dw_sp)
    res_a, res_b = comm_call([red.scatter(["qkv"]), red.exchange(["small"])], name="tail_1")
    red.scattered(["qkv"], res_a, where)
    red.exchanged(["small"], res_b)
    names, rider = red.broadcast([("qkv", None)])
    res_a, res_b = comm_call([red.scatter(["small"]), rider], name="tail_2")
    red.scattered(["small"], res_a, where)
    red.broadcasted(names, res_b)
    ((slab_full,),) = comm_call([allcast_rider(red.dest["slab"])], name="tail_3")
    g_qkv, g_wo, g_win, g_wout, g_wgu, g_wd = (red.dest[n] for n in ("qkv", "wo", "win", "wout", "wgu", "wd"))
    g_norms, g_bqkv, g_bo, g_sinks, g_bsp, g_lng, g_lnb, g_wsp = _unpack_small(slab_full, chip)

    def big_update(w, g, m, v, tag):
        return adamw(w, g.reshape(w.shape), m, v, name=f"adamw_{tag}")

    upd = {
        "attn_w_qkv": big_update(attn_w_qkv, g_qkv, m_attn_w_qkv, v_attn_w_qkv, "qkv"),
        "attn_w_o": big_update(attn_w_o, g_wo, m_attn_w_o, v_attn_w_o, "wo"),
        "sgu_w_in": big_update(sgu_w_in, g_win, m_sgu_w_in, v_sgu_w_in, "win"),
        "sgu_w_out": big_update(sgu_w_out, g_wout, m_sgu_w_out, v_sgu_w_out, "wout"),
        "ffn_w_gate_up": big_update(ffn_w_gate_up, g_wgu, m_ffn_w_gate_up, v_ffn_w_gate_up, "wgu"),
        "ffn_w_down": big_update(ffn_w_down, g_wd, m_ffn_w_down, v_ffn_w_down, "wd"),
    }
    small_names = ["norm_mix_pre", "norm_mix_post", "norm_ffn_pre", "norm_ffn_post", "attn_b_qkv", "attn_sinks", "attn_b_o",
                   "sgu_ln_g", "sgu_ln_b", "sgu_w_spatial", "sgu_b_spatial"]
    small_w = [norm_mix_pre, norm_mix_post, norm_ffn_pre, norm_ffn_post, attn_b_qkv, attn_sinks, attn_b_o, sgu_ln_g, sgu_ln_b,
               sgu_w_spatial, sgu_b_spatial]
    small_m = [m_norm_mix_pre, m_norm_mix_post, m_norm_ffn_pre, m_norm_ffn_post, m_attn_b_qkv, m_attn_sinks, m_attn_b_o,
               m_sgu_ln_g, m_sgu_ln_b, m_sgu_w_spatial, m_sgu_b_spatial]
    small_v = [v_norm_mix_pre, v_norm_mix_post, v_norm_ffn_pre, v_norm_ffn_post, v_attn_b_qkv, v_attn_sinks, v_attn_b_o,
               v_sgu_ln_g, v_sgu_ln_b, v_sgu_w_spatial, v_sgu_b_spatial]
    small_g = g_norms + [g_bqkv, g_sinks, g_bo, g_lng, g_lnb, g_wsp, g_bsp]

    def flat2(a):
        return a.reshape(-1, a.shape[-1])

    res = adamw_small([flat2(a) for a in small_w], [flat2(a) for a in small_g], [flat2(a) for a in small_m],
                      [flat2(a) for a in small_v], name="adamw_small")
    for i, nm in enumerate(small_names):
        upd[nm] = tuple(r[i].reshape(small_w[i].shape) for r in res)

    order = ["norm_mix_pre", "norm_mix_post", "norm_ffn_pre", "norm_ffn_post", "attn_w_qkv", "attn_b_qkv", "attn_sinks",
             "attn_w_o", "attn_b_o", "sgu_w_in", "sgu_ln_g", "sgu_ln_b", "sgu_w_spatial", "sgu_b_spatial", "sgu_w_out",
             "ffn_w_gate_up", "ffn_w_down"]
    outs = [loss, grad_x.reshape(1, s, D_MODEL)]
    for part in range(4):
        outs += [upd[nm][part] for nm in order]
    return tuple(outs)
```

```python
import types

import jax
import jax.numpy as jnp
from jax import lax
from jax.experimental import pallas as pl
from jax.experimental.pallas import tpu as pltpu

F32 = jnp.float32
BF16 = jnp.bfloat16
I32 = jnp.int32

D_MODEL = 1024
HEAD_DIM = 64
N_Q_HEADS = 16
N_KV_HEADS = 4
GQA_GROUP = 4
WINDOW = 128
Q_WIDTH = 1024
KV_WIDTH = 256
QKV_WIDTH = 1536
ROPE_THETA = 10000.0
SGU_GROUPS = 8
SGU_CHUNK = 128
D_FF = 2816
FF_HALF = D_FF // 2
EPS = 1e-6
N_CHIPS = 4
LANES = 128

ADAM_LR = 0.001
ADAM_B1 = 0.9
ADAM_B2 = 0.999
ADAM_EPS = 1e-08
ADAM_WD = 0.01
ADAM_STEP = 10

VMEM_LIMIT = 52 * 1024 * 1024
MESH = pl.DeviceIdType.MESH
NEG = -1e30
NT_DIMS = (((1,), (1,)), ((), ()))
TN_DIMS = (((0,), (0,)), ((), ()))
NN_DIMS = (((1,), (0,)), ((), ()))
ANY = pl.BlockSpec(memory_space=pl.ANY)


def _row_tile(s, want):
    return want if s % want == 0 else s


def _call(body, *, name, grid=(), in_specs=(), out_specs=(), out_shape=(), scratch_shapes=(), operands=(), prefetch=(),
          aliases=None, riders=(), sem=None):
    n_pre, n_in, n_out, n_scr = len(prefetch), len(operands), len(out_shape), len(scratch_shapes)
    in_specs, out_specs, out_shape = list(in_specs), list(out_specs), list(out_shape)
    operands, scratch_shapes = list(operands), list(scratch_shapes)
    io_alias = {n_pre + i: o for i, o in (aliases or {}).items()}
    for r in riders:
        base_in, base_out = n_pre + len(operands), len(out_shape)
        operands += list(r.inputs)
        in_specs += [ANY] * len(r.inputs)
        for pos, i in enumerate(r.aliased):
            io_alias[base_in + i] = base_out + pos
            out_shape.append(jax.ShapeDtypeStruct(r.inputs[i].shape, r.inputs[i].dtype))
        out_shape += list(r.fresh)
        out_specs += [ANY] * (len(r.aliased) + len(r.fresh))
        scratch_shapes += [pltpu.SemaphoreType.DMA((r.nsem,)), pltpu.SemaphoreType.DMA((r.nsem,))]

    def wrapped(*refs):
        pre, p = refs[:n_pre], n_pre
        core_in, p = refs[p:p + n_in], p + n_in
        r_in = []
        for r in riders:
            r_in.append(refs[p:p + len(r.inputs)])
            p += len(r.inputs)
        core_out, p = refs[p:p + n_out], p + n_out
        r_out = []
        for r in riders:
            k = len(r.aliased) + len(r.fresh)
            r_out.append(refs[p:p + k])
            p += k
        core_scr, p = refs[p:p + n_scr], p + n_scr
        r_sem = [refs[p + 2 * i:p + 2 * i + 2] for i in range(len(riders))]

        def edge(at_last, fns):
            def run():
                for i, r in enumerate(riders):
                    getattr(r, fns)(r_in[i], r_out[i], r_sem[i][0], r_sem[i][1])
            if not riders:
                return
            if not grid:
                run()
                return
            cond = None
            for d, n in enumerate(grid):
                c = pl.program_id(d) == (n - 1 if at_last else 0)
                cond = c if cond is None else jnp.logical_and(cond, c)
            pl.when(cond)(run)

        edge(False, "start")
        if body is not None:
            body(*pre, *core_in, *core_out, *core_scr)
        edge(True, "finish")

    if sem is None or riders:
        sem = ("arbitrary",) * len(grid)
    kwargs = dict(out_shape=out_shape, input_output_aliases=io_alias, name=name)
    if grid:
        kwargs["compiler_params"] = pltpu.CompilerParams(dimension_semantics=sem, vmem_limit_bytes=VMEM_LIMIT)
    if n_pre:
        kwargs["grid_spec"] = pltpu.PrefetchScalarGridSpec(
            num_scalar_prefetch=n_pre, grid=grid, in_specs=in_specs, out_specs=out_specs, scratch_shapes=scratch_shapes)
    else:
        kwargs.update(grid=grid, in_specs=in_specs, out_specs=out_specs, scratch_shapes=scratch_shapes)
    res = pl.pallas_call(wrapped, **kwargs)(*prefetch, *operands)
    core, rest, rider_res = list(res[:n_out]), list(res[n_out:]), []
    for r in riders:
        k = len(r.aliased) + len(r.fresh)
        rider_res.append(rest[:k])
        rest = rest[k:]
    return core, rider_res


def _mm_call(*, grid, in_specs, out_spec, out_shape, dims, nk, kaxis, acc_shape, name, operands, riders=()):
    out_dtype = out_shape.dtype

    def body(a_ref, b_ref, o_ref, *scratch):
        p = lax.dot_general(a_ref[...].astype(BF16), b_ref[...].astype(BF16), dims, preferred_element_type=F32)
        if nk == 1:
            o_ref[...] = p.astype(out_dtype)
        else:
            acc = scratch[0]
            kk = pl.program_id(kaxis)

            @pl.when(kk == 0)
            def _():
                acc[...] = p

            @pl.when(kk > 0)
            def _():
                acc[...] += p

            @pl.when(kk == nk - 1)
            def _():
                o_ref[...] = acc[...].astype(out_dtype)

    sem = ["parallel"] * len(grid)
    if nk > 1:
        sem[kaxis] = "arbitrary"
    (out,), rider_res = _call(
        body, grid=grid, in_specs=in_specs, out_specs=[out_spec], out_shape=[out_shape],
        scratch_shapes=[pltpu.VMEM(acc_shape, F32)] if nk > 1 else [], operands=operands, name=name, riders=riders,
        sem=tuple(sem))
    return (out, rider_res) if riders else out


def mm_nn(a, w, *, out_dtype, name, tm=512, tn=512, riders=()):
    m, k = a.shape
    tm = _row_tile(m, tm)
    if w.ndim == 3:
        ns = w.shape[2]
        grid = (N_CHIPS, m // tm)
        w_spec = pl.BlockSpec((None, k, ns), lambda j, i: (j, 0, 0))
        o_spec = pl.BlockSpec((tm, ns), lambda j, i: (i, j))
        n = N_CHIPS * ns
    else:
        n = w.shape[1]
        grid = (n // tn, m // tm)
        w_spec = pl.BlockSpec((k, tn), lambda j, i: (0, j))
        o_spec = pl.BlockSpec((tm, tn), lambda j, i: (i, j))
    return _mm_call(grid=grid, in_specs=[pl.BlockSpec((tm, k), lambda j, i: (i, 0)), w_spec], out_spec=o_spec,
                    out_shape=jax.ShapeDtypeStruct((m, n), out_dtype), dims=NN_DIMS, nk=1, kaxis=0, acc_shape=None,
                    name=name, operands=(a, w), riders=riders)


def mm_nt(a, w, *, out_dtype, name, tm=512, tn=512, riders=()):
    if w.ndim == 2:
        m, n = a.shape
        kout = w.shape[0]
        tm = _row_tile(m, tm)
        return _mm_call(grid=(kout // tn, m // tm),
                        in_specs=[pl.BlockSpec((tm, n), lambda j, i: (i, 0)), pl.BlockSpec((tn, n), lambda j, i: (j, 0))],
                        out_spec=pl.BlockSpec((tm, tn), lambda j, i: (i, j)),
                        out_shape=jax.ShapeDtypeStruct((m, kout), out_dtype), dims=NT_DIMS, nk=1, kaxis=0,
                        acc_shape=None, name=name, operands=(a, w), riders=riders)
    _, kout, ns = w.shape
    planes = a.ndim == 3
    m = a.shape[1] if planes else a.shape[0]
    tm = _row_tile(m, tm)
    a_spec = pl.BlockSpec((2, tm, 2 * ns), lambda i: (0, i, 0)) if planes else pl.BlockSpec((tm, N_CHIPS * ns), lambda i: (i, 0))

    def body(a_ref, w0, w1, w2, w3, o_ref):
        acc = None
        for j, w_ref in enumerate((w0, w1, w2, w3)):
            if planes:
                a_j = a_ref[j // 2, :, (j % 2) * ns:(j % 2 + 1) * ns]
            else:
                a_j = a_ref[:, j * ns:(j + 1) * ns]
            p = lax.dot_general(a_j, w_ref[...], NT_DIMS, preferred_element_type=F32)
            acc = p if acc is None else acc + p
        o_ref[...] = acc.astype(out_dtype)

    def shard(j):
        return pl.BlockSpec((None, kout, ns), lambda i: (j, 0, 0))

    (out,), rider_res = _call(
        body, grid=(m // tm,), in_specs=[a_spec] + [shard(j) for j in range(N_CHIPS)],
        out_specs=[pl.BlockSpec((tm, kout), lambda i: (i, 0))], out_shape=[jax.ShapeDtypeStruct((m, kout), out_dtype)],
        operands=(a, w, w, w, w), sem=("parallel",), name=name, riders=riders)
    return (out, rider_res) if riders else out


def mm_tn(a, b, *, shard_major, name, tm, tn, tk=None, out_dtype=BF16, riders=()):
    s, m = a.shape
    tk = s if tk is None else _row_tile(s, tk)
    if b.ndim == 3:
        n = 2 * b.shape[2]
        b_spec = pl.BlockSpec((None, tk, tn), lambda j, i, kk: (j // 2, kk, j % 2))
    else:
        n = b.shape[1]
        b_spec = pl.BlockSpec((tk, tn), lambda j, i, kk: (kk, j))
    if shard_major:
        assert tn == n // N_CHIPS
        o_spec = pl.BlockSpec((None, tm, tn), lambda j, i, kk: (j, i, 0))
        o_shape = jax.ShapeDtypeStruct((N_CHIPS, m, tn), out_dtype)
    else:
        o_spec = pl.BlockSpec((tm, tn), lambda j, i, kk: (i, j))
        o_shape = jax.ShapeDtypeStruct((m, n), out_dtype)
    return _mm_call(grid=(n // tn, m // tm, s // tk),
                    in_specs=[pl.BlockSpec((tk, tm), lambda j, i, kk: (kk, i)), b_spec], out_spec=o_spec,
                    out_shape=o_shape, dims=TN_DIMS, nk=s // tk, kaxis=2, acc_shape=(tm, tn), name=name, operands=(a, b),
                    riders=riders)


def _rstd(x):
    return lax.rsqrt(jnp.mean(x * x, axis=-1, keepdims=True) + EPS)


def _rms_bwd(dy, x, g):
    r = _rstd(x)
    xhat = x * r
    gy = dy * g
    dx = r * (gy - xhat * jnp.mean(gy * xhat, axis=-1, keepdims=True))
    return dx, jnp.sum(dy * xhat, axis=0, keepdims=True)


def _accum(ref, val, first):
    @pl.when(first)
    def _():
        ref[...] = val

    @pl.when(jnp.logical_not(first))
    def _():
        ref[...] += val


def _row_spec(tm, width):
    return pl.BlockSpec((tm, width), lambda i: (i, 0))


def _vec_spec(width):
    return pl.BlockSpec((1, width), lambda i: (0, 0))


def _ret(core, rider_res, riders):
    core = core[0] if len(core) == 1 else core
    return (core, rider_res) if riders else core


def prenorm(x, g, *, name, tm=256, riders=()):
    s = x.shape[0]
    tm = _row_tile(s, tm)

    def body(x_ref, g_ref, h_ref):
        xv = x_ref[...]
        h_ref[...] = (xv * _rstd(xv) * g_ref[...]).astype(BF16)

    core, rr = _call(
        body, grid=(s // tm,), in_specs=[_row_spec(tm, D_MODEL), _vec_spec(D_MODEL)], out_specs=[_row_spec(tm, D_MODEL)],
        out_shape=[jax.ShapeDtypeStruct((s, D_MODEL), BF16)], operands=(x, g), sem=("parallel",), name=name, riders=riders)
    return _ret(core, rr, riders)


def proj_residual_norm(a, w, x, bias, g_post, g_next, *, name, tm=256, riders=()):
    s, k = a.shape
    tm = _row_tile(s, tm)

    def body(a_ref, w_ref, x_ref, b_ref, gp_ref, gn_ref, xo_ref, h_ref, m_ref):
        mv = jnp.dot(a_ref[...], w_ref[...], preferred_element_type=F32) + b_ref[...]
        m_ref[...] = mv.astype(BF16)
        xn = x_ref[...] + mv * _rstd(mv) * gp_ref[...]
        xo_ref[...] = xn
        h_ref[...] = (xn * _rstd(xn) * gn_ref[...]).astype(BF16)

    row, vec = _row_spec(tm, D_MODEL), _vec_spec(D_MODEL)
    core, rr = _call(
        body, grid=(s // tm,),
        in_specs=[_row_spec(tm, k), pl.BlockSpec((k, D_MODEL), lambda i: (0, 0)), row, vec, vec, vec], out_specs=[row, row, row],
        out_shape=[jax.ShapeDtypeStruct((s, D_MODEL), F32), jax.ShapeDtypeStruct((s, D_MODEL), BF16),
                   jax.ShapeDtypeStruct((s, D_MODEL), BF16)],
        operands=(a, w, x, bias, g_post, g_next), sem=("parallel",), name=name, riders=riders)
    return _ret(core, rr, riders)


def proj_loss_head(a, w, x, g_post, target, *, name, tm=256, riders=()):
    s, k = a.shape
    tm = _row_tile(s, tm)

    def body(a_ref, w_ref, x_ref, g_ref, t_ref, dx_ref, df_ref, dg_ref, loss_ref):
        first = pl.program_id(0) == 0
        fv = jnp.dot(a_ref[...], w_ref[...], preferred_element_type=F32)
        g = g_ref[...]
        err = x_ref[...] + fv * _rstd(fv) * g - t_ref[...]
        dx = err * (1.0 / D_MODEL)
        dx_ref[...] = dx
        df, dg = _rms_bwd(dx, fv, g)
        df_ref[...] = df.astype(BF16)
        _accum(dg_ref, dg, first)
        part = jnp.sum(jnp.sum(err * err, axis=-1, keepdims=True), axis=0, keepdims=True) * (0.5 / D_MODEL)
        _accum(loss_ref, jnp.broadcast_to(part, (8, LANES)), first)

    row, vec = _row_spec(tm, D_MODEL), _vec_spec(D_MODEL)
    core, rr = _call(
        body, grid=(s // tm,), in_specs=[_row_spec(tm, k), pl.BlockSpec((k, D_MODEL), lambda i: (0, 0)), row, vec, row],
        out_specs=[row, row, vec, pl.BlockSpec((8, LANES), lambda i: (0, 0))],
        out_shape=[jax.ShapeDtypeStruct((s, D_MODEL), F32), jax.ShapeDtypeStruct((s, D_MODEL), BF16),
                   jax.ShapeDtypeStruct((1, D_MODEL), F32), jax.ShapeDtypeStruct((8, LANES), F32)],
        operands=(a, w, x, g_post, target), name=name, riders=riders)
    return _ret(core, rr, riders)


def dh_norm_bwd_pair(a, w, dres, x, g_pre, m, g_post, *, name, tm=256, riders=()):
    _, kout, ns = w.shape
    planes = a.ndim == 3
    s = x.shape[0]
    tm = _row_tile(s, tm)
    a_spec = pl.BlockSpec((2, tm, 2 * ns), lambda i: (0, i, 0)) if planes else pl.BlockSpec((tm, N_CHIPS * ns), lambda i: (i, 0))

    def body(a_ref, w0, w1, w2, w3, dres_ref, x_ref, gpre_ref, m_ref, gpost_ref, dx_ref, dm_ref, dgpre_ref, dgpost_ref, db_ref):
        first = pl.program_id(0) == 0
        dh = None
        for j, w_ref in enumerate((w0, w1, w2, w3)):
            a_j = a_ref[j // 2, :, (j % 2) * ns:(j % 2 + 1) * ns] if planes else a_ref[:, j * ns:(j + 1) * ns]
            p = lax.dot_general(a_j, w_ref[...], NT_DIMS, preferred_element_type=F32)
            dh = p if dh is None else dh + p
        d1, dgpre = _rms_bwd(dh, x_ref[...], gpre_ref[...])
        dx = dres_ref[...] + d1
        dx_ref[...] = dx
        dm, dgpost = _rms_bwd(dx, m_ref[...].astype(F32), gpost_ref[...])
        dm_ref[...] = dm.astype(BF16)
        _accum(dgpre_ref, dgpre, first)
        _accum(dgpost_ref, dgpost, first)
        _accum(db_ref, jnp.sum(dm, axis=0, keepdims=True), first)

    def shard(j):
        return pl.BlockSpec((None, kout, ns), lambda i: (j, 0, 0))

    row, vec = _row_spec(tm, D_MODEL), _vec_spec(D_MODEL)
    vshape = jax.ShapeDtypeStruct((1, D_MODEL), F32)
    core, rr = _call(
        body, grid=(s // tm,), in_specs=[a_spec] + [shard(j) for j in range(N_CHIPS)] + [row, row, vec, row, vec],
        out_specs=[row, row, vec, vec, vec],
        out_shape=[jax.ShapeDtypeStruct((s, D_MODEL), F32), jax.ShapeDtypeStruct((s, D_MODEL), BF16), vshape, vshape, vshape],
        operands=(a, w, w, w, w, dres, x, g_pre, m, g_post), name=name, riders=riders)
    return _ret(core, rr, riders)


def norm_bwd_last(dres, dh, x, g_pre, *, name, tm=256, riders=()):
    s = x.shape[0]
    tm = _row_tile(s, tm)

    def body(dres_ref, dh_ref, x_ref, g_ref, dx_ref, dg_ref):
        d1, dg = _rms_bwd(dh_ref[...], x_ref[...], g_ref[...])
        dx_ref[...] = dres_ref[...] + d1
        _accum(dg_ref, dg, pl.program_id(0) == 0)

    row, vec = _row_spec(tm, D_MODEL), _vec_spec(D_MODEL)
    core, rr = _call(
        body, grid=(s // tm,), in_specs=[row, row, row, vec], out_specs=[row, vec],
        out_shape=[jax.ShapeDtypeStruct((s, D_MODEL), F32), jax.ShapeDtypeStruct((1, D_MODEL), F32)],
        operands=(dres, dh, x, g_pre), name=name, riders=riders)
    return _ret(core, rr, riders)


def _rope_tables(s):
    half = HEAD_DIM // 2
    inv_freq = ROPE_THETA ** (-(jnp.arange(half, dtype=F32) * 2.0) / HEAD_DIM)
    ang = jnp.arange(s, dtype=I32).astype(F32)[:, None] * inv_freq[None, :]
    cos, sin = jnp.cos(ang), jnp.sin(ang)
    return jnp.tile(cos, (1, 4)), jnp.concatenate([-sin, sin, -sin, sin], axis=1)


def _swap_halves(x):
    lane = lax.broadcasted_iota(I32, x.shape, 1)
    return jnp.where((lane & (HEAD_DIM - 1)) < HEAD_DIM // 2, pltpu.roll(x, LANES - 32, 1), pltpu.roll(x, 32, 1))


N_ROPE_BLOCKS = (Q_WIDTH + KV_WIDTH) // LANES


def qkv_proj(h, w, bias, cos, sin, *, name, tm=512, riders=()):
    s, k = h.shape
    ns = w.shape[2]
    tm = _row_tile(s, tm)

    def body(h_ref, w_ref, b_ref, c_ref, s_ref, o_ref):
        j = pl.program_id(0)
        p = jnp.dot(h_ref[...], w_ref[...], preferred_element_type=F32) + b_ref[...]
        cosv, sinv = c_ref[...], s_ref[...]
        for blk in range(ns // LANES):
            xb = p[:, blk * LANES:(blk + 1) * LANES]
            roped = xb * cosv + _swap_halves(xb) * sinv
            is_qk = j * (ns // LANES) + blk < N_ROPE_BLOCKS
            o_ref[:, blk * LANES:(blk + 1) * LANES] = jnp.where(is_qk, roped, xb).astype(BF16)

    core, rr = _call(
        body, grid=(N_CHIPS, s // tm),
        in_specs=[pl.BlockSpec((tm, k), lambda j, i: (i, 0)), pl.BlockSpec((None, k, ns), lambda j, i: (j, 0, 0)),
                  pl.BlockSpec((1, ns), lambda j, i: (0, j)), pl.BlockSpec((tm, LANES), lambda j, i: (i, 0)),
                  pl.BlockSpec((tm, LANES), lambda j, i: (i, 0))],
        out_specs=[pl.BlockSpec((tm, ns), lambda j, i: (i, j))], out_shape=[jax.ShapeDtypeStruct((s, N_CHIPS * ns), BF16)],
        operands=(h, w, bias, cos, sin), sem=("parallel", "parallel"), name=name, riders=riders)
    return _ret(core, rr, riders)


def rope_bwd(dq, dkc, dkp, dvc, dvp, cos, sin, *, name, riders=()):
    s = dq.shape[0]
    tm = WINDOW
    nb = s // tm

    def body(dq_ref, dkc_ref, dkp_ref, dvc_ref, dvp_ref, c_ref, s_ref, o_ref, db_ref):
        i = pl.program_id(0)
        has_next = (i < nb - 1).astype(F32)
        cosv, sinv = c_ref[...], s_ref[...]
        for blk in range(QKV_WIDTH // LANES):
            if blk < Q_WIDTH // LANES:
                g = dq_ref[:, blk * LANES:(blk + 1) * LANES].astype(F32)
            else:
                own, nxt = (dkc_ref, dkp_ref) if blk < N_ROPE_BLOCKS else (dvc_ref, dvp_ref)
                cols = slice((blk % 2) * LANES, (blk % 2 + 1) * LANES)
                g = own[:, cols].astype(F32) + has_next * nxt[:, cols].astype(F32)
            if blk < N_ROPE_BLOCKS:
                g = g * cosv + _swap_halves(g * sinv)
            o_ref[:, blk * LANES:(blk + 1) * LANES] = g.astype(BF16)
            part = jnp.sum(g, axis=0, keepdims=True)

            @pl.when(i == 0)
            def _():
                db_ref[:, blk * LANES:(blk + 1) * LANES] = part

            @pl.when(i > 0)
            def _():
                db_ref[:, blk * LANES:(blk + 1) * LANES] += part

    own_spec = _row_spec(tm, KV_WIDTH)
    next_spec = pl.BlockSpec((tm, KV_WIDTH), lambda i: (jnp.minimum(i + 1, nb - 1), 0))
    core, rr = _call(
        body, grid=(nb,),
        in_specs=[_row_spec(tm, Q_WIDTH), own_spec, next_spec, own_spec, next_spec, _row_spec(tm, LANES), _row_spec(tm, LANES)],
        out_specs=[_row_spec(tm, QKV_WIDTH), _vec_spec(QKV_WIDTH)],
        out_shape=[jax.ShapeDtypeStruct((s, QKV_WIDTH), BF16), jax.ShapeDtypeStruct((1, QKV_WIDTH), F32)],
        operands=(dq, dkc, dkp, dvc, dvp, cos, sin), name=name, riders=riders)
    return _ret(core, rr, riders)


ROWS = GQA_GROUP * WINDOW


def _prev_slots():
    kpos = lax.broadcasted_iota(I32, (WINDOW, ROWS), 0)
    qpos = lax.broadcasted_iota(I32, (WINDOW, ROWS), 1) & (WINDOW - 1)
    return kpos > qpos


def _head_cols(ref, head):
    return ref[:, head * HEAD_DIM:(head + 1) * HEAD_DIM]


def _stack_heads(ref, h):
    return jnp.concatenate([_head_cols(ref, GQA_GROUP * h + g) for g in range(GQA_GROUP)], axis=0)


def _band(prev_ref, cur_ref, h):
    return jnp.concatenate([_head_cols(prev_ref, h), _head_cols(cur_ref, h)], axis=0)


def _pick(prev, band):
    return jnp.where(prev, band[:WINDOW], band[WINDOW:])


def _spread(prev, x):
    return jnp.concatenate([jnp.where(prev, x, 0.0), jnp.where(prev, 0.0, x)], axis=0).astype(BF16)


def _attn_probs(q, kband, sink, prev, has_prev):
    scale = HEAD_DIM ** -0.5
    s_band = lax.dot_general(kband, q, NT_DIMS, preferred_element_type=F32)
    s = jnp.where(prev, jnp.where(has_prev, s_band[:WINDOW], NEG), s_band[WINDOW:]) * scale
    m = jnp.maximum(jnp.max(s, axis=0, keepdims=True), sink)
    e, es = jnp.exp(s - m), jnp.exp(sink - m)
    inv = 1.0 / (jnp.sum(e, axis=0, keepdims=True) + es)
    return e * inv, es * inv


def _attn_specs(nb):
    kcol, vcol = Q_WIDTH // KV_WIDTH, Q_WIDTH // KV_WIDTH + 1
    q_spec = pl.BlockSpec((WINDOW, Q_WIDTH), lambda n: (n, 0))
    return [q_spec,
            pl.BlockSpec((WINDOW, KV_WIDTH), lambda n: (n, kcol)),
            pl.BlockSpec((WINDOW, KV_WIDTH), lambda n: (jnp.maximum(n - 1, 0), kcol)),
            pl.BlockSpec((WINDOW, KV_WIDTH), lambda n: (n, vcol)),
            pl.BlockSpec((WINDOW, KV_WIDTH), lambda n: (jnp.maximum(n - 1, 0), vcol)),
            pl.BlockSpec((N_KV_HEADS, 8, ROWS), lambda n: (0, 0, 0))]


def attn_fwd(qkv, sink_rows, *, name, riders=()):
    s = qkv.shape[0]

    def body(q_ref, kc_ref, kp_ref, vc_ref, vp_ref, sink_ref, o_ref):
        prev = _prev_slots()
        has_prev = pl.program_id(0) > 0
        for h in range(N_KV_HEADS):
            p, _ = _attn_probs(_stack_heads(q_ref, h), _band(kp_ref, kc_ref, h), sink_ref[h, 0:1, :], prev, has_prev)
            o = lax.dot_general(_band(vp_ref, vc_ref, h), _spread(prev, p), TN_DIMS, preferred_element_type=F32).T
            for g in range(GQA_GROUP):
                head = GQA_GROUP * h + g
                o_ref[:, head * HEAD_DIM:(head + 1) * HEAD_DIM] = o[g * WINDOW:(g + 1) * WINDOW].astype(BF16)

    core, rr = _call(
        body, grid=(s // WINDOW,), in_specs=_attn_specs(s // WINDOW), out_specs=[pl.BlockSpec((WINDOW, Q_WIDTH), lambda n: (n, 0))],
        out_shape=[jax.ShapeDtypeStruct((s, Q_WIDTH), BF16)], operands=(qkv, qkv, qkv, qkv, qkv, sink_rows), sem=("parallel",),
        name=name, riders=riders)
    return _ret(core, rr, riders)


def attn_bwd(qkv, sink_rows, do, *, name, riders=()):
    s = qkv.shape[0]

    def body(q_ref, kc_ref, kp_ref, vc_ref, vp_ref, sink_ref, do_ref, dq_ref, dkc_ref, dkp_ref, dvc_ref, dvp_ref, dsink_ref):
        n = pl.program_id(0)
        prev = _prev_slots()
        scale = HEAD_DIM ** -0.5
        parts = []
        for h in range(N_KV_HEADS):
            qv, dov = _stack_heads(q_ref, h), _stack_heads(do_ref, h)
            kband, vband = _band(kp_ref, kc_ref, h), _band(vp_ref, vc_ref, h)
            p, ps = _attn_probs(qv, kband, sink_ref[h, 0:1, :], prev, n > 0)
            dp = _pick(prev, lax.dot_general(vband, dov, NT_DIMS, preferred_element_type=F32))
            delta = jnp.sum(p * dp, axis=0, keepdims=True)
            ds_band = _spread(prev, p * (dp - delta) * scale)
            p_band = _spread(prev, p)
            dk = jnp.dot(ds_band, qv, preferred_element_type=F32).astype(BF16)
            dv = jnp.dot(p_band, dov, preferred_element_type=F32).astype(BF16)
            dq = lax.dot_general(kband, ds_band, TN_DIMS, preferred_element_type=F32).T
            cols = slice(h * HEAD_DIM, (h + 1) * HEAD_DIM)
            dkp_ref[:, cols], dkc_ref[:, cols] = dk[:WINDOW], dk[WINDOW:]
            dvp_ref[:, cols], dvc_ref[:, cols] = dv[:WINDOW], dv[WINDOW:]
            dsink = -(ps * delta)
            for g in range(GQA_GROUP):
                head = GQA_GROUP * h + g
                dq_ref[:, head * HEAD_DIM:(head + 1) * HEAD_DIM] = dq[g * WINDOW:(g + 1) * WINDOW].astype(BF16)
                parts.append(jnp.broadcast_to(jnp.sum(dsink[:, g * WINDOW:(g + 1) * WINDOW], axis=1, keepdims=True), (8, LANES)))

        @pl.when(n == 0)
        def _():
            for i, part in enumerate(parts):
                dsink_ref[i // GQA_GROUP, i % GQA_GROUP] = part

        @pl.when(n > 0)
        def _():
            for i, part in enumerate(parts):
                dsink_ref[i // GQA_GROUP, i % GQA_GROUP] += part

    rows_q = pl.BlockSpec((WINDOW, Q_WIDTH), lambda n: (n, 0))
    rows_kv = pl.BlockSpec((WINDOW, KV_WIDTH), lambda n: (n, 0))
    kv_shape = jax.ShapeDtypeStruct((s, KV_WIDTH), BF16)
    core, rr = _call(
        body, grid=(s // WINDOW,), in_specs=_attn_specs(s // WINDOW) + [rows_q],
        out_specs=[rows_q, rows_kv, rows_kv, rows_kv, rows_kv,
                   pl.BlockSpec((N_KV_HEADS, GQA_GROUP, 8, LANES), lambda n: (0, 0, 0, 0))],
        out_shape=[jax.ShapeDtypeStruct((s, Q_WIDTH), BF16), kv_shape, kv_shape, kv_shape, kv_shape,
                   jax.ShapeDtypeStruct((N_KV_HEADS, GQA_GROUP, 8, LANES), F32)],
        operands=(qkv, qkv, qkv, qkv, qkv, sink_rows, do), sem=("arbitrary",), name=name, riders=riders)
    return _ret(core, rr, riders)


GELU_C = 0.7978845608028654
GELU_A = 0.044715


def _gelu(x):
    return 0.5 * x * (1.0 + jnp.tanh(GELU_C * (x + GELU_A * x * x * x)))


def _gelu_grad(x):
    t = jnp.tanh(GELU_C * (x + GELU_A * x * x * x))
    return 0.5 * (1.0 + t) + 0.5 * x * (1.0 - t * t) * GELU_C * (1.0 + 3.0 * GELU_A * x * x)


def _tril_bf16(w):
    row = lax.broadcasted_iota(I32, (SGU_CHUNK, SGU_CHUNK), 0)
    col = lax.broadcasted_iota(I32, (SGU_CHUNK, SGU_CHUNK), 1)
    return jnp.where(row >= col, w, 0.0).astype(BF16)


def _sgu_norm(vg, g, b):
    mu = jnp.mean(vg, axis=-1, keepdims=True)
    cen = vg - mu
    rstd = lax.rsqrt(jnp.mean(cen * cen, axis=-1, keepdims=True) + EPS)
    xhat = cen * rstd
    return xhat, rstd, xhat * g + b


def sgu_in_fwd(h, w_in, ln_g, ln_b, w_sp, b_sp, *, name, tm=256, riders=()):
    s, k = h.shape
    ns = w_in.shape[2]
    tm = _row_tile(s, tm)

    def body(h_ref, w0, w1, w2, w3, g_ref, b_ref, w_ref, bs_ref, z_ref, y_ref):
        hv = h_ref[...]
        zs = [jnp.dot(hv, w_ref_j[...], preferred_element_type=F32) for w_ref_j in (w0, w1, w2, w3)]
        for j, zj in enumerate(zs):
            z_ref[:, j * ns:(j + 1) * ns] = zj.astype(BF16)
        u = _gelu(jnp.concatenate(zs[:2], axis=1))
        _, _, vn = _sgu_norm(_gelu(jnp.concatenate(zs[2:], axis=1)), g_ref[...], b_ref[...])
        vn = vn.astype(BF16)
        for grp in range(SGU_GROUPS):
            w = _tril_bf16(w_ref[grp])
            cols = slice(grp * LANES, (grp + 1) * LANES)
            for ch in range(tm // SGU_CHUNK):
                rows = slice(ch * SGU_CHUNK, (ch + 1) * SGU_CHUNK)
                mixed = jnp.dot(w, vn[rows, cols], preferred_element_type=F32) + bs_ref[grp]
                y_ref[rows, cols] = (u[rows, cols] * mixed).astype(BF16)

    def shard(j):
        return pl.BlockSpec((None, k, ns), lambda i: (j, 0, 0))

    full3 = pl.BlockSpec((SGU_GROUPS, SGU_CHUNK, SGU_CHUNK), lambda i: (0, 0, 0))
    core, rr = _call(
        body, grid=(s // tm,),
        in_specs=[_row_spec(tm, k)] + [shard(j) for j in range(N_CHIPS)] + [_vec_spec(D_MODEL), _vec_spec(D_MODEL), full3, full3],
        out_specs=[_row_spec(tm, 2 * D_MODEL), _row_spec(tm, D_MODEL)],
        out_shape=[jax.ShapeDtypeStruct((s, 2 * D_MODEL), BF16), jax.ShapeDtypeStruct((s, D_MODEL), BF16)],
        operands=(h, w_in, w_in, w_in, w_in, ln_g, ln_b, w_sp, b_sp), sem=("parallel",), name=name, riders=riders)
    return _ret(core, rr, riders)


def sgu_bwd(z, dy, ln_g, ln_b, w_sp, b_sp, *, name, tm=256, riders=()):
    s = z.shape[0]
    tm = _row_tile(s, tm)

    def body(z_ref, dy_ref, g_ref, b_ref, w_ref, bs_ref, dz_ref, dw_ref, dbs_ref, dg_ref, db_ref, dvn_buf):
        first = pl.program_id(0) == 0
        zu, zv = z_ref[:, :D_MODEL].astype(F32), z_ref[:, D_MODEL:].astype(F32)
        u = _gelu(zu)
        xhat, rstd, vn = _sgu_norm(_gelu(zv), g_ref[...], b_ref[...])
        vn = vn.astype(BF16)
        dyv = dy_ref[...]
        dmixed = dyv * u
        row = lax.broadcasted_iota(I32, (SGU_CHUNK, SGU_CHUNK), 0)
        col = lax.broadcasted_iota(I32, (SGU_CHUNK, SGU_CHUNK), 1)
        for grp in range(SGU_GROUPS):
            w = _tril_bf16(w_ref[grp])
            cols = slice(grp * LANES, (grp + 1) * LANES)
            dw = jnp.zeros((SGU_CHUNK, SGU_CHUNK), F32)
            dbs = jnp.zeros((SGU_CHUNK, 1), F32)
            for ch in range(tm // SGU_CHUNK):
                rows = slice(ch * SGU_CHUNK, (ch + 1) * SGU_CHUNK)
                vblk = vn[rows, cols]
                mixed = jnp.dot(w, vblk, preferred_element_type=F32) + bs_ref[grp]
                dz_ref[rows, cols] = (dyv[rows, cols] * mixed * _gelu_grad(zu[rows, cols])).astype(BF16)
                dm = dmixed[rows, cols]
                dmb = dm.astype(BF16)
                dvn_buf[rows, cols] = lax.dot_general(w, dmb, TN_DIMS, preferred_element_type=F32)
                dw += lax.dot_general(dmb, vblk, NT_DIMS, preferred_element_type=F32)
                dbs += jnp.sum(dm, axis=-1, keepdims=True)
            dw = jnp.where(row >= col, dw, 0.0)
            dbs = jnp.broadcast_to(dbs, (SGU_CHUNK, SGU_CHUNK))

            @pl.when(first)
            def _():
                dw_ref[grp] = dw
                dbs_ref[grp] = dbs

            @pl.when(jnp.logical_not(first))
            def _():
                dw_ref[grp] += dw
                dbs_ref[grp] += dbs

        dvn = dvn_buf[...]
        dxhat = dvn * g_ref[...]
        dvg = rstd * (dxhat - jnp.mean(dxhat, axis=-1, keepdims=True) - xhat * jnp.mean(dxhat * xhat, axis=-1, keepdims=True))
        dz_ref[:, D_MODEL:] = (dvg * _gelu_grad(zv)).astype(BF16)
        _accum(dg_ref, jnp.sum(dvn * xhat, axis=0, keepdims=True), first)
        _accum(db_ref, jnp.sum(dvn, axis=0, keepdims=True), first)

    full3 = pl.BlockSpec((SGU_GROUPS, SGU_CHUNK, SGU_CHUNK), lambda i: (0, 0, 0))
    s3 = jax.ShapeDtypeStruct((SGU_GROUPS, SGU_CHUNK, SGU_CHUNK), F32)
    vshape = jax.ShapeDtypeStruct((1, D_MODEL), F32)
    core, rr = _call(
        body, grid=(s // tm,),
        in_specs=[_row_spec(tm, 2 * D_MODEL), _row_spec(tm, D_MODEL), _vec_spec(D_MODEL), _vec_spec(D_MODEL), full3, full3],
        out_specs=[_row_spec(tm, 2 * D_MODEL), full3, full3, _vec_spec(D_MODEL), _vec_spec(D_MODEL)],
        out_shape=[jax.ShapeDtypeStruct((s, 2 * D_MODEL), BF16), s3, s3, vshape, vshape],
        scratch_shapes=[pltpu.VMEM((tm, D_MODEL), F32)], operands=(z, dy, ln_g, ln_b, w_sp, b_sp), name=name, riders=riders)
    return _ret(core, rr, riders)


def _sigmoid(x):
    return 1.0 / (1.0 + jnp.exp(-x))


def ffn_up(h, w_gu, *, name, tm=512, riders=()):
    s = h.shape[0]
    tm = _row_tile(s, tm)

    def body(h_ref, wg_ref, wu_ref, d_ref, a_ref):
        hv = h_ref[...]
        g = jnp.dot(hv, wg_ref[...], preferred_element_type=F32)
        u = jnp.dot(hv, wu_ref[...], preferred_element_type=F32)
        sig = _sigmoid(g)
        silu = g * sig
        d_ref[0] = (u * (sig + silu * (1.0 - sig))).astype(BF16)
        d_ref[1] = silu.astype(BF16)
        a_ref[...] = (silu * u).astype(BF16)

    core, rr = _call(
        body, grid=(2, s // tm),
        in_specs=[pl.BlockSpec((tm, D_MODEL), lambda j, i: (i, 0)),
                  pl.BlockSpec((None, D_MODEL, FF_HALF), lambda j, i: (j, 0, 0)),
                  pl.BlockSpec((None, D_MODEL, FF_HALF), lambda j, i: (j + 2, 0, 0))],
        out_specs=[pl.BlockSpec((2, tm, FF_HALF), lambda j, i: (0, i, j)), pl.BlockSpec((tm, FF_HALF), lambda j, i: (i, j))],
        out_shape=[jax.ShapeDtypeStruct((2, s, D_FF), BF16), jax.ShapeDtypeStruct((s, D_FF), BF16)],
        operands=(h, w_gu, w_gu), sem=("parallel", "parallel"), name=name, riders=riders)
    return _ret(core, rr, riders)


def ffn_dact(df, w_d, gu, *, name, tm=512, riders=()):
    s = df.shape[0]
    tm = _row_tile(s, tm)

    def body(df_ref, w_ref, d_ref, o_ref):
        da = lax.dot_general(df_ref[...], w_ref[...], NT_DIMS, preferred_element_type=F32)
        o_ref[0] = (da * d_ref[0].astype(F32)).astype(BF16)
        o_ref[1] = (da * d_ref[1].astype(F32)).astype(BF16)

    planes = pl.BlockSpec((2, tm, FF_HALF), lambda j, i: (0, i, j))
    core, rr = _call(
        body, grid=(2, s // tm),
        in_specs=[pl.BlockSpec((tm, D_MODEL), lambda j, i: (i, 0)), pl.BlockSpec((FF_HALF, D_MODEL), lambda j, i: (j, 0)), planes],
        out_specs=[planes], out_shape=[jax.ShapeDtypeStruct((2, s, D_FF), BF16)], operands=(df, w_d, gu),
        sem=("parallel", "parallel"), name=name, riders=riders)
    return _ret(core, rr, riders)


def _weight_tile(rows):
    for tr in (512, 352, 256, 128):
        if rows % tr == 0:
            return tr
    return rows


def place_shard(w, layer, chip_arr, dtype, *, name, riders=()):
    _, r, c = w.shape
    tr = _weight_tile(r)

    def body(chip_ref, w_ref, o_ref):
        o_ref[...] = w_ref[...].astype(dtype)

    core, rr = _call(
        body, grid=(r // tr,), prefetch=(chip_arr,),
        in_specs=[pl.BlockSpec((None, tr, c), lambda i, chip: (layer, i, 0))],
        out_specs=[pl.BlockSpec((None, tr, c), lambda i, chip: (chip[0], i, 0))],
        out_shape=[jax.ShapeDtypeStruct((N_CHIPS, r, c), dtype)], operands=(w,), sem=("parallel",), name=name, riders=riders)
    return _ret(core, rr, riders)


def _adamw_math(w, g, m, v):
    m = ADAM_B1 * m + (1.0 - ADAM_B1) * g
    v = ADAM_B2 * v + (1.0 - ADAM_B2) * (g * g)
    m_hat = m / (1.0 - ADAM_B1 ** ADAM_STEP)
    v_hat = v / (1.0 - ADAM_B2 ** ADAM_STEP)
    delta = -ADAM_LR * (m_hat / (jnp.sqrt(v_hat) + ADAM_EPS) + ADAM_WD * w)
    return delta, m, v


def adamw(w, g, m, v, *, name):
    nl, r, c = w.shape
    tr = _weight_tile(r)

    def body(w_ref, g_ref, m_ref, v_ref, go_ref, d_ref, mo_ref, vo_ref):
        gv = g_ref[...]
        go_ref[...] = gv
        d_ref[...], mo_ref[...], vo_ref[...] = _adamw_math(w_ref[...], gv, m_ref[...], v_ref[...])

    spec = pl.BlockSpec((None, tr, c), lambda l, i: (l, i, 0))
    shape = jax.ShapeDtypeStruct(w.shape, F32)
    outs, _ = _call(body, grid=(nl, r // tr), in_specs=[spec] * 4, out_specs=[spec] * 4, out_shape=[shape] * 4,
                    operands=(w, g, m, v), sem=("parallel", "parallel"), name=name)
    return outs


def adamw_small(ws, gs, ms, vs, *, name):
    n = len(ws)

    def body(*refs):
        ins, outs = refs[:4 * n], refs[4 * n:]
        for t in range(n):
            gv = ins[n + t][...]
            outs[t][...] = gv
            outs[n + t][...], outs[2 * n + t][...], outs[3 * n + t][...] = _adamw_math(
                ins[t][...], gv, ins[2 * n + t][...], ins[3 * n + t][...])

    shapes = [jax.ShapeDtypeStruct(w.shape, F32) for w in ws]
    res = pl.pallas_call(body, out_shape=shapes * 4, name=name)(*ws, *gs, *ms, *vs)
    return res[:n], res[n:2 * n], res[2 * n:3 * n], res[3 * n:]


def pair_add(g, r1, c_arr, *, name):
    _, rows, cdim = g.shape
    h = rows // 2

    def body(c_ref, g_ref, r_ref, o_ref):
        o_ref[...] = (g_ref[...].astype(F32) + r_ref[...].astype(F32)).astype(o_ref.dtype)

    (out,), _ = _call(
        body, grid=(N_CHIPS,), prefetch=(c_arr,),
        in_specs=[pl.BlockSpec((None, h, cdim), lambda s, c: (s, c[0], 0)), pl.BlockSpec((None, h, cdim), lambda s, c: (s, 0, 0))],
        out_specs=[pl.BlockSpec((None, h, cdim), lambda s, c: (s, 0, 0))],
        out_shape=[jax.ShapeDtypeStruct((N_CHIPS, h, cdim), g.dtype)], operands=(g, r1), sem=("parallel",), name=name)
    return out


def final_add(g, r1, r2, jc_arr, *, dest_shape, lead, prev, name):
    _, rows, cdim = g.shape
    h = rows // 2

    def body(jc_ref, g_ref, r1_ref, r2_ref, *rest):
        o_ref = rest[-1]
        acc = g_ref[...].astype(F32) + r1_ref[...].astype(F32)
        for k in range(3):
            acc = acc + r2_ref[k].astype(F32)
        o_ref[...] = acc

    if lead is None:
        o_spec = pl.BlockSpec((h, cdim), lambda i, jc: (jc[1], 0))
    elif lead == "chip":
        o_spec = pl.BlockSpec((None, h, cdim), lambda i, jc: (jc[0], jc[1], 0))
    else:
        o_spec = pl.BlockSpec((None, h, cdim), lambda i, jc: (lead, jc[1], 0))
    in_specs = [pl.BlockSpec((None, h, cdim), lambda i, jc: (jc[0], jc[1], 0)),
                pl.BlockSpec((None, h, cdim), lambda i, jc: (jc[0], 0, 0)),
                pl.BlockSpec((3, h, cdim), lambda i, jc: (0, 0, 0))]
    operands = [g, r1, r2]
    aliases = None
    if prev is not None:
        in_specs.append(ANY)
        operands.append(prev)
        aliases = {3: 0}
    (out,), _ = _call(body, grid=(1,), prefetch=(jc_arr,), in_specs=in_specs, out_specs=[o_spec],
                      out_shape=[jax.ShapeDtypeStruct(dest_shape, F32)], operands=operands, aliases=aliases, name=name)
    return out


def _place():
    return lax.axis_index("x"), lax.axis_index("y"), lax.axis_index("c")


def _partner(x, y, k):
    return (1 - x if k >> 1 else x), (1 - y if k & 1 else y)


def _half(rows, sel, dtype):
    align = 16 if dtype == BF16 else 8
    return pl.ds(pl.multiple_of(sel * (rows // 2), align), rows // 2)


def _rider(inputs, aliased, fresh, nsem, copies, arrivals):
    def start(ins, outs, send, recv):
        for cp in copies(ins, outs, send, recv):
            cp.start()

    def finish(ins, outs, send, recv):
        for cp in arrivals(ins, outs, send, recv):
            cp.wait_recv()
        for cp in copies(ins, outs, send, recv):
            cp.wait_send()

    return types.SimpleNamespace(inputs=list(inputs), aliased=list(aliased), fresh=list(fresh), nsem=nsem, start=start,
                                 finish=finish)


def _remote(src, dst, send, recv, idx, dev):
    return pltpu.make_async_remote_copy(src_ref=src, dst_ref=dst, send_sem=send.at[idx], recv_sem=recv.at[idx],
                                        device_id=dev, device_id_type=MESH)


def gather_ici_rider(fulls):
    nt = len(fulls)

    def region(outs, t, slot, sel):
        return outs[t].at[slot, _half(fulls[t].shape[1], sel, fulls[t].dtype)]

    def copies(ins, outs, send, recv):
        x, y, c = _place()
        res = []
        for t in range(nt):
            for k in (1, 2, 3):
                px, py = _partner(x, y, k)
                mine = region(outs, t, 2 * x + y, c)
                res.append(_remote(mine, mine, send, recv, 3 * t + k - 1, (px, py, c)))
        return res

    def arrivals(ins, outs, send, recv):
        x, y, c = _place()
        res = []
        for t in range(nt):
            for k in (1, 2, 3):
                px, py = _partner(x, y, k)
                theirs = region(outs, t, 2 * px + py, c)
                res.append(_remote(theirs, theirs, send, recv, 3 * t + k - 1, (x, y, c)))
        return res

    return _rider(fulls, range(nt), [], 3 * nt, copies, arrivals)


def gather_d2d_rider(fulls):
    nt = len(fulls)

    def region(outs, t, slot, sel):
        return outs[t].at[slot, _half(fulls[t].shape[1], sel, fulls[t].dtype)]

    def both(outs, send, recv, mine):
        x, y, c = _place()
        res = []
        for t in range(nt):
            for k in (1, 2, 3):
                px, py = _partner(x, y, k)
                part = region(outs, t, 2 * px + py, c if mine else 1 - c)
                res.append(_remote(part, part, send, recv, 3 * t + k - 1, (x, y, 1 - c)))
        return res

    return _rider(fulls, range(nt), [], 3 * nt, lambda i, o, s, r: both(o, s, r, True), lambda i, o, s, r: both(o, s, r, False))


def exchange_rider(grads):
    nt = len(grads)

    def both(ins, outs, send, recv):
        x, y, c = _place()
        return [_remote(ins[t].at[:, _half(grads[t].shape[1], 1 - c, grads[t].dtype)], outs[t], send, recv, t, (x, y, 1 - c))
                for t in range(nt)]

    fresh = [jax.ShapeDtypeStruct((N_CHIPS, g.shape[1] // 2, g.shape[2]), g.dtype) for g in grads]
    return _rider(grads, [], fresh, nt, both, both)


def scatter_rider(parts):
    nt = len(parts)

    def both(ins, outs, send, recv):
        x, y, c = _place()
        res = []
        for t in range(nt):
            for k in (1, 2, 3):
                px, py = _partner(x, y, k)
                res.append(_remote(ins[t].at[2 * px + py], outs[t].at[k - 1], send, recv, 3 * t + k - 1, (px, py, c)))
        return res

    fresh = [jax.ShapeDtypeStruct((3,) + p.shape[1:], p.dtype) for p in parts]
    return _rider(parts, [], fresh, 3 * nt, both, both)


def broadcast_rider(bufs, items):
    def region(outs, item, sel):
        bi, lead = item
        ref = outs[bi]
        if lead == "chip":
            x, y, _ = _place()
            ref = ref.at[2 * x + y]
        elif lead is not None:
            ref = ref.at[lead]
        return ref.at[_half(ref.shape[0], sel, F32)]

    def both(outs, send, recv, mine):
        x, y, c = _place()
        res = []
        for i, item in enumerate(items):
            part = region(outs, item, c if mine else 1 - c)
            res.append(_remote(part, part, send, recv, i, (x, y, 1 - c)))
        return res

    return _rider(bufs, range(len(bufs)), [], len(items), lambda i, o, s, r: both(o, s, r, True),
                  lambda i, o, s, r: both(o, s, r, False))


def allcast_rider(buf):
    peers = [(k, flip) for k in range(N_CHIPS) for flip in (0, 1) if (k, flip) != (0, 0)]

    def both(outs, send, recv, mine):
        x, y, c = _place()
        res = []
        for i, (k, flip) in enumerate(peers):
            px, py = _partner(x, y, k)
            pc = 1 - c if flip else c
            slot, sel = (2 * x + y, c) if mine else (2 * px + py, pc)
            part = outs[0].at[slot, _half(buf.shape[1], sel, F32)]
            res.append(_remote(part, part, send, recv, i, (px, py, pc)))
        return res

    return _rider([buf], [0], [], len(peers), lambda i, o, s, r: both(o, s, r, True), lambda i, o, s, r: both(o, s, r, False))


def comm_call(riders, *, name):
    _, res = _call(None, riders=riders, name=name)
    return res


SLAB_ROWS = 192


def _pad_rows(a, rows=8):
    return jnp.pad(a, ((0, rows - a.shape[0]), (0, 0)))


def _pack_small(norm_grads, db_qkv, db_o, dsinks, db_sp, dln_g, dln_b, dw_sp):
    parts = [
        jnp.concatenate(norm_grads, axis=0),
        _pad_rows(jnp.pad(db_qkv, ((0, 0), (0, 2 * D_MODEL - QKV_WIDTH))).reshape(2, D_MODEL)),
        _pad_rows(db_o),
        _pad_rows(jnp.pad(dsinks.reshape(1, N_Q_HEADS), ((0, 0), (0, D_MODEL - N_Q_HEADS)))),
        _pad_rows(db_sp.reshape(1, D_MODEL)),
        _pad_rows(jnp.concatenate([dln_g, dln_b], axis=0)),
        dw_sp.reshape(SGU_CHUNK, D_MODEL),
    ]
    slab = jnp.concatenate(parts, axis=0)
    return jnp.pad(slab, ((0, SLAB_ROWS - slab.shape[0]), (0, 0))).reshape(N_CHIPS, SLAB_ROWS // N_CHIPS, D_MODEL)


def _unpack_small(slab, j):
    slab = slab.reshape(SLAB_ROWS, D_MODEL)
    norms = [slab[2 * i:2 * i + 2] for i in range(4)]
    db_qkv = slab[8:10].reshape(1, 2 * D_MODEL)[:, :QKV_WIDTH]
    db_o = slab[16:17]
    dsinks = slab[24:25, :N_Q_HEADS]
    db_sp = slab[32:33].reshape(SGU_GROUPS, SGU_CHUNK)
    width = D_MODEL // N_CHIPS
    dln_g = lax.dynamic_slice(slab[40:41], (0, j * width), (1, width))
    dln_b = lax.dynamic_slice(slab[41:42], (0, j * width), (1, width))
    dw_sp = slab[48:48 + SGU_CHUNK].reshape(SGU_GROUPS * SGU_CHUNK, SGU_CHUNK)
    return norms, db_qkv, db_o, dsinks, db_sp, dln_g, dln_b, dw_sp


class _GradReduce:
    def __init__(self, c_arr, jc_arr, dest_shapes):
        self.c_arr, self.jc_arr, self.dest_shapes = c_arr, jc_arr, dest_shapes
        self.grad, self.sibling, self.pair, self.chips, self.dest = {}, {}, {}, {}, {}

    def exchange(self, tags):
        return exchange_rider([self.grad[t] for t in tags])

    def exchanged(self, tags, res):
        for t, r in zip(tags, res):
            self.sibling[t] = r
            self.pair[t] = pair_add(self.grad[t], r, self.c_arr, name=f"pair_add_{t}")

    def scatter(self, tags):
        return scatter_rider([self.pair[t] for t in tags])

    def scattered(self, tags, res, where):
        for t, r in zip(tags, res):
            name, lead = where[t]
            self.dest[name] = final_add(self.grad[t], self.sibling[t], r, self.jc_arr, dest_shape=self.dest_shapes[name],
                                        lead=lead, prev=self.dest.get(name), name=f"final_add_{t}")

    def broadcast(self, items):
        names = []
        for n, _ in items:
            if n not in names:
                names.append(n)
        return names, broadcast_rider([self.dest[n] for n in names], [(names.index(n), lead) for n, lead in items])

    def broadcasted(self, names, res):
        for n, r in zip(names, res):
            self.dest[n] = r


def kernel(x, norm_mix_pre, norm_mix_post, norm_ffn_pre, norm_ffn_post, attn_w_qkv, attn_b_qkv, attn_sinks, attn_w_o, attn_b_o, sgu_w_in, sgu_ln_g, sgu_ln_b, sgu_w_spatial, sgu_b_spatial, sgu_w_out, ffn_w_gate_up, ffn_w_down, loss_target, m_norm_mix_pre, m_norm_mix_post, m_norm_ffn_pre, m_norm_ffn_post, m_attn_w_qkv, m_attn_b_qkv, m_attn_sinks, m_attn_w_o, m_attn_b_o, m_sgu_w_in, m_sgu_ln_g, m_sgu_ln_b, m_sgu_w_spatial, m_sgu_b_spatial, m_sgu_w_out, m_ffn_w_gate_up, m_ffn_w_down, v_norm_mix_pre, v_norm_mix_post, v_norm_ffn_pre, v_norm_ffn_post, v_attn_w_qkv, v_attn_b_qkv, v_attn_sinks, v_attn_w_o, v_attn_b_o, v_sgu_w_in, v_sgu_ln_g, v_sgu_ln_b, v_sgu_w_spatial, v_sgu_b_spatial, v_sgu_w_out, v_ffn_w_gate_up, v_ffn_w_down):
    s = x.shape[1]
    x0 = x.reshape(s, D_MODEL)
    target = loss_target.reshape(s, D_MODEL)
    mx, my, mc = lax.axis_index("x"), lax.axis_index("y"), lax.axis_index("c")
    chip = 2 * mx + my
    chip_arr = jnp.reshape(chip, (1,)).astype(I32)
    c_arr = jnp.reshape(mc, (1,)).astype(I32)
    jc_arr = jnp.stack([chip, mc]).astype(I32)
    zero_bias = jnp.zeros((1, D_MODEL), F32)

    def gain(p, i):
        return p[i:i + 1]

    big = [attn_w_qkv, attn_w_o, sgu_w_in, sgu_w_out, ffn_w_gate_up, ffn_w_gate_up, ffn_w_down, ffn_w_down]
    layers = [0, 0, 0, 0, 0, 1, 0, 1]
    tags = ["qkv", "wo", "win", "wout", "wgu0", "wgu1", "wd0", "wd1"]
    full = {t: place_shard(w, l, chip_arr, BF16, name=f"place_{t}") for w, l, t in zip(big, layers, tags) if t != "wgu1"}
    ln_pack = _pad_rows(jnp.concatenate([sgu_ln_g, sgu_ln_b], axis=0), 16)[None]
    full["ln"] = place_shard(ln_pack, 0, chip_arr, F32, name="place_ln")

    def ici(*names):
        return gather_ici_rider([full[n] for n in names])

    def d2d(*names):
        return gather_d2d_rider([full[n] for n in names])

    def landed(names, res):
        for n, r in zip(names, res):
            full[n] = r

    cos, sin = _rope_tables(s)
    sink_rows = jnp.broadcast_to(
        jnp.repeat(attn_sinks.reshape(N_KV_HEADS, GQA_GROUP), WINDOW, axis=1)[:, None, :], (N_KV_HEADS, 8, ROWS))
    w_sp = sgu_w_spatial.reshape(SGU_GROUPS, SGU_CHUNK, SGU_CHUNK)
    b_sp = jnp.broadcast_to(sgu_b_spatial.reshape(SGU_GROUPS, SGU_CHUNK)[:, :, None], (SGU_GROUPS, SGU_CHUNK, LANES))

    h0, (res,) = prenorm(x0, gain(norm_mix_pre, 0), name="prenorm_0", riders=[ici("qkv", "ln")])
    landed(("qkv", "ln"), res)
    full["wgu1"], (res,) = place_shard(ffn_w_gate_up, 1, chip_arr, BF16, name="place_wgu1", riders=[d2d("qkv", "ln")])
    landed(("qkv", "ln"), res)
    ln_g = full["ln"][:, 0, :].reshape(1, D_MODEL)
    ln_b = full["ln"][:, 1, :].reshape(1, D_MODEL)

    qkv, (res,) = qkv_proj(h0, full["qkv"], attn_b_qkv, cos, sin, name="qkv_proj", riders=[ici("wo", "wd0")])
    landed(("wo", "wd0"), res)
    o, (res_a, res_b) = attn_fwd(qkv, sink_rows, name="attn_fwd", riders=[d2d("wo", "wd0"), ici("wgu0")])
    landed(("wo", "wd0"), res_a)
    landed(("wgu0",), res_b)
    w_o = full["wo"].reshape(Q_WIDTH, D_MODEL)
    (x1, h1, m0), (res_a, res_b) = proj_residual_norm(
        o, w_o, x0, attn_b_o, gain(norm_mix_post, 0), gain(norm_ffn_pre, 0), name="attn_out_norm",
        riders=[d2d("wgu0"), ici("win", "wout")])
    landed(("wgu0",), res_a)
    landed(("win", "wout"), res_b)
    (gu0, a0), (res_a, res_b) = ffn_up(h1, full["wgu0"], name="ffn_up_0", riders=[d2d("win", "wout"), ici("wgu1")])
    landed(("win", "wout"), res_a)
    landed(("wgu1",), res_b)
    w_d0 = full["wd0"].reshape(D_FF, D_MODEL)
    (x2, h2, f0), (res_a, res_b) = proj_residual_norm(
        a0, w_d0, x1, zero_bias, gain(norm_ffn_post, 0), gain(norm_mix_pre, 1), name="ffn_down_norm_0",
        riders=[d2d("wgu1"), ici("wd1")])
    landed(("wgu1",), res_a)
    landed(("wd1",), res_b)
    (z, y), (res,) = sgu_in_fwd(h2, full["win"], ln_g, ln_b, w_sp, b_sp, name="sgu_in_fwd", riders=[d2d("wd1")])
    landed(("wd1",), res)
    w_qkv, w_in, w_gu0, w_gu1 = full["qkv"], full["win"], full["wgu0"], full["wgu1"]
    w_out = full["wout"].reshape(D_MODEL, D_MODEL)
    w_d1 = full["wd1"].reshape(D_FF, D_MODEL)
    x3, h3, m1 = proj_residual_norm(y, w_out, x2, zero_bias, gain(norm_mix_post, 1), gain(norm_ffn_pre, 1), name="sgu_out_norm")
    gu1, a1 = ffn_up(h3, w_gu1, name="ffn_up_1")
    dx4, df1, dg_fpost1, loss_part = proj_loss_head(a1, w_d1, x3, gain(norm_ffn_post, 1), target, name="ffn_down_loss")
    loss = lax.psum(loss_part[0, 0], ("x", "y", "c"))

    red = _GradReduce(c_arr, jc_arr, {
        "qkv": attn_w_qkv.shape[1:], "wo": attn_w_o.shape[1:], "win": sgu_w_in.shape[1:], "wout": sgu_w_out.shape[1:],
        "wgu": ffn_w_gate_up.shape, "wd": ffn_w_down.shape, "slab": (N_CHIPS, SLAB_ROWS // N_CHIPS, D_MODEL)})
    where = {"qkv": ("qkv", None), "wo": ("wo", None), "win": ("win", None), "wout": ("wout", None), "wgu0": ("wgu", 0),
             "wgu1": ("wgu", 1), "wd0": ("wd", 0), "wd1": ("wd", 1), "small": ("slab", "chip")}

    dgu1 = ffn_dact(df1, w_d1, gu1, name="ffn_dact_1")
    red.grad["wd1"] = mm_tn(a1, df1, shard_major=False, tm=256, tn=D_MODEL, name="dw_down_1").reshape(
        N_CHIPS, D_FF // N_CHIPS, D_MODEL)
    red.grad["wgu1"], (res,) = mm_tn(h3, dgu1, shard_major=True, tm=512, tn=FF_HALF, name="dw_gate_up_1",
                                     riders=[red.exchange(["wd1"])])
    red.exchanged(["wd1"], res)
    (dx3, dm1, dg_fpre1, dg_mpost1, _), (res_a, res_b) = dh_norm_bwd_pair(
        dgu1, w_gu1, dx4, x3, gain(norm_ffn_pre, 1), m1, gain(norm_mix_post, 1), name="dh_ffn_norm_1",
        riders=[red.exchange(["wgu1"]), red.scatter(["wd1"])])
    red.exchanged(["wgu1"], res_a)
    red.scattered(["wd1"], res_b, where)
    names, rider = red.broadcast([("wd", 1)])
    dy, (res,) = mm_nt(dm1, w_out, out_dtype=F32, name="dy_sgu", riders=[rider])
    red.broadcasted(names, res)
    red.grad["wout"] = mm_tn(y, dm1, shard_major=False, tm=512, tn=D_MODEL, name="dw_sgu_out").reshape(
        N_CHIPS, D_MODEL // N_CHIPS, D_MODEL)
    (dz, dw_sp, db_sp, dln_g, dln_b), (res_a, res_b) = sgu_bwd(
        z, dy, ln_g, ln_b, w_sp, b_sp, name="sgu_bwd", riders=[red.scatter(["wgu1"]), red.exchange(["wout"])])
    red.scattered(["wgu1"], res_a, where)
    red.exchanged(["wout"], res_b)
    names, rider = red.broadcast([("wgu", 1)])
    red.grad["win"], (res_a, res_b) = mm_tn(h2, dz, shard_major=True, tm=D_MODEL, tn=2 * D_MODEL // N_CHIPS, name="dw_sgu_in",
                                            riders=[rider, red.scatter(["wout"])])
    red.broadcasted(names, res_a)
    red.scattered(["wout"], res_b, where)
    names, rider = red.broadcast([("wout", None)])
    (dx2, df0, dg_mpre1, dg_fpost0, _), (res_a, res_b) = dh_norm_bwd_pair(
        dz, w_in, dx3, x2, gain(norm_mix_pre, 1), f0, gain(norm_ffn_post, 0), name="dh_sgu_norm",
        riders=[red.exchange(["win"]), rider])
    red.exchanged(["win"], res_a)
    red.broadcasted(names, res_b)
    dgu0, (res,) = ffn_dact(df0, w_d0, gu0, name="ffn_dact_0", riders=[red.scatter(["win"])])
    red.scattered(["win"], res, where)
    names, rider = red.broadcast([("win", None)])
    dw_d0, (res,) = mm_tn(a0, df0, shard_major=False, tm=256, tn=D_MODEL, name="dw_down_0", riders=[rider])
    red.broadcasted(names, res)
    red.grad["wd0"] = dw_d0.reshape(N_CHIPS, D_FF // N_CHIPS, D_MODEL)
    red.grad["wgu0"], (res,) = mm_tn(h1, dgu0, shard_major=True, tm=512, tn=FF_HALF, name="dw_gate_up_0",
                                     riders=[red.exchange(["wd0"])])
    red.exchanged(["wd0"], res)
    (dx1, dm0, dg_fpre0, dg_mpost0, db_o), (res_a, res_b) = dh_norm_bwd_pair(
        dgu0, w_gu0, dx2, x1, gain(norm_ffn_pre, 0), m0, gain(norm_mix_post, 0), name="dh_ffn_norm_0",
        riders=[red.exchange(["wgu0"]), red.scatter(["wd0"])])
    red.exchanged(["wgu0"], res_a)
    red.scattered(["wd0"], res_b, where)
    names, rider = red.broadcast([("wd", 0)])
    do, (res,) = mm_nt(dm0, w_o, out_dtype=BF16, name="do_attn", riders=[rider])
    red.broadcasted(names, res)
    red.grad["wo"] = mm_tn(o, dm0, shard_major=False, tm=512, tn=D_MODEL, name="dw_attn_out").reshape(
        N_CHIPS, Q_WIDTH // N_CHIPS, D_MODEL)
    (dq, dkc, dkp, dvc, dvp, dsink), (res_a, res_b) = attn_bwd(
        qkv, sink_rows, do, name="attn_bwd", riders=[red.scatter(["wgu0"]), red.exchange(["wo"])])
    red.scattered(["wgu0"], res_a, where)
    red.exchanged(["wo"], res_b)
    names, rider = red.broadcast([("wgu", 0)])
    (dqkv, db_qkv), (res_a, res_b) = rope_bwd(dq, dkc, dkp, dvc, dvp, cos, sin, name="rope_bwd",
                                              riders=[rider, red.scatter(["wo"])])
    red.broadcasted(names, res_a)
    red.scattered(["wo"], res_b, where)
    names, rider = red.broadcast([("wo", None)])
    red.grad["qkv"], (res,) = mm_tn(h0, dqkv, shard_major=True, tm=D_MODEL, tn=QKV_WIDTH // N_CHIPS, name="dw_qkv",
                                    riders=[rider])
    red.broadcasted(names, res)
    dh0, (res,) = mm_nt(dqkv, w_qkv, out_dtype=F32, tm=1024, name="dh_attn", riders=[red.exchange(["qkv"])])
    red.exchanged(["qkv"], res)
    grad_x, dg_mpre0 = norm_bwd_last(dx1, dh0, x0, gain(norm_mix_pre, 0), name="norm_bwd_in")

    norm_grads = [jnp.concatenate(p, axis=0) for p in
                  ((dg_mpre0, dg_mpre1), (dg_mpost0, dg_mpost1), (dg_fpre0, dg_fpre1), (dg_fpost0, dg_fpost1))]
    red.grad["small"] = _pack_small(norm_grads, db_qkv, db_o, dsink[:, :, 0, 0], db_sp[:, :, 0], dln_g, dln_b, dw_sp)
    res_a, res_b = comm_call([red.scatter(["qkv"]), red.exchange(["small"])], name="tail_1")
    red.scattered(["qkv"], res_a, where)
    red.exchanged(["small"], res_b)
    names, rider = red.broadcast([("qkv", None)])
    res_a, res_b = comm_call([red.scatter(["small"]), rider], name="tail_2")
    red.scattered(["small"], res_a, where)
    red.broadcasted(names, res_b)
    ((slab_full,),) = comm_call([allcast_rider(red.dest["slab"])], name="tail_3")
    g_qkv, g_wo, g_win, g_wout, g_wgu, g_wd = (red.dest[n] for n in ("qkv", "wo", "win", "wout", "wgu", "wd"))
    g_norms, g_bqkv, g_bo, g_sinks, g_bsp, g_lng, g_lnb, g_wsp = _unpack_small(slab_full, chip)

    def big_update(w, g, m, v, tag):
        return adamw(w, g.reshape(w.shape), m, v, name=f"adamw_{tag}")

    upd = {
        "attn_w_qkv": big_update(attn_w_qkv, g_qkv, m_attn_w_qkv, v_attn_w_qkv, "qkv"),
        "attn_w_o": big_update(attn_w_o, g_wo, m_attn_w_o, v_attn_w_o, "wo"),
        "sgu_w_in": big_update(sgu_w_in, g_win, m_sgu_w_in, v_sgu_w_in, "win"),
        "sgu_w_out": big_update(sgu_w_out, g_wout, m_sgu_w_out, v_sgu_w_out, "wout"),
        "ffn_w_gate_up": big_update(ffn_w_gate_up, g_wgu, m_ffn_w_gate_up, v_ffn_w_gate_up, "wgu"),
        "ffn_w_down": big_update(ffn_w_down, g_wd, m_ffn_w_down, v_ffn_w_down, "wd"),
    }
    small_names = ["norm_mix_pre", "norm_mix_post", "norm_ffn_pre", "norm_ffn_post", "attn_b_qkv", "attn_sinks", "attn_b_o",
                   "sgu_ln_g", "sgu_ln_b", "sgu_w_spatial", "sgu_b_spatial"]
    small_w = [norm_mix_pre, norm_mix_post, norm_ffn_pre, norm_ffn_post, attn_b_qkv, attn_sinks, attn_b_o, sgu_ln_g, sgu_ln_b,
               sgu_w_spatial, sgu_b_spatial]
    small_m = [m_norm_mix_pre, m_norm_mix_post, m_norm_ffn_pre, m_norm_ffn_post, m_attn_b_qkv, m_attn_sinks, m_attn_b_o,
               m_sgu_ln_g, m_sgu_ln_b, m_sgu_w_spatial, m_sgu_b_spatial]
    small_v = [v_norm_mix_pre, v_norm_mix_post, v_norm_ffn_pre, v_norm_ffn_post, v_attn_b_qkv, v_attn_sinks, v_attn_b_o,
               v_sgu_ln_g, v_sgu_ln_b, v_sgu_w_spatial, v_sgu_b_spatial]
    small_g = g_norms + [g_bqkv, g_sinks, g_bo, g_lng, g_lnb, g_wsp, g_bsp]

    def flat2(a):
        return a.reshape(-1, a.shape[-1])

    res = adamw_small([flat2(a) for a in small_w], [flat2(a) for a in small_g], [flat2(a) for a in small_m],
                      [flat2(a) for a in small_v], name="adamw_small")
    for i, nm in enumerate(small_names):
        upd[nm] = tuple(r[i].reshape(small_w[i].shape) for r in res)

    order = ["norm_mix_pre", "norm_mix_post", "norm_ffn_pre", "norm_ffn_post", "attn_w_qkv", "attn_b_qkv", "attn_sinks",
             "attn_w_o", "attn_b_o", "sgu_w_in", "sgu_ln_g", "sgu_ln_b", "sgu_w_spatial", "sgu_b_spatial", "sgu_w_out",
             "ffn_w_gate_up", "ffn_w_down"]
    outs = [loss, grad_x.reshape(1, s, D_MODEL)]
    for part in range(4):
        outs += [upd[nm][part] for nm in order]
    return tuple(outs)
```

```python
import types

import jax
import jax.numpy as jnp
from jax import lax
from jax.experimental import pallas as pl
from jax.experimental.pallas import tpu as pltpu

F32 = jnp.float32
BF16 = jnp.bfloat16
I32 = jnp.int32

D_MODEL = 1024
HEAD_DIM = 64
N_Q_HEADS = 16
N_KV_HEADS = 4
GQA_GROUP = 4
WINDOW = 128
Q_WIDTH = 1024
KV_WIDTH = 256
QKV_WIDTH = 1536
ROPE_THETA = 10000.0
SGU_GROUPS = 8
SGU_CHUNK = 128
D_FF = 2816
FF_HALF = D_FF // 2
EPS = 1e-6
N_CHIPS = 4
LANES = 128

ADAM_LR = 0.001
ADAM_B1 = 0.9
ADAM_B2 = 0.999
ADAM_EPS = 1e-08
ADAM_WD = 0.01
ADAM_STEP = 10

VMEM_LIMIT = 52 * 1024 * 1024
MESH = pl.DeviceIdType.MESH
NEG = -1e30
NT_DIMS = (((1,), (1,)), ((), ()))
TN_DIMS = (((0,), (0,)), ((), ()))
NN_DIMS = (((1,), (0,)), ((), ()))
ANY = pl.BlockSpec(memory_space=pl.ANY)


def _row_tile(s, want):
    return want if s % want == 0 else s


PEER_KINDS = ("sibling", "chips", "sibling+chips", "everyone")


def _peer_kind(riders):
    kinds = {r.peers for r in riders}
    if not kinds:
        return None
    if "everyone" in kinds:
        return "everyone"
    return "sibling+chips" if len(kinds) == 2 else kinds.pop()


def _peer_barrier(kind):
    x, y, c = _place()
    chips = [(*_partner(x, y, k), c) for k in (1, 2, 3)]
    peers = {"sibling": [(x, y, 1 - c)], "chips": chips, "sibling+chips": [(x, y, 1 - c)] + chips,
             "everyone": [(x, y, 1 - c)] + chips + [(px, py, 1 - c) for px, py, _ in chips]}[kind]
    barrier = pltpu.get_barrier_semaphore()
    for dev in peers:
        pl.semaphore_signal(barrier, inc=1, device_id=dev, device_id_type=MESH)
    pl.semaphore_wait(barrier, len(peers))


def _call(body, *, name, grid=(), in_specs=(), out_specs=(), out_shape=(), scratch_shapes=(), operands=(), prefetch=(),
          aliases=None, riders=(), sem=None):
    n_pre, n_in, n_out, n_scr = len(prefetch), len(operands), len(out_shape), len(scratch_shapes)
    in_specs, out_specs, out_shape = list(in_specs), list(out_specs), list(out_shape)
    operands, scratch_shapes = list(operands), list(scratch_shapes)
    io_alias = {n_pre + i: o for i, o in (aliases or {}).items()}
    for r in riders:
        base_in, base_out = n_pre + len(operands), len(out_shape)
        operands += list(r.inputs)
        in_specs += [ANY] * len(r.inputs)
        for pos, i in enumerate(r.aliased):
            io_alias[base_in + i] = base_out + pos
            out_shape.append(jax.ShapeDtypeStruct(r.inputs[i].shape, r.inputs[i].dtype))
        out_shape += list(r.fresh)
        out_specs += [ANY] * (len(r.aliased) + len(r.fresh))
        scratch_shapes += [pltpu.SemaphoreType.DMA((r.nsem,)), pltpu.SemaphoreType.DMA((r.nsem,))]

    def wrapped(*refs):
        pre, p = refs[:n_pre], n_pre
        core_in, p = refs[p:p + n_in], p + n_in
        r_in = []
        for r in riders:
            r_in.append(refs[p:p + len(r.inputs)])
            p += len(r.inputs)
        core_out, p = refs[p:p + n_out], p + n_out
        r_out = []
        for r in riders:
            k = len(r.aliased) + len(r.fresh)
            r_out.append(refs[p:p + k])
            p += k
        core_scr, p = refs[p:p + n_scr], p + n_scr
        r_sem = [refs[p + 2 * i:p + 2 * i + 2] for i in range(len(riders))]

        def edge(at_last, fns):
            def run():
                if not at_last:
                    _peer_barrier(peer_kind)
                for i, r in enumerate(riders):
                    getattr(r, fns)(r_in[i], r_out[i], r_sem[i][0], r_sem[i][1])
            if not riders:
                return
            if not grid:
                run()
                return
            cond = None
            for d, n in enumerate(grid):
                c = pl.program_id(d) == (n - 1 if at_last else 0)
                cond = c if cond is None else jnp.logical_and(cond, c)
            pl.when(cond)(run)

        edge(False, "start")
        if body is not None:
            body(*pre, *core_in, *core_out, *core_scr)
        edge(True, "finish")

    if sem is None or riders:
        sem = ("arbitrary",) * len(grid)
    kwargs = dict(out_shape=out_shape, input_output_aliases=io_alias, name=name)
    peer_kind = _peer_kind(riders)
    collective = {} if peer_kind is None else {"collective_id": PEER_KINDS.index(peer_kind)}
    if grid:
        kwargs["compiler_params"] = pltpu.CompilerParams(dimension_semantics=sem, vmem_limit_bytes=VMEM_LIMIT, **collective)
    elif collective:
        kwargs["compiler_params"] = pltpu.CompilerParams(**collective)
    if n_pre:
        kwargs["grid_spec"] = pltpu.PrefetchScalarGridSpec(
            num_scalar_prefetch=n_pre, grid=grid, in_specs=in_specs, out_specs=out_specs, scratch_shapes=scratch_shapes)
    else:
        kwargs.update(grid=grid, in_specs=in_specs, out_specs=out_specs, scratch_shapes=scratch_shapes)
    res = pl.pallas_call(wrapped, **kwargs)(*prefetch, *operands)
    core, rest, rider_res = list(res[:n_out]), list(res[n_out:]), []
    for r in riders:
        k = len(r.aliased) + len(r.fresh)
        rider_res.append(rest[:k])
        rest = rest[k:]
    return core, rider_res


def _mm_call(*, grid, in_specs, out_spec, out_shape, dims, nk, kaxis, acc_shape, name, operands, riders=()):
    out_dtype = out_shape.dtype

    def body(a_ref, b_ref, o_ref, *scratch):
        p = lax.dot_general(a_ref[...].astype(BF16), b_ref[...].astype(BF16), dims, preferred_element_type=F32)
        if nk == 1:
            o_ref[...] = p.astype(out_dtype)
        else:
            acc = scratch[0]
            kk = pl.program_id(kaxis)

            @pl.when(kk == 0)
            def _():
                acc[...] = p

            @pl.when(kk > 0)
            def _():
                acc[...] += p

            @pl.when(kk == nk - 1)
            def _():
                o_ref[...] = acc[...].astype(out_dtype)

    sem = ["parallel"] * len(grid)
    if nk > 1:
        sem[kaxis] = "arbitrary"
    (out,), rider_res = _call(
        body, grid=grid, in_specs=in_specs, out_specs=[out_spec], out_shape=[out_shape],
        scratch_shapes=[pltpu.VMEM(acc_shape, F32)] if nk > 1 else [], operands=operands, name=name, riders=riders,
        sem=tuple(sem))
    return (out, rider_res) if riders else out


def mm_nn(a, w, *, out_dtype, name, tm=512, tn=512, riders=()):
    m, k = a.shape
    tm = _row_tile(m, tm)
    if w.ndim == 3:
        ns = w.shape[2]
        grid = (N_CHIPS, m // tm)
        w_spec = pl.BlockSpec((None, k, ns), lambda j, i: (j, 0, 0))
        o_spec = pl.BlockSpec((tm, ns), lambda j, i: (i, j))
        n = N_CHIPS * ns
    else:
        n = w.shape[1]
        grid = (n // tn, m // tm)
        w_spec = pl.BlockSpec((k, tn), lambda j, i: (0, j))
        o_spec = pl.BlockSpec((tm, tn), lambda j, i: (i, j))
    return _mm_call(grid=grid, in_specs=[pl.BlockSpec((tm, k), lambda j, i: (i, 0)), w_spec], out_spec=o_spec,
                    out_shape=jax.ShapeDtypeStruct((m, n), out_dtype), dims=NN_DIMS, nk=1, kaxis=0, acc_shape=None,
                    name=name, operands=(a, w), riders=riders)


def mm_nt(a, w, *, out_dtype, name, tm=512, tn=512, riders=()):
    if w.ndim == 2:
        m, n = a.shape
        kout = w.shape[0]
        tm = _row_tile(m, tm)
        return _mm_call(grid=(kout // tn, m // tm),
                        in_specs=[pl.BlockSpec((tm, n), lambda j, i: (i, 0)), pl.BlockSpec((tn, n), lambda j, i: (j, 0))],
                        out_spec=pl.BlockSpec((tm, tn), lambda j, i: (i, j)),
                        out_shape=jax.ShapeDtypeStruct((m, kout), out_dtype), dims=NT_DIMS, nk=1, kaxis=0,
                        acc_shape=None, name=name, operands=(a, w), riders=riders)
    _, kout, ns = w.shape
    planes = a.ndim == 3
    m = a.shape[1] if planes else a.shape[0]
    tm = _row_tile(m, tm)
    a_spec = pl.BlockSpec((2, tm, 2 * ns), lambda i: (0, i, 0)) if planes else pl.BlockSpec((tm, N_CHIPS * ns), lambda i: (i, 0))

    def body(a_ref, w0, w1, w2, w3, o_ref):
        acc = None
        for j, w_ref in enumerate((w0, w1, w2, w3)):
            if planes:
                a_j = a_ref[j // 2, :, (j % 2) * ns:(j % 2 + 1) * ns]
            else:
                a_j = a_ref[:, j * ns:(j + 1) * ns]
            p = lax.dot_general(a_j, w_ref[...], NT_DIMS, preferred_element_type=F32)
            acc = p if acc is None else acc + p
        o_ref[...] = acc.astype(out_dtype)

    def shard(j):
        return pl.BlockSpec((None, kout, ns), lambda i: (j, 0, 0))

    (out,), rider_res = _call(
        body, grid=(m // tm,), in_specs=[a_spec] + [shard(j) for j in range(N_CHIPS)],
        out_specs=[pl.BlockSpec((tm, kout), lambda i: (i, 0))], out_shape=[jax.ShapeDtypeStruct((m, kout), out_dtype)],
        operands=(a, w, w, w, w), sem=("parallel",), name=name, riders=riders)
    return (out, rider_res) if riders else out


def mm_tn(a, b, *, shard_major, name, tm, tn, tk=None, out_dtype=BF16, riders=()):
    s, m = a.shape
    tk = s if tk is None else _row_tile(s, tk)
    if b.ndim == 3:
        n = 2 * b.shape[2]
        b_spec = pl.BlockSpec((None, tk, tn), lambda j, i, kk: (j // 2, kk, j % 2))
    else:
        n = b.shape[1]
        b_spec = pl.BlockSpec((tk, tn), lambda j, i, kk: (kk, j))
    if shard_major:
        assert tn == n // N_CHIPS
        o_spec = pl.BlockSpec((None, tm, tn), lambda j, i, kk: (j, i, 0))
        o_shape = jax.ShapeDtypeStruct((N_CHIPS, m, tn), out_dtype)
    else:
        o_spec = pl.BlockSpec((tm, tn), lambda j, i, kk: (i, j))
        o_shape = jax.ShapeDtypeStruct((m, n), out_dtype)
    return _mm_call(grid=(n // tn, m // tm, s // tk),
                    in_specs=[pl.BlockSpec((tk, tm), lambda j, i, kk: (kk, i)), b_spec], out_spec=o_spec,
                    out_shape=o_shape, dims=TN_DIMS, nk=s // tk, kaxis=2, acc_shape=(tm, tn), name=name, operands=(a, b),
                    riders=riders)


def _rstd(x):
    return lax.rsqrt(jnp.mean(x * x, axis=-1, keepdims=True) + EPS)


def _rms_bwd(dy, x, g):
    r = _rstd(x)
    xhat = x * r
    gy = dy * g
    dx = r * (gy - xhat * jnp.mean(gy * xhat, axis=-1, keepdims=True))
    return dx, jnp.sum(dy * xhat, axis=0, keepdims=True)


def _accum(ref, val, first):
    @pl.when(first)
    def _():
        ref[...] = val

    @pl.when(jnp.logical_not(first))
    def _():
        ref[...] += val


def _row_spec(tm, width):
    return pl.BlockSpec((tm, width), lambda i: (i, 0))


def _vec_spec(width):
    return pl.BlockSpec((1, width), lambda i: (0, 0))


def _ret(core, rider_res, riders):
    core = core[0] if len(core) == 1 else core
    return (core, rider_res) if riders else core


def prenorm(x, g, *, name, tm=256, riders=()):
    s = x.shape[0]
    tm = _row_tile(s, tm)

    def body(x_ref, g_ref, h_ref):
        xv = x_ref[...]
        h_ref[...] = (xv * _rstd(xv) * g_ref[...]).astype(BF16)

    core, rr = _call(
        body, grid=(s // tm,), in_specs=[_row_spec(tm, D_MODEL), _vec_spec(D_MODEL)], out_specs=[_row_spec(tm, D_MODEL)],
        out_shape=[jax.ShapeDtypeStruct((s, D_MODEL), BF16)], operands=(x, g), sem=("parallel",), name=name, riders=riders)
    return _ret(core, rr, riders)


def proj_residual_norm(a, w, x, bias, g_post, g_next, *, name, tm=256, riders=()):
    s, k = a.shape
    tm = _row_tile(s, tm)

    def body(a_ref, w_ref, x_ref, b_ref, gp_ref, gn_ref, xo_ref, h_ref, m_ref):
        mv = jnp.dot(a_ref[...], w_ref[...], preferred_element_type=F32) + b_ref[...]
        m_ref[...] = mv.astype(BF16)
        xn = x_ref[...] + mv * _rstd(mv) * gp_ref[...]
        xo_ref[...] = xn
        h_ref[...] = (xn * _rstd(xn) * gn_ref[...]).astype(BF16)

    row, vec = _row_spec(tm, D_MODEL), _vec_spec(D_MODEL)
    core, rr = _call(
        body, grid=(s // tm,),
        in_specs=[_row_spec(tm, k), pl.BlockSpec((k, D_MODEL), lambda i: (0, 0)), row, vec, vec, vec], out_specs=[row, row, row],
        out_shape=[jax.ShapeDtypeStruct((s, D_MODEL), F32), jax.ShapeDtypeStruct((s, D_MODEL), BF16),
                   jax.ShapeDtypeStruct((s, D_MODEL), BF16)],
        operands=(a, w, x, bias, g_post, g_next), sem=("parallel",), name=name, riders=riders)
    return _ret(core, rr, riders)


def proj_loss_head(a, w, x, g_post, target, *, name, tm=256, riders=()):
    s, k = a.shape
    tm = _row_tile(s, tm)

    def body(a_ref, w_ref, x_ref, g_ref, t_ref, dx_ref, df_ref, dg_ref, loss_ref):
        first = pl.program_id(0) == 0
        fv = jnp.dot(a_ref[...], w_ref[...], preferred_element_type=F32)
        g = g_ref[...]
        err = x_ref[...] + fv * _rstd(fv) * g - t_ref[...]
        dx = err * (1.0 / D_MODEL)
        dx_ref[...] = dx
        df, dg = _rms_bwd(dx, fv, g)
        df_ref[...] = df.astype(BF16)
        _accum(dg_ref, dg, first)
        part = jnp.sum(jnp.sum(err * err, axis=-1, keepdims=True), axis=0, keepdims=True) * (0.5 / D_MODEL)
        _accum(loss_ref, jnp.broadcast_to(part, (8, LANES)), first)

    row, vec = _row_spec(tm, D_MODEL), _vec_spec(D_MODEL)
    core, rr = _call(
        body, grid=(s // tm,), in_specs=[_row_spec(tm, k), pl.BlockSpec((k, D_MODEL), lambda i: (0, 0)), row, vec, row],
        out_specs=[row, row, vec, pl.BlockSpec((8, LANES), lambda i: (0, 0))],
        out_shape=[jax.ShapeDtypeStruct((s, D_MODEL), F32), jax.ShapeDtypeStruct((s, D_MODEL), BF16),
                   jax.ShapeDtypeStruct((1, D_MODEL), F32), jax.ShapeDtypeStruct((8, LANES), F32)],
        operands=(a, w, x, g_post, target), name=name, riders=riders)
    return _ret(core, rr, riders)


def dh_norm_bwd_pair(a, w, dres, x, g_pre, m, g_post, *, name, tm=512, sub=256, riders=()):
    _, kout, ns = w.shape
    planes = a.ndim == 3
    s = x.shape[0]
    tm = _row_tile(s, tm)
    sub = min(sub, tm)
    a_spec = pl.BlockSpec((2, tm, 2 * ns), lambda i: (0, i, 0)) if planes else pl.BlockSpec((tm, N_CHIPS * ns), lambda i: (i, 0))

    def body(a_ref, w0, w1, w2, w3, dres_ref, x_ref, gpre_ref, m_ref, gpost_ref, dx_ref, dm_ref, dgpre_ref, dgpost_ref, db_ref):
        first = pl.program_id(0) == 0
        sums = None
        for t in range(tm // sub):
            rows = slice(t * sub, (t + 1) * sub)
            dh = None
            for j, w_ref in enumerate((w0, w1, w2, w3)):
                a_j = a_ref[j // 2, rows, (j % 2) * ns:(j % 2 + 1) * ns] if planes else a_ref[rows, j * ns:(j + 1) * ns]
                p = lax.dot_general(a_j, w_ref[...], NT_DIMS, preferred_element_type=F32)
                dh = p if dh is None else dh + p
            d1, dgpre = _rms_bwd(dh, x_ref[rows, :], gpre_ref[...])
            dx = dres_ref[rows, :] + d1
            dx_ref[rows, :] = dx
            dm, dgpost = _rms_bwd(dx, m_ref[rows, :].astype(F32), gpost_ref[...])
            dm_ref[rows, :] = dm.astype(BF16)
            part = (dgpre, dgpost, jnp.sum(dm, axis=0, keepdims=True))
            sums = part if sums is None else tuple(u + v for u, v in zip(sums, part))
        _accum(dgpre_ref, sums[0], first)
        _accum(dgpost_ref, sums[1], first)
        _accum(db_ref, sums[2], first)

    def shard(j):
        return pl.BlockSpec((None, kout, ns), lambda i: (j, 0, 0))

    row, vec = _row_spec(tm, D_MODEL), _vec_spec(D_MODEL)
    vshape = jax.ShapeDtypeStruct((1, D_MODEL), F32)
    core, rr = _call(
        body, grid=(s // tm,), in_specs=[a_spec] + [shard(j) for j in range(N_CHIPS)] + [row, row, vec, row, vec],
        out_specs=[row, row, vec, vec, vec],
        out_shape=[jax.ShapeDtypeStruct((s, D_MODEL), F32), jax.ShapeDtypeStruct((s, D_MODEL), BF16), vshape, vshape, vshape],
        operands=(a, w, w, w, w, dres, x, g_pre, m, g_post), name=name, riders=riders)
    return _ret(core, rr, riders)


def norm_bwd_last(dres, dh, x, g_pre, *, name, tm=256, riders=()):
    s = x.shape[0]
    tm = _row_tile(s, tm)

    def body(dres_ref, dh_ref, x_ref, g_ref, dx_ref, dg_ref):
        d1, dg = _rms_bwd(dh_ref[...], x_ref[...], g_ref[...])
        dx_ref[...] = dres_ref[...] + d1
        _accum(dg_ref, dg, pl.program_id(0) == 0)

    row, vec = _row_spec(tm, D_MODEL), _vec_spec(D_MODEL)
    core, rr = _call(
        body, grid=(s // tm,), in_specs=[row, row, row, vec], out_specs=[row, vec],
        out_shape=[jax.ShapeDtypeStruct((s, D_MODEL), F32), jax.ShapeDtypeStruct((1, D_MODEL), F32)],
        operands=(dres, dh, x, g_pre), name=name, riders=riders)
    return _ret(core, rr, riders)


def _rope_tables(s):
    half = HEAD_DIM // 2
    inv_freq = ROPE_THETA ** (-(jnp.arange(half, dtype=F32) * 2.0) / HEAD_DIM)
    ang = jnp.arange(s, dtype=I32).astype(F32)[:, None] * inv_freq[None, :]
    cos, sin = jnp.cos(ang), jnp.sin(ang)
    return jnp.tile(cos, (1, 4)), jnp.concatenate([-sin, sin, -sin, sin], axis=1)


def _swap_halves(x):
    lane = lax.broadcasted_iota(I32, x.shape, 1)
    return jnp.where((lane & (HEAD_DIM - 1)) < HEAD_DIM // 2, pltpu.roll(x, LANES - 32, 1), pltpu.roll(x, 32, 1))


N_ROPE_BLOCKS = (Q_WIDTH + KV_WIDTH) // LANES


def qkv_proj(h, w, bias, cos, sin, *, name, tm=512, riders=()):
    s, k = h.shape
    ns = w.shape[2]
    tm = _row_tile(s, tm)

    def body(h_ref, w_ref, b_ref, c_ref, s_ref, o_ref):
        j = pl.program_id(0)
        p = jnp.dot(h_ref[...], w_ref[...], preferred_element_type=F32) + b_ref[...]
        cosv, sinv = c_ref[...], s_ref[...]
        for blk in range(ns // LANES):
            xb = p[:, blk * LANES:(blk + 1) * LANES]
            roped = xb * cosv + _swap_halves(xb) * sinv
            is_qk = j * (ns // LANES) + blk < N_ROPE_BLOCKS
            o_ref[:, blk * LANES:(blk + 1) * LANES] = jnp.where(is_qk, roped, xb).astype(BF16)

    core, rr = _call(
        body, grid=(N_CHIPS, s // tm),
        in_specs=[pl.BlockSpec((tm, k), lambda j, i: (i, 0)), pl.BlockSpec((None, k, ns), lambda j, i: (j, 0, 0)),
                  pl.BlockSpec((1, ns), lambda j, i: (0, j)), pl.BlockSpec((tm, LANES), lambda j, i: (i, 0)),
                  pl.BlockSpec((tm, LANES), lambda j, i: (i, 0))],
        out_specs=[pl.BlockSpec((tm, ns), lambda j, i: (i, j))], out_shape=[jax.ShapeDtypeStruct((s, N_CHIPS * ns), BF16)],
        operands=(h, w, bias, cos, sin), sem=("parallel", "parallel"), name=name, riders=riders)
    return _ret(core, rr, riders)


def rope_bwd(dq, dkc, dkp, dvc, dvp, cos, sin, *, name, riders=()):
    s = dq.shape[0]
    tm = WINDOW
    nb = s // tm

    def body(dq_ref, dkc_ref, dkp_ref, dvc_ref, dvp_ref, c_ref, s_ref, o_ref, db_ref):
        i = pl.program_id(0)
        has_next = (i < nb - 1).astype(F32)
        cosv, sinv = c_ref[...], s_ref[...]
        for blk in range(QKV_WIDTH // LANES):
            if blk < Q_WIDTH // LANES:
                g = dq_ref[:, blk * LANES:(blk + 1) * LANES].astype(F32)
            else:
                own, nxt = (dkc_ref, dkp_ref) if blk < N_ROPE_BLOCKS else (dvc_ref, dvp_ref)
                cols = slice((blk % 2) * LANES, (blk % 2 + 1) * LANES)
                g = own[:, cols].astype(F32) + has_next * nxt[:, cols].astype(F32)
            if blk < N_ROPE_BLOCKS:
                g = g * cosv + _swap_halves(g * sinv)
            o_ref[:, blk * LANES:(blk + 1) * LANES] = g.astype(BF16)
            part = jnp.sum(g, axis=0, keepdims=True)

            @pl.when(i == 0)
            def _():
                db_ref[:, blk * LANES:(blk + 1) * LANES] = part

            @pl.when(i > 0)
            def _():
                db_ref[:, blk * LANES:(blk + 1) * LANES] += part

    own_spec = _row_spec(tm, KV_WIDTH)
    next_spec = pl.BlockSpec((tm, KV_WIDTH), lambda i: (jnp.minimum(i + 1, nb - 1), 0))
    core, rr = _call(
        body, grid=(nb,),
        in_specs=[_row_spec(tm, Q_WIDTH), own_spec, next_spec, own_spec, next_spec, _row_spec(tm, LANES), _row_spec(tm, LANES)],
        out_specs=[_row_spec(tm, QKV_WIDTH), _vec_spec(QKV_WIDTH)],
        out_shape=[jax.ShapeDtypeStruct((s, QKV_WIDTH), BF16), jax.ShapeDtypeStruct((1, QKV_WIDTH), F32)],
        operands=(dq, dkc, dkp, dvc, dvp, cos, sin), name=name, riders=riders)
    return _ret(core, rr, riders)


ROWS = GQA_GROUP * WINDOW


def _prev_slots():
    kpos = lax.broadcasted_iota(I32, (WINDOW, ROWS), 0)
    qpos = lax.broadcasted_iota(I32, (WINDOW, ROWS), 1) & (WINDOW - 1)
    return kpos > qpos


def _head_cols(ref, head):
    return ref[:, head * HEAD_DIM:(head + 1) * HEAD_DIM]


def _stack_heads(ref, h):
    return jnp.concatenate([_head_cols(ref, GQA_GROUP * h + g) for g in range(GQA_GROUP)], axis=0)


def _band(prev_ref, cur_ref, h):
    return jnp.concatenate([_head_cols(prev_ref, h), _head_cols(cur_ref, h)], axis=0)


def _pick(prev, band):
    return jnp.where(prev, band[:WINDOW], band[WINDOW:])


def _spread(prev, x):
    return jnp.concatenate([jnp.where(prev, x, 0.0), jnp.where(prev, 0.0, x)], axis=0).astype(BF16)


def _attn_probs(q, kband, sink, prev, has_prev):
    scale = HEAD_DIM ** -0.5
    s_band = lax.dot_general(kband, q, NT_DIMS, preferred_element_type=F32)
    s = jnp.where(prev, jnp.where(has_prev, s_band[:WINDOW], NEG), s_band[WINDOW:]) * scale
    m = jnp.maximum(jnp.max(s, axis=0, keepdims=True), sink)
    e, es = jnp.exp(s - m), jnp.exp(sink - m)
    inv = 1.0 / (jnp.sum(e, axis=0, keepdims=True) + es)
    return e * inv, es * inv


def _attn_specs(nb):
    kcol, vcol = Q_WIDTH // KV_WIDTH, Q_WIDTH // KV_WIDTH + 1
    q_spec = pl.BlockSpec((WINDOW, Q_WIDTH), lambda n: (n, 0))
    return [q_spec,
            pl.BlockSpec((WINDOW, KV_WIDTH), lambda n: (n, kcol)),
            pl.BlockSpec((WINDOW, KV_WIDTH), lambda n: (jnp.maximum(n - 1, 0), kcol)),
            pl.BlockSpec((WINDOW, KV_WIDTH), lambda n: (n, vcol)),
            pl.BlockSpec((WINDOW, KV_WIDTH), lambda n: (jnp.maximum(n - 1, 0), vcol)),
            pl.BlockSpec((N_KV_HEADS, 8, ROWS), lambda n: (0, 0, 0))]


def attn_fwd(qkv, sink_rows, *, name, riders=()):
    s = qkv.shape[0]

    def body(q_ref, kc_ref, kp_ref, vc_ref, vp_ref, sink_ref, o_ref):
        prev = _prev_slots()
        has_prev = pl.program_id(0) > 0
        for h in range(N_KV_HEADS):
            p, _ = _attn_probs(_stack_heads(q_ref, h), _band(kp_ref, kc_ref, h), sink_ref[h, 0:1, :], prev, has_prev)
            o = lax.dot_general(_band(vp_ref, vc_ref, h), _spread(prev, p), TN_DIMS, preferred_element_type=F32).T
            for g in range(GQA_GROUP):
                head = GQA_GROUP * h + g
                o_ref[:, head * HEAD_DIM:(head + 1) * HEAD_DIM] = o[g * WINDOW:(g + 1) * WINDOW].astype(BF16)

    core, rr = _call(
        body, grid=(s // WINDOW,), in_specs=_attn_specs(s // WINDOW), out_specs=[pl.BlockSpec((WINDOW, Q_WIDTH), lambda n: (n, 0))],
        out_shape=[jax.ShapeDtypeStruct((s, Q_WIDTH), BF16)], operands=(qkv, qkv, qkv, qkv, qkv, sink_rows), sem=("parallel",),
        name=name, riders=riders)
    return _ret(core, rr, riders)


def attn_bwd(qkv, sink_rows, do, *, name, riders=()):
    s = qkv.shape[0]

    def body(q_ref, kc_ref, kp_ref, vc_ref, vp_ref, sink_ref, do_ref, dq_ref, dkc_ref, dkp_ref, dvc_ref, dvp_ref, dsink_ref):
        n = pl.program_id(0)
        prev = _prev_slots()
        scale = HEAD_DIM ** -0.5
        parts = []
        for h in range(N_KV_HEADS):
            qv, dov = _stack_heads(q_ref, h), _stack_heads(do_ref, h)
            kband, vband = _band(kp_ref, kc_ref, h), _band(vp_ref, vc_ref, h)
            p, ps = _attn_probs(qv, kband, sink_ref[h, 0:1, :], prev, n > 0)
            dp = _pick(prev, lax.dot_general(vband, dov, NT_DIMS, preferred_element_type=F32))
            delta = jnp.sum(p * dp, axis=0, keepdims=True)
            ds_band = _spread(prev, p * (dp - delta) * scale)
            p_band = _spread(prev, p)
            dk = jnp.dot(ds_band, qv, preferred_element_type=F32).astype(BF16)
            dv = jnp.dot(p_band, dov, preferred_element_type=F32).astype(BF16)
            dq = lax.dot_general(kband, ds_band, TN_DIMS, preferred_element_type=F32).T
            cols = slice(h * HEAD_DIM, (h + 1) * HEAD_DIM)
            dkp_ref[:, cols], dkc_ref[:, cols] = dk[:WINDOW], dk[WINDOW:]
            dvp_ref[:, cols], dvc_ref[:, cols] = dv[:WINDOW], dv[WINDOW:]
            dsink = -(ps * delta)
            for g in range(GQA_GROUP):
                head = GQA_GROUP * h + g
                dq_ref[:, head * HEAD_DIM:(head + 1) * HEAD_DIM] = dq[g * WINDOW:(g + 1) * WINDOW].astype(BF16)
                parts.append(jnp.broadcast_to(jnp.sum(dsink[:, g * WINDOW:(g + 1) * WINDOW], axis=1, keepdims=True), (8, LANES)))

        @pl.when(n == 0)
        def _():
            for i, part in enumerate(parts):
                dsink_ref[i // GQA_GROUP, i % GQA_GROUP] = part

        @pl.when(n > 0)
        def _():
            for i, part in enumerate(parts):
                dsink_ref[i // GQA_GROUP, i % GQA_GROUP] += part

    rows_q = pl.BlockSpec((WINDOW, Q_WIDTH), lambda n: (n, 0))
    rows_kv = pl.BlockSpec((WINDOW, KV_WIDTH), lambda n: (n, 0))
    kv_shape = jax.ShapeDtypeStruct((s, KV_WIDTH), BF16)
    core, rr = _call(
        body, grid=(s // WINDOW,), in_specs=_attn_specs(s // WINDOW) + [rows_q],
        out_specs=[rows_q, rows_kv, rows_kv, rows_kv, rows_kv,
                   pl.BlockSpec((N_KV_HEADS, GQA_GROUP, 8, LANES), lambda n: (0, 0, 0, 0))],
        out_shape=[jax.ShapeDtypeStruct((s, Q_WIDTH), BF16), kv_shape, kv_shape, kv_shape, kv_shape,
                   jax.ShapeDtypeStruct((N_KV_HEADS, GQA_GROUP, 8, LANES), F32)],
        operands=(qkv, qkv, qkv, qkv, qkv, sink_rows, do), sem=("arbitrary",), name=name, riders=riders)
    return _ret(core, rr, riders)


GELU_C = 0.7978845608028654
GELU_A = 0.044715


def _gelu(x):
    return 0.5 * x * (1.0 + jnp.tanh(GELU_C * (x + GELU_A * x * x * x)))


def _gelu_grad(x):
    t = jnp.tanh(GELU_C * (x + GELU_A * x * x * x))
    return 0.5 * (1.0 + t) + 0.5 * x * (1.0 - t * t) * GELU_C * (1.0 + 3.0 * GELU_A * x * x)


def _tril_bf16(w):
    row = lax.broadcasted_iota(I32, (SGU_CHUNK, SGU_CHUNK), 0)
    col = lax.broadcasted_iota(I32, (SGU_CHUNK, SGU_CHUNK), 1)
    return jnp.where(row >= col, w, 0.0).astype(BF16)


def _sgu_norm(vg, g, b):
    mu = jnp.mean(vg, axis=-1, keepdims=True)
    cen = vg - mu
    rstd = lax.rsqrt(jnp.mean(cen * cen, axis=-1, keepdims=True) + EPS)
    xhat = cen * rstd
    return xhat, rstd, xhat * g + b


def sgu_in_fwd(h, w_in, ln_g, ln_b, w_sp, b_sp, *, name, tm=256, riders=()):
    s, k = h.shape
    ns = w_in.shape[2]
    tm = _row_tile(s, tm)

    def body(h_ref, w0, w1, w2, w3, g_ref, b_ref, w_ref, bs_ref, z_ref, y_ref):
        hv = h_ref[...]
        zs = [jnp.dot(hv, w_ref_j[...], preferred_element_type=F32) for w_ref_j in (w0, w1, w2, w3)]
        for j, zj in enumerate(zs):
            z_ref[:, j * ns:(j + 1) * ns] = zj.astype(BF16)
        u = _gelu(jnp.concatenate(zs[:2], axis=1))
        _, _, vn = _sgu_norm(_gelu(jnp.concatenate(zs[2:], axis=1)), g_ref[...], b_ref[...])
        vn = vn.astype(BF16)
        for grp in range(SGU_GROUPS):
            w = _tril_bf16(w_ref[grp])
            cols = slice(grp * LANES, (grp + 1) * LANES)
            for ch in range(tm // SGU_CHUNK):
                rows = slice(ch * SGU_CHUNK, (ch + 1) * SGU_CHUNK)
                mixed = jnp.dot(w, vn[rows, cols], preferred_element_type=F32) + bs_ref[grp]
                y_ref[rows, cols] = (u[rows, cols] * mixed).astype(BF16)

    def shard(j):
        return pl.BlockSpec((None, k, ns), lambda i: (j, 0, 0))

    full3 = pl.BlockSpec((SGU_GROUPS, SGU_CHUNK, SGU_CHUNK), lambda i: (0, 0, 0))
    core, rr = _call(
        body, grid=(s // tm,),
        in_specs=[_row_spec(tm, k)] + [shard(j) for j in range(N_CHIPS)] + [_vec_spec(D_MODEL), _vec_spec(D_MODEL), full3, full3],
        out_specs=[_row_spec(tm, 2 * D_MODEL), _row_spec(tm, D_MODEL)],
        out_shape=[jax.ShapeDtypeStruct((s, 2 * D_MODEL), BF16), jax.ShapeDtypeStruct((s, D_MODEL), BF16)],
        operands=(h, w_in, w_in, w_in, w_in, ln_g, ln_b, w_sp, b_sp), sem=("parallel",), name=name, riders=riders)
    return _ret(core, rr, riders)


def sgu_bwd(z, dy, ln_g, ln_b, w_sp, b_sp, *, name, tm=256, riders=()):
    s = z.shape[0]
    tm = _row_tile(s, tm)

    def body(z_ref, dy_ref, g_ref, b_ref, w_ref, bs_ref, dz_ref, dw_ref, dbs_ref, dg_ref, db_ref, dvn_buf):
        first = pl.program_id(0) == 0
        zu, zv = z_ref[:, :D_MODEL].astype(F32), z_ref[:, D_MODEL:].astype(F32)
        u = _gelu(zu)
        xhat, rstd, vn = _sgu_norm(_gelu(zv), g_ref[...], b_ref[...])
        vn = vn.astype(BF16)
        dyv = dy_ref[...]
        dmixed = dyv * u
        row = lax.broadcasted_iota(I32, (SGU_CHUNK, SGU_CHUNK), 0)
        col = lax.broadcasted_iota(I32, (SGU_CHUNK, SGU_CHUNK), 1)
        for grp in range(SGU_GROUPS):
            w = _tril_bf16(w_ref[grp])
            cols = slice(grp * LANES, (grp + 1) * LANES)
            dw = jnp.zeros((SGU_CHUNK, SGU_CHUNK), F32)
            dbs = jnp.zeros((SGU_CHUNK, 1), F32)
            for ch in range(tm // SGU_CHUNK):
                rows = slice(ch * SGU_CHUNK, (ch + 1) * SGU_CHUNK)
                vblk = vn[rows, cols]
                mixed = jnp.dot(w, vblk, preferred_element_type=F32) + bs_ref[grp]
                dz_ref[rows, cols] = (dyv[rows, cols] * mixed * _gelu_grad(zu[rows, cols])).astype(BF16)
                dm = dmixed[rows, cols]
                dmb = dm.astype(BF16)
                dvn_buf[rows, cols] = lax.dot_general(w, dmb, TN_DIMS, preferred_element_type=F32)
                dw += lax.dot_general(dmb, vblk, NT_DIMS, preferred_element_type=F32)
                dbs += jnp.sum(dm, axis=-1, keepdims=True)
            dw = jnp.where(row >= col, dw, 0.0)
            dbs = jnp.broadcast_to(dbs, (SGU_CHUNK, SGU_CHUNK))

            @pl.when(first)
            def _():
                dw_ref[grp] = dw
                dbs_ref[grp] = dbs

            @pl.when(jnp.logical_not(first))
            def _():
                dw_ref[grp] += dw
                dbs_ref[grp] += dbs

        dvn = dvn_buf[...]
        dxhat = dvn * g_ref[...]
        dvg = rstd * (dxhat - jnp.mean(dxhat, axis=-1, keepdims=True) - xhat * jnp.mean(dxhat * xhat, axis=-1, keepdims=True))
        dz_ref[:, D_MODEL:] = (dvg * _gelu_grad(zv)).astype(BF16)
        _accum(dg_ref, jnp.sum(dvn * xhat, axis=0, keepdims=True), first)
        _accum(db_ref, jnp.sum(dvn, axis=0, keepdims=True), first)

    full3 = pl.BlockSpec((SGU_GROUPS, SGU_CHUNK, SGU_CHUNK), lambda i: (0, 0, 0))
    s3 = jax.ShapeDtypeStruct((SGU_GROUPS, SGU_CHUNK, SGU_CHUNK), F32)
    vshape = jax.ShapeDtypeStruct((1, D_MODEL), F32)
    core, rr = _call(
        body, grid=(s // tm,),
        in_specs=[_row_spec(tm, 2 * D_MODEL), _row_spec(tm, D_MODEL), _vec_spec(D_MODEL), _vec_spec(D_MODEL), full3, full3],
        out_specs=[_row_spec(tm, 2 * D_MODEL), full3, full3, _vec_spec(D_MODEL), _vec_spec(D_MODEL)],
        out_shape=[jax.ShapeDtypeStruct((s, 2 * D_MODEL), BF16), s3, s3, vshape, vshape],
        scratch_shapes=[pltpu.VMEM((tm, D_MODEL), F32)], operands=(z, dy, ln_g, ln_b, w_sp, b_sp), name=name, riders=riders)
    return _ret(core, rr, riders)


def _sigmoid(x):
    return 1.0 / (1.0 + jnp.exp(-x))


def ffn_up(h, w_gu, *, name, tm=512, riders=()):
    s = h.shape[0]
    tm = _row_tile(s, tm)

    def body(h_ref, wg_ref, wu_ref, d_ref, a_ref):
        hv = h_ref[...]
        sub = min(256, tm)
        for t in range(tm // sub):
            rows = slice(t * sub, (t + 1) * sub)
            g = jnp.dot(hv[rows], wg_ref[...], preferred_element_type=F32)
            u = jnp.dot(hv[rows], wu_ref[...], preferred_element_type=F32)
            sig = _sigmoid(g)
            silu = g * sig
            d_ref[0, rows, :] = (u * (sig + silu * (1.0 - sig))).astype(BF16)
            d_ref[1, rows, :] = silu.astype(BF16)
            a_ref[rows, :] = (silu * u).astype(BF16)

    core, rr = _call(
        body, grid=(2, s // tm),
        in_specs=[pl.BlockSpec((tm, D_MODEL), lambda j, i: (i, 0)),
                  pl.BlockSpec((None, D_MODEL, FF_HALF), lambda j, i: (j, 0, 0)),
                  pl.BlockSpec((None, D_MODEL, FF_HALF), lambda j, i: (j + 2, 0, 0))],
        out_specs=[pl.BlockSpec((2, tm, FF_HALF), lambda j, i: (0, i, j)), pl.BlockSpec((tm, FF_HALF), lambda j, i: (i, j))],
        out_shape=[jax.ShapeDtypeStruct((2, s, D_FF), BF16), jax.ShapeDtypeStruct((s, D_FF), BF16)],
        operands=(h, w_gu, w_gu), sem=("parallel", "parallel"), name=name, riders=riders)
    return _ret(core, rr, riders)


def ffn_dact(df, w_d, gu, *, name, tm=512, riders=()):
    s = df.shape[0]
    tm = _row_tile(s, tm)

    def body(df_ref, w_ref, d_ref, o_ref):
        da = lax.dot_general(df_ref[...], w_ref[...], NT_DIMS, preferred_element_type=F32)
        o_ref[0] = (da * d_ref[0].astype(F32)).astype(BF16)
        o_ref[1] = (da * d_ref[1].astype(F32)).astype(BF16)

    planes = pl.BlockSpec((2, tm, FF_HALF), lambda j, i: (0, i, j))
    core, rr = _call(
        body, grid=(2, s // tm),
        in_specs=[pl.BlockSpec((tm, D_MODEL), lambda j, i: (i, 0)), pl.BlockSpec((FF_HALF, D_MODEL), lambda j, i: (j, 0)), planes],
        out_specs=[planes], out_shape=[jax.ShapeDtypeStruct((2, s, D_FF), BF16)], operands=(df, w_d, gu),
        sem=("parallel", "parallel"), name=name, riders=riders)
    return _ret(core, rr, riders)


def _weight_tile(rows):
    for tr in (512, 352, 256, 128):
        if rows % tr == 0:
            return tr
    return rows


def place_shard(w, layer, chip_arr, dtype, *, name, riders=()):
    _, r, c = w.shape
    tr = _weight_tile(r)

    def body(chip_ref, w_ref, o_ref):
        o_ref[...] = w_ref[...].astype(dtype)

    core, rr = _call(
        body, grid=(r // tr,), prefetch=(chip_arr,),
        in_specs=[pl.BlockSpec((None, tr, c), lambda i, chip: (layer, i, 0))],
        out_specs=[pl.BlockSpec((None, tr, c), lambda i, chip: (chip[0], i, 0))],
        out_shape=[jax.ShapeDtypeStruct((N_CHIPS, r, c), dtype)], operands=(w,), sem=("parallel",), name=name, riders=riders)
    return _ret(core, rr, riders)


def _adamw_math(w, g, m, v):
    m = ADAM_B1 * m + (1.0 - ADAM_B1) * g
    v = ADAM_B2 * v + (1.0 - ADAM_B2) * (g * g)
    m_hat = m / (1.0 - ADAM_B1 ** ADAM_STEP)
    v_hat = v / (1.0 - ADAM_B2 ** ADAM_STEP)
    delta = -ADAM_LR * (m_hat / (jnp.sqrt(v_hat) + ADAM_EPS) + ADAM_WD * w)
    return delta, m, v


def adamw(w, g, m, v, *, name):
    nl, r, c = w.shape
    tr = _weight_tile(r)

    def body(w_ref, g_ref, m_ref, v_ref, go_ref, d_ref, mo_ref, vo_ref):
        gv = g_ref[...]
        go_ref[...] = gv
        d_ref[...], mo_ref[...], vo_ref[...] = _adamw_math(w_ref[...], gv, m_ref[...], v_ref[...])

    spec = pl.BlockSpec((None, tr, c), lambda l, i: (l, i, 0))
    shape = jax.ShapeDtypeStruct(w.shape, F32)
    outs, _ = _call(body, grid=(nl, r // tr), in_specs=[spec] * 4, out_specs=[spec] * 4, out_shape=[shape] * 4,
                    operands=(w, g, m, v), sem=("parallel", "parallel"), name=name)
    return outs


def adamw_small(ws, gs, ms, vs, *, name):
    n = len(ws)

    def body(*refs):
        ins, outs = refs[:4 * n], refs[4 * n:]
        for t in range(n):
            gv = ins[n + t][...]
            outs[t][...] = gv
            outs[n + t][...], outs[2 * n + t][...], outs[3 * n + t][...] = _adamw_math(
                ins[t][...], gv, ins[2 * n + t][...], ins[3 * n + t][...])

    shapes = [jax.ShapeDtypeStruct(w.shape, F32) for w in ws]
    res = pl.pallas_call(body, out_shape=shapes * 4, name=name)(*ws, *gs, *ms, *vs)
    return res[:n], res[n:2 * n], res[2 * n:3 * n], res[3 * n:]


def pair_add(g, r1, c_arr, *, name):
    _, rows, cdim = g.shape
    h = rows // 2

    def body(c_ref, g_ref, r_ref, o_ref):
        o_ref[...] = (g_ref[...].astype(F32) + r_ref[...].astype(F32)).astype(o_ref.dtype)

    (out,), _ = _call(
        body, grid=(N_CHIPS,), prefetch=(c_arr,),
        in_specs=[pl.BlockSpec((None, h, cdim), lambda s, c: (s, c[0], 0)), pl.BlockSpec((None, h, cdim), lambda s, c: (s, 0, 0))],
        out_specs=[pl.BlockSpec((None, h, cdim), lambda s, c: (s, 0, 0))],
        out_shape=[jax.ShapeDtypeStruct((N_CHIPS, h, cdim), g.dtype)], operands=(g, r1), sem=("parallel",), name=name)
    return out


def final_add(g, r1, r2, jc_arr, *, dest_shape, lead, prev, name):
    _, rows, cdim = g.shape
    h = rows // 2

    def body(jc_ref, g_ref, r1_ref, r2_ref, *rest):
        o_ref = rest[-1]
        acc = g_ref[...].astype(F32) + r1_ref[...].astype(F32)
        for k in range(3):
            acc = acc + r2_ref[k].astype(F32)
        o_ref[...] = acc

    if lead is None:
        o_spec = pl.BlockSpec((h, cdim), lambda i, jc: (jc[1], 0))
    elif lead == "chip":
        o_spec = pl.BlockSpec((None, h, cdim), lambda i, jc: (jc[0], jc[1], 0))
    else:
        o_spec = pl.BlockSpec((None, h, cdim), lambda i, jc: (lead, jc[1], 0))
    in_specs = [pl.BlockSpec((None, h, cdim), lambda i, jc: (jc[0], jc[1], 0)),
                pl.BlockSpec((None, h, cdim), lambda i, jc: (jc[0], 0, 0)),
                pl.BlockSpec((3, h, cdim), lambda i, jc: (0, 0, 0))]
    operands = [g, r1, r2]
    aliases = None
    if prev is not None:
        in_specs.append(ANY)
        operands.append(prev)
        aliases = {3: 0}
    (out,), _ = _call(body, grid=(1,), prefetch=(jc_arr,), in_specs=in_specs, out_specs=[o_spec],
                      out_shape=[jax.ShapeDtypeStruct(dest_shape, F32)], operands=operands, aliases=aliases, name=name)
    return out


def _place():
    return lax.axis_index("x"), lax.axis_index("y"), lax.axis_index("c")


def _partner(x, y, k):
    return (1 - x if k >> 1 else x), (1 - y if k & 1 else y)


def _half(rows, sel, dtype):
    align = 16 if dtype == BF16 else 8
    return pl.ds(pl.multiple_of(sel * (rows // 2), align), rows // 2)


def _rider(peers, inputs, aliased, fresh, nsem, copies, arrivals):
    def start(ins, outs, send, recv):
        for cp in copies(ins, outs, send, recv):
            cp.start()

    def finish(ins, outs, send, recv):
        for cp in arrivals(ins, outs, send, recv):
            cp.wait_recv()
        for cp in copies(ins, outs, send, recv):
            cp.wait_send()

    return types.SimpleNamespace(peers=peers, inputs=list(inputs), aliased=list(aliased), fresh=list(fresh), nsem=nsem,
                                 start=start, finish=finish)


def _remote(src, dst, send, recv, idx, dev):
    return pltpu.make_async_remote_copy(src_ref=src, dst_ref=dst, send_sem=send.at[idx], recv_sem=recv.at[idx],
                                        device_id=dev, device_id_type=MESH)


def gather_ici_rider(fulls):
    nt = len(fulls)

    def region(outs, t, slot, sel):
        return outs[t].at[slot, _half(fulls[t].shape[1], sel, fulls[t].dtype)]

    def copies(ins, outs, send, recv):
        x, y, c = _place()
        res = []
        for t in range(nt):
            for k in (1, 2, 3):
                px, py = _partner(x, y, k)
                mine = region(outs, t, 2 * x + y, c)
                res.append(_remote(mine, mine, send, recv, 3 * t + k - 1, (px, py, c)))
        return res

    def arrivals(ins, outs, send, recv):
        x, y, c = _place()
        res = []
        for t in range(nt):
            for k in (1, 2, 3):
                px, py = _partner(x, y, k)
                theirs = region(outs, t, 2 * px + py, c)
                res.append(_remote(theirs, theirs, send, recv, 3 * t + k - 1, (x, y, c)))
        return res

    return _rider("chips", fulls, range(nt), [], 3 * nt, copies, arrivals)


def gather_d2d_rider(fulls):
    nt = len(fulls)

    def region(outs, t, slot, sel):
        return outs[t].at[slot, _half(fulls[t].shape[1], sel, fulls[t].dtype)]

    def both(outs, send, recv, mine):
        x, y, c = _place()
        res = []
        for t in range(nt):
            for k in (1, 2, 3):
                px, py = _partner(x, y, k)
                part = region(outs, t, 2 * px + py, c if mine else 1 - c)
                res.append(_remote(part, part, send, recv, 3 * t + k - 1, (x, y, 1 - c)))
        return res

    return _rider("sibling", fulls, range(nt), [], 3 * nt, lambda i, o, s, r: both(o, s, r, True),
                  lambda i, o, s, r: both(o, s, r, False))


def exchange_rider(grads):
    nt = len(grads)

    def both(ins, outs, send, recv):
        x, y, c = _place()
        return [_remote(ins[t].at[:, _half(grads[t].shape[1], 1 - c, grads[t].dtype)], outs[t], send, recv, t, (x, y, 1 - c))
                for t in range(nt)]

    fresh = [jax.ShapeDtypeStruct((N_CHIPS, g.shape[1] // 2, g.shape[2]), g.dtype) for g in grads]
    return _rider("sibling", grads, [], fresh, nt, both, both)


def scatter_rider(parts):
    nt = len(parts)

    def both(ins, outs, send, recv):
        x, y, c = _place()
        res = []
        for t in range(nt):
            for k in (1, 2, 3):
                px, py = _partner(x, y, k)
                res.append(_remote(ins[t].at[2 * px + py], outs[t].at[k - 1], send, recv, 3 * t + k - 1, (px, py, c)))
        return res

    fresh = [jax.ShapeDtypeStruct((3,) + p.shape[1:], p.dtype) for p in parts]
    return _rider("chips", parts, [], fresh, 3 * nt, both, both)


def broadcast_rider(bufs, items):
    def region(outs, item, sel):
        bi, lead = item
        ref = outs[bi]
        if lead == "chip":
            x, y, _ = _place()
            ref = ref.at[2 * x + y]
        elif lead is not None:
            ref = ref.at[lead]
        return ref.at[_half(ref.shape[0], sel, F32)]

    def both(outs, send, recv, mine):
        x, y, c = _place()
        res = []
        for i, item in enumerate(items):
            part = region(outs, item, c if mine else 1 - c)
            res.append(_remote(part, part, send, recv, i, (x, y, 1 - c)))
        return res

    return _rider("sibling", bufs, range(len(bufs)), [], len(items), lambda i, o, s, r: both(o, s, r, True),
                  lambda i, o, s, r: both(o, s, r, False))


def allcast_rider(buf):
    peers = [(k, flip) for k in range(N_CHIPS) for flip in (0, 1) if (k, flip) != (0, 0)]

    def both(outs, send, recv, mine):
        x, y, c = _place()
        res = []
        for i, (k, flip) in enumerate(peers):
            px, py = _partner(x, y, k)
            pc = 1 - c if flip else c
            slot, sel = (2 * x + y, c) if mine else (2 * px + py, pc)
            part = outs[0].at[slot, _half(buf.shape[1], sel, F32)]
            res.append(_remote(part, part, send, recv, i, (px, py, pc)))
        return res

    return _rider("everyone", [buf], [0], [], len(peers), lambda i, o, s, r: both(o, s, r, True),
                  lambda i, o, s, r: both(o, s, r, False))


def comm_call(riders, *, name):
    _, res = _call(None, riders=riders, name=name)
    return res


SLAB_ROWS = 192


def _pad_rows(a, rows=8):
    return jnp.pad(a, ((0, rows - a.shape[0]), (0, 0)))


def _pack_small(norm_grads, db_qkv, db_o, dsinks, db_sp, dln_g, dln_b, dw_sp):
    parts = [
        jnp.concatenate(norm_grads, axis=0),
        _pad_rows(jnp.pad(db_qkv, ((0, 0), (0, 2 * D_MODEL - QKV_WIDTH))).reshape(2, D_MODEL)),
        _pad_rows(db_o),
        _pad_rows(jnp.pad(dsinks.reshape(1, N_Q_HEADS), ((0, 0), (0, D_MODEL - N_Q_HEADS)))),
        _pad_rows(db_sp.reshape(1, D_MODEL)),
        _pad_rows(jnp.concatenate([dln_g, dln_b], axis=0)),
        dw_sp.reshape(SGU_CHUNK, D_MODEL),
    ]
    slab = jnp.concatenate(parts, axis=0)
    return jnp.pad(slab, ((0, SLAB_ROWS - slab.shape[0]), (0, 0))).reshape(N_CHIPS, SLAB_ROWS // N_CHIPS, D_MODEL)


def _unpack_small(slab, j):
    slab = slab.reshape(SLAB_ROWS, D_MODEL)
    norms = [slab[2 * i:2 * i + 2] for i in range(4)]
    db_qkv = slab[8:10].reshape(1, 2 * D_MODEL)[:, :QKV_WIDTH]
    db_o = slab[16:17]
    dsinks = slab[24:25, :N_Q_HEADS]
    db_sp = slab[32:33].reshape(SGU_GROUPS, SGU_CHUNK)
    width = D_MODEL // N_CHIPS
    dln_g = lax.dynamic_slice(slab[40:41], (0, j * width), (1, width))
    dln_b = lax.dynamic_slice(slab[41:42], (0, j * width), (1, width))
    dw_sp = slab[48:48 + SGU_CHUNK].reshape(SGU_GROUPS * SGU_CHUNK, SGU_CHUNK)
    return norms, db_qkv, db_o, dsinks, db_sp, dln_g, dln_b, dw_sp


class _GradReduce:
    def __init__(self, c_arr, jc_arr, dest_shapes):
        self.c_arr, self.jc_arr, self.dest_shapes = c_arr, jc_arr, dest_shapes
        self.grad, self.sibling, self.pair, self.chips, self.dest = {}, {}, {}, {}, {}

    def exchange(self, tags):
        return exchange_rider([self.grad[t] for t in tags])

    def exchanged(self, tags, res):
        for t, r in zip(tags, res):
            self.sibling[t] = r
            self.pair[t] = pair_add(self.grad[t], r, self.c_arr, name=f"pair_add_{t}")

    def scatter(self, tags):
        return scatter_rider([self.pair[t] for t in tags])

    def scattered(self, tags, res, where):
        for t, r in zip(tags, res):
            name, lead = where[t]
            self.dest[name] = final_add(self.grad[t], self.sibling[t], r, self.jc_arr, dest_shape=self.dest_shapes[name],
                                        lead=lead, prev=self.dest.get(name), name=f"final_add_{t}")

    def broadcast(self, items):
        names = []
        for n, _ in items:
            if n not in names:
                names.append(n)
        return names, broadcast_rider([self.dest[n] for n in names], [(names.index(n), lead) for n, lead in items])

    def broadcasted(self, names, res):
        for n, r in zip(names, res):
            self.dest[n] = r


def kernel(x, norm_mix_pre, norm_mix_post, norm_ffn_pre, norm_ffn_post, attn_w_qkv, attn_b_qkv, attn_sinks, attn_w_o, attn_b_o, sgu_w_in, sgu_ln_g, sgu_ln_b, sgu_w_spatial, sgu_b_spatial, sgu_w_out, ffn_w_gate_up, ffn_w_down, loss_target, m_norm_mix_pre, m_norm_mix_post, m_norm_ffn_pre, m_norm_ffn_post, m_attn_w_qkv, m_attn_b_qkv, m_attn_sinks, m_attn_w_o, m_attn_b_o, m_sgu_w_in, m_sgu_ln_g, m_sgu_ln_b, m_sgu_w_spatial, m_sgu_b_spatial, m_sgu_w_out, m_ffn_w_gate_up, m_ffn_w_down, v_norm_mix_pre, v_norm_mix_post, v_norm_ffn_pre, v_norm_ffn_post, v_attn_w_qkv, v_attn_b_qkv, v_attn_sinks, v_attn_w_o, v_attn_b_o, v_sgu_w_in, v_sgu_ln_g, v_sgu_ln_b, v_sgu_w_spatial, v_sgu_b_spatial, v_sgu_w_out, v_ffn_w_gate_up, v_ffn_w_down):
    s = x.shape[1]
    x0 = x.reshape(s, D_MODEL)
    target = loss_target.reshape(s, D_MODEL)
    mx, my, mc = lax.axis_index("x"), lax.axis_index("y"), lax.axis_index("c")
    chip = 2 * mx + my
    chip_arr = jnp.reshape(chip, (1,)).astype(I32)
    c_arr = jnp.reshape(mc, (1,)).astype(I32)
    jc_arr = jnp.stack([chip, mc]).astype(I32)
    zero_bias = jnp.zeros((1, D_MODEL), F32)

    def gain(p, i):
        return p[i:i + 1]

    big = [attn_w_qkv, attn_w_o, sgu_w_in, sgu_w_out, ffn_w_gate_up, ffn_w_gate_up, ffn_w_down, ffn_w_down]
    layers = [0, 0, 0, 0, 0, 1, 0, 1]
    tags = ["qkv", "wo", "win", "wout", "wgu0", "wgu1", "wd0", "wd1"]
    full = {t: place_shard(w, l, chip_arr, BF16, name=f"place_{t}") for w, l, t in zip(big, layers, tags) if t != "wgu1"}
    ln_pack = _pad_rows(jnp.concatenate([sgu_ln_g, sgu_ln_b], axis=0), 16)[None]
    full["ln"] = place_shard(ln_pack, 0, chip_arr, F32, name="place_ln")

    def ici(*names):
        return gather_ici_rider([full[n] for n in names])

    def d2d(*names):
        return gather_d2d_rider([full[n] for n in names])

    def landed(names, res):
        for n, r in zip(names, res):
            full[n] = r

    cos, sin = _rope_tables(s)
    sink_rows = jnp.broadcast_to(
        jnp.repeat(attn_sinks.reshape(N_KV_HEADS, GQA_GROUP), WINDOW, axis=1)[:, None, :], (N_KV_HEADS, 8, ROWS))
    w_sp = sgu_w_spatial.reshape(SGU_GROUPS, SGU_CHUNK, SGU_CHUNK)
    b_sp = jnp.broadcast_to(sgu_b_spatial.reshape(SGU_GROUPS, SGU_CHUNK)[:, :, None], (SGU_GROUPS, SGU_CHUNK, LANES))

    h0, (res,) = prenorm(x0, gain(norm_mix_pre, 0), name="prenorm_0", riders=[ici("qkv", "ln")])
    landed(("qkv", "ln"), res)
    full["wgu1"], (res,) = place_shard(ffn_w_gate_up, 1, chip_arr, BF16, name="place_wgu1", riders=[d2d("qkv", "ln")])
    landed(("qkv", "ln"), res)
    ln_g = full["ln"][:, 0, :].reshape(1, D_MODEL)
    ln_b = full["ln"][:, 1, :].reshape(1, D_MODEL)

    qkv, (res,) = qkv_proj(h0, full["qkv"], attn_b_qkv, cos, sin, name="qkv_proj", riders=[ici("wo", "wd0")])
    landed(("wo", "wd0"), res)
    o, (res_a, res_b) = attn_fwd(qkv, sink_rows, name="attn_fwd", riders=[d2d("wo", "wd0"), ici("wgu0")])
    landed(("wo", "wd0"), res_a)
    landed(("wgu0",), res_b)
    w_o = full["wo"].reshape(Q_WIDTH, D_MODEL)
    (x1, h1, m0), (res_a, res_b) = proj_residual_norm(
        o, w_o, x0, attn_b_o, gain(norm_mix_post, 0), gain(norm_ffn_pre, 0), name="attn_out_norm",
        riders=[d2d("wgu0"), ici("win", "wout")])
    landed(("wgu0",), res_a)
    landed(("win", "wout"), res_b)
    (gu0, a0), (res_a, res_b) = ffn_up(h1, full["wgu0"], name="ffn_up_0", riders=[d2d("win", "wout"), ici("wgu1")])
    landed(("win", "wout"), res_a)
    landed(("wgu1",), res_b)
    w_d0 = full["wd0"].reshape(D_FF, D_MODEL)
    (x2, h2, f0), (res_a, res_b) = proj_residual_norm(
        a0, w_d0, x1, zero_bias, gain(norm_ffn_post, 0), gain(norm_mix_pre, 1), name="ffn_down_norm_0",
        riders=[d2d("wgu1"), ici("wd1")])
    landed(("wgu1",), res_a)
    landed(("wd1",), res_b)
    (z, y), (res,) = sgu_in_fwd(h2, full["win"], ln_g, ln_b, w_sp, b_sp, name="sgu_in_fwd", riders=[d2d("wd1")])
    landed(("wd1",), res)
    w_qkv, w_in, w_gu0, w_gu1 = full["qkv"], full["win"], full["wgu0"], full["wgu1"]
    w_out = full["wout"].reshape(D_MODEL, D_MODEL)
    w_d1 = full["wd1"].reshape(D_FF, D_MODEL)
    x3, h3, m1 = proj_residual_norm(y, w_out, x2, zero_bias, gain(norm_mix_post, 1), gain(norm_ffn_pre, 1), name="sgu_out_norm")
    gu1, a1 = ffn_up(h3, w_gu1, name="ffn_up_1")
    dx4, df1, dg_fpost1, loss_part = proj_loss_head(a1, w_d1, x3, gain(norm_ffn_post, 1), target, name="ffn_down_loss")
    loss = lax.psum(loss_part[0, 0], ("x", "y", "c"))

    red = _GradReduce(c_arr, jc_arr, {
        "qkv": attn_w_qkv.shape[1:], "wo": attn_w_o.shape[1:], "win": sgu_w_in.shape[1:], "wout": sgu_w_out.shape[1:],
        "wgu": ffn_w_gate_up.shape, "wd": ffn_w_down.shape, "slab": (N_CHIPS, SLAB_ROWS // N_CHIPS, D_MODEL)})
    where = {"qkv": ("qkv", None), "wo": ("wo", None), "win": ("win", None), "wout": ("wout", None), "wgu0": ("wgu", 0),
             "wgu1": ("wgu", 1), "wd0": ("wd", 0), "wd1": ("wd", 1), "small": ("slab", "chip")}

    dgu1 = ffn_dact(df1, w_d1, gu1, name="ffn_dact_1")
    red.grad["wd1"] = mm_tn(a1, df1, shard_major=False, tm=256, tn=D_MODEL, name="dw_down_1").reshape(
        N_CHIPS, D_FF // N_CHIPS, D_MODEL)
    red.grad["wgu1"], (res,) = mm_tn(h3, dgu1, shard_major=True, tm=512, tn=FF_HALF, name="dw_gate_up_1",
                                     riders=[red.exchange(["wd1"])])
    red.exchanged(["wd1"], res)
    (dx3, dm1, dg_fpre1, dg_mpost1, _), (res_a, res_b) = dh_norm_bwd_pair(
        dgu1, w_gu1, dx4, x3, gain(norm_ffn_pre, 1), m1, gain(norm_mix_post, 1), name="dh_ffn_norm_1",
        riders=[red.exchange(["wgu1"]), red.scatter(["wd1"])])
    red.exchanged(["wgu1"], res_a)
    red.scattered(["wd1"], res_b, where)
    names, rider = red.broadcast([("wd", 1)])
    dy, (res,) = mm_nt(dm1, w_out, out_dtype=F32, name="dy_sgu", riders=[rider])
    red.broadcasted(names, res)
    red.grad["wout"] = mm_tn(y, dm1, shard_major=False, tm=512, tn=D_MODEL, name="dw_sgu_out").reshape(
        N_CHIPS, D_MODEL // N_CHIPS, D_MODEL)
    (dz, dw_sp, db_sp, dln_g, dln_b), (res_a, res_b) = sgu_bwd(
        z, dy, ln_g, ln_b, w_sp, b_sp, name="sgu_bwd", riders=[red.scatter(["wgu1"]), red.exchange(["wout"])])
    red.scattered(["wgu1"], res_a, where)
    red.exchanged(["wout"], res_b)
    names, rider = red.broadcast([("wgu", 1)])
    red.grad["win"], (res_a, res_b) = mm_tn(h2, dz, shard_major=True, tm=D_MODEL, tn=2 * D_MODEL // N_CHIPS, name="dw_sgu_in",
                                            riders=[rider, red.scatter(["wout"])])
    red.broadcasted(names, res_a)
    red.scattered(["wout"], res_b, where)
    names, rider = red.broadcast([("wout", None)])
    (dx2, df0, dg_mpre1, dg_fpost0, _), (res_a, res_b) = dh_norm_bwd_pair(
        dz, w_in, dx3, x2, gain(norm_mix_pre, 1), f0, gain(norm_ffn_post, 0), name="dh_sgu_norm",
        riders=[red.exchange(["win"]), rider])
    red.exchanged(["win"], res_a)
    red.broadcasted(names, res_b)
    dgu0, (res,) = ffn_dact(df0, w_d0, gu0, name="ffn_dact_0", riders=[red.scatter(["win"])])
    red.scattered(["win"], res, where)
    names, rider = red.broadcast([("win", None)])
    dw_d0, (res,) = mm_tn(a0, df0, shard_major=False, tm=256, tn=D_MODEL, name="dw_down_0", riders=[rider])
    red.broadcasted(names, res)
    red.grad["wd0"] = dw_d0.reshape(N_CHIPS, D_FF // N_CHIPS, D_MODEL)
    red.grad["wgu0"], (res,) = mm_tn(h1, dgu0, shard_major=True, tm=512, tn=FF_HALF, name="dw_gate_up_0",
                                     riders=[red.exchange(["wd0"])])
    red.exchanged(["wd0"], res)
    (dx1, dm0, dg_fpre0, dg_mpost0, db_o), (res_a, res_b) = dh_norm_bwd_pair(
        dgu0, w_gu0, dx2, x1, gain(norm_ffn_pre, 0), m0, gain(norm_mix_post, 0), name="dh_ffn_norm_0",
        riders=[red.exchange(["wgu0"]), red.scatter(["wd0"])])
    red.exchanged(["wgu0"], res_a)
    red.scattered(["wd0"], res_b, where)
    names, rider = red.broadcast([("wd", 0)])
    do, (res,) = mm_nt(dm0, w_o, out_dtype=BF16, name="do_attn", riders=[rider])
    red.broadcasted(names, res)
    red.grad["wo"] = mm_tn(o, dm0, shard_major=False, tm=512, tn=D_MODEL, name="dw_attn_out").reshape(
        N_CHIPS, Q_WIDTH // N_CHIPS, D_MODEL)
    (dq, dkc, dkp, dvc, dvp, dsink), (res_a, res_b) = attn_bwd(
        qkv, sink_rows, do, name="attn_bwd", riders=[red.scatter(["wgu0"]), red.exchange(["wo"])])
    red.scattered(["wgu0"], res_a, where)
    red.exchanged(["wo"], res_b)
    names, rider = red.broadcast([("wgu", 0)])
    (dqkv, db_qkv), (res_a, res_b) = rope_bwd(dq, dkc, dkp, dvc, dvp, cos, sin, name="rope_bwd",
                                              riders=[rider, red.scatter(["wo"])])
    red.broadcasted(names, res_a)
    red.scattered(["wo"], res_b, where)
    names, rider = red.broadcast([("wo", None)])
    red.grad["qkv"], (res,) = mm_tn(h0, dqkv, shard_major=True, tm=D_MODEL, tn=QKV_WIDTH // N_CHIPS, name="dw_qkv",
                                    riders=[rider])
    red.broadcasted(names, res)
    dh0, (res,) = mm_nt(dqkv, w_qkv, out_dtype=F32, tm=1024, name="dh_attn", riders=[red.exchange(["qkv"])])
    red.exchanged(["qkv"], res)
    grad_x, dg_mpre0 = norm_bwd_last(dx1, dh0, x0, gain(norm_mix_pre, 0), name="norm_bwd_in")

    norm_grads = [jnp.concatenate(p, axis=0) for p in
                  ((dg_mpre0, dg_mpre1), (dg_mpost0, dg_mpost1), (dg_fpre0, dg_fpre1), (dg_fpost0, dg_fpost1))]
    red.grad["small"] = _pack_small(norm_grads, db_qkv, db_o, dsink[:, :, 0, 0], db_sp[:, :, 0], dln_g, dln_b, dw_sp)
    res_a, res_b = comm_call([red.scatter(["qkv"]), red.exchange(["small"])], name="tail_1")
    red.scattered(["qkv"], res_a, where)
    red.exchanged(["small"], res_b)
    names, rider = red.broadcast([("qkv", None)])
    res_a, res_b = comm_call([red.scatter(["small"]), rider], name="tail_2")
    red.scattered(["small"], res_a, where)
    red.broadcasted(names, res_b)
    ((slab_full,),) = comm_call([allcast_rider(red.dest["slab"])], name="tail_3")
    g_qkv, g_wo, g_win, g_wout, g_wgu, g_wd = (red.dest[n] for n in ("qkv", "wo", "win", "wout", "wgu", "wd"))
    g_norms, g_bqkv, g_bo, g_sinks, g_bsp, g_lng, g_lnb, g_wsp = _unpack_small(slab_full, chip)

    def big_update(w, g, m, v, tag):
        return adamw(w, g.reshape(w.shape), m, v, name=f"adamw_{tag}")

    upd = {
        "attn_w_qkv": big_update(attn_w_qkv, g_qkv, m_attn_w_qkv, v_attn_w_qkv, "qkv"),
        "attn_w_o": big_update(attn_w_o, g_wo, m_attn_w_o, v_attn_w_o, "wo"),
        "sgu_w_in": big_update(sgu_w_in, g_win, m_sgu_w_in, v_sgu_w_in, "win"),
        "sgu_w_out": big_update(sgu_w_out, g_wout, m_sgu_w_out, v_sgu_w_out, "wout"),
        "ffn_w_gate_up": big_update(ffn_w_gate_up, g_wgu, m_ffn_w_gate_up, v_ffn_w_gate_up, "wgu"),
        "ffn_w_down": big_update(ffn_w_down, g_wd, m_ffn_w_down, v_ffn_w_down, "wd"),
    }
    small_names = ["norm_mix_pre", "norm_mix_post", "norm_ffn_pre", "norm_ffn_post", "attn_b_qkv", "attn_sinks", "attn_b_o",
                   "sgu_ln_g", "sgu_ln_b", "sgu_w_spatial", "sgu_b_spatial"]
    small_w = [norm_mix_pre, norm_mix_post, norm_ffn_pre, norm_ffn_post, attn_b_qkv, attn_sinks, attn_b_o, sgu_ln_g, sgu_ln_b,
               sgu_w_spatial, sgu_b_spatial]
    small_m = [m_norm_mix_pre, m_norm_mix_post, m_norm_ffn_pre, m_norm_ffn_post, m_attn_b_qkv, m_attn_sinks, m_attn_b_o,
               m_sgu_ln_g, m_sgu_ln_b, m_sgu_w_spatial, m_sgu_b_spatial]
    small_v = [v_norm_mix_pre, v_norm_mix_post, v_norm_ffn_pre, v_norm_ffn_post, v_attn_b_qkv, v_attn_sinks, v_attn_b_o,
               v_sgu_ln_g, v_sgu_ln_b, v_sgu_w_spatial, v_sgu_b_spatial]
    small_g = g_norms + [g_bqkv, g_sinks, g_bo, g_lng, g_lnb, g_wsp, g_bsp]

    def flat2(a):
        return a.reshape(-1, a.shape[-1])

    res = adamw_small([flat2(a) for a in small_w], [flat2(a) for a in small_g], [flat2(a) for a in small_m],
                      [flat2(a) for a in small_v], name="adamw_small")
    for i, nm in enumerate(small_names):
        upd[nm] = tuple(r[i].reshape(small_w[i].shape) for r in res)

    order = ["norm_mix_pre", "norm_mix_post", "norm_ffn_pre", "norm_ffn_post", "attn_w_qkv", "attn_b_qkv", "attn_sinks",
             "attn_w_o", "attn_b_o", "sgu_w_in", "sgu_ln_g", "sgu_ln_b", "sgu_w_spatial", "sgu_b_spatial", "sgu_w_out",
             "ffn_w_gate_up", "ffn_w_down"]
    outs = [loss, grad_x.reshape(1, s, D_MODEL)]
    for part in range(4):
        outs += [upd[nm][part] for nm in order]
    return tuple(outs)
```

```python
import types

import numpy as np
import jax
import jax.numpy as jnp
from jax import lax
from jax.experimental import pallas as pl
from jax.experimental.pallas import tpu as pltpu

F32 = jnp.float32
BF16 = jnp.bfloat16
I32 = jnp.int32

D_MODEL = 1024
HEAD_DIM = 64
N_Q_HEADS = 16
N_KV_HEADS = 4
GQA_GROUP = 4
WINDOW = 128
Q_WIDTH = 1024
KV_WIDTH = 256
QKV_WIDTH = 1536
ROPE_THETA = 10000.0
SGU_GROUPS = 8
SGU_CHUNK = 128
D_FF = 2816
FF_HALF = D_FF // 2
EPS = 1e-6
N_CHIPS = 4
LANES = 128

ADAM_LR = 0.001
ADAM_B1 = 0.9
ADAM_B2 = 0.999
ADAM_EPS = 1e-08
ADAM_WD = 0.01
ADAM_STEP = 10

VMEM_LIMIT = 52 * 1024 * 1024
MESH = pl.DeviceIdType.MESH
NEG = -1e30
NT_DIMS = (((1,), (1,)), ((), ()))
TN_DIMS = (((0,), (0,)), ((), ()))
NN_DIMS = (((1,), (0,)), ((), ()))
ANY = pl.BlockSpec(memory_space=pl.ANY)


def _row_tile(s, want):
    return want if s % want == 0 else s


PEER_KINDS = ("sibling", "chips", "sibling+chips", "everyone")


def _peer_kind(riders):
    kinds = {r.peers for r in riders}
    if not kinds:
        return None
    if "everyone" in kinds:
        return "everyone"
    return "sibling+chips" if len(kinds) == 2 else kinds.pop()


def _peer_barrier(kind):
    x, y, c = _place()
    chips = [(*_partner(x, y, k), c) for k in (1, 2, 3)]
    peers = {"sibling": [(x, y, 1 - c)], "chips": chips, "sibling+chips": [(x, y, 1 - c)] + chips,
             "everyone": [(x, y, 1 - c)] + chips + [(px, py, 1 - c) for px, py, _ in chips]}[kind]
    barrier = pltpu.get_barrier_semaphore()
    for dev in peers:
        pl.semaphore_signal(barrier, inc=1, device_id=dev, device_id_type=MESH)
    pl.semaphore_wait(barrier, len(peers))


def _call(body, *, name, grid=(), in_specs=(), out_specs=(), out_shape=(), scratch_shapes=(), operands=(), prefetch=(),
          aliases=None, riders=(), sem=None):
    n_pre, n_in, n_out, n_scr = len(prefetch), len(operands), len(out_shape), len(scratch_shapes)
    in_specs, out_specs, out_shape = list(in_specs), list(out_specs), list(out_shape)
    operands, scratch_shapes = list(operands), list(scratch_shapes)
    io_alias = {n_pre + i: o for i, o in (aliases or {}).items()}
    for r in riders:
        base_in, base_out = n_pre + len(operands), len(out_shape)
        operands += list(r.inputs)
        in_specs += [ANY] * len(r.inputs)
        for pos, i in enumerate(r.aliased):
            io_alias[base_in + i] = base_out + pos
            out_shape.append(jax.ShapeDtypeStruct(r.inputs[i].shape, r.inputs[i].dtype))
        out_shape += list(r.fresh)
        out_specs += [ANY] * (len(r.aliased) + len(r.fresh))
        scratch_shapes += [pltpu.SemaphoreType.DMA((r.nsem,)), pltpu.SemaphoreType.DMA((r.nsem,))]

    def wrapped(*refs):
        pre, p = refs[:n_pre], n_pre
        core_in, p = refs[p:p + n_in], p + n_in
        r_in = []
        for r in riders:
            r_in.append(refs[p:p + len(r.inputs)])
            p += len(r.inputs)
        core_out, p = refs[p:p + n_out], p + n_out
        r_out = []
        for r in riders:
            k = len(r.aliased) + len(r.fresh)
            r_out.append(refs[p:p + k])
            p += k
        core_scr, p = refs[p:p + n_scr], p + n_scr
        r_sem = [refs[p + 2 * i:p + 2 * i + 2] for i in range(len(riders))]

        def edge(at_last, fns):
            def run():
                if not at_last:
                    _peer_barrier(peer_kind)
                for i, r in enumerate(riders):
                    getattr(r, fns)(r_in[i], r_out[i], r_sem[i][0], r_sem[i][1])
            if not riders:
                return
            if not grid:
                run()
                return
            cond = None
            for d, n in enumerate(grid):
                c = pl.program_id(d) == (n - 1 if at_last else 0)
                cond = c if cond is None else jnp.logical_and(cond, c)
            pl.when(cond)(run)

        edge(False, "start")
        if body is not None:
            body(*pre, *core_in, *core_out, *core_scr)
        edge(True, "finish")

    if sem is None or riders:
        sem = ("arbitrary",) * len(grid)
    kwargs = dict(out_shape=out_shape, input_output_aliases=io_alias, name=name)
    peer_kind = _peer_kind(riders)
    collective = {} if peer_kind is None else {"collective_id": PEER_KINDS.index(peer_kind)}
    if grid:
        kwargs["compiler_params"] = pltpu.CompilerParams(dimension_semantics=sem, vmem_limit_bytes=VMEM_LIMIT, **collective)
    elif collective:
        kwargs["compiler_params"] = pltpu.CompilerParams(**collective)
    if n_pre:
        kwargs["grid_spec"] = pltpu.PrefetchScalarGridSpec(
            num_scalar_prefetch=n_pre, grid=grid, in_specs=in_specs, out_specs=out_specs, scratch_shapes=scratch_shapes)
    else:
        kwargs.update(grid=grid, in_specs=in_specs, out_specs=out_specs, scratch_shapes=scratch_shapes)
    res = pl.pallas_call(wrapped, **kwargs)(*prefetch, *operands)
    core, rest, rider_res = list(res[:n_out]), list(res[n_out:]), []
    for r in riders:
        k = len(r.aliased) + len(r.fresh)
        rider_res.append(rest[:k])
        rest = rest[k:]
    return core, rider_res


def _mm_call(*, grid, in_specs, out_spec, out_shape, dims, nk, kaxis, acc_shape, name, operands, riders=()):
    out_dtype = out_shape.dtype

    def body(a_ref, b_ref, o_ref, *scratch):
        p = lax.dot_general(a_ref[...].astype(BF16), b_ref[...].astype(BF16), dims, preferred_element_type=F32)
        if nk == 1:
            o_ref[...] = p.astype(out_dtype)
        else:
            acc = scratch[0]
            kk = pl.program_id(kaxis)

            @pl.when(kk == 0)
            def _():
                acc[...] = p

            @pl.when(kk > 0)
            def _():
                acc[...] += p

            @pl.when(kk == nk - 1)
            def _():
                o_ref[...] = acc[...].astype(out_dtype)

    sem = ["parallel"] * len(grid)
    if nk > 1:
        sem[kaxis] = "arbitrary"
    (out,), rider_res = _call(
        body, grid=grid, in_specs=in_specs, out_specs=[out_spec], out_shape=[out_shape],
        scratch_shapes=[pltpu.VMEM(acc_shape, F32)] if nk > 1 else [], operands=operands, name=name, riders=riders,
        sem=tuple(sem))
    return (out, rider_res) if riders else out


def mm_nn(a, w, *, out_dtype, name, tm=512, tn=512, riders=()):
    m, k = a.shape
    tm = _row_tile(m, tm)
    if w.ndim == 3:
        ns = w.shape[2]
        grid = (N_CHIPS, m // tm)
        w_spec = pl.BlockSpec((None, k, ns), lambda j, i: (j, 0, 0))
        o_spec = pl.BlockSpec((tm, ns), lambda j, i: (i, j))
        n = N_CHIPS * ns
    else:
        n = w.shape[1]
        grid = (n // tn, m // tm)
        w_spec = pl.BlockSpec((k, tn), lambda j, i: (0, j))
        o_spec = pl.BlockSpec((tm, tn), lambda j, i: (i, j))
    return _mm_call(grid=grid, in_specs=[pl.BlockSpec((tm, k), lambda j, i: (i, 0)), w_spec], out_spec=o_spec,
                    out_shape=jax.ShapeDtypeStruct((m, n), out_dtype), dims=NN_DIMS, nk=1, kaxis=0, acc_shape=None,
                    name=name, operands=(a, w), riders=riders)


def mm_nt(a, w, *, out_dtype, name, tm=512, tn=512, riders=()):
    if w.ndim == 2:
        m, n = a.shape
        kout = w.shape[0]
        tm = _row_tile(m, tm)
        return _mm_call(grid=(kout // tn, m // tm),
                        in_specs=[pl.BlockSpec((tm, n), lambda j, i: (i, 0)), pl.BlockSpec((tn, n), lambda j, i: (j, 0))],
                        out_spec=pl.BlockSpec((tm, tn), lambda j, i: (i, j)),
                        out_shape=jax.ShapeDtypeStruct((m, kout), out_dtype), dims=NT_DIMS, nk=1, kaxis=0,
                        acc_shape=None, name=name, operands=(a, w), riders=riders)
    _, kout, ns = w.shape
    planes = a.ndim == 3
    m = a.shape[1] if planes else a.shape[0]
    tm = _row_tile(m, tm)
    a_spec = pl.BlockSpec((2, tm, 2 * ns), lambda i: (0, i, 0)) if planes else pl.BlockSpec((tm, N_CHIPS * ns), lambda i: (i, 0))

    def body(a_ref, w0, w1, w2, w3, o_ref):
        acc = None
        for j, w_ref in enumerate((w0, w1, w2, w3)):
            if planes:
                a_j = a_ref[j // 2, :, (j % 2) * ns:(j % 2 + 1) * ns]
            else:
                a_j = a_ref[:, j * ns:(j + 1) * ns]
            p = lax.dot_general(a_j, w_ref[...], NT_DIMS, preferred_element_type=F32)
            acc = p if acc is None else acc + p
        o_ref[...] = acc.astype(out_dtype)

    def shard(j):
        return pl.BlockSpec((None, kout, ns), lambda i: (j, 0, 0))

    (out,), rider_res = _call(
        body, grid=(m // tm,), in_specs=[a_spec] + [shard(j) for j in range(N_CHIPS)],
        out_specs=[pl.BlockSpec((tm, kout), lambda i: (i, 0))], out_shape=[jax.ShapeDtypeStruct((m, kout), out_dtype)],
        operands=(a, w, w, w, w), sem=("parallel",), name=name, riders=riders)
    return (out, rider_res) if riders else out


def mm_tn(a, b, *, shard_major, name, tm, tn, tk=None, out_dtype=BF16, riders=()):
    s, m = a.shape
    tk = s if tk is None else _row_tile(s, tk)
    if b.ndim == 3:
        n = 2 * b.shape[2]
        b_spec = pl.BlockSpec((None, tk, tn), lambda j, i, kk: (j // 2, kk, j % 2))
    else:
        n = b.shape[1]
        b_spec = pl.BlockSpec((tk, tn), lambda j, i, kk: (kk, j))
    if shard_major:
        assert tn == n // N_CHIPS
        o_spec = pl.BlockSpec((None, tm, tn), lambda j, i, kk: (j, i, 0))
        o_shape = jax.ShapeDtypeStruct((N_CHIPS, m, tn), out_dtype)
    else:
        o_spec = pl.BlockSpec((tm, tn), lambda j, i, kk: (i, j))
        o_shape = jax.ShapeDtypeStruct((m, n), out_dtype)
    return _mm_call(grid=(n // tn, m // tm, s // tk),
                    in_specs=[pl.BlockSpec((tk, tm), lambda j, i, kk: (kk, i)), b_spec], out_spec=o_spec,
                    out_shape=o_shape, dims=TN_DIMS, nk=s // tk, kaxis=2, acc_shape=(tm, tn), name=name, operands=(a, b),
                    riders=riders)


def _rstd(x):
    return lax.rsqrt(jnp.mean(x * x, axis=-1, keepdims=True) + EPS)


def _rms_bwd(dy, x, g):
    r = _rstd(x)
    xhat = x * r
    gy = dy * g
    dx = r * (gy - xhat * jnp.mean(gy * xhat, axis=-1, keepdims=True))
    return dx, jnp.sum(dy * xhat, axis=0, keepdims=True)


def _accum(ref, val, first):
    @pl.when(first)
    def _():
        ref[...] = val

    @pl.when(jnp.logical_not(first))
    def _():
        ref[...] += val


def _row_spec(tm, width):
    return pl.BlockSpec((tm, width), lambda i: (i, 0))


def _vec_spec(width):
    return pl.BlockSpec((1, width), lambda i: (0, 0))


def _ret(core, rider_res, riders):
    core = core[0] if len(core) == 1 else core
    return (core, rider_res) if riders else core


def prenorm(x, g, *, name, tm=256, riders=()):
    s = x.shape[0]
    tm = _row_tile(s, tm)

    def body(x_ref, g_ref, h_ref):
        xv = x_ref[...]
        h_ref[...] = (xv * _rstd(xv) * g_ref[...]).astype(BF16)

    core, rr = _call(
        body, grid=(s // tm,), in_specs=[_row_spec(tm, D_MODEL), _vec_spec(D_MODEL)], out_specs=[_row_spec(tm, D_MODEL)],
        out_shape=[jax.ShapeDtypeStruct((s, D_MODEL), BF16)], operands=(x, g), sem=("parallel",), name=name, riders=riders)
    return _ret(core, rr, riders)


def proj_residual_norm(a, w, x, bias, g_post, g_next, *, name, tm=256, riders=()):
    s, k = a.shape
    tm = _row_tile(s, tm)

    def body(a_ref, w_ref, x_ref, b_ref, gp_ref, gn_ref, xo_ref, h_ref, m_ref):
        mv = jnp.dot(a_ref[...], w_ref[...], preferred_element_type=F32) + b_ref[...]
        m_ref[...] = mv.astype(BF16)
        xn = x_ref[...] + mv * _rstd(mv) * gp_ref[...]
        xo_ref[...] = xn
        h_ref[...] = (xn * _rstd(xn) * gn_ref[...]).astype(BF16)

    row, vec = _row_spec(tm, D_MODEL), _vec_spec(D_MODEL)
    core, rr = _call(
        body, grid=(s // tm,),
        in_specs=[_row_spec(tm, k), pl.BlockSpec((k, D_MODEL), lambda i: (0, 0)), row, vec, vec, vec], out_specs=[row, row, row],
        out_shape=[jax.ShapeDtypeStruct((s, D_MODEL), F32), jax.ShapeDtypeStruct((s, D_MODEL), BF16),
                   jax.ShapeDtypeStruct((s, D_MODEL), BF16)],
        operands=(a, w, x, bias, g_post, g_next), sem=("parallel",), name=name, riders=riders)
    return _ret(core, rr, riders)


def proj_loss_head(a, w, x, g_post, target, *, name, tm=256, riders=()):
    s, k = a.shape
    tm = _row_tile(s, tm)

    def body(a_ref, w_ref, x_ref, g_ref, t_ref, dx_ref, df_ref, dg_ref, loss_ref):
        first = pl.program_id(0) == 0
        fv = jnp.dot(a_ref[...], w_ref[...], preferred_element_type=F32)
        g = g_ref[...]
        err = x_ref[...] + fv * _rstd(fv) * g - t_ref[...]
        dx = err * (1.0 / D_MODEL)
        dx_ref[...] = dx
        df, dg = _rms_bwd(dx, fv, g)
        df_ref[...] = df.astype(BF16)
        _accum(dg_ref, dg, first)
        part = jnp.sum(jnp.sum(err * err, axis=-1, keepdims=True), axis=0, keepdims=True) * (0.5 / D_MODEL)
        _accum(loss_ref, jnp.broadcast_to(part, (8, LANES)), first)

    row, vec = _row_spec(tm, D_MODEL), _vec_spec(D_MODEL)
    core, rr = _call(
        body, grid=(s // tm,), in_specs=[_row_spec(tm, k), pl.BlockSpec((k, D_MODEL), lambda i: (0, 0)), row, vec, row],
        out_specs=[row, row, vec, pl.BlockSpec((8, LANES), lambda i: (0, 0))],
        out_shape=[jax.ShapeDtypeStruct((s, D_MODEL), F32), jax.ShapeDtypeStruct((s, D_MODEL), BF16),
                   jax.ShapeDtypeStruct((1, D_MODEL), F32), jax.ShapeDtypeStruct((8, LANES), F32)],
        operands=(a, w, x, g_post, target), name=name, riders=riders)
    return _ret(core, rr, riders)


def dh_norm_bwd_pair(a, w, dres, x, g_pre, m, g_post, *, name, tm=512, sub=256, riders=()):
    _, kout, ns = w.shape
    planes = a.ndim == 3
    s = x.shape[0]
    tm = _row_tile(s, tm)
    sub = min(sub, tm)
    a_spec = pl.BlockSpec((2, tm, 2 * ns), lambda i: (0, i, 0)) if planes else pl.BlockSpec((tm, N_CHIPS * ns), lambda i: (i, 0))

    def body(a_ref, w0, w1, w2, w3, dres_ref, x_ref, gpre_ref, m_ref, gpost_ref, dx_ref, dm_ref, dgpre_ref, dgpost_ref, db_ref):
        first = pl.program_id(0) == 0
        sums = None
        for t in range(tm // sub):
            rows = slice(t * sub, (t + 1) * sub)
            dh = None
            for j, w_ref in enumerate((w0, w1, w2, w3)):
                a_j = a_ref[j // 2, rows, (j % 2) * ns:(j % 2 + 1) * ns] if planes else a_ref[rows, j * ns:(j + 1) * ns]
                p = lax.dot_general(a_j, w_ref[...], NT_DIMS, preferred_element_type=F32)
                dh = p if dh is None else dh + p
            d1, dgpre = _rms_bwd(dh, x_ref[rows, :], gpre_ref[...])
            dx = dres_ref[rows, :] + d1
            dx_ref[rows, :] = dx
            dm, dgpost = _rms_bwd(dx, m_ref[rows, :].astype(F32), gpost_ref[...])
            dm_ref[rows, :] = dm.astype(BF16)
            part = (dgpre, dgpost, jnp.sum(dm, axis=0, keepdims=True))
            sums = part if sums is None else tuple(u + v for u, v in zip(sums, part))
        _accum(dgpre_ref, sums[0], first)
        _accum(dgpost_ref, sums[1], first)
        _accum(db_ref, sums[2], first)

    def shard(j):
        return pl.BlockSpec((None, kout, ns), lambda i: (j, 0, 0))

    row, vec = _row_spec(tm, D_MODEL), _vec_spec(D_MODEL)
    vshape = jax.ShapeDtypeStruct((1, D_MODEL), F32)
    core, rr = _call(
        body, grid=(s // tm,), in_specs=[a_spec] + [shard(j) for j in range(N_CHIPS)] + [row, row, vec, row, vec],
        out_specs=[row, row, vec, vec, vec],
        out_shape=[jax.ShapeDtypeStruct((s, D_MODEL), F32), jax.ShapeDtypeStruct((s, D_MODEL), BF16), vshape, vshape, vshape],
        operands=(a, w, w, w, w, dres, x, g_pre, m, g_post), name=name, riders=riders)
    return _ret(core, rr, riders)


def norm_bwd_last(dres, dh, x, g_pre, *, name, tm=256, riders=()):
    s = x.shape[0]
    tm = _row_tile(s, tm)

    def body(dres_ref, dh_ref, x_ref, g_ref, dx_ref, dg_ref):
        d1, dg = _rms_bwd(dh_ref[...], x_ref[...], g_ref[...])
        dx_ref[...] = dres_ref[...] + d1
        _accum(dg_ref, dg, pl.program_id(0) == 0)

    row, vec = _row_spec(tm, D_MODEL), _vec_spec(D_MODEL)
    core, rr = _call(
        body, grid=(s // tm,), in_specs=[row, row, row, vec], out_specs=[row, vec],
        out_shape=[jax.ShapeDtypeStruct((s, D_MODEL), F32), jax.ShapeDtypeStruct((1, D_MODEL), F32)],
        operands=(dres, dh, x, g_pre), name=name, riders=riders)
    return _ret(core, rr, riders)


def _rope_tables(s):
    half = HEAD_DIM // 2
    inv_freq = np.float32(ROPE_THETA) ** (-(np.arange(half, dtype=np.float32) * np.float32(2.0)) / np.float32(HEAD_DIM))
    ang = np.arange(s, dtype=np.float32)[:, None] * inv_freq[None, :]
    cos, sin = np.cos(ang).astype(np.float32), np.sin(ang).astype(np.float32)
    return jnp.asarray(np.tile(cos, (1, 4))), jnp.asarray(np.concatenate([-sin, sin, -sin, sin], axis=1))


def _swap_halves(x):
    lane = lax.broadcasted_iota(I32, x.shape, 1)
    return jnp.where((lane & (HEAD_DIM - 1)) < HEAD_DIM // 2, pltpu.roll(x, LANES - 32, 1), pltpu.roll(x, 32, 1))


N_ROPE_BLOCKS = (Q_WIDTH + KV_WIDTH) // LANES


def qkv_proj(h, w, bias, cos, sin, *, name, tm=512, riders=()):
    s, k = h.shape
    ns = w.shape[2]
    tm = _row_tile(s, tm)

    def body(h_ref, w_ref, b_ref, c_ref, s_ref, o_ref):
        j = pl.program_id(0)
        p = jnp.dot(h_ref[...], w_ref[...], preferred_element_type=F32) + b_ref[...]
        cosv, sinv = c_ref[...], s_ref[...]
        for blk in range(ns // LANES):
            xb = p[:, blk * LANES:(blk + 1) * LANES]
            roped = xb * cosv + _swap_halves(xb) * sinv
            is_qk = j * (ns // LANES) + blk < N_ROPE_BLOCKS
            o_ref[:, blk * LANES:(blk + 1) * LANES] = jnp.where(is_qk, roped, xb).astype(BF16)

    core, rr = _call(
        body, grid=(N_CHIPS, s // tm),
        in_specs=[pl.BlockSpec((tm, k), lambda j, i: (i, 0)), pl.BlockSpec((None, k, ns), lambda j, i: (j, 0, 0)),
                  pl.BlockSpec((1, ns), lambda j, i: (0, j)), pl.BlockSpec((tm, LANES), lambda j, i: (i, 0)),
                  pl.BlockSpec((tm, LANES), lambda j, i: (i, 0))],
        out_specs=[pl.BlockSpec((tm, ns), lambda j, i: (i, j))], out_shape=[jax.ShapeDtypeStruct((s, N_CHIPS * ns), BF16)],
        operands=(h, w, bias, cos, sin), sem=("parallel", "parallel"), name=name, riders=riders)
    return _ret(core, rr, riders)


def rope_bwd(dq, dkc, dkp, dvc, dvp, cos, sin, *, name, riders=()):
    s = dq.shape[0]
    tm = WINDOW
    nb = s // tm

    def body(dq_ref, dkc_ref, dkp_ref, dvc_ref, dvp_ref, c_ref, s_ref, o_ref, db_ref):
        i = pl.program_id(0)
        has_next = (i < nb - 1).astype(F32)
        cosv, sinv = c_ref[...], s_ref[...]
        for blk in range(QKV_WIDTH // LANES):
            if blk < Q_WIDTH // LANES:
                g = dq_ref[:, blk * LANES:(blk + 1) * LANES].astype(F32)
            else:
                own, nxt = (dkc_ref, dkp_ref) if blk < N_ROPE_BLOCKS else (dvc_ref, dvp_ref)
                cols = slice((blk % 2) * LANES, (blk % 2 + 1) * LANES)
                g = own[:, cols].astype(F32) + has_next * nxt[:, cols].astype(F32)
            if blk < N_ROPE_BLOCKS:
                g = g * cosv + _swap_halves(g * sinv)
            o_ref[:, blk * LANES:(blk + 1) * LANES] = g.astype(BF16)
            part = jnp.sum(g, axis=0, keepdims=True)

            @pl.when(i == 0)
            def _():
                db_ref[:, blk * LANES:(blk + 1) * LANES] = part

            @pl.when(i > 0)
            def _():
                db_ref[:, blk * LANES:(blk + 1) * LANES] += part

    own_spec = _row_spec(tm, KV_WIDTH)
    next_spec = pl.BlockSpec((tm, KV_WIDTH), lambda i: (jnp.minimum(i + 1, nb - 1), 0))
    core, rr = _call(
        body, grid=(nb,),
        in_specs=[_row_spec(tm, Q_WIDTH), own_spec, next_spec, own_spec, next_spec, _row_spec(tm, LANES), _row_spec(tm, LANES)],
        out_specs=[_row_spec(tm, QKV_WIDTH), _vec_spec(QKV_WIDTH)],
        out_shape=[jax.ShapeDtypeStruct((s, QKV_WIDTH), BF16), jax.ShapeDtypeStruct((1, QKV_WIDTH), F32)],
        operands=(dq, dkc, dkp, dvc, dvp, cos, sin), name=name, riders=riders)
    return _ret(core, rr, riders)


ROWS = GQA_GROUP * WINDOW


def _prev_slots():
    kpos = lax.broadcasted_iota(I32, (WINDOW, ROWS), 0)
    qpos = lax.broadcasted_iota(I32, (WINDOW, ROWS), 1) & (WINDOW - 1)
    return kpos > qpos


def _head_cols(ref, head):
    return ref[:, head * HEAD_DIM:(head + 1) * HEAD_DIM]


def _stack_heads(ref, h):
    return jnp.concatenate([_head_cols(ref, GQA_GROUP * h + g) for g in range(GQA_GROUP)], axis=0)


def _band(prev_ref, cur_ref, h):
    return jnp.concatenate([_head_cols(prev_ref, h), _head_cols(cur_ref, h)], axis=0)


def _pick(prev, band):
    return jnp.where(prev, band[:WINDOW], band[WINDOW:])


def _spread(prev, x):
    return jnp.concatenate([jnp.where(prev, x, 0.0), jnp.where(prev, 0.0, x)], axis=0).astype(BF16)


def _attn_probs(q, kband, sink, prev, has_prev):
    scale = HEAD_DIM ** -0.5
    s_band = lax.dot_general(kband, q, NT_DIMS, preferred_element_type=F32)
    s = jnp.where(prev, jnp.where(has_prev, s_band[:WINDOW], NEG), s_band[WINDOW:]) * scale
    m = jnp.maximum(jnp.max(s, axis=0, keepdims=True), sink)
    e, es = jnp.exp(s - m), jnp.exp(sink - m)
    inv = 1.0 / (jnp.sum(e, axis=0, keepdims=True) + es)
    return e * inv, es * inv


def _attn_specs(nb):
    kcol, vcol = Q_WIDTH // KV_WIDTH, Q_WIDTH // KV_WIDTH + 1
    q_spec = pl.BlockSpec((WINDOW, Q_WIDTH), lambda n: (n, 0))
    return [q_spec,
            pl.BlockSpec((WINDOW, KV_WIDTH), lambda n: (n, kcol)),
            pl.BlockSpec((WINDOW, KV_WIDTH), lambda n: (jnp.maximum(n - 1, 0), kcol)),
            pl.BlockSpec((WINDOW, KV_WIDTH), lambda n: (n, vcol)),
            pl.BlockSpec((WINDOW, KV_WIDTH), lambda n: (jnp.maximum(n - 1, 0), vcol)),
            pl.BlockSpec((N_KV_HEADS, 8, ROWS), lambda n: (0, 0, 0))]


def attn_fwd(qkv, sink_rows, *, name, riders=()):
    s = qkv.shape[0]

    def body(q_ref, kc_ref, kp_ref, vc_ref, vp_ref, sink_ref, o_ref):
        prev = _prev_slots()
        has_prev = pl.program_id(0) > 0
        for h in range(N_KV_HEADS):
            p, _ = _attn_probs(_stack_heads(q_ref, h), _band(kp_ref, kc_ref, h), sink_ref[h, 0:1, :], prev, has_prev)
            o = lax.dot_general(_band(vp_ref, vc_ref, h), _spread(prev, p), TN_DIMS, preferred_element_type=F32).T
            for g in range(GQA_GROUP):
                head = GQA_GROUP * h + g
                o_ref[:, head * HEAD_DIM:(head + 1) * HEAD_DIM] = o[g * WINDOW:(g + 1) * WINDOW].astype(BF16)

    core, rr = _call(
        body, grid=(s // WINDOW,), in_specs=_attn_specs(s // WINDOW), out_specs=[pl.BlockSpec((WINDOW, Q_WIDTH), lambda n: (n, 0))],
        out_shape=[jax.ShapeDtypeStruct((s, Q_WIDTH), BF16)], operands=(qkv, qkv, qkv, qkv, qkv, sink_rows), sem=("parallel",),
        name=name, riders=riders)
    return _ret(core, rr, riders)


def attn_bwd(qkv, sink_rows, do, *, name, riders=()):
    s = qkv.shape[0]

    def body(q_ref, kc_ref, kp_ref, vc_ref, vp_ref, sink_ref, do_ref, dq_ref, dkc_ref, dkp_ref, dvc_ref, dvp_ref, dsink_ref):
        n = pl.program_id(0)
        prev = _prev_slots()
        scale = HEAD_DIM ** -0.5
        parts = []
        for h in range(N_KV_HEADS):
            qv, dov = _stack_heads(q_ref, h), _stack_heads(do_ref, h)
            kband, vband = _band(kp_ref, kc_ref, h), _band(vp_ref, vc_ref, h)
            p, ps = _attn_probs(qv, kband, sink_ref[h, 0:1, :], prev, n > 0)
            dp = _pick(prev, lax.dot_general(vband, dov, NT_DIMS, preferred_element_type=F32))
            delta = jnp.sum(p * dp, axis=0, keepdims=True)
            ds_band = _spread(prev, p * (dp - delta) * scale)
            p_band = _spread(prev, p)
            dk = jnp.dot(ds_band, qv, preferred_element_type=F32).astype(BF16)
            dv = jnp.dot(p_band, dov, preferred_element_type=F32).astype(BF16)
            dq = lax.dot_general(kband, ds_band, TN_DIMS, preferred_element_type=F32).T
            cols = slice(h * HEAD_DIM, (h + 1) * HEAD_DIM)
            dkp_ref[:, cols], dkc_ref[:, cols] = dk[:WINDOW], dk[WINDOW:]
            dvp_ref[:, cols], dvc_ref[:, cols] = dv[:WINDOW], dv[WINDOW:]
            dsink = -(ps * delta)
            for g in range(GQA_GROUP):
                head = GQA_GROUP * h + g
                dq_ref[:, head * HEAD_DIM:(head + 1) * HEAD_DIM] = dq[g * WINDOW:(g + 1) * WINDOW].astype(BF16)
                parts.append(jnp.broadcast_to(jnp.sum(dsink[:, g * WINDOW:(g + 1) * WINDOW], axis=1, keepdims=True), (8, LANES)))

        @pl.when(n == 0)
        def _():
            for i, part in enumerate(parts):
                dsink_ref[i // GQA_GROUP, i % GQA_GROUP] = part

        @pl.when(n > 0)
        def _():
            for i, part in enumerate(parts):
                dsink_ref[i // GQA_GROUP, i % GQA_GROUP] += part

    rows_q = pl.BlockSpec((WINDOW, Q_WIDTH), lambda n: (n, 0))
    rows_kv = pl.BlockSpec((WINDOW, KV_WIDTH), lambda n: (n, 0))
    kv_shape = jax.ShapeDtypeStruct((s, KV_WIDTH), BF16)
    core, rr = _call(
        body, grid=(s // WINDOW,), in_specs=_attn_specs(s // WINDOW) + [rows_q],
        out_specs=[rows_q, rows_kv, rows_kv, rows_kv, rows_kv,
                   pl.BlockSpec((N_KV_HEADS, GQA_GROUP, 8, LANES), lambda n: (0, 0, 0, 0))],
        out_shape=[jax.ShapeDtypeStruct((s, Q_WIDTH), BF16), kv_shape, kv_shape, kv_shape, kv_shape,
                   jax.ShapeDtypeStruct((N_KV_HEADS, GQA_GROUP, 8, LANES), F32)],
        operands=(qkv, qkv, qkv, qkv, qkv, sink_rows, do), sem=("arbitrary",), name=name, riders=riders)
    return _ret(core, rr, riders)


GELU_C = 0.7978845608028654
GELU_A = 0.044715


def _gelu(x):
    return 0.5 * x * (1.0 + jnp.tanh(x * (GELU_C + (GELU_C * GELU_A) * (x * x))))


def _gelu_and_grad(x):
    x2 = x * x
    t = jnp.tanh(x * (GELU_C + (GELU_C * GELU_A) * x2))
    half_x, one_t = 0.5 * x, 1.0 + t
    return half_x * one_t, 0.5 * one_t + half_x * (1.0 - t * t) * (GELU_C + (3.0 * GELU_C * GELU_A) * x2)


def _tril_bf16(w):
    row = lax.broadcasted_iota(I32, (SGU_CHUNK, SGU_CHUNK), 0)
    col = lax.broadcasted_iota(I32, (SGU_CHUNK, SGU_CHUNK), 1)
    return jnp.where(row >= col, w, 0.0).astype(BF16)


def _sgu_norm(vg, g, b):
    mu = jnp.mean(vg, axis=-1, keepdims=True)
    cen = vg - mu
    rstd = lax.rsqrt(jnp.mean(cen * cen, axis=-1, keepdims=True) + EPS)
    xhat = cen * rstd
    return xhat, rstd, xhat * g + b


def sgu_in_fwd(h, w_in, ln_g, ln_b, w_sp, b_sp, *, name, tm=256, riders=()):
    s, k = h.shape
    ns = w_in.shape[2]
    tm = _row_tile(s, tm)

    def body(h_ref, w0, w1, w2, w3, g_ref, b_ref, w_ref, bs_ref, z_ref, y_ref):
        hv = h_ref[...]
        zs = [jnp.dot(hv, w_ref_j[...], preferred_element_type=F32) for w_ref_j in (w0, w1, w2, w3)]
        for j, zj in enumerate(zs):
            z_ref[:, j * ns:(j + 1) * ns] = zj.astype(BF16)
        u = _gelu(jnp.concatenate(zs[:2], axis=1))
        _, _, vn = _sgu_norm(_gelu(jnp.concatenate(zs[2:], axis=1)), g_ref[...], b_ref[...])
        vn = vn.astype(BF16)
        for grp in range(SGU_GROUPS):
            w = _tril_bf16(w_ref[grp])
            cols = slice(grp * LANES, (grp + 1) * LANES)
            for ch in range(tm // SGU_CHUNK):
                rows = slice(ch * SGU_CHUNK, (ch + 1) * SGU_CHUNK)
                mixed = jnp.dot(w, vn[rows, cols], preferred_element_type=F32) + bs_ref[grp]
                y_ref[rows, cols] = (u[rows, cols] * mixed).astype(BF16)

    def shard(j):
        return pl.BlockSpec((None, k, ns), lambda i: (j, 0, 0))

    full3 = pl.BlockSpec((SGU_GROUPS, SGU_CHUNK, SGU_CHUNK), lambda i: (0, 0, 0))
    core, rr = _call(
        body, grid=(s // tm,),
        in_specs=[_row_spec(tm, k)] + [shard(j) for j in range(N_CHIPS)] + [_vec_spec(D_MODEL), _vec_spec(D_MODEL), full3, full3],
        out_specs=[_row_spec(tm, 2 * D_MODEL), _row_spec(tm, D_MODEL)],
        out_shape=[jax.ShapeDtypeStruct((s, 2 * D_MODEL), BF16), jax.ShapeDtypeStruct((s, D_MODEL), BF16)],
        operands=(h, w_in, w_in, w_in, w_in, ln_g, ln_b, w_sp, b_sp), sem=("parallel",), name=name, riders=riders)
    return _ret(core, rr, riders)


def sgu_bwd(z, dy, ln_g, ln_b, w_sp, b_sp, *, name, tm=256, riders=()):
    s = z.shape[0]
    tm = _row_tile(s, tm)

    def body(z_ref, dy_ref, g_ref, b_ref, w_ref, bs_ref, dz_ref, dw_ref, dbs_ref, dg_ref, db_ref, dvn_buf):
        first = pl.program_id(0) == 0
        u, u_grad = _gelu_and_grad(z_ref[:, :D_MODEL].astype(F32))
        vg, v_grad = _gelu_and_grad(z_ref[:, D_MODEL:].astype(F32))
        xhat, rstd, vn = _sgu_norm(vg, g_ref[...], b_ref[...])
        vn = vn.astype(BF16)
        dyv = dy_ref[...]
        dmixed = dyv * u
        dz_gate = dyv * u_grad
        row = lax.broadcasted_iota(I32, (SGU_CHUNK, SGU_CHUNK), 0)
        col = lax.broadcasted_iota(I32, (SGU_CHUNK, SGU_CHUNK), 1)
        dws, dbss = [], []
        for grp in range(SGU_GROUPS):
            w = _tril_bf16(w_ref[grp])
            cols = slice(grp * LANES, (grp + 1) * LANES)
            dw = jnp.zeros((SGU_CHUNK, SGU_CHUNK), F32)
            dbs = jnp.zeros((SGU_CHUNK, 1), F32)
            for ch in range(tm // SGU_CHUNK):
                rows = slice(ch * SGU_CHUNK, (ch + 1) * SGU_CHUNK)
                vblk = vn[rows, cols]
                mixed = jnp.dot(w, vblk, preferred_element_type=F32) + bs_ref[grp]
                dz_ref[rows, cols] = (dz_gate[rows, cols] * mixed).astype(BF16)
                dm = dmixed[rows, cols]
                dmb = dm.astype(BF16)
                dvn_buf[rows, cols] = lax.dot_general(w, dmb, TN_DIMS, preferred_element_type=F32)
                dw += lax.dot_general(dmb, vblk, NT_DIMS, preferred_element_type=F32)
                dbs += jnp.sum(dm, axis=-1, keepdims=True)
            dws.append(jnp.where(row >= col, dw, 0.0))
            dbss.append(jnp.broadcast_to(dbs, (SGU_CHUNK, SGU_CHUNK)))

        dvn = dvn_buf[...]
        dxhat = dvn * g_ref[...]
        dvg = rstd * (dxhat - jnp.mean(dxhat, axis=-1, keepdims=True) - xhat * jnp.mean(dxhat * xhat, axis=-1, keepdims=True))
        dz_ref[:, D_MODEL:] = (dvg * v_grad).astype(BF16)
        dlng, dlnb = jnp.sum(dvn * xhat, axis=0, keepdims=True), jnp.sum(dvn, axis=0, keepdims=True)

        @pl.when(first)
        def _():
            for grp in range(SGU_GROUPS):
                dw_ref[grp] = dws[grp]
                dbs_ref[grp] = dbss[grp]
            dg_ref[...] = dlng
            db_ref[...] = dlnb

        @pl.when(jnp.logical_not(first))
        def _():
            for grp in range(SGU_GROUPS):
                dw_ref[grp] += dws[grp]
                dbs_ref[grp] += dbss[grp]
            dg_ref[...] += dlng
            db_ref[...] += dlnb

    full3 = pl.BlockSpec((SGU_GROUPS, SGU_CHUNK, SGU_CHUNK), lambda i: (0, 0, 0))
    s3 = jax.ShapeDtypeStruct((SGU_GROUPS, SGU_CHUNK, SGU_CHUNK), F32)
    vshape = jax.ShapeDtypeStruct((1, D_MODEL), F32)
    core, rr = _call(
        body, grid=(s // tm,),
        in_specs=[_row_spec(tm, 2 * D_MODEL), _row_spec(tm, D_MODEL), _vec_spec(D_MODEL), _vec_spec(D_MODEL), full3, full3],
        out_specs=[_row_spec(tm, 2 * D_MODEL), full3, full3, _vec_spec(D_MODEL), _vec_spec(D_MODEL)],
        out_shape=[jax.ShapeDtypeStruct((s, 2 * D_MODEL), BF16), s3, s3, vshape, vshape],
        scratch_shapes=[pltpu.VMEM((tm, D_MODEL), F32)], operands=(z, dy, ln_g, ln_b, w_sp, b_sp), name=name, riders=riders)
    return _ret(core, rr, riders)


def _sigmoid(x):
    return 1.0 / (1.0 + jnp.exp(-x))


def ffn_up(h, w_gu, *, name, tm=512, riders=()):
    s = h.shape[0]
    tm = _row_tile(s, tm)

    def body(h_ref, wg_ref, wu_ref, d_ref, a_ref):
        hv = h_ref[...]
        sub = min(256, tm)
        for t in range(tm // sub):
            rows = slice(t * sub, (t + 1) * sub)
            g = jnp.dot(hv[rows], wg_ref[...], preferred_element_type=F32)
            u = jnp.dot(hv[rows], wu_ref[...], preferred_element_type=F32)
            sig = _sigmoid(g)
            silu = g * sig
            d_ref[0, rows, :] = (u * (sig + silu * (1.0 - sig))).astype(BF16)
            d_ref[1, rows, :] = silu.astype(BF16)
            a_ref[rows, :] = (silu * u).astype(BF16)

    core, rr = _call(
        body, grid=(2, s // tm),
        in_specs=[pl.BlockSpec((tm, D_MODEL), lambda j, i: (i, 0)),
                  pl.BlockSpec((None, D_MODEL, FF_HALF), lambda j, i: (j, 0, 0)),
                  pl.BlockSpec((None, D_MODEL, FF_HALF), lambda j, i: (j + 2, 0, 0))],
        out_specs=[pl.BlockSpec((2, tm, FF_HALF), lambda j, i: (0, i, j)), pl.BlockSpec((tm, FF_HALF), lambda j, i: (i, j))],
        out_shape=[jax.ShapeDtypeStruct((2, s, D_FF), BF16), jax.ShapeDtypeStruct((s, D_FF), BF16)],
        operands=(h, w_gu, w_gu), sem=("parallel", "parallel"), name=name, riders=riders)
    return _ret(core, rr, riders)


def ffn_dact(df, w_d, gu, *, name, tm=512, riders=()):
    s = df.shape[0]
    tm = _row_tile(s, tm)

    def body(df_ref, w_ref, d_ref, o_ref):
        da = lax.dot_general(df_ref[...], w_ref[...], NT_DIMS, preferred_element_type=F32)
        o_ref[0] = (da * d_ref[0].astype(F32)).astype(BF16)
        o_ref[1] = (da * d_ref[1].astype(F32)).astype(BF16)

    planes = pl.BlockSpec((2, tm, FF_HALF), lambda j, i: (0, i, j))
    core, rr = _call(
        body, grid=(2, s // tm),
        in_specs=[pl.BlockSpec((tm, D_MODEL), lambda j, i: (i, 0)), pl.BlockSpec((FF_HALF, D_MODEL), lambda j, i: (j, 0)), planes],
        out_specs=[planes], out_shape=[jax.ShapeDtypeStruct((2, s, D_FF), BF16)], operands=(df, w_d, gu),
        sem=("parallel", "parallel"), name=name, riders=riders)
    return _ret(core, rr, riders)


def _weight_tile(rows):
    for tr in (512, 352, 256, 128):
        if rows % tr == 0:
            return tr
    return rows


def place_shard(w, layer, chip_arr, dtype, *, name, riders=()):
    _, r, c = w.shape
    tr = _weight_tile(r)

    def body(chip_ref, w_ref, o_ref):
        o_ref[...] = w_ref[...].astype(dtype)

    core, rr = _call(
        body, grid=(r // tr,), prefetch=(chip_arr,),
        in_specs=[pl.BlockSpec((None, tr, c), lambda i, chip: (layer, i, 0))],
        out_specs=[pl.BlockSpec((None, tr, c), lambda i, chip: (chip[0], i, 0))],
        out_shape=[jax.ShapeDtypeStruct((N_CHIPS, r, c), dtype)], operands=(w,), sem=("parallel",), name=name, riders=riders)
    return _ret(core, rr, riders)


def _adamw_math(w, g, m, v):
    m = ADAM_B1 * m + (1.0 - ADAM_B1) * g
    v = ADAM_B2 * v + (1.0 - ADAM_B2) * (g * g)
    m_hat = m / (1.0 - ADAM_B1 ** ADAM_STEP)
    v_hat = v / (1.0 - ADAM_B2 ** ADAM_STEP)
    delta = -ADAM_LR * (m_hat / (jnp.sqrt(v_hat) + ADAM_EPS) + ADAM_WD * w)
    return delta, m, v


def adamw(w, g, m, v, *, name):
    nl, r, c = w.shape
    tr = _weight_tile(r)

    def body(w_ref, g_ref, m_ref, v_ref, go_ref, d_ref, mo_ref, vo_ref):
        gv = g_ref[...]
        go_ref[...] = gv
        d_ref[...], mo_ref[...], vo_ref[...] = _adamw_math(w_ref[...], gv, m_ref[...], v_ref[...])

    spec = pl.BlockSpec((None, tr, c), lambda l, i: (l, i, 0))
    shape = jax.ShapeDtypeStruct(w.shape, F32)
    outs, _ = _call(body, grid=(nl, r // tr), in_specs=[spec] * 4, out_specs=[spec] * 4, out_shape=[shape] * 4,
                    operands=(w, g, m, v), sem=("parallel", "parallel"), name=name)
    return outs


def adamw_small(ws, gs, ms, vs, *, name):
    n = len(ws)

    def body(*refs):
        ins, outs = refs[:4 * n], refs[4 * n:]
        for t in range(n):
            gv = ins[n + t][...]
            outs[t][...] = gv
            outs[n + t][...], outs[2 * n + t][...], outs[3 * n + t][...] = _adamw_math(
                ins[t][...], gv, ins[2 * n + t][...], ins[3 * n + t][...])

    shapes = [jax.ShapeDtypeStruct(w.shape, F32) for w in ws]
    res = pl.pallas_call(body, out_shape=shapes * 4, name=name)(*ws, *gs, *ms, *vs)
    return res[:n], res[n:2 * n], res[2 * n:3 * n], res[3 * n:]


def pair_add(g, r1, c_arr, *, name):
    _, rows, cdim = g.shape
    h = rows // 2

    def body(c_ref, g_ref, r_ref, o_ref):
        o_ref[...] = (g_ref[...].astype(F32) + r_ref[...].astype(F32)).astype(o_ref.dtype)

    (out,), _ = _call(
        body, grid=(N_CHIPS,), prefetch=(c_arr,),
        in_specs=[pl.BlockSpec((None, h, cdim), lambda s, c: (s, c[0], 0)), pl.BlockSpec((None, h, cdim), lambda s, c: (s, 0, 0))],
        out_specs=[pl.BlockSpec((None, h, cdim), lambda s, c: (s, 0, 0))],
        out_shape=[jax.ShapeDtypeStruct((N_CHIPS, h, cdim), g.dtype)], operands=(g, r1), sem=("parallel",), name=name)
    return out


def final_add(g, r1, r2, jc_arr, *, dest_shape, lead, prev, name):
    _, rows, cdim = g.shape
    h = rows // 2

    def body(jc_ref, g_ref, r1_ref, r2_ref, *rest):
        o_ref = rest[-1]
        acc = g_ref[...].astype(F32) + r1_ref[...].astype(F32)
        for k in range(3):
            acc = acc + r2_ref[k].astype(F32)
        o_ref[...] = acc

    if lead is None:
        o_spec = pl.BlockSpec((h, cdim), lambda i, jc: (jc[1], 0))
    elif lead == "chip":
        o_spec = pl.BlockSpec((None, h, cdim), lambda i, jc: (jc[0], jc[1], 0))
    else:
        o_spec = pl.BlockSpec((None, h, cdim), lambda i, jc: (lead, jc[1], 0))
    in_specs = [pl.BlockSpec((None, h, cdim), lambda i, jc: (jc[0], jc[1], 0)),
                pl.BlockSpec((None, h, cdim), lambda i, jc: (jc[0], 0, 0)),
                pl.BlockSpec((3, h, cdim), lambda i, jc: (0, 0, 0))]
    operands = [g, r1, r2]
    aliases = None
    if prev is not None:
        in_specs.append(ANY)
        operands.append(prev)
        aliases = {3: 0}
    (out,), _ = _call(body, grid=(1,), prefetch=(jc_arr,), in_specs=in_specs, out_specs=[o_spec],
                      out_shape=[jax.ShapeDtypeStruct(dest_shape, F32)], operands=operands, aliases=aliases, name=name)
    return out


def _place():
    return lax.axis_index("x"), lax.axis_index("y"), lax.axis_index("c")


def _partner(x, y, k):
    return (1 - x if k >> 1 else x), (1 - y if k & 1 else y)


WHOLE = (0, 1, 1)


def _half(rows, sel, dtype, piece=WHOLE):
    lo, hi, n = piece
    align = 16 if dtype == BF16 else 8
    step = rows // 2 // n
    assert rows // 2 == step * n and step % align == 0
    return pl.ds(pl.multiple_of(sel * (rows // 2) + lo * step, align), (hi - lo) * step)


def _rider(peers, inputs, aliased, fresh, nsem, copies, arrivals):
    def start(ins, outs, send, recv):
        for cp in copies(ins, outs, send, recv):
            cp.start()

    def finish(ins, outs, send, recv):
        for cp in arrivals(ins, outs, send, recv):
            cp.wait_recv()
        for cp in copies(ins, outs, send, recv):
            cp.wait_send()

    return types.SimpleNamespace(peers=peers, inputs=list(inputs), aliased=list(aliased), fresh=list(fresh), nsem=nsem,
                                 start=start, finish=finish)


def _remote(src, dst, send, recv, idx, dev):
    return pltpu.make_async_remote_copy(src_ref=src, dst_ref=dst, send_sem=send.at[idx], recv_sem=recv.at[idx],
                                        device_id=dev, device_id_type=MESH)


def gather_ici_rider(fulls, pieces=None):
    nt = len(fulls)
    pieces = pieces or [WHOLE] * nt

    def region(outs, t, slot, sel):
        return outs[t].at[slot, _half(fulls[t].shape[1], sel, fulls[t].dtype, pieces[t])]

    def copies(ins, outs, send, recv):
        x, y, c = _place()
        res = []
        for t in range(nt):
            for k in (1, 2, 3):
                px, py = _partner(x, y, k)
                mine = region(outs, t, 2 * x + y, c)
                res.append(_remote(mine, mine, send, recv, 3 * t + k - 1, (px, py, c)))
        return res

    def arrivals(ins, outs, send, recv):
        x, y, c = _place()
        res = []
        for t in range(nt):
            for k in (1, 2, 3):
                px, py = _partner(x, y, k)
                theirs = region(outs, t, 2 * px + py, c)
                res.append(_remote(theirs, theirs, send, recv, 3 * t + k - 1, (x, y, c)))
        return res

    return _rider("chips", fulls, range(nt), [], 3 * nt, copies, arrivals)


def gather_d2d_rider(fulls, pieces=None):
    nt = len(fulls)
    pieces = pieces or [WHOLE] * nt

    def region(outs, t, slot, sel):
        return outs[t].at[slot, _half(fulls[t].shape[1], sel, fulls[t].dtype, pieces[t])]

    def both(outs, send, recv, mine):
        x, y, c = _place()
        res = []
        for t in range(nt):
            for k in (1, 2, 3):
                px, py = _partner(x, y, k)
                part = region(outs, t, 2 * px + py, c if mine else 1 - c)
                res.append(_remote(part, part, send, recv, 3 * t + k - 1, (x, y, 1 - c)))
        return res

    return _rider("sibling", fulls, range(nt), [], 3 * nt, lambda i, o, s, r: both(o, s, r, True),
                  lambda i, o, s, r: both(o, s, r, False))


def exchange_rider(grads):
    nt = len(grads)

    def both(ins, outs, send, recv):
        x, y, c = _place()
        return [_remote(ins[t].at[:, _half(grads[t].shape[1], 1 - c, grads[t].dtype)], outs[t], send, recv, t, (x, y, 1 - c))
                for t in range(nt)]

    fresh = [jax.ShapeDtypeStruct((N_CHIPS, g.shape[1] // 2, g.shape[2]), g.dtype) for g in grads]
    return _rider("sibling", grads, [], fresh, nt, both, both)


def scatter_rider(parts):
    nt = len(parts)

    def both(ins, outs, send, recv):
        x, y, c = _place()
        res = []
        for t in range(nt):
            for k in (1, 2, 3):
                px, py = _partner(x, y, k)
                res.append(_remote(ins[t].at[2 * px + py], outs[t].at[k - 1], send, recv, 3 * t + k - 1, (px, py, c)))
        return res

    fresh = [jax.ShapeDtypeStruct((3,) + p.shape[1:], p.dtype) for p in parts]
    return _rider("chips", parts, [], fresh, 3 * nt, both, both)


def broadcast_rider(bufs, items):
    def region(outs, item, sel):
        bi, lead = item
        ref = outs[bi]
        if lead == "chip":
            x, y, _ = _place()
            ref = ref.at[2 * x + y]
        elif lead is not None:
            ref = ref.at[lead]
        return ref.at[_half(ref.shape[0], sel, F32)]

    def both(outs, send, recv, mine):
        x, y, c = _place()
        res = []
        for i, item in enumerate(items):
            part = region(outs, item, c if mine else 1 - c)
            res.append(_remote(part, part, send, recv, i, (x, y, 1 - c)))
        return res

    return _rider("sibling", bufs, range(len(bufs)), [], len(items), lambda i, o, s, r: both(o, s, r, True),
                  lambda i, o, s, r: both(o, s, r, False))


def allcast_rider(buf):
    peers = [(k, flip) for k in range(N_CHIPS) for flip in (0, 1) if (k, flip) != (0, 0)]

    def both(outs, send, recv, mine):
        x, y, c = _place()
        res = []
        for i, (k, flip) in enumerate(peers):
            px, py = _partner(x, y, k)
            pc = 1 - c if flip else c
            slot, sel = (2 * x + y, c) if mine else (2 * px + py, pc)
            part = outs[0].at[slot, _half(buf.shape[1], sel, F32)]
            res.append(_remote(part, part, send, recv, i, (px, py, pc)))
        return res

    return _rider("everyone", [buf], [0], [], len(peers), lambda i, o, s, r: both(o, s, r, True),
                  lambda i, o, s, r: both(o, s, r, False))


def comm_call(riders, *, name):
    _, res = _call(None, riders=riders, name=name)
    return res


SLAB_ROWS = 192


def _pad_rows(a, rows=8):
    return jnp.pad(a, ((0, rows - a.shape[0]), (0, 0)))


def _pack_small(norm_grads, db_qkv, db_o, dsinks, db_sp, dln_g, dln_b, dw_sp, loss_part):
    parts = [
        jnp.concatenate(norm_grads, axis=0),
        _pad_rows(jnp.pad(db_qkv, ((0, 0), (0, 2 * D_MODEL - QKV_WIDTH))).reshape(2, D_MODEL)),
        _pad_rows(db_o),
        _pad_rows(jnp.pad(dsinks.reshape(1, N_Q_HEADS), ((0, 0), (0, D_MODEL - N_Q_HEADS)))),
        _pad_rows(db_sp.reshape(1, D_MODEL)),
        _pad_rows(jnp.concatenate([dln_g, dln_b, jnp.pad(loss_part[0:1], ((0, 0), (0, D_MODEL - LANES)))], axis=0)),
        dw_sp.reshape(SGU_CHUNK, D_MODEL),
    ]
    slab = jnp.concatenate(parts, axis=0)
    return jnp.pad(slab, ((0, SLAB_ROWS - slab.shape[0]), (0, 0))).reshape(N_CHIPS, SLAB_ROWS // N_CHIPS, D_MODEL)


def _unpack_small(slab, j):
    slab = slab.reshape(SLAB_ROWS, D_MODEL)
    norms = [slab[2 * i:2 * i + 2] for i in range(4)]
    db_qkv = slab[8:10].reshape(1, 2 * D_MODEL)[:, :QKV_WIDTH]
    db_o = slab[16:17]
    dsinks = slab[24:25, :N_Q_HEADS]
    db_sp = slab[32:33].reshape(SGU_GROUPS, SGU_CHUNK)
    width = D_MODEL // N_CHIPS
    dln_g = lax.dynamic_slice(slab[40:41], (0, j * width), (1, width))
    dln_b = lax.dynamic_slice(slab[41:42], (0, j * width), (1, width))
    dw_sp = slab[48:48 + SGU_CHUNK].reshape(SGU_GROUPS * SGU_CHUNK, SGU_CHUNK)
    return norms, db_qkv, db_o, dsinks, db_sp, dln_g, dln_b, dw_sp, slab[42, 0]


class _GradReduce:
    def __init__(self, c_arr, jc_arr, dest_shapes):
        self.c_arr, self.jc_arr, self.dest_shapes = c_arr, jc_arr, dest_shapes
        self.grad, self.sibling, self.pair, self.chips, self.dest = {}, {}, {}, {}, {}

    def exchange(self, tags):
        return exchange_rider([self.grad[t] for t in tags])

    def exchanged(self, tags, res):
        for t, r in zip(tags, res):
            self.sibling[t] = r
            self.pair[t] = pair_add(self.grad[t], r, self.c_arr, name=f"pair_add_{t}")

    def scatter(self, tags):
        return scatter_rider([self.pair[t] for t in tags])

    def scattered(self, tags, res, where):
        for t, r in zip(tags, res):
            name, lead = where[t]
            self.dest[name] = final_add(self.grad[t], self.sibling[t], r, self.jc_arr, dest_shape=self.dest_shapes[name],
                                        lead=lead, prev=self.dest.get(name), name=f"final_add_{t}")

    def broadcast(self, items):
        names = []
        for n, _ in items:
            if n not in names:
                names.append(n)
        return names, broadcast_rider([self.dest[n] for n in names], [(names.index(n), lead) for n, lead in items])

    def broadcasted(self, names, res):
        for n, r in zip(names, res):
            self.dest[n] = r


def kernel(x, norm_mix_pre, norm_mix_post, norm_ffn_pre, norm_ffn_post, attn_w_qkv, attn_b_qkv, attn_sinks, attn_w_o, attn_b_o, sgu_w_in, sgu_ln_g, sgu_ln_b, sgu_w_spatial, sgu_b_spatial, sgu_w_out, ffn_w_gate_up, ffn_w_down, loss_target, m_norm_mix_pre, m_norm_mix_post, m_norm_ffn_pre, m_norm_ffn_post, m_attn_w_qkv, m_attn_b_qkv, m_attn_sinks, m_attn_w_o, m_attn_b_o, m_sgu_w_in, m_sgu_ln_g, m_sgu_ln_b, m_sgu_w_spatial, m_sgu_b_spatial, m_sgu_w_out, m_ffn_w_gate_up, m_ffn_w_down, v_norm_mix_pre, v_norm_mix_post, v_norm_ffn_pre, v_norm_ffn_post, v_attn_w_qkv, v_attn_b_qkv, v_attn_sinks, v_attn_w_o, v_attn_b_o, v_sgu_w_in, v_sgu_ln_g, v_sgu_ln_b, v_sgu_w_spatial, v_sgu_b_spatial, v_sgu_w_out, v_ffn_w_gate_up, v_ffn_w_down):
    s = x.shape[1]
    x0 = x.reshape(s, D_MODEL)
    target = loss_target.reshape(s, D_MODEL)
    mx, my, mc = lax.axis_index("x"), lax.axis_index("y"), lax.axis_index("c")
    chip = 2 * mx + my
    chip_arr = jnp.reshape(chip, (1,)).astype(I32)
    c_arr = jnp.reshape(mc, (1,)).astype(I32)
    jc_arr = jnp.stack([chip, mc]).astype(I32)
    zero_bias = jnp.zeros((1, D_MODEL), F32)

    def gain(p, i):
        return p[i:i + 1]

    big = [attn_w_qkv, attn_w_o, sgu_w_in, sgu_w_out, ffn_w_gate_up, ffn_w_gate_up, ffn_w_down, ffn_w_down]
    layers = [0, 0, 0, 0, 0, 1, 0, 1]
    tags = ["qkv", "wo", "win", "wout", "wgu0", "wgu1", "wd0", "wd1"]
    full = {t: place_shard(w, l, chip_arr, BF16, name=f"place_{t}") for w, l, t in zip(big, layers, tags) if t != "wgu1"}
    ln_pack = _pad_rows(jnp.concatenate([sgu_ln_g, sgu_ln_b], axis=0), 16)[None]
    full["ln"] = place_shard(ln_pack, 0, chip_arr, F32, name="place_ln")

    def split(items):
        return [i if isinstance(i, str) else i[0] for i in items], [WHOLE if isinstance(i, str) else tuple(i[1:]) for i in items]

    def ici(*items):
        names, pieces = split(items)
        return gather_ici_rider([full[n] for n in names], pieces)

    def d2d(*items):
        names, pieces = split(items)
        return gather_d2d_rider([full[n] for n in names], pieces)

    def landed(items, res):
        for n, r in zip(split(items)[0], res):
            full[n] = r

    cos, sin = _rope_tables(s)
    sink_rows = jnp.broadcast_to(
        jnp.repeat(attn_sinks.reshape(N_KV_HEADS, GQA_GROUP), WINDOW, axis=1)[:, None, :], (N_KV_HEADS, 8, ROWS))
    w_sp = sgu_w_spatial.reshape(SGU_GROUPS, SGU_CHUNK, SGU_CHUNK)
    b_sp = jnp.broadcast_to(sgu_b_spatial.reshape(SGU_GROUPS, SGU_CHUNK)[:, :, None], (SGU_GROUPS, SGU_CHUNK, LANES))

    h0, (res,) = prenorm(x0, gain(norm_mix_pre, 0), name="prenorm_0", riders=[ici("qkv", "ln")])
    landed(("qkv", "ln"), res)
    full["wgu1"], (res,) = place_shard(ffn_w_gate_up, 1, chip_arr, BF16, name="place_wgu1", riders=[d2d("qkv", "ln")])
    landed(("qkv", "ln"), res)
    ln_g = full["ln"][:, 0, :].reshape(1, D_MODEL)
    ln_b = full["ln"][:, 1, :].reshape(1, D_MODEL)

    def hosted(call, stages):
        outputs, results = call([{"ici": ici, "d2d": d2d}[kind](*items) for kind, items in stages])
        for (_, items), res in zip(stages, results):
            landed(items, res)
        return outputs

    qkv = hosted(lambda r: qkv_proj(h0, full["qkv"], attn_b_qkv, cos, sin, name="qkv_proj", riders=r),
                 [("ici", ("wo", ("wgu0", 0, 4, 8)))])
    o = hosted(lambda r: attn_fwd(qkv, sink_rows, name="attn_fwd", riders=r),
               [("d2d", ("wo",)), ("ici", (("wgu0", 4, 8, 8), ("wd0", 0, 1, 2)))])
    w_o = full["wo"].reshape(Q_WIDTH, D_MODEL)
    x1, h1, m0 = hosted(lambda r: proj_residual_norm(o, w_o, x0, attn_b_o, gain(norm_mix_post, 0), gain(norm_ffn_pre, 0),
                                                     name="attn_out_norm", riders=r),
                        [("d2d", ("wgu0",)), ("ici", (("wd0", 1, 2, 2), ("win", 0, 2, 8)))])
    gu0, a0 = hosted(lambda r: ffn_up(h1, full["wgu0"], name="ffn_up_0", riders=r),
                     [("d2d", ("wd0",)), ("ici", (("win", 2, 8, 8), "wout", ("wgu1", 0, 4, 8)))])
    w_d0 = full["wd0"].reshape(D_FF, D_MODEL)
    x2, h2, f0 = hosted(lambda r: proj_residual_norm(a0, w_d0, x1, zero_bias, gain(norm_ffn_post, 0), gain(norm_mix_pre, 1),
                                                     name="ffn_down_norm_0", riders=r),
                        [("d2d", ("win", "wout")), ("ici", (("wgu1", 4, 7, 8),))])
    w_in = full["win"]
    z, y = hosted(lambda r: sgu_in_fwd(h2, w_in, ln_g, ln_b, w_sp, b_sp, name="sgu_in_fwd", riders=r),
                  [("ici", (("wgu1", 7, 8, 8), ("wd1", 0, 1, 2)))])
    w_out = full["wout"].reshape(D_MODEL, D_MODEL)
    x3, h3, m1 = hosted(lambda r: proj_residual_norm(y, w_out, x2, zero_bias, gain(norm_mix_post, 1), gain(norm_ffn_pre, 1),
                                                     name="sgu_out_norm", riders=r),
                        [("d2d", ("wgu1",)), ("ici", (("wd1", 1, 2, 2),))])
    w_qkv, w_gu0, w_gu1 = full["qkv"], full["wgu0"], full["wgu1"]
    gu1, a1 = hosted(lambda r: ffn_up(h3, w_gu1, name="ffn_up_1", riders=r), [("d2d", ("wd1",))])
    w_d1 = full["wd1"].reshape(D_FF, D_MODEL)
    dx4, df1, dg_fpost1, loss_part = proj_loss_head(a1, w_d1, x3, gain(norm_ffn_post, 1), target, name="ffn_down_loss")

    red = _GradReduce(c_arr, jc_arr, {
        "qkv": attn_w_qkv.shape[1:], "wo": attn_w_o.shape[1:], "win": sgu_w_in.shape[1:], "wout": sgu_w_out.shape[1:],
        "wgu": ffn_w_gate_up.shape, "wd": ffn_w_down.shape, "slab": (N_CHIPS, SLAB_ROWS // N_CHIPS, D_MODEL)})
    where = {"qkv": ("qkv", None), "wo": ("wo", None), "win": ("win", None), "wout": ("wout", None), "wgu0": ("wgu", 0),
             "wgu1": ("wgu", 1), "wd0": ("wd", 0), "wd1": ("wd", 1), "small": ("slab", "chip")}

    dgu1 = ffn_dact(df1, w_d1, gu1, name="ffn_dact_1")
    red.grad["wd1"] = mm_tn(a1, df1, shard_major=False, tm=256, tn=D_MODEL, name="dw_down_1").reshape(
        N_CHIPS, D_FF // N_CHIPS, D_MODEL)
    red.grad["wgu1"], (res,) = mm_tn(h3, dgu1, shard_major=True, tm=512, tn=FF_HALF, name="dw_gate_up_1",
                                     riders=[red.exchange(["wd1"])])
    red.exchanged(["wd1"], res)
    (dx3, dm1, dg_fpre1, dg_mpost1, _), (res_a, res_b) = dh_norm_bwd_pair(
        dgu1, w_gu1, dx4, x3, gain(norm_ffn_pre, 1), m1, gain(norm_mix_post, 1), name="dh_ffn_norm_1",
        riders=[red.exchange(["wgu1"]), red.scatter(["wd1"])])
    red.exchanged(["wgu1"], res_a)
    red.scattered(["wd1"], res_b, where)
    names, rider = red.broadcast([("wd", 1)])
    dy, (res,) = mm_nt(dm1, w_out, out_dtype=F32, name="dy_sgu", riders=[rider])
    red.broadcasted(names, res)
    red.grad["wout"] = mm_tn(y, dm1, shard_major=False, tm=512, tn=D_MODEL, name="dw_sgu_out").reshape(
        N_CHIPS, D_MODEL // N_CHIPS, D_MODEL)
    (dz, dw_sp, db_sp, dln_g, dln_b), (res_a, res_b) = sgu_bwd(
        z, dy, ln_g, ln_b, w_sp, b_sp, name="sgu_bwd", riders=[red.scatter(["wgu1"]), red.exchange(["wout"])])
    red.scattered(["wgu1"], res_a, where)
    red.exchanged(["wout"], res_b)
    names, rider = red.broadcast([("wgu", 1)])
    red.grad["win"], (res_a, res_b) = mm_tn(h2, dz, shard_major=True, tm=D_MODEL, tn=2 * D_MODEL // N_CHIPS, name="dw_sgu_in",
                                            riders=[rider, red.scatter(["wout"])])
    red.broadcasted(names, res_a)
    red.scattered(["wout"], res_b, where)
    names, rider = red.broadcast([("wout", None)])
    (dx2, df0, dg_mpre1, dg_fpost0, _), (res_a, res_b) = dh_norm_bwd_pair(
        dz, w_in, dx3, x2, gain(norm_mix_pre, 1), f0, gain(norm_ffn_post, 0), name="dh_sgu_norm",
        riders=[red.exchange(["win"]), rider])
    red.exchanged(["win"], res_a)
    red.broadcasted(names, res_b)
    dgu0, (res,) = ffn_dact(df0, w_d0, gu0, name="ffn_dact_0", riders=[red.scatter(["win"])])
    red.scattered(["win"], res, where)
    names, rider = red.broadcast([("win", None)])
    dw_d0, (res,) = mm_tn(a0, df0, shard_major=False, tm=256, tn=D_MODEL, name="dw_down_0", riders=[rider])
    red.broadcasted(names, res)
    red.grad["wd0"] = dw_d0.reshape(N_CHIPS, D_FF // N_CHIPS, D_MODEL)
    red.grad["wgu0"], (res,) = mm_tn(h1, dgu0, shard_major=True, tm=512, tn=FF_HALF, name="dw_gate_up_0",
                                     riders=[red.exchange(["wd0"])])
    red.exchanged(["wd0"], res)
    (dx1, dm0, dg_fpre0, dg_mpost0, db_o), (res_a, res_b) = dh_norm_bwd_pair(
        dgu0, w_gu0, dx2, x1, gain(norm_ffn_pre, 0), m0, gain(norm_mix_post, 0), name="dh_ffn_norm_0",
        riders=[red.exchange(["wgu0"]), red.scatter(["wd0"])])
    red.exchanged(["wgu0"], res_a)
    red.scattered(["wd0"], res_b, where)
    names, rider = red.broadcast([("wd", 0)])
    do, (res,) = mm_nt(dm0, w_o, out_dtype=BF16, name="do_attn", riders=[rider])
    red.broadcasted(names, res)
    red.grad["wo"] = mm_tn(o, dm0, shard_major=False, tm=512, tn=D_MODEL, name="dw_attn_out").reshape(
        N_CHIPS, Q_WIDTH // N_CHIPS, D_MODEL)
    (dq, dkc, dkp, dvc, dvp, dsink), (res_a, res_b) = attn_bwd(
        qkv, sink_rows, do, name="attn_bwd", riders=[red.scatter(["wgu0"]), red.exchange(["wo"])])
    red.scattered(["wgu0"], res_a, where)
    red.exchanged(["wo"], res_b)
    names, rider = red.broadcast([("wgu", 0)])
    (dqkv, db_qkv), (res_a, res_b) = rope_bwd(dq, dkc, dkp, dvc, dvp, cos, sin, name="rope_bwd",
                                              riders=[rider, red.scatter(["wo"])])
    red.broadcasted(names, res_a)
    red.scattered(["wo"], res_b, where)
    names, rider = red.broadcast([("wo", None)])
    red.grad["qkv"], (res,) = mm_tn(h0, dqkv, shard_major=True, tm=D_MODEL, tn=QKV_WIDTH // N_CHIPS, name="dw_qkv",
                                    riders=[rider])
    red.broadcasted(names, res)
    dh0, (res,) = mm_nt(dqkv, w_qkv, out_dtype=F32, tm=1024, name="dh_attn", riders=[red.exchange(["qkv"])])
    red.exchanged(["qkv"], res)
    grad_x, dg_mpre0 = norm_bwd_last(dx1, dh0, x0, gain(norm_mix_pre, 0), name="norm_bwd_in")

    norm_grads = [jnp.concatenate(p, axis=0) for p in
                  ((dg_mpre0, dg_mpre1), (dg_mpost0, dg_mpost1), (dg_fpre0, dg_fpre1), (dg_fpost0, dg_fpost1))]
    red.grad["small"] = _pack_small(norm_grads, db_qkv, db_o, dsink[:, :, 0, 0], db_sp[:, :, 0], dln_g, dln_b, dw_sp,
                                    loss_part)
    res_a, res_b = comm_call([red.scatter(["qkv"]), red.exchange(["small"])], name="tail_1")
    red.scattered(["qkv"], res_a, where)
    red.exchanged(["small"], res_b)
    names, rider = red.broadcast([("qkv", None)])
    res_a, res_b = comm_call([red.scatter(["small"]), rider], name="tail_2")
    red.scattered(["small"], res_a, where)
    red.broadcasted(names, res_b)
    ((slab_full,),) = comm_call([allcast_rider(red.dest["slab"])], name="tail_3")
    g_qkv, g_wo, g_win, g_wout, g_wgu, g_wd = (red.dest[n] for n in ("qkv", "wo", "win", "wout", "wgu", "wd"))
    g_norms, g_bqkv, g_bo, g_sinks, g_bsp, g_lng, g_lnb, g_wsp, loss = _unpack_small(slab_full, chip)

    def big_update(w, g, m, v, tag):
        return adamw(w, g.reshape(w.shape), m, v, name=f"adamw_{tag}")

    upd = {
        "attn_w_qkv": big_update(attn_w_qkv, g_qkv, m_attn_w_qkv, v_attn_w_qkv, "qkv"),
        "attn_w_o": big_update(attn_w_o, g_wo, m_attn_w_o, v_attn_w_o, "wo"),
        "sgu_w_in": big_update(sgu_w_in, g_win, m_sgu_w_in, v_sgu_w_in, "win"),
        "sgu_w_out": big_update(sgu_w_out, g_wout, m_sgu_w_out, v_sgu_w_out, "wout"),
        "ffn_w_gate_up": big_update(ffn_w_gate_up, g_wgu, m_ffn_w_gate_up, v_ffn_w_gate_up, "wgu"),
        "ffn_w_down": big_update(ffn_w_down, g_wd, m_ffn_w_down, v_ffn_w_down, "wd"),
    }
    small_names = ["norm_mix_pre", "norm_mix_post", "norm_ffn_pre", "norm_ffn_post", "attn_b_qkv", "attn_sinks", "attn_b_o",
                   "sgu_ln_g", "sgu_ln_b", "sgu_w_spatial", "sgu_b_spatial"]
    small_w = [norm_mix_pre, norm_mix_post, norm_ffn_pre, norm_ffn_post, attn_b_qkv, attn_sinks, attn_b_o, sgu_ln_g, sgu_ln_b,
               sgu_w_spatial, sgu_b_spatial]
    small_m = [m_norm_mix_pre, m_norm_mix_post, m_norm_ffn_pre, m_norm_ffn_post, m_attn_b_qkv, m_attn_sinks, m_attn_b_o,
               m_sgu_ln_g, m_sgu_ln_b, m_sgu_w_spatial, m_sgu_b_spatial]
    small_v = [v_norm_mix_pre, v_norm_mix_post, v_norm_ffn_pre, v_norm_ffn_post, v_attn_b_qkv, v_attn_sinks, v_attn_b_o,
               v_sgu_ln_g, v_sgu_ln_b, v_sgu_w_spatial, v_sgu_b_spatial]
    small_g = g_norms + [g_bqkv, g_sinks, g_bo, g_lng, g_lnb, g_wsp, g_bsp]

    def flat2(a):
        return a.reshape(-1, a.shape[-1])

    res = adamw_small([flat2(a) for a in small_w], [flat2(a) for a in small_g], [flat2(a) for a in small_m],
                      [flat2(a) for a in small_v], name="adamw_small")
    for i, nm in enumerate(small_names):
        upd[nm] = tuple(r[i].reshape(small_w[i].shape) for r in res)

    order = ["norm_mix_pre", "norm_mix_post", "norm_ffn_pre", "norm_ffn_post", "attn_w_qkv", "attn_b_qkv", "attn_sinks",
             "attn_w_o", "attn_b_o", "sgu_w_in", "sgu_ln_g", "sgu_ln_b", "sgu_w_spatial", "sgu_b_spatial", "sgu_w_out",
             "ffn_w_gate_up", "ffn_w_down"]
    outs = [loss, grad_x.reshape(1, s, D_MODEL)]
    for part in range(4):
        outs += [upd[nm][part] for nm in order]
    return tuple(outs)
```

```python
import types

import numpy as np
import jax
import jax.numpy as jnp
from jax import lax
from jax.experimental import pallas as pl
from jax.experimental.pallas import tpu as pltpu

F32 = jnp.float32
BF16 = jnp.bfloat16
I32 = jnp.int32

D_MODEL = 1024
HEAD_DIM = 64
N_Q_HEADS = 16
N_KV_HEADS = 4
GQA_GROUP = 4
WINDOW = 128
Q_WIDTH = 1024
KV_WIDTH = 256
QKV_WIDTH = 1536
ROPE_THETA = 10000.0
SGU_GROUPS = 8
SGU_CHUNK = 128
D_FF = 2816
FF_HALF = D_FF // 2
EPS = 1e-6
N_CHIPS = 4
LANES = 128

ADAM_LR = 0.001
ADAM_B1 = 0.9
ADAM_B2 = 0.999
ADAM_EPS = 1e-08
ADAM_WD = 0.01
ADAM_STEP = 10

VMEM_LIMIT = 52 * 1024 * 1024
MESH = pl.DeviceIdType.MESH
NEG = -1e30
NT_DIMS = (((1,), (1,)), ((), ()))
TN_DIMS = (((0,), (0,)), ((), ()))
NN_DIMS = (((1,), (0,)), ((), ()))
ANY = pl.BlockSpec(memory_space=pl.ANY)


def _row_tile(s, want):
    return want if s % want == 0 else s


PEER_KINDS = ("sibling", "chips", "sibling+chips", "everyone")


def _peer_kind(riders):
    kinds = {r.peers for r in riders}
    if not kinds:
        return None
    if "everyone" in kinds:
        return "everyone"
    return "sibling+chips" if len(kinds) == 2 else kinds.pop()


def _peer_barrier(kind):
    x, y, c = _place()
    chips = [(*_partner(x, y, k), c) for k in (1, 2, 3)]
    peers = {"sibling": [(x, y, 1 - c)], "chips": chips, "sibling+chips": [(x, y, 1 - c)] + chips,
             "everyone": [(x, y, 1 - c)] + chips + [(px, py, 1 - c) for px, py, _ in chips]}[kind]
    barrier = pltpu.get_barrier_semaphore()
    for dev in peers:
        pl.semaphore_signal(barrier, inc=1, device_id=dev, device_id_type=MESH)
    pl.semaphore_wait(barrier, len(peers))


def _call(body, *, name, grid=(), in_specs=(), out_specs=(), out_shape=(), scratch_shapes=(), operands=(), prefetch=(),
          aliases=None, riders=(), sem=None):
    n_pre, n_in, n_out, n_scr = len(prefetch), len(operands), len(out_shape), len(scratch_shapes)
    in_specs, out_specs, out_shape = list(in_specs), list(out_specs), list(out_shape)
    operands, scratch_shapes = list(operands), list(scratch_shapes)
    io_alias = {n_pre + i: o for i, o in (aliases or {}).items()}
    for r in riders:
        base_in, base_out = n_pre + len(operands), len(out_shape)
        operands += list(r.inputs)
        in_specs += [ANY] * len(r.inputs)
        for pos, i in enumerate(r.aliased):
            io_alias[base_in + i] = base_out + pos
            out_shape.append(jax.ShapeDtypeStruct(r.inputs[i].shape, r.inputs[i].dtype))
        out_shape += list(r.fresh)
        out_specs += [ANY] * (len(r.aliased) + len(r.fresh))
        scratch_shapes += [pltpu.SemaphoreType.DMA((r.nsem,)), pltpu.SemaphoreType.DMA((r.nsem,))]

    def wrapped(*refs):
        pre, p = refs[:n_pre], n_pre
        core_in, p = refs[p:p + n_in], p + n_in
        r_in = []
        for r in riders:
            r_in.append(refs[p:p + len(r.inputs)])
            p += len(r.inputs)
        core_out, p = refs[p:p + n_out], p + n_out
        r_out = []
        for r in riders:
            k = len(r.aliased) + len(r.fresh)
            r_out.append(refs[p:p + k])
            p += k
        core_scr, p = refs[p:p + n_scr], p + n_scr
        r_sem = [refs[p + 2 * i:p + 2 * i + 2] for i in range(len(riders))]

        def edge(at_last, fns):
            def run():
                if not at_last:
                    _peer_barrier(peer_kind)
                for i, r in enumerate(riders):
                    getattr(r, fns)(r_in[i], r_out[i], r_sem[i][0], r_sem[i][1])
            if not riders:
                return
            if not grid:
                run()
                return
            cond = None
            for d, n in enumerate(grid):
                c = pl.program_id(d) == (n - 1 if at_last else 0)
                cond = c if cond is None else jnp.logical_and(cond, c)
            pl.when(cond)(run)

        edge(False, "start")
        if body is not None:
            body(*pre, *core_in, *core_out, *core_scr)
        edge(True, "finish")

    if sem is None or riders:
        sem = ("arbitrary",) * len(grid)
    kwargs = dict(out_shape=out_shape, input_output_aliases=io_alias, name=name)
    peer_kind = _peer_kind(riders)
    collective = {} if peer_kind is None else {"collective_id": PEER_KINDS.index(peer_kind)}
    if grid:
        kwargs["compiler_params"] = pltpu.CompilerParams(dimension_semantics=sem, vmem_limit_bytes=VMEM_LIMIT, **collective)
    elif collective:
        kwargs["compiler_params"] = pltpu.CompilerParams(**collective)
    if n_pre:
        kwargs["grid_spec"] = pltpu.PrefetchScalarGridSpec(
            num_scalar_prefetch=n_pre, grid=grid, in_specs=in_specs, out_specs=out_specs, scratch_shapes=scratch_shapes)
    else:
        kwargs.update(grid=grid, in_specs=in_specs, out_specs=out_specs, scratch_shapes=scratch_shapes)
    res = pl.pallas_call(wrapped, **kwargs)(*prefetch, *operands)
    core, rest, rider_res = list(res[:n_out]), list(res[n_out:]), []
    for r in riders:
        k = len(r.aliased) + len(r.fresh)
        rider_res.append(rest[:k])
        rest = rest[k:]
    return core, rider_res


def _mm_call(*, grid, in_specs, out_spec, out_shape, dims, nk, kaxis, acc_shape, name, operands, riders=()):
    out_dtype = out_shape.dtype

    def body(a_ref, b_ref, o_ref, *scratch):
        p = lax.dot_general(a_ref[...].astype(BF16), b_ref[...].astype(BF16), dims, preferred_element_type=F32)
        if nk == 1:
            o_ref[...] = p.astype(out_dtype)
        else:
            acc = scratch[0]
            kk = pl.program_id(kaxis)

            @pl.when(kk == 0)
            def _():
                acc[...] = p

            @pl.when(kk > 0)
            def _():
                acc[...] += p

            @pl.when(kk == nk - 1)
            def _():
                o_ref[...] = acc[...].astype(out_dtype)

    sem = ["parallel"] * len(grid)
    if nk > 1:
        sem[kaxis] = "arbitrary"
    (out,), rider_res = _call(
        body, grid=grid, in_specs=in_specs, out_specs=[out_spec], out_shape=[out_shape],
        scratch_shapes=[pltpu.VMEM(acc_shape, F32)] if nk > 1 else [], operands=operands, name=name, riders=riders,
        sem=tuple(sem))
    return (out, rider_res) if riders else out


def mm_nn(a, w, *, out_dtype, name, tm=512, tn=512, riders=()):
    m, k = a.shape
    tm = _row_tile(m, tm)
    if w.ndim == 3:
        ns = w.shape[2]
        grid = (N_CHIPS, m // tm)
        w_spec = pl.BlockSpec((None, k, ns), lambda j, i: (j, 0, 0))
        o_spec = pl.BlockSpec((tm, ns), lambda j, i: (i, j))
        n = N_CHIPS * ns
    else:
        n = w.shape[1]
        grid = (n // tn, m // tm)
        w_spec = pl.BlockSpec((k, tn), lambda j, i: (0, j))
        o_spec = pl.BlockSpec((tm, tn), lambda j, i: (i, j))
    return _mm_call(grid=grid, in_specs=[pl.BlockSpec((tm, k), lambda j, i: (i, 0)), w_spec], out_spec=o_spec,
                    out_shape=jax.ShapeDtypeStruct((m, n), out_dtype), dims=NN_DIMS, nk=1, kaxis=0, acc_shape=None,
                    name=name, operands=(a, w), riders=riders)


def mm_nt(a, w, *, out_dtype, name, tm=512, tn=512, riders=()):
    if w.ndim == 2:
        m, n = a.shape
        kout = w.shape[0]
        tm = _row_tile(m, tm)
        return _mm_call(grid=(kout // tn, m // tm),
                        in_specs=[pl.BlockSpec((tm, n), lambda j, i: (i, 0)), pl.BlockSpec((tn, n), lambda j, i: (j, 0))],
                        out_spec=pl.BlockSpec((tm, tn), lambda j, i: (i, j)),
                        out_shape=jax.ShapeDtypeStruct((m, kout), out_dtype), dims=NT_DIMS, nk=1, kaxis=0,
                        acc_shape=None, name=name, operands=(a, w), riders=riders)
    _, kout, ns = w.shape
    planes = a.ndim == 3
    m = a.shape[1] if planes else a.shape[0]
    tm = _row_tile(m, tm)
    a_spec = pl.BlockSpec((2, tm, 2 * ns), lambda i: (0, i, 0)) if planes else pl.BlockSpec((tm, N_CHIPS * ns), lambda i: (i, 0))

    def body(a_ref, w0, w1, w2, w3, o_ref):
        acc = None
        for j, w_ref in enumerate((w0, w1, w2, w3)):
            if planes:
                a_j = a_ref[j // 2, :, (j % 2) * ns:(j % 2 + 1) * ns]
            else:
                a_j = a_ref[:, j * ns:(j + 1) * ns]
            p = lax.dot_general(a_j, w_ref[...], NT_DIMS, preferred_element_type=F32)
            acc = p if acc is None else acc + p
        o_ref[...] = acc.astype(out_dtype)

    def shard(j):
        return pl.BlockSpec((None, kout, ns), lambda i: (j, 0, 0))

    (out,), rider_res = _call(
        body, grid=(m // tm,), in_specs=[a_spec] + [shard(j) for j in range(N_CHIPS)],
        out_specs=[pl.BlockSpec((tm, kout), lambda i: (i, 0))], out_shape=[jax.ShapeDtypeStruct((m, kout), out_dtype)],
        operands=(a, w, w, w, w), sem=("parallel",), name=name, riders=riders)
    return (out, rider_res) if riders else out


def mm_tn(a, b, *, shard_major, name, tm, tn, tk=None, out_dtype=BF16, riders=()):
    s, m = a.shape
    tk = s if tk is None else _row_tile(s, tk)
    if b.ndim == 3:
        n = 2 * b.shape[2]
        b_spec = pl.BlockSpec((None, tk, tn), lambda j, i, kk: (j // 2, kk, j % 2))
    else:
        n = b.shape[1]
        b_spec = pl.BlockSpec((tk, tn), lambda j, i, kk: (kk, j))
    if shard_major:
        assert tn == n // N_CHIPS
        o_spec = pl.BlockSpec((None, tm, tn), lambda j, i, kk: (j, i, 0))
        o_shape = jax.ShapeDtypeStruct((N_CHIPS, m, tn), out_dtype)
    else:
        o_spec = pl.BlockSpec((tm, tn), lambda j, i, kk: (i, j))
        o_shape = jax.ShapeDtypeStruct((m, n), out_dtype)
    return _mm_call(grid=(n // tn, m // tm, s // tk),
                    in_specs=[pl.BlockSpec((tk, tm), lambda j, i, kk: (kk, i)), b_spec], out_spec=o_spec,
                    out_shape=o_shape, dims=TN_DIMS, nk=s // tk, kaxis=2, acc_shape=(tm, tn), name=name, operands=(a, b),
                    riders=riders)


def _rstd(x):
    return lax.rsqrt(jnp.mean(x * x, axis=-1, keepdims=True) + EPS)


def _rms_bwd(dy, x, g):
    r = _rstd(x)
    xhat = x * r
    gy = dy * g
    dx = r * (gy - xhat * jnp.mean(gy * xhat, axis=-1, keepdims=True))
    return dx, jnp.sum(dy * xhat, axis=0, keepdims=True)


def _accum(ref, val, first):
    @pl.when(first)
    def _():
        ref[...] = val

    @pl.when(jnp.logical_not(first))
    def _():
        ref[...] += val


def _row_spec(tm, width):
    return pl.BlockSpec((tm, width), lambda i: (i, 0))


def _vec_spec(width):
    return pl.BlockSpec((1, width), lambda i: (0, 0))


def _ret(core, rider_res, riders):
    core = core[0] if len(core) == 1 else core
    return (core, rider_res) if riders else core


def prenorm(x, g, *, name, tm=256, riders=()):
    s = x.shape[0]
    tm = _row_tile(s, tm)

    def body(x_ref, g_ref, h_ref):
        xv = x_ref[...]
        h_ref[...] = (xv * _rstd(xv) * g_ref[...]).astype(BF16)

    core, rr = _call(
        body, grid=(s // tm,), in_specs=[_row_spec(tm, D_MODEL), _vec_spec(D_MODEL)], out_specs=[_row_spec(tm, D_MODEL)],
        out_shape=[jax.ShapeDtypeStruct((s, D_MODEL), BF16)], operands=(x, g), sem=("parallel",), name=name, riders=riders)
    return _ret(core, rr, riders)


def proj_residual_norm(a, w, x, bias, g_post, g_next, *, name, tm=256, riders=()):
    s, k = a.shape
    tm = _row_tile(s, tm)

    def body(a_ref, w_ref, x_ref, b_ref, gp_ref, gn_ref, xo_ref, h_ref, m_ref):
        mv = jnp.dot(a_ref[...], w_ref[...], preferred_element_type=F32) + b_ref[...]
        m_ref[...] = mv.astype(BF16)
        xn = x_ref[...] + mv * _rstd(mv) * gp_ref[...]
        xo_ref[...] = xn
        h_ref[...] = (xn * _rstd(xn) * gn_ref[...]).astype(BF16)

    row, vec = _row_spec(tm, D_MODEL), _vec_spec(D_MODEL)
    core, rr = _call(
        body, grid=(s // tm,),
        in_specs=[_row_spec(tm, k), pl.BlockSpec((k, D_MODEL), lambda i: (0, 0)), row, vec, vec, vec], out_specs=[row, row, row],
        out_shape=[jax.ShapeDtypeStruct((s, D_MODEL), F32), jax.ShapeDtypeStruct((s, D_MODEL), BF16),
                   jax.ShapeDtypeStruct((s, D_MODEL), BF16)],
        operands=(a, w, x, bias, g_post, g_next), sem=("parallel",), name=name, riders=riders)
    return _ret(core, rr, riders)


def proj_loss_head(a, w, x, g_post, target, *, name, tm=256, riders=()):
    s, k = a.shape
    tm = _row_tile(s, tm)

    def body(a_ref, w_ref, x_ref, g_ref, t_ref, dx_ref, df_ref, dg_ref, loss_ref):
        first = pl.program_id(0) == 0
        fv = jnp.dot(a_ref[...], w_ref[...], preferred_element_type=F32)
        g = g_ref[...]
        err = x_ref[...] + fv * _rstd(fv) * g - t_ref[...]
        dx = err * (1.0 / D_MODEL)
        dx_ref[...] = dx
        df, dg = _rms_bwd(dx, fv, g)
        df_ref[...] = df.astype(BF16)
        _accum(dg_ref, dg, first)
        part = jnp.sum(jnp.sum(err * err, axis=-1, keepdims=True), axis=0, keepdims=True) * (0.5 / D_MODEL)
        _accum(loss_ref, jnp.broadcast_to(part, (8, LANES)), first)

    row, vec = _row_spec(tm, D_MODEL), _vec_spec(D_MODEL)
    core, rr = _call(
        body, grid=(s // tm,), in_specs=[_row_spec(tm, k), pl.BlockSpec((k, D_MODEL), lambda i: (0, 0)), row, vec, row],
        out_specs=[row, row, vec, pl.BlockSpec((8, LANES), lambda i: (0, 0))],
        out_shape=[jax.ShapeDtypeStruct((s, D_MODEL), F32), jax.ShapeDtypeStruct((s, D_MODEL), BF16),
                   jax.ShapeDtypeStruct((1, D_MODEL), F32), jax.ShapeDtypeStruct((8, LANES), F32)],
        operands=(a, w, x, g_post, target), name=name, riders=riders)
    return _ret(core, rr, riders)


def dh_norm_bwd_pair(a, w, dres, x, g_pre, m, g_post, *, name, tm=512, sub=256, riders=()):
    _, kout, ns = w.shape
    planes = a.ndim == 3
    s = x.shape[0]
    tm = _row_tile(s, tm)
    sub = min(sub, tm)
    a_spec = pl.BlockSpec((2, tm, 2 * ns), lambda i: (0, i, 0)) if planes else pl.BlockSpec((tm, N_CHIPS * ns), lambda i: (i, 0))

    def body(a_ref, w0, w1, w2, w3, dres_ref, x_ref, gpre_ref, m_ref, gpost_ref, dx_ref, dm_ref, dgpre_ref, dgpost_ref, db_ref):
        first = pl.program_id(0) == 0
        sums = None
        for t in range(tm // sub):
            rows = slice(t * sub, (t + 1) * sub)
            dh = None
            for j, w_ref in enumerate((w0, w1, w2, w3)):
                a_j = a_ref[j // 2, rows, (j % 2) * ns:(j % 2 + 1) * ns] if planes else a_ref[rows, j * ns:(j + 1) * ns]
                p = lax.dot_general(a_j, w_ref[...], NT_DIMS, preferred_element_type=F32)
                dh = p if dh is None else dh + p
            d1, dgpre = _rms_bwd(dh, x_ref[rows, :], gpre_ref[...])
            dx = dres_ref[rows, :] + d1
            dx_ref[rows, :] = dx
            dm, dgpost = _rms_bwd(dx, m_ref[rows, :].astype(F32), gpost_ref[...])
            dm_ref[rows, :] = dm.astype(BF16)
            part = (dgpre, dgpost, jnp.sum(dm, axis=0, keepdims=True))
            sums = part if sums is None else tuple(u + v for u, v in zip(sums, part))
        _accum(dgpre_ref, sums[0], first)
        _accum(dgpost_ref, sums[1], first)
        _accum(db_ref, sums[2], first)

    def shard(j):
        return pl.BlockSpec((None, kout, ns), lambda i: (j, 0, 0))

    row, vec = _row_spec(tm, D_MODEL), _vec_spec(D_MODEL)
    vshape = jax.ShapeDtypeStruct((1, D_MODEL), F32)
    core, rr = _call(
        body, grid=(s // tm,), in_specs=[a_spec] + [shard(j) for j in range(N_CHIPS)] + [row, row, vec, row, vec],
        out_specs=[row, row, vec, vec, vec],
        out_shape=[jax.ShapeDtypeStruct((s, D_MODEL), F32), jax.ShapeDtypeStruct((s, D_MODEL), BF16), vshape, vshape, vshape],
        operands=(a, w, w, w, w, dres, x, g_pre, m, g_post), name=name, riders=riders)
    return _ret(core, rr, riders)


def norm_bwd_last(dres, dh, x, g_pre, *, name, tm=256, riders=()):
    s = x.shape[0]
    tm = _row_tile(s, tm)

    def body(dres_ref, dh_ref, x_ref, g_ref, dx_ref, dg_ref):
        d1, dg = _rms_bwd(dh_ref[...], x_ref[...], g_ref[...])
        dx_ref[...] = dres_ref[...] + d1
        _accum(dg_ref, dg, pl.program_id(0) == 0)

    row, vec = _row_spec(tm, D_MODEL), _vec_spec(D_MODEL)
    core, rr = _call(
        body, grid=(s // tm,), in_specs=[row, row, row, vec], out_specs=[row, vec],
        out_shape=[jax.ShapeDtypeStruct((s, D_MODEL), F32), jax.ShapeDtypeStruct((1, D_MODEL), F32)],
        operands=(dres, dh, x, g_pre), name=name, riders=riders)
    return _ret(core, rr, riders)


def _rope_tables(s):
    half = HEAD_DIM // 2
    inv_freq = np.float32(ROPE_THETA) ** (-(np.arange(half, dtype=np.float32) * np.float32(2.0)) / np.float32(HEAD_DIM))
    ang = np.arange(s, dtype=np.float32)[:, None] * inv_freq[None, :]
    cos, sin = np.cos(ang).astype(np.float32), np.sin(ang).astype(np.float32)
    return jnp.asarray(np.tile(cos, (1, 4))), jnp.asarray(np.concatenate([-sin, sin, -sin, sin], axis=1))


def _swap_halves(x):
    lane = lax.broadcasted_iota(I32, x.shape, 1)
    return jnp.where((lane & (HEAD_DIM - 1)) < HEAD_DIM // 2, pltpu.roll(x, LANES - 32, 1), pltpu.roll(x, 32, 1))


N_ROPE_BLOCKS = (Q_WIDTH + KV_WIDTH) // LANES


def qkv_proj(h, w, bias, cos, sin, *, name, tm=512, riders=()):
    s, k = h.shape
    ns = w.shape[2]
    tm = _row_tile(s, tm)

    def body(h_ref, w_ref, b_ref, c_ref, s_ref, o_ref):
        j = pl.program_id(0)
        sub = min(256, tm)
        for t in range(tm // sub):
            rows = slice(t * sub, (t + 1) * sub)
            p = jnp.dot(h_ref[rows, :], w_ref[...], preferred_element_type=F32) + b_ref[...]
            cosv, sinv = c_ref[rows, :], s_ref[rows, :]
            for blk in range(ns // LANES):
                xb = p[:, blk * LANES:(blk + 1) * LANES]
                roped = xb * cosv + _swap_halves(xb) * sinv
                is_qk = j * (ns // LANES) + blk < N_ROPE_BLOCKS
                o_ref[rows, blk * LANES:(blk + 1) * LANES] = jnp.where(is_qk, roped, xb).astype(BF16)

    core, rr = _call(
        body, grid=(N_CHIPS, s // tm),
        in_specs=[pl.BlockSpec((tm, k), lambda j, i: (i, 0)), pl.BlockSpec((None, k, ns), lambda j, i: (j, 0, 0)),
                  pl.BlockSpec((1, ns), lambda j, i: (0, j)), pl.BlockSpec((tm, LANES), lambda j, i: (i, 0)),
                  pl.BlockSpec((tm, LANES), lambda j, i: (i, 0))],
        out_specs=[pl.BlockSpec((tm, ns), lambda j, i: (i, j))], out_shape=[jax.ShapeDtypeStruct((s, N_CHIPS * ns), BF16)],
        operands=(h, w, bias, cos, sin), sem=("parallel", "parallel"), name=name, riders=riders)
    return _ret(core, rr, riders)


def rope_bwd(dq, dkc, dkp, dvc, dvp, cos, sin, *, name, riders=()):
    s = dq.shape[0]
    tm = 2 * WINDOW if s % (2 * WINDOW) == 0 else WINDOW
    nb = s // tm

    def body(dq_ref, dkc_ref, dkp_ref, dkp_next_ref, dvc_ref, dvp_ref, dvp_next_ref, c_ref, s_ref, o_ref, db_ref):
        i = pl.program_id(0)
        has_next = (i < nb - 1).astype(F32)
        cosv, sinv = c_ref[...], s_ref[...]

        def shifted(ref, next_ref, cols):
            last = has_next * next_ref[:WINDOW, cols].astype(F32)
            return last if tm == WINDOW else jnp.concatenate([ref[WINDOW:, cols].astype(F32), last], axis=0)

        parts = []
        for blk in range(QKV_WIDTH // LANES):
            if blk < Q_WIDTH // LANES:
                g = dq_ref[:, blk * LANES:(blk + 1) * LANES].astype(F32)
            else:
                own, prv, nxt = (dkc_ref, dkp_ref, dkp_next_ref) if blk < N_ROPE_BLOCKS else (dvc_ref, dvp_ref, dvp_next_ref)
                cols = slice((blk % 2) * LANES, (blk % 2 + 1) * LANES)
                g = own[:, cols].astype(F32) + shifted(prv, nxt, cols)
            if blk < N_ROPE_BLOCKS:
                g = g * cosv + _swap_halves(g * sinv)
            o_ref[:, blk * LANES:(blk + 1) * LANES] = g.astype(BF16)
            parts.append(jnp.sum(g, axis=0, keepdims=True))
        sums = jnp.concatenate(parts, axis=1)
        _accum(db_ref, sums, i == 0)

    own_spec = _row_spec(tm, KV_WIDTH)
    next_spec = pl.BlockSpec((tm, KV_WIDTH), lambda i: (jnp.minimum(i + 1, nb - 1), 0))
    core, rr = _call(
        body, grid=(nb,),
        in_specs=[_row_spec(tm, Q_WIDTH), own_spec, own_spec, next_spec, own_spec, own_spec, next_spec,
                  _row_spec(tm, LANES), _row_spec(tm, LANES)],
        out_specs=[_row_spec(tm, QKV_WIDTH), _vec_spec(QKV_WIDTH)],
        out_shape=[jax.ShapeDtypeStruct((s, QKV_WIDTH), BF16), jax.ShapeDtypeStruct((1, QKV_WIDTH), F32)],
        operands=(dq, dkc, dkp, dkp, dvc, dvp, dvp, cos, sin), name=name, riders=riders)
    return _ret(core, rr, riders)


ROWS = GQA_GROUP * WINDOW


def _prev_slots():
    kpos = lax.broadcasted_iota(I32, (WINDOW, ROWS), 0)
    qpos = lax.broadcasted_iota(I32, (WINDOW, ROWS), 1) & (WINDOW - 1)
    return kpos > qpos


def _head_cols(ref, head):
    return ref[:, head * HEAD_DIM:(head + 1) * HEAD_DIM]


def _stack_heads(ref, h):
    return jnp.concatenate([_head_cols(ref, GQA_GROUP * h + g) for g in range(GQA_GROUP)], axis=0)


def _band(prev_ref, cur_ref, h):
    return jnp.concatenate([_head_cols(prev_ref, h), _head_cols(cur_ref, h)], axis=0)


def _pick(prev, band):
    return jnp.where(prev, band[:WINDOW], band[WINDOW:])


def _spread(prev, x):
    return jnp.concatenate([jnp.where(prev, x, 0.0), jnp.where(prev, 0.0, x)], axis=0).astype(BF16)


def _attn_probs(q, kband, sink, prev, has_prev):
    scale = HEAD_DIM ** -0.5
    s_band = lax.dot_general(kband, q, NT_DIMS, preferred_element_type=F32)
    s = jnp.where(prev, jnp.where(has_prev, s_band[:WINDOW], NEG), s_band[WINDOW:]) * scale
    m = jnp.maximum(jnp.max(s, axis=0, keepdims=True), sink)
    e, es = jnp.exp(s - m), jnp.exp(sink - m)
    inv = 1.0 / (jnp.sum(e, axis=0, keepdims=True) + es)
    return e * inv, es * inv


def _attn_specs(nb):
    kcol, vcol = Q_WIDTH // KV_WIDTH, Q_WIDTH // KV_WIDTH + 1
    q_spec = pl.BlockSpec((WINDOW, Q_WIDTH), lambda n: (n, 0))
    return [q_spec,
            pl.BlockSpec((WINDOW, KV_WIDTH), lambda n: (n, kcol)),
            pl.BlockSpec((WINDOW, KV_WIDTH), lambda n: (jnp.maximum(n - 1, 0), kcol)),
            pl.BlockSpec((WINDOW, KV_WIDTH), lambda n: (n, vcol)),
            pl.BlockSpec((WINDOW, KV_WIDTH), lambda n: (jnp.maximum(n - 1, 0), vcol)),
            pl.BlockSpec((N_KV_HEADS, 8, ROWS), lambda n: (0, 0, 0))]


def attn_fwd(qkv, sink_rows, *, name, riders=()):
    s = qkv.shape[0]

    def body(q_ref, kc_ref, kp_ref, vc_ref, vp_ref, sink_ref, o_ref):
        prev = _prev_slots()
        has_prev = pl.program_id(0) > 0
        for h in range(N_KV_HEADS):
            p, _ = _attn_probs(_stack_heads(q_ref, h), _band(kp_ref, kc_ref, h), sink_ref[h, 0:1, :], prev, has_prev)
            o = lax.dot_general(_band(vp_ref, vc_ref, h), _spread(prev, p), TN_DIMS, preferred_element_type=F32).T
            for g in range(GQA_GROUP):
                head = GQA_GROUP * h + g
                o_ref[:, head * HEAD_DIM:(head + 1) * HEAD_DIM] = o[g * WINDOW:(g + 1) * WINDOW].astype(BF16)

    core, rr = _call(
        body, grid=(s // WINDOW,), in_specs=_attn_specs(s // WINDOW), out_specs=[pl.BlockSpec((WINDOW, Q_WIDTH), lambda n: (n, 0))],
        out_shape=[jax.ShapeDtypeStruct((s, Q_WIDTH), BF16)], operands=(qkv, qkv, qkv, qkv, qkv, sink_rows), sem=("parallel",),
        name=name, riders=riders)
    return _ret(core, rr, riders)


def attn_bwd(qkv, sink_rows, do, *, name, riders=()):
    s = qkv.shape[0]

    def body(q_ref, kc_ref, kp_ref, vc_ref, vp_ref, sink_ref, do_ref, dq_ref, dkc_ref, dkp_ref, dvc_ref, dvp_ref, dsink_ref):
        n = pl.program_id(0)
        prev = _prev_slots()
        scale = HEAD_DIM ** -0.5
        parts = []
        for h in range(N_KV_HEADS):
            qv, dov = _stack_heads(q_ref, h), _stack_heads(do_ref, h)
            kband, vband = _band(kp_ref, kc_ref, h), _band(vp_ref, vc_ref, h)
            p, ps = _attn_probs(qv, kband, sink_ref[h, 0:1, :], prev, n > 0)
            dp = _pick(prev, lax.dot_general(vband, dov, NT_DIMS, preferred_element_type=F32))
            delta = jnp.sum(p * dp, axis=0, keepdims=True)
            ds_band = _spread(prev, p * (dp - delta) * scale)
            p_band = _spread(prev, p)
            dk = jnp.dot(ds_band, qv, preferred_element_type=F32).astype(BF16)
            dv = jnp.dot(p_band, dov, preferred_element_type=F32).astype(BF16)
            dq = lax.dot_general(kband, ds_band, TN_DIMS, preferred_element_type=F32).T
            cols = slice(h * HEAD_DIM, (h + 1) * HEAD_DIM)
            dkp_ref[:, cols], dkc_ref[:, cols] = dk[:WINDOW], dk[WINDOW:]
            dvp_ref[:, cols], dvc_ref[:, cols] = dv[:WINDOW], dv[WINDOW:]
            dsink = -(ps * delta)
            for g in range(GQA_GROUP):
                head = GQA_GROUP * h + g
                dq_ref[:, head * HEAD_DIM:(head + 1) * HEAD_DIM] = dq[g * WINDOW:(g + 1) * WINDOW].astype(BF16)
                parts.append(jnp.broadcast_to(jnp.sum(dsink[:, g * WINDOW:(g + 1) * WINDOW], axis=1, keepdims=True), (8, LANES)))

        @pl.when(n == 0)
        def _():
            for i, part in enumerate(parts):
                dsink_ref[i // GQA_GROUP, i % GQA_GROUP] = part

        @pl.when(n > 0)
        def _():
            for i, part in enumerate(parts):
                dsink_ref[i // GQA_GROUP, i % GQA_GROUP] += part

    rows_q = pl.BlockSpec((WINDOW, Q_WIDTH), lambda n: (n, 0))
    rows_kv = pl.BlockSpec((WINDOW, KV_WIDTH), lambda n: (n, 0))
    kv_shape = jax.ShapeDtypeStruct((s, KV_WIDTH), BF16)
    core, rr = _call(
        body, grid=(s // WINDOW,), in_specs=_attn_specs(s // WINDOW) + [rows_q],
        out_specs=[rows_q, rows_kv, rows_kv, rows_kv, rows_kv,
                   pl.BlockSpec((N_KV_HEADS, GQA_GROUP, 8, LANES), lambda n: (0, 0, 0, 0))],
        out_shape=[jax.ShapeDtypeStruct((s, Q_WIDTH), BF16), kv_shape, kv_shape, kv_shape, kv_shape,
                   jax.ShapeDtypeStruct((N_KV_HEADS, GQA_GROUP, 8, LANES), F32)],
        operands=(qkv, qkv, qkv, qkv, qkv, sink_rows, do), sem=("arbitrary",), name=name, riders=riders)
    return _ret(core, rr, riders)


GELU_C = 0.7978845608028654
GELU_A = 0.044715


def _gelu(x):
    return 0.5 * x * (1.0 + jnp.tanh(x * (GELU_C + (GELU_C * GELU_A) * (x * x))))


def _gelu_and_grad(x):
    x2 = x * x
    t = jnp.tanh(x * (GELU_C + (GELU_C * GELU_A) * x2))
    half_x, one_t = 0.5 * x, 1.0 + t
    return half_x * one_t, 0.5 * one_t + half_x * (1.0 - t * t) * (GELU_C + (3.0 * GELU_C * GELU_A) * x2)


def _tril_bf16(w):
    row = lax.broadcasted_iota(I32, (SGU_CHUNK, SGU_CHUNK), 0)
    col = lax.broadcasted_iota(I32, (SGU_CHUNK, SGU_CHUNK), 1)
    return jnp.where(row >= col, w, 0.0).astype(BF16)


def _sgu_norm(vg, g, b):
    mu = jnp.mean(vg, axis=-1, keepdims=True)
    cen = vg - mu
    rstd = lax.rsqrt(jnp.mean(cen * cen, axis=-1, keepdims=True) + EPS)
    xhat = cen * rstd
    return xhat, rstd, xhat * g + b


def sgu_in_fwd(h, w_in, ln_g, ln_b, w_sp, b_sp, *, name, tm=256, riders=()):
    s, k = h.shape
    ns = w_in.shape[2]
    tm = _row_tile(s, tm)

    def body(h_ref, w0, w1, w2, w3, g_ref, b_ref, w_ref, bs_ref, z_ref, y_ref):
        hv = h_ref[...]
        zs = [jnp.dot(hv, w_ref_j[...], preferred_element_type=F32) for w_ref_j in (w0, w1, w2, w3)]
        for j, zj in enumerate(zs):
            z_ref[:, j * ns:(j + 1) * ns] = zj.astype(BF16)
        u = _gelu(jnp.concatenate(zs[:2], axis=1))
        _, _, vn = _sgu_norm(_gelu(jnp.concatenate(zs[2:], axis=1)), g_ref[...], b_ref[...])
        vn = vn.astype(BF16)
        for grp in range(SGU_GROUPS):
            w = _tril_bf16(w_ref[grp])
            cols = slice(grp * LANES, (grp + 1) * LANES)
            for ch in range(tm // SGU_CHUNK):
                rows = slice(ch * SGU_CHUNK, (ch + 1) * SGU_CHUNK)
                mixed = jnp.dot(w, vn[rows, cols], preferred_element_type=F32) + bs_ref[grp]
                y_ref[rows, cols] = (u[rows, cols] * mixed).astype(BF16)

    def shard(j):
        return pl.BlockSpec((None, k, ns), lambda i: (j, 0, 0))

    full3 = pl.BlockSpec((SGU_GROUPS, SGU_CHUNK, SGU_CHUNK), lambda i: (0, 0, 0))
    core, rr = _call(
        body, grid=(s // tm,),
        in_specs=[_row_spec(tm, k)] + [shard(j) for j in range(N_CHIPS)] + [_vec_spec(D_MODEL), _vec_spec(D_MODEL), full3, full3],
        out_specs=[_row_spec(tm, 2 * D_MODEL), _row_spec(tm, D_MODEL)],
        out_shape=[jax.ShapeDtypeStruct((s, 2 * D_MODEL), BF16), jax.ShapeDtypeStruct((s, D_MODEL), BF16)],
        operands=(h, w_in, w_in, w_in, w_in, ln_g, ln_b, w_sp, b_sp), sem=("parallel",), name=name, riders=riders)
    return _ret(core, rr, riders)


def sgu_bwd(z, dy, ln_g, ln_b, w_sp, b_sp, *, name, tm=256, riders=()):
    s = z.shape[0]
    tm = _row_tile(s, tm)

    def body(z_ref, dy_ref, g_ref, b_ref, w_ref, bs_ref, dz_ref, dw_ref, dbs_ref, dg_ref, db_ref, dvn_buf):
        first = pl.program_id(0) == 0
        u, u_grad = _gelu_and_grad(z_ref[:, :D_MODEL].astype(F32))
        vg, v_grad = _gelu_and_grad(z_ref[:, D_MODEL:].astype(F32))
        xhat, rstd, vn = _sgu_norm(vg, g_ref[...], b_ref[...])
        vn = vn.astype(BF16)
        dyv = dy_ref[...]
        dmixed = dyv * u
        dz_gate = dyv * u_grad
        row = lax.broadcasted_iota(I32, (SGU_CHUNK, SGU_CHUNK), 0)
        col = lax.broadcasted_iota(I32, (SGU_CHUNK, SGU_CHUNK), 1)
        dws, dbss = [], []
        for grp in range(SGU_GROUPS):
            w = _tril_bf16(w_ref[grp])
            cols = slice(grp * LANES, (grp + 1) * LANES)
            dw = jnp.zeros((SGU_CHUNK, SGU_CHUNK), F32)
            dbs = jnp.zeros((SGU_CHUNK, 1), F32)
            for ch in range(tm // SGU_CHUNK):
                rows = slice(ch * SGU_CHUNK, (ch + 1) * SGU_CHUNK)
                vblk = vn[rows, cols]
                mixed = jnp.dot(w, vblk, preferred_element_type=F32) + bs_ref[grp]
                dz_ref[rows, cols] = (dz_gate[rows, cols] * mixed).astype(BF16)
                dm = dmixed[rows, cols]
                dmb = dm.astype(BF16)
                dvn_buf[rows, cols] = lax.dot_general(w, dmb, TN_DIMS, preferred_element_type=F32)
                dw += lax.dot_general(dmb, vblk, NT_DIMS, preferred_element_type=F32)
                dbs += jnp.sum(dm, axis=-1, keepdims=True)
            dws.append(jnp.where(row >= col, dw, 0.0))
            dbss.append(jnp.broadcast_to(dbs, (SGU_CHUNK, SGU_CHUNK)))

        dvn = dvn_buf[...]
        dxhat = dvn * g_ref[...]
        dvg = rstd * (dxhat - jnp.mean(dxhat, axis=-1, keepdims=True) - xhat * jnp.mean(dxhat * xhat, axis=-1, keepdims=True))
        dz_ref[:, D_MODEL:] = (dvg * v_grad).astype(BF16)
        dlng, dlnb = jnp.sum(dvn * xhat, axis=0, keepdims=True), jnp.sum(dvn, axis=0, keepdims=True)

        @pl.when(first)
        def _():
            for grp in range(SGU_GROUPS):
                dw_ref[grp] = dws[grp]
                dbs_ref[grp] = dbss[grp]
            dg_ref[...] = dlng
            db_ref[...] = dlnb

        @pl.when(jnp.logical_not(first))
        def _():
            for grp in range(SGU_GROUPS):
                dw_ref[grp] += dws[grp]
                dbs_ref[grp] += dbss[grp]
            dg_ref[...] += dlng
            db_ref[...] += dlnb

    full3 = pl.BlockSpec((SGU_GROUPS, SGU_CHUNK, SGU_CHUNK), lambda i: (0, 0, 0))
    s3 = jax.ShapeDtypeStruct((SGU_GROUPS, SGU_CHUNK, SGU_CHUNK), F32)
    vshape = jax.ShapeDtypeStruct((1, D_MODEL), F32)
    core, rr = _call(
        body, grid=(s // tm,),
        in_specs=[_row_spec(tm, 2 * D_MODEL), _row_spec(tm, D_MODEL), _vec_spec(D_MODEL), _vec_spec(D_MODEL), full3, full3],
        out_specs=[_row_spec(tm, 2 * D_MODEL), full3, full3, _vec_spec(D_MODEL), _vec_spec(D_MODEL)],
        out_shape=[jax.ShapeDtypeStruct((s, 2 * D_MODEL), BF16), s3, s3, vshape, vshape],
        scratch_shapes=[pltpu.VMEM((tm, D_MODEL), F32)], operands=(z, dy, ln_g, ln_b, w_sp, b_sp), name=name, riders=riders)
    return _ret(core, rr, riders)


def _sigmoid(x):
    return 1.0 / (1.0 + jnp.exp(-x))


def ffn_up(h, w_gu, *, name, tm=512, riders=()):
    s = h.shape[0]
    tm = _row_tile(s, tm)

    def body(h_ref, wg_ref, wu_ref, d_ref, a_ref):
        hv = h_ref[...]
        sub = min(256, tm)
        for t in range(tm // sub):
            rows = slice(t * sub, (t + 1) * sub)
            g = jnp.dot(hv[rows], wg_ref[...], preferred_element_type=F32)
            u = jnp.dot(hv[rows], wu_ref[...], preferred_element_type=F32)
            sig = _sigmoid(g)
            silu = g * sig
            d_ref[0, rows, :] = (u * (sig + silu * (1.0 - sig))).astype(BF16)
            d_ref[1, rows, :] = silu.astype(BF16)
            a_ref[rows, :] = (silu * u).astype(BF16)

    core, rr = _call(
        body, grid=(2, s // tm),
        in_specs=[pl.BlockSpec((tm, D_MODEL), lambda j, i: (i, 0)),
                  pl.BlockSpec((None, D_MODEL, FF_HALF), lambda j, i: (j, 0, 0)),
                  pl.BlockSpec((None, D_MODEL, FF_HALF), lambda j, i: (j + 2, 0, 0))],
        out_specs=[pl.BlockSpec((2, tm, FF_HALF), lambda j, i: (0, i, j)), pl.BlockSpec((tm, FF_HALF), lambda j, i: (i, j))],
        out_shape=[jax.ShapeDtypeStruct((2, s, D_FF), BF16), jax.ShapeDtypeStruct((s, D_FF), BF16)],
        operands=(h, w_gu, w_gu), sem=("parallel", "parallel"), name=name, riders=riders)
    return _ret(core, rr, riders)


def ffn_dact(df, w_d, gu, *, name, tm=512, riders=()):
    s = df.shape[0]
    tm = _row_tile(s, tm)

    def body(df_ref, w_ref, d_ref, o_ref):
        da = lax.dot_general(df_ref[...], w_ref[...], NT_DIMS, preferred_element_type=F32)
        o_ref[0] = (da * d_ref[0].astype(F32)).astype(BF16)
        o_ref[1] = (da * d_ref[1].astype(F32)).astype(BF16)

    planes = pl.BlockSpec((2, tm, FF_HALF), lambda j, i: (0, i, j))
    core, rr = _call(
        body, grid=(2, s // tm),
        in_specs=[pl.BlockSpec((tm, D_MODEL), lambda j, i: (i, 0)), pl.BlockSpec((FF_HALF, D_MODEL), lambda j, i: (j, 0)), planes],
        out_specs=[planes], out_shape=[jax.ShapeDtypeStruct((2, s, D_FF), BF16)], operands=(df, w_d, gu),
        sem=("parallel", "parallel"), name=name, riders=riders)
    return _ret(core, rr, riders)


def _weight_tile(rows):
    for tr in (512, 352, 256, 128):
        if rows % tr == 0:
            return tr
    return rows


def place_shard(w, layer, chip_arr, dtype, *, name, riders=()):
    _, r, c = w.shape
    tr = _weight_tile(r)

    def body(chip_ref, w_ref, o_ref):
        o_ref[...] = w_ref[...].astype(dtype)

    core, rr = _call(
        body, grid=(r // tr,), prefetch=(chip_arr,),
        in_specs=[pl.BlockSpec((None, tr, c), lambda i, chip: (layer, i, 0))],
        out_specs=[pl.BlockSpec((None, tr, c), lambda i, chip: (chip[0], i, 0))],
        out_shape=[jax.ShapeDtypeStruct((N_CHIPS, r, c), dtype)], operands=(w,), sem=("parallel",), name=name, riders=riders)
    return _ret(core, rr, riders)


def _adamw_math(w, g, m, v):
    m = ADAM_B1 * m + (1.0 - ADAM_B1) * g
    v = ADAM_B2 * v + (1.0 - ADAM_B2) * (g * g)
    m_hat = m / (1.0 - ADAM_B1 ** ADAM_STEP)
    v_hat = v / (1.0 - ADAM_B2 ** ADAM_STEP)
    delta = -ADAM_LR * (m_hat / (jnp.sqrt(v_hat) + ADAM_EPS) + ADAM_WD * w)
    return delta, m, v


def adamw(w, g, m, v, *, name):
    nl, r, c = w.shape
    tr = _weight_tile(r)

    def body(w_ref, g_ref, m_ref, v_ref, go_ref, d_ref, mo_ref, vo_ref):
        gv = g_ref[...]
        go_ref[...] = gv
        d_ref[...], mo_ref[...], vo_ref[...] = _adamw_math(w_ref[...], gv, m_ref[...], v_ref[...])

    spec = pl.BlockSpec((None, tr, c), lambda l, i: (l, i, 0))
    shape = jax.ShapeDtypeStruct(w.shape, F32)
    outs, _ = _call(body, grid=(nl, r // tr), in_specs=[spec] * 4, out_specs=[spec] * 4, out_shape=[shape] * 4,
                    operands=(w, g, m, v), sem=("parallel", "parallel"), name=name)
    return outs


def adamw_small(ws, gs, ms, vs, *, name):
    n = len(ws)

    def body(*refs):
        ins, outs = refs[:4 * n], refs[4 * n:]
        for t in range(n):
            gv = ins[n + t][...]
            outs[t][...] = gv
            outs[n + t][...], outs[2 * n + t][...], outs[3 * n + t][...] = _adamw_math(
                ins[t][...], gv, ins[2 * n + t][...], ins[3 * n + t][...])

    shapes = [jax.ShapeDtypeStruct(w.shape, F32) for w in ws]
    res = pl.pallas_call(body, out_shape=shapes * 4, name=name)(*ws, *gs, *ms, *vs)
    return res[:n], res[n:2 * n], res[2 * n:3 * n], res[3 * n:]


def pair_add(g, r1, c_arr, *, name):
    _, rows, cdim = g.shape
    h = rows // 2

    def body(c_ref, g_ref, r_ref, o_ref):
        o_ref[...] = (g_ref[...].astype(F32) + r_ref[...].astype(F32)).astype(o_ref.dtype)

    (out,), _ = _call(
        body, grid=(N_CHIPS,), prefetch=(c_arr,),
        in_specs=[pl.BlockSpec((None, h, cdim), lambda s, c: (s, c[0], 0)), pl.BlockSpec((None, h, cdim), lambda s, c: (s, 0, 0))],
        out_specs=[pl.BlockSpec((None, h, cdim), lambda s, c: (s, 0, 0))],
        out_shape=[jax.ShapeDtypeStruct((N_CHIPS, h, cdim), g.dtype)], operands=(g, r1), sem=("parallel",), name=name)
    return out


def final_add(g, r1, r2, jc_arr, *, dest_shape, lead, prev, name):
    _, rows, cdim = g.shape
    h = rows // 2

    def body(jc_ref, g_ref, r1_ref, r2_ref, *rest):
        o_ref = rest[-1]
        acc = g_ref[...].astype(F32) + r1_ref[...].astype(F32)
        for k in range(3):
            acc = acc + r2_ref[k].astype(F32)
        o_ref[...] = acc

    if lead is None:
        o_spec = pl.BlockSpec((h, cdim), lambda i, jc: (jc[1], 0))
    elif lead == "chip":
        o_spec = pl.BlockSpec((None, h, cdim), lambda i, jc: (jc[0], jc[1], 0))
    else:
        o_spec = pl.BlockSpec((None, h, cdim), lambda i, jc: (lead, jc[1], 0))
    in_specs = [pl.BlockSpec((None, h, cdim), lambda i, jc: (jc[0], jc[1], 0)),
                pl.BlockSpec((None, h, cdim), lambda i, jc: (jc[0], 0, 0)),
                pl.BlockSpec((3, h, cdim), lambda i, jc: (0, 0, 0))]
    operands = [g, r1, r2]
    aliases = None
    if prev is not None:
        in_specs.append(ANY)
        operands.append(prev)
        aliases = {3: 0}
    (out,), _ = _call(body, grid=(1,), prefetch=(jc_arr,), in_specs=in_specs, out_specs=[o_spec],
                      out_shape=[jax.ShapeDtypeStruct(dest_shape, F32)], operands=operands, aliases=aliases, name=name)
    return out


def _place():
    return lax.axis_index("x"), lax.axis_index("y"), lax.axis_index("c")


def _partner(x, y, k):
    return (1 - x if k >> 1 else x), (1 - y if k & 1 else y)


WHOLE = (0, 1, 1)


def _half(rows, sel, dtype, piece=WHOLE):
    lo, hi, n = piece
    align = 16 if dtype == BF16 else 8
    step = rows // 2 // n
    assert rows // 2 == step * n and step % align == 0
    return pl.ds(pl.multiple_of(sel * (rows // 2) + lo * step, align), (hi - lo) * step)


def _rider(peers, inputs, aliased, fresh, nsem, copies, arrivals):
    def start(ins, outs, send, recv):
        for cp in copies(ins, outs, send, recv):
            cp.start()

    def finish(ins, outs, send, recv):
        for cp in arrivals(ins, outs, send, recv):
            cp.wait_recv()
        for cp in copies(ins, outs, send, recv):
            cp.wait_send()

    return types.SimpleNamespace(peers=peers, inputs=list(inputs), aliased=list(aliased), fresh=list(fresh), nsem=nsem,
                                 start=start, finish=finish)


def _remote(src, dst, send, recv, idx, dev):
    return pltpu.make_async_remote_copy(src_ref=src, dst_ref=dst, send_sem=send.at[idx], recv_sem=recv.at[idx],
                                        device_id=dev, device_id_type=MESH)


def gather_ici_rider(fulls, pieces=None):
    nt = len(fulls)
    pieces = pieces or [WHOLE] * nt

    def region(outs, t, slot, sel):
        return outs[t].at[slot, _half(fulls[t].shape[1], sel, fulls[t].dtype, pieces[t])]

    def copies(ins, outs, send, recv):
        x, y, c = _place()
        res = []
        for t in range(nt):
            for k in (1, 2, 3):
                px, py = _partner(x, y, k)
                mine = region(outs, t, 2 * x + y, c)
                res.append(_remote(mine, mine, send, recv, 3 * t + k - 1, (px, py, c)))
        return res

    def arrivals(ins, outs, send, recv):
        x, y, c = _place()
        res = []
        for t in range(nt):
            for k in (1, 2, 3):
                px, py = _partner(x, y, k)
                theirs = region(outs, t, 2 * px + py, c)
                res.append(_remote(theirs, theirs, send, recv, 3 * t + k - 1, (x, y, c)))
        return res

    return _rider("chips", fulls, range(nt), [], 3 * nt, copies, arrivals)


def gather_d2d_rider(fulls, pieces=None):
    nt = len(fulls)
    pieces = pieces or [WHOLE] * nt

    def region(outs, t, slot, sel):
        return outs[t].at[slot, _half(fulls[t].shape[1], sel, fulls[t].dtype, pieces[t])]

    def both(outs, send, recv, mine):
        x, y, c = _place()
        res = []
        for t in range(nt):
            for k in (1, 2, 3):
                px, py = _partner(x, y, k)
                part = region(outs, t, 2 * px + py, c if mine else 1 - c)
                res.append(_remote(part, part, send, recv, 3 * t + k - 1, (x, y, 1 - c)))
        return res

    return _rider("sibling", fulls, range(nt), [], 3 * nt, lambda i, o, s, r: both(o, s, r, True),
                  lambda i, o, s, r: both(o, s, r, False))


def exchange_rider(grads):
    nt = len(grads)

    def both(ins, outs, send, recv):
        x, y, c = _place()
        return [_remote(ins[t].at[:, _half(grads[t].shape[1], 1 - c, grads[t].dtype)], outs[t], send, recv, t, (x, y, 1 - c))
                for t in range(nt)]

    fresh = [jax.ShapeDtypeStruct((N_CHIPS, g.shape[1] // 2, g.shape[2]), g.dtype) for g in grads]
    return _rider("sibling", grads, [], fresh, nt, both, both)


def scatter_rider(parts):
    nt = len(parts)

    def both(ins, outs, send, recv):
        x, y, c = _place()
        res = []
        for t in range(nt):
            for k in (1, 2, 3):
                px, py = _partner(x, y, k)
                res.append(_remote(ins[t].at[2 * px + py], outs[t].at[k - 1], send, recv, 3 * t + k - 1, (px, py, c)))
        return res

    fresh = [jax.ShapeDtypeStruct((3,) + p.shape[1:], p.dtype) for p in parts]
    return _rider("chips", parts, [], fresh, 3 * nt, both, both)


def broadcast_rider(bufs, items):
    def region(outs, item, sel):
        bi, lead = item
        ref = outs[bi]
        if lead == "chip":
            x, y, _ = _place()
            ref = ref.at[2 * x + y]
        elif lead is not None:
            ref = ref.at[lead]
        return ref.at[_half(ref.shape[0], sel, F32)]

    def both(outs, send, recv, mine):
        x, y, c = _place()
        res = []
        for i, item in enumerate(items):
            part = region(outs, item, c if mine else 1 - c)
            res.append(_remote(part, part, send, recv, i, (x, y, 1 - c)))
        return res

    return _rider("sibling", bufs, range(len(bufs)), [], len(items), lambda i, o, s, r: both(o, s, r, True),
                  lambda i, o, s, r: both(o, s, r, False))


def allcast_rider(buf):
    peers = [(k, flip) for k in range(N_CHIPS) for flip in (0, 1) if (k, flip) != (0, 0)]

    def both(outs, send, recv, mine):
        x, y, c = _place()
        res = []
        for i, (k, flip) in enumerate(peers):
            px, py = _partner(x, y, k)
            pc = 1 - c if flip else c
            slot, sel = (2 * x + y, c) if mine else (2 * px + py, pc)
            part = outs[0].at[slot, _half(buf.shape[1], sel, F32)]
            res.append(_remote(part, part, send, recv, i, (px, py, pc)))
        return res

    return _rider("everyone", [buf], [0], [], len(peers), lambda i, o, s, r: both(o, s, r, True),
                  lambda i, o, s, r: both(o, s, r, False))


def comm_call(riders, *, name):
    _, res = _call(None, riders=riders, name=name)
    return res


SLAB_ROWS = 192


def _pad_rows(a, rows=8):
    return jnp.pad(a, ((0, rows - a.shape[0]), (0, 0)))


def _pack_small(norm_grads, db_qkv, db_o, dsinks, db_sp, dln_g, dln_b, dw_sp, loss_part):
    parts = [
        jnp.concatenate(norm_grads, axis=0),
        _pad_rows(jnp.pad(db_qkv, ((0, 0), (0, 2 * D_MODEL - QKV_WIDTH))).reshape(2, D_MODEL)),
        _pad_rows(db_o),
        _pad_rows(jnp.pad(dsinks.reshape(1, N_Q_HEADS), ((0, 0), (0, D_MODEL - N_Q_HEADS)))),
        _pad_rows(db_sp.reshape(1, D_MODEL)),
        _pad_rows(jnp.concatenate([dln_g, dln_b, jnp.pad(loss_part[0:1], ((0, 0), (0, D_MODEL - LANES)))], axis=0)),
        dw_sp.reshape(SGU_CHUNK, D_MODEL),
    ]
    slab = jnp.concatenate(parts, axis=0)
    return jnp.pad(slab, ((0, SLAB_ROWS - slab.shape[0]), (0, 0))).reshape(N_CHIPS, SLAB_ROWS // N_CHIPS, D_MODEL)


def _unpack_small(slab, j):
    slab = slab.reshape(SLAB_ROWS, D_MODEL)
    norms = [slab[2 * i:2 * i + 2] for i in range(4)]
    db_qkv = slab[8:10].reshape(1, 2 * D_MODEL)[:, :QKV_WIDTH]
    db_o = slab[16:17]
    dsinks = slab[24:25, :N_Q_HEADS]
    db_sp = slab[32:33].reshape(SGU_GROUPS, SGU_CHUNK)
    width = D_MODEL // N_CHIPS
    dln_g = lax.dynamic_slice(slab[40:41], (0, j * width), (1, width))
    dln_b = lax.dynamic_slice(slab[41:42], (0, j * width), (1, width))
    dw_sp = slab[48:48 + SGU_CHUNK].reshape(SGU_GROUPS * SGU_CHUNK, SGU_CHUNK)
    return norms, db_qkv, db_o, dsinks, db_sp, dln_g, dln_b, dw_sp, slab[42, 0]


class _GradReduce:
    def __init__(self, c_arr, jc_arr, dest_shapes):
        self.c_arr, self.jc_arr, self.dest_shapes = c_arr, jc_arr, dest_shapes
        self.grad, self.sibling, self.pair, self.chips, self.dest = {}, {}, {}, {}, {}

    def exchange(self, tags):
        return exchange_rider([self.grad[t] for t in tags])

    def exchanged(self, tags, res):
        for t, r in zip(tags, res):
            self.sibling[t] = r
            self.pair[t] = pair_add(self.grad[t], r, self.c_arr, name=f"pair_add_{t}")

    def scatter(self, tags):
        return scatter_rider([self.pair[t] for t in tags])

    def scattered(self, tags, res, where):
        for t, r in zip(tags, res):
            name, lead = where[t]
            self.dest[name] = final_add(self.grad[t], self.sibling[t], r, self.jc_arr, dest_shape=self.dest_shapes[name],
                                        lead=lead, prev=self.dest.get(name), name=f"final_add_{t}")

    def broadcast(self, items):
        names = []
        for n, _ in items:
            if n not in names:
                names.append(n)
        return names, broadcast_rider([self.dest[n] for n in names], [(names.index(n), lead) for n, lead in items])

    def broadcasted(self, names, res):
        for n, r in zip(names, res):
            self.dest[n] = r


def kernel(x, norm_mix_pre, norm_mix_post, norm_ffn_pre, norm_ffn_post, attn_w_qkv, attn_b_qkv, attn_sinks, attn_w_o, attn_b_o, sgu_w_in, sgu_ln_g, sgu_ln_b, sgu_w_spatial, sgu_b_spatial, sgu_w_out, ffn_w_gate_up, ffn_w_down, loss_target, m_norm_mix_pre, m_norm_mix_post, m_norm_ffn_pre, m_norm_ffn_post, m_attn_w_qkv, m_attn_b_qkv, m_attn_sinks, m_attn_w_o, m_attn_b_o, m_sgu_w_in, m_sgu_ln_g, m_sgu_ln_b, m_sgu_w_spatial, m_sgu_b_spatial, m_sgu_w_out, m_ffn_w_gate_up, m_ffn_w_down, v_norm_mix_pre, v_norm_mix_post, v_norm_ffn_pre, v_norm_ffn_post, v_attn_w_qkv, v_attn_b_qkv, v_attn_sinks, v_attn_w_o, v_attn_b_o, v_sgu_w_in, v_sgu_ln_g, v_sgu_ln_b, v_sgu_w_spatial, v_sgu_b_spatial, v_sgu_w_out, v_ffn_w_gate_up, v_ffn_w_down):
    s = x.shape[1]
    x0 = x.reshape(s, D_MODEL)
    target = loss_target.reshape(s, D_MODEL)
    mx, my, mc = lax.axis_index("x"), lax.axis_index("y"), lax.axis_index("c")
    chip = 2 * mx + my
    chip_arr = jnp.reshape(chip, (1,)).astype(I32)
    c_arr = jnp.reshape(mc, (1,)).astype(I32)
    jc_arr = jnp.stack([chip, mc]).astype(I32)
    zero_bias = jnp.zeros((1, D_MODEL), F32)

    def gain(p, i):
        return p[i:i + 1]

    big = [attn_w_qkv, attn_w_o, sgu_w_in, sgu_w_out, ffn_w_gate_up, ffn_w_gate_up, ffn_w_down, ffn_w_down]
    layers = [0, 0, 0, 0, 0, 1, 0, 1]
    tags = ["qkv", "wo", "win", "wout", "wgu0", "wgu1", "wd0", "wd1"]
    full = {t: place_shard(w, l, chip_arr, BF16, name=f"place_{t}") for w, l, t in zip(big, layers, tags) if t != "wgu1"}
    ln_pack = _pad_rows(jnp.concatenate([sgu_ln_g, sgu_ln_b], axis=0), 16)[None]
    full["ln"] = place_shard(ln_pack, 0, chip_arr, F32, name="place_ln")

    def split(items):
        return [i if isinstance(i, str) else i[0] for i in items], [WHOLE if isinstance(i, str) else tuple(i[1:]) for i in items]

    def ici(*items):
        names, pieces = split(items)
        return gather_ici_rider([full[n] for n in names], pieces)

    def d2d(*items):
        names, pieces = split(items)
        return gather_d2d_rider([full[n] for n in names], pieces)

    def landed(items, res):
        for n, r in zip(split(items)[0], res):
            full[n] = r

    cos, sin = _rope_tables(s)
    sink_rows = jnp.broadcast_to(
        jnp.repeat(attn_sinks.reshape(N_KV_HEADS, GQA_GROUP), WINDOW, axis=1)[:, None, :], (N_KV_HEADS, 8, ROWS))
    w_sp = sgu_w_spatial.reshape(SGU_GROUPS, SGU_CHUNK, SGU_CHUNK)
    b_sp = jnp.broadcast_to(sgu_b_spatial.reshape(SGU_GROUPS, SGU_CHUNK)[:, :, None], (SGU_GROUPS, SGU_CHUNK, LANES))

    h0, (res,) = prenorm(x0, gain(norm_mix_pre, 0), name="prenorm_0", riders=[ici("qkv", "ln")])
    landed(("qkv", "ln"), res)
    full["wgu1"], (res,) = place_shard(ffn_w_gate_up, 1, chip_arr, BF16, name="place_wgu1", riders=[d2d("qkv", "ln")])
    landed(("qkv", "ln"), res)
    ln_g = full["ln"][:, 0, :].reshape(1, D_MODEL)
    ln_b = full["ln"][:, 1, :].reshape(1, D_MODEL)

    def hosted(call, stages):
        outputs, results = call([{"ici": ici, "d2d": d2d}[kind](*items) for kind, items in stages])
        for (_, items), res in zip(stages, results):
            landed(items, res)
        return outputs

    qkv = hosted(lambda r: qkv_proj(h0, full["qkv"], attn_b_qkv, cos, sin, name="qkv_proj", riders=r),
                 [("ici", ("wo", ("wgu0", 0, 4, 8)))])
    o = hosted(lambda r: attn_fwd(qkv, sink_rows, name="attn_fwd", riders=r),
               [("d2d", ("wo",)), ("ici", (("wgu0", 4, 8, 8), ("wd0", 0, 1, 2)))])
    w_o = full["wo"].reshape(Q_WIDTH, D_MODEL)
    x1, h1, m0 = hosted(lambda r: proj_residual_norm(o, w_o, x0, attn_b_o, gain(norm_mix_post, 0), gain(norm_ffn_pre, 0),
                                                     name="attn_out_norm", riders=r),
                        [("d2d", ("wgu0",)), ("ici", (("wd0", 1, 2, 2), ("win", 0, 2, 8)))])
    gu0, a0 = hosted(lambda r: ffn_up(h1, full["wgu0"], name="ffn_up_0", riders=r),
                     [("d2d", ("wd0",)), ("ici", (("win", 2, 8, 8), "wout", ("wgu1", 0, 4, 8)))])
    w_d0 = full["wd0"].reshape(D_FF, D_MODEL)
    x2, h2, f0 = hosted(lambda r: proj_residual_norm(a0, w_d0, x1, zero_bias, gain(norm_ffn_post, 0), gain(norm_mix_pre, 1),
                                                     name="ffn_down_norm_0", riders=r),
                        [("d2d", ("win", "wout")), ("ici", (("wgu1", 4, 7, 8),))])
    w_in = full["win"]
    z, y = hosted(lambda r: sgu_in_fwd(h2, w_in, ln_g, ln_b, w_sp, b_sp, name="sgu_in_fwd", riders=r),
                  [("ici", (("wgu1", 7, 8, 8), ("wd1", 0, 1, 2)))])
    w_out = full["wout"].reshape(D_MODEL, D_MODEL)
    x3, h3, m1 = hosted(lambda r: proj_residual_norm(y, w_out, x2, zero_bias, gain(norm_mix_post, 1), gain(norm_ffn_pre, 1),
                                                     name="sgu_out_norm", riders=r),
                        [("d2d", ("wgu1",)), ("ici", (("wd1", 1, 2, 2),))])
    w_qkv, w_gu0, w_gu1 = full["qkv"], full["wgu0"], full["wgu1"]
    gu1, a1 = hosted(lambda r: ffn_up(h3, w_gu1, name="ffn_up_1", riders=r), [("d2d", ("wd1",))])
    w_d1 = full["wd1"].reshape(D_FF, D_MODEL)
    dx4, df1, dg_fpost1, loss_part = proj_loss_head(a1, w_d1, x3, gain(norm_ffn_post, 1), target, name="ffn_down_loss")

    red = _GradReduce(c_arr, jc_arr, {
        "qkv": attn_w_qkv.shape[1:], "wo": attn_w_o.shape[1:], "win": sgu_w_in.shape[1:], "wout": sgu_w_out.shape[1:],
        "wgu": ffn_w_gate_up.shape, "wd": ffn_w_down.shape, "slab": (N_CHIPS, SLAB_ROWS // N_CHIPS, D_MODEL)})
    where = {"qkv": ("qkv", None), "wo": ("wo", None), "win": ("win", None), "wout": ("wout", None), "wgu0": ("wgu", 0),
             "wgu1": ("wgu", 1), "wd0": ("wd", 0), "wd1": ("wd", 1), "small": ("slab", "chip")}

    dgu1 = ffn_dact(df1, w_d1, gu1, name="ffn_dact_1")
    red.grad["wd1"] = mm_tn(a1, df1, shard_major=False, tm=256, tn=D_MODEL, name="dw_down_1").reshape(
        N_CHIPS, D_FF // N_CHIPS, D_MODEL)
    red.grad["wgu1"], (res,) = mm_tn(h3, dgu1, shard_major=True, tm=512, tn=FF_HALF, name="dw_gate_up_1",
                                     riders=[red.exchange(["wd1"])])
    red.exchanged(["wd1"], res)
    (dx3, dm1, dg_fpre1, dg_mpost1, _), (res_a, res_b) = dh_norm_bwd_pair(
        dgu1, w_gu1, dx4, x3, gain(norm_ffn_pre, 1), m1, gain(norm_mix_post, 1), name="dh_ffn_norm_1",
        riders=[red.exchange(["wgu1"]), red.scatter(["wd1"])])
    red.exchanged(["wgu1"], res_a)
    red.scattered(["wd1"], res_b, where)
    names, rider = red.broadcast([("wd", 1)])
    dy, (res,) = mm_nt(dm1, w_out, out_dtype=F32, name="dy_sgu", riders=[rider])
    red.broadcasted(names, res)
    red.grad["wout"] = mm_tn(y, dm1, shard_major=False, tm=512, tn=D_MODEL, name="dw_sgu_out").reshape(
        N_CHIPS, D_MODEL // N_CHIPS, D_MODEL)
    (dz, dw_sp, db_sp, dln_g, dln_b), (res_a, res_b) = sgu_bwd(
        z, dy, ln_g, ln_b, w_sp, b_sp, name="sgu_bwd", riders=[red.scatter(["wgu1"]), red.exchange(["wout"])])
    red.scattered(["wgu1"], res_a, where)
    red.exchanged(["wout"], res_b)
    names, rider = red.broadcast([("wgu", 1)])
    red.grad["win"], (res_a, res_b) = mm_tn(h2, dz, shard_major=True, tm=D_MODEL, tn=2 * D_MODEL // N_CHIPS, name="dw_sgu_in",
                                            riders=[rider, red.scatter(["wout"])])
    red.broadcasted(names, res_a)
    red.scattered(["wout"], res_b, where)
    names, rider = red.broadcast([("wout", None)])
    (dx2, df0, dg_mpre1, dg_fpost0, _), (res_a, res_b) = dh_norm_bwd_pair(
        dz, w_in, dx3, x2, gain(norm_mix_pre, 1), f0, gain(norm_ffn_post, 0), name="dh_sgu_norm",
        riders=[red.exchange(["win"]), rider])
    red.exchanged(["win"], res_a)
    red.broadcasted(names, res_b)
    dgu0, (res,) = ffn_dact(df0, w_d0, gu0, name="ffn_dact_0", riders=[red.scatter(["win"])])
    red.scattered(["win"], res, where)
    names, rider = red.broadcast([("win", None)])
    dw_d0, (res,) = mm_tn(a0, df0, shard_major=False, tm=256, tn=D_MODEL, name="dw_down_0", riders=[rider])
    red.broadcasted(names, res)
    red.grad["wd0"] = dw_d0.reshape(N_CHIPS, D_FF // N_CHIPS, D_MODEL)
    red.grad["wgu0"], (res,) = mm_tn(h1, dgu0, shard_major=True, tm=512, tn=FF_HALF, name="dw_gate_up_0",
                                     riders=[red.exchange(["wd0"])])
    red.exchanged(["wd0"], res)
    (dx1, dm0, dg_fpre0, dg_mpost0, db_o), (res_a, res_b) = dh_norm_bwd_pair(
        dgu0, w_gu0, dx2, x1, gain(norm_ffn_pre, 0), m0, gain(norm_mix_post, 0), name="dh_ffn_norm_0",
        riders=[red.exchange(["wgu0"]), red.scatter(["wd0"])])
    red.exchanged(["wgu0"], res_a)
    red.scattered(["wd0"], res_b, where)
    names, rider = red.broadcast([("wd", 0)])
    do, (res,) = mm_nt(dm0, w_o, out_dtype=BF16, name="do_attn", riders=[rider])
    red.broadcasted(names, res)
    red.grad["wo"] = mm_tn(o, dm0, shard_major=False, tm=512, tn=D_MODEL, name="dw_attn_out").reshape(
        N_CHIPS, Q_WIDTH // N_CHIPS, D_MODEL)
    (dq, dkc, dkp, dvc, dvp, dsink), (res_a, res_b) = attn_bwd(
        qkv, sink_rows, do, name="attn_bwd", riders=[red.scatter(["wgu0"]), red.exchange(["wo"])])
    red.scattered(["wgu0"], res_a, where)
    red.exchanged(["wo"], res_b)
    names, rider = red.broadcast([("wgu", 0)])
    (dqkv, db_qkv), (res_a, res_b) = rope_bwd(dq, dkc, dkp, dvc, dvp, cos, sin, name="rope_bwd",
                                              riders=[rider, red.scatter(["wo"])])
    red.broadcasted(names, res_a)
    red.scattered(["wo"], res_b, where)
    names, rider = red.broadcast([("wo", None)])
    red.grad["qkv"], (res,) = mm_tn(h0, dqkv, shard_major=True, tm=D_MODEL, tn=QKV_WIDTH // N_CHIPS, name="dw_qkv",
                                    riders=[rider])
    red.broadcasted(names, res)
    dh0, (res,) = mm_nt(dqkv, w_qkv, out_dtype=F32, tm=1024, name="dh_attn", riders=[red.exchange(["qkv"])])
    red.exchanged(["qkv"], res)
    grad_x, dg_mpre0 = norm_bwd_last(dx1, dh0, x0, gain(norm_mix_pre, 0), name="norm_bwd_in")

    norm_grads = [jnp.concatenate(p, axis=0) for p in
                  ((dg_mpre0, dg_mpre1), (dg_mpost0, dg_mpost1), (dg_fpre0, dg_fpre1), (dg_fpost0, dg_fpost1))]
    red.grad["small"] = _pack_small(norm_grads, db_qkv, db_o, dsink[:, :, 0, 0], db_sp[:, :, 0], dln_g, dln_b, dw_sp,
                                    loss_part)
    res_a, res_b = comm_call([red.scatter(["qkv"]), red.exchange(["small"])], name="tail_1")
    red.scattered(["qkv"], res_a, where)
    red.exchanged(["small"], res_b)
    names, rider = red.broadcast([("qkv", None)])
    res_a, res_b = comm_call([red.scatter(["small"]), rider], name="tail_2")
    red.scattered(["small"], res_a, where)
    red.broadcasted(names, res_b)
    ((slab_full,),) = comm_call([allcast_rider(red.dest["slab"])], name="tail_3")
    g_qkv, g_wo, g_win, g_wout, g_wgu, g_wd = (red.dest[n] for n in ("qkv", "wo", "win", "wout", "wgu", "wd"))
    g_norms, g_bqkv, g_bo, g_sinks, g_bsp, g_lng, g_lnb, g_wsp, loss = _unpack_small(slab_full, chip)

    def big_update(w, g, m, v, tag):
        return adamw(w, g.reshape(w.shape), m, v, name=f"adamw_{tag}")

    upd = {
        "attn_w_qkv": big_update(attn_w_qkv, g_qkv, m_attn_w_qkv, v_attn_w_qkv, "qkv"),
        "attn_w_o": big_update(attn_w_o, g_wo, m_attn_w_o, v_attn_w_o, "wo"),
        "sgu_w_in": big_update(sgu_w_in, g_win, m_sgu_w_in, v_sgu_w_in, "win"),
        "sgu_w_out": big_update(sgu_w_out, g_wout, m_sgu_w_out, v_sgu_w_out, "wout"),
        "ffn_w_gate_up": big_update(ffn_w_gate_up, g_wgu, m_ffn_w_gate_up, v_ffn_w_gate_up, "wgu"),
        "ffn_w_down": big_update(ffn_w_down, g_wd, m_ffn_w_down, v_ffn_w_down, "wd"),
    }
    small_names = ["norm_mix_pre", "norm_mix_post", "norm_ffn_pre", "norm_ffn_post", "attn_b_qkv", "attn_sinks", "attn_b_o",
                   "sgu_ln_g", "sgu_ln_b", "sgu_w_spatial", "sgu_b_spatial"]
    small_w = [norm_mix_pre, norm_mix_post, norm_ffn_pre, norm_ffn_post, attn_b_qkv, attn_sinks, attn_b_o, sgu_ln_g, sgu_ln_b,
               sgu_w_spatial, sgu_b_spatial]
    small_m = [m_norm_mix_pre, m_norm_mix_post, m_norm_ffn_pre, m_norm_ffn_post, m_attn_b_qkv, m_attn_sinks, m_attn_b_o,
               m_sgu_ln_g, m_sgu_ln_b, m_sgu_w_spatial, m_sgu_b_spatial]
    small_v = [v_norm_mix_pre, v_norm_mix_post, v_norm_ffn_pre, v_norm_ffn_post, v_attn_b_qkv, v_attn_sinks, v_attn_b_o,
               v_sgu_ln_g, v_sgu_ln_b, v_sgu_w_spatial, v_sgu_b_spatial]
    small_g = g_norms + [g_bqkv, g_sinks, g_bo, g_lng, g_lnb, g_wsp, g_bsp]

    def flat2(a):
        return a.reshape(-1, a.shape[-1])

    res = adamw_small([flat2(a) for a in small_w], [flat2(a) for a in small_g], [flat2(a) for a in small_m],
                      [flat2(a) for a in small_v], name="adamw_small")
    for i, nm in enumerate(small_names):
        upd[nm] = tuple(r[i].reshape(small_w[i].shape) for r in res)

    order = ["norm_mix_pre", "norm_mix_post", "norm_ffn_pre", "norm_ffn_post", "attn_w_qkv", "attn_b_qkv", "attn_sinks",
             "attn_w_o", "attn_b_o", "sgu_w_in", "sgu_ln_g", "sgu_ln_b", "sgu_w_spatial", "sgu_b_spatial", "sgu_w_out",
             "ffn_w_gate_up", "ffn_w_down"]
    outs = [loss, grad_x.reshape(1, s, D_MODEL)]
    for part in range(4):
        outs += [upd[nm][part] for nm in order]
    return tuple(outs)
```

```python
import types

import numpy as np
import jax
import jax.numpy as jnp
from jax import lax
from jax.experimental import pallas as pl
from jax.experimental.pallas import tpu as pltpu

F32 = jnp.float32
BF16 = jnp.bfloat16
I32 = jnp.int32

D_MODEL = 1024
HEAD_DIM = 64
N_Q_HEADS = 16
N_KV_HEADS = 4
GQA_GROUP = 4
WINDOW = 128
Q_WIDTH = 1024
KV_WIDTH = 256
QKV_WIDTH = 1536
ROPE_THETA = 10000.0
SGU_GROUPS = 8
SGU_CHUNK = 128
D_FF = 2816
FF_HALF = D_FF // 2
EPS = 1e-6
N_CHIPS = 4
LANES = 128

ADAM_LR = 0.001
ADAM_B1 = 0.9
ADAM_B2 = 0.999
ADAM_EPS = 1e-08
ADAM_WD = 0.01
ADAM_STEP = 10

VMEM_LIMIT = 52 * 1024 * 1024
MESH = pl.DeviceIdType.MESH
NEG = -1e30
NT_DIMS = (((1,), (1,)), ((), ()))
TN_DIMS = (((0,), (0,)), ((), ()))
NN_DIMS = (((1,), (0,)), ((), ()))
ANY = pl.BlockSpec(memory_space=pl.ANY)


def _row_tile(s, want):
    return want if s % want == 0 else s


PEER_KINDS = ("sibling", "chips", "sibling+chips", "everyone")


def _peer_kind(riders):
    kinds = {r.peers for r in riders}
    if not kinds:
        return None
    if "everyone" in kinds:
        return "everyone"
    return "sibling+chips" if len(kinds) == 2 else kinds.pop()


def _peer_barrier(kind):
    x, y, c = _place()
    chips = [(*_partner(x, y, k), c) for k in (1, 2, 3)]
    peers = {"sibling": [(x, y, 1 - c)], "chips": chips, "sibling+chips": [(x, y, 1 - c)] + chips,
             "everyone": [(x, y, 1 - c)] + chips + [(px, py, 1 - c) for px, py, _ in chips]}[kind]
    barrier = pltpu.get_barrier_semaphore()
    for dev in peers:
        pl.semaphore_signal(barrier, inc=1, device_id=dev, device_id_type=MESH)
    pl.semaphore_wait(barrier, len(peers))


def _call(body, *, name, grid=(), in_specs=(), out_specs=(), out_shape=(), scratch_shapes=(), operands=(), prefetch=(),
          aliases=None, riders=(), sem=None):
    n_pre, n_in, n_out, n_scr = len(prefetch), len(operands), len(out_shape), len(scratch_shapes)
    in_specs, out_specs, out_shape = list(in_specs), list(out_specs), list(out_shape)
    operands, scratch_shapes = list(operands), list(scratch_shapes)
    io_alias = {n_pre + i: o for i, o in (aliases or {}).items()}
    for r in riders:
        base_in, base_out = n_pre + len(operands), len(out_shape)
        operands += list(r.inputs)
        in_specs += [ANY] * len(r.inputs)
        for pos, i in enumerate(r.aliased):
            io_alias[base_in + i] = base_out + pos
            out_shape.append(jax.ShapeDtypeStruct(r.inputs[i].shape, r.inputs[i].dtype))
        out_shape += list(r.fresh)
        out_specs += [ANY] * (len(r.aliased) + len(r.fresh))
        scratch_shapes += [pltpu.SemaphoreType.DMA((r.nsem,)), pltpu.SemaphoreType.DMA((r.nsem,))]

    def wrapped(*refs):
        pre, p = refs[:n_pre], n_pre
        core_in, p = refs[p:p + n_in], p + n_in
        r_in = []
        for r in riders:
            r_in.append(refs[p:p + len(r.inputs)])
            p += len(r.inputs)
        core_out, p = refs[p:p + n_out], p + n_out
        r_out = []
        for r in riders:
            k = len(r.aliased) + len(r.fresh)
            r_out.append(refs[p:p + k])
            p += k
        core_scr, p = refs[p:p + n_scr], p + n_scr
        r_sem = [refs[p + 2 * i:p + 2 * i + 2] for i in range(len(riders))]

        def edge(at_last, fns):
            def run():
                if not at_last:
                    _peer_barrier(peer_kind)
                for i, r in enumerate(riders):
                    getattr(r, fns)(r_in[i], r_out[i], r_sem[i][0], r_sem[i][1])
            if not riders:
                return
            if not grid:
                run()
                return
            cond = None
            for d, n in enumerate(grid):
                c = pl.program_id(d) == (n - 1 if at_last else 0)
                cond = c if cond is None else jnp.logical_and(cond, c)
            pl.when(cond)(run)

        edge(False, "start")
        if body is not None:
            body(*pre, *core_in, *core_out, *core_scr)
        edge(True, "finish")

    if sem is None or riders:
        sem = ("arbitrary",) * len(grid)
    kwargs = dict(out_shape=out_shape, input_output_aliases=io_alias, name=name)
    peer_kind = _peer_kind(riders)
    collective = {} if peer_kind is None else {"collective_id": PEER_KINDS.index(peer_kind)}
    if grid:
        kwargs["compiler_params"] = pltpu.CompilerParams(dimension_semantics=sem, vmem_limit_bytes=VMEM_LIMIT, **collective)
    elif collective:
        kwargs["compiler_params"] = pltpu.CompilerParams(**collective)
    if n_pre:
        kwargs["grid_spec"] = pltpu.PrefetchScalarGridSpec(
            num_scalar_prefetch=n_pre, grid=grid, in_specs=in_specs, out_specs=out_specs, scratch_shapes=scratch_shapes)
    else:
        kwargs.update(grid=grid, in_specs=in_specs, out_specs=out_specs, scratch_shapes=scratch_shapes)
    res = pl.pallas_call(wrapped, **kwargs)(*prefetch, *operands)
    core, rest, rider_res = list(res[:n_out]), list(res[n_out:]), []
    for r in riders:
        k = len(r.aliased) + len(r.fresh)
        rider_res.append(rest[:k])
        rest = rest[k:]
    return core, rider_res


def _mm_call(*, grid, in_specs, out_spec, out_shape, dims, nk, kaxis, acc_shape, name, operands, riders=()):
    out_dtype = out_shape.dtype

    def body(a_ref, b_ref, o_ref, *scratch):
        p = lax.dot_general(a_ref[...].astype(BF16), b_ref[...].astype(BF16), dims, preferred_element_type=F32)
        if nk == 1:
            o_ref[...] = p.astype(out_dtype)
        else:
            acc = scratch[0]
            kk = pl.program_id(kaxis)

            @pl.when(kk == 0)
            def _():
                acc[...] = p

            @pl.when(kk > 0)
            def _():
                acc[...] += p

            @pl.when(kk == nk - 1)
            def _():
                o_ref[...] = acc[...].astype(out_dtype)

    sem = ["parallel"] * len(grid)
    if nk > 1:
        sem[kaxis] = "arbitrary"
    (out,), rider_res = _call(
        body, grid=grid, in_specs=in_specs, out_specs=[out_spec], out_shape=[out_shape],
        scratch_shapes=[pltpu.VMEM(acc_shape, F32)] if nk > 1 else [], operands=operands, name=name, riders=riders,
        sem=tuple(sem))
    return (out, rider_res) if riders else out


def mm_nn(a, w, *, out_dtype, name, tm=512, tn=512, riders=()):
    m, k = a.shape
    tm = _row_tile(m, tm)
    if w.ndim == 3:
        ns = w.shape[2]
        grid = (N_CHIPS, m // tm)
        w_spec = pl.BlockSpec((None, k, ns), lambda j, i: (j, 0, 0))
        o_spec = pl.BlockSpec((tm, ns), lambda j, i: (i, j))
        n = N_CHIPS * ns
    else:
        n = w.shape[1]
        grid = (n // tn, m // tm)
        w_spec = pl.BlockSpec((k, tn), lambda j, i: (0, j))
        o_spec = pl.BlockSpec((tm, tn), lambda j, i: (i, j))
    return _mm_call(grid=grid, in_specs=[pl.BlockSpec((tm, k), lambda j, i: (i, 0)), w_spec], out_spec=o_spec,
                    out_shape=jax.ShapeDtypeStruct((m, n), out_dtype), dims=NN_DIMS, nk=1, kaxis=0, acc_shape=None,
                    name=name, operands=(a, w), riders=riders)


def mm_nt(a, w, *, out_dtype, name, tm=512, tn=512, riders=()):
    if w.ndim == 2:
        m, n = a.shape
        kout = w.shape[0]
        tm = _row_tile(m, tm)
        return _mm_call(grid=(kout // tn, m // tm),
                        in_specs=[pl.BlockSpec((tm, n), lambda j, i: (i, 0)), pl.BlockSpec((tn, n), lambda j, i: (j, 0))],
                        out_spec=pl.BlockSpec((tm, tn), lambda j, i: (i, j)),
                        out_shape=jax.ShapeDtypeStruct((m, kout), out_dtype), dims=NT_DIMS, nk=1, kaxis=0,
                        acc_shape=None, name=name, operands=(a, w), riders=riders)
    _, kout, ns = w.shape
    planes = a.ndim == 3
    m = a.shape[1] if planes else a.shape[0]
    tm = _row_tile(m, tm)
    a_spec = pl.BlockSpec((2, tm, 2 * ns), lambda i: (0, i, 0)) if planes else pl.BlockSpec((tm, N_CHIPS * ns), lambda i: (i, 0))

    def body(a_ref, w0, w1, w2, w3, o_ref):
        acc = None
        for j, w_ref in enumerate((w0, w1, w2, w3)):
            if planes:
                a_j = a_ref[j // 2, :, (j % 2) * ns:(j % 2 + 1) * ns]
            else:
                a_j = a_ref[:, j * ns:(j + 1) * ns]
            p = lax.dot_general(a_j, w_ref[...], NT_DIMS, preferred_element_type=F32)
            acc = p if acc is None else acc + p
        o_ref[...] = acc.astype(out_dtype)

    def shard(j):
        return pl.BlockSpec((None, kout, ns), lambda i: (j, 0, 0))

    (out,), rider_res = _call(
        body, grid=(m // tm,), in_specs=[a_spec] + [shard(j) for j in range(N_CHIPS)],
        out_specs=[pl.BlockSpec((tm, kout), lambda i: (i, 0))], out_shape=[jax.ShapeDtypeStruct((m, kout), out_dtype)],
        operands=(a, w, w, w, w), sem=("parallel",), name=name, riders=riders)
    return (out, rider_res) if riders else out


def mm_tn(a, b, *, shard_major, name, tm, tn, tk=None, out_dtype=BF16, riders=()):
    s, m = a.shape
    tk = s if tk is None else _row_tile(s, tk)
    if b.ndim == 3:
        n = 2 * b.shape[2]
        b_spec = pl.BlockSpec((None, tk, tn), lambda j, i, kk: (j // 2, kk, j % 2))
    else:
        n = b.shape[1]
        b_spec = pl.BlockSpec((tk, tn), lambda j, i, kk: (kk, j))
    if shard_major:
        assert tn == n // N_CHIPS
        o_spec = pl.BlockSpec((None, tm, tn), lambda j, i, kk: (j, i, 0))
        o_shape = jax.ShapeDtypeStruct((N_CHIPS, m, tn), out_dtype)
    else:
        o_spec = pl.BlockSpec((tm, tn), lambda j, i, kk: (i, j))
        o_shape = jax.ShapeDtypeStruct((m, n), out_dtype)
    return _mm_call(grid=(n // tn, m // tm, s // tk),
                    in_specs=[pl.BlockSpec((tk, tm), lambda j, i, kk: (kk, i)), b_spec], out_spec=o_spec,
                    out_shape=o_shape, dims=TN_DIMS, nk=s // tk, kaxis=2, acc_shape=(tm, tn), name=name, operands=(a, b),
                    riders=riders)


def _rstd(x):
    return lax.rsqrt(jnp.mean(x * x, axis=-1, keepdims=True) + EPS)


def _rms_bwd(dy, x, g):
    r = _rstd(x)
    xhat = x * r
    gy = dy * g
    dx = r * (gy - xhat * jnp.mean(gy * xhat, axis=-1, keepdims=True))
    return dx, jnp.sum(dy * xhat, axis=0, keepdims=True)


def _accum(ref, val, first):
    @pl.when(first)
    def _():
        ref[...] = val

    @pl.when(jnp.logical_not(first))
    def _():
        ref[...] += val


def _row_spec(tm, width):
    return pl.BlockSpec((tm, width), lambda i: (i, 0))


def _vec_spec(width):
    return pl.BlockSpec((1, width), lambda i: (0, 0))


def _ret(core, rider_res, riders):
    core = core[0] if len(core) == 1 else core
    return (core, rider_res) if riders else core


def prenorm(x, g, *, name, tm=256, riders=()):
    s = x.shape[0]
    tm = _row_tile(s, tm)

    def body(x_ref, g_ref, h_ref):
        xv = x_ref[...]
        h_ref[...] = (xv * _rstd(xv) * g_ref[...]).astype(BF16)

    core, rr = _call(
        body, grid=(s // tm,), in_specs=[_row_spec(tm, D_MODEL), _vec_spec(D_MODEL)], out_specs=[_row_spec(tm, D_MODEL)],
        out_shape=[jax.ShapeDtypeStruct((s, D_MODEL), BF16)], operands=(x, g), sem=("parallel",), name=name, riders=riders)
    return _ret(core, rr, riders)


def proj_residual_norm(a, w, x, bias, g_post, g_next, *, name, tm=256, riders=()):
    s, k = a.shape
    tm = _row_tile(s, tm)

    def body(a_ref, w_ref, x_ref, b_ref, gp_ref, gn_ref, xo_ref, h_ref, m_ref):
        mv = jnp.dot(a_ref[...], w_ref[...], preferred_element_type=F32) + b_ref[...]
        m_ref[...] = mv.astype(BF16)
        xn = x_ref[...] + mv * _rstd(mv) * gp_ref[...]
        xo_ref[...] = xn
        h_ref[...] = (xn * _rstd(xn) * gn_ref[...]).astype(BF16)

    row, vec = _row_spec(tm, D_MODEL), _vec_spec(D_MODEL)
    core, rr = _call(
        body, grid=(s // tm,),
        in_specs=[_row_spec(tm, k), pl.BlockSpec((k, D_MODEL), lambda i: (0, 0)), row, vec, vec, vec], out_specs=[row, row, row],
        out_shape=[jax.ShapeDtypeStruct((s, D_MODEL), F32), jax.ShapeDtypeStruct((s, D_MODEL), BF16),
                   jax.ShapeDtypeStruct((s, D_MODEL), BF16)],
        operands=(a, w, x, bias, g_post, g_next), sem=("parallel",), name=name, riders=riders)
    return _ret(core, rr, riders)


def proj_loss_head(a, w, x, g_post, target, *, name, tm=256, riders=()):
    s, k = a.shape
    tm = _row_tile(s, tm)

    def body(a_ref, w_ref, x_ref, g_ref, t_ref, dx_ref, df_ref, dg_ref, loss_ref):
        first = pl.program_id(0) == 0
        fv = jnp.dot(a_ref[...], w_ref[...], preferred_element_type=F32)
        g = g_ref[...]
        err = x_ref[...] + fv * _rstd(fv) * g - t_ref[...]
        dx = err * (1.0 / D_MODEL)
        dx_ref[...] = dx
        df, dg = _rms_bwd(dx, fv, g)
        df_ref[...] = df.astype(BF16)
        _accum(dg_ref, dg, first)
        part = jnp.sum(jnp.sum(err * err, axis=-1, keepdims=True), axis=0, keepdims=True) * (0.5 / D_MODEL)
        _accum(loss_ref, jnp.broadcast_to(part, (8, LANES)), first)

    row, vec = _row_spec(tm, D_MODEL), _vec_spec(D_MODEL)
    core, rr = _call(
        body, grid=(s // tm,), in_specs=[_row_spec(tm, k), pl.BlockSpec((k, D_MODEL), lambda i: (0, 0)), row, vec, row],
        out_specs=[row, row, vec, pl.BlockSpec((8, LANES), lambda i: (0, 0))],
        out_shape=[jax.ShapeDtypeStruct((s, D_MODEL), F32), jax.ShapeDtypeStruct((s, D_MODEL), BF16),
                   jax.ShapeDtypeStruct((1, D_MODEL), F32), jax.ShapeDtypeStruct((8, LANES), F32)],
        operands=(a, w, x, g_post, target), name=name, riders=riders)
    return _ret(core, rr, riders)


def dh_norm_bwd_pair(a, w, dres, x, g_pre, m, g_post, *, name, tm=512, sub=256, riders=()):
    _, kout, ns = w.shape
    planes = a.ndim == 3
    s = x.shape[0]
    tm = _row_tile(s, tm)
    sub = min(sub, tm)
    a_spec = pl.BlockSpec((2, tm, 2 * ns), lambda i: (0, i, 0)) if planes else pl.BlockSpec((tm, N_CHIPS * ns), lambda i: (i, 0))

    def body(a_ref, w0, w1, w2, w3, dres_ref, x_ref, gpre_ref, m_ref, gpost_ref, dx_ref, dm_ref, dgpre_ref, dgpost_ref, db_ref):
        first = pl.program_id(0) == 0
        sums = None
        for t in range(tm // sub):
            rows = slice(t * sub, (t + 1) * sub)
            dh = None
            for j, w_ref in enumerate((w0, w1, w2, w3)):
                a_j = a_ref[j // 2, rows, (j % 2) * ns:(j % 2 + 1) * ns] if planes else a_ref[rows, j * ns:(j + 1) * ns]
                p = lax.dot_general(a_j, w_ref[...], NT_DIMS, preferred_element_type=F32)
                dh = p if dh is None else dh + p
            d1, dgpre = _rms_bwd(dh, x_ref[rows, :], gpre_ref[...])
            dx = dres_ref[rows, :] + d1
            dx_ref[rows, :] = dx
            dm, dgpost = _rms_bwd(dx, m_ref[rows, :].astype(F32), gpost_ref[...])
            dm_ref[rows, :] = dm.astype(BF16)
            part = (dgpre, dgpost, jnp.sum(dm, axis=0, keepdims=True))
            sums = part if sums is None else tuple(u + v for u, v in zip(sums, part))
        _accum(dgpre_ref, sums[0], first)
        _accum(dgpost_ref, sums[1], first)
        _accum(db_ref, sums[2], first)

    def shard(j):
        return pl.BlockSpec((None, kout, ns), lambda i: (j, 0, 0))

    row, vec = _row_spec(tm, D_MODEL), _vec_spec(D_MODEL)
    vshape = jax.ShapeDtypeStruct((1, D_MODEL), F32)
    core, rr = _call(
        body, grid=(s // tm,), in_specs=[a_spec] + [shard(j) for j in range(N_CHIPS)] + [row, row, vec, row, vec],
        out_specs=[row, row, vec, vec, vec],
        out_shape=[jax.ShapeDtypeStruct((s, D_MODEL), F32), jax.ShapeDtypeStruct((s, D_MODEL), BF16), vshape, vshape, vshape],
        operands=(a, w, w, w, w, dres, x, g_pre, m, g_post), name=name, riders=riders)
    return _ret(core, rr, riders)


def ffn_bwd_rows(df, w_d, d_planes, w_gu, dres, x, g_pre, m, g_post, *, name, tm=256, riders=()):
    s = x.shape[0]
    tm = _row_tile(s, tm)

    def body(df_ref, wd_ref, d_ref, w0, w1, w2, w3, dres_ref, x_ref, gpre_ref, m_ref, gpost_ref,
             o_ref, dx_ref, dm_ref, dgpre_ref, dgpost_ref, db_ref):
        first = pl.program_id(0) == 0
        dfv = df_ref[...]
        dh = None
        for half, (wg_ref, wu_ref) in enumerate(((w0, w2), (w1, w3))):
            cols = slice(half * FF_HALF, (half + 1) * FF_HALF)
            da = lax.dot_general(dfv, wd_ref[cols, :], NT_DIMS, preferred_element_type=F32)
            dg = (da * d_ref[0, :, cols].astype(F32)).astype(BF16)
            du = (da * d_ref[1, :, cols].astype(F32)).astype(BF16)
            o_ref[0, :, cols] = dg
            o_ref[1, :, cols] = du
            p = lax.dot_general(dg, wg_ref[...], NT_DIMS, preferred_element_type=F32)
            p += lax.dot_general(du, wu_ref[...], NT_DIMS, preferred_element_type=F32)
            dh = p if dh is None else dh + p
        d1, dgpre = _rms_bwd(dh, x_ref[...], gpre_ref[...])
        dx = dres_ref[...] + d1
        dx_ref[...] = dx
        dm, dgpost = _rms_bwd(dx, m_ref[...].astype(F32), gpost_ref[...])
        dm_ref[...] = dm.astype(BF16)
        _accum(dgpre_ref, dgpre, first)
        _accum(dgpost_ref, dgpost, first)
        _accum(db_ref, jnp.sum(dm, axis=0, keepdims=True), first)

    def resident(shape, index):
        return pl.BlockSpec(shape, index, pipeline_mode=pl.Buffered(1))

    planes = pl.BlockSpec((2, tm, D_FF), lambda i: (0, i, 0))
    row, vec = _row_spec(tm, D_MODEL), _vec_spec(D_MODEL)
    vshape = jax.ShapeDtypeStruct((1, D_MODEL), F32)
    shards = [resident((None, D_MODEL, FF_HALF), (lambda j: (lambda i: (j, 0, 0)))(j)) for j in range(N_CHIPS)]
    core, rr = _call(
        body, grid=(s // tm,),
        in_specs=[row, resident((D_FF, D_MODEL), lambda i: (0, 0)), planes] + shards + [row, row, vec, row, vec],
        out_specs=[planes, row, row, vec, vec, vec],
        out_shape=[jax.ShapeDtypeStruct((2, s, D_FF), BF16), jax.ShapeDtypeStruct((s, D_MODEL), F32),
                   jax.ShapeDtypeStruct((s, D_MODEL), BF16), vshape, vshape, vshape],
        operands=(df, w_d, d_planes, w_gu, w_gu, w_gu, w_gu, dres, x, g_pre, m, g_post), name=name, riders=riders)
    return _ret(core, rr, riders)


def norm_bwd_last(dres, dh, x, g_pre, *, name, tm=256, riders=()):
    s = x.shape[0]
    tm = _row_tile(s, tm)

    def body(dres_ref, dh_ref, x_ref, g_ref, dx_ref, dg_ref):
        d1, dg = _rms_bwd(dh_ref[...], x_ref[...], g_ref[...])
        dx_ref[...] = dres_ref[...] + d1
        _accum(dg_ref, dg, pl.program_id(0) == 0)

    row, vec = _row_spec(tm, D_MODEL), _vec_spec(D_MODEL)
    core, rr = _call(
        body, grid=(s // tm,), in_specs=[row, row, row, vec], out_specs=[row, vec],
        out_shape=[jax.ShapeDtypeStruct((s, D_MODEL), F32), jax.ShapeDtypeStruct((1, D_MODEL), F32)],
        operands=(dres, dh, x, g_pre), name=name, riders=riders)
    return _ret(core, rr, riders)


def _rope_tables(s):
    half = HEAD_DIM // 2
    inv_freq = np.float32(ROPE_THETA) ** (-(np.arange(half, dtype=np.float32) * np.float32(2.0)) / np.float32(HEAD_DIM))
    ang = np.arange(s, dtype=np.float32)[:, None] * inv_freq[None, :]
    cos, sin = np.cos(ang).astype(np.float32), np.sin(ang).astype(np.float32)
    return jnp.asarray(np.tile(cos, (1, 4))), jnp.asarray(np.concatenate([-sin, sin, -sin, sin], axis=1))


def _swap_halves(x):
    lane = lax.broadcasted_iota(I32, x.shape, 1)
    return jnp.where((lane & (HEAD_DIM - 1)) < HEAD_DIM // 2, pltpu.roll(x, LANES - 32, 1), pltpu.roll(x, 32, 1))


N_ROPE_BLOCKS = (Q_WIDTH + KV_WIDTH) // LANES


def qkv_proj(h, w, bias, cos, sin, *, name, tm=512, riders=()):
    s, k = h.shape
    ns = w.shape[2]
    tm = _row_tile(s, tm)

    def body(h_ref, w_ref, b_ref, c_ref, s_ref, o_ref):
        j = pl.program_id(0)
        sub = min(256, tm)
        for t in range(tm // sub):
            rows = slice(t * sub, (t + 1) * sub)
            p = jnp.dot(h_ref[rows, :], w_ref[...], preferred_element_type=F32) + b_ref[...]
            cosv, sinv = c_ref[rows, :], s_ref[rows, :]
            for blk in range(ns // LANES):
                xb = p[:, blk * LANES:(blk + 1) * LANES]
                roped = xb * cosv + _swap_halves(xb) * sinv
                is_qk = j * (ns // LANES) + blk < N_ROPE_BLOCKS
                o_ref[rows, blk * LANES:(blk + 1) * LANES] = jnp.where(is_qk, roped, xb).astype(BF16)

    core, rr = _call(
        body, grid=(N_CHIPS, s // tm),
        in_specs=[pl.BlockSpec((tm, k), lambda j, i: (i, 0)), pl.BlockSpec((None, k, ns), lambda j, i: (j, 0, 0)),
                  pl.BlockSpec((1, ns), lambda j, i: (0, j)), pl.BlockSpec((tm, LANES), lambda j, i: (i, 0)),
                  pl.BlockSpec((tm, LANES), lambda j, i: (i, 0))],
        out_specs=[pl.BlockSpec((tm, ns), lambda j, i: (i, j))], out_shape=[jax.ShapeDtypeStruct((s, N_CHIPS * ns), BF16)],
        operands=(h, w, bias, cos, sin), sem=("parallel", "parallel"), name=name, riders=riders)
    return _ret(core, rr, riders)


def rope_bwd(dq, dkc, dkp, dvc, dvp, cos, sin, *, name, riders=()):
    s = dq.shape[0]
    tm = 2 * WINDOW if s % (2 * WINDOW) == 0 else WINDOW
    nb = s // tm

    def body(dq_ref, dkc_ref, dkp_ref, dkp_next_ref, dvc_ref, dvp_ref, dvp_next_ref, c_ref, s_ref, o_ref, db_ref):
        i = pl.program_id(0)
        has_next = (i < nb - 1).astype(F32)
        cosv, sinv = c_ref[...], s_ref[...]

        def shifted(ref, next_ref, cols):
            last = has_next * next_ref[:WINDOW, cols].astype(F32)
            return last if tm == WINDOW else jnp.concatenate([ref[WINDOW:, cols].astype(F32), last], axis=0)

        parts = []
        for blk in range(QKV_WIDTH // LANES):
            if blk < Q_WIDTH // LANES:
                g = dq_ref[:, blk * LANES:(blk + 1) * LANES].astype(F32)
            else:
                own, prv, nxt = (dkc_ref, dkp_ref, dkp_next_ref) if blk < N_ROPE_BLOCKS else (dvc_ref, dvp_ref, dvp_next_ref)
                cols = slice((blk % 2) * LANES, (blk % 2 + 1) * LANES)
                g = own[:, cols].astype(F32) + shifted(prv, nxt, cols)
            if blk < N_ROPE_BLOCKS:
                g = g * cosv + _swap_halves(g * sinv)
            o_ref[:, blk * LANES:(blk + 1) * LANES] = g.astype(BF16)
            parts.append(jnp.sum(g, axis=0, keepdims=True))
        sums = jnp.concatenate(parts, axis=1)
        _accum(db_ref, sums, i == 0)

    own_spec = _row_spec(tm, KV_WIDTH)
    next_spec = pl.BlockSpec((tm, KV_WIDTH), lambda i: (jnp.minimum(i + 1, nb - 1), 0))
    core, rr = _call(
        body, grid=(nb,),
        in_specs=[_row_spec(tm, Q_WIDTH), own_spec, own_spec, next_spec, own_spec, own_spec, next_spec,
                  _row_spec(tm, LANES), _row_spec(tm, LANES)],
        out_specs=[_row_spec(tm, QKV_WIDTH), _vec_spec(QKV_WIDTH)],
        out_shape=[jax.ShapeDtypeStruct((s, QKV_WIDTH), BF16), jax.ShapeDtypeStruct((1, QKV_WIDTH), F32)],
        operands=(dq, dkc, dkp, dkp, dvc, dvp, dvp, cos, sin), name=name, riders=riders)
    return _ret(core, rr, riders)


ROWS = GQA_GROUP * WINDOW


def _prev_slots():
    kpos = lax.broadcasted_iota(I32, (WINDOW, ROWS), 0)
    qpos = lax.broadcasted_iota(I32, (WINDOW, ROWS), 1) & (WINDOW - 1)
    return kpos > qpos


def _head_cols(ref, head):
    return ref[:, head * HEAD_DIM:(head + 1) * HEAD_DIM]


def _stack_heads(ref, h):
    return jnp.concatenate([_head_cols(ref, GQA_GROUP * h + g) for g in range(GQA_GROUP)], axis=0)


def _band(prev_ref, cur_ref, h):
    return jnp.concatenate([_head_cols(prev_ref, h), _head_cols(cur_ref, h)], axis=0)


def _pick(prev, band):
    return jnp.where(prev, band[:WINDOW], band[WINDOW:])


def _spread(prev, x):
    return jnp.concatenate([jnp.where(prev, x, 0.0), jnp.where(prev, 0.0, x)], axis=0).astype(BF16)


def _attn_probs(q, kband, sink, prev, has_prev):
    scale = HEAD_DIM ** -0.5
    s_band = lax.dot_general(kband, q, NT_DIMS, preferred_element_type=F32)
    s = jnp.where(prev, jnp.where(has_prev, s_band[:WINDOW], NEG), s_band[WINDOW:]) * scale
    m = jnp.maximum(jnp.max(s, axis=0, keepdims=True), sink)
    e, es = jnp.exp(s - m), jnp.exp(sink - m)
    inv = 1.0 / (jnp.sum(e, axis=0, keepdims=True) + es)
    return e * inv, es * inv


def _attn_specs(nb):
    kcol, vcol = Q_WIDTH // KV_WIDTH, Q_WIDTH // KV_WIDTH + 1
    q_spec = pl.BlockSpec((WINDOW, Q_WIDTH), lambda n: (n, 0))
    return [q_spec,
            pl.BlockSpec((WINDOW, KV_WIDTH), lambda n: (n, kcol)),
            pl.BlockSpec((WINDOW, KV_WIDTH), lambda n: (jnp.maximum(n - 1, 0), kcol)),
            pl.BlockSpec((WINDOW, KV_WIDTH), lambda n: (n, vcol)),
            pl.BlockSpec((WINDOW, KV_WIDTH), lambda n: (jnp.maximum(n - 1, 0), vcol)),
            pl.BlockSpec((N_KV_HEADS, 8, ROWS), lambda n: (0, 0, 0))]


def attn_fwd(qkv, sink_rows, *, name, riders=()):
    s = qkv.shape[0]

    def body(q_ref, kc_ref, kp_ref, vc_ref, vp_ref, sink_ref, o_ref):
        prev = _prev_slots()
        has_prev = pl.program_id(0) > 0
        for h in range(N_KV_HEADS):
            p, _ = _attn_probs(_stack_heads(q_ref, h), _band(kp_ref, kc_ref, h), sink_ref[h, 0:1, :], prev, has_prev)
            o = lax.dot_general(_band(vp_ref, vc_ref, h), _spread(prev, p), TN_DIMS, preferred_element_type=F32).T
            for g in range(GQA_GROUP):
                head = GQA_GROUP * h + g
                o_ref[:, head * HEAD_DIM:(head + 1) * HEAD_DIM] = o[g * WINDOW:(g + 1) * WINDOW].astype(BF16)

    core, rr = _call(
        body, grid=(s // WINDOW,), in_specs=_attn_specs(s // WINDOW), out_specs=[pl.BlockSpec((WINDOW, Q_WIDTH), lambda n: (n, 0))],
        out_shape=[jax.ShapeDtypeStruct((s, Q_WIDTH), BF16)], operands=(qkv, qkv, qkv, qkv, qkv, sink_rows), sem=("parallel",),
        name=name, riders=riders)
    return _ret(core, rr, riders)


def attn_bwd(qkv, sink_rows, do, *, name, riders=()):
    s = qkv.shape[0]

    def body(q_ref, kc_ref, kp_ref, vc_ref, vp_ref, sink_ref, do_ref, dq_ref, dkc_ref, dkp_ref, dvc_ref, dvp_ref, dsink_ref):
        n = pl.program_id(0)
        prev = _prev_slots()
        scale = HEAD_DIM ** -0.5
        parts = []
        for h in range(N_KV_HEADS):
            qv, dov = _stack_heads(q_ref, h), _stack_heads(do_ref, h)
            kband, vband = _band(kp_ref, kc_ref, h), _band(vp_ref, vc_ref, h)
            p, ps = _attn_probs(qv, kband, sink_ref[h, 0:1, :], prev, n > 0)
            dp = _pick(prev, lax.dot_general(vband, dov, NT_DIMS, preferred_element_type=F32))
            delta = jnp.sum(p * dp, axis=0, keepdims=True)
            ds_band = _spread(prev, p * (dp - delta) * scale)
            p_band = _spread(prev, p)
            dk = jnp.dot(ds_band, qv, preferred_element_type=F32).astype(BF16)
            dv = jnp.dot(p_band, dov, preferred_element_type=F32).astype(BF16)
            dq = lax.dot_general(kband, ds_band, TN_DIMS, preferred_element_type=F32).T
            cols = slice(h * HEAD_DIM, (h + 1) * HEAD_DIM)
            dkp_ref[:, cols], dkc_ref[:, cols] = dk[:WINDOW], dk[WINDOW:]
            dvp_ref[:, cols], dvc_ref[:, cols] = dv[:WINDOW], dv[WINDOW:]
            dsink = -(ps * delta)
            for g in range(GQA_GROUP):
                head = GQA_GROUP * h + g
                dq_ref[:, head * HEAD_DIM:(head + 1) * HEAD_DIM] = dq[g * WINDOW:(g + 1) * WINDOW].astype(BF16)
                parts.append(jnp.broadcast_to(jnp.sum(dsink[:, g * WINDOW:(g + 1) * WINDOW], axis=1, keepdims=True), (8, LANES)))

        @pl.when(n == 0)
        def _():
            for i, part in enumerate(parts):
                dsink_ref[i // GQA_GROUP, i % GQA_GROUP] = part

        @pl.when(n > 0)
        def _():
            for i, part in enumerate(parts):
                dsink_ref[i // GQA_GROUP, i % GQA_GROUP] += part

    rows_q = pl.BlockSpec((WINDOW, Q_WIDTH), lambda n: (n, 0))
    rows_kv = pl.BlockSpec((WINDOW, KV_WIDTH), lambda n: (n, 0))
    kv_shape = jax.ShapeDtypeStruct((s, KV_WIDTH), BF16)
    core, rr = _call(
        body, grid=(s // WINDOW,), in_specs=_attn_specs(s // WINDOW) + [rows_q],
        out_specs=[rows_q, rows_kv, rows_kv, rows_kv, rows_kv,
                   pl.BlockSpec((N_KV_HEADS, GQA_GROUP, 8, LANES), lambda n: (0, 0, 0, 0))],
        out_shape=[jax.ShapeDtypeStruct((s, Q_WIDTH), BF16), kv_shape, kv_shape, kv_shape, kv_shape,
                   jax.ShapeDtypeStruct((N_KV_HEADS, GQA_GROUP, 8, LANES), F32)],
        operands=(qkv, qkv, qkv, qkv, qkv, sink_rows, do), sem=("arbitrary",), name=name, riders=riders)
    return _ret(core, rr, riders)


GELU_C = 0.7978845608028654
GELU_A = 0.044715


def _gelu(x):
    return 0.5 * x * (1.0 + jnp.tanh(x * (GELU_C + (GELU_C * GELU_A) * (x * x))))


def _gelu_and_grad(x):
    x2 = x * x
    t = jnp.tanh(x * (GELU_C + (GELU_C * GELU_A) * x2))
    half_x, one_t = 0.5 * x, 1.0 + t
    return half_x * one_t, 0.5 * one_t + half_x * (1.0 - t * t) * (GELU_C + (3.0 * GELU_C * GELU_A) * x2)


def _tril_bf16(w):
    row = lax.broadcasted_iota(I32, (SGU_CHUNK, SGU_CHUNK), 0)
    col = lax.broadcasted_iota(I32, (SGU_CHUNK, SGU_CHUNK), 1)
    return jnp.where(row >= col, w, 0.0).astype(BF16)


def _sgu_norm(vg, g, b):
    mu = jnp.mean(vg, axis=-1, keepdims=True)
    cen = vg - mu
    rstd = lax.rsqrt(jnp.mean(cen * cen, axis=-1, keepdims=True) + EPS)
    xhat = cen * rstd
    return xhat, rstd, xhat * g + b


def sgu_in_fwd(h, w_in, ln_g, ln_b, w_sp, b_sp, *, name, tm=256, riders=()):
    s, k = h.shape
    ns = w_in.shape[2]
    tm = _row_tile(s, tm)

    def body(h_ref, w0, w1, w2, w3, g_ref, b_ref, w_ref, bs_ref, z_ref, y_ref):
        hv = h_ref[...]
        zs = [jnp.dot(hv, w_ref_j[...], preferred_element_type=F32) for w_ref_j in (w0, w1, w2, w3)]
        for j, zj in enumerate(zs):
            z_ref[:, j * ns:(j + 1) * ns] = zj.astype(BF16)
        u = _gelu(jnp.concatenate(zs[:2], axis=1))
        _, _, vn = _sgu_norm(_gelu(jnp.concatenate(zs[2:], axis=1)), g_ref[...], b_ref[...])
        vn = vn.astype(BF16)
        for grp in range(SGU_GROUPS):
            w = _tril_bf16(w_ref[grp])
            cols = slice(grp * LANES, (grp + 1) * LANES)
            for ch in range(tm // SGU_CHUNK):
                rows = slice(ch * SGU_CHUNK, (ch + 1) * SGU_CHUNK)
                mixed = jnp.dot(w, vn[rows, cols], preferred_element_type=F32) + bs_ref[grp]
                y_ref[rows, cols] = (u[rows, cols] * mixed).astype(BF16)

    def shard(j):
        return pl.BlockSpec((None, k, ns), lambda i: (j, 0, 0))

    full3 = pl.BlockSpec((SGU_GROUPS, SGU_CHUNK, SGU_CHUNK), lambda i: (0, 0, 0))
    core, rr = _call(
        body, grid=(s // tm,),
        in_specs=[_row_spec(tm, k)] + [shard(j) for j in range(N_CHIPS)] + [_vec_spec(D_MODEL), _vec_spec(D_MODEL), full3, full3],
        out_specs=[_row_spec(tm, 2 * D_MODEL), _row_spec(tm, D_MODEL)],
        out_shape=[jax.ShapeDtypeStruct((s, 2 * D_MODEL), BF16), jax.ShapeDtypeStruct((s, D_MODEL), BF16)],
        operands=(h, w_in, w_in, w_in, w_in, ln_g, ln_b, w_sp, b_sp), sem=("parallel",), name=name, riders=riders)
    return _ret(core, rr, riders)


def sgu_bwd(z, dy, ln_g, ln_b, w_sp, b_sp, *, name, tm=256, riders=()):
    s = z.shape[0]
    tm = _row_tile(s, tm)

    def body(z_ref, dy_ref, g_ref, b_ref, w_ref, bs_ref, dz_ref, dw_ref, dbs_ref, dg_ref, db_ref, dvn_buf):
        first = pl.program_id(0) == 0
        u, u_grad = _gelu_and_grad(z_ref[:, :D_MODEL].astype(F32))
        vg, v_grad = _gelu_and_grad(z_ref[:, D_MODEL:].astype(F32))
        xhat, rstd, vn = _sgu_norm(vg, g_ref[...], b_ref[...])
        vn = vn.astype(BF16)
        dyv = dy_ref[...]
        dmixed = dyv * u
        dz_gate = dyv * u_grad
        row = lax.broadcasted_iota(I32, (SGU_CHUNK, SGU_CHUNK), 0)
        col = lax.broadcasted_iota(I32, (SGU_CHUNK, SGU_CHUNK), 1)
        dws, dbss = [], []
        for grp in range(SGU_GROUPS):
            w = _tril_bf16(w_ref[grp])
            cols = slice(grp * LANES, (grp + 1) * LANES)
            dw = jnp.zeros((SGU_CHUNK, SGU_CHUNK), F32)
            dbs = jnp.zeros((SGU_CHUNK, 1), F32)
            for ch in range(tm // SGU_CHUNK):
                rows = slice(ch * SGU_CHUNK, (ch + 1) * SGU_CHUNK)
                vblk = vn[rows, cols]
                mixed = jnp.dot(w, vblk, preferred_element_type=F32) + bs_ref[grp]
                dz_ref[rows, cols] = (dz_gate[rows, cols] * mixed).astype(BF16)
                dm = dmixed[rows, cols]
                dmb = dm.astype(BF16)
                dvn_buf[rows, cols] = lax.dot_general(w, dmb, TN_DIMS, preferred_element_type=F32)
                dw += lax.dot_general(dmb, vblk, NT_DIMS, preferred_element_type=F32)
                dbs += jnp.sum(dm, axis=-1, keepdims=True)
            dws.append(jnp.where(row >= col, dw, 0.0))
            dbss.append(jnp.broadcast_to(dbs, (SGU_CHUNK, SGU_CHUNK)))

        dvn = dvn_buf[...]
        dxhat = dvn * g_ref[...]
        dvg = rstd * (dxhat - jnp.mean(dxhat, axis=-1, keepdims=True) - xhat * jnp.mean(dxhat * xhat, axis=-1, keepdims=True))
        dz_ref[:, D_MODEL:] = (dvg * v_grad).astype(BF16)
        dlng, dlnb = jnp.sum(dvn * xhat, axis=0, keepdims=True), jnp.sum(dvn, axis=0, keepdims=True)

        @pl.when(first)
        def _():
            for grp in range(SGU_GROUPS):
                dw_ref[grp] = dws[grp]
                dbs_ref[grp] = dbss[grp]
            dg_ref[...] = dlng
            db_ref[...] = dlnb

        @pl.when(jnp.logical_not(first))
        def _():
            for grp in range(SGU_GROUPS):
                dw_ref[grp] += dws[grp]
                dbs_ref[grp] += dbss[grp]
            dg_ref[...] += dlng
            db_ref[...] += dlnb

    full3 = pl.BlockSpec((SGU_GROUPS, SGU_CHUNK, SGU_CHUNK), lambda i: (0, 0, 0))
    s3 = jax.ShapeDtypeStruct((SGU_GROUPS, SGU_CHUNK, SGU_CHUNK), F32)
    vshape = jax.ShapeDtypeStruct((1, D_MODEL), F32)
    core, rr = _call(
        body, grid=(s // tm,),
        in_specs=[_row_spec(tm, 2 * D_MODEL), _row_spec(tm, D_MODEL), _vec_spec(D_MODEL), _vec_spec(D_MODEL), full3, full3],
        out_specs=[_row_spec(tm, 2 * D_MODEL), full3, full3, _vec_spec(D_MODEL), _vec_spec(D_MODEL)],
        out_shape=[jax.ShapeDtypeStruct((s, 2 * D_MODEL), BF16), s3, s3, vshape, vshape],
        scratch_shapes=[pltpu.VMEM((tm, D_MODEL), F32)], operands=(z, dy, ln_g, ln_b, w_sp, b_sp), name=name, riders=riders)
    return _ret(core, rr, riders)


def _sigmoid(x):
    return 1.0 / (1.0 + jnp.exp(-x))


def ffn_up(h, w_gu, *, name, tm=512, riders=()):
    s = h.shape[0]
    tm = _row_tile(s, tm)

    def body(h_ref, wg_ref, wu_ref, d_ref, a_ref):
        hv = h_ref[...]
        sub = min(256, tm)
        for t in range(tm // sub):
            rows = slice(t * sub, (t + 1) * sub)
            g = jnp.dot(hv[rows], wg_ref[...], preferred_element_type=F32)
            u = jnp.dot(hv[rows], wu_ref[...], preferred_element_type=F32)
            sig = _sigmoid(g)
            silu = g * sig
            d_ref[0, rows, :] = (u * (sig + silu * (1.0 - sig))).astype(BF16)
            d_ref[1, rows, :] = silu.astype(BF16)
            a_ref[rows, :] = (silu * u).astype(BF16)

    core, rr = _call(
        body, grid=(2, s // tm),
        in_specs=[pl.BlockSpec((tm, D_MODEL), lambda j, i: (i, 0)),
                  pl.BlockSpec((None, D_MODEL, FF_HALF), lambda j, i: (j, 0, 0)),
                  pl.BlockSpec((None, D_MODEL, FF_HALF), lambda j, i: (j + 2, 0, 0))],
        out_specs=[pl.BlockSpec((2, tm, FF_HALF), lambda j, i: (0, i, j)), pl.BlockSpec((tm, FF_HALF), lambda j, i: (i, j))],
        out_shape=[jax.ShapeDtypeStruct((2, s, D_FF), BF16), jax.ShapeDtypeStruct((s, D_FF), BF16)],
        operands=(h, w_gu, w_gu), sem=("parallel", "parallel"), name=name, riders=riders)
    return _ret(core, rr, riders)


def ffn_dact(df, w_d, gu, *, name, tm=512, riders=()):
    s = df.shape[0]
    tm = _row_tile(s, tm)

    def body(df_ref, w_ref, d_ref, o_ref):
        da = lax.dot_general(df_ref[...], w_ref[...], NT_DIMS, preferred_element_type=F32)
        o_ref[0] = (da * d_ref[0].astype(F32)).astype(BF16)
        o_ref[1] = (da * d_ref[1].astype(F32)).astype(BF16)

    planes = pl.BlockSpec((2, tm, FF_HALF), lambda j, i: (0, i, j))
    core, rr = _call(
        body, grid=(2, s // tm),
        in_specs=[pl.BlockSpec((tm, D_MODEL), lambda j, i: (i, 0)), pl.BlockSpec((FF_HALF, D_MODEL), lambda j, i: (j, 0)), planes],
        out_specs=[planes], out_shape=[jax.ShapeDtypeStruct((2, s, D_FF), BF16)], operands=(df, w_d, gu),
        sem=("parallel", "parallel"), name=name, riders=riders)
    return _ret(core, rr, riders)


def _weight_tile(rows):
    for tr in (512, 352, 256, 128):
        if rows % tr == 0:
            return tr
    return rows


def place_shard(w, layer, chip_arr, dtype, *, name, riders=()):
    _, r, c = w.shape
    tr = _weight_tile(r)

    def body(chip_ref, w_ref, o_ref):
        o_ref[...] = w_ref[...].astype(dtype)

    core, rr = _call(
        body, grid=(r // tr,), prefetch=(chip_arr,),
        in_specs=[pl.BlockSpec((None, tr, c), lambda i, chip: (layer, i, 0))],
        out_specs=[pl.BlockSpec((None, tr, c), lambda i, chip: (chip[0], i, 0))],
        out_shape=[jax.ShapeDtypeStruct((N_CHIPS, r, c), dtype)], operands=(w,), sem=("parallel",), name=name, riders=riders)
    return _ret(core, rr, riders)


def _adamw_math(w, g, m, v):
    m = ADAM_B1 * m + (1.0 - ADAM_B1) * g
    v = ADAM_B2 * v + (1.0 - ADAM_B2) * (g * g)
    m_hat = m / (1.0 - ADAM_B1 ** ADAM_STEP)
    v_hat = v / (1.0 - ADAM_B2 ** ADAM_STEP)
    delta = -ADAM_LR * (m_hat / (jnp.sqrt(v_hat) + ADAM_EPS) + ADAM_WD * w)
    return delta, m, v


def adamw(w, g, m, v, *, name):
    nl, r, c = w.shape
    tr = _weight_tile(r)

    def body(w_ref, g_ref, m_ref, v_ref, go_ref, d_ref, mo_ref, vo_ref):
        gv = g_ref[...]
        go_ref[...] = gv
        d_ref[...], mo_ref[...], vo_ref[...] = _adamw_math(w_ref[...], gv, m_ref[...], v_ref[...])

    spec = pl.BlockSpec((None, tr, c), lambda l, i: (l, i, 0))
    shape = jax.ShapeDtypeStruct(w.shape, F32)
    outs, _ = _call(body, grid=(nl, r // tr), in_specs=[spec] * 4, out_specs=[spec] * 4, out_shape=[shape] * 4,
                    operands=(w, g, m, v), sem=("parallel", "parallel"), name=name)
    return outs


def adamw_small(ws, gs, ms, vs, *, name):
    n = len(ws)

    def body(*refs):
        ins, outs = refs[:4 * n], refs[4 * n:]
        for t in range(n):
            gv = ins[n + t][...]
            outs[t][...] = gv
            outs[n + t][...], outs[2 * n + t][...], outs[3 * n + t][...] = _adamw_math(
                ins[t][...], gv, ins[2 * n + t][...], ins[3 * n + t][...])

    shapes = [jax.ShapeDtypeStruct(w.shape, F32) for w in ws]
    res = pl.pallas_call(body, out_shape=shapes * 4, name=name)(*ws, *gs, *ms, *vs)
    return res[:n], res[n:2 * n], res[2 * n:3 * n], res[3 * n:]


def pair_add(g, r1, c_arr, *, name):
    _, rows, cdim = g.shape
    h = rows // 2

    def body(c_ref, g_ref, r_ref, o_ref):
        o_ref[...] = (g_ref[...].astype(F32) + r_ref[...].astype(F32)).astype(o_ref.dtype)

    (out,), _ = _call(
        body, grid=(N_CHIPS,), prefetch=(c_arr,),
        in_specs=[pl.BlockSpec((None, h, cdim), lambda s, c: (s, c[0], 0)), pl.BlockSpec((None, h, cdim), lambda s, c: (s, 0, 0))],
        out_specs=[pl.BlockSpec((None, h, cdim), lambda s, c: (s, 0, 0))],
        out_shape=[jax.ShapeDtypeStruct((N_CHIPS, h, cdim), g.dtype)], operands=(g, r1), sem=("parallel",), name=name)
    return out


def final_add(g, r1, r2, jc_arr, *, dest_shape, lead, prev, name):
    _, rows, cdim = g.shape
    h = rows // 2

    def body(jc_ref, g_ref, r1_ref, r2_ref, *rest):
        o_ref = rest[-1]
        acc = g_ref[...].astype(F32) + r1_ref[...].astype(F32)
        for k in range(3):
            acc = acc + r2_ref[k].astype(F32)
        o_ref[...] = acc

    if lead is None:
        o_spec = pl.BlockSpec((h, cdim), lambda i, jc: (jc[1], 0))
    elif lead == "chip":
        o_spec = pl.BlockSpec((None, h, cdim), lambda i, jc: (jc[0], jc[1], 0))
    else:
        o_spec = pl.BlockSpec((None, h, cdim), lambda i, jc: (lead, jc[1], 0))
    in_specs = [pl.BlockSpec((None, h, cdim), lambda i, jc: (jc[0], jc[1], 0)),
                pl.BlockSpec((None, h, cdim), lambda i, jc: (jc[0], 0, 0)),
                pl.BlockSpec((3, h, cdim), lambda i, jc: (0, 0, 0))]
    operands = [g, r1, r2]
    aliases = None
    if prev is not None:
        in_specs.append(ANY)
        operands.append(prev)
        aliases = {3: 0}
    (out,), _ = _call(body, grid=(1,), prefetch=(jc_arr,), in_specs=in_specs, out_specs=[o_spec],
                      out_shape=[jax.ShapeDtypeStruct(dest_shape, F32)], operands=operands, aliases=aliases, name=name)
    return out


def _place():
    return lax.axis_index("x"), lax.axis_index("y"), lax.axis_index("c")


def _partner(x, y, k):
    return (1 - x if k >> 1 else x), (1 - y if k & 1 else y)


WHOLE = (0, 1, 1)


def _half(rows, sel, dtype, piece=WHOLE):
    lo, hi, n = piece
    align = 16 if dtype == BF16 else 8
    step = rows // 2 // n
    assert rows // 2 == step * n and step % align == 0
    return pl.ds(pl.multiple_of(sel * (rows // 2) + lo * step, align), (hi - lo) * step)


def _rider(peers, inputs, aliased, fresh, nsem, copies, arrivals):
    def start(ins, outs, send, recv):
        for cp in copies(ins, outs, send, recv):
            cp.start()

    def finish(ins, outs, send, recv):
        for cp in arrivals(ins, outs, send, recv):
            cp.wait_recv()
        for cp in copies(ins, outs, send, recv):
            cp.wait_send()

    return types.SimpleNamespace(peers=peers, inputs=list(inputs), aliased=list(aliased), fresh=list(fresh), nsem=nsem,
                                 start=start, finish=finish)


def _remote(src, dst, send, recv, idx, dev):
    return pltpu.make_async_remote_copy(src_ref=src, dst_ref=dst, send_sem=send.at[idx], recv_sem=recv.at[idx],
                                        device_id=dev, device_id_type=MESH)


def gather_ici_rider(fulls, pieces=None):
    nt = len(fulls)
    pieces = pieces or [WHOLE] * nt

    def region(outs, t, slot, sel):
        return outs[t].at[slot, _half(fulls[t].shape[1], sel, fulls[t].dtype, pieces[t])]

    def copies(ins, outs, send, recv):
        x, y, c = _place()
        res = []
        for t in range(nt):
            for k in (1, 2, 3):
                px, py = _partner(x, y, k)
                mine = region(outs, t, 2 * x + y, c)
                res.append(_remote(mine, mine, send, recv, 3 * t + k - 1, (px, py, c)))
        return res

    def arrivals(ins, outs, send, recv):
        x, y, c = _place()
        res = []
        for t in range(nt):
            for k in (1, 2, 3):
                px, py = _partner(x, y, k)
                theirs = region(outs, t, 2 * px + py, c)
                res.append(_remote(theirs, theirs, send, recv, 3 * t + k - 1, (x, y, c)))
        return res

    return _rider("chips", fulls, range(nt), [], 3 * nt, copies, arrivals)


def gather_d2d_rider(fulls, pieces=None):
    nt = len(fulls)
    pieces = pieces or [WHOLE] * nt

    def region(outs, t, slot, sel):
        return outs[t].at[slot, _half(fulls[t].shape[1], sel, fulls[t].dtype, pieces[t])]

    def both(outs, send, recv, mine):
        x, y, c = _place()
        res = []
        for t in range(nt):
            for k in (1, 2, 3):
                px, py = _partner(x, y, k)
                part = region(outs, t, 2 * px + py, c if mine else 1 - c)
                res.append(_remote(part, part, send, recv, 3 * t + k - 1, (x, y, 1 - c)))
        return res

    return _rider("sibling", fulls, range(nt), [], 3 * nt, lambda i, o, s, r: both(o, s, r, True),
                  lambda i, o, s, r: both(o, s, r, False))


def exchange_rider(grads):
    nt = len(grads)

    def both(ins, outs, send, recv):
        x, y, c = _place()
        return [_remote(ins[t].at[:, _half(grads[t].shape[1], 1 - c, grads[t].dtype)], outs[t], send, recv, t, (x, y, 1 - c))
                for t in range(nt)]

    fresh = [jax.ShapeDtypeStruct((N_CHIPS, g.shape[1] // 2, g.shape[2]), g.dtype) for g in grads]
    return _rider("sibling", grads, [], fresh, nt, both, both)


def scatter_rider(parts):
    nt = len(parts)

    def both(ins, outs, send, recv):
        x, y, c = _place()
        res = []
        for t in range(nt):
            for k in (1, 2, 3):
                px, py = _partner(x, y, k)
                res.append(_remote(ins[t].at[2 * px + py], outs[t].at[k - 1], send, recv, 3 * t + k - 1, (px, py, c)))
        return res

    fresh = [jax.ShapeDtypeStruct((3,) + p.shape[1:], p.dtype) for p in parts]
    return _rider("chips", parts, [], fresh, 3 * nt, both, both)


def broadcast_rider(bufs, items):
    def region(outs, item, sel):
        bi, lead = item
        ref = outs[bi]
        if lead == "chip":
            x, y, _ = _place()
            ref = ref.at[2 * x + y]
        elif lead is not None:
            ref = ref.at[lead]
        return ref.at[_half(ref.shape[0], sel, F32)]

    def both(outs, send, recv, mine):
        x, y, c = _place()
        res = []
        for i, item in enumerate(items):
            part = region(outs, item, c if mine else 1 - c)
            res.append(_remote(part, part, send, recv, i, (x, y, 1 - c)))
        return res

    return _rider("sibling", bufs, range(len(bufs)), [], len(items), lambda i, o, s, r: both(o, s, r, True),
                  lambda i, o, s, r: both(o, s, r, False))


def allcast_rider(buf):
    peers = [(k, flip) for k in range(N_CHIPS) for flip in (0, 1) if (k, flip) != (0, 0)]

    def both(outs, send, recv, mine):
        x, y, c = _place()
        res = []
        for i, (k, flip) in enumerate(peers):
            px, py = _partner(x, y, k)
            pc = 1 - c if flip else c
            slot, sel = (2 * x + y, c) if mine else (2 * px + py, pc)
            part = outs[0].at[slot, _half(buf.shape[1], sel, F32)]
            res.append(_remote(part, part, send, recv, i, (px, py, pc)))
        return res

    return _rider("everyone", [buf], [0], [], len(peers), lambda i, o, s, r: both(o, s, r, True),
                  lambda i, o, s, r: both(o, s, r, False))


def comm_call(riders, *, name):
    _, res = _call(None, riders=riders, name=name)
    return res


SLAB_ROWS = 192


def _pad_rows(a, rows=8):
    return jnp.pad(a, ((0, rows - a.shape[0]), (0, 0)))


def _pack_small(norm_grads, db_qkv, db_o, dsinks, db_sp, dln_g, dln_b, dw_sp, loss_part):
    parts = [
        jnp.concatenate(norm_grads, axis=0),
        _pad_rows(jnp.pad(db_qkv, ((0, 0), (0, 2 * D_MODEL - QKV_WIDTH))).reshape(2, D_MODEL)),
        _pad_rows(db_o),
        _pad_rows(jnp.pad(dsinks.reshape(1, N_Q_HEADS), ((0, 0), (0, D_MODEL - N_Q_HEADS)))),
        _pad_rows(db_sp.reshape(1, D_MODEL)),
        _pad_rows(jnp.concatenate([dln_g, dln_b, jnp.pad(loss_part[0:1], ((0, 0), (0, D_MODEL - LANES)))], axis=0)),
        dw_sp.reshape(SGU_CHUNK, D_MODEL),
    ]
    slab = jnp.concatenate(parts, axis=0)
    return jnp.pad(slab, ((0, SLAB_ROWS - slab.shape[0]), (0, 0))).reshape(N_CHIPS, SLAB_ROWS // N_CHIPS, D_MODEL)


def _unpack_small(slab, j):
    slab = slab.reshape(SLAB_ROWS, D_MODEL)
    norms = [slab[2 * i:2 * i + 2] for i in range(4)]
    db_qkv = slab[8:10].reshape(1, 2 * D_MODEL)[:, :QKV_WIDTH]
    db_o = slab[16:17]
    dsinks = slab[24:25, :N_Q_HEADS]
    db_sp = slab[32:33].reshape(SGU_GROUPS, SGU_CHUNK)
    width = D_MODEL // N_CHIPS
    dln_g = lax.dynamic_slice(slab[40:41], (0, j * width), (1, width))
    dln_b = lax.dynamic_slice(slab[41:42], (0, j * width), (1, width))
    dw_sp = slab[48:48 + SGU_CHUNK].reshape(SGU_GROUPS * SGU_CHUNK, SGU_CHUNK)
    return norms, db_qkv, db_o, dsinks, db_sp, dln_g, dln_b, dw_sp, slab[42, 0]


class _GradReduce:
    def __init__(self, c_arr, jc_arr, dest_shapes):
        self.c_arr, self.jc_arr, self.dest_shapes = c_arr, jc_arr, dest_shapes
        self.grad, self.sibling, self.pair, self.chips, self.dest = {}, {}, {}, {}, {}

    def exchange(self, tags):
        return exchange_rider([self.grad[t] for t in tags])

    def exchanged(self, tags, res):
        for t, r in zip(tags, res):
            self.sibling[t] = r
            self.pair[t] = pair_add(self.grad[t], r, self.c_arr, name=f"pair_add_{t}")

    def scatter(self, tags):
        return scatter_rider([self.pair[t] for t in tags])

    def scattered(self, tags, res, where):
        for t, r in zip(tags, res):
            name, lead = where[t]
            self.dest[name] = final_add(self.grad[t], self.sibling[t], r, self.jc_arr, dest_shape=self.dest_shapes[name],
                                        lead=lead, prev=self.dest.get(name), name=f"final_add_{t}")

    def broadcast(self, items):
        names = []
        for n, _ in items:
            if n not in names:
                names.append(n)
        return names, broadcast_rider([self.dest[n] for n in names], [(names.index(n), lead) for n, lead in items])

    def broadcasted(self, names, res):
        for n, r in zip(names, res):
            self.dest[n] = r


def kernel(x, norm_mix_pre, norm_mix_post, norm_ffn_pre, norm_ffn_post, attn_w_qkv, attn_b_qkv, attn_sinks, attn_w_o, attn_b_o, sgu_w_in, sgu_ln_g, sgu_ln_b, sgu_w_spatial, sgu_b_spatial, sgu_w_out, ffn_w_gate_up, ffn_w_down, loss_target, m_norm_mix_pre, m_norm_mix_post, m_norm_ffn_pre, m_norm_ffn_post, m_attn_w_qkv, m_attn_b_qkv, m_attn_sinks, m_attn_w_o, m_attn_b_o, m_sgu_w_in, m_sgu_ln_g, m_sgu_ln_b, m_sgu_w_spatial, m_sgu_b_spatial, m_sgu_w_out, m_ffn_w_gate_up, m_ffn_w_down, v_norm_mix_pre, v_norm_mix_post, v_norm_ffn_pre, v_norm_ffn_post, v_attn_w_qkv, v_attn_b_qkv, v_attn_sinks, v_attn_w_o, v_attn_b_o, v_sgu_w_in, v_sgu_ln_g, v_sgu_ln_b, v_sgu_w_spatial, v_sgu_b_spatial, v_sgu_w_out, v_ffn_w_gate_up, v_ffn_w_down):
    s = x.shape[1]
    x0 = x.reshape(s, D_MODEL)
    target = loss_target.reshape(s, D_MODEL)
    mx, my, mc = lax.axis_index("x"), lax.axis_index("y"), lax.axis_index("c")
    chip = 2 * mx + my
    chip_arr = jnp.reshape(chip, (1,)).astype(I32)
    c_arr = jnp.reshape(mc, (1,)).astype(I32)
    jc_arr = jnp.stack([chip, mc]).astype(I32)
    zero_bias = jnp.zeros((1, D_MODEL), F32)

    def gain(p, i):
        return p[i:i + 1]

    big = [attn_w_qkv, attn_w_o, sgu_w_in, sgu_w_out, ffn_w_gate_up, ffn_w_gate_up, ffn_w_down, ffn_w_down]
    layers = [0, 0, 0, 0, 0, 1, 0, 1]
    tags = ["qkv", "wo", "win", "wout", "wgu0", "wgu1", "wd0", "wd1"]
    full = {t: place_shard(w, l, chip_arr, BF16, name=f"place_{t}") for w, l, t in zip(big, layers, tags) if t != "wgu1"}
    ln_pack = _pad_rows(jnp.concatenate([sgu_ln_g, sgu_ln_b], axis=0), 16)[None]
    full["ln"] = place_shard(ln_pack, 0, chip_arr, F32, name="place_ln")

    def split(items):
        return [i if isinstance(i, str) else i[0] for i in items], [WHOLE if isinstance(i, str) else tuple(i[1:]) for i in items]

    def ici(*items):
        names, pieces = split(items)
        return gather_ici_rider([full[n] for n in names], pieces)

    def d2d(*items):
        names, pieces = split(items)
        return gather_d2d_rider([full[n] for n in names], pieces)

    def landed(items, res):
        for n, r in zip(split(items)[0], res):
            full[n] = r

    cos, sin = _rope_tables(s)
    sink_rows = jnp.broadcast_to(
        jnp.repeat(attn_sinks.reshape(N_KV_HEADS, GQA_GROUP), WINDOW, axis=1)[:, None, :], (N_KV_HEADS, 8, ROWS))
    w_sp = sgu_w_spatial.reshape(SGU_GROUPS, SGU_CHUNK, SGU_CHUNK)
    b_sp = jnp.broadcast_to(sgu_b_spatial.reshape(SGU_GROUPS, SGU_CHUNK)[:, :, None], (SGU_GROUPS, SGU_CHUNK, LANES))

    h0, (res,) = prenorm(x0, gain(norm_mix_pre, 0), name="prenorm_0", riders=[ici("qkv", "ln")])
    landed(("qkv", "ln"), res)
    full["wgu1"], (res,) = place_shard(ffn_w_gate_up, 1, chip_arr, BF16, name="place_wgu1", riders=[d2d("qkv", "ln")])
    landed(("qkv", "ln"), res)
    ln_g = full["ln"][:, 0, :].reshape(1, D_MODEL)
    ln_b = full["ln"][:, 1, :].reshape(1, D_MODEL)

    def hosted(call, stages):
        outputs, results = call([{"ici": ici, "d2d": d2d}[kind](*items) for kind, items in stages])
        for (_, items), res in zip(stages, results):
            landed(items, res)
        return outputs

    qkv = hosted(lambda r: qkv_proj(h0, full["qkv"], attn_b_qkv, cos, sin, name="qkv_proj", riders=r),
                 [("ici", ("wo", ("wgu0", 0, 4, 8)))])
    o = hosted(lambda r: attn_fwd(qkv, sink_rows, name="attn_fwd", riders=r),
               [("d2d", ("wo",)), ("ici", (("wgu0", 4, 8, 8), ("wd0", 0, 1, 2)))])
    w_o = full["wo"].reshape(Q_WIDTH, D_MODEL)
    x1, h1, m0 = hosted(lambda r: proj_residual_norm(o, w_o, x0, attn_b_o, gain(norm_mix_post, 0), gain(norm_ffn_pre, 0),
                                                     name="attn_out_norm", riders=r),
                        [("d2d", ("wgu0",)), ("ici", (("wd0", 1, 2, 2), ("win", 0, 2, 8)))])
    gu0, a0 = hosted(lambda r: ffn_up(h1, full["wgu0"], name="ffn_up_0", riders=r),
                     [("d2d", ("wd0",)), ("ici", (("win", 2, 8, 8), "wout", ("wgu1", 0, 4, 8)))])
    w_d0 = full["wd0"].reshape(D_FF, D_MODEL)
    x2, h2, f0 = hosted(lambda r: proj_residual_norm(a0, w_d0, x1, zero_bias, gain(norm_ffn_post, 0), gain(norm_mix_pre, 1),
                                                     name="ffn_down_norm_0", riders=r),
                        [("d2d", ("win", "wout")), ("ici", (("wgu1", 4, 7, 8),))])
    w_in = full["win"]
    z, y = hosted(lambda r: sgu_in_fwd(h2, w_in, ln_g, ln_b, w_sp, b_sp, name="sgu_in_fwd", riders=r),
                  [("ici", (("wgu1", 7, 8, 8), ("wd1", 0, 1, 2)))])
    w_out = full["wout"].reshape(D_MODEL, D_MODEL)
    x3, h3, m1 = hosted(lambda r: proj_residual_norm(y, w_out, x2, zero_bias, gain(norm_mix_post, 1), gain(norm_ffn_pre, 1),
                                                     name="sgu_out_norm", riders=r),
                        [("d2d", ("wgu1",)), ("ici", (("wd1", 1, 2, 2),))])
    w_qkv, w_gu0, w_gu1 = full["qkv"], full["wgu0"], full["wgu1"]
    gu1, a1 = hosted(lambda r: ffn_up(h3, w_gu1, name="ffn_up_1", riders=r), [("d2d", ("wd1",))])
    w_d1 = full["wd1"].reshape(D_FF, D_MODEL)
    dx4, df1, dg_fpost1, loss_part = proj_loss_head(a1, w_d1, x3, gain(norm_ffn_post, 1), target, name="ffn_down_loss")

    red = _GradReduce(c_arr, jc_arr, {
        "qkv": attn_w_qkv.shape[1:], "wo": attn_w_o.shape[1:], "win": sgu_w_in.shape[1:], "wout": sgu_w_out.shape[1:],
        "wgu": ffn_w_gate_up.shape, "wd": ffn_w_down.shape, "slab": (N_CHIPS, SLAB_ROWS // N_CHIPS, D_MODEL)})
    where = {"qkv": ("qkv", None), "wo": ("wo", None), "win": ("win", None), "wout": ("wout", None), "wgu0": ("wgu", 0),
             "wgu1": ("wgu", 1), "wd0": ("wd", 0), "wd1": ("wd", 1), "small": ("slab", "chip")}

    dgu1, dx3, dm1, dg_fpre1, dg_mpost1, _ = ffn_bwd_rows(
        df1, w_d1, gu1, w_gu1, dx4, x3, gain(norm_ffn_pre, 1), m1, gain(norm_mix_post, 1), name="ffn_bwd_rows_1")
    red.grad["wd1"] = mm_tn(a1, df1, shard_major=False, tm=256, tn=D_MODEL, name="dw_down_1").reshape(
        N_CHIPS, D_FF // N_CHIPS, D_MODEL)
    red.grad["wgu1"], (res,) = mm_tn(h3, dgu1, shard_major=True, tm=512, tn=FF_HALF, name="dw_gate_up_1",
                                     riders=[red.exchange(["wd1"])])
    red.exchanged(["wd1"], res)
    dy, (res,) = mm_nt(dm1, w_out, out_dtype=F32, name="dy_sgu", riders=[red.exchange(["wgu1"])])
    red.exchanged(["wgu1"], res)
    red.grad["wout"] = mm_tn(y, dm1, shard_major=False, tm=512, tn=D_MODEL, name="dw_sgu_out").reshape(
        N_CHIPS, D_MODEL // N_CHIPS, D_MODEL)
    (dz, dw_sp, db_sp, dln_g, dln_b), (res_a, res_b) = sgu_bwd(
        z, dy, ln_g, ln_b, w_sp, b_sp, name="sgu_bwd", riders=[red.scatter(["wgu1"]), red.exchange(["wout"])])
    red.scattered(["wgu1"], res_a, where)
    red.exchanged(["wout"], res_b)
    names, rider = red.broadcast([("wgu", 1)])
    red.grad["win"], (res_a, res_b) = mm_tn(h2, dz, shard_major=True, tm=D_MODEL, tn=2 * D_MODEL // N_CHIPS, name="dw_sgu_in",
                                            riders=[rider, red.scatter(["wout"])])
    red.broadcasted(names, res_a)
    red.scattered(["wout"], res_b, where)
    names, rider = red.broadcast([("wout", None)])
    (dx2, df0, dg_mpre1, dg_fpost0, _), (res_a, res_b) = dh_norm_bwd_pair(
        dz, w_in, dx3, x2, gain(norm_mix_pre, 1), f0, gain(norm_ffn_post, 0), name="dh_sgu_norm",
        riders=[red.exchange(["win"]), rider])
    red.exchanged(["win"], res_a)
    red.broadcasted(names, res_b)
    (dgu0, dx1, dm0, dg_fpre0, dg_mpost0, db_o), (res,) = ffn_bwd_rows(
        df0, w_d0, gu0, w_gu0, dx2, x1, gain(norm_ffn_pre, 0), m0, gain(norm_mix_post, 0), name="ffn_bwd_rows_0",
        riders=[red.scatter(["wd1", "win"])])
    red.scattered(["wd1", "win"], res, where)
    names, rider = red.broadcast([("wd", 1), ("win", None)])
    dw_d0, (res,) = mm_tn(a0, df0, shard_major=False, tm=256, tn=D_MODEL, name="dw_down_0", riders=[rider])
    red.broadcasted(names, res)
    red.grad["wd0"] = dw_d0.reshape(N_CHIPS, D_FF // N_CHIPS, D_MODEL)
    red.grad["wgu0"], (res,) = mm_tn(h1, dgu0, shard_major=True, tm=512, tn=FF_HALF, name="dw_gate_up_0",
                                     riders=[red.exchange(["wd0"])])
    red.exchanged(["wd0"], res)
    do, (res,) = mm_nt(dm0, w_o, out_dtype=BF16, name="do_attn", riders=[red.exchange(["wgu0"])])
    red.exchanged(["wgu0"], res)
    red.grad["wo"] = mm_tn(o, dm0, shard_major=False, tm=512, tn=D_MODEL, name="dw_attn_out").reshape(
        N_CHIPS, Q_WIDTH // N_CHIPS, D_MODEL)
    (dq, dkc, dkp, dvc, dvp, dsink), (res_a, res_b) = attn_bwd(
        qkv, sink_rows, do, name="attn_bwd", riders=[red.scatter(["wgu0", "wd0"]), red.exchange(["wo"])])
    red.scattered(["wgu0", "wd0"], res_a, where)
    red.exchanged(["wo"], res_b)
    names, rider = red.broadcast([("wgu", 0), ("wd", 0)])
    (dqkv, db_qkv), (res_a, res_b) = rope_bwd(dq, dkc, dkp, dvc, dvp, cos, sin, name="rope_bwd",
                                              riders=[rider, red.scatter(["wo"])])
    red.broadcasted(names, res_a)
    red.scattered(["wo"], res_b, where)
    names, rider = red.broadcast([("wo", None)])
    red.grad["qkv"], (res,) = mm_tn(h0, dqkv, shard_major=True, tm=D_MODEL, tn=QKV_WIDTH // N_CHIPS, name="dw_qkv",
                                    riders=[rider])
    red.broadcasted(names, res)
    dh0, (res,) = mm_nt(dqkv, w_qkv, out_dtype=F32, tm=1024, name="dh_attn", riders=[red.exchange(["qkv"])])
    red.exchanged(["qkv"], res)
    grad_x, dg_mpre0 = norm_bwd_last(dx1, dh0, x0, gain(norm_mix_pre, 0), name="norm_bwd_in")

    norm_grads = [jnp.concatenate(p, axis=0) for p in
                  ((dg_mpre0, dg_mpre1), (dg_mpost0, dg_mpost1), (dg_fpre0, dg_fpre1), (dg_fpost0, dg_fpost1))]
    red.grad["small"] = _pack_small(norm_grads, db_qkv, db_o, dsink[:, :, 0, 0], db_sp[:, :, 0], dln_g, dln_b, dw_sp,
                                    loss_part)
    res_a, res_b = comm_call([red.scatter(["qkv"]), red.exchange(["small"])], name="tail_1")
    red.scattered(["qkv"], res_a, where)
    red.exchanged(["small"], res_b)
    names, rider = red.broadcast([("qkv", None)])
    res_a, res_b = comm_call([red.scatter(["small"]), rider], name="tail_2")
    red.scattered(["small"], res_a, where)
    red.broadcasted(names, res_b)
    ((slab_full,),) = comm_call([allcast_rider(red.dest["slab"])], name="tail_3")
    g_qkv, g_wo, g_win, g_wout, g_wgu, g_wd = (red.dest[n] for n in ("qkv", "wo", "win", "wout", "wgu", "wd"))
    g_norms, g_bqkv, g_bo, g_sinks, g_bsp, g_lng, g_lnb, g_wsp, loss = _unpack_small(slab_full, chip)

    def big_update(w, g, m, v, tag):
        return adamw(w, g.reshape(w.shape), m, v, name=f"adamw_{tag}")

    upd = {
        "attn_w_qkv": big_update(attn_w_qkv, g_qkv, m_attn_w_qkv, v_attn_w_qkv, "qkv"),
        "attn_w_o": big_update(attn_w_o, g_wo, m_attn_w_o, v_attn_w_o, "wo"),
        "sgu_w_in": big_update(sgu_w_in, g_win, m_sgu_w_in, v_sgu_w_in, "win"),
        "sgu_w_out": big_update(sgu_w_out, g_wout, m_sgu_w_out, v_sgu_w_out, "wout"),
        "ffn_w_gate_up": big_update(ffn_w_gate_up, g_wgu, m_ffn_w_gate_up, v_ffn_w_gate_up, "wgu"),
        "ffn_w_down": big_update(ffn_w_down, g_wd, m_ffn_w_down, v_ffn_w_down, "wd"),
    }
    small_names = ["norm_mix_pre", "norm_mix_post", "norm_ffn_pre", "norm_ffn_post", "attn_b_qkv", "attn_sinks", "attn_b_o",
                   "sgu_ln_g", "sgu_ln_b", "sgu_w_spatial", "sgu_b_spatial"]
    small_w = [norm_mix_pre, norm_mix_post, norm_ffn_pre, norm_ffn_post, attn_b_qkv, attn_sinks, attn_b_o, sgu_ln_g, sgu_ln_b,
               sgu_w_spatial, sgu_b_spatial]
    small_m = [m_norm_mix_pre, m_norm_mix_post, m_norm_ffn_pre, m_norm_ffn_post, m_attn_b_qkv, m_attn_sinks, m_attn_b_o,
               m_sgu_ln_g, m_sgu_ln_b, m_sgu_w_spatial, m_sgu_b_spatial]
    small_v = [v_norm_mix_pre, v_norm_mix_post, v_norm_ffn_pre, v_norm_ffn_post, v_attn_b_qkv, v_attn_sinks, v_attn_b_o,
               v_sgu_ln_g, v_sgu_ln_b, v_sgu_w_spatial, v_sgu_b_spatial]
    small_g = g_norms + [g_bqkv, g_sinks, g_bo, g_lng, g_lnb, g_wsp, g_bsp]

    def flat2(a):
        return a.reshape(-1, a.shape[-1])

    res = adamw_small([flat2(a) for a in small_w], [flat2(a) for a in small_g], [flat2(a) for a in small_m],
                      [flat2(a) for a in small_v], name="adamw_small")
    for i, nm in enumerate(small_names):
        upd[nm] = tuple(r[i].reshape(small_w[i].shape) for r in res)

    order = ["norm_mix_pre", "norm_mix_post", "norm_ffn_pre", "norm_ffn_post", "attn_w_qkv", "attn_b_qkv", "attn_sinks",
             "attn_w_o", "attn_b_o", "sgu_w_in", "sgu_ln_g", "sgu_ln_b", "sgu_w_spatial", "sgu_b_spatial", "sgu_w_out",
             "ffn_w_gate_up", "ffn_w_down"]
    outs = [loss, grad_x.reshape(1, s, D_MODEL)]
    for part in range(4):
        outs += [upd[nm][part] for nm in order]
    return tuple(outs)
```

```python
import types

import numpy as np
import jax
import jax.numpy as jnp
from jax import lax
from jax.experimental import pallas as pl
from jax.experimental.pallas import tpu as pltpu

F32 = jnp.float32
BF16 = jnp.bfloat16
I32 = jnp.int32

D_MODEL = 1024
HEAD_DIM = 64
N_Q_HEADS = 16
N_KV_HEADS = 4
GQA_GROUP = 4
WINDOW = 128
Q_WIDTH = 1024
KV_WIDTH = 256
QKV_WIDTH = 1536
ROPE_THETA = 10000.0
SGU_GROUPS = 8
SGU_CHUNK = 128
D_FF = 2816
FF_HALF = D_FF // 2
EPS = 1e-6
N_CHIPS = 4
LANES = 128

ADAM_LR = 0.001
ADAM_B1 = 0.9
ADAM_B2 = 0.999
ADAM_EPS = 1e-08
ADAM_WD = 0.01
ADAM_STEP = 10

VMEM_LIMIT = 52 * 1024 * 1024
MESH = pl.DeviceIdType.MESH
NEG = -1e30
NT_DIMS = (((1,), (1,)), ((), ()))
TN_DIMS = (((0,), (0,)), ((), ()))
NN_DIMS = (((1,), (0,)), ((), ()))
ANY = pl.BlockSpec(memory_space=pl.ANY)


def _row_tile(s, want):
    return want if s % want == 0 else s


PEER_KINDS = ("sibling", "chips", "sibling+chips", "everyone")


def _peer_kind(riders):
    kinds = {r.peers for r in riders}
    if not kinds:
        return None
    if "everyone" in kinds:
        return "everyone"
    return "sibling+chips" if len(kinds) == 2 else kinds.pop()


def _peer_barrier(kind):
    x, y, c = _place()
    chips = [(*_partner(x, y, k), c) for k in (1, 2, 3)]
    peers = {"sibling": [(x, y, 1 - c)], "chips": chips, "sibling+chips": [(x, y, 1 - c)] + chips,
             "everyone": [(x, y, 1 - c)] + chips + [(px, py, 1 - c) for px, py, _ in chips]}[kind]
    barrier = pltpu.get_barrier_semaphore()
    for dev in peers:
        pl.semaphore_signal(barrier, inc=1, device_id=dev, device_id_type=MESH)
    pl.semaphore_wait(barrier, len(peers))


def _call(body, *, name, grid=(), in_specs=(), out_specs=(), out_shape=(), scratch_shapes=(), operands=(), prefetch=(),
          aliases=None, riders=(), sem=None):
    n_pre, n_in, n_out, n_scr = len(prefetch), len(operands), len(out_shape), len(scratch_shapes)
    in_specs, out_specs, out_shape = list(in_specs), list(out_specs), list(out_shape)
    operands, scratch_shapes = list(operands), list(scratch_shapes)
    io_alias = {n_pre + i: o for i, o in (aliases or {}).items()}
    for r in riders:
        base_in, base_out = n_pre + len(operands), len(out_shape)
        operands += list(r.inputs)
        in_specs += [ANY] * len(r.inputs)
        for pos, i in enumerate(r.aliased):
            io_alias[base_in + i] = base_out + pos
            out_shape.append(jax.ShapeDtypeStruct(r.inputs[i].shape, r.inputs[i].dtype))
        out_shape += list(r.fresh)
        out_specs += [ANY] * (len(r.aliased) + len(r.fresh))
        scratch_shapes += [pltpu.SemaphoreType.DMA((r.nsem,)), pltpu.SemaphoreType.DMA((r.nsem,))]

    def wrapped(*refs):
        pre, p = refs[:n_pre], n_pre
        core_in, p = refs[p:p + n_in], p + n_in
        r_in = []
        for r in riders:
            r_in.append(refs[p:p + len(r.inputs)])
            p += len(r.inputs)
        core_out, p = refs[p:p + n_out], p + n_out
        r_out = []
        for r in riders:
            k = len(r.aliased) + len(r.fresh)
            r_out.append(refs[p:p + k])
            p += k
        core_scr, p = refs[p:p + n_scr], p + n_scr
        r_sem = [refs[p + 2 * i:p + 2 * i + 2] for i in range(len(riders))]

        def edge(at_last, fns):
            def run():
                if not at_last:
                    _peer_barrier(peer_kind)
                for i, r in enumerate(riders):
                    getattr(r, fns)(r_in[i], r_out[i], r_sem[i][0], r_sem[i][1])
            if not riders:
                return
            if not grid:
                run()
                return
            cond = None
            for d, n in enumerate(grid):
                c = pl.program_id(d) == (n - 1 if at_last else 0)
                cond = c if cond is None else jnp.logical_and(cond, c)
            pl.when(cond)(run)

        edge(False, "start")
        if body is not None:
            body(*pre, *core_in, *core_out, *core_scr)
        edge(True, "finish")

    if sem is None or riders:
        sem = ("arbitrary",) * len(grid)
    kwargs = dict(out_shape=out_shape, input_output_aliases=io_alias, name=name)
    peer_kind = _peer_kind(riders)
    collective = {} if peer_kind is None else {"collective_id": PEER_KINDS.index(peer_kind)}
    if grid:
        kwargs["compiler_params"] = pltpu.CompilerParams(dimension_semantics=sem, vmem_limit_bytes=VMEM_LIMIT, **collective)
    elif collective:
        kwargs["compiler_params"] = pltpu.CompilerParams(**collective)
    if n_pre:
        kwargs["grid_spec"] = pltpu.PrefetchScalarGridSpec(
            num_scalar_prefetch=n_pre, grid=grid, in_specs=in_specs, out_specs=out_specs, scratch_shapes=scratch_shapes)
    else:
        kwargs.update(grid=grid, in_specs=in_specs, out_specs=out_specs, scratch_shapes=scratch_shapes)
    res = pl.pallas_call(wrapped, **kwargs)(*prefetch, *operands)
    core, rest, rider_res = list(res[:n_out]), list(res[n_out:]), []
    for r in riders:
        k = len(r.aliased) + len(r.fresh)
        rider_res.append(rest[:k])
        rest = rest[k:]
    return core, rider_res


def _mm_call(*, grid, in_specs, out_spec, out_shape, dims, nk, kaxis, acc_shape, name, operands, riders=()):
    out_dtype = out_shape.dtype

    def body(a_ref, b_ref, o_ref, *scratch):
        p = lax.dot_general(a_ref[...].astype(BF16), b_ref[...].astype(BF16), dims, preferred_element_type=F32)
        if nk == 1:
            o_ref[...] = p.astype(out_dtype)
        else:
            acc = scratch[0]
            kk = pl.program_id(kaxis)

            @pl.when(kk == 0)
            def _():
                acc[...] = p

            @pl.when(kk > 0)
            def _():
                acc[...] += p

            @pl.when(kk == nk - 1)
            def _():
                o_ref[...] = acc[...].astype(out_dtype)

    sem = ["parallel"] * len(grid)
    if nk > 1:
        sem[kaxis] = "arbitrary"
    (out,), rider_res = _call(
        body, grid=grid, in_specs=in_specs, out_specs=[out_spec], out_shape=[out_shape],
        scratch_shapes=[pltpu.VMEM(acc_shape, F32)] if nk > 1 else [], operands=operands, name=name, riders=riders,
        sem=tuple(sem))
    return (out, rider_res) if riders else out


def mm_nn(a, w, *, out_dtype, name, tm=512, tn=512, riders=()):
    m, k = a.shape
    tm = _row_tile(m, tm)
    if w.ndim == 3:
        ns = w.shape[2]
        grid = (N_CHIPS, m // tm)
        w_spec = pl.BlockSpec((None, k, ns), lambda j, i: (j, 0, 0))
        o_spec = pl.BlockSpec((tm, ns), lambda j, i: (i, j))
        n = N_CHIPS * ns
    else:
        n = w.shape[1]
        grid = (n // tn, m // tm)
        w_spec = pl.BlockSpec((k, tn), lambda j, i: (0, j))
        o_spec = pl.BlockSpec((tm, tn), lambda j, i: (i, j))
    return _mm_call(grid=grid, in_specs=[pl.BlockSpec((tm, k), lambda j, i: (i, 0)), w_spec], out_spec=o_spec,
                    out_shape=jax.ShapeDtypeStruct((m, n), out_dtype), dims=NN_DIMS, nk=1, kaxis=0, acc_shape=None,
                    name=name, operands=(a, w), riders=riders)


def mm_nt(a, w, *, out_dtype, name, tm=512, tn=512, riders=()):
    if w.ndim == 2:
        m, n = a.shape
        kout = w.shape[0]
        tm = _row_tile(m, tm)
        return _mm_call(grid=(kout // tn, m // tm),
                        in_specs=[pl.BlockSpec((tm, n), lambda j, i: (i, 0)), pl.BlockSpec((tn, n), lambda j, i: (j, 0))],
                        out_spec=pl.BlockSpec((tm, tn), lambda j, i: (i, j)),
                        out_shape=jax.ShapeDtypeStruct((m, kout), out_dtype), dims=NT_DIMS, nk=1, kaxis=0,
                        acc_shape=None, name=name, operands=(a, w), riders=riders)
    _, kout, ns = w.shape
    planes = a.ndim == 3
    m = a.shape[1] if planes else a.shape[0]
    tm = _row_tile(m, tm)
    a_spec = pl.BlockSpec((2, tm, 2 * ns), lambda i: (0, i, 0)) if planes else pl.BlockSpec((tm, N_CHIPS * ns), lambda i: (i, 0))

    def body(a_ref, w0, w1, w2, w3, o_ref):
        acc = None
        for j, w_ref in enumerate((w0, w1, w2, w3)):
            if planes:
                a_j = a_ref[j // 2, :, (j % 2) * ns:(j % 2 + 1) * ns]
            else:
                a_j = a_ref[:, j * ns:(j + 1) * ns]
            p = lax.dot_general(a_j, w_ref[...], NT_DIMS, preferred_element_type=F32)
            acc = p if acc is None else acc + p
        o_ref[...] = acc.astype(out_dtype)

    def shard(j):
        return pl.BlockSpec((None, kout, ns), lambda i: (j, 0, 0))

    (out,), rider_res = _call(
        body, grid=(m // tm,), in_specs=[a_spec] + [shard(j) for j in range(N_CHIPS)],
        out_specs=[pl.BlockSpec((tm, kout), lambda i: (i, 0))], out_shape=[jax.ShapeDtypeStruct((m, kout), out_dtype)],
        operands=(a, w, w, w, w), sem=("parallel",), name=name, riders=riders)
    return (out, rider_res) if riders else out


def mm_tn(a, b, *, shard_major, name, tm, tn, tk=None, out_dtype=BF16, riders=()):
    s, m = a.shape
    tk = s if tk is None else _row_tile(s, tk)
    if b.ndim == 3:
        n = 2 * b.shape[2]
        b_spec = pl.BlockSpec((None, tk, tn), lambda j, i, kk: (j // 2, kk, j % 2))
    else:
        n = b.shape[1]
        b_spec = pl.BlockSpec((tk, tn), lambda j, i, kk: (kk, j))
    if shard_major:
        assert tn == n // N_CHIPS
        o_spec = pl.BlockSpec((None, tm, tn), lambda j, i, kk: (j, i, 0))
        o_shape = jax.ShapeDtypeStruct((N_CHIPS, m, tn), out_dtype)
    else:
        o_spec = pl.BlockSpec((tm, tn), lambda j, i, kk: (i, j))
        o_shape = jax.ShapeDtypeStruct((m, n), out_dtype)
    return _mm_call(grid=(n // tn, m // tm, s // tk),
                    in_specs=[pl.BlockSpec((tk, tm), lambda j, i, kk: (kk, i)), b_spec], out_spec=o_spec,
                    out_shape=o_shape, dims=TN_DIMS, nk=s // tk, kaxis=2, acc_shape=(tm, tn), name=name, operands=(a, b),
                    riders=riders)


def _rstd(x):
    return lax.rsqrt(jnp.mean(x * x, axis=-1, keepdims=True) + EPS)


def _rms_bwd(dy, x, g):
    r = _rstd(x)
    xhat = x * r
    gy = dy * g
    dx = r * (gy - xhat * jnp.mean(gy * xhat, axis=-1, keepdims=True))
    return dx, jnp.sum(dy * xhat, axis=0, keepdims=True)


def _accum(ref, val, first):
    @pl.when(first)
    def _():
        ref[...] = val

    @pl.when(jnp.logical_not(first))
    def _():
        ref[...] += val


def _row_spec(tm, width):
    return pl.BlockSpec((tm, width), lambda i: (i, 0))


def _vec_spec(width):
    return pl.BlockSpec((1, width), lambda i: (0, 0))


def _ret(core, rider_res, riders):
    core = core[0] if len(core) == 1 else core
    return (core, rider_res) if riders else core


def prenorm(x, g, *, name, tm=256, riders=()):
    s = x.shape[0]
    tm = _row_tile(s, tm)

    def body(x_ref, g_ref, h_ref):
        xv = x_ref[...]
        h_ref[...] = (xv * _rstd(xv) * g_ref[...]).astype(BF16)

    core, rr = _call(
        body, grid=(s // tm,), in_specs=[_row_spec(tm, D_MODEL), _vec_spec(D_MODEL)], out_specs=[_row_spec(tm, D_MODEL)],
        out_shape=[jax.ShapeDtypeStruct((s, D_MODEL), BF16)], operands=(x, g), sem=("parallel",), name=name, riders=riders)
    return _ret(core, rr, riders)


def proj_residual_norm(a, w, x, bias, g_post, g_next, *, name, tm=256, riders=()):
    s, k = a.shape
    tm = _row_tile(s, tm)

    def body(a_ref, w_ref, x_ref, b_ref, gp_ref, gn_ref, xo_ref, h_ref, m_ref):
        mv = jnp.dot(a_ref[...], w_ref[...], preferred_element_type=F32) + b_ref[...]
        m_ref[...] = mv.astype(BF16)
        xn = x_ref[...] + mv * _rstd(mv) * gp_ref[...]
        xo_ref[...] = xn
        h_ref[...] = (xn * _rstd(xn) * gn_ref[...]).astype(BF16)

    row, vec = _row_spec(tm, D_MODEL), _vec_spec(D_MODEL)
    core, rr = _call(
        body, grid=(s // tm,),
        in_specs=[_row_spec(tm, k), pl.BlockSpec((k, D_MODEL), lambda i: (0, 0)), row, vec, vec, vec], out_specs=[row, row, row],
        out_shape=[jax.ShapeDtypeStruct((s, D_MODEL), F32), jax.ShapeDtypeStruct((s, D_MODEL), BF16),
                   jax.ShapeDtypeStruct((s, D_MODEL), BF16)],
        operands=(a, w, x, bias, g_post, g_next), sem=("parallel",), name=name, riders=riders)
    return _ret(core, rr, riders)


def proj_loss_head(a, w, x, g_post, target, *, name, tm=256, riders=()):
    s, k = a.shape
    tm = _row_tile(s, tm)

    def body(a_ref, w_ref, x_ref, g_ref, t_ref, dx_ref, df_ref, dg_ref, loss_ref):
        first = pl.program_id(0) == 0
        fv = jnp.dot(a_ref[...], w_ref[...], preferred_element_type=F32)
        g = g_ref[...]
        err = x_ref[...] + fv * _rstd(fv) * g - t_ref[...]
        dx = err * (1.0 / D_MODEL)
        dx_ref[...] = dx
        df, dg = _rms_bwd(dx, fv, g)
        df_ref[...] = df.astype(BF16)
        _accum(dg_ref, dg, first)
        part = jnp.sum(jnp.sum(err * err, axis=-1, keepdims=True), axis=0, keepdims=True) * (0.5 / D_MODEL)
        _accum(loss_ref, jnp.broadcast_to(part, (8, LANES)), first)

    row, vec = _row_spec(tm, D_MODEL), _vec_spec(D_MODEL)
    core, rr = _call(
        body, grid=(s // tm,), in_specs=[_row_spec(tm, k), pl.BlockSpec((k, D_MODEL), lambda i: (0, 0)), row, vec, row],
        out_specs=[row, row, vec, pl.BlockSpec((8, LANES), lambda i: (0, 0))],
        out_shape=[jax.ShapeDtypeStruct((s, D_MODEL), F32), jax.ShapeDtypeStruct((s, D_MODEL), BF16),
                   jax.ShapeDtypeStruct((1, D_MODEL), F32), jax.ShapeDtypeStruct((8, LANES), F32)],
        operands=(a, w, x, g_post, target), name=name, riders=riders)
    return _ret(core, rr, riders)


def dh_norm_bwd_pair(a, w, dres, x, g_pre, m, g_post, *, name, tm=512, sub=256, riders=()):
    _, kout, ns = w.shape
    planes = a.ndim == 3
    s = x.shape[0]
    tm = _row_tile(s, tm)
    sub = min(sub, tm)
    a_spec = pl.BlockSpec((2, tm, 2 * ns), lambda i: (0, i, 0)) if planes else pl.BlockSpec((tm, N_CHIPS * ns), lambda i: (i, 0))

    def body(a_ref, w0, w1, w2, w3, dres_ref, x_ref, gpre_ref, m_ref, gpost_ref, dx_ref, dm_ref, dgpre_ref, dgpost_ref, db_ref):
        first = pl.program_id(0) == 0
        sums = None
        for t in range(tm // sub):
            rows = slice(t * sub, (t + 1) * sub)
            dh = None
            for j, w_ref in enumerate((w0, w1, w2, w3)):
                a_j = a_ref[j // 2, rows, (j % 2) * ns:(j % 2 + 1) * ns] if planes else a_ref[rows, j * ns:(j + 1) * ns]
                p = lax.dot_general(a_j, w_ref[...], NT_DIMS, preferred_element_type=F32)
                dh = p if dh is None else dh + p
            d1, dgpre = _rms_bwd(dh, x_ref[rows, :], gpre_ref[...])
            dx = dres_ref[rows, :] + d1
            dx_ref[rows, :] = dx
            dm, dgpost = _rms_bwd(dx, m_ref[rows, :].astype(F32), gpost_ref[...])
            dm_ref[rows, :] = dm.astype(BF16)
            part = (dgpre, dgpost, jnp.sum(dm, axis=0, keepdims=True))
            sums = part if sums is None else tuple(u + v for u, v in zip(sums, part))
        _accum(dgpre_ref, sums[0], first)
        _accum(dgpost_ref, sums[1], first)
        _accum(db_ref, sums[2], first)

    def shard(j):
        return pl.BlockSpec((None, kout, ns), lambda i: (j, 0, 0))

    row, vec = _row_spec(tm, D_MODEL), _vec_spec(D_MODEL)
    vshape = jax.ShapeDtypeStruct((1, D_MODEL), F32)
    core, rr = _call(
        body, grid=(s // tm,), in_specs=[a_spec] + [shard(j) for j in range(N_CHIPS)] + [row, row, vec, row, vec],
        out_specs=[row, row, vec, vec, vec],
        out_shape=[jax.ShapeDtypeStruct((s, D_MODEL), F32), jax.ShapeDtypeStruct((s, D_MODEL), BF16), vshape, vshape, vshape],
        operands=(a, w, w, w, w, dres, x, g_pre, m, g_post), name=name, riders=riders)
    return _ret(core, rr, riders)


def ffn_bwd_rows(df, w_d, d_planes, w_gu, dres, x, g_pre, m, g_post, *, name, tm=256, riders=()):
    s = x.shape[0]
    tm = _row_tile(s, tm)

    def body(df_ref, wd_ref, d_ref, w0, w1, w2, w3, dres_ref, x_ref, gpre_ref, m_ref, gpost_ref,
             o_ref, dx_ref, dm_ref, dgpre_ref, dgpost_ref, db_ref):
        first = pl.program_id(0) == 0
        dfv = df_ref[...]
        dh = None
        for half, (wg_ref, wu_ref) in enumerate(((w0, w2), (w1, w3))):
            cols = slice(half * FF_HALF, (half + 1) * FF_HALF)
            da = lax.dot_general(dfv, wd_ref[cols, :], NT_DIMS, preferred_element_type=F32)
            dg = (da * d_ref[0, :, cols].astype(F32)).astype(BF16)
            du = (da * d_ref[1, :, cols].astype(F32)).astype(BF16)
            o_ref[0, :, cols] = dg
            o_ref[1, :, cols] = du
            p = lax.dot_general(dg, wg_ref[...], NT_DIMS, preferred_element_type=F32)
            p += lax.dot_general(du, wu_ref[...], NT_DIMS, preferred_element_type=F32)
            dh = p if dh is None else dh + p
        d1, dgpre = _rms_bwd(dh, x_ref[...], gpre_ref[...])
        dx = dres_ref[...] + d1
        dx_ref[...] = dx
        dm, dgpost = _rms_bwd(dx, m_ref[...].astype(F32), gpost_ref[...])
        dm_ref[...] = dm.astype(BF16)
        _accum(dgpre_ref, dgpre, first)
        _accum(dgpost_ref, dgpost, first)
        _accum(db_ref, jnp.sum(dm, axis=0, keepdims=True), first)

    def resident(shape, index):
        return pl.BlockSpec(shape, index, pipeline_mode=pl.Buffered(1))

    planes = pl.BlockSpec((2, tm, D_FF), lambda i: (0, i, 0))
    row, vec = _row_spec(tm, D_MODEL), _vec_spec(D_MODEL)
    vshape = jax.ShapeDtypeStruct((1, D_MODEL), F32)
    shards = [resident((None, D_MODEL, FF_HALF), (lambda j: (lambda i: (j, 0, 0)))(j)) for j in range(N_CHIPS)]
    core, rr = _call(
        body, grid=(s // tm,),
        in_specs=[row, resident((D_FF, D_MODEL), lambda i: (0, 0)), planes] + shards + [row, row, vec, row, vec],
        out_specs=[planes, row, row, vec, vec, vec],
        out_shape=[jax.ShapeDtypeStruct((2, s, D_FF), BF16), jax.ShapeDtypeStruct((s, D_MODEL), F32),
                   jax.ShapeDtypeStruct((s, D_MODEL), BF16), vshape, vshape, vshape],
        operands=(df, w_d, d_planes, w_gu, w_gu, w_gu, w_gu, dres, x, g_pre, m, g_post), name=name, riders=riders)
    return _ret(core, rr, riders)


def norm_bwd_last(dres, dh, x, g_pre, *, name, tm=256, riders=()):
    s = x.shape[0]
    tm = _row_tile(s, tm)

    def body(dres_ref, dh_ref, x_ref, g_ref, dx_ref, dg_ref):
        d1, dg = _rms_bwd(dh_ref[...], x_ref[...], g_ref[...])
        dx_ref[...] = dres_ref[...] + d1
        _accum(dg_ref, dg, pl.program_id(0) == 0)

    row, vec = _row_spec(tm, D_MODEL), _vec_spec(D_MODEL)
    core, rr = _call(
        body, grid=(s // tm,), in_specs=[row, row, row, vec], out_specs=[row, vec],
        out_shape=[jax.ShapeDtypeStruct((s, D_MODEL), F32), jax.ShapeDtypeStruct((1, D_MODEL), F32)],
        operands=(dres, dh, x, g_pre), name=name, riders=riders)
    return _ret(core, rr, riders)


def _rope_tables(s):
    half = HEAD_DIM // 2
    inv_freq = np.float32(ROPE_THETA) ** (-(np.arange(half, dtype=np.float32) * np.float32(2.0)) / np.float32(HEAD_DIM))
    ang = np.arange(s, dtype=np.float32)[:, None] * inv_freq[None, :]
    cos, sin = np.cos(ang).astype(np.float32), np.sin(ang).astype(np.float32)
    return jnp.asarray(np.tile(cos, (1, 4))), jnp.asarray(np.concatenate([-sin, sin, -sin, sin], axis=1))


def _swap_halves(x):
    lane = lax.broadcasted_iota(I32, x.shape, 1)
    return jnp.where((lane & (HEAD_DIM - 1)) < HEAD_DIM // 2, pltpu.roll(x, LANES - 32, 1), pltpu.roll(x, 32, 1))


N_ROPE_BLOCKS = (Q_WIDTH + KV_WIDTH) // LANES


def qkv_proj(h, w, bias, cos, sin, *, name, tm=512, riders=()):
    s, k = h.shape
    ns = w.shape[2]
    tm = _row_tile(s, tm)

    def body(h_ref, w_ref, b_ref, c_ref, s_ref, o_ref):
        j = pl.program_id(0)
        sub = min(256, tm)
        for t in range(tm // sub):
            rows = slice(t * sub, (t + 1) * sub)
            p = jnp.dot(h_ref[rows, :], w_ref[...], preferred_element_type=F32) + b_ref[...]
            cosv, sinv = c_ref[rows, :], s_ref[rows, :]
            for blk in range(ns // LANES):
                xb = p[:, blk * LANES:(blk + 1) * LANES]
                roped = xb * cosv + _swap_halves(xb) * sinv
                is_qk = j * (ns // LANES) + blk < N_ROPE_BLOCKS
                o_ref[rows, blk * LANES:(blk + 1) * LANES] = jnp.where(is_qk, roped, xb).astype(BF16)

    core, rr = _call(
        body, grid=(N_CHIPS, s // tm),
        in_specs=[pl.BlockSpec((tm, k), lambda j, i: (i, 0)), pl.BlockSpec((None, k, ns), lambda j, i: (j, 0, 0)),
                  pl.BlockSpec((1, ns), lambda j, i: (0, j)), pl.BlockSpec((tm, LANES), lambda j, i: (i, 0)),
                  pl.BlockSpec((tm, LANES), lambda j, i: (i, 0))],
        out_specs=[pl.BlockSpec((tm, ns), lambda j, i: (i, j))], out_shape=[jax.ShapeDtypeStruct((s, N_CHIPS * ns), BF16)],
        operands=(h, w, bias, cos, sin), sem=("parallel", "parallel"), name=name, riders=riders)
    return _ret(core, rr, riders)


def rope_bwd(dq, dkc, dkp, dvc, dvp, cos, sin, *, name, riders=()):
    s = dq.shape[0]
    tm = 2 * WINDOW if s % (2 * WINDOW) == 0 else WINDOW
    nb = s // tm

    def body(dq_ref, dkc_ref, dkp_ref, dkp_next_ref, dvc_ref, dvp_ref, dvp_next_ref, c_ref, s_ref, o_ref, db_ref):
        i = pl.program_id(0)
        has_next = (i < nb - 1).astype(F32)
        cosv, sinv = c_ref[...], s_ref[...]

        def shifted(ref, next_ref, cols):
            last = has_next * next_ref[:WINDOW, cols].astype(F32)
            return last if tm == WINDOW else jnp.concatenate([ref[WINDOW:, cols].astype(F32), last], axis=0)

        parts = []
        for blk in range(QKV_WIDTH // LANES):
            if blk < Q_WIDTH // LANES:
                g = dq_ref[:, blk * LANES:(blk + 1) * LANES].astype(F32)
            else:
                own, prv, nxt = (dkc_ref, dkp_ref, dkp_next_ref) if blk < N_ROPE_BLOCKS else (dvc_ref, dvp_ref, dvp_next_ref)
                cols = slice((blk % 2) * LANES, (blk % 2 + 1) * LANES)
                g = own[:, cols].astype(F32) + shifted(prv, nxt, cols)
            if blk < N_ROPE_BLOCKS:
                g = g * cosv + _swap_halves(g * sinv)
            o_ref[:, blk * LANES:(blk + 1) * LANES] = g.astype(BF16)
            parts.append(jnp.sum(g, axis=0, keepdims=True))
        sums = jnp.concatenate(parts, axis=1)
        _accum(db_ref, sums, i == 0)

    own_spec = _row_spec(tm, KV_WIDTH)
    next_spec = pl.BlockSpec((tm, KV_WIDTH), lambda i: (jnp.minimum(i + 1, nb - 1), 0))
    core, rr = _call(
        body, grid=(nb,),
        in_specs=[_row_spec(tm, Q_WIDTH), own_spec, own_spec, next_spec, own_spec, own_spec, next_spec,
                  _row_spec(tm, LANES), _row_spec(tm, LANES)],
        out_specs=[_row_spec(tm, QKV_WIDTH), _vec_spec(QKV_WIDTH)],
        out_shape=[jax.ShapeDtypeStruct((s, QKV_WIDTH), BF16), jax.ShapeDtypeStruct((1, QKV_WIDTH), F32)],
        operands=(dq, dkc, dkp, dkp, dvc, dvp, dvp, cos, sin), name=name, riders=riders)
    return _ret(core, rr, riders)


ROWS = GQA_GROUP * WINDOW


def _prev_slots():
    kpos = lax.broadcasted_iota(I32, (WINDOW, ROWS), 0)
    qpos = lax.broadcasted_iota(I32, (WINDOW, ROWS), 1) & (WINDOW - 1)
    return kpos > qpos


def _head_cols(ref, head):
    return ref[:, head * HEAD_DIM:(head + 1) * HEAD_DIM]


def _stack_heads(ref, h):
    return jnp.concatenate([_head_cols(ref, GQA_GROUP * h + g) for g in range(GQA_GROUP)], axis=0)


def _band(prev_ref, cur_ref, h):
    return jnp.concatenate([_head_cols(prev_ref, h), _head_cols(cur_ref, h)], axis=0)


def _pick(prev, band):
    return jnp.where(prev, band[:WINDOW], band[WINDOW:])


def _spread(prev, x):
    return jnp.concatenate([jnp.where(prev, x, 0.0), jnp.where(prev, 0.0, x)], axis=0).astype(BF16)


def _attn_probs(s_band, sink, prev, has_prev):
    scale = HEAD_DIM ** -0.5
    s = jnp.where(prev, jnp.where(has_prev, s_band[:WINDOW], NEG), s_band[WINDOW:]) * scale
    m = jnp.maximum(jnp.max(s, axis=0, keepdims=True), sink)
    e, es = jnp.exp(s - m), jnp.exp(sink - m)
    inv = 1.0 / (jnp.sum(e, axis=0, keepdims=True) + es)
    return e * inv, es * inv


def _attn_specs(nb):
    kcol, vcol = Q_WIDTH // KV_WIDTH, Q_WIDTH // KV_WIDTH + 1
    q_spec = pl.BlockSpec((WINDOW, Q_WIDTH), lambda n: (n, 0))
    return [q_spec,
            pl.BlockSpec((WINDOW, KV_WIDTH), lambda n: (n, kcol)),
            pl.BlockSpec((WINDOW, KV_WIDTH), lambda n: (jnp.maximum(n - 1, 0), kcol)),
            pl.BlockSpec((WINDOW, KV_WIDTH), lambda n: (n, vcol)),
            pl.BlockSpec((WINDOW, KV_WIDTH), lambda n: (jnp.maximum(n - 1, 0), vcol)),
            pl.BlockSpec((N_KV_HEADS, 8, ROWS), lambda n: (0, 0, 0))]


def attn_fwd(qkv, sink_rows, *, name, riders=()):
    s = qkv.shape[0]

    def body(q_ref, kc_ref, kp_ref, vc_ref, vp_ref, sink_ref, o_ref):
        prev = _prev_slots()
        has_prev = pl.program_id(0) > 0
        heads = range(N_KV_HEADS)
        s_bands = [lax.dot_general(_band(kp_ref, kc_ref, h), _stack_heads(q_ref, h), NT_DIMS, preferred_element_type=F32)
                   for h in heads]
        p_bands = [_spread(prev, _attn_probs(s_bands[h], sink_ref[h, 0:1, :], prev, has_prev)[0]) for h in heads]
        outs = [lax.dot_general(_band(vp_ref, vc_ref, h), p_bands[h], TN_DIMS, preferred_element_type=F32).T for h in heads]
        for h in heads:
            for g in range(GQA_GROUP):
                head = GQA_GROUP * h + g
                o_ref[:, head * HEAD_DIM:(head + 1) * HEAD_DIM] = outs[h][g * WINDOW:(g + 1) * WINDOW].astype(BF16)

    core, rr = _call(
        body, grid=(s // WINDOW,), in_specs=_attn_specs(s // WINDOW), out_specs=[pl.BlockSpec((WINDOW, Q_WIDTH), lambda n: (n, 0))],
        out_shape=[jax.ShapeDtypeStruct((s, Q_WIDTH), BF16)], operands=(qkv, qkv, qkv, qkv, qkv, sink_rows), sem=("parallel",),
        name=name, riders=riders)
    return _ret(core, rr, riders)


def attn_bwd(qkv, sink_rows, do, *, name, riders=()):
    s = qkv.shape[0]

    def body(q_ref, kc_ref, kp_ref, vc_ref, vp_ref, sink_ref, do_ref, dq_ref, dkc_ref, dkp_ref, dvc_ref, dvp_ref, dsink_ref):
        n = pl.program_id(0)
        prev = _prev_slots()
        scale = HEAD_DIM ** -0.5
        heads = range(N_KV_HEADS)
        qs, dos = [_stack_heads(q_ref, h) for h in heads], [_stack_heads(do_ref, h) for h in heads]
        kbands, vbands = [_band(kp_ref, kc_ref, h) for h in heads], [_band(vp_ref, vc_ref, h) for h in heads]
        s_bands = [lax.dot_general(kbands[h], qs[h], NT_DIMS, preferred_element_type=F32) for h in heads]
        dp_bands = [lax.dot_general(vbands[h], dos[h], NT_DIMS, preferred_element_type=F32) for h in heads]
        ds_bands, p_bands, parts = [], [], []
        for h in heads:
            p, ps = _attn_probs(s_bands[h], sink_ref[h, 0:1, :], prev, n > 0)
            dp = _pick(prev, dp_bands[h])
            delta = jnp.sum(p * dp, axis=0, keepdims=True)
            ds_bands.append(_spread(prev, p * (dp - delta) * scale))
            p_bands.append(_spread(prev, p))
            dsink = -(ps * delta)
            for g in range(GQA_GROUP):
                parts.append(jnp.broadcast_to(jnp.sum(dsink[:, g * WINDOW:(g + 1) * WINDOW], axis=1, keepdims=True), (8, LANES)))
        for h in heads:
            dk = jnp.dot(ds_bands[h], qs[h], preferred_element_type=F32).astype(BF16)
            dv = jnp.dot(p_bands[h], dos[h], preferred_element_type=F32).astype(BF16)
            dq = lax.dot_general(kbands[h], ds_bands[h], TN_DIMS, preferred_element_type=F32).T
            cols = slice(h * HEAD_DIM, (h + 1) * HEAD_DIM)
            dkp_ref[:, cols], dkc_ref[:, cols] = dk[:WINDOW], dk[WINDOW:]
            dvp_ref[:, cols], dvc_ref[:, cols] = dv[:WINDOW], dv[WINDOW:]
            for g in range(GQA_GROUP):
                head = GQA_GROUP * h + g
                dq_ref[:, head * HEAD_DIM:(head + 1) * HEAD_DIM] = dq[g * WINDOW:(g + 1) * WINDOW].astype(BF16)

        @pl.when(n == 0)
        def _():
            for i, part in enumerate(parts):
                dsink_ref[i // GQA_GROUP, i % GQA_GROUP] = part

        @pl.when(n > 0)
        def _():
            for i, part in enumerate(parts):
                dsink_ref[i // GQA_GROUP, i % GQA_GROUP] += part

    rows_q = pl.BlockSpec((WINDOW, Q_WIDTH), lambda n: (n, 0))
    rows_kv = pl.BlockSpec((WINDOW, KV_WIDTH), lambda n: (n, 0))
    kv_shape = jax.ShapeDtypeStruct((s, KV_WIDTH), BF16)
    core, rr = _call(
        body, grid=(s // WINDOW,), in_specs=_attn_specs(s // WINDOW) + [rows_q],
        out_specs=[rows_q, rows_kv, rows_kv, rows_kv, rows_kv,
                   pl.BlockSpec((N_KV_HEADS, GQA_GROUP, 8, LANES), lambda n: (0, 0, 0, 0))],
        out_shape=[jax.ShapeDtypeStruct((s, Q_WIDTH), BF16), kv_shape, kv_shape, kv_shape, kv_shape,
                   jax.ShapeDtypeStruct((N_KV_HEADS, GQA_GROUP, 8, LANES), F32)],
        operands=(qkv, qkv, qkv, qkv, qkv, sink_rows, do), sem=("arbitrary",), name=name, riders=riders)
    return _ret(core, rr, riders)


GELU_C = 0.7978845608028654
GELU_A = 0.044715


def _gelu(x):
    return 0.5 * x * (1.0 + jnp.tanh(x * (GELU_C + (GELU_C * GELU_A) * (x * x))))


def _gelu_and_grad(x):
    x2 = x * x
    t = jnp.tanh(x * (GELU_C + (GELU_C * GELU_A) * x2))
    half_x, one_t = 0.5 * x, 1.0 + t
    return half_x * one_t, 0.5 * one_t + half_x * (1.0 - t * t) * (GELU_C + (3.0 * GELU_C * GELU_A) * x2)


def _tril_bf16(w):
    row = lax.broadcasted_iota(I32, (SGU_CHUNK, SGU_CHUNK), 0)
    col = lax.broadcasted_iota(I32, (SGU_CHUNK, SGU_CHUNK), 1)
    return jnp.where(row >= col, w, 0.0).astype(BF16)


def _sgu_norm(vg, g, b):
    mu = jnp.mean(vg, axis=-1, keepdims=True)
    cen = vg - mu
    rstd = lax.rsqrt(jnp.mean(cen * cen, axis=-1, keepdims=True) + EPS)
    xhat = cen * rstd
    return xhat, rstd, xhat * g + b


def sgu_in_fwd(h, w_in, ln_g, ln_b, w_sp, b_sp, *, name, tm=256, riders=()):
    s, k = h.shape
    ns = w_in.shape[2]
    tm = _row_tile(s, tm)

    def body(h_ref, w0, w1, w2, w3, g_ref, b_ref, w_ref, bs_ref, z_ref, y_ref):
        hv = h_ref[...]
        zs = [jnp.dot(hv, w_ref_j[...], preferred_element_type=F32) for w_ref_j in (w0, w1, w2, w3)]
        for j, zj in enumerate(zs):
            z_ref[:, j * ns:(j + 1) * ns] = zj.astype(BF16)
        u = _gelu(jnp.concatenate(zs[:2], axis=1))
        _, _, vn = _sgu_norm(_gelu(jnp.concatenate(zs[2:], axis=1)), g_ref[...], b_ref[...])
        vn = vn.astype(BF16)
        for grp in range(SGU_GROUPS):
            w = _tril_bf16(w_ref[grp])
            cols = slice(grp * LANES, (grp + 1) * LANES)
            for ch in range(tm // SGU_CHUNK):
                rows = slice(ch * SGU_CHUNK, (ch + 1) * SGU_CHUNK)
                mixed = jnp.dot(w, vn[rows, cols], preferred_element_type=F32) + bs_ref[grp]
                y_ref[rows, cols] = (u[rows, cols] * mixed).astype(BF16)

    def shard(j):
        return pl.BlockSpec((None, k, ns), lambda i: (j, 0, 0))

    full3 = pl.BlockSpec((SGU_GROUPS, SGU_CHUNK, SGU_CHUNK), lambda i: (0, 0, 0))
    core, rr = _call(
        body, grid=(s // tm,),
        in_specs=[_row_spec(tm, k)] + [shard(j) for j in range(N_CHIPS)] + [_vec_spec(D_MODEL), _vec_spec(D_MODEL), full3, full3],
        out_specs=[_row_spec(tm, 2 * D_MODEL), _row_spec(tm, D_MODEL)],
        out_shape=[jax.ShapeDtypeStruct((s, 2 * D_MODEL), BF16), jax.ShapeDtypeStruct((s, D_MODEL), BF16)],
        operands=(h, w_in, w_in, w_in, w_in, ln_g, ln_b, w_sp, b_sp), sem=("parallel",), name=name, riders=riders)
    return _ret(core, rr, riders)


def sgu_bwd(z, dy, ln_g, ln_b, w_sp, b_sp, *, name, tm=256, riders=()):
    s = z.shape[0]
    tm = _row_tile(s, tm)

    def body(z_ref, dy_ref, g_ref, b_ref, w_ref, bs_ref, dz_ref, dw_ref, dbs_ref, dg_ref, db_ref, dvn_buf):
        first = pl.program_id(0) == 0
        u, u_grad = _gelu_and_grad(z_ref[:, :D_MODEL].astype(F32))
        vg, v_grad = _gelu_and_grad(z_ref[:, D_MODEL:].astype(F32))
        xhat, rstd, vn = _sgu_norm(vg, g_ref[...], b_ref[...])
        vn = vn.astype(BF16)
        dyv = dy_ref[...]
        dmixed = dyv * u
        dz_gate = dyv * u_grad
        row = lax.broadcasted_iota(I32, (SGU_CHUNK, SGU_CHUNK), 0)
        col = lax.broadcasted_iota(I32, (SGU_CHUNK, SGU_CHUNK), 1)
        dws, dbss = [], []
        for grp in range(SGU_GROUPS):
            w = _tril_bf16(w_ref[grp])
            cols = slice(grp * LANES, (grp + 1) * LANES)
            dw = jnp.zeros((SGU_CHUNK, SGU_CHUNK), F32)
            dbs = jnp.zeros((SGU_CHUNK, 1), F32)
            for ch in range(tm // SGU_CHUNK):
                rows = slice(ch * SGU_CHUNK, (ch + 1) * SGU_CHUNK)
                vblk = vn[rows, cols]
                mixed = jnp.dot(w, vblk, preferred_element_type=F32) + bs_ref[grp]
                dz_ref[rows, cols] = (dz_gate[rows, cols] * mixed).astype(BF16)
                dm = dmixed[rows, cols]
                dmb = dm.astype(BF16)
                dvn_buf[rows, cols] = lax.dot_general(w, dmb, TN_DIMS, preferred_element_type=F32)
                dw += lax.dot_general(dmb, vblk, NT_DIMS, preferred_element_type=F32)
                dbs += jnp.sum(dm, axis=-1, keepdims=True)
            dws.append(jnp.where(row >= col, dw, 0.0))
            dbss.append(jnp.broadcast_to(dbs, (SGU_CHUNK, SGU_CHUNK)))

        dvn = dvn_buf[...]
        dxhat = dvn * g_ref[...]
        dvg = rstd * (dxhat - jnp.mean(dxhat, axis=-1, keepdims=True) - xhat * jnp.mean(dxhat * xhat, axis=-1, keepdims=True))
        dz_ref[:, D_MODEL:] = (dvg * v_grad).astype(BF16)
        dlng, dlnb = jnp.sum(dvn * xhat, axis=0, keepdims=True), jnp.sum(dvn, axis=0, keepdims=True)

        @pl.when(first)
        def _():
            for grp in range(SGU_GROUPS):
                dw_ref[grp] = dws[grp]
                dbs_ref[grp] = dbss[grp]
            dg_ref[...] = dlng
            db_ref[...] = dlnb

        @pl.when(jnp.logical_not(first))
        def _():
            for grp in range(SGU_GROUPS):
                dw_ref[grp] += dws[grp]
                dbs_ref[grp] += dbss[grp]
            dg_ref[...] += dlng
            db_ref[...] += dlnb

    full3 = pl.BlockSpec((SGU_GROUPS, SGU_CHUNK, SGU_CHUNK), lambda i: (0, 0, 0))
    s3 = jax.ShapeDtypeStruct((SGU_GROUPS, SGU_CHUNK, SGU_CHUNK), F32)
    vshape = jax.ShapeDtypeStruct((1, D_MODEL), F32)
    core, rr = _call(
        body, grid=(s // tm,),
        in_specs=[_row_spec(tm, 2 * D_MODEL), _row_spec(tm, D_MODEL), _vec_spec(D_MODEL), _vec_spec(D_MODEL), full3, full3],
        out_specs=[_row_spec(tm, 2 * D_MODEL), full3, full3, _vec_spec(D_MODEL), _vec_spec(D_MODEL)],
        out_shape=[jax.ShapeDtypeStruct((s, 2 * D_MODEL), BF16), s3, s3, vshape, vshape],
        scratch_shapes=[pltpu.VMEM((tm, D_MODEL), F32)], operands=(z, dy, ln_g, ln_b, w_sp, b_sp), name=name, riders=riders)
    return _ret(core, rr, riders)


def _sigmoid(x):
    return 1.0 / (1.0 + jnp.exp(-x))


def ffn_up(h, w_gu, *, name, tm=512, riders=()):
    s = h.shape[0]
    tm = _row_tile(s, tm)

    def body(h_ref, wg_ref, wu_ref, d_ref, a_ref):
        hv = h_ref[...]
        sub = min(256, tm)
        for t in range(tm // sub):
            rows = slice(t * sub, (t + 1) * sub)
            g = jnp.dot(hv[rows], wg_ref[...], preferred_element_type=F32)
            u = jnp.dot(hv[rows], wu_ref[...], preferred_element_type=F32)
            sig = _sigmoid(g)
            silu = g * sig
            d_ref[0, rows, :] = (u * (sig + silu * (1.0 - sig))).astype(BF16)
            d_ref[1, rows, :] = silu.astype(BF16)
            a_ref[rows, :] = (silu * u).astype(BF16)

    core, rr = _call(
        body, grid=(2, s // tm),
        in_specs=[pl.BlockSpec((tm, D_MODEL), lambda j, i: (i, 0)),
                  pl.BlockSpec((None, D_MODEL, FF_HALF), lambda j, i: (j, 0, 0)),
                  pl.BlockSpec((None, D_MODEL, FF_HALF), lambda j, i: (j + 2, 0, 0))],
        out_specs=[pl.BlockSpec((2, tm, FF_HALF), lambda j, i: (0, i, j)), pl.BlockSpec((tm, FF_HALF), lambda j, i: (i, j))],
        out_shape=[jax.ShapeDtypeStruct((2, s, D_FF), BF16), jax.ShapeDtypeStruct((s, D_FF), BF16)],
        operands=(h, w_gu, w_gu), sem=("parallel", "parallel"), name=name, riders=riders)
    return _ret(core, rr, riders)


def ffn_dact(df, w_d, gu, *, name, tm=512, riders=()):
    s = df.shape[0]
    tm = _row_tile(s, tm)

    def body(df_ref, w_ref, d_ref, o_ref):
        da = lax.dot_general(df_ref[...], w_ref[...], NT_DIMS, preferred_element_type=F32)
        o_ref[0] = (da * d_ref[0].astype(F32)).astype(BF16)
        o_ref[1] = (da * d_ref[1].astype(F32)).astype(BF16)

    planes = pl.BlockSpec((2, tm, FF_HALF), lambda j, i: (0, i, j))
    core, rr = _call(
        body, grid=(2, s // tm),
        in_specs=[pl.BlockSpec((tm, D_MODEL), lambda j, i: (i, 0)), pl.BlockSpec((FF_HALF, D_MODEL), lambda j, i: (j, 0)), planes],
        out_specs=[planes], out_shape=[jax.ShapeDtypeStruct((2, s, D_FF), BF16)], operands=(df, w_d, gu),
        sem=("parallel", "parallel"), name=name, riders=riders)
    return _ret(core, rr, riders)


def _weight_tile(rows):
    for tr in (512, 352, 256, 128):
        if rows % tr == 0:
            return tr
    return rows


def place_shard(w, layer, chip_arr, dtype, *, name, riders=()):
    _, r, c = w.shape
    tr = _weight_tile(r)

    def body(chip_ref, w_ref, o_ref):
        o_ref[...] = w_ref[...].astype(dtype)

    core, rr = _call(
        body, grid=(r // tr,), prefetch=(chip_arr,),
        in_specs=[pl.BlockSpec((None, tr, c), lambda i, chip: (layer, i, 0))],
        out_specs=[pl.BlockSpec((None, tr, c), lambda i, chip: (chip[0], i, 0))],
        out_shape=[jax.ShapeDtypeStruct((N_CHIPS, r, c), dtype)], operands=(w,), sem=("parallel",), name=name, riders=riders)
    return _ret(core, rr, riders)


def _adamw_math(w, g, m, v):
    m = ADAM_B1 * m + (1.0 - ADAM_B1) * g
    v = ADAM_B2 * v + (1.0 - ADAM_B2) * (g * g)
    m_hat = m / (1.0 - ADAM_B1 ** ADAM_STEP)
    v_hat = v / (1.0 - ADAM_B2 ** ADAM_STEP)
    delta = -ADAM_LR * (m_hat / (jnp.sqrt(v_hat) + ADAM_EPS) + ADAM_WD * w)
    return delta, m, v


def adamw(w, g, m, v, *, name):
    nl, r, c = w.shape
    tr = _weight_tile(r)

    def body(w_ref, g_ref, m_ref, v_ref, go_ref, d_ref, mo_ref, vo_ref):
        gv = g_ref[...]
        go_ref[...] = gv
        d_ref[...], mo_ref[...], vo_ref[...] = _adamw_math(w_ref[...], gv, m_ref[...], v_ref[...])

    spec = pl.BlockSpec((None, tr, c), lambda l, i: (l, i, 0))
    shape = jax.ShapeDtypeStruct(w.shape, F32)
    outs, _ = _call(body, grid=(nl, r // tr), in_specs=[spec] * 4, out_specs=[spec] * 4, out_shape=[shape] * 4,
                    operands=(w, g, m, v), sem=("parallel", "parallel"), name=name)
    return outs


def adamw_small(ws, gs, ms, vs, *, name):
    n = len(ws)

    def body(*refs):
        ins, outs = refs[:4 * n], refs[4 * n:]
        for t in range(n):
            gv = ins[n + t][...]
            outs[t][...] = gv
            outs[n + t][...], outs[2 * n + t][...], outs[3 * n + t][...] = _adamw_math(
                ins[t][...], gv, ins[2 * n + t][...], ins[3 * n + t][...])

    shapes = [jax.ShapeDtypeStruct(w.shape, F32) for w in ws]
    res = pl.pallas_call(body, out_shape=shapes * 4, name=name)(*ws, *gs, *ms, *vs)
    return res[:n], res[n:2 * n], res[2 * n:3 * n], res[3 * n:]


def pair_add(g, r1, c_arr, *, name):
    _, rows, cdim = g.shape
    h = rows // 2

    def body(c_ref, g_ref, r_ref, o_ref):
        o_ref[...] = (g_ref[...].astype(F32) + r_ref[...].astype(F32)).astype(o_ref.dtype)

    (out,), _ = _call(
        body, grid=(N_CHIPS,), prefetch=(c_arr,),
        in_specs=[pl.BlockSpec((None, h, cdim), lambda s, c: (s, c[0], 0)), pl.BlockSpec((None, h, cdim), lambda s, c: (s, 0, 0))],
        out_specs=[pl.BlockSpec((None, h, cdim), lambda s, c: (s, 0, 0))],
        out_shape=[jax.ShapeDtypeStruct((N_CHIPS, h, cdim), g.dtype)], operands=(g, r1), sem=("parallel",), name=name)
    return out


def final_add(g, r1, r2, jc_arr, *, dest_shape, lead, prev, name):
    _, rows, cdim = g.shape
    h = rows // 2

    def body(jc_ref, g_ref, r1_ref, r2_ref, *rest):
        o_ref = rest[-1]
        acc = g_ref[...].astype(F32) + r1_ref[...].astype(F32)
        for k in range(3):
            acc = acc + r2_ref[k].astype(F32)
        o_ref[...] = acc

    if lead is None:
        o_spec = pl.BlockSpec((h, cdim), lambda i, jc: (jc[1], 0))
    elif lead == "chip":
        o_spec = pl.BlockSpec((None, h, cdim), lambda i, jc: (jc[0], jc[1], 0))
    else:
        o_spec = pl.BlockSpec((None, h, cdim), lambda i, jc: (lead, jc[1], 0))
    in_specs = [pl.BlockSpec((None, h, cdim), lambda i, jc: (jc[0], jc[1], 0)),
                pl.BlockSpec((None, h, cdim), lambda i, jc: (jc[0], 0, 0)),
                pl.BlockSpec((3, h, cdim), lambda i, jc: (0, 0, 0))]
    operands = [g, r1, r2]
    aliases = None
    if prev is not None:
        in_specs.append(ANY)
        operands.append(prev)
        aliases = {3: 0}
    (out,), _ = _call(body, grid=(1,), prefetch=(jc_arr,), in_specs=in_specs, out_specs=[o_spec],
                      out_shape=[jax.ShapeDtypeStruct(dest_shape, F32)], operands=operands, aliases=aliases, name=name)
    return out


def _place():
    return lax.axis_index("x"), lax.axis_index("y"), lax.axis_index("c")


def _partner(x, y, k):
    return (1 - x if k >> 1 else x), (1 - y if k & 1 else y)


WHOLE = (0, 1, 1)


def _half(rows, sel, dtype, piece=WHOLE):
    lo, hi, n = piece
    align = 16 if dtype == BF16 else 8
    step = rows // 2 // n
    assert rows // 2 == step * n and step % align == 0
    return pl.ds(pl.multiple_of(sel * (rows // 2) + lo * step, align), (hi - lo) * step)


def _rider(peers, inputs, aliased, fresh, nsem, copies, arrivals):
    def start(ins, outs, send, recv):
        for cp in copies(ins, outs, send, recv):
            cp.start()

    def finish(ins, outs, send, recv):
        for cp in arrivals(ins, outs, send, recv):
            cp.wait_recv()
        for cp in copies(ins, outs, send, recv):
            cp.wait_send()

    return types.SimpleNamespace(peers=peers, inputs=list(inputs), aliased=list(aliased), fresh=list(fresh), nsem=nsem,
                                 start=start, finish=finish)


def _remote(src, dst, send, recv, idx, dev):
    return pltpu.make_async_remote_copy(src_ref=src, dst_ref=dst, send_sem=send.at[idx], recv_sem=recv.at[idx],
                                        device_id=dev, device_id_type=MESH)


def gather_ici_rider(fulls, pieces=None):
    nt = len(fulls)
    pieces = pieces or [WHOLE] * nt

    def region(outs, t, slot, sel):
        return outs[t].at[slot, _half(fulls[t].shape[1], sel, fulls[t].dtype, pieces[t])]

    def copies(ins, outs, send, recv):
        x, y, c = _place()
        res = []
        for t in range(nt):
            for k in (1, 2, 3):
                px, py = _partner(x, y, k)
                mine = region(outs, t, 2 * x + y, c)
                res.append(_remote(mine, mine, send, recv, 3 * t + k - 1, (px, py, c)))
        return res

    def arrivals(ins, outs, send, recv):
        x, y, c = _place()
        res = []
        for t in range(nt):
            for k in (1, 2, 3):
                px, py = _partner(x, y, k)
                theirs = region(outs, t, 2 * px + py, c)
                res.append(_remote(theirs, theirs, send, recv, 3 * t + k - 1, (x, y, c)))
        return res

    return _rider("chips", fulls, range(nt), [], 3 * nt, copies, arrivals)


def gather_d2d_rider(fulls, pieces=None):
    nt = len(fulls)
    pieces = pieces or [WHOLE] * nt

    def region(outs, t, slot, sel):
        return outs[t].at[slot, _half(fulls[t].shape[1], sel, fulls[t].dtype, pieces[t])]

    def both(outs, send, recv, mine):
        x, y, c = _place()
        res = []
        for t in range(nt):
            for k in (1, 2, 3):
                px, py = _partner(x, y, k)
                part = region(outs, t, 2 * px + py, c if mine else 1 - c)
                res.append(_remote(part, part, send, recv, 3 * t + k - 1, (x, y, 1 - c)))
        return res

    return _rider("sibling", fulls, range(nt), [], 3 * nt, lambda i, o, s, r: both(o, s, r, True),
                  lambda i, o, s, r: both(o, s, r, False))


def exchange_rider(grads):
    nt = len(grads)

    def both(ins, outs, send, recv):
        x, y, c = _place()
        return [_remote(ins[t].at[:, _half(grads[t].shape[1], 1 - c, grads[t].dtype)], outs[t], send, recv, t, (x, y, 1 - c))
                for t in range(nt)]

    fresh = [jax.ShapeDtypeStruct((N_CHIPS, g.shape[1] // 2, g.shape[2]), g.dtype) for g in grads]
    return _rider("sibling", grads, [], fresh, nt, both, both)


def scatter_rider(parts):
    nt = len(parts)

    def both(ins, outs, send, recv):
        x, y, c = _place()
        res = []
        for t in range(nt):
            for k in (1, 2, 3):
                px, py = _partner(x, y, k)
                res.append(_remote(ins[t].at[2 * px + py], outs[t].at[k - 1], send, recv, 3 * t + k - 1, (px, py, c)))
        return res

    fresh = [jax.ShapeDtypeStruct((3,) + p.shape[1:], p.dtype) for p in parts]
    return _rider("chips", parts, [], fresh, 3 * nt, both, both)


def broadcast_rider(bufs, items):
    def region(outs, item, sel):
        bi, lead = item
        ref = outs[bi]
        if lead == "chip":
            x, y, _ = _place()
            ref = ref.at[2 * x + y]
        elif lead is not None:
            ref = ref.at[lead]
        return ref.at[_half(ref.shape[0], sel, F32)]

    def both(outs, send, recv, mine):
        x, y, c = _place()
        res = []
        for i, item in enumerate(items):
            part = region(outs, item, c if mine else 1 - c)
            res.append(_remote(part, part, send, recv, i, (x, y, 1 - c)))
        return res

    return _rider("sibling", bufs, range(len(bufs)), [], len(items), lambda i, o, s, r: both(o, s, r, True),
                  lambda i, o, s, r: both(o, s, r, False))


def allcast_rider(buf):
    peers = [(k, flip) for k in range(N_CHIPS) for flip in (0, 1) if (k, flip) != (0, 0)]

    def both(outs, send, recv, mine):
        x, y, c = _place()
        res = []
        for i, (k, flip) in enumerate(peers):
            px, py = _partner(x, y, k)
            pc = 1 - c if flip else c
            slot, sel = (2 * x + y, c) if mine else (2 * px + py, pc)
            part = outs[0].at[slot, _half(buf.shape[1], sel, F32)]
            res.append(_remote(part, part, send, recv, i, (px, py, pc)))
        return res

    return _rider("everyone", [buf], [0], [], len(peers), lambda i, o, s, r: both(o, s, r, True),
                  lambda i, o, s, r: both(o, s, r, False))


def comm_call(riders, *, name):
    _, res = _call(None, riders=riders, name=name)
    return res


SLAB_ROWS = 192


def _pad_rows(a, rows=8):
    return jnp.pad(a, ((0, rows - a.shape[0]), (0, 0)))


def _pack_small(norm_grads, db_qkv, db_o, dsinks, db_sp, dln_g, dln_b, dw_sp, loss_part):
    parts = [
        jnp.concatenate(norm_grads, axis=0),
        _pad_rows(jnp.pad(db_qkv, ((0, 0), (0, 2 * D_MODEL - QKV_WIDTH))).reshape(2, D_MODEL)),
        _pad_rows(db_o),
        _pad_rows(jnp.pad(dsinks.reshape(1, N_Q_HEADS), ((0, 0), (0, D_MODEL - N_Q_HEADS)))),
        _pad_rows(db_sp.reshape(1, D_MODEL)),
        _pad_rows(jnp.concatenate([dln_g, dln_b, jnp.pad(loss_part[0:1], ((0, 0), (0, D_MODEL - LANES)))], axis=0)),
        dw_sp.reshape(SGU_CHUNK, D_MODEL),
    ]
    slab = jnp.concatenate(parts, axis=0)
    return jnp.pad(slab, ((0, SLAB_ROWS - slab.shape[0]), (0, 0))).reshape(N_CHIPS, SLAB_ROWS // N_CHIPS, D_MODEL)


def _unpack_small(slab, j):
    slab = slab.reshape(SLAB_ROWS, D_MODEL)
    norms = [slab[2 * i:2 * i + 2] for i in range(4)]
    db_qkv = slab[8:10].reshape(1, 2 * D_MODEL)[:, :QKV_WIDTH]
    db_o = slab[16:17]
    dsinks = slab[24:25, :N_Q_HEADS]
    db_sp = slab[32:33].reshape(SGU_GROUPS, SGU_CHUNK)
    width = D_MODEL // N_CHIPS
    dln_g = lax.dynamic_slice(slab[40:41], (0, j * width), (1, width))
    dln_b = lax.dynamic_slice(slab[41:42], (0, j * width), (1, width))
    dw_sp = slab[48:48 + SGU_CHUNK].reshape(SGU_GROUPS * SGU_CHUNK, SGU_CHUNK)
    return norms, db_qkv, db_o, dsinks, db_sp, dln_g, dln_b, dw_sp, slab[42, 0]


class _GradReduce:
    def __init__(self, c_arr, jc_arr, dest_shapes):
        self.c_arr, self.jc_arr, self.dest_shapes = c_arr, jc_arr, dest_shapes
        self.grad, self.sibling, self.pair, self.chips, self.dest = {}, {}, {}, {}, {}

    def exchange(self, tags):
        return exchange_rider([self.grad[t] for t in tags])

    def exchanged(self, tags, res):
        for t, r in zip(tags, res):
            self.sibling[t] = r
            self.pair[t] = pair_add(self.grad[t], r, self.c_arr, name=f"pair_add_{t}")

    def scatter(self, tags):
        return scatter_rider([self.pair[t] for t in tags])

    def scattered(self, tags, res, where):
        for t, r in zip(tags, res):
            name, lead = where[t]
            self.dest[name] = final_add(self.grad[t], self.sibling[t], r, self.jc_arr, dest_shape=self.dest_shapes[name],
                                        lead=lead, prev=self.dest.get(name), name=f"final_add_{t}")

    def broadcast(self, items):
        names = []
        for n, _ in items:
            if n not in names:
                names.append(n)
        return names, broadcast_rider([self.dest[n] for n in names], [(names.index(n), lead) for n, lead in items])

    def broadcasted(self, names, res):
        for n, r in zip(names, res):
            self.dest[n] = r


def kernel(x, norm_mix_pre, norm_mix_post, norm_ffn_pre, norm_ffn_post, attn_w_qkv, attn_b_qkv, attn_sinks, attn_w_o, attn_b_o, sgu_w_in, sgu_ln_g, sgu_ln_b, sgu_w_spatial, sgu_b_spatial, sgu_w_out, ffn_w_gate_up, ffn_w_down, loss_target, m_norm_mix_pre, m_norm_mix_post, m_norm_ffn_pre, m_norm_ffn_post, m_attn_w_qkv, m_attn_b_qkv, m_attn_sinks, m_attn_w_o, m_attn_b_o, m_sgu_w_in, m_sgu_ln_g, m_sgu_ln_b, m_sgu_w_spatial, m_sgu_b_spatial, m_sgu_w_out, m_ffn_w_gate_up, m_ffn_w_down, v_norm_mix_pre, v_norm_mix_post, v_norm_ffn_pre, v_norm_ffn_post, v_attn_w_qkv, v_attn_b_qkv, v_attn_sinks, v_attn_w_o, v_attn_b_o, v_sgu_w_in, v_sgu_ln_g, v_sgu_ln_b, v_sgu_w_spatial, v_sgu_b_spatial, v_sgu_w_out, v_ffn_w_gate_up, v_ffn_w_down):
    s = x.shape[1]
    x0 = x.reshape(s, D_MODEL)
    target = loss_target.reshape(s, D_MODEL)
    mx, my, mc = lax.axis_index("x"), lax.axis_index("y"), lax.axis_index("c")
    chip = 2 * mx + my
    chip_arr = jnp.reshape(chip, (1,)).astype(I32)
    c_arr = jnp.reshape(mc, (1,)).astype(I32)
    jc_arr = jnp.stack([chip, mc]).astype(I32)
    zero_bias = jnp.zeros((1, D_MODEL), F32)

    def gain(p, i):
        return p[i:i + 1]

    big = [attn_w_qkv, attn_w_o, sgu_w_in, sgu_w_out, ffn_w_gate_up, ffn_w_gate_up, ffn_w_down, ffn_w_down]
    layers = [0, 0, 0, 0, 0, 1, 0, 1]
    tags = ["qkv", "wo", "win", "wout", "wgu0", "wgu1", "wd0", "wd1"]
    full = {t: place_shard(w, l, chip_arr, BF16, name=f"place_{t}") for w, l, t in zip(big, layers, tags) if t != "wgu1"}
    ln_pack = _pad_rows(jnp.concatenate([sgu_ln_g, sgu_ln_b], axis=0), 16)[None]
    full["ln"] = place_shard(ln_pack, 0, chip_arr, F32, name="place_ln")

    def split(items):
        return [i if isinstance(i, str) else i[0] for i in items], [WHOLE if isinstance(i, str) else tuple(i[1:]) for i in items]

    def ici(*items):
        names, pieces = split(items)
        return gather_ici_rider([full[n] for n in names], pieces)

    def d2d(*items):
        names, pieces = split(items)
        return gather_d2d_rider([full[n] for n in names], pieces)

    def landed(items, res):
        for n, r in zip(split(items)[0], res):
            full[n] = r

    cos, sin = _rope_tables(s)
    sink_rows = jnp.broadcast_to(
        jnp.repeat(attn_sinks.reshape(N_KV_HEADS, GQA_GROUP), WINDOW, axis=1)[:, None, :], (N_KV_HEADS, 8, ROWS))
    w_sp = sgu_w_spatial.reshape(SGU_GROUPS, SGU_CHUNK, SGU_CHUNK)
    b_sp = jnp.broadcast_to(sgu_b_spatial.reshape(SGU_GROUPS, SGU_CHUNK)[:, :, None], (SGU_GROUPS, SGU_CHUNK, LANES))

    h0, (res,) = prenorm(x0, gain(norm_mix_pre, 0), name="prenorm_0", riders=[ici("qkv", "ln")])
    landed(("qkv", "ln"), res)
    full["wgu1"], (res,) = place_shard(ffn_w_gate_up, 1, chip_arr, BF16, name="place_wgu1", riders=[d2d("qkv", "ln")])
    landed(("qkv", "ln"), res)
    ln_g = full["ln"][:, 0, :].reshape(1, D_MODEL)
    ln_b = full["ln"][:, 1, :].reshape(1, D_MODEL)

    def hosted(call, stages):
        outputs, results = call([{"ici": ici, "d2d": d2d}[kind](*items) for kind, items in stages])
        for (_, items), res in zip(stages, results):
            landed(items, res)
        return outputs

    qkv = hosted(lambda r: qkv_proj(h0, full["qkv"], attn_b_qkv, cos, sin, name="qkv_proj", riders=r),
                 [("ici", ("wo", ("wgu0", 0, 4, 8)))])
    o = hosted(lambda r: attn_fwd(qkv, sink_rows, name="attn_fwd", riders=r),
               [("d2d", ("wo",)), ("ici", (("wgu0", 4, 8, 8), ("wd0", 0, 1, 2)))])
    w_o = full["wo"].reshape(Q_WIDTH, D_MODEL)
    x1, h1, m0 = hosted(lambda r: proj_residual_norm(o, w_o, x0, attn_b_o, gain(norm_mix_post, 0), gain(norm_ffn_pre, 0),
                                                     name="attn_out_norm", riders=r),
                        [("d2d", ("wgu0",)), ("ici", (("wd0", 1, 2, 2), ("win", 0, 2, 8)))])
    gu0, a0 = hosted(lambda r: ffn_up(h1, full["wgu0"], name="ffn_up_0", riders=r),
                     [("d2d", ("wd0",)), ("ici", (("win", 2, 8, 8), "wout", ("wgu1", 0, 4, 8)))])
    w_d0 = full["wd0"].reshape(D_FF, D_MODEL)
    x2, h2, f0 = hosted(lambda r: proj_residual_norm(a0, w_d0, x1, zero_bias, gain(norm_ffn_post, 0), gain(norm_mix_pre, 1),
                                                     name="ffn_down_norm_0", riders=r),
                        [("d2d", ("win", "wout")), ("ici", (("wgu1", 4, 7, 8),))])
    w_in = full["win"]
    z, y = hosted(lambda r: sgu_in_fwd(h2, w_in, ln_g, ln_b, w_sp, b_sp, name="sgu_in_fwd", riders=r),
                  [("ici", (("wgu1", 7, 8, 8), ("wd1", 0, 1, 2)))])
    w_out = full["wout"].reshape(D_MODEL, D_MODEL)
    x3, h3, m1 = hosted(lambda r: proj_residual_norm(y, w_out, x2, zero_bias, gain(norm_mix_post, 1), gain(norm_ffn_pre, 1),
                                                     name="sgu_out_norm", riders=r),
                        [("d2d", ("wgu1",)), ("ici", (("wd1", 1, 2, 2),))])
    w_qkv, w_gu0, w_gu1 = full["qkv"], full["wgu0"], full["wgu1"]
    gu1, a1 = hosted(lambda r: ffn_up(h3, w_gu1, name="ffn_up_1", riders=r), [("d2d", ("wd1",))])
    w_d1 = full["wd1"].reshape(D_FF, D_MODEL)
    dx4, df1, dg_fpost1, loss_part = proj_loss_head(a1, w_d1, x3, gain(norm_ffn_post, 1), target, name="ffn_down_loss")

    red = _GradReduce(c_arr, jc_arr, {
        "qkv": attn_w_qkv.shape[1:], "wo": attn_w_o.shape[1:], "win": sgu_w_in.shape[1:], "wout": sgu_w_out.shape[1:],
        "wgu": ffn_w_gate_up.shape, "wd": ffn_w_down.shape, "slab": (N_CHIPS, SLAB_ROWS // N_CHIPS, D_MODEL)})
    where = {"qkv": ("qkv", None), "wo": ("wo", None), "win": ("win", None), "wout": ("wout", None), "wgu0": ("wgu", 0),
             "wgu1": ("wgu", 1), "wd0": ("wd", 0), "wd1": ("wd", 1), "small": ("slab", "chip")}

    dgu1, dx3, dm1, dg_fpre1, dg_mpost1, _ = ffn_bwd_rows(
        df1, w_d1, gu1, w_gu1, dx4, x3, gain(norm_ffn_pre, 1), m1, gain(norm_mix_post, 1), name="ffn_bwd_rows_1")
    red.grad["wd1"] = mm_tn(a1, df1, shard_major=False, tm=256, tn=D_MODEL, name="dw_down_1").reshape(
        N_CHIPS, D_FF // N_CHIPS, D_MODEL)
    red.grad["wgu1"], (res,) = mm_tn(h3, dgu1, shard_major=True, tm=512, tn=FF_HALF, name="dw_gate_up_1",
                                     riders=[red.exchange(["wd1"])])
    red.exchanged(["wd1"], res)
    dy, (res,) = mm_nt(dm1, w_out, out_dtype=F32, name="dy_sgu", riders=[red.exchange(["wgu1"])])
    red.exchanged(["wgu1"], res)
    red.grad["wout"] = mm_tn(y, dm1, shard_major=False, tm=512, tn=D_MODEL, name="dw_sgu_out").reshape(
        N_CHIPS, D_MODEL // N_CHIPS, D_MODEL)
    (dz, dw_sp, db_sp, dln_g, dln_b), (res_a, res_b) = sgu_bwd(
        z, dy, ln_g, ln_b, w_sp, b_sp, name="sgu_bwd", riders=[red.scatter(["wgu1"]), red.exchange(["wout"])])
    red.scattered(["wgu1"], res_a, where)
    red.exchanged(["wout"], res_b)
    names, rider = red.broadcast([("wgu", 1)])
    red.grad["win"], (res_a, res_b) = mm_tn(h2, dz, shard_major=True, tm=D_MODEL, tn=2 * D_MODEL // N_CHIPS, name="dw_sgu_in",
                                            riders=[rider, red.scatter(["wout"])])
    red.broadcasted(names, res_a)
    red.scattered(["wout"], res_b, where)
    names, rider = red.broadcast([("wout", None)])
    (dx2, df0, dg_mpre1, dg_fpost0, _), (res_a, res_b) = dh_norm_bwd_pair(
        dz, w_in, dx3, x2, gain(norm_mix_pre, 1), f0, gain(norm_ffn_post, 0), name="dh_sgu_norm",
        riders=[red.exchange(["win"]), rider])
    red.exchanged(["win"], res_a)
    red.broadcasted(names, res_b)
    (dgu0, dx1, dm0, dg_fpre0, dg_mpost0, db_o), (res,) = ffn_bwd_rows(
        df0, w_d0, gu0, w_gu0, dx2, x1, gain(norm_ffn_pre, 0), m0, gain(norm_mix_post, 0), name="ffn_bwd_rows_0",
        riders=[red.scatter(["wd1", "win"])])
    red.scattered(["wd1", "win"], res, where)
    names, rider = red.broadcast([("wd", 1), ("win", None)])
    dw_d0, (res,) = mm_tn(a0, df0, shard_major=False, tm=256, tn=D_MODEL, name="dw_down_0", riders=[rider])
    red.broadcasted(names, res)
    red.grad["wd0"] = dw_d0.reshape(N_CHIPS, D_FF // N_CHIPS, D_MODEL)
    red.grad["wgu0"], (res,) = mm_tn(h1, dgu0, shard_major=True, tm=512, tn=FF_HALF, name="dw_gate_up_0",
                                     riders=[red.exchange(["wd0"])])
    red.exchanged(["wd0"], res)
    do, (res,) = mm_nt(dm0, w_o, out_dtype=BF16, name="do_attn", riders=[red.exchange(["wgu0"])])
    red.exchanged(["wgu0"], res)
    red.grad["wo"] = mm_tn(o, dm0, shard_major=False, tm=512, tn=D_MODEL, name="dw_attn_out").reshape(
        N_CHIPS, Q_WIDTH // N_CHIPS, D_MODEL)
    (dq, dkc, dkp, dvc, dvp, dsink), (res_a, res_b) = attn_bwd(
        qkv, sink_rows, do, name="attn_bwd", riders=[red.scatter(["wgu0", "wd0"]), red.exchange(["wo"])])
    red.scattered(["wgu0", "wd0"], res_a, where)
    red.exchanged(["wo"], res_b)
    names, rider = red.broadcast([("wgu", 0), ("wd", 0)])
    (dqkv, db_qkv), (res_a, res_b) = rope_bwd(dq, dkc, dkp, dvc, dvp, cos, sin, name="rope_bwd",
                                              riders=[rider, red.scatter(["wo"])])
    red.broadcasted(names, res_a)
    red.scattered(["wo"], res_b, where)
    names, rider = red.broadcast([("wo", None)])
    red.grad["qkv"], (res,) = mm_tn(h0, dqkv, shard_major=True, tm=D_MODEL, tn=QKV_WIDTH // N_CHIPS, name="dw_qkv",
                                    riders=[rider])
    red.broadcasted(names, res)
    dh0, (res,) = mm_nt(dqkv, w_qkv, out_dtype=F32, tm=1024, name="dh_attn", riders=[red.exchange(["qkv"])])
    red.exchanged(["qkv"], res)
    grad_x, dg_mpre0 = norm_bwd_last(dx1, dh0, x0, gain(norm_mix_pre, 0), name="norm_bwd_in")

    norm_grads = [jnp.concatenate(p, axis=0) for p in
                  ((dg_mpre0, dg_mpre1), (dg_mpost0, dg_mpost1), (dg_fpre0, dg_fpre1), (dg_fpost0, dg_fpost1))]
    red.grad["small"] = _pack_small(norm_grads, db_qkv, db_o, dsink[:, :, 0, 0], db_sp[:, :, 0], dln_g, dln_b, dw_sp,
                                    loss_part)
    res_a, res_b = comm_call([red.scatter(["qkv"]), red.exchange(["small"])], name="tail_1")
    red.scattered(["qkv"], res_a, where)
    red.exchanged(["small"], res_b)
    names, rider = red.broadcast([("qkv", None)])
    res_a, res_b = comm_call([red.scatter(["small"]), rider], name="tail_2")
    red.scattered(["small"], res_a, where)
    red.broadcasted(names, res_b)
    ((slab_full,),) = comm_call([allcast_rider(red.dest["slab"])], name="tail_3")
    g_qkv, g_wo, g_win, g_wout, g_wgu, g_wd = (red.dest[n] for n in ("qkv", "wo", "win", "wout", "wgu", "wd"))
    g_norms, g_bqkv, g_bo, g_sinks, g_bsp, g_lng, g_lnb, g_wsp, loss = _unpack_small(slab_full, chip)

    def big_update(w, g, m, v, tag):
        return adamw(w, g.reshape(w.shape), m, v, name=f"adamw_{tag}")

    upd = {
        "attn_w_qkv": big_update(attn_w_qkv, g_qkv, m_attn_w_qkv, v_attn_w_qkv, "qkv"),
        "attn_w_o": big_update(attn_w_o, g_wo, m_attn_w_o, v_attn_w_o, "wo"),
        "sgu_w_in": big_update(sgu_w_in, g_win, m_sgu_w_in, v_sgu_w_in, "win"),
        "sgu_w_out": big_update(sgu_w_out, g_wout, m_sgu_w_out, v_sgu_w_out, "wout"),
        "ffn_w_gate_up": big_update(ffn_w_gate_up, g_wgu, m_ffn_w_gate_up, v_ffn_w_gate_up, "wgu"),
        "ffn_w_down": big_update(ffn_w_down, g_wd, m_ffn_w_down, v_ffn_w_down, "wd"),
    }
    small_names = ["norm_mix_pre", "norm_mix_post", "norm_ffn_pre", "norm_ffn_post", "attn_b_qkv", "attn_sinks", "attn_b_o",
                   "sgu_ln_g", "sgu_ln_b", "sgu_w_spatial", "sgu_b_spatial"]
    small_w = [norm_mix_pre, norm_mix_post, norm_ffn_pre, norm_ffn_post, attn_b_qkv, attn_sinks, attn_b_o, sgu_ln_g, sgu_ln_b,
               sgu_w_spatial, sgu_b_spatial]
    small_m = [m_norm_mix_pre, m_norm_mix_post, m_norm_ffn_pre, m_norm_ffn_post, m_attn_b_qkv, m_attn_sinks, m_attn_b_o,
               m_sgu_ln_g, m_sgu_ln_b, m_sgu_w_spatial, m_sgu_b_spatial]
    small_v = [v_norm_mix_pre, v_norm_mix_post, v_norm_ffn_pre, v_norm_ffn_post, v_attn_b_qkv, v_attn_sinks, v_attn_b_o,
               v_sgu_ln_g, v_sgu_ln_b, v_sgu_w_spatial, v_sgu_b_spatial]
    small_g = g_norms + [g_bqkv, g_sinks, g_bo, g_lng, g_lnb, g_wsp, g_bsp]

    def flat2(a):
        return a.reshape(-1, a.shape[-1])

    res = adamw_small([flat2(a) for a in small_w], [flat2(a) for a in small_g], [flat2(a) for a in small_m],
                      [flat2(a) for a in small_v], name="adamw_small")
    for i, nm in enumerate(small_names):
        upd[nm] = tuple(r[i].reshape(small_w[i].shape) for r in res)

    order = ["norm_mix_pre", "norm_mix_post", "norm_ffn_pre", "norm_ffn_post", "attn_w_qkv", "attn_b_qkv", "attn_sinks",
             "attn_w_o", "attn_b_o", "sgu_w_in", "sgu_ln_g", "sgu_ln_b", "sgu_w_spatial", "sgu_b_spatial", "sgu_w_out",
             "ffn_w_gate_up", "ffn_w_down"]
    outs = [loss, grad_x.reshape(1, s, D_MODEL)]
    for part in range(4):
        outs += [upd[nm][part] for nm in order]
    return tuple(outs)
```

```python
import types

import numpy as np
import jax
import jax.numpy as jnp
from jax import lax
from jax.experimental import pallas as pl
from jax.experimental.pallas import tpu as pltpu

F32 = jnp.float32
BF16 = jnp.bfloat16
I32 = jnp.int32

D_MODEL = 1024
HEAD_DIM = 64
N_Q_HEADS = 16
N_KV_HEADS = 4
GQA_GROUP = 4
WINDOW = 128
Q_WIDTH = 1024
KV_WIDTH = 256
QKV_WIDTH = 1536
ROPE_THETA = 10000.0
SGU_GROUPS = 8
SGU_CHUNK = 128
D_FF = 2816
FF_HALF = D_FF // 2
EPS = 1e-6
N_CHIPS = 4
LANES = 128

ADAM_LR = 0.001
ADAM_B1 = 0.9
ADAM_B2 = 0.999
ADAM_EPS = 1e-08
ADAM_WD = 0.01
ADAM_STEP = 10

VMEM_LIMIT = 52 * 1024 * 1024
MESH = pl.DeviceIdType.MESH
NEG = -1e30
NT_DIMS = (((1,), (1,)), ((), ()))
TN_DIMS = (((0,), (0,)), ((), ()))
NN_DIMS = (((1,), (0,)), ((), ()))
ANY = pl.BlockSpec(memory_space=pl.ANY)


def _row_tile(s, want):
    return want if s % want == 0 else s


PEER_KINDS = ("sibling", "chips", "sibling+chips", "everyone")


def _peer_kind(riders):
    kinds = {r.peers for r in riders}
    if not kinds:
        return None
    if "everyone" in kinds:
        return "everyone"
    return "sibling+chips" if len(kinds) == 2 else kinds.pop()


def _peer_barrier(kind):
    x, y, c = _place()
    chips = [(*_partner(x, y, k), c) for k in (1, 2, 3)]
    peers = {"sibling": [(x, y, 1 - c)], "chips": chips, "sibling+chips": [(x, y, 1 - c)] + chips,
             "everyone": [(x, y, 1 - c)] + chips + [(px, py, 1 - c) for px, py, _ in chips]}[kind]
    barrier = pltpu.get_barrier_semaphore()
    for dev in peers:
        pl.semaphore_signal(barrier, inc=1, device_id=dev, device_id_type=MESH)
    pl.semaphore_wait(barrier, len(peers))


def _call(body, *, name, grid=(), in_specs=(), out_specs=(), out_shape=(), scratch_shapes=(), operands=(), prefetch=(),
          aliases=None, riders=(), sem=None):
    n_pre, n_in, n_out, n_scr = len(prefetch), len(operands), len(out_shape), len(scratch_shapes)
    in_specs, out_specs, out_shape = list(in_specs), list(out_specs), list(out_shape)
    operands, scratch_shapes = list(operands), list(scratch_shapes)
    io_alias = {n_pre + i: o for i, o in (aliases or {}).items()}
    for r in riders:
        base_in, base_out = n_pre + len(operands), len(out_shape)
        operands += list(r.inputs)
        in_specs += [ANY] * len(r.inputs)
        for pos, i in enumerate(r.aliased):
            io_alias[base_in + i] = base_out + pos
            out_shape.append(jax.ShapeDtypeStruct(r.inputs[i].shape, r.inputs[i].dtype))
        out_shape += list(r.fresh)
        out_specs += [ANY] * (len(r.aliased) + len(r.fresh))
        scratch_shapes += [pltpu.SemaphoreType.DMA((r.nsem,)), pltpu.SemaphoreType.DMA((r.nsem,))]

    def wrapped(*refs):
        pre, p = refs[:n_pre], n_pre
        core_in, p = refs[p:p + n_in], p + n_in
        r_in = []
        for r in riders:
            r_in.append(refs[p:p + len(r.inputs)])
            p += len(r.inputs)
        core_out, p = refs[p:p + n_out], p + n_out
        r_out = []
        for r in riders:
            k = len(r.aliased) + len(r.fresh)
            r_out.append(refs[p:p + k])
            p += k
        core_scr, p = refs[p:p + n_scr], p + n_scr
        r_sem = [refs[p + 2 * i:p + 2 * i + 2] for i in range(len(riders))]

        def edge(at_last, fns):
            def run():
                if not at_last:
                    _peer_barrier(peer_kind)
                for i, r in enumerate(riders):
                    getattr(r, fns)(r_in[i], r_out[i], r_sem[i][0], r_sem[i][1])
            if not riders:
                return
            if not grid:
                run()
                return
            cond = None
            for d, n in enumerate(grid):
                c = pl.program_id(d) == (n - 1 if at_last else 0)
                cond = c if cond is None else jnp.logical_and(cond, c)
            pl.when(cond)(run)

        edge(False, "start")
        if body is not None:
            body(*pre, *core_in, *core_out, *core_scr)
        edge(True, "finish")

    if sem is None or riders:
        sem = ("arbitrary",) * len(grid)
    kwargs = dict(out_shape=out_shape, input_output_aliases=io_alias, name=name)
    peer_kind = _peer_kind(riders)
    collective = {} if peer_kind is None else {"collective_id": PEER_KINDS.index(peer_kind)}
    if grid:
        kwargs["compiler_params"] = pltpu.CompilerParams(dimension_semantics=sem, vmem_limit_bytes=VMEM_LIMIT, **collective)
    elif collective:
        kwargs["compiler_params"] = pltpu.CompilerParams(**collective)
    if n_pre:
        kwargs["grid_spec"] = pltpu.PrefetchScalarGridSpec(
            num_scalar_prefetch=n_pre, grid=grid, in_specs=in_specs, out_specs=out_specs, scratch_shapes=scratch_shapes)
    else:
        kwargs.update(grid=grid, in_specs=in_specs, out_specs=out_specs, scratch_shapes=scratch_shapes)
    res = pl.pallas_call(wrapped, **kwargs)(*prefetch, *operands)
    core, rest, rider_res = list(res[:n_out]), list(res[n_out:]), []
    for r in riders:
        k = len(r.aliased) + len(r.fresh)
        rider_res.append(rest[:k])
        rest = rest[k:]
    return core, rider_res


def _mm_call(*, grid, in_specs, out_spec, out_shape, dims, nk, kaxis, acc_shape, name, operands, riders=()):
    out_dtype = out_shape.dtype

    def body(a_ref, b_ref, o_ref, *scratch):
        p = lax.dot_general(a_ref[...].astype(BF16), b_ref[...].astype(BF16), dims, preferred_element_type=F32)
        if nk == 1:
            o_ref[...] = p.astype(out_dtype)
        else:
            acc = scratch[0]
            kk = pl.program_id(kaxis)

            @pl.when(kk == 0)
            def _():
                acc[...] = p

            @pl.when(kk > 0)
            def _():
                acc[...] += p

            @pl.when(kk == nk - 1)
            def _():
                o_ref[...] = acc[...].astype(out_dtype)

    sem = ["parallel"] * len(grid)
    if nk > 1:
        sem[kaxis] = "arbitrary"
    (out,), rider_res = _call(
        body, grid=grid, in_specs=in_specs, out_specs=[out_spec], out_shape=[out_shape],
        scratch_shapes=[pltpu.VMEM(acc_shape, F32)] if nk > 1 else [], operands=operands, name=name, riders=riders,
        sem=tuple(sem))
    return (out, rider_res) if riders else out


def mm_nn(a, w, *, out_dtype, name, tm=512, tn=512, riders=()):
    m, k = a.shape
    tm = _row_tile(m, tm)
    if w.ndim == 3:
        ns = w.shape[2]
        grid = (N_CHIPS, m // tm)
        w_spec = pl.BlockSpec((None, k, ns), lambda j, i: (j, 0, 0))
        o_spec = pl.BlockSpec((tm, ns), lambda j, i: (i, j))
        n = N_CHIPS * ns
    else:
        n = w.shape[1]
        grid = (n // tn, m // tm)
        w_spec = pl.BlockSpec((k, tn), lambda j, i: (0, j))
        o_spec = pl.BlockSpec((tm, tn), lambda j, i: (i, j))
    return _mm_call(grid=grid, in_specs=[pl.BlockSpec((tm, k), lambda j, i: (i, 0)), w_spec], out_spec=o_spec,
                    out_shape=jax.ShapeDtypeStruct((m, n), out_dtype), dims=NN_DIMS, nk=1, kaxis=0, acc_shape=None,
                    name=name, operands=(a, w), riders=riders)


def mm_nt(a, w, *, out_dtype, name, tm=512, tn=512, riders=()):
    if w.ndim == 2:
        m, n = a.shape
        kout = w.shape[0]
        tm = _row_tile(m, tm)
        return _mm_call(grid=(kout // tn, m // tm),
                        in_specs=[pl.BlockSpec((tm, n), lambda j, i: (i, 0)), pl.BlockSpec((tn, n), lambda j, i: (j, 0))],
                        out_spec=pl.BlockSpec((tm, tn), lambda j, i: (i, j)),
                        out_shape=jax.ShapeDtypeStruct((m, kout), out_dtype), dims=NT_DIMS, nk=1, kaxis=0,
                        acc_shape=None, name=name, operands=(a, w), riders=riders)
    _, kout, ns = w.shape
    planes = a.ndim == 3
    m = a.shape[1] if planes else a.shape[0]
    tm = _row_tile(m, tm)
    a_spec = pl.BlockSpec((2, tm, 2 * ns), lambda i: (0, i, 0)) if planes else pl.BlockSpec((tm, N_CHIPS * ns), lambda i: (i, 0))

    def body(a_ref, w0, w1, w2, w3, o_ref):
        acc = None
        for j, w_ref in enumerate((w0, w1, w2, w3)):
            if planes:
                a_j = a_ref[j // 2, :, (j % 2) * ns:(j % 2 + 1) * ns]
            else:
                a_j = a_ref[:, j * ns:(j + 1) * ns]
            p = lax.dot_general(a_j, w_ref[...], NT_DIMS, preferred_element_type=F32)
            acc = p if acc is None else acc + p
        o_ref[...] = acc.astype(out_dtype)

    def shard(j):
        return pl.BlockSpec((None, kout, ns), lambda i: (j, 0, 0))

    (out,), rider_res = _call(
        body, grid=(m // tm,), in_specs=[a_spec] + [shard(j) for j in range(N_CHIPS)],
        out_specs=[pl.BlockSpec((tm, kout), lambda i: (i, 0))], out_shape=[jax.ShapeDtypeStruct((m, kout), out_dtype)],
        operands=(a, w, w, w, w), sem=("parallel",), name=name, riders=riders)
    return (out, rider_res) if riders else out


def mm_tn(a, b, *, shard_major, name, tm, tn, tk=None, out_dtype=BF16, riders=()):
    s, m = a.shape
    tk = s if tk is None else _row_tile(s, tk)
    if b.ndim == 3:
        n = 2 * b.shape[2]
        b_spec = pl.BlockSpec((None, tk, tn), lambda j, i, kk: (j // 2, kk, j % 2))
    else:
        n = b.shape[1]
        b_spec = pl.BlockSpec((tk, tn), lambda j, i, kk: (kk, j))
    if shard_major:
        assert tn == n // N_CHIPS
        o_spec = pl.BlockSpec((None, tm, tn), lambda j, i, kk: (j, i, 0))
        o_shape = jax.ShapeDtypeStruct((N_CHIPS, m, tn), out_dtype)
    else:
        o_spec = pl.BlockSpec((tm, tn), lambda j, i, kk: (i, j))
        o_shape = jax.ShapeDtypeStruct((m, n), out_dtype)
    return _mm_call(grid=(n // tn, m // tm, s // tk),
                    in_specs=[pl.BlockSpec((tk, tm), lambda j, i, kk: (kk, i)), b_spec], out_spec=o_spec,
                    out_shape=o_shape, dims=TN_DIMS, nk=s // tk, kaxis=2, acc_shape=(tm, tn), name=name, operands=(a, b),
                    riders=riders)


def _rstd(x):
    return lax.rsqrt(jnp.mean(x * x, axis=-1, keepdims=True) + EPS)


def _rms_bwd(dy, x, g):
    r = _rstd(x)
    xhat = x * r
    gy = dy * g
    dx = r * (gy - xhat * jnp.mean(gy * xhat, axis=-1, keepdims=True))
    return dx, jnp.sum(dy * xhat, axis=0, keepdims=True)


def _accum(ref, val, first):
    @pl.when(first)
    def _():
        ref[...] = val

    @pl.when(jnp.logical_not(first))
    def _():
        ref[...] += val


def _row_spec(tm, width):
    return pl.BlockSpec((tm, width), lambda i: (i, 0))


def _vec_spec(width):
    return pl.BlockSpec((1, width), lambda i: (0, 0))


def _ret(core, rider_res, riders):
    core = core[0] if len(core) == 1 else core
    return (core, rider_res) if riders else core


def prenorm(x, g, *, name, tm=256, riders=()):
    s = x.shape[0]
    tm = _row_tile(s, tm)

    def body(x_ref, g_ref, h_ref):
        xv = x_ref[...]
        h_ref[...] = (xv * _rstd(xv) * g_ref[...]).astype(BF16)

    core, rr = _call(
        body, grid=(s // tm,), in_specs=[_row_spec(tm, D_MODEL), _vec_spec(D_MODEL)], out_specs=[_row_spec(tm, D_MODEL)],
        out_shape=[jax.ShapeDtypeStruct((s, D_MODEL), BF16)], operands=(x, g), sem=("parallel",), name=name, riders=riders)
    return _ret(core, rr, riders)


def proj_residual_norm(a, w, x, bias, g_post, g_next, *, name, tm=256, riders=()):
    s, k = a.shape
    tm = _row_tile(s, tm)

    def body(a_ref, w_ref, x_ref, b_ref, gp_ref, gn_ref, xo_ref, h_ref, m_ref):
        mv = jnp.dot(a_ref[...], w_ref[...], preferred_element_type=F32) + b_ref[...]
        m_ref[...] = mv.astype(BF16)
        xn = x_ref[...] + mv * _rstd(mv) * gp_ref[...]
        xo_ref[...] = xn
        h_ref[...] = (xn * _rstd(xn) * gn_ref[...]).astype(BF16)

    row, vec = _row_spec(tm, D_MODEL), _vec_spec(D_MODEL)
    core, rr = _call(
        body, grid=(s // tm,),
        in_specs=[_row_spec(tm, k), pl.BlockSpec((k, D_MODEL), lambda i: (0, 0)), row, vec, vec, vec], out_specs=[row, row, row],
        out_shape=[jax.ShapeDtypeStruct((s, D_MODEL), F32), jax.ShapeDtypeStruct((s, D_MODEL), BF16),
                   jax.ShapeDtypeStruct((s, D_MODEL), BF16)],
        operands=(a, w, x, bias, g_post, g_next), sem=("parallel",), name=name, riders=riders)
    return _ret(core, rr, riders)


def proj_loss_head(a, w, x, g_post, target, *, name, tm=256, riders=()):
    s, k = a.shape
    tm = _row_tile(s, tm)

    def body(a_ref, w_ref, x_ref, g_ref, t_ref, dx_ref, df_ref, dg_ref, loss_ref):
        first = pl.program_id(0) == 0
        fv = jnp.dot(a_ref[...], w_ref[...], preferred_element_type=F32)
        g = g_ref[...]
        err = x_ref[...] + fv * _rstd(fv) * g - t_ref[...]
        dx = err * (1.0 / D_MODEL)
        dx_ref[...] = dx
        df, dg = _rms_bwd(dx, fv, g)
        df_ref[...] = df.astype(BF16)
        _accum(dg_ref, dg, first)
        part = jnp.sum(jnp.sum(err * err, axis=-1, keepdims=True), axis=0, keepdims=True) * (0.5 / D_MODEL)
        _accum(loss_ref, jnp.broadcast_to(part, (8, LANES)), first)

    row, vec = _row_spec(tm, D_MODEL), _vec_spec(D_MODEL)
    core, rr = _call(
        body, grid=(s // tm,), in_specs=[_row_spec(tm, k), pl.BlockSpec((k, D_MODEL), lambda i: (0, 0)), row, vec, row],
        out_specs=[row, row, vec, pl.BlockSpec((8, LANES), lambda i: (0, 0))],
        out_shape=[jax.ShapeDtypeStruct((s, D_MODEL), F32), jax.ShapeDtypeStruct((s, D_MODEL), BF16),
                   jax.ShapeDtypeStruct((1, D_MODEL), F32), jax.ShapeDtypeStruct((8, LANES), F32)],
        operands=(a, w, x, g_post, target), name=name, riders=riders)
    return _ret(core, rr, riders)


def ffn_fwd_loss_rows(h, w_gu, w_d, x, g_post, target, *, name, tm=256, riders=()):
    s = x.shape[0]
    tm = _row_tile(s, tm)

    def body(h_ref, w0, w1, w2, w3, wd_ref, x_ref, g_ref, t_ref, d_ref, a_ref, dx_ref, df_ref, dg_ref, loss_ref):
        first = pl.program_id(0) == 0
        hv = h_ref[...]
        fv = None
        for half, (wg_ref, wu_ref) in enumerate(((w0, w2), (w1, w3))):
            cols = slice(half * FF_HALF, (half + 1) * FF_HALF)
            g = jnp.dot(hv, wg_ref[...], preferred_element_type=F32)
            u = jnp.dot(hv, wu_ref[...], preferred_element_type=F32)
            sig = _sigmoid(g)
            silu = g * sig
            d_ref[0, :, cols] = (u * (sig + silu * (1.0 - sig))).astype(BF16)
            d_ref[1, :, cols] = silu.astype(BF16)
            act = (silu * u).astype(BF16)
            a_ref[:, cols] = act
            p = jnp.dot(act, wd_ref[cols, :], preferred_element_type=F32)
            fv = p if fv is None else fv + p
        gain = g_ref[...]
        err = x_ref[...] + fv * _rstd(fv) * gain - t_ref[...]
        dx = err * (1.0 / D_MODEL)
        dx_ref[...] = dx
        df, dg = _rms_bwd(dx, fv, gain)
        df_ref[...] = df.astype(BF16)
        _accum(dg_ref, dg, first)
        part = jnp.sum(jnp.sum(err * err, axis=-1, keepdims=True), axis=0, keepdims=True) * (0.5 / D_MODEL)
        _accum(loss_ref, jnp.broadcast_to(part, (8, LANES)), first)

    def resident(shape, index):
        return pl.BlockSpec(shape, index, pipeline_mode=pl.Buffered(1))

    row, vec = _row_spec(tm, D_MODEL), _vec_spec(D_MODEL)
    shards = [resident((None, D_MODEL, FF_HALF), (lambda j: (lambda i: (j, 0, 0)))(j)) for j in range(N_CHIPS)]
    core, rr = _call(
        body, grid=(s // tm,),
        in_specs=[row] + shards + [resident((D_FF, D_MODEL), lambda i: (0, 0)), row, vec, row],
        out_specs=[pl.BlockSpec((2, tm, D_FF), lambda i: (0, i, 0)), _row_spec(tm, D_FF), row, row, vec,
                   pl.BlockSpec((8, LANES), lambda i: (0, 0))],
        out_shape=[jax.ShapeDtypeStruct((2, s, D_FF), BF16), jax.ShapeDtypeStruct((s, D_FF), BF16),
                   jax.ShapeDtypeStruct((s, D_MODEL), F32), jax.ShapeDtypeStruct((s, D_MODEL), BF16),
                   jax.ShapeDtypeStruct((1, D_MODEL), F32), jax.ShapeDtypeStruct((8, LANES), F32)],
        operands=(h, w_gu, w_gu, w_gu, w_gu, w_d, x, g_post, target), name=name, riders=riders)
    return _ret(core, rr, riders)


def dh_norm_bwd_pair(a, w, dres, x, g_pre, m, g_post, *, name, tm=512, sub=256, riders=()):
    _, kout, ns = w.shape
    planes = a.ndim == 3
    s = x.shape[0]
    tm = _row_tile(s, tm)
    sub = min(sub, tm)
    a_spec = pl.BlockSpec((2, tm, 2 * ns), lambda i: (0, i, 0)) if planes else pl.BlockSpec((tm, N_CHIPS * ns), lambda i: (i, 0))

    def body(a_ref, w0, w1, w2, w3, dres_ref, x_ref, gpre_ref, m_ref, gpost_ref, dx_ref, dm_ref, dgpre_ref, dgpost_ref, db_ref):
        first = pl.program_id(0) == 0
        sums = None
        for t in range(tm // sub):
            rows = slice(t * sub, (t + 1) * sub)
            dh = None
            for j, w_ref in enumerate((w0, w1, w2, w3)):
                a_j = a_ref[j // 2, rows, (j % 2) * ns:(j % 2 + 1) * ns] if planes else a_ref[rows, j * ns:(j + 1) * ns]
                p = lax.dot_general(a_j, w_ref[...], NT_DIMS, preferred_element_type=F32)
                dh = p if dh is None else dh + p
            d1, dgpre = _rms_bwd(dh, x_ref[rows, :], gpre_ref[...])
            dx = dres_ref[rows, :] + d1
            dx_ref[rows, :] = dx
            dm, dgpost = _rms_bwd(dx, m_ref[rows, :].astype(F32), gpost_ref[...])
            dm_ref[rows, :] = dm.astype(BF16)
            part = (dgpre, dgpost, jnp.sum(dm, axis=0, keepdims=True))
            sums = part if sums is None else tuple(u + v for u, v in zip(sums, part))
        _accum(dgpre_ref, sums[0], first)
        _accum(dgpost_ref, sums[1], first)
        _accum(db_ref, sums[2], first)

    def shard(j):
        return pl.BlockSpec((None, kout, ns), lambda i: (j, 0, 0))

    row, vec = _row_spec(tm, D_MODEL), _vec_spec(D_MODEL)
    vshape = jax.ShapeDtypeStruct((1, D_MODEL), F32)
    core, rr = _call(
        body, grid=(s // tm,), in_specs=[a_spec] + [shard(j) for j in range(N_CHIPS)] + [row, row, vec, row, vec],
        out_specs=[row, row, vec, vec, vec],
        out_shape=[jax.ShapeDtypeStruct((s, D_MODEL), F32), jax.ShapeDtypeStruct((s, D_MODEL), BF16), vshape, vshape, vshape],
        operands=(a, w, w, w, w, dres, x, g_pre, m, g_post), name=name, riders=riders)
    return _ret(core, rr, riders)


def ffn_bwd_rows(df, w_d, d_planes, w_gu, dres, x, g_pre, m, g_post, *, name, tm=256, riders=()):
    s = x.shape[0]
    tm = _row_tile(s, tm)

    def body(df_ref, wd_ref, d_ref, w0, w1, w2, w3, dres_ref, x_ref, gpre_ref, m_ref, gpost_ref,
             o_ref, dx_ref, dm_ref, dgpre_ref, dgpost_ref, db_ref):
        first = pl.program_id(0) == 0
        dfv = df_ref[...]
        dh = None
        for half, (wg_ref, wu_ref) in enumerate(((w0, w2), (w1, w3))):
            cols = slice(half * FF_HALF, (half + 1) * FF_HALF)
            da = lax.dot_general(dfv, wd_ref[cols, :], NT_DIMS, preferred_element_type=F32)
            dg = (da * d_ref[0, :, cols].astype(F32)).astype(BF16)
            du = (da * d_ref[1, :, cols].astype(F32)).astype(BF16)
            o_ref[0, :, cols] = dg
            o_ref[1, :, cols] = du
            p = lax.dot_general(dg, wg_ref[...], NT_DIMS, preferred_element_type=F32)
            p += lax.dot_general(du, wu_ref[...], NT_DIMS, preferred_element_type=F32)
            dh = p if dh is None else dh + p
        d1, dgpre = _rms_bwd(dh, x_ref[...], gpre_ref[...])
        dx = dres_ref[...] + d1
        dx_ref[...] = dx
        dm, dgpost = _rms_bwd(dx, m_ref[...].astype(F32), gpost_ref[...])
        dm_ref[...] = dm.astype(BF16)
        _accum(dgpre_ref, dgpre, first)
        _accum(dgpost_ref, dgpost, first)
        _accum(db_ref, jnp.sum(dm, axis=0, keepdims=True), first)

    def resident(shape, index):
        return pl.BlockSpec(shape, index, pipeline_mode=pl.Buffered(1))

    planes = pl.BlockSpec((2, tm, D_FF), lambda i: (0, i, 0))
    row, vec = _row_spec(tm, D_MODEL), _vec_spec(D_MODEL)
    vshape = jax.ShapeDtypeStruct((1, D_MODEL), F32)
    shards = [resident((None, D_MODEL, FF_HALF), (lambda j: (lambda i: (j, 0, 0)))(j)) for j in range(N_CHIPS)]
    core, rr = _call(
        body, grid=(s // tm,),
        in_specs=[row, resident((D_FF, D_MODEL), lambda i: (0, 0)), planes] + shards + [row, row, vec, row, vec],
        out_specs=[planes, row, row, vec, vec, vec],
        out_shape=[jax.ShapeDtypeStruct((2, s, D_FF), BF16), jax.ShapeDtypeStruct((s, D_MODEL), F32),
                   jax.ShapeDtypeStruct((s, D_MODEL), BF16), vshape, vshape, vshape],
        operands=(df, w_d, d_planes, w_gu, w_gu, w_gu, w_gu, dres, x, g_pre, m, g_post), name=name, riders=riders)
    return _ret(core, rr, riders)


def norm_bwd_last(dres, dh, x, g_pre, *, name, tm=256, riders=()):
    s = x.shape[0]
    tm = _row_tile(s, tm)

    def body(dres_ref, dh_ref, x_ref, g_ref, dx_ref, dg_ref):
        d1, dg = _rms_bwd(dh_ref[...], x_ref[...], g_ref[...])
        dx_ref[...] = dres_ref[...] + d1
        _accum(dg_ref, dg, pl.program_id(0) == 0)

    row, vec = _row_spec(tm, D_MODEL), _vec_spec(D_MODEL)
    core, rr = _call(
        body, grid=(s // tm,), in_specs=[row, row, row, vec], out_specs=[row, vec],
        out_shape=[jax.ShapeDtypeStruct((s, D_MODEL), F32), jax.ShapeDtypeStruct((1, D_MODEL), F32)],
        operands=(dres, dh, x, g_pre), name=name, riders=riders)
    return _ret(core, rr, riders)


def _rope_tables(s):
    half = HEAD_DIM // 2
    inv_freq = np.float32(ROPE_THETA) ** (-(np.arange(half, dtype=np.float32) * np.float32(2.0)) / np.float32(HEAD_DIM))
    ang = np.arange(s, dtype=np.float32)[:, None] * inv_freq[None, :]
    cos, sin = np.cos(ang).astype(np.float32), np.sin(ang).astype(np.float32)
    return jnp.asarray(np.tile(cos, (1, 4))), jnp.asarray(np.concatenate([-sin, sin, -sin, sin], axis=1))


def _swap_halves(x):
    lane = lax.broadcasted_iota(I32, x.shape, 1)
    return jnp.where((lane & (HEAD_DIM - 1)) < HEAD_DIM // 2, pltpu.roll(x, LANES - 32, 1), pltpu.roll(x, 32, 1))


N_ROPE_BLOCKS = (Q_WIDTH + KV_WIDTH) // LANES


def qkv_proj(h, w, bias, cos, sin, *, name, tm=512, riders=()):
    s, k = h.shape
    ns = w.shape[2]
    tm = _row_tile(s, tm)

    def body(h_ref, w_ref, b_ref, c_ref, s_ref, o_ref):
        j = pl.program_id(0)
        sub = min(256, tm)
        for t in range(tm // sub):
            rows = slice(t * sub, (t + 1) * sub)
            p = jnp.dot(h_ref[rows, :], w_ref[...], preferred_element_type=F32) + b_ref[...]
            cosv, sinv = c_ref[rows, :], s_ref[rows, :]
            for blk in range(ns // LANES):
                xb = p[:, blk * LANES:(blk + 1) * LANES]
                roped = xb * cosv + _swap_halves(xb) * sinv
                is_qk = j * (ns // LANES) + blk < N_ROPE_BLOCKS
                o_ref[rows, blk * LANES:(blk + 1) * LANES] = jnp.where(is_qk, roped, xb).astype(BF16)

    core, rr = _call(
        body, grid=(N_CHIPS, s // tm),
        in_specs=[pl.BlockSpec((tm, k), lambda j, i: (i, 0)), pl.BlockSpec((None, k, ns), lambda j, i: (j, 0, 0)),
                  pl.BlockSpec((1, ns), lambda j, i: (0, j)), pl.BlockSpec((tm, LANES), lambda j, i: (i, 0)),
                  pl.BlockSpec((tm, LANES), lambda j, i: (i, 0))],
        out_specs=[pl.BlockSpec((tm, ns), lambda j, i: (i, j))], out_shape=[jax.ShapeDtypeStruct((s, N_CHIPS * ns), BF16)],
        operands=(h, w, bias, cos, sin), sem=("parallel", "parallel"), name=name, riders=riders)
    return _ret(core, rr, riders)


def rope_bwd(dq, dkc, dkp, dvc, dvp, cos, sin, *, name, riders=()):
    s = dq.shape[0]
    tm = 2 * WINDOW if s % (2 * WINDOW) == 0 else WINDOW
    nb = s // tm

    def body(dq_ref, dkc_ref, dkp_ref, dkp_next_ref, dvc_ref, dvp_ref, dvp_next_ref, c_ref, s_ref, o_ref, db_ref):
        i = pl.program_id(0)
        has_next = (i < nb - 1).astype(F32)
        cosv, sinv = c_ref[...], s_ref[...]

        def shifted(ref, next_ref, cols):
            last = has_next * next_ref[:WINDOW, cols].astype(F32)
            return last if tm == WINDOW else jnp.concatenate([ref[WINDOW:, cols].astype(F32), last], axis=0)

        parts = []
        for blk in range(QKV_WIDTH // LANES):
            if blk < Q_WIDTH // LANES:
                g = dq_ref[:, blk * LANES:(blk + 1) * LANES].astype(F32)
            else:
                own, prv, nxt = (dkc_ref, dkp_ref, dkp_next_ref) if blk < N_ROPE_BLOCKS else (dvc_ref, dvp_ref, dvp_next_ref)
                cols = slice((blk % 2) * LANES, (blk % 2 + 1) * LANES)
                g = own[:, cols].astype(F32) + shifted(prv, nxt, cols)
            if blk < N_ROPE_BLOCKS:
                g = g * cosv + _swap_halves(g * sinv)
            o_ref[:, blk * LANES:(blk + 1) * LANES] = g.astype(BF16)
            parts.append(jnp.sum(g, axis=0, keepdims=True))
        sums = jnp.concatenate(parts, axis=1)
        _accum(db_ref, sums, i == 0)

    own_spec = _row_spec(tm, KV_WIDTH)
    next_spec = pl.BlockSpec((tm, KV_WIDTH), lambda i: (jnp.minimum(i + 1, nb - 1), 0))
    core, rr = _call(
        body, grid=(nb,),
        in_specs=[_row_spec(tm, Q_WIDTH), own_spec, own_spec, next_spec, own_spec, own_spec, next_spec,
                  _row_spec(tm, LANES), _row_spec(tm, LANES)],
        out_specs=[_row_spec(tm, QKV_WIDTH), _vec_spec(QKV_WIDTH)],
        out_shape=[jax.ShapeDtypeStruct((s, QKV_WIDTH), BF16), jax.ShapeDtypeStruct((1, QKV_WIDTH), F32)],
        operands=(dq, dkc, dkp, dkp, dvc, dvp, dvp, cos, sin), name=name, riders=riders)
    return _ret(core, rr, riders)


ROWS = GQA_GROUP * WINDOW


def _prev_slots():
    kpos = lax.broadcasted_iota(I32, (WINDOW, ROWS), 0)
    qpos = lax.broadcasted_iota(I32, (WINDOW, ROWS), 1) & (WINDOW - 1)
    return kpos > qpos


def _head_cols(ref, head):
    return ref[:, head * HEAD_DIM:(head + 1) * HEAD_DIM]


def _stack_heads(ref, h):
    return jnp.concatenate([_head_cols(ref, GQA_GROUP * h + g) for g in range(GQA_GROUP)], axis=0)


def _band(prev_ref, cur_ref, h):
    return jnp.concatenate([_head_cols(prev_ref, h), _head_cols(cur_ref, h)], axis=0)


def _pick(prev, band):
    return jnp.where(prev, band[:WINDOW], band[WINDOW:])


def _spread(prev, x):
    return jnp.concatenate([jnp.where(prev, x, 0.0), jnp.where(prev, 0.0, x)], axis=0).astype(BF16)


def _attn_probs(s_band, sink, prev, has_prev):
    scale = HEAD_DIM ** -0.5
    s = jnp.where(prev, jnp.where(has_prev, s_band[:WINDOW], NEG), s_band[WINDOW:]) * scale
    m = jnp.maximum(jnp.max(s, axis=0, keepdims=True), sink)
    e, es = jnp.exp(s - m), jnp.exp(sink - m)
    inv = 1.0 / (jnp.sum(e, axis=0, keepdims=True) + es)
    return e * inv, es * inv


def _attn_specs(nb):
    kcol, vcol = Q_WIDTH // KV_WIDTH, Q_WIDTH // KV_WIDTH + 1
    q_spec = pl.BlockSpec((WINDOW, Q_WIDTH), lambda n: (n, 0))
    return [q_spec,
            pl.BlockSpec((WINDOW, KV_WIDTH), lambda n: (n, kcol)),
            pl.BlockSpec((WINDOW, KV_WIDTH), lambda n: (jnp.maximum(n - 1, 0), kcol)),
            pl.BlockSpec((WINDOW, KV_WIDTH), lambda n: (n, vcol)),
            pl.BlockSpec((WINDOW, KV_WIDTH), lambda n: (jnp.maximum(n - 1, 0), vcol)),
            pl.BlockSpec((N_KV_HEADS, 8, ROWS), lambda n: (0, 0, 0))]


def attn_fwd(qkv, sink_rows, *, name, riders=()):
    s = qkv.shape[0]

    def body(q_ref, kc_ref, kp_ref, vc_ref, vp_ref, sink_ref, o_ref):
        prev = _prev_slots()
        has_prev = pl.program_id(0) > 0
        heads = range(N_KV_HEADS)
        s_bands = [lax.dot_general(_band(kp_ref, kc_ref, h), _stack_heads(q_ref, h), NT_DIMS, preferred_element_type=F32)
                   for h in heads]
        p_bands = [_spread(prev, _attn_probs(s_bands[h], sink_ref[h, 0:1, :], prev, has_prev)[0]) for h in heads]
        outs = [lax.dot_general(_band(vp_ref, vc_ref, h), p_bands[h], TN_DIMS, preferred_element_type=F32).T for h in heads]
        for h in heads:
            for g in range(GQA_GROUP):
                head = GQA_GROUP * h + g
                o_ref[:, head * HEAD_DIM:(head + 1) * HEAD_DIM] = outs[h][g * WINDOW:(g + 1) * WINDOW].astype(BF16)

    core, rr = _call(
        body, grid=(s // WINDOW,), in_specs=_attn_specs(s // WINDOW), out_specs=[pl.BlockSpec((WINDOW, Q_WIDTH), lambda n: (n, 0))],
        out_shape=[jax.ShapeDtypeStruct((s, Q_WIDTH), BF16)], operands=(qkv, qkv, qkv, qkv, qkv, sink_rows), sem=("parallel",),
        name=name, riders=riders)
    return _ret(core, rr, riders)


def attn_bwd(qkv, sink_rows, do, *, name, riders=()):
    s = qkv.shape[0]

    def body(q_ref, kc_ref, kp_ref, vc_ref, vp_ref, sink_ref, do_ref, dq_ref, dkc_ref, dkp_ref, dvc_ref, dvp_ref, dsink_ref):
        n = pl.program_id(0)
        prev = _prev_slots()
        scale = HEAD_DIM ** -0.5
        heads = range(N_KV_HEADS)
        qs, dos = [_stack_heads(q_ref, h) for h in heads], [_stack_heads(do_ref, h) for h in heads]
        kbands, vbands = [_band(kp_ref, kc_ref, h) for h in heads], [_band(vp_ref, vc_ref, h) for h in heads]
        s_bands = [lax.dot_general(kbands[h], qs[h], NT_DIMS, preferred_element_type=F32) for h in heads]
        dp_bands = [lax.dot_general(vbands[h], dos[h], NT_DIMS, preferred_element_type=F32) for h in heads]
        ds_bands, p_bands, parts = [], [], []
        for h in heads:
            p, ps = _attn_probs(s_bands[h], sink_ref[h, 0:1, :], prev, n > 0)
            dp = _pick(prev, dp_bands[h])
            delta = jnp.sum(p * dp, axis=0, keepdims=True)
            ds_bands.append(_spread(prev, p * (dp - delta) * scale))
            p_bands.append(_spread(prev, p))
            dsink = -(ps * delta)
            for g in range(GQA_GROUP):
                parts.append(jnp.broadcast_to(jnp.sum(dsink[:, g * WINDOW:(g + 1) * WINDOW], axis=1, keepdims=True), (8, LANES)))
        for h in heads:
            dk = jnp.dot(ds_bands[h], qs[h], preferred_element_type=F32).astype(BF16)
            dv = jnp.dot(p_bands[h], dos[h], preferred_element_type=F32).astype(BF16)
            dq = lax.dot_general(kbands[h], ds_bands[h], TN_DIMS, preferred_element_type=F32).T
            cols = slice(h * HEAD_DIM, (h + 1) * HEAD_DIM)
            dkp_ref[:, cols], dkc_ref[:, cols] = dk[:WINDOW], dk[WINDOW:]
            dvp_ref[:, cols], dvc_ref[:, cols] = dv[:WINDOW], dv[WINDOW:]
            for g in range(GQA_GROUP):
                head = GQA_GROUP * h + g
                dq_ref[:, head * HEAD_DIM:(head + 1) * HEAD_DIM] = dq[g * WINDOW:(g + 1) * WINDOW].astype(BF16)

        @pl.when(n == 0)
        def _():
            for i, part in enumerate(parts):
                dsink_ref[i // GQA_GROUP, i % GQA_GROUP] = part

        @pl.when(n > 0)
        def _():
            for i, part in enumerate(parts):
                dsink_ref[i // GQA_GROUP, i % GQA_GROUP] += part

    rows_q = pl.BlockSpec((WINDOW, Q_WIDTH), lambda n: (n, 0))
    rows_kv = pl.BlockSpec((WINDOW, KV_WIDTH), lambda n: (n, 0))
    kv_shape = jax.ShapeDtypeStruct((s, KV_WIDTH), BF16)
    core, rr = _call(
        body, grid=(s // WINDOW,), in_specs=_attn_specs(s // WINDOW) + [rows_q],
        out_specs=[rows_q, rows_kv, rows_kv, rows_kv, rows_kv,
                   pl.BlockSpec((N_KV_HEADS, GQA_GROUP, 8, LANES), lambda n: (0, 0, 0, 0))],
        out_shape=[jax.ShapeDtypeStruct((s, Q_WIDTH), BF16), kv_shape, kv_shape, kv_shape, kv_shape,
                   jax.ShapeDtypeStruct((N_KV_HEADS, GQA_GROUP, 8, LANES), F32)],
        operands=(qkv, qkv, qkv, qkv, qkv, sink_rows, do), sem=("arbitrary",), name=name, riders=riders)
    return _ret(core, rr, riders)


GELU_C = 0.7978845608028654
GELU_A = 0.044715


def _gelu(x):
    return 0.5 * x * (1.0 + jnp.tanh(x * (GELU_C + (GELU_C * GELU_A) * (x * x))))


def _gelu_and_grad(x):
    x2 = x * x
    t = jnp.tanh(x * (GELU_C + (GELU_C * GELU_A) * x2))
    half_x, one_t = 0.5 * x, 1.0 + t
    return half_x * one_t, 0.5 * one_t + half_x * (1.0 - t * t) * (GELU_C + (3.0 * GELU_C * GELU_A) * x2)


def _tril_bf16(w):
    row = lax.broadcasted_iota(I32, (SGU_CHUNK, SGU_CHUNK), 0)
    col = lax.broadcasted_iota(I32, (SGU_CHUNK, SGU_CHUNK), 1)
    return jnp.where(row >= col, w, 0.0).astype(BF16)


def _sgu_norm(vg, g, b):
    mu = jnp.mean(vg, axis=-1, keepdims=True)
    cen = vg - mu
    rstd = lax.rsqrt(jnp.mean(cen * cen, axis=-1, keepdims=True) + EPS)
    xhat = cen * rstd
    return xhat, rstd, xhat * g + b


def sgu_in_fwd(h, w_in, ln_g, ln_b, w_sp, b_sp, *, name, tm=256, riders=()):
    s, k = h.shape
    ns = w_in.shape[2]
    tm = _row_tile(s, tm)

    def body(h_ref, w0, w1, w2, w3, g_ref, b_ref, w_ref, bs_ref, z_ref, y_ref):
        hv = h_ref[...]
        zs = [jnp.dot(hv, w_ref_j[...], preferred_element_type=F32) for w_ref_j in (w0, w1, w2, w3)]
        for j, zj in enumerate(zs):
            z_ref[:, j * ns:(j + 1) * ns] = zj.astype(BF16)
        u = _gelu(jnp.concatenate(zs[:2], axis=1))
        _, _, vn = _sgu_norm(_gelu(jnp.concatenate(zs[2:], axis=1)), g_ref[...], b_ref[...])
        vn = vn.astype(BF16)
        for grp in range(SGU_GROUPS):
            w = _tril_bf16(w_ref[grp])
            cols = slice(grp * LANES, (grp + 1) * LANES)
            for ch in range(tm // SGU_CHUNK):
                rows = slice(ch * SGU_CHUNK, (ch + 1) * SGU_CHUNK)
                mixed = jnp.dot(w, vn[rows, cols], preferred_element_type=F32) + bs_ref[grp]
                y_ref[rows, cols] = (u[rows, cols] * mixed).astype(BF16)

    def shard(j):
        return pl.BlockSpec((None, k, ns), lambda i: (j, 0, 0))

    full3 = pl.BlockSpec((SGU_GROUPS, SGU_CHUNK, SGU_CHUNK), lambda i: (0, 0, 0))
    core, rr = _call(
        body, grid=(s // tm,),
        in_specs=[_row_spec(tm, k)] + [shard(j) for j in range(N_CHIPS)] + [_vec_spec(D_MODEL), _vec_spec(D_MODEL), full3, full3],
        out_specs=[_row_spec(tm, 2 * D_MODEL), _row_spec(tm, D_MODEL)],
        out_shape=[jax.ShapeDtypeStruct((s, 2 * D_MODEL), BF16), jax.ShapeDtypeStruct((s, D_MODEL), BF16)],
        operands=(h, w_in, w_in, w_in, w_in, ln_g, ln_b, w_sp, b_sp), sem=("parallel",), name=name, riders=riders)
    return _ret(core, rr, riders)


def sgu_bwd(z, dy, ln_g, ln_b, w_sp, b_sp, *, name, tm=256, riders=()):
    s = z.shape[0]
    tm = _row_tile(s, tm)

    def body(z_ref, dy_ref, g_ref, b_ref, w_ref, bs_ref, dz_ref, dw_ref, dbs_ref, dg_ref, db_ref, dvn_buf):
        first = pl.program_id(0) == 0
        u, u_grad = _gelu_and_grad(z_ref[:, :D_MODEL].astype(F32))
        vg, v_grad = _gelu_and_grad(z_ref[:, D_MODEL:].astype(F32))
        xhat, rstd, vn = _sgu_norm(vg, g_ref[...], b_ref[...])
        vn = vn.astype(BF16)
        dyv = dy_ref[...]
        dmixed = dyv * u
        dz_gate = dyv * u_grad
        row = lax.broadcasted_iota(I32, (SGU_CHUNK, SGU_CHUNK), 0)
        col = lax.broadcasted_iota(I32, (SGU_CHUNK, SGU_CHUNK), 1)
        dws, dbss = [], []
        for grp in range(SGU_GROUPS):
            w = _tril_bf16(w_ref[grp])
            cols = slice(grp * LANES, (grp + 1) * LANES)
            dw = jnp.zeros((SGU_CHUNK, SGU_CHUNK), F32)
            dbs = jnp.zeros((SGU_CHUNK, 1), F32)
            for ch in range(tm // SGU_CHUNK):
                rows = slice(ch * SGU_CHUNK, (ch + 1) * SGU_CHUNK)
                vblk = vn[rows, cols]
                mixed = jnp.dot(w, vblk, preferred_element_type=F32) + bs_ref[grp]
                dz_ref[rows, cols] = (dz_gate[rows, cols] * mixed).astype(BF16)
                dm = dmixed[rows, cols]
                dmb = dm.astype(BF16)
                dvn_buf[rows, cols] = lax.dot_general(w, dmb, TN_DIMS, preferred_element_type=F32)
                dw += lax.dot_general(dmb, vblk, NT_DIMS, preferred_element_type=F32)
                dbs += jnp.sum(dm, axis=-1, keepdims=True)
            dws.append(jnp.where(row >= col, dw, 0.0))
            dbss.append(jnp.broadcast_to(dbs, (SGU_CHUNK, SGU_CHUNK)))

        dvn = dvn_buf[...]
        dxhat = dvn * g_ref[...]
        dvg = rstd * (dxhat - jnp.mean(dxhat, axis=-1, keepdims=True) - xhat * jnp.mean(dxhat * xhat, axis=-1, keepdims=True))
        dz_ref[:, D_MODEL:] = (dvg * v_grad).astype(BF16)
        dlng, dlnb = jnp.sum(dvn * xhat, axis=0, keepdims=True), jnp.sum(dvn, axis=0, keepdims=True)

        @pl.when(first)
        def _():
            for grp in range(SGU_GROUPS):
                dw_ref[grp] = dws[grp]
                dbs_ref[grp] = dbss[grp]
            dg_ref[...] = dlng
            db_ref[...] = dlnb

        @pl.when(jnp.logical_not(first))
        def _():
            for grp in range(SGU_GROUPS):
                dw_ref[grp] += dws[grp]
                dbs_ref[grp] += dbss[grp]
            dg_ref[...] += dlng
            db_ref[...] += dlnb

    full3 = pl.BlockSpec((SGU_GROUPS, SGU_CHUNK, SGU_CHUNK), lambda i: (0, 0, 0))
    s3 = jax.ShapeDtypeStruct((SGU_GROUPS, SGU_CHUNK, SGU_CHUNK), F32)
    vshape = jax.ShapeDtypeStruct((1, D_MODEL), F32)
    core, rr = _call(
        body, grid=(s // tm,),
        in_specs=[_row_spec(tm, 2 * D_MODEL), _row_spec(tm, D_MODEL), _vec_spec(D_MODEL), _vec_spec(D_MODEL), full3, full3],
        out_specs=[_row_spec(tm, 2 * D_MODEL), full3, full3, _vec_spec(D_MODEL), _vec_spec(D_MODEL)],
        out_shape=[jax.ShapeDtypeStruct((s, 2 * D_MODEL), BF16), s3, s3, vshape, vshape],
        scratch_shapes=[pltpu.VMEM((tm, D_MODEL), F32)], operands=(z, dy, ln_g, ln_b, w_sp, b_sp), name=name, riders=riders)
    return _ret(core, rr, riders)


def _sigmoid(x):
    return 1.0 / (1.0 + jnp.exp(-x))


def ffn_up(h, w_gu, *, name, tm=512, riders=()):
    s = h.shape[0]
    tm = _row_tile(s, tm)

    def body(h_ref, wg_ref, wu_ref, d_ref, a_ref):
        hv = h_ref[...]
        sub = min(256, tm)
        for t in range(tm // sub):
            rows = slice(t * sub, (t + 1) * sub)
            g = jnp.dot(hv[rows], wg_ref[...], preferred_element_type=F32)
            u = jnp.dot(hv[rows], wu_ref[...], preferred_element_type=F32)
            sig = _sigmoid(g)
            silu = g * sig
            d_ref[0, rows, :] = (u * (sig + silu * (1.0 - sig))).astype(BF16)
            d_ref[1, rows, :] = silu.astype(BF16)
            a_ref[rows, :] = (silu * u).astype(BF16)

    core, rr = _call(
        body, grid=(2, s // tm),
        in_specs=[pl.BlockSpec((tm, D_MODEL), lambda j, i: (i, 0)),
                  pl.BlockSpec((None, D_MODEL, FF_HALF), lambda j, i: (j, 0, 0)),
                  pl.BlockSpec((None, D_MODEL, FF_HALF), lambda j, i: (j + 2, 0, 0))],
        out_specs=[pl.BlockSpec((2, tm, FF_HALF), lambda j, i: (0, i, j)), pl.BlockSpec((tm, FF_HALF), lambda j, i: (i, j))],
        out_shape=[jax.ShapeDtypeStruct((2, s, D_FF), BF16), jax.ShapeDtypeStruct((s, D_FF), BF16)],
        operands=(h, w_gu, w_gu), sem=("parallel", "parallel"), name=name, riders=riders)
    return _ret(core, rr, riders)


def ffn_dact(df, w_d, gu, *, name, tm=512, riders=()):
    s = df.shape[0]
    tm = _row_tile(s, tm)

    def body(df_ref, w_ref, d_ref, o_ref):
        da = lax.dot_general(df_ref[...], w_ref[...], NT_DIMS, preferred_element_type=F32)
        o_ref[0] = (da * d_ref[0].astype(F32)).astype(BF16)
        o_ref[1] = (da * d_ref[1].astype(F32)).astype(BF16)

    planes = pl.BlockSpec((2, tm, FF_HALF), lambda j, i: (0, i, j))
    core, rr = _call(
        body, grid=(2, s // tm),
        in_specs=[pl.BlockSpec((tm, D_MODEL), lambda j, i: (i, 0)), pl.BlockSpec((FF_HALF, D_MODEL), lambda j, i: (j, 0)), planes],
        out_specs=[planes], out_shape=[jax.ShapeDtypeStruct((2, s, D_FF), BF16)], operands=(df, w_d, gu),
        sem=("parallel", "parallel"), name=name, riders=riders)
    return _ret(core, rr, riders)


def _weight_tile(rows):
    for tr in (512, 352, 256, 128):
        if rows % tr == 0:
            return tr
    return rows


def place_shard(w, layer, chip_arr, dtype, *, name, riders=()):
    _, r, c = w.shape
    tr = _weight_tile(r)

    def body(chip_ref, w_ref, o_ref):
        o_ref[...] = w_ref[...].astype(dtype)

    core, rr = _call(
        body, grid=(r // tr,), prefetch=(chip_arr,),
        in_specs=[pl.BlockSpec((None, tr, c), lambda i, chip: (layer, i, 0))],
        out_specs=[pl.BlockSpec((None, tr, c), lambda i, chip: (chip[0], i, 0))],
        out_shape=[jax.ShapeDtypeStruct((N_CHIPS, r, c), dtype)], operands=(w,), sem=("parallel",), name=name, riders=riders)
    return _ret(core, rr, riders)


def _adamw_math(w, g, m, v):
    m = ADAM_B1 * m + (1.0 - ADAM_B1) * g
    v = ADAM_B2 * v + (1.0 - ADAM_B2) * (g * g)
    m_hat = m / (1.0 - ADAM_B1 ** ADAM_STEP)
    v_hat = v / (1.0 - ADAM_B2 ** ADAM_STEP)
    delta = -ADAM_LR * (m_hat / (jnp.sqrt(v_hat) + ADAM_EPS) + ADAM_WD * w)
    return delta, m, v


def adamw(w, g, m, v, *, name):
    nl, r, c = w.shape
    tr = _weight_tile(r)

    def body(w_ref, g_ref, m_ref, v_ref, go_ref, d_ref, mo_ref, vo_ref):
        gv = g_ref[...]
        go_ref[...] = gv
        d_ref[...], mo_ref[...], vo_ref[...] = _adamw_math(w_ref[...], gv, m_ref[...], v_ref[...])

    spec = pl.BlockSpec((None, tr, c), lambda l, i: (l, i, 0))
    shape = jax.ShapeDtypeStruct(w.shape, F32)
    outs, _ = _call(body, grid=(nl, r // tr), in_specs=[spec] * 4, out_specs=[spec] * 4, out_shape=[shape] * 4,
                    operands=(w, g, m, v), sem=("parallel", "parallel"), name=name)
    return outs


def adamw_small(ws, gs, ms, vs, *, name):
    n = len(ws)

    def body(*refs):
        ins, outs = refs[:4 * n], refs[4 * n:]
        for t in range(n):
            gv = ins[n + t][...]
            outs[t][...] = gv
            outs[n + t][...], outs[2 * n + t][...], outs[3 * n + t][...] = _adamw_math(
                ins[t][...], gv, ins[2 * n + t][...], ins[3 * n + t][...])

    shapes = [jax.ShapeDtypeStruct(w.shape, F32) for w in ws]
    res = pl.pallas_call(body, out_shape=shapes * 4, name=name)(*ws, *gs, *ms, *vs)
    return res[:n], res[n:2 * n], res[2 * n:3 * n], res[3 * n:]


def pair_add(g, r1, c_arr, *, name):
    _, rows, cdim = g.shape
    h = rows // 2

    def body(c_ref, g_ref, r_ref, o_ref):
        o_ref[...] = (g_ref[...].astype(F32) + r_ref[...].astype(F32)).astype(o_ref.dtype)

    (out,), _ = _call(
        body, grid=(N_CHIPS,), prefetch=(c_arr,),
        in_specs=[pl.BlockSpec((None, h, cdim), lambda s, c: (s, c[0], 0)), pl.BlockSpec((None, h, cdim), lambda s, c: (s, 0, 0))],
        out_specs=[pl.BlockSpec((None, h, cdim), lambda s, c: (s, 0, 0))],
        out_shape=[jax.ShapeDtypeStruct((N_CHIPS, h, cdim), g.dtype)], operands=(g, r1), sem=("parallel",), name=name)
    return out


def final_add(g, r1, r2, jc_arr, *, dest_shape, lead, prev, name):
    _, rows, cdim = g.shape
    h = rows // 2

    def body(jc_ref, g_ref, r1_ref, r2_ref, *rest):
        o_ref = rest[-1]
        acc = g_ref[...].astype(F32) + r1_ref[...].astype(F32)
        for k in range(3):
            acc = acc + r2_ref[k].astype(F32)
        o_ref[...] = acc

    if lead is None:
        o_spec = pl.BlockSpec((h, cdim), lambda i, jc: (jc[1], 0))
    elif lead == "chip":
        o_spec = pl.BlockSpec((None, h, cdim), lambda i, jc: (jc[0], jc[1], 0))
    else:
        o_spec = pl.BlockSpec((None, h, cdim), lambda i, jc: (lead, jc[1], 0))
    in_specs = [pl.BlockSpec((None, h, cdim), lambda i, jc: (jc[0], jc[1], 0)),
                pl.BlockSpec((None, h, cdim), lambda i, jc: (jc[0], 0, 0)),
                pl.BlockSpec((3, h, cdim), lambda i, jc: (0, 0, 0))]
    operands = [g, r1, r2]
    aliases = None
    if prev is not None:
        in_specs.append(ANY)
        operands.append(prev)
        aliases = {3: 0}
    (out,), _ = _call(body, grid=(1,), prefetch=(jc_arr,), in_specs=in_specs, out_specs=[o_spec],
                      out_shape=[jax.ShapeDtypeStruct(dest_shape, F32)], operands=operands, aliases=aliases, name=name)
    return out


def _place():
    return lax.axis_index("x"), lax.axis_index("y"), lax.axis_index("c")


def _partner(x, y, k):
    return (1 - x if k >> 1 else x), (1 - y if k & 1 else y)


WHOLE = (0, 1, 1)


def _half(rows, sel, dtype, piece=WHOLE):
    lo, hi, n = piece
    align = 16 if dtype == BF16 else 8
    step = rows // 2 // n
    assert rows // 2 == step * n and step % align == 0
    return pl.ds(pl.multiple_of(sel * (rows // 2) + lo * step, align), (hi - lo) * step)


def _rider(peers, inputs, aliased, fresh, nsem, copies, arrivals):
    def start(ins, outs, send, recv):
        for cp in copies(ins, outs, send, recv):
            cp.start()

    def finish(ins, outs, send, recv):
        for cp in arrivals(ins, outs, send, recv):
            cp.wait_recv()
        for cp in copies(ins, outs, send, recv):
            cp.wait_send()

    return types.SimpleNamespace(peers=peers, inputs=list(inputs), aliased=list(aliased), fresh=list(fresh), nsem=nsem,
                                 start=start, finish=finish)


def _remote(src, dst, send, recv, idx, dev):
    return pltpu.make_async_remote_copy(src_ref=src, dst_ref=dst, send_sem=send.at[idx], recv_sem=recv.at[idx],
                                        device_id=dev, device_id_type=MESH)


def gather_ici_rider(fulls, pieces=None):
    nt = len(fulls)
    pieces = pieces or [WHOLE] * nt

    def region(outs, t, slot, sel):
        return outs[t].at[slot, _half(fulls[t].shape[1], sel, fulls[t].dtype, pieces[t])]

    def copies(ins, outs, send, recv):
        x, y, c = _place()
        res = []
        for t in range(nt):
            for k in (1, 2, 3):
                px, py = _partner(x, y, k)
                mine = region(outs, t, 2 * x + y, c)
                res.append(_remote(mine, mine, send, recv, 3 * t + k - 1, (px, py, c)))
        return res

    def arrivals(ins, outs, send, recv):
        x, y, c = _place()
        res = []
        for t in range(nt):
            for k in (1, 2, 3):
                px, py = _partner(x, y, k)
                theirs = region(outs, t, 2 * px + py, c)
                res.append(_remote(theirs, theirs, send, recv, 3 * t + k - 1, (x, y, c)))
        return res

    return _rider("chips", fulls, range(nt), [], 3 * nt, copies, arrivals)


def gather_d2d_rider(fulls, pieces=None):
    nt = len(fulls)
    pieces = pieces or [WHOLE] * nt

    def region(outs, t, slot, sel):
        return outs[t].at[slot, _half(fulls[t].shape[1], sel, fulls[t].dtype, pieces[t])]

    def both(outs, send, recv, mine):
        x, y, c = _place()
        res = []
        for t in range(nt):
            for k in (1, 2, 3):
                px, py = _partner(x, y, k)
                part = region(outs, t, 2 * px + py, c if mine else 1 - c)
                res.append(_remote(part, part, send, recv, 3 * t + k - 1, (x, y, 1 - c)))
        return res

    return _rider("sibling", fulls, range(nt), [], 3 * nt, lambda i, o, s, r: both(o, s, r, True),
                  lambda i, o, s, r: both(o, s, r, False))


def exchange_rider(grads):
    nt = len(grads)

    def both(ins, outs, send, recv):
        x, y, c = _place()
        return [_remote(ins[t].at[:, _half(grads[t].shape[1], 1 - c, grads[t].dtype)], outs[t], send, recv, t, (x, y, 1 - c))
                for t in range(nt)]

    fresh = [jax.ShapeDtypeStruct((N_CHIPS, g.shape[1] // 2, g.shape[2]), g.dtype) for g in grads]
    return _rider("sibling", grads, [], fresh, nt, both, both)


def scatter_rider(parts):
    nt = len(parts)

    def both(ins, outs, send, recv):
        x, y, c = _place()
        res = []
        for t in range(nt):
            for k in (1, 2, 3):
                px, py = _partner(x, y, k)
                res.append(_remote(ins[t].at[2 * px + py], outs[t].at[k - 1], send, recv, 3 * t + k - 1, (px, py, c)))
        return res

    fresh = [jax.ShapeDtypeStruct((3,) + p.shape[1:], p.dtype) for p in parts]
    return _rider("chips", parts, [], fresh, 3 * nt, both, both)


def broadcast_rider(bufs, items):
    def region(outs, item, sel):
        bi, lead = item
        ref = outs[bi]
        if lead == "chip":
            x, y, _ = _place()
            ref = ref.at[2 * x + y]
        elif lead is not None:
            ref = ref.at[lead]
        return ref.at[_half(ref.shape[0], sel, F32)]

    def both(outs, send, recv, mine):
        x, y, c = _place()
        res = []
        for i, item in enumerate(items):
            part = region(outs, item, c if mine else 1 - c)
            res.append(_remote(part, part, send, recv, i, (x, y, 1 - c)))
        return res

    return _rider("sibling", bufs, range(len(bufs)), [], len(items), lambda i, o, s, r: both(o, s, r, True),
                  lambda i, o, s, r: both(o, s, r, False))


def allcast_rider(buf):
    peers = [(k, flip) for k in range(N_CHIPS) for flip in (0, 1) if (k, flip) != (0, 0)]

    def both(outs, send, recv, mine):
        x, y, c = _place()
        res = []
        for i, (k, flip) in enumerate(peers):
            px, py = _partner(x, y, k)
            pc = 1 - c if flip else c
            slot, sel = (2 * x + y, c) if mine else (2 * px + py, pc)
            part = outs[0].at[slot, _half(buf.shape[1], sel, F32)]
            res.append(_remote(part, part, send, recv, i, (px, py, pc)))
        return res

    return _rider("everyone", [buf], [0], [], len(peers), lambda i, o, s, r: both(o, s, r, True),
                  lambda i, o, s, r: both(o, s, r, False))


def comm_call(riders, *, name):
    _, res = _call(None, riders=riders, name=name)
    return res


SLAB_ROWS = 192


def _pad_rows(a, rows=8):
    return jnp.pad(a, ((0, rows - a.shape[0]), (0, 0)))


def _pack_small(norm_grads, db_qkv, db_o, dsinks, db_sp, dln_g, dln_b, dw_sp, loss_part):
    parts = [
        jnp.concatenate(norm_grads, axis=0),
        _pad_rows(jnp.pad(db_qkv, ((0, 0), (0, 2 * D_MODEL - QKV_WIDTH))).reshape(2, D_MODEL)),
        _pad_rows(db_o),
        _pad_rows(jnp.pad(dsinks.reshape(1, N_Q_HEADS), ((0, 0), (0, D_MODEL - N_Q_HEADS)))),
        _pad_rows(db_sp.reshape(1, D_MODEL)),
        _pad_rows(jnp.concatenate([dln_g, dln_b, jnp.pad(loss_part[0:1], ((0, 0), (0, D_MODEL - LANES)))], axis=0)),
        dw_sp.reshape(SGU_CHUNK, D_MODEL),
    ]
    slab = jnp.concatenate(parts, axis=0)
    return jnp.pad(slab, ((0, SLAB_ROWS - slab.shape[0]), (0, 0))).reshape(N_CHIPS, SLAB_ROWS // N_CHIPS, D_MODEL)


def _unpack_small(slab, j):
    slab = slab.reshape(SLAB_ROWS, D_MODEL)
    norms = [slab[2 * i:2 * i + 2] for i in range(4)]
    db_qkv = slab[8:10].reshape(1, 2 * D_MODEL)[:, :QKV_WIDTH]
    db_o = slab[16:17]
    dsinks = slab[24:25, :N_Q_HEADS]
    db_sp = slab[32:33].reshape(SGU_GROUPS, SGU_CHUNK)
    width = D_MODEL // N_CHIPS
    dln_g = lax.dynamic_slice(slab[40:41], (0, j * width), (1, width))
    dln_b = lax.dynamic_slice(slab[41:42], (0, j * width), (1, width))
    dw_sp = slab[48:48 + SGU_CHUNK].reshape(SGU_GROUPS * SGU_CHUNK, SGU_CHUNK)
    return norms, db_qkv, db_o, dsinks, db_sp, dln_g, dln_b, dw_sp, slab[42, 0]


class _GradReduce:
    def __init__(self, c_arr, jc_arr, dest_shapes):
        self.c_arr, self.jc_arr, self.dest_shapes = c_arr, jc_arr, dest_shapes
        self.grad, self.sibling, self.pair, self.chips, self.dest = {}, {}, {}, {}, {}

    def exchange(self, tags):
        return exchange_rider([self.grad[t] for t in tags])

    def exchanged(self, tags, res):
        for t, r in zip(tags, res):
            self.sibling[t] = r
            self.pair[t] = pair_add(self.grad[t], r, self.c_arr, name=f"pair_add_{t}")

    def scatter(self, tags):
        return scatter_rider([self.pair[t] for t in tags])

    def scattered(self, tags, res, where):
        for t, r in zip(tags, res):
            name, lead = where[t]
            self.dest[name] = final_add(self.grad[t], self.sibling[t], r, self.jc_arr, dest_shape=self.dest_shapes[name],
                                        lead=lead, prev=self.dest.get(name), name=f"final_add_{t}")

    def broadcast(self, items):
        names = []
        for n, _ in items:
            if n not in names:
                names.append(n)
        return names, broadcast_rider([self.dest[n] for n in names], [(names.index(n), lead) for n, lead in items])

    def broadcasted(self, names, res):
        for n, r in zip(names, res):
            self.dest[n] = r


def kernel(x, norm_mix_pre, norm_mix_post, norm_ffn_pre, norm_ffn_post, attn_w_qkv, attn_b_qkv, attn_sinks, attn_w_o, attn_b_o, sgu_w_in, sgu_ln_g, sgu_ln_b, sgu_w_spatial, sgu_b_spatial, sgu_w_out, ffn_w_gate_up, ffn_w_down, loss_target, m_norm_mix_pre, m_norm_mix_post, m_norm_ffn_pre, m_norm_ffn_post, m_attn_w_qkv, m_attn_b_qkv, m_attn_sinks, m_attn_w_o, m_attn_b_o, m_sgu_w_in, m_sgu_ln_g, m_sgu_ln_b, m_sgu_w_spatial, m_sgu_b_spatial, m_sgu_w_out, m_ffn_w_gate_up, m_ffn_w_down, v_norm_mix_pre, v_norm_mix_post, v_norm_ffn_pre, v_norm_ffn_post, v_attn_w_qkv, v_attn_b_qkv, v_attn_sinks, v_attn_w_o, v_attn_b_o, v_sgu_w_in, v_sgu_ln_g, v_sgu_ln_b, v_sgu_w_spatial, v_sgu_b_spatial, v_sgu_w_out, v_ffn_w_gate_up, v_ffn_w_down):
    s = x.shape[1]
    x0 = x.reshape(s, D_MODEL)
    target = loss_target.reshape(s, D_MODEL)
    mx, my, mc = lax.axis_index("x"), lax.axis_index("y"), lax.axis_index("c")
    chip = 2 * mx + my
    chip_arr = jnp.reshape(chip, (1,)).astype(I32)
    c_arr = jnp.reshape(mc, (1,)).astype(I32)
    jc_arr = jnp.stack([chip, mc]).astype(I32)
    zero_bias = jnp.zeros((1, D_MODEL), F32)

    def gain(p, i):
        return p[i:i + 1]

    big = [attn_w_qkv, attn_w_o, sgu_w_in, sgu_w_out, ffn_w_gate_up, ffn_w_gate_up, ffn_w_down, ffn_w_down]
    layers = [0, 0, 0, 0, 0, 1, 0, 1]
    tags = ["qkv", "wo", "win", "wout", "wgu0", "wgu1", "wd0", "wd1"]
    full = {t: place_shard(w, l, chip_arr, BF16, name=f"place_{t}") for w, l, t in zip(big, layers, tags) if t != "wgu1"}
    ln_pack = _pad_rows(jnp.concatenate([sgu_ln_g, sgu_ln_b], axis=0), 16)[None]
    full["ln"] = place_shard(ln_pack, 0, chip_arr, F32, name="place_ln")

    def split(items):
        return [i if isinstance(i, str) else i[0] for i in items], [WHOLE if isinstance(i, str) else tuple(i[1:]) for i in items]

    def ici(*items):
        names, pieces = split(items)
        return gather_ici_rider([full[n] for n in names], pieces)

    def d2d(*items):
        names, pieces = split(items)
        return gather_d2d_rider([full[n] for n in names], pieces)

    def landed(items, res):
        for n, r in zip(split(items)[0], res):
            full[n] = r

    cos, sin = _rope_tables(s)
    sink_rows = jnp.broadcast_to(
        jnp.repeat(attn_sinks.reshape(N_KV_HEADS, GQA_GROUP), WINDOW, axis=1)[:, None, :], (N_KV_HEADS, 8, ROWS))
    w_sp = sgu_w_spatial.reshape(SGU_GROUPS, SGU_CHUNK, SGU_CHUNK)
    b_sp = jnp.broadcast_to(sgu_b_spatial.reshape(SGU_GROUPS, SGU_CHUNK)[:, :, None], (SGU_GROUPS, SGU_CHUNK, LANES))

    h0, (res,) = prenorm(x0, gain(norm_mix_pre, 0), name="prenorm_0", riders=[ici("qkv", "ln")])
    landed(("qkv", "ln"), res)
    full["wgu1"], (res,) = place_shard(ffn_w_gate_up, 1, chip_arr, BF16, name="place_wgu1", riders=[d2d("qkv", "ln")])
    landed(("qkv", "ln"), res)
    ln_g = full["ln"][:, 0, :].reshape(1, D_MODEL)
    ln_b = full["ln"][:, 1, :].reshape(1, D_MODEL)

    def hosted(call, stages):
        outputs, results = call([{"ici": ici, "d2d": d2d}[kind](*items) for kind, items in stages])
        for (_, items), res in zip(stages, results):
            landed(items, res)
        return outputs

    qkv = hosted(lambda r: qkv_proj(h0, full["qkv"], attn_b_qkv, cos, sin, name="qkv_proj", riders=r),
                 [("ici", ("wo", ("wgu0", 0, 4, 8)))])
    o = hosted(lambda r: attn_fwd(qkv, sink_rows, name="attn_fwd", riders=r),
               [("d2d", ("wo",)), ("ici", (("wgu0", 4, 8, 8), ("wd0", 0, 1, 2)))])
    w_o = full["wo"].reshape(Q_WIDTH, D_MODEL)
    x1, h1, m0 = hosted(lambda r: proj_residual_norm(o, w_o, x0, attn_b_o, gain(norm_mix_post, 0), gain(norm_ffn_pre, 0),
                                                     name="attn_out_norm", riders=r),
                        [("d2d", ("wgu0",)), ("ici", (("wd0", 1, 2, 2), ("win", 0, 2, 8)))])
    gu0, a0 = hosted(lambda r: ffn_up(h1, full["wgu0"], name="ffn_up_0", riders=r),
                     [("d2d", ("wd0",)), ("ici", (("win", 2, 8, 8), "wout", ("wgu1", 0, 4, 8)))])
    w_d0 = full["wd0"].reshape(D_FF, D_MODEL)
    x2, h2, f0 = hosted(lambda r: proj_residual_norm(a0, w_d0, x1, zero_bias, gain(norm_ffn_post, 0), gain(norm_mix_pre, 1),
                                                     name="ffn_down_norm_0", riders=r),
                        [("d2d", ("win", "wout")), ("ici", (("wgu1", 4, 7, 8),))])
    w_in = full["win"]
    z, y = hosted(lambda r: sgu_in_fwd(h2, w_in, ln_g, ln_b, w_sp, b_sp, name="sgu_in_fwd", riders=r),
                  [("ici", (("wgu1", 7, 8, 8), "wd1"))])
    w_out = full["wout"].reshape(D_MODEL, D_MODEL)
    x3, h3, m1 = hosted(lambda r: proj_residual_norm(y, w_out, x2, zero_bias, gain(norm_mix_post, 1), gain(norm_ffn_pre, 1),
                                                     name="sgu_out_norm", riders=r),
                        [("d2d", ("wgu1", "wd1"))])
    w_qkv, w_gu0, w_gu1 = full["qkv"], full["wgu0"], full["wgu1"]
    w_d1 = full["wd1"].reshape(D_FF, D_MODEL)
    gu1, a1, dx4, df1, dg_fpost1, loss_part = ffn_fwd_loss_rows(
        h3, w_gu1, w_d1, x3, gain(norm_ffn_post, 1), target, name="ffn_fwd_loss_rows")

    red = _GradReduce(c_arr, jc_arr, {
        "qkv": attn_w_qkv.shape[1:], "wo": attn_w_o.shape[1:], "win": sgu_w_in.shape[1:], "wout": sgu_w_out.shape[1:],
        "wgu": ffn_w_gate_up.shape, "wd": ffn_w_down.shape, "slab": (N_CHIPS, SLAB_ROWS // N_CHIPS, D_MODEL)})
    where = {"qkv": ("qkv", None), "wo": ("wo", None), "win": ("win", None), "wout": ("wout", None), "wgu0": ("wgu", 0),
             "wgu1": ("wgu", 1), "wd0": ("wd", 0), "wd1": ("wd", 1), "small": ("slab", "chip")}

    dgu1, dx3, dm1, dg_fpre1, dg_mpost1, _ = ffn_bwd_rows(
        df1, w_d1, gu1, w_gu1, dx4, x3, gain(norm_ffn_pre, 1), m1, gain(norm_mix_post, 1), name="ffn_bwd_rows_1")
    red.grad["wd1"] = mm_tn(a1, df1, shard_major=False, tm=256, tn=D_MODEL, name="dw_down_1").reshape(
        N_CHIPS, D_FF // N_CHIPS, D_MODEL)
    red.grad["wgu1"], (res,) = mm_tn(h3, dgu1, shard_major=True, tm=512, tn=FF_HALF, name="dw_gate_up_1",
                                     riders=[red.exchange(["wd1"])])
    red.exchanged(["wd1"], res)
    dy, (res,) = mm_nt(dm1, w_out, out_dtype=F32, name="dy_sgu", riders=[red.exchange(["wgu1"])])
    red.exchanged(["wgu1"], res)
    red.grad["wout"] = mm_tn(y, dm1, shard_major=False, tm=512, tn=D_MODEL, name="dw_sgu_out").reshape(
        N_CHIPS, D_MODEL // N_CHIPS, D_MODEL)
    (dz, dw_sp, db_sp, dln_g, dln_b), (res_a, res_b) = sgu_bwd(
        z, dy, ln_g, ln_b, w_sp, b_sp, name="sgu_bwd", riders=[red.scatter(["wgu1"]), red.exchange(["wout"])])
    red.scattered(["wgu1"], res_a, where)
    red.exchanged(["wout"], res_b)
    names, rider = red.broadcast([("wgu", 1)])
    red.grad["win"], (res_a, res_b) = mm_tn(h2, dz, shard_major=True, tm=D_MODEL, tn=2 * D_MODEL // N_CHIPS, name="dw_sgu_in",
                                            riders=[rider, red.scatter(["wout"])])
    red.broadcasted(names, res_a)
    red.scattered(["wout"], res_b, where)
    names, rider = red.broadcast([("wout", None)])
    (dx2, df0, dg_mpre1, dg_fpost0, _), (res_a, res_b) = dh_norm_bwd_pair(
        dz, w_in, dx3, x2, gain(norm_mix_pre, 1), f0, gain(norm_ffn_post, 0), name="dh_sgu_norm",
        riders=[red.exchange(["win"]), rider])
    red.exchanged(["win"], res_a)
    red.broadcasted(names, res_b)
    (dgu0, dx1, dm0, dg_fpre0, dg_mpost0, db_o), (res,) = ffn_bwd_rows(
        df0, w_d0, gu0, w_gu0, dx2, x1, gain(norm_ffn_pre, 0), m0, gain(norm_mix_post, 0), name="ffn_bwd_rows_0",
        riders=[red.scatter(["wd1", "win"])])
    red.scattered(["wd1", "win"], res, where)
    names, rider = red.broadcast([("wd", 1), ("win", None)])
    dw_d0, (res,) = mm_tn(a0, df0, shard_major=False, tm=256, tn=D_MODEL, name="dw_down_0", riders=[rider])
    red.broadcasted(names, res)
    red.grad["wd0"] = dw_d0.reshape(N_CHIPS, D_FF // N_CHIPS, D_MODEL)
    red.grad["wgu0"], (res,) = mm_tn(h1, dgu0, shard_major=True, tm=512, tn=FF_HALF, name="dw_gate_up_0",
                                     riders=[red.exchange(["wd0"])])
    red.exchanged(["wd0"], res)
    do, (res,) = mm_nt(dm0, w_o, out_dtype=BF16, name="do_attn", riders=[red.exchange(["wgu0"])])
    red.exchanged(["wgu0"], res)
    red.grad["wo"] = mm_tn(o, dm0, shard_major=False, tm=512, tn=D_MODEL, name="dw_attn_out").reshape(
        N_CHIPS, Q_WIDTH // N_CHIPS, D_MODEL)
    (dq, dkc, dkp, dvc, dvp, dsink), (res_a, res_b) = attn_bwd(
        qkv, sink_rows, do, name="attn_bwd", riders=[red.scatter(["wgu0"]), red.exchange(["wo"])])
    red.scattered(["wgu0"], res_a, where)
    red.exchanged(["wo"], res_b)
    names, rider = red.broadcast([("wgu", 0)])
    (dqkv, db_qkv), (res_a, res_b) = rope_bwd(dq, dkc, dkp, dvc, dvp, cos, sin, name="rope_bwd",
                                              riders=[rider, red.scatter(["wd0"])])
    red.broadcasted(names, res_a)
    red.scattered(["wd0"], res_b, where)
    names, rider = red.broadcast([("wd", 0)])
    red.grad["qkv"], (res_a, res_b) = mm_tn(h0, dqkv, shard_major=True, tm=D_MODEL, tn=QKV_WIDTH // N_CHIPS, name="dw_qkv",
                                            riders=[rider, red.scatter(["wo"])])
    red.broadcasted(names, res_a)
    red.scattered(["wo"], res_b, where)
    names, rider = red.broadcast([("wo", None)])
    dh0, (res_a, res_b) = mm_nt(dqkv, w_qkv, out_dtype=F32, tm=1024, name="dh_attn", riders=[rider, red.exchange(["qkv"])])
    red.broadcasted(names, res_a)
    red.exchanged(["qkv"], res_b)
    grad_x, dg_mpre0 = norm_bwd_last(dx1, dh0, x0, gain(norm_mix_pre, 0), name="norm_bwd_in")

    norm_grads = [jnp.concatenate(p, axis=0) for p in
                  ((dg_mpre0, dg_mpre1), (dg_mpost0, dg_mpost1), (dg_fpre0, dg_fpre1), (dg_fpost0, dg_fpost1))]
    red.grad["small"] = _pack_small(norm_grads, db_qkv, db_o, dsink[:, :, 0, 0], db_sp[:, :, 0], dln_g, dln_b, dw_sp,
                                    loss_part)
    res_a, res_b = comm_call([red.scatter(["qkv"]), red.exchange(["small"])], name="tail_1")
    red.scattered(["qkv"], res_a, where)
    red.exchanged(["small"], res_b)
    names, rider = red.broadcast([("qkv", None)])
    res_a, res_b = comm_call([red.scatter(["small"]), rider], name="tail_2")
    red.scattered(["small"], res_a, where)
    red.broadcasted(names, res_b)
    ((slab_full,),) = comm_call([allcast_rider(red.dest["slab"])], name="tail_3")
    g_qkv, g_wo, g_win, g_wout, g_wgu, g_wd = (red.dest[n] for n in ("qkv", "wo", "win", "wout", "wgu", "wd"))
    g_norms, g_bqkv, g_bo, g_sinks, g_bsp, g_lng, g_lnb, g_wsp, loss = _unpack_small(slab_full, chip)

    def big_update(w, g, m, v, tag):
        return adamw(w, g.reshape(w.shape), m, v, name=f"adamw_{tag}")

    upd = {
        "attn_w_qkv": big_update(attn_w_qkv, g_qkv, m_attn_w_qkv, v_attn_w_qkv, "qkv"),
        "attn_w_o": big_update(attn_w_o, g_wo, m_attn_w_o, v_attn_w_o, "wo"),
        "sgu_w_in": big_update(sgu_w_in, g_win, m_sgu_w_in, v_sgu_w_in, "win"),
        "sgu_w_out": big_update(sgu_w_out, g_wout, m_sgu_w_out, v_sgu_w_out, "wout"),
        "ffn_w_gate_up": big_update(ffn_w_gate_up, g_wgu, m_ffn_w_gate_up, v_ffn_w_gate_up, "wgu"),
        "ffn_w_down": big_update(ffn_w_down, g_wd, m_ffn_w_down, v_ffn_w_down, "wd"),
    }
    small_names = ["norm_mix_pre", "norm_mix_post", "norm_ffn_pre", "norm_ffn_post", "attn_b_qkv", "attn_sinks", "attn_b_o",
                   "sgu_ln_g", "sgu_ln_b", "sgu_w_spatial", "sgu_b_spatial"]
    small_w = [norm_mix_pre, norm_mix_post, norm_ffn_pre, norm_ffn_post, attn_b_qkv, attn_sinks, attn_b_o, sgu_ln_g, sgu_ln_b,
               sgu_w_spatial, sgu_b_spatial]
    small_m = [m_norm_mix_pre, m_norm_mix_post, m_norm_ffn_pre, m_norm_ffn_post, m_attn_b_qkv, m_attn_sinks, m_attn_b_o,
               m_sgu_ln_g, m_sgu_ln_b, m_sgu_w_spatial, m_sgu_b_spatial]
    small_v = [v_norm_mix_pre, v_norm_mix_post, v_norm_ffn_pre, v_norm_ffn_post, v_attn_b_qkv, v_attn_sinks, v_attn_b_o,
               v_sgu_ln_g, v_sgu_ln_b, v_sgu_w_spatial, v_sgu_b_spatial]
    small_g = g_norms + [g_bqkv, g_sinks, g_bo, g_lng, g_lnb, g_wsp, g_bsp]

    def flat2(a):
        return a.reshape(-1, a.shape[-1])

    res = adamw_small([flat2(a) for a in small_w], [flat2(a) for a in small_g], [flat2(a) for a in small_m],
                      [flat2(a) for a in small_v], name="adamw_small")
    for i, nm in enumerate(small_names):
        upd[nm] = tuple(r[i].reshape(small_w[i].shape) for r in res)

    order = ["norm_mix_pre", "norm_mix_post", "norm_ffn_pre", "norm_ffn_post", "attn_w_qkv", "attn_b_qkv", "attn_sinks",
             "attn_w_o", "attn_b_o", "sgu_w_in", "sgu_ln_g", "sgu_ln_b", "sgu_w_spatial", "sgu_b_spatial", "sgu_w_out",
             "ffn_w_gate_up", "ffn_w_down"]
    outs = [loss, grad_x.reshape(1, s, D_MODEL)]
    for part in range(4):
        outs += [upd[nm][part] for nm in order]
    return tuple(outs)
```

```python
import types

import numpy as np
import jax
import jax.numpy as jnp
from jax import lax
from jax.experimental import pallas as pl
from jax.experimental.pallas import tpu as pltpu

F32 = jnp.float32
BF16 = jnp.bfloat16
I32 = jnp.int32

D_MODEL = 1024
HEAD_DIM = 64
N_Q_HEADS = 16
N_KV_HEADS = 4
GQA_GROUP = 4
WINDOW = 128
Q_WIDTH = 1024
KV_WIDTH = 256
QKV_WIDTH = 1536
ROPE_THETA = 10000.0
SGU_GROUPS = 8
SGU_CHUNK = 128
D_FF = 2816
FF_HALF = D_FF // 2
EPS = 1e-6
N_CHIPS = 4
LANES = 128

ADAM_LR = 0.001
ADAM_B1 = 0.9
ADAM_B2 = 0.999
ADAM_EPS = 1e-08
ADAM_WD = 0.01
ADAM_STEP = 10

VMEM_LIMIT = 52 * 1024 * 1024
MESH = pl.DeviceIdType.MESH
NEG = -1e30
NT_DIMS = (((1,), (1,)), ((), ()))
TN_DIMS = (((0,), (0,)), ((), ()))
NN_DIMS = (((1,), (0,)), ((), ()))
ANY = pl.BlockSpec(memory_space=pl.ANY)


def _row_tile(s, want):
    return want if s % want == 0 else s


PEER_KINDS = ("sibling", "chips", "sibling+chips", "everyone")


def _peer_kind(riders):
    kinds = {r.peers for r in riders}
    if not kinds:
        return None
    if "everyone" in kinds:
        return "everyone"
    return "sibling+chips" if len(kinds) == 2 else kinds.pop()


def _peer_barrier(kind):
    x, y, c = _place()
    chips = [(*_partner(x, y, k), c) for k in (1, 2, 3)]
    peers = {"sibling": [(x, y, 1 - c)], "chips": chips, "sibling+chips": [(x, y, 1 - c)] + chips,
             "everyone": [(x, y, 1 - c)] + chips + [(px, py, 1 - c) for px, py, _ in chips]}[kind]
    barrier = pltpu.get_barrier_semaphore()
    for dev in peers:
        pl.semaphore_signal(barrier, inc=1, device_id=dev, device_id_type=MESH)
    pl.semaphore_wait(barrier, len(peers))


def _call(body, *, name, grid=(), in_specs=(), out_specs=(), out_shape=(), scratch_shapes=(), operands=(), prefetch=(),
          aliases=None, riders=(), sem=None):
    n_pre, n_in, n_out, n_scr = len(prefetch), len(operands), len(out_shape), len(scratch_shapes)
    in_specs, out_specs, out_shape = list(in_specs), list(out_specs), list(out_shape)
    operands, scratch_shapes = list(operands), list(scratch_shapes)
    io_alias = {n_pre + i: o for i, o in (aliases or {}).items()}
    for r in riders:
        base_in, base_out = n_pre + len(operands), len(out_shape)
        operands += list(r.inputs)
        in_specs += [ANY] * len(r.inputs)
        for pos, i in enumerate(r.aliased):
            io_alias[base_in + i] = base_out + pos
            out_shape.append(jax.ShapeDtypeStruct(r.inputs[i].shape, r.inputs[i].dtype))
        out_shape += list(r.fresh)
        out_specs += [ANY] * (len(r.aliased) + len(r.fresh))
        scratch_shapes += [pltpu.SemaphoreType.DMA((r.nsem,)), pltpu.SemaphoreType.DMA((r.nsem,))]

    def wrapped(*refs):
        pre, p = refs[:n_pre], n_pre
        core_in, p = refs[p:p + n_in], p + n_in
        r_in = []
        for r in riders:
            r_in.append(refs[p:p + len(r.inputs)])
            p += len(r.inputs)
        core_out, p = refs[p:p + n_out], p + n_out
        r_out = []
        for r in riders:
            k = len(r.aliased) + len(r.fresh)
            r_out.append(refs[p:p + k])
            p += k
        core_scr, p = refs[p:p + n_scr], p + n_scr
        r_sem = [refs[p + 2 * i:p + 2 * i + 2] for i in range(len(riders))]

        def edge(at_last, fns):
            def run():
                if not at_last:
                    _peer_barrier(peer_kind)
                for i, r in enumerate(riders):
                    getattr(r, fns)(r_in[i], r_out[i], r_sem[i][0], r_sem[i][1])
            if not riders:
                return
            if not grid:
                run()
                return
            cond = None
            for d, n in enumerate(grid):
                c = pl.program_id(d) == (n - 1 if at_last else 0)
                cond = c if cond is None else jnp.logical_and(cond, c)
            pl.when(cond)(run)

        edge(False, "start")
        if body is not None:
            body(*pre, *core_in, *core_out, *core_scr)
        edge(True, "finish")

    if sem is None or riders:
        sem = ("arbitrary",) * len(grid)
    kwargs = dict(out_shape=out_shape, input_output_aliases=io_alias, name=name)
    peer_kind = _peer_kind(riders)
    collective = {} if peer_kind is None else {"collective_id": PEER_KINDS.index(peer_kind)}
    if grid:
        kwargs["compiler_params"] = pltpu.CompilerParams(dimension_semantics=sem, vmem_limit_bytes=VMEM_LIMIT, **collective)
    elif collective:
        kwargs["compiler_params"] = pltpu.CompilerParams(**collective)
    if n_pre:
        kwargs["grid_spec"] = pltpu.PrefetchScalarGridSpec(
            num_scalar_prefetch=n_pre, grid=grid, in_specs=in_specs, out_specs=out_specs, scratch_shapes=scratch_shapes)
    else:
        kwargs.update(grid=grid, in_specs=in_specs, out_specs=out_specs, scratch_shapes=scratch_shapes)
    res = pl.pallas_call(wrapped, **kwargs)(*prefetch, *operands)
    core, rest, rider_res = list(res[:n_out]), list(res[n_out:]), []
    for r in riders:
        k = len(r.aliased) + len(r.fresh)
        rider_res.append(rest[:k])
        rest = rest[k:]
    return core, rider_res


def _mm_call(*, grid, in_specs, out_spec, out_shape, dims, nk, kaxis, acc_shape, name, operands, riders=()):
    out_dtype = out_shape.dtype

    def body(a_ref, b_ref, o_ref, *scratch):
        p = lax.dot_general(a_ref[...].astype(BF16), b_ref[...].astype(BF16), dims, preferred_element_type=F32)
        if nk == 1:
            o_ref[...] = p.astype(out_dtype)
        else:
            acc = scratch[0]
            kk = pl.program_id(kaxis)

            @pl.when(kk == 0)
            def _():
                acc[...] = p

            @pl.when(kk > 0)
            def _():
                acc[...] += p

            @pl.when(kk == nk - 1)
            def _():
                o_ref[...] = acc[...].astype(out_dtype)

    sem = ["parallel"] * len(grid)
    if nk > 1:
        sem[kaxis] = "arbitrary"
    (out,), rider_res = _call(
        body, grid=grid, in_specs=in_specs, out_specs=[out_spec], out_shape=[out_shape],
        scratch_shapes=[pltpu.VMEM(acc_shape, F32)] if nk > 1 else [], operands=operands, name=name, riders=riders,
        sem=tuple(sem))
    return (out, rider_res) if riders else out


def mm_nn(a, w, *, out_dtype, name, tm=512, tn=512, riders=()):
    m, k = a.shape
    tm = _row_tile(m, tm)
    if w.ndim == 3:
        ns = w.shape[2]
        grid = (N_CHIPS, m // tm)
        w_spec = pl.BlockSpec((None, k, ns), lambda j, i: (j, 0, 0))
        o_spec = pl.BlockSpec((tm, ns), lambda j, i: (i, j))
        n = N_CHIPS * ns
    else:
        n = w.shape[1]
        grid = (n // tn, m // tm)
        w_spec = pl.BlockSpec((k, tn), lambda j, i: (0, j))
        o_spec = pl.BlockSpec((tm, tn), lambda j, i: (i, j))
    return _mm_call(grid=grid, in_specs=[pl.BlockSpec((tm, k), lambda j, i: (i, 0)), w_spec], out_spec=o_spec,
                    out_shape=jax.ShapeDtypeStruct((m, n), out_dtype), dims=NN_DIMS, nk=1, kaxis=0, acc_shape=None,
                    name=name, operands=(a, w), riders=riders)


def mm_nt(a, w, *, out_dtype, name, tm=512, tn=512, riders=()):
    if w.ndim == 2:
        m, n = a.shape
        kout = w.shape[0]
        tm = _row_tile(m, tm)
        return _mm_call(grid=(kout // tn, m // tm),
                        in_specs=[pl.BlockSpec((tm, n), lambda j, i: (i, 0)), pl.BlockSpec((tn, n), lambda j, i: (j, 0))],
                        out_spec=pl.BlockSpec((tm, tn), lambda j, i: (i, j)),
                        out_shape=jax.ShapeDtypeStruct((m, kout), out_dtype), dims=NT_DIMS, nk=1, kaxis=0,
                        acc_shape=None, name=name, operands=(a, w), riders=riders)
    _, kout, ns = w.shape
    planes = a.ndim == 3
    m = a.shape[1] if planes else a.shape[0]
    tm = _row_tile(m, tm)
    a_spec = pl.BlockSpec((2, tm, 2 * ns), lambda i: (0, i, 0)) if planes else pl.BlockSpec((tm, N_CHIPS * ns), lambda i: (i, 0))

    def body(a_ref, w0, w1, w2, w3, o_ref):
        acc = None
        for j, w_ref in enumerate((w0, w1, w2, w3)):
            if planes:
                a_j = a_ref[j // 2, :, (j % 2) * ns:(j % 2 + 1) * ns]
            else:
                a_j = a_ref[:, j * ns:(j + 1) * ns]
            p = lax.dot_general(a_j, w_ref[...], NT_DIMS, preferred_element_type=F32)
            acc = p if acc is None else acc + p
        o_ref[...] = acc.astype(out_dtype)

    def shard(j):
        return pl.BlockSpec((None, kout, ns), lambda i: (j, 0, 0))

    (out,), rider_res = _call(
        body, grid=(m // tm,), in_specs=[a_spec] + [shard(j) for j in range(N_CHIPS)],
        out_specs=[pl.BlockSpec((tm, kout), lambda i: (i, 0))], out_shape=[jax.ShapeDtypeStruct((m, kout), out_dtype)],
        operands=(a, w, w, w, w), sem=("parallel",), name=name, riders=riders)
    return (out, rider_res) if riders else out


def mm_tn(a, b, *, shard_major, name, tm, tn, tk=None, out_dtype=BF16, riders=()):
    s, m = a.shape
    tk = s if tk is None else _row_tile(s, tk)
    if b.ndim == 3:
        n = 2 * b.shape[2]
        b_spec = pl.BlockSpec((None, tk, tn), lambda j, i, kk: (j // 2, kk, j % 2))
    else:
        n = b.shape[1]
        b_spec = pl.BlockSpec((tk, tn), lambda j, i, kk: (kk, j))
    if shard_major:
        assert tn == n // N_CHIPS
        o_spec = pl.BlockSpec((None, tm, tn), lambda j, i, kk: (j, i, 0))
        o_shape = jax.ShapeDtypeStruct((N_CHIPS, m, tn), out_dtype)
    else:
        o_spec = pl.BlockSpec((tm, tn), lambda j, i, kk: (i, j))
        o_shape = jax.ShapeDtypeStruct((m, n), out_dtype)
    return _mm_call(grid=(n // tn, m // tm, s // tk),
                    in_specs=[pl.BlockSpec((tk, tm), lambda j, i, kk: (kk, i)), b_spec], out_spec=o_spec,
                    out_shape=o_shape, dims=TN_DIMS, nk=s // tk, kaxis=2, acc_shape=(tm, tn), name=name, operands=(a, b),
                    riders=riders)


def _rstd(x):
    return lax.rsqrt(jnp.mean(x * x, axis=-1, keepdims=True) + EPS)


def _rms_bwd(dy, x, g):
    r = _rstd(x)
    xhat = x * r
    gy = dy * g
    dx = r * (gy - xhat * jnp.mean(gy * xhat, axis=-1, keepdims=True))
    return dx, jnp.sum(dy * xhat, axis=0, keepdims=True)


def _accum(ref, val, first):
    @pl.when(first)
    def _():
        ref[...] = val

    @pl.when(jnp.logical_not(first))
    def _():
        ref[...] += val


def _row_spec(tm, width):
    return pl.BlockSpec((tm, width), lambda i: (i, 0))


def _vec_spec(width):
    return pl.BlockSpec((1, width), lambda i: (0, 0))


def _ret(core, rider_res, riders):
    core = core[0] if len(core) == 1 else core
    return (core, rider_res) if riders else core


def prenorm(x, g, *, name, tm=256, riders=()):
    s = x.shape[0]
    tm = _row_tile(s, tm)

    def body(x_ref, g_ref, h_ref):
        xv = x_ref[...]
        h_ref[...] = (xv * _rstd(xv) * g_ref[...]).astype(BF16)

    core, rr = _call(
        body, grid=(s // tm,), in_specs=[_row_spec(tm, D_MODEL), _vec_spec(D_MODEL)], out_specs=[_row_spec(tm, D_MODEL)],
        out_shape=[jax.ShapeDtypeStruct((s, D_MODEL), BF16)], operands=(x, g), sem=("parallel",), name=name, riders=riders)
    return _ret(core, rr, riders)


def proj_residual_norm(a, w, x, bias, g_post, g_next, *, name, tm=256, riders=()):
    s, k = a.shape
    tm = _row_tile(s, tm)

    def body(a_ref, w_ref, x_ref, b_ref, gp_ref, gn_ref, xo_ref, h_ref, m_ref):
        mv = jnp.dot(a_ref[...], w_ref[...], preferred_element_type=F32) + b_ref[...]
        m_ref[...] = mv.astype(BF16)
        xn = x_ref[...] + mv * _rstd(mv) * gp_ref[...]
        xo_ref[...] = xn
        h_ref[...] = (xn * _rstd(xn) * gn_ref[...]).astype(BF16)

    row, vec = _row_spec(tm, D_MODEL), _vec_spec(D_MODEL)
    core, rr = _call(
        body, grid=(s // tm,),
        in_specs=[_row_spec(tm, k), pl.BlockSpec((k, D_MODEL), lambda i: (0, 0)), row, vec, vec, vec], out_specs=[row, row, row],
        out_shape=[jax.ShapeDtypeStruct((s, D_MODEL), F32), jax.ShapeDtypeStruct((s, D_MODEL), BF16),
                   jax.ShapeDtypeStruct((s, D_MODEL), BF16)],
        operands=(a, w, x, bias, g_post, g_next), sem=("parallel",), name=name, riders=riders)
    return _ret(core, rr, riders)


def proj_loss_head(a, w, x, g_post, target, *, name, tm=256, riders=()):
    s, k = a.shape
    tm = _row_tile(s, tm)

    def body(a_ref, w_ref, x_ref, g_ref, t_ref, dx_ref, df_ref, dg_ref, loss_ref):
        first = pl.program_id(0) == 0
        fv = jnp.dot(a_ref[...], w_ref[...], preferred_element_type=F32)
        g = g_ref[...]
        err = x_ref[...] + fv * _rstd(fv) * g - t_ref[...]
        dx = err * (1.0 / D_MODEL)
        dx_ref[...] = dx
        df, dg = _rms_bwd(dx, fv, g)
        df_ref[...] = df.astype(BF16)
        _accum(dg_ref, dg, first)
        part = jnp.sum(jnp.sum(err * err, axis=-1, keepdims=True), axis=0, keepdims=True) * (0.5 / D_MODEL)
        _accum(loss_ref, jnp.broadcast_to(part, (8, LANES)), first)

    row, vec = _row_spec(tm, D_MODEL), _vec_spec(D_MODEL)
    core, rr = _call(
        body, grid=(s // tm,), in_specs=[_row_spec(tm, k), pl.BlockSpec((k, D_MODEL), lambda i: (0, 0)), row, vec, row],
        out_specs=[row, row, vec, pl.BlockSpec((8, LANES), lambda i: (0, 0))],
        out_shape=[jax.ShapeDtypeStruct((s, D_MODEL), F32), jax.ShapeDtypeStruct((s, D_MODEL), BF16),
                   jax.ShapeDtypeStruct((1, D_MODEL), F32), jax.ShapeDtypeStruct((8, LANES), F32)],
        operands=(a, w, x, g_post, target), name=name, riders=riders)
    return _ret(core, rr, riders)


def ffn_fwd_loss_rows(h, w_gu, w_d, x, g_post, target, *, name, tm=256, riders=()):
    s = x.shape[0]
    tm = _row_tile(s, tm)

    def body(h_ref, w0, w1, w2, w3, wd_ref, x_ref, g_ref, t_ref, d_ref, a_ref, dx_ref, df_ref, dg_ref, loss_ref):
        first = pl.program_id(0) == 0
        hv = h_ref[...]
        fv = None
        for half, (wg_ref, wu_ref) in enumerate(((w0, w2), (w1, w3))):
            cols = slice(half * FF_HALF, (half + 1) * FF_HALF)
            g = jnp.dot(hv, wg_ref[...], preferred_element_type=F32)
            u = jnp.dot(hv, wu_ref[...], preferred_element_type=F32)
            sig = _sigmoid(g)
            silu = g * sig
            d_ref[0, :, cols] = (u * (sig + silu * (1.0 - sig))).astype(BF16)
            d_ref[1, :, cols] = silu.astype(BF16)
            act = (silu * u).astype(BF16)
            a_ref[:, cols] = act
            p = jnp.dot(act, wd_ref[cols, :], preferred_element_type=F32)
            fv = p if fv is None else fv + p
        gain = g_ref[...]
        err = x_ref[...] + fv * _rstd(fv) * gain - t_ref[...]
        dx = err * (1.0 / D_MODEL)
        dx_ref[...] = dx
        df, dg = _rms_bwd(dx, fv, gain)
        df_ref[...] = df.astype(BF16)
        _accum(dg_ref, dg, first)
        part = jnp.sum(jnp.sum(err * err, axis=-1, keepdims=True), axis=0, keepdims=True) * (0.5 / D_MODEL)
        _accum(loss_ref, jnp.broadcast_to(part, (8, LANES)), first)

    def resident(shape, index):
        return pl.BlockSpec(shape, index, pipeline_mode=pl.Buffered(1))

    row, vec = _row_spec(tm, D_MODEL), _vec_spec(D_MODEL)
    shards = [resident((None, D_MODEL, FF_HALF), (lambda j: (lambda i: (j, 0, 0)))(j)) for j in range(N_CHIPS)]
    core, rr = _call(
        body, grid=(s // tm,),
        in_specs=[row] + shards + [resident((D_FF, D_MODEL), lambda i: (0, 0)), row, vec, row],
        out_specs=[pl.BlockSpec((2, tm, D_FF), lambda i: (0, i, 0)), _row_spec(tm, D_FF), row, row, vec,
                   pl.BlockSpec((8, LANES), lambda i: (0, 0))],
        out_shape=[jax.ShapeDtypeStruct((2, s, D_FF), BF16), jax.ShapeDtypeStruct((s, D_FF), BF16),
                   jax.ShapeDtypeStruct((s, D_MODEL), F32), jax.ShapeDtypeStruct((s, D_MODEL), BF16),
                   jax.ShapeDtypeStruct((1, D_MODEL), F32), jax.ShapeDtypeStruct((8, LANES), F32)],
        operands=(h, w_gu, w_gu, w_gu, w_gu, w_d, x, g_post, target), name=name, riders=riders)
    return _ret(core, rr, riders)


def dh_norm_bwd_pair(a, w, dres, x, g_pre, m, g_post, *, name, tm=512, sub=256, riders=()):
    _, kout, ns = w.shape
    planes = a.ndim == 3
    s = x.shape[0]
    tm = _row_tile(s, tm)
    sub = min(sub, tm)
    a_spec = pl.BlockSpec((2, tm, 2 * ns), lambda i: (0, i, 0)) if planes else pl.BlockSpec((tm, N_CHIPS * ns), lambda i: (i, 0))

    def body(a_ref, w0, w1, w2, w3, dres_ref, x_ref, gpre_ref, m_ref, gpost_ref, dx_ref, dm_ref, dgpre_ref, dgpost_ref, db_ref):
        first = pl.program_id(0) == 0
        sums = None
        for t in range(tm // sub):
            rows = slice(t * sub, (t + 1) * sub)
            dh = None
            for j, w_ref in enumerate((w0, w1, w2, w3)):
                a_j = a_ref[j // 2, rows, (j % 2) * ns:(j % 2 + 1) * ns] if planes else a_ref[rows, j * ns:(j + 1) * ns]
                p = lax.dot_general(a_j, w_ref[...], NT_DIMS, preferred_element_type=F32)
                dh = p if dh is None else dh + p
            d1, dgpre = _rms_bwd(dh, x_ref[rows, :], gpre_ref[...])
            dx = dres_ref[rows, :] + d1
            dx_ref[rows, :] = dx
            dm, dgpost = _rms_bwd(dx, m_ref[rows, :].astype(F32), gpost_ref[...])
            dm_ref[rows, :] = dm.astype(BF16)
            part = (dgpre, dgpost, jnp.sum(dm, axis=0, keepdims=True))
            sums = part if sums is None else tuple(u + v for u, v in zip(sums, part))
        _accum(dgpre_ref, sums[0], first)
        _accum(dgpost_ref, sums[1], first)
        _accum(db_ref, sums[2], first)

    def shard(j):
        return pl.BlockSpec((None, kout, ns), lambda i: (j, 0, 0))

    row, vec = _row_spec(tm, D_MODEL), _vec_spec(D_MODEL)
    vshape = jax.ShapeDtypeStruct((1, D_MODEL), F32)
    core, rr = _call(
        body, grid=(s // tm,), in_specs=[a_spec] + [shard(j) for j in range(N_CHIPS)] + [row, row, vec, row, vec],
        out_specs=[row, row, vec, vec, vec],
        out_shape=[jax.ShapeDtypeStruct((s, D_MODEL), F32), jax.ShapeDtypeStruct((s, D_MODEL), BF16), vshape, vshape, vshape],
        operands=(a, w, w, w, w, dres, x, g_pre, m, g_post), name=name, riders=riders)
    return _ret(core, rr, riders)


def ffn_bwd_rows(df, w_d, d_planes, w_gu, dres, x, g_pre, m, g_post, *, name, tm=256, riders=()):
    s = x.shape[0]
    tm = _row_tile(s, tm)

    def body(df_ref, wd_ref, d_ref, w0, w1, w2, w3, dres_ref, x_ref, gpre_ref, m_ref, gpost_ref,
             o_ref, dx_ref, dm_ref, dgpre_ref, dgpost_ref, db_ref):
        first = pl.program_id(0) == 0
        dfv = df_ref[...]
        dh = None
        for half, (wg_ref, wu_ref) in enumerate(((w0, w2), (w1, w3))):
            cols = slice(half * FF_HALF, (half + 1) * FF_HALF)
            da = lax.dot_general(dfv, wd_ref[cols, :], NT_DIMS, preferred_element_type=F32)
            dg = (da * d_ref[0, :, cols].astype(F32)).astype(BF16)
            du = (da * d_ref[1, :, cols].astype(F32)).astype(BF16)
            o_ref[0, :, cols] = dg
            o_ref[1, :, cols] = du
            p = lax.dot_general(dg, wg_ref[...], NT_DIMS, preferred_element_type=F32)
            p += lax.dot_general(du, wu_ref[...], NT_DIMS, preferred_element_type=F32)
            dh = p if dh is None else dh + p
        d1, dgpre = _rms_bwd(dh, x_ref[...], gpre_ref[...])
        dx = dres_ref[...] + d1
        dx_ref[...] = dx
        dm, dgpost = _rms_bwd(dx, m_ref[...].astype(F32), gpost_ref[...])
        dm_ref[...] = dm.astype(BF16)
        _accum(dgpre_ref, dgpre, first)
        _accum(dgpost_ref, dgpost, first)
        _accum(db_ref, jnp.sum(dm, axis=0, keepdims=True), first)

    def resident(shape, index):
        return pl.BlockSpec(shape, index, pipeline_mode=pl.Buffered(1))

    planes = pl.BlockSpec((2, tm, D_FF), lambda i: (0, i, 0))
    row, vec = _row_spec(tm, D_MODEL), _vec_spec(D_MODEL)
    vshape = jax.ShapeDtypeStruct((1, D_MODEL), F32)
    shards = [resident((None, D_MODEL, FF_HALF), (lambda j: (lambda i: (j, 0, 0)))(j)) for j in range(N_CHIPS)]
    core, rr = _call(
        body, grid=(s // tm,),
        in_specs=[row, resident((D_FF, D_MODEL), lambda i: (0, 0)), planes] + shards + [row, row, vec, row, vec],
        out_specs=[planes, row, row, vec, vec, vec],
        out_shape=[jax.ShapeDtypeStruct((2, s, D_FF), BF16), jax.ShapeDtypeStruct((s, D_MODEL), F32),
                   jax.ShapeDtypeStruct((s, D_MODEL), BF16), vshape, vshape, vshape],
        operands=(df, w_d, d_planes, w_gu, w_gu, w_gu, w_gu, dres, x, g_pre, m, g_post), name=name, riders=riders)
    return _ret(core, rr, riders)


def norm_bwd_last(dres, dh, x, g_pre, *, name, tm=256, riders=()):
    s = x.shape[0]
    tm = _row_tile(s, tm)

    def body(dres_ref, dh_ref, x_ref, g_ref, dx_ref, dg_ref):
        d1, dg = _rms_bwd(dh_ref[...], x_ref[...], g_ref[...])
        dx_ref[...] = dres_ref[...] + d1
        _accum(dg_ref, dg, pl.program_id(0) == 0)

    row, vec = _row_spec(tm, D_MODEL), _vec_spec(D_MODEL)
    core, rr = _call(
        body, grid=(s // tm,), in_specs=[row, row, row, vec], out_specs=[row, vec],
        out_shape=[jax.ShapeDtypeStruct((s, D_MODEL), F32), jax.ShapeDtypeStruct((1, D_MODEL), F32)],
        operands=(dres, dh, x, g_pre), name=name, riders=riders)
    return _ret(core, rr, riders)


def _rope_tables(s):
    half = HEAD_DIM // 2
    inv_freq = np.float32(ROPE_THETA) ** (-(np.arange(half, dtype=np.float32) * np.float32(2.0)) / np.float32(HEAD_DIM))
    ang = np.arange(s, dtype=np.float32)[:, None] * inv_freq[None, :]
    cos, sin = np.cos(ang).astype(np.float32), np.sin(ang).astype(np.float32)
    return jnp.asarray(np.tile(cos, (1, 4))), jnp.asarray(np.concatenate([-sin, sin, -sin, sin], axis=1))


def _swap_halves(x):
    lane = lax.broadcasted_iota(I32, x.shape, 1)
    return jnp.where((lane & (HEAD_DIM - 1)) < HEAD_DIM // 2, pltpu.roll(x, LANES - 32, 1), pltpu.roll(x, 32, 1))


N_ROPE_BLOCKS = (Q_WIDTH + KV_WIDTH) // LANES


def qkv_proj(h, w, bias, cos, sin, *, name, tm=512, riders=()):
    s, k = h.shape
    ns = w.shape[2]
    tm = _row_tile(s, tm)

    def body(h_ref, w_ref, b_ref, c_ref, s_ref, o_ref):
        j = pl.program_id(0)
        sub = min(256, tm)
        for t in range(tm // sub):
            rows = slice(t * sub, (t + 1) * sub)
            p = jnp.dot(h_ref[rows, :], w_ref[...], preferred_element_type=F32) + b_ref[...]
            cosv, sinv = c_ref[rows, :], s_ref[rows, :]
            for blk in range(ns // LANES):
                xb = p[:, blk * LANES:(blk + 1) * LANES]
                roped = xb * cosv + _swap_halves(xb) * sinv
                is_qk = j * (ns // LANES) + blk < N_ROPE_BLOCKS
                o_ref[rows, blk * LANES:(blk + 1) * LANES] = jnp.where(is_qk, roped, xb).astype(BF16)

    core, rr = _call(
        body, grid=(N_CHIPS, s // tm),
        in_specs=[pl.BlockSpec((tm, k), lambda j, i: (i, 0)), pl.BlockSpec((None, k, ns), lambda j, i: (j, 0, 0)),
                  pl.BlockSpec((1, ns), lambda j, i: (0, j)), pl.BlockSpec((tm, LANES), lambda j, i: (i, 0)),
                  pl.BlockSpec((tm, LANES), lambda j, i: (i, 0))],
        out_specs=[pl.BlockSpec((tm, ns), lambda j, i: (i, j))], out_shape=[jax.ShapeDtypeStruct((s, N_CHIPS * ns), BF16)],
        operands=(h, w, bias, cos, sin), sem=("parallel", "parallel"), name=name, riders=riders)
    return _ret(core, rr, riders)


def rope_bwd(dq, dkc, dkp, dvc, dvp, cos, sin, *, name, riders=()):
    s = dq.shape[0]
    tm = 2 * WINDOW if s % (2 * WINDOW) == 0 else WINDOW
    nb = s // tm

    def body(dq_ref, dkc_ref, dkp_ref, dkp_next_ref, dvc_ref, dvp_ref, dvp_next_ref, c_ref, s_ref, o_ref, db_ref):
        i = pl.program_id(0)
        has_next = (i < nb - 1).astype(F32)
        cosv, sinv = c_ref[...], s_ref[...]

        def shifted(ref, next_ref, cols):
            last = has_next * next_ref[:WINDOW, cols].astype(F32)
            return last if tm == WINDOW else jnp.concatenate([ref[WINDOW:, cols].astype(F32), last], axis=0)

        parts = []
        for blk in range(QKV_WIDTH // LANES):
            if blk < Q_WIDTH // LANES:
                g = dq_ref[:, blk * LANES:(blk + 1) * LANES].astype(F32)
            else:
                own, prv, nxt = (dkc_ref, dkp_ref, dkp_next_ref) if blk < N_ROPE_BLOCKS else (dvc_ref, dvp_ref, dvp_next_ref)
                cols = slice((blk % 2) * LANES, (blk % 2 + 1) * LANES)
                g = own[:, cols].astype(F32) + shifted(prv, nxt, cols)
            if blk < N_ROPE_BLOCKS:
                g = g * cosv + _swap_halves(g * sinv)
            o_ref[:, blk * LANES:(blk + 1) * LANES] = g.astype(BF16)
            parts.append(jnp.sum(g, axis=0, keepdims=True))
        sums = jnp.concatenate(parts, axis=1)
        _accum(db_ref, sums, i == 0)

    own_spec = _row_spec(tm, KV_WIDTH)
    next_spec = pl.BlockSpec((tm, KV_WIDTH), lambda i: (jnp.minimum(i + 1, nb - 1), 0))
    core, rr = _call(
        body, grid=(nb,),
        in_specs=[_row_spec(tm, Q_WIDTH), own_spec, own_spec, next_spec, own_spec, own_spec, next_spec,
                  _row_spec(tm, LANES), _row_spec(tm, LANES)],
        out_specs=[_row_spec(tm, QKV_WIDTH), _vec_spec(QKV_WIDTH)],
        out_shape=[jax.ShapeDtypeStruct((s, QKV_WIDTH), BF16), jax.ShapeDtypeStruct((1, QKV_WIDTH), F32)],
        operands=(dq, dkc, dkp, dkp, dvc, dvp, dvp, cos, sin), name=name, riders=riders)
    return _ret(core, rr, riders)


ROWS = GQA_GROUP * WINDOW


def _prev_slots():
    kpos = lax.broadcasted_iota(I32, (WINDOW, ROWS), 0)
    qpos = lax.broadcasted_iota(I32, (WINDOW, ROWS), 1) & (WINDOW - 1)
    return kpos > qpos


def _head_cols(ref, head):
    return ref[:, head * HEAD_DIM:(head + 1) * HEAD_DIM]


def _stack_heads(ref, h):
    return jnp.concatenate([_head_cols(ref, GQA_GROUP * h + g) for g in range(GQA_GROUP)], axis=0)


def _band(prev_ref, cur_ref, h):
    return jnp.concatenate([_head_cols(prev_ref, h), _head_cols(cur_ref, h)], axis=0)


def _pick(prev, band):
    return jnp.where(prev, band[:WINDOW], band[WINDOW:])


def _spread(prev, x):
    return jnp.concatenate([jnp.where(prev, x, 0.0), jnp.where(prev, 0.0, x)], axis=0).astype(BF16)


def _attn_probs(s_band, sink, prev, has_prev):
    scale = HEAD_DIM ** -0.5
    s = jnp.where(prev, jnp.where(has_prev, s_band[:WINDOW], NEG), s_band[WINDOW:]) * scale
    m = jnp.maximum(jnp.max(s, axis=0, keepdims=True), sink)
    e, es = jnp.exp(s - m), jnp.exp(sink - m)
    inv = 1.0 / (jnp.sum(e, axis=0, keepdims=True) + es)
    return e * inv, es * inv


def _attn_specs(nb):
    kcol, vcol = Q_WIDTH // KV_WIDTH, Q_WIDTH // KV_WIDTH + 1
    q_spec = pl.BlockSpec((WINDOW, Q_WIDTH), lambda n: (n, 0))
    return [q_spec,
            pl.BlockSpec((WINDOW, KV_WIDTH), lambda n: (n, kcol)),
            pl.BlockSpec((WINDOW, KV_WIDTH), lambda n: (jnp.maximum(n - 1, 0), kcol)),
            pl.BlockSpec((WINDOW, KV_WIDTH), lambda n: (n, vcol)),
            pl.BlockSpec((WINDOW, KV_WIDTH), lambda n: (jnp.maximum(n - 1, 0), vcol)),
            pl.BlockSpec((N_KV_HEADS, 8, ROWS), lambda n: (0, 0, 0))]


def attn_fwd(qkv, sink_rows, *, name, riders=()):
    s = qkv.shape[0]

    def body(q_ref, kc_ref, kp_ref, vc_ref, vp_ref, sink_ref, o_ref):
        prev = _prev_slots()
        has_prev = pl.program_id(0) > 0
        heads = range(N_KV_HEADS)
        s_bands = [lax.dot_general(_band(kp_ref, kc_ref, h), _stack_heads(q_ref, h), NT_DIMS, preferred_element_type=F32)
                   for h in heads]
        p_bands = [_spread(prev, _attn_probs(s_bands[h], sink_ref[h, 0:1, :], prev, has_prev)[0]) for h in heads]
        outs = [lax.dot_general(_band(vp_ref, vc_ref, h), p_bands[h], TN_DIMS, preferred_element_type=F32).T for h in heads]
        for h in heads:
            for g in range(GQA_GROUP):
                head = GQA_GROUP * h + g
                o_ref[:, head * HEAD_DIM:(head + 1) * HEAD_DIM] = outs[h][g * WINDOW:(g + 1) * WINDOW].astype(BF16)

    core, rr = _call(
        body, grid=(s // WINDOW,), in_specs=_attn_specs(s // WINDOW), out_specs=[pl.BlockSpec((WINDOW, Q_WIDTH), lambda n: (n, 0))],
        out_shape=[jax.ShapeDtypeStruct((s, Q_WIDTH), BF16)], operands=(qkv, qkv, qkv, qkv, qkv, sink_rows), sem=("parallel",),
        name=name, riders=riders)
    return _ret(core, rr, riders)


def attn_bwd(qkv, sink_rows, do, *, name, riders=()):
    s = qkv.shape[0]

    def body(q_ref, kc_ref, kp_ref, vc_ref, vp_ref, sink_ref, do_ref, dq_ref, dkc_ref, dkp_ref, dvc_ref, dvp_ref, dsink_ref):
        n = pl.program_id(0)
        prev = _prev_slots()
        scale = HEAD_DIM ** -0.5
        heads = range(N_KV_HEADS)
        qs, dos = [_stack_heads(q_ref, h) for h in heads], [_stack_heads(do_ref, h) for h in heads]
        kbands, vbands = [_band(kp_ref, kc_ref, h) for h in heads], [_band(vp_ref, vc_ref, h) for h in heads]
        s_bands = [lax.dot_general(kbands[h], qs[h], NT_DIMS, preferred_element_type=F32) for h in heads]
        dp_bands = [lax.dot_general(vbands[h], dos[h], NT_DIMS, preferred_element_type=F32) for h in heads]
        ds_bands, p_bands, parts = [], [], []
        for h in heads:
            p, ps = _attn_probs(s_bands[h], sink_ref[h, 0:1, :], prev, n > 0)
            dp = _pick(prev, dp_bands[h])
            delta = jnp.sum(p * dp, axis=0, keepdims=True)
            ds_bands.append(_spread(prev, p * (dp - delta) * scale))
            p_bands.append(_spread(prev, p))
            dsink = -(ps * delta)
            for g in range(GQA_GROUP):
                parts.append(jnp.broadcast_to(jnp.sum(dsink[:, g * WINDOW:(g + 1) * WINDOW], axis=1, keepdims=True), (8, LANES)))
        for h in heads:
            dk = jnp.dot(ds_bands[h], qs[h], preferred_element_type=F32).astype(BF16)
            dv = jnp.dot(p_bands[h], dos[h], preferred_element_type=F32).astype(BF16)
            dq = lax.dot_general(kbands[h], ds_bands[h], TN_DIMS, preferred_element_type=F32).T
            cols = slice(h * HEAD_DIM, (h + 1) * HEAD_DIM)
            dkp_ref[:, cols], dkc_ref[:, cols] = dk[:WINDOW], dk[WINDOW:]
            dvp_ref[:, cols], dvc_ref[:, cols] = dv[:WINDOW], dv[WINDOW:]
            for g in range(GQA_GROUP):
                head = GQA_GROUP * h + g
                dq_ref[:, head * HEAD_DIM:(head + 1) * HEAD_DIM] = dq[g * WINDOW:(g + 1) * WINDOW].astype(BF16)

        @pl.when(n == 0)
        def _():
            for i, part in enumerate(parts):
                dsink_ref[i // GQA_GROUP, i % GQA_GROUP] = part

        @pl.when(n > 0)
        def _():
            for i, part in enumerate(parts):
                dsink_ref[i // GQA_GROUP, i % GQA_GROUP] += part

    rows_q = pl.BlockSpec((WINDOW, Q_WIDTH), lambda n: (n, 0))
    rows_kv = pl.BlockSpec((WINDOW, KV_WIDTH), lambda n: (n, 0))
    kv_shape = jax.ShapeDtypeStruct((s, KV_WIDTH), BF16)
    core, rr = _call(
        body, grid=(s // WINDOW,), in_specs=_attn_specs(s // WINDOW) + [rows_q],
        out_specs=[rows_q, rows_kv, rows_kv, rows_kv, rows_kv,
                   pl.BlockSpec((N_KV_HEADS, GQA_GROUP, 8, LANES), lambda n: (0, 0, 0, 0))],
        out_shape=[jax.ShapeDtypeStruct((s, Q_WIDTH), BF16), kv_shape, kv_shape, kv_shape, kv_shape,
                   jax.ShapeDtypeStruct((N_KV_HEADS, GQA_GROUP, 8, LANES), F32)],
        operands=(qkv, qkv, qkv, qkv, qkv, sink_rows, do), sem=("arbitrary",), name=name, riders=riders)
    return _ret(core, rr, riders)


GELU_C = 0.7978845608028654
GELU_A = 0.044715


def _gelu(x):
    return 0.5 * x * (1.0 + jnp.tanh(x * (GELU_C + (GELU_C * GELU_A) * (x * x))))


def _gelu_and_grad(x):
    x2 = x * x
    t = jnp.tanh(x * (GELU_C + (GELU_C * GELU_A) * x2))
    half_x, one_t = 0.5 * x, 1.0 + t
    return half_x * one_t, 0.5 * one_t + half_x * (1.0 - t * t) * (GELU_C + (3.0 * GELU_C * GELU_A) * x2)


def _tril_bf16(w):
    row = lax.broadcasted_iota(I32, (SGU_CHUNK, SGU_CHUNK), 0)
    col = lax.broadcasted_iota(I32, (SGU_CHUNK, SGU_CHUNK), 1)
    return jnp.where(row >= col, w, 0.0).astype(BF16)


def _sgu_norm(vg, g, b):
    mu = jnp.mean(vg, axis=-1, keepdims=True)
    cen = vg - mu
    rstd = lax.rsqrt(jnp.mean(cen * cen, axis=-1, keepdims=True) + EPS)
    xhat = cen * rstd
    return xhat, rstd, xhat * g + b


def sgu_in_fwd(h, w_in, ln_g, ln_b, w_sp, b_sp, *, name, tm=256, riders=()):
    s, k = h.shape
    ns = w_in.shape[2]
    tm = _row_tile(s, tm)

    def body(h_ref, w0, w1, w2, w3, g_ref, b_ref, w_ref, bs_ref, z_ref, y_ref):
        hv = h_ref[...]
        zs = [jnp.dot(hv, w_ref_j[...], preferred_element_type=F32) for w_ref_j in (w0, w1, w2, w3)]
        for j, zj in enumerate(zs):
            z_ref[:, j * ns:(j + 1) * ns] = zj.astype(BF16)
        u = _gelu(jnp.concatenate(zs[:2], axis=1))
        _, _, vn = _sgu_norm(_gelu(jnp.concatenate(zs[2:], axis=1)), g_ref[...], b_ref[...])
        vn = vn.astype(BF16)
        for grp in range(SGU_GROUPS):
            w = _tril_bf16(w_ref[grp])
            cols = slice(grp * LANES, (grp + 1) * LANES)
            for ch in range(tm // SGU_CHUNK):
                rows = slice(ch * SGU_CHUNK, (ch + 1) * SGU_CHUNK)
                mixed = jnp.dot(w, vn[rows, cols], preferred_element_type=F32) + bs_ref[grp]
                y_ref[rows, cols] = (u[rows, cols] * mixed).astype(BF16)

    def shard(j):
        return pl.BlockSpec((None, k, ns), lambda i: (j, 0, 0))

    full3 = pl.BlockSpec((SGU_GROUPS, SGU_CHUNK, SGU_CHUNK), lambda i: (0, 0, 0))
    core, rr = _call(
        body, grid=(s // tm,),
        in_specs=[_row_spec(tm, k)] + [shard(j) for j in range(N_CHIPS)] + [_vec_spec(D_MODEL), _vec_spec(D_MODEL), full3, full3],
        out_specs=[_row_spec(tm, 2 * D_MODEL), _row_spec(tm, D_MODEL)],
        out_shape=[jax.ShapeDtypeStruct((s, 2 * D_MODEL), BF16), jax.ShapeDtypeStruct((s, D_MODEL), BF16)],
        operands=(h, w_in, w_in, w_in, w_in, ln_g, ln_b, w_sp, b_sp), sem=("parallel",), name=name, riders=riders)
    return _ret(core, rr, riders)


def sgu_bwd(z, dy, ln_g, ln_b, w_sp, b_sp, *, name, tm=256, riders=()):
    s = z.shape[0]
    tm = _row_tile(s, tm)

    def body(z_ref, dy_ref, g_ref, b_ref, w_ref, bs_ref, dz_ref, dw_ref, dbs_ref, dg_ref, db_ref, dvn_buf):
        first = pl.program_id(0) == 0
        u, u_grad = _gelu_and_grad(z_ref[:, :D_MODEL].astype(F32))
        vg, v_grad = _gelu_and_grad(z_ref[:, D_MODEL:].astype(F32))
        xhat, rstd, vn = _sgu_norm(vg, g_ref[...], b_ref[...])
        vn = vn.astype(BF16)
        dyv = dy_ref[...]
        dmixed = dyv * u
        dz_gate = dyv * u_grad
        row = lax.broadcasted_iota(I32, (SGU_CHUNK, SGU_CHUNK), 0)
        col = lax.broadcasted_iota(I32, (SGU_CHUNK, SGU_CHUNK), 1)
        dws, dbss = [], []
        for grp in range(SGU_GROUPS):
            w = _tril_bf16(w_ref[grp])
            cols = slice(grp * LANES, (grp + 1) * LANES)
            dw = jnp.zeros((SGU_CHUNK, SGU_CHUNK), F32)
            dbs = jnp.zeros((SGU_CHUNK, 1), F32)
            for ch in range(tm // SGU_CHUNK):
                rows = slice(ch * SGU_CHUNK, (ch + 1) * SGU_CHUNK)
                vblk = vn[rows, cols]
                mixed = jnp.dot(w, vblk, preferred_element_type=F32) + bs_ref[grp]
                dz_ref[rows, cols] = (dz_gate[rows, cols] * mixed).astype(BF16)
                dm = dmixed[rows, cols]
                dmb = dm.astype(BF16)
                dvn_buf[rows, cols] = lax.dot_general(w, dmb, TN_DIMS, preferred_element_type=F32)
                dw += lax.dot_general(dmb, vblk, NT_DIMS, preferred_element_type=F32)
                dbs += jnp.sum(dm, axis=-1, keepdims=True)
            dws.append(jnp.where(row >= col, dw, 0.0))
            dbss.append(jnp.broadcast_to(dbs, (SGU_CHUNK, SGU_CHUNK)))

        dvn = dvn_buf[...]
        dxhat = dvn * g_ref[...]
        dvg = rstd * (dxhat - jnp.mean(dxhat, axis=-1, keepdims=True) - xhat * jnp.mean(dxhat * xhat, axis=-1, keepdims=True))
        dz_ref[:, D_MODEL:] = (dvg * v_grad).astype(BF16)
        dlng, dlnb = jnp.sum(dvn * xhat, axis=0, keepdims=True), jnp.sum(dvn, axis=0, keepdims=True)

        @pl.when(first)
        def _():
            for grp in range(SGU_GROUPS):
                dw_ref[grp] = dws[grp]
                dbs_ref[grp] = dbss[grp]
            dg_ref[...] = dlng
            db_ref[...] = dlnb

        @pl.when(jnp.logical_not(first))
        def _():
            for grp in range(SGU_GROUPS):
                dw_ref[grp] += dws[grp]
                dbs_ref[grp] += dbss[grp]
            dg_ref[...] += dlng
            db_ref[...] += dlnb

    full3 = pl.BlockSpec((SGU_GROUPS, SGU_CHUNK, SGU_CHUNK), lambda i: (0, 0, 0))
    s3 = jax.ShapeDtypeStruct((SGU_GROUPS, SGU_CHUNK, SGU_CHUNK), F32)
    vshape = jax.ShapeDtypeStruct((1, D_MODEL), F32)
    core, rr = _call(
        body, grid=(s // tm,),
        in_specs=[_row_spec(tm, 2 * D_MODEL), _row_spec(tm, D_MODEL), _vec_spec(D_MODEL), _vec_spec(D_MODEL), full3, full3],
        out_specs=[_row_spec(tm, 2 * D_MODEL), full3, full3, _vec_spec(D_MODEL), _vec_spec(D_MODEL)],
        out_shape=[jax.ShapeDtypeStruct((s, 2 * D_MODEL), BF16), s3, s3, vshape, vshape],
        scratch_shapes=[pltpu.VMEM((tm, D_MODEL), F32)], operands=(z, dy, ln_g, ln_b, w_sp, b_sp), name=name, riders=riders)
    return _ret(core, rr, riders)


def _sigmoid(x):
    return 1.0 / (1.0 + jnp.exp(-x))


def ffn_up(h, w_gu, *, name, tm=512, riders=()):
    s = h.shape[0]
    tm = _row_tile(s, tm)

    def body(h_ref, wg_ref, wu_ref, d_ref, a_ref):
        hv = h_ref[...]
        sub = min(256, tm)
        for t in range(tm // sub):
            rows = slice(t * sub, (t + 1) * sub)
            g = jnp.dot(hv[rows], wg_ref[...], preferred_element_type=F32)
            u = jnp.dot(hv[rows], wu_ref[...], preferred_element_type=F32)
            sig = _sigmoid(g)
            silu = g * sig
            d_ref[0, rows, :] = (u * (sig + silu * (1.0 - sig))).astype(BF16)
            d_ref[1, rows, :] = silu.astype(BF16)
            a_ref[rows, :] = (silu * u).astype(BF16)

    core, rr = _call(
        body, grid=(2, s // tm),
        in_specs=[pl.BlockSpec((tm, D_MODEL), lambda j, i: (i, 0)),
                  pl.BlockSpec((None, D_MODEL, FF_HALF), lambda j, i: (j, 0, 0)),
                  pl.BlockSpec((None, D_MODEL, FF_HALF), lambda j, i: (j + 2, 0, 0))],
        out_specs=[pl.BlockSpec((2, tm, FF_HALF), lambda j, i: (0, i, j)), pl.BlockSpec((tm, FF_HALF), lambda j, i: (i, j))],
        out_shape=[jax.ShapeDtypeStruct((2, s, D_FF), BF16), jax.ShapeDtypeStruct((s, D_FF), BF16)],
        operands=(h, w_gu, w_gu), sem=("parallel", "parallel"), name=name, riders=riders)
    return _ret(core, rr, riders)


def ffn_dact(df, w_d, gu, *, name, tm=512, riders=()):
    s = df.shape[0]
    tm = _row_tile(s, tm)

    def body(df_ref, w_ref, d_ref, o_ref):
        da = lax.dot_general(df_ref[...], w_ref[...], NT_DIMS, preferred_element_type=F32)
        o_ref[0] = (da * d_ref[0].astype(F32)).astype(BF16)
        o_ref[1] = (da * d_ref[1].astype(F32)).astype(BF16)

    planes = pl.BlockSpec((2, tm, FF_HALF), lambda j, i: (0, i, j))
    core, rr = _call(
        body, grid=(2, s // tm),
        in_specs=[pl.BlockSpec((tm, D_MODEL), lambda j, i: (i, 0)), pl.BlockSpec((FF_HALF, D_MODEL), lambda j, i: (j, 0)), planes],
        out_specs=[planes], out_shape=[jax.ShapeDtypeStruct((2, s, D_FF), BF16)], operands=(df, w_d, gu),
        sem=("parallel", "parallel"), name=name, riders=riders)
    return _ret(core, rr, riders)


def _weight_tile(rows):
    for tr in (512, 352, 256, 128):
        if rows % tr == 0:
            return tr
    return rows


def place_shard(w, layer, chip_arr, dtype, *, name, riders=()):
    _, r, c = w.shape
    tr = _weight_tile(r)

    def body(chip_ref, w_ref, o_ref):
        o_ref[...] = w_ref[...].astype(dtype)

    core, rr = _call(
        body, grid=(r // tr,), prefetch=(chip_arr,),
        in_specs=[pl.BlockSpec((None, tr, c), lambda i, chip: (layer, i, 0))],
        out_specs=[pl.BlockSpec((None, tr, c), lambda i, chip: (chip[0], i, 0))],
        out_shape=[jax.ShapeDtypeStruct((N_CHIPS, r, c), dtype)], operands=(w,), sem=("parallel",), name=name, riders=riders)
    return _ret(core, rr, riders)


def _adamw_math(w, g, m, v):
    m = ADAM_B1 * m + (1.0 - ADAM_B1) * g
    v = ADAM_B2 * v + (1.0 - ADAM_B2) * (g * g)
    m_hat = m / (1.0 - ADAM_B1 ** ADAM_STEP)
    v_hat = v / (1.0 - ADAM_B2 ** ADAM_STEP)
    delta = -ADAM_LR * (m_hat / (jnp.sqrt(v_hat) + ADAM_EPS) + ADAM_WD * w)
    return delta, m, v


def adamw(w, g, m, v, *, name):
    nl, r, c = w.shape
    tr = _weight_tile(r)

    def body(w_ref, g_ref, m_ref, v_ref, go_ref, d_ref, mo_ref, vo_ref):
        gv = g_ref[...]
        go_ref[...] = gv
        d_ref[...], mo_ref[...], vo_ref[...] = _adamw_math(w_ref[...], gv, m_ref[...], v_ref[...])

    spec = pl.BlockSpec((None, tr, c), lambda l, i: (l, i, 0))
    shape = jax.ShapeDtypeStruct(w.shape, F32)
    outs, _ = _call(body, grid=(nl, r // tr), in_specs=[spec] * 4, out_specs=[spec] * 4, out_shape=[shape] * 4,
                    operands=(w, g, m, v), sem=("parallel", "parallel"), name=name)
    return outs


def adamw_small(ws, gs, ms, vs, *, name):
    n = len(ws)

    def body(*refs):
        ins, outs = refs[:4 * n], refs[4 * n:]
        for t in range(n):
            gv = ins[n + t][...]
            outs[t][...] = gv
            outs[n + t][...], outs[2 * n + t][...], outs[3 * n + t][...] = _adamw_math(
                ins[t][...], gv, ins[2 * n + t][...], ins[3 * n + t][...])

    shapes = [jax.ShapeDtypeStruct(w.shape, F32) for w in ws]
    res = pl.pallas_call(body, out_shape=shapes * 4, name=name)(*ws, *gs, *ms, *vs)
    return res[:n], res[n:2 * n], res[2 * n:3 * n], res[3 * n:]


def pair_add(g, r1, c_arr, *, name):
    _, rows, cdim = g.shape
    h = rows // 2

    def body(c_ref, g_ref, r_ref, o_ref):
        o_ref[...] = (g_ref[...].astype(F32) + r_ref[...].astype(F32)).astype(o_ref.dtype)

    (out,), _ = _call(
        body, grid=(N_CHIPS,), prefetch=(c_arr,),
        in_specs=[pl.BlockSpec((None, h, cdim), lambda s, c: (s, c[0], 0)), pl.BlockSpec((None, h, cdim), lambda s, c: (s, 0, 0))],
        out_specs=[pl.BlockSpec((None, h, cdim), lambda s, c: (s, 0, 0))],
        out_shape=[jax.ShapeDtypeStruct((N_CHIPS, h, cdim), g.dtype)], operands=(g, r1), sem=("parallel",), name=name)
    return out


def final_add(g, r1, r2, jc_arr, *, dest_shape, lead, prev, name):
    _, rows, cdim = g.shape
    h = rows // 2

    def body(jc_ref, g_ref, r1_ref, r2_ref, *rest):
        o_ref = rest[-1]
        acc = g_ref[...].astype(F32) + r1_ref[...].astype(F32)
        for k in range(3):
            acc = acc + r2_ref[k].astype(F32)
        o_ref[...] = acc

    if lead is None:
        o_spec = pl.BlockSpec((h, cdim), lambda i, jc: (jc[1], 0))
    elif lead == "chip":
        o_spec = pl.BlockSpec((None, h, cdim), lambda i, jc: (jc[0], jc[1], 0))
    else:
        o_spec = pl.BlockSpec((None, h, cdim), lambda i, jc: (lead, jc[1], 0))
    in_specs = [pl.BlockSpec((None, h, cdim), lambda i, jc: (jc[0], jc[1], 0)),
                pl.BlockSpec((None, h, cdim), lambda i, jc: (jc[0], 0, 0)),
                pl.BlockSpec((3, h, cdim), lambda i, jc: (0, 0, 0))]
    operands = [g, r1, r2]
    aliases = None
    if prev is not None:
        in_specs.append(ANY)
        operands.append(prev)
        aliases = {3: 0}
    (out,), _ = _call(body, grid=(1,), prefetch=(jc_arr,), in_specs=in_specs, out_specs=[o_spec],
                      out_shape=[jax.ShapeDtypeStruct(dest_shape, F32)], operands=operands, aliases=aliases, name=name)
    return out


def _place():
    return lax.axis_index("x"), lax.axis_index("y"), lax.axis_index("c")


def _partner(x, y, k):
    return (1 - x if k >> 1 else x), (1 - y if k & 1 else y)


WHOLE = (0, 1, 1)


def _half(rows, sel, dtype, piece=WHOLE):
    lo, hi, n = piece
    align = 16 if dtype == BF16 else 8
    step = rows // 2 // n
    assert rows // 2 == step * n and step % align == 0
    return pl.ds(pl.multiple_of(sel * (rows // 2) + lo * step, align), (hi - lo) * step)


def _rider(peers, inputs, aliased, fresh, nsem, copies, arrivals):
    def start(ins, outs, send, recv):
        for cp in copies(ins, outs, send, recv):
            cp.start()

    def finish(ins, outs, send, recv):
        for cp in arrivals(ins, outs, send, recv):
            cp.wait_recv()
        for cp in copies(ins, outs, send, recv):
            cp.wait_send()

    return types.SimpleNamespace(peers=peers, inputs=list(inputs), aliased=list(aliased), fresh=list(fresh), nsem=nsem,
                                 start=start, finish=finish)


def _remote(src, dst, send, recv, idx, dev):
    return pltpu.make_async_remote_copy(src_ref=src, dst_ref=dst, send_sem=send.at[idx], recv_sem=recv.at[idx],
                                        device_id=dev, device_id_type=MESH)


def gather_ici_rider(fulls, pieces=None):
    nt = len(fulls)
    pieces = pieces or [WHOLE] * nt

    def region(outs, t, slot, sel):
        return outs[t].at[slot, _half(fulls[t].shape[1], sel, fulls[t].dtype, pieces[t])]

    def copies(ins, outs, send, recv):
        x, y, c = _place()
        res = []
        for t in range(nt):
            for k in (1, 2, 3):
                px, py = _partner(x, y, k)
                mine = region(outs, t, 2 * x + y, c)
                res.append(_remote(mine, mine, send, recv, 3 * t + k - 1, (px, py, c)))
        return res

    def arrivals(ins, outs, send, recv):
        x, y, c = _place()
        res = []
        for t in range(nt):
            for k in (1, 2, 3):
                px, py = _partner(x, y, k)
                theirs = region(outs, t, 2 * px + py, c)
                res.append(_remote(theirs, theirs, send, recv, 3 * t + k - 1, (x, y, c)))
        return res

    return _rider("chips", fulls, range(nt), [], 3 * nt, copies, arrivals)


def gather_d2d_rider(fulls, pieces=None):
    nt = len(fulls)
    pieces = pieces or [WHOLE] * nt

    def region(outs, t, slot, sel):
        return outs[t].at[slot, _half(fulls[t].shape[1], sel, fulls[t].dtype, pieces[t])]

    def both(outs, send, recv, mine):
        x, y, c = _place()
        res = []
        for t in range(nt):
            for k in (1, 2, 3):
                px, py = _partner(x, y, k)
                part = region(outs, t, 2 * px + py, c if mine else 1 - c)
                res.append(_remote(part, part, send, recv, 3 * t + k - 1, (x, y, 1 - c)))
        return res

    return _rider("sibling", fulls, range(nt), [], 3 * nt, lambda i, o, s, r: both(o, s, r, True),
                  lambda i, o, s, r: both(o, s, r, False))


def exchange_rider(grads):
    nt = len(grads)

    def both(ins, outs, send, recv):
        x, y, c = _place()
        return [_remote(ins[t].at[:, _half(grads[t].shape[1], 1 - c, grads[t].dtype)], outs[t], send, recv, t, (x, y, 1 - c))
                for t in range(nt)]

    fresh = [jax.ShapeDtypeStruct((N_CHIPS, g.shape[1] // 2, g.shape[2]), g.dtype) for g in grads]
    return _rider("sibling", grads, [], fresh, nt, both, both)


def scatter_rider(parts):
    nt = len(parts)

    def both(ins, outs, send, recv):
        x, y, c = _place()
        res = []
        for t in range(nt):
            for k in (1, 2, 3):
                px, py = _partner(x, y, k)
                res.append(_remote(ins[t].at[2 * px + py], outs[t].at[k - 1], send, recv, 3 * t + k - 1, (px, py, c)))
        return res

    fresh = [jax.ShapeDtypeStruct((3,) + p.shape[1:], p.dtype) for p in parts]
    return _rider("chips", parts, [], fresh, 3 * nt, both, both)


def broadcast_rider(bufs, items):
    def region(outs, item, sel):
        bi, lead = item
        ref = outs[bi]
        if lead == "chip":
            x, y, _ = _place()
            ref = ref.at[2 * x + y]
        elif lead is not None:
            ref = ref.at[lead]
        return ref.at[_half(ref.shape[0], sel, F32)]

    def both(outs, send, recv, mine):
        x, y, c = _place()
        res = []
        for i, item in enumerate(items):
            part = region(outs, item, c if mine else 1 - c)
            res.append(_remote(part, part, send, recv, i, (x, y, 1 - c)))
        return res

    return _rider("sibling", bufs, range(len(bufs)), [], len(items), lambda i, o, s, r: both(o, s, r, True),
                  lambda i, o, s, r: both(o, s, r, False))


def allcast_rider(buf):
    peers = [(k, flip) for k in range(N_CHIPS) for flip in (0, 1) if (k, flip) != (0, 0)]

    def both(outs, send, recv, mine):
        x, y, c = _place()
        res = []
        for i, (k, flip) in enumerate(peers):
            px, py = _partner(x, y, k)
            pc = 1 - c if flip else c
            slot, sel = (2 * x + y, c) if mine else (2 * px + py, pc)
            part = outs[0].at[slot, _half(buf.shape[1], sel, F32)]
            res.append(_remote(part, part, send, recv, i, (px, py, pc)))
        return res

    return _rider("everyone", [buf], [0], [], len(peers), lambda i, o, s, r: both(o, s, r, True),
                  lambda i, o, s, r: both(o, s, r, False))


def comm_call(riders, *, name):
    _, res = _call(None, riders=riders, name=name)
    return res


SLAB_ROWS = 192


def _pad_rows(a, rows=8):
    return jnp.pad(a, ((0, rows - a.shape[0]), (0, 0)))


def _pack_small(norm_grads, db_qkv, db_o, dsinks, db_sp, dln_g, dln_b, dw_sp, loss_part):
    parts = [
        jnp.concatenate(norm_grads, axis=0),
        _pad_rows(jnp.pad(db_qkv, ((0, 0), (0, 2 * D_MODEL - QKV_WIDTH))).reshape(2, D_MODEL)),
        _pad_rows(db_o),
        _pad_rows(jnp.pad(dsinks.reshape(1, N_Q_HEADS), ((0, 0), (0, D_MODEL - N_Q_HEADS)))),
        _pad_rows(db_sp.reshape(1, D_MODEL)),
        _pad_rows(jnp.concatenate([dln_g, dln_b, jnp.pad(loss_part[0:1], ((0, 0), (0, D_MODEL - LANES)))], axis=0)),
        dw_sp.reshape(SGU_CHUNK, D_MODEL),
    ]
    slab = jnp.concatenate(parts, axis=0)
    return jnp.pad(slab, ((0, SLAB_ROWS - slab.shape[0]), (0, 0))).reshape(N_CHIPS, SLAB_ROWS // N_CHIPS, D_MODEL)


def _unpack_small(slab, j):
    slab = slab.reshape(SLAB_ROWS, D_MODEL)
    norms = [slab[2 * i:2 * i + 2] for i in range(4)]
    db_qkv = slab[8:10].reshape(1, 2 * D_MODEL)[:, :QKV_WIDTH]
    db_o = slab[16:17]
    dsinks = slab[24:25, :N_Q_HEADS]
    db_sp = slab[32:33].reshape(SGU_GROUPS, SGU_CHUNK)
    width = D_MODEL // N_CHIPS
    dln_g = lax.dynamic_slice(slab[40:41], (0, j * width), (1, width))
    dln_b = lax.dynamic_slice(slab[41:42], (0, j * width), (1, width))
    dw_sp = slab[48:48 + SGU_CHUNK].reshape(SGU_GROUPS * SGU_CHUNK, SGU_CHUNK)
    return norms, db_qkv, db_o, dsinks, db_sp, dln_g, dln_b, dw_sp, slab[42, 0]


class _GradReduce:
    def __init__(self, c_arr, jc_arr, dest_shapes):
        self.c_arr, self.jc_arr, self.dest_shapes = c_arr, jc_arr, dest_shapes
        self.grad, self.sibling, self.pair, self.chips, self.dest = {}, {}, {}, {}, {}

    def exchange(self, tags):
        return exchange_rider([self.grad[t] for t in tags])

    def exchanged(self, tags, res):
        for t, r in zip(tags, res):
            self.sibling[t] = r
            self.pair[t] = pair_add(self.grad[t], r, self.c_arr, name=f"pair_add_{t}")

    def scatter(self, tags):
        return scatter_rider([self.pair[t] for t in tags])

    def scattered(self, tags, res, where):
        for t, r in zip(tags, res):
            name, lead = where[t]
            self.dest[name] = final_add(self.grad[t], self.sibling[t], r, self.jc_arr, dest_shape=self.dest_shapes[name],
                                        lead=lead, prev=self.dest.get(name), name=f"final_add_{t}")

    def broadcast(self, items):
        names = []
        for n, _ in items:
            if n not in names:
                names.append(n)
        return names, broadcast_rider([self.dest[n] for n in names], [(names.index(n), lead) for n, lead in items])

    def broadcasted(self, names, res):
        for n, r in zip(names, res):
            self.dest[n] = r


def kernel(x, norm_mix_pre, norm_mix_post, norm_ffn_pre, norm_ffn_post, attn_w_qkv, attn_b_qkv, attn_sinks, attn_w_o, attn_b_o, sgu_w_in, sgu_ln_g, sgu_ln_b, sgu_w_spatial, sgu_b_spatial, sgu_w_out, ffn_w_gate_up, ffn_w_down, loss_target, m_norm_mix_pre, m_norm_mix_post, m_norm_ffn_pre, m_norm_ffn_post, m_attn_w_qkv, m_attn_b_qkv, m_attn_sinks, m_attn_w_o, m_attn_b_o, m_sgu_w_in, m_sgu_ln_g, m_sgu_ln_b, m_sgu_w_spatial, m_sgu_b_spatial, m_sgu_w_out, m_ffn_w_gate_up, m_ffn_w_down, v_norm_mix_pre, v_norm_mix_post, v_norm_ffn_pre, v_norm_ffn_post, v_attn_w_qkv, v_attn_b_qkv, v_attn_sinks, v_attn_w_o, v_attn_b_o, v_sgu_w_in, v_sgu_ln_g, v_sgu_ln_b, v_sgu_w_spatial, v_sgu_b_spatial, v_sgu_w_out, v_ffn_w_gate_up, v_ffn_w_down):
    s = x.shape[1]
    x0 = x.reshape(s, D_MODEL)
    target = loss_target.reshape(s, D_MODEL)
    mx, my, mc = lax.axis_index("x"), lax.axis_index("y"), lax.axis_index("c")
    chip = 2 * mx + my
    chip_arr = jnp.reshape(chip, (1,)).astype(I32)
    c_arr = jnp.reshape(mc, (1,)).astype(I32)
    jc_arr = jnp.stack([chip, mc]).astype(I32)
    zero_bias = jnp.zeros((1, D_MODEL), F32)

    def gain(p, i):
        return p[i:i + 1]

    big = [attn_w_qkv, attn_w_o, sgu_w_in, sgu_w_out, ffn_w_gate_up, ffn_w_gate_up, ffn_w_down, ffn_w_down]
    layers = [0, 0, 0, 0, 0, 1, 0, 1]
    tags = ["qkv", "wo", "win", "wout", "wgu0", "wgu1", "wd0", "wd1"]
    full = {t: place_shard(w, l, chip_arr, BF16, name=f"place_{t}") for w, l, t in zip(big, layers, tags) if t != "wgu1"}
    ln_pack = _pad_rows(jnp.concatenate([sgu_ln_g, sgu_ln_b], axis=0), 16)[None]
    full["ln"] = place_shard(ln_pack, 0, chip_arr, F32, name="place_ln")

    def split(items):
        return [i if isinstance(i, str) else i[0] for i in items], [WHOLE if isinstance(i, str) else tuple(i[1:]) for i in items]

    def ici(*items):
        names, pieces = split(items)
        return gather_ici_rider([full[n] for n in names], pieces)

    def d2d(*items):
        names, pieces = split(items)
        return gather_d2d_rider([full[n] for n in names], pieces)

    def landed(items, res):
        for n, r in zip(split(items)[0], res):
            full[n] = r

    cos, sin = _rope_tables(s)
    sink_rows = jnp.broadcast_to(
        jnp.repeat(attn_sinks.reshape(N_KV_HEADS, GQA_GROUP), WINDOW, axis=1)[:, None, :], (N_KV_HEADS, 8, ROWS))
    w_sp = sgu_w_spatial.reshape(SGU_GROUPS, SGU_CHUNK, SGU_CHUNK)
    b_sp = jnp.broadcast_to(sgu_b_spatial.reshape(SGU_GROUPS, SGU_CHUNK)[:, :, None], (SGU_GROUPS, SGU_CHUNK, LANES))

    h0, (res,) = prenorm(x0, gain(norm_mix_pre, 0), name="prenorm_0", riders=[ici("qkv", "ln")])
    landed(("qkv", "ln"), res)
    full["wgu1"], (res,) = place_shard(ffn_w_gate_up, 1, chip_arr, BF16, name="place_wgu1", riders=[d2d("qkv", "ln")])
    landed(("qkv", "ln"), res)
    ln_g = full["ln"][:, 0, :].reshape(1, D_MODEL)
    ln_b = full["ln"][:, 1, :].reshape(1, D_MODEL)

    def hosted(call, stages):
        outputs, results = call([{"ici": ici, "d2d": d2d}[kind](*items) for kind, items in stages])
        for (_, items), res in zip(stages, results):
            landed(items, res)
        return outputs

    qkv = hosted(lambda r: qkv_proj(h0, full["qkv"], attn_b_qkv, cos, sin, name="qkv_proj", riders=r),
                 [("ici", ("wo", ("wgu0", 0, 3, 8)))])
    o = hosted(lambda r: attn_fwd(qkv, sink_rows, name="attn_fwd", riders=r),
               [("d2d", ("wo",)), ("ici", (("wgu0", 3, 8, 8), ("wd0", 0, 2, 11)))])
    w_o = full["wo"].reshape(Q_WIDTH, D_MODEL)
    x1, h1, m0 = hosted(lambda r: proj_residual_norm(o, w_o, x0, attn_b_o, gain(norm_mix_post, 0), gain(norm_ffn_pre, 0),
                                                     name="attn_out_norm", riders=r),
                        [("d2d", ("wgu0",)), ("ici", (("wd0", 2, 11, 11),))])
    gu0, a0 = hosted(lambda r: ffn_up(h1, full["wgu0"], name="ffn_up_0", riders=r),
                     [("d2d", ("wd0",)), ("ici", ("win", "wout", ("wgu1", 0, 4, 8)))])
    w_d0 = full["wd0"].reshape(D_FF, D_MODEL)
    x2, h2, f0 = hosted(lambda r: proj_residual_norm(a0, w_d0, x1, zero_bias, gain(norm_ffn_post, 0), gain(norm_mix_pre, 1),
                                                     name="ffn_down_norm_0", riders=r),
                        [("d2d", ("win", "wout")), ("ici", (("wgu1", 4, 8, 8),))])
    w_in = full["win"]
    z, y = hosted(lambda r: sgu_in_fwd(h2, w_in, ln_g, ln_b, w_sp, b_sp, name="sgu_in_fwd", riders=r),
                  [("d2d", ("wgu1",)), ("ici", ("wd1",))])
    w_out = full["wout"].reshape(D_MODEL, D_MODEL)
    x3, h3, m1 = hosted(lambda r: proj_residual_norm(y, w_out, x2, zero_bias, gain(norm_mix_post, 1), gain(norm_ffn_pre, 1),
                                                     name="sgu_out_norm", riders=r),
                        [("d2d", ("wd1",))])
    w_qkv, w_gu0, w_gu1 = full["qkv"], full["wgu0"], full["wgu1"]
    w_d1 = full["wd1"].reshape(D_FF, D_MODEL)
    gu1, a1, dx4, df1, dg_fpost1, loss_part = ffn_fwd_loss_rows(
        h3, w_gu1, w_d1, x3, gain(norm_ffn_post, 1), target, name="ffn_fwd_loss_rows")

    red = _GradReduce(c_arr, jc_arr, {
        "qkv": attn_w_qkv.shape[1:], "wo": attn_w_o.shape[1:], "win": sgu_w_in.shape[1:], "wout": sgu_w_out.shape[1:],
        "wgu": ffn_w_gate_up.shape, "wd": ffn_w_down.shape, "slab": (N_CHIPS, SLAB_ROWS // N_CHIPS, D_MODEL)})
    where = {"qkv": ("qkv", None), "wo": ("wo", None), "win": ("win", None), "wout": ("wout", None), "wgu0": ("wgu", 0),
             "wgu1": ("wgu", 1), "wd0": ("wd", 0), "wd1": ("wd", 1), "small": ("slab", "chip")}

    dgu1, dx3, dm1, dg_fpre1, dg_mpost1, _ = ffn_bwd_rows(
        df1, w_d1, gu1, w_gu1, dx4, x3, gain(norm_ffn_pre, 1), m1, gain(norm_mix_post, 1), name="ffn_bwd_rows_1")
    red.grad["wd1"] = mm_tn(a1, df1, shard_major=False, tm=256, tn=D_MODEL, name="dw_down_1").reshape(
        N_CHIPS, D_FF // N_CHIPS, D_MODEL)
    red.grad["wgu1"], (res,) = mm_tn(h3, dgu1, shard_major=True, tm=512, tn=FF_HALF, name="dw_gate_up_1",
                                     riders=[red.exchange(["wd1"])])
    red.exchanged(["wd1"], res)
    dy, (res,) = mm_nt(dm1, w_out, out_dtype=F32, name="dy_sgu", riders=[red.exchange(["wgu1"])])
    red.exchanged(["wgu1"], res)
    red.grad["wout"] = mm_tn(y, dm1, shard_major=False, tm=512, tn=D_MODEL, name="dw_sgu_out").reshape(
        N_CHIPS, D_MODEL // N_CHIPS, D_MODEL)
    (dz, dw_sp, db_sp, dln_g, dln_b), (res_a, res_b) = sgu_bwd(
        z, dy, ln_g, ln_b, w_sp, b_sp, name="sgu_bwd", riders=[red.scatter(["wgu1"]), red.exchange(["wout"])])
    red.scattered(["wgu1"], res_a, where)
    red.exchanged(["wout"], res_b)
    names, rider = red.broadcast([("wgu", 1)])
    red.grad["win"], (res_a, res_b) = mm_tn(h2, dz, shard_major=True, tm=D_MODEL, tn=2 * D_MODEL // N_CHIPS, name="dw_sgu_in",
                                            riders=[rider, red.scatter(["wout"])])
    red.broadcasted(names, res_a)
    red.scattered(["wout"], res_b, where)
    names, rider = red.broadcast([("wout", None)])
    (dx2, df0, dg_mpre1, dg_fpost0, _), (res_a, res_b) = dh_norm_bwd_pair(
        dz, w_in, dx3, x2, gain(norm_mix_pre, 1), f0, gain(norm_ffn_post, 0), name="dh_sgu_norm",
        riders=[red.exchange(["win"]), rider])
    red.exchanged(["win"], res_a)
    red.broadcasted(names, res_b)
    (dgu0, dx1, dm0, dg_fpre0, dg_mpost0, db_o), (res,) = ffn_bwd_rows(
        df0, w_d0, gu0, w_gu0, dx2, x1, gain(norm_ffn_pre, 0), m0, gain(norm_mix_post, 0), name="ffn_bwd_rows_0",
        riders=[red.scatter(["wd1", "win"])])
    red.scattered(["wd1", "win"], res, where)
    names, rider = red.broadcast([("wd", 1), ("win", None)])
    dw_d0, (res,) = mm_tn(a0, df0, shard_major=False, tm=256, tn=D_MODEL, name="dw_down_0", riders=[rider])
    red.broadcasted(names, res)
    red.grad["wd0"] = dw_d0.reshape(N_CHIPS, D_FF // N_CHIPS, D_MODEL)
    do, (res,) = mm_nt(dm0, w_o, out_dtype=BF16, name="do_attn", riders=[red.exchange(["wd0"])])
    red.exchanged(["wd0"], res)
    red.grad["wgu0"], (res,) = mm_tn(h1, dgu0, shard_major=True, tm=512, tn=FF_HALF, name="dw_gate_up_0",
                                     riders=[red.scatter(["wd0"])])
    red.scattered(["wd0"], res, where)
    names, rider = red.broadcast([("wd", 0)])
    dw_o, (res_a, res_b) = mm_tn(o, dm0, shard_major=False, tm=512, tn=D_MODEL, name="dw_attn_out",
                                 riders=[red.exchange(["wgu0"]), rider])
    red.exchanged(["wgu0"], res_a)
    red.broadcasted(names, res_b)
    red.grad["wo"] = dw_o.reshape(N_CHIPS, Q_WIDTH // N_CHIPS, D_MODEL)
    (dq, dkc, dkp, dvc, dvp, dsink), (res_a, res_b) = attn_bwd(
        qkv, sink_rows, do, name="attn_bwd", riders=[red.scatter(["wgu0"]), red.exchange(["wo"])])
    red.scattered(["wgu0"], res_a, where)
    red.exchanged(["wo"], res_b)
    names, rider = red.broadcast([("wgu", 0)])
    (dqkv, db_qkv), (res_a, res_b) = rope_bwd(dq, dkc, dkp, dvc, dvp, cos, sin, name="rope_bwd",
                                              riders=[rider, red.scatter(["wo"])])
    red.broadcasted(names, res_a)
    red.scattered(["wo"], res_b, where)
    names, rider = red.broadcast([("wo", None)])
    red.grad["qkv"], (res,) = mm_tn(h0, dqkv, shard_major=True, tm=D_MODEL, tn=QKV_WIDTH // N_CHIPS, name="dw_qkv",
                                    riders=[rider])
    red.broadcasted(names, res)
    dh0, (res,) = mm_nt(dqkv, w_qkv, out_dtype=F32, tm=1024, name="dh_attn", riders=[red.exchange(["qkv"])])
    red.exchanged(["qkv"], res)
    grad_x, dg_mpre0 = norm_bwd_last(dx1, dh0, x0, gain(norm_mix_pre, 0), name="norm_bwd_in")

    norm_grads = [jnp.concatenate(p, axis=0) for p in
                  ((dg_mpre0, dg_mpre1), (dg_mpost0, dg_mpost1), (dg_fpre0, dg_fpre1), (dg_fpost0, dg_fpost1))]
    red.grad["small"] = _pack_small(norm_grads, db_qkv, db_o, dsink[:, :, 0, 0], db_sp[:, :, 0], dln_g, dln_b, dw_sp,
                                    loss_part)
    res_a, res_b = comm_call([red.scatter(["qkv"]), red.exchange(["small"])], name="tail_1")
    red.scattered(["qkv"], res_a, where)
    red.exchanged(["small"], res_b)
    names, rider = red.broadcast([("qkv", None)])
    res_a, res_b = comm_call([red.scatter(["small"]), rider], name="tail_2")
    red.scattered(["small"], res_a, where)
    red.broadcasted(names, res_b)
    ((slab_full,),) = comm_call([allcast_rider(red.dest["slab"])], name="tail_3")
    g_qkv, g_wo, g_win, g_wout, g_wgu, g_wd = (red.dest[n] for n in ("qkv", "wo", "win", "wout", "wgu", "wd"))
    g_norms, g_bqkv, g_bo, g_sinks, g_bsp, g_lng, g_lnb, g_wsp, loss = _unpack_small(slab_full, chip)

    def big_update(w, g, m, v, tag):
        return adamw(w, g.reshape(w.shape), m, v, name=f"adamw_{tag}")

    upd = {
        "attn_w_qkv": big_update(attn_w_qkv, g_qkv, m_attn_w_qkv, v_attn_w_qkv, "qkv"),
        "attn_w_o": big_update(attn_w_o, g_wo, m_attn_w_o, v_attn_w_o, "wo"),
        "sgu_w_in": big_update(sgu_w_in, g_win, m_sgu_w_in, v_sgu_w_in, "win"),
        "sgu_w_out": big_update(sgu_w_out, g_wout, m_sgu_w_out, v_sgu_w_out, "wout"),
        "ffn_w_gate_up": big_update(ffn_w_gate_up, g_wgu, m_ffn_w_gate_up, v_ffn_w_gate_up, "wgu"),
        "ffn_w_down": big_update(ffn_w_down, g_wd, m_ffn_w_down, v_ffn_w_down, "wd"),
    }
    small_names = ["norm_mix_pre", "norm_mix_post", "norm_ffn_pre", "norm_ffn_post", "attn_b_qkv", "attn_sinks", "attn_b_o",
                   "sgu_ln_g", "sgu_ln_b", "sgu_w_spatial", "sgu_b_spatial"]
    small_w = [norm_mix_pre, norm_mix_post, norm_ffn_pre, norm_ffn_post, attn_b_qkv, attn_sinks, attn_b_o, sgu_ln_g, sgu_ln_b,
               sgu_w_spatial, sgu_b_spatial]
    small_m = [m_norm_mix_pre, m_norm_mix_post, m_norm_ffn_pre, m_norm_ffn_post, m_attn_b_qkv, m_attn_sinks, m_attn_b_o,
               m_sgu_ln_g, m_sgu_ln_b, m_sgu_w_spatial, m_sgu_b_spatial]
    small_v = [v_norm_mix_pre, v_norm_mix_post, v_norm_ffn_pre, v_norm_ffn_post, v_attn_b_qkv, v_attn_sinks, v_attn_b_o,
               v_sgu_ln_g, v_sgu_ln_b, v_sgu_w_spatial, v_sgu_b_spatial]
    small_g = g_norms + [g_bqkv, g_sinks, g_bo, g_lng, g_lnb, g_wsp, g_bsp]

    def flat2(a):
        return a.reshape(-1, a.shape[-1])

    res = adamw_small([flat2(a) for a in small_w], [flat2(a) for a in small_g], [flat2(a) for a in small_m],
                      [flat2(a) for a in small_v], name="adamw_small")
    for i, nm in enumerate(small_names):
        upd[nm] = tuple(r[i].reshape(small_w[i].shape) for r in res)

    order = ["norm_mix_pre", "norm_mix_post", "norm_ffn_pre", "norm_ffn_post", "attn_w_qkv", "attn_b_qkv", "attn_sinks",
             "attn_w_o", "attn_b_o", "sgu_w_in", "sgu_ln_g", "sgu_ln_b", "sgu_w_spatial", "sgu_b_spatial", "sgu_w_out",
             "ffn_w_gate_up", "ffn_w_down"]
    outs = [loss, grad_x.reshape(1, s, D_MODEL)]
    for part in range(4):
        outs += [upd[nm][part] for nm in order]
    return tuple(outs)
```

```python
import types

import numpy as np
import jax
import jax.numpy as jnp
from jax import lax
from jax.experimental import pallas as pl
from jax.experimental.pallas import tpu as pltpu

F32 = jnp.float32
BF16 = jnp.bfloat16
I32 = jnp.int32

D_MODEL = 1024
HEAD_DIM = 64
N_Q_HEADS = 16
N_KV_HEADS = 4
GQA_GROUP = 4
WINDOW = 128
Q_WIDTH = 1024
KV_WIDTH = 256
QKV_WIDTH = 1536
ROPE_THETA = 10000.0
SGU_GROUPS = 8
SGU_CHUNK = 128
D_FF = 2816
FF_HALF = D_FF // 2
EPS = 1e-6
N_CHIPS = 4
LANES = 128

ADAM_LR = 0.001
ADAM_B1 = 0.9
ADAM_B2 = 0.999
ADAM_EPS = 1e-08
ADAM_WD = 0.01
ADAM_STEP = 10

VMEM_LIMIT = 52 * 1024 * 1024
MESH = pl.DeviceIdType.MESH
NEG = -1e30
NT_DIMS = (((1,), (1,)), ((), ()))
TN_DIMS = (((0,), (0,)), ((), ()))
NN_DIMS = (((1,), (0,)), ((), ()))
ANY = pl.BlockSpec(memory_space=pl.ANY)


def _row_tile(s, want):
    return want if s % want == 0 else s


PEER_KINDS = ("sibling", "chips", "sibling+chips", "everyone")


def _peer_kind(riders):
    kinds = {r.peers for r in riders}
    if not kinds:
        return None
    if "everyone" in kinds:
        return "everyone"
    return "sibling+chips" if len(kinds) == 2 else kinds.pop()


def _peer_barrier(kind):
    x, y, c = _place()
    chips = [(*_partner(x, y, k), c) for k in (1, 2, 3)]
    peers = {"sibling": [(x, y, 1 - c)], "chips": chips, "sibling+chips": [(x, y, 1 - c)] + chips,
             "everyone": [(x, y, 1 - c)] + chips + [(px, py, 1 - c) for px, py, _ in chips]}[kind]
    barrier = pltpu.get_barrier_semaphore()
    for dev in peers:
        pl.semaphore_signal(barrier, inc=1, device_id=dev, device_id_type=MESH)
    pl.semaphore_wait(barrier, len(peers))


def _call(body, *, name, grid=(), in_specs=(), out_specs=(), out_shape=(), scratch_shapes=(), operands=(), prefetch=(),
          aliases=None, riders=(), sem=None):
    n_pre, n_in, n_out, n_scr = len(prefetch), len(operands), len(out_shape), len(scratch_shapes)
    in_specs, out_specs, out_shape = list(in_specs), list(out_specs), list(out_shape)
    operands, scratch_shapes = list(operands), list(scratch_shapes)
    io_alias = {n_pre + i: o for i, o in (aliases or {}).items()}
    for r in riders:
        base_in, base_out = n_pre + len(operands), len(out_shape)
        operands += list(r.inputs)
        in_specs += [ANY] * len(r.inputs)
        for pos, i in enumerate(r.aliased):
            io_alias[base_in + i] = base_out + pos
            out_shape.append(jax.ShapeDtypeStruct(r.inputs[i].shape, r.inputs[i].dtype))
        out_shape += list(r.fresh)
        out_specs += [ANY] * (len(r.aliased) + len(r.fresh))
        scratch_shapes += [pltpu.SemaphoreType.DMA((r.nsem,)), pltpu.SemaphoreType.DMA((r.nsem,))]

    def wrapped(*refs):
        pre, p = refs[:n_pre], n_pre
        core_in, p = refs[p:p + n_in], p + n_in
        r_in = []
        for r in riders:
            r_in.append(refs[p:p + len(r.inputs)])
            p += len(r.inputs)
        core_out, p = refs[p:p + n_out], p + n_out
        r_out = []
        for r in riders:
            k = len(r.aliased) + len(r.fresh)
            r_out.append(refs[p:p + k])
            p += k
        core_scr, p = refs[p:p + n_scr], p + n_scr
        r_sem = [refs[p + 2 * i:p + 2 * i + 2] for i in range(len(riders))]

        def edge(at_last, fns):
            def run():
                if not at_last:
                    _peer_barrier(peer_kind)
                for i, r in enumerate(riders):
                    getattr(r, fns)(r_in[i], r_out[i], r_sem[i][0], r_sem[i][1])
            if not riders:
                return
            if not grid:
                run()
                return
            cond = None
            for d, n in enumerate(grid):
                c = pl.program_id(d) == (n - 1 if at_last else 0)
                cond = c if cond is None else jnp.logical_and(cond, c)
            pl.when(cond)(run)

        edge(False, "start")
        if body is not None:
            body(*pre, *core_in, *core_out, *core_scr)
        edge(True, "finish")

    if sem is None or riders:
        sem = ("arbitrary",) * len(grid)
    kwargs = dict(out_shape=out_shape, input_output_aliases=io_alias, name=name)
    peer_kind = _peer_kind(riders)
    collective = {} if peer_kind is None else {"collective_id": PEER_KINDS.index(peer_kind)}
    if grid:
        kwargs["compiler_params"] = pltpu.CompilerParams(dimension_semantics=sem, vmem_limit_bytes=VMEM_LIMIT, **collective)
    elif collective:
        kwargs["compiler_params"] = pltpu.CompilerParams(**collective)
    if n_pre:
        kwargs["grid_spec"] = pltpu.PrefetchScalarGridSpec(
            num_scalar_prefetch=n_pre, grid=grid, in_specs=in_specs, out_specs=out_specs, scratch_shapes=scratch_shapes)
    else:
        kwargs.update(grid=grid, in_specs=in_specs, out_specs=out_specs, scratch_shapes=scratch_shapes)
    res = pl.pallas_call(wrapped, **kwargs)(*prefetch, *operands)
    core, rest, rider_res = list(res[:n_out]), list(res[n_out:]), []
    for r in riders:
        k = len(r.aliased) + len(r.fresh)
        rider_res.append(rest[:k])
        rest = rest[k:]
    return core, rider_res


def _mm_call(*, grid, in_specs, out_spec, out_shape, dims, nk, kaxis, acc_shape, name, operands, riders=()):
    out_dtype = out_shape.dtype

    def body(a_ref, b_ref, o_ref, *scratch):
        p = lax.dot_general(a_ref[...].astype(BF16), b_ref[...].astype(BF16), dims, preferred_element_type=F32)
        if nk == 1:
            o_ref[...] = p.astype(out_dtype)
        else:
            acc = scratch[0]
            kk = pl.program_id(kaxis)

            @pl.when(kk == 0)
            def _():
                acc[...] = p

            @pl.when(kk > 0)
            def _():
                acc[...] += p

            @pl.when(kk == nk - 1)
            def _():
                o_ref[...] = acc[...].astype(out_dtype)

    sem = ["parallel"] * len(grid)
    if nk > 1:
        sem[kaxis] = "arbitrary"
    (out,), rider_res = _call(
        body, grid=grid, in_specs=in_specs, out_specs=[out_spec], out_shape=[out_shape],
        scratch_shapes=[pltpu.VMEM(acc_shape, F32)] if nk > 1 else [], operands=operands, name=name, riders=riders,
        sem=tuple(sem))
    return (out, rider_res) if riders else out


def mm_nn(a, w, *, out_dtype, name, tm=512, tn=512, riders=()):
    m, k = a.shape
    tm = _row_tile(m, tm)
    if w.ndim == 3:
        ns = w.shape[2]
        grid = (N_CHIPS, m // tm)
        w_spec = pl.BlockSpec((None, k, ns), lambda j, i: (j, 0, 0))
        o_spec = pl.BlockSpec((tm, ns), lambda j, i: (i, j))
        n = N_CHIPS * ns
    else:
        n = w.shape[1]
        grid = (n // tn, m // tm)
        w_spec = pl.BlockSpec((k, tn), lambda j, i: (0, j))
        o_spec = pl.BlockSpec((tm, tn), lambda j, i: (i, j))
    return _mm_call(grid=grid, in_specs=[pl.BlockSpec((tm, k), lambda j, i: (i, 0)), w_spec], out_spec=o_spec,
                    out_shape=jax.ShapeDtypeStruct((m, n), out_dtype), dims=NN_DIMS, nk=1, kaxis=0, acc_shape=None,
                    name=name, operands=(a, w), riders=riders)


def mm_nt(a, w, *, out_dtype, name, tm=512, tn=512, riders=()):
    if w.ndim == 2:
        m, n = a.shape
        kout = w.shape[0]
        tm = _row_tile(m, tm)
        return _mm_call(grid=(kout // tn, m // tm),
                        in_specs=[pl.BlockSpec((tm, n), lambda j, i: (i, 0)), pl.BlockSpec((tn, n), lambda j, i: (j, 0))],
                        out_spec=pl.BlockSpec((tm, tn), lambda j, i: (i, j)),
                        out_shape=jax.ShapeDtypeStruct((m, kout), out_dtype), dims=NT_DIMS, nk=1, kaxis=0,
                        acc_shape=None, name=name, operands=(a, w), riders=riders)
    _, kout, ns = w.shape
    planes = a.ndim == 3
    m = a.shape[1] if planes else a.shape[0]
    tm = _row_tile(m, tm)
    a_spec = pl.BlockSpec((2, tm, 2 * ns), lambda i: (0, i, 0)) if planes else pl.BlockSpec((tm, N_CHIPS * ns), lambda i: (i, 0))

    def body(a_ref, w0, w1, w2, w3, o_ref):
        acc = None
        for j, w_ref in enumerate((w0, w1, w2, w3)):
            if planes:
                a_j = a_ref[j // 2, :, (j % 2) * ns:(j % 2 + 1) * ns]
            else:
                a_j = a_ref[:, j * ns:(j + 1) * ns]
            p = lax.dot_general(a_j, w_ref[...], NT_DIMS, preferred_element_type=F32)
            acc = p if acc is None else acc + p
        o_ref[...] = acc.astype(out_dtype)

    def shard(j):
        return pl.BlockSpec((None, kout, ns), lambda i: (j, 0, 0))

    (out,), rider_res = _call(
        body, grid=(m // tm,), in_specs=[a_spec] + [shard(j) for j in range(N_CHIPS)],
        out_specs=[pl.BlockSpec((tm, kout), lambda i: (i, 0))], out_shape=[jax.ShapeDtypeStruct((m, kout), out_dtype)],
        operands=(a, w, w, w, w), sem=("parallel",), name=name, riders=riders)
    return (out, rider_res) if riders else out


def mm_tn(a, b, *, shard_major, name, tm, tn, tk=None, out_dtype=BF16, riders=()):
    s, m = a.shape
    tk = s if tk is None else _row_tile(s, tk)
    if b.ndim == 3:
        n = 2 * b.shape[2]
        b_spec = pl.BlockSpec((None, tk, tn), lambda j, i, kk: (j // 2, kk, j % 2))
    else:
        n = b.shape[1]
        b_spec = pl.BlockSpec((tk, tn), lambda j, i, kk: (kk, j))
    if shard_major:
        assert tn == n // N_CHIPS
        o_spec = pl.BlockSpec((None, tm, tn), lambda j, i, kk: (j, i, 0))
        o_shape = jax.ShapeDtypeStruct((N_CHIPS, m, tn), out_dtype)
    else:
        o_spec = pl.BlockSpec((tm, tn), lambda j, i, kk: (i, j))
        o_shape = jax.ShapeDtypeStruct((m, n), out_dtype)
    return _mm_call(grid=(n // tn, m // tm, s // tk),
                    in_specs=[pl.BlockSpec((tk, tm), lambda j, i, kk: (kk, i)), b_spec], out_spec=o_spec,
                    out_shape=o_shape, dims=TN_DIMS, nk=s // tk, kaxis=2, acc_shape=(tm, tn), name=name, operands=(a, b),
                    riders=riders)


def _rstd(x):
    return lax.rsqrt(jnp.mean(x * x, axis=-1, keepdims=True) + EPS)


def _rms_bwd(dy, x, g):
    r = _rstd(x)
    xhat = x * r
    gy = dy * g
    dx = r * (gy - xhat * jnp.mean(gy * xhat, axis=-1, keepdims=True))
    return dx, jnp.sum(dy * xhat, axis=0, keepdims=True)


def _accum(ref, val, first):
    @pl.when(first)
    def _():
        ref[...] = val

    @pl.when(jnp.logical_not(first))
    def _():
        ref[...] += val


def _row_spec(tm, width):
    return pl.BlockSpec((tm, width), lambda i: (i, 0))


def _vec_spec(width):
    return pl.BlockSpec((1, width), lambda i: (0, 0))


def _ret(core, rider_res, riders):
    core = core[0] if len(core) == 1 else core
    return (core, rider_res) if riders else core


def prenorm(x, g, *, name, tm=256, riders=()):
    s = x.shape[0]
    tm = _row_tile(s, tm)

    def body(x_ref, g_ref, h_ref):
        xv = x_ref[...]
        h_ref[...] = (xv * _rstd(xv) * g_ref[...]).astype(BF16)

    core, rr = _call(
        body, grid=(s // tm,), in_specs=[_row_spec(tm, D_MODEL), _vec_spec(D_MODEL)], out_specs=[_row_spec(tm, D_MODEL)],
        out_shape=[jax.ShapeDtypeStruct((s, D_MODEL), BF16)], operands=(x, g), sem=("parallel",), name=name, riders=riders)
    return _ret(core, rr, riders)


def proj_residual_norm(a, w, x, bias, g_post, g_next, *, name, tm=256, riders=()):
    s, k = a.shape
    tm = _row_tile(s, tm)

    def body(a_ref, w_ref, x_ref, b_ref, gp_ref, gn_ref, xo_ref, h_ref, m_ref):
        mv = jnp.dot(a_ref[...], w_ref[...], preferred_element_type=F32) + b_ref[...]
        m_ref[...] = mv.astype(BF16)
        xn = x_ref[...] + mv * _rstd(mv) * gp_ref[...]
        xo_ref[...] = xn
        h_ref[...] = (xn * _rstd(xn) * gn_ref[...]).astype(BF16)

    row, vec = _row_spec(tm, D_MODEL), _vec_spec(D_MODEL)
    core, rr = _call(
        body, grid=(s // tm,),
        in_specs=[_row_spec(tm, k), pl.BlockSpec((k, D_MODEL), lambda i: (0, 0)), row, vec, vec, vec], out_specs=[row, row, row],
        out_shape=[jax.ShapeDtypeStruct((s, D_MODEL), F32), jax.ShapeDtypeStruct((s, D_MODEL), BF16),
                   jax.ShapeDtypeStruct((s, D_MODEL), BF16)],
        operands=(a, w, x, bias, g_post, g_next), sem=("parallel",), name=name, riders=riders)
    return _ret(core, rr, riders)


def proj_loss_head(a, w, x, g_post, target, *, name, tm=256, riders=()):
    s, k = a.shape
    tm = _row_tile(s, tm)

    def body(a_ref, w_ref, x_ref, g_ref, t_ref, dx_ref, df_ref, dg_ref, loss_ref):
        first = pl.program_id(0) == 0
        fv = jnp.dot(a_ref[...], w_ref[...], preferred_element_type=F32)
        g = g_ref[...]
        err = x_ref[...] + fv * _rstd(fv) * g - t_ref[...]
        dx = err * (1.0 / D_MODEL)
        dx_ref[...] = dx
        df, dg = _rms_bwd(dx, fv, g)
        df_ref[...] = df.astype(BF16)
        _accum(dg_ref, dg, first)
        part = jnp.sum(jnp.sum(err * err, axis=-1, keepdims=True), axis=0, keepdims=True) * (0.5 / D_MODEL)
        _accum(loss_ref, jnp.broadcast_to(part, (8, LANES)), first)

    row, vec = _row_spec(tm, D_MODEL), _vec_spec(D_MODEL)
    core, rr = _call(
        body, grid=(s // tm,), in_specs=[_row_spec(tm, k), pl.BlockSpec((k, D_MODEL), lambda i: (0, 0)), row, vec, row],
        out_specs=[row, row, vec, pl.BlockSpec((8, LANES), lambda i: (0, 0))],
        out_shape=[jax.ShapeDtypeStruct((s, D_MODEL), F32), jax.ShapeDtypeStruct((s, D_MODEL), BF16),
                   jax.ShapeDtypeStruct((1, D_MODEL), F32), jax.ShapeDtypeStruct((8, LANES), F32)],
        operands=(a, w, x, g_post, target), name=name, riders=riders)
    return _ret(core, rr, riders)


def ffn_fwd_loss_rows(h, w_gu, w_d, x, g_post, target, *, name, tm=256, riders=()):
    s = x.shape[0]
    tm = _row_tile(s, tm)

    def body(h_ref, w0, w1, w2, w3, wd_ref, x_ref, g_ref, t_ref, d_ref, a_ref, dx_ref, df_ref, dg_ref, loss_ref):
        first = pl.program_id(0) == 0
        hv = h_ref[...]
        fv = None
        for half, (wg_ref, wu_ref) in enumerate(((w0, w2), (w1, w3))):
            cols = slice(half * FF_HALF, (half + 1) * FF_HALF)
            g = jnp.dot(hv, wg_ref[...], preferred_element_type=F32)
            u = jnp.dot(hv, wu_ref[...], preferred_element_type=F32)
            sig = _sigmoid(g)
            silu = g * sig
            d_ref[0, :, cols] = (u * (sig + silu * (1.0 - sig))).astype(BF16)
            d_ref[1, :, cols] = silu.astype(BF16)
            act = (silu * u).astype(BF16)
            a_ref[:, cols] = act
            p = jnp.dot(act, wd_ref[cols, :], preferred_element_type=F32)
            fv = p if fv is None else fv + p
        gain = g_ref[...]
        err = x_ref[...] + fv * _rstd(fv) * gain - t_ref[...]
        dx = err * (1.0 / D_MODEL)
        dx_ref[...] = dx
        df, dg = _rms_bwd(dx, fv, gain)
        df_ref[...] = df.astype(BF16)
        _accum(dg_ref, dg, first)
        part = jnp.sum(jnp.sum(err * err, axis=-1, keepdims=True), axis=0, keepdims=True) * (0.5 / D_MODEL)
        _accum(loss_ref, jnp.broadcast_to(part, (8, LANES)), first)

    def resident(shape, index):
        return pl.BlockSpec(shape, index, pipeline_mode=pl.Buffered(1))

    row, vec = _row_spec(tm, D_MODEL), _vec_spec(D_MODEL)
    shards = [resident((None, D_MODEL, FF_HALF), (lambda j: (lambda i: (j, 0, 0)))(j)) for j in range(N_CHIPS)]
    core, rr = _call(
        body, grid=(s // tm,),
        in_specs=[row] + shards + [resident((D_FF, D_MODEL), lambda i: (0, 0)), row, vec, row],
        out_specs=[pl.BlockSpec((2, tm, D_FF), lambda i: (0, i, 0)), _row_spec(tm, D_FF), row, row, vec,
                   pl.BlockSpec((8, LANES), lambda i: (0, 0))],
        out_shape=[jax.ShapeDtypeStruct((2, s, D_FF), BF16), jax.ShapeDtypeStruct((s, D_FF), BF16),
                   jax.ShapeDtypeStruct((s, D_MODEL), F32), jax.ShapeDtypeStruct((s, D_MODEL), BF16),
                   jax.ShapeDtypeStruct((1, D_MODEL), F32), jax.ShapeDtypeStruct((8, LANES), F32)],
        operands=(h, w_gu, w_gu, w_gu, w_gu, w_d, x, g_post, target), name=name, riders=riders)
    return _ret(core, rr, riders)


def dh_norm_bwd_pair(a, w, dres, x, g_pre, m, g_post, *, name, tm=512, sub=256, riders=()):
    _, kout, ns = w.shape
    planes = a.ndim == 3
    s = x.shape[0]
    tm = _row_tile(s, tm)
    sub = min(sub, tm)
    a_spec = pl.BlockSpec((2, tm, 2 * ns), lambda i: (0, i, 0)) if planes else pl.BlockSpec((tm, N_CHIPS * ns), lambda i: (i, 0))

    def body(a_ref, w0, w1, w2, w3, dres_ref, x_ref, gpre_ref, m_ref, gpost_ref, dx_ref, dm_ref, dgpre_ref, dgpost_ref, db_ref):
        first = pl.program_id(0) == 0
        sums = None
        for t in range(tm // sub):
            rows = slice(t * sub, (t + 1) * sub)
            dh = None
            for j, w_ref in enumerate((w0, w1, w2, w3)):
                a_j = a_ref[j // 2, rows, (j % 2) * ns:(j % 2 + 1) * ns] if planes else a_ref[rows, j * ns:(j + 1) * ns]
                p = lax.dot_general(a_j, w_ref[...], NT_DIMS, preferred_element_type=F32)
                dh = p if dh is None else dh + p
            d1, dgpre = _rms_bwd(dh, x_ref[rows, :], gpre_ref[...])
            dx = dres_ref[rows, :] + d1
            dx_ref[rows, :] = dx
            dm, dgpost = _rms_bwd(dx, m_ref[rows, :].astype(F32), gpost_ref[...])
            dm_ref[rows, :] = dm.astype(BF16)
            part = (dgpre, dgpost, jnp.sum(dm, axis=0, keepdims=True))
            sums = part if sums is None else tuple(u + v for u, v in zip(sums, part))
        _accum(dgpre_ref, sums[0], first)
        _accum(dgpost_ref, sums[1], first)
        _accum(db_ref, sums[2], first)

    def shard(j):
        return pl.BlockSpec((None, kout, ns), lambda i: (j, 0, 0))

    row, vec = _row_spec(tm, D_MODEL), _vec_spec(D_MODEL)
    vshape = jax.ShapeDtypeStruct((1, D_MODEL), F32)
    core, rr = _call(
        body, grid=(s // tm,), in_specs=[a_spec] + [shard(j) for j in range(N_CHIPS)] + [row, row, vec, row, vec],
        out_specs=[row, row, vec, vec, vec],
        out_shape=[jax.ShapeDtypeStruct((s, D_MODEL), F32), jax.ShapeDtypeStruct((s, D_MODEL), BF16), vshape, vshape, vshape],
        operands=(a, w, w, w, w, dres, x, g_pre, m, g_post), name=name, riders=riders)
    return _ret(core, rr, riders)


def ffn_bwd_rows(df, w_d, d_planes, w_gu, dres, x, g_pre, m, g_post, *, name, tm=256, riders=()):
    s = x.shape[0]
    tm = _row_tile(s, tm)

    def body(df_ref, wd_ref, d_ref, w0, w1, w2, w3, dres_ref, x_ref, gpre_ref, m_ref, gpost_ref,
             o_ref, dx_ref, dm_ref, dgpre_ref, dgpost_ref, db_ref):
        first = pl.program_id(0) == 0
        dfv = df_ref[...]
        dh = None
        for half, (wg_ref, wu_ref) in enumerate(((w0, w2), (w1, w3))):
            cols = slice(half * FF_HALF, (half + 1) * FF_HALF)
            da = lax.dot_general(dfv, wd_ref[cols, :], NT_DIMS, preferred_element_type=F32)
            dg = (da * d_ref[0, :, cols].astype(F32)).astype(BF16)
            du = (da * d_ref[1, :, cols].astype(F32)).astype(BF16)
            o_ref[0, :, cols] = dg
            o_ref[1, :, cols] = du
            p = lax.dot_general(dg, wg_ref[...], NT_DIMS, preferred_element_type=F32)
            p += lax.dot_general(du, wu_ref[...], NT_DIMS, preferred_element_type=F32)
            dh = p if dh is None else dh + p
        d1, dgpre = _rms_bwd(dh, x_ref[...], gpre_ref[...])
        dx = dres_ref[...] + d1
        dx_ref[...] = dx
        dm, dgpost = _rms_bwd(dx, m_ref[...].astype(F32), gpost_ref[...])
        dm_ref[...] = dm.astype(BF16)
        _accum(dgpre_ref, dgpre, first)
        _accum(dgpost_ref, dgpost, first)
        _accum(db_ref, jnp.sum(dm, axis=0, keepdims=True), first)

    def resident(shape, index):
        return pl.BlockSpec(shape, index, pipeline_mode=pl.Buffered(1))

    planes = pl.BlockSpec((2, tm, D_FF), lambda i: (0, i, 0))
    row, vec = _row_spec(tm, D_MODEL), _vec_spec(D_MODEL)
    vshape = jax.ShapeDtypeStruct((1, D_MODEL), F32)
    shards = [resident((None, D_MODEL, FF_HALF), (lambda j: (lambda i: (j, 0, 0)))(j)) for j in range(N_CHIPS)]
    core, rr = _call(
        body, grid=(s // tm,),
        in_specs=[row, resident((D_FF, D_MODEL), lambda i: (0, 0)), planes] + shards + [row, row, vec, row, vec],
        out_specs=[planes, row, row, vec, vec, vec],
        out_shape=[jax.ShapeDtypeStruct((2, s, D_FF), BF16), jax.ShapeDtypeStruct((s, D_MODEL), F32),
                   jax.ShapeDtypeStruct((s, D_MODEL), BF16), vshape, vshape, vshape],
        operands=(df, w_d, d_planes, w_gu, w_gu, w_gu, w_gu, dres, x, g_pre, m, g_post), name=name, riders=riders)
    return _ret(core, rr, riders)


def dh_norm_bwd_last(a, w, dres, x, g_pre, *, name, tm=512, sub=256):
    _, kout, ns = w.shape
    s = x.shape[0]
    tm = _row_tile(s, tm)
    sub = min(sub, tm)

    def body(a_ref, w0, w1, w2, w3, dres_ref, x_ref, g_ref, dx_ref, dg_ref):
        total = None
        for t in range(tm // sub):
            rows = slice(t * sub, (t + 1) * sub)
            dh = None
            for j, w_ref in enumerate((w0, w1, w2, w3)):
                p = lax.dot_general(a_ref[rows, j * ns:(j + 1) * ns], w_ref[...], NT_DIMS, preferred_element_type=F32)
                dh = p if dh is None else dh + p
            d1, dg = _rms_bwd(dh, x_ref[rows, :], g_ref[...])
            dx_ref[rows, :] = dres_ref[rows, :] + d1
            total = dg if total is None else total + dg
        _accum(dg_ref, total, pl.program_id(0) == 0)

    def shard(j):
        return pl.BlockSpec((None, kout, ns), lambda i: (j, 0, 0))

    row, vec = _row_spec(tm, D_MODEL), _vec_spec(D_MODEL)
    (dx, dg), _ = _call(
        body, grid=(s // tm,), in_specs=[_row_spec(tm, N_CHIPS * ns)] + [shard(j) for j in range(N_CHIPS)] + [row, row, vec],
        out_specs=[row, vec], out_shape=[jax.ShapeDtypeStruct((s, D_MODEL), F32), jax.ShapeDtypeStruct((1, D_MODEL), F32)],
        operands=(a, w, w, w, w, dres, x, g_pre), name=name)
    return dx, dg


def _rope_tables(s):
    half = HEAD_DIM // 2
    inv_freq = np.float32(ROPE_THETA) ** (-(np.arange(half, dtype=np.float32) * np.float32(2.0)) / np.float32(HEAD_DIM))
    ang = np.arange(s, dtype=np.float32)[:, None] * inv_freq[None, :]
    cos, sin = np.cos(ang).astype(np.float32), np.sin(ang).astype(np.float32)
    return jnp.asarray(np.tile(cos, (1, 4))), jnp.asarray(np.concatenate([-sin, sin, -sin, sin], axis=1))


def _swap_halves(x):
    lane = lax.broadcasted_iota(I32, x.shape, 1)
    return jnp.where((lane & (HEAD_DIM - 1)) < HEAD_DIM // 2, pltpu.roll(x, LANES - 32, 1), pltpu.roll(x, 32, 1))


N_ROPE_BLOCKS = (Q_WIDTH + KV_WIDTH) // LANES


def qkv_proj(h, w, bias, cos, sin, *, name, tm=512, riders=()):
    s, k = h.shape
    ns = w.shape[2]
    tm = _row_tile(s, tm)

    def body(h_ref, w_ref, b_ref, c_ref, s_ref, o_ref):
        j = pl.program_id(0)
        sub = min(256, tm)
        for t in range(tm // sub):
            rows = slice(t * sub, (t + 1) * sub)
            p = jnp.dot(h_ref[rows, :], w_ref[...], preferred_element_type=F32) + b_ref[...]
            cosv, sinv = c_ref[rows, :], s_ref[rows, :]
            for blk in range(ns // LANES):
                xb = p[:, blk * LANES:(blk + 1) * LANES]
                roped = xb * cosv + _swap_halves(xb) * sinv
                is_qk = j * (ns // LANES) + blk < N_ROPE_BLOCKS
                o_ref[rows, blk * LANES:(blk + 1) * LANES] = jnp.where(is_qk, roped, xb).astype(BF16)

    core, rr = _call(
        body, grid=(N_CHIPS, s // tm),
        in_specs=[pl.BlockSpec((tm, k), lambda j, i: (i, 0)), pl.BlockSpec((None, k, ns), lambda j, i: (j, 0, 0)),
                  pl.BlockSpec((1, ns), lambda j, i: (0, j)), pl.BlockSpec((tm, LANES), lambda j, i: (i, 0)),
                  pl.BlockSpec((tm, LANES), lambda j, i: (i, 0))],
        out_specs=[pl.BlockSpec((tm, ns), lambda j, i: (i, j))], out_shape=[jax.ShapeDtypeStruct((s, N_CHIPS * ns), BF16)],
        operands=(h, w, bias, cos, sin), sem=("parallel", "parallel"), name=name, riders=riders)
    return _ret(core, rr, riders)


def rope_bwd(dq, dkc, dkp, dvc, dvp, cos, sin, *, name, riders=()):
    s = dq.shape[0]
    tm = 2 * WINDOW if s % (2 * WINDOW) == 0 else WINDOW
    nb = s // tm

    def body(dq_ref, dkc_ref, dkp_ref, dkp_next_ref, dvc_ref, dvp_ref, dvp_next_ref, c_ref, s_ref, o_ref, db_ref):
        i = pl.program_id(0)
        has_next = (i < nb - 1).astype(F32)
        cosv, sinv = c_ref[...], s_ref[...]

        def shifted(ref, next_ref, cols):
            last = has_next * next_ref[:WINDOW, cols].astype(F32)
            return last if tm == WINDOW else jnp.concatenate([ref[WINDOW:, cols].astype(F32), last], axis=0)

        parts = []
        for blk in range(QKV_WIDTH // LANES):
            if blk < Q_WIDTH // LANES:
                g = dq_ref[:, blk * LANES:(blk + 1) * LANES].astype(F32)
            else:
                own, prv, nxt = (dkc_ref, dkp_ref, dkp_next_ref) if blk < N_ROPE_BLOCKS else (dvc_ref, dvp_ref, dvp_next_ref)
                cols = slice((blk % 2) * LANES, (blk % 2 + 1) * LANES)
                g = own[:, cols].astype(F32) + shifted(prv, nxt, cols)
            if blk < N_ROPE_BLOCKS:
                g = g * cosv + _swap_halves(g * sinv)
            o_ref[:, blk * LANES:(blk + 1) * LANES] = g.astype(BF16)
            parts.append(jnp.sum(g, axis=0, keepdims=True))
        sums = jnp.concatenate(parts, axis=1)
        _accum(db_ref, sums, i == 0)

    own_spec = _row_spec(tm, KV_WIDTH)
    next_spec = pl.BlockSpec((tm, KV_WIDTH), lambda i: (jnp.minimum(i + 1, nb - 1), 0))
    core, rr = _call(
        body, grid=(nb,),
        in_specs=[_row_spec(tm, Q_WIDTH), own_spec, own_spec, next_spec, own_spec, own_spec, next_spec,
                  _row_spec(tm, LANES), _row_spec(tm, LANES)],
        out_specs=[_row_spec(tm, QKV_WIDTH), _vec_spec(QKV_WIDTH)],
        out_shape=[jax.ShapeDtypeStruct((s, QKV_WIDTH), BF16), jax.ShapeDtypeStruct((1, QKV_WIDTH), F32)],
        operands=(dq, dkc, dkp, dkp, dvc, dvp, dvp, cos, sin), name=name, riders=riders)
    return _ret(core, rr, riders)


ROWS = GQA_GROUP * WINDOW


def _prev_slots():
    kpos = lax.broadcasted_iota(I32, (WINDOW, ROWS), 0)
    qpos = lax.broadcasted_iota(I32, (WINDOW, ROWS), 1) & (WINDOW - 1)
    return kpos > qpos


def _head_cols(ref, head):
    return ref[:, head * HEAD_DIM:(head + 1) * HEAD_DIM]


def _stack_heads(ref, h):
    return jnp.concatenate([_head_cols(ref, GQA_GROUP * h + g) for g in range(GQA_GROUP)], axis=0)


def _band(prev_ref, cur_ref, h):
    return jnp.concatenate([_head_cols(prev_ref, h), _head_cols(cur_ref, h)], axis=0)


def _pick(prev, band):
    return jnp.where(prev, band[:WINDOW], band[WINDOW:])


def _spread(prev, x):
    return jnp.concatenate([jnp.where(prev, x, 0.0), jnp.where(prev, 0.0, x)], axis=0).astype(BF16)


def _attn_probs(s_band, sink, prev, has_prev):
    scale = HEAD_DIM ** -0.5
    s = jnp.where(prev, jnp.where(has_prev, s_band[:WINDOW], NEG), s_band[WINDOW:]) * scale
    m = jnp.maximum(jnp.max(s, axis=0, keepdims=True), sink)
    e, es = jnp.exp(s - m), jnp.exp(sink - m)
    inv = 1.0 / (jnp.sum(e, axis=0, keepdims=True) + es)
    return e * inv, es * inv


def _attn_specs(nb):
    kcol, vcol = Q_WIDTH // KV_WIDTH, Q_WIDTH // KV_WIDTH + 1
    q_spec = pl.BlockSpec((WINDOW, Q_WIDTH), lambda n: (n, 0))
    return [q_spec,
            pl.BlockSpec((WINDOW, KV_WIDTH), lambda n: (n, kcol)),
            pl.BlockSpec((WINDOW, KV_WIDTH), lambda n: (jnp.maximum(n - 1, 0), kcol)),
            pl.BlockSpec((WINDOW, KV_WIDTH), lambda n: (n, vcol)),
            pl.BlockSpec((WINDOW, KV_WIDTH), lambda n: (jnp.maximum(n - 1, 0), vcol)),
            pl.BlockSpec((N_KV_HEADS, 8, ROWS), lambda n: (0, 0, 0))]


def attn_fwd(qkv, sink_rows, *, name, riders=()):
    s = qkv.shape[0]

    def body(q_ref, kc_ref, kp_ref, vc_ref, vp_ref, sink_ref, o_ref):
        prev = _prev_slots()
        has_prev = pl.program_id(0) > 0
        heads = range(N_KV_HEADS)
        s_bands = [lax.dot_general(_band(kp_ref, kc_ref, h), _stack_heads(q_ref, h), NT_DIMS, preferred_element_type=F32)
                   for h in heads]
        p_bands = [_spread(prev, _attn_probs(s_bands[h], sink_ref[h, 0:1, :], prev, has_prev)[0]) for h in heads]
        outs = [lax.dot_general(_band(vp_ref, vc_ref, h), p_bands[h], TN_DIMS, preferred_element_type=F32).T for h in heads]
        for h in heads:
            for g in range(GQA_GROUP):
                head = GQA_GROUP * h + g
                o_ref[:, head * HEAD_DIM:(head + 1) * HEAD_DIM] = outs[h][g * WINDOW:(g + 1) * WINDOW].astype(BF16)

    core, rr = _call(
        body, grid=(s // WINDOW,), in_specs=_attn_specs(s // WINDOW), out_specs=[pl.BlockSpec((WINDOW, Q_WIDTH), lambda n: (n, 0))],
        out_shape=[jax.ShapeDtypeStruct((s, Q_WIDTH), BF16)], operands=(qkv, qkv, qkv, qkv, qkv, sink_rows), sem=("parallel",),
        name=name, riders=riders)
    return _ret(core, rr, riders)


def attn_bwd(qkv, sink_rows, do, *, name, riders=()):
    s = qkv.shape[0]

    def body(q_ref, kc_ref, kp_ref, vc_ref, vp_ref, sink_ref, do_ref, dq_ref, dkc_ref, dkp_ref, dvc_ref, dvp_ref, dsink_ref):
        n = pl.program_id(0)
        prev = _prev_slots()
        scale = HEAD_DIM ** -0.5
        heads = range(N_KV_HEADS)
        qs, dos = [_stack_heads(q_ref, h) for h in heads], [_stack_heads(do_ref, h) for h in heads]
        kbands, vbands = [_band(kp_ref, kc_ref, h) for h in heads], [_band(vp_ref, vc_ref, h) for h in heads]
        s_bands = [lax.dot_general(kbands[h], qs[h], NT_DIMS, preferred_element_type=F32) for h in heads]
        dp_bands = [lax.dot_general(vbands[h], dos[h], NT_DIMS, preferred_element_type=F32) for h in heads]
        ds_bands, p_bands, parts = [], [], []
        for h in heads:
            p, ps = _attn_probs(s_bands[h], sink_ref[h, 0:1, :], prev, n > 0)
            dp = _pick(prev, dp_bands[h])
            delta = jnp.sum(p * dp, axis=0, keepdims=True)
            ds_bands.append(_spread(prev, p * (dp - delta) * scale))
            p_bands.append(_spread(prev, p))
            dsink = -(ps * delta)
            for g in range(GQA_GROUP):
                parts.append(jnp.broadcast_to(jnp.sum(dsink[:, g * WINDOW:(g + 1) * WINDOW], axis=1, keepdims=True), (8, LANES)))
        for h in heads:
            dk = jnp.dot(ds_bands[h], qs[h], preferred_element_type=F32).astype(BF16)
            dv = jnp.dot(p_bands[h], dos[h], preferred_element_type=F32).astype(BF16)
            dq = lax.dot_general(kbands[h], ds_bands[h], TN_DIMS, preferred_element_type=F32).T
            cols = slice(h * HEAD_DIM, (h + 1) * HEAD_DIM)
            dkp_ref[:, cols], dkc_ref[:, cols] = dk[:WINDOW], dk[WINDOW:]
            dvp_ref[:, cols], dvc_ref[:, cols] = dv[:WINDOW], dv[WINDOW:]
            for g in range(GQA_GROUP):
                head = GQA_GROUP * h + g
                dq_ref[:, head * HEAD_DIM:(head + 1) * HEAD_DIM] = dq[g * WINDOW:(g + 1) * WINDOW].astype(BF16)

        @pl.when(n == 0)
        def _():
            for i, part in enumerate(parts):
                dsink_ref[i // GQA_GROUP, i % GQA_GROUP] = part

        @pl.when(n > 0)
        def _():
            for i, part in enumerate(parts):
                dsink_ref[i // GQA_GROUP, i % GQA_GROUP] += part

    rows_q = pl.BlockSpec((WINDOW, Q_WIDTH), lambda n: (n, 0))
    rows_kv = pl.BlockSpec((WINDOW, KV_WIDTH), lambda n: (n, 0))
    kv_shape = jax.ShapeDtypeStruct((s, KV_WIDTH), BF16)
    core, rr = _call(
        body, grid=(s // WINDOW,), in_specs=_attn_specs(s // WINDOW) + [rows_q],
        out_specs=[rows_q, rows_kv, rows_kv, rows_kv, rows_kv,
                   pl.BlockSpec((N_KV_HEADS, GQA_GROUP, 8, LANES), lambda n: (0, 0, 0, 0))],
        out_shape=[jax.ShapeDtypeStruct((s, Q_WIDTH), BF16), kv_shape, kv_shape, kv_shape, kv_shape,
                   jax.ShapeDtypeStruct((N_KV_HEADS, GQA_GROUP, 8, LANES), F32)],
        operands=(qkv, qkv, qkv, qkv, qkv, sink_rows, do), sem=("arbitrary",), name=name, riders=riders)
    return _ret(core, rr, riders)


GELU_C = 0.7978845608028654
GELU_A = 0.044715


def _gelu(x):
    return 0.5 * x * (1.0 + jnp.tanh(x * (GELU_C + (GELU_C * GELU_A) * (x * x))))


def _gelu_and_grad(x):
    x2 = x * x
    t = jnp.tanh(x * (GELU_C + (GELU_C * GELU_A) * x2))
    half_x, one_t = 0.5 * x, 1.0 + t
    return half_x * one_t, 0.5 * one_t + half_x * (1.0 - t * t) * (GELU_C + (3.0 * GELU_C * GELU_A) * x2)


def _tril_bf16(w):
    row = lax.broadcasted_iota(I32, (SGU_CHUNK, SGU_CHUNK), 0)
    col = lax.broadcasted_iota(I32, (SGU_CHUNK, SGU_CHUNK), 1)
    return jnp.where(row >= col, w, 0.0).astype(BF16)


def _sgu_norm(vg, g, b):
    mu = jnp.mean(vg, axis=-1, keepdims=True)
    cen = vg - mu
    rstd = lax.rsqrt(jnp.mean(cen * cen, axis=-1, keepdims=True) + EPS)
    xhat = cen * rstd
    return xhat, rstd, xhat * g + b


def sgu_in_fwd(h, w_in, ln_g, ln_b, w_sp, b_sp, *, name, tm=256, riders=()):
    s, k = h.shape
    ns = w_in.shape[2]
    tm = _row_tile(s, tm)

    def body(h_ref, w0, w1, w2, w3, g_ref, b_ref, w_ref, bs_ref, z_ref, y_ref):
        hv = h_ref[...]
        zs = [jnp.dot(hv, w_ref_j[...], preferred_element_type=F32) for w_ref_j in (w0, w1, w2, w3)]
        for j, zj in enumerate(zs):
            z_ref[:, j * ns:(j + 1) * ns] = zj.astype(BF16)
        u = _gelu(jnp.concatenate(zs[:2], axis=1))
        _, _, vn = _sgu_norm(_gelu(jnp.concatenate(zs[2:], axis=1)), g_ref[...], b_ref[...])
        vn = vn.astype(BF16)
        for grp in range(SGU_GROUPS):
            w = _tril_bf16(w_ref[grp])
            cols = slice(grp * LANES, (grp + 1) * LANES)
            for ch in range(tm // SGU_CHUNK):
                rows = slice(ch * SGU_CHUNK, (ch + 1) * SGU_CHUNK)
                mixed = jnp.dot(w, vn[rows, cols], preferred_element_type=F32) + bs_ref[grp]
                y_ref[rows, cols] = (u[rows, cols] * mixed).astype(BF16)

    def shard(j):
        return pl.BlockSpec((None, k, ns), lambda i: (j, 0, 0))

    full3 = pl.BlockSpec((SGU_GROUPS, SGU_CHUNK, SGU_CHUNK), lambda i: (0, 0, 0))
    core, rr = _call(
        body, grid=(s // tm,),
        in_specs=[_row_spec(tm, k)] + [shard(j) for j in range(N_CHIPS)] + [_vec_spec(D_MODEL), _vec_spec(D_MODEL), full3, full3],
        out_specs=[_row_spec(tm, 2 * D_MODEL), _row_spec(tm, D_MODEL)],
        out_shape=[jax.ShapeDtypeStruct((s, 2 * D_MODEL), BF16), jax.ShapeDtypeStruct((s, D_MODEL), BF16)],
        operands=(h, w_in, w_in, w_in, w_in, ln_g, ln_b, w_sp, b_sp), sem=("parallel",), name=name, riders=riders)
    return _ret(core, rr, riders)


def sgu_bwd(z, dy, ln_g, ln_b, w_sp, b_sp, *, name, tm=256, riders=()):
    s = z.shape[0]
    tm = _row_tile(s, tm)

    def body(z_ref, dy_ref, g_ref, b_ref, w_ref, bs_ref, dz_ref, dw_ref, dbs_ref, dg_ref, db_ref, dvn_buf):
        first = pl.program_id(0) == 0
        u, u_grad = _gelu_and_grad(z_ref[:, :D_MODEL].astype(F32))
        vg, v_grad = _gelu_and_grad(z_ref[:, D_MODEL:].astype(F32))
        xhat, rstd, vn = _sgu_norm(vg, g_ref[...], b_ref[...])
        vn = vn.astype(BF16)
        dyv = dy_ref[...]
        dmixed = dyv * u
        dz_gate = dyv * u_grad
        row = lax.broadcasted_iota(I32, (SGU_CHUNK, SGU_CHUNK), 0)
        col = lax.broadcasted_iota(I32, (SGU_CHUNK, SGU_CHUNK), 1)
        dws, dbss = [], []
        for grp in range(SGU_GROUPS):
            w = _tril_bf16(w_ref[grp])
            cols = slice(grp * LANES, (grp + 1) * LANES)
            dw = jnp.zeros((SGU_CHUNK, SGU_CHUNK), F32)
            dbs = jnp.zeros((SGU_CHUNK, 1), F32)
            for ch in range(tm // SGU_CHUNK):
                rows = slice(ch * SGU_CHUNK, (ch + 1) * SGU_CHUNK)
                vblk = vn[rows, cols]
                mixed = jnp.dot(w, vblk, preferred_element_type=F32) + bs_ref[grp]
                dz_ref[rows, cols] = (dz_gate[rows, cols] * mixed).astype(BF16)
                dm = dmixed[rows, cols]
                dmb = dm.astype(BF16)
                dvn_buf[rows, cols] = lax.dot_general(w, dmb, TN_DIMS, preferred_element_type=F32)
                dw += lax.dot_general(dmb, vblk, NT_DIMS, preferred_element_type=F32)
                dbs += jnp.sum(dm, axis=-1, keepdims=True)
            dws.append(jnp.where(row >= col, dw, 0.0))
            dbss.append(jnp.broadcast_to(dbs, (SGU_CHUNK, SGU_CHUNK)))

        dvn = dvn_buf[...]
        dxhat = dvn * g_ref[...]
        dvg = rstd * (dxhat - jnp.mean(dxhat, axis=-1, keepdims=True) - xhat * jnp.mean(dxhat * xhat, axis=-1, keepdims=True))
        dz_ref[:, D_MODEL:] = (dvg * v_grad).astype(BF16)
        dlng, dlnb = jnp.sum(dvn * xhat, axis=0, keepdims=True), jnp.sum(dvn, axis=0, keepdims=True)

        @pl.when(first)
        def _():
            for grp in range(SGU_GROUPS):
                dw_ref[grp] = dws[grp]
                dbs_ref[grp] = dbss[grp]
            dg_ref[...] = dlng
            db_ref[...] = dlnb

        @pl.when(jnp.logical_not(first))
        def _():
            for grp in range(SGU_GROUPS):
                dw_ref[grp] += dws[grp]
                dbs_ref[grp] += dbss[grp]
            dg_ref[...] += dlng
            db_ref[...] += dlnb

    full3 = pl.BlockSpec((SGU_GROUPS, SGU_CHUNK, SGU_CHUNK), lambda i: (0, 0, 0))
    s3 = jax.ShapeDtypeStruct((SGU_GROUPS, SGU_CHUNK, SGU_CHUNK), F32)
    vshape = jax.ShapeDtypeStruct((1, D_MODEL), F32)
    core, rr = _call(
        body, grid=(s // tm,),
        in_specs=[_row_spec(tm, 2 * D_MODEL), _row_spec(tm, D_MODEL), _vec_spec(D_MODEL), _vec_spec(D_MODEL), full3, full3],
        out_specs=[_row_spec(tm, 2 * D_MODEL), full3, full3, _vec_spec(D_MODEL), _vec_spec(D_MODEL)],
        out_shape=[jax.ShapeDtypeStruct((s, 2 * D_MODEL), BF16), s3, s3, vshape, vshape],
        scratch_shapes=[pltpu.VMEM((tm, D_MODEL), F32)], operands=(z, dy, ln_g, ln_b, w_sp, b_sp), name=name, riders=riders)
    return _ret(core, rr, riders)


def _sigmoid(x):
    return 1.0 / (1.0 + jnp.exp(-x))


def ffn_up(h, w_gu, *, name, tm=512, riders=()):
    s = h.shape[0]
    tm = _row_tile(s, tm)

    def body(h_ref, wg_ref, wu_ref, d_ref, a_ref):
        hv = h_ref[...]
        sub = min(256, tm)
        for t in range(tm // sub):
            rows = slice(t * sub, (t + 1) * sub)
            g = jnp.dot(hv[rows], wg_ref[...], preferred_element_type=F32)
            u = jnp.dot(hv[rows], wu_ref[...], preferred_element_type=F32)
            sig = _sigmoid(g)
            silu = g * sig
            d_ref[0, rows, :] = (u * (sig + silu * (1.0 - sig))).astype(BF16)
            d_ref[1, rows, :] = silu.astype(BF16)
            a_ref[rows, :] = (silu * u).astype(BF16)

    core, rr = _call(
        body, grid=(2, s // tm),
        in_specs=[pl.BlockSpec((tm, D_MODEL), lambda j, i: (i, 0)),
                  pl.BlockSpec((None, D_MODEL, FF_HALF), lambda j, i: (j, 0, 0)),
                  pl.BlockSpec((None, D_MODEL, FF_HALF), lambda j, i: (j + 2, 0, 0))],
        out_specs=[pl.BlockSpec((2, tm, FF_HALF), lambda j, i: (0, i, j)), pl.BlockSpec((tm, FF_HALF), lambda j, i: (i, j))],
        out_shape=[jax.ShapeDtypeStruct((2, s, D_FF), BF16), jax.ShapeDtypeStruct((s, D_FF), BF16)],
        operands=(h, w_gu, w_gu), sem=("parallel", "parallel"), name=name, riders=riders)
    return _ret(core, rr, riders)


def ffn_dact(df, w_d, gu, *, name, tm=512, riders=()):
    s = df.shape[0]
    tm = _row_tile(s, tm)

    def body(df_ref, w_ref, d_ref, o_ref):
        da = lax.dot_general(df_ref[...], w_ref[...], NT_DIMS, preferred_element_type=F32)
        o_ref[0] = (da * d_ref[0].astype(F32)).astype(BF16)
        o_ref[1] = (da * d_ref[1].astype(F32)).astype(BF16)

    planes = pl.BlockSpec((2, tm, FF_HALF), lambda j, i: (0, i, j))
    core, rr = _call(
        body, grid=(2, s // tm),
        in_specs=[pl.BlockSpec((tm, D_MODEL), lambda j, i: (i, 0)), pl.BlockSpec((FF_HALF, D_MODEL), lambda j, i: (j, 0)), planes],
        out_specs=[planes], out_shape=[jax.ShapeDtypeStruct((2, s, D_FF), BF16)], operands=(df, w_d, gu),
        sem=("parallel", "parallel"), name=name, riders=riders)
    return _ret(core, rr, riders)


def _weight_tile(rows):
    for tr in (512, 352, 256, 128):
        if rows % tr == 0:
            return tr
    return rows


def place_shard(w, layer, chip_arr, dtype, *, name, riders=()):
    _, r, c = w.shape
    tr = _weight_tile(r)

    def body(chip_ref, w_ref, o_ref):
        o_ref[...] = w_ref[...].astype(dtype)

    core, rr = _call(
        body, grid=(r // tr,), prefetch=(chip_arr,),
        in_specs=[pl.BlockSpec((None, tr, c), lambda i, chip: (layer, i, 0))],
        out_specs=[pl.BlockSpec((None, tr, c), lambda i, chip: (chip[0], i, 0))],
        out_shape=[jax.ShapeDtypeStruct((N_CHIPS, r, c), dtype)], operands=(w,), sem=("parallel",), name=name, riders=riders)
    return _ret(core, rr, riders)


def _adamw_math(w, g, m, v):
    m = ADAM_B1 * m + (1.0 - ADAM_B1) * g
    v = ADAM_B2 * v + (1.0 - ADAM_B2) * (g * g)
    m_hat = m / (1.0 - ADAM_B1 ** ADAM_STEP)
    v_hat = v / (1.0 - ADAM_B2 ** ADAM_STEP)
    delta = -ADAM_LR * (m_hat / (jnp.sqrt(v_hat) + ADAM_EPS) + ADAM_WD * w)
    return delta, m, v


def adamw(w, g, m, v, *, name):
    nl, r, c = w.shape
    tr = _weight_tile(r)

    def body(w_ref, g_ref, m_ref, v_ref, go_ref, d_ref, mo_ref, vo_ref):
        gv = g_ref[...]
        go_ref[...] = gv
        d_ref[...], mo_ref[...], vo_ref[...] = _adamw_math(w_ref[...], gv, m_ref[...], v_ref[...])

    spec = pl.BlockSpec((None, tr, c), lambda l, i: (l, i, 0))
    shape = jax.ShapeDtypeStruct(w.shape, F32)
    outs, _ = _call(body, grid=(nl, r // tr), in_specs=[spec] * 4, out_specs=[spec] * 4, out_shape=[shape] * 4,
                    operands=(w, g, m, v), sem=("parallel", "parallel"), name=name)
    return outs


def adamw_small(ws, gs, ms, vs, *, name):
    n = len(ws)

    def body(*refs):
        ins, outs = refs[:4 * n], refs[4 * n:]
        for t in range(n):
            gv = ins[n + t][...]
            outs[t][...] = gv
            outs[n + t][...], outs[2 * n + t][...], outs[3 * n + t][...] = _adamw_math(
                ins[t][...], gv, ins[2 * n + t][...], ins[3 * n + t][...])

    shapes = [jax.ShapeDtypeStruct(w.shape, F32) for w in ws]
    res = pl.pallas_call(body, out_shape=shapes * 4, name=name)(*ws, *gs, *ms, *vs)
    return res[:n], res[n:2 * n], res[2 * n:3 * n], res[3 * n:]


def pair_add(g, r1, c_arr, *, name):
    _, rows, cdim = g.shape
    h = rows // 2

    def body(c_ref, g_ref, r_ref, o_ref):
        o_ref[...] = (g_ref[...].astype(F32) + r_ref[...].astype(F32)).astype(o_ref.dtype)

    (out,), _ = _call(
        body, grid=(N_CHIPS,), prefetch=(c_arr,),
        in_specs=[pl.BlockSpec((None, h, cdim), lambda s, c: (s, c[0], 0)), pl.BlockSpec((None, h, cdim), lambda s, c: (s, 0, 0))],
        out_specs=[pl.BlockSpec((None, h, cdim), lambda s, c: (s, 0, 0))],
        out_shape=[jax.ShapeDtypeStruct((N_CHIPS, h, cdim), g.dtype)], operands=(g, r1), sem=("parallel",), name=name)
    return out


def final_add(g, r1, r2, jc_arr, *, dest_shape, lead, prev, name):
    _, rows, cdim = g.shape
    h = rows // 2

    def body(jc_ref, g_ref, r1_ref, r2_ref, *rest):
        o_ref = rest[-1]
        acc = g_ref[...].astype(F32) + r1_ref[...].astype(F32)
        for k in range(3):
            acc = acc + r2_ref[k].astype(F32)
        o_ref[...] = acc

    if lead is None:
        o_spec = pl.BlockSpec((h, cdim), lambda i, jc: (jc[1], 0))
    elif lead == "chip":
        o_spec = pl.BlockSpec((None, h, cdim), lambda i, jc: (jc[0], jc[1], 0))
    else:
        o_spec = pl.BlockSpec((None, h, cdim), lambda i, jc: (lead, jc[1], 0))
    in_specs = [pl.BlockSpec((None, h, cdim), lambda i, jc: (jc[0], jc[1], 0)),
                pl.BlockSpec((None, h, cdim), lambda i, jc: (jc[0], 0, 0)),
                pl.BlockSpec((3, h, cdim), lambda i, jc: (0, 0, 0))]
    operands = [g, r1, r2]
    aliases = None
    if prev is not None:
        in_specs.append(ANY)
        operands.append(prev)
        aliases = {3: 0}
    (out,), _ = _call(body, grid=(1,), prefetch=(jc_arr,), in_specs=in_specs, out_specs=[o_spec],
                      out_shape=[jax.ShapeDtypeStruct(dest_shape, F32)], operands=operands, aliases=aliases, name=name)
    return out


def _place():
    return lax.axis_index("x"), lax.axis_index("y"), lax.axis_index("c")


def _partner(x, y, k):
    return (1 - x if k >> 1 else x), (1 - y if k & 1 else y)


WHOLE = (0, 1, 1)


def _half(rows, sel, dtype, piece=WHOLE):
    lo, hi, n = piece
    align = 16 if dtype == BF16 else 8
    step = rows // 2 // n
    assert rows // 2 == step * n and step % align == 0
    return pl.ds(pl.multiple_of(sel * (rows // 2) + lo * step, align), (hi - lo) * step)


def _rider(peers, inputs, aliased, fresh, nsem, copies, arrivals):
    def start(ins, outs, send, recv):
        for cp in copies(ins, outs, send, recv):
            cp.start()

    def finish(ins, outs, send, recv):
        for cp in arrivals(ins, outs, send, recv):
            cp.wait_recv()
        for cp in copies(ins, outs, send, recv):
            cp.wait_send()

    return types.SimpleNamespace(peers=peers, inputs=list(inputs), aliased=list(aliased), fresh=list(fresh), nsem=nsem,
                                 start=start, finish=finish)


def _remote(src, dst, send, recv, idx, dev):
    return pltpu.make_async_remote_copy(src_ref=src, dst_ref=dst, send_sem=send.at[idx], recv_sem=recv.at[idx],
                                        device_id=dev, device_id_type=MESH)


def gather_ici_rider(fulls, pieces=None):
    nt = len(fulls)
    pieces = pieces or [WHOLE] * nt

    def region(outs, t, slot, sel):
        return outs[t].at[slot, _half(fulls[t].shape[1], sel, fulls[t].dtype, pieces[t])]

    def copies(ins, outs, send, recv):
        x, y, c = _place()
        res = []
        for t in range(nt):
            for k in (1, 2, 3):
                px, py = _partner(x, y, k)
                mine = region(outs, t, 2 * x + y, c)
                res.append(_remote(mine, mine, send, recv, 3 * t + k - 1, (px, py, c)))
        return res

    def arrivals(ins, outs, send, recv):
        x, y, c = _place()
        res = []
        for t in range(nt):
            for k in (1, 2, 3):
                px, py = _partner(x, y, k)
                theirs = region(outs, t, 2 * px + py, c)
                res.append(_remote(theirs, theirs, send, recv, 3 * t + k - 1, (x, y, c)))
        return res

    return _rider("chips", fulls, range(nt), [], 3 * nt, copies, arrivals)


def gather_d2d_rider(fulls, pieces=None):
    nt = len(fulls)
    pieces = pieces or [WHOLE] * nt

    def region(outs, t, slot, sel):
        return outs[t].at[slot, _half(fulls[t].shape[1], sel, fulls[t].dtype, pieces[t])]

    def both(outs, send, recv, mine):
        x, y, c = _place()
        res = []
        for t in range(nt):
            for k in (1, 2, 3):
                px, py = _partner(x, y, k)
                part = region(outs, t, 2 * px + py, c if mine else 1 - c)
                res.append(_remote(part, part, send, recv, 3 * t + k - 1, (x, y, 1 - c)))
        return res

    return _rider("sibling", fulls, range(nt), [], 3 * nt, lambda i, o, s, r: both(o, s, r, True),
                  lambda i, o, s, r: both(o, s, r, False))


def exchange_rider(grads):
    nt = len(grads)

    def both(ins, outs, send, recv):
        x, y, c = _place()
        return [_remote(ins[t].at[:, _half(grads[t].shape[1], 1 - c, grads[t].dtype)], outs[t], send, recv, t, (x, y, 1 - c))
                for t in range(nt)]

    fresh = [jax.ShapeDtypeStruct((N_CHIPS, g.shape[1] // 2, g.shape[2]), g.dtype) for g in grads]
    return _rider("sibling", grads, [], fresh, nt, both, both)


def scatter_rider(parts):
    nt = len(parts)

    def both(ins, outs, send, recv):
        x, y, c = _place()
        res = []
        for t in range(nt):
            for k in (1, 2, 3):
                px, py = _partner(x, y, k)
                res.append(_remote(ins[t].at[2 * px + py], outs[t].at[k - 1], send, recv, 3 * t + k - 1, (px, py, c)))
        return res

    fresh = [jax.ShapeDtypeStruct((3,) + p.shape[1:], p.dtype) for p in parts]
    return _rider("chips", parts, [], fresh, 3 * nt, both, both)


def broadcast_rider(bufs, items):
    def region(outs, item, sel):
        bi, lead = item
        ref = outs[bi]
        if lead == "chip":
            x, y, _ = _place()
            ref = ref.at[2 * x + y]
        elif lead is not None:
            ref = ref.at[lead]
        return ref.at[_half(ref.shape[0], sel, F32)]

    def both(outs, send, recv, mine):
        x, y, c = _place()
        res = []
        for i, item in enumerate(items):
            part = region(outs, item, c if mine else 1 - c)
            res.append(_remote(part, part, send, recv, i, (x, y, 1 - c)))
        return res

    return _rider("sibling", bufs, range(len(bufs)), [], len(items), lambda i, o, s, r: both(o, s, r, True),
                  lambda i, o, s, r: both(o, s, r, False))


def allcast_rider(buf):
    peers = [(k, flip) for k in range(N_CHIPS) for flip in (0, 1) if (k, flip) != (0, 0)]

    def both(outs, send, recv, mine):
        x, y, c = _place()
        res = []
        for i, (k, flip) in enumerate(peers):
            px, py = _partner(x, y, k)
            pc = 1 - c if flip else c
            slot, sel = (2 * x + y, c) if mine else (2 * px + py, pc)
            part = outs[0].at[slot, _half(buf.shape[1], sel, F32)]
            res.append(_remote(part, part, send, recv, i, (px, py, pc)))
        return res

    return _rider("everyone", [buf], [0], [], len(peers), lambda i, o, s, r: both(o, s, r, True),
                  lambda i, o, s, r: both(o, s, r, False))


def comm_call(riders, *, name):
    _, res = _call(None, riders=riders, name=name)
    return res


SLAB_ROWS = 192


def _pad_rows(a, rows=8):
    return jnp.pad(a, ((0, rows - a.shape[0]), (0, 0)))


def _pack_small(norm_grads, db_qkv, db_o, dsinks, db_sp, dln_g, dln_b, dw_sp, loss_part):
    parts = [
        jnp.concatenate(norm_grads, axis=0),
        _pad_rows(jnp.pad(db_qkv, ((0, 0), (0, 2 * D_MODEL - QKV_WIDTH))).reshape(2, D_MODEL)),
        _pad_rows(db_o),
        _pad_rows(jnp.pad(dsinks.reshape(1, N_Q_HEADS), ((0, 0), (0, D_MODEL - N_Q_HEADS)))),
        _pad_rows(db_sp.reshape(1, D_MODEL)),
        _pad_rows(jnp.concatenate([dln_g, dln_b, jnp.pad(loss_part[0:1], ((0, 0), (0, D_MODEL - LANES)))], axis=0)),
        dw_sp.reshape(SGU_CHUNK, D_MODEL),
    ]
    slab = jnp.concatenate(parts, axis=0)
    return jnp.pad(slab, ((0, SLAB_ROWS - slab.shape[0]), (0, 0))).reshape(N_CHIPS, SLAB_ROWS // N_CHIPS, D_MODEL)


def _unpack_small(slab, j):
    slab = slab.reshape(SLAB_ROWS, D_MODEL)
    norms = [slab[2 * i:2 * i + 2] for i in range(4)]
    db_qkv = slab[8:10].reshape(1, 2 * D_MODEL)[:, :QKV_WIDTH]
    db_o = slab[16:17]
    dsinks = slab[24:25, :N_Q_HEADS]
    db_sp = slab[32:33].reshape(SGU_GROUPS, SGU_CHUNK)
    width = D_MODEL // N_CHIPS
    dln_g = lax.dynamic_slice(slab[40:41], (0, j * width), (1, width))
    dln_b = lax.dynamic_slice(slab[41:42], (0, j * width), (1, width))
    dw_sp = slab[48:48 + SGU_CHUNK].reshape(SGU_GROUPS * SGU_CHUNK, SGU_CHUNK)
    return norms, db_qkv, db_o, dsinks, db_sp, dln_g, dln_b, dw_sp, slab[42, 0]


class _GradReduce:
    def __init__(self, c_arr, jc_arr, dest_shapes):
        self.c_arr, self.jc_arr, self.dest_shapes = c_arr, jc_arr, dest_shapes
        self.grad, self.sibling, self.pair, self.chips, self.dest = {}, {}, {}, {}, {}

    def exchange(self, tags):
        return exchange_rider([self.grad[t] for t in tags])

    def exchanged(self, tags, res):
        for t, r in zip(tags, res):
            self.sibling[t] = r
            self.pair[t] = pair_add(self.grad[t], r, self.c_arr, name=f"pair_add_{t}")

    def scatter(self, tags):
        return scatter_rider([self.pair[t] for t in tags])

    def scattered(self, tags, res, where):
        for t, r in zip(tags, res):
            name, lead = where[t]
            self.dest[name] = final_add(self.grad[t], self.sibling[t], r, self.jc_arr, dest_shape=self.dest_shapes[name],
                                        lead=lead, prev=self.dest.get(name), name=f"final_add_{t}")

    def broadcast(self, items):
        names = []
        for n, _ in items:
            if n not in names:
                names.append(n)
        return names, broadcast_rider([self.dest[n] for n in names], [(names.index(n), lead) for n, lead in items])

    def broadcasted(self, names, res):
        for n, r in zip(names, res):
            self.dest[n] = r


def kernel(x, norm_mix_pre, norm_mix_post, norm_ffn_pre, norm_ffn_post, attn_w_qkv, attn_b_qkv, attn_sinks, attn_w_o, attn_b_o, sgu_w_in, sgu_ln_g, sgu_ln_b, sgu_w_spatial, sgu_b_spatial, sgu_w_out, ffn_w_gate_up, ffn_w_down, loss_target, m_norm_mix_pre, m_norm_mix_post, m_norm_ffn_pre, m_norm_ffn_post, m_attn_w_qkv, m_attn_b_qkv, m_attn_sinks, m_attn_w_o, m_attn_b_o, m_sgu_w_in, m_sgu_ln_g, m_sgu_ln_b, m_sgu_w_spatial, m_sgu_b_spatial, m_sgu_w_out, m_ffn_w_gate_up, m_ffn_w_down, v_norm_mix_pre, v_norm_mix_post, v_norm_ffn_pre, v_norm_ffn_post, v_attn_w_qkv, v_attn_b_qkv, v_attn_sinks, v_attn_w_o, v_attn_b_o, v_sgu_w_in, v_sgu_ln_g, v_sgu_ln_b, v_sgu_w_spatial, v_sgu_b_spatial, v_sgu_w_out, v_ffn_w_gate_up, v_ffn_w_down):
    s = x.shape[1]
    x0 = x.reshape(s, D_MODEL)
    target = loss_target.reshape(s, D_MODEL)
    mx, my, mc = lax.axis_index("x"), lax.axis_index("y"), lax.axis_index("c")
    chip = 2 * mx + my
    chip_arr = jnp.reshape(chip, (1,)).astype(I32)
    c_arr = jnp.reshape(mc, (1,)).astype(I32)
    jc_arr = jnp.stack([chip, mc]).astype(I32)
    zero_bias = jnp.zeros((1, D_MODEL), F32)

    def gain(p, i):
        return p[i:i + 1]

    big = [attn_w_qkv, attn_w_o, sgu_w_in, sgu_w_out, ffn_w_gate_up, ffn_w_gate_up, ffn_w_down, ffn_w_down]
    layers = [0, 0, 0, 0, 0, 1, 0, 1]
    tags = ["qkv", "wo", "win", "wout", "wgu0", "wgu1", "wd0", "wd1"]
    full = {t: place_shard(w, l, chip_arr, BF16, name=f"place_{t}") for w, l, t in zip(big, layers, tags) if t != "wgu1"}
    ln_pack = _pad_rows(jnp.concatenate([sgu_ln_g, sgu_ln_b], axis=0), 16)[None]
    full["ln"] = place_shard(ln_pack, 0, chip_arr, F32, name="place_ln")

    def split(items):
        return [i if isinstance(i, str) else i[0] for i in items], [WHOLE if isinstance(i, str) else tuple(i[1:]) for i in items]

    def ici(*items):
        names, pieces = split(items)
        return gather_ici_rider([full[n] for n in names], pieces)

    def d2d(*items):
        names, pieces = split(items)
        return gather_d2d_rider([full[n] for n in names], pieces)

    def landed(items, res):
        for n, r in zip(split(items)[0], res):
            full[n] = r

    cos, sin = _rope_tables(s)
    sink_rows = jnp.broadcast_to(
        jnp.repeat(attn_sinks.reshape(N_KV_HEADS, GQA_GROUP), WINDOW, axis=1)[:, None, :], (N_KV_HEADS, 8, ROWS))
    w_sp = sgu_w_spatial.reshape(SGU_GROUPS, SGU_CHUNK, SGU_CHUNK)
    b_sp = jnp.broadcast_to(sgu_b_spatial.reshape(SGU_GROUPS, SGU_CHUNK)[:, :, None], (SGU_GROUPS, SGU_CHUNK, LANES))

    h0, (res,) = prenorm(x0, gain(norm_mix_pre, 0), name="prenorm_0", riders=[ici("qkv", "ln")])
    landed(("qkv", "ln"), res)
    full["wgu1"], (res,) = place_shard(ffn_w_gate_up, 1, chip_arr, BF16, name="place_wgu1", riders=[d2d("qkv", "ln")])
    landed(("qkv", "ln"), res)
    ln_g = full["ln"][:, 0, :].reshape(1, D_MODEL)
    ln_b = full["ln"][:, 1, :].reshape(1, D_MODEL)

    def hosted(call, stages):
        outputs, results = call([{"ici": ici, "d2d": d2d}[kind](*items) for kind, items in stages])
        for (_, items), res in zip(stages, results):
            landed(items, res)
        return outputs

    qkv = hosted(lambda r: qkv_proj(h0, full["qkv"], attn_b_qkv, cos, sin, name="qkv_proj", riders=r),
                 [("ici", ("wo", ("wgu0", 0, 3, 8)))])
    o = hosted(lambda r: attn_fwd(qkv, sink_rows, name="attn_fwd", riders=r),
               [("d2d", ("wo",)), ("ici", (("wgu0", 3, 8, 8), ("wd0", 0, 2, 11)))])
    w_o = full["wo"].reshape(Q_WIDTH, D_MODEL)
    x1, h1, m0 = hosted(lambda r: proj_residual_norm(o, w_o, x0, attn_b_o, gain(norm_mix_post, 0), gain(norm_ffn_pre, 0),
                                                     name="attn_out_norm", riders=r),
                        [("d2d", ("wgu0",)), ("ici", (("wd0", 2, 11, 11),))])
    gu0, a0 = hosted(lambda r: ffn_up(h1, full["wgu0"], name="ffn_up_0", riders=r),
                     [("d2d", ("wd0",)), ("ici", ("win", "wout", ("wgu1", 0, 4, 8)))])
    w_d0 = full["wd0"].reshape(D_FF, D_MODEL)
    x2, h2, f0 = hosted(lambda r: proj_residual_norm(a0, w_d0, x1, zero_bias, gain(norm_ffn_post, 0), gain(norm_mix_pre, 1),
                                                     name="ffn_down_norm_0", riders=r),
                        [("d2d", ("win", "wout")), ("ici", (("wgu1", 4, 8, 8),))])
    w_in = full["win"]
    z, y = hosted(lambda r: sgu_in_fwd(h2, w_in, ln_g, ln_b, w_sp, b_sp, name="sgu_in_fwd", riders=r),
                  [("d2d", ("wgu1",)), ("ici", ("wd1",))])
    w_out = full["wout"].reshape(D_MODEL, D_MODEL)
    x3, h3, m1 = hosted(lambda r: proj_residual_norm(y, w_out, x2, zero_bias, gain(norm_mix_post, 1), gain(norm_ffn_pre, 1),
                                                     name="sgu_out_norm", riders=r),
                        [("d2d", ("wd1",))])
    w_qkv, w_gu0, w_gu1 = full["qkv"], full["wgu0"], full["wgu1"]
    w_d1 = full["wd1"].reshape(D_FF, D_MODEL)
    gu1, a1, dx4, df1, dg_fpost1, loss_part = ffn_fwd_loss_rows(
        h3, w_gu1, w_d1, x3, gain(norm_ffn_post, 1), target, name="ffn_fwd_loss_rows")

    red = _GradReduce(c_arr, jc_arr, {
        "qkv": attn_w_qkv.shape[1:], "wo": attn_w_o.shape[1:], "win": sgu_w_in.shape[1:], "wout": sgu_w_out.shape[1:],
        "wgu": ffn_w_gate_up.shape, "wd": ffn_w_down.shape, "slab": (N_CHIPS, SLAB_ROWS // N_CHIPS, D_MODEL)})
    where = {"qkv": ("qkv", None), "wo": ("wo", None), "win": ("win", None), "wout": ("wout", None), "wgu0": ("wgu", 0),
             "wgu1": ("wgu", 1), "wd0": ("wd", 0), "wd1": ("wd", 1), "small": ("slab", "chip")}

    dgu1, dx3, dm1, dg_fpre1, dg_mpost1, _ = ffn_bwd_rows(
        df1, w_d1, gu1, w_gu1, dx4, x3, gain(norm_ffn_pre, 1), m1, gain(norm_mix_post, 1), name="ffn_bwd_rows_1")
    red.grad["wd1"] = mm_tn(a1, df1, shard_major=False, tm=256, tn=D_MODEL, name="dw_down_1").reshape(
        N_CHIPS, D_FF // N_CHIPS, D_MODEL)
    red.grad["wgu1"], (res,) = mm_tn(h3, dgu1, shard_major=True, tm=512, tn=FF_HALF, name="dw_gate_up_1",
                                     riders=[red.exchange(["wd1"])])
    red.exchanged(["wd1"], res)
    dy, (res,) = mm_nt(dm1, w_out, out_dtype=F32, name="dy_sgu", riders=[red.exchange(["wgu1"])])
    red.exchanged(["wgu1"], res)
    red.grad["wout"] = mm_tn(y, dm1, shard_major=False, tm=512, tn=D_MODEL, name="dw_sgu_out").reshape(
        N_CHIPS, D_MODEL // N_CHIPS, D_MODEL)
    (dz, dw_sp, db_sp, dln_g, dln_b), (res_a, res_b) = sgu_bwd(
        z, dy, ln_g, ln_b, w_sp, b_sp, name="sgu_bwd", riders=[red.scatter(["wgu1"]), red.exchange(["wout"])])
    red.scattered(["wgu1"], res_a, where)
    red.exchanged(["wout"], res_b)
    names, rider = red.broadcast([("wgu", 1)])
    red.grad["win"], (res_a, res_b) = mm_tn(h2, dz, shard_major=True, tm=D_MODEL, tn=2 * D_MODEL // N_CHIPS, name="dw_sgu_in",
                                            riders=[rider, red.scatter(["wout"])])
    red.broadcasted(names, res_a)
    red.scattered(["wout"], res_b, where)
    names, rider = red.broadcast([("wout", None)])
    (dx2, df0, dg_mpre1, dg_fpost0, _), (res_a, res_b) = dh_norm_bwd_pair(
        dz, w_in, dx3, x2, gain(norm_mix_pre, 1), f0, gain(norm_ffn_post, 0), name="dh_sgu_norm",
        riders=[red.exchange(["win"]), rider])
    red.exchanged(["win"], res_a)
    red.broadcasted(names, res_b)
    (dgu0, dx1, dm0, dg_fpre0, dg_mpost0, db_o), (res,) = ffn_bwd_rows(
        df0, w_d0, gu0, w_gu0, dx2, x1, gain(norm_ffn_pre, 0), m0, gain(norm_mix_post, 0), name="ffn_bwd_rows_0",
        riders=[red.scatter(["wd1", "win"])])
    red.scattered(["wd1", "win"], res, where)
    names, rider = red.broadcast([("wd", 1), ("win", None)])
    dw_d0, (res,) = mm_tn(a0, df0, shard_major=False, tm=256, tn=D_MODEL, name="dw_down_0", riders=[rider])
    red.broadcasted(names, res)
    red.grad["wd0"] = dw_d0.reshape(N_CHIPS, D_FF // N_CHIPS, D_MODEL)
    do, (res,) = mm_nt(dm0, w_o, out_dtype=BF16, name="do_attn", riders=[red.exchange(["wd0"])])
    red.exchanged(["wd0"], res)
    red.grad["wgu0"], (res,) = mm_tn(h1, dgu0, shard_major=True, tm=512, tn=FF_HALF, name="dw_gate_up_0",
                                     riders=[red.scatter(["wd0"])])
    red.scattered(["wd0"], res, where)
    names, rider = red.broadcast([("wd", 0)])
    dw_o, (res_a, res_b) = mm_tn(o, dm0, shard_major=False, tm=512, tn=D_MODEL, name="dw_attn_out",
                                 riders=[red.exchange(["wgu0"]), rider])
    red.exchanged(["wgu0"], res_a)
    red.broadcasted(names, res_b)
    red.grad["wo"] = dw_o.reshape(N_CHIPS, Q_WIDTH // N_CHIPS, D_MODEL)
    (dq, dkc, dkp, dvc, dvp, dsink), (res_a, res_b) = attn_bwd(
        qkv, sink_rows, do, name="attn_bwd", riders=[red.scatter(["wgu0"]), red.exchange(["wo"])])
    red.scattered(["wgu0"], res_a, where)
    red.exchanged(["wo"], res_b)
    names, rider = red.broadcast([("wgu", 0)])
    (dqkv, db_qkv), (res_a, res_b) = rope_bwd(dq, dkc, dkp, dvc, dvp, cos, sin, name="rope_bwd",
                                              riders=[rider, red.scatter(["wo"])])
    red.broadcasted(names, res_a)
    red.scattered(["wo"], res_b, where)
    names, rider = red.broadcast([("wo", None)])
    red.grad["qkv"], (res,) = mm_tn(h0, dqkv, shard_major=True, tm=D_MODEL, tn=QKV_WIDTH // N_CHIPS, name="dw_qkv",
                                    riders=[rider])
    red.broadcasted(names, res)
    grad_x, dg_mpre0 = dh_norm_bwd_last(dqkv, w_qkv, dx1, x0, gain(norm_mix_pre, 0), name="dh_attn_norm_in")

    norm_grads = [jnp.concatenate(p, axis=0) for p in
                  ((dg_mpre0, dg_mpre1), (dg_mpost0, dg_mpost1), (dg_fpre0, dg_fpre1), (dg_fpost0, dg_fpost1))]
    red.grad["small"] = _pack_small(norm_grads, db_qkv, db_o, dsink[:, :, 0, 0], db_sp[:, :, 0], dln_g, dln_b, dw_sp,
                                    loss_part)
    (res,) = comm_call([red.exchange(["qkv", "small"])], name="tail_1")
    red.exchanged(["qkv", "small"], res)
    (res,) = comm_call([red.scatter(["qkv", "small"])], name="tail_2")
    red.scattered(["qkv", "small"], res, where)
    names, rider = red.broadcast([("qkv", None)])
    (res_a,), (slab_full,) = comm_call([rider, allcast_rider(red.dest["slab"])], name="tail_3")
    red.broadcasted(names, [res_a])
    g_qkv, g_wo, g_win, g_wout, g_wgu, g_wd = (red.dest[n] for n in ("qkv", "wo", "win", "wout", "wgu", "wd"))
    g_norms, g_bqkv, g_bo, g_sinks, g_bsp, g_lng, g_lnb, g_wsp, loss = _unpack_small(slab_full, chip)

    def big_update(w, g, m, v, tag):
        return adamw(w, g.reshape(w.shape), m, v, name=f"adamw_{tag}")

    upd = {
        "attn_w_qkv": big_update(attn_w_qkv, g_qkv, m_attn_w_qkv, v_attn_w_qkv, "qkv"),
        "attn_w_o": big_update(attn_w_o, g_wo, m_attn_w_o, v_attn_w_o, "wo"),
        "sgu_w_in": big_update(sgu_w_in, g_win, m_sgu_w_in, v_sgu_w_in, "win"),
        "sgu_w_out": big_update(sgu_w_out, g_wout, m_sgu_w_out, v_sgu_w_out, "wout"),
        "ffn_w_gate_up": big_update(ffn_w_gate_up, g_wgu, m_ffn_w_gate_up, v_ffn_w_gate_up, "wgu"),
        "ffn_w_down": big_update(ffn_w_down, g_wd, m_ffn_w_down, v_ffn_w_down, "wd"),
    }
    small_names = ["norm_mix_pre", "norm_mix_post", "norm_ffn_pre", "norm_ffn_post", "attn_b_qkv", "attn_sinks", "attn_b_o",
                   "sgu_ln_g", "sgu_ln_b", "sgu_w_spatial", "sgu_b_spatial"]
    small_w = [norm_mix_pre, norm_mix_post, norm_ffn_pre, norm_ffn_post, attn_b_qkv, attn_sinks, attn_b_o, sgu_ln_g, sgu_ln_b,
               sgu_w_spatial, sgu_b_spatial]
    small_m = [m_norm_mix_pre, m_norm_mix_post, m_norm_ffn_pre, m_norm_ffn_post, m_attn_b_qkv, m_attn_sinks, m_attn_b_o,
               m_sgu_ln_g, m_sgu_ln_b, m_sgu_w_spatial, m_sgu_b_spatial]
    small_v = [v_norm_mix_pre, v_norm_mix_post, v_norm_ffn_pre, v_norm_ffn_post, v_attn_b_qkv, v_attn_sinks, v_attn_b_o,
               v_sgu_ln_g, v_sgu_ln_b, v_sgu_w_spatial, v_sgu_b_spatial]
    small_g = g_norms + [g_bqkv, g_sinks, g_bo, g_lng, g_lnb, g_wsp, g_bsp]

    def flat2(a):
        return a.reshape(-1, a.shape[-1])

    res = adamw_small([flat2(a) for a in small_w], [flat2(a) for a in small_g], [flat2(a) for a in small_m],
                      [flat2(a) for a in small_v], name="adamw_small")
    for i, nm in enumerate(small_names):
        upd[nm] = tuple(r[i].reshape(small_w[i].shape) for r in res)

    order = ["norm_mix_pre", "norm_mix_post", "norm_ffn_pre", "norm_ffn_post", "attn_w_qkv", "attn_b_qkv", "attn_sinks",
             "attn_w_o", "attn_b_o", "sgu_w_in", "sgu_ln_g", "sgu_ln_b", "sgu_w_spatial", "sgu_b_spatial", "sgu_w_out",
             "ffn_w_gate_up", "ffn_w_down"]
    outs = [loss, grad_x.reshape(1, s, D_MODEL)]
    for part in range(4):
        outs += [upd[nm][part] for nm in order]
    return tuple(outs)
```

```python
import types

import numpy as np
import jax
import jax.numpy as jnp
from jax import lax
from jax.experimental import pallas as pl
from jax.experimental.pallas import tpu as pltpu

F32 = jnp.float32
BF16 = jnp.bfloat16
I32 = jnp.int32

D_MODEL = 1024
HEAD_DIM = 64
N_Q_HEADS = 16
N_KV_HEADS = 4
GQA_GROUP = 4
WINDOW = 128
Q_WIDTH = 1024
KV_WIDTH = 256
QKV_WIDTH = 1536
ROPE_THETA = 10000.0
SGU_GROUPS = 8
SGU_CHUNK = 128
D_FF = 2816
FF_HALF = D_FF // 2
EPS = 1e-6
N_CHIPS = 4
LANES = 128

ADAM_LR = 0.001
ADAM_B1 = 0.9
ADAM_B2 = 0.999
ADAM_EPS = 1e-08
ADAM_WD = 0.01
ADAM_STEP = 10

VMEM_LIMIT = 52 * 1024 * 1024
MESH = pl.DeviceIdType.MESH
NEG = -1e30
NT_DIMS = (((1,), (1,)), ((), ()))
TN_DIMS = (((0,), (0,)), ((), ()))
NN_DIMS = (((1,), (0,)), ((), ()))
ANY = pl.BlockSpec(memory_space=pl.ANY)


def _row_tile(s, want):
    return want if s % want == 0 else s


PEER_KINDS = ("sibling", "chips", "sibling+chips", "everyone")


def _peer_kind(riders):
    kinds = {r.peers for r in riders}
    if not kinds:
        return None
    if "everyone" in kinds:
        return "everyone"
    return "sibling+chips" if len(kinds) == 2 else kinds.pop()


def _peer_barrier(kind):
    x, y, c = _place()
    chips = [(*_partner(x, y, k), c) for k in (1, 2, 3)]
    peers = {"sibling": [(x, y, 1 - c)], "chips": chips, "sibling+chips": [(x, y, 1 - c)] + chips,
             "everyone": [(x, y, 1 - c)] + chips + [(px, py, 1 - c) for px, py, _ in chips]}[kind]
    barrier = pltpu.get_barrier_semaphore()
    for dev in peers:
        pl.semaphore_signal(barrier, inc=1, device_id=dev, device_id_type=MESH)
    pl.semaphore_wait(barrier, len(peers))


def _call(body, *, name, grid=(), in_specs=(), out_specs=(), out_shape=(), scratch_shapes=(), operands=(), prefetch=(),
          aliases=None, riders=(), sem=None):
    n_pre, n_in, n_out, n_scr = len(prefetch), len(operands), len(out_shape), len(scratch_shapes)
    in_specs, out_specs, out_shape = list(in_specs), list(out_specs), list(out_shape)
    operands, scratch_shapes = list(operands), list(scratch_shapes)
    io_alias = {n_pre + i: o for i, o in (aliases or {}).items()}
    for r in riders:
        base_in, base_out = n_pre + len(operands), len(out_shape)
        operands += list(r.inputs)
        in_specs += [ANY] * len(r.inputs)
        for pos, i in enumerate(r.aliased):
            io_alias[base_in + i] = base_out + pos
            out_shape.append(jax.ShapeDtypeStruct(r.inputs[i].shape, r.inputs[i].dtype))
        out_shape += list(r.fresh)
        out_specs += [ANY] * (len(r.aliased) + len(r.fresh))
        scratch_shapes += [pltpu.SemaphoreType.DMA((r.nsem,)), pltpu.SemaphoreType.DMA((r.nsem,))]

    def wrapped(*refs):
        pre, p = refs[:n_pre], n_pre
        core_in, p = refs[p:p + n_in], p + n_in
        r_in = []
        for r in riders:
            r_in.append(refs[p:p + len(r.inputs)])
            p += len(r.inputs)
        core_out, p = refs[p:p + n_out], p + n_out
        r_out = []
        for r in riders:
            k = len(r.aliased) + len(r.fresh)
            r_out.append(refs[p:p + k])
            p += k
        core_scr, p = refs[p:p + n_scr], p + n_scr
        r_sem = [refs[p + 2 * i:p + 2 * i + 2] for i in range(len(riders))]

        def edge(at_last, fns):
            def run():
                if not at_last:
                    _peer_barrier(peer_kind)
                for i, r in enumerate(riders):
                    getattr(r, fns)(r_in[i], r_out[i], r_sem[i][0], r_sem[i][1])
            if not riders:
                return
            if not grid:
                run()
                return
            cond = None
            for d, n in enumerate(grid):
                c = pl.program_id(d) == (n - 1 if at_last else 0)
                cond = c if cond is None else jnp.logical_and(cond, c)
            pl.when(cond)(run)

        edge(False, "start")
        if body is not None:
            body(*pre, *core_in, *core_out, *core_scr)
        edge(True, "finish")

    if sem is None or riders:
        sem = ("arbitrary",) * len(grid)
    kwargs = dict(out_shape=out_shape, input_output_aliases=io_alias, name=name)
    peer_kind = _peer_kind(riders)
    collective = {} if peer_kind is None else {"collective_id": PEER_KINDS.index(peer_kind)}
    if grid:
        kwargs["compiler_params"] = pltpu.CompilerParams(dimension_semantics=sem, vmem_limit_bytes=VMEM_LIMIT, **collective)
    elif collective:
        kwargs["compiler_params"] = pltpu.CompilerParams(**collective)
    if n_pre:
        kwargs["grid_spec"] = pltpu.PrefetchScalarGridSpec(
            num_scalar_prefetch=n_pre, grid=grid, in_specs=in_specs, out_specs=out_specs, scratch_shapes=scratch_shapes)
    else:
        kwargs.update(grid=grid, in_specs=in_specs, out_specs=out_specs, scratch_shapes=scratch_shapes)
    res = pl.pallas_call(wrapped, **kwargs)(*prefetch, *operands)
    core, rest, rider_res = list(res[:n_out]), list(res[n_out:]), []
    for r in riders:
        k = len(r.aliased) + len(r.fresh)
        rider_res.append(rest[:k])
        rest = rest[k:]
    return core, rider_res


def _mm_call(*, grid, in_specs, out_spec, out_shape, dims, nk, kaxis, acc_shape, name, operands, riders=()):
    out_dtype = out_shape.dtype

    def body(a_ref, b_ref, o_ref, *scratch):
        p = lax.dot_general(a_ref[...].astype(BF16), b_ref[...].astype(BF16), dims, preferred_element_type=F32)
        if nk == 1:
            o_ref[...] = p.astype(out_dtype)
        else:
            acc = scratch[0]
            kk = pl.program_id(kaxis)

            @pl.when(kk == 0)
            def _():
                acc[...] = p

            @pl.when(kk > 0)
            def _():
                acc[...] += p

            @pl.when(kk == nk - 1)
            def _():
                o_ref[...] = acc[...].astype(out_dtype)

    sem = ["parallel"] * len(grid)
    if nk > 1:
        sem[kaxis] = "arbitrary"
    (out,), rider_res = _call(
        body, grid=grid, in_specs=in_specs, out_specs=[out_spec], out_shape=[out_shape],
        scratch_shapes=[pltpu.VMEM(acc_shape, F32)] if nk > 1 else [], operands=operands, name=name, riders=riders,
        sem=tuple(sem))
    return (out, rider_res) if riders else out


def mm_nn(a, w, *, out_dtype, name, tm=512, tn=512, riders=()):
    m, k = a.shape
    tm = _row_tile(m, tm)
    if w.ndim == 3:
        ns = w.shape[2]
        grid = (N_CHIPS, m // tm)
        w_spec = pl.BlockSpec((None, k, ns), lambda j, i: (j, 0, 0))
        o_spec = pl.BlockSpec((tm, ns), lambda j, i: (i, j))
        n = N_CHIPS * ns
    else:
        n = w.shape[1]
        grid = (n // tn, m // tm)
        w_spec = pl.BlockSpec((k, tn), lambda j, i: (0, j))
        o_spec = pl.BlockSpec((tm, tn), lambda j, i: (i, j))
    return _mm_call(grid=grid, in_specs=[pl.BlockSpec((tm, k), lambda j, i: (i, 0)), w_spec], out_spec=o_spec,
                    out_shape=jax.ShapeDtypeStruct((m, n), out_dtype), dims=NN_DIMS, nk=1, kaxis=0, acc_shape=None,
                    name=name, operands=(a, w), riders=riders)


def mm_nt(a, w, *, out_dtype, name, tm=512, tn=512, riders=()):
    if w.ndim == 2:
        m, n = a.shape
        kout = w.shape[0]
        tm = _row_tile(m, tm)
        return _mm_call(grid=(kout // tn, m // tm),
                        in_specs=[pl.BlockSpec((tm, n), lambda j, i: (i, 0)), pl.BlockSpec((tn, n), lambda j, i: (j, 0))],
                        out_spec=pl.BlockSpec((tm, tn), lambda j, i: (i, j)),
                        out_shape=jax.ShapeDtypeStruct((m, kout), out_dtype), dims=NT_DIMS, nk=1, kaxis=0,
                        acc_shape=None, name=name, operands=(a, w), riders=riders)
    _, kout, ns = w.shape
    planes = a.ndim == 3
    m = a.shape[1] if planes else a.shape[0]
    tm = _row_tile(m, tm)
    a_spec = pl.BlockSpec((2, tm, 2 * ns), lambda i: (0, i, 0)) if planes else pl.BlockSpec((tm, N_CHIPS * ns), lambda i: (i, 0))

    def body(a_ref, w0, w1, w2, w3, o_ref):
        acc = None
        for j, w_ref in enumerate((w0, w1, w2, w3)):
            if planes:
                a_j = a_ref[j // 2, :, (j % 2) * ns:(j % 2 + 1) * ns]
            else:
                a_j = a_ref[:, j * ns:(j + 1) * ns]
            p = lax.dot_general(a_j, w_ref[...], NT_DIMS, preferred_element_type=F32)
            acc = p if acc is None else acc + p
        o_ref[...] = acc.astype(out_dtype)

    def shard(j):
        return pl.BlockSpec((None, kout, ns), lambda i: (j, 0, 0))

    (out,), rider_res = _call(
        body, grid=(m // tm,), in_specs=[a_spec] + [shard(j) for j in range(N_CHIPS)],
        out_specs=[pl.BlockSpec((tm, kout), lambda i: (i, 0))], out_shape=[jax.ShapeDtypeStruct((m, kout), out_dtype)],
        operands=(a, w, w, w, w), sem=("parallel",), name=name, riders=riders)
    return (out, rider_res) if riders else out


def mm_tn(a, b, *, shard_major, name, tm, tn, tk=None, out_dtype=BF16, riders=()):
    s, m = a.shape
    tk = s if tk is None else _row_tile(s, tk)
    if b.ndim == 3:
        n = 2 * b.shape[2]
        b_spec = pl.BlockSpec((None, tk, tn), lambda j, i, kk: (j // 2, kk, j % 2))
    else:
        n = b.shape[1]
        b_spec = pl.BlockSpec((tk, tn), lambda j, i, kk: (kk, j))
    if shard_major:
        assert tn == n // N_CHIPS
        o_spec = pl.BlockSpec((None, tm, tn), lambda j, i, kk: (j, i, 0))
        o_shape = jax.ShapeDtypeStruct((N_CHIPS, m, tn), out_dtype)
    else:
        o_spec = pl.BlockSpec((tm, tn), lambda j, i, kk: (i, j))
        o_shape = jax.ShapeDtypeStruct((m, n), out_dtype)
    return _mm_call(grid=(n // tn, m // tm, s // tk),
                    in_specs=[pl.BlockSpec((tk, tm), lambda j, i, kk: (kk, i)), b_spec], out_spec=o_spec,
                    out_shape=o_shape, dims=TN_DIMS, nk=s // tk, kaxis=2, acc_shape=(tm, tn), name=name, operands=(a, b),
                    riders=riders)


def _rstd(x):
    return lax.rsqrt(jnp.mean(x * x, axis=-1, keepdims=True) + EPS)


def _rms_bwd(dy, x, g):
    r = _rstd(x)
    xhat = x * r
    gy = dy * g
    dx = r * (gy - xhat * jnp.mean(gy * xhat, axis=-1, keepdims=True))
    return dx, jnp.sum(dy * xhat, axis=0, keepdims=True)


def _accum(ref, val, first):
    @pl.when(first)
    def _():
        ref[...] = val

    @pl.when(jnp.logical_not(first))
    def _():
        ref[...] += val


def _row_spec(tm, width):
    return pl.BlockSpec((tm, width), lambda i: (i, 0))


def _vec_spec(width):
    return pl.BlockSpec((1, width), lambda i: (0, 0))


def _ret(core, rider_res, riders):
    core = core[0] if len(core) == 1 else core
    return (core, rider_res) if riders else core


def prenorm(x, g, *, name, tm=256, riders=()):
    s = x.shape[0]
    tm = _row_tile(s, tm)

    def body(x_ref, g_ref, h_ref):
        xv = x_ref[...]
        h_ref[...] = (xv * _rstd(xv) * g_ref[...]).astype(BF16)

    core, rr = _call(
        body, grid=(s // tm,), in_specs=[_row_spec(tm, D_MODEL), _vec_spec(D_MODEL)], out_specs=[_row_spec(tm, D_MODEL)],
        out_shape=[jax.ShapeDtypeStruct((s, D_MODEL), BF16)], operands=(x, g), sem=("parallel",), name=name, riders=riders)
    return _ret(core, rr, riders)


def proj_residual_norm(a, w, x, bias, g_post, g_next, *, name, tm=256, riders=()):
    s, k = a.shape
    tm = _row_tile(s, tm)

    def body(a_ref, w_ref, x_ref, b_ref, gp_ref, gn_ref, xo_ref, h_ref, m_ref):
        mv = jnp.dot(a_ref[...], w_ref[...], preferred_element_type=F32) + b_ref[...]
        m_ref[...] = mv.astype(BF16)
        xn = x_ref[...] + mv * _rstd(mv) * gp_ref[...]
        xo_ref[...] = xn
        h_ref[...] = (xn * _rstd(xn) * gn_ref[...]).astype(BF16)

    row, vec = _row_spec(tm, D_MODEL), _vec_spec(D_MODEL)
    core, rr = _call(
        body, grid=(s // tm,),
        in_specs=[_row_spec(tm, k), pl.BlockSpec((k, D_MODEL), lambda i: (0, 0)), row, vec, vec, vec], out_specs=[row, row, row],
        out_shape=[jax.ShapeDtypeStruct((s, D_MODEL), F32), jax.ShapeDtypeStruct((s, D_MODEL), BF16),
                   jax.ShapeDtypeStruct((s, D_MODEL), BF16)],
        operands=(a, w, x, bias, g_post, g_next), sem=("parallel",), name=name, riders=riders)
    return _ret(core, rr, riders)


def proj_loss_head(a, w, x, g_post, target, *, name, tm=256, riders=()):
    s, k = a.shape
    tm = _row_tile(s, tm)

    def body(a_ref, w_ref, x_ref, g_ref, t_ref, dx_ref, df_ref, dg_ref, loss_ref):
        first = pl.program_id(0) == 0
        fv = jnp.dot(a_ref[...], w_ref[...], preferred_element_type=F32)
        g = g_ref[...]
        err = x_ref[...] + fv * _rstd(fv) * g - t_ref[...]
        dx = err * (1.0 / D_MODEL)
        dx_ref[...] = dx
        df, dg = _rms_bwd(dx, fv, g)
        df_ref[...] = df.astype(BF16)
        _accum(dg_ref, dg, first)
        part = jnp.sum(jnp.sum(err * err, axis=-1, keepdims=True), axis=0, keepdims=True) * (0.5 / D_MODEL)
        _accum(loss_ref, jnp.broadcast_to(part, (8, LANES)), first)

    row, vec = _row_spec(tm, D_MODEL), _vec_spec(D_MODEL)
    core, rr = _call(
        body, grid=(s // tm,), in_specs=[_row_spec(tm, k), pl.BlockSpec((k, D_MODEL), lambda i: (0, 0)), row, vec, row],
        out_specs=[row, row, vec, pl.BlockSpec((8, LANES), lambda i: (0, 0))],
        out_shape=[jax.ShapeDtypeStruct((s, D_MODEL), F32), jax.ShapeDtypeStruct((s, D_MODEL), BF16),
                   jax.ShapeDtypeStruct((1, D_MODEL), F32), jax.ShapeDtypeStruct((8, LANES), F32)],
        operands=(a, w, x, g_post, target), name=name, riders=riders)
    return _ret(core, rr, riders)


def ffn_fwd_loss_rows(h, w_gu, w_d, x, g_post, target, *, name, tm=256, riders=()):
    s = x.shape[0]
    tm = _row_tile(s, tm)

    def body(h_ref, w0, w1, w2, w3, wd_ref, x_ref, g_ref, t_ref, d_ref, a_ref, dx_ref, df_ref, dg_ref, loss_ref):
        first = pl.program_id(0) == 0
        hv = h_ref[...]
        fv = None
        for half, (wg_ref, wu_ref) in enumerate(((w0, w2), (w1, w3))):
            cols = slice(half * FF_HALF, (half + 1) * FF_HALF)
            g = jnp.dot(hv, wg_ref[...], preferred_element_type=F32)
            u = jnp.dot(hv, wu_ref[...], preferred_element_type=F32)
            sig = _sigmoid(g)
            silu = g * sig
            d_ref[0, :, cols] = (u * (sig + silu * (1.0 - sig))).astype(BF16)
            d_ref[1, :, cols] = silu.astype(BF16)
            act = (silu * u).astype(BF16)
            a_ref[:, cols] = act
            p = jnp.dot(act, wd_ref[cols, :], preferred_element_type=F32)
            fv = p if fv is None else fv + p
        gain = g_ref[...]
        err = x_ref[...] + fv * _rstd(fv) * gain - t_ref[...]
        dx = err * (1.0 / D_MODEL)
        dx_ref[...] = dx
        df, dg = _rms_bwd(dx, fv, gain)
        df_ref[...] = df.astype(BF16)
        _accum(dg_ref, dg, first)
        part = jnp.sum(jnp.sum(err * err, axis=-1, keepdims=True), axis=0, keepdims=True) * (0.5 / D_MODEL)
        _accum(loss_ref, jnp.broadcast_to(part, (8, LANES)), first)

    def resident(shape, index):
        return pl.BlockSpec(shape, index, pipeline_mode=pl.Buffered(1))

    row, vec = _row_spec(tm, D_MODEL), _vec_spec(D_MODEL)
    shards = [resident((None, D_MODEL, FF_HALF), (lambda j: (lambda i: (j, 0, 0)))(j)) for j in range(N_CHIPS)]
    core, rr = _call(
        body, grid=(s // tm,),
        in_specs=[row] + shards + [resident((D_FF, D_MODEL), lambda i: (0, 0)), row, vec, row],
        out_specs=[pl.BlockSpec((2, tm, D_FF), lambda i: (0, i, 0)), _row_spec(tm, D_FF), row, row, vec,
                   pl.BlockSpec((8, LANES), lambda i: (0, 0))],
        out_shape=[jax.ShapeDtypeStruct((2, s, D_FF), BF16), jax.ShapeDtypeStruct((s, D_FF), BF16),
                   jax.ShapeDtypeStruct((s, D_MODEL), F32), jax.ShapeDtypeStruct((s, D_MODEL), BF16),
                   jax.ShapeDtypeStruct((1, D_MODEL), F32), jax.ShapeDtypeStruct((8, LANES), F32)],
        operands=(h, w_gu, w_gu, w_gu, w_gu, w_d, x, g_post, target), name=name, riders=riders)
    return _ret(core, rr, riders)


def dh_norm_bwd_pair(a, w, dres, x, g_pre, m, g_post, *, name, tm=512, sub=256, riders=()):
    _, kout, ns = w.shape
    planes = a.ndim == 3
    s = x.shape[0]
    tm = _row_tile(s, tm)
    sub = min(sub, tm)
    a_spec = pl.BlockSpec((2, tm, 2 * ns), lambda i: (0, i, 0)) if planes else pl.BlockSpec((tm, N_CHIPS * ns), lambda i: (i, 0))

    def body(a_ref, w0, w1, w2, w3, dres_ref, x_ref, gpre_ref, m_ref, gpost_ref, dx_ref, dm_ref, dgpre_ref, dgpost_ref, db_ref):
        first = pl.program_id(0) == 0
        sums = None
        for t in range(tm // sub):
            rows = slice(t * sub, (t + 1) * sub)
            dh = None
            for j, w_ref in enumerate((w0, w1, w2, w3)):
                a_j = a_ref[j // 2, rows, (j % 2) * ns:(j % 2 + 1) * ns] if planes else a_ref[rows, j * ns:(j + 1) * ns]
                p = lax.dot_general(a_j, w_ref[...], NT_DIMS, preferred_element_type=F32)
                dh = p if dh is None else dh + p
            d1, dgpre = _rms_bwd(dh, x_ref[rows, :], gpre_ref[...])
            dx = dres_ref[rows, :] + d1
            dx_ref[rows, :] = dx
            dm, dgpost = _rms_bwd(dx, m_ref[rows, :].astype(F32), gpost_ref[...])
            dm_ref[rows, :] = dm.astype(BF16)
            part = (dgpre, dgpost, jnp.sum(dm, axis=0, keepdims=True))
            sums = part if sums is None else tuple(u + v for u, v in zip(sums, part))
        _accum(dgpre_ref, sums[0], first)
        _accum(dgpost_ref, sums[1], first)
        _accum(db_ref, sums[2], first)

    def shard(j):
        return pl.BlockSpec((None, kout, ns), lambda i: (j, 0, 0))

    row, vec = _row_spec(tm, D_MODEL), _vec_spec(D_MODEL)
    vshape = jax.ShapeDtypeStruct((1, D_MODEL), F32)
    core, rr = _call(
        body, grid=(s // tm,), in_specs=[a_spec] + [shard(j) for j in range(N_CHIPS)] + [row, row, vec, row, vec],
        out_specs=[row, row, vec, vec, vec],
        out_shape=[jax.ShapeDtypeStruct((s, D_MODEL), F32), jax.ShapeDtypeStruct((s, D_MODEL), BF16), vshape, vshape, vshape],
        operands=(a, w, w, w, w, dres, x, g_pre, m, g_post), name=name, riders=riders)
    return _ret(core, rr, riders)


def ffn_bwd_rows(df, w_d, d_planes, w_gu, dres, x, g_pre, m, g_post, *, name, tm=256, riders=()):
    s = x.shape[0]
    tm = _row_tile(s, tm)

    def body(df_ref, wd_ref, d_ref, w0, w1, w2, w3, dres_ref, x_ref, gpre_ref, m_ref, gpost_ref,
             o_ref, dx_ref, dm_ref, dgpre_ref, dgpost_ref, db_ref):
        first = pl.program_id(0) == 0
        dfv = df_ref[...]
        dh = None
        for half, (wg_ref, wu_ref) in enumerate(((w0, w2), (w1, w3))):
            cols = slice(half * FF_HALF, (half + 1) * FF_HALF)
            da = lax.dot_general(dfv, wd_ref[cols, :], NT_DIMS, preferred_element_type=F32)
            dg = (da * d_ref[0, :, cols].astype(F32)).astype(BF16)
            du = (da * d_ref[1, :, cols].astype(F32)).astype(BF16)
            o_ref[0, :, cols] = dg
            o_ref[1, :, cols] = du
            p = lax.dot_general(dg, wg_ref[...], NT_DIMS, preferred_element_type=F32)
            p += lax.dot_general(du, wu_ref[...], NT_DIMS, preferred_element_type=F32)
            dh = p if dh is None else dh + p
        d1, dgpre = _rms_bwd(dh, x_ref[...], gpre_ref[...])
        dx = dres_ref[...] + d1
        dx_ref[...] = dx
        dm, dgpost = _rms_bwd(dx, m_ref[...].astype(F32), gpost_ref[...])
        dm_ref[...] = dm.astype(BF16)
        _accum(dgpre_ref, dgpre, first)
        _accum(dgpost_ref, dgpost, first)
        _accum(db_ref, jnp.sum(dm, axis=0, keepdims=True), first)

    def resident(shape, index):
        return pl.BlockSpec(shape, index, pipeline_mode=pl.Buffered(1))

    planes = pl.BlockSpec((2, tm, D_FF), lambda i: (0, i, 0))
    row, vec = _row_spec(tm, D_MODEL), _vec_spec(D_MODEL)
    vshape = jax.ShapeDtypeStruct((1, D_MODEL), F32)
    shards = [resident((None, D_MODEL, FF_HALF), (lambda j: (lambda i: (j, 0, 0)))(j)) for j in range(N_CHIPS)]
    core, rr = _call(
        body, grid=(s // tm,),
        in_specs=[row, resident((D_FF, D_MODEL), lambda i: (0, 0)), planes] + shards + [row, row, vec, row, vec],
        out_specs=[planes, row, row, vec, vec, vec],
        out_shape=[jax.ShapeDtypeStruct((2, s, D_FF), BF16), jax.ShapeDtypeStruct((s, D_MODEL), F32),
                   jax.ShapeDtypeStruct((s, D_MODEL), BF16), vshape, vshape, vshape],
        operands=(df, w_d, d_planes, w_gu, w_gu, w_gu, w_gu, dres, x, g_pre, m, g_post), name=name, riders=riders)
    return _ret(core, rr, riders)


def dh_norm_bwd_last(a, w, dres, x, g_pre, *, name, tm=512, sub=256):
    _, kout, ns = w.shape
    s = x.shape[0]
    tm = _row_tile(s, tm)
    sub = min(sub, tm)

    def body(a_ref, w0, w1, w2, w3, dres_ref, x_ref, g_ref, dx_ref, dg_ref):
        total = None
        for t in range(tm // sub):
            rows = slice(t * sub, (t + 1) * sub)
            dh = None
            for j, w_ref in enumerate((w0, w1, w2, w3)):
                p = lax.dot_general(a_ref[rows, j * ns:(j + 1) * ns], w_ref[...], NT_DIMS, preferred_element_type=F32)
                dh = p if dh is None else dh + p
            d1, dg = _rms_bwd(dh, x_ref[rows, :], g_ref[...])
            dx_ref[rows, :] = dres_ref[rows, :] + d1
            total = dg if total is None else total + dg
        _accum(dg_ref, total, pl.program_id(0) == 0)

    def shard(j):
        return pl.BlockSpec((None, kout, ns), lambda i: (j, 0, 0))

    row, vec = _row_spec(tm, D_MODEL), _vec_spec(D_MODEL)
    (dx, dg), _ = _call(
        body, grid=(s // tm,), in_specs=[_row_spec(tm, N_CHIPS * ns)] + [shard(j) for j in range(N_CHIPS)] + [row, row, vec],
        out_specs=[row, vec], out_shape=[jax.ShapeDtypeStruct((s, D_MODEL), F32), jax.ShapeDtypeStruct((1, D_MODEL), F32)],
        operands=(a, w, w, w, w, dres, x, g_pre), name=name)
    return dx, dg


def _rope_tables(s):
    half = HEAD_DIM // 2
    inv_freq = np.float32(ROPE_THETA) ** (-(np.arange(half, dtype=np.float32) * np.float32(2.0)) / np.float32(HEAD_DIM))
    ang = np.arange(s, dtype=np.float32)[:, None] * inv_freq[None, :]
    cos, sin = np.cos(ang).astype(np.float32), np.sin(ang).astype(np.float32)
    return jnp.asarray(np.tile(cos, (1, 4))), jnp.asarray(np.concatenate([-sin, sin, -sin, sin], axis=1))


def _swap_halves(x):
    lane = lax.broadcasted_iota(I32, x.shape, 1)
    return jnp.where((lane & (HEAD_DIM - 1)) < HEAD_DIM // 2, pltpu.roll(x, LANES - 32, 1), pltpu.roll(x, 32, 1))


N_ROPE_BLOCKS = (Q_WIDTH + KV_WIDTH) // LANES


def qkv_proj(h, w, bias, cos, sin, *, name, tm=512, riders=()):
    s, k = h.shape
    ns = w.shape[2]
    tm = _row_tile(s, tm)

    def body(h_ref, w_ref, b_ref, c_ref, s_ref, o_ref):
        j = pl.program_id(0)
        sub = min(256, tm)
        for t in range(tm // sub):
            rows = slice(t * sub, (t + 1) * sub)
            p = jnp.dot(h_ref[rows, :], w_ref[...], preferred_element_type=F32) + b_ref[...]
            cosv, sinv = c_ref[rows, :], s_ref[rows, :]
            for blk in range(ns // LANES):
                xb = p[:, blk * LANES:(blk + 1) * LANES]
                roped = xb * cosv + _swap_halves(xb) * sinv
                is_qk = j * (ns // LANES) + blk < N_ROPE_BLOCKS
                o_ref[rows, blk * LANES:(blk + 1) * LANES] = jnp.where(is_qk, roped, xb).astype(BF16)

    core, rr = _call(
        body, grid=(N_CHIPS, s // tm),
        in_specs=[pl.BlockSpec((tm, k), lambda j, i: (i, 0)), pl.BlockSpec((None, k, ns), lambda j, i: (j, 0, 0)),
                  pl.BlockSpec((1, ns), lambda j, i: (0, j)), pl.BlockSpec((tm, LANES), lambda j, i: (i, 0)),
                  pl.BlockSpec((tm, LANES), lambda j, i: (i, 0))],
        out_specs=[pl.BlockSpec((tm, ns), lambda j, i: (i, j))], out_shape=[jax.ShapeDtypeStruct((s, N_CHIPS * ns), BF16)],
        operands=(h, w, bias, cos, sin), sem=("parallel", "parallel"), name=name, riders=riders)
    return _ret(core, rr, riders)


def rope_bwd(dq, dkc, dkp, dvc, dvp, cos, sin, *, name, riders=()):
    s = dq.shape[0]
    tm = 2 * WINDOW if s % (2 * WINDOW) == 0 else WINDOW
    nb = s // tm

    def body(dq_ref, dkc_ref, dkp_ref, dkp_next_ref, dvc_ref, dvp_ref, dvp_next_ref, c_ref, s_ref, o_ref, db_ref):
        i = pl.program_id(0)
        has_next = (i < nb - 1).astype(F32)
        cosv, sinv = c_ref[...], s_ref[...]

        def shifted(ref, next_ref, cols):
            last = has_next * next_ref[:WINDOW, cols].astype(F32)
            return last if tm == WINDOW else jnp.concatenate([ref[WINDOW:, cols].astype(F32), last], axis=0)

        parts = []
        for blk in range(QKV_WIDTH // LANES):
            if blk < Q_WIDTH // LANES:
                g = dq_ref[:, blk * LANES:(blk + 1) * LANES].astype(F32)
            else:
                own, prv, nxt = (dkc_ref, dkp_ref, dkp_next_ref) if blk < N_ROPE_BLOCKS else (dvc_ref, dvp_ref, dvp_next_ref)
                cols = slice((blk % 2) * LANES, (blk % 2 + 1) * LANES)
                g = own[:, cols].astype(F32) + shifted(prv, nxt, cols)
            if blk < N_ROPE_BLOCKS:
                g = g * cosv + _swap_halves(g * sinv)
            o_ref[:, blk * LANES:(blk + 1) * LANES] = g.astype(BF16)
            parts.append(jnp.sum(g, axis=0, keepdims=True))
        sums = jnp.concatenate(parts, axis=1)
        _accum(db_ref, sums, i == 0)

    own_spec = _row_spec(tm, KV_WIDTH)
    next_spec = pl.BlockSpec((tm, KV_WIDTH), lambda i: (jnp.minimum(i + 1, nb - 1), 0))
    core, rr = _call(
        body, grid=(nb,),
        in_specs=[_row_spec(tm, Q_WIDTH), own_spec, own_spec, next_spec, own_spec, own_spec, next_spec,
                  _row_spec(tm, LANES), _row_spec(tm, LANES)],
        out_specs=[_row_spec(tm, QKV_WIDTH), _vec_spec(QKV_WIDTH)],
        out_shape=[jax.ShapeDtypeStruct((s, QKV_WIDTH), BF16), jax.ShapeDtypeStruct((1, QKV_WIDTH), F32)],
        operands=(dq, dkc, dkp, dkp, dvc, dvp, dvp, cos, sin), name=name, riders=riders)
    return _ret(core, rr, riders)


ROWS = GQA_GROUP * WINDOW


def _prev_slots():
    kpos = lax.broadcasted_iota(I32, (WINDOW, ROWS), 0)
    qpos = lax.broadcasted_iota(I32, (WINDOW, ROWS), 1) & (WINDOW - 1)
    return kpos > qpos


def _head_cols(ref, head):
    return ref[:, head * HEAD_DIM:(head + 1) * HEAD_DIM]


def _stack_heads(ref, h):
    return jnp.concatenate([_head_cols(ref, GQA_GROUP * h + g) for g in range(GQA_GROUP)], axis=0)


def _band(prev_ref, cur_ref, h):
    return jnp.concatenate([_head_cols(prev_ref, h), _head_cols(cur_ref, h)], axis=0)


def _pick(prev, band):
    return jnp.where(prev, band[:WINDOW], band[WINDOW:])


def _spread(prev, x):
    return jnp.concatenate([jnp.where(prev, x, 0.0), jnp.where(prev, 0.0, x)], axis=0).astype(BF16)


def _attn_probs(s_band, sink, prev, has_prev):
    scale = HEAD_DIM ** -0.5
    s = jnp.where(prev, jnp.where(has_prev, s_band[:WINDOW], NEG), s_band[WINDOW:]) * scale
    m = jnp.maximum(jnp.max(s, axis=0, keepdims=True), sink)
    e, es = jnp.exp(s - m), jnp.exp(sink - m)
    inv = 1.0 / (jnp.sum(e, axis=0, keepdims=True) + es)
    return e * inv, es * inv


def _attn_specs(nb):
    kcol, vcol = Q_WIDTH // KV_WIDTH, Q_WIDTH // KV_WIDTH + 1
    q_spec = pl.BlockSpec((WINDOW, Q_WIDTH), lambda n: (n, 0))
    return [q_spec,
            pl.BlockSpec((WINDOW, KV_WIDTH), lambda n: (n, kcol)),
            pl.BlockSpec((WINDOW, KV_WIDTH), lambda n: (jnp.maximum(n - 1, 0), kcol)),
            pl.BlockSpec((WINDOW, KV_WIDTH), lambda n: (n, vcol)),
            pl.BlockSpec((WINDOW, KV_WIDTH), lambda n: (jnp.maximum(n - 1, 0), vcol)),
            pl.BlockSpec((N_KV_HEADS, 8, ROWS), lambda n: (0, 0, 0))]


def attn_fwd(qkv, sink_rows, *, name, riders=()):
    s = qkv.shape[0]

    def body(q_ref, kc_ref, kp_ref, vc_ref, vp_ref, sink_ref, o_ref):
        prev = _prev_slots()
        has_prev = pl.program_id(0) > 0
        heads = range(N_KV_HEADS)
        s_bands = [lax.dot_general(_band(kp_ref, kc_ref, h), _stack_heads(q_ref, h), NT_DIMS, preferred_element_type=F32)
                   for h in heads]
        p_bands = [_spread(prev, _attn_probs(s_bands[h], sink_ref[h, 0:1, :], prev, has_prev)[0]) for h in heads]
        outs = [lax.dot_general(_band(vp_ref, vc_ref, h), p_bands[h], TN_DIMS, preferred_element_type=F32).T for h in heads]
        for h in heads:
            for g in range(GQA_GROUP):
                head = GQA_GROUP * h + g
                o_ref[:, head * HEAD_DIM:(head + 1) * HEAD_DIM] = outs[h][g * WINDOW:(g + 1) * WINDOW].astype(BF16)

    core, rr = _call(
        body, grid=(s // WINDOW,), in_specs=_attn_specs(s // WINDOW), out_specs=[pl.BlockSpec((WINDOW, Q_WIDTH), lambda n: (n, 0))],
        out_shape=[jax.ShapeDtypeStruct((s, Q_WIDTH), BF16)], operands=(qkv, qkv, qkv, qkv, qkv, sink_rows), sem=("parallel",),
        name=name, riders=riders)
    return _ret(core, rr, riders)


def attn_bwd(qkv, sink_rows, do, *, name, riders=()):
    s = qkv.shape[0]

    def body(q_ref, kc_ref, kp_ref, vc_ref, vp_ref, sink_ref, do_ref, dq_ref, dkc_ref, dkp_ref, dvc_ref, dvp_ref, dsink_ref):
        n = pl.program_id(0)
        prev = _prev_slots()
        scale = HEAD_DIM ** -0.5
        heads = range(N_KV_HEADS)
        qs, dos = [_stack_heads(q_ref, h) for h in heads], [_stack_heads(do_ref, h) for h in heads]
        kbands, vbands = [_band(kp_ref, kc_ref, h) for h in heads], [_band(vp_ref, vc_ref, h) for h in heads]
        s_bands = [lax.dot_general(kbands[h], qs[h], NT_DIMS, preferred_element_type=F32) for h in heads]
        dp_bands = [lax.dot_general(vbands[h], dos[h], NT_DIMS, preferred_element_type=F32) for h in heads]
        ds_bands, p_bands, parts = [], [], []
        for h in heads:
            p, ps = _attn_probs(s_bands[h], sink_ref[h, 0:1, :], prev, n > 0)
            dp = _pick(prev, dp_bands[h])
            delta = jnp.sum(p * dp, axis=0, keepdims=True)
            ds_bands.append(_spread(prev, p * (dp - delta) * scale))
            p_bands.append(_spread(prev, p))
            dsink = -(ps * delta)
            for g in range(GQA_GROUP):
                parts.append(jnp.broadcast_to(jnp.sum(dsink[:, g * WINDOW:(g + 1) * WINDOW], axis=1, keepdims=True), (8, LANES)))
        for h in heads:
            dk = jnp.dot(ds_bands[h], qs[h], preferred_element_type=F32).astype(BF16)
            dv = jnp.dot(p_bands[h], dos[h], preferred_element_type=F32).astype(BF16)
            dq = lax.dot_general(kbands[h], ds_bands[h], TN_DIMS, preferred_element_type=F32).T
            cols = slice(h * HEAD_DIM, (h + 1) * HEAD_DIM)
            dkp_ref[:, cols], dkc_ref[:, cols] = dk[:WINDOW], dk[WINDOW:]
            dvp_ref[:, cols], dvc_ref[:, cols] = dv[:WINDOW], dv[WINDOW:]
            for g in range(GQA_GROUP):
                head = GQA_GROUP * h + g
                dq_ref[:, head * HEAD_DIM:(head + 1) * HEAD_DIM] = dq[g * WINDOW:(g + 1) * WINDOW].astype(BF16)

        @pl.when(n == 0)
        def _():
            for i, part in enumerate(parts):
                dsink_ref[i // GQA_GROUP, i % GQA_GROUP] = part

        @pl.when(n > 0)
        def _():
            for i, part in enumerate(parts):
                dsink_ref[i // GQA_GROUP, i % GQA_GROUP] += part

    rows_q = pl.BlockSpec((WINDOW, Q_WIDTH), lambda n: (n, 0))
    rows_kv = pl.BlockSpec((WINDOW, KV_WIDTH), lambda n: (n, 0))
    kv_shape = jax.ShapeDtypeStruct((s, KV_WIDTH), BF16)
    core, rr = _call(
        body, grid=(s // WINDOW,), in_specs=_attn_specs(s // WINDOW) + [rows_q],
        out_specs=[rows_q, rows_kv, rows_kv, rows_kv, rows_kv,
                   pl.BlockSpec((N_KV_HEADS, GQA_GROUP, 8, LANES), lambda n: (0, 0, 0, 0))],
        out_shape=[jax.ShapeDtypeStruct((s, Q_WIDTH), BF16), kv_shape, kv_shape, kv_shape, kv_shape,
                   jax.ShapeDtypeStruct((N_KV_HEADS, GQA_GROUP, 8, LANES), F32)],
        operands=(qkv, qkv, qkv, qkv, qkv, sink_rows, do), sem=("arbitrary",), name=name, riders=riders)
    return _ret(core, rr, riders)


GELU_C = 0.7978845608028654
GELU_A = 0.044715


def _gelu(x):
    return 0.5 * x * (1.0 + jnp.tanh(x * (GELU_C + (GELU_C * GELU_A) * (x * x))))


def _gelu_and_grad(x):
    x2 = x * x
    t = jnp.tanh(x * (GELU_C + (GELU_C * GELU_A) * x2))
    half_x, one_t = 0.5 * x, 1.0 + t
    return half_x * one_t, 0.5 * one_t + half_x * (1.0 - t * t) * (GELU_C + (3.0 * GELU_C * GELU_A) * x2)


def _tril_bf16(w):
    row = lax.broadcasted_iota(I32, (SGU_CHUNK, SGU_CHUNK), 0)
    col = lax.broadcasted_iota(I32, (SGU_CHUNK, SGU_CHUNK), 1)
    return jnp.where(row >= col, w, 0.0).astype(BF16)


def _sgu_norm(vg, g, b):
    mu = jnp.mean(vg, axis=-1, keepdims=True)
    cen = vg - mu
    rstd = lax.rsqrt(jnp.mean(cen * cen, axis=-1, keepdims=True) + EPS)
    xhat = cen * rstd
    return xhat, rstd, xhat * g + b


def sgu_in_fwd(h, w_in, ln_g, ln_b, w_sp, b_sp, *, name, tm=256, riders=()):
    s, k = h.shape
    ns = w_in.shape[2]
    tm = _row_tile(s, tm)

    def body(h_ref, w0, w1, w2, w3, g_ref, b_ref, w_ref, bs_ref, z_ref, y_ref):
        hv = h_ref[...]
        zs = [jnp.dot(hv, w_ref_j[...], preferred_element_type=F32) for w_ref_j in (w0, w1, w2, w3)]
        for j, zj in enumerate(zs):
            z_ref[:, j * ns:(j + 1) * ns] = zj.astype(BF16)
        u = _gelu(jnp.concatenate(zs[:2], axis=1))
        _, _, vn = _sgu_norm(_gelu(jnp.concatenate(zs[2:], axis=1)), g_ref[...], b_ref[...])
        vn = vn.astype(BF16)
        for grp in range(SGU_GROUPS):
            w = _tril_bf16(w_ref[grp])
            cols = slice(grp * LANES, (grp + 1) * LANES)
            for ch in range(tm // SGU_CHUNK):
                rows = slice(ch * SGU_CHUNK, (ch + 1) * SGU_CHUNK)
                mixed = jnp.dot(w, vn[rows, cols], preferred_element_type=F32) + bs_ref[grp]
                y_ref[rows, cols] = (u[rows, cols] * mixed).astype(BF16)

    def shard(j):
        return pl.BlockSpec((None, k, ns), lambda i: (j, 0, 0))

    full3 = pl.BlockSpec((SGU_GROUPS, SGU_CHUNK, SGU_CHUNK), lambda i: (0, 0, 0))
    core, rr = _call(
        body, grid=(s // tm,),
        in_specs=[_row_spec(tm, k)] + [shard(j) for j in range(N_CHIPS)] + [_vec_spec(D_MODEL), _vec_spec(D_MODEL), full3, full3],
        out_specs=[_row_spec(tm, 2 * D_MODEL), _row_spec(tm, D_MODEL)],
        out_shape=[jax.ShapeDtypeStruct((s, 2 * D_MODEL), BF16), jax.ShapeDtypeStruct((s, D_MODEL), BF16)],
        operands=(h, w_in, w_in, w_in, w_in, ln_g, ln_b, w_sp, b_sp), sem=("parallel",), name=name, riders=riders)
    return _ret(core, rr, riders)


def sgu_bwd(z, dy, ln_g, ln_b, w_sp, b_sp, *, name, tm=256, riders=()):
    s = z.shape[0]
    tm = _row_tile(s, tm)

    def body(z_ref, dy_ref, g_ref, b_ref, w_ref, bs_ref, dz_ref, dw_ref, dbs_ref, dg_ref, db_ref, dvn_buf):
        first = pl.program_id(0) == 0
        u, u_grad = _gelu_and_grad(z_ref[:, :D_MODEL].astype(F32))
        vg, v_grad = _gelu_and_grad(z_ref[:, D_MODEL:].astype(F32))
        xhat, rstd, vn = _sgu_norm(vg, g_ref[...], b_ref[...])
        vn = vn.astype(BF16)
        dyv = dy_ref[...]
        dmixed = dyv * u
        dz_gate = dyv * u_grad
        row = lax.broadcasted_iota(I32, (SGU_CHUNK, SGU_CHUNK), 0)
        col = lax.broadcasted_iota(I32, (SGU_CHUNK, SGU_CHUNK), 1)
        dws, dbss = [], []
        for grp in range(SGU_GROUPS):
            w = _tril_bf16(w_ref[grp])
            cols = slice(grp * LANES, (grp + 1) * LANES)
            dw = jnp.zeros((SGU_CHUNK, SGU_CHUNK), F32)
            dbs = jnp.zeros((SGU_CHUNK, 1), F32)
            for ch in range(tm // SGU_CHUNK):
                rows = slice(ch * SGU_CHUNK, (ch + 1) * SGU_CHUNK)
                vblk = vn[rows, cols]
                mixed = jnp.dot(w, vblk, preferred_element_type=F32) + bs_ref[grp]
                dz_ref[rows, cols] = (dz_gate[rows, cols] * mixed).astype(BF16)
                dm = dmixed[rows, cols]
                dmb = dm.astype(BF16)
                dvn_buf[rows, cols] = lax.dot_general(w, dmb, TN_DIMS, preferred_element_type=F32)
                dw += lax.dot_general(dmb, vblk, NT_DIMS, preferred_element_type=F32)
                dbs += jnp.sum(dm, axis=-1, keepdims=True)
            dws.append(jnp.where(row >= col, dw, 0.0))
            dbss.append(jnp.broadcast_to(dbs, (SGU_CHUNK, SGU_CHUNK)))

        dvn = dvn_buf[...]
        dxhat = dvn * g_ref[...]
        dvg = rstd * (dxhat - jnp.mean(dxhat, axis=-1, keepdims=True) - xhat * jnp.mean(dxhat * xhat, axis=-1, keepdims=True))
        dz_ref[:, D_MODEL:] = (dvg * v_grad).astype(BF16)
        dlng, dlnb = jnp.sum(dvn * xhat, axis=0, keepdims=True), jnp.sum(dvn, axis=0, keepdims=True)

        @pl.when(first)
        def _():
            for grp in range(SGU_GROUPS):
                dw_ref[grp] = dws[grp]
                dbs_ref[grp] = dbss[grp]
            dg_ref[...] = dlng
            db_ref[...] = dlnb

        @pl.when(jnp.logical_not(first))
        def _():
            for grp in range(SGU_GROUPS):
                dw_ref[grp] += dws[grp]
                dbs_ref[grp] += dbss[grp]
            dg_ref[...] += dlng
            db_ref[...] += dlnb

    full3 = pl.BlockSpec((SGU_GROUPS, SGU_CHUNK, SGU_CHUNK), lambda i: (0, 0, 0))
    s3 = jax.ShapeDtypeStruct((SGU_GROUPS, SGU_CHUNK, SGU_CHUNK), F32)
    vshape = jax.ShapeDtypeStruct((1, D_MODEL), F32)
    core, rr = _call(
        body, grid=(s // tm,),
        in_specs=[_row_spec(tm, 2 * D_MODEL), _row_spec(tm, D_MODEL), _vec_spec(D_MODEL), _vec_spec(D_MODEL), full3, full3],
        out_specs=[_row_spec(tm, 2 * D_MODEL), full3, full3, _vec_spec(D_MODEL), _vec_spec(D_MODEL)],
        out_shape=[jax.ShapeDtypeStruct((s, 2 * D_MODEL), BF16), s3, s3, vshape, vshape],
        scratch_shapes=[pltpu.VMEM((tm, D_MODEL), F32)], operands=(z, dy, ln_g, ln_b, w_sp, b_sp), name=name, riders=riders)
    return _ret(core, rr, riders)


def _sigmoid(x):
    return 1.0 / (1.0 + jnp.exp(-x))


def ffn_up(h, w_gu, *, name, tm=512, riders=()):
    s = h.shape[0]
    tm = _row_tile(s, tm)

    def body(h_ref, wg_ref, wu_ref, d_ref, a_ref):
        hv = h_ref[...]
        sub = min(256, tm)
        for t in range(tm // sub):
            rows = slice(t * sub, (t + 1) * sub)
            g = jnp.dot(hv[rows], wg_ref[...], preferred_element_type=F32)
            u = jnp.dot(hv[rows], wu_ref[...], preferred_element_type=F32)
            sig = _sigmoid(g)
            silu = g * sig
            d_ref[0, rows, :] = (u * (sig + silu * (1.0 - sig))).astype(BF16)
            d_ref[1, rows, :] = silu.astype(BF16)
            a_ref[rows, :] = (silu * u).astype(BF16)

    core, rr = _call(
        body, grid=(2, s // tm),
        in_specs=[pl.BlockSpec((tm, D_MODEL), lambda j, i: (i, 0)),
                  pl.BlockSpec((None, D_MODEL, FF_HALF), lambda j, i: (j, 0, 0)),
                  pl.BlockSpec((None, D_MODEL, FF_HALF), lambda j, i: (j + 2, 0, 0))],
        out_specs=[pl.BlockSpec((2, tm, FF_HALF), lambda j, i: (0, i, j)), pl.BlockSpec((tm, FF_HALF), lambda j, i: (i, j))],
        out_shape=[jax.ShapeDtypeStruct((2, s, D_FF), BF16), jax.ShapeDtypeStruct((s, D_FF), BF16)],
        operands=(h, w_gu, w_gu), sem=("parallel", "parallel"), name=name, riders=riders)
    return _ret(core, rr, riders)


def ffn_dact(df, w_d, gu, *, name, tm=512, riders=()):
    s = df.shape[0]
    tm = _row_tile(s, tm)

    def body(df_ref, w_ref, d_ref, o_ref):
        da = lax.dot_general(df_ref[...], w_ref[...], NT_DIMS, preferred_element_type=F32)
        o_ref[0] = (da * d_ref[0].astype(F32)).astype(BF16)
        o_ref[1] = (da * d_ref[1].astype(F32)).astype(BF16)

    planes = pl.BlockSpec((2, tm, FF_HALF), lambda j, i: (0, i, j))
    core, rr = _call(
        body, grid=(2, s // tm),
        in_specs=[pl.BlockSpec((tm, D_MODEL), lambda j, i: (i, 0)), pl.BlockSpec((FF_HALF, D_MODEL), lambda j, i: (j, 0)), planes],
        out_specs=[planes], out_shape=[jax.ShapeDtypeStruct((2, s, D_FF), BF16)], operands=(df, w_d, gu),
        sem=("parallel", "parallel"), name=name, riders=riders)
    return _ret(core, rr, riders)


def _weight_tile(rows):
    for tr in (512, 352, 256, 128):
        if rows % tr == 0:
            return tr
    return rows


def place_shard(w, layer, chip_arr, dtype, *, name, riders=()):
    _, r, c = w.shape
    tr = _weight_tile(r)

    def body(chip_ref, w_ref, o_ref):
        o_ref[...] = w_ref[...].astype(dtype)

    core, rr = _call(
        body, grid=(r // tr,), prefetch=(chip_arr,),
        in_specs=[pl.BlockSpec((None, tr, c), lambda i, chip: (layer, i, 0))],
        out_specs=[pl.BlockSpec((None, tr, c), lambda i, chip: (chip[0], i, 0))],
        out_shape=[jax.ShapeDtypeStruct((N_CHIPS, r, c), dtype)], operands=(w,), sem=("parallel",), name=name, riders=riders)
    return _ret(core, rr, riders)


def _adamw_math(w, g, m, v):
    m = ADAM_B1 * m + (1.0 - ADAM_B1) * g
    v = ADAM_B2 * v + (1.0 - ADAM_B2) * (g * g)
    m_hat = m / (1.0 - ADAM_B1 ** ADAM_STEP)
    v_hat = v / (1.0 - ADAM_B2 ** ADAM_STEP)
    delta = -ADAM_LR * (m_hat / (jnp.sqrt(v_hat) + ADAM_EPS) + ADAM_WD * w)
    return delta, m, v


def adamw(w, g, m, v, *, name, after=None):
    nl, r, c = w.shape
    tr = _weight_tile(r)

    def body(w_ref, g_ref, m_ref, v_ref, *rest):
        go_ref, d_ref, mo_ref, vo_ref = rest[-4:]
        gv = g_ref[...]
        go_ref[...] = gv
        d_ref[...], mo_ref[...], vo_ref[...] = _adamw_math(w_ref[...], gv, m_ref[...], v_ref[...])

    spec = pl.BlockSpec((None, tr, c), lambda l, i: (l, i, 0))
    shape = jax.ShapeDtypeStruct(w.shape, F32)
    extra = [] if after is None else [after]
    outs, _ = _call(body, grid=(nl, r // tr), in_specs=[spec] * 4 + [ANY] * len(extra), out_specs=[spec] * 4,
                    out_shape=[shape] * 4, operands=(w, g, m, v, *extra), sem=("parallel", "parallel"), name=name)
    return outs


def adamw_small(ws, gs, ms, vs, *, name):
    n = len(ws)

    def body(*refs):
        ins, outs = refs[:4 * n], refs[4 * n:]
        for t in range(n):
            gv = ins[n + t][...]
            outs[t][...] = gv
            outs[n + t][...], outs[2 * n + t][...], outs[3 * n + t][...] = _adamw_math(
                ins[t][...], gv, ins[2 * n + t][...], ins[3 * n + t][...])

    shapes = [jax.ShapeDtypeStruct(w.shape, F32) for w in ws]
    res = pl.pallas_call(body, out_shape=shapes * 4, name=name)(*ws, *gs, *ms, *vs)
    return res[:n], res[n:2 * n], res[2 * n:3 * n], res[3 * n:]


def pair_add(g, r1, c_arr, *, name):
    _, rows, cdim = g.shape
    h = rows // 2

    def body(c_ref, g_ref, r_ref, o_ref):
        o_ref[...] = (g_ref[...].astype(F32) + r_ref[...].astype(F32)).astype(o_ref.dtype)

    (out,), _ = _call(
        body, grid=(N_CHIPS,), prefetch=(c_arr,),
        in_specs=[pl.BlockSpec((None, h, cdim), lambda s, c: (s, c[0], 0)), pl.BlockSpec((None, h, cdim), lambda s, c: (s, 0, 0))],
        out_specs=[pl.BlockSpec((None, h, cdim), lambda s, c: (s, 0, 0))],
        out_shape=[jax.ShapeDtypeStruct((N_CHIPS, h, cdim), g.dtype)], operands=(g, r1), sem=("parallel",), name=name)
    return out


def final_add(g, r1, r2, jc_arr, *, dest_shape, lead, prev, name):
    _, rows, cdim = g.shape
    h = rows // 2

    def body(jc_ref, g_ref, r1_ref, r2_ref, *rest):
        o_ref = rest[-1]
        acc = g_ref[...].astype(F32) + r1_ref[...].astype(F32)
        for k in range(3):
            acc = acc + r2_ref[k].astype(F32)
        o_ref[...] = acc

    if lead is None:
        o_spec = pl.BlockSpec((h, cdim), lambda i, jc: (jc[1], 0))
    elif lead == "chip":
        o_spec = pl.BlockSpec((None, h, cdim), lambda i, jc: (jc[0], jc[1], 0))
    else:
        o_spec = pl.BlockSpec((None, h, cdim), lambda i, jc: (lead, jc[1], 0))
    in_specs = [pl.BlockSpec((None, h, cdim), lambda i, jc: (jc[0], jc[1], 0)),
                pl.BlockSpec((None, h, cdim), lambda i, jc: (jc[0], 0, 0)),
                pl.BlockSpec((3, h, cdim), lambda i, jc: (0, 0, 0))]
    operands = [g, r1, r2]
    aliases = None
    if prev is not None:
        in_specs.append(ANY)
        operands.append(prev)
        aliases = {3: 0}
    (out,), _ = _call(body, grid=(1,), prefetch=(jc_arr,), in_specs=in_specs, out_specs=[o_spec],
                      out_shape=[jax.ShapeDtypeStruct(dest_shape, F32)], operands=operands, aliases=aliases, name=name)
    return out


def _place():
    return lax.axis_index("x"), lax.axis_index("y"), lax.axis_index("c")


def _partner(x, y, k):
    return (1 - x if k >> 1 else x), (1 - y if k & 1 else y)


WHOLE = (0, 1, 1)


def _half(rows, sel, dtype, piece=WHOLE):
    lo, hi, n = piece
    align = 16 if dtype == BF16 else 8
    step = rows // 2 // n
    assert rows // 2 == step * n and step % align == 0
    return pl.ds(pl.multiple_of(sel * (rows // 2) + lo * step, align), (hi - lo) * step)


def _rider(peers, inputs, aliased, fresh, nsem, copies, arrivals):
    def start(ins, outs, send, recv):
        for cp in copies(ins, outs, send, recv):
            cp.start()

    def finish(ins, outs, send, recv):
        for cp in arrivals(ins, outs, send, recv):
            cp.wait_recv()
        for cp in copies(ins, outs, send, recv):
            cp.wait_send()

    return types.SimpleNamespace(peers=peers, inputs=list(inputs), aliased=list(aliased), fresh=list(fresh), nsem=nsem,
                                 start=start, finish=finish)


def _remote(src, dst, send, recv, idx, dev):
    return pltpu.make_async_remote_copy(src_ref=src, dst_ref=dst, send_sem=send.at[idx], recv_sem=recv.at[idx],
                                        device_id=dev, device_id_type=MESH)


def gather_ici_rider(fulls, pieces=None):
    nt = len(fulls)
    pieces = pieces or [WHOLE] * nt

    def region(outs, t, slot, sel):
        return outs[t].at[slot, _half(fulls[t].shape[1], sel, fulls[t].dtype, pieces[t])]

    def copies(ins, outs, send, recv):
        x, y, c = _place()
        res = []
        for t in range(nt):
            for k in (1, 2, 3):
                px, py = _partner(x, y, k)
                mine = region(outs, t, 2 * x + y, c)
                res.append(_remote(mine, mine, send, recv, 3 * t + k - 1, (px, py, c)))
        return res

    def arrivals(ins, outs, send, recv):
        x, y, c = _place()
        res = []
        for t in range(nt):
            for k in (1, 2, 3):
                px, py = _partner(x, y, k)
                theirs = region(outs, t, 2 * px + py, c)
                res.append(_remote(theirs, theirs, send, recv, 3 * t + k - 1, (x, y, c)))
        return res

    return _rider("chips", fulls, range(nt), [], 3 * nt, copies, arrivals)


def gather_d2d_rider(fulls, pieces=None):
    nt = len(fulls)
    pieces = pieces or [WHOLE] * nt

    def region(outs, t, slot, sel):
        return outs[t].at[slot, _half(fulls[t].shape[1], sel, fulls[t].dtype, pieces[t])]

    def both(outs, send, recv, mine):
        x, y, c = _place()
        res = []
        for t in range(nt):
            for k in (1, 2, 3):
                px, py = _partner(x, y, k)
                part = region(outs, t, 2 * px + py, c if mine else 1 - c)
                res.append(_remote(part, part, send, recv, 3 * t + k - 1, (x, y, 1 - c)))
        return res

    return _rider("sibling", fulls, range(nt), [], 3 * nt, lambda i, o, s, r: both(o, s, r, True),
                  lambda i, o, s, r: both(o, s, r, False))


def exchange_rider(grads):
    nt = len(grads)

    def both(ins, outs, send, recv):
        x, y, c = _place()
        return [_remote(ins[t].at[:, _half(grads[t].shape[1], 1 - c, grads[t].dtype)], outs[t], send, recv, t, (x, y, 1 - c))
                for t in range(nt)]

    fresh = [jax.ShapeDtypeStruct((N_CHIPS, g.shape[1] // 2, g.shape[2]), g.dtype) for g in grads]
    return _rider("sibling", grads, [], fresh, nt, both, both)


def scatter_rider(parts):
    nt = len(parts)

    def both(ins, outs, send, recv):
        x, y, c = _place()
        res = []
        for t in range(nt):
            for k in (1, 2, 3):
                px, py = _partner(x, y, k)
                res.append(_remote(ins[t].at[2 * px + py], outs[t].at[k - 1], send, recv, 3 * t + k - 1, (px, py, c)))
        return res

    fresh = [jax.ShapeDtypeStruct((3,) + p.shape[1:], p.dtype) for p in parts]
    return _rider("chips", parts, [], fresh, 3 * nt, both, both)


def broadcast_rider(bufs, items):
    def region(outs, item, sel):
        bi, lead = item
        ref = outs[bi]
        if lead == "chip":
            x, y, _ = _place()
            ref = ref.at[2 * x + y]
        elif lead is not None:
            ref = ref.at[lead]
        return ref.at[_half(ref.shape[0], sel, F32)]

    def both(outs, send, recv, mine):
        x, y, c = _place()
        res = []
        for i, item in enumerate(items):
            part = region(outs, item, c if mine else 1 - c)
            res.append(_remote(part, part, send, recv, i, (x, y, 1 - c)))
        return res

    return _rider("sibling", bufs, range(len(bufs)), [], len(items), lambda i, o, s, r: both(o, s, r, True),
                  lambda i, o, s, r: both(o, s, r, False))


def allcast_rider(buf):
    peers = [(k, flip) for k in range(N_CHIPS) for flip in (0, 1) if (k, flip) != (0, 0)]

    def both(outs, send, recv, mine):
        x, y, c = _place()
        res = []
        for i, (k, flip) in enumerate(peers):
            px, py = _partner(x, y, k)
            pc = 1 - c if flip else c
            slot, sel = (2 * x + y, c) if mine else (2 * px + py, pc)
            part = outs[0].at[slot, _half(buf.shape[1], sel, F32)]
            res.append(_remote(part, part, send, recv, i, (px, py, pc)))
        return res

    return _rider("everyone", [buf], [0], [], len(peers), lambda i, o, s, r: both(o, s, r, True),
                  lambda i, o, s, r: both(o, s, r, False))


def comm_call(riders, *, name):
    _, res = _call(None, riders=riders, name=name)
    return res


SEMS = pl.BlockSpec(memory_space=pltpu.SEMAPHORE)
SIDE_EFFECT = pltpu.SideEffectType.DATAFLOW_SIDE_EFFECTING


def _split_refs(riders, refs):
    views, p = [], 0
    for r in riders:
        bufs = refs[p:p + len(r.inputs) + len(r.fresh)]
        p += len(bufs)
        ins = bufs[:len(r.inputs)]
        views.append([ins, [ins[i] for i in r.aliased] + list(bufs[len(r.inputs):])])
    for view in views:
        view += [refs[p], refs[p + 1]]
        p += 2
    return views


def comm_start(riders, *, name):
    kind = _peer_kind(riders)
    bufs = [a for r in riders for a in r.inputs]
    fresh = [f for r in riders for f in r.fresh]
    n_buf, n_fresh = len(bufs), len(fresh)

    def body(*refs):
        ins, outs = refs[:n_buf], refs[n_buf:]
        through, land, sems = outs[:n_buf], outs[n_buf:n_buf + n_fresh], outs[n_buf + n_fresh:-1]
        _peer_barrier(kind)
        per_rider, pb, pf = [], 0, 0
        for r in riders:
            per_rider += list(through[pb:pb + len(r.inputs)]) + list(land[pf:pf + len(r.fresh)])
            pb, pf = pb + len(r.inputs), pf + len(r.fresh)
        for r, (r_ins, r_outs, send, recv) in zip(riders, _split_refs(riders, per_rider + list(sems))):
            r.start(r_ins, r_outs, send, recv)
        outs[-1][...] = jnp.zeros((8, LANES), F32)

    sem_shapes = [pltpu.SemaphoreType.DMA((r.nsem,)) for r in riders for _ in (0, 1)]
    res = pl.pallas_call(
        body, name=name, in_specs=[ANY] * n_buf,
        out_specs=[ANY] * (n_buf + n_fresh) + [SEMS] * len(sem_shapes) + [pl.BlockSpec(memory_space=pltpu.VMEM)],
        out_shape=[jax.ShapeDtypeStruct(a.shape, a.dtype) for a in bufs] + fresh + sem_shapes
        + [jax.ShapeDtypeStruct((8, LANES), F32)],
        input_output_aliases={i: i for i in range(n_buf)},
        compiler_params=pltpu.CompilerParams(has_side_effects=SIDE_EFFECT, collective_id=PEER_KINDS.index(kind)))(*bufs)
    return (riders, list(res[:n_buf + n_fresh]), list(res[n_buf + n_fresh:-1])), res[-1]


def comm_wait(state, after, *, name):
    riders, bufs, sems = state
    n_buf, n_sem = len(bufs), len(sems)
    n_in = sum(len(r.inputs) for r in riders)

    def body(*refs):
        held, sem_refs = refs[:n_buf], refs[n_buf:n_buf + n_sem]
        through, land = held[:n_in], held[n_in:]
        per_rider, pb, pf = [], 0, 0
        for r in riders:
            per_rider += list(through[pb:pb + len(r.inputs)]) + list(land[pf:pf + len(r.fresh)])
            pb, pf = pb + len(r.inputs), pf + len(r.fresh)
        for r, (r_ins, r_outs, send, recv) in zip(riders, _split_refs(riders, per_rider + list(sem_refs))):
            r.finish(r_ins, r_outs, send, recv)

    res = pl.pallas_call(
        body, name=name, in_specs=[ANY] * n_buf + [SEMS] * n_sem + [ANY], out_specs=[ANY] * n_buf,
        out_shape=[jax.ShapeDtypeStruct(a.shape, a.dtype) for a in bufs],
        input_output_aliases={i: i for i in range(n_buf)},
        compiler_params=pltpu.CompilerParams(has_side_effects=SIDE_EFFECT))(*bufs, *sems, after)
    through, land = list(res[:n_in]), list(res[n_in:])
    out, pb, pf = [], 0, 0
    for r in riders:
        r_ins, r_land = through[pb:pb + len(r.inputs)], land[pf:pf + len(r.fresh)]
        pb, pf = pb + len(r.inputs), pf + len(r.fresh)
        out.append([r_ins[i] for i in r.aliased] + r_land)
    return out


SLAB_ROWS = 192


def _pad_rows(a, rows=8):
    return jnp.pad(a, ((0, rows - a.shape[0]), (0, 0)))


def _pack_small(norm_grads, db_qkv, db_o, dsinks, db_sp, dln_g, dln_b, dw_sp, loss_part):
    parts = [
        jnp.concatenate(norm_grads, axis=0),
        _pad_rows(jnp.pad(db_qkv, ((0, 0), (0, 2 * D_MODEL - QKV_WIDTH))).reshape(2, D_MODEL)),
        _pad_rows(db_o),
        _pad_rows(jnp.pad(dsinks.reshape(1, N_Q_HEADS), ((0, 0), (0, D_MODEL - N_Q_HEADS)))),
        _pad_rows(db_sp.reshape(1, D_MODEL)),
        _pad_rows(jnp.concatenate([dln_g, dln_b, jnp.pad(loss_part[0:1], ((0, 0), (0, D_MODEL - LANES)))], axis=0)),
        dw_sp.reshape(SGU_CHUNK, D_MODEL),
    ]
    slab = jnp.concatenate(parts, axis=0)
    return jnp.pad(slab, ((0, SLAB_ROWS - slab.shape[0]), (0, 0))).reshape(N_CHIPS, SLAB_ROWS // N_CHIPS, D_MODEL)


def _unpack_small(slab, j):
    slab = slab.reshape(SLAB_ROWS, D_MODEL)
    norms = [slab[2 * i:2 * i + 2] for i in range(4)]
    db_qkv = slab[8:10].reshape(1, 2 * D_MODEL)[:, :QKV_WIDTH]
    db_o = slab[16:17]
    dsinks = slab[24:25, :N_Q_HEADS]
    db_sp = slab[32:33].reshape(SGU_GROUPS, SGU_CHUNK)
    width = D_MODEL // N_CHIPS
    dln_g = lax.dynamic_slice(slab[40:41], (0, j * width), (1, width))
    dln_b = lax.dynamic_slice(slab[41:42], (0, j * width), (1, width))
    dw_sp = slab[48:48 + SGU_CHUNK].reshape(SGU_GROUPS * SGU_CHUNK, SGU_CHUNK)
    return norms, db_qkv, db_o, dsinks, db_sp, dln_g, dln_b, dw_sp, slab[42, 0]


class _GradReduce:
    def __init__(self, c_arr, jc_arr, dest_shapes):
        self.c_arr, self.jc_arr, self.dest_shapes = c_arr, jc_arr, dest_shapes
        self.grad, self.sibling, self.pair, self.chips, self.dest = {}, {}, {}, {}, {}

    def exchange(self, tags):
        return exchange_rider([self.grad[t] for t in tags])

    def exchanged(self, tags, res):
        for t, r in zip(tags, res):
            self.sibling[t] = r
            self.pair[t] = pair_add(self.grad[t], r, self.c_arr, name=f"pair_add_{t}")

    def scatter(self, tags):
        return scatter_rider([self.pair[t] for t in tags])

    def scattered(self, tags, res, where):
        for t, r in zip(tags, res):
            name, lead = where[t]
            self.dest[name] = final_add(self.grad[t], self.sibling[t], r, self.jc_arr, dest_shape=self.dest_shapes[name],
                                        lead=lead, prev=self.dest.get(name), name=f"final_add_{t}")

    def broadcast(self, items):
        names = []
        for n, _ in items:
            if n not in names:
                names.append(n)
        return names, broadcast_rider([self.dest[n] for n in names], [(names.index(n), lead) for n, lead in items])

    def broadcasted(self, names, res):
        for n, r in zip(names, res):
            self.dest[n] = r


def kernel(x, norm_mix_pre, norm_mix_post, norm_ffn_pre, norm_ffn_post, attn_w_qkv, attn_b_qkv, attn_sinks, attn_w_o, attn_b_o, sgu_w_in, sgu_ln_g, sgu_ln_b, sgu_w_spatial, sgu_b_spatial, sgu_w_out, ffn_w_gate_up, ffn_w_down, loss_target, m_norm_mix_pre, m_norm_mix_post, m_norm_ffn_pre, m_norm_ffn_post, m_attn_w_qkv, m_attn_b_qkv, m_attn_sinks, m_attn_w_o, m_attn_b_o, m_sgu_w_in, m_sgu_ln_g, m_sgu_ln_b, m_sgu_w_spatial, m_sgu_b_spatial, m_sgu_w_out, m_ffn_w_gate_up, m_ffn_w_down, v_norm_mix_pre, v_norm_mix_post, v_norm_ffn_pre, v_norm_ffn_post, v_attn_w_qkv, v_attn_b_qkv, v_attn_sinks, v_attn_w_o, v_attn_b_o, v_sgu_w_in, v_sgu_ln_g, v_sgu_ln_b, v_sgu_w_spatial, v_sgu_b_spatial, v_sgu_w_out, v_ffn_w_gate_up, v_ffn_w_down):
    s = x.shape[1]
    x0 = x.reshape(s, D_MODEL)
    target = loss_target.reshape(s, D_MODEL)
    mx, my, mc = lax.axis_index("x"), lax.axis_index("y"), lax.axis_index("c")
    chip = 2 * mx + my
    chip_arr = jnp.reshape(chip, (1,)).astype(I32)
    c_arr = jnp.reshape(mc, (1,)).astype(I32)
    jc_arr = jnp.stack([chip, mc]).astype(I32)
    zero_bias = jnp.zeros((1, D_MODEL), F32)

    def gain(p, i):
        return p[i:i + 1]

    big = [attn_w_qkv, attn_w_o, sgu_w_in, sgu_w_out, ffn_w_gate_up, ffn_w_gate_up, ffn_w_down, ffn_w_down]
    layers = [0, 0, 0, 0, 0, 1, 0, 1]
    tags = ["qkv", "wo", "win", "wout", "wgu0", "wgu1", "wd0", "wd1"]
    full = {t: place_shard(w, l, chip_arr, BF16, name=f"place_{t}") for w, l, t in zip(big, layers, tags) if t != "wgu1"}
    ln_pack = _pad_rows(jnp.concatenate([sgu_ln_g, sgu_ln_b], axis=0), 16)[None]
    full["ln"] = place_shard(ln_pack, 0, chip_arr, F32, name="place_ln")

    def split(items):
        return [i if isinstance(i, str) else i[0] for i in items], [WHOLE if isinstance(i, str) else tuple(i[1:]) for i in items]

    def ici(*items):
        names, pieces = split(items)
        return gather_ici_rider([full[n] for n in names], pieces)

    def d2d(*items):
        names, pieces = split(items)
        return gather_d2d_rider([full[n] for n in names], pieces)

    def landed(items, res):
        for n, r in zip(split(items)[0], res):
            full[n] = r

    cos, sin = _rope_tables(s)
    sink_rows = jnp.broadcast_to(
        jnp.repeat(attn_sinks.reshape(N_KV_HEADS, GQA_GROUP), WINDOW, axis=1)[:, None, :], (N_KV_HEADS, 8, ROWS))
    w_sp = sgu_w_spatial.reshape(SGU_GROUPS, SGU_CHUNK, SGU_CHUNK)
    b_sp = jnp.broadcast_to(sgu_b_spatial.reshape(SGU_GROUPS, SGU_CHUNK)[:, :, None], (SGU_GROUPS, SGU_CHUNK, LANES))

    h0, (res,) = prenorm(x0, gain(norm_mix_pre, 0), name="prenorm_0", riders=[ici("qkv", "ln")])
    landed(("qkv", "ln"), res)
    full["wgu1"], (res,) = place_shard(ffn_w_gate_up, 1, chip_arr, BF16, name="place_wgu1", riders=[d2d("qkv", "ln")])
    landed(("qkv", "ln"), res)
    ln_g = full["ln"][:, 0, :].reshape(1, D_MODEL)
    ln_b = full["ln"][:, 1, :].reshape(1, D_MODEL)

    def hosted(call, stages):
        outputs, results = call([{"ici": ici, "d2d": d2d}[kind](*items) for kind, items in stages])
        for (_, items), res in zip(stages, results):
            landed(items, res)
        return outputs

    qkv = hosted(lambda r: qkv_proj(h0, full["qkv"], attn_b_qkv, cos, sin, name="qkv_proj", riders=r),
                 [("ici", ("wo", ("wgu0", 0, 3, 8)))])
    o = hosted(lambda r: attn_fwd(qkv, sink_rows, name="attn_fwd", riders=r),
               [("d2d", ("wo",)), ("ici", (("wgu0", 3, 8, 8), ("wd0", 0, 2, 11)))])
    w_o = full["wo"].reshape(Q_WIDTH, D_MODEL)
    x1, h1, m0 = hosted(lambda r: proj_residual_norm(o, w_o, x0, attn_b_o, gain(norm_mix_post, 0), gain(norm_ffn_pre, 0),
                                                     name="attn_out_norm", riders=r),
                        [("d2d", ("wgu0",)), ("ici", (("wd0", 2, 11, 11),))])
    gu0, a0 = hosted(lambda r: ffn_up(h1, full["wgu0"], name="ffn_up_0", riders=r),
                     [("d2d", ("wd0",)), ("ici", ("win", "wout", ("wgu1", 0, 4, 8)))])
    w_d0 = full["wd0"].reshape(D_FF, D_MODEL)
    x2, h2, f0 = hosted(lambda r: proj_residual_norm(a0, w_d0, x1, zero_bias, gain(norm_ffn_post, 0), gain(norm_mix_pre, 1),
                                                     name="ffn_down_norm_0", riders=r),
                        [("d2d", ("win", "wout")), ("ici", (("wgu1", 4, 8, 8),))])
    w_in = full["win"]
    z, y = hosted(lambda r: sgu_in_fwd(h2, w_in, ln_g, ln_b, w_sp, b_sp, name="sgu_in_fwd", riders=r),
                  [("d2d", ("wgu1",)), ("ici", ("wd1",))])
    w_out = full["wout"].reshape(D_MODEL, D_MODEL)
    x3, h3, m1 = hosted(lambda r: proj_residual_norm(y, w_out, x2, zero_bias, gain(norm_mix_post, 1), gain(norm_ffn_pre, 1),
                                                     name="sgu_out_norm", riders=r),
                        [("d2d", ("wd1",))])
    w_qkv, w_gu0, w_gu1 = full["qkv"], full["wgu0"], full["wgu1"]
    w_d1 = full["wd1"].reshape(D_FF, D_MODEL)
    gu1, a1, dx4, df1, dg_fpost1, loss_part = ffn_fwd_loss_rows(
        h3, w_gu1, w_d1, x3, gain(norm_ffn_post, 1), target, name="ffn_fwd_loss_rows")

    red = _GradReduce(c_arr, jc_arr, {
        "qkv": attn_w_qkv.shape[1:], "wo": attn_w_o.shape[1:], "win": sgu_w_in.shape[1:], "wout": sgu_w_out.shape[1:],
        "wgu": ffn_w_gate_up.shape, "wd": ffn_w_down.shape, "slab": (N_CHIPS, SLAB_ROWS // N_CHIPS, D_MODEL)})
    where = {"qkv": ("qkv", None), "wo": ("wo", None), "win": ("win", None), "wout": ("wout", None), "wgu0": ("wgu", 0),
             "wgu1": ("wgu", 1), "wd0": ("wd", 0), "wd1": ("wd", 1), "small": ("slab", "chip")}

    dgu1, dx3, dm1, dg_fpre1, dg_mpost1, _ = ffn_bwd_rows(
        df1, w_d1, gu1, w_gu1, dx4, x3, gain(norm_ffn_pre, 1), m1, gain(norm_mix_post, 1), name="ffn_bwd_rows_1")
    red.grad["wd1"] = mm_tn(a1, df1, shard_major=False, tm=256, tn=D_MODEL, name="dw_down_1").reshape(
        N_CHIPS, D_FF // N_CHIPS, D_MODEL)
    red.grad["wgu1"], (res,) = mm_tn(h3, dgu1, shard_major=True, tm=512, tn=FF_HALF, name="dw_gate_up_1",
                                     riders=[red.exchange(["wd1"])])
    red.exchanged(["wd1"], res)
    dy, (res,) = mm_nt(dm1, w_out, out_dtype=F32, name="dy_sgu", riders=[red.exchange(["wgu1"])])
    red.exchanged(["wgu1"], res)
    red.grad["wout"] = mm_tn(y, dm1, shard_major=False, tm=512, tn=D_MODEL, name="dw_sgu_out").reshape(
        N_CHIPS, D_MODEL // N_CHIPS, D_MODEL)
    (dz, dw_sp, db_sp, dln_g, dln_b), (res_a, res_b) = sgu_bwd(
        z, dy, ln_g, ln_b, w_sp, b_sp, name="sgu_bwd", riders=[red.scatter(["wgu1"]), red.exchange(["wout"])])
    red.scattered(["wgu1"], res_a, where)
    red.exchanged(["wout"], res_b)
    names, rider = red.broadcast([("wgu", 1)])
    red.grad["win"], (res_a, res_b) = mm_tn(h2, dz, shard_major=True, tm=D_MODEL, tn=2 * D_MODEL // N_CHIPS, name="dw_sgu_in",
                                            riders=[rider, red.scatter(["wout"])])
    red.broadcasted(names, res_a)
    red.scattered(["wout"], res_b, where)
    names, rider = red.broadcast([("wout", None)])
    (dx2, df0, dg_mpre1, dg_fpost0, _), (res_a, res_b) = dh_norm_bwd_pair(
        dz, w_in, dx3, x2, gain(norm_mix_pre, 1), f0, gain(norm_ffn_post, 0), name="dh_sgu_norm",
        riders=[red.exchange(["win"]), rider])
    red.exchanged(["win"], res_a)
    red.broadcasted(names, res_b)
    (dgu0, dx1, dm0, dg_fpre0, dg_mpost0, db_o), (res,) = ffn_bwd_rows(
        df0, w_d0, gu0, w_gu0, dx2, x1, gain(norm_ffn_pre, 0), m0, gain(norm_mix_post, 0), name="ffn_bwd_rows_0",
        riders=[red.scatter(["wd1", "win"])])
    red.scattered(["wd1", "win"], res, where)
    names, rider = red.broadcast([("wd", 1), ("win", None)])
    dw_d0, (res,) = mm_tn(a0, df0, shard_major=False, tm=256, tn=D_MODEL, name="dw_down_0", riders=[rider])
    red.broadcasted(names, res)
    red.grad["wd0"] = dw_d0.reshape(N_CHIPS, D_FF // N_CHIPS, D_MODEL)
    do, (res,) = mm_nt(dm0, w_o, out_dtype=BF16, name="do_attn", riders=[red.exchange(["wd0"])])
    red.exchanged(["wd0"], res)
    red.grad["wgu0"], (res,) = mm_tn(h1, dgu0, shard_major=True, tm=512, tn=FF_HALF, name="dw_gate_up_0",
                                     riders=[red.scatter(["wd0"])])
    red.scattered(["wd0"], res, where)
    names, rider = red.broadcast([("wd", 0)])
    dw_o, (res_a, res_b) = mm_tn(o, dm0, shard_major=False, tm=512, tn=D_MODEL, name="dw_attn_out",
                                 riders=[red.exchange(["wgu0"]), rider])
    red.exchanged(["wgu0"], res_a)
    red.broadcasted(names, res_b)
    red.grad["wo"] = dw_o.reshape(N_CHIPS, Q_WIDTH // N_CHIPS, D_MODEL)
    (dq, dkc, dkp, dvc, dvp, dsink), (res_a, res_b) = attn_bwd(
        qkv, sink_rows, do, name="attn_bwd", riders=[red.scatter(["wgu0"]), red.exchange(["wo"])])
    red.scattered(["wgu0"], res_a, where)
    red.exchanged(["wo"], res_b)
    names, rider = red.broadcast([("wgu", 0)])
    (dqkv, db_qkv), (res_a, res_b) = rope_bwd(dq, dkc, dkp, dvc, dvp, cos, sin, name="rope_bwd",
                                              riders=[rider, red.scatter(["wo"])])
    red.broadcasted(names, res_a)
    red.scattered(["wo"], res_b, where)
    names, rider = red.broadcast([("wo", None)])
    red.grad["qkv"], (res,) = mm_tn(h0, dqkv, shard_major=True, tm=D_MODEL, tn=QKV_WIDTH // N_CHIPS, name="dw_qkv",
                                    riders=[rider])
    red.broadcasted(names, res)
    grad_x, dg_mpre0 = dh_norm_bwd_last(dqkv, w_qkv, dx1, x0, gain(norm_mix_pre, 0), name="dh_attn_norm_in")

    norm_grads = [jnp.concatenate(p, axis=0) for p in
                  ((dg_mpre0, dg_mpre1), (dg_mpost0, dg_mpost1), (dg_fpre0, dg_fpre1), (dg_fpost0, dg_fpost1))]
    red.grad["small"] = _pack_small(norm_grads, db_qkv, db_o, dsink[:, :, 0, 0], db_sp[:, :, 0], dln_g, dln_b, dw_sp,
                                    loss_part)
    def big_update(w, g, m, v, tag, after=None):
        return adamw(w, g.reshape(w.shape), m, v, name=f"adamw_{tag}", after=after)

    (res,) = comm_call([red.exchange(["qkv", "small"])], name="tail_1")
    red.exchanged(["qkv", "small"], res)
    state, token = comm_start([red.scatter(["qkv", "small"])], name="tail_2_start")
    upd_wgu = big_update(ffn_w_gate_up, red.dest["wgu"], m_ffn_w_gate_up, v_ffn_w_gate_up, "wgu", after=token)
    (res,) = comm_wait(state, upd_wgu[1], name="tail_2_wait")
    red.scattered(["qkv", "small"], res, where)
    names, rider = red.broadcast([("qkv", None)])
    state, token = comm_start([rider, allcast_rider(red.dest["slab"])], name="tail_3_start")
    upd_wd = big_update(ffn_w_down, red.dest["wd"], m_ffn_w_down, v_ffn_w_down, "wd", after=token)
    (res_a,), (slab_full,) = comm_wait(state, upd_wd[1], name="tail_3_wait")
    red.broadcasted(names, [res_a])
    g_qkv, g_wo, g_win, g_wout = (red.dest[n] for n in ("qkv", "wo", "win", "wout"))
    g_norms, g_bqkv, g_bo, g_sinks, g_bsp, g_lng, g_lnb, g_wsp, loss = _unpack_small(slab_full, chip)

    upd = {
        "attn_w_qkv": big_update(attn_w_qkv, g_qkv, m_attn_w_qkv, v_attn_w_qkv, "qkv"),
        "attn_w_o": big_update(attn_w_o, g_wo, m_attn_w_o, v_attn_w_o, "wo"),
        "sgu_w_in": big_update(sgu_w_in, g_win, m_sgu_w_in, v_sgu_w_in, "win"),
        "sgu_w_out": big_update(sgu_w_out, g_wout, m_sgu_w_out, v_sgu_w_out, "wout"),
        "ffn_w_gate_up": upd_wgu,
        "ffn_w_down": upd_wd,
    }
    small_names = ["norm_mix_pre", "norm_mix_post", "norm_ffn_pre", "norm_ffn_post", "attn_b_qkv", "attn_sinks", "attn_b_o",
                   "sgu_ln_g", "sgu_ln_b", "sgu_w_spatial", "sgu_b_spatial"]
    small_w = [norm_mix_pre, norm_mix_post, norm_ffn_pre, norm_ffn_post, attn_b_qkv, attn_sinks, attn_b_o, sgu_ln_g, sgu_ln_b,
               sgu_w_spatial, sgu_b_spatial]
    small_m = [m_norm_mix_pre, m_norm_mix_post, m_norm_ffn_pre, m_norm_ffn_post, m_attn_b_qkv, m_attn_sinks, m_attn_b_o,
               m_sgu_ln_g, m_sgu_ln_b, m_sgu_w_spatial, m_sgu_b_spatial]
    small_v = [v_norm_mix_pre, v_norm_mix_post, v_norm_ffn_pre, v_norm_ffn_post, v_attn_b_qkv, v_attn_sinks, v_attn_b_o,
               v_sgu_ln_g, v_sgu_ln_b, v_sgu_w_spatial, v_sgu_b_spatial]
    small_g = g_norms + [g_bqkv, g_sinks, g_bo, g_lng, g_lnb, g_wsp, g_bsp]

    def flat2(a):
        return a.reshape(-1, a.shape[-1])

    res = adamw_small([flat2(a) for a in small_w], [flat2(a) for a in small_g], [flat2(a) for a in small_m],
                      [flat2(a) for a in small_v], name="adamw_small")
    for i, nm in enumerate(small_names):
        upd[nm] = tuple(r[i].reshape(small_w[i].shape) for r in res)

    order = ["norm_mix_pre", "norm_mix_post", "norm_ffn_pre", "norm_ffn_post", "attn_w_qkv", "attn_b_qkv", "attn_sinks",
             "attn_w_o", "attn_b_o", "sgu_w_in", "sgu_ln_g", "sgu_ln_b", "sgu_w_spatial", "sgu_b_spatial", "sgu_w_out",
             "ffn_w_gate_up", "ffn_w_down"]
    outs = [loss, grad_x.reshape(1, s, D_MODEL)]
    for part in range(4):
        outs += [upd[nm][part] for nm in order]
    return tuple(outs)
```

```python
import types

import numpy as np
import jax
import jax.numpy as jnp
from jax import lax
from jax.experimental import pallas as pl
from jax.experimental.pallas import tpu as pltpu

F32 = jnp.float32
BF16 = jnp.bfloat16
I32 = jnp.int32

D_MODEL = 1024
HEAD_DIM = 64
N_Q_HEADS = 16
N_KV_HEADS = 4
GQA_GROUP = 4
WINDOW = 128
Q_WIDTH = 1024
KV_WIDTH = 256
QKV_WIDTH = 1536
ROPE_THETA = 10000.0
SGU_GROUPS = 8
SGU_CHUNK = 128
D_FF = 2816
FF_HALF = D_FF // 2
EPS = 1e-6
N_CHIPS = 4
LANES = 128

ADAM_LR = 0.001
ADAM_B1 = 0.9
ADAM_B2 = 0.999
ADAM_EPS = 1e-08
ADAM_WD = 0.01
ADAM_STEP = 10

VMEM_LIMIT = 52 * 1024 * 1024
MESH = pl.DeviceIdType.MESH
NEG = -1e30
NT_DIMS = (((1,), (1,)), ((), ()))
TN_DIMS = (((0,), (0,)), ((), ()))
NN_DIMS = (((1,), (0,)), ((), ()))
ANY = pl.BlockSpec(memory_space=pl.ANY)


def _row_tile(s, want):
    return want if s % want == 0 else s


PEER_KINDS = ("sibling", "chips", "sibling+chips", "everyone")


def _peer_kind(riders):
    kinds = {r.peers for r in riders}
    if not kinds:
        return None
    if "everyone" in kinds:
        return "everyone"
    return "sibling+chips" if len(kinds) == 2 else kinds.pop()


def _peer_barrier(kind):
    x, y, c = _place()
    chips = [(*_partner(x, y, k), c) for k in (1, 2, 3)]
    peers = {"sibling": [(x, y, 1 - c)], "chips": chips, "sibling+chips": [(x, y, 1 - c)] + chips,
             "everyone": [(x, y, 1 - c)] + chips + [(px, py, 1 - c) for px, py, _ in chips]}[kind]
    barrier = pltpu.get_barrier_semaphore()
    for dev in peers:
        pl.semaphore_signal(barrier, inc=1, device_id=dev, device_id_type=MESH)
    pl.semaphore_wait(barrier, len(peers))


def _call(body, *, name, grid=(), in_specs=(), out_specs=(), out_shape=(), scratch_shapes=(), operands=(), prefetch=(),
          aliases=None, riders=(), sem=None):
    n_pre, n_in, n_out, n_scr = len(prefetch), len(operands), len(out_shape), len(scratch_shapes)
    in_specs, out_specs, out_shape = list(in_specs), list(out_specs), list(out_shape)
    operands, scratch_shapes = list(operands), list(scratch_shapes)
    io_alias = {n_pre + i: o for i, o in (aliases or {}).items()}
    for r in riders:
        base_in, base_out = n_pre + len(operands), len(out_shape)
        operands += list(r.inputs)
        in_specs += [ANY] * len(r.inputs)
        for pos, i in enumerate(r.aliased):
            io_alias[base_in + i] = base_out + pos
            out_shape.append(jax.ShapeDtypeStruct(r.inputs[i].shape, r.inputs[i].dtype))
        out_shape += list(r.fresh)
        out_specs += [ANY] * (len(r.aliased) + len(r.fresh))
        scratch_shapes += [pltpu.SemaphoreType.DMA((r.nsem,)), pltpu.SemaphoreType.DMA((r.nsem,))]

    def wrapped(*refs):
        pre, p = refs[:n_pre], n_pre
        core_in, p = refs[p:p + n_in], p + n_in
        r_in = []
        for r in riders:
            r_in.append(refs[p:p + len(r.inputs)])
            p += len(r.inputs)
        core_out, p = refs[p:p + n_out], p + n_out
        r_out = []
        for r in riders:
            k = len(r.aliased) + len(r.fresh)
            r_out.append(refs[p:p + k])
            p += k
        core_scr, p = refs[p:p + n_scr], p + n_scr
        r_sem = [refs[p + 2 * i:p + 2 * i + 2] for i in range(len(riders))]

        def edge(at_last, fns):
            def run():
                if not at_last:
                    _peer_barrier(peer_kind)
                for i, r in enumerate(riders):
                    getattr(r, fns)(r_in[i], r_out[i], r_sem[i][0], r_sem[i][1])
            if not riders:
                return
            if not grid:
                run()
                return
            cond = None
            for d, n in enumerate(grid):
                c = pl.program_id(d) == (n - 1 if at_last else 0)
                cond = c if cond is None else jnp.logical_and(cond, c)
            pl.when(cond)(run)

        edge(False, "start")
        if body is not None:
            body(*pre, *core_in, *core_out, *core_scr)
        edge(True, "finish")

    if sem is None or riders:
        sem = ("arbitrary",) * len(grid)
    kwargs = dict(out_shape=out_shape, input_output_aliases=io_alias, name=name)
    peer_kind = _peer_kind(riders)
    collective = {} if peer_kind is None else {"collective_id": PEER_KINDS.index(peer_kind)}
    if grid:
        kwargs["compiler_params"] = pltpu.CompilerParams(dimension_semantics=sem, vmem_limit_bytes=VMEM_LIMIT, **collective)
    elif collective:
        kwargs["compiler_params"] = pltpu.CompilerParams(**collective)
    if n_pre:
        kwargs["grid_spec"] = pltpu.PrefetchScalarGridSpec(
            num_scalar_prefetch=n_pre, grid=grid, in_specs=in_specs, out_specs=out_specs, scratch_shapes=scratch_shapes)
    else:
        kwargs.update(grid=grid, in_specs=in_specs, out_specs=out_specs, scratch_shapes=scratch_shapes)
    res = pl.pallas_call(wrapped, **kwargs)(*prefetch, *operands)
    core, rest, rider_res = list(res[:n_out]), list(res[n_out:]), []
    for r in riders:
        k = len(r.aliased) + len(r.fresh)
        rider_res.append(rest[:k])
        rest = rest[k:]
    return core, rider_res


def _mm_call(*, grid, in_specs, out_spec, out_shape, dims, nk, kaxis, acc_shape, name, operands, riders=()):
    out_dtype = out_shape.dtype

    def body(a_ref, b_ref, o_ref, *scratch):
        p = lax.dot_general(a_ref[...].astype(BF16), b_ref[...].astype(BF16), dims, preferred_element_type=F32)
        if nk == 1:
            o_ref[...] = p.astype(out_dtype)
        else:
            acc = scratch[0]
            kk = pl.program_id(kaxis)

            @pl.when(kk == 0)
            def _():
                acc[...] = p

            @pl.when(kk > 0)
            def _():
                acc[...] += p

            @pl.when(kk == nk - 1)
            def _():
                o_ref[...] = acc[...].astype(out_dtype)

    sem = ["parallel"] * len(grid)
    if nk > 1:
        sem[kaxis] = "arbitrary"
    (out,), rider_res = _call(
        body, grid=grid, in_specs=in_specs, out_specs=[out_spec], out_shape=[out_shape],
        scratch_shapes=[pltpu.VMEM(acc_shape, F32)] if nk > 1 else [], operands=operands, name=name, riders=riders,
        sem=tuple(sem))
    return (out, rider_res) if riders else out


def mm_nt(a, w, *, out_dtype, name, tm=1024, riders=()):
    m, n = a.shape
    kout = w.shape[0]
    tm = _row_tile(m, tm)
    return _mm_call(grid=(m // tm,),
                    in_specs=[pl.BlockSpec((tm, n), lambda i: (i, 0)), pl.BlockSpec((kout, n), lambda i: (0, 0))],
                    out_spec=pl.BlockSpec((tm, kout), lambda i: (i, 0)),
                    out_shape=jax.ShapeDtypeStruct((m, kout), out_dtype), dims=NT_DIMS, nk=1, kaxis=0,
                    acc_shape=None, name=name, operands=(a, w), riders=riders)


def mm_tn(a, b, *, shard_major, name, tm, tn, tk=None, out_dtype=BF16, riders=()):
    s, m = a.shape
    tk = s if tk is None else _row_tile(s, tk)
    if b.ndim == 3:
        n = 2 * b.shape[2]
        b_spec = pl.BlockSpec((None, tk, tn), lambda j, i, kk: (j // 2, kk, j % 2))
    else:
        n = b.shape[1]
        b_spec = pl.BlockSpec((tk, tn), lambda j, i, kk: (kk, j))
    if shard_major:
        assert tn == n // N_CHIPS
        o_spec = pl.BlockSpec((None, tm, tn), lambda j, i, kk: (j, i, 0))
        o_shape = jax.ShapeDtypeStruct((N_CHIPS, m, tn), out_dtype)
    else:
        o_spec = pl.BlockSpec((tm, tn), lambda j, i, kk: (i, j))
        o_shape = jax.ShapeDtypeStruct((m, n), out_dtype)
    return _mm_call(grid=(n // tn, m // tm, s // tk),
                    in_specs=[pl.BlockSpec((tk, tm), lambda j, i, kk: (kk, i)), b_spec], out_spec=o_spec,
                    out_shape=o_shape, dims=TN_DIMS, nk=s // tk, kaxis=2, acc_shape=(tm, tn), name=name, operands=(a, b),
                    riders=riders)


def _rstd(x):
    return lax.rsqrt(jnp.mean(x * x, axis=-1, keepdims=True) + EPS)


def _rms_bwd(dy, x, g):
    r = _rstd(x)
    xhat = x * r
    gy = dy * g
    dx = r * (gy - xhat * jnp.mean(gy * xhat, axis=-1, keepdims=True))
    return dx, jnp.sum(dy * xhat, axis=0, keepdims=True)


def _accum(ref, val, first):
    @pl.when(first)
    def _():
        ref[...] = val

    @pl.when(jnp.logical_not(first))
    def _():
        ref[...] += val


def _row_spec(tm, width):
    return pl.BlockSpec((tm, width), lambda i: (i, 0))


def _vec_spec(width):
    return pl.BlockSpec((1, width), lambda i: (0, 0))


def _ret(core, rider_res, riders):
    core = core[0] if len(core) == 1 else core
    return (core, rider_res) if riders else core


def prenorm(x, g, *, name, tm=256, riders=()):
    s = x.shape[0]
    tm = _row_tile(s, tm)

    def body(x_ref, g_ref, h_ref):
        xv = x_ref[...]
        h_ref[...] = (xv * _rstd(xv) * g_ref[...]).astype(BF16)

    core, rr = _call(
        body, grid=(s // tm,), in_specs=[_row_spec(tm, D_MODEL), _vec_spec(D_MODEL)], out_specs=[_row_spec(tm, D_MODEL)],
        out_shape=[jax.ShapeDtypeStruct((s, D_MODEL), BF16)], operands=(x, g), sem=("parallel",), name=name, riders=riders)
    return _ret(core, rr, riders)


def proj_residual_norm(a, w, x, bias, g_post, g_next, *, name, tm=256, riders=()):
    s, k = a.shape
    tm = _row_tile(s, tm)

    def body(a_ref, w_ref, x_ref, b_ref, gp_ref, gn_ref, xo_ref, h_ref, m_ref):
        mv = jnp.dot(a_ref[...], w_ref[...], preferred_element_type=F32) + b_ref[...]
        m_ref[...] = mv.astype(BF16)
        xn = x_ref[...] + mv * _rstd(mv) * gp_ref[...]
        xo_ref[...] = xn
        h_ref[...] = (xn * _rstd(xn) * gn_ref[...]).astype(BF16)

    row, vec = _row_spec(tm, D_MODEL), _vec_spec(D_MODEL)
    core, rr = _call(
        body, grid=(s // tm,),
        in_specs=[_row_spec(tm, k), pl.BlockSpec((k, D_MODEL), lambda i: (0, 0)), row, vec, vec, vec], out_specs=[row, row, row],
        out_shape=[jax.ShapeDtypeStruct((s, D_MODEL), F32), jax.ShapeDtypeStruct((s, D_MODEL), BF16),
                   jax.ShapeDtypeStruct((s, D_MODEL), BF16)],
        operands=(a, w, x, bias, g_post, g_next), sem=("parallel",), name=name, riders=riders)
    return _ret(core, rr, riders)


def ffn_fwd_loss_rows(h, w_gu, w_d, x, g_post, target, *, name, tm=256, riders=()):
    s = x.shape[0]
    tm = _row_tile(s, tm)

    def body(h_ref, w0, w1, w2, w3, wd_ref, x_ref, g_ref, t_ref, d_ref, a_ref, dx_ref, df_ref, dg_ref, loss_ref):
        first = pl.program_id(0) == 0
        hv = h_ref[...]
        fv = None
        for half, (wg_ref, wu_ref) in enumerate(((w0, w2), (w1, w3))):
            cols = slice(half * FF_HALF, (half + 1) * FF_HALF)
            g = jnp.dot(hv, wg_ref[...], preferred_element_type=F32)
            u = jnp.dot(hv, wu_ref[...], preferred_element_type=F32)
            sig = _sigmoid(g)
            silu = g * sig
            d_ref[0, :, cols] = (u * (sig + silu * (1.0 - sig))).astype(BF16)
            d_ref[1, :, cols] = silu.astype(BF16)
            act = (silu * u).astype(BF16)
            a_ref[:, cols] = act
            p = jnp.dot(act, wd_ref[cols, :], preferred_element_type=F32)
            fv = p if fv is None else fv + p
        gain = g_ref[...]
        err = x_ref[...] + fv * _rstd(fv) * gain - t_ref[...]
        dx = err * (1.0 / D_MODEL)
        dx_ref[...] = dx
        df, dg = _rms_bwd(dx, fv, gain)
        df_ref[...] = df.astype(BF16)
        _accum(dg_ref, dg, first)
        part = jnp.sum(jnp.sum(err * err, axis=-1, keepdims=True), axis=0, keepdims=True) * (0.5 / D_MODEL)
        _accum(loss_ref, jnp.broadcast_to(part, (8, LANES)), first)

    def resident(shape, index):
        return pl.BlockSpec(shape, index, pipeline_mode=pl.Buffered(1))

    row, vec = _row_spec(tm, D_MODEL), _vec_spec(D_MODEL)
    shards = [resident((None, D_MODEL, FF_HALF), (lambda j: (lambda i: (j, 0, 0)))(j)) for j in range(N_CHIPS)]
    core, rr = _call(
        body, grid=(s // tm,),
        in_specs=[row] + shards + [resident((D_FF, D_MODEL), lambda i: (0, 0)), row, vec, row],
        out_specs=[pl.BlockSpec((2, tm, D_FF), lambda i: (0, i, 0)), _row_spec(tm, D_FF), row, row, vec,
                   pl.BlockSpec((8, LANES), lambda i: (0, 0))],
        out_shape=[jax.ShapeDtypeStruct((2, s, D_FF), BF16), jax.ShapeDtypeStruct((s, D_FF), BF16),
                   jax.ShapeDtypeStruct((s, D_MODEL), F32), jax.ShapeDtypeStruct((s, D_MODEL), BF16),
                   jax.ShapeDtypeStruct((1, D_MODEL), F32), jax.ShapeDtypeStruct((8, LANES), F32)],
        operands=(h, w_gu, w_gu, w_gu, w_gu, w_d, x, g_post, target), name=name, riders=riders)
    return _ret(core, rr, riders)


def dh_norm_bwd_pair(a, w, dres, x, g_pre, m, g_post, *, name, tm=512, sub=256, riders=()):
    _, kout, ns = w.shape
    planes = a.ndim == 3
    s = x.shape[0]
    tm = _row_tile(s, tm)
    sub = min(sub, tm)
    a_spec = pl.BlockSpec((2, tm, 2 * ns), lambda i: (0, i, 0)) if planes else pl.BlockSpec((tm, N_CHIPS * ns), lambda i: (i, 0))

    def body(a_ref, w0, w1, w2, w3, dres_ref, x_ref, gpre_ref, m_ref, gpost_ref, dx_ref, dm_ref, dgpre_ref, dgpost_ref, db_ref):
        first = pl.program_id(0) == 0
        sums = None
        for t in range(tm // sub):
            rows = slice(t * sub, (t + 1) * sub)
            dh = None
            for j, w_ref in enumerate((w0, w1, w2, w3)):
                a_j = a_ref[j // 2, rows, (j % 2) * ns:(j % 2 + 1) * ns] if planes else a_ref[rows, j * ns:(j + 1) * ns]
                p = lax.dot_general(a_j, w_ref[...], NT_DIMS, preferred_element_type=F32)
                dh = p if dh is None else dh + p
            d1, dgpre = _rms_bwd(dh, x_ref[rows, :], gpre_ref[...])
            dx = dres_ref[rows, :] + d1
            dx_ref[rows, :] = dx
            dm, dgpost = _rms_bwd(dx, m_ref[rows, :].astype(F32), gpost_ref[...])
            dm_ref[rows, :] = dm.astype(BF16)
            part = (dgpre, dgpost, jnp.sum(dm, axis=0, keepdims=True))
            sums = part if sums is None else tuple(u + v for u, v in zip(sums, part))
        _accum(dgpre_ref, sums[0], first)
        _accum(dgpost_ref, sums[1], first)
        _accum(db_ref, sums[2], first)

    def shard(j):
        return pl.BlockSpec((None, kout, ns), lambda i: (j, 0, 0))

    row, vec = _row_spec(tm, D_MODEL), _vec_spec(D_MODEL)
    vshape = jax.ShapeDtypeStruct((1, D_MODEL), F32)
    core, rr = _call(
        body, grid=(s // tm,), in_specs=[a_spec] + [shard(j) for j in range(N_CHIPS)] + [row, row, vec, row, vec],
        out_specs=[row, row, vec, vec, vec],
        out_shape=[jax.ShapeDtypeStruct((s, D_MODEL), F32), jax.ShapeDtypeStruct((s, D_MODEL), BF16), vshape, vshape, vshape],
        operands=(a, w, w, w, w, dres, x, g_pre, m, g_post), name=name, riders=riders)
    return _ret(core, rr, riders)


def ffn_bwd_rows(df, w_d, d_planes, w_gu, dres, x, g_pre, m, g_post, *, name, tm=256, riders=()):
    s = x.shape[0]
    tm = _row_tile(s, tm)

    def body(df_ref, wd_ref, d_ref, w0, w1, w2, w3, dres_ref, x_ref, gpre_ref, m_ref, gpost_ref,
             o_ref, dx_ref, dm_ref, dgpre_ref, dgpost_ref, db_ref):
        first = pl.program_id(0) == 0
        dfv = df_ref[...]
        dh = None
        for half, (wg_ref, wu_ref) in enumerate(((w0, w2), (w1, w3))):
            cols = slice(half * FF_HALF, (half + 1) * FF_HALF)
            da = lax.dot_general(dfv, wd_ref[cols, :], NT_DIMS, preferred_element_type=F32)
            dg = (da * d_ref[0, :, cols].astype(F32)).astype(BF16)
            du = (da * d_ref[1, :, cols].astype(F32)).astype(BF16)
            o_ref[0, :, cols] = dg
            o_ref[1, :, cols] = du
            p = lax.dot_general(dg, wg_ref[...], NT_DIMS, preferred_element_type=F32)
            p += lax.dot_general(du, wu_ref[...], NT_DIMS, preferred_element_type=F32)
            dh = p if dh is None else dh + p
        d1, dgpre = _rms_bwd(dh, x_ref[...], gpre_ref[...])
        dx = dres_ref[...] + d1
        dx_ref[...] = dx
        dm, dgpost = _rms_bwd(dx, m_ref[...].astype(F32), gpost_ref[...])
        dm_ref[...] = dm.astype(BF16)
        _accum(dgpre_ref, dgpre, first)
        _accum(dgpost_ref, dgpost, first)
        _accum(db_ref, jnp.sum(dm, axis=0, keepdims=True), first)

    def resident(shape, index):
        return pl.BlockSpec(shape, index, pipeline_mode=pl.Buffered(1))

    planes = pl.BlockSpec((2, tm, D_FF), lambda i: (0, i, 0))
    row, vec = _row_spec(tm, D_MODEL), _vec_spec(D_MODEL)
    vshape = jax.ShapeDtypeStruct((1, D_MODEL), F32)
    shards = [resident((None, D_MODEL, FF_HALF), (lambda j: (lambda i: (j, 0, 0)))(j)) for j in range(N_CHIPS)]
    core, rr = _call(
        body, grid=(s // tm,),
        in_specs=[row, resident((D_FF, D_MODEL), lambda i: (0, 0)), planes] + shards + [row, row, vec, row, vec],
        out_specs=[planes, row, row, vec, vec, vec],
        out_shape=[jax.ShapeDtypeStruct((2, s, D_FF), BF16), jax.ShapeDtypeStruct((s, D_MODEL), F32),
                   jax.ShapeDtypeStruct((s, D_MODEL), BF16), vshape, vshape, vshape],
        operands=(df, w_d, d_planes, w_gu, w_gu, w_gu, w_gu, dres, x, g_pre, m, g_post), name=name, riders=riders)
    return _ret(core, rr, riders)


def dh_norm_bwd_last(a, w, dres, x, g_pre, *, name, tm=512, sub=256):
    _, kout, ns = w.shape
    s = x.shape[0]
    tm = _row_tile(s, tm)
    sub = min(sub, tm)

    def body(a_ref, w0, w1, w2, w3, dres_ref, x_ref, g_ref, dx_ref, dg_ref):
        total = None
        for t in range(tm // sub):
            rows = slice(t * sub, (t + 1) * sub)
            dh = None
            for j, w_ref in enumerate((w0, w1, w2, w3)):
                p = lax.dot_general(a_ref[rows, j * ns:(j + 1) * ns], w_ref[...], NT_DIMS, preferred_element_type=F32)
                dh = p if dh is None else dh + p
            d1, dg = _rms_bwd(dh, x_ref[rows, :], g_ref[...])
            dx_ref[rows, :] = dres_ref[rows, :] + d1
            total = dg if total is None else total + dg
        _accum(dg_ref, total, pl.program_id(0) == 0)

    def shard(j):
        return pl.BlockSpec((None, kout, ns), lambda i: (j, 0, 0))

    row, vec = _row_spec(tm, D_MODEL), _vec_spec(D_MODEL)
    (dx, dg), _ = _call(
        body, grid=(s // tm,), in_specs=[_row_spec(tm, N_CHIPS * ns)] + [shard(j) for j in range(N_CHIPS)] + [row, row, vec],
        out_specs=[row, vec], out_shape=[jax.ShapeDtypeStruct((s, D_MODEL), F32), jax.ShapeDtypeStruct((1, D_MODEL), F32)],
        operands=(a, w, w, w, w, dres, x, g_pre), name=name)
    return dx, dg


def _rope_tables(s):
    half = HEAD_DIM // 2
    inv_freq = np.float32(ROPE_THETA) ** (-(np.arange(half, dtype=np.float32) * np.float32(2.0)) / np.float32(HEAD_DIM))
    ang = np.arange(s, dtype=np.float32)[:, None] * inv_freq[None, :]
    cos, sin = np.cos(ang).astype(np.float32), np.sin(ang).astype(np.float32)
    return jnp.asarray(np.tile(cos, (1, 4))), jnp.asarray(np.concatenate([-sin, sin, -sin, sin], axis=1))


def _swap_halves(x):
    lane = lax.broadcasted_iota(I32, x.shape, 1)
    return jnp.where((lane & (HEAD_DIM - 1)) < HEAD_DIM // 2, pltpu.roll(x, LANES - 32, 1), pltpu.roll(x, 32, 1))


N_ROPE_BLOCKS = (Q_WIDTH + KV_WIDTH) // LANES


def qkv_proj(h, w, bias, cos, sin, *, name, tm=512, riders=()):
    s, k = h.shape
    ns = w.shape[2]
    tm = _row_tile(s, tm)

    def body(h_ref, w_ref, b_ref, c_ref, s_ref, o_ref):
        j = pl.program_id(0)
        sub = min(256, tm)
        for t in range(tm // sub):
            rows = slice(t * sub, (t + 1) * sub)
            p = jnp.dot(h_ref[rows, :], w_ref[...], preferred_element_type=F32) + b_ref[...]
            cosv, sinv = c_ref[rows, :], s_ref[rows, :]
            for blk in range(ns // LANES):
                xb = p[:, blk * LANES:(blk + 1) * LANES]
                roped = xb * cosv + _swap_halves(xb) * sinv
                is_qk = j * (ns // LANES) + blk < N_ROPE_BLOCKS
                o_ref[rows, blk * LANES:(blk + 1) * LANES] = jnp.where(is_qk, roped, xb).astype(BF16)

    core, rr = _call(
        body, grid=(N_CHIPS, s // tm),
        in_specs=[pl.BlockSpec((tm, k), lambda j, i: (i, 0)), pl.BlockSpec((None, k, ns), lambda j, i: (j, 0, 0)),
                  pl.BlockSpec((1, ns), lambda j, i: (0, j)), pl.BlockSpec((tm, LANES), lambda j, i: (i, 0)),
                  pl.BlockSpec((tm, LANES), lambda j, i: (i, 0))],
        out_specs=[pl.BlockSpec((tm, ns), lambda j, i: (i, j))], out_shape=[jax.ShapeDtypeStruct((s, N_CHIPS * ns), BF16)],
        operands=(h, w, bias, cos, sin), sem=("parallel", "parallel"), name=name, riders=riders)
    return _ret(core, rr, riders)


def rope_bwd(dq, dkc, dkp, dvc, dvp, cos, sin, *, name, riders=()):
    s = dq.shape[0]
    tm = 2 * WINDOW if s % (2 * WINDOW) == 0 else WINDOW
    nb = s // tm

    def body(dq_ref, dkc_ref, dkp_ref, dkp_next_ref, dvc_ref, dvp_ref, dvp_next_ref, c_ref, s_ref, o_ref, db_ref):
        i = pl.program_id(0)
        has_next = (i < nb - 1).astype(F32)
        cosv, sinv = c_ref[...], s_ref[...]

        def shifted(ref, next_ref, cols):
            last = has_next * next_ref[:WINDOW, cols].astype(F32)
            return last if tm == WINDOW else jnp.concatenate([ref[WINDOW:, cols].astype(F32), last], axis=0)

        parts = []
        for blk in range(QKV_WIDTH // LANES):
            if blk < Q_WIDTH // LANES:
                g = dq_ref[:, blk * LANES:(blk + 1) * LANES].astype(F32)
            else:
                own, prv, nxt = (dkc_ref, dkp_ref, dkp_next_ref) if blk < N_ROPE_BLOCKS else (dvc_ref, dvp_ref, dvp_next_ref)
                cols = slice((blk % 2) * LANES, (blk % 2 + 1) * LANES)
                g = own[:, cols].astype(F32) + shifted(prv, nxt, cols)
            if blk < N_ROPE_BLOCKS:
                g = g * cosv + _swap_halves(g * sinv)
            o_ref[:, blk * LANES:(blk + 1) * LANES] = g.astype(BF16)
            parts.append(jnp.sum(g, axis=0, keepdims=True))
        sums = jnp.concatenate(parts, axis=1)
        _accum(db_ref, sums, i == 0)

    own_spec = _row_spec(tm, KV_WIDTH)
    next_spec = pl.BlockSpec((tm, KV_WIDTH), lambda i: (jnp.minimum(i + 1, nb - 1), 0))
    core, rr = _call(
        body, grid=(nb,),
        in_specs=[_row_spec(tm, Q_WIDTH), own_spec, own_spec, next_spec, own_spec, own_spec, next_spec,
                  _row_spec(tm, LANES), _row_spec(tm, LANES)],
        out_specs=[_row_spec(tm, QKV_WIDTH), _vec_spec(QKV_WIDTH)],
        out_shape=[jax.ShapeDtypeStruct((s, QKV_WIDTH), BF16), jax.ShapeDtypeStruct((1, QKV_WIDTH), F32)],
        operands=(dq, dkc, dkp, dkp, dvc, dvp, dvp, cos, sin), name=name, riders=riders)
    return _ret(core, rr, riders)


ROWS = GQA_GROUP * WINDOW


def _prev_slots():
    kpos = lax.broadcasted_iota(I32, (WINDOW, ROWS), 0)
    qpos = lax.broadcasted_iota(I32, (WINDOW, ROWS), 1) & (WINDOW - 1)
    return kpos > qpos


def _head_cols(ref, head):
    return ref[:, head * HEAD_DIM:(head + 1) * HEAD_DIM]


def _stack_heads(ref, h):
    return jnp.concatenate([_head_cols(ref, GQA_GROUP * h + g) for g in range(GQA_GROUP)], axis=0)


def _band(prev_ref, cur_ref, h):
    return jnp.concatenate([_head_cols(prev_ref, h), _head_cols(cur_ref, h)], axis=0)


def _pick(prev, band):
    return jnp.where(prev, band[:WINDOW], band[WINDOW:])


def _spread(prev, x):
    return jnp.concatenate([jnp.where(prev, x, 0.0), jnp.where(prev, 0.0, x)], axis=0).astype(BF16)


def _attn_probs(s_band, sink, prev, has_prev):
    scale = HEAD_DIM ** -0.5
    s = jnp.where(prev, jnp.where(has_prev, s_band[:WINDOW], NEG), s_band[WINDOW:]) * scale
    m = jnp.maximum(jnp.max(s, axis=0, keepdims=True), sink)
    e, es = jnp.exp(s - m), jnp.exp(sink - m)
    inv = 1.0 / (jnp.sum(e, axis=0, keepdims=True) + es)
    return e * inv, es * inv


def _attn_specs(nb):
    kcol, vcol = Q_WIDTH // KV_WIDTH, Q_WIDTH // KV_WIDTH + 1
    q_spec = pl.BlockSpec((WINDOW, Q_WIDTH), lambda n: (n, 0))
    return [q_spec,
            pl.BlockSpec((WINDOW, KV_WIDTH), lambda n: (n, kcol)),
            pl.BlockSpec((WINDOW, KV_WIDTH), lambda n: (jnp.maximum(n - 1, 0), kcol)),
            pl.BlockSpec((WINDOW, KV_WIDTH), lambda n: (n, vcol)),
            pl.BlockSpec((WINDOW, KV_WIDTH), lambda n: (jnp.maximum(n - 1, 0), vcol)),
            pl.BlockSpec((N_KV_HEADS, 8, ROWS), lambda n: (0, 0, 0))]


def attn_fwd(qkv, sink_rows, *, name, riders=()):
    s = qkv.shape[0]

    def body(q_ref, kc_ref, kp_ref, vc_ref, vp_ref, sink_ref, o_ref):
        prev = _prev_slots()
        has_prev = pl.program_id(0) > 0
        heads = range(N_KV_HEADS)
        s_bands = [lax.dot_general(_band(kp_ref, kc_ref, h), _stack_heads(q_ref, h), NT_DIMS, preferred_element_type=F32)
                   for h in heads]
        p_bands = [_spread(prev, _attn_probs(s_bands[h], sink_ref[h, 0:1, :], prev, has_prev)[0]) for h in heads]
        outs = [lax.dot_general(_band(vp_ref, vc_ref, h), p_bands[h], TN_DIMS, preferred_element_type=F32).T for h in heads]
        for h in heads:
            for g in range(GQA_GROUP):
                head = GQA_GROUP * h + g
                o_ref[:, head * HEAD_DIM:(head + 1) * HEAD_DIM] = outs[h][g * WINDOW:(g + 1) * WINDOW].astype(BF16)

    core, rr = _call(
        body, grid=(s // WINDOW,), in_specs=_attn_specs(s // WINDOW), out_specs=[pl.BlockSpec((WINDOW, Q_WIDTH), lambda n: (n, 0))],
        out_shape=[jax.ShapeDtypeStruct((s, Q_WIDTH), BF16)], operands=(qkv, qkv, qkv, qkv, qkv, sink_rows), sem=("parallel",),
        name=name, riders=riders)
    return _ret(core, rr, riders)


def attn_bwd(qkv, sink_rows, do, *, name, riders=()):
    s = qkv.shape[0]

    def body(q_ref, kc_ref, kp_ref, vc_ref, vp_ref, sink_ref, do_ref, dq_ref, dkc_ref, dkp_ref, dvc_ref, dvp_ref, dsink_ref):
        n = pl.program_id(0)
        prev = _prev_slots()
        scale = HEAD_DIM ** -0.5
        heads = range(N_KV_HEADS)
        qs, dos = [_stack_heads(q_ref, h) for h in heads], [_stack_heads(do_ref, h) for h in heads]
        kbands, vbands = [_band(kp_ref, kc_ref, h) for h in heads], [_band(vp_ref, vc_ref, h) for h in heads]
        s_bands = [lax.dot_general(kbands[h], qs[h], NT_DIMS, preferred_element_type=F32) for h in heads]
        dp_bands = [lax.dot_general(vbands[h], dos[h], NT_DIMS, preferred_element_type=F32) for h in heads]
        ds_bands, p_bands, parts = [], [], []
        for h in heads:
            p, ps = _attn_probs(s_bands[h], sink_ref[h, 0:1, :], prev, n > 0)
            dp = _pick(prev, dp_bands[h])
            delta = jnp.sum(p * dp, axis=0, keepdims=True)
            ds_bands.append(_spread(prev, p * (dp - delta) * scale))
            p_bands.append(_spread(prev, p))
            dsink = -(ps * delta)
            for g in range(GQA_GROUP):
                parts.append(jnp.broadcast_to(jnp.sum(dsink[:, g * WINDOW:(g + 1) * WINDOW], axis=1, keepdims=True), (8, LANES)))
        for h in heads:
            dk = jnp.dot(ds_bands[h], qs[h], preferred_element_type=F32).astype(BF16)
            dv = jnp.dot(p_bands[h], dos[h], preferred_element_type=F32).astype(BF16)
            dq = lax.dot_general(kbands[h], ds_bands[h], TN_DIMS, preferred_element_type=F32).T
            cols = slice(h * HEAD_DIM, (h + 1) * HEAD_DIM)
            dkp_ref[:, cols], dkc_ref[:, cols] = dk[:WINDOW], dk[WINDOW:]
            dvp_ref[:, cols], dvc_ref[:, cols] = dv[:WINDOW], dv[WINDOW:]
            for g in range(GQA_GROUP):
                head = GQA_GROUP * h + g
                dq_ref[:, head * HEAD_DIM:(head + 1) * HEAD_DIM] = dq[g * WINDOW:(g + 1) * WINDOW].astype(BF16)

        @pl.when(n == 0)
        def _():
            for i, part in enumerate(parts):
                dsink_ref[i // GQA_GROUP, i % GQA_GROUP] = part

        @pl.when(n > 0)
        def _():
            for i, part in enumerate(parts):
                dsink_ref[i // GQA_GROUP, i % GQA_GROUP] += part

    rows_q = pl.BlockSpec((WINDOW, Q_WIDTH), lambda n: (n, 0))
    rows_kv = pl.BlockSpec((WINDOW, KV_WIDTH), lambda n: (n, 0))
    kv_shape = jax.ShapeDtypeStruct((s, KV_WIDTH), BF16)
    core, rr = _call(
        body, grid=(s // WINDOW,), in_specs=_attn_specs(s // WINDOW) + [rows_q],
        out_specs=[rows_q, rows_kv, rows_kv, rows_kv, rows_kv,
                   pl.BlockSpec((N_KV_HEADS, GQA_GROUP, 8, LANES), lambda n: (0, 0, 0, 0))],
        out_shape=[jax.ShapeDtypeStruct((s, Q_WIDTH), BF16), kv_shape, kv_shape, kv_shape, kv_shape,
                   jax.ShapeDtypeStruct((N_KV_HEADS, GQA_GROUP, 8, LANES), F32)],
        operands=(qkv, qkv, qkv, qkv, qkv, sink_rows, do), sem=("arbitrary",), name=name, riders=riders)
    return _ret(core, rr, riders)


GELU_C = 0.7978845608028654
GELU_A = 0.044715


def _gelu(x):
    return 0.5 * x * (1.0 + jnp.tanh(x * (GELU_C + (GELU_C * GELU_A) * (x * x))))


def _gelu_and_grad(x):
    x2 = x * x
    t = jnp.tanh(x * (GELU_C + (GELU_C * GELU_A) * x2))
    half_x, one_t = 0.5 * x, 1.0 + t
    return half_x * one_t, 0.5 * one_t + half_x * (1.0 - t * t) * (GELU_C + (3.0 * GELU_C * GELU_A) * x2)


def _tril_bf16(w):
    row = lax.broadcasted_iota(I32, (SGU_CHUNK, SGU_CHUNK), 0)
    col = lax.broadcasted_iota(I32, (SGU_CHUNK, SGU_CHUNK), 1)
    return jnp.where(row >= col, w, 0.0).astype(BF16)


def _sgu_norm(vg, g, b):
    mu = jnp.mean(vg, axis=-1, keepdims=True)
    cen = vg - mu
    rstd = lax.rsqrt(jnp.mean(cen * cen, axis=-1, keepdims=True) + EPS)
    xhat = cen * rstd
    return xhat, rstd, xhat * g + b


def sgu_in_fwd(h, w_in, ln_g, ln_b, w_sp, b_sp, *, name, tm=256, riders=()):
    s, k = h.shape
    ns = w_in.shape[2]
    tm = _row_tile(s, tm)

    def body(h_ref, w0, w1, w2, w3, g_ref, b_ref, w_ref, bs_ref, z_ref, y_ref):
        hv = h_ref[...]
        zs = [jnp.dot(hv, w_ref_j[...], preferred_element_type=F32) for w_ref_j in (w0, w1, w2, w3)]
        for j, zj in enumerate(zs):
            z_ref[:, j * ns:(j + 1) * ns] = zj.astype(BF16)
        u = _gelu(jnp.concatenate(zs[:2], axis=1))
        _, _, vn = _sgu_norm(_gelu(jnp.concatenate(zs[2:], axis=1)), g_ref[...], b_ref[...])
        vn = vn.astype(BF16)
        for grp in range(SGU_GROUPS):
            w = _tril_bf16(w_ref[grp])
            cols = slice(grp * LANES, (grp + 1) * LANES)
            for ch in range(tm // SGU_CHUNK):
                rows = slice(ch * SGU_CHUNK, (ch + 1) * SGU_CHUNK)
                mixed = jnp.dot(w, vn[rows, cols], preferred_element_type=F32) + bs_ref[grp]
                y_ref[rows, cols] = (u[rows, cols] * mixed).astype(BF16)

    def shard(j):
        return pl.BlockSpec((None, k, ns), lambda i: (j, 0, 0))

    full3 = pl.BlockSpec((SGU_GROUPS, SGU_CHUNK, SGU_CHUNK), lambda i: (0, 0, 0))
    core, rr = _call(
        body, grid=(s // tm,),
        in_specs=[_row_spec(tm, k)] + [shard(j) for j in range(N_CHIPS)] + [_vec_spec(D_MODEL), _vec_spec(D_MODEL), full3, full3],
        out_specs=[_row_spec(tm, 2 * D_MODEL), _row_spec(tm, D_MODEL)],
        out_shape=[jax.ShapeDtypeStruct((s, 2 * D_MODEL), BF16), jax.ShapeDtypeStruct((s, D_MODEL), BF16)],
        operands=(h, w_in, w_in, w_in, w_in, ln_g, ln_b, w_sp, b_sp), sem=("parallel",), name=name, riders=riders)
    return _ret(core, rr, riders)


def sgu_bwd(z, dy, ln_g, ln_b, w_sp, b_sp, *, name, tm=256, riders=()):
    s = z.shape[0]
    tm = _row_tile(s, tm)

    def body(z_ref, dy_ref, g_ref, b_ref, w_ref, bs_ref, dz_ref, dw_ref, dbs_ref, dg_ref, db_ref, dvn_buf):
        first = pl.program_id(0) == 0
        u, u_grad = _gelu_and_grad(z_ref[:, :D_MODEL].astype(F32))
        vg, v_grad = _gelu_and_grad(z_ref[:, D_MODEL:].astype(F32))
        xhat, rstd, vn = _sgu_norm(vg, g_ref[...], b_ref[...])
        vn = vn.astype(BF16)
        dyv = dy_ref[...]
        dmixed = dyv * u
        dz_gate = dyv * u_grad
        row = lax.broadcasted_iota(I32, (SGU_CHUNK, SGU_CHUNK), 0)
        col = lax.broadcasted_iota(I32, (SGU_CHUNK, SGU_CHUNK), 1)
        dws, dbss = [], []
        for grp in range(SGU_GROUPS):
            w = _tril_bf16(w_ref[grp])
            cols = slice(grp * LANES, (grp + 1) * LANES)
            dw = jnp.zeros((SGU_CHUNK, SGU_CHUNK), F32)
            dbs = jnp.zeros((SGU_CHUNK, 1), F32)
            for ch in range(tm // SGU_CHUNK):
                rows = slice(ch * SGU_CHUNK, (ch + 1) * SGU_CHUNK)
                vblk = vn[rows, cols]
                mixed = jnp.dot(w, vblk, preferred_element_type=F32) + bs_ref[grp]
                dz_ref[rows, cols] = (dz_gate[rows, cols] * mixed).astype(BF16)
                dm = dmixed[rows, cols]
                dmb = dm.astype(BF16)
                dvn_buf[rows, cols] = lax.dot_general(w, dmb, TN_DIMS, preferred_element_type=F32)
                dw += lax.dot_general(dmb, vblk, NT_DIMS, preferred_element_type=F32)
                dbs += jnp.sum(dm, axis=-1, keepdims=True)
            dws.append(jnp.where(row >= col, dw, 0.0))
            dbss.append(jnp.broadcast_to(dbs, (SGU_CHUNK, SGU_CHUNK)))

        dvn = dvn_buf[...]
        dxhat = dvn * g_ref[...]
        dvg = rstd * (dxhat - jnp.mean(dxhat, axis=-1, keepdims=True) - xhat * jnp.mean(dxhat * xhat, axis=-1, keepdims=True))
        dz_ref[:, D_MODEL:] = (dvg * v_grad).astype(BF16)
        dlng, dlnb = jnp.sum(dvn * xhat, axis=0, keepdims=True), jnp.sum(dvn, axis=0, keepdims=True)

        @pl.when(first)
        def _():
            for grp in range(SGU_GROUPS):
                dw_ref[grp] = dws[grp]
                dbs_ref[grp] = dbss[grp]
            dg_ref[...] = dlng
            db_ref[...] = dlnb

        @pl.when(jnp.logical_not(first))
        def _():
            for grp in range(SGU_GROUPS):
                dw_ref[grp] += dws[grp]
                dbs_ref[grp] += dbss[grp]
            dg_ref[...] += dlng
            db_ref[...] += dlnb

    full3 = pl.BlockSpec((SGU_GROUPS, SGU_CHUNK, SGU_CHUNK), lambda i: (0, 0, 0))
    s3 = jax.ShapeDtypeStruct((SGU_GROUPS, SGU_CHUNK, SGU_CHUNK), F32)
    vshape = jax.ShapeDtypeStruct((1, D_MODEL), F32)
    core, rr = _call(
        body, grid=(s // tm,),
        in_specs=[_row_spec(tm, 2 * D_MODEL), _row_spec(tm, D_MODEL), _vec_spec(D_MODEL), _vec_spec(D_MODEL), full3, full3],
        out_specs=[_row_spec(tm, 2 * D_MODEL), full3, full3, _vec_spec(D_MODEL), _vec_spec(D_MODEL)],
        out_shape=[jax.ShapeDtypeStruct((s, 2 * D_MODEL), BF16), s3, s3, vshape, vshape],
        scratch_shapes=[pltpu.VMEM((tm, D_MODEL), F32)], operands=(z, dy, ln_g, ln_b, w_sp, b_sp), name=name, riders=riders)
    return _ret(core, rr, riders)


def _sigmoid(x):
    return 1.0 / (1.0 + jnp.exp(-x))


def ffn_up(h, w_gu, *, name, tm=512, riders=()):
    s = h.shape[0]
    tm = _row_tile(s, tm)

    def body(h_ref, wg_ref, wu_ref, d_ref, a_ref):
        hv = h_ref[...]
        sub = min(256, tm)
        for t in range(tm // sub):
            rows = slice(t * sub, (t + 1) * sub)
            g = jnp.dot(hv[rows], wg_ref[...], preferred_element_type=F32)
            u = jnp.dot(hv[rows], wu_ref[...], preferred_element_type=F32)
            sig = _sigmoid(g)
            silu = g * sig
            d_ref[0, rows, :] = (u * (sig + silu * (1.0 - sig))).astype(BF16)
            d_ref[1, rows, :] = silu.astype(BF16)
            a_ref[rows, :] = (silu * u).astype(BF16)

    core, rr = _call(
        body, grid=(2, s // tm),
        in_specs=[pl.BlockSpec((tm, D_MODEL), lambda j, i: (i, 0)),
                  pl.BlockSpec((None, D_MODEL, FF_HALF), lambda j, i: (j, 0, 0)),
                  pl.BlockSpec((None, D_MODEL, FF_HALF), lambda j, i: (j + 2, 0, 0))],
        out_specs=[pl.BlockSpec((2, tm, FF_HALF), lambda j, i: (0, i, j)), pl.BlockSpec((tm, FF_HALF), lambda j, i: (i, j))],
        out_shape=[jax.ShapeDtypeStruct((2, s, D_FF), BF16), jax.ShapeDtypeStruct((s, D_FF), BF16)],
        operands=(h, w_gu, w_gu), sem=("parallel", "parallel"), name=name, riders=riders)
    return _ret(core, rr, riders)


def _weight_tile(rows):
    for tr in (512, 352, 256, 128):
        if rows % tr == 0:
            return tr
    return rows


def place_shard(w, layer, chip_arr, dtype, *, name, riders=()):
    _, r, c = w.shape
    tr = _weight_tile(r)

    def body(chip_ref, w_ref, o_ref):
        o_ref[...] = w_ref[...].astype(dtype)

    core, rr = _call(
        body, grid=(r // tr,), prefetch=(chip_arr,),
        in_specs=[pl.BlockSpec((None, tr, c), lambda i, chip: (layer, i, 0))],
        out_specs=[pl.BlockSpec((None, tr, c), lambda i, chip: (chip[0], i, 0))],
        out_shape=[jax.ShapeDtypeStruct((N_CHIPS, r, c), dtype)], operands=(w,), sem=("parallel",), name=name, riders=riders)
    return _ret(core, rr, riders)


def _adamw_math(w, g, m, v):
    m = ADAM_B1 * m + (1.0 - ADAM_B1) * g
    v = ADAM_B2 * v + (1.0 - ADAM_B2) * (g * g)
    m_hat = m / (1.0 - ADAM_B1 ** ADAM_STEP)
    v_hat = v / (1.0 - ADAM_B2 ** ADAM_STEP)
    delta = -ADAM_LR * (m_hat / (jnp.sqrt(v_hat) + ADAM_EPS) + ADAM_WD * w)
    return delta, m, v


def adamw(w, g, m, v, *, name, after=None):
    nl, r, c = w.shape
    tr = _weight_tile(r)

    def body(w_ref, g_ref, m_ref, v_ref, *rest):
        go_ref, d_ref, mo_ref, vo_ref = rest[-4:]
        gv = g_ref[...]
        go_ref[...] = gv
        d_ref[...], mo_ref[...], vo_ref[...] = _adamw_math(w_ref[...], gv, m_ref[...], v_ref[...])

    spec = pl.BlockSpec((None, tr, c), lambda l, i: (l, i, 0))
    shape = jax.ShapeDtypeStruct(w.shape, F32)
    extra = [] if after is None else [after]
    outs, _ = _call(body, grid=(nl, r // tr), in_specs=[spec] * 4 + [ANY] * len(extra), out_specs=[spec] * 4,
                    out_shape=[shape] * 4, operands=(w, g, m, v, *extra), sem=("parallel", "parallel"), name=name)
    return outs


def adamw_small(ws, gs, ms, vs, *, name):
    n = len(ws)

    def body(*refs):
        ins, outs = refs[:4 * n], refs[4 * n:]
        for t in range(n):
            gv = ins[n + t][...]
            outs[t][...] = gv
            outs[n + t][...], outs[2 * n + t][...], outs[3 * n + t][...] = _adamw_math(
                ins[t][...], gv, ins[2 * n + t][...], ins[3 * n + t][...])

    shapes = [jax.ShapeDtypeStruct(w.shape, F32) for w in ws]
    res = pl.pallas_call(body, out_shape=shapes * 4, name=name)(*ws, *gs, *ms, *vs)
    return res[:n], res[n:2 * n], res[2 * n:3 * n], res[3 * n:]


def pair_add(g, r1, c_arr, *, name):
    _, rows, cdim = g.shape
    h = rows // 2

    def body(c_ref, g_ref, r_ref, o_ref):
        o_ref[...] = (g_ref[...].astype(F32) + r_ref[...].astype(F32)).astype(o_ref.dtype)

    (out,), _ = _call(
        body, grid=(N_CHIPS,), prefetch=(c_arr,),
        in_specs=[pl.BlockSpec((None, h, cdim), lambda s, c: (s, c[0], 0)), pl.BlockSpec((None, h, cdim), lambda s, c: (s, 0, 0))],
        out_specs=[pl.BlockSpec((None, h, cdim), lambda s, c: (s, 0, 0))],
        out_shape=[jax.ShapeDtypeStruct((N_CHIPS, h, cdim), g.dtype)], operands=(g, r1), sem=("parallel",), name=name)
    return out


def final_add(g, r1, r2, jc_arr, *, dest_shape, lead, prev, name):
    _, rows, cdim = g.shape
    h = rows // 2

    def body(jc_ref, g_ref, r1_ref, r2_ref, *rest):
        o_ref = rest[-1]
        acc = g_ref[...].astype(F32) + r1_ref[...].astype(F32)
        for k in range(3):
            acc = acc + r2_ref[k].astype(F32)
        o_ref[...] = acc

    if lead is None:
        o_spec = pl.BlockSpec((h, cdim), lambda i, jc: (jc[1], 0))
    elif lead == "chip":
        o_spec = pl.BlockSpec((None, h, cdim), lambda i, jc: (jc[0], jc[1], 0))
    else:
        o_spec = pl.BlockSpec((None, h, cdim), lambda i, jc: (lead, jc[1], 0))
    in_specs = [pl.BlockSpec((None, h, cdim), lambda i, jc: (jc[0], jc[1], 0)),
                pl.BlockSpec((None, h, cdim), lambda i, jc: (jc[0], 0, 0)),
                pl.BlockSpec((3, h, cdim), lambda i, jc: (0, 0, 0))]
    operands = [g, r1, r2]
    aliases = None
    if prev is not None:
        in_specs.append(ANY)
        operands.append(prev)
        aliases = {3: 0}
    (out,), _ = _call(body, grid=(1,), prefetch=(jc_arr,), in_specs=in_specs, out_specs=[o_spec],
                      out_shape=[jax.ShapeDtypeStruct(dest_shape, F32)], operands=operands, aliases=aliases, name=name)
    return out


def _place():
    return lax.axis_index("x"), lax.axis_index("y"), lax.axis_index("c")


def _partner(x, y, k):
    return (1 - x if k >> 1 else x), (1 - y if k & 1 else y)


WHOLE = (0, 1, 1)


def _half(rows, sel, dtype, piece=WHOLE):
    lo, hi, n = piece
    align = 16 if dtype == BF16 else 8
    step = rows // 2 // n
    assert rows // 2 == step * n and step % align == 0
    return pl.ds(pl.multiple_of(sel * (rows // 2) + lo * step, align), (hi - lo) * step)


def _rider(peers, inputs, aliased, fresh, nsem, copies, arrivals):
    def start(ins, outs, send, recv):
        for cp in copies(ins, outs, send, recv):
            cp.start()

    def finish(ins, outs, send, recv):
        for cp in arrivals(ins, outs, send, recv):
            cp.wait_recv()
        for cp in copies(ins, outs, send, recv):
            cp.wait_send()

    return types.SimpleNamespace(peers=peers, inputs=list(inputs), aliased=list(aliased), fresh=list(fresh), nsem=nsem,
                                 start=start, finish=finish)


def _remote(src, dst, send, recv, idx, dev):
    return pltpu.make_async_remote_copy(src_ref=src, dst_ref=dst, send_sem=send.at[idx], recv_sem=recv.at[idx],
                                        device_id=dev, device_id_type=MESH)


def gather_ici_rider(fulls, pieces=None):
    nt = len(fulls)
    pieces = pieces or [WHOLE] * nt

    def region(outs, t, slot, sel):
        return outs[t].at[slot, _half(fulls[t].shape[1], sel, fulls[t].dtype, pieces[t])]

    def copies(ins, outs, send, recv):
        x, y, c = _place()
        res = []
        for t in range(nt):
            for k in (1, 2, 3):
                px, py = _partner(x, y, k)
                mine = region(outs, t, 2 * x + y, c)
                res.append(_remote(mine, mine, send, recv, 3 * t + k - 1, (px, py, c)))
        return res

    def arrivals(ins, outs, send, recv):
        x, y, c = _place()
        res = []
        for t in range(nt):
            for k in (1, 2, 3):
                px, py = _partner(x, y, k)
                theirs = region(outs, t, 2 * px + py, c)
                res.append(_remote(theirs, theirs, send, recv, 3 * t + k - 1, (x, y, c)))
        return res

    return _rider("chips", fulls, range(nt), [], 3 * nt, copies, arrivals)


def gather_d2d_rider(fulls, pieces=None):
    nt = len(fulls)
    pieces = pieces or [WHOLE] * nt

    def region(outs, t, slot, sel):
        return outs[t].at[slot, _half(fulls[t].shape[1], sel, fulls[t].dtype, pieces[t])]

    def both(outs, send, recv, mine):
        x, y, c = _place()
        res = []
        for t in range(nt):
            for k in (1, 2, 3):
                px, py = _partner(x, y, k)
                part = region(outs, t, 2 * px + py, c if mine else 1 - c)
                res.append(_remote(part, part, send, recv, 3 * t + k - 1, (x, y, 1 - c)))
        return res

    return _rider("sibling", fulls, range(nt), [], 3 * nt, lambda i, o, s, r: both(o, s, r, True),
                  lambda i, o, s, r: both(o, s, r, False))


def exchange_rider(grads):
    nt = len(grads)

    def both(ins, outs, send, recv):
        x, y, c = _place()
        return [_remote(ins[t].at[:, _half(grads[t].shape[1], 1 - c, grads[t].dtype)], outs[t], send, recv, t, (x, y, 1 - c))
                for t in range(nt)]

    fresh = [jax.ShapeDtypeStruct((N_CHIPS, g.shape[1] // 2, g.shape[2]), g.dtype) for g in grads]
    return _rider("sibling", grads, [], fresh, nt, both, both)


def scatter_rider(parts):
    nt = len(parts)

    def both(ins, outs, send, recv):
        x, y, c = _place()
        res = []
        for t in range(nt):
            for k in (1, 2, 3):
                px, py = _partner(x, y, k)
                res.append(_remote(ins[t].at[2 * px + py], outs[t].at[k - 1], send, recv, 3 * t + k - 1, (px, py, c)))
        return res

    fresh = [jax.ShapeDtypeStruct((3,) + p.shape[1:], p.dtype) for p in parts]
    return _rider("chips", parts, [], fresh, 3 * nt, both, both)


def broadcast_rider(bufs, items):
    def region(outs, item, sel):
        bi, lead = item
        ref = outs[bi]
        if lead == "chip":
            x, y, _ = _place()
            ref = ref.at[2 * x + y]
        elif lead is not None:
            ref = ref.at[lead]
        return ref.at[_half(ref.shape[0], sel, F32)]

    def both(outs, send, recv, mine):
        x, y, c = _place()
        res = []
        for i, item in enumerate(items):
            part = region(outs, item, c if mine else 1 - c)
            res.append(_remote(part, part, send, recv, i, (x, y, 1 - c)))
        return res

    return _rider("sibling", bufs, range(len(bufs)), [], len(items), lambda i, o, s, r: both(o, s, r, True),
                  lambda i, o, s, r: both(o, s, r, False))


def allcast_rider(buf):
    peers = [(k, flip) for k in range(N_CHIPS) for flip in (0, 1) if (k, flip) != (0, 0)]

    def both(outs, send, recv, mine):
        x, y, c = _place()
        res = []
        for i, (k, flip) in enumerate(peers):
            px, py = _partner(x, y, k)
            pc = 1 - c if flip else c
            slot, sel = (2 * x + y, c) if mine else (2 * px + py, pc)
            part = outs[0].at[slot, _half(buf.shape[1], sel, F32)]
            res.append(_remote(part, part, send, recv, i, (px, py, pc)))
        return res

    return _rider("everyone", [buf], [0], [], len(peers), lambda i, o, s, r: both(o, s, r, True),
                  lambda i, o, s, r: both(o, s, r, False))


def comm_call(riders, *, name):
    _, res = _call(None, riders=riders, name=name)
    return res


SEMS = pl.BlockSpec(memory_space=pltpu.SEMAPHORE)
SIDE_EFFECT = pltpu.SideEffectType.DATAFLOW_SIDE_EFFECTING


def _split_refs(riders, refs):
    views, p = [], 0
    for r in riders:
        bufs = refs[p:p + len(r.inputs) + len(r.fresh)]
        p += len(bufs)
        ins = bufs[:len(r.inputs)]
        views.append([ins, [ins[i] for i in r.aliased] + list(bufs[len(r.inputs):])])
    for view in views:
        view += [refs[p], refs[p + 1]]
        p += 2
    return views


def comm_start(riders, *, name):
    kind = _peer_kind(riders)
    bufs = [a for r in riders for a in r.inputs]
    fresh = [f for r in riders for f in r.fresh]
    n_buf, n_fresh = len(bufs), len(fresh)

    def body(*refs):
        ins, outs = refs[:n_buf], refs[n_buf:]
        through, land, sems = outs[:n_buf], outs[n_buf:n_buf + n_fresh], outs[n_buf + n_fresh:-1]
        _peer_barrier(kind)
        per_rider, pb, pf = [], 0, 0
        for r in riders:
            per_rider += list(through[pb:pb + len(r.inputs)]) + list(land[pf:pf + len(r.fresh)])
            pb, pf = pb + len(r.inputs), pf + len(r.fresh)
        for r, (r_ins, r_outs, send, recv) in zip(riders, _split_refs(riders, per_rider + list(sems))):
            r.start(r_ins, r_outs, send, recv)
        outs[-1][...] = jnp.zeros((8, LANES), F32)

    sem_shapes = [pltpu.SemaphoreType.DMA((r.nsem,)) for r in riders for _ in (0, 1)]
    res = pl.pallas_call(
        body, name=name, in_specs=[ANY] * n_buf,
        out_specs=[ANY] * (n_buf + n_fresh) + [SEMS] * len(sem_shapes) + [pl.BlockSpec(memory_space=pltpu.VMEM)],
        out_shape=[jax.ShapeDtypeStruct(a.shape, a.dtype) for a in bufs] + fresh + sem_shapes
        + [jax.ShapeDtypeStruct((8, LANES), F32)],
        input_output_aliases={i: i for i in range(n_buf)},
        compiler_params=pltpu.CompilerParams(has_side_effects=SIDE_EFFECT, collective_id=PEER_KINDS.index(kind)))(*bufs)
    return (riders, list(res[:n_buf + n_fresh]), list(res[n_buf + n_fresh:-1])), res[-1]


def comm_wait(state, after, *, name):
    riders, bufs, sems = state
    n_buf, n_sem = len(bufs), len(sems)
    n_in = sum(len(r.inputs) for r in riders)

    def body(*refs):
        held, sem_refs = refs[:n_buf], refs[n_buf:n_buf + n_sem]
        through, land = held[:n_in], held[n_in:]
        per_rider, pb, pf = [], 0, 0
        for r in riders:
            per_rider += list(through[pb:pb + len(r.inputs)]) + list(land[pf:pf + len(r.fresh)])
            pb, pf = pb + len(r.inputs), pf + len(r.fresh)
        for r, (r_ins, r_outs, send, recv) in zip(riders, _split_refs(riders, per_rider + list(sem_refs))):
            r.finish(r_ins, r_outs, send, recv)

    res = pl.pallas_call(
        body, name=name, in_specs=[ANY] * n_buf + [SEMS] * n_sem + [ANY], out_specs=[ANY] * n_buf,
        out_shape=[jax.ShapeDtypeStruct(a.shape, a.dtype) for a in bufs],
        input_output_aliases={i: i for i in range(n_buf)},
        compiler_params=pltpu.CompilerParams(has_side_effects=SIDE_EFFECT))(*bufs, *sems, after)
    through, land = list(res[:n_in]), list(res[n_in:])
    out, pb, pf = [], 0, 0
    for r in riders:
        r_ins, r_land = through[pb:pb + len(r.inputs)], land[pf:pf + len(r.fresh)]
        pb, pf = pb + len(r.inputs), pf + len(r.fresh)
        out.append([r_ins[i] for i in r.aliased] + r_land)
    return out


SLAB_ROWS = 192


def _pad_rows(a, rows=8):
    return jnp.pad(a, ((0, rows - a.shape[0]), (0, 0)))


def _pack_small(norm_grads, db_qkv, db_o, dsinks, db_sp, dln_g, dln_b, dw_sp, loss_part):
    parts = [
        jnp.concatenate(norm_grads, axis=0),
        _pad_rows(jnp.pad(db_qkv, ((0, 0), (0, 2 * D_MODEL - QKV_WIDTH))).reshape(2, D_MODEL)),
        _pad_rows(db_o),
        _pad_rows(jnp.pad(dsinks.reshape(1, N_Q_HEADS), ((0, 0), (0, D_MODEL - N_Q_HEADS)))),
        _pad_rows(db_sp.reshape(1, D_MODEL)),
        _pad_rows(jnp.concatenate([dln_g, dln_b, jnp.pad(loss_part[0:1], ((0, 0), (0, D_MODEL - LANES)))], axis=0)),
        dw_sp.reshape(SGU_CHUNK, D_MODEL),
    ]
    slab = jnp.concatenate(parts, axis=0)
    return jnp.pad(slab, ((0, SLAB_ROWS - slab.shape[0]), (0, 0))).reshape(N_CHIPS, SLAB_ROWS // N_CHIPS, D_MODEL)


def _unpack_small(slab, j):
    slab = slab.reshape(SLAB_ROWS, D_MODEL)
    norms = [slab[2 * i:2 * i + 2] for i in range(4)]
    db_qkv = slab[8:10].reshape(1, 2 * D_MODEL)[:, :QKV_WIDTH]
    db_o = slab[16:17]
    dsinks = slab[24:25, :N_Q_HEADS]
    db_sp = slab[32:33].reshape(SGU_GROUPS, SGU_CHUNK)
    width = D_MODEL // N_CHIPS
    dln_g = lax.dynamic_slice(slab[40:41], (0, j * width), (1, width))
    dln_b = lax.dynamic_slice(slab[41:42], (0, j * width), (1, width))
    dw_sp = slab[48:48 + SGU_CHUNK].reshape(SGU_GROUPS * SGU_CHUNK, SGU_CHUNK)
    return norms, db_qkv, db_o, dsinks, db_sp, dln_g, dln_b, dw_sp, slab[42, 0]


class _GradReduce:
    def __init__(self, c_arr, jc_arr, dest_shapes):
        self.c_arr, self.jc_arr, self.dest_shapes = c_arr, jc_arr, dest_shapes
        self.grad, self.sibling, self.pair, self.chips, self.dest = {}, {}, {}, {}, {}

    def exchange(self, tags):
        return exchange_rider([self.grad[t] for t in tags])

    def exchanged(self, tags, res):
        for t, r in zip(tags, res):
            self.sibling[t] = r
            self.pair[t] = pair_add(self.grad[t], r, self.c_arr, name=f"pair_add_{t}")

    def scatter(self, tags):
        return scatter_rider([self.pair[t] for t in tags])

    def scattered(self, tags, res, where):
        for t, r in zip(tags, res):
            name, lead = where[t]
            self.dest[name] = final_add(self.grad[t], self.sibling[t], r, self.jc_arr, dest_shape=self.dest_shapes[name],
                                        lead=lead, prev=self.dest.get(name), name=f"final_add_{t}")

    def broadcast(self, items):
        names = []
        for n, _ in items:
            if n not in names:
                names.append(n)
        return names, broadcast_rider([self.dest[n] for n in names], [(names.index(n), lead) for n, lead in items])

    def broadcasted(self, names, res):
        for n, r in zip(names, res):
            self.dest[n] = r


def kernel(x, norm_mix_pre, norm_mix_post, norm_ffn_pre, norm_ffn_post, attn_w_qkv, attn_b_qkv, attn_sinks, attn_w_o, attn_b_o, sgu_w_in, sgu_ln_g, sgu_ln_b, sgu_w_spatial, sgu_b_spatial, sgu_w_out, ffn_w_gate_up, ffn_w_down, loss_target, m_norm_mix_pre, m_norm_mix_post, m_norm_ffn_pre, m_norm_ffn_post, m_attn_w_qkv, m_attn_b_qkv, m_attn_sinks, m_attn_w_o, m_attn_b_o, m_sgu_w_in, m_sgu_ln_g, m_sgu_ln_b, m_sgu_w_spatial, m_sgu_b_spatial, m_sgu_w_out, m_ffn_w_gate_up, m_ffn_w_down, v_norm_mix_pre, v_norm_mix_post, v_norm_ffn_pre, v_norm_ffn_post, v_attn_w_qkv, v_attn_b_qkv, v_attn_sinks, v_attn_w_o, v_attn_b_o, v_sgu_w_in, v_sgu_ln_g, v_sgu_ln_b, v_sgu_w_spatial, v_sgu_b_spatial, v_sgu_w_out, v_ffn_w_gate_up, v_ffn_w_down):
    s = x.shape[1]
    x0 = x.reshape(s, D_MODEL)
    target = loss_target.reshape(s, D_MODEL)
    mx, my, mc = lax.axis_index("x"), lax.axis_index("y"), lax.axis_index("c")
    chip = 2 * mx + my
    chip_arr = jnp.reshape(chip, (1,)).astype(I32)
    c_arr = jnp.reshape(mc, (1,)).astype(I32)
    jc_arr = jnp.stack([chip, mc]).astype(I32)
    zero_bias = jnp.zeros((1, D_MODEL), F32)

    def gain(p, i):
        return p[i:i + 1]

    big = [attn_w_qkv, attn_w_o, sgu_w_in, sgu_w_out, ffn_w_gate_up, ffn_w_gate_up, ffn_w_down, ffn_w_down]
    layers = [0, 0, 0, 0, 0, 1, 0, 1]
    tags = ["qkv", "wo", "win", "wout", "wgu0", "wgu1", "wd0", "wd1"]
    full = {t: place_shard(w, l, chip_arr, BF16, name=f"place_{t}") for w, l, t in zip(big, layers, tags) if t != "wgu1"}
    ln_pack = _pad_rows(jnp.concatenate([sgu_ln_g, sgu_ln_b], axis=0), 16)[None]
    full["ln"] = place_shard(ln_pack, 0, chip_arr, F32, name="place_ln")

    def split(items):
        return [i if isinstance(i, str) else i[0] for i in items], [WHOLE if isinstance(i, str) else tuple(i[1:]) for i in items]

    def ici(*items):
        names, pieces = split(items)
        return gather_ici_rider([full[n] for n in names], pieces)

    def d2d(*items):
        names, pieces = split(items)
        return gather_d2d_rider([full[n] for n in names], pieces)

    def landed(items, res):
        for n, r in zip(split(items)[0], res):
            full[n] = r

    cos, sin = _rope_tables(s)
    sink_rows = jnp.broadcast_to(
        jnp.repeat(attn_sinks.reshape(N_KV_HEADS, GQA_GROUP), WINDOW, axis=1)[:, None, :], (N_KV_HEADS, 8, ROWS))
    w_sp = sgu_w_spatial.reshape(SGU_GROUPS, SGU_CHUNK, SGU_CHUNK)
    b_sp = jnp.broadcast_to(sgu_b_spatial.reshape(SGU_GROUPS, SGU_CHUNK)[:, :, None], (SGU_GROUPS, SGU_CHUNK, LANES))

    h0, (res,) = prenorm(x0, gain(norm_mix_pre, 0), name="prenorm_0", riders=[ici("qkv", "ln")])
    landed(("qkv", "ln"), res)
    full["wgu1"], (res,) = place_shard(ffn_w_gate_up, 1, chip_arr, BF16, name="place_wgu1", riders=[d2d("qkv", "ln")])
    landed(("qkv", "ln"), res)
    ln_g = full["ln"][:, 0, :].reshape(1, D_MODEL)
    ln_b = full["ln"][:, 1, :].reshape(1, D_MODEL)

    def hosted(call, stages):
        outputs, results = call([{"ici": ici, "d2d": d2d}[kind](*items) for kind, items in stages])
        for (_, items), res in zip(stages, results):
            landed(items, res)
        return outputs

    qkv = hosted(lambda r: qkv_proj(h0, full["qkv"], attn_b_qkv, cos, sin, name="qkv_proj", riders=r),
                 [("ici", ("wo", ("wgu0", 0, 3, 8)))])
    o = hosted(lambda r: attn_fwd(qkv, sink_rows, name="attn_fwd", riders=r),
               [("d2d", ("wo",)), ("ici", (("wgu0", 3, 8, 8), ("wd0", 0, 2, 11)))])
    w_o = full["wo"].reshape(Q_WIDTH, D_MODEL)
    x1, h1, m0 = hosted(lambda r: proj_residual_norm(o, w_o, x0, attn_b_o, gain(norm_mix_post, 0), gain(norm_ffn_pre, 0),
                                                     name="attn_out_norm", riders=r),
                        [("d2d", ("wgu0",)), ("ici", (("wd0", 2, 11, 11),))])
    gu0, a0 = hosted(lambda r: ffn_up(h1, full["wgu0"], name="ffn_up_0", riders=r),
                     [("d2d", ("wd0",)), ("ici", ("win", "wout", ("wgu1", 0, 4, 8)))])
    w_d0 = full["wd0"].reshape(D_FF, D_MODEL)
    x2, h2, f0 = hosted(lambda r: proj_residual_norm(a0, w_d0, x1, zero_bias, gain(norm_ffn_post, 0), gain(norm_mix_pre, 1),
                                                     name="ffn_down_norm_0", riders=r),
                        [("d2d", ("win", "wout")), ("ici", (("wgu1", 4, 8, 8),))])
    w_in = full["win"]
    z, y = hosted(lambda r: sgu_in_fwd(h2, w_in, ln_g, ln_b, w_sp, b_sp, name="sgu_in_fwd", riders=r),
                  [("d2d", ("wgu1",)), ("ici", ("wd1",))])
    w_out = full["wout"].reshape(D_MODEL, D_MODEL)
    x3, h3, m1 = hosted(lambda r: proj_residual_norm(y, w_out, x2, zero_bias, gain(norm_mix_post, 1), gain(norm_ffn_pre, 1),
                                                     name="sgu_out_norm", riders=r),
                        [("d2d", ("wd1",))])
    w_qkv, w_gu0, w_gu1 = full["qkv"], full["wgu0"], full["wgu1"]
    w_d1 = full["wd1"].reshape(D_FF, D_MODEL)
    gu1, a1, dx4, df1, dg_fpost1, loss_part = ffn_fwd_loss_rows(
        h3, w_gu1, w_d1, x3, gain(norm_ffn_post, 1), target, name="ffn_fwd_loss_rows")

    red = _GradReduce(c_arr, jc_arr, {
        "qkv": attn_w_qkv.shape[1:], "wo": attn_w_o.shape[1:], "win": sgu_w_in.shape[1:], "wout": sgu_w_out.shape[1:],
        "wgu": ffn_w_gate_up.shape, "wd": ffn_w_down.shape, "slab": (N_CHIPS, SLAB_ROWS // N_CHIPS, D_MODEL)})
    where = {"qkv": ("qkv", None), "wo": ("wo", None), "win": ("win", None), "wout": ("wout", None), "wgu0": ("wgu", 0),
             "wgu1": ("wgu", 1), "wd0": ("wd", 0), "wd1": ("wd", 1), "small": ("slab", "chip")}

    dgu1, dx3, dm1, dg_fpre1, dg_mpost1, _ = ffn_bwd_rows(
        df1, w_d1, gu1, w_gu1, dx4, x3, gain(norm_ffn_pre, 1), m1, gain(norm_mix_post, 1), name="ffn_bwd_rows_1")
    red.grad["wd1"] = mm_tn(a1, df1, shard_major=False, tm=256, tn=D_MODEL, name="dw_down_1").reshape(
        N_CHIPS, D_FF // N_CHIPS, D_MODEL)
    red.grad["wgu1"], (res,) = mm_tn(h3, dgu1, shard_major=True, tm=512, tn=FF_HALF, name="dw_gate_up_1",
                                     riders=[red.exchange(["wd1"])])
    red.exchanged(["wd1"], res)
    dy, (res,) = mm_nt(dm1, w_out, out_dtype=F32, name="dy_sgu", riders=[red.exchange(["wgu1"])])
    red.exchanged(["wgu1"], res)
    red.grad["wout"] = mm_tn(y, dm1, shard_major=False, tm=512, tn=D_MODEL, name="dw_sgu_out").reshape(
        N_CHIPS, D_MODEL // N_CHIPS, D_MODEL)
    (dz, dw_sp, db_sp, dln_g, dln_b), (res_a, res_b) = sgu_bwd(
        z, dy, ln_g, ln_b, w_sp, b_sp, name="sgu_bwd", riders=[red.scatter(["wgu1"]), red.exchange(["wout"])])
    red.scattered(["wgu1"], res_a, where)
    red.exchanged(["wout"], res_b)
    names, rider = red.broadcast([("wgu", 1)])
    red.grad["win"], (res_a, res_b) = mm_tn(h2, dz, shard_major=True, tm=D_MODEL, tn=2 * D_MODEL // N_CHIPS, name="dw_sgu_in",
                                            riders=[rider, red.scatter(["wout"])])
    red.broadcasted(names, res_a)
    red.scattered(["wout"], res_b, where)
    names, rider = red.broadcast([("wout", None)])
    (dx2, df0, dg_mpre1, dg_fpost0, _), (res_a, res_b) = dh_norm_bwd_pair(
        dz, w_in, dx3, x2, gain(norm_mix_pre, 1), f0, gain(norm_ffn_post, 0), name="dh_sgu_norm",
        riders=[red.exchange(["win"]), rider])
    red.exchanged(["win"], res_a)
    red.broadcasted(names, res_b)
    (dgu0, dx1, dm0, dg_fpre0, dg_mpost0, db_o), (res,) = ffn_bwd_rows(
        df0, w_d0, gu0, w_gu0, dx2, x1, gain(norm_ffn_pre, 0), m0, gain(norm_mix_post, 0), name="ffn_bwd_rows_0",
        riders=[red.scatter(["wd1", "win"])])
    red.scattered(["wd1", "win"], res, where)
    names, rider = red.broadcast([("wd", 1), ("win", None)])
    dw_d0, (res,) = mm_tn(a0, df0, shard_major=False, tm=256, tn=D_MODEL, name="dw_down_0", riders=[rider])
    red.broadcasted(names, res)
    red.grad["wd0"] = dw_d0.reshape(N_CHIPS, D_FF // N_CHIPS, D_MODEL)
    do, (res,) = mm_nt(dm0, w_o, out_dtype=BF16, name="do_attn", riders=[red.exchange(["wd0"])])
    red.exchanged(["wd0"], res)
    red.grad["wgu0"], (res,) = mm_tn(h1, dgu0, shard_major=True, tm=512, tn=FF_HALF, name="dw_gate_up_0",
                                     riders=[red.scatter(["wd0"])])
    red.scattered(["wd0"], res, where)
    names, rider = red.broadcast([("wd", 0)])
    dw_o, (res_a, res_b) = mm_tn(o, dm0, shard_major=False, tm=512, tn=D_MODEL, name="dw_attn_out",
                                 riders=[red.exchange(["wgu0"]), rider])
    red.exchanged(["wgu0"], res_a)
    red.broadcasted(names, res_b)
    red.grad["wo"] = dw_o.reshape(N_CHIPS, Q_WIDTH // N_CHIPS, D_MODEL)
    (dq, dkc, dkp, dvc, dvp, dsink), (res_a, res_b) = attn_bwd(
        qkv, sink_rows, do, name="attn_bwd", riders=[red.scatter(["wgu0"]), red.exchange(["wo"])])
    red.scattered(["wgu0"], res_a, where)
    red.exchanged(["wo"], res_b)
    names, rider = red.broadcast([("wgu", 0)])
    (dqkv, db_qkv), (res_a, res_b) = rope_bwd(dq, dkc, dkp, dvc, dvp, cos, sin, name="rope_bwd",
                                              riders=[rider, red.scatter(["wo"])])
    red.broadcasted(names, res_a)
    red.scattered(["wo"], res_b, where)
    names, rider = red.broadcast([("wo", None)])
    red.grad["qkv"], (res,) = mm_tn(h0, dqkv, shard_major=True, tm=D_MODEL, tn=QKV_WIDTH // N_CHIPS, name="dw_qkv",
                                    riders=[rider])
    red.broadcasted(names, res)
    grad_x, dg_mpre0 = dh_norm_bwd_last(dqkv, w_qkv, dx1, x0, gain(norm_mix_pre, 0), name="dh_attn_norm_in")

    norm_grads = [jnp.concatenate(p, axis=0) for p in
                  ((dg_mpre0, dg_mpre1), (dg_mpost0, dg_mpost1), (dg_fpre0, dg_fpre1), (dg_fpost0, dg_fpost1))]
    red.grad["small"] = _pack_small(norm_grads, db_qkv, db_o, dsink[:, :, 0, 0], db_sp[:, :, 0], dln_g, dln_b, dw_sp,
                                    loss_part)
    def big_update(w, g, m, v, tag, after=None):
        return adamw(w, g.reshape(w.shape), m, v, name=f"adamw_{tag}", after=after)

    (res,) = comm_call([red.exchange(["qkv", "small"])], name="tail_1")
    red.exchanged(["qkv", "small"], res)
    state, token = comm_start([red.scatter(["qkv", "small"])], name="tail_2_start")
    upd_wgu = big_update(ffn_w_gate_up, red.dest["wgu"], m_ffn_w_gate_up, v_ffn_w_gate_up, "wgu", after=token)
    (res,) = comm_wait(state, upd_wgu[1], name="tail_2_wait")
    red.scattered(["qkv", "small"], res, where)
    names, rider = red.broadcast([("qkv", None)])
    state, token = comm_start([rider, allcast_rider(red.dest["slab"])], name="tail_3_start")
    upd_wd = big_update(ffn_w_down, red.dest["wd"], m_ffn_w_down, v_ffn_w_down, "wd", after=token)
    (res_a,), (slab_full,) = comm_wait(state, upd_wd[1], name="tail_3_wait")
    red.broadcasted(names, [res_a])
    g_qkv, g_wo, g_win, g_wout = (red.dest[n] for n in ("qkv", "wo", "win", "wout"))
    g_norms, g_bqkv, g_bo, g_sinks, g_bsp, g_lng, g_lnb, g_wsp, loss = _unpack_small(slab_full, chip)

    upd = {
        "attn_w_qkv": big_update(attn_w_qkv, g_qkv, m_attn_w_qkv, v_attn_w_qkv, "qkv"),
        "attn_w_o": big_update(attn_w_o, g_wo, m_attn_w_o, v_attn_w_o, "wo"),
        "sgu_w_in": big_update(sgu_w_in, g_win, m_sgu_w_in, v_sgu_w_in, "win"),
        "sgu_w_out": big_update(sgu_w_out, g_wout, m_sgu_w_out, v_sgu_w_out, "wout"),
        "ffn_w_gate_up": upd_wgu,
        "ffn_w_down": upd_wd,
    }
    small_names = ["norm_mix_pre", "norm_mix_post", "norm_ffn_pre", "norm_ffn_post", "attn_b_qkv", "attn_sinks", "attn_b_o",
                   "sgu_ln_g", "sgu_ln_b", "sgu_w_spatial", "sgu_b_spatial"]
    small_w = [norm_mix_pre, norm_mix_post, norm_ffn_pre, norm_ffn_post, attn_b_qkv, attn_sinks, attn_b_o, sgu_ln_g, sgu_ln_b,
               sgu_w_spatial, sgu_b_spatial]
    small_m = [m_norm_mix_pre, m_norm_mix_post, m_norm_ffn_pre, m_norm_ffn_post, m_attn_b_qkv, m_attn_sinks, m_attn_b_o,
               m_sgu_ln_g, m_sgu_ln_b, m_sgu_w_spatial, m_sgu_b_spatial]
    small_v = [v_norm_mix_pre, v_norm_mix_post, v_norm_ffn_pre, v_norm_ffn_post, v_attn_b_qkv, v_attn_sinks, v_attn_b_o,
               v_sgu_ln_g, v_sgu_ln_b, v_sgu_w_spatial, v_sgu_b_spatial]
    small_g = g_norms + [g_bqkv, g_sinks, g_bo, g_lng, g_lnb, g_wsp, g_bsp]

    def flat2(a):
        return a.reshape(-1, a.shape[-1])

    res = adamw_small([flat2(a) for a in small_w], [flat2(a) for a in small_g], [flat2(a) for a in small_m],
                      [flat2(a) for a in small_v], name="adamw_small")
    for i, nm in enumerate(small_names):
        upd[nm] = tuple(r[i].reshape(small_w[i].shape) for r in res)

    order = ["norm_mix_pre", "norm_mix_post", "norm_ffn_pre", "norm_ffn_post", "attn_w_qkv", "attn_b_qkv", "attn_sinks",
             "attn_w_o", "attn_b_o", "sgu_w_in", "sgu_ln_g", "sgu_ln_b", "sgu_w_spatial", "sgu_b_spatial", "sgu_w_out",
             "ffn_w_gate_up", "ffn_w_down"]
    outs = [loss, grad_x.reshape(1, s, D_MODEL)]
    for part in range(4):
        outs += [upd[nm][part] for nm in order]
    return tuple(outs)
```

```python
import types

import numpy as np
import jax
import jax.numpy as jnp
from jax import lax
from jax.experimental import pallas as pl
from jax.experimental.pallas import tpu as pltpu

F32 = jnp.float32
BF16 = jnp.bfloat16
I32 = jnp.int32

D_MODEL = 1024
HEAD_DIM = 64
N_Q_HEADS = 16
N_KV_HEADS = 4
GQA_GROUP = 4
WINDOW = 128
Q_WIDTH = 1024
KV_WIDTH = 256
QKV_WIDTH = 1536
ROPE_THETA = 10000.0
SGU_GROUPS = 8
SGU_CHUNK = 128
D_FF = 2816
FF_HALF = D_FF // 2
EPS = 1e-6
N_CHIPS = 4
LANES = 128

ADAM_LR = 0.001
ADAM_B1 = 0.9
ADAM_B2 = 0.999
ADAM_EPS = 1e-08
ADAM_WD = 0.01
ADAM_STEP = 10

VMEM_LIMIT = 52 * 1024 * 1024
MESH = pl.DeviceIdType.MESH
NEG = -1e30
NT_DIMS = (((1,), (1,)), ((), ()))
TN_DIMS = (((0,), (0,)), ((), ()))
NN_DIMS = (((1,), (0,)), ((), ()))
ANY = pl.BlockSpec(memory_space=pl.ANY)


def _row_tile(s, want):
    return want if s % want == 0 else s


PEER_KINDS = ("sibling", "chips", "sibling+chips", "everyone")


def _peer_kind(riders):
    kinds = {r.peers for r in riders}
    if not kinds:
        return None
    if "everyone" in kinds:
        return "everyone"
    return "sibling+chips" if len(kinds) == 2 else kinds.pop()


def _peer_barrier(kind):
    x, y, c = _place()
    chips = [(*_partner(x, y, k), c) for k in (1, 2, 3)]
    peers = {"sibling": [(x, y, 1 - c)], "chips": chips, "sibling+chips": [(x, y, 1 - c)] + chips,
             "everyone": [(x, y, 1 - c)] + chips + [(px, py, 1 - c) for px, py, _ in chips]}[kind]
    barrier = pltpu.get_barrier_semaphore()
    for dev in peers:
        pl.semaphore_signal(barrier, inc=1, device_id=dev, device_id_type=MESH)
    pl.semaphore_wait(barrier, len(peers))


def _call(body, *, name, grid=(), in_specs=(), out_specs=(), out_shape=(), scratch_shapes=(), operands=(), prefetch=(),
          aliases=None, riders=(), sem=None):
    n_pre, n_in, n_out, n_scr = len(prefetch), len(operands), len(out_shape), len(scratch_shapes)
    in_specs, out_specs, out_shape = list(in_specs), list(out_specs), list(out_shape)
    operands, scratch_shapes = list(operands), list(scratch_shapes)
    io_alias = {n_pre + i: o for i, o in (aliases or {}).items()}
    for r in riders:
        base_in, base_out = n_pre + len(operands), len(out_shape)
        operands += list(r.inputs)
        in_specs += [ANY] * len(r.inputs)
        for pos, i in enumerate(r.aliased):
            io_alias[base_in + i] = base_out + pos
            out_shape.append(jax.ShapeDtypeStruct(r.inputs[i].shape, r.inputs[i].dtype))
        out_shape += list(r.fresh)
        out_specs += [ANY] * (len(r.aliased) + len(r.fresh))
        scratch_shapes += [pltpu.SemaphoreType.DMA((r.nsem,)), pltpu.SemaphoreType.DMA((r.nsem,))]

    def wrapped(*refs):
        pre, p = refs[:n_pre], n_pre
        core_in, p = refs[p:p + n_in], p + n_in
        r_in = []
        for r in riders:
            r_in.append(refs[p:p + len(r.inputs)])
            p += len(r.inputs)
        core_out, p = refs[p:p + n_out], p + n_out
        r_out = []
        for r in riders:
            k = len(r.aliased) + len(r.fresh)
            r_out.append(refs[p:p + k])
            p += k
        core_scr, p = refs[p:p + n_scr], p + n_scr
        r_sem = [refs[p + 2 * i:p + 2 * i + 2] for i in range(len(riders))]

        def edge(at_last, fns):
            def run():
                if not at_last:
                    _peer_barrier(peer_kind)
                for i, r in enumerate(riders):
                    getattr(r, fns)(r_in[i], r_out[i], r_sem[i][0], r_sem[i][1])
            if not riders:
                return
            if not grid:
                run()
                return
            cond = None
            for d, n in enumerate(grid):
                c = pl.program_id(d) == (n - 1 if at_last else 0)
                cond = c if cond is None else jnp.logical_and(cond, c)
            pl.when(cond)(run)

        edge(False, "start")
        if body is not None:
            body(*pre, *core_in, *core_out, *core_scr)
        edge(True, "finish")

    if sem is None or riders:
        sem = ("arbitrary",) * len(grid)
    kwargs = dict(out_shape=out_shape, input_output_aliases=io_alias, name=name)
    peer_kind = _peer_kind(riders)
    collective = {} if peer_kind is None else {"collective_id": PEER_KINDS.index(peer_kind)}
    if grid:
        kwargs["compiler_params"] = pltpu.CompilerParams(dimension_semantics=sem, vmem_limit_bytes=VMEM_LIMIT, **collective)
    elif collective:
        kwargs["compiler_params"] = pltpu.CompilerParams(**collective)
    if n_pre:
        kwargs["grid_spec"] = pltpu.PrefetchScalarGridSpec(
            num_scalar_prefetch=n_pre, grid=grid, in_specs=in_specs, out_specs=out_specs, scratch_shapes=scratch_shapes)
    else:
        kwargs.update(grid=grid, in_specs=in_specs, out_specs=out_specs, scratch_shapes=scratch_shapes)
    res = pl.pallas_call(wrapped, **kwargs)(*prefetch, *operands)
    core, rest, rider_res = list(res[:n_out]), list(res[n_out:]), []
    for r in riders:
        k = len(r.aliased) + len(r.fresh)
        rider_res.append(rest[:k])
        rest = rest[k:]
    return core, rider_res


def _mm_call(*, grid, in_specs, out_spec, out_shape, dims, nk, kaxis, acc_shape, name, operands, riders=()):
    out_dtype = out_shape.dtype

    def body(a_ref, b_ref, o_ref, *scratch):
        p = lax.dot_general(a_ref[...].astype(BF16), b_ref[...].astype(BF16), dims, preferred_element_type=F32)
        if nk == 1:
            o_ref[...] = p.astype(out_dtype)
        else:
            acc = scratch[0]
            kk = pl.program_id(kaxis)

            @pl.when(kk == 0)
            def _():
                acc[...] = p

            @pl.when(kk > 0)
            def _():
                acc[...] += p

            @pl.when(kk == nk - 1)
            def _():
                o_ref[...] = acc[...].astype(out_dtype)

    sem = ["parallel"] * len(grid)
    if nk > 1:
        sem[kaxis] = "arbitrary"
    (out,), rider_res = _call(
        body, grid=grid, in_specs=in_specs, out_specs=[out_spec], out_shape=[out_shape],
        scratch_shapes=[pltpu.VMEM(acc_shape, F32)] if nk > 1 else [], operands=operands, name=name, riders=riders,
        sem=tuple(sem))
    return (out, rider_res) if riders else out


def mm_nt(a, w, *, out_dtype, name, tm=1024, riders=()):
    m, n = a.shape
    kout = w.shape[0]
    tm = _row_tile(m, tm)
    return _mm_call(grid=(m // tm,),
                    in_specs=[pl.BlockSpec((tm, n), lambda i: (i, 0)), pl.BlockSpec((kout, n), lambda i: (0, 0))],
                    out_spec=pl.BlockSpec((tm, kout), lambda i: (i, 0)),
                    out_shape=jax.ShapeDtypeStruct((m, kout), out_dtype), dims=NT_DIMS, nk=1, kaxis=0,
                    acc_shape=None, name=name, operands=(a, w), riders=riders)


def mm_tn(a, b, *, shard_major, name, tm, tn, tk=None, out_dtype=BF16, riders=()):
    s, m = a.shape
    tk = s if tk is None else _row_tile(s, tk)
    if b.ndim == 3:
        n = 2 * b.shape[2]
        b_spec = pl.BlockSpec((None, tk, tn), lambda j, i, kk: (j // 2, kk, j % 2))
    else:
        n = b.shape[1]
        b_spec = pl.BlockSpec((tk, tn), lambda j, i, kk: (kk, j))
    if shard_major:
        assert tn == n // N_CHIPS
        o_spec = pl.BlockSpec((None, tm, tn), lambda j, i, kk: (j, i, 0))
        o_shape = jax.ShapeDtypeStruct((N_CHIPS, m, tn), out_dtype)
    else:
        o_spec = pl.BlockSpec((tm, tn), lambda j, i, kk: (i, j))
        o_shape = jax.ShapeDtypeStruct((m, n), out_dtype)
    return _mm_call(grid=(n // tn, m // tm, s // tk),
                    in_specs=[pl.BlockSpec((tk, tm), lambda j, i, kk: (kk, i)), b_spec], out_spec=o_spec,
                    out_shape=o_shape, dims=TN_DIMS, nk=s // tk, kaxis=2, acc_shape=(tm, tn), name=name, operands=(a, b),
                    riders=riders)


def _rstd(x):
    return lax.rsqrt(jnp.mean(x * x, axis=-1, keepdims=True) + EPS)


def _rms_bwd(dy, x, g):
    r = _rstd(x)
    xhat = x * r
    gy = dy * g
    dx = r * (gy - xhat * jnp.mean(gy * xhat, axis=-1, keepdims=True))
    return dx, jnp.sum(dy * xhat, axis=0, keepdims=True)


def _accum(ref, val, first):
    @pl.when(first)
    def _():
        ref[...] = val

    @pl.when(jnp.logical_not(first))
    def _():
        ref[...] += val


def _row_spec(tm, width):
    return pl.BlockSpec((tm, width), lambda i: (i, 0))


def _vec_spec(width):
    return pl.BlockSpec((1, width), lambda i: (0, 0))


def _ret(core, rider_res, riders):
    core = core[0] if len(core) == 1 else core
    return (core, rider_res) if riders else core


def prenorm(x, g, *, name, tm=256, riders=()):
    s = x.shape[0]
    tm = _row_tile(s, tm)

    def body(x_ref, g_ref, h_ref):
        xv = x_ref[...]
        h_ref[...] = (xv * _rstd(xv) * g_ref[...]).astype(BF16)

    core, rr = _call(
        body, grid=(s // tm,), in_specs=[_row_spec(tm, D_MODEL), _vec_spec(D_MODEL)], out_specs=[_row_spec(tm, D_MODEL)],
        out_shape=[jax.ShapeDtypeStruct((s, D_MODEL), BF16)], operands=(x, g), sem=("parallel",), name=name, riders=riders)
    return _ret(core, rr, riders)


def proj_residual_norm(a, w, x, bias, g_post, g_next, *, name, tm=512, sub=256, riders=()):
    s, k = a.shape
    tm = _row_tile(s, tm)
    sub = min(sub, tm)

    def body(a_ref, w_ref, x_ref, b_ref, gp_ref, gn_ref, xo_ref, h_ref, m_ref):
        for t in range(tm // sub):
            rows = slice(t * sub, (t + 1) * sub)
            mv = jnp.dot(a_ref[rows, :], w_ref[...], preferred_element_type=F32) + b_ref[...]
            m_ref[rows, :] = mv.astype(BF16)
            xn = x_ref[rows, :] + mv * _rstd(mv) * gp_ref[...]
            xo_ref[rows, :] = xn
            h_ref[rows, :] = (xn * _rstd(xn) * gn_ref[...]).astype(BF16)

    row, vec = _row_spec(tm, D_MODEL), _vec_spec(D_MODEL)
    core, rr = _call(
        body, grid=(s // tm,),
        in_specs=[_row_spec(tm, k), pl.BlockSpec((k, D_MODEL), lambda i: (0, 0)), row, vec, vec, vec], out_specs=[row, row, row],
        out_shape=[jax.ShapeDtypeStruct((s, D_MODEL), F32), jax.ShapeDtypeStruct((s, D_MODEL), BF16),
                   jax.ShapeDtypeStruct((s, D_MODEL), BF16)],
        operands=(a, w, x, bias, g_post, g_next), sem=("parallel",), name=name, riders=riders)
    return _ret(core, rr, riders)


def ffn_fwd_loss_rows(h, w_gu, w_d, x, g_post, target, *, name, tm=256, riders=()):
    s = x.shape[0]
    tm = _row_tile(s, tm)

    def body(h_ref, w0, w1, w2, w3, wd_ref, x_ref, g_ref, t_ref, d_ref, a_ref, dx_ref, df_ref, dg_ref, loss_ref):
        first = pl.program_id(0) == 0
        hv = h_ref[...]
        fv = None
        for half, (wg_ref, wu_ref) in enumerate(((w0, w2), (w1, w3))):
            cols = slice(half * FF_HALF, (half + 1) * FF_HALF)
            g = jnp.dot(hv, wg_ref[...], preferred_element_type=F32)
            u = jnp.dot(hv, wu_ref[...], preferred_element_type=F32)
            sig = _sigmoid(g)
            silu = g * sig
            d_ref[0, :, cols] = (u * (sig + silu * (1.0 - sig))).astype(BF16)
            d_ref[1, :, cols] = silu.astype(BF16)
            act = (silu * u).astype(BF16)
            a_ref[:, cols] = act
            p = jnp.dot(act, wd_ref[cols, :], preferred_element_type=F32)
            fv = p if fv is None else fv + p
        gain = g_ref[...]
        err = x_ref[...] + fv * _rstd(fv) * gain - t_ref[...]
        dx = err * (1.0 / D_MODEL)
        dx_ref[...] = dx
        df, dg = _rms_bwd(dx, fv, gain)
        df_ref[...] = df.astype(BF16)
        _accum(dg_ref, dg, first)
        part = jnp.sum(jnp.sum(err * err, axis=-1, keepdims=True), axis=0, keepdims=True) * (0.5 / D_MODEL)
        _accum(loss_ref, jnp.broadcast_to(part, (8, LANES)), first)

    def resident(shape, index):
        return pl.BlockSpec(shape, index, pipeline_mode=pl.Buffered(1))

    row, vec = _row_spec(tm, D_MODEL), _vec_spec(D_MODEL)
    shards = [resident((None, D_MODEL, FF_HALF), (lambda j: (lambda i: (j, 0, 0)))(j)) for j in range(N_CHIPS)]
    core, rr = _call(
        body, grid=(s // tm,),
        in_specs=[row] + shards + [resident((D_FF, D_MODEL), lambda i: (0, 0)), row, vec, row],
        out_specs=[pl.BlockSpec((2, tm, D_FF), lambda i: (0, i, 0)), _row_spec(tm, D_FF), row, row, vec,
                   pl.BlockSpec((8, LANES), lambda i: (0, 0))],
        out_shape=[jax.ShapeDtypeStruct((2, s, D_FF), BF16), jax.ShapeDtypeStruct((s, D_FF), BF16),
                   jax.ShapeDtypeStruct((s, D_MODEL), F32), jax.ShapeDtypeStruct((s, D_MODEL), BF16),
                   jax.ShapeDtypeStruct((1, D_MODEL), F32), jax.ShapeDtypeStruct((8, LANES), F32)],
        operands=(h, w_gu, w_gu, w_gu, w_gu, w_d, x, g_post, target), name=name, riders=riders)
    return _ret(core, rr, riders)


def dh_norm_bwd_pair(a, w, dres, x, g_pre, m, g_post, *, name, tm=512, sub=256, riders=()):
    _, kout, ns = w.shape
    planes = a.ndim == 3
    s = x.shape[0]
    tm = _row_tile(s, tm)
    sub = min(sub, tm)
    a_spec = pl.BlockSpec((2, tm, 2 * ns), lambda i: (0, i, 0)) if planes else pl.BlockSpec((tm, N_CHIPS * ns), lambda i: (i, 0))

    def body(a_ref, w0, w1, w2, w3, dres_ref, x_ref, gpre_ref, m_ref, gpost_ref, dx_ref, dm_ref, dgpre_ref, dgpost_ref, db_ref):
        first = pl.program_id(0) == 0
        sums = None
        for t in range(tm // sub):
            rows = slice(t * sub, (t + 1) * sub)
            dh = None
            for j, w_ref in enumerate((w0, w1, w2, w3)):
                a_j = a_ref[j // 2, rows, (j % 2) * ns:(j % 2 + 1) * ns] if planes else a_ref[rows, j * ns:(j + 1) * ns]
                p = lax.dot_general(a_j, w_ref[...], NT_DIMS, preferred_element_type=F32)
                dh = p if dh is None else dh + p
            d1, dgpre = _rms_bwd(dh, x_ref[rows, :], gpre_ref[...])
            dx = dres_ref[rows, :] + d1
            dx_ref[rows, :] = dx
            dm, dgpost = _rms_bwd(dx, m_ref[rows, :].astype(F32), gpost_ref[...])
            dm_ref[rows, :] = dm.astype(BF16)
            part = (dgpre, dgpost, jnp.sum(dm, axis=0, keepdims=True))
            sums = part if sums is None else tuple(u + v for u, v in zip(sums, part))
        _accum(dgpre_ref, sums[0], first)
        _accum(dgpost_ref, sums[1], first)
        _accum(db_ref, sums[2], first)

    def shard(j):
        return pl.BlockSpec((None, kout, ns), lambda i: (j, 0, 0))

    row, vec = _row_spec(tm, D_MODEL), _vec_spec(D_MODEL)
    vshape = jax.ShapeDtypeStruct((1, D_MODEL), F32)
    core, rr = _call(
        body, grid=(s // tm,), in_specs=[a_spec] + [shard(j) for j in range(N_CHIPS)] + [row, row, vec, row, vec],
        out_specs=[row, row, vec, vec, vec],
        out_shape=[jax.ShapeDtypeStruct((s, D_MODEL), F32), jax.ShapeDtypeStruct((s, D_MODEL), BF16), vshape, vshape, vshape],
        operands=(a, w, w, w, w, dres, x, g_pre, m, g_post), name=name, riders=riders)
    return _ret(core, rr, riders)


def ffn_bwd_rows(df, w_d, d_planes, w_gu, dres, x, g_pre, m, g_post, *, name, tm=256, riders=()):
    s = x.shape[0]
    tm = _row_tile(s, tm)

    def body(df_ref, wd_ref, d_ref, w0, w1, w2, w3, dres_ref, x_ref, gpre_ref, m_ref, gpost_ref,
             o_ref, dx_ref, dm_ref, dgpre_ref, dgpost_ref, db_ref):
        first = pl.program_id(0) == 0
        dfv = df_ref[...]
        dh = None
        for half, (wg_ref, wu_ref) in enumerate(((w0, w2), (w1, w3))):
            cols = slice(half * FF_HALF, (half + 1) * FF_HALF)
            da = lax.dot_general(dfv, wd_ref[cols, :], NT_DIMS, preferred_element_type=F32)
            dg = (da * d_ref[0, :, cols].astype(F32)).astype(BF16)
            du = (da * d_ref[1, :, cols].astype(F32)).astype(BF16)
            o_ref[0, :, cols] = dg
            o_ref[1, :, cols] = du
            p = lax.dot_general(dg, wg_ref[...], NT_DIMS, preferred_element_type=F32)
            p += lax.dot_general(du, wu_ref[...], NT_DIMS, preferred_element_type=F32)
            dh = p if dh is None else dh + p
        d1, dgpre = _rms_bwd(dh, x_ref[...], gpre_ref[...])
        dx = dres_ref[...] + d1
        dx_ref[...] = dx
        dm, dgpost = _rms_bwd(dx, m_ref[...].astype(F32), gpost_ref[...])
        dm_ref[...] = dm.astype(BF16)
        _accum(dgpre_ref, dgpre, first)
        _accum(dgpost_ref, dgpost, first)
        _accum(db_ref, jnp.sum(dm, axis=0, keepdims=True), first)

    def resident(shape, index):
        return pl.BlockSpec(shape, index, pipeline_mode=pl.Buffered(1))

    planes = pl.BlockSpec((2, tm, D_FF), lambda i: (0, i, 0))
    row, vec = _row_spec(tm, D_MODEL), _vec_spec(D_MODEL)
    vshape = jax.ShapeDtypeStruct((1, D_MODEL), F32)
    shards = [resident((None, D_MODEL, FF_HALF), (lambda j: (lambda i: (j, 0, 0)))(j)) for j in range(N_CHIPS)]
    core, rr = _call(
        body, grid=(s // tm,),
        in_specs=[row, resident((D_FF, D_MODEL), lambda i: (0, 0)), planes] + shards + [row, row, vec, row, vec],
        out_specs=[planes, row, row, vec, vec, vec],
        out_shape=[jax.ShapeDtypeStruct((2, s, D_FF), BF16), jax.ShapeDtypeStruct((s, D_MODEL), F32),
                   jax.ShapeDtypeStruct((s, D_MODEL), BF16), vshape, vshape, vshape],
        operands=(df, w_d, d_planes, w_gu, w_gu, w_gu, w_gu, dres, x, g_pre, m, g_post), name=name, riders=riders)
    return _ret(core, rr, riders)


def dh_norm_bwd_last(a, w, dres, x, g_pre, *, name, tm=512, sub=256):
    _, kout, ns = w.shape
    s = x.shape[0]
    tm = _row_tile(s, tm)
    sub = min(sub, tm)

    def body(a_ref, w0, w1, w2, w3, dres_ref, x_ref, g_ref, dx_ref, dg_ref):
        total = None
        for t in range(tm // sub):
            rows = slice(t * sub, (t + 1) * sub)
            dh = None
            for j, w_ref in enumerate((w0, w1, w2, w3)):
                p = lax.dot_general(a_ref[rows, j * ns:(j + 1) * ns], w_ref[...], NT_DIMS, preferred_element_type=F32)
                dh = p if dh is None else dh + p
            d1, dg = _rms_bwd(dh, x_ref[rows, :], g_ref[...])
            dx_ref[rows, :] = dres_ref[rows, :] + d1
            total = dg if total is None else total + dg
        _accum(dg_ref, total, pl.program_id(0) == 0)

    def shard(j):
        return pl.BlockSpec((None, kout, ns), lambda i: (j, 0, 0))

    row, vec = _row_spec(tm, D_MODEL), _vec_spec(D_MODEL)
    (dx, dg), _ = _call(
        body, grid=(s // tm,), in_specs=[_row_spec(tm, N_CHIPS * ns)] + [shard(j) for j in range(N_CHIPS)] + [row, row, vec],
        out_specs=[row, vec], out_shape=[jax.ShapeDtypeStruct((s, D_MODEL), F32), jax.ShapeDtypeStruct((1, D_MODEL), F32)],
        operands=(a, w, w, w, w, dres, x, g_pre), name=name)
    return dx, dg


def _rope_tables(s):
    half = HEAD_DIM // 2
    inv_freq = np.float32(ROPE_THETA) ** (-(np.arange(half, dtype=np.float32) * np.float32(2.0)) / np.float32(HEAD_DIM))
    ang = np.arange(s, dtype=np.float32)[:, None] * inv_freq[None, :]
    cos, sin = np.cos(ang).astype(np.float32), np.sin(ang).astype(np.float32)
    return jnp.asarray(np.tile(cos, (1, 4))), jnp.asarray(np.concatenate([-sin, sin, -sin, sin], axis=1))


def _swap_halves(x):
    lane = lax.broadcasted_iota(I32, x.shape, 1)
    return jnp.where((lane & (HEAD_DIM - 1)) < HEAD_DIM // 2, pltpu.roll(x, LANES - 32, 1), pltpu.roll(x, 32, 1))


N_ROPE_BLOCKS = (Q_WIDTH + KV_WIDTH) // LANES


def qkv_proj(h, w, bias, cos, sin, *, name, tm=1024, riders=()):
    s, k = h.shape
    ns = w.shape[2]
    tm = _row_tile(s, tm)

    def body(h_ref, w_ref, b_ref, c_ref, s_ref, o_ref):
        j = pl.program_id(0)
        sub = min(256, tm)
        for t in range(tm // sub):
            rows = slice(t * sub, (t + 1) * sub)
            p = jnp.dot(h_ref[rows, :], w_ref[...], preferred_element_type=F32) + b_ref[...]
            cosv, sinv = c_ref[rows, :], s_ref[rows, :]
            for blk in range(ns // LANES):
                xb = p[:, blk * LANES:(blk + 1) * LANES]
                roped = xb * cosv + _swap_halves(xb) * sinv
                is_qk = j * (ns // LANES) + blk < N_ROPE_BLOCKS
                o_ref[rows, blk * LANES:(blk + 1) * LANES] = jnp.where(is_qk, roped, xb).astype(BF16)

    core, rr = _call(
        body, grid=(N_CHIPS, s // tm),
        in_specs=[pl.BlockSpec((tm, k), lambda j, i: (i, 0)), pl.BlockSpec((None, k, ns), lambda j, i: (j, 0, 0)),
                  pl.BlockSpec((1, ns), lambda j, i: (0, j)), pl.BlockSpec((tm, LANES), lambda j, i: (i, 0)),
                  pl.BlockSpec((tm, LANES), lambda j, i: (i, 0))],
        out_specs=[pl.BlockSpec((tm, ns), lambda j, i: (i, j))], out_shape=[jax.ShapeDtypeStruct((s, N_CHIPS * ns), BF16)],
        operands=(h, w, bias, cos, sin), sem=("parallel", "parallel"), name=name, riders=riders)
    return _ret(core, rr, riders)


def rope_bwd(dq, dkc, dkp, dvc, dvp, cos, sin, *, name, riders=()):
    s = dq.shape[0]
    tm = 2 * WINDOW if s % (2 * WINDOW) == 0 else WINDOW
    nb = s // tm

    def body(dq_ref, dkc_ref, dkp_ref, dkp_next_ref, dvc_ref, dvp_ref, dvp_next_ref, c_ref, s_ref, o_ref, db_ref):
        i = pl.program_id(0)
        has_next = (i < nb - 1).astype(F32)
        cosv, sinv = c_ref[...], s_ref[...]

        def shifted(ref, next_ref, cols):
            last = has_next * next_ref[:WINDOW, cols].astype(F32)
            return last if tm == WINDOW else jnp.concatenate([ref[WINDOW:, cols].astype(F32), last], axis=0)

        parts = []
        for blk in range(QKV_WIDTH // LANES):
            if blk < Q_WIDTH // LANES:
                g = dq_ref[:, blk * LANES:(blk + 1) * LANES].astype(F32)
            else:
                own, prv, nxt = (dkc_ref, dkp_ref, dkp_next_ref) if blk < N_ROPE_BLOCKS else (dvc_ref, dvp_ref, dvp_next_ref)
                cols = slice((blk % 2) * LANES, (blk % 2 + 1) * LANES)
                g = own[:, cols].astype(F32) + shifted(prv, nxt, cols)
            if blk < N_ROPE_BLOCKS:
                g = g * cosv + _swap_halves(g * sinv)
            o_ref[:, blk * LANES:(blk + 1) * LANES] = g.astype(BF16)
            parts.append(jnp.sum(g, axis=0, keepdims=True))
        sums = jnp.concatenate(parts, axis=1)
        _accum(db_ref, sums, i == 0)

    own_spec = _row_spec(tm, KV_WIDTH)
    next_spec = pl.BlockSpec((tm, KV_WIDTH), lambda i: (jnp.minimum(i + 1, nb - 1), 0))
    core, rr = _call(
        body, grid=(nb,),
        in_specs=[_row_spec(tm, Q_WIDTH), own_spec, own_spec, next_spec, own_spec, own_spec, next_spec,
                  _row_spec(tm, LANES), _row_spec(tm, LANES)],
        out_specs=[_row_spec(tm, QKV_WIDTH), _vec_spec(QKV_WIDTH)],
        out_shape=[jax.ShapeDtypeStruct((s, QKV_WIDTH), BF16), jax.ShapeDtypeStruct((1, QKV_WIDTH), F32)],
        operands=(dq, dkc, dkp, dkp, dvc, dvp, dvp, cos, sin), name=name, riders=riders)
    return _ret(core, rr, riders)


ROWS = GQA_GROUP * WINDOW


def _prev_slots():
    kpos = lax.broadcasted_iota(I32, (WINDOW, ROWS), 0)
    qpos = lax.broadcasted_iota(I32, (WINDOW, ROWS), 1) & (WINDOW - 1)
    return kpos > qpos


def _head_cols(ref, head):
    return ref[:, head * HEAD_DIM:(head + 1) * HEAD_DIM]


def _stack_heads(ref, h):
    return jnp.concatenate([_head_cols(ref, GQA_GROUP * h + g) for g in range(GQA_GROUP)], axis=0)


def _band(prev_ref, cur_ref, h):
    return jnp.concatenate([_head_cols(prev_ref, h), _head_cols(cur_ref, h)], axis=0)


def _pick(prev, band):
    return jnp.where(prev, band[:WINDOW], band[WINDOW:])


def _spread(prev, x):
    return jnp.concatenate([jnp.where(prev, x, 0.0), jnp.where(prev, 0.0, x)], axis=0).astype(BF16)


def _attn_probs(s_band, sink, prev, has_prev):
    scale = HEAD_DIM ** -0.5
    s = jnp.where(prev, jnp.where(has_prev, s_band[:WINDOW], NEG), s_band[WINDOW:]) * scale
    m = jnp.maximum(jnp.max(s, axis=0, keepdims=True), sink)
    e, es = jnp.exp(s - m), jnp.exp(sink - m)
    inv = 1.0 / (jnp.sum(e, axis=0, keepdims=True) + es)
    return e * inv, es * inv


def _attn_specs(nb):
    kcol, vcol = Q_WIDTH // KV_WIDTH, Q_WIDTH // KV_WIDTH + 1
    q_spec = pl.BlockSpec((WINDOW, Q_WIDTH), lambda n: (n, 0))
    return [q_spec,
            pl.BlockSpec((WINDOW, KV_WIDTH), lambda n: (n, kcol)),
            pl.BlockSpec((WINDOW, KV_WIDTH), lambda n: (jnp.maximum(n - 1, 0), kcol)),
            pl.BlockSpec((WINDOW, KV_WIDTH), lambda n: (n, vcol)),
            pl.BlockSpec((WINDOW, KV_WIDTH), lambda n: (jnp.maximum(n - 1, 0), vcol)),
            pl.BlockSpec((N_KV_HEADS, 8, ROWS), lambda n: (0, 0, 0))]


def attn_fwd(qkv, sink_rows, *, name, riders=()):
    s = qkv.shape[0]

    def body(q_ref, kc_ref, kp_ref, vc_ref, vp_ref, sink_ref, o_ref):
        prev = _prev_slots()
        has_prev = pl.program_id(0) > 0
        heads = range(N_KV_HEADS)
        s_bands = [lax.dot_general(_band(kp_ref, kc_ref, h), _stack_heads(q_ref, h), NT_DIMS, preferred_element_type=F32)
                   for h in heads]
        p_bands = [_spread(prev, _attn_probs(s_bands[h], sink_ref[h, 0:1, :], prev, has_prev)[0]) for h in heads]
        outs = [lax.dot_general(_band(vp_ref, vc_ref, h), p_bands[h], TN_DIMS, preferred_element_type=F32).T for h in heads]
        for h in heads:
            for g in range(GQA_GROUP):
                head = GQA_GROUP * h + g
                o_ref[:, head * HEAD_DIM:(head + 1) * HEAD_DIM] = outs[h][g * WINDOW:(g + 1) * WINDOW].astype(BF16)

    core, rr = _call(
        body, grid=(s // WINDOW,), in_specs=_attn_specs(s // WINDOW), out_specs=[pl.BlockSpec((WINDOW, Q_WIDTH), lambda n: (n, 0))],
        out_shape=[jax.ShapeDtypeStruct((s, Q_WIDTH), BF16)], operands=(qkv, qkv, qkv, qkv, qkv, sink_rows), sem=("parallel",),
        name=name, riders=riders)
    return _ret(core, rr, riders)


def attn_bwd(qkv, sink_rows, do, *, name, riders=()):
    s = qkv.shape[0]

    def body(q_ref, kc_ref, kp_ref, vc_ref, vp_ref, sink_ref, do_ref, dq_ref, dkc_ref, dkp_ref, dvc_ref, dvp_ref, dsink_ref):
        n = pl.program_id(0)
        prev = _prev_slots()
        scale = HEAD_DIM ** -0.5
        heads = range(N_KV_HEADS)
        qs, dos = [_stack_heads(q_ref, h) for h in heads], [_stack_heads(do_ref, h) for h in heads]
        kbands, vbands = [_band(kp_ref, kc_ref, h) for h in heads], [_band(vp_ref, vc_ref, h) for h in heads]
        s_bands = [lax.dot_general(kbands[h], qs[h], NT_DIMS, preferred_element_type=F32) for h in heads]
        dp_bands = [lax.dot_general(vbands[h], dos[h], NT_DIMS, preferred_element_type=F32) for h in heads]
        ds_bands, p_bands, parts = [], [], []
        for h in heads:
            p, ps = _attn_probs(s_bands[h], sink_ref[h, 0:1, :], prev, n > 0)
            dp = _pick(prev, dp_bands[h])
            delta = jnp.sum(p * dp, axis=0, keepdims=True)
            ds_bands.append(_spread(prev, p * (dp - delta) * scale))
            p_bands.append(_spread(prev, p))
            dsink = -(ps * delta)
            for g in range(GQA_GROUP):
                parts.append(jnp.broadcast_to(jnp.sum(dsink[:, g * WINDOW:(g + 1) * WINDOW], axis=1, keepdims=True), (8, LANES)))
        for h in heads:
            dk = jnp.dot(ds_bands[h], qs[h], preferred_element_type=F32).astype(BF16)
            dv = jnp.dot(p_bands[h], dos[h], preferred_element_type=F32).astype(BF16)
            dq = lax.dot_general(kbands[h], ds_bands[h], TN_DIMS, preferred_element_type=F32).T
            cols = slice(h * HEAD_DIM, (h + 1) * HEAD_DIM)
            dkp_ref[:, cols], dkc_ref[:, cols] = dk[:WINDOW], dk[WINDOW:]
            dvp_ref[:, cols], dvc_ref[:, cols] = dv[:WINDOW], dv[WINDOW:]
            for g in range(GQA_GROUP):
                head = GQA_GROUP * h + g
                dq_ref[:, head * HEAD_DIM:(head + 1) * HEAD_DIM] = dq[g * WINDOW:(g + 1) * WINDOW].astype(BF16)

        @pl.when(n == 0)
        def _():
            for i, part in enumerate(parts):
                dsink_ref[i // GQA_GROUP, i % GQA_GROUP] = part

        @pl.when(n > 0)
        def _():
            for i, part in enumerate(parts):
                dsink_ref[i // GQA_GROUP, i % GQA_GROUP] += part

    rows_q = pl.BlockSpec((WINDOW, Q_WIDTH), lambda n: (n, 0))
    rows_kv = pl.BlockSpec((WINDOW, KV_WIDTH), lambda n: (n, 0))
    kv_shape = jax.ShapeDtypeStruct((s, KV_WIDTH), BF16)
    core, rr = _call(
        body, grid=(s // WINDOW,), in_specs=_attn_specs(s // WINDOW) + [rows_q],
        out_specs=[rows_q, rows_kv, rows_kv, rows_kv, rows_kv,
                   pl.BlockSpec((N_KV_HEADS, GQA_GROUP, 8, LANES), lambda n: (0, 0, 0, 0))],
        out_shape=[jax.ShapeDtypeStruct((s, Q_WIDTH), BF16), kv_shape, kv_shape, kv_shape, kv_shape,
                   jax.ShapeDtypeStruct((N_KV_HEADS, GQA_GROUP, 8, LANES), F32)],
        operands=(qkv, qkv, qkv, qkv, qkv, sink_rows, do), sem=("arbitrary",), name=name, riders=riders)
    return _ret(core, rr, riders)


GELU_C = 0.7978845608028654
GELU_A = 0.044715


def _gelu(x):
    return 0.5 * x * (1.0 + jnp.tanh(x * (GELU_C + (GELU_C * GELU_A) * (x * x))))


def _gelu_and_grad(x):
    x2 = x * x
    t = jnp.tanh(x * (GELU_C + (GELU_C * GELU_A) * x2))
    half_x, one_t = 0.5 * x, 1.0 + t
    return half_x * one_t, 0.5 * one_t + half_x * (1.0 - t * t) * (GELU_C + (3.0 * GELU_C * GELU_A) * x2)


def _tril_bf16(w):
    row = lax.broadcasted_iota(I32, (SGU_CHUNK, SGU_CHUNK), 0)
    col = lax.broadcasted_iota(I32, (SGU_CHUNK, SGU_CHUNK), 1)
    return jnp.where(row >= col, w, 0.0).astype(BF16)


def _sgu_norm(vg, g, b):
    mu = jnp.mean(vg, axis=-1, keepdims=True)
    cen = vg - mu
    rstd = lax.rsqrt(jnp.mean(cen * cen, axis=-1, keepdims=True) + EPS)
    xhat = cen * rstd
    return xhat, rstd, xhat * g + b


def sgu_in_fwd(h, w_in, ln_g, ln_b, w_sp, b_sp, *, name, tm=512, riders=()):
    s, k = h.shape
    ns = w_in.shape[2]
    tm = _row_tile(s, tm)

    def body(h_ref, w0, w1, w2, w3, g_ref, b_ref, w_ref, bs_ref, z_ref, y_ref):
        hv = h_ref[...]
        zs = [jnp.dot(hv, w_ref_j[...], preferred_element_type=F32) for w_ref_j in (w0, w1, w2, w3)]
        for j, zj in enumerate(zs):
            z_ref[:, j * ns:(j + 1) * ns] = zj.astype(BF16)
        u = _gelu(jnp.concatenate(zs[:2], axis=1))
        _, _, vn = _sgu_norm(_gelu(jnp.concatenate(zs[2:], axis=1)), g_ref[...], b_ref[...])
        vn = vn.astype(BF16)
        for grp in range(SGU_GROUPS):
            w = _tril_bf16(w_ref[grp])
            cols = slice(grp * LANES, (grp + 1) * LANES)
            for ch in range(tm // SGU_CHUNK):
                rows = slice(ch * SGU_CHUNK, (ch + 1) * SGU_CHUNK)
                mixed = jnp.dot(w, vn[rows, cols], preferred_element_type=F32) + bs_ref[grp]
                y_ref[rows, cols] = (u[rows, cols] * mixed).astype(BF16)

    def shard(j):
        return pl.BlockSpec((None, k, ns), lambda i: (j, 0, 0))

    full3 = pl.BlockSpec((SGU_GROUPS, SGU_CHUNK, SGU_CHUNK), lambda i: (0, 0, 0))
    core, rr = _call(
        body, grid=(s // tm,),
        in_specs=[_row_spec(tm, k)] + [shard(j) for j in range(N_CHIPS)] + [_vec_spec(D_MODEL), _vec_spec(D_MODEL), full3, full3],
        out_specs=[_row_spec(tm, 2 * D_MODEL), _row_spec(tm, D_MODEL)],
        out_shape=[jax.ShapeDtypeStruct((s, 2 * D_MODEL), BF16), jax.ShapeDtypeStruct((s, D_MODEL), BF16)],
        operands=(h, w_in, w_in, w_in, w_in, ln_g, ln_b, w_sp, b_sp), sem=("parallel",), name=name, riders=riders)
    return _ret(core, rr, riders)


def sgu_bwd(z, dy, ln_g, ln_b, w_sp, b_sp, *, name, tm=256, riders=()):
    s = z.shape[0]
    tm = _row_tile(s, tm)

    def body(z_ref, dy_ref, g_ref, b_ref, w_ref, bs_ref, dz_ref, dw_ref, dbs_ref, dg_ref, db_ref, dvn_buf):
        first = pl.program_id(0) == 0
        u, u_grad = _gelu_and_grad(z_ref[:, :D_MODEL].astype(F32))
        vg, v_grad = _gelu_and_grad(z_ref[:, D_MODEL:].astype(F32))
        xhat, rstd, vn = _sgu_norm(vg, g_ref[...], b_ref[...])
        vn = vn.astype(BF16)
        dyv = dy_ref[...]
        dmixed = dyv * u
        dz_gate = dyv * u_grad
        row = lax.broadcasted_iota(I32, (SGU_CHUNK, SGU_CHUNK), 0)
        col = lax.broadcasted_iota(I32, (SGU_CHUNK, SGU_CHUNK), 1)
        dws, dbss = [], []
        for grp in range(SGU_GROUPS):
            w = _tril_bf16(w_ref[grp])
            cols = slice(grp * LANES, (grp + 1) * LANES)
            dw = jnp.zeros((SGU_CHUNK, SGU_CHUNK), F32)
            dbs = jnp.zeros((SGU_CHUNK, 1), F32)
            for ch in range(tm // SGU_CHUNK):
                rows = slice(ch * SGU_CHUNK, (ch + 1) * SGU_CHUNK)
                vblk = vn[rows, cols]
                mixed = jnp.dot(w, vblk, preferred_element_type=F32) + bs_ref[grp]
                dz_ref[rows, cols] = (dz_gate[rows, cols] * mixed).astype(BF16)
                dm = dmixed[rows, cols]
                dmb = dm.astype(BF16)
                dvn_buf[rows, cols] = lax.dot_general(w, dmb, TN_DIMS, preferred_element_type=F32)
                dw += lax.dot_general(dmb, vblk, NT_DIMS, preferred_element_type=F32)
                dbs += jnp.sum(dm, axis=-1, keepdims=True)
            dws.append(jnp.where(row >= col, dw, 0.0))
            dbss.append(jnp.broadcast_to(dbs, (SGU_CHUNK, SGU_CHUNK)))

        dvn = dvn_buf[...]
        dxhat = dvn * g_ref[...]
        dvg = rstd * (dxhat - jnp.mean(dxhat, axis=-1, keepdims=True) - xhat * jnp.mean(dxhat * xhat, axis=-1, keepdims=True))
        dz_ref[:, D_MODEL:] = (dvg * v_grad).astype(BF16)
        dlng, dlnb = jnp.sum(dvn * xhat, axis=0, keepdims=True), jnp.sum(dvn, axis=0, keepdims=True)

        @pl.when(first)
        def _():
            for grp in range(SGU_GROUPS):
                dw_ref[grp] = dws[grp]
                dbs_ref[grp] = dbss[grp]
            dg_ref[...] = dlng
            db_ref[...] = dlnb

        @pl.when(jnp.logical_not(first))
        def _():
            for grp in range(SGU_GROUPS):
                dw_ref[grp] += dws[grp]
                dbs_ref[grp] += dbss[grp]
            dg_ref[...] += dlng
            db_ref[...] += dlnb

    full3 = pl.BlockSpec((SGU_GROUPS, SGU_CHUNK, SGU_CHUNK), lambda i: (0, 0, 0))
    s3 = jax.ShapeDtypeStruct((SGU_GROUPS, SGU_CHUNK, SGU_CHUNK), F32)
    vshape = jax.ShapeDtypeStruct((1, D_MODEL), F32)
    core, rr = _call(
        body, grid=(s // tm,),
        in_specs=[_row_spec(tm, 2 * D_MODEL), _row_spec(tm, D_MODEL), _vec_spec(D_MODEL), _vec_spec(D_MODEL), full3, full3],
        out_specs=[_row_spec(tm, 2 * D_MODEL), full3, full3, _vec_spec(D_MODEL), _vec_spec(D_MODEL)],
        out_shape=[jax.ShapeDtypeStruct((s, 2 * D_MODEL), BF16), s3, s3, vshape, vshape],
        scratch_shapes=[pltpu.VMEM((tm, D_MODEL), F32)], operands=(z, dy, ln_g, ln_b, w_sp, b_sp), name=name, riders=riders)
    return _ret(core, rr, riders)


def _sigmoid(x):
    return 1.0 / (1.0 + jnp.exp(-x))


def ffn_up(h, w_gu, *, name, tm=512, riders=()):
    s = h.shape[0]
    tm = _row_tile(s, tm)

    def body(h_ref, wg_ref, wu_ref, d_ref, a_ref):
        hv = h_ref[...]
        sub = min(256, tm)
        for t in range(tm // sub):
            rows = slice(t * sub, (t + 1) * sub)
            g = jnp.dot(hv[rows], wg_ref[...], preferred_element_type=F32)
            u = jnp.dot(hv[rows], wu_ref[...], preferred_element_type=F32)
            sig = _sigmoid(g)
            silu = g * sig
            d_ref[0, rows, :] = (u * (sig + silu * (1.0 - sig))).astype(BF16)
            d_ref[1, rows, :] = silu.astype(BF16)
            a_ref[rows, :] = (silu * u).astype(BF16)

    core, rr = _call(
        body, grid=(2, s // tm),
        in_specs=[pl.BlockSpec((tm, D_MODEL), lambda j, i: (i, 0)),
                  pl.BlockSpec((None, D_MODEL, FF_HALF), lambda j, i: (j, 0, 0)),
                  pl.BlockSpec((None, D_MODEL, FF_HALF), lambda j, i: (j + 2, 0, 0))],
        out_specs=[pl.BlockSpec((2, tm, FF_HALF), lambda j, i: (0, i, j)), pl.BlockSpec((tm, FF_HALF), lambda j, i: (i, j))],
        out_shape=[jax.ShapeDtypeStruct((2, s, D_FF), BF16), jax.ShapeDtypeStruct((s, D_FF), BF16)],
        operands=(h, w_gu, w_gu), sem=("parallel", "parallel"), name=name, riders=riders)
    return _ret(core, rr, riders)


def _weight_tile(rows):
    for tr in (512, 352, 256, 128):
        if rows % tr == 0:
            return tr
    return rows


def place_shard(w, layer, chip_arr, dtype, *, name, riders=()):
    _, r, c = w.shape
    tr = _weight_tile(r)

    def body(chip_ref, w_ref, o_ref):
        o_ref[...] = w_ref[...].astype(dtype)

    core, rr = _call(
        body, grid=(r // tr,), prefetch=(chip_arr,),
        in_specs=[pl.BlockSpec((None, tr, c), lambda i, chip: (layer, i, 0))],
        out_specs=[pl.BlockSpec((None, tr, c), lambda i, chip: (chip[0], i, 0))],
        out_shape=[jax.ShapeDtypeStruct((N_CHIPS, r, c), dtype)], operands=(w,), sem=("parallel",), name=name, riders=riders)
    return _ret(core, rr, riders)


def _adamw_math(w, g, m, v):
    m = ADAM_B1 * m + (1.0 - ADAM_B1) * g
    v = ADAM_B2 * v + (1.0 - ADAM_B2) * (g * g)
    m_hat = m / (1.0 - ADAM_B1 ** ADAM_STEP)
    v_hat = v / (1.0 - ADAM_B2 ** ADAM_STEP)
    delta = -ADAM_LR * (m_hat / (jnp.sqrt(v_hat) + ADAM_EPS) + ADAM_WD * w)
    return delta, m, v


def adamw(w, g, m, v, *, name, after=None):
    nl, r, c = w.shape
    tr = _weight_tile(r)

    def body(w_ref, g_ref, m_ref, v_ref, *rest):
        go_ref, d_ref, mo_ref, vo_ref = rest[-4:]
        gv = g_ref[...]
        go_ref[...] = gv
        d_ref[...], mo_ref[...], vo_ref[...] = _adamw_math(w_ref[...], gv, m_ref[...], v_ref[...])

    spec = pl.BlockSpec((None, tr, c), lambda l, i: (l, i, 0))
    shape = jax.ShapeDtypeStruct(w.shape, F32)
    extra = [] if after is None else [after]
    outs, _ = _call(body, grid=(nl, r // tr), in_specs=[spec] * 4 + [ANY] * len(extra), out_specs=[spec] * 4,
                    out_shape=[shape] * 4, operands=(w, g, m, v, *extra), sem=("parallel", "parallel"), name=name)
    return outs


def adamw_small(ws, gs, ms, vs, *, name):
    n = len(ws)

    def body(*refs):
        ins, outs = refs[:4 * n], refs[4 * n:]
        for t in range(n):
            gv = ins[n + t][...]
            outs[t][...] = gv
            outs[n + t][...], outs[2 * n + t][...], outs[3 * n + t][...] = _adamw_math(
                ins[t][...], gv, ins[2 * n + t][...], ins[3 * n + t][...])

    shapes = [jax.ShapeDtypeStruct(w.shape, F32) for w in ws]
    res = pl.pallas_call(body, out_shape=shapes * 4, name=name)(*ws, *gs, *ms, *vs)
    return res[:n], res[n:2 * n], res[2 * n:3 * n], res[3 * n:]


def pair_add(g, r1, c_arr, *, name):
    _, rows, cdim = g.shape
    h = rows // 2

    def body(c_ref, g_ref, r_ref, o_ref):
        o_ref[...] = (g_ref[...].astype(F32) + r_ref[...].astype(F32)).astype(o_ref.dtype)

    (out,), _ = _call(
        body, grid=(N_CHIPS,), prefetch=(c_arr,),
        in_specs=[pl.BlockSpec((None, h, cdim), lambda s, c: (s, c[0], 0)), pl.BlockSpec((None, h, cdim), lambda s, c: (s, 0, 0))],
        out_specs=[pl.BlockSpec((None, h, cdim), lambda s, c: (s, 0, 0))],
        out_shape=[jax.ShapeDtypeStruct((N_CHIPS, h, cdim), g.dtype)], operands=(g, r1), sem=("parallel",), name=name)
    return out


def final_add(g, r1, r2, jc_arr, *, dest_shape, lead, prev, name):
    _, rows, cdim = g.shape
    h = rows // 2

    def body(jc_ref, g_ref, r1_ref, r2_ref, *rest):
        o_ref = rest[-1]
        acc = g_ref[...].astype(F32) + r1_ref[...].astype(F32)
        for k in range(3):
            acc = acc + r2_ref[k].astype(F32)
        o_ref[...] = acc

    if lead is None:
        o_spec = pl.BlockSpec((h, cdim), lambda i, jc: (jc[1], 0))
    elif lead == "chip":
        o_spec = pl.BlockSpec((None, h, cdim), lambda i, jc: (jc[0], jc[1], 0))
    else:
        o_spec = pl.BlockSpec((None, h, cdim), lambda i, jc: (lead, jc[1], 0))
    in_specs = [pl.BlockSpec((None, h, cdim), lambda i, jc: (jc[0], jc[1], 0)),
                pl.BlockSpec((None, h, cdim), lambda i, jc: (jc[0], 0, 0)),
                pl.BlockSpec((3, h, cdim), lambda i, jc: (0, 0, 0))]
    operands = [g, r1, r2]
    aliases = None
    if prev is not None:
        in_specs.append(ANY)
        operands.append(prev)
        aliases = {3: 0}
    (out,), _ = _call(body, grid=(1,), prefetch=(jc_arr,), in_specs=in_specs, out_specs=[o_spec],
                      out_shape=[jax.ShapeDtypeStruct(dest_shape, F32)], operands=operands, aliases=aliases, name=name)
    return out


def _place():
    return lax.axis_index("x"), lax.axis_index("y"), lax.axis_index("c")


def _partner(x, y, k):
    return (1 - x if k >> 1 else x), (1 - y if k & 1 else y)


WHOLE = (0, 1, 1)


def _half(rows, sel, dtype, piece=WHOLE):
    lo, hi, n = piece
    align = 16 if dtype == BF16 else 8
    step = rows // 2 // n
    assert rows // 2 == step * n and step % align == 0
    return pl.ds(pl.multiple_of(sel * (rows // 2) + lo * step, align), (hi - lo) * step)


def _rider(peers, inputs, aliased, fresh, nsem, copies, arrivals):
    def start(ins, outs, send, recv):
        for cp in copies(ins, outs, send, recv):
            cp.start()

    def finish(ins, outs, send, recv):
        for cp in arrivals(ins, outs, send, recv):
            cp.wait_recv()
        for cp in copies(ins, outs, send, recv):
            cp.wait_send()

    return types.SimpleNamespace(peers=peers, inputs=list(inputs), aliased=list(aliased), fresh=list(fresh), nsem=nsem,
                                 start=start, finish=finish)


def _remote(src, dst, send, recv, idx, dev):
    return pltpu.make_async_remote_copy(src_ref=src, dst_ref=dst, send_sem=send.at[idx], recv_sem=recv.at[idx],
                                        device_id=dev, device_id_type=MESH)


def gather_ici_rider(fulls, pieces=None):
    nt = len(fulls)
    pieces = pieces or [WHOLE] * nt

    def region(outs, t, slot, sel):
        return outs[t].at[slot, _half(fulls[t].shape[1], sel, fulls[t].dtype, pieces[t])]

    def copies(ins, outs, send, recv):
        x, y, c = _place()
        res = []
        for t in range(nt):
            for k in (1, 2, 3):
                px, py = _partner(x, y, k)
                mine = region(outs, t, 2 * x + y, c)
                res.append(_remote(mine, mine, send, recv, 3 * t + k - 1, (px, py, c)))
        return res

    def arrivals(ins, outs, send, recv):
        x, y, c = _place()
        res = []
        for t in range(nt):
            for k in (1, 2, 3):
                px, py = _partner(x, y, k)
                theirs = region(outs, t, 2 * px + py, c)
                res.append(_remote(theirs, theirs, send, recv, 3 * t + k - 1, (x, y, c)))
        return res

    return _rider("chips", fulls, range(nt), [], 3 * nt, copies, arrivals)


def gather_d2d_rider(fulls, pieces=None):
    nt = len(fulls)
    pieces = pieces or [WHOLE] * nt

    def region(outs, t, slot, sel):
        return outs[t].at[slot, _half(fulls[t].shape[1], sel, fulls[t].dtype, pieces[t])]

    def both(outs, send, recv, mine):
        x, y, c = _place()
        res = []
        for t in range(nt):
            for k in (1, 2, 3):
                px, py = _partner(x, y, k)
                part = region(outs, t, 2 * px + py, c if mine else 1 - c)
                res.append(_remote(part, part, send, recv, 3 * t + k - 1, (x, y, 1 - c)))
        return res

    return _rider("sibling", fulls, range(nt), [], 3 * nt, lambda i, o, s, r: both(o, s, r, True),
                  lambda i, o, s, r: both(o, s, r, False))


def exchange_rider(grads):
    nt = len(grads)

    def both(ins, outs, send, recv):
        x, y, c = _place()
        return [_remote(ins[t].at[:, _half(grads[t].shape[1], 1 - c, grads[t].dtype)], outs[t], send, recv, t, (x, y, 1 - c))
                for t in range(nt)]

    fresh = [jax.ShapeDtypeStruct((N_CHIPS, g.shape[1] // 2, g.shape[2]), g.dtype) for g in grads]
    return _rider("sibling", grads, [], fresh, nt, both, both)


def scatter_rider(parts):
    nt = len(parts)

    def both(ins, outs, send, recv):
        x, y, c = _place()
        res = []
        for t in range(nt):
            for k in (1, 2, 3):
                px, py = _partner(x, y, k)
                res.append(_remote(ins[t].at[2 * px + py], outs[t].at[k - 1], send, recv, 3 * t + k - 1, (px, py, c)))
        return res

    fresh = [jax.ShapeDtypeStruct((3,) + p.shape[1:], p.dtype) for p in parts]
    return _rider("chips", parts, [], fresh, 3 * nt, both, both)


def broadcast_rider(bufs, items):
    def region(outs, item, sel):
        bi, lead = item
        ref = outs[bi]
        if lead == "chip":
            x, y, _ = _place()
            ref = ref.at[2 * x + y]
        elif lead is not None:
            ref = ref.at[lead]
        return ref.at[_half(ref.shape[0], sel, F32)]

    def both(outs, send, recv, mine):
        x, y, c = _place()
        res = []
        for i, item in enumerate(items):
            part = region(outs, item, c if mine else 1 - c)
            res.append(_remote(part, part, send, recv, i, (x, y, 1 - c)))
        return res

    return _rider("sibling", bufs, range(len(bufs)), [], len(items), lambda i, o, s, r: both(o, s, r, True),
                  lambda i, o, s, r: both(o, s, r, False))


def allcast_rider(buf):
    peers = [(k, flip) for k in range(N_CHIPS) for flip in (0, 1) if (k, flip) != (0, 0)]

    def both(outs, send, recv, mine):
        x, y, c = _place()
        res = []
        for i, (k, flip) in enumerate(peers):
            px, py = _partner(x, y, k)
            pc = 1 - c if flip else c
            slot, sel = (2 * x + y, c) if mine else (2 * px + py, pc)
            part = outs[0].at[slot, _half(buf.shape[1], sel, F32)]
            res.append(_remote(part, part, send, recv, i, (px, py, pc)))
        return res

    return _rider("everyone", [buf], [0], [], len(peers), lambda i, o, s, r: both(o, s, r, True),
                  lambda i, o, s, r: both(o, s, r, False))


def comm_call(riders, *, name):
    _, res = _call(None, riders=riders, name=name)
    return res


SEMS = pl.BlockSpec(memory_space=pltpu.SEMAPHORE)
SIDE_EFFECT = pltpu.SideEffectType.DATAFLOW_SIDE_EFFECTING


def _split_refs(riders, refs):
    views, p = [], 0
    for r in riders:
        bufs = refs[p:p + len(r.inputs) + len(r.fresh)]
        p += len(bufs)
        ins = bufs[:len(r.inputs)]
        views.append([ins, [ins[i] for i in r.aliased] + list(bufs[len(r.inputs):])])
    for view in views:
        view += [refs[p], refs[p + 1]]
        p += 2
    return views


def comm_start(riders, *, name):
    kind = _peer_kind(riders)
    bufs = [a for r in riders for a in r.inputs]
    fresh = [f for r in riders for f in r.fresh]
    n_buf, n_fresh = len(bufs), len(fresh)

    def body(*refs):
        ins, outs = refs[:n_buf], refs[n_buf:]
        through, land, sems = outs[:n_buf], outs[n_buf:n_buf + n_fresh], outs[n_buf + n_fresh:-1]
        _peer_barrier(kind)
        per_rider, pb, pf = [], 0, 0
        for r in riders:
            per_rider += list(through[pb:pb + len(r.inputs)]) + list(land[pf:pf + len(r.fresh)])
            pb, pf = pb + len(r.inputs), pf + len(r.fresh)
        for r, (r_ins, r_outs, send, recv) in zip(riders, _split_refs(riders, per_rider + list(sems))):
            r.start(r_ins, r_outs, send, recv)
        outs[-1][...] = jnp.zeros((8, LANES), F32)

    sem_shapes = [pltpu.SemaphoreType.DMA((r.nsem,)) for r in riders for _ in (0, 1)]
    res = pl.pallas_call(
        body, name=name, in_specs=[ANY] * n_buf,
        out_specs=[ANY] * (n_buf + n_fresh) + [SEMS] * len(sem_shapes) + [pl.BlockSpec(memory_space=pltpu.VMEM)],
        out_shape=[jax.ShapeDtypeStruct(a.shape, a.dtype) for a in bufs] + fresh + sem_shapes
        + [jax.ShapeDtypeStruct((8, LANES), F32)],
        input_output_aliases={i: i for i in range(n_buf)},
        compiler_params=pltpu.CompilerParams(has_side_effects=SIDE_EFFECT, collective_id=PEER_KINDS.index(kind)))(*bufs)
    return (riders, list(res[:n_buf + n_fresh]), list(res[n_buf + n_fresh:-1])), res[-1]


def comm_wait(state, after, *, name):
    riders, bufs, sems = state
    n_buf, n_sem = len(bufs), len(sems)
    n_in = sum(len(r.inputs) for r in riders)

    def body(*refs):
        held, sem_refs = refs[:n_buf], refs[n_buf:n_buf + n_sem]
        through, land = held[:n_in], held[n_in:]
        per_rider, pb, pf = [], 0, 0
        for r in riders:
            per_rider += list(through[pb:pb + len(r.inputs)]) + list(land[pf:pf + len(r.fresh)])
            pb, pf = pb + len(r.inputs), pf + len(r.fresh)
        for r, (r_ins, r_outs, send, recv) in zip(riders, _split_refs(riders, per_rider + list(sem_refs))):
            r.finish(r_ins, r_outs, send, recv)

    res = pl.pallas_call(
        body, name=name, in_specs=[ANY] * n_buf + [SEMS] * n_sem + [ANY], out_specs=[ANY] * n_buf,
        out_shape=[jax.ShapeDtypeStruct(a.shape, a.dtype) for a in bufs],
        input_output_aliases={i: i for i in range(n_buf)},
        compiler_params=pltpu.CompilerParams(has_side_effects=SIDE_EFFECT))(*bufs, *sems, after)
    through, land = list(res[:n_in]), list(res[n_in:])
    out, pb, pf = [], 0, 0
    for r in riders:
        r_ins, r_land = through[pb:pb + len(r.inputs)], land[pf:pf + len(r.fresh)]
        pb, pf = pb + len(r.inputs), pf + len(r.fresh)
        out.append([r_ins[i] for i in r.aliased] + r_land)
    return out


SLAB_ROWS = 192


def _pad_rows(a, rows=8):
    return jnp.pad(a, ((0, rows - a.shape[0]), (0, 0)))


def _pack_small(norm_grads, db_qkv, db_o, dsinks, db_sp, dln_g, dln_b, dw_sp, loss_part):
    parts = [
        jnp.concatenate(norm_grads, axis=0),
        _pad_rows(jnp.pad(db_qkv, ((0, 0), (0, 2 * D_MODEL - QKV_WIDTH))).reshape(2, D_MODEL)),
        _pad_rows(db_o),
        _pad_rows(jnp.pad(dsinks.reshape(1, N_Q_HEADS), ((0, 0), (0, D_MODEL - N_Q_HEADS)))),
        _pad_rows(db_sp.reshape(1, D_MODEL)),
        _pad_rows(jnp.concatenate([dln_g, dln_b, jnp.pad(loss_part[0:1], ((0, 0), (0, D_MODEL - LANES)))], axis=0)),
        dw_sp.reshape(SGU_CHUNK, D_MODEL),
    ]
    slab = jnp.concatenate(parts, axis=0)
    return jnp.pad(slab, ((0, SLAB_ROWS - slab.shape[0]), (0, 0))).reshape(N_CHIPS, SLAB_ROWS // N_CHIPS, D_MODEL)


def _unpack_small(slab, j):
    slab = slab.reshape(SLAB_ROWS, D_MODEL)
    norms = [slab[2 * i:2 * i + 2] for i in range(4)]
    db_qkv = slab[8:10].reshape(1, 2 * D_MODEL)[:, :QKV_WIDTH]
    db_o = slab[16:17]
    dsinks = slab[24:25, :N_Q_HEADS]
    db_sp = slab[32:33].reshape(SGU_GROUPS, SGU_CHUNK)
    width = D_MODEL // N_CHIPS
    dln_g = lax.dynamic_slice(slab[40:41], (0, j * width), (1, width))
    dln_b = lax.dynamic_slice(slab[41:42], (0, j * width), (1, width))
    dw_sp = slab[48:48 + SGU_CHUNK].reshape(SGU_GROUPS * SGU_CHUNK, SGU_CHUNK)
    return norms, db_qkv, db_o, dsinks, db_sp, dln_g, dln_b, dw_sp, slab[42, 0]


class _GradReduce:
    def __init__(self, c_arr, jc_arr, dest_shapes):
        self.c_arr, self.jc_arr, self.dest_shapes = c_arr, jc_arr, dest_shapes
        self.grad, self.sibling, self.pair, self.chips, self.dest = {}, {}, {}, {}, {}

    def exchange(self, tags):
        return exchange_rider([self.grad[t] for t in tags])

    def exchanged(self, tags, res):
        for t, r in zip(tags, res):
            self.sibling[t] = r
            self.pair[t] = pair_add(self.grad[t], r, self.c_arr, name=f"pair_add_{t}")

    def scatter(self, tags):
        return scatter_rider([self.pair[t] for t in tags])

    def scattered(self, tags, res, where):
        for t, r in zip(tags, res):
            name, lead = where[t]
            self.dest[name] = final_add(self.grad[t], self.sibling[t], r, self.jc_arr, dest_shape=self.dest_shapes[name],
                                        lead=lead, prev=self.dest.get(name), name=f"final_add_{t}")

    def broadcast(self, items):
        names = []
        for n, _ in items:
            if n not in names:
                names.append(n)
        return names, broadcast_rider([self.dest[n] for n in names], [(names.index(n), lead) for n, lead in items])

    def broadcasted(self, names, res):
        for n, r in zip(names, res):
            self.dest[n] = r


def kernel(x, norm_mix_pre, norm_mix_post, norm_ffn_pre, norm_ffn_post, attn_w_qkv, attn_b_qkv, attn_sinks, attn_w_o, attn_b_o, sgu_w_in, sgu_ln_g, sgu_ln_b, sgu_w_spatial, sgu_b_spatial, sgu_w_out, ffn_w_gate_up, ffn_w_down, loss_target, m_norm_mix_pre, m_norm_mix_post, m_norm_ffn_pre, m_norm_ffn_post, m_attn_w_qkv, m_attn_b_qkv, m_attn_sinks, m_attn_w_o, m_attn_b_o, m_sgu_w_in, m_sgu_ln_g, m_sgu_ln_b, m_sgu_w_spatial, m_sgu_b_spatial, m_sgu_w_out, m_ffn_w_gate_up, m_ffn_w_down, v_norm_mix_pre, v_norm_mix_post, v_norm_ffn_pre, v_norm_ffn_post, v_attn_w_qkv, v_attn_b_qkv, v_attn_sinks, v_attn_w_o, v_attn_b_o, v_sgu_w_in, v_sgu_ln_g, v_sgu_ln_b, v_sgu_w_spatial, v_sgu_b_spatial, v_sgu_w_out, v_ffn_w_gate_up, v_ffn_w_down):
    s = x.shape[1]
    x0 = x.reshape(s, D_MODEL)
    target = loss_target.reshape(s, D_MODEL)
    mx, my, mc = lax.axis_index("x"), lax.axis_index("y"), lax.axis_index("c")
    chip = 2 * mx + my
    chip_arr = jnp.reshape(chip, (1,)).astype(I32)
    c_arr = jnp.reshape(mc, (1,)).astype(I32)
    jc_arr = jnp.stack([chip, mc]).astype(I32)
    zero_bias = jnp.zeros((1, D_MODEL), F32)

    def gain(p, i):
        return p[i:i + 1]

    big = [attn_w_qkv, attn_w_o, sgu_w_in, sgu_w_out, ffn_w_gate_up, ffn_w_gate_up, ffn_w_down, ffn_w_down]
    layers = [0, 0, 0, 0, 0, 1, 0, 1]
    tags = ["qkv", "wo", "win", "wout", "wgu0", "wgu1", "wd0", "wd1"]
    full = {t: place_shard(w, l, chip_arr, BF16, name=f"place_{t}") for w, l, t in zip(big, layers, tags) if t != "wgu1"}
    ln_pack = _pad_rows(jnp.concatenate([sgu_ln_g, sgu_ln_b], axis=0), 16)[None]
    full["ln"] = place_shard(ln_pack, 0, chip_arr, F32, name="place_ln")

    def split(items):
        return [i if isinstance(i, str) else i[0] for i in items], [WHOLE if isinstance(i, str) else tuple(i[1:]) for i in items]

    def ici(*items):
        names, pieces = split(items)
        return gather_ici_rider([full[n] for n in names], pieces)

    def d2d(*items):
        names, pieces = split(items)
        return gather_d2d_rider([full[n] for n in names], pieces)

    def landed(items, res):
        for n, r in zip(split(items)[0], res):
            full[n] = r

    cos, sin = _rope_tables(s)
    sink_rows = jnp.broadcast_to(
        jnp.repeat(attn_sinks.reshape(N_KV_HEADS, GQA_GROUP), WINDOW, axis=1)[:, None, :], (N_KV_HEADS, 8, ROWS))
    w_sp = sgu_w_spatial.reshape(SGU_GROUPS, SGU_CHUNK, SGU_CHUNK)
    b_sp = jnp.broadcast_to(sgu_b_spatial.reshape(SGU_GROUPS, SGU_CHUNK)[:, :, None], (SGU_GROUPS, SGU_CHUNK, LANES))

    h0, (res,) = prenorm(x0, gain(norm_mix_pre, 0), name="prenorm_0", riders=[ici("qkv", "ln")])
    landed(("qkv", "ln"), res)
    full["wgu1"], (res,) = place_shard(ffn_w_gate_up, 1, chip_arr, BF16, name="place_wgu1", riders=[d2d("qkv", "ln")])
    landed(("qkv", "ln"), res)
    ln_g = full["ln"][:, 0, :].reshape(1, D_MODEL)
    ln_b = full["ln"][:, 1, :].reshape(1, D_MODEL)

    def hosted(call, stages):
        outputs, results = call([{"ici": ici, "d2d": d2d}[kind](*items) for kind, items in stages])
        for (_, items), res in zip(stages, results):
            landed(items, res)
        return outputs

    qkv = hosted(lambda r: qkv_proj(h0, full["qkv"], attn_b_qkv, cos, sin, name="qkv_proj", riders=r),
                 [("ici", ("wo", ("wgu0", 0, 3, 8)))])
    o = hosted(lambda r: attn_fwd(qkv, sink_rows, name="attn_fwd", riders=r),
               [("d2d", ("wo",)), ("ici", (("wgu0", 3, 8, 8), ("wd0", 0, 2, 11)))])
    w_o = full["wo"].reshape(Q_WIDTH, D_MODEL)
    x1, h1, m0 = hosted(lambda r: proj_residual_norm(o, w_o, x0, attn_b_o, gain(norm_mix_post, 0), gain(norm_ffn_pre, 0),
                                                     name="attn_out_norm", riders=r),
                        [("d2d", ("wgu0",)), ("ici", (("wd0", 2, 11, 11),))])
    gu0, a0 = hosted(lambda r: ffn_up(h1, full["wgu0"], name="ffn_up_0", riders=r),
                     [("d2d", ("wd0",)), ("ici", ("win", "wout", ("wgu1", 0, 4, 8)))])
    w_d0 = full["wd0"].reshape(D_FF, D_MODEL)
    x2, h2, f0 = hosted(lambda r: proj_residual_norm(a0, w_d0, x1, zero_bias, gain(norm_ffn_post, 0), gain(norm_mix_pre, 1),
                                                     name="ffn_down_norm_0", riders=r),
                        [("d2d", ("win", "wout")), ("ici", (("wgu1", 4, 8, 8),))])
    w_in = full["win"]
    z, y = hosted(lambda r: sgu_in_fwd(h2, w_in, ln_g, ln_b, w_sp, b_sp, name="sgu_in_fwd", riders=r),
                  [("d2d", ("wgu1",)), ("ici", ("wd1",))])
    w_out = full["wout"].reshape(D_MODEL, D_MODEL)
    x3, h3, m1 = hosted(lambda r: proj_residual_norm(y, w_out, x2, zero_bias, gain(norm_mix_post, 1), gain(norm_ffn_pre, 1),
                                                     name="sgu_out_norm", riders=r),
                        [("d2d", ("wd1",))])
    w_qkv, w_gu0, w_gu1 = full["qkv"], full["wgu0"], full["wgu1"]
    w_d1 = full["wd1"].reshape(D_FF, D_MODEL)
    gu1, a1, dx4, df1, dg_fpost1, loss_part = ffn_fwd_loss_rows(
        h3, w_gu1, w_d1, x3, gain(norm_ffn_post, 1), target, name="ffn_fwd_loss_rows")

    red = _GradReduce(c_arr, jc_arr, {
        "qkv": attn_w_qkv.shape[1:], "wo": attn_w_o.shape[1:], "win": sgu_w_in.shape[1:], "wout": sgu_w_out.shape[1:],
        "wgu": ffn_w_gate_up.shape, "wd": ffn_w_down.shape, "slab": (N_CHIPS, SLAB_ROWS // N_CHIPS, D_MODEL)})
    where = {"qkv": ("qkv", None), "wo": ("wo", None), "win": ("win", None), "wout": ("wout", None), "wgu0": ("wgu", 0),
             "wgu1": ("wgu", 1), "wd0": ("wd", 0), "wd1": ("wd", 1), "small": ("slab", "chip")}

    dgu1, dx3, dm1, dg_fpre1, dg_mpost1, _ = ffn_bwd_rows(
        df1, w_d1, gu1, w_gu1, dx4, x3, gain(norm_ffn_pre, 1), m1, gain(norm_mix_post, 1), name="ffn_bwd_rows_1")
    red.grad["wd1"] = mm_tn(a1, df1, shard_major=False, tm=256, tn=D_MODEL, name="dw_down_1").reshape(
        N_CHIPS, D_FF // N_CHIPS, D_MODEL)
    red.grad["wgu1"], (res,) = mm_tn(h3, dgu1, shard_major=True, tm=512, tn=FF_HALF, name="dw_gate_up_1",
                                     riders=[red.exchange(["wd1"])])
    red.exchanged(["wd1"], res)
    dy, (res,) = mm_nt(dm1, w_out, out_dtype=F32, name="dy_sgu", riders=[red.exchange(["wgu1"])])
    red.exchanged(["wgu1"], res)
    red.grad["wout"] = mm_tn(y, dm1, shard_major=False, tm=512, tn=D_MODEL, name="dw_sgu_out").reshape(
        N_CHIPS, D_MODEL // N_CHIPS, D_MODEL)
    (dz, dw_sp, db_sp, dln_g, dln_b), (res_a, res_b) = sgu_bwd(
        z, dy, ln_g, ln_b, w_sp, b_sp, name="sgu_bwd", riders=[red.scatter(["wgu1"]), red.exchange(["wout"])])
    red.scattered(["wgu1"], res_a, where)
    red.exchanged(["wout"], res_b)
    names, rider = red.broadcast([("wgu", 1)])
    red.grad["win"], (res_a, res_b) = mm_tn(h2, dz, shard_major=True, tm=D_MODEL, tn=2 * D_MODEL // N_CHIPS, name="dw_sgu_in",
                                            riders=[rider, red.scatter(["wout"])])
    red.broadcasted(names, res_a)
    red.scattered(["wout"], res_b, where)
    names, rider = red.broadcast([("wout", None)])
    (dx2, df0, dg_mpre1, dg_fpost0, _), (res_a, res_b) = dh_norm_bwd_pair(
        dz, w_in, dx3, x2, gain(norm_mix_pre, 1), f0, gain(norm_ffn_post, 0), name="dh_sgu_norm",
        riders=[red.exchange(["win"]), rider])
    red.exchanged(["win"], res_a)
    red.broadcasted(names, res_b)
    (dgu0, dx1, dm0, dg_fpre0, dg_mpost0, db_o), (res,) = ffn_bwd_rows(
        df0, w_d0, gu0, w_gu0, dx2, x1, gain(norm_ffn_pre, 0), m0, gain(norm_mix_post, 0), name="ffn_bwd_rows_0",
        riders=[red.scatter(["wd1", "win"])])
    red.scattered(["wd1", "win"], res, where)
    names, rider = red.broadcast([("wd", 1), ("win", None)])
    dw_d0, (res,) = mm_tn(a0, df0, shard_major=False, tm=256, tn=D_MODEL, name="dw_down_0", riders=[rider])
    red.broadcasted(names, res)
    red.grad["wd0"] = dw_d0.reshape(N_CHIPS, D_FF // N_CHIPS, D_MODEL)
    do, (res,) = mm_nt(dm0, w_o, out_dtype=BF16, name="do_attn", riders=[red.exchange(["wd0"])])
    red.exchanged(["wd0"], res)
    red.grad["wgu0"], (res,) = mm_tn(h1, dgu0, shard_major=True, tm=512, tn=FF_HALF, name="dw_gate_up_0",
                                     riders=[red.scatter(["wd0"])])
    red.scattered(["wd0"], res, where)
    names, rider = red.broadcast([("wd", 0)])
    dw_o, (res_a, res_b) = mm_tn(o, dm0, shard_major=False, tm=512, tn=D_MODEL, name="dw_attn_out",
                                 riders=[red.exchange(["wgu0"]), rider])
    red.exchanged(["wgu0"], res_a)
    red.broadcasted(names, res_b)
    red.grad["wo"] = dw_o.reshape(N_CHIPS, Q_WIDTH // N_CHIPS, D_MODEL)
    (dq, dkc, dkp, dvc, dvp, dsink), (res_a, res_b) = attn_bwd(
        qkv, sink_rows, do, name="attn_bwd", riders=[red.scatter(["wgu0"]), red.exchange(["wo"])])
    red.scattered(["wgu0"], res_a, where)
    red.exchanged(["wo"], res_b)
    names, rider = red.broadcast([("wgu", 0)])
    (dqkv, db_qkv), (res_a, res_b) = rope_bwd(dq, dkc, dkp, dvc, dvp, cos, sin, name="rope_bwd",
                                              riders=[rider, red.scatter(["wo"])])
    red.broadcasted(names, res_a)
    red.scattered(["wo"], res_b, where)
    names, rider = red.broadcast([("wo", None)])
    red.grad["qkv"], (res,) = mm_tn(h0, dqkv, shard_major=True, tm=D_MODEL, tn=QKV_WIDTH // N_CHIPS, name="dw_qkv",
                                    riders=[rider])
    red.broadcasted(names, res)
    grad_x, dg_mpre0 = dh_norm_bwd_last(dqkv, w_qkv, dx1, x0, gain(norm_mix_pre, 0), name="dh_attn_norm_in")

    norm_grads = [jnp.concatenate(p, axis=0) for p in
                  ((dg_mpre0, dg_mpre1), (dg_mpost0, dg_mpost1), (dg_fpre0, dg_fpre1), (dg_fpost0, dg_fpost1))]
    red.grad["small"] = _pack_small(norm_grads, db_qkv, db_o, dsink[:, :, 0, 0], db_sp[:, :, 0], dln_g, dln_b, dw_sp,
                                    loss_part)
    def big_update(w, g, m, v, tag, after=None):
        return adamw(w, g.reshape(w.shape), m, v, name=f"adamw_{tag}", after=after)

    (res,) = comm_call([red.exchange(["qkv", "small"])], name="tail_1")
    red.exchanged(["qkv", "small"], res)
    state, token = comm_start([red.scatter(["qkv", "small"])], name="tail_2_start")
    upd_wgu = big_update(ffn_w_gate_up, red.dest["wgu"], m_ffn_w_gate_up, v_ffn_w_gate_up, "wgu", after=token)
    (res,) = comm_wait(state, upd_wgu[1], name="tail_2_wait")
    red.scattered(["qkv", "small"], res, where)
    names, rider = red.broadcast([("qkv", None)])
    state, token = comm_start([rider, allcast_rider(red.dest["slab"])], name="tail_3_start")
    upd_wd = big_update(ffn_w_down, red.dest["wd"], m_ffn_w_down, v_ffn_w_down, "wd", after=token)
    (res_a,), (slab_full,) = comm_wait(state, upd_wd[1], name="tail_3_wait")
    red.broadcasted(names, [res_a])
    g_qkv, g_wo, g_win, g_wout = (red.dest[n] for n in ("qkv", "wo", "win", "wout"))
    g_norms, g_bqkv, g_bo, g_sinks, g_bsp, g_lng, g_lnb, g_wsp, loss = _unpack_small(slab_full, chip)

    upd = {
        "attn_w_qkv": big_update(attn_w_qkv, g_qkv, m_attn_w_qkv, v_attn_w_qkv, "qkv"),
        "attn_w_o": big_update(attn_w_o, g_wo, m_attn_w_o, v_attn_w_o, "wo"),
        "sgu_w_in": big_update(sgu_w_in, g_win, m_sgu_w_in, v_sgu_w_in, "win"),
        "sgu_w_out": big_update(sgu_w_out, g_wout, m_sgu_w_out, v_sgu_w_out, "wout"),
        "ffn_w_gate_up": upd_wgu,
        "ffn_w_down": upd_wd,
    }
    small_names = ["norm_mix_pre", "norm_mix_post", "norm_ffn_pre", "norm_ffn_post", "attn_b_qkv", "attn_sinks", "attn_b_o",
                   "sgu_ln_g", "sgu_ln_b", "sgu_w_spatial", "sgu_b_spatial"]
    small_w = [norm_mix_pre, norm_mix_post, norm_ffn_pre, norm_ffn_post, attn_b_qkv, attn_sinks, attn_b_o, sgu_ln_g, sgu_ln_b,
               sgu_w_spatial, sgu_b_spatial]
    small_m = [m_norm_mix_pre, m_norm_mix_post, m_norm_ffn_pre, m_norm_ffn_post, m_attn_b_qkv, m_attn_sinks, m_attn_b_o,
               m_sgu_ln_g, m_sgu_ln_b, m_sgu_w_spatial, m_sgu_b_spatial]
    small_v = [v_norm_mix_pre, v_norm_mix_post, v_norm_ffn_pre, v_norm_ffn_post, v_attn_b_qkv, v_attn_sinks, v_attn_b_o,
               v_sgu_ln_g, v_sgu_ln_b, v_sgu_w_spatial, v_sgu_b_spatial]
    small_g = g_norms + [g_bqkv, g_sinks, g_bo, g_lng, g_lnb, g_wsp, g_bsp]

    def flat2(a):
        return a.reshape(-1, a.shape[-1])

    res = adamw_small([flat2(a) for a in small_w], [flat2(a) for a in small_g], [flat2(a) for a in small_m],
                      [flat2(a) for a in small_v], name="adamw_small")
    for i, nm in enumerate(small_names):
        upd[nm] = tuple(r[i].reshape(small_w[i].shape) for r in res)

    order = ["norm_mix_pre", "norm_mix_post", "norm_ffn_pre", "norm_ffn_post", "attn_w_qkv", "attn_b_qkv", "attn_sinks",
             "attn_w_o", "attn_b_o", "sgu_w_in", "sgu_ln_g", "sgu_ln_b", "sgu_w_spatial", "sgu_b_spatial", "sgu_w_out",
             "ffn_w_gate_up", "ffn_w_down"]
    outs = [loss, grad_x.reshape(1, s, D_MODEL)]
    for part in range(4):
        outs += [upd[nm][part] for nm in order]
    return tuple(outs)
```

```python
import types

import numpy as np
import jax
import jax.numpy as jnp
from jax import lax
from jax.experimental import pallas as pl
from jax.experimental.pallas import tpu as pltpu

F32 = jnp.float32
BF16 = jnp.bfloat16
I32 = jnp.int32

D_MODEL = 1024
HEAD_DIM = 64
N_Q_HEADS = 16
N_KV_HEADS = 4
GQA_GROUP = 4
WINDOW = 128
Q_WIDTH = 1024
KV_WIDTH = 256
QKV_WIDTH = 1536
ROPE_THETA = 10000.0
SGU_GROUPS = 8
SGU_CHUNK = 128
D_FF = 2816
FF_HALF = D_FF // 2
EPS = 1e-6
N_CHIPS = 4
LANES = 128

ADAM_LR = 0.001
ADAM_B1 = 0.9
ADAM_B2 = 0.999
ADAM_EPS = 1e-08
ADAM_WD = 0.01
ADAM_STEP = 10

VMEM_LIMIT = 52 * 1024 * 1024
MESH = pl.DeviceIdType.MESH
NEG = -1e30
NT_DIMS = (((1,), (1,)), ((), ()))
TN_DIMS = (((0,), (0,)), ((), ()))
NN_DIMS = (((1,), (0,)), ((), ()))
ANY = pl.BlockSpec(memory_space=pl.ANY)


def _row_tile(s, want):
    return want if s % want == 0 else s


PEER_KINDS = ("sibling", "chips", "sibling+chips", "everyone")


def _peer_kind(riders):
    kinds = {r.peers for r in riders}
    if not kinds:
        return None
    if "everyone" in kinds:
        return "everyone"
    return "sibling+chips" if len(kinds) == 2 else kinds.pop()


def _peer_barrier(kind):
    x, y, c = _place()
    chips = [(*_partner(x, y, k), c) for k in (1, 2, 3)]
    peers = {"sibling": [(x, y, 1 - c)], "chips": chips, "sibling+chips": [(x, y, 1 - c)] + chips,
             "everyone": [(x, y, 1 - c)] + chips + [(px, py, 1 - c) for px, py, _ in chips]}[kind]
    barrier = pltpu.get_barrier_semaphore()
    for dev in peers:
        pl.semaphore_signal(barrier, inc=1, device_id=dev, device_id_type=MESH)
    pl.semaphore_wait(barrier, len(peers))


def _call(body, *, name, grid=(), in_specs=(), out_specs=(), out_shape=(), scratch_shapes=(), operands=(), prefetch=(),
          aliases=None, riders=(), sem=None):
    n_pre, n_in, n_out, n_scr = len(prefetch), len(operands), len(out_shape), len(scratch_shapes)
    in_specs, out_specs, out_shape = list(in_specs), list(out_specs), list(out_shape)
    operands, scratch_shapes = list(operands), list(scratch_shapes)
    io_alias = {n_pre + i: o for i, o in (aliases or {}).items()}
    for r in riders:
        base_in, base_out = n_pre + len(operands), len(out_shape)
        operands += list(r.inputs)
        in_specs += [ANY] * len(r.inputs)
        for pos, i in enumerate(r.aliased):
            io_alias[base_in + i] = base_out + pos
            out_shape.append(jax.ShapeDtypeStruct(r.inputs[i].shape, r.inputs[i].dtype))
        out_shape += list(r.fresh)
        out_specs += [ANY] * (len(r.aliased) + len(r.fresh))
        scratch_shapes += [pltpu.SemaphoreType.DMA((r.nsem,)), pltpu.SemaphoreType.DMA((r.nsem,))]

    def wrapped(*refs):
        pre, p = refs[:n_pre], n_pre
        core_in, p = refs[p:p + n_in], p + n_in
        r_in = []
        for r in riders:
            r_in.append(refs[p:p + len(r.inputs)])
            p += len(r.inputs)
        core_out, p = refs[p:p + n_out], p + n_out
        r_out = []
        for r in riders:
            k = len(r.aliased) + len(r.fresh)
            r_out.append(refs[p:p + k])
            p += k
        core_scr, p = refs[p:p + n_scr], p + n_scr
        r_sem = [refs[p + 2 * i:p + 2 * i + 2] for i in range(len(riders))]

        def edge(at_last, fns):
            def run():
                if not at_last:
                    _peer_barrier(peer_kind)
                for i, r in enumerate(riders):
                    getattr(r, fns)(r_in[i], r_out[i], r_sem[i][0], r_sem[i][1])
            if not riders:
                return
            if not grid:
                run()
                return
            cond = None
            for d, n in enumerate(grid):
                c = pl.program_id(d) == (n - 1 if at_last else 0)
                cond = c if cond is None else jnp.logical_and(cond, c)
            pl.when(cond)(run)

        edge(False, "start")
        if body is not None:
            body(*pre, *core_in, *core_out, *core_scr)
        edge(True, "finish")

    if sem is None or riders:
        sem = ("arbitrary",) * len(grid)
    kwargs = dict(out_shape=out_shape, input_output_aliases=io_alias, name=name)
    peer_kind = _peer_kind(riders)
    collective = {} if peer_kind is None else {"collective_id": PEER_KINDS.index(peer_kind)}
    if grid:
        kwargs["compiler_params"] = pltpu.CompilerParams(dimension_semantics=sem, vmem_limit_bytes=VMEM_LIMIT, **collective)
    elif collective:
        kwargs["compiler_params"] = pltpu.CompilerParams(**collective)
    if n_pre:
        kwargs["grid_spec"] = pltpu.PrefetchScalarGridSpec(
            num_scalar_prefetch=n_pre, grid=grid, in_specs=in_specs, out_specs=out_specs, scratch_shapes=scratch_shapes)
    else:
        kwargs.update(grid=grid, in_specs=in_specs, out_specs=out_specs, scratch_shapes=scratch_shapes)
    res = pl.pallas_call(wrapped, **kwargs)(*prefetch, *operands)
    core, rest, rider_res = list(res[:n_out]), list(res[n_out:]), []
    for r in riders:
        k = len(r.aliased) + len(r.fresh)
        rider_res.append(rest[:k])
        rest = rest[k:]
    return core, rider_res


def _mm_call(*, grid, in_specs, out_spec, out_shape, dims, nk, kaxis, acc_shape, name, operands, riders=()):
    out_dtype = out_shape.dtype

    def body(a_ref, b_ref, o_ref, *scratch):
        p = lax.dot_general(a_ref[...].astype(BF16), b_ref[...].astype(BF16), dims, preferred_element_type=F32)
        if nk == 1:
            o_ref[...] = p.astype(out_dtype)
        else:
            acc = scratch[0]
            kk = pl.program_id(kaxis)

            @pl.when(kk == 0)
            def _():
                acc[...] = p

            @pl.when(kk > 0)
            def _():
                acc[...] += p

            @pl.when(kk == nk - 1)
            def _():
                o_ref[...] = acc[...].astype(out_dtype)

    sem = ["parallel"] * len(grid)
    if nk > 1:
        sem[kaxis] = "arbitrary"
    (out,), rider_res = _call(
        body, grid=grid, in_specs=in_specs, out_specs=[out_spec], out_shape=[out_shape],
        scratch_shapes=[pltpu.VMEM(acc_shape, F32)] if nk > 1 else [], operands=operands, name=name, riders=riders,
        sem=tuple(sem))
    return (out, rider_res) if riders else out


def mm_nt(a, w, *, out_dtype, name, tm=1024, riders=()):
    m, n = a.shape
    kout = w.shape[0]
    tm = _row_tile(m, tm)
    return _mm_call(grid=(m // tm,),
                    in_specs=[pl.BlockSpec((tm, n), lambda i: (i, 0)), pl.BlockSpec((kout, n), lambda i: (0, 0))],
                    out_spec=pl.BlockSpec((tm, kout), lambda i: (i, 0)),
                    out_shape=jax.ShapeDtypeStruct((m, kout), out_dtype), dims=NT_DIMS, nk=1, kaxis=0,
                    acc_shape=None, name=name, operands=(a, w), riders=riders)


def mm_tn(a, b, *, shard_major, name, tm, tn, tk=None, out_dtype=BF16, riders=()):
    s, m = a.shape
    tk = s if tk is None else _row_tile(s, tk)
    if b.ndim == 3:
        n = 2 * b.shape[2]
        b_spec = pl.BlockSpec((None, tk, tn), lambda j, i, kk: (j // 2, kk, j % 2))
    else:
        n = b.shape[1]
        b_spec = pl.BlockSpec((tk, tn), lambda j, i, kk: (kk, j))
    if shard_major:
        assert tn == n // N_CHIPS
        o_spec = pl.BlockSpec((None, tm, tn), lambda j, i, kk: (j, i, 0))
        o_shape = jax.ShapeDtypeStruct((N_CHIPS, m, tn), out_dtype)
    else:
        o_spec = pl.BlockSpec((tm, tn), lambda j, i, kk: (i, j))
        o_shape = jax.ShapeDtypeStruct((m, n), out_dtype)
    return _mm_call(grid=(n // tn, m // tm, s // tk),
                    in_specs=[pl.BlockSpec((tk, tm), lambda j, i, kk: (kk, i)), b_spec], out_spec=o_spec,
                    out_shape=o_shape, dims=TN_DIMS, nk=s // tk, kaxis=2, acc_shape=(tm, tn), name=name, operands=(a, b),
                    riders=riders)


def _rstd(x):
    return lax.rsqrt(jnp.mean(x * x, axis=-1, keepdims=True) + EPS)


def _rms_bwd(dy, x, g):
    r = _rstd(x)
    xhat = x * r
    gy = dy * g
    dx = r * (gy - xhat * jnp.mean(gy * xhat, axis=-1, keepdims=True))
    return dx, jnp.sum(dy * xhat, axis=0, keepdims=True)


def _accum(ref, val, first):
    @pl.when(first)
    def _():
        ref[...] = val

    @pl.when(jnp.logical_not(first))
    def _():
        ref[...] += val


def _row_spec(tm, width):
    return pl.BlockSpec((tm, width), lambda i: (i, 0))


def _vec_spec(width):
    return pl.BlockSpec((1, width), lambda i: (0, 0))


def _ret(core, rider_res, riders):
    core = core[0] if len(core) == 1 else core
    return (core, rider_res) if riders else core


def prenorm_and_place(x, g, w, layer, chip_arr, *, name, tm=256, riders=()):
    s = x.shape[0]
    tm = _row_tile(s, tm)
    steps = s // tm
    _, r, c = w.shape
    tr = r // steps
    assert tr * steps == r and tr % 16 == 0

    def body(chip_ref, x_ref, g_ref, w_ref, h_ref, o_ref):
        xv = x_ref[...]
        h_ref[...] = (xv * _rstd(xv) * g_ref[...]).astype(BF16)
        o_ref[...] = w_ref[...].astype(BF16)

    core, rr = _call(
        body, grid=(steps,), prefetch=(chip_arr,),
        in_specs=[pl.BlockSpec((tm, D_MODEL), lambda i, chip: (i, 0)), pl.BlockSpec((1, D_MODEL), lambda i, chip: (0, 0)),
                  pl.BlockSpec((None, tr, c), lambda i, chip: (layer, i, 0))],
        out_specs=[pl.BlockSpec((tm, D_MODEL), lambda i, chip: (i, 0)), pl.BlockSpec((None, tr, c), lambda i, chip: (chip[0], i, 0))],
        out_shape=[jax.ShapeDtypeStruct((s, D_MODEL), BF16), jax.ShapeDtypeStruct((N_CHIPS, r, c), BF16)],
        operands=(x, g, w), sem=("parallel",), name=name, riders=riders)
    return _ret(core, rr, riders)


def proj_residual_norm(a, w, x, bias, g_post, g_next, *, name, tm=512, sub=256, riders=()):
    s, k = a.shape
    tm = _row_tile(s, tm)
    sub = min(sub, tm)

    def body(a_ref, w_ref, x_ref, b_ref, gp_ref, gn_ref, xo_ref, h_ref, m_ref):
        for t in range(tm // sub):
            rows = slice(t * sub, (t + 1) * sub)
            mv = jnp.dot(a_ref[rows, :], w_ref[...], preferred_element_type=F32) + b_ref[...]
            m_ref[rows, :] = mv.astype(BF16)
            xn = x_ref[rows, :] + mv * _rstd(mv) * gp_ref[...]
            xo_ref[rows, :] = xn
            h_ref[rows, :] = (xn * _rstd(xn) * gn_ref[...]).astype(BF16)

    row, vec = _row_spec(tm, D_MODEL), _vec_spec(D_MODEL)
    core, rr = _call(
        body, grid=(s // tm,),
        in_specs=[_row_spec(tm, k), pl.BlockSpec((k, D_MODEL), lambda i: (0, 0)), row, vec, vec, vec], out_specs=[row, row, row],
        out_shape=[jax.ShapeDtypeStruct((s, D_MODEL), F32), jax.ShapeDtypeStruct((s, D_MODEL), BF16),
                   jax.ShapeDtypeStruct((s, D_MODEL), BF16)],
        operands=(a, w, x, bias, g_post, g_next), sem=("parallel",), name=name, riders=riders)
    return _ret(core, rr, riders)


def ffn_fwd_loss_rows(h, w_gu, w_d, x, g_post, target, *, name, tm=256, riders=()):
    s = x.shape[0]
    tm = _row_tile(s, tm)

    def body(h_ref, w0, w1, w2, w3, wd_ref, x_ref, g_ref, t_ref, d_ref, a_ref, dx_ref, df_ref, dg_ref, loss_ref):
        first = pl.program_id(0) == 0
        hv = h_ref[...]
        fv = None
        for half, (wg_ref, wu_ref) in enumerate(((w0, w2), (w1, w3))):
            cols = slice(half * FF_HALF, (half + 1) * FF_HALF)
            g = jnp.dot(hv, wg_ref[...], preferred_element_type=F32)
            u = jnp.dot(hv, wu_ref[...], preferred_element_type=F32)
            sig = _sigmoid(g)
            silu = g * sig
            d_ref[0, :, cols] = (u * (sig + silu * (1.0 - sig))).astype(BF16)
            d_ref[1, :, cols] = silu.astype(BF16)
            act = (silu * u).astype(BF16)
            a_ref[:, cols] = act
            p = jnp.dot(act, wd_ref[cols, :], preferred_element_type=F32)
            fv = p if fv is None else fv + p
        gain = g_ref[...]
        err = x_ref[...] + fv * _rstd(fv) * gain - t_ref[...]
        dx = err * (1.0 / D_MODEL)
        dx_ref[...] = dx
        df, dg = _rms_bwd(dx, fv, gain)
        df_ref[...] = df.astype(BF16)
        _accum(dg_ref, dg, first)
        part = jnp.sum(jnp.sum(err * err, axis=-1, keepdims=True), axis=0, keepdims=True) * (0.5 / D_MODEL)
        _accum(loss_ref, jnp.broadcast_to(part, (8, LANES)), first)

    def resident(shape, index):
        return pl.BlockSpec(shape, index, pipeline_mode=pl.Buffered(1))

    row, vec = _row_spec(tm, D_MODEL), _vec_spec(D_MODEL)
    shards = [resident((None, D_MODEL, FF_HALF), (lambda j: (lambda i: (j, 0, 0)))(j)) for j in range(N_CHIPS)]
    core, rr = _call(
        body, grid=(s // tm,),
        in_specs=[row] + shards + [resident((D_FF, D_MODEL), lambda i: (0, 0)), row, vec, row],
        out_specs=[pl.BlockSpec((2, tm, D_FF), lambda i: (0, i, 0)), _row_spec(tm, D_FF), row, row, vec,
                   pl.BlockSpec((8, LANES), lambda i: (0, 0))],
        out_shape=[jax.ShapeDtypeStruct((2, s, D_FF), BF16), jax.ShapeDtypeStruct((s, D_FF), BF16),
                   jax.ShapeDtypeStruct((s, D_MODEL), F32), jax.ShapeDtypeStruct((s, D_MODEL), BF16),
                   jax.ShapeDtypeStruct((1, D_MODEL), F32), jax.ShapeDtypeStruct((8, LANES), F32)],
        operands=(h, w_gu, w_gu, w_gu, w_gu, w_d, x, g_post, target), name=name, riders=riders)
    return _ret(core, rr, riders)


def dh_norm_bwd_pair(a, w, dres, x, g_pre, m, g_post, *, name, tm=512, sub=256, riders=()):
    _, kout, ns = w.shape
    planes = a.ndim == 3
    s = x.shape[0]
    tm = _row_tile(s, tm)
    sub = min(sub, tm)
    a_spec = pl.BlockSpec((2, tm, 2 * ns), lambda i: (0, i, 0)) if planes else pl.BlockSpec((tm, N_CHIPS * ns), lambda i: (i, 0))

    def body(a_ref, w0, w1, w2, w3, dres_ref, x_ref, gpre_ref, m_ref, gpost_ref, dx_ref, dm_ref, dgpre_ref, dgpost_ref, db_ref):
        first = pl.program_id(0) == 0
        sums = None
        for t in range(tm // sub):
            rows = slice(t * sub, (t + 1) * sub)
            dh = None
            for j, w_ref in enumerate((w0, w1, w2, w3)):
                a_j = a_ref[j // 2, rows, (j % 2) * ns:(j % 2 + 1) * ns] if planes else a_ref[rows, j * ns:(j + 1) * ns]
                p = lax.dot_general(a_j, w_ref[...], NT_DIMS, preferred_element_type=F32)
                dh = p if dh is None else dh + p
            d1, dgpre = _rms_bwd(dh, x_ref[rows, :], gpre_ref[...])
            dx = dres_ref[rows, :] + d1
            dx_ref[rows, :] = dx
            dm, dgpost = _rms_bwd(dx, m_ref[rows, :].astype(F32), gpost_ref[...])
            dm_ref[rows, :] = dm.astype(BF16)
            part = (dgpre, dgpost, jnp.sum(dm, axis=0, keepdims=True))
            sums = part if sums is None else tuple(u + v for u, v in zip(sums, part))
        _accum(dgpre_ref, sums[0], first)
        _accum(dgpost_ref, sums[1], first)
        _accum(db_ref, sums[2], first)

    def shard(j):
        return pl.BlockSpec((None, kout, ns), lambda i: (j, 0, 0))

    row, vec = _row_spec(tm, D_MODEL), _vec_spec(D_MODEL)
    vshape = jax.ShapeDtypeStruct((1, D_MODEL), F32)
    core, rr = _call(
        body, grid=(s // tm,), in_specs=[a_spec] + [shard(j) for j in range(N_CHIPS)] + [row, row, vec, row, vec],
        out_specs=[row, row, vec, vec, vec],
        out_shape=[jax.ShapeDtypeStruct((s, D_MODEL), F32), jax.ShapeDtypeStruct((s, D_MODEL), BF16), vshape, vshape, vshape],
        operands=(a, w, w, w, w, dres, x, g_pre, m, g_post), name=name, riders=riders)
    return _ret(core, rr, riders)


def ffn_bwd_rows(df, w_d, d_planes, w_gu, dres, x, g_pre, m, g_post, *, name, tm=256, riders=()):
    s = x.shape[0]
    tm = _row_tile(s, tm)

    def body(df_ref, wd_ref, d_ref, w0, w1, w2, w3, dres_ref, x_ref, gpre_ref, m_ref, gpost_ref,
             o_ref, dx_ref, dm_ref, dgpre_ref, dgpost_ref, db_ref):
        first = pl.program_id(0) == 0
        dfv = df_ref[...]
        dh = None
        for half, (wg_ref, wu_ref) in enumerate(((w0, w2), (w1, w3))):
            cols = slice(half * FF_HALF, (half + 1) * FF_HALF)
            da = lax.dot_general(dfv, wd_ref[cols, :], NT_DIMS, preferred_element_type=F32)
            dg = (da * d_ref[0, :, cols].astype(F32)).astype(BF16)
            du = (da * d_ref[1, :, cols].astype(F32)).astype(BF16)
            o_ref[0, :, cols] = dg
            o_ref[1, :, cols] = du
            p = lax.dot_general(dg, wg_ref[...], NT_DIMS, preferred_element_type=F32)
            p += lax.dot_general(du, wu_ref[...], NT_DIMS, preferred_element_type=F32)
            dh = p if dh is None else dh + p
        d1, dgpre = _rms_bwd(dh, x_ref[...], gpre_ref[...])
        dx = dres_ref[...] + d1
        dx_ref[...] = dx
        dm, dgpost = _rms_bwd(dx, m_ref[...].astype(F32), gpost_ref[...])
        dm_ref[...] = dm.astype(BF16)
        _accum(dgpre_ref, dgpre, first)
        _accum(dgpost_ref, dgpost, first)
        _accum(db_ref, jnp.sum(dm, axis=0, keepdims=True), first)

    def resident(shape, index):
        return pl.BlockSpec(shape, index, pipeline_mode=pl.Buffered(1))

    planes = pl.BlockSpec((2, tm, D_FF), lambda i: (0, i, 0))
    row, vec = _row_spec(tm, D_MODEL), _vec_spec(D_MODEL)
    vshape = jax.ShapeDtypeStruct((1, D_MODEL), F32)
    shards = [resident((None, D_MODEL, FF_HALF), (lambda j: (lambda i: (j, 0, 0)))(j)) for j in range(N_CHIPS)]
    core, rr = _call(
        body, grid=(s // tm,),
        in_specs=[row, resident((D_FF, D_MODEL), lambda i: (0, 0)), planes] + shards + [row, row, vec, row, vec],
        out_specs=[planes, row, row, vec, vec, vec],
        out_shape=[jax.ShapeDtypeStruct((2, s, D_FF), BF16), jax.ShapeDtypeStruct((s, D_MODEL), F32),
                   jax.ShapeDtypeStruct((s, D_MODEL), BF16), vshape, vshape, vshape],
        operands=(df, w_d, d_planes, w_gu, w_gu, w_gu, w_gu, dres, x, g_pre, m, g_post), name=name, riders=riders)
    return _ret(core, rr, riders)


def dh_norm_bwd_last(a, w, dres, x, g_pre, *, name, tm=512, sub=256):
    _, kout, ns = w.shape
    s = x.shape[0]
    tm = _row_tile(s, tm)
    sub = min(sub, tm)

    def body(a_ref, w0, w1, w2, w3, dres_ref, x_ref, g_ref, dx_ref, dg_ref):
        total = None
        for t in range(tm // sub):
            rows = slice(t * sub, (t + 1) * sub)
            dh = None
            for j, w_ref in enumerate((w0, w1, w2, w3)):
                p = lax.dot_general(a_ref[rows, j * ns:(j + 1) * ns], w_ref[...], NT_DIMS, preferred_element_type=F32)
                dh = p if dh is None else dh + p
            d1, dg = _rms_bwd(dh, x_ref[rows, :], g_ref[...])
            dx_ref[rows, :] = dres_ref[rows, :] + d1
            total = dg if total is None else total + dg
        _accum(dg_ref, total, pl.program_id(0) == 0)

    def shard(j):
        return pl.BlockSpec((None, kout, ns), lambda i: (j, 0, 0))

    row, vec = _row_spec(tm, D_MODEL), _vec_spec(D_MODEL)
    (dx, dg), _ = _call(
        body, grid=(s // tm,), in_specs=[_row_spec(tm, N_CHIPS * ns)] + [shard(j) for j in range(N_CHIPS)] + [row, row, vec],
        out_specs=[row, vec], out_shape=[jax.ShapeDtypeStruct((s, D_MODEL), F32), jax.ShapeDtypeStruct((1, D_MODEL), F32)],
        operands=(a, w, w, w, w, dres, x, g_pre), name=name)
    return dx, dg


def _rope_tables(s):
    half = HEAD_DIM // 2
    inv_freq = np.float32(ROPE_THETA) ** (-(np.arange(half, dtype=np.float32) * np.float32(2.0)) / np.float32(HEAD_DIM))
    ang = np.arange(s, dtype=np.float32)[:, None] * inv_freq[None, :]
    cos, sin = np.cos(ang).astype(np.float32), np.sin(ang).astype(np.float32)
    return jnp.asarray(np.tile(cos, (1, 4))), jnp.asarray(np.concatenate([-sin, sin, -sin, sin], axis=1))


def _swap_halves(x):
    lane = lax.broadcasted_iota(I32, x.shape, 1)
    return jnp.where((lane & (HEAD_DIM - 1)) < HEAD_DIM // 2, pltpu.roll(x, LANES - 32, 1), pltpu.roll(x, 32, 1))


N_ROPE_BLOCKS = (Q_WIDTH + KV_WIDTH) // LANES


def qkv_proj(h, w, bias, cos, sin, *, name, tm=1024, riders=()):
    s, k = h.shape
    ns = w.shape[2]
    tm = _row_tile(s, tm)

    def body(h_ref, w_ref, b_ref, c_ref, s_ref, o_ref):
        j = pl.program_id(0)
        sub = min(256, tm)
        for t in range(tm // sub):
            rows = slice(t * sub, (t + 1) * sub)
            p = jnp.dot(h_ref[rows, :], w_ref[...], preferred_element_type=F32) + b_ref[...]
            cosv, sinv = c_ref[rows, :], s_ref[rows, :]
            for blk in range(ns // LANES):
                xb = p[:, blk * LANES:(blk + 1) * LANES]
                roped = xb * cosv + _swap_halves(xb) * sinv
                is_qk = j * (ns // LANES) + blk < N_ROPE_BLOCKS
                o_ref[rows, blk * LANES:(blk + 1) * LANES] = jnp.where(is_qk, roped, xb).astype(BF16)

    core, rr = _call(
        body, grid=(N_CHIPS, s // tm),
        in_specs=[pl.BlockSpec((tm, k), lambda j, i: (i, 0)), pl.BlockSpec((None, k, ns), lambda j, i: (j, 0, 0)),
                  pl.BlockSpec((1, ns), lambda j, i: (0, j)), pl.BlockSpec((tm, LANES), lambda j, i: (i, 0)),
                  pl.BlockSpec((tm, LANES), lambda j, i: (i, 0))],
        out_specs=[pl.BlockSpec((tm, ns), lambda j, i: (i, j))], out_shape=[jax.ShapeDtypeStruct((s, N_CHIPS * ns), BF16)],
        operands=(h, w, bias, cos, sin), sem=("parallel", "parallel"), name=name, riders=riders)
    return _ret(core, rr, riders)


def rope_bwd(dq, dkc, dkp, dvc, dvp, cos, sin, *, name, riders=()):
    s = dq.shape[0]
    tm = 2 * WINDOW if s % (2 * WINDOW) == 0 else WINDOW
    nb = s // tm

    def body(dq_ref, dkc_ref, dkp_ref, dkp_next_ref, dvc_ref, dvp_ref, dvp_next_ref, c_ref, s_ref, o_ref, db_ref):
        i = pl.program_id(0)
        has_next = (i < nb - 1).astype(F32)
        cosv, sinv = c_ref[...], s_ref[...]

        def shifted(ref, next_ref, cols):
            last = has_next * next_ref[:WINDOW, cols].astype(F32)
            return last if tm == WINDOW else jnp.concatenate([ref[WINDOW:, cols].astype(F32), last], axis=0)

        parts = []
        for blk in range(QKV_WIDTH // LANES):
            if blk < Q_WIDTH // LANES:
                g = dq_ref[:, blk * LANES:(blk + 1) * LANES].astype(F32)
            else:
                own, prv, nxt = (dkc_ref, dkp_ref, dkp_next_ref) if blk < N_ROPE_BLOCKS else (dvc_ref, dvp_ref, dvp_next_ref)
                cols = slice((blk % 2) * LANES, (blk % 2 + 1) * LANES)
                g = own[:, cols].astype(F32) + shifted(prv, nxt, cols)
            if blk < N_ROPE_BLOCKS:
                g = g * cosv + _swap_halves(g * sinv)
            o_ref[:, blk * LANES:(blk + 1) * LANES] = g.astype(BF16)
            parts.append(jnp.sum(g, axis=0, keepdims=True))
        sums = jnp.concatenate(parts, axis=1)
        _accum(db_ref, sums, i == 0)

    own_spec = _row_spec(tm, KV_WIDTH)
    next_spec = pl.BlockSpec((tm, KV_WIDTH), lambda i: (jnp.minimum(i + 1, nb - 1), 0))
    core, rr = _call(
        body, grid=(nb,),
        in_specs=[_row_spec(tm, Q_WIDTH), own_spec, own_spec, next_spec, own_spec, own_spec, next_spec,
                  _row_spec(tm, LANES), _row_spec(tm, LANES)],
        out_specs=[_row_spec(tm, QKV_WIDTH), _vec_spec(QKV_WIDTH)],
        out_shape=[jax.ShapeDtypeStruct((s, QKV_WIDTH), BF16), jax.ShapeDtypeStruct((1, QKV_WIDTH), F32)],
        operands=(dq, dkc, dkp, dkp, dvc, dvp, dvp, cos, sin), name=name, riders=riders)
    return _ret(core, rr, riders)


ROWS = GQA_GROUP * WINDOW


def _prev_slots():
    kpos = lax.broadcasted_iota(I32, (WINDOW, ROWS), 0)
    qpos = lax.broadcasted_iota(I32, (WINDOW, ROWS), 1) & (WINDOW - 1)
    return kpos > qpos


def _head_cols(ref, head):
    return ref[:, head * HEAD_DIM:(head + 1) * HEAD_DIM]


def _stack_heads(ref, h):
    return jnp.concatenate([_head_cols(ref, GQA_GROUP * h + g) for g in range(GQA_GROUP)], axis=0)


def _band(prev_ref, cur_ref, h):
    return jnp.concatenate([_head_cols(prev_ref, h), _head_cols(cur_ref, h)], axis=0)


def _pick(prev, band):
    return jnp.where(prev, band[:WINDOW], band[WINDOW:])


def _spread(prev, x):
    return jnp.concatenate([jnp.where(prev, x, 0.0), jnp.where(prev, 0.0, x)], axis=0).astype(BF16)


def _attn_probs(s_band, sink, prev, has_prev):
    scale = HEAD_DIM ** -0.5
    s = jnp.where(prev, jnp.where(has_prev, s_band[:WINDOW], NEG), s_band[WINDOW:]) * scale
    m = jnp.maximum(jnp.max(s, axis=0, keepdims=True), sink)
    e, es = jnp.exp(s - m), jnp.exp(sink - m)
    inv = 1.0 / (jnp.sum(e, axis=0, keepdims=True) + es)
    return e * inv, es * inv


def _attn_specs(nb):
    kcol, vcol = Q_WIDTH // KV_WIDTH, Q_WIDTH // KV_WIDTH + 1
    q_spec = pl.BlockSpec((WINDOW, Q_WIDTH), lambda n: (n, 0))
    return [q_spec,
            pl.BlockSpec((WINDOW, KV_WIDTH), lambda n: (n, kcol)),
            pl.BlockSpec((WINDOW, KV_WIDTH), lambda n: (jnp.maximum(n - 1, 0), kcol)),
            pl.BlockSpec((WINDOW, KV_WIDTH), lambda n: (n, vcol)),
            pl.BlockSpec((WINDOW, KV_WIDTH), lambda n: (jnp.maximum(n - 1, 0), vcol)),
            pl.BlockSpec((N_KV_HEADS, 8, ROWS), lambda n: (0, 0, 0))]


def attn_fwd(qkv, sink_rows, *, name, riders=()):
    s = qkv.shape[0]

    def body(q_ref, kc_ref, kp_ref, vc_ref, vp_ref, sink_ref, o_ref):
        prev = _prev_slots()
        has_prev = pl.program_id(0) > 0
        heads = range(N_KV_HEADS)
        s_bands = [lax.dot_general(_band(kp_ref, kc_ref, h), _stack_heads(q_ref, h), NT_DIMS, preferred_element_type=F32)
                   for h in heads]
        p_bands = [_spread(prev, _attn_probs(s_bands[h], sink_ref[h, 0:1, :], prev, has_prev)[0]) for h in heads]
        outs = [lax.dot_general(_band(vp_ref, vc_ref, h), p_bands[h], TN_DIMS, preferred_element_type=F32).T for h in heads]
        for h in heads:
            for g in range(GQA_GROUP):
                head = GQA_GROUP * h + g
                o_ref[:, head * HEAD_DIM:(head + 1) * HEAD_DIM] = outs[h][g * WINDOW:(g + 1) * WINDOW].astype(BF16)

    core, rr = _call(
        body, grid=(s // WINDOW,), in_specs=_attn_specs(s // WINDOW), out_specs=[pl.BlockSpec((WINDOW, Q_WIDTH), lambda n: (n, 0))],
        out_shape=[jax.ShapeDtypeStruct((s, Q_WIDTH), BF16)], operands=(qkv, qkv, qkv, qkv, qkv, sink_rows), sem=("parallel",),
        name=name, riders=riders)
    return _ret(core, rr, riders)


def attn_bwd(qkv, sink_rows, do, *, name, riders=()):
    s = qkv.shape[0]

    def body(q_ref, kc_ref, kp_ref, vc_ref, vp_ref, sink_ref, do_ref, dq_ref, dkc_ref, dkp_ref, dvc_ref, dvp_ref, dsink_ref):
        n = pl.program_id(0)
        prev = _prev_slots()
        scale = HEAD_DIM ** -0.5
        heads = range(N_KV_HEADS)
        qs, dos = [_stack_heads(q_ref, h) for h in heads], [_stack_heads(do_ref, h) for h in heads]
        kbands, vbands = [_band(kp_ref, kc_ref, h) for h in heads], [_band(vp_ref, vc_ref, h) for h in heads]
        s_bands = [lax.dot_general(kbands[h], qs[h], NT_DIMS, preferred_element_type=F32) for h in heads]
        dp_bands = [lax.dot_general(vbands[h], dos[h], NT_DIMS, preferred_element_type=F32) for h in heads]
        ds_bands, p_bands, parts = [], [], []
        for h in heads:
            p, ps = _attn_probs(s_bands[h], sink_ref[h, 0:1, :], prev, n > 0)
            dp = _pick(prev, dp_bands[h])
            delta = jnp.sum(p * dp, axis=0, keepdims=True)
            ds_bands.append(_spread(prev, p * (dp - delta) * scale))
            p_bands.append(_spread(prev, p))
            dsink = -(ps * delta)
            for g in range(GQA_GROUP):
                parts.append(jnp.broadcast_to(jnp.sum(dsink[:, g * WINDOW:(g + 1) * WINDOW], axis=1, keepdims=True), (8, LANES)))
        for h in heads:
            dk = jnp.dot(ds_bands[h], qs[h], preferred_element_type=F32).astype(BF16)
            dv = jnp.dot(p_bands[h], dos[h], preferred_element_type=F32).astype(BF16)
            dq = lax.dot_general(kbands[h], ds_bands[h], TN_DIMS, preferred_element_type=F32).T
            cols = slice(h * HEAD_DIM, (h + 1) * HEAD_DIM)
            dkp_ref[:, cols], dkc_ref[:, cols] = dk[:WINDOW], dk[WINDOW:]
            dvp_ref[:, cols], dvc_ref[:, cols] = dv[:WINDOW], dv[WINDOW:]
            for g in range(GQA_GROUP):
                head = GQA_GROUP * h + g
                dq_ref[:, head * HEAD_DIM:(head + 1) * HEAD_DIM] = dq[g * WINDOW:(g + 1) * WINDOW].astype(BF16)

        @pl.when(n == 0)
        def _():
            for i, part in enumerate(parts):
                dsink_ref[i // GQA_GROUP, i % GQA_GROUP] = part

        @pl.when(n > 0)
        def _():
            for i, part in enumerate(parts):
                dsink_ref[i // GQA_GROUP, i % GQA_GROUP] += part

    rows_q = pl.BlockSpec((WINDOW, Q_WIDTH), lambda n: (n, 0))
    rows_kv = pl.BlockSpec((WINDOW, KV_WIDTH), lambda n: (n, 0))
    kv_shape = jax.ShapeDtypeStruct((s, KV_WIDTH), BF16)
    core, rr = _call(
        body, grid=(s // WINDOW,), in_specs=_attn_specs(s // WINDOW) + [rows_q],
        out_specs=[rows_q, rows_kv, rows_kv, rows_kv, rows_kv,
                   pl.BlockSpec((N_KV_HEADS, GQA_GROUP, 8, LANES), lambda n: (0, 0, 0, 0))],
        out_shape=[jax.ShapeDtypeStruct((s, Q_WIDTH), BF16), kv_shape, kv_shape, kv_shape, kv_shape,
                   jax.ShapeDtypeStruct((N_KV_HEADS, GQA_GROUP, 8, LANES), F32)],
        operands=(qkv, qkv, qkv, qkv, qkv, sink_rows, do), sem=("arbitrary",), name=name, riders=riders)
    return _ret(core, rr, riders)


GELU_C = 0.7978845608028654
GELU_A = 0.044715


def _gelu(x):
    return 0.5 * x * (1.0 + jnp.tanh(x * (GELU_C + (GELU_C * GELU_A) * (x * x))))


def _gelu_and_grad(x):
    x2 = x * x
    t = jnp.tanh(x * (GELU_C + (GELU_C * GELU_A) * x2))
    half_x, one_t = 0.5 * x, 1.0 + t
    return half_x * one_t, 0.5 * one_t + half_x * (1.0 - t * t) * (GELU_C + (3.0 * GELU_C * GELU_A) * x2)


def _tril_bf16(w):
    row = lax.broadcasted_iota(I32, (SGU_CHUNK, SGU_CHUNK), 0)
    col = lax.broadcasted_iota(I32, (SGU_CHUNK, SGU_CHUNK), 1)
    return jnp.where(row >= col, w, 0.0).astype(BF16)


def _sgu_norm(vg, g, b):
    mu = jnp.mean(vg, axis=-1, keepdims=True)
    cen = vg - mu
    rstd = lax.rsqrt(jnp.mean(cen * cen, axis=-1, keepdims=True) + EPS)
    xhat = cen * rstd
    return xhat, rstd, xhat * g + b


def sgu_in_fwd(h, w_in, ln_g, ln_b, w_sp, b_sp, *, name, tm=512, riders=()):
    s, k = h.shape
    ns = w_in.shape[2]
    tm = _row_tile(s, tm)

    def body(h_ref, w0, w1, w2, w3, g_ref, b_ref, w_ref, bs_ref, z_ref, y_ref):
        hv = h_ref[...]
        zs = [jnp.dot(hv, w_ref_j[...], preferred_element_type=F32) for w_ref_j in (w0, w1, w2, w3)]
        for j, zj in enumerate(zs):
            z_ref[:, j * ns:(j + 1) * ns] = zj.astype(BF16)
        u = _gelu(jnp.concatenate(zs[:2], axis=1))
        _, _, vn = _sgu_norm(_gelu(jnp.concatenate(zs[2:], axis=1)), g_ref[...], b_ref[...])
        vn = vn.astype(BF16)
        for grp in range(SGU_GROUPS):
            w = _tril_bf16(w_ref[grp])
            cols = slice(grp * LANES, (grp + 1) * LANES)
            for ch in range(tm // SGU_CHUNK):
                rows = slice(ch * SGU_CHUNK, (ch + 1) * SGU_CHUNK)
                mixed = jnp.dot(w, vn[rows, cols], preferred_element_type=F32) + bs_ref[grp]
                y_ref[rows, cols] = (u[rows, cols] * mixed).astype(BF16)

    def shard(j):
        return pl.BlockSpec((None, k, ns), lambda i: (j, 0, 0))

    full3 = pl.BlockSpec((SGU_GROUPS, SGU_CHUNK, SGU_CHUNK), lambda i: (0, 0, 0))
    core, rr = _call(
        body, grid=(s // tm,),
        in_specs=[_row_spec(tm, k)] + [shard(j) for j in range(N_CHIPS)] + [_vec_spec(D_MODEL), _vec_spec(D_MODEL), full3, full3],
        out_specs=[_row_spec(tm, 2 * D_MODEL), _row_spec(tm, D_MODEL)],
        out_shape=[jax.ShapeDtypeStruct((s, 2 * D_MODEL), BF16), jax.ShapeDtypeStruct((s, D_MODEL), BF16)],
        operands=(h, w_in, w_in, w_in, w_in, ln_g, ln_b, w_sp, b_sp), sem=("parallel",), name=name, riders=riders)
    return _ret(core, rr, riders)


def sgu_bwd(z, dy, ln_g, ln_b, w_sp, b_sp, *, name, tm=256, riders=()):
    s = z.shape[0]
    tm = _row_tile(s, tm)

    def body(z_ref, dy_ref, g_ref, b_ref, w_ref, bs_ref, dz_ref, dw_ref, dbs_ref, dg_ref, db_ref, dvn_buf):
        first = pl.program_id(0) == 0
        u, u_grad = _gelu_and_grad(z_ref[:, :D_MODEL].astype(F32))
        vg, v_grad = _gelu_and_grad(z_ref[:, D_MODEL:].astype(F32))
        xhat, rstd, vn = _sgu_norm(vg, g_ref[...], b_ref[...])
        vn = vn.astype(BF16)
        dyv = dy_ref[...]
        dmixed = dyv * u
        dz_gate = dyv * u_grad
        row = lax.broadcasted_iota(I32, (SGU_CHUNK, SGU_CHUNK), 0)
        col = lax.broadcasted_iota(I32, (SGU_CHUNK, SGU_CHUNK), 1)
        dws, dbss = [], []
        for grp in range(SGU_GROUPS):
            w = _tril_bf16(w_ref[grp])
            cols = slice(grp * LANES, (grp + 1) * LANES)
            dw = jnp.zeros((SGU_CHUNK, SGU_CHUNK), F32)
            dbs = jnp.zeros((SGU_CHUNK, 1), F32)
            for ch in range(tm // SGU_CHUNK):
                rows = slice(ch * SGU_CHUNK, (ch + 1) * SGU_CHUNK)
                vblk = vn[rows, cols]
                mixed = jnp.dot(w, vblk, preferred_element_type=F32) + bs_ref[grp]
                dz_ref[rows, cols] = (dz_gate[rows, cols] * mixed).astype(BF16)
                dm = dmixed[rows, cols]
                dmb = dm.astype(BF16)
                dvn_buf[rows, cols] = lax.dot_general(w, dmb, TN_DIMS, preferred_element_type=F32)
                dw += lax.dot_general(dmb, vblk, NT_DIMS, preferred_element_type=F32)
                dbs += jnp.sum(dm, axis=-1, keepdims=True)
            dws.append(jnp.where(row >= col, dw, 0.0))
            dbss.append(jnp.broadcast_to(dbs, (SGU_CHUNK, SGU_CHUNK)))

        dvn = dvn_buf[...]
        dxhat = dvn * g_ref[...]
        dvg = rstd * (dxhat - jnp.mean(dxhat, axis=-1, keepdims=True) - xhat * jnp.mean(dxhat * xhat, axis=-1, keepdims=True))
        dz_ref[:, D_MODEL:] = (dvg * v_grad).astype(BF16)
        dlng, dlnb = jnp.sum(dvn * xhat, axis=0, keepdims=True), jnp.sum(dvn, axis=0, keepdims=True)

        @pl.when(first)
        def _():
            for grp in range(SGU_GROUPS):
                dw_ref[grp] = dws[grp]
                dbs_ref[grp] = dbss[grp]
            dg_ref[...] = dlng
            db_ref[...] = dlnb

        @pl.when(jnp.logical_not(first))
        def _():
            for grp in range(SGU_GROUPS):
                dw_ref[grp] += dws[grp]
                dbs_ref[grp] += dbss[grp]
            dg_ref[...] += dlng
            db_ref[...] += dlnb

    full3 = pl.BlockSpec((SGU_GROUPS, SGU_CHUNK, SGU_CHUNK), lambda i: (0, 0, 0))
    s3 = jax.ShapeDtypeStruct((SGU_GROUPS, SGU_CHUNK, SGU_CHUNK), F32)
    vshape = jax.ShapeDtypeStruct((1, D_MODEL), F32)
    core, rr = _call(
        body, grid=(s // tm,),
        in_specs=[_row_spec(tm, 2 * D_MODEL), _row_spec(tm, D_MODEL), _vec_spec(D_MODEL), _vec_spec(D_MODEL), full3, full3],
        out_specs=[_row_spec(tm, 2 * D_MODEL), full3, full3, _vec_spec(D_MODEL), _vec_spec(D_MODEL)],
        out_shape=[jax.ShapeDtypeStruct((s, 2 * D_MODEL), BF16), s3, s3, vshape, vshape],
        scratch_shapes=[pltpu.VMEM((tm, D_MODEL), F32)], operands=(z, dy, ln_g, ln_b, w_sp, b_sp), name=name, riders=riders)
    return _ret(core, rr, riders)


def _sigmoid(x):
    return 1.0 / (1.0 + jnp.exp(-x))


def ffn_up(h, w_gu, *, name, tm=512, riders=()):
    s = h.shape[0]
    tm = _row_tile(s, tm)

    def body(h_ref, wg_ref, wu_ref, d_ref, a_ref):
        hv = h_ref[...]
        sub = min(256, tm)
        for t in range(tm // sub):
            rows = slice(t * sub, (t + 1) * sub)
            g = jnp.dot(hv[rows], wg_ref[...], preferred_element_type=F32)
            u = jnp.dot(hv[rows], wu_ref[...], preferred_element_type=F32)
            sig = _sigmoid(g)
            silu = g * sig
            d_ref[0, rows, :] = (u * (sig + silu * (1.0 - sig))).astype(BF16)
            d_ref[1, rows, :] = silu.astype(BF16)
            a_ref[rows, :] = (silu * u).astype(BF16)

    core, rr = _call(
        body, grid=(2, s // tm),
        in_specs=[pl.BlockSpec((tm, D_MODEL), lambda j, i: (i, 0)),
                  pl.BlockSpec((None, D_MODEL, FF_HALF), lambda j, i: (j, 0, 0)),
                  pl.BlockSpec((None, D_MODEL, FF_HALF), lambda j, i: (j + 2, 0, 0))],
        out_specs=[pl.BlockSpec((2, tm, FF_HALF), lambda j, i: (0, i, j)), pl.BlockSpec((tm, FF_HALF), lambda j, i: (i, j))],
        out_shape=[jax.ShapeDtypeStruct((2, s, D_FF), BF16), jax.ShapeDtypeStruct((s, D_FF), BF16)],
        operands=(h, w_gu, w_gu), sem=("parallel", "parallel"), name=name, riders=riders)
    return _ret(core, rr, riders)


def _weight_tile(rows):
    for tr in (512, 352, 256, 128):
        if rows % tr == 0:
            return tr
    return rows


def place_shard(w, layer, chip_arr, dtype, *, name, riders=()):
    _, r, c = w.shape
    tr = _weight_tile(r)

    def body(chip_ref, w_ref, o_ref):
        o_ref[...] = w_ref[...].astype(dtype)

    core, rr = _call(
        body, grid=(r // tr,), prefetch=(chip_arr,),
        in_specs=[pl.BlockSpec((None, tr, c), lambda i, chip: (layer, i, 0))],
        out_specs=[pl.BlockSpec((None, tr, c), lambda i, chip: (chip[0], i, 0))],
        out_shape=[jax.ShapeDtypeStruct((N_CHIPS, r, c), dtype)], operands=(w,), sem=("parallel",), name=name, riders=riders)
    return _ret(core, rr, riders)


def _adamw_math(w, g, m, v):
    m = ADAM_B1 * m + (1.0 - ADAM_B1) * g
    v = ADAM_B2 * v + (1.0 - ADAM_B2) * (g * g)
    m_hat = m / (1.0 - ADAM_B1 ** ADAM_STEP)
    v_hat = v / (1.0 - ADAM_B2 ** ADAM_STEP)
    delta = -ADAM_LR * (m_hat / (jnp.sqrt(v_hat) + ADAM_EPS) + ADAM_WD * w)
    return delta, m, v


def adamw(w, g, m, v, *, name, after=None):
    nl, r, c = w.shape
    tr = _weight_tile(r)

    def body(w_ref, g_ref, m_ref, v_ref, *rest):
        go_ref, d_ref, mo_ref, vo_ref = rest[-4:]
        gv = g_ref[...]
        go_ref[...] = gv
        d_ref[...], mo_ref[...], vo_ref[...] = _adamw_math(w_ref[...], gv, m_ref[...], v_ref[...])

    spec = pl.BlockSpec((None, tr, c), lambda l, i: (l, i, 0))
    shape = jax.ShapeDtypeStruct(w.shape, F32)
    extra = [] if after is None else [after]
    outs, _ = _call(body, grid=(nl, r // tr), in_specs=[spec] * 4 + [ANY] * len(extra), out_specs=[spec] * 4,
                    out_shape=[shape] * 4, operands=(w, g, m, v, *extra), sem=("parallel", "parallel"), name=name)
    return outs


def adamw_small(ws, gs, ms, vs, *, name):
    n = len(ws)

    def body(*refs):
        ins, outs = refs[:4 * n], refs[4 * n:]
        for t in range(n):
            gv = ins[n + t][...]
            outs[t][...] = gv
            outs[n + t][...], outs[2 * n + t][...], outs[3 * n + t][...] = _adamw_math(
                ins[t][...], gv, ins[2 * n + t][...], ins[3 * n + t][...])

    shapes = [jax.ShapeDtypeStruct(w.shape, F32) for w in ws]
    res = pl.pallas_call(body, out_shape=shapes * 4, name=name)(*ws, *gs, *ms, *vs)
    return res[:n], res[n:2 * n], res[2 * n:3 * n], res[3 * n:]


def pair_add(g, r1, c_arr, *, name):
    _, rows, cdim = g.shape
    h = rows // 2

    def body(c_ref, g_ref, r_ref, o_ref):
        o_ref[...] = (g_ref[...].astype(F32) + r_ref[...].astype(F32)).astype(o_ref.dtype)

    (out,), _ = _call(
        body, grid=(N_CHIPS,), prefetch=(c_arr,),
        in_specs=[pl.BlockSpec((None, h, cdim), lambda s, c: (s, c[0], 0)), pl.BlockSpec((None, h, cdim), lambda s, c: (s, 0, 0))],
        out_specs=[pl.BlockSpec((None, h, cdim), lambda s, c: (s, 0, 0))],
        out_shape=[jax.ShapeDtypeStruct((N_CHIPS, h, cdim), g.dtype)], operands=(g, r1), sem=("parallel",), name=name)
    return out


def final_add(g, r1, r2, jc_arr, *, dest_shape, lead, prev, name):
    _, rows, cdim = g.shape
    h = rows // 2

    def body(jc_ref, g_ref, r1_ref, r2_ref, *rest):
        o_ref = rest[-1]
        acc = g_ref[...].astype(F32) + r1_ref[...].astype(F32)
        for k in range(3):
            acc = acc + r2_ref[k].astype(F32)
        o_ref[...] = acc

    if lead is None:
        o_spec = pl.BlockSpec((h, cdim), lambda i, jc: (jc[1], 0))
    elif lead == "chip":
        o_spec = pl.BlockSpec((None, h, cdim), lambda i, jc: (jc[0], jc[1], 0))
    else:
        o_spec = pl.BlockSpec((None, h, cdim), lambda i, jc: (lead, jc[1], 0))
    in_specs = [pl.BlockSpec((None, h, cdim), lambda i, jc: (jc[0], jc[1], 0)),
                pl.BlockSpec((None, h, cdim), lambda i, jc: (jc[0], 0, 0)),
                pl.BlockSpec((3, h, cdim), lambda i, jc: (0, 0, 0))]
    operands = [g, r1, r2]
    aliases = None
    if prev is not None:
        in_specs.append(ANY)
        operands.append(prev)
        aliases = {3: 0}
    (out,), _ = _call(body, grid=(1,), prefetch=(jc_arr,), in_specs=in_specs, out_specs=[o_spec],
                      out_shape=[jax.ShapeDtypeStruct(dest_shape, F32)], operands=operands, aliases=aliases, name=name)
    return out


def _place():
    return lax.axis_index("x"), lax.axis_index("y"), lax.axis_index("c")


def _partner(x, y, k):
    return (1 - x if k >> 1 else x), (1 - y if k & 1 else y)


WHOLE = (0, 1, 1)


def _half(rows, sel, dtype, piece=WHOLE):
    lo, hi, n = piece
    align = 16 if dtype == BF16 else 8
    step = rows // 2 // n
    assert rows // 2 == step * n and step % align == 0
    return pl.ds(pl.multiple_of(sel * (rows // 2) + lo * step, align), (hi - lo) * step)


def _rider(peers, inputs, aliased, fresh, nsem, copies, arrivals):
    def start(ins, outs, send, recv):
        for cp in copies(ins, outs, send, recv):
            cp.start()

    def finish(ins, outs, send, recv):
        for cp in arrivals(ins, outs, send, recv):
            cp.wait_recv()
        for cp in copies(ins, outs, send, recv):
            cp.wait_send()

    return types.SimpleNamespace(peers=peers, inputs=list(inputs), aliased=list(aliased), fresh=list(fresh), nsem=nsem,
                                 start=start, finish=finish)


def _remote(src, dst, send, recv, idx, dev):
    return pltpu.make_async_remote_copy(src_ref=src, dst_ref=dst, send_sem=send.at[idx], recv_sem=recv.at[idx],
                                        device_id=dev, device_id_type=MESH)


def gather_ici_rider(fulls, pieces=None):
    nt = len(fulls)
    pieces = pieces or [WHOLE] * nt

    def region(outs, t, slot, sel):
        return outs[t].at[slot, _half(fulls[t].shape[1], sel, fulls[t].dtype, pieces[t])]

    def copies(ins, outs, send, recv):
        x, y, c = _place()
        res = []
        for t in range(nt):
            for k in (1, 2, 3):
                px, py = _partner(x, y, k)
                mine = region(outs, t, 2 * x + y, c)
                res.append(_remote(mine, mine, send, recv, 3 * t + k - 1, (px, py, c)))
        return res

    def arrivals(ins, outs, send, recv):
        x, y, c = _place()
        res = []
        for t in range(nt):
            for k in (1, 2, 3):
                px, py = _partner(x, y, k)
                theirs = region(outs, t, 2 * px + py, c)
                res.append(_remote(theirs, theirs, send, recv, 3 * t + k - 1, (x, y, c)))
        return res

    return _rider("chips", fulls, range(nt), [], 3 * nt, copies, arrivals)


def gather_d2d_rider(fulls, pieces=None):
    nt = len(fulls)
    pieces = pieces or [WHOLE] * nt

    def region(outs, t, slot, sel):
        return outs[t].at[slot, _half(fulls[t].shape[1], sel, fulls[t].dtype, pieces[t])]

    def both(outs, send, recv, mine):
        x, y, c = _place()
        res = []
        for t in range(nt):
            for k in (1, 2, 3):
                px, py = _partner(x, y, k)
                part = region(outs, t, 2 * px + py, c if mine else 1 - c)
                res.append(_remote(part, part, send, recv, 3 * t + k - 1, (x, y, 1 - c)))
        return res

    return _rider("sibling", fulls, range(nt), [], 3 * nt, lambda i, o, s, r: both(o, s, r, True),
                  lambda i, o, s, r: both(o, s, r, False))


def exchange_rider(grads):
    nt = len(grads)

    def both(ins, outs, send, recv):
        x, y, c = _place()
        return [_remote(ins[t].at[:, _half(grads[t].shape[1], 1 - c, grads[t].dtype)], outs[t], send, recv, t, (x, y, 1 - c))
                for t in range(nt)]

    fresh = [jax.ShapeDtypeStruct((N_CHIPS, g.shape[1] // 2, g.shape[2]), g.dtype) for g in grads]
    return _rider("sibling", grads, [], fresh, nt, both, both)


def scatter_rider(parts):
    nt = len(parts)

    def both(ins, outs, send, recv):
        x, y, c = _place()
        res = []
        for t in range(nt):
            for k in (1, 2, 3):
                px, py = _partner(x, y, k)
                res.append(_remote(ins[t].at[2 * px + py], outs[t].at[k - 1], send, recv, 3 * t + k - 1, (px, py, c)))
        return res

    fresh = [jax.ShapeDtypeStruct((3,) + p.shape[1:], p.dtype) for p in parts]
    return _rider("chips", parts, [], fresh, 3 * nt, both, both)


def broadcast_rider(bufs, items):
    def region(outs, item, sel):
        bi, lead = item
        ref = outs[bi]
        if lead == "chip":
            x, y, _ = _place()
            ref = ref.at[2 * x + y]
        elif lead is not None:
            ref = ref.at[lead]
        return ref.at[_half(ref.shape[0], sel, F32)]

    def both(outs, send, recv, mine):
        x, y, c = _place()
        res = []
        for i, item in enumerate(items):
            part = region(outs, item, c if mine else 1 - c)
            res.append(_remote(part, part, send, recv, i, (x, y, 1 - c)))
        return res

    return _rider("sibling", bufs, range(len(bufs)), [], len(items), lambda i, o, s, r: both(o, s, r, True),
                  lambda i, o, s, r: both(o, s, r, False))


def allcast_rider(buf):
    peers = [(k, flip) for k in range(N_CHIPS) for flip in (0, 1) if (k, flip) != (0, 0)]

    def both(outs, send, recv, mine):
        x, y, c = _place()
        res = []
        for i, (k, flip) in enumerate(peers):
            px, py = _partner(x, y, k)
            pc = 1 - c if flip else c
            slot, sel = (2 * x + y, c) if mine else (2 * px + py, pc)
            part = outs[0].at[slot, _half(buf.shape[1], sel, F32)]
            res.append(_remote(part, part, send, recv, i, (px, py, pc)))
        return res

    return _rider("everyone", [buf], [0], [], len(peers), lambda i, o, s, r: both(o, s, r, True),
                  lambda i, o, s, r: both(o, s, r, False))


def comm_call(riders, *, name):
    _, res = _call(None, riders=riders, name=name)
    return res


SEMS = pl.BlockSpec(memory_space=pltpu.SEMAPHORE)
SIDE_EFFECT = pltpu.SideEffectType.DATAFLOW_SIDE_EFFECTING


def _split_refs(riders, refs):
    views, p = [], 0
    for r in riders:
        bufs = refs[p:p + len(r.inputs) + len(r.fresh)]
        p += len(bufs)
        ins = bufs[:len(r.inputs)]
        views.append([ins, [ins[i] for i in r.aliased] + list(bufs[len(r.inputs):])])
    for view in views:
        view += [refs[p], refs[p + 1]]
        p += 2
    return views


def comm_start(riders, *, name):
    kind = _peer_kind(riders)
    bufs = [a for r in riders for a in r.inputs]
    fresh = [f for r in riders for f in r.fresh]
    n_buf, n_fresh = len(bufs), len(fresh)

    def body(*refs):
        ins, outs = refs[:n_buf], refs[n_buf:]
        through, land, sems = outs[:n_buf], outs[n_buf:n_buf + n_fresh], outs[n_buf + n_fresh:-1]
        _peer_barrier(kind)
        per_rider, pb, pf = [], 0, 0
        for r in riders:
            per_rider += list(through[pb:pb + len(r.inputs)]) + list(land[pf:pf + len(r.fresh)])
            pb, pf = pb + len(r.inputs), pf + len(r.fresh)
        for r, (r_ins, r_outs, send, recv) in zip(riders, _split_refs(riders, per_rider + list(sems))):
            r.start(r_ins, r_outs, send, recv)
        outs[-1][...] = jnp.zeros((8, LANES), F32)

    sem_shapes = [pltpu.SemaphoreType.DMA((r.nsem,)) for r in riders for _ in (0, 1)]
    res = pl.pallas_call(
        body, name=name, in_specs=[ANY] * n_buf,
        out_specs=[ANY] * (n_buf + n_fresh) + [SEMS] * len(sem_shapes) + [pl.BlockSpec(memory_space=pltpu.VMEM)],
        out_shape=[jax.ShapeDtypeStruct(a.shape, a.dtype) for a in bufs] + fresh + sem_shapes
        + [jax.ShapeDtypeStruct((8, LANES), F32)],
        input_output_aliases={i: i for i in range(n_buf)},
        compiler_params=pltpu.CompilerParams(has_side_effects=SIDE_EFFECT, collective_id=PEER_KINDS.index(kind)))(*bufs)
    return (riders, list(res[:n_buf + n_fresh]), list(res[n_buf + n_fresh:-1])), res[-1]


def comm_wait(state, after, *, name):
    riders, bufs, sems = state
    n_buf, n_sem = len(bufs), len(sems)
    n_in = sum(len(r.inputs) for r in riders)

    def body(*refs):
        held, sem_refs = refs[:n_buf], refs[n_buf:n_buf + n_sem]
        through, land = held[:n_in], held[n_in:]
        per_rider, pb, pf = [], 0, 0
        for r in riders:
            per_rider += list(through[pb:pb + len(r.inputs)]) + list(land[pf:pf + len(r.fresh)])
            pb, pf = pb + len(r.inputs), pf + len(r.fresh)
        for r, (r_ins, r_outs, send, recv) in zip(riders, _split_refs(riders, per_rider + list(sem_refs))):
            r.finish(r_ins, r_outs, send, recv)

    res = pl.pallas_call(
        body, name=name, in_specs=[ANY] * n_buf + [SEMS] * n_sem + [ANY], out_specs=[ANY] * n_buf,
        out_shape=[jax.ShapeDtypeStruct(a.shape, a.dtype) for a in bufs],
        input_output_aliases={i: i for i in range(n_buf)},
        compiler_params=pltpu.CompilerParams(has_side_effects=SIDE_EFFECT))(*bufs, *sems, after)
    through, land = list(res[:n_in]), list(res[n_in:])
    out, pb, pf = [], 0, 0
    for r in riders:
        r_ins, r_land = through[pb:pb + len(r.inputs)], land[pf:pf + len(r.fresh)]
        pb, pf = pb + len(r.inputs), pf + len(r.fresh)
        out.append([r_ins[i] for i in r.aliased] + r_land)
    return out


SLAB_ROWS = 192


def _pad_rows(a, rows=8):
    return jnp.pad(a, ((0, rows - a.shape[0]), (0, 0)))


def _pack_small(norm_grads, db_qkv, db_o, dsinks, db_sp, dln_g, dln_b, dw_sp, loss_part):
    parts = [
        jnp.concatenate(norm_grads, axis=0),
        _pad_rows(jnp.pad(db_qkv, ((0, 0), (0, 2 * D_MODEL - QKV_WIDTH))).reshape(2, D_MODEL)),
        _pad_rows(db_o),
        _pad_rows(jnp.pad(dsinks.reshape(1, N_Q_HEADS), ((0, 0), (0, D_MODEL - N_Q_HEADS)))),
        _pad_rows(db_sp.reshape(1, D_MODEL)),
        _pad_rows(jnp.concatenate([dln_g, dln_b, jnp.pad(loss_part[0:1], ((0, 0), (0, D_MODEL - LANES)))], axis=0)),
        dw_sp.reshape(SGU_CHUNK, D_MODEL),
    ]
    slab = jnp.concatenate(parts, axis=0)
    return jnp.pad(slab, ((0, SLAB_ROWS - slab.shape[0]), (0, 0))).reshape(N_CHIPS, SLAB_ROWS // N_CHIPS, D_MODEL)


def _unpack_small(slab, j):
    slab = slab.reshape(SLAB_ROWS, D_MODEL)
    norms = [slab[2 * i:2 * i + 2] for i in range(4)]
    db_qkv = slab[8:10].reshape(1, 2 * D_MODEL)[:, :QKV_WIDTH]
    db_o = slab[16:17]
    dsinks = slab[24:25, :N_Q_HEADS]
    db_sp = slab[32:33].reshape(SGU_GROUPS, SGU_CHUNK)
    width = D_MODEL // N_CHIPS
    dln_g = lax.dynamic_slice(slab[40:41], (0, j * width), (1, width))
    dln_b = lax.dynamic_slice(slab[41:42], (0, j * width), (1, width))
    dw_sp = slab[48:48 + SGU_CHUNK].reshape(SGU_GROUPS * SGU_CHUNK, SGU_CHUNK)
    return norms, db_qkv, db_o, dsinks, db_sp, dln_g, dln_b, dw_sp, slab[42, 0]


class _GradReduce:
    def __init__(self, c_arr, jc_arr, dest_shapes):
        self.c_arr, self.jc_arr, self.dest_shapes = c_arr, jc_arr, dest_shapes
        self.grad, self.sibling, self.pair, self.chips, self.dest = {}, {}, {}, {}, {}

    def exchange(self, tags):
        return exchange_rider([self.grad[t] for t in tags])

    def exchanged(self, tags, res):
        for t, r in zip(tags, res):
            self.sibling[t] = r
            self.pair[t] = pair_add(self.grad[t], r, self.c_arr, name=f"pair_add_{t}")

    def scatter(self, tags):
        return scatter_rider([self.pair[t] for t in tags])

    def scattered(self, tags, res, where):
        for t, r in zip(tags, res):
            name, lead = where[t]
            self.dest[name] = final_add(self.grad[t], self.sibling[t], r, self.jc_arr, dest_shape=self.dest_shapes[name],
                                        lead=lead, prev=self.dest.get(name), name=f"final_add_{t}")

    def broadcast(self, items):
        names = []
        for n, _ in items:
            if n not in names:
                names.append(n)
        return names, broadcast_rider([self.dest[n] for n in names], [(names.index(n), lead) for n, lead in items])

    def broadcasted(self, names, res):
        for n, r in zip(names, res):
            self.dest[n] = r


def kernel(x, norm_mix_pre, norm_mix_post, norm_ffn_pre, norm_ffn_post, attn_w_qkv, attn_b_qkv, attn_sinks, attn_w_o, attn_b_o, sgu_w_in, sgu_ln_g, sgu_ln_b, sgu_w_spatial, sgu_b_spatial, sgu_w_out, ffn_w_gate_up, ffn_w_down, loss_target, m_norm_mix_pre, m_norm_mix_post, m_norm_ffn_pre, m_norm_ffn_post, m_attn_w_qkv, m_attn_b_qkv, m_attn_sinks, m_attn_w_o, m_attn_b_o, m_sgu_w_in, m_sgu_ln_g, m_sgu_ln_b, m_sgu_w_spatial, m_sgu_b_spatial, m_sgu_w_out, m_ffn_w_gate_up, m_ffn_w_down, v_norm_mix_pre, v_norm_mix_post, v_norm_ffn_pre, v_norm_ffn_post, v_attn_w_qkv, v_attn_b_qkv, v_attn_sinks, v_attn_w_o, v_attn_b_o, v_sgu_w_in, v_sgu_ln_g, v_sgu_ln_b, v_sgu_w_spatial, v_sgu_b_spatial, v_sgu_w_out, v_ffn_w_gate_up, v_ffn_w_down):
    s = x.shape[1]
    x0 = x.reshape(s, D_MODEL)
    target = loss_target.reshape(s, D_MODEL)
    mx, my, mc = lax.axis_index("x"), lax.axis_index("y"), lax.axis_index("c")
    chip = 2 * mx + my
    chip_arr = jnp.reshape(chip, (1,)).astype(I32)
    c_arr = jnp.reshape(mc, (1,)).astype(I32)
    jc_arr = jnp.stack([chip, mc]).astype(I32)
    zero_bias = jnp.zeros((1, D_MODEL), F32)

    def gain(p, i):
        return p[i:i + 1]

    big = [attn_w_qkv, attn_w_o, sgu_w_in, sgu_w_out, ffn_w_gate_up, ffn_w_gate_up, ffn_w_down, ffn_w_down]
    layers = [0, 0, 0, 0, 0, 1, 0, 1]
    tags = ["qkv", "wo", "win", "wout", "wgu0", "wgu1", "wd0", "wd1"]
    full = {t: place_shard(w, l, chip_arr, BF16, name=f"place_{t}") for w, l, t in zip(big, layers, tags)
            if t not in ("wgu0", "wgu1")}
    ln_pack = _pad_rows(jnp.concatenate([sgu_ln_g, sgu_ln_b], axis=0), 16)[None]
    full["ln"] = place_shard(ln_pack, 0, chip_arr, F32, name="place_ln")

    def split(items):
        return [i if isinstance(i, str) else i[0] for i in items], [WHOLE if isinstance(i, str) else tuple(i[1:]) for i in items]

    def ici(*items):
        names, pieces = split(items)
        return gather_ici_rider([full[n] for n in names], pieces)

    def d2d(*items):
        names, pieces = split(items)
        return gather_d2d_rider([full[n] for n in names], pieces)

    def landed(items, res):
        for n, r in zip(split(items)[0], res):
            full[n] = r

    cos, sin = _rope_tables(s)
    sink_rows = jnp.broadcast_to(
        jnp.repeat(attn_sinks.reshape(N_KV_HEADS, GQA_GROUP), WINDOW, axis=1)[:, None, :], (N_KV_HEADS, 8, ROWS))
    w_sp = sgu_w_spatial.reshape(SGU_GROUPS, SGU_CHUNK, SGU_CHUNK)
    b_sp = jnp.broadcast_to(sgu_b_spatial.reshape(SGU_GROUPS, SGU_CHUNK)[:, :, None], (SGU_GROUPS, SGU_CHUNK, LANES))

    (h0, full["wgu0"]), (res,) = prenorm_and_place(x0, gain(norm_mix_pre, 0), ffn_w_gate_up, 0, chip_arr, name="prenorm_0",
                                                   riders=[ici("qkv", "ln")])
    landed(("qkv", "ln"), res)
    full["wgu1"], (res,) = place_shard(ffn_w_gate_up, 1, chip_arr, BF16, name="place_wgu1", riders=[d2d("qkv", "ln")])
    landed(("qkv", "ln"), res)
    ln_g = full["ln"][:, 0, :].reshape(1, D_MODEL)
    ln_b = full["ln"][:, 1, :].reshape(1, D_MODEL)

    def hosted(call, stages):
        outputs, results = call([{"ici": ici, "d2d": d2d}[kind](*items) for kind, items in stages])
        for (_, items), res in zip(stages, results):
            landed(items, res)
        return outputs

    qkv = hosted(lambda r: qkv_proj(h0, full["qkv"], attn_b_qkv, cos, sin, name="qkv_proj", riders=r),
                 [("ici", ("wo", ("wgu0", 0, 3, 8)))])
    o = hosted(lambda r: attn_fwd(qkv, sink_rows, name="attn_fwd", riders=r),
               [("d2d", ("wo",)), ("ici", (("wgu0", 3, 8, 8), ("wd0", 0, 2, 11)))])
    w_o = full["wo"].reshape(Q_WIDTH, D_MODEL)
    x1, h1, m0 = hosted(lambda r: proj_residual_norm(o, w_o, x0, attn_b_o, gain(norm_mix_post, 0), gain(norm_ffn_pre, 0),
                                                     name="attn_out_norm", riders=r),
                        [("d2d", ("wgu0",)), ("ici", (("wd0", 2, 11, 11),))])
    gu0, a0 = hosted(lambda r: ffn_up(h1, full["wgu0"], name="ffn_up_0", riders=r),
                     [("d2d", ("wd0",)), ("ici", ("win", "wout", ("wgu1", 0, 4, 8)))])
    w_d0 = full["wd0"].reshape(D_FF, D_MODEL)
    x2, h2, f0 = hosted(lambda r: proj_residual_norm(a0, w_d0, x1, zero_bias, gain(norm_ffn_post, 0), gain(norm_mix_pre, 1),
                                                     name="ffn_down_norm_0", riders=r),
                        [("d2d", ("win", "wout")), ("ici", (("wgu1", 4, 8, 8),))])
    w_in = full["win"]
    z, y = hosted(lambda r: sgu_in_fwd(h2, w_in, ln_g, ln_b, w_sp, b_sp, name="sgu_in_fwd", riders=r),
                  [("d2d", ("wgu1",)), ("ici", ("wd1",))])
    w_out = full["wout"].reshape(D_MODEL, D_MODEL)
    x3, h3, m1 = hosted(lambda r: proj_residual_norm(y, w_out, x2, zero_bias, gain(norm_mix_post, 1), gain(norm_ffn_pre, 1),
                                                     name="sgu_out_norm", riders=r),
                        [("d2d", ("wd1",))])
    w_qkv, w_gu0, w_gu1 = full["qkv"], full["wgu0"], full["wgu1"]
    w_d1 = full["wd1"].reshape(D_FF, D_MODEL)
    gu1, a1, dx4, df1, dg_fpost1, loss_part = ffn_fwd_loss_rows(
        h3, w_gu1, w_d1, x3, gain(norm_ffn_post, 1), target, name="ffn_fwd_loss_rows")

    red = _GradReduce(c_arr, jc_arr, {
        "qkv": attn_w_qkv.shape[1:], "wo": attn_w_o.shape[1:], "win": sgu_w_in.shape[1:], "wout": sgu_w_out.shape[1:],
        "wgu": ffn_w_gate_up.shape, "wd": ffn_w_down.shape, "slab": (N_CHIPS, SLAB_ROWS // N_CHIPS, D_MODEL)})
    where = {"qkv": ("qkv", None), "wo": ("wo", None), "win": ("win", None), "wout": ("wout", None), "wgu0": ("wgu", 0),
             "wgu1": ("wgu", 1), "wd0": ("wd", 0), "wd1": ("wd", 1), "small": ("slab", "chip")}

    dgu1, dx3, dm1, dg_fpre1, dg_mpost1, _ = ffn_bwd_rows(
        df1, w_d1, gu1, w_gu1, dx4, x3, gain(norm_ffn_pre, 1), m1, gain(norm_mix_post, 1), name="ffn_bwd_rows_1")
    red.grad["wd1"] = mm_tn(a1, df1, shard_major=False, tm=256, tn=D_MODEL, name="dw_down_1").reshape(
        N_CHIPS, D_FF // N_CHIPS, D_MODEL)
    red.grad["wgu1"], (res,) = mm_tn(h3, dgu1, shard_major=True, tm=512, tn=FF_HALF, name="dw_gate_up_1",
                                     riders=[red.exchange(["wd1"])])
    red.exchanged(["wd1"], res)
    dy, (res,) = mm_nt(dm1, w_out, out_dtype=F32, name="dy_sgu", riders=[red.exchange(["wgu1"])])
    red.exchanged(["wgu1"], res)
    red.grad["wout"] = mm_tn(y, dm1, shard_major=False, tm=512, tn=D_MODEL, name="dw_sgu_out").reshape(
        N_CHIPS, D_MODEL // N_CHIPS, D_MODEL)
    (dz, dw_sp, db_sp, dln_g, dln_b), (res_a, res_b) = sgu_bwd(
        z, dy, ln_g, ln_b, w_sp, b_sp, name="sgu_bwd", riders=[red.scatter(["wgu1"]), red.exchange(["wout"])])
    red.scattered(["wgu1"], res_a, where)
    red.exchanged(["wout"], res_b)
    names, rider = red.broadcast([("wgu", 1)])
    red.grad["win"], (res_a, res_b) = mm_tn(h2, dz, shard_major=True, tm=512, tn=2 * D_MODEL // N_CHIPS, name="dw_sgu_in",
                                            riders=[rider, red.scatter(["wout"])])
    red.broadcasted(names, res_a)
    red.scattered(["wout"], res_b, where)
    names, rider = red.broadcast([("wout", None)])
    (dx2, df0, dg_mpre1, dg_fpost0, _), (res_a, res_b) = dh_norm_bwd_pair(
        dz, w_in, dx3, x2, gain(norm_mix_pre, 1), f0, gain(norm_ffn_post, 0), name="dh_sgu_norm",
        riders=[red.exchange(["win"]), rider])
    red.exchanged(["win"], res_a)
    red.broadcasted(names, res_b)
    (dgu0, dx1, dm0, dg_fpre0, dg_mpost0, db_o), (res,) = ffn_bwd_rows(
        df0, w_d0, gu0, w_gu0, dx2, x1, gain(norm_ffn_pre, 0), m0, gain(norm_mix_post, 0), name="ffn_bwd_rows_0",
        riders=[red.scatter(["wd1", "win"])])
    red.scattered(["wd1", "win"], res, where)
    names, rider = red.broadcast([("wd", 1), ("win", None)])
    dw_d0, (res,) = mm_tn(a0, df0, shard_major=False, tm=256, tn=D_MODEL, name="dw_down_0", riders=[rider])
    red.broadcasted(names, res)
    red.grad["wd0"] = dw_d0.reshape(N_CHIPS, D_FF // N_CHIPS, D_MODEL)
    do, (res,) = mm_nt(dm0, w_o, out_dtype=BF16, name="do_attn", riders=[red.exchange(["wd0"])])
    red.exchanged(["wd0"], res)
    red.grad["wgu0"], (res,) = mm_tn(h1, dgu0, shard_major=True, tm=512, tn=FF_HALF, name="dw_gate_up_0",
                                     riders=[red.scatter(["wd0"])])
    red.scattered(["wd0"], res, where)
    names, rider = red.broadcast([("wd", 0)])
    dw_o, (res_a, res_b) = mm_tn(o, dm0, shard_major=False, tm=512, tn=D_MODEL, name="dw_attn_out",
                                 riders=[red.exchange(["wgu0"]), rider])
    red.exchanged(["wgu0"], res_a)
    red.broadcasted(names, res_b)
    red.grad["wo"] = dw_o.reshape(N_CHIPS, Q_WIDTH // N_CHIPS, D_MODEL)
    (dq, dkc, dkp, dvc, dvp, dsink), (res_a, res_b) = attn_bwd(
        qkv, sink_rows, do, name="attn_bwd", riders=[red.scatter(["wgu0"]), red.exchange(["wo"])])
    red.scattered(["wgu0"], res_a, where)
    red.exchanged(["wo"], res_b)
    names, rider = red.broadcast([("wgu", 0)])
    (dqkv, db_qkv), (res_a, res_b) = rope_bwd(dq, dkc, dkp, dvc, dvp, cos, sin, name="rope_bwd",
                                              riders=[rider, red.scatter(["wo"])])
    red.broadcasted(names, res_a)
    red.scattered(["wo"], res_b, where)
    names, rider = red.broadcast([("wo", None)])
    red.grad["qkv"], (res,) = mm_tn(h0, dqkv, shard_major=True, tm=512, tn=QKV_WIDTH // N_CHIPS, name="dw_qkv",
                                    riders=[rider])
    red.broadcasted(names, res)
    grad_x, dg_mpre0 = dh_norm_bwd_last(dqkv, w_qkv, dx1, x0, gain(norm_mix_pre, 0), name="dh_attn_norm_in")

    norm_grads = [jnp.concatenate(p, axis=0) for p in
                  ((dg_mpre0, dg_mpre1), (dg_mpost0, dg_mpost1), (dg_fpre0, dg_fpre1), (dg_fpost0, dg_fpost1))]
    red.grad["small"] = _pack_small(norm_grads, db_qkv, db_o, dsink[:, :, 0, 0], db_sp[:, :, 0], dln_g, dln_b, dw_sp,
                                    loss_part)
    def big_update(w, g, m, v, tag, after=None):
        return adamw(w, g.reshape(w.shape), m, v, name=f"adamw_{tag}", after=after)

    (res,) = comm_call([red.exchange(["qkv", "small"])], name="tail_1")
    red.exchanged(["qkv", "small"], res)
    state, token = comm_start([red.scatter(["qkv", "small"])], name="tail_2_start")
    upd_wgu = big_update(ffn_w_gate_up, red.dest["wgu"], m_ffn_w_gate_up, v_ffn_w_gate_up, "wgu", after=token)
    (res,) = comm_wait(state, upd_wgu[1], name="tail_2_wait")
    red.scattered(["qkv", "small"], res, where)
    names, rider = red.broadcast([("qkv", None)])
    state, token = comm_start([rider, allcast_rider(red.dest["slab"])], name="tail_3_start")
    upd_wd = big_update(ffn_w_down, red.dest["wd"], m_ffn_w_down, v_ffn_w_down, "wd", after=token)
    (res_a,), (slab_full,) = comm_wait(state, upd_wd[1], name="tail_3_wait")
    red.broadcasted(names, [res_a])
    g_qkv, g_wo, g_win, g_wout = (red.dest[n] for n in ("qkv", "wo", "win", "wout"))
    g_norms, g_bqkv, g_bo, g_sinks, g_bsp, g_lng, g_lnb, g_wsp, loss = _unpack_small(slab_full, chip)

    upd = {
        "attn_w_qkv": big_update(attn_w_qkv, g_qkv, m_attn_w_qkv, v_attn_w_qkv, "qkv"),
        "attn_w_o": big_update(attn_w_o, g_wo, m_attn_w_o, v_attn_w_o, "wo"),
        "sgu_w_in": big_update(sgu_w_in, g_win, m_sgu_w_in, v_sgu_w_in, "win"),
        "sgu_w_out": big_update(sgu_w_out, g_wout, m_sgu_w_out, v_sgu_w_out, "wout"),
        "ffn_w_gate_up": upd_wgu,
        "ffn_w_down": upd_wd,
    }
    small_names = ["norm_mix_pre", "norm_mix_post", "norm_ffn_pre", "norm_ffn_post", "attn_b_qkv", "attn_sinks", "attn_b_o",
                   "sgu_ln_g", "sgu_ln_b", "sgu_w_spatial", "sgu_b_spatial"]
    small_w = [norm_mix_pre, norm_mix_post, norm_ffn_pre, norm_ffn_post, attn_b_qkv, attn_sinks, attn_b_o, sgu_ln_g, sgu_ln_b,
               sgu_w_spatial, sgu_b_spatial]
    small_m = [m_norm_mix_pre, m_norm_mix_post, m_norm_ffn_pre, m_norm_ffn_post, m_attn_b_qkv, m_attn_sinks, m_attn_b_o,
               m_sgu_ln_g, m_sgu_ln_b, m_sgu_w_spatial, m_sgu_b_spatial]
    small_v = [v_norm_mix_pre, v_norm_mix_post, v_norm_ffn_pre, v_norm_ffn_post, v_attn_b_qkv, v_attn_sinks, v_attn_b_o,
               v_sgu_ln_g, v_sgu_ln_b, v_sgu_w_spatial, v_sgu_b_spatial]
    small_g = g_norms + [g_bqkv, g_sinks, g_bo, g_lng, g_lnb, g_wsp, g_bsp]

    def flat2(a):
        return a.reshape(-1, a.shape[-1])

    res = adamw_small([flat2(a) for a in small_w], [flat2(a) for a in small_g], [flat2(a) for a in small_m],
                      [flat2(a) for a in small_v], name="adamw_small")
    for i, nm in enumerate(small_names):
        upd[nm] = tuple(r[i].reshape(small_w[i].shape) for r in res)

    order = ["norm_mix_pre", "norm_mix_post", "norm_ffn_pre", "norm_ffn_post", "attn_w_qkv", "attn_b_qkv", "attn_sinks",
             "attn_w_o", "attn_b_o", "sgu_w_in", "sgu_ln_g", "sgu_ln_b", "sgu_w_spatial", "sgu_b_spatial", "sgu_w_out",
             "ffn_w_gate_up", "ffn_w_down"]
    outs = [loss, grad_x.reshape(1, s, D_MODEL)]
    for part in range(4):
        outs += [upd[nm][part] for nm in order]
    return tuple(outs)
```

```python
import types

import numpy as np
import jax
import jax.numpy as jnp
from jax import lax
from jax.experimental import pallas as pl
from jax.experimental.pallas import tpu as pltpu

F32 = jnp.float32
BF16 = jnp.bfloat16
I32 = jnp.int32

D_MODEL = 1024
HEAD_DIM = 64
N_Q_HEADS = 16
N_KV_HEADS = 4
GQA_GROUP = 4
WINDOW = 128
Q_WIDTH = 1024
KV_WIDTH = 256
QKV_WIDTH = 1536
ROPE_THETA = 10000.0
SGU_GROUPS = 8
SGU_CHUNK = 128
D_FF = 2816
FF_HALF = D_FF // 2
EPS = 1e-6
N_CHIPS = 4
LANES = 128

ADAM_LR = 0.001
ADAM_B1 = 0.9
ADAM_B2 = 0.999
ADAM_EPS = 1e-08
ADAM_WD = 0.01
ADAM_STEP = 10

VMEM_LIMIT = 52 * 1024 * 1024
MESH = pl.DeviceIdType.MESH
NEG = -1e30
NT_DIMS = (((1,), (1,)), ((), ()))
TN_DIMS = (((0,), (0,)), ((), ()))
NN_DIMS = (((1,), (0,)), ((), ()))
ANY = pl.BlockSpec(memory_space=pl.ANY)


def _row_tile(s, want):
    return want if s % want == 0 else s


PEER_KINDS = ("sibling", "chips", "sibling+chips", "everyone")


def _peer_kind(riders):
    kinds = {r.peers for r in riders}
    if not kinds:
        return None
    if "everyone" in kinds:
        return "everyone"
    return "sibling+chips" if len(kinds) == 2 else kinds.pop()


def _peer_barrier(kind):
    x, y, c = _place()
    chips = [(*_partner(x, y, k), c) for k in (1, 2, 3)]
    peers = {"sibling": [(x, y, 1 - c)], "chips": chips, "sibling+chips": [(x, y, 1 - c)] + chips,
             "everyone": [(x, y, 1 - c)] + chips + [(px, py, 1 - c) for px, py, _ in chips]}[kind]
    barrier = pltpu.get_barrier_semaphore()
    for dev in peers:
        pl.semaphore_signal(barrier, inc=1, device_id=dev, device_id_type=MESH)
    pl.semaphore_wait(barrier, len(peers))


def _call(body, *, name, grid=(), in_specs=(), out_specs=(), out_shape=(), scratch_shapes=(), operands=(), prefetch=(),
          aliases=None, riders=(), sem=None):
    n_pre, n_in, n_out, n_scr = len(prefetch), len(operands), len(out_shape), len(scratch_shapes)
    in_specs, out_specs, out_shape = list(in_specs), list(out_specs), list(out_shape)
    operands, scratch_shapes = list(operands), list(scratch_shapes)
    io_alias = {n_pre + i: o for i, o in (aliases or {}).items()}
    for r in riders:
        base_in, base_out = n_pre + len(operands), len(out_shape)
        operands += list(r.inputs)
        in_specs += [ANY] * len(r.inputs)
        for pos, i in enumerate(r.aliased):
            io_alias[base_in + i] = base_out + pos
            out_shape.append(jax.ShapeDtypeStruct(r.inputs[i].shape, r.inputs[i].dtype))
        out_shape += list(r.fresh)
        out_specs += [ANY] * (len(r.aliased) + len(r.fresh))
        scratch_shapes += [pltpu.SemaphoreType.DMA((r.nsem,)), pltpu.SemaphoreType.DMA((r.nsem,))]

    def wrapped(*refs):
        pre, p = refs[:n_pre], n_pre
        core_in, p = refs[p:p + n_in], p + n_in
        r_in = []
        for r in riders:
            r_in.append(refs[p:p + len(r.inputs)])
            p += len(r.inputs)
        core_out, p = refs[p:p + n_out], p + n_out
        r_out = []
        for r in riders:
            k = len(r.aliased) + len(r.fresh)
            r_out.append(refs[p:p + k])
            p += k
        core_scr, p = refs[p:p + n_scr], p + n_scr
        r_sem = [refs[p + 2 * i:p + 2 * i + 2] for i in range(len(riders))]

        def edge(at_last, fns):
            def run():
                if not at_last:
                    _peer_barrier(peer_kind)
                for i, r in enumerate(riders):
                    getattr(r, fns)(r_in[i], r_out[i], r_sem[i][0], r_sem[i][1])
            if not riders:
                return
            if not grid:
                run()
                return
            cond = None
            for d, n in enumerate(grid):
                c = pl.program_id(d) == (n - 1 if at_last else 0)
                cond = c if cond is None else jnp.logical_and(cond, c)
            pl.when(cond)(run)

        edge(False, "start")
        if body is not None:
            body(*pre, *core_in, *core_out, *core_scr)
        edge(True, "finish")

    if sem is None or riders:
        sem = ("arbitrary",) * len(grid)
    kwargs = dict(out_shape=out_shape, input_output_aliases=io_alias, name=name)
    peer_kind = _peer_kind(riders)
    collective = {} if peer_kind is None else {"collective_id": PEER_KINDS.index(peer_kind)}
    if grid:
        kwargs["compiler_params"] = pltpu.CompilerParams(dimension_semantics=sem, vmem_limit_bytes=VMEM_LIMIT, **collective)
    elif collective:
        kwargs["compiler_params"] = pltpu.CompilerParams(**collective)
    if n_pre:
        kwargs["grid_spec"] = pltpu.PrefetchScalarGridSpec(
            num_scalar_prefetch=n_pre, grid=grid, in_specs=in_specs, out_specs=out_specs, scratch_shapes=scratch_shapes)
    else:
        kwargs.update(grid=grid, in_specs=in_specs, out_specs=out_specs, scratch_shapes=scratch_shapes)
    res = pl.pallas_call(wrapped, **kwargs)(*prefetch, *operands)
    core, rest, rider_res = list(res[:n_out]), list(res[n_out:]), []
    for r in riders:
        k = len(r.aliased) + len(r.fresh)
        rider_res.append(rest[:k])
        rest = rest[k:]
    return core, rider_res


def _mm_call(*, grid, in_specs, out_spec, out_shape, dims, nk, kaxis, acc_shape, name, operands, riders=()):
    out_dtype = out_shape.dtype

    def body(a_ref, b_ref, o_ref, *scratch):
        p = lax.dot_general(a_ref[...].astype(BF16), b_ref[...].astype(BF16), dims, preferred_element_type=F32)
        if nk == 1:
            o_ref[...] = p.astype(out_dtype)
        else:
            acc = scratch[0]
            kk = pl.program_id(kaxis)

            @pl.when(kk == 0)
            def _():
                acc[...] = p

            @pl.when(kk > 0)
            def _():
                acc[...] += p

            @pl.when(kk == nk - 1)
            def _():
                o_ref[...] = acc[...].astype(out_dtype)

    sem = ["parallel"] * len(grid)
    if nk > 1:
        sem[kaxis] = "arbitrary"
    (out,), rider_res = _call(
        body, grid=grid, in_specs=in_specs, out_specs=[out_spec], out_shape=[out_shape],
        scratch_shapes=[pltpu.VMEM(acc_shape, F32)] if nk > 1 else [], operands=operands, name=name, riders=riders,
        sem=tuple(sem))
    return (out, rider_res) if riders else out


def mm_nt(a, w, *, out_dtype, name, tm=1024, riders=()):
    m, n = a.shape
    kout = w.shape[0]
    tm = _row_tile(m, tm)
    return _mm_call(grid=(m // tm,),
                    in_specs=[pl.BlockSpec((tm, n), lambda i: (i, 0)), pl.BlockSpec((kout, n), lambda i: (0, 0))],
                    out_spec=pl.BlockSpec((tm, kout), lambda i: (i, 0)),
                    out_shape=jax.ShapeDtypeStruct((m, kout), out_dtype), dims=NT_DIMS, nk=1, kaxis=0,
                    acc_shape=None, name=name, operands=(a, w), riders=riders)


def mm_tn(a, b, *, shard_major, name, tm, tn, tk=None, out_dtype=BF16, riders=()):
    s, m = a.shape
    tk = s if tk is None else _row_tile(s, tk)
    if b.ndim == 3:
        n = 2 * b.shape[2]
        b_spec = pl.BlockSpec((None, tk, tn), lambda j, i, kk: (j // 2, kk, j % 2))
    else:
        n = b.shape[1]
        b_spec = pl.BlockSpec((tk, tn), lambda j, i, kk: (kk, j))
    if shard_major:
        assert tn == n // N_CHIPS
        o_spec = pl.BlockSpec((None, tm, tn), lambda j, i, kk: (j, i, 0))
        o_shape = jax.ShapeDtypeStruct((N_CHIPS, m, tn), out_dtype)
    else:
        o_spec = pl.BlockSpec((tm, tn), lambda j, i, kk: (i, j))
        o_shape = jax.ShapeDtypeStruct((m, n), out_dtype)
    return _mm_call(grid=(n // tn, m // tm, s // tk),
                    in_specs=[pl.BlockSpec((tk, tm), lambda j, i, kk: (kk, i)), b_spec], out_spec=o_spec,
                    out_shape=o_shape, dims=TN_DIMS, nk=s // tk, kaxis=2, acc_shape=(tm, tn), name=name, operands=(a, b),
                    riders=riders)


def _rstd(x):
    return lax.rsqrt(jnp.mean(x * x, axis=-1, keepdims=True) + EPS)


def _rms_bwd(dy, x, g):
    r = _rstd(x)
    xhat = x * r
    gy = dy * g
    dx = r * (gy - xhat * jnp.mean(gy * xhat, axis=-1, keepdims=True))
    return dx, jnp.sum(dy * xhat, axis=0, keepdims=True)


def _accum(ref, val, first):
    @pl.when(first)
    def _():
        ref[...] = val

    @pl.when(jnp.logical_not(first))
    def _():
        ref[...] += val


def _row_spec(tm, width):
    return pl.BlockSpec((tm, width), lambda i: (i, 0))


def _vec_spec(width):
    return pl.BlockSpec((1, width), lambda i: (0, 0))


def _ret(core, rider_res, riders):
    core = core[0] if len(core) == 1 else core
    return (core, rider_res) if riders else core


def prenorm_and_place(x, g, w, layer, chip_arr, *, name, tm=256, riders=()):
    s = x.shape[0]
    tm = _row_tile(s, tm)
    steps = s // tm
    _, r, c = w.shape
    tr = r // steps
    assert tr * steps == r and tr % 16 == 0

    def body(chip_ref, x_ref, g_ref, w_ref, h_ref, o_ref):
        xv = x_ref[...]
        h_ref[...] = (xv * _rstd(xv) * g_ref[...]).astype(BF16)
        o_ref[...] = w_ref[...].astype(BF16)

    core, rr = _call(
        body, grid=(steps,), prefetch=(chip_arr,),
        in_specs=[pl.BlockSpec((tm, D_MODEL), lambda i, chip: (i, 0)), pl.BlockSpec((1, D_MODEL), lambda i, chip: (0, 0)),
                  pl.BlockSpec((None, tr, c), lambda i, chip: (layer, i, 0))],
        out_specs=[pl.BlockSpec((tm, D_MODEL), lambda i, chip: (i, 0)), pl.BlockSpec((None, tr, c), lambda i, chip: (chip[0], i, 0))],
        out_shape=[jax.ShapeDtypeStruct((s, D_MODEL), BF16), jax.ShapeDtypeStruct((N_CHIPS, r, c), BF16)],
        operands=(x, g, w), sem=("parallel",), name=name, riders=riders)
    return _ret(core, rr, riders)


def proj_residual_norm(a, w, x, bias, g_post, g_next, *, name, tm=512, sub=256, riders=()):
    s, k = a.shape
    tm = _row_tile(s, tm)
    sub = min(sub, tm)

    def body(a_ref, w_ref, x_ref, b_ref, gp_ref, gn_ref, xo_ref, h_ref, m_ref):
        for t in range(tm // sub):
            rows = slice(t * sub, (t + 1) * sub)
            mv = jnp.dot(a_ref[rows, :], w_ref[...], preferred_element_type=F32) + b_ref[...]
            m_ref[rows, :] = mv.astype(BF16)
            xn = x_ref[rows, :] + mv * _rstd(mv) * gp_ref[...]
            xo_ref[rows, :] = xn
            h_ref[rows, :] = (xn * _rstd(xn) * gn_ref[...]).astype(BF16)

    row, vec = _row_spec(tm, D_MODEL), _vec_spec(D_MODEL)
    core, rr = _call(
        body, grid=(s // tm,),
        in_specs=[_row_spec(tm, k), pl.BlockSpec((k, D_MODEL), lambda i: (0, 0)), row, vec, vec, vec], out_specs=[row, row, row],
        out_shape=[jax.ShapeDtypeStruct((s, D_MODEL), F32), jax.ShapeDtypeStruct((s, D_MODEL), BF16),
                   jax.ShapeDtypeStruct((s, D_MODEL), BF16)],
        operands=(a, w, x, bias, g_post, g_next), sem=("parallel",), name=name, riders=riders)
    return _ret(core, rr, riders)


def ffn_fwd_loss_rows(h, w_gu, w_d, x, g_post, target, *, name, tm=256, riders=()):
    s = x.shape[0]
    tm = _row_tile(s, tm)

    def body(h_ref, w0, w1, w2, w3, wd_ref, x_ref, g_ref, t_ref, d_ref, a_ref, dx_ref, df_ref, dg_ref, loss_ref):
        first = pl.program_id(0) == 0
        hv = h_ref[...]
        fv = None
        for half, (wg_ref, wu_ref) in enumerate(((w0, w2), (w1, w3))):
            cols = slice(half * FF_HALF, (half + 1) * FF_HALF)
            g = jnp.dot(hv, wg_ref[...], preferred_element_type=F32)
            u = jnp.dot(hv, wu_ref[...], preferred_element_type=F32)
            sig = _sigmoid(g)
            silu = g * sig
            d_ref[0, :, cols] = (u * (sig + silu * (1.0 - sig))).astype(BF16)
            d_ref[1, :, cols] = silu.astype(BF16)
            act = (silu * u).astype(BF16)
            a_ref[:, cols] = act
            p = jnp.dot(act, wd_ref[cols, :], preferred_element_type=F32)
            fv = p if fv is None else fv + p
        gain = g_ref[...]
        err = x_ref[...] + fv * _rstd(fv) * gain - t_ref[...]
        dx = err * (1.0 / D_MODEL)
        dx_ref[...] = dx
        df, dg = _rms_bwd(dx, fv, gain)
        df_ref[...] = df.astype(BF16)
        _accum(dg_ref, dg, first)
        part = jnp.sum(jnp.sum(err * err, axis=-1, keepdims=True), axis=0, keepdims=True) * (0.5 / D_MODEL)
        _accum(loss_ref, jnp.broadcast_to(part, (8, LANES)), first)

    def resident(shape, index):
        return pl.BlockSpec(shape, index, pipeline_mode=pl.Buffered(1))

    row, vec = _row_spec(tm, D_MODEL), _vec_spec(D_MODEL)
    shards = [resident((None, D_MODEL, FF_HALF), (lambda j: (lambda i: (j, 0, 0)))(j)) for j in range(N_CHIPS)]
    core, rr = _call(
        body, grid=(s // tm,),
        in_specs=[row] + shards + [resident((D_FF, D_MODEL), lambda i: (0, 0)), row, vec, row],
        out_specs=[pl.BlockSpec((2, tm, D_FF), lambda i: (0, i, 0)), _row_spec(tm, D_FF), row, row, vec,
                   pl.BlockSpec((8, LANES), lambda i: (0, 0))],
        out_shape=[jax.ShapeDtypeStruct((2, s, D_FF), BF16), jax.ShapeDtypeStruct((s, D_FF), BF16),
                   jax.ShapeDtypeStruct((s, D_MODEL), F32), jax.ShapeDtypeStruct((s, D_MODEL), BF16),
                   jax.ShapeDtypeStruct((1, D_MODEL), F32), jax.ShapeDtypeStruct((8, LANES), F32)],
        operands=(h, w_gu, w_gu, w_gu, w_gu, w_d, x, g_post, target), name=name, riders=riders)
    return _ret(core, rr, riders)


def dh_norm_bwd_pair(a, w, dres, x, g_pre, m, g_post, *, name, tm=512, sub=256, riders=()):
    _, kout, ns = w.shape
    planes = a.ndim == 3
    s = x.shape[0]
    tm = _row_tile(s, tm)
    sub = min(sub, tm)
    a_spec = pl.BlockSpec((2, tm, 2 * ns), lambda i: (0, i, 0)) if planes else pl.BlockSpec((tm, N_CHIPS * ns), lambda i: (i, 0))

    def body(a_ref, w0, w1, w2, w3, dres_ref, x_ref, gpre_ref, m_ref, gpost_ref, dx_ref, dm_ref, dgpre_ref, dgpost_ref, db_ref):
        first = pl.program_id(0) == 0
        sums = None
        for t in range(tm // sub):
            rows = slice(t * sub, (t + 1) * sub)
            dh = None
            for j, w_ref in enumerate((w0, w1, w2, w3)):
                a_j = a_ref[j // 2, rows, (j % 2) * ns:(j % 2 + 1) * ns] if planes else a_ref[rows, j * ns:(j + 1) * ns]
                p = lax.dot_general(a_j, w_ref[...], NT_DIMS, preferred_element_type=F32)
                dh = p if dh is None else dh + p
            d1, dgpre = _rms_bwd(dh, x_ref[rows, :], gpre_ref[...])
            dx = dres_ref[rows, :] + d1
            dx_ref[rows, :] = dx
            dm, dgpost = _rms_bwd(dx, m_ref[rows, :].astype(F32), gpost_ref[...])
            dm_ref[rows, :] = dm.astype(BF16)
            part = (dgpre, dgpost, jnp.sum(dm, axis=0, keepdims=True))
            sums = part if sums is None else tuple(u + v for u, v in zip(sums, part))
        _accum(dgpre_ref, sums[0], first)
        _accum(dgpost_ref, sums[1], first)
        _accum(db_ref, sums[2], first)

    def shard(j):
        return pl.BlockSpec((None, kout, ns), lambda i: (j, 0, 0))

    row, vec = _row_spec(tm, D_MODEL), _vec_spec(D_MODEL)
    vshape = jax.ShapeDtypeStruct((1, D_MODEL), F32)
    core, rr = _call(
        body, grid=(s // tm,), in_specs=[a_spec] + [shard(j) for j in range(N_CHIPS)] + [row, row, vec, row, vec],
        out_specs=[row, row, vec, vec, vec],
        out_shape=[jax.ShapeDtypeStruct((s, D_MODEL), F32), jax.ShapeDtypeStruct((s, D_MODEL), BF16), vshape, vshape, vshape],
        operands=(a, w, w, w, w, dres, x, g_pre, m, g_post), name=name, riders=riders)
    return _ret(core, rr, riders)


def ffn_bwd_rows(df, w_d, d_planes, w_gu, dres, x, g_pre, m, g_post, *, name, tm=256, riders=()):
    s = x.shape[0]
    tm = _row_tile(s, tm)

    def body(df_ref, wd_ref, d_ref, w0, w1, w2, w3, dres_ref, x_ref, gpre_ref, m_ref, gpost_ref,
             o_ref, dx_ref, dm_ref, dgpre_ref, dgpost_ref, db_ref):
        first = pl.program_id(0) == 0
        dfv = df_ref[...]
        dh = None
        for half, (wg_ref, wu_ref) in enumerate(((w0, w2), (w1, w3))):
            cols = slice(half * FF_HALF, (half + 1) * FF_HALF)
            da = lax.dot_general(dfv, wd_ref[cols, :], NT_DIMS, preferred_element_type=F32)
            dg = (da * d_ref[0, :, cols].astype(F32)).astype(BF16)
            du = (da * d_ref[1, :, cols].astype(F32)).astype(BF16)
            o_ref[0, :, cols] = dg
            o_ref[1, :, cols] = du
            p = lax.dot_general(dg, wg_ref[...], NT_DIMS, preferred_element_type=F32)
            p += lax.dot_general(du, wu_ref[...], NT_DIMS, preferred_element_type=F32)
            dh = p if dh is None else dh + p
        d1, dgpre = _rms_bwd(dh, x_ref[...], gpre_ref[...])
        dx = dres_ref[...] + d1
        dx_ref[...] = dx
        dm, dgpost = _rms_bwd(dx, m_ref[...].astype(F32), gpost_ref[...])
        dm_ref[...] = dm.astype(BF16)
        _accum(dgpre_ref, dgpre, first)
        _accum(dgpost_ref, dgpost, first)
        _accum(db_ref, jnp.sum(dm, axis=0, keepdims=True), first)

    def resident(shape, index):
        return pl.BlockSpec(shape, index, pipeline_mode=pl.Buffered(1))

    planes = pl.BlockSpec((2, tm, D_FF), lambda i: (0, i, 0))
    row, vec = _row_spec(tm, D_MODEL), _vec_spec(D_MODEL)
    vshape = jax.ShapeDtypeStruct((1, D_MODEL), F32)
    shards = [resident((None, D_MODEL, FF_HALF), (lambda j: (lambda i: (j, 0, 0)))(j)) for j in range(N_CHIPS)]
    core, rr = _call(
        body, grid=(s // tm,),
        in_specs=[row, resident((D_FF, D_MODEL), lambda i: (0, 0)), planes] + shards + [row, row, vec, row, vec],
        out_specs=[planes, row, row, vec, vec, vec],
        out_shape=[jax.ShapeDtypeStruct((2, s, D_FF), BF16), jax.ShapeDtypeStruct((s, D_MODEL), F32),
                   jax.ShapeDtypeStruct((s, D_MODEL), BF16), vshape, vshape, vshape],
        operands=(df, w_d, d_planes, w_gu, w_gu, w_gu, w_gu, dres, x, g_pre, m, g_post), name=name, riders=riders)
    return _ret(core, rr, riders)


def dh_norm_bwd_last(a, w, dres, x, g_pre, *, name, tm=512, sub=256):
    _, kout, ns = w.shape
    s = x.shape[0]
    tm = _row_tile(s, tm)
    sub = min(sub, tm)

    def body(a_ref, w0, w1, w2, w3, dres_ref, x_ref, g_ref, dx_ref, dg_ref):
        total = None
        for t in range(tm // sub):
            rows = slice(t * sub, (t + 1) * sub)
            dh = None
            for j, w_ref in enumerate((w0, w1, w2, w3)):
                p = lax.dot_general(a_ref[rows, j * ns:(j + 1) * ns], w_ref[...], NT_DIMS, preferred_element_type=F32)
                dh = p if dh is None else dh + p
            d1, dg = _rms_bwd(dh, x_ref[rows, :], g_ref[...])
            dx_ref[rows, :] = dres_ref[rows, :] + d1
            total = dg if total is None else total + dg
        _accum(dg_ref, total, pl.program_id(0) == 0)

    def shard(j):
        return pl.BlockSpec((None, kout, ns), lambda i: (j, 0, 0))

    row, vec = _row_spec(tm, D_MODEL), _vec_spec(D_MODEL)
    (dx, dg), _ = _call(
        body, grid=(s // tm,), in_specs=[_row_spec(tm, N_CHIPS * ns)] + [shard(j) for j in range(N_CHIPS)] + [row, row, vec],
        out_specs=[row, vec], out_shape=[jax.ShapeDtypeStruct((s, D_MODEL), F32), jax.ShapeDtypeStruct((1, D_MODEL), F32)],
        operands=(a, w, w, w, w, dres, x, g_pre), name=name)
    return dx, dg


def _rope_tables(s):
    half = HEAD_DIM // 2
    inv_freq = np.float32(ROPE_THETA) ** (-(np.arange(half, dtype=np.float32) * np.float32(2.0)) / np.float32(HEAD_DIM))
    ang = np.arange(s, dtype=np.float32)[:, None] * inv_freq[None, :]
    cos, sin = np.cos(ang).astype(np.float32), np.sin(ang).astype(np.float32)
    return jnp.asarray(np.tile(cos, (1, 4))), jnp.asarray(np.concatenate([-sin, sin, -sin, sin], axis=1))


def _swap_halves(x):
    lane = lax.broadcasted_iota(I32, x.shape, 1)
    return jnp.where((lane & (HEAD_DIM - 1)) < HEAD_DIM // 2, pltpu.roll(x, LANES - 32, 1), pltpu.roll(x, 32, 1))


N_ROPE_BLOCKS = (Q_WIDTH + KV_WIDTH) // LANES


def qkv_proj(h, w, bias, cos, sin, *, name, tm=1024, riders=()):
    s, k = h.shape
    ns = w.shape[2]
    tm = _row_tile(s, tm)

    def body(h_ref, w_ref, b_ref, c_ref, s_ref, o_ref):
        j = pl.program_id(0)
        sub = min(256, tm)
        for t in range(tm // sub):
            rows = slice(t * sub, (t + 1) * sub)
            p = jnp.dot(h_ref[rows, :], w_ref[...], preferred_element_type=F32) + b_ref[...]
            cosv, sinv = c_ref[rows, :], s_ref[rows, :]
            for blk in range(ns // LANES):
                xb = p[:, blk * LANES:(blk + 1) * LANES]
                roped = xb * cosv + _swap_halves(xb) * sinv
                is_qk = j * (ns // LANES) + blk < N_ROPE_BLOCKS
                o_ref[rows, blk * LANES:(blk + 1) * LANES] = jnp.where(is_qk, roped, xb).astype(BF16)

    core, rr = _call(
        body, grid=(N_CHIPS, s // tm),
        in_specs=[pl.BlockSpec((tm, k), lambda j, i: (i, 0)), pl.BlockSpec((None, k, ns), lambda j, i: (j, 0, 0)),
                  pl.BlockSpec((1, ns), lambda j, i: (0, j)), pl.BlockSpec((tm, LANES), lambda j, i: (i, 0)),
                  pl.BlockSpec((tm, LANES), lambda j, i: (i, 0))],
        out_specs=[pl.BlockSpec((tm, ns), lambda j, i: (i, j))], out_shape=[jax.ShapeDtypeStruct((s, N_CHIPS * ns), BF16)],
        operands=(h, w, bias, cos, sin), sem=("parallel", "parallel"), name=name, riders=riders)
    return _ret(core, rr, riders)


def rope_bwd(dq, dkc, dkp, dvc, dvp, cos, sin, *, name, riders=()):
    s = dq.shape[0]
    tm = 2 * WINDOW if s % (2 * WINDOW) == 0 else WINDOW
    nb = s // tm

    def body(dq_ref, dkc_ref, dkp_ref, dkp_next_ref, dvc_ref, dvp_ref, dvp_next_ref, c_ref, s_ref, o_ref, db_ref):
        i = pl.program_id(0)
        has_next = (i < nb - 1).astype(F32)
        cosv, sinv = c_ref[...], s_ref[...]

        def shifted(ref, next_ref, cols):
            last = has_next * next_ref[:WINDOW, cols].astype(F32)
            return last if tm == WINDOW else jnp.concatenate([ref[WINDOW:, cols].astype(F32), last], axis=0)

        parts = []
        for blk in range(QKV_WIDTH // LANES):
            if blk < Q_WIDTH // LANES:
                g = dq_ref[:, blk * LANES:(blk + 1) * LANES].astype(F32)
            else:
                own, prv, nxt = (dkc_ref, dkp_ref, dkp_next_ref) if blk < N_ROPE_BLOCKS else (dvc_ref, dvp_ref, dvp_next_ref)
                cols = slice((blk % 2) * LANES, (blk % 2 + 1) * LANES)
                g = own[:, cols].astype(F32) + shifted(prv, nxt, cols)
            if blk < N_ROPE_BLOCKS:
                g = g * cosv + _swap_halves(g * sinv)
            o_ref[:, blk * LANES:(blk + 1) * LANES] = g.astype(BF16)
            parts.append(jnp.sum(g, axis=0, keepdims=True))
        sums = jnp.concatenate(parts, axis=1)
        _accum(db_ref, sums, i == 0)

    own_spec = _row_spec(tm, KV_WIDTH)
    next_spec = pl.BlockSpec((tm, KV_WIDTH), lambda i: (jnp.minimum(i + 1, nb - 1), 0))
    core, rr = _call(
        body, grid=(nb,),
        in_specs=[_row_spec(tm, Q_WIDTH), own_spec, own_spec, next_spec, own_spec, own_spec, next_spec,
                  _row_spec(tm, LANES), _row_spec(tm, LANES)],
        out_specs=[_row_spec(tm, QKV_WIDTH), _vec_spec(QKV_WIDTH)],
        out_shape=[jax.ShapeDtypeStruct((s, QKV_WIDTH), BF16), jax.ShapeDtypeStruct((1, QKV_WIDTH), F32)],
        operands=(dq, dkc, dkp, dkp, dvc, dvp, dvp, cos, sin), name=name, riders=riders)
    return _ret(core, rr, riders)


ROWS = GQA_GROUP * WINDOW


def _prev_slots():
    kpos = lax.broadcasted_iota(I32, (WINDOW, ROWS), 0)
    qpos = lax.broadcasted_iota(I32, (WINDOW, ROWS), 1) & (WINDOW - 1)
    return kpos > qpos


def _head_cols(ref, head):
    return ref[:, head * HEAD_DIM:(head + 1) * HEAD_DIM]


def _stack_heads(ref, h):
    return jnp.concatenate([_head_cols(ref, GQA_GROUP * h + g) for g in range(GQA_GROUP)], axis=0)


def _band(prev_ref, cur_ref, h):
    return jnp.concatenate([_head_cols(prev_ref, h), _head_cols(cur_ref, h)], axis=0)


def _pick(prev, band):
    return jnp.where(prev, band[:WINDOW], band[WINDOW:])


def _spread(prev, x):
    return jnp.concatenate([jnp.where(prev, x, 0.0), jnp.where(prev, 0.0, x)], axis=0).astype(BF16)


def _attn_probs(s_band, sink, prev, has_prev):
    scale = HEAD_DIM ** -0.5
    s = jnp.where(prev, jnp.where(has_prev, s_band[:WINDOW], NEG), s_band[WINDOW:]) * scale
    m = jnp.maximum(jnp.max(s, axis=0, keepdims=True), sink)
    e, es = jnp.exp(s - m), jnp.exp(sink - m)
    inv = 1.0 / (jnp.sum(e, axis=0, keepdims=True) + es)
    return e * inv, es * inv


def _attn_specs(nb):
    kcol, vcol = Q_WIDTH // KV_WIDTH, Q_WIDTH // KV_WIDTH + 1
    q_spec = pl.BlockSpec((WINDOW, Q_WIDTH), lambda n: (n, 0))
    return [q_spec,
            pl.BlockSpec((WINDOW, KV_WIDTH), lambda n: (n, kcol)),
            pl.BlockSpec((WINDOW, KV_WIDTH), lambda n: (jnp.maximum(n - 1, 0), kcol)),
            pl.BlockSpec((WINDOW, KV_WIDTH), lambda n: (n, vcol)),
            pl.BlockSpec((WINDOW, KV_WIDTH), lambda n: (jnp.maximum(n - 1, 0), vcol)),
            pl.BlockSpec((N_KV_HEADS, 8, ROWS), lambda n: (0, 0, 0))]


def attn_fwd(qkv, sink_rows, *, name, riders=()):
    s = qkv.shape[0]

    def body(q_ref, kc_ref, kp_ref, vc_ref, vp_ref, sink_ref, o_ref):
        prev = _prev_slots()
        has_prev = pl.program_id(0) > 0
        heads = range(N_KV_HEADS)
        s_bands = [lax.dot_general(_band(kp_ref, kc_ref, h), _stack_heads(q_ref, h), NT_DIMS, preferred_element_type=F32)
                   for h in heads]
        p_bands = [_spread(prev, _attn_probs(s_bands[h], sink_ref[h, 0:1, :], prev, has_prev)[0]) for h in heads]
        outs = [lax.dot_general(_band(vp_ref, vc_ref, h), p_bands[h], TN_DIMS, preferred_element_type=F32).T for h in heads]
        for h in heads:
            for g in range(GQA_GROUP):
                head = GQA_GROUP * h + g
                o_ref[:, head * HEAD_DIM:(head + 1) * HEAD_DIM] = outs[h][g * WINDOW:(g + 1) * WINDOW].astype(BF16)

    core, rr = _call(
        body, grid=(s // WINDOW,), in_specs=_attn_specs(s // WINDOW), out_specs=[pl.BlockSpec((WINDOW, Q_WIDTH), lambda n: (n, 0))],
        out_shape=[jax.ShapeDtypeStruct((s, Q_WIDTH), BF16)], operands=(qkv, qkv, qkv, qkv, qkv, sink_rows), sem=("parallel",),
        name=name, riders=riders)
    return _ret(core, rr, riders)


def attn_bwd(qkv, sink_rows, do, *, name, riders=()):
    s = qkv.shape[0]

    def body(q_ref, kc_ref, kp_ref, vc_ref, vp_ref, sink_ref, do_ref, dq_ref, dkc_ref, dkp_ref, dvc_ref, dvp_ref, dsink_ref):
        n = pl.program_id(0)
        prev = _prev_slots()
        scale = HEAD_DIM ** -0.5
        heads = range(N_KV_HEADS)
        qs, dos = [_stack_heads(q_ref, h) for h in heads], [_stack_heads(do_ref, h) for h in heads]
        kbands, vbands = [_band(kp_ref, kc_ref, h) for h in heads], [_band(vp_ref, vc_ref, h) for h in heads]
        s_bands = [lax.dot_general(kbands[h], qs[h], NT_DIMS, preferred_element_type=F32) for h in heads]
        dp_bands = [lax.dot_general(vbands[h], dos[h], NT_DIMS, preferred_element_type=F32) for h in heads]
        ds_bands, p_bands, parts = [], [], []
        for h in heads:
            p, ps = _attn_probs(s_bands[h], sink_ref[h, 0:1, :], prev, n > 0)
            dp = _pick(prev, dp_bands[h])
            delta = jnp.sum(p * dp, axis=0, keepdims=True)
            ds_bands.append(_spread(prev, p * (dp - delta) * scale))
            p_bands.append(_spread(prev, p))
            dsink = -(ps * delta)
            for g in range(GQA_GROUP):
                parts.append(jnp.broadcast_to(jnp.sum(dsink[:, g * WINDOW:(g + 1) * WINDOW], axis=1, keepdims=True), (8, LANES)))
        for h in heads:
            dk = jnp.dot(ds_bands[h], qs[h], preferred_element_type=F32).astype(BF16)
            dv = jnp.dot(p_bands[h], dos[h], preferred_element_type=F32).astype(BF16)
            dq = lax.dot_general(kbands[h], ds_bands[h], TN_DIMS, preferred_element_type=F32).T
            cols = slice(h * HEAD_DIM, (h + 1) * HEAD_DIM)
            dkp_ref[:, cols], dkc_ref[:, cols] = dk[:WINDOW], dk[WINDOW:]
            dvp_ref[:, cols], dvc_ref[:, cols] = dv[:WINDOW], dv[WINDOW:]
            for g in range(GQA_GROUP):
                head = GQA_GROUP * h + g
                dq_ref[:, head * HEAD_DIM:(head + 1) * HEAD_DIM] = dq[g * WINDOW:(g + 1) * WINDOW].astype(BF16)

        @pl.when(n == 0)
        def _():
            for i, part in enumerate(parts):
                dsink_ref[i // GQA_GROUP, i % GQA_GROUP] = part

        @pl.when(n > 0)
        def _():
            for i, part in enumerate(parts):
                dsink_ref[i // GQA_GROUP, i % GQA_GROUP] += part

    rows_q = pl.BlockSpec((WINDOW, Q_WIDTH), lambda n: (n, 0))
    rows_kv = pl.BlockSpec((WINDOW, KV_WIDTH), lambda n: (n, 0))
    kv_shape = jax.ShapeDtypeStruct((s, KV_WIDTH), BF16)
    core, rr = _call(
        body, grid=(s // WINDOW,), in_specs=_attn_specs(s // WINDOW) + [rows_q],
        out_specs=[rows_q, rows_kv, rows_kv, rows_kv, rows_kv,
                   pl.BlockSpec((N_KV_HEADS, GQA_GROUP, 8, LANES), lambda n: (0, 0, 0, 0))],
        out_shape=[jax.ShapeDtypeStruct((s, Q_WIDTH), BF16), kv_shape, kv_shape, kv_shape, kv_shape,
                   jax.ShapeDtypeStruct((N_KV_HEADS, GQA_GROUP, 8, LANES), F32)],
        operands=(qkv, qkv, qkv, qkv, qkv, sink_rows, do), sem=("arbitrary",), name=name, riders=riders)
    return _ret(core, rr, riders)


GELU_C = 0.7978845608028654
GELU_A = 0.044715


def _gelu(x):
    return 0.5 * x * (1.0 + jnp.tanh(x * (GELU_C + (GELU_C * GELU_A) * (x * x))))


def _gelu_and_grad(x):
    x2 = x * x
    t = jnp.tanh(x * (GELU_C + (GELU_C * GELU_A) * x2))
    half_x, one_t = 0.5 * x, 1.0 + t
    return half_x * one_t, 0.5 * one_t + half_x * (1.0 - t * t) * (GELU_C + (3.0 * GELU_C * GELU_A) * x2)


def _tril_bf16(w):
    row = lax.broadcasted_iota(I32, (SGU_CHUNK, SGU_CHUNK), 0)
    col = lax.broadcasted_iota(I32, (SGU_CHUNK, SGU_CHUNK), 1)
    return jnp.where(row >= col, w, 0.0).astype(BF16)


def _sgu_norm(vg, g, b):
    mu = jnp.mean(vg, axis=-1, keepdims=True)
    cen = vg - mu
    rstd = lax.rsqrt(jnp.mean(cen * cen, axis=-1, keepdims=True) + EPS)
    xhat = cen * rstd
    return xhat, rstd, xhat * g + b


def sgu_in_fwd(h, w_in, ln_g, ln_b, w_sp, b_sp, *, name, tm=512, riders=()):
    s, k = h.shape
    ns = w_in.shape[2]
    tm = _row_tile(s, tm)

    def body(h_ref, w0, w1, w2, w3, g_ref, b_ref, w_ref, bs_ref, z_ref, y_ref):
        hv = h_ref[...]
        zs = [jnp.dot(hv, w_ref_j[...], preferred_element_type=F32) for w_ref_j in (w0, w1, w2, w3)]
        for j, zj in enumerate(zs):
            z_ref[:, j * ns:(j + 1) * ns] = zj.astype(BF16)
        u = _gelu(jnp.concatenate(zs[:2], axis=1))
        _, _, vn = _sgu_norm(_gelu(jnp.concatenate(zs[2:], axis=1)), g_ref[...], b_ref[...])
        vn = vn.astype(BF16)
        for grp in range(SGU_GROUPS):
            w = _tril_bf16(w_ref[grp])
            cols = slice(grp * LANES, (grp + 1) * LANES)
            for ch in range(tm // SGU_CHUNK):
                rows = slice(ch * SGU_CHUNK, (ch + 1) * SGU_CHUNK)
                mixed = jnp.dot(w, vn[rows, cols], preferred_element_type=F32) + bs_ref[grp]
                y_ref[rows, cols] = (u[rows, cols] * mixed).astype(BF16)

    def shard(j):
        return pl.BlockSpec((None, k, ns), lambda i: (j, 0, 0))

    full3 = pl.BlockSpec((SGU_GROUPS, SGU_CHUNK, SGU_CHUNK), lambda i: (0, 0, 0))
    core, rr = _call(
        body, grid=(s // tm,),
        in_specs=[_row_spec(tm, k)] + [shard(j) for j in range(N_CHIPS)] + [_vec_spec(D_MODEL), _vec_spec(D_MODEL), full3, full3],
        out_specs=[_row_spec(tm, 2 * D_MODEL), _row_spec(tm, D_MODEL)],
        out_shape=[jax.ShapeDtypeStruct((s, 2 * D_MODEL), BF16), jax.ShapeDtypeStruct((s, D_MODEL), BF16)],
        operands=(h, w_in, w_in, w_in, w_in, ln_g, ln_b, w_sp, b_sp), sem=("parallel",), name=name, riders=riders)
    return _ret(core, rr, riders)


def sgu_bwd(z, dy, ln_g, ln_b, w_sp, b_sp, *, name, tm=256, riders=()):
    s = z.shape[0]
    tm = _row_tile(s, tm)

    def body(z_ref, dy_ref, g_ref, b_ref, w_ref, bs_ref, dz_ref, dw_ref, dbs_ref, dg_ref, db_ref, dvn_buf):
        first = pl.program_id(0) == 0
        u, u_grad = _gelu_and_grad(z_ref[:, :D_MODEL].astype(F32))
        vg, v_grad = _gelu_and_grad(z_ref[:, D_MODEL:].astype(F32))
        xhat, rstd, vn = _sgu_norm(vg, g_ref[...], b_ref[...])
        vn = vn.astype(BF16)
        dyv = dy_ref[...]
        dmixed = dyv * u
        dz_gate = dyv * u_grad
        row = lax.broadcasted_iota(I32, (SGU_CHUNK, SGU_CHUNK), 0)
        col = lax.broadcasted_iota(I32, (SGU_CHUNK, SGU_CHUNK), 1)
        dws, dbss = [], []
        for grp in range(SGU_GROUPS):
            w = _tril_bf16(w_ref[grp])
            cols = slice(grp * LANES, (grp + 1) * LANES)
            dw = jnp.zeros((SGU_CHUNK, SGU_CHUNK), F32)
            dbs = jnp.zeros((SGU_CHUNK, 1), F32)
            for ch in range(tm // SGU_CHUNK):
                rows = slice(ch * SGU_CHUNK, (ch + 1) * SGU_CHUNK)
                vblk = vn[rows, cols]
                mixed = jnp.dot(w, vblk, preferred_element_type=F32) + bs_ref[grp]
                dz_ref[rows, cols] = (dz_gate[rows, cols] * mixed).astype(BF16)
                dm = dmixed[rows, cols]
                dmb = dm.astype(BF16)
                dvn_buf[rows, cols] = lax.dot_general(w, dmb, TN_DIMS, preferred_element_type=F32)
                dw += lax.dot_general(dmb, vblk, NT_DIMS, preferred_element_type=F32)
                dbs += jnp.sum(dm, axis=-1, keepdims=True)
            dws.append(jnp.where(row >= col, dw, 0.0))
            dbss.append(jnp.broadcast_to(dbs, (SGU_CHUNK, SGU_CHUNK)))

        dvn = dvn_buf[...]
        dxhat = dvn * g_ref[...]
        dvg = rstd * (dxhat - jnp.mean(dxhat, axis=-1, keepdims=True) - xhat * jnp.mean(dxhat * xhat, axis=-1, keepdims=True))
        dz_ref[:, D_MODEL:] = (dvg * v_grad).astype(BF16)
        dlng, dlnb = jnp.sum(dvn * xhat, axis=0, keepdims=True), jnp.sum(dvn, axis=0, keepdims=True)

        @pl.when(first)
        def _():
            for grp in range(SGU_GROUPS):
                dw_ref[grp] = dws[grp]
                dbs_ref[grp] = dbss[grp]
            dg_ref[...] = dlng
            db_ref[...] = dlnb

        @pl.when(jnp.logical_not(first))
        def _():
            for grp in range(SGU_GROUPS):
                dw_ref[grp] += dws[grp]
                dbs_ref[grp] += dbss[grp]
            dg_ref[...] += dlng
            db_ref[...] += dlnb

    full3 = pl.BlockSpec((SGU_GROUPS, SGU_CHUNK, SGU_CHUNK), lambda i: (0, 0, 0))
    s3 = jax.ShapeDtypeStruct((SGU_GROUPS, SGU_CHUNK, SGU_CHUNK), F32)
    vshape = jax.ShapeDtypeStruct((1, D_MODEL), F32)
    core, rr = _call(
        body, grid=(s // tm,),
        in_specs=[_row_spec(tm, 2 * D_MODEL), _row_spec(tm, D_MODEL), _vec_spec(D_MODEL), _vec_spec(D_MODEL), full3, full3],
        out_specs=[_row_spec(tm, 2 * D_MODEL), full3, full3, _vec_spec(D_MODEL), _vec_spec(D_MODEL)],
        out_shape=[jax.ShapeDtypeStruct((s, 2 * D_MODEL), BF16), s3, s3, vshape, vshape],
        scratch_shapes=[pltpu.VMEM((tm, D_MODEL), F32)], operands=(z, dy, ln_g, ln_b, w_sp, b_sp), name=name, riders=riders)
    return _ret(core, rr, riders)


def _sigmoid(x):
    return 1.0 / (1.0 + jnp.exp(-x))


def ffn_up(h, w_gu, *, name, tm=512, riders=()):
    s = h.shape[0]
    tm = _row_tile(s, tm)

    def body(h_ref, wg_ref, wu_ref, d_ref, a_ref):
        hv = h_ref[...]
        sub = min(256, tm)
        for t in range(tm // sub):
            rows = slice(t * sub, (t + 1) * sub)
            g = jnp.dot(hv[rows], wg_ref[...], preferred_element_type=F32)
            u = jnp.dot(hv[rows], wu_ref[...], preferred_element_type=F32)
            sig = _sigmoid(g)
            silu = g * sig
            d_ref[0, rows, :] = (u * (sig + silu * (1.0 - sig))).astype(BF16)
            d_ref[1, rows, :] = silu.astype(BF16)
            a_ref[rows, :] = (silu * u).astype(BF16)

    core, rr = _call(
        body, grid=(2, s // tm),
        in_specs=[pl.BlockSpec((tm, D_MODEL), lambda j, i: (i, 0)),
                  pl.BlockSpec((None, D_MODEL, FF_HALF), lambda j, i: (j, 0, 0)),
                  pl.BlockSpec((None, D_MODEL, FF_HALF), lambda j, i: (j + 2, 0, 0))],
        out_specs=[pl.BlockSpec((2, tm, FF_HALF), lambda j, i: (0, i, j)), pl.BlockSpec((tm, FF_HALF), lambda j, i: (i, j))],
        out_shape=[jax.ShapeDtypeStruct((2, s, D_FF), BF16), jax.ShapeDtypeStruct((s, D_FF), BF16)],
        operands=(h, w_gu, w_gu), sem=("parallel", "parallel"), name=name, riders=riders)
    return _ret(core, rr, riders)


def _weight_tile(rows):
    for tr in (512, 352, 256, 128):
        if rows % tr == 0:
            return tr
    return rows


def place_shard(w, layer, chip_arr, dtype, *, name, riders=()):
    _, r, c = w.shape
    tr = _weight_tile(r)

    def body(chip_ref, w_ref, o_ref):
        o_ref[...] = w_ref[...].astype(dtype)

    core, rr = _call(
        body, grid=(r // tr,), prefetch=(chip_arr,),
        in_specs=[pl.BlockSpec((None, tr, c), lambda i, chip: (layer, i, 0))],
        out_specs=[pl.BlockSpec((None, tr, c), lambda i, chip: (chip[0], i, 0))],
        out_shape=[jax.ShapeDtypeStruct((N_CHIPS, r, c), dtype)], operands=(w,), sem=("parallel",), name=name, riders=riders)
    return _ret(core, rr, riders)


def _adamw_math(w, g, m, v):
    m = ADAM_B1 * m + (1.0 - ADAM_B1) * g
    v = ADAM_B2 * v + (1.0 - ADAM_B2) * (g * g)
    m_hat = m / (1.0 - ADAM_B1 ** ADAM_STEP)
    v_hat = v / (1.0 - ADAM_B2 ** ADAM_STEP)
    delta = -ADAM_LR * (m_hat / (jnp.sqrt(v_hat) + ADAM_EPS) + ADAM_WD * w)
    return delta, m, v


def adamw(w, g, m, v, *, name, after=None):
    nl, r, c = w.shape
    tr = _weight_tile(r)

    def body(w_ref, g_ref, m_ref, v_ref, *rest):
        go_ref, d_ref, mo_ref, vo_ref = rest[-4:]
        gv = g_ref[...]
        go_ref[...] = gv
        d_ref[...], mo_ref[...], vo_ref[...] = _adamw_math(w_ref[...], gv, m_ref[...], v_ref[...])

    spec = pl.BlockSpec((None, tr, c), lambda l, i: (l, i, 0))
    shape = jax.ShapeDtypeStruct(w.shape, F32)
    extra = [] if after is None else [after]
    outs, _ = _call(body, grid=(nl, r // tr), in_specs=[spec] * 4 + [ANY] * len(extra), out_specs=[spec] * 4,
                    out_shape=[shape] * 4, operands=(w, g, m, v, *extra), sem=("parallel", "parallel"), name=name)
    return outs


def adamw_small(ws, gs, ms, vs, *, name):
    n = len(ws)

    def body(*refs):
        ins, outs = refs[:4 * n], refs[4 * n:]
        for t in range(n):
            gv = ins[n + t][...]
            outs[t][...] = gv
            outs[n + t][...], outs[2 * n + t][...], outs[3 * n + t][...] = _adamw_math(
                ins[t][...], gv, ins[2 * n + t][...], ins[3 * n + t][...])

    shapes = [jax.ShapeDtypeStruct(w.shape, F32) for w in ws]
    res = pl.pallas_call(body, out_shape=shapes * 4, name=name)(*ws, *gs, *ms, *vs)
    return res[:n], res[n:2 * n], res[2 * n:3 * n], res[3 * n:]


def pair_add(g, r1, c_arr, *, name):
    _, rows, cdim = g.shape
    h = rows // 2

    def body(c_ref, g_ref, r_ref, o_ref):
        o_ref[...] = (g_ref[...].astype(F32) + r_ref[...].astype(F32)).astype(o_ref.dtype)

    (out,), _ = _call(
        body, grid=(N_CHIPS,), prefetch=(c_arr,),
        in_specs=[pl.BlockSpec((None, h, cdim), lambda s, c: (s, c[0], 0)), pl.BlockSpec((None, h, cdim), lambda s, c: (s, 0, 0))],
        out_specs=[pl.BlockSpec((None, h, cdim), lambda s, c: (s, 0, 0))],
        out_shape=[jax.ShapeDtypeStruct((N_CHIPS, h, cdim), g.dtype)], operands=(g, r1), sem=("parallel",), name=name)
    return out


def final_add(g, r1, r2, jc_arr, *, dest_shape, lead, prev, name):
    _, rows, cdim = g.shape
    h = rows // 2

    def body(jc_ref, g_ref, r1_ref, r2_ref, *rest):
        o_ref = rest[-1]
        acc = g_ref[...].astype(F32) + r1_ref[...].astype(F32)
        for k in range(3):
            acc = acc + r2_ref[k].astype(F32)
        o_ref[...] = acc

    if lead is None:
        o_spec = pl.BlockSpec((h, cdim), lambda i, jc: (jc[1], 0))
    elif lead == "chip":
        o_spec = pl.BlockSpec((None, h, cdim), lambda i, jc: (jc[0], jc[1], 0))
    else:
        o_spec = pl.BlockSpec((None, h, cdim), lambda i, jc: (lead, jc[1], 0))
    in_specs = [pl.BlockSpec((None, h, cdim), lambda i, jc: (jc[0], jc[1], 0)),
                pl.BlockSpec((None, h, cdim), lambda i, jc: (jc[0], 0, 0)),
                pl.BlockSpec((3, h, cdim), lambda i, jc: (0, 0, 0))]
    operands = [g, r1, r2]
    aliases = None
    if prev is not None:
        in_specs.append(ANY)
        operands.append(prev)
        aliases = {3: 0}
    (out,), _ = _call(body, grid=(1,), prefetch=(jc_arr,), in_specs=in_specs, out_specs=[o_spec],
                      out_shape=[jax.ShapeDtypeStruct(dest_shape, F32)], operands=operands, aliases=aliases, name=name)
    return out


def _place():
    return lax.axis_index("x"), lax.axis_index("y"), lax.axis_index("c")


def _partner(x, y, k):
    return (1 - x if k >> 1 else x), (1 - y if k & 1 else y)


WHOLE = (0, 1, 1)


def _half(rows, sel, dtype, piece=WHOLE):
    lo, hi, n = piece
    align = 16 if dtype == BF16 else 8
    step = rows // 2 // n
    assert rows // 2 == step * n and step % align == 0
    return pl.ds(pl.multiple_of(sel * (rows // 2) + lo * step, align), (hi - lo) * step)


def _rider(peers, inputs, aliased, fresh, nsem, copies, arrivals):
    def start(ins, outs, send, recv):
        for cp in copies(ins, outs, send, recv):
            cp.start()

    def finish(ins, outs, send, recv):
        for cp in arrivals(ins, outs, send, recv):
            cp.wait_recv()
        for cp in copies(ins, outs, send, recv):
            cp.wait_send()

    return types.SimpleNamespace(peers=peers, inputs=list(inputs), aliased=list(aliased), fresh=list(fresh), nsem=nsem,
                                 start=start, finish=finish)


def _remote(src, dst, send, recv, idx, dev):
    return pltpu.make_async_remote_copy(src_ref=src, dst_ref=dst, send_sem=send.at[idx], recv_sem=recv.at[idx],
                                        device_id=dev, device_id_type=MESH)


def gather_ici_rider(fulls, pieces=None):
    nt = len(fulls)
    pieces = pieces or [WHOLE] * nt

    def region(outs, t, slot, sel):
        return outs[t].at[slot, _half(fulls[t].shape[1], sel, fulls[t].dtype, pieces[t])]

    def copies(ins, outs, send, recv):
        x, y, c = _place()
        res = []
        for t in range(nt):
            for k in (1, 2, 3):
                px, py = _partner(x, y, k)
                mine = region(outs, t, 2 * x + y, c)
                res.append(_remote(mine, mine, send, recv, 3 * t + k - 1, (px, py, c)))
        return res

    def arrivals(ins, outs, send, recv):
        x, y, c = _place()
        res = []
        for t in range(nt):
            for k in (1, 2, 3):
                px, py = _partner(x, y, k)
                theirs = region(outs, t, 2 * px + py, c)
                res.append(_remote(theirs, theirs, send, recv, 3 * t + k - 1, (x, y, c)))
        return res

    return _rider("chips", fulls, range(nt), [], 3 * nt, copies, arrivals)


def gather_d2d_rider(fulls, pieces=None):
    nt = len(fulls)
    pieces = pieces or [WHOLE] * nt

    def region(outs, t, slot, sel):
        return outs[t].at[slot, _half(fulls[t].shape[1], sel, fulls[t].dtype, pieces[t])]

    def both(outs, send, recv, mine):
        x, y, c = _place()
        res = []
        for t in range(nt):
            for k in (1, 2, 3):
                px, py = _partner(x, y, k)
                part = region(outs, t, 2 * px + py, c if mine else 1 - c)
                res.append(_remote(part, part, send, recv, 3 * t + k - 1, (x, y, 1 - c)))
        return res

    return _rider("sibling", fulls, range(nt), [], 3 * nt, lambda i, o, s, r: both(o, s, r, True),
                  lambda i, o, s, r: both(o, s, r, False))


def exchange_rider(grads):
    nt = len(grads)

    def both(ins, outs, send, recv):
        x, y, c = _place()
        return [_remote(ins[t].at[:, _half(grads[t].shape[1], 1 - c, grads[t].dtype)], outs[t], send, recv, t, (x, y, 1 - c))
                for t in range(nt)]

    fresh = [jax.ShapeDtypeStruct((N_CHIPS, g.shape[1] // 2, g.shape[2]), g.dtype) for g in grads]
    return _rider("sibling", grads, [], fresh, nt, both, both)


def scatter_rider(parts):
    nt = len(parts)

    def both(ins, outs, send, recv):
        x, y, c = _place()
        res = []
        for t in range(nt):
            for k in (1, 2, 3):
                px, py = _partner(x, y, k)
                res.append(_remote(ins[t].at[2 * px + py], outs[t].at[k - 1], send, recv, 3 * t + k - 1, (px, py, c)))
        return res

    fresh = [jax.ShapeDtypeStruct((3,) + p.shape[1:], p.dtype) for p in parts]
    return _rider("chips", parts, [], fresh, 3 * nt, both, both)


def broadcast_rider(bufs, items):
    def region(outs, item, sel):
        bi, lead = item
        ref = outs[bi]
        if lead == "chip":
            x, y, _ = _place()
            ref = ref.at[2 * x + y]
        elif lead is not None:
            ref = ref.at[lead]
        return ref.at[_half(ref.shape[0], sel, F32)]

    def both(outs, send, recv, mine):
        x, y, c = _place()
        res = []
        for i, item in enumerate(items):
            part = region(outs, item, c if mine else 1 - c)
            res.append(_remote(part, part, send, recv, i, (x, y, 1 - c)))
        return res

    return _rider("sibling", bufs, range(len(bufs)), [], len(items), lambda i, o, s, r: both(o, s, r, True),
                  lambda i, o, s, r: both(o, s, r, False))


def allcast_rider(buf):
    peers = [(k, flip) for k in range(N_CHIPS) for flip in (0, 1) if (k, flip) != (0, 0)]

    def both(outs, send, recv, mine):
        x, y, c = _place()
        res = []
        for i, (k, flip) in enumerate(peers):
            px, py = _partner(x, y, k)
            pc = 1 - c if flip else c
            slot, sel = (2 * x + y, c) if mine else (2 * px + py, pc)
            part = outs[0].at[slot, _half(buf.shape[1], sel, F32)]
            res.append(_remote(part, part, send, recv, i, (px, py, pc)))
        return res

    return _rider("everyone", [buf], [0], [], len(peers), lambda i, o, s, r: both(o, s, r, True),
                  lambda i, o, s, r: both(o, s, r, False))


def comm_call(riders, *, name):
    _, res = _call(None, riders=riders, name=name)
    return res


SEMS = pl.BlockSpec(memory_space=pltpu.SEMAPHORE)
SIDE_EFFECT = pltpu.SideEffectType.DATAFLOW_SIDE_EFFECTING


def _split_refs(riders, refs):
    views, p = [], 0
    for r in riders:
        bufs = refs[p:p + len(r.inputs) + len(r.fresh)]
        p += len(bufs)
        ins = bufs[:len(r.inputs)]
        views.append([ins, [ins[i] for i in r.aliased] + list(bufs[len(r.inputs):])])
    for view in views:
        view += [refs[p], refs[p + 1]]
        p += 2
    return views


def comm_start(riders, *, name):
    kind = _peer_kind(riders)
    bufs = [a for r in riders for a in r.inputs]
    fresh = [f for r in riders for f in r.fresh]
    n_buf, n_fresh = len(bufs), len(fresh)

    def body(*refs):
        ins, outs = refs[:n_buf], refs[n_buf:]
        through, land, sems = outs[:n_buf], outs[n_buf:n_buf + n_fresh], outs[n_buf + n_fresh:-1]
        _peer_barrier(kind)
        per_rider, pb, pf = [], 0, 0
        for r in riders:
            per_rider += list(through[pb:pb + len(r.inputs)]) + list(land[pf:pf + len(r.fresh)])
            pb, pf = pb + len(r.inputs), pf + len(r.fresh)
        for r, (r_ins, r_outs, send, recv) in zip(riders, _split_refs(riders, per_rider + list(sems))):
            r.start(r_ins, r_outs, send, recv)
        outs[-1][...] = jnp.zeros((8, LANES), F32)

    sem_shapes = [pltpu.SemaphoreType.DMA((r.nsem,)) for r in riders for _ in (0, 1)]
    res = pl.pallas_call(
        body, name=name, in_specs=[ANY] * n_buf,
        out_specs=[ANY] * (n_buf + n_fresh) + [SEMS] * len(sem_shapes) + [pl.BlockSpec(memory_space=pltpu.VMEM)],
        out_shape=[jax.ShapeDtypeStruct(a.shape, a.dtype) for a in bufs] + fresh + sem_shapes
        + [jax.ShapeDtypeStruct((8, LANES), F32)],
        input_output_aliases={i: i for i in range(n_buf)},
        compiler_params=pltpu.CompilerParams(has_side_effects=SIDE_EFFECT, collective_id=PEER_KINDS.index(kind)))(*bufs)
    return (riders, list(res[:n_buf + n_fresh]), list(res[n_buf + n_fresh:-1])), res[-1]


def comm_wait(state, after, *, name):
    riders, bufs, sems = state
    n_buf, n_sem = len(bufs), len(sems)
    n_in = sum(len(r.inputs) for r in riders)

    def body(*refs):
        held, sem_refs = refs[:n_buf], refs[n_buf:n_buf + n_sem]
        through, land = held[:n_in], held[n_in:]
        per_rider, pb, pf = [], 0, 0
        for r in riders:
            per_rider += list(through[pb:pb + len(r.inputs)]) + list(land[pf:pf + len(r.fresh)])
            pb, pf = pb + len(r.inputs), pf + len(r.fresh)
        for r, (r_ins, r_outs, send, recv) in zip(riders, _split_refs(riders, per_rider + list(sem_refs))):
            r.finish(r_ins, r_outs, send, recv)

    res = pl.pallas_call(
        body, name=name, in_specs=[ANY] * n_buf + [SEMS] * n_sem + [ANY], out_specs=[ANY] * n_buf,
        out_shape=[jax.ShapeDtypeStruct(a.shape, a.dtype) for a in bufs],
        input_output_aliases={i: i for i in range(n_buf)},
        compiler_params=pltpu.CompilerParams(has_side_effects=SIDE_EFFECT))(*bufs, *sems, after)
    through, land = list(res[:n_in]), list(res[n_in:])
    out, pb, pf = [], 0, 0
    for r in riders:
        r_ins, r_land = through[pb:pb + len(r.inputs)], land[pf:pf + len(r.fresh)]
        pb, pf = pb + len(r.inputs), pf + len(r.fresh)
        out.append([r_ins[i] for i in r.aliased] + r_land)
    return out


SLAB_ROWS = 192


def _pad_rows(a, rows=8):
    return jnp.pad(a, ((0, rows - a.shape[0]), (0, 0)))


def _pack_small(norm_grads, db_qkv, db_o, dsinks, db_sp, dln_g, dln_b, dw_sp, loss_part):
    parts = [
        jnp.concatenate(norm_grads, axis=0),
        _pad_rows(jnp.pad(db_qkv, ((0, 0), (0, 2 * D_MODEL - QKV_WIDTH))).reshape(2, D_MODEL)),
        _pad_rows(db_o),
        _pad_rows(jnp.pad(dsinks.reshape(1, N_Q_HEADS), ((0, 0), (0, D_MODEL - N_Q_HEADS)))),
        _pad_rows(db_sp.reshape(1, D_MODEL)),
        _pad_rows(jnp.concatenate([dln_g, dln_b, jnp.pad(loss_part[0:1], ((0, 0), (0, D_MODEL - LANES)))], axis=0)),
        dw_sp.reshape(SGU_CHUNK, D_MODEL),
    ]
    slab = jnp.concatenate(parts, axis=0)
    return jnp.pad(slab, ((0, SLAB_ROWS - slab.shape[0]), (0, 0))).reshape(N_CHIPS, SLAB_ROWS // N_CHIPS, D_MODEL)


def _unpack_small(slab, j):
    slab = slab.reshape(SLAB_ROWS, D_MODEL)
    norms = [slab[2 * i:2 * i + 2] for i in range(4)]
    db_qkv = slab[8:10].reshape(1, 2 * D_MODEL)[:, :QKV_WIDTH]
    db_o = slab[16:17]
    dsinks = slab[24:25, :N_Q_HEADS]
    db_sp = slab[32:33].reshape(SGU_GROUPS, SGU_CHUNK)
    width = D_MODEL // N_CHIPS
    dln_g = lax.dynamic_slice(slab[40:41], (0, j * width), (1, width))
    dln_b = lax.dynamic_slice(slab[41:42], (0, j * width), (1, width))
    dw_sp = slab[48:48 + SGU_CHUNK].reshape(SGU_GROUPS * SGU_CHUNK, SGU_CHUNK)
    return norms, db_qkv, db_o, dsinks, db_sp, dln_g, dln_b, dw_sp, slab[42, 0]


class _GradReduce:
    def __init__(self, c_arr, jc_arr, dest_shapes):
        self.c_arr, self.jc_arr, self.dest_shapes = c_arr, jc_arr, dest_shapes
        self.grad, self.sibling, self.pair, self.chips, self.dest = {}, {}, {}, {}, {}

    def exchange(self, tags):
        return exchange_rider([self.grad[t] for t in tags])

    def exchanged(self, tags, res):
        for t, r in zip(tags, res):
            self.sibling[t] = r
            self.pair[t] = pair_add(self.grad[t], r, self.c_arr, name=f"pair_add_{t}")

    def scatter(self, tags):
        return scatter_rider([self.pair[t] for t in tags])

    def scattered(self, tags, res, where):
        for t, r in zip(tags, res):
            name, lead = where[t]
            self.dest[name] = final_add(self.grad[t], self.sibling[t], r, self.jc_arr, dest_shape=self.dest_shapes[name],
                                        lead=lead, prev=self.dest.get(name), name=f"final_add_{t}")

    def broadcast(self, items):
        names = []
        for n, _ in items:
            if n not in names:
                names.append(n)
        return names, broadcast_rider([self.dest[n] for n in names], [(names.index(n), lead) for n, lead in items])

    def broadcasted(self, names, res):
        for n, r in zip(names, res):
            self.dest[n] = r


def kernel(x, norm_mix_pre, norm_mix_post, norm_ffn_pre, norm_ffn_post, attn_w_qkv, attn_b_qkv, attn_sinks, attn_w_o, attn_b_o, sgu_w_in, sgu_ln_g, sgu_ln_b, sgu_w_spatial, sgu_b_spatial, sgu_w_out, ffn_w_gate_up, ffn_w_down, loss_target, m_norm_mix_pre, m_norm_mix_post, m_norm_ffn_pre, m_norm_ffn_post, m_attn_w_qkv, m_attn_b_qkv, m_attn_sinks, m_attn_w_o, m_attn_b_o, m_sgu_w_in, m_sgu_ln_g, m_sgu_ln_b, m_sgu_w_spatial, m_sgu_b_spatial, m_sgu_w_out, m_ffn_w_gate_up, m_ffn_w_down, v_norm_mix_pre, v_norm_mix_post, v_norm_ffn_pre, v_norm_ffn_post, v_attn_w_qkv, v_attn_b_qkv, v_attn_sinks, v_attn_w_o, v_attn_b_o, v_sgu_w_in, v_sgu_ln_g, v_sgu_ln_b, v_sgu_w_spatial, v_sgu_b_spatial, v_sgu_w_out, v_ffn_w_gate_up, v_ffn_w_down):
    s = x.shape[1]
    x0 = x.reshape(s, D_MODEL)
    target = loss_target.reshape(s, D_MODEL)
    mx, my, mc = lax.axis_index("x"), lax.axis_index("y"), lax.axis_index("c")
    chip = 2 * mx + my
    chip_arr = jnp.reshape(chip, (1,)).astype(I32)
    c_arr = jnp.reshape(mc, (1,)).astype(I32)
    jc_arr = jnp.stack([chip, mc]).astype(I32)
    zero_bias = jnp.zeros((1, D_MODEL), F32)

    def gain(p, i):
        return p[i:i + 1]

    big = [attn_w_qkv, attn_w_o, sgu_w_in, sgu_w_out, ffn_w_gate_up, ffn_w_gate_up, ffn_w_down, ffn_w_down]
    layers = [0, 0, 0, 0, 0, 1, 0, 1]
    tags = ["qkv", "wo", "win", "wout", "wgu0", "wgu1", "wd0", "wd1"]
    full = {t: place_shard(w, l, chip_arr, BF16, name=f"place_{t}") for w, l, t in zip(big, layers, tags)
            if t not in ("wgu0", "wgu1")}
    ln_pack = _pad_rows(jnp.concatenate([sgu_ln_g, sgu_ln_b], axis=0), 16)[None]
    full["ln"] = place_shard(ln_pack, 0, chip_arr, F32, name="place_ln")

    def split(items):
        return [i if isinstance(i, str) else i[0] for i in items], [WHOLE if isinstance(i, str) else tuple(i[1:]) for i in items]

    def ici(*items):
        names, pieces = split(items)
        return gather_ici_rider([full[n] for n in names], pieces)

    def d2d(*items):
        names, pieces = split(items)
        return gather_d2d_rider([full[n] for n in names], pieces)

    def landed(items, res):
        for n, r in zip(split(items)[0], res):
            full[n] = r

    cos, sin = _rope_tables(s)
    sink_rows = jnp.broadcast_to(
        jnp.repeat(attn_sinks.reshape(N_KV_HEADS, GQA_GROUP), WINDOW, axis=1)[:, None, :], (N_KV_HEADS, 8, ROWS))
    w_sp = sgu_w_spatial.reshape(SGU_GROUPS, SGU_CHUNK, SGU_CHUNK)
    b_sp = jnp.broadcast_to(sgu_b_spatial.reshape(SGU_GROUPS, SGU_CHUNK)[:, :, None], (SGU_GROUPS, SGU_CHUNK, LANES))

    (h0, full["wgu0"]), (res,) = prenorm_and_place(x0, gain(norm_mix_pre, 0), ffn_w_gate_up, 0, chip_arr, name="prenorm_0",
                                                   riders=[ici("qkv", "ln")])
    landed(("qkv", "ln"), res)
    full["wgu1"], (res,) = place_shard(ffn_w_gate_up, 1, chip_arr, BF16, name="place_wgu1", riders=[d2d("qkv", "ln")])
    landed(("qkv", "ln"), res)
    ln_g = full["ln"][:, 0, :].reshape(1, D_MODEL)
    ln_b = full["ln"][:, 1, :].reshape(1, D_MODEL)

    def hosted(call, stages):
        outputs, results = call([{"ici": ici, "d2d": d2d}[kind](*items) for kind, items in stages])
        for (_, items), res in zip(stages, results):
            landed(items, res)
        return outputs

    qkv = hosted(lambda r: qkv_proj(h0, full["qkv"], attn_b_qkv, cos, sin, name="qkv_proj", riders=r),
                 [("ici", ("wo", ("wgu0", 0, 3, 8)))])
    o = hosted(lambda r: attn_fwd(qkv, sink_rows, name="attn_fwd", riders=r),
               [("d2d", ("wo",)), ("ici", (("wgu0", 3, 8, 8), ("wd0", 0, 2, 11)))])
    w_o = full["wo"].reshape(Q_WIDTH, D_MODEL)
    x1, h1, m0 = hosted(lambda r: proj_residual_norm(o, w_o, x0, attn_b_o, gain(norm_mix_post, 0), gain(norm_ffn_pre, 0),
                                                     name="attn_out_norm", riders=r),
                        [("d2d", ("wgu0",)), ("ici", (("wd0", 2, 11, 11),))])
    gu0, a0 = hosted(lambda r: ffn_up(h1, full["wgu0"], name="ffn_up_0", riders=r),
                     [("d2d", ("wd0",)), ("ici", ("win", "wout", ("wgu1", 0, 4, 8)))])
    w_d0 = full["wd0"].reshape(D_FF, D_MODEL)
    x2, h2, f0 = hosted(lambda r: proj_residual_norm(a0, w_d0, x1, zero_bias, gain(norm_ffn_post, 0), gain(norm_mix_pre, 1),
                                                     name="ffn_down_norm_0", riders=r),
                        [("d2d", ("win", "wout")), ("ici", (("wgu1", 4, 8, 8),))])
    w_in = full["win"]
    z, y = hosted(lambda r: sgu_in_fwd(h2, w_in, ln_g, ln_b, w_sp, b_sp, name="sgu_in_fwd", riders=r),
                  [("d2d", ("wgu1",)), ("ici", ("wd1",))])
    w_out = full["wout"].reshape(D_MODEL, D_MODEL)
    x3, h3, m1 = hosted(lambda r: proj_residual_norm(y, w_out, x2, zero_bias, gain(norm_mix_post, 1), gain(norm_ffn_pre, 1),
                                                     name="sgu_out_norm", riders=r),
                        [("d2d", ("wd1",))])
    w_qkv, w_gu0, w_gu1 = full["qkv"], full["wgu0"], full["wgu1"]
    w_d1 = full["wd1"].reshape(D_FF, D_MODEL)
    gu1, a1, dx4, df1, dg_fpost1, loss_part = ffn_fwd_loss_rows(
        h3, w_gu1, w_d1, x3, gain(norm_ffn_post, 1), target, name="ffn_fwd_loss_rows")

    red = _GradReduce(c_arr, jc_arr, {
        "qkv": attn_w_qkv.shape[1:], "wo": attn_w_o.shape[1:], "win": sgu_w_in.shape[1:], "wout": sgu_w_out.shape[1:],
        "wgu": ffn_w_gate_up.shape, "wd": ffn_w_down.shape, "slab": (N_CHIPS, SLAB_ROWS // N_CHIPS, D_MODEL)})
    where = {"qkv": ("qkv", None), "wo": ("wo", None), "win": ("win", None), "wout": ("wout", None), "wgu0": ("wgu", 0),
             "wgu1": ("wgu", 1), "wd0": ("wd", 0), "wd1": ("wd", 1), "small": ("slab", "chip")}

    dgu1, dx3, dm1, dg_fpre1, dg_mpost1, _ = ffn_bwd_rows(
        df1, w_d1, gu1, w_gu1, dx4, x3, gain(norm_ffn_pre, 1), m1, gain(norm_mix_post, 1), name="ffn_bwd_rows_1")
    red.grad["wd1"] = mm_tn(a1, df1, shard_major=False, tm=256, tn=D_MODEL, name="dw_down_1").reshape(
        N_CHIPS, D_FF // N_CHIPS, D_MODEL)
    red.grad["wgu1"], (res,) = mm_tn(h3, dgu1, shard_major=True, tm=512, tn=FF_HALF, name="dw_gate_up_1",
                                     riders=[red.exchange(["wd1"])])
    red.exchanged(["wd1"], res)
    dy, (res,) = mm_nt(dm1, w_out, out_dtype=F32, name="dy_sgu", riders=[red.exchange(["wgu1"])])
    red.exchanged(["wgu1"], res)
    red.grad["wout"] = mm_tn(y, dm1, shard_major=False, tm=512, tn=D_MODEL, name="dw_sgu_out").reshape(
        N_CHIPS, D_MODEL // N_CHIPS, D_MODEL)
    (dz, dw_sp, db_sp, dln_g, dln_b), (res_a, res_b) = sgu_bwd(
        z, dy, ln_g, ln_b, w_sp, b_sp, name="sgu_bwd", riders=[red.scatter(["wgu1"]), red.exchange(["wout"])])
    red.scattered(["wgu1"], res_a, where)
    red.exchanged(["wout"], res_b)
    names, rider = red.broadcast([("wgu", 1)])
    red.grad["win"], (res_a, res_b) = mm_tn(h2, dz, shard_major=True, tm=D_MODEL, tn=2 * D_MODEL // N_CHIPS, name="dw_sgu_in",
                                            riders=[rider, red.scatter(["wout"])])
    red.broadcasted(names, res_a)
    red.scattered(["wout"], res_b, where)
    names, rider = red.broadcast([("wout", None)])
    (dx2, df0, dg_mpre1, dg_fpost0, _), (res_a, res_b) = dh_norm_bwd_pair(
        dz, w_in, dx3, x2, gain(norm_mix_pre, 1), f0, gain(norm_ffn_post, 0), name="dh_sgu_norm",
        riders=[red.exchange(["win"]), rider])
    red.exchanged(["win"], res_a)
    red.broadcasted(names, res_b)
    (dgu0, dx1, dm0, dg_fpre0, dg_mpost0, db_o), (res,) = ffn_bwd_rows(
        df0, w_d0, gu0, w_gu0, dx2, x1, gain(norm_ffn_pre, 0), m0, gain(norm_mix_post, 0), name="ffn_bwd_rows_0",
        riders=[red.scatter(["wd1", "win"])])
    red.scattered(["wd1", "win"], res, where)
    names, rider = red.broadcast([("wd", 1), ("win", None)])
    dw_d0, (res,) = mm_tn(a0, df0, shard_major=False, tm=256, tn=D_MODEL, name="dw_down_0", riders=[rider])
    red.broadcasted(names, res)
    red.grad["wd0"] = dw_d0.reshape(N_CHIPS, D_FF // N_CHIPS, D_MODEL)
    do, (res,) = mm_nt(dm0, w_o, out_dtype=BF16, name="do_attn", riders=[red.exchange(["wd0"])])
    red.exchanged(["wd0"], res)
    red.grad["wgu0"], (res,) = mm_tn(h1, dgu0, shard_major=True, tm=512, tn=FF_HALF, name="dw_gate_up_0",
                                     riders=[red.scatter(["wd0"])])
    red.scattered(["wd0"], res, where)
    names, rider = red.broadcast([("wd", 0)])
    dw_o, (res_a, res_b) = mm_tn(o, dm0, shard_major=False, tm=512, tn=D_MODEL, name="dw_attn_out",
                                 riders=[red.exchange(["wgu0"]), rider])
    red.exchanged(["wgu0"], res_a)
    red.broadcasted(names, res_b)
    red.grad["wo"] = dw_o.reshape(N_CHIPS, Q_WIDTH // N_CHIPS, D_MODEL)
    (dq, dkc, dkp, dvc, dvp, dsink), (res_a, res_b) = attn_bwd(
        qkv, sink_rows, do, name="attn_bwd", riders=[red.scatter(["wgu0"]), red.exchange(["wo"])])
    red.scattered(["wgu0"], res_a, where)
    red.exchanged(["wo"], res_b)
    names, rider = red.broadcast([("wgu", 0)])
    (dqkv, db_qkv), (res_a, res_b) = rope_bwd(dq, dkc, dkp, dvc, dvp, cos, sin, name="rope_bwd",
                                              riders=[rider, red.scatter(["wo"])])
    red.broadcasted(names, res_a)
    red.scattered(["wo"], res_b, where)
    names, rider = red.broadcast([("wo", None)])
    red.grad["qkv"], (res,) = mm_tn(h0, dqkv, shard_major=True, tm=D_MODEL, tn=QKV_WIDTH // N_CHIPS, name="dw_qkv",
                                    riders=[rider])
    red.broadcasted(names, res)
    grad_x, dg_mpre0 = dh_norm_bwd_last(dqkv, w_qkv, dx1, x0, gain(norm_mix_pre, 0), name="dh_attn_norm_in")

    norm_grads = [jnp.concatenate(p, axis=0) for p in
                  ((dg_mpre0, dg_mpre1), (dg_mpost0, dg_mpost1), (dg_fpre0, dg_fpre1), (dg_fpost0, dg_fpost1))]
    red.grad["small"] = _pack_small(norm_grads, db_qkv, db_o, dsink[:, :, 0, 0], db_sp[:, :, 0], dln_g, dln_b, dw_sp,
                                    loss_part)
    def big_update(w, g, m, v, tag, after=None):
        return adamw(w, g.reshape(w.shape), m, v, name=f"adamw_{tag}", after=after)

    (res,) = comm_call([red.exchange(["qkv", "small"])], name="tail_1")
    red.exchanged(["qkv", "small"], res)
    state, token = comm_start([red.scatter(["qkv", "small"])], name="tail_2_start")
    upd_wgu = big_update(ffn_w_gate_up, red.dest["wgu"], m_ffn_w_gate_up, v_ffn_w_gate_up, "wgu", after=token)
    (res,) = comm_wait(state, upd_wgu[1], name="tail_2_wait")
    red.scattered(["qkv", "small"], res, where)
    names, rider = red.broadcast([("qkv", None)])
    state, token = comm_start([rider, allcast_rider(red.dest["slab"])], name="tail_3_start")
    upd_wd = big_update(ffn_w_down, red.dest["wd"], m_ffn_w_down, v_ffn_w_down, "wd", after=token)
    (res_a,), (slab_full,) = comm_wait(state, upd_wd[1], name="tail_3_wait")
    red.broadcasted(names, [res_a])
    g_qkv, g_wo, g_win, g_wout = (red.dest[n] for n in ("qkv", "wo", "win", "wout"))
    g_norms, g_bqkv, g_bo, g_sinks, g_bsp, g_lng, g_lnb, g_wsp, loss = _unpack_small(slab_full, chip)

    upd = {
        "attn_w_qkv": big_update(attn_w_qkv, g_qkv, m_attn_w_qkv, v_attn_w_qkv, "qkv"),
        "attn_w_o": big_update(attn_w_o, g_wo, m_attn_w_o, v_attn_w_o, "wo"),
        "sgu_w_in": big_update(sgu_w_in, g_win, m_sgu_w_in, v_sgu_w_in, "win"),
        "sgu_w_out": big_update(sgu_w_out, g_wout, m_sgu_w_out, v_sgu_w_out, "wout"),
        "ffn_w_gate_up": upd_wgu,
        "ffn_w_down": upd_wd,
    }
    small_names = ["norm_mix_pre", "norm_mix_post", "norm_ffn_pre", "norm_ffn_post", "attn_b_qkv", "attn_sinks", "attn_b_o",
                   "sgu_ln_g", "sgu_ln_b", "sgu_w_spatial", "sgu_b_spatial"]
    small_w = [norm_mix_pre, norm_mix_post, norm_ffn_pre, norm_ffn_post, attn_b_qkv, attn_sinks, attn_b_o, sgu_ln_g, sgu_ln_b,
               sgu_w_spatial, sgu_b_spatial]
    small_m = [m_norm_mix_pre, m_norm_mix_post, m_norm_ffn_pre, m_norm_ffn_post, m_attn_b_qkv, m_attn_sinks, m_attn_b_o,
               m_sgu_ln_g, m_sgu_ln_b, m_sgu_w_spatial, m_sgu_b_spatial]
    small_v = [v_norm_mix_pre, v_norm_mix_post, v_norm_ffn_pre, v_norm_ffn_post, v_attn_b_qkv, v_attn_sinks, v_attn_b_o,
               v_sgu_ln_g, v_sgu_ln_b, v_sgu_w_spatial, v_sgu_b_spatial]
    small_g = g_norms + [g_bqkv, g_sinks, g_bo, g_lng, g_lnb, g_wsp, g_bsp]

    def flat2(a):
        return a.reshape(-1, a.shape[-1])

    res = adamw_small([flat2(a) for a in small_w], [flat2(a) for a in small_g], [flat2(a) for a in small_m],
                      [flat2(a) for a in small_v], name="adamw_small")
    for i, nm in enumerate(small_names):
        upd[nm] = tuple(r[i].reshape(small_w[i].shape) for r in res)

    order = ["norm_mix_pre", "norm_mix_post", "norm_ffn_pre", "norm_ffn_post", "attn_w_qkv", "attn_b_qkv", "attn_sinks",
             "attn_w_o", "attn_b_o", "sgu_w_in", "sgu_ln_g", "sgu_ln_b", "sgu_w_spatial", "sgu_b_spatial", "sgu_w_out",
             "ffn_w_gate_up", "ffn_w_down"]
    outs = [loss, grad_x.reshape(1, s, D_MODEL)]
    for part in range(4):
        outs += [upd[nm][part] for nm in order]
    return tuple(outs)
```

```python
import types

import numpy as np
import jax
import jax.numpy as jnp
from jax import lax
from jax.experimental import pallas as pl
from jax.experimental.pallas import tpu as pltpu

F32 = jnp.float32
BF16 = jnp.bfloat16
I32 = jnp.int32

D_MODEL = 1024
HEAD_DIM = 64
N_Q_HEADS = 16
N_KV_HEADS = 4
GQA_GROUP = 4
WINDOW = 128
Q_WIDTH = 1024
KV_WIDTH = 256
QKV_WIDTH = 1536
ROPE_THETA = 10000.0
SGU_GROUPS = 8
SGU_CHUNK = 128
D_FF = 2816
FF_HALF = D_FF // 2
EPS = 1e-6
N_CHIPS = 4
LANES = 128

ADAM_LR = 0.001
ADAM_B1 = 0.9
ADAM_B2 = 0.999
ADAM_EPS = 1e-08
ADAM_WD = 0.01
ADAM_STEP = 10

VMEM_LIMIT = 52 * 1024 * 1024
BIG_VMEM_LIMIT = 62 * 1024 * 1024
SUB_ROWS = 256
MESH = pl.DeviceIdType.MESH
NEG = -1e30
NT_DIMS = (((1,), (1,)), ((), ()))
TN_DIMS = (((0,), (0,)), ((), ()))
NN_DIMS = (((1,), (0,)), ((), ()))
ANY = pl.BlockSpec(memory_space=pl.ANY)


def _row_tile(s, want):
    return want if s % want == 0 else s


PEER_KINDS = ("sibling", "chips", "sibling+chips", "everyone")


def _peer_kind(riders):
    kinds = {r.peers for r in riders}
    if not kinds:
        return None
    if "everyone" in kinds:
        return "everyone"
    return "sibling+chips" if len(kinds) == 2 else kinds.pop()


def _peer_barrier(kind):
    x, y, c = _place()
    chips = [(*_partner(x, y, k), c) for k in (1, 2, 3)]
    peers = {"sibling": [(x, y, 1 - c)], "chips": chips, "sibling+chips": [(x, y, 1 - c)] + chips,
             "everyone": [(x, y, 1 - c)] + chips + [(px, py, 1 - c) for px, py, _ in chips]}[kind]
    barrier = pltpu.get_barrier_semaphore()
    for dev in peers:
        pl.semaphore_signal(barrier, inc=1, device_id=dev, device_id_type=MESH)
    pl.semaphore_wait(barrier, len(peers))


def _call(body, *, name, grid=(), in_specs=(), out_specs=(), out_shape=(), scratch_shapes=(), operands=(), prefetch=(),
          aliases=None, riders=(), sem=None, vmem_limit=VMEM_LIMIT):
    n_pre, n_in, n_out, n_scr = len(prefetch), len(operands), len(out_shape), len(scratch_shapes)
    in_specs, out_specs, out_shape = list(in_specs), list(out_specs), list(out_shape)
    operands, scratch_shapes = list(operands), list(scratch_shapes)
    io_alias = {n_pre + i: o for i, o in (aliases or {}).items()}
    for r in riders:
        base_in, base_out = n_pre + len(operands), len(out_shape)
        operands += list(r.inputs)
        in_specs += [ANY] * len(r.inputs)
        for pos, i in enumerate(r.aliased):
            io_alias[base_in + i] = base_out + pos
            out_shape.append(jax.ShapeDtypeStruct(r.inputs[i].shape, r.inputs[i].dtype))
        out_shape += list(r.fresh)
        out_specs += [ANY] * (len(r.aliased) + len(r.fresh))
        scratch_shapes += [pltpu.SemaphoreType.DMA((r.nsem,)), pltpu.SemaphoreType.DMA((r.nsem,))]

    def wrapped(*refs):
        pre, p = refs[:n_pre], n_pre
        core_in, p = refs[p:p + n_in], p + n_in
        r_in = []
        for r in riders:
            r_in.append(refs[p:p + len(r.inputs)])
            p += len(r.inputs)
        core_out, p = refs[p:p + n_out], p + n_out
        r_out = []
        for r in riders:
            k = len(r.aliased) + len(r.fresh)
            r_out.append(refs[p:p + k])
            p += k
        core_scr, p = refs[p:p + n_scr], p + n_scr
        r_sem = [refs[p + 2 * i:p + 2 * i + 2] for i in range(len(riders))]

        def edge(at_last, fns):
            def run():
                if not at_last:
                    _peer_barrier(peer_kind)
                for i, r in enumerate(riders):
                    getattr(r, fns)(r_in[i], r_out[i], r_sem[i][0], r_sem[i][1])
            if not riders:
                return
            if not grid:
                run()
                return
            cond = None
            for d, n in enumerate(grid):
                c = pl.program_id(d) == (n - 1 if at_last else 0)
                cond = c if cond is None else jnp.logical_and(cond, c)
            pl.when(cond)(run)

        edge(False, "start")
        if body is not None:
            body(*pre, *core_in, *core_out, *core_scr)
        edge(True, "finish")

    if sem is None or riders:
        sem = ("arbitrary",) * len(grid)
    kwargs = dict(out_shape=out_shape, input_output_aliases=io_alias, name=name)
    peer_kind = _peer_kind(riders)
    collective = {} if peer_kind is None else {"collective_id": PEER_KINDS.index(peer_kind)}
    if grid:
        kwargs["compiler_params"] = pltpu.CompilerParams(dimension_semantics=sem, vmem_limit_bytes=vmem_limit, **collective)
    elif collective:
        kwargs["compiler_params"] = pltpu.CompilerParams(**collective)
    if n_pre:
        kwargs["grid_spec"] = pltpu.PrefetchScalarGridSpec(
            num_scalar_prefetch=n_pre, grid=grid, in_specs=in_specs, out_specs=out_specs, scratch_shapes=scratch_shapes)
    else:
        kwargs.update(grid=grid, in_specs=in_specs, out_specs=out_specs, scratch_shapes=scratch_shapes)
    res = pl.pallas_call(wrapped, **kwargs)(*prefetch, *operands)
    core, rest, rider_res = list(res[:n_out]), list(res[n_out:]), []
    for r in riders:
        k = len(r.aliased) + len(r.fresh)
        rider_res.append(rest[:k])
        rest = rest[k:]
    return core, rider_res


def _mm_call(*, grid, in_specs, out_spec, out_shape, dims, nk, kaxis, acc_shape, name, operands, riders=()):
    out_dtype = out_shape.dtype

    def body(a_ref, b_ref, o_ref, *scratch):
        p = lax.dot_general(a_ref[...].astype(BF16), b_ref[...].astype(BF16), dims, preferred_element_type=F32)
        if nk == 1:
            o_ref[...] = p.astype(out_dtype)
        else:
            acc = scratch[0]
            kk = pl.program_id(kaxis)

            @pl.when(kk == 0)
            def _():
                acc[...] = p

            @pl.when(kk > 0)
            def _():
                acc[...] += p

            @pl.when(kk == nk - 1)
            def _():
                o_ref[...] = acc[...].astype(out_dtype)

    sem = ["parallel"] * len(grid)
    if nk > 1:
        sem[kaxis] = "arbitrary"
    (out,), rider_res = _call(
        body, grid=grid, in_specs=in_specs, out_specs=[out_spec], out_shape=[out_shape],
        scratch_shapes=[pltpu.VMEM(acc_shape, F32)] if nk > 1 else [], operands=operands, name=name, riders=riders,
        sem=tuple(sem))
    return (out, rider_res) if riders else out


def mm_nt(a, w, *, out_dtype, name, tm=1024, riders=()):
    m, n = a.shape
    kout = w.shape[0]
    tm = _row_tile(m, tm)
    return _mm_call(grid=(m // tm,),
                    in_specs=[pl.BlockSpec((tm, n), lambda i: (i, 0)), pl.BlockSpec((kout, n), lambda i: (0, 0))],
                    out_spec=pl.BlockSpec((tm, kout), lambda i: (i, 0)),
                    out_shape=jax.ShapeDtypeStruct((m, kout), out_dtype), dims=NT_DIMS, nk=1, kaxis=0,
                    acc_shape=None, name=name, operands=(a, w), riders=riders)


def mm_tn(a, b, *, shard_major, name, tm, tn, tk=None, out_dtype=BF16, riders=()):
    s, m = a.shape
    tk = s if tk is None else _row_tile(s, tk)
    if b.ndim == 3:
        n = 2 * b.shape[2]
        b_spec = pl.BlockSpec((None, tk, tn), lambda j, i, kk: (j // 2, kk, j % 2))
    else:
        n = b.shape[1]
        b_spec = pl.BlockSpec((tk, tn), lambda j, i, kk: (kk, j))
    if shard_major:
        assert tn == n // N_CHIPS
        o_spec = pl.BlockSpec((None, tm, tn), lambda j, i, kk: (j, i, 0))
        o_shape = jax.ShapeDtypeStruct((N_CHIPS, m, tn), out_dtype)
    else:
        o_spec = pl.BlockSpec((tm, tn), lambda j, i, kk: (i, j))
        o_shape = jax.ShapeDtypeStruct((m, n), out_dtype)
    return _mm_call(grid=(n // tn, m // tm, s // tk),
                    in_specs=[pl.BlockSpec((tk, tm), lambda j, i, kk: (kk, i)), b_spec], out_spec=o_spec,
                    out_shape=o_shape, dims=TN_DIMS, nk=s // tk, kaxis=2, acc_shape=(tm, tn), name=name, operands=(a, b),
                    riders=riders)


def _rstd(x):
    return lax.rsqrt(jnp.mean(x * x, axis=-1, keepdims=True) + EPS)


def _rms_bwd(dy, x, g):
    r = _rstd(x)
    xhat = x * r
    gy = dy * g
    dx = r * (gy - xhat * jnp.mean(gy * xhat, axis=-1, keepdims=True))
    return dx, jnp.sum(dy * xhat, axis=0, keepdims=True)


def _accum(ref, val, first):
    @pl.when(first)
    def _():
        ref[...] = val

    @pl.when(jnp.logical_not(first))
    def _():
        ref[...] += val


def _row_spec(tm, width):
    return pl.BlockSpec((tm, width), lambda i: (i, 0))


def _vec_spec(width):
    return pl.BlockSpec((1, width), lambda i: (0, 0))


def _ret(core, rider_res, riders):
    core = core[0] if len(core) == 1 else core
    return (core, rider_res) if riders else core


def prenorm_and_place(x, g, w, layer, chip_arr, *, name, tm=256, riders=()):
    s = x.shape[0]
    tm = _row_tile(s, tm)
    steps = s // tm
    _, r, c = w.shape
    tr = r // steps
    assert tr * steps == r and tr % 16 == 0

    def body(chip_ref, x_ref, g_ref, w_ref, h_ref, o_ref):
        xv = x_ref[...]
        h_ref[...] = (xv * _rstd(xv) * g_ref[...]).astype(BF16)
        o_ref[...] = w_ref[...].astype(BF16)

    core, rr = _call(
        body, grid=(steps,), prefetch=(chip_arr,),
        in_specs=[pl.BlockSpec((tm, D_MODEL), lambda i, chip: (i, 0)), pl.BlockSpec((1, D_MODEL), lambda i, chip: (0, 0)),
                  pl.BlockSpec((None, tr, c), lambda i, chip: (layer, i, 0))],
        out_specs=[pl.BlockSpec((tm, D_MODEL), lambda i, chip: (i, 0)), pl.BlockSpec((None, tr, c), lambda i, chip: (chip[0], i, 0))],
        out_shape=[jax.ShapeDtypeStruct((s, D_MODEL), BF16), jax.ShapeDtypeStruct((N_CHIPS, r, c), BF16)],
        operands=(x, g, w), sem=("parallel",), name=name, riders=riders)
    return _ret(core, rr, riders)


def proj_residual_norm(a, w, x, bias, g_post, g_next, *, name, tm=512, sub=256, riders=()):
    s, k = a.shape
    tm = _row_tile(s, tm)
    sub = min(sub, tm)

    def body(a_ref, w_ref, x_ref, b_ref, gp_ref, gn_ref, xo_ref, h_ref, m_ref):
        for t in range(tm // sub):
            rows = slice(t * sub, (t + 1) * sub)
            mv = jnp.dot(a_ref[rows, :], w_ref[...], preferred_element_type=F32) + b_ref[...]
            m_ref[rows, :] = mv.astype(BF16)
            xn = x_ref[rows, :] + mv * _rstd(mv) * gp_ref[...]
            xo_ref[rows, :] = xn
            h_ref[rows, :] = (xn * _rstd(xn) * gn_ref[...]).astype(BF16)

    row, vec = _row_spec(tm, D_MODEL), _vec_spec(D_MODEL)
    core, rr = _call(
        body, grid=(s // tm,),
        in_specs=[_row_spec(tm, k), pl.BlockSpec((k, D_MODEL), lambda i: (0, 0)), row, vec, vec, vec], out_specs=[row, row, row],
        out_shape=[jax.ShapeDtypeStruct((s, D_MODEL), F32), jax.ShapeDtypeStruct((s, D_MODEL), BF16),
                   jax.ShapeDtypeStruct((s, D_MODEL), BF16)],
        operands=(a, w, x, bias, g_post, g_next), sem=("parallel",), name=name, riders=riders)
    return _ret(core, rr, riders)


def ffn_fwd_loss_rows(h, w_gu, w_d, x, g_post, target, *, name, tm=512, riders=()):
    s = x.shape[0]
    tm = _row_tile(s, tm)

    def body(h_ref, w0, w1, w2, w3, wd_ref, x_ref, g_ref, t_ref, d_ref, a_ref, dx_ref, df_ref, dg_ref, loss_ref):
        first = pl.program_id(0) == 0
        halves = [slice(half * FF_HALF, (half + 1) * FF_HALF) for half in (0, 1)]
        gain = g_ref[...]
        sub = min(SUB_ROWS, tm)
        sums = None
        for t in range(tm // sub):
            rows = slice(t * sub, (t + 1) * sub)
            hv = h_ref[rows, :]
            fv = None
            for cols, (wg_ref, wu_ref) in zip(halves, ((w0, w2), (w1, w3))):
                g = jnp.dot(hv, wg_ref[...], preferred_element_type=F32)
                u = jnp.dot(hv, wu_ref[...], preferred_element_type=F32)
                sig = _sigmoid(g)
                silu = g * sig
                d_ref[0, rows, cols] = (u * (sig + silu * (1.0 - sig))).astype(BF16)
                d_ref[1, rows, cols] = silu.astype(BF16)
                act = (silu * u).astype(BF16)
                a_ref[rows, cols] = act
                p = jnp.dot(act, wd_ref[cols, :], preferred_element_type=F32)
                fv = p if fv is None else fv + p
            err = x_ref[rows, :] + fv * _rstd(fv) * gain - t_ref[rows, :]
            dx = err * (1.0 / D_MODEL)
            dx_ref[rows, :] = dx
            df, dg = _rms_bwd(dx, fv, gain)
            df_ref[rows, :] = df.astype(BF16)
            part = (dg, jnp.sum(jnp.sum(err * err, axis=-1, keepdims=True), axis=0, keepdims=True) * (0.5 / D_MODEL))
            sums = part if sums is None else tuple(a + b for a, b in zip(sums, part))
        _accum(dg_ref, sums[0], first)
        _accum(loss_ref, jnp.broadcast_to(sums[1], (8, LANES)), first)

    def resident(shape, index):
        return pl.BlockSpec(shape, index, pipeline_mode=pl.Buffered(1))

    row, vec = _row_spec(tm, D_MODEL), _vec_spec(D_MODEL)
    shards = [resident((None, D_MODEL, FF_HALF), (lambda j: (lambda i: (j, 0, 0)))(j)) for j in range(N_CHIPS)]
    core, rr = _call(
        body, grid=(s // tm,),
        in_specs=[row] + shards + [resident((D_FF, D_MODEL), lambda i: (0, 0)), row, vec, row],
        out_specs=[pl.BlockSpec((2, tm, D_FF), lambda i: (0, i, 0)), _row_spec(tm, D_FF), row, row, vec,
                   pl.BlockSpec((8, LANES), lambda i: (0, 0))],
        out_shape=[jax.ShapeDtypeStruct((2, s, D_FF), BF16), jax.ShapeDtypeStruct((s, D_FF), BF16),
                   jax.ShapeDtypeStruct((s, D_MODEL), F32), jax.ShapeDtypeStruct((s, D_MODEL), BF16),
                   jax.ShapeDtypeStruct((1, D_MODEL), F32), jax.ShapeDtypeStruct((8, LANES), F32)],
        operands=(h, w_gu, w_gu, w_gu, w_gu, w_d, x, g_post, target), name=name, riders=riders, vmem_limit=BIG_VMEM_LIMIT)
    return _ret(core, rr, riders)


def dh_norm_bwd_pair(a, w, dres, x, g_pre, m, g_post, *, name, tm=512, sub=256, riders=()):
    _, kout, ns = w.shape
    planes = a.ndim == 3
    s = x.shape[0]
    tm = _row_tile(s, tm)
    sub = min(sub, tm)
    a_spec = pl.BlockSpec((2, tm, 2 * ns), lambda i: (0, i, 0)) if planes else pl.BlockSpec((tm, N_CHIPS * ns), lambda i: (i, 0))

    def body(a_ref, w0, w1, w2, w3, dres_ref, x_ref, gpre_ref, m_ref, gpost_ref, dx_ref, dm_ref, dgpre_ref, dgpost_ref, db_ref):
        first = pl.program_id(0) == 0
        sums = None
        for t in range(tm // sub):
            rows = slice(t * sub, (t + 1) * sub)
            dh = None
            for j, w_ref in enumerate((w0, w1, w2, w3)):
                a_j = a_ref[j // 2, rows, (j % 2) * ns:(j % 2 + 1) * ns] if planes else a_ref[rows, j * ns:(j + 1) * ns]
                p = lax.dot_general(a_j, w_ref[...], NT_DIMS, preferred_element_type=F32)
                dh = p if dh is None else dh + p
            d1, dgpre = _rms_bwd(dh, x_ref[rows, :], gpre_ref[...])
            dx = dres_ref[rows, :] + d1
            dx_ref[rows, :] = dx
            dm, dgpost = _rms_bwd(dx, m_ref[rows, :].astype(F32), gpost_ref[...])
            dm_ref[rows, :] = dm.astype(BF16)
            part = (dgpre, dgpost, jnp.sum(dm, axis=0, keepdims=True))
            sums = part if sums is None else tuple(u + v for u, v in zip(sums, part))
        _accum(dgpre_ref, sums[0], first)
        _accum(dgpost_ref, sums[1], first)
        _accum(db_ref, sums[2], first)

    def shard(j):
        return pl.BlockSpec((None, kout, ns), lambda i: (j, 0, 0))

    row, vec = _row_spec(tm, D_MODEL), _vec_spec(D_MODEL)
    vshape = jax.ShapeDtypeStruct((1, D_MODEL), F32)
    core, rr = _call(
        body, grid=(s // tm,), in_specs=[a_spec] + [shard(j) for j in range(N_CHIPS)] + [row, row, vec, row, vec],
        out_specs=[row, row, vec, vec, vec],
        out_shape=[jax.ShapeDtypeStruct((s, D_MODEL), F32), jax.ShapeDtypeStruct((s, D_MODEL), BF16), vshape, vshape, vshape],
        operands=(a, w, w, w, w, dres, x, g_pre, m, g_post), name=name, riders=riders)
    return _ret(core, rr, riders)


def ffn_bwd_rows(df, w_d, d_planes, w_gu, dres, x, g_pre, m, g_post, *, name, tm=512, riders=()):
    s = x.shape[0]
    tm = _row_tile(s, tm)

    def body(df_ref, wd_ref, d_ref, w0, w1, w2, w3, dres_ref, x_ref, gpre_ref, m_ref, gpost_ref,
             o_ref, dx_ref, dm_ref, dgpre_ref, dgpost_ref, db_ref):
        first = pl.program_id(0) == 0
        halves = [slice(half * FF_HALF, (half + 1) * FF_HALF) for half in (0, 1)]
        sub = min(SUB_ROWS, tm)
        sums = None
        for t in range(tm // sub):
            rows = slice(t * sub, (t + 1) * sub)
            dfv = df_ref[rows, :]
            dh = None
            for cols, (wg_ref, wu_ref) in zip(halves, ((w0, w2), (w1, w3))):
                da = lax.dot_general(dfv, wd_ref[cols, :], NT_DIMS, preferred_element_type=F32)
                dg = (da * d_ref[0, rows, cols].astype(F32)).astype(BF16)
                du = (da * d_ref[1, rows, cols].astype(F32)).astype(BF16)
                o_ref[0, rows, cols] = dg
                o_ref[1, rows, cols] = du
                p = lax.dot_general(dg, wg_ref[...], NT_DIMS, preferred_element_type=F32)
                p += lax.dot_general(du, wu_ref[...], NT_DIMS, preferred_element_type=F32)
                dh = p if dh is None else dh + p
            d1, dgpre = _rms_bwd(dh, x_ref[rows, :], gpre_ref[...])
            dx = dres_ref[rows, :] + d1
            dx_ref[rows, :] = dx
            dm, dgpost = _rms_bwd(dx, m_ref[rows, :].astype(F32), gpost_ref[...])
            dm_ref[rows, :] = dm.astype(BF16)
            part = (dgpre, dgpost, jnp.sum(dm, axis=0, keepdims=True))
            sums = part if sums is None else tuple(a + b for a, b in zip(sums, part))
        _accum(dgpre_ref, sums[0], first)
        _accum(dgpost_ref, sums[1], first)
        _accum(db_ref, sums[2], first)

    def resident(shape, index):
        return pl.BlockSpec(shape, index, pipeline_mode=pl.Buffered(1))

    planes = pl.BlockSpec((2, tm, D_FF), lambda i: (0, i, 0))
    row, vec = _row_spec(tm, D_MODEL), _vec_spec(D_MODEL)
    vshape = jax.ShapeDtypeStruct((1, D_MODEL), F32)
    shards = [resident((None, D_MODEL, FF_HALF), (lambda j: (lambda i: (j, 0, 0)))(j)) for j in range(N_CHIPS)]
    core, rr = _call(
        body, grid=(s // tm,),
        in_specs=[row, resident((D_FF, D_MODEL), lambda i: (0, 0)), planes] + shards + [row, row, vec, row, vec],
        out_specs=[planes, row, row, vec, vec, vec],
        out_shape=[jax.ShapeDtypeStruct((2, s, D_FF), BF16), jax.ShapeDtypeStruct((s, D_MODEL), F32),
                   jax.ShapeDtypeStruct((s, D_MODEL), BF16), vshape, vshape, vshape],
        operands=(df, w_d, d_planes, w_gu, w_gu, w_gu, w_gu, dres, x, g_pre, m, g_post), name=name, riders=riders,
        vmem_limit=BIG_VMEM_LIMIT)
    return _ret(core, rr, riders)


def dh_norm_bwd_last(a, w, dres, x, g_pre, *, name, tm=512, sub=256):
    _, kout, ns = w.shape
    s = x.shape[0]
    tm = _row_tile(s, tm)
    sub = min(sub, tm)

    def body(a_ref, w0, w1, w2, w3, dres_ref, x_ref, g_ref, dx_ref, dg_ref):
        total = None
        for t in range(tm // sub):
            rows = slice(t * sub, (t + 1) * sub)
            dh = None
            for j, w_ref in enumerate((w0, w1, w2, w3)):
                p = lax.dot_general(a_ref[rows, j * ns:(j + 1) * ns], w_ref[...], NT_DIMS, preferred_element_type=F32)
                dh = p if dh is None else dh + p
            d1, dg = _rms_bwd(dh, x_ref[rows, :], g_ref[...])
            dx_ref[rows, :] = dres_ref[rows, :] + d1
            total = dg if total is None else total + dg
        _accum(dg_ref, total, pl.program_id(0) == 0)

    def shard(j):
        return pl.BlockSpec((None, kout, ns), lambda i: (j, 0, 0))

    row, vec = _row_spec(tm, D_MODEL), _vec_spec(D_MODEL)
    (dx, dg), _ = _call(
        body, grid=(s // tm,), in_specs=[_row_spec(tm, N_CHIPS * ns)] + [shard(j) for j in range(N_CHIPS)] + [row, row, vec],
        out_specs=[row, vec], out_shape=[jax.ShapeDtypeStruct((s, D_MODEL), F32), jax.ShapeDtypeStruct((1, D_MODEL), F32)],
        operands=(a, w, w, w, w, dres, x, g_pre), name=name)
    return dx, dg


def _rope_tables(s):
    half = HEAD_DIM // 2
    inv_freq = np.float32(ROPE_THETA) ** (-(np.arange(half, dtype=np.float32) * np.float32(2.0)) / np.float32(HEAD_DIM))
    ang = np.arange(s, dtype=np.float32)[:, None] * inv_freq[None, :]
    cos, sin = np.cos(ang).astype(np.float32), np.sin(ang).astype(np.float32)
    return jnp.asarray(np.tile(cos, (1, 4))), jnp.asarray(np.concatenate([-sin, sin, -sin, sin], axis=1))


def _swap_halves(x):
    lane = lax.broadcasted_iota(I32, x.shape, 1)
    return jnp.where((lane & (HEAD_DIM - 1)) < HEAD_DIM // 2, pltpu.roll(x, LANES - 32, 1), pltpu.roll(x, 32, 1))


N_ROPE_BLOCKS = (Q_WIDTH + KV_WIDTH) // LANES


def qkv_proj(h, w, bias, cos, sin, *, name, tm=1024, riders=()):
    s, k = h.shape
    ns = w.shape[2]
    tm = _row_tile(s, tm)

    def body(h_ref, w_ref, b_ref, c_ref, s_ref, o_ref):
        j = pl.program_id(0)
        sub = min(256, tm)
        for t in range(tm // sub):
            rows = slice(t * sub, (t + 1) * sub)
            p = jnp.dot(h_ref[rows, :], w_ref[...], preferred_element_type=F32) + b_ref[...]
            cosv, sinv = c_ref[rows, :], s_ref[rows, :]
            for blk in range(ns // LANES):
                xb = p[:, blk * LANES:(blk + 1) * LANES]
                roped = xb * cosv + _swap_halves(xb) * sinv
                is_qk = j * (ns // LANES) + blk < N_ROPE_BLOCKS
                o_ref[rows, blk * LANES:(blk + 1) * LANES] = jnp.where(is_qk, roped, xb).astype(BF16)

    core, rr = _call(
        body, grid=(N_CHIPS, s // tm),
        in_specs=[pl.BlockSpec((tm, k), lambda j, i: (i, 0)), pl.BlockSpec((None, k, ns), lambda j, i: (j, 0, 0)),
                  pl.BlockSpec((1, ns), lambda j, i: (0, j)), pl.BlockSpec((tm, LANES), lambda j, i: (i, 0)),
                  pl.BlockSpec((tm, LANES), lambda j, i: (i, 0))],
        out_specs=[pl.BlockSpec((tm, ns), lambda j, i: (i, j))], out_shape=[jax.ShapeDtypeStruct((s, N_CHIPS * ns), BF16)],
        operands=(h, w, bias, cos, sin), sem=("parallel", "parallel"), name=name, riders=riders)
    return _ret(core, rr, riders)


def rope_bwd(dq, dkc, dkp, dvc, dvp, cos, sin, *, name, riders=()):
    s = dq.shape[0]
    tm = 2 * WINDOW if s % (2 * WINDOW) == 0 else WINDOW
    nb = s // tm

    def body(dq_ref, dkc_ref, dkp_ref, dkp_next_ref, dvc_ref, dvp_ref, dvp_next_ref, c_ref, s_ref, o_ref, db_ref):
        i = pl.program_id(0)
        has_next = (i < nb - 1).astype(F32)
        cosv, sinv = c_ref[...], s_ref[...]

        def shifted(ref, next_ref, cols):
            last = has_next * next_ref[:WINDOW, cols].astype(F32)
            return last if tm == WINDOW else jnp.concatenate([ref[WINDOW:, cols].astype(F32), last], axis=0)

        parts = []
        for blk in range(QKV_WIDTH // LANES):
            if blk < Q_WIDTH // LANES:
                g = dq_ref[:, blk * LANES:(blk + 1) * LANES].astype(F32)
            else:
                own, prv, nxt = (dkc_ref, dkp_ref, dkp_next_ref) if blk < N_ROPE_BLOCKS else (dvc_ref, dvp_ref, dvp_next_ref)
                cols = slice((blk % 2) * LANES, (blk % 2 + 1) * LANES)
                g = own[:, cols].astype(F32) + shifted(prv, nxt, cols)
            if blk < N_ROPE_BLOCKS:
                g = g * cosv + _swap_halves(g * sinv)
            o_ref[:, blk * LANES:(blk + 1) * LANES] = g.astype(BF16)
            parts.append(jnp.sum(g, axis=0, keepdims=True))
        sums = jnp.concatenate(parts, axis=1)
        _accum(db_ref, sums, i == 0)

    own_spec = _row_spec(tm, KV_WIDTH)
    next_spec = pl.BlockSpec((tm, KV_WIDTH), lambda i: (jnp.minimum(i + 1, nb - 1), 0))
    core, rr = _call(
        body, grid=(nb,),
        in_specs=[_row_spec(tm, Q_WIDTH), own_spec, own_spec, next_spec, own_spec, own_spec, next_spec,
                  _row_spec(tm, LANES), _row_spec(tm, LANES)],
        out_specs=[_row_spec(tm, QKV_WIDTH), _vec_spec(QKV_WIDTH)],
        out_shape=[jax.ShapeDtypeStruct((s, QKV_WIDTH), BF16), jax.ShapeDtypeStruct((1, QKV_WIDTH), F32)],
        operands=(dq, dkc, dkp, dkp, dvc, dvp, dvp, cos, sin), name=name, riders=riders)
    return _ret(core, rr, riders)


ROWS = GQA_GROUP * WINDOW


def _prev_slots():
    kpos = lax.broadcasted_iota(I32, (WINDOW, ROWS), 0)
    qpos = lax.broadcasted_iota(I32, (WINDOW, ROWS), 1) & (WINDOW - 1)
    return kpos > qpos


def _head_cols(ref, head):
    return ref[:, head * HEAD_DIM:(head + 1) * HEAD_DIM]


def _stack_heads(ref, h):
    return jnp.concatenate([_head_cols(ref, GQA_GROUP * h + g) for g in range(GQA_GROUP)], axis=0)


def _band(prev_ref, cur_ref, h):
    return jnp.concatenate([_head_cols(prev_ref, h), _head_cols(cur_ref, h)], axis=0)


def _pick(prev, band):
    return jnp.where(prev, band[:WINDOW], band[WINDOW:])


def _spread(prev, x):
    return jnp.concatenate([jnp.where(prev, x, 0.0), jnp.where(prev, 0.0, x)], axis=0).astype(BF16)


def _attn_probs(s_band, sink, prev, has_prev):
    scale = HEAD_DIM ** -0.5
    s = jnp.where(prev, jnp.where(has_prev, s_band[:WINDOW], NEG), s_band[WINDOW:]) * scale
    m = jnp.maximum(jnp.max(s, axis=0, keepdims=True), sink)
    e, es = jnp.exp(s - m), jnp.exp(sink - m)
    inv = 1.0 / (jnp.sum(e, axis=0, keepdims=True) + es)
    return e * inv, es * inv


def _attn_specs(nb):
    kcol, vcol = Q_WIDTH // KV_WIDTH, Q_WIDTH // KV_WIDTH + 1
    q_spec = pl.BlockSpec((WINDOW, Q_WIDTH), lambda n: (n, 0))
    return [q_spec,
            pl.BlockSpec((WINDOW, KV_WIDTH), lambda n: (n, kcol)),
            pl.BlockSpec((WINDOW, KV_WIDTH), lambda n: (jnp.maximum(n - 1, 0), kcol)),
            pl.BlockSpec((WINDOW, KV_WIDTH), lambda n: (n, vcol)),
            pl.BlockSpec((WINDOW, KV_WIDTH), lambda n: (jnp.maximum(n - 1, 0), vcol)),
            pl.BlockSpec((N_KV_HEADS, 8, ROWS), lambda n: (0, 0, 0))]


def attn_fwd(qkv, sink_rows, *, name, riders=()):
    s = qkv.shape[0]

    def body(q_ref, kc_ref, kp_ref, vc_ref, vp_ref, sink_ref, o_ref):
        prev = _prev_slots()
        has_prev = pl.program_id(0) > 0
        heads = range(N_KV_HEADS)
        s_bands = [lax.dot_general(_band(kp_ref, kc_ref, h), _stack_heads(q_ref, h), NT_DIMS, preferred_element_type=F32)
                   for h in heads]
        p_bands = [_spread(prev, _attn_probs(s_bands[h], sink_ref[h, 0:1, :], prev, has_prev)[0]) for h in heads]
        outs = [lax.dot_general(_band(vp_ref, vc_ref, h), p_bands[h], TN_DIMS, preferred_element_type=F32).T for h in heads]
        for h in heads:
            for g in range(GQA_GROUP):
                head = GQA_GROUP * h + g
                o_ref[:, head * HEAD_DIM:(head + 1) * HEAD_DIM] = outs[h][g * WINDOW:(g + 1) * WINDOW].astype(BF16)

    core, rr = _call(
        body, grid=(s // WINDOW,), in_specs=_attn_specs(s // WINDOW), out_specs=[pl.BlockSpec((WINDOW, Q_WIDTH), lambda n: (n, 0))],
        out_shape=[jax.ShapeDtypeStruct((s, Q_WIDTH), BF16)], operands=(qkv, qkv, qkv, qkv, qkv, sink_rows), sem=("parallel",),
        name=name, riders=riders)
    return _ret(core, rr, riders)


def attn_bwd(qkv, sink_rows, do, *, name, riders=()):
    s = qkv.shape[0]

    def body(q_ref, kc_ref, kp_ref, vc_ref, vp_ref, sink_ref, do_ref, dq_ref, dkc_ref, dkp_ref, dvc_ref, dvp_ref, dsink_ref):
        n = pl.program_id(0)
        prev = _prev_slots()
        scale = HEAD_DIM ** -0.5
        heads = range(N_KV_HEADS)
        qs, dos = [_stack_heads(q_ref, h) for h in heads], [_stack_heads(do_ref, h) for h in heads]
        kbands, vbands = [_band(kp_ref, kc_ref, h) for h in heads], [_band(vp_ref, vc_ref, h) for h in heads]
        s_bands = [lax.dot_general(kbands[h], qs[h], NT_DIMS, preferred_element_type=F32) for h in heads]
        dp_bands = [lax.dot_general(vbands[h], dos[h], NT_DIMS, preferred_element_type=F32) for h in heads]
        ds_bands, p_bands, parts = [], [], []
        for h in heads:
            p, ps = _attn_probs(s_bands[h], sink_ref[h, 0:1, :], prev, n > 0)
            dp = _pick(prev, dp_bands[h])
            delta = jnp.sum(p * dp, axis=0, keepdims=True)
            ds_bands.append(_spread(prev, p * (dp - delta) * scale))
            p_bands.append(_spread(prev, p))
            dsink = -(ps * delta)
            for g in range(GQA_GROUP):
                parts.append(jnp.broadcast_to(jnp.sum(dsink[:, g * WINDOW:(g + 1) * WINDOW], axis=1, keepdims=True), (8, LANES)))
        for h in heads:
            dk = jnp.dot(ds_bands[h], qs[h], preferred_element_type=F32).astype(BF16)
            dv = jnp.dot(p_bands[h], dos[h], preferred_element_type=F32).astype(BF16)
            dq = lax.dot_general(kbands[h], ds_bands[h], TN_DIMS, preferred_element_type=F32).T
            cols = slice(h * HEAD_DIM, (h + 1) * HEAD_DIM)
            dkp_ref[:, cols], dkc_ref[:, cols] = dk[:WINDOW], dk[WINDOW:]
            dvp_ref[:, cols], dvc_ref[:, cols] = dv[:WINDOW], dv[WINDOW:]
            for g in range(GQA_GROUP):
                head = GQA_GROUP * h + g
                dq_ref[:, head * HEAD_DIM:(head + 1) * HEAD_DIM] = dq[g * WINDOW:(g + 1) * WINDOW].astype(BF16)

        @pl.when(n == 0)
        def _():
            for i, part in enumerate(parts):
                dsink_ref[i // GQA_GROUP, i % GQA_GROUP] = part

        @pl.when(n > 0)
        def _():
            for i, part in enumerate(parts):
                dsink_ref[i // GQA_GROUP, i % GQA_GROUP] += part

    rows_q = pl.BlockSpec((WINDOW, Q_WIDTH), lambda n: (n, 0))
    rows_kv = pl.BlockSpec((WINDOW, KV_WIDTH), lambda n: (n, 0))
    kv_shape = jax.ShapeDtypeStruct((s, KV_WIDTH), BF16)
    core, rr = _call(
        body, grid=(s // WINDOW,), in_specs=_attn_specs(s // WINDOW) + [rows_q],
        out_specs=[rows_q, rows_kv, rows_kv, rows_kv, rows_kv,
                   pl.BlockSpec((N_KV_HEADS, GQA_GROUP, 8, LANES), lambda n: (0, 0, 0, 0))],
        out_shape=[jax.ShapeDtypeStruct((s, Q_WIDTH), BF16), kv_shape, kv_shape, kv_shape, kv_shape,
                   jax.ShapeDtypeStruct((N_KV_HEADS, GQA_GROUP, 8, LANES), F32)],
        operands=(qkv, qkv, qkv, qkv, qkv, sink_rows, do), sem=("arbitrary",), name=name, riders=riders)
    return _ret(core, rr, riders)


GELU_C = 0.7978845608028654
GELU_A = 0.044715


def _gelu(x):
    return 0.5 * x * (1.0 + jnp.tanh(x * (GELU_C + (GELU_C * GELU_A) * (x * x))))


def _gelu_and_grad(x):
    x2 = x * x
    t = jnp.tanh(x * (GELU_C + (GELU_C * GELU_A) * x2))
    half_x, one_t = 0.5 * x, 1.0 + t
    return half_x * one_t, 0.5 * one_t + half_x * (1.0 - t * t) * (GELU_C + (3.0 * GELU_C * GELU_A) * x2)


def _tril_bf16(w):
    row = lax.broadcasted_iota(I32, (SGU_CHUNK, SGU_CHUNK), 0)
    col = lax.broadcasted_iota(I32, (SGU_CHUNK, SGU_CHUNK), 1)
    return jnp.where(row >= col, w, 0.0).astype(BF16)


def _sgu_norm(vg, g, b):
    mu = jnp.mean(vg, axis=-1, keepdims=True)
    cen = vg - mu
    rstd = lax.rsqrt(jnp.mean(cen * cen, axis=-1, keepdims=True) + EPS)
    xhat = cen * rstd
    return xhat, rstd, xhat * g + b


def sgu_in_fwd(h, w_in, ln_g, ln_b, w_sp, b_sp, *, name, tm=512, riders=()):
    s, k = h.shape
    ns = w_in.shape[2]
    tm = _row_tile(s, tm)

    def body(h_ref, w0, w1, w2, w3, g_ref, b_ref, w_ref, bs_ref, z_ref, y_ref):
        hv = h_ref[...]
        zs = [jnp.dot(hv, w_ref_j[...], preferred_element_type=F32) for w_ref_j in (w0, w1, w2, w3)]
        for j, zj in enumerate(zs):
            z_ref[:, j * ns:(j + 1) * ns] = zj.astype(BF16)
        u = _gelu(jnp.concatenate(zs[:2], axis=1))
        _, _, vn = _sgu_norm(_gelu(jnp.concatenate(zs[2:], axis=1)), g_ref[...], b_ref[...])
        vn = vn.astype(BF16)
        for grp in range(SGU_GROUPS):
            w = _tril_bf16(w_ref[grp])
            cols = slice(grp * LANES, (grp + 1) * LANES)
            for ch in range(tm // SGU_CHUNK):
                rows = slice(ch * SGU_CHUNK, (ch + 1) * SGU_CHUNK)
                mixed = jnp.dot(w, vn[rows, cols], preferred_element_type=F32) + bs_ref[grp]
                y_ref[rows, cols] = (u[rows, cols] * mixed).astype(BF16)

    def shard(j):
        return pl.BlockSpec((None, k, ns), lambda i: (j, 0, 0))

    full3 = pl.BlockSpec((SGU_GROUPS, SGU_CHUNK, SGU_CHUNK), lambda i: (0, 0, 0))
    core, rr = _call(
        body, grid=(s // tm,),
        in_specs=[_row_spec(tm, k)] + [shard(j) for j in range(N_CHIPS)] + [_vec_spec(D_MODEL), _vec_spec(D_MODEL), full3, full3],
        out_specs=[_row_spec(tm, 2 * D_MODEL), _row_spec(tm, D_MODEL)],
        out_shape=[jax.ShapeDtypeStruct((s, 2 * D_MODEL), BF16), jax.ShapeDtypeStruct((s, D_MODEL), BF16)],
        operands=(h, w_in, w_in, w_in, w_in, ln_g, ln_b, w_sp, b_sp), sem=("parallel",), name=name, riders=riders)
    return _ret(core, rr, riders)


def sgu_bwd(z, dy, ln_g, ln_b, w_sp, b_sp, *, name, tm=256, riders=()):
    s = z.shape[0]
    tm = _row_tile(s, tm)

    def body(z_ref, dy_ref, g_ref, b_ref, w_ref, bs_ref, dz_ref, dw_ref, dbs_ref, dg_ref, db_ref, dvn_buf):
        first = pl.program_id(0) == 0
        u, u_grad = _gelu_and_grad(z_ref[:, :D_MODEL].astype(F32))
        vg, v_grad = _gelu_and_grad(z_ref[:, D_MODEL:].astype(F32))
        xhat, rstd, vn = _sgu_norm(vg, g_ref[...], b_ref[...])
        vn = vn.astype(BF16)
        dyv = dy_ref[...]
        dmixed = dyv * u
        dz_gate = dyv * u_grad
        row = lax.broadcasted_iota(I32, (SGU_CHUNK, SGU_CHUNK), 0)
        col = lax.broadcasted_iota(I32, (SGU_CHUNK, SGU_CHUNK), 1)
        dws, dbss = [], []
        for grp in range(SGU_GROUPS):
            w = _tril_bf16(w_ref[grp])
            cols = slice(grp * LANES, (grp + 1) * LANES)
            dw = jnp.zeros((SGU_CHUNK, SGU_CHUNK), F32)
            dbs = jnp.zeros((SGU_CHUNK, 1), F32)
            for ch in range(tm // SGU_CHUNK):
                rows = slice(ch * SGU_CHUNK, (ch + 1) * SGU_CHUNK)
                vblk = vn[rows, cols]
                mixed = jnp.dot(w, vblk, preferred_element_type=F32) + bs_ref[grp]
                dz_ref[rows, cols] = (dz_gate[rows, cols] * mixed).astype(BF16)
                dm = dmixed[rows, cols]
                dmb = dm.astype(BF16)
                dvn_buf[rows, cols] = lax.dot_general(w, dmb, TN_DIMS, preferred_element_type=F32)
                dw += lax.dot_general(dmb, vblk, NT_DIMS, preferred_element_type=F32)
                dbs += jnp.sum(dm, axis=-1, keepdims=True)
            dws.append(jnp.where(row >= col, dw, 0.0))
            dbss.append(jnp.broadcast_to(dbs, (SGU_CHUNK, SGU_CHUNK)))

        dvn = dvn_buf[...]
        dxhat = dvn * g_ref[...]
        dvg = rstd * (dxhat - jnp.mean(dxhat, axis=-1, keepdims=True) - xhat * jnp.mean(dxhat * xhat, axis=-1, keepdims=True))
        dz_ref[:, D_MODEL:] = (dvg * v_grad).astype(BF16)
        dlng, dlnb = jnp.sum(dvn * xhat, axis=0, keepdims=True), jnp.sum(dvn, axis=0, keepdims=True)

        @pl.when(first)
        def _():
            for grp in range(SGU_GROUPS):
                dw_ref[grp] = dws[grp]
                dbs_ref[grp] = dbss[grp]
            dg_ref[...] = dlng
            db_ref[...] = dlnb

        @pl.when(jnp.logical_not(first))
        def _():
            for grp in range(SGU_GROUPS):
                dw_ref[grp] += dws[grp]
                dbs_ref[grp] += dbss[grp]
            dg_ref[...] += dlng
            db_ref[...] += dlnb

    full3 = pl.BlockSpec((SGU_GROUPS, SGU_CHUNK, SGU_CHUNK), lambda i: (0, 0, 0))
    s3 = jax.ShapeDtypeStruct((SGU_GROUPS, SGU_CHUNK, SGU_CHUNK), F32)
    vshape = jax.ShapeDtypeStruct((1, D_MODEL), F32)
    core, rr = _call(
        body, grid=(s // tm,),
        in_specs=[_row_spec(tm, 2 * D_MODEL), _row_spec(tm, D_MODEL), _vec_spec(D_MODEL), _vec_spec(D_MODEL), full3, full3],
        out_specs=[_row_spec(tm, 2 * D_MODEL), full3, full3, _vec_spec(D_MODEL), _vec_spec(D_MODEL)],
        out_shape=[jax.ShapeDtypeStruct((s, 2 * D_MODEL), BF16), s3, s3, vshape, vshape],
        scratch_shapes=[pltpu.VMEM((tm, D_MODEL), F32)], operands=(z, dy, ln_g, ln_b, w_sp, b_sp), name=name, riders=riders)
    return _ret(core, rr, riders)


def _sigmoid(x):
    return 1.0 / (1.0 + jnp.exp(-x))


def ffn_up(h, w_gu, *, name, tm=512, riders=()):
    s = h.shape[0]
    tm = _row_tile(s, tm)

    def body(h_ref, wg_ref, wu_ref, d_ref, a_ref):
        hv = h_ref[...]
        sub = min(256, tm)
        for t in range(tm // sub):
            rows = slice(t * sub, (t + 1) * sub)
            g = jnp.dot(hv[rows], wg_ref[...], preferred_element_type=F32)
            u = jnp.dot(hv[rows], wu_ref[...], preferred_element_type=F32)
            sig = _sigmoid(g)
            silu = g * sig
            d_ref[0, rows, :] = (u * (sig + silu * (1.0 - sig))).astype(BF16)
            d_ref[1, rows, :] = silu.astype(BF16)
            a_ref[rows, :] = (silu * u).astype(BF16)

    core, rr = _call(
        body, grid=(2, s // tm),
        in_specs=[pl.BlockSpec((tm, D_MODEL), lambda j, i: (i, 0)),
                  pl.BlockSpec((None, D_MODEL, FF_HALF), lambda j, i: (j, 0, 0)),
                  pl.BlockSpec((None, D_MODEL, FF_HALF), lambda j, i: (j + 2, 0, 0))],
        out_specs=[pl.BlockSpec((2, tm, FF_HALF), lambda j, i: (0, i, j)), pl.BlockSpec((tm, FF_HALF), lambda j, i: (i, j))],
        out_shape=[jax.ShapeDtypeStruct((2, s, D_FF), BF16), jax.ShapeDtypeStruct((s, D_FF), BF16)],
        operands=(h, w_gu, w_gu), sem=("parallel", "parallel"), name=name, riders=riders)
    return _ret(core, rr, riders)


def _weight_tile(rows):
    for tr in (512, 352, 256, 128):
        if rows % tr == 0:
            return tr
    return rows


def place_shard(w, layer, chip_arr, dtype, *, name, riders=()):
    _, r, c = w.shape
    tr = _weight_tile(r)

    def body(chip_ref, w_ref, o_ref):
        o_ref[...] = w_ref[...].astype(dtype)

    core, rr = _call(
        body, grid=(r // tr,), prefetch=(chip_arr,),
        in_specs=[pl.BlockSpec((None, tr, c), lambda i, chip: (layer, i, 0))],
        out_specs=[pl.BlockSpec((None, tr, c), lambda i, chip: (chip[0], i, 0))],
        out_shape=[jax.ShapeDtypeStruct((N_CHIPS, r, c), dtype)], operands=(w,), sem=("parallel",), name=name, riders=riders)
    return _ret(core, rr, riders)


def _adamw_math(w, g, m, v):
    m = ADAM_B1 * m + (1.0 - ADAM_B1) * g
    v = ADAM_B2 * v + (1.0 - ADAM_B2) * (g * g)
    m_hat = m / (1.0 - ADAM_B1 ** ADAM_STEP)
    v_hat = v / (1.0 - ADAM_B2 ** ADAM_STEP)
    delta = -ADAM_LR * (m_hat / (jnp.sqrt(v_hat) + ADAM_EPS) + ADAM_WD * w)
    return delta, m, v


def adamw(w, g, m, v, *, name, after=None):
    nl, r, c = w.shape
    tr = _weight_tile(r)

    def body(w_ref, g_ref, m_ref, v_ref, *rest):
        go_ref, d_ref, mo_ref, vo_ref = rest[-4:]
        gv = g_ref[...]
        go_ref[...] = gv
        d_ref[...], mo_ref[...], vo_ref[...] = _adamw_math(w_ref[...], gv, m_ref[...], v_ref[...])

    spec = pl.BlockSpec((None, tr, c), lambda l, i: (l, i, 0))
    shape = jax.ShapeDtypeStruct(w.shape, F32)
    extra = [] if after is None else [after]
    outs, _ = _call(body, grid=(nl, r // tr), in_specs=[spec] * 4 + [ANY] * len(extra), out_specs=[spec] * 4,
                    out_shape=[shape] * 4, operands=(w, g, m, v, *extra), sem=("parallel", "parallel"), name=name)
    return outs


def adamw_small(ws, gs, ms, vs, *, name):
    n = len(ws)

    def body(*refs):
        ins, outs = refs[:4 * n], refs[4 * n:]
        for t in range(n):
            gv = ins[n + t][...]
            outs[t][...] = gv
            outs[n + t][...], outs[2 * n + t][...], outs[3 * n + t][...] = _adamw_math(
                ins[t][...], gv, ins[2 * n + t][...], ins[3 * n + t][...])

    shapes = [jax.ShapeDtypeStruct(w.shape, F32) for w in ws]
    res = pl.pallas_call(body, out_shape=shapes * 4, name=name)(*ws, *gs, *ms, *vs)
    return res[:n], res[n:2 * n], res[2 * n:3 * n], res[3 * n:]


def pair_add(g, r1, c_arr, *, name):
    _, rows, cdim = g.shape
    h = rows // 2

    def body(c_ref, g_ref, r_ref, o_ref):
        o_ref[...] = (g_ref[...].astype(F32) + r_ref[...].astype(F32)).astype(o_ref.dtype)

    (out,), _ = _call(
        body, grid=(N_CHIPS,), prefetch=(c_arr,),
        in_specs=[pl.BlockSpec((None, h, cdim), lambda s, c: (s, c[0], 0)), pl.BlockSpec((None, h, cdim), lambda s, c: (s, 0, 0))],
        out_specs=[pl.BlockSpec((None, h, cdim), lambda s, c: (s, 0, 0))],
        out_shape=[jax.ShapeDtypeStruct((N_CHIPS, h, cdim), g.dtype)], operands=(g, r1), sem=("parallel",), name=name)
    return out


def final_add(g, r1, r2, jc_arr, *, dest_shape, lead, prev, name):
    _, rows, cdim = g.shape
    h = rows // 2

    def body(jc_ref, g_ref, r1_ref, r2_ref, *rest):
        o_ref = rest[-1]
        acc = g_ref[...].astype(F32) + r1_ref[...].astype(F32)
        for k in range(3):
            acc = acc + r2_ref[k].astype(F32)
        o_ref[...] = acc

    if lead is None:
        o_spec = pl.BlockSpec((h, cdim), lambda i, jc: (jc[1], 0))
    elif lead == "chip":
        o_spec = pl.BlockSpec((None, h, cdim), lambda i, jc: (jc[0], jc[1], 0))
    else:
        o_spec = pl.BlockSpec((None, h, cdim), lambda i, jc: (lead, jc[1], 0))
    in_specs = [pl.BlockSpec((None, h, cdim), lambda i, jc: (jc[0], jc[1], 0)),
                pl.BlockSpec((None, h, cdim), lambda i, jc: (jc[0], 0, 0)),
                pl.BlockSpec((3, h, cdim), lambda i, jc: (0, 0, 0))]
    operands = [g, r1, r2]
    aliases = None
    if prev is not None:
        in_specs.append(ANY)
        operands.append(prev)
        aliases = {3: 0}
    (out,), _ = _call(body, grid=(1,), prefetch=(jc_arr,), in_specs=in_specs, out_specs=[o_spec],
                      out_shape=[jax.ShapeDtypeStruct(dest_shape, F32)], operands=operands, aliases=aliases, name=name)
    return out


def _place():
    return lax.axis_index("x"), lax.axis_index("y"), lax.axis_index("c")


def _partner(x, y, k):
    return (1 - x if k >> 1 else x), (1 - y if k & 1 else y)


WHOLE = (0, 1, 1)


def _half(rows, sel, dtype, piece=WHOLE):
    lo, hi, n = piece
    align = 16 if dtype == BF16 else 8
    step = rows // 2 // n
    assert rows // 2 == step * n and step % align == 0
    return pl.ds(pl.multiple_of(sel * (rows // 2) + lo * step, align), (hi - lo) * step)


def _rider(peers, inputs, aliased, fresh, nsem, copies, arrivals):
    def start(ins, outs, send, recv):
        for cp in copies(ins, outs, send, recv):
            cp.start()

    def finish(ins, outs, send, recv):
        for cp in arrivals(ins, outs, send, recv):
            cp.wait_recv()
        for cp in copies(ins, outs, send, recv):
            cp.wait_send()

    return types.SimpleNamespace(peers=peers, inputs=list(inputs), aliased=list(aliased), fresh=list(fresh), nsem=nsem,
                                 start=start, finish=finish)


def _remote(src, dst, send, recv, idx, dev):
    return pltpu.make_async_remote_copy(src_ref=src, dst_ref=dst, send_sem=send.at[idx], recv_sem=recv.at[idx],
                                        device_id=dev, device_id_type=MESH)


def gather_ici_rider(fulls, pieces=None):
    nt = len(fulls)
    pieces = pieces or [WHOLE] * nt

    def region(outs, t, slot, sel):
        return outs[t].at[slot, _half(fulls[t].shape[1], sel, fulls[t].dtype, pieces[t])]

    def copies(ins, outs, send, recv):
        x, y, c = _place()
        res = []
        for t in range(nt):
            for k in (1, 2, 3):
                px, py = _partner(x, y, k)
                mine = region(outs, t, 2 * x + y, c)
                res.append(_remote(mine, mine, send, recv, 3 * t + k - 1, (px, py, c)))
        return res

    def arrivals(ins, outs, send, recv):
        x, y, c = _place()
        res = []
        for t in range(nt):
            for k in (1, 2, 3):
                px, py = _partner(x, y, k)
                theirs = region(outs, t, 2 * px + py, c)
                res.append(_remote(theirs, theirs, send, recv, 3 * t + k - 1, (x, y, c)))
        return res

    return _rider("chips", fulls, range(nt), [], 3 * nt, copies, arrivals)


def gather_d2d_rider(fulls, pieces=None):
    nt = len(fulls)
    pieces = pieces or [WHOLE] * nt

    def region(outs, t, slot, sel):
        return outs[t].at[slot, _half(fulls[t].shape[1], sel, fulls[t].dtype, pieces[t])]

    def both(outs, send, recv, mine):
        x, y, c = _place()
        res = []
        for t in range(nt):
            for k in (1, 2, 3):
                px, py = _partner(x, y, k)
                part = region(outs, t, 2 * px + py, c if mine else 1 - c)
                res.append(_remote(part, part, send, recv, 3 * t + k - 1, (x, y, 1 - c)))
        return res

    return _rider("sibling", fulls, range(nt), [], 3 * nt, lambda i, o, s, r: both(o, s, r, True),
                  lambda i, o, s, r: both(o, s, r, False))


def exchange_rider(grads):
    nt = len(grads)

    def both(ins, outs, send, recv):
        x, y, c = _place()
        return [_remote(ins[t].at[:, _half(grads[t].shape[1], 1 - c, grads[t].dtype)], outs[t], send, recv, t, (x, y, 1 - c))
                for t in range(nt)]

    fresh = [jax.ShapeDtypeStruct((N_CHIPS, g.shape[1] // 2, g.shape[2]), g.dtype) for g in grads]
    return _rider("sibling", grads, [], fresh, nt, both, both)


def scatter_rider(parts):
    nt = len(parts)

    def both(ins, outs, send, recv):
        x, y, c = _place()
        res = []
        for t in range(nt):
            for k in (1, 2, 3):
                px, py = _partner(x, y, k)
                res.append(_remote(ins[t].at[2 * px + py], outs[t].at[k - 1], send, recv, 3 * t + k - 1, (px, py, c)))
        return res

    fresh = [jax.ShapeDtypeStruct((3,) + p.shape[1:], p.dtype) for p in parts]
    return _rider("chips", parts, [], fresh, 3 * nt, both, both)


def broadcast_rider(bufs, items):
    def region(outs, item, sel):
        bi, lead = item
        ref = outs[bi]
        if lead == "chip":
            x, y, _ = _place()
            ref = ref.at[2 * x + y]
        elif lead is not None:
            ref = ref.at[lead]
        return ref.at[_half(ref.shape[0], sel, F32)]

    def both(outs, send, recv, mine):
        x, y, c = _place()
        res = []
        for i, item in enumerate(items):
            part = region(outs, item, c if mine else 1 - c)
            res.append(_remote(part, part, send, recv, i, (x, y, 1 - c)))
        return res

    return _rider("sibling", bufs, range(len(bufs)), [], len(items), lambda i, o, s, r: both(o, s, r, True),
                  lambda i, o, s, r: both(o, s, r, False))


def allcast_rider(buf):
    peers = [(k, flip) for k in range(N_CHIPS) for flip in (0, 1) if (k, flip) != (0, 0)]

    def both(outs, send, recv, mine):
        x, y, c = _place()
        res = []
        for i, (k, flip) in enumerate(peers):
            px, py = _partner(x, y, k)
            pc = 1 - c if flip else c
            slot, sel = (2 * x + y, c) if mine else (2 * px + py, pc)
            part = outs[0].at[slot, _half(buf.shape[1], sel, F32)]
            res.append(_remote(part, part, send, recv, i, (px, py, pc)))
        return res

    return _rider("everyone", [buf], [0], [], len(peers), lambda i, o, s, r: both(o, s, r, True),
                  lambda i, o, s, r: both(o, s, r, False))


def comm_call(riders, *, name):
    _, res = _call(None, riders=riders, name=name)
    return res


SEMS = pl.BlockSpec(memory_space=pltpu.SEMAPHORE)
SIDE_EFFECT = pltpu.SideEffectType.DATAFLOW_SIDE_EFFECTING


def _split_refs(riders, refs):
    views, p = [], 0
    for r in riders:
        bufs = refs[p:p + len(r.inputs) + len(r.fresh)]
        p += len(bufs)
        ins = bufs[:len(r.inputs)]
        views.append([ins, [ins[i] for i in r.aliased] + list(bufs[len(r.inputs):])])
    for view in views:
        view += [refs[p], refs[p + 1]]
        p += 2
    return views


def comm_start(riders, *, name):
    kind = _peer_kind(riders)
    bufs = [a for r in riders for a in r.inputs]
    fresh = [f for r in riders for f in r.fresh]
    n_buf, n_fresh = len(bufs), len(fresh)

    def body(*refs):
        ins, outs = refs[:n_buf], refs[n_buf:]
        through, land, sems = outs[:n_buf], outs[n_buf:n_buf + n_fresh], outs[n_buf + n_fresh:-1]
        _peer_barrier(kind)
        per_rider, pb, pf = [], 0, 0
        for r in riders:
            per_rider += list(through[pb:pb + len(r.inputs)]) + list(land[pf:pf + len(r.fresh)])
            pb, pf = pb + len(r.inputs), pf + len(r.fresh)
        for r, (r_ins, r_outs, send, recv) in zip(riders, _split_refs(riders, per_rider + list(sems))):
            r.start(r_ins, r_outs, send, recv)
        outs[-1][...] = jnp.zeros((8, LANES), F32)

    sem_shapes = [pltpu.SemaphoreType.DMA((r.nsem,)) for r in riders for _ in (0, 1)]
    res = pl.pallas_call(
        body, name=name, in_specs=[ANY] * n_buf,
        out_specs=[ANY] * (n_buf + n_fresh) + [SEMS] * len(sem_shapes) + [pl.BlockSpec(memory_space=pltpu.VMEM)],
        out_shape=[jax.ShapeDtypeStruct(a.shape, a.dtype) for a in bufs] + fresh + sem_shapes
        + [jax.ShapeDtypeStruct((8, LANES), F32)],
        input_output_aliases={i: i for i in range(n_buf)},
        compiler_params=pltpu.CompilerParams(has_side_effects=SIDE_EFFECT, collective_id=PEER_KINDS.index(kind)))(*bufs)
    return (riders, list(res[:n_buf + n_fresh]), list(res[n_buf + n_fresh:-1])), res[-1]


def comm_wait(state, after, *, name):
    riders, bufs, sems = state
    n_buf, n_sem = len(bufs), len(sems)
    n_in = sum(len(r.inputs) for r in riders)

    def body(*refs):
        held, sem_refs = refs[:n_buf], refs[n_buf:n_buf + n_sem]
        through, land = held[:n_in], held[n_in:]
        per_rider, pb, pf = [], 0, 0
        for r in riders:
            per_rider += list(through[pb:pb + len(r.inputs)]) + list(land[pf:pf + len(r.fresh)])
            pb, pf = pb + len(r.inputs), pf + len(r.fresh)
        for r, (r_ins, r_outs, send, recv) in zip(riders, _split_refs(riders, per_rider + list(sem_refs))):
            r.finish(r_ins, r_outs, send, recv)

    res = pl.pallas_call(
        body, name=name, in_specs=[ANY] * n_buf + [SEMS] * n_sem + [ANY], out_specs=[ANY] * n_buf,
        out_shape=[jax.ShapeDtypeStruct(a.shape, a.dtype) for a in bufs],
        input_output_aliases={i: i for i in range(n_buf)},
        compiler_params=pltpu.CompilerParams(has_side_effects=SIDE_EFFECT))(*bufs, *sems, after)
    through, land = list(res[:n_in]), list(res[n_in:])
    out, pb, pf = [], 0, 0
    for r in riders:
        r_ins, r_land = through[pb:pb + len(r.inputs)], land[pf:pf + len(r.fresh)]
        pb, pf = pb + len(r.inputs), pf + len(r.fresh)
        out.append([r_ins[i] for i in r.aliased] + r_land)
    return out


SLAB_ROWS = 192


def _pad_rows(a, rows=8):
    return jnp.pad(a, ((0, rows - a.shape[0]), (0, 0)))


def _pack_small(norm_grads, db_qkv, db_o, dsinks, db_sp, dln_g, dln_b, dw_sp, loss_part):
    parts = [
        jnp.concatenate(norm_grads, axis=0),
        _pad_rows(jnp.pad(db_qkv, ((0, 0), (0, 2 * D_MODEL - QKV_WIDTH))).reshape(2, D_MODEL)),
        _pad_rows(db_o),
        _pad_rows(jnp.pad(dsinks.reshape(1, N_Q_HEADS), ((0, 0), (0, D_MODEL - N_Q_HEADS)))),
        _pad_rows(db_sp.reshape(1, D_MODEL)),
        _pad_rows(jnp.concatenate([dln_g, dln_b, jnp.pad(loss_part[0:1], ((0, 0), (0, D_MODEL - LANES)))], axis=0)),
        dw_sp.reshape(SGU_CHUNK, D_MODEL),
    ]
    slab = jnp.concatenate(parts, axis=0)
    return jnp.pad(slab, ((0, SLAB_ROWS - slab.shape[0]), (0, 0))).reshape(N_CHIPS, SLAB_ROWS // N_CHIPS, D_MODEL)


def _unpack_small(slab, j):
    slab = slab.reshape(SLAB_ROWS, D_MODEL)
    norms = [slab[2 * i:2 * i + 2] for i in range(4)]
    db_qkv = slab[8:10].reshape(1, 2 * D_MODEL)[:, :QKV_WIDTH]
    db_o = slab[16:17]
    dsinks = slab[24:25, :N_Q_HEADS]
    db_sp = slab[32:33].reshape(SGU_GROUPS, SGU_CHUNK)
    width = D_MODEL // N_CHIPS
    dln_g = lax.dynamic_slice(slab[40:41], (0, j * width), (1, width))
    dln_b = lax.dynamic_slice(slab[41:42], (0, j * width), (1, width))
    dw_sp = slab[48:48 + SGU_CHUNK].reshape(SGU_GROUPS * SGU_CHUNK, SGU_CHUNK)
    return norms, db_qkv, db_o, dsinks, db_sp, dln_g, dln_b, dw_sp, slab[42, 0]


class _GradReduce:
    def __init__(self, c_arr, jc_arr, dest_shapes):
        self.c_arr, self.jc_arr, self.dest_shapes = c_arr, jc_arr, dest_shapes
        self.grad, self.sibling, self.pair, self.chips, self.dest = {}, {}, {}, {}, {}

    def exchange(self, tags):
        return exchange_rider([self.grad[t] for t in tags])

    def exchanged(self, tags, res):
        for t, r in zip(tags, res):
            self.sibling[t] = r
            self.pair[t] = pair_add(self.grad[t], r, self.c_arr, name=f"pair_add_{t}")

    def scatter(self, tags):
        return scatter_rider([self.pair[t] for t in tags])

    def scattered(self, tags, res, where):
        for t, r in zip(tags, res):
            name, lead = where[t]
            self.dest[name] = final_add(self.grad[t], self.sibling[t], r, self.jc_arr, dest_shape=self.dest_shapes[name],
                                        lead=lead, prev=self.dest.get(name), name=f"final_add_{t}")

    def broadcast(self, items):
        names = []
        for n, _ in items:
            if n not in names:
                names.append(n)
        return names, broadcast_rider([self.dest[n] for n in names], [(names.index(n), lead) for n, lead in items])

    def broadcasted(self, names, res):
        for n, r in zip(names, res):
            self.dest[n] = r


def kernel(x, norm_mix_pre, norm_mix_post, norm_ffn_pre, norm_ffn_post, attn_w_qkv, attn_b_qkv, attn_sinks, attn_w_o, attn_b_o, sgu_w_in, sgu_ln_g, sgu_ln_b, sgu_w_spatial, sgu_b_spatial, sgu_w_out, ffn_w_gate_up, ffn_w_down, loss_target, m_norm_mix_pre, m_norm_mix_post, m_norm_ffn_pre, m_norm_ffn_post, m_attn_w_qkv, m_attn_b_qkv, m_attn_sinks, m_attn_w_o, m_attn_b_o, m_sgu_w_in, m_sgu_ln_g, m_sgu_ln_b, m_sgu_w_spatial, m_sgu_b_spatial, m_sgu_w_out, m_ffn_w_gate_up, m_ffn_w_down, v_norm_mix_pre, v_norm_mix_post, v_norm_ffn_pre, v_norm_ffn_post, v_attn_w_qkv, v_attn_b_qkv, v_attn_sinks, v_attn_w_o, v_attn_b_o, v_sgu_w_in, v_sgu_ln_g, v_sgu_ln_b, v_sgu_w_spatial, v_sgu_b_spatial, v_sgu_w_out, v_ffn_w_gate_up, v_ffn_w_down):
    s = x.shape[1]
    x0 = x.reshape(s, D_MODEL)
    target = loss_target.reshape(s, D_MODEL)
    mx, my, mc = lax.axis_index("x"), lax.axis_index("y"), lax.axis_index("c")
    chip = 2 * mx + my
    chip_arr = jnp.reshape(chip, (1,)).astype(I32)
    c_arr = jnp.reshape(mc, (1,)).astype(I32)
    jc_arr = jnp.stack([chip, mc]).astype(I32)
    zero_bias = jnp.zeros((1, D_MODEL), F32)

    def gain(p, i):
        return p[i:i + 1]

    big = [attn_w_qkv, attn_w_o, sgu_w_in, sgu_w_out, ffn_w_gate_up, ffn_w_gate_up, ffn_w_down, ffn_w_down]
    layers = [0, 0, 0, 0, 0, 1, 0, 1]
    tags = ["qkv", "wo", "win", "wout", "wgu0", "wgu1", "wd0", "wd1"]
    full = {t: place_shard(w, l, chip_arr, BF16, name=f"place_{t}") for w, l, t in zip(big, layers, tags)
            if t not in ("wgu0", "wgu1")}
    ln_pack = _pad_rows(jnp.concatenate([sgu_ln_g, sgu_ln_b], axis=0), 16)[None]
    full["ln"] = place_shard(ln_pack, 0, chip_arr, F32, name="place_ln")

    def split(items):
        return [i if isinstance(i, str) else i[0] for i in items], [WHOLE if isinstance(i, str) else tuple(i[1:]) for i in items]

    def ici(*items):
        names, pieces = split(items)
        return gather_ici_rider([full[n] for n in names], pieces)

    def d2d(*items):
        names, pieces = split(items)
        return gather_d2d_rider([full[n] for n in names], pieces)

    def landed(items, res):
        for n, r in zip(split(items)[0], res):
            full[n] = r

    cos, sin = _rope_tables(s)
    sink_rows = jnp.broadcast_to(
        jnp.repeat(attn_sinks.reshape(N_KV_HEADS, GQA_GROUP), WINDOW, axis=1)[:, None, :], (N_KV_HEADS, 8, ROWS))
    w_sp = sgu_w_spatial.reshape(SGU_GROUPS, SGU_CHUNK, SGU_CHUNK)
    b_sp = jnp.broadcast_to(sgu_b_spatial.reshape(SGU_GROUPS, SGU_CHUNK)[:, :, None], (SGU_GROUPS, SGU_CHUNK, LANES))

    (h0, full["wgu0"]), (res,) = prenorm_and_place(x0, gain(norm_mix_pre, 0), ffn_w_gate_up, 0, chip_arr, name="prenorm_0",
                                                   riders=[ici("qkv", "ln")])
    landed(("qkv", "ln"), res)
    full["wgu1"], (res,) = place_shard(ffn_w_gate_up, 1, chip_arr, BF16, name="place_wgu1", riders=[d2d("qkv", "ln")])
    landed(("qkv", "ln"), res)
    ln_g = full["ln"][:, 0, :].reshape(1, D_MODEL)
    ln_b = full["ln"][:, 1, :].reshape(1, D_MODEL)

    def hosted(call, stages):
        outputs, results = call([{"ici": ici, "d2d": d2d}[kind](*items) for kind, items in stages])
        for (_, items), res in zip(stages, results):
            landed(items, res)
        return outputs

    qkv = hosted(lambda r: qkv_proj(h0, full["qkv"], attn_b_qkv, cos, sin, name="qkv_proj", riders=r),
                 [("ici", ("wo", ("wgu0", 0, 3, 8)))])
    o = hosted(lambda r: attn_fwd(qkv, sink_rows, name="attn_fwd", riders=r),
               [("d2d", ("wo",)), ("ici", (("wgu0", 3, 8, 8), ("wd0", 0, 2, 11)))])
    w_o = full["wo"].reshape(Q_WIDTH, D_MODEL)
    x1, h1, m0 = hosted(lambda r: proj_residual_norm(o, w_o, x0, attn_b_o, gain(norm_mix_post, 0), gain(norm_ffn_pre, 0),
                                                     name="attn_out_norm", riders=r),
                        [("d2d", ("wgu0",)), ("ici", (("wd0", 2, 11, 11),))])
    gu0, a0 = hosted(lambda r: ffn_up(h1, full["wgu0"], name="ffn_up_0", riders=r),
                     [("d2d", ("wd0",)), ("ici", ("win", "wout", ("wgu1", 0, 4, 8)))])
    w_d0 = full["wd0"].reshape(D_FF, D_MODEL)
    x2, h2, f0 = hosted(lambda r: proj_residual_norm(a0, w_d0, x1, zero_bias, gain(norm_ffn_post, 0), gain(norm_mix_pre, 1),
                                                     name="ffn_down_norm_0", riders=r),
                        [("d2d", ("win", "wout")), ("ici", (("wgu1", 4, 8, 8),))])
    w_in = full["win"]
    z, y = hosted(lambda r: sgu_in_fwd(h2, w_in, ln_g, ln_b, w_sp, b_sp, name="sgu_in_fwd", riders=r),
                  [("d2d", ("wgu1",)), ("ici", ("wd1",))])
    w_out = full["wout"].reshape(D_MODEL, D_MODEL)
    x3, h3, m1 = hosted(lambda r: proj_residual_norm(y, w_out, x2, zero_bias, gain(norm_mix_post, 1), gain(norm_ffn_pre, 1),
                                                     name="sgu_out_norm", riders=r),
                        [("d2d", ("wd1",))])
    w_qkv, w_gu0, w_gu1 = full["qkv"], full["wgu0"], full["wgu1"]
    w_d1 = full["wd1"].reshape(D_FF, D_MODEL)
    gu1, a1, dx4, df1, dg_fpost1, loss_part = ffn_fwd_loss_rows(
        h3, w_gu1, w_d1, x3, gain(norm_ffn_post, 1), target, name="ffn_fwd_loss_rows")

    red = _GradReduce(c_arr, jc_arr, {
        "qkv": attn_w_qkv.shape[1:], "wo": attn_w_o.shape[1:], "win": sgu_w_in.shape[1:], "wout": sgu_w_out.shape[1:],
        "wgu": ffn_w_gate_up.shape, "wd": ffn_w_down.shape, "slab": (N_CHIPS, SLAB_ROWS // N_CHIPS, D_MODEL)})
    where = {"qkv": ("qkv", None), "wo": ("wo", None), "win": ("win", None), "wout": ("wout", None), "wgu0": ("wgu", 0),
             "wgu1": ("wgu", 1), "wd0": ("wd", 0), "wd1": ("wd", 1), "small": ("slab", "chip")}

    dgu1, dx3, dm1, dg_fpre1, dg_mpost1, _ = ffn_bwd_rows(
        df1, w_d1, gu1, w_gu1, dx4, x3, gain(norm_ffn_pre, 1), m1, gain(norm_mix_post, 1), name="ffn_bwd_rows_1")
    red.grad["wd1"] = mm_tn(a1, df1, shard_major=False, tm=256, tn=D_MODEL, name="dw_down_1").reshape(
        N_CHIPS, D_FF // N_CHIPS, D_MODEL)
    red.grad["wgu1"], (res,) = mm_tn(h3, dgu1, shard_major=True, tm=512, tn=FF_HALF, name="dw_gate_up_1",
                                     riders=[red.exchange(["wd1"])])
    red.exchanged(["wd1"], res)
    dy, (res,) = mm_nt(dm1, w_out, out_dtype=F32, name="dy_sgu", riders=[red.exchange(["wgu1"])])
    red.exchanged(["wgu1"], res)
    red.grad["wout"] = mm_tn(y, dm1, shard_major=False, tm=512, tn=D_MODEL, name="dw_sgu_out").reshape(
        N_CHIPS, D_MODEL // N_CHIPS, D_MODEL)
    (dz, dw_sp, db_sp, dln_g, dln_b), (res_a, res_b) = sgu_bwd(
        z, dy, ln_g, ln_b, w_sp, b_sp, name="sgu_bwd", riders=[red.scatter(["wgu1"]), red.exchange(["wout"])])
    red.scattered(["wgu1"], res_a, where)
    red.exchanged(["wout"], res_b)
    names, rider = red.broadcast([("wgu", 1)])
    red.grad["win"], (res_a, res_b) = mm_tn(h2, dz, shard_major=True, tm=D_MODEL, tn=2 * D_MODEL // N_CHIPS, name="dw_sgu_in",
                                            riders=[rider, red.scatter(["wout"])])
    red.broadcasted(names, res_a)
    red.scattered(["wout"], res_b, where)
    names, rider = red.broadcast([("wout", None)])
    (dx2, df0, dg_mpre1, dg_fpost0, _), (res_a, res_b) = dh_norm_bwd_pair(
        dz, w_in, dx3, x2, gain(norm_mix_pre, 1), f0, gain(norm_ffn_post, 0), name="dh_sgu_norm",
        riders=[red.exchange(["win"]), rider])
    red.exchanged(["win"], res_a)
    red.broadcasted(names, res_b)
    (dgu0, dx1, dm0, dg_fpre0, dg_mpost0, db_o), (res,) = ffn_bwd_rows(
        df0, w_d0, gu0, w_gu0, dx2, x1, gain(norm_ffn_pre, 0), m0, gain(norm_mix_post, 0), name="ffn_bwd_rows_0",
        riders=[red.scatter(["wd1", "win"])])
    red.scattered(["wd1", "win"], res, where)
    names, rider = red.broadcast([("wd", 1), ("win", None)])
    dw_d0, (res,) = mm_tn(a0, df0, shard_major=False, tm=256, tn=D_MODEL, name="dw_down_0", riders=[rider])
    red.broadcasted(names, res)
    red.grad["wd0"] = dw_d0.reshape(N_CHIPS, D_FF // N_CHIPS, D_MODEL)
    do, (res,) = mm_nt(dm0, w_o, out_dtype=BF16, name="do_attn", riders=[red.exchange(["wd0"])])
    red.exchanged(["wd0"], res)
    red.grad["wgu0"], (res,) = mm_tn(h1, dgu0, shard_major=True, tm=512, tn=FF_HALF, name="dw_gate_up_0",
                                     riders=[red.scatter(["wd0"])])
    red.scattered(["wd0"], res, where)
    names, rider = red.broadcast([("wd", 0)])
    dw_o, (res_a, res_b) = mm_tn(o, dm0, shard_major=False, tm=512, tn=D_MODEL, name="dw_attn_out",
                                 riders=[red.exchange(["wgu0"]), rider])
    red.exchanged(["wgu0"], res_a)
    red.broadcasted(names, res_b)
    red.grad["wo"] = dw_o.reshape(N_CHIPS, Q_WIDTH // N_CHIPS, D_MODEL)
    (dq, dkc, dkp, dvc, dvp, dsink), (res_a, res_b) = attn_bwd(
        qkv, sink_rows, do, name="attn_bwd", riders=[red.scatter(["wgu0"]), red.exchange(["wo"])])
    red.scattered(["wgu0"], res_a, where)
    red.exchanged(["wo"], res_b)
    names, rider = red.broadcast([("wgu", 0)])
    (dqkv, db_qkv), (res_a, res_b) = rope_bwd(dq, dkc, dkp, dvc, dvp, cos, sin, name="rope_bwd",
                                              riders=[rider, red.scatter(["wo"])])
    red.broadcasted(names, res_a)
    red.scattered(["wo"], res_b, where)
    names, rider = red.broadcast([("wo", None)])
    red.grad["qkv"], (res,) = mm_tn(h0, dqkv, shard_major=True, tm=D_MODEL, tn=QKV_WIDTH // N_CHIPS, name="dw_qkv",
                                    riders=[rider])
    red.broadcasted(names, res)
    grad_x, dg_mpre0 = dh_norm_bwd_last(dqkv, w_qkv, dx1, x0, gain(norm_mix_pre, 0), name="dh_attn_norm_in")

    norm_grads = [jnp.concatenate(p, axis=0) for p in
                  ((dg_mpre0, dg_mpre1), (dg_mpost0, dg_mpost1), (dg_fpre0, dg_fpre1), (dg_fpost0, dg_fpost1))]
    red.grad["small"] = _pack_small(norm_grads, db_qkv, db_o, dsink[:, :, 0, 0], db_sp[:, :, 0], dln_g, dln_b, dw_sp,
                                    loss_part)
    def big_update(w, g, m, v, tag, after=None):
        return adamw(w, g.reshape(w.shape), m, v, name=f"adamw_{tag}", after=after)

    (res,) = comm_call([red.exchange(["qkv", "small"])], name="tail_1")
    red.exchanged(["qkv", "small"], res)
    state, token = comm_start([red.scatter(["qkv", "small"])], name="tail_2_start")
    upd_wgu = big_update(ffn_w_gate_up, red.dest["wgu"], m_ffn_w_gate_up, v_ffn_w_gate_up, "wgu", after=token)
    (res,) = comm_wait(state, upd_wgu[1], name="tail_2_wait")
    red.scattered(["qkv", "small"], res, where)
    names, rider = red.broadcast([("qkv", None)])
    state, token = comm_start([rider, allcast_rider(red.dest["slab"])], name="tail_3_start")
    upd_wd = big_update(ffn_w_down, red.dest["wd"], m_ffn_w_down, v_ffn_w_down, "wd", after=token)
    (res_a,), (slab_full,) = comm_wait(state, upd_wd[1], name="tail_3_wait")
    red.broadcasted(names, [res_a])
    g_qkv, g_wo, g_win, g_wout = (red.dest[n] for n in ("qkv", "wo", "win", "wout"))
    g_norms, g_bqkv, g_bo, g_sinks, g_bsp, g_lng, g_lnb, g_wsp, loss = _unpack_small(slab_full, chip)

    upd = {
        "attn_w_qkv": big_update(attn_w_qkv, g_qkv, m_attn_w_qkv, v_attn_w_qkv, "qkv"),
        "attn_w_o": big_update(attn_w_o, g_wo, m_attn_w_o, v_attn_w_o, "wo"),
        "sgu_w_in": big_update(sgu_w_in, g_win, m_sgu_w_in, v_sgu_w_in, "win"),
        "sgu_w_out": big_update(sgu_w_out, g_wout, m_sgu_w_out, v_sgu_w_out, "wout"),
        "ffn_w_gate_up": upd_wgu,
        "ffn_w_down": upd_wd,
    }
    small_names = ["norm_mix_pre", "norm_mix_post", "norm_ffn_pre", "norm_ffn_post", "attn_b_qkv", "attn_sinks", "attn_b_o",
                   "sgu_ln_g", "sgu_ln_b", "sgu_w_spatial", "sgu_b_spatial"]
    small_w = [norm_mix_pre, norm_mix_post, norm_ffn_pre, norm_ffn_post, attn_b_qkv, attn_sinks, attn_b_o, sgu_ln_g, sgu_ln_b,
               sgu_w_spatial, sgu_b_spatial]
    small_m = [m_norm_mix_pre, m_norm_mix_post, m_norm_ffn_pre, m_norm_ffn_post, m_attn_b_qkv, m_attn_sinks, m_attn_b_o,
               m_sgu_ln_g, m_sgu_ln_b, m_sgu_w_spatial, m_sgu_b_spatial]
    small_v = [v_norm_mix_pre, v_norm_mix_post, v_norm_ffn_pre, v_norm_ffn_post, v_attn_b_qkv, v_attn_sinks, v_attn_b_o,
               v_sgu_ln_g, v_sgu_ln_b, v_sgu_w_spatial, v_sgu_b_spatial]
    small_g = g_norms + [g_bqkv, g_sinks, g_bo, g_lng, g_lnb, g_wsp, g_bsp]

    def flat2(a):
        return a.reshape(-1, a.shape[-1])

    res = adamw_small([flat2(a) for a in small_w], [flat2(a) for a in small_g], [flat2(a) for a in small_m],
                      [flat2(a) for a in small_v], name="adamw_small")
    for i, nm in enumerate(small_names):
        upd[nm] = tuple(r[i].reshape(small_w[i].shape) for r in res)

    order = ["norm_mix_pre", "norm_mix_post", "norm_ffn_pre", "norm_ffn_post", "attn_w_qkv", "attn_b_qkv", "attn_sinks",
             "attn_w_o", "attn_b_o", "sgu_w_in", "sgu_ln_g", "sgu_ln_b", "sgu_w_spatial", "sgu_b_spatial", "sgu_w_out",
             "ffn_w_gate_up", "ffn_w_down"]
    outs = [loss, grad_x.reshape(1, s, D_MODEL)]
    for part in range(4):
        outs += [upd[nm][part] for nm in order]
    return tuple(outs)
```

```python
import types

import numpy as np
import jax
import jax.numpy as jnp
from jax import lax
from jax.experimental import pallas as pl
from jax.experimental.pallas import tpu as pltpu

F32 = jnp.float32
BF16 = jnp.bfloat16
I32 = jnp.int32

D_MODEL = 1024
HEAD_DIM = 64
N_Q_HEADS = 16
N_KV_HEADS = 4
GQA_GROUP = 4
WINDOW = 128
Q_WIDTH = 1024
KV_WIDTH = 256
QKV_WIDTH = 1536
ROPE_THETA = 10000.0
SGU_GROUPS = 8
SGU_CHUNK = 128
D_FF = 2816
FF_HALF = D_FF // 2
EPS = 1e-6
N_CHIPS = 4
LANES = 128

ADAM_LR = 0.001
ADAM_B1 = 0.9
ADAM_B2 = 0.999
ADAM_EPS = 1e-08
ADAM_WD = 0.01
ADAM_STEP = 10

VMEM_LIMIT = 52 * 1024 * 1024
BIG_VMEM_LIMIT = 62 * 1024 * 1024
SUB_ROWS = 256
MESH = pl.DeviceIdType.MESH
NEG = -1e30
NT_DIMS = (((1,), (1,)), ((), ()))
TN_DIMS = (((0,), (0,)), ((), ()))
NN_DIMS = (((1,), (0,)), ((), ()))
ANY = pl.BlockSpec(memory_space=pl.ANY)


def _row_tile(s, want):
    return want if s % want == 0 else s


PEER_KINDS = ("sibling", "chips", "sibling+chips", "everyone")


def _peer_kind(riders):
    kinds = {r.peers for r in riders}
    if not kinds:
        return None
    if "everyone" in kinds:
        return "everyone"
    return "sibling+chips" if len(kinds) == 2 else kinds.pop()


def _peer_barrier(kind):
    x, y, c = _place()
    chips = [(*_partner(x, y, k), c) for k in (1, 2, 3)]
    peers = {"sibling": [(x, y, 1 - c)], "chips": chips, "sibling+chips": [(x, y, 1 - c)] + chips,
             "everyone": [(x, y, 1 - c)] + chips + [(px, py, 1 - c) for px, py, _ in chips]}[kind]
    barrier = pltpu.get_barrier_semaphore()
    for dev in peers:
        pl.semaphore_signal(barrier, inc=1, device_id=dev, device_id_type=MESH)
    pl.semaphore_wait(barrier, len(peers))


def _call(body, *, name, grid=(), in_specs=(), out_specs=(), out_shape=(), scratch_shapes=(), operands=(), prefetch=(),
          aliases=None, riders=(), sem=None, vmem_limit=VMEM_LIMIT):
    n_pre, n_in, n_out, n_scr = len(prefetch), len(operands), len(out_shape), len(scratch_shapes)
    in_specs, out_specs, out_shape = list(in_specs), list(out_specs), list(out_shape)
    operands, scratch_shapes = list(operands), list(scratch_shapes)
    io_alias = {n_pre + i: o for i, o in (aliases or {}).items()}
    for r in riders:
        base_in, base_out = n_pre + len(operands), len(out_shape)
        operands += list(r.inputs)
        in_specs += [ANY] * len(r.inputs)
        for pos, i in enumerate(r.aliased):
            io_alias[base_in + i] = base_out + pos
            out_shape.append(jax.ShapeDtypeStruct(r.inputs[i].shape, r.inputs[i].dtype))
        out_shape += list(r.fresh)
        out_specs += [ANY] * (len(r.aliased) + len(r.fresh))
        scratch_shapes += [pltpu.SemaphoreType.DMA((r.nsem,)), pltpu.SemaphoreType.DMA((r.nsem,))]

    def wrapped(*refs):
        pre, p = refs[:n_pre], n_pre
        core_in, p = refs[p:p + n_in], p + n_in
        r_in = []
        for r in riders:
            r_in.append(refs[p:p + len(r.inputs)])
            p += len(r.inputs)
        core_out, p = refs[p:p + n_out], p + n_out
        r_out = []
        for r in riders:
            k = len(r.aliased) + len(r.fresh)
            r_out.append(refs[p:p + k])
            p += k
        core_scr, p = refs[p:p + n_scr], p + n_scr
        r_sem = [refs[p + 2 * i:p + 2 * i + 2] for i in range(len(riders))]

        def edge(at_last, fns):
            def run():
                if not at_last:
                    _peer_barrier(peer_kind)
                for i, r in enumerate(riders):
                    getattr(r, fns)(r_in[i], r_out[i], r_sem[i][0], r_sem[i][1])
            if not riders:
                return
            if not grid:
                run()
                return
            cond = None
            for d, n in enumerate(grid):
                c = pl.program_id(d) == (n - 1 if at_last else 0)
                cond = c if cond is None else jnp.logical_and(cond, c)
            pl.when(cond)(run)

        edge(False, "start")
        if body is not None:
            body(*pre, *core_in, *core_out, *core_scr)
        edge(True, "finish")

    if sem is None or riders:
        sem = ("arbitrary",) * len(grid)
    kwargs = dict(out_shape=out_shape, input_output_aliases=io_alias, name=name)
    peer_kind = _peer_kind(riders)
    collective = {} if peer_kind is None else {"collective_id": PEER_KINDS.index(peer_kind)}
    if grid:
        kwargs["compiler_params"] = pltpu.CompilerParams(dimension_semantics=sem, vmem_limit_bytes=vmem_limit, **collective)
    elif collective:
        kwargs["compiler_params"] = pltpu.CompilerParams(**collective)
    if n_pre:
        kwargs["grid_spec"] = pltpu.PrefetchScalarGridSpec(
            num_scalar_prefetch=n_pre, grid=grid, in_specs=in_specs, out_specs=out_specs, scratch_shapes=scratch_shapes)
    else:
        kwargs.update(grid=grid, in_specs=in_specs, out_specs=out_specs, scratch_shapes=scratch_shapes)
    res = pl.pallas_call(wrapped, **kwargs)(*prefetch, *operands)
    core, rest, rider_res = list(res[:n_out]), list(res[n_out:]), []
    for r in riders:
        k = len(r.aliased) + len(r.fresh)
        rider_res.append(rest[:k])
        rest = rest[k:]
    return core, rider_res


def _mm_call(*, grid, in_specs, out_spec, out_shape, dims, nk, kaxis, acc_shape, name, operands, riders=()):
    out_dtype = out_shape.dtype

    def body(a_ref, b_ref, o_ref, *scratch):
        p = lax.dot_general(a_ref[...].astype(BF16), b_ref[...].astype(BF16), dims, preferred_element_type=F32)
        if nk == 1:
            o_ref[...] = p.astype(out_dtype)
        else:
            acc = scratch[0]
            kk = pl.program_id(kaxis)

            @pl.when(kk == 0)
            def _():
                acc[...] = p

            @pl.when(kk > 0)
            def _():
                acc[...] += p

            @pl.when(kk == nk - 1)
            def _():
                o_ref[...] = acc[...].astype(out_dtype)

    sem = ["parallel"] * len(grid)
    if nk > 1:
        sem[kaxis] = "arbitrary"
    (out,), rider_res = _call(
        body, grid=grid, in_specs=in_specs, out_specs=[out_spec], out_shape=[out_shape],
        scratch_shapes=[pltpu.VMEM(acc_shape, F32)] if nk > 1 else [], operands=operands, name=name, riders=riders,
        sem=tuple(sem))
    return (out, rider_res) if riders else out


def mm_nt(a, w, *, out_dtype, name, tm=1024, riders=()):
    m, n = a.shape
    kout = w.shape[0]
    tm = _row_tile(m, tm)
    return _mm_call(grid=(m // tm,),
                    in_specs=[pl.BlockSpec((tm, n), lambda i: (i, 0)), pl.BlockSpec((kout, n), lambda i: (0, 0))],
                    out_spec=pl.BlockSpec((tm, kout), lambda i: (i, 0)),
                    out_shape=jax.ShapeDtypeStruct((m, kout), out_dtype), dims=NT_DIMS, nk=1, kaxis=0,
                    acc_shape=None, name=name, operands=(a, w), riders=riders)


def mm_tn(a, b, *, shard_major, name, tm, tn, tk=None, out_dtype=BF16, riders=()):
    s, m = a.shape
    tk = s if tk is None else _row_tile(s, tk)
    if b.ndim == 3:
        n = 2 * b.shape[2]
        b_spec = pl.BlockSpec((None, tk, tn), lambda j, i, kk: (j // 2, kk, j % 2))
    else:
        n = b.shape[1]
        b_spec = pl.BlockSpec((tk, tn), lambda j, i, kk: (kk, j))
    if shard_major:
        assert tn == n // N_CHIPS
        o_spec = pl.BlockSpec((None, tm, tn), lambda j, i, kk: (j, i, 0))
        o_shape = jax.ShapeDtypeStruct((N_CHIPS, m, tn), out_dtype)
    else:
        o_spec = pl.BlockSpec((tm, tn), lambda j, i, kk: (i, j))
        o_shape = jax.ShapeDtypeStruct((m, n), out_dtype)
    return _mm_call(grid=(n // tn, m // tm, s // tk),
                    in_specs=[pl.BlockSpec((tk, tm), lambda j, i, kk: (kk, i)), b_spec], out_spec=o_spec,
                    out_shape=o_shape, dims=TN_DIMS, nk=s // tk, kaxis=2, acc_shape=(tm, tn), name=name, operands=(a, b),
                    riders=riders)


def _rstd(x):
    return lax.rsqrt(jnp.mean(x * x, axis=-1, keepdims=True) + EPS)


def _rms_bwd(dy, x, g):
    r = _rstd(x)
    xhat = x * r
    gy = dy * g
    dx = r * (gy - xhat * jnp.mean(gy * xhat, axis=-1, keepdims=True))
    return dx, jnp.sum(dy * xhat, axis=0, keepdims=True)


def _accum(ref, val, first):
    @pl.when(first)
    def _():
        ref[...] = val

    @pl.when(jnp.logical_not(first))
    def _():
        ref[...] += val


def _row_spec(tm, width):
    return pl.BlockSpec((tm, width), lambda i: (i, 0))


def _vec_spec(width):
    return pl.BlockSpec((1, width), lambda i: (0, 0))


def _ret(core, rider_res, riders):
    core = core[0] if len(core) == 1 else core
    return (core, rider_res) if riders else core


def prenorm_and_place(x, g, w, layer, chip_arr, *, name, tm=256, riders=()):
    s = x.shape[0]
    tm = _row_tile(s, tm)
    steps = s // tm
    _, r, c = w.shape
    tr = r // steps
    assert tr * steps == r and tr % 16 == 0

    def body(chip_ref, x_ref, g_ref, w_ref, h_ref, o_ref):
        xv = x_ref[...]
        h_ref[...] = (xv * _rstd(xv) * g_ref[...]).astype(BF16)
        o_ref[...] = w_ref[...].astype(BF16)

    core, rr = _call(
        body, grid=(steps,), prefetch=(chip_arr,),
        in_specs=[pl.BlockSpec((tm, D_MODEL), lambda i, chip: (i, 0)), pl.BlockSpec((1, D_MODEL), lambda i, chip: (0, 0)),
                  pl.BlockSpec((None, tr, c), lambda i, chip: (layer, i, 0))],
        out_specs=[pl.BlockSpec((tm, D_MODEL), lambda i, chip: (i, 0)), pl.BlockSpec((None, tr, c), lambda i, chip: (chip[0], i, 0))],
        out_shape=[jax.ShapeDtypeStruct((s, D_MODEL), BF16), jax.ShapeDtypeStruct((N_CHIPS, r, c), BF16)],
        operands=(x, g, w), sem=("parallel",), name=name, riders=riders)
    return _ret(core, rr, riders)


def proj_residual_norm(a, w, x, bias, g_post, g_next, *, name, tm=512, sub=256, riders=()):
    s, k = a.shape
    tm = _row_tile(s, tm)
    sub = min(sub, tm)

    def body(a_ref, w_ref, x_ref, b_ref, gp_ref, gn_ref, xo_ref, h_ref, m_ref):
        for t in range(tm // sub):
            rows = slice(t * sub, (t + 1) * sub)
            mv = jnp.dot(a_ref[rows, :], w_ref[...], preferred_element_type=F32) + b_ref[...]
            m_ref[rows, :] = mv.astype(BF16)
            xn = x_ref[rows, :] + mv * _rstd(mv) * gp_ref[...]
            xo_ref[rows, :] = xn
            h_ref[rows, :] = (xn * _rstd(xn) * gn_ref[...]).astype(BF16)

    row, vec = _row_spec(tm, D_MODEL), _vec_spec(D_MODEL)
    core, rr = _call(
        body, grid=(s // tm,),
        in_specs=[_row_spec(tm, k), pl.BlockSpec((k, D_MODEL), lambda i: (0, 0)), row, vec, vec, vec], out_specs=[row, row, row],
        out_shape=[jax.ShapeDtypeStruct((s, D_MODEL), F32), jax.ShapeDtypeStruct((s, D_MODEL), BF16),
                   jax.ShapeDtypeStruct((s, D_MODEL), BF16)],
        operands=(a, w, x, bias, g_post, g_next), sem=("parallel",), name=name, riders=riders)
    return _ret(core, rr, riders)


def ffn_fwd_loss_rows(h, w_gu, w_d, x, g_post, target, *, name, tm=512, riders=()):
    s = x.shape[0]
    tm = _row_tile(s, tm)

    def body(h_ref, w0, w1, w2, w3, wd_ref, x_ref, g_ref, t_ref, d_ref, a_ref, dx_ref, df_ref, dg_ref, loss_ref):
        first = pl.program_id(0) == 0
        halves = [slice(half * FF_HALF, (half + 1) * FF_HALF) for half in (0, 1)]
        gain = g_ref[...]
        sub = min(SUB_ROWS, tm)
        sums = None
        for t in range(tm // sub):
            rows = slice(t * sub, (t + 1) * sub)
            hv = h_ref[rows, :]
            fv = None
            for cols, (wg_ref, wu_ref) in zip(halves, ((w0, w2), (w1, w3))):
                g = jnp.dot(hv, wg_ref[...], preferred_element_type=F32)
                u = jnp.dot(hv, wu_ref[...], preferred_element_type=F32)
                sig = _sigmoid(g)
                silu = g * sig
                d_ref[0, rows, cols] = (u * (sig + silu * (1.0 - sig))).astype(BF16)
                d_ref[1, rows, cols] = silu.astype(BF16)
                act = (silu * u).astype(BF16)
                a_ref[rows, cols] = act
                p = jnp.dot(act, wd_ref[cols, :], preferred_element_type=F32)
                fv = p if fv is None else fv + p
            err = x_ref[rows, :] + fv * _rstd(fv) * gain - t_ref[rows, :]
            dx = err * (1.0 / D_MODEL)
            dx_ref[rows, :] = dx
            df, dg = _rms_bwd(dx, fv, gain)
            df_ref[rows, :] = df.astype(BF16)
            part = (dg, jnp.sum(jnp.sum(err * err, axis=-1, keepdims=True), axis=0, keepdims=True) * (0.5 / D_MODEL))
            sums = part if sums is None else tuple(a + b for a, b in zip(sums, part))
        _accum(dg_ref, sums[0], first)
        _accum(loss_ref, jnp.broadcast_to(sums[1], (8, LANES)), first)

    def resident(shape, index):
        return pl.BlockSpec(shape, index, pipeline_mode=pl.Buffered(1))

    row, vec = _row_spec(tm, D_MODEL), _vec_spec(D_MODEL)
    shards = [resident((None, D_MODEL, FF_HALF), (lambda j: (lambda i: (j, 0, 0)))(j)) for j in range(N_CHIPS)]
    core, rr = _call(
        body, grid=(s // tm,),
        in_specs=[row] + shards + [resident((D_FF, D_MODEL), lambda i: (0, 0)), row, vec, row],
        out_specs=[pl.BlockSpec((2, tm, D_FF), lambda i: (0, i, 0)), _row_spec(tm, D_FF), row, row, vec,
                   pl.BlockSpec((8, LANES), lambda i: (0, 0))],
        out_shape=[jax.ShapeDtypeStruct((2, s, D_FF), BF16), jax.ShapeDtypeStruct((s, D_FF), BF16),
                   jax.ShapeDtypeStruct((s, D_MODEL), F32), jax.ShapeDtypeStruct((s, D_MODEL), BF16),
                   jax.ShapeDtypeStruct((1, D_MODEL), F32), jax.ShapeDtypeStruct((8, LANES), F32)],
        operands=(h, w_gu, w_gu, w_gu, w_gu, w_d, x, g_post, target), name=name, riders=riders, vmem_limit=BIG_VMEM_LIMIT)
    return _ret(core, rr, riders)


def dh_norm_bwd_pair(a, w, dres, x, g_pre, m, g_post, *, name, tm=512, sub=256, riders=()):
    _, kout, ns = w.shape
    planes = a.ndim == 3
    s = x.shape[0]
    tm = _row_tile(s, tm)
    sub = min(sub, tm)
    a_spec = pl.BlockSpec((2, tm, 2 * ns), lambda i: (0, i, 0)) if planes else pl.BlockSpec((tm, N_CHIPS * ns), lambda i: (i, 0))

    def body(a_ref, w0, w1, w2, w3, dres_ref, x_ref, gpre_ref, m_ref, gpost_ref, dx_ref, dm_ref, dgpre_ref, dgpost_ref, db_ref):
        first = pl.program_id(0) == 0
        sums = None
        for t in range(tm // sub):
            rows = slice(t * sub, (t + 1) * sub)
            dh = None
            for j, w_ref in enumerate((w0, w1, w2, w3)):
                a_j = a_ref[j // 2, rows, (j % 2) * ns:(j % 2 + 1) * ns] if planes else a_ref[rows, j * ns:(j + 1) * ns]
                p = lax.dot_general(a_j, w_ref[...], NT_DIMS, preferred_element_type=F32)
                dh = p if dh is None else dh + p
            d1, dgpre = _rms_bwd(dh, x_ref[rows, :], gpre_ref[...])
            dx = dres_ref[rows, :] + d1
            dx_ref[rows, :] = dx
            dm, dgpost = _rms_bwd(dx, m_ref[rows, :].astype(F32), gpost_ref[...])
            dm_ref[rows, :] = dm.astype(BF16)
            part = (dgpre, dgpost, jnp.sum(dm, axis=0, keepdims=True))
            sums = part if sums is None else tuple(u + v for u, v in zip(sums, part))
        _accum(dgpre_ref, sums[0], first)
        _accum(dgpost_ref, sums[1], first)
        _accum(db_ref, sums[2], first)

    def shard(j):
        return pl.BlockSpec((None, kout, ns), lambda i: (j, 0, 0))

    row, vec = _row_spec(tm, D_MODEL), _vec_spec(D_MODEL)
    vshape = jax.ShapeDtypeStruct((1, D_MODEL), F32)
    core, rr = _call(
        body, grid=(s // tm,), in_specs=[a_spec] + [shard(j) for j in range(N_CHIPS)] + [row, row, vec, row, vec],
        out_specs=[row, row, vec, vec, vec],
        out_shape=[jax.ShapeDtypeStruct((s, D_MODEL), F32), jax.ShapeDtypeStruct((s, D_MODEL), BF16), vshape, vshape, vshape],
        operands=(a, w, w, w, w, dres, x, g_pre, m, g_post), name=name, riders=riders)
    return _ret(core, rr, riders)


def ffn_bwd_rows(df, w_d, d_planes, w_gu, dres, x, g_pre, m, g_post, *, name, tm=512, riders=()):
    s = x.shape[0]
    tm = _row_tile(s, tm)

    def body(df_ref, wd_ref, d_ref, w0, w1, w2, w3, dres_ref, x_ref, gpre_ref, m_ref, gpost_ref,
             o_ref, dx_ref, dm_ref, dgpre_ref, dgpost_ref, db_ref):
        first = pl.program_id(0) == 0
        halves = [slice(half * FF_HALF, (half + 1) * FF_HALF) for half in (0, 1)]
        sub = min(SUB_ROWS, tm)
        sums = None
        for t in range(tm // sub):
            rows = slice(t * sub, (t + 1) * sub)
            dfv = df_ref[rows, :]
            dh = None
            for cols, (wg_ref, wu_ref) in zip(halves, ((w0, w2), (w1, w3))):
                da = lax.dot_general(dfv, wd_ref[cols, :], NT_DIMS, preferred_element_type=F32)
                dg = (da * d_ref[0, rows, cols].astype(F32)).astype(BF16)
                du = (da * d_ref[1, rows, cols].astype(F32)).astype(BF16)
                o_ref[0, rows, cols] = dg
                o_ref[1, rows, cols] = du
                p = lax.dot_general(dg, wg_ref[...], NT_DIMS, preferred_element_type=F32)
                p += lax.dot_general(du, wu_ref[...], NT_DIMS, preferred_element_type=F32)
                dh = p if dh is None else dh + p
            d1, dgpre = _rms_bwd(dh, x_ref[rows, :], gpre_ref[...])
            dx = dres_ref[rows, :] + d1
            dx_ref[rows, :] = dx
            dm, dgpost = _rms_bwd(dx, m_ref[rows, :].astype(F32), gpost_ref[...])
            dm_ref[rows, :] = dm.astype(BF16)
            part = (dgpre, dgpost, jnp.sum(dm, axis=0, keepdims=True))
            sums = part if sums is None else tuple(a + b for a, b in zip(sums, part))
        _accum(dgpre_ref, sums[0], first)
        _accum(dgpost_ref, sums[1], first)
        _accum(db_ref, sums[2], first)

    def resident(shape, index):
        return pl.BlockSpec(shape, index, pipeline_mode=pl.Buffered(1))

    planes = pl.BlockSpec((2, tm, D_FF), lambda i: (0, i, 0))
    row, vec = _row_spec(tm, D_MODEL), _vec_spec(D_MODEL)
    vshape = jax.ShapeDtypeStruct((1, D_MODEL), F32)
    shards = [resident((None, D_MODEL, FF_HALF), (lambda j: (lambda i: (j, 0, 0)))(j)) for j in range(N_CHIPS)]
    core, rr = _call(
        body, grid=(s // tm,),
        in_specs=[row, resident((D_FF, D_MODEL), lambda i: (0, 0)), planes] + shards + [row, row, vec, row, vec],
        out_specs=[planes, row, row, vec, vec, vec],
        out_shape=[jax.ShapeDtypeStruct((2, s, D_FF), BF16), jax.ShapeDtypeStruct((s, D_MODEL), F32),
                   jax.ShapeDtypeStruct((s, D_MODEL), BF16), vshape, vshape, vshape],
        operands=(df, w_d, d_planes, w_gu, w_gu, w_gu, w_gu, dres, x, g_pre, m, g_post), name=name, riders=riders,
        vmem_limit=BIG_VMEM_LIMIT)
    return _ret(core, rr, riders)


def dh_norm_bwd_last(a, w, dres, x, g_pre, *, name, tm=512, sub=256):
    _, kout, ns = w.shape
    s = x.shape[0]
    tm = _row_tile(s, tm)
    sub = min(sub, tm)

    def body(a_ref, w0, w1, w2, w3, dres_ref, x_ref, g_ref, dx_ref, dg_ref):
        total = None
        for t in range(tm // sub):
            rows = slice(t * sub, (t + 1) * sub)
            dh = None
            for j, w_ref in enumerate((w0, w1, w2, w3)):
                p = lax.dot_general(a_ref[rows, j * ns:(j + 1) * ns], w_ref[...], NT_DIMS, preferred_element_type=F32)
                dh = p if dh is None else dh + p
            d1, dg = _rms_bwd(dh, x_ref[rows, :], g_ref[...])
            dx_ref[rows, :] = dres_ref[rows, :] + d1
            total = dg if total is None else total + dg
        _accum(dg_ref, total, pl.program_id(0) == 0)

    def shard(j):
        return pl.BlockSpec((None, kout, ns), lambda i: (j, 0, 0))

    row, vec = _row_spec(tm, D_MODEL), _vec_spec(D_MODEL)
    (dx, dg), _ = _call(
        body, grid=(s // tm,), in_specs=[_row_spec(tm, N_CHIPS * ns)] + [shard(j) for j in range(N_CHIPS)] + [row, row, vec],
        out_specs=[row, vec], out_shape=[jax.ShapeDtypeStruct((s, D_MODEL), F32), jax.ShapeDtypeStruct((1, D_MODEL), F32)],
        operands=(a, w, w, w, w, dres, x, g_pre), name=name)
    return dx, dg


def _rope_tables(s):
    half = HEAD_DIM // 2
    inv_freq = np.float32(ROPE_THETA) ** (-(np.arange(half, dtype=np.float32) * np.float32(2.0)) / np.float32(HEAD_DIM))
    ang = np.arange(s, dtype=np.float32)[:, None] * inv_freq[None, :]
    cos, sin = np.cos(ang).astype(np.float32), np.sin(ang).astype(np.float32)
    return jnp.asarray(np.tile(cos, (1, 4))), jnp.asarray(np.concatenate([-sin, sin, -sin, sin], axis=1))


def _swap_halves(x):
    lane = lax.broadcasted_iota(I32, x.shape, 1)
    return jnp.where((lane & (HEAD_DIM - 1)) < HEAD_DIM // 2, pltpu.roll(x, LANES - 32, 1), pltpu.roll(x, 32, 1))


N_ROPE_BLOCKS = (Q_WIDTH + KV_WIDTH) // LANES


def qkv_proj(h, w, bias, cos, sin, *, name, tm=1024, riders=()):
    s, k = h.shape
    ns = w.shape[2]
    tm = _row_tile(s, tm)

    def body(h_ref, w_ref, b_ref, c_ref, s_ref, o_ref):
        j = pl.program_id(0)
        sub = min(256, tm)
        for t in range(tm // sub):
            rows = slice(t * sub, (t + 1) * sub)
            p = jnp.dot(h_ref[rows, :], w_ref[...], preferred_element_type=F32) + b_ref[...]
            cosv, sinv = c_ref[rows, :], s_ref[rows, :]
            for blk in range(ns // LANES):
                xb = p[:, blk * LANES:(blk + 1) * LANES]
                roped = xb * cosv + _swap_halves(xb) * sinv
                is_qk = j * (ns // LANES) + blk < N_ROPE_BLOCKS
                o_ref[rows, blk * LANES:(blk + 1) * LANES] = jnp.where(is_qk, roped, xb).astype(BF16)

    core, rr = _call(
        body, grid=(N_CHIPS, s // tm),
        in_specs=[pl.BlockSpec((tm, k), lambda j, i: (i, 0)), pl.BlockSpec((None, k, ns), lambda j, i: (j, 0, 0)),
                  pl.BlockSpec((1, ns), lambda j, i: (0, j)), pl.BlockSpec((tm, LANES), lambda j, i: (i, 0)),
                  pl.BlockSpec((tm, LANES), lambda j, i: (i, 0))],
        out_specs=[pl.BlockSpec((tm, ns), lambda j, i: (i, j))], out_shape=[jax.ShapeDtypeStruct((s, N_CHIPS * ns), BF16)],
        operands=(h, w, bias, cos, sin), sem=("parallel", "parallel"), name=name, riders=riders)
    return _ret(core, rr, riders)


def rope_bwd(dq, dkc, dkp, dvc, dvp, cos, sin, *, name, riders=()):
    s = dq.shape[0]
    tm = 2 * WINDOW if s % (2 * WINDOW) == 0 else WINDOW
    nb = s // tm

    def body(dq_ref, dkc_ref, dkp_ref, dkp_next_ref, dvc_ref, dvp_ref, dvp_next_ref, c_ref, s_ref, o_ref, db_ref):
        i = pl.program_id(0)
        has_next = (i < nb - 1).astype(F32)
        cosv, sinv = c_ref[...], s_ref[...]

        def shifted(ref, next_ref, cols):
            last = has_next * next_ref[:WINDOW, cols].astype(F32)
            return last if tm == WINDOW else jnp.concatenate([ref[WINDOW:, cols].astype(F32), last], axis=0)

        parts = []
        for blk in range(QKV_WIDTH // LANES):
            if blk < Q_WIDTH // LANES:
                g = dq_ref[:, blk * LANES:(blk + 1) * LANES].astype(F32)
            else:
                own, prv, nxt = (dkc_ref, dkp_ref, dkp_next_ref) if blk < N_ROPE_BLOCKS else (dvc_ref, dvp_ref, dvp_next_ref)
                cols = slice((blk % 2) * LANES, (blk % 2 + 1) * LANES)
                g = own[:, cols].astype(F32) + shifted(prv, nxt, cols)
            if blk < N_ROPE_BLOCKS:
                g = g * cosv + _swap_halves(g * sinv)
            o_ref[:, blk * LANES:(blk + 1) * LANES] = g.astype(BF16)
            parts.append(jnp.sum(g, axis=0, keepdims=True))
        sums = jnp.concatenate(parts, axis=1)
        _accum(db_ref, sums, i == 0)

    own_spec = _row_spec(tm, KV_WIDTH)
    next_spec = pl.BlockSpec((tm, KV_WIDTH), lambda i: (jnp.minimum(i + 1, nb - 1), 0))
    core, rr = _call(
        body, grid=(nb,),
        in_specs=[_row_spec(tm, Q_WIDTH), own_spec, own_spec, next_spec, own_spec, own_spec, next_spec,
                  _row_spec(tm, LANES), _row_spec(tm, LANES)],
        out_specs=[_row_spec(tm, QKV_WIDTH), _vec_spec(QKV_WIDTH)],
        out_shape=[jax.ShapeDtypeStruct((s, QKV_WIDTH), BF16), jax.ShapeDtypeStruct((1, QKV_WIDTH), F32)],
        operands=(dq, dkc, dkp, dkp, dvc, dvp, dvp, cos, sin), name=name, riders=riders)
    return _ret(core, rr, riders)


ROWS = GQA_GROUP * WINDOW


def _prev_slots():
    kpos = lax.broadcasted_iota(I32, (WINDOW, ROWS), 0)
    qpos = lax.broadcasted_iota(I32, (WINDOW, ROWS), 1) & (WINDOW - 1)
    return kpos > qpos


def _head_cols(ref, head):
    return ref[:, head * HEAD_DIM:(head + 1) * HEAD_DIM]


def _stack_heads(ref, h):
    return jnp.concatenate([_head_cols(ref, GQA_GROUP * h + g) for g in range(GQA_GROUP)], axis=0)


def _band(prev_ref, cur_ref, h):
    return jnp.concatenate([_head_cols(prev_ref, h), _head_cols(cur_ref, h)], axis=0)


def _pick(prev, band):
    return jnp.where(prev, band[:WINDOW], band[WINDOW:])


def _spread(prev, x):
    return jnp.concatenate([jnp.where(prev, x, 0.0), jnp.where(prev, 0.0, x)], axis=0).astype(BF16)


def _attn_probs(s_band, sink, prev, has_prev):
    scale = HEAD_DIM ** -0.5
    s = jnp.where(prev, jnp.where(has_prev, s_band[:WINDOW], NEG), s_band[WINDOW:]) * scale
    m = jnp.maximum(jnp.max(s, axis=0, keepdims=True), sink)
    e, es = jnp.exp(s - m), jnp.exp(sink - m)
    inv = 1.0 / (jnp.sum(e, axis=0, keepdims=True) + es)
    return e * inv, es * inv


def _attn_specs(nb):
    kcol, vcol = Q_WIDTH // KV_WIDTH, Q_WIDTH // KV_WIDTH + 1
    q_spec = pl.BlockSpec((WINDOW, Q_WIDTH), lambda n: (n, 0))
    return [q_spec,
            pl.BlockSpec((WINDOW, KV_WIDTH), lambda n: (n, kcol)),
            pl.BlockSpec((WINDOW, KV_WIDTH), lambda n: (jnp.maximum(n - 1, 0), kcol)),
            pl.BlockSpec((WINDOW, KV_WIDTH), lambda n: (n, vcol)),
            pl.BlockSpec((WINDOW, KV_WIDTH), lambda n: (jnp.maximum(n - 1, 0), vcol)),
            pl.BlockSpec((N_KV_HEADS, 8, ROWS), lambda n: (0, 0, 0))]


def attn_fwd(qkv, sink_rows, *, name, riders=()):
    s = qkv.shape[0]

    def body(q_ref, kc_ref, kp_ref, vc_ref, vp_ref, sink_ref, o_ref):
        prev = _prev_slots()
        has_prev = pl.program_id(0) > 0
        heads = range(N_KV_HEADS)
        s_bands = [lax.dot_general(_band(kp_ref, kc_ref, h), _stack_heads(q_ref, h), NT_DIMS, preferred_element_type=F32)
                   for h in heads]
        p_bands = [_spread(prev, _attn_probs(s_bands[h], sink_ref[h, 0:1, :], prev, has_prev)[0]) for h in heads]
        outs = [lax.dot_general(_band(vp_ref, vc_ref, h), p_bands[h], TN_DIMS, preferred_element_type=F32).T for h in heads]
        for h in heads:
            for g in range(GQA_GROUP):
                head = GQA_GROUP * h + g
                o_ref[:, head * HEAD_DIM:(head + 1) * HEAD_DIM] = outs[h][g * WINDOW:(g + 1) * WINDOW].astype(BF16)

    core, rr = _call(
        body, grid=(s // WINDOW,), in_specs=_attn_specs(s // WINDOW), out_specs=[pl.BlockSpec((WINDOW, Q_WIDTH), lambda n: (n, 0))],
        out_shape=[jax.ShapeDtypeStruct((s, Q_WIDTH), BF16)], operands=(qkv, qkv, qkv, qkv, qkv, sink_rows), sem=("parallel",),
        name=name, riders=riders)
    return _ret(core, rr, riders)


def attn_bwd(qkv, sink_rows, do, *, name, riders=()):
    s = qkv.shape[0]

    def body(q_ref, kc_ref, kp_ref, vc_ref, vp_ref, sink_ref, do_ref, dq_ref, dkc_ref, dkp_ref, dvc_ref, dvp_ref, dsink_ref):
        n = pl.program_id(0)
        prev = _prev_slots()
        scale = HEAD_DIM ** -0.5
        heads = range(N_KV_HEADS)
        qs, dos = [_stack_heads(q_ref, h) for h in heads], [_stack_heads(do_ref, h) for h in heads]
        kbands, vbands = [_band(kp_ref, kc_ref, h) for h in heads], [_band(vp_ref, vc_ref, h) for h in heads]
        s_bands = [lax.dot_general(kbands[h], qs[h], NT_DIMS, preferred_element_type=F32) for h in heads]
        dp_bands = [lax.dot_general(vbands[h], dos[h], NT_DIMS, preferred_element_type=F32) for h in heads]
        ds_bands, p_bands, parts = [], [], []
        for h in heads:
            p, ps = _attn_probs(s_bands[h], sink_ref[h, 0:1, :], prev, n > 0)
            dp = _pick(prev, dp_bands[h])
            delta = jnp.sum(p * dp, axis=0, keepdims=True)
            ds_bands.append(_spread(prev, p * (dp - delta) * scale))
            p_bands.append(_spread(prev, p))
            dsink = -(ps * delta)
            for g in range(GQA_GROUP):
                parts.append(jnp.broadcast_to(jnp.sum(dsink[:, g * WINDOW:(g + 1) * WINDOW], axis=1, keepdims=True), (8, LANES)))
        for h in heads:
            dk = jnp.dot(ds_bands[h], qs[h], preferred_element_type=F32).astype(BF16)
            dv = jnp.dot(p_bands[h], dos[h], preferred_element_type=F32).astype(BF16)
            dq = lax.dot_general(kbands[h], ds_bands[h], TN_DIMS, preferred_element_type=F32).T
            cols = slice(h * HEAD_DIM, (h + 1) * HEAD_DIM)
            dkp_ref[:, cols], dkc_ref[:, cols] = dk[:WINDOW], dk[WINDOW:]
            dvp_ref[:, cols], dvc_ref[:, cols] = dv[:WINDOW], dv[WINDOW:]
            for g in range(GQA_GROUP):
                head = GQA_GROUP * h + g
                dq_ref[:, head * HEAD_DIM:(head + 1) * HEAD_DIM] = dq[g * WINDOW:(g + 1) * WINDOW].astype(BF16)

        @pl.when(n == 0)
        def _():
            for i, part in enumerate(parts):
                dsink_ref[i // GQA_GROUP, i % GQA_GROUP] = part

        @pl.when(n > 0)
        def _():
            for i, part in enumerate(parts):
                dsink_ref[i // GQA_GROUP, i % GQA_GROUP] += part

    rows_q = pl.BlockSpec((WINDOW, Q_WIDTH), lambda n: (n, 0))
    rows_kv = pl.BlockSpec((WINDOW, KV_WIDTH), lambda n: (n, 0))
    kv_shape = jax.ShapeDtypeStruct((s, KV_WIDTH), BF16)
    core, rr = _call(
        body, grid=(s // WINDOW,), in_specs=_attn_specs(s // WINDOW) + [rows_q],
        out_specs=[rows_q, rows_kv, rows_kv, rows_kv, rows_kv,
                   pl.BlockSpec((N_KV_HEADS, GQA_GROUP, 8, LANES), lambda n: (0, 0, 0, 0))],
        out_shape=[jax.ShapeDtypeStruct((s, Q_WIDTH), BF16), kv_shape, kv_shape, kv_shape, kv_shape,
                   jax.ShapeDtypeStruct((N_KV_HEADS, GQA_GROUP, 8, LANES), F32)],
        operands=(qkv, qkv, qkv, qkv, qkv, sink_rows, do), sem=("arbitrary",), name=name, riders=riders)
    return _ret(core, rr, riders)


GELU_C = 0.7978845608028654
GELU_A = 0.044715


def _gelu(x):
    return 0.5 * x * (1.0 + jnp.tanh(x * (GELU_C + (GELU_C * GELU_A) * (x * x))))


def _gelu_and_grad(x):
    x2 = x * x
    t = jnp.tanh(x * (GELU_C + (GELU_C * GELU_A) * x2))
    half_x, one_t = 0.5 * x, 1.0 + t
    return half_x * one_t, 0.5 * one_t + half_x * (1.0 - t * t) * (GELU_C + (3.0 * GELU_C * GELU_A) * x2)


def _tril_bf16(w):
    row = lax.broadcasted_iota(I32, (SGU_CHUNK, SGU_CHUNK), 0)
    col = lax.broadcasted_iota(I32, (SGU_CHUNK, SGU_CHUNK), 1)
    return jnp.where(row >= col, w, 0.0).astype(BF16)


def _sgu_norm(vg, g, b):
    mu = jnp.mean(vg, axis=-1, keepdims=True)
    cen = vg - mu
    rstd = lax.rsqrt(jnp.mean(cen * cen, axis=-1, keepdims=True) + EPS)
    xhat = cen * rstd
    return xhat, rstd, xhat * g + b


def sgu_in_fwd(h, w_in, ln_g, ln_b, w_sp, b_sp, *, name, tm=512, riders=()):
    s, k = h.shape
    ns = w_in.shape[2]
    tm = _row_tile(s, tm)

    def body(h_ref, w0, w1, w2, w3, g_ref, b_ref, w_ref, bs_ref, z_ref, y_ref):
        hv = h_ref[...]
        zs = [jnp.dot(hv, w_ref_j[...], preferred_element_type=F32) for w_ref_j in (w0, w1, w2, w3)]
        for j, zj in enumerate(zs):
            z_ref[:, j * ns:(j + 1) * ns] = zj.astype(BF16)
        u = _gelu(jnp.concatenate(zs[:2], axis=1))
        _, _, vn = _sgu_norm(_gelu(jnp.concatenate(zs[2:], axis=1)), g_ref[...], b_ref[...])
        vn = vn.astype(BF16)
        for grp in range(SGU_GROUPS):
            w = _tril_bf16(w_ref[grp])
            cols = slice(grp * LANES, (grp + 1) * LANES)
            for ch in range(tm // SGU_CHUNK):
                rows = slice(ch * SGU_CHUNK, (ch + 1) * SGU_CHUNK)
                mixed = jnp.dot(w, vn[rows, cols], preferred_element_type=F32) + bs_ref[grp]
                y_ref[rows, cols] = (u[rows, cols] * mixed).astype(BF16)

    def shard(j):
        return pl.BlockSpec((None, k, ns), lambda i: (j, 0, 0))

    full3 = pl.BlockSpec((SGU_GROUPS, SGU_CHUNK, SGU_CHUNK), lambda i: (0, 0, 0))
    core, rr = _call(
        body, grid=(s // tm,),
        in_specs=[_row_spec(tm, k)] + [shard(j) for j in range(N_CHIPS)] + [_vec_spec(D_MODEL), _vec_spec(D_MODEL), full3, full3],
        out_specs=[_row_spec(tm, 2 * D_MODEL), _row_spec(tm, D_MODEL)],
        out_shape=[jax.ShapeDtypeStruct((s, 2 * D_MODEL), BF16), jax.ShapeDtypeStruct((s, D_MODEL), BF16)],
        operands=(h, w_in, w_in, w_in, w_in, ln_g, ln_b, w_sp, b_sp), sem=("parallel",), name=name, riders=riders)
    return _ret(core, rr, riders)


def sgu_bwd(z, dy, ln_g, ln_b, w_sp, b_sp, *, name, tm=256, riders=()):
    s = z.shape[0]
    tm = _row_tile(s, tm)

    def body(z_ref, dy_ref, g_ref, b_ref, w_ref, bs_ref, dz_ref, dw_ref, dbs_ref, dg_ref, db_ref, dvn_buf):
        first = pl.program_id(0) == 0
        u, u_grad = _gelu_and_grad(z_ref[:, :D_MODEL].astype(F32))
        vg, v_grad = _gelu_and_grad(z_ref[:, D_MODEL:].astype(F32))
        xhat, rstd, vn = _sgu_norm(vg, g_ref[...], b_ref[...])
        vn = vn.astype(BF16)
        dyv = dy_ref[...]
        dmixed = dyv * u
        dz_gate = dyv * u_grad
        row = lax.broadcasted_iota(I32, (SGU_CHUNK, SGU_CHUNK), 0)
        col = lax.broadcasted_iota(I32, (SGU_CHUNK, SGU_CHUNK), 1)
        dws, dbss = [], []
        for grp in range(SGU_GROUPS):
            w = _tril_bf16(w_ref[grp])
            cols = slice(grp * LANES, (grp + 1) * LANES)
            dw = jnp.zeros((SGU_CHUNK, SGU_CHUNK), F32)
            dbs = jnp.zeros((SGU_CHUNK, 1), F32)
            for ch in range(tm // SGU_CHUNK):
                rows = slice(ch * SGU_CHUNK, (ch + 1) * SGU_CHUNK)
                vblk = vn[rows, cols]
                mixed = jnp.dot(w, vblk, preferred_element_type=F32) + bs_ref[grp]
                dz_ref[rows, cols] = (dz_gate[rows, cols] * mixed).astype(BF16)
                dm = dmixed[rows, cols]
                dmb = dm.astype(BF16)
                dvn_buf[rows, cols] = lax.dot_general(w, dmb, TN_DIMS, preferred_element_type=F32)
                dw += lax.dot_general(dmb, vblk, NT_DIMS, preferred_element_type=F32)
                dbs += jnp.sum(dm, axis=-1, keepdims=True)
            dws.append(jnp.where(row >= col, dw, 0.0))
            dbss.append(jnp.broadcast_to(dbs, (SGU_CHUNK, SGU_CHUNK)))

        dvn = dvn_buf[...]
        dxhat = dvn * g_ref[...]
        dvg = rstd * (dxhat - jnp.mean(dxhat, axis=-1, keepdims=True) - xhat * jnp.mean(dxhat * xhat, axis=-1, keepdims=True))
        dz_ref[:, D_MODEL:] = (dvg * v_grad).astype(BF16)
        dlng, dlnb = jnp.sum(dvn * xhat, axis=0, keepdims=True), jnp.sum(dvn, axis=0, keepdims=True)

        @pl.when(first)
        def _():
            for grp in range(SGU_GROUPS):
                dw_ref[grp] = dws[grp]
                dbs_ref[grp] = dbss[grp]
            dg_ref[...] = dlng
            db_ref[...] = dlnb

        @pl.when(jnp.logical_not(first))
        def _():
            for grp in range(SGU_GROUPS):
                dw_ref[grp] += dws[grp]
                dbs_ref[grp] += dbss[grp]
            dg_ref[...] += dlng
            db_ref[...] += dlnb

    full3 = pl.BlockSpec((SGU_GROUPS, SGU_CHUNK, SGU_CHUNK), lambda i: (0, 0, 0))
    s3 = jax.ShapeDtypeStruct((SGU_GROUPS, SGU_CHUNK, SGU_CHUNK), F32)
    vshape = jax.ShapeDtypeStruct((1, D_MODEL), F32)
    core, rr = _call(
        body, grid=(s // tm,),
        in_specs=[_row_spec(tm, 2 * D_MODEL), _row_spec(tm, D_MODEL), _vec_spec(D_MODEL), _vec_spec(D_MODEL), full3, full3],
        out_specs=[_row_spec(tm, 2 * D_MODEL), full3, full3, _vec_spec(D_MODEL), _vec_spec(D_MODEL)],
        out_shape=[jax.ShapeDtypeStruct((s, 2 * D_MODEL), BF16), s3, s3, vshape, vshape],
        scratch_shapes=[pltpu.VMEM((tm, D_MODEL), F32)], operands=(z, dy, ln_g, ln_b, w_sp, b_sp), name=name, riders=riders)
    return _ret(core, rr, riders)


def _sigmoid(x):
    return 1.0 / (1.0 + jnp.exp(-x))


def ffn_up(h, w_gu, *, name, tm=512, riders=()):
    s = h.shape[0]
    tm = _row_tile(s, tm)

    def body(h_ref, wg_ref, wu_ref, d_ref, a_ref):
        hv = h_ref[...]
        sub = min(256, tm)
        for t in range(tm // sub):
            rows = slice(t * sub, (t + 1) * sub)
            g = jnp.dot(hv[rows], wg_ref[...], preferred_element_type=F32)
            u = jnp.dot(hv[rows], wu_ref[...], preferred_element_type=F32)
            sig = _sigmoid(g)
            silu = g * sig
            d_ref[0, rows, :] = (u * (sig + silu * (1.0 - sig))).astype(BF16)
            d_ref[1, rows, :] = silu.astype(BF16)
            a_ref[rows, :] = (silu * u).astype(BF16)

    core, rr = _call(
        body, grid=(2, s // tm),
        in_specs=[pl.BlockSpec((tm, D_MODEL), lambda j, i: (i, 0)),
                  pl.BlockSpec((None, D_MODEL, FF_HALF), lambda j, i: (j, 0, 0)),
                  pl.BlockSpec((None, D_MODEL, FF_HALF), lambda j, i: (j + 2, 0, 0))],
        out_specs=[pl.BlockSpec((2, tm, FF_HALF), lambda j, i: (0, i, j)), pl.BlockSpec((tm, FF_HALF), lambda j, i: (i, j))],
        out_shape=[jax.ShapeDtypeStruct((2, s, D_FF), BF16), jax.ShapeDtypeStruct((s, D_FF), BF16)],
        operands=(h, w_gu, w_gu), sem=("parallel", "parallel"), name=name, riders=riders)
    return _ret(core, rr, riders)


def _weight_tile(rows):
    for tr in (512, 352, 256, 128):
        if rows % tr == 0:
            return tr
    return rows


def place_shard(w, layer, chip_arr, dtype, *, name, riders=()):
    _, r, c = w.shape
    tr = _weight_tile(r)

    def body(chip_ref, w_ref, o_ref):
        o_ref[...] = w_ref[...].astype(dtype)

    core, rr = _call(
        body, grid=(r // tr,), prefetch=(chip_arr,),
        in_specs=[pl.BlockSpec((None, tr, c), lambda i, chip: (layer, i, 0))],
        out_specs=[pl.BlockSpec((None, tr, c), lambda i, chip: (chip[0], i, 0))],
        out_shape=[jax.ShapeDtypeStruct((N_CHIPS, r, c), dtype)], operands=(w,), sem=("parallel",), name=name, riders=riders)
    return _ret(core, rr, riders)


def _adamw_math(w, g, m, v):
    m = ADAM_B1 * m + (1.0 - ADAM_B1) * g
    v = ADAM_B2 * v + (1.0 - ADAM_B2) * (g * g)
    m_hat = m / (1.0 - ADAM_B1 ** ADAM_STEP)
    v_hat = v / (1.0 - ADAM_B2 ** ADAM_STEP)
    delta = -ADAM_LR * (m_hat / (jnp.sqrt(v_hat) + ADAM_EPS) + ADAM_WD * w)
    return delta, m, v


def adamw(w, g, m, v, *, name, after=None):
    nl, r, c = w.shape
    tr = _weight_tile(r)

    def body(w_ref, g_ref, m_ref, v_ref, *rest):
        go_ref, d_ref, mo_ref, vo_ref = rest[-4:]
        gv = g_ref[...]
        go_ref[...] = gv
        d_ref[...], mo_ref[...], vo_ref[...] = _adamw_math(w_ref[...], gv, m_ref[...], v_ref[...])

    spec = pl.BlockSpec((None, tr, c), lambda l, i: (l, i, 0))
    shape = jax.ShapeDtypeStruct(w.shape, F32)
    extra = [] if after is None else [after]
    outs, _ = _call(body, grid=(nl, r // tr), in_specs=[spec] * 4 + [ANY] * len(extra), out_specs=[spec] * 4,
                    out_shape=[shape] * 4, operands=(w, g, m, v, *extra), sem=("parallel", "parallel"), name=name)
    return outs


def adamw_small(ws, gs, ms, vs, *, name):
    n = len(ws)

    def body(*refs):
        ins, outs = refs[:4 * n], refs[4 * n:]
        for t in range(n):
            gv = ins[n + t][...]
            outs[t][...] = gv
            outs[n + t][...], outs[2 * n + t][...], outs[3 * n + t][...] = _adamw_math(
                ins[t][...], gv, ins[2 * n + t][...], ins[3 * n + t][...])

    shapes = [jax.ShapeDtypeStruct(w.shape, F32) for w in ws]
    res = pl.pallas_call(body, out_shape=shapes * 4, name=name)(*ws, *gs, *ms, *vs)
    return res[:n], res[n:2 * n], res[2 * n:3 * n], res[3 * n:]


def pair_add(g, r1, c_arr, *, name):
    _, rows, cdim = g.shape
    h = rows // 2

    def body(c_ref, g_ref, r_ref, o_ref):
        o_ref[...] = (g_ref[...].astype(F32) + r_ref[...].astype(F32)).astype(o_ref.dtype)

    (out,), _ = _call(
        body, grid=(N_CHIPS,), prefetch=(c_arr,),
        in_specs=[pl.BlockSpec((None, h, cdim), lambda s, c: (s, c[0], 0)), pl.BlockSpec((None, h, cdim), lambda s, c: (s, 0, 0))],
        out_specs=[pl.BlockSpec((None, h, cdim), lambda s, c: (s, 0, 0))],
        out_shape=[jax.ShapeDtypeStruct((N_CHIPS, h, cdim), g.dtype)], operands=(g, r1), sem=("parallel",), name=name)
    return out


def final_add(g, r1, r2, jc_arr, *, dest_shape, lead, prev, name):
    _, rows, cdim = g.shape
    h = rows // 2

    def body(jc_ref, g_ref, r1_ref, r2_ref, *rest):
        o_ref = rest[-1]
        acc = g_ref[...].astype(F32) + r1_ref[...].astype(F32)
        for k in range(3):
            acc = acc + r2_ref[k].astype(F32)
        o_ref[...] = acc

    if lead is None:
        o_spec = pl.BlockSpec((h, cdim), lambda i, jc: (jc[1], 0))
    elif lead == "chip":
        o_spec = pl.BlockSpec((None, h, cdim), lambda i, jc: (jc[0], jc[1], 0))
    else:
        o_spec = pl.BlockSpec((None, h, cdim), lambda i, jc: (lead, jc[1], 0))
    in_specs = [pl.BlockSpec((None, h, cdim), lambda i, jc: (jc[0], jc[1], 0)),
                pl.BlockSpec((None, h, cdim), lambda i, jc: (jc[0], 0, 0)),
                pl.BlockSpec((3, h, cdim), lambda i, jc: (0, 0, 0))]
    operands = [g, r1, r2]
    aliases = None
    if prev is not None:
        in_specs.append(ANY)
        operands.append(prev)
        aliases = {3: 0}
    (out,), _ = _call(body, grid=(1,), prefetch=(jc_arr,), in_specs=in_specs, out_specs=[o_spec],
                      out_shape=[jax.ShapeDtypeStruct(dest_shape, F32)], operands=operands, aliases=aliases, name=name)
    return out


def _place():
    return lax.axis_index("x"), lax.axis_index("y"), lax.axis_index("c")


def _partner(x, y, k):
    return (1 - x if k >> 1 else x), (1 - y if k & 1 else y)


WHOLE = (0, 1, 1)


def _half(rows, sel, dtype, piece=WHOLE):
    lo, hi, n = piece
    align = 16 if dtype == BF16 else 8
    step = rows // 2 // n
    assert rows // 2 == step * n and step % align == 0
    return pl.ds(pl.multiple_of(sel * (rows // 2) + lo * step, align), (hi - lo) * step)


def _rider(peers, inputs, aliased, fresh, nsem, copies, arrivals):
    def start(ins, outs, send, recv):
        for cp in copies(ins, outs, send, recv):
            cp.start()

    def finish(ins, outs, send, recv):
        for cp in arrivals(ins, outs, send, recv):
            cp.wait_recv()
        for cp in copies(ins, outs, send, recv):
            cp.wait_send()

    return types.SimpleNamespace(peers=peers, inputs=list(inputs), aliased=list(aliased), fresh=list(fresh), nsem=nsem,
                                 start=start, finish=finish)


def _remote(src, dst, send, recv, idx, dev):
    return pltpu.make_async_remote_copy(src_ref=src, dst_ref=dst, send_sem=send.at[idx], recv_sem=recv.at[idx],
                                        device_id=dev, device_id_type=MESH)


def gather_ici_rider(fulls, pieces=None):
    nt = len(fulls)
    pieces = pieces or [WHOLE] * nt

    def region(outs, t, slot, sel):
        return outs[t].at[slot, _half(fulls[t].shape[1], sel, fulls[t].dtype, pieces[t])]

    def copies(ins, outs, send, recv):
        x, y, c = _place()
        res = []
        for t in range(nt):
            for k in (1, 2, 3):
                px, py = _partner(x, y, k)
                mine = region(outs, t, 2 * x + y, c)
                res.append(_remote(mine, mine, send, recv, 3 * t + k - 1, (px, py, c)))
        return res

    def arrivals(ins, outs, send, recv):
        x, y, c = _place()
        res = []
        for t in range(nt):
            for k in (1, 2, 3):
                px, py = _partner(x, y, k)
                theirs = region(outs, t, 2 * px + py, c)
                res.append(_remote(theirs, theirs, send, recv, 3 * t + k - 1, (x, y, c)))
        return res

    return _rider("chips", fulls, range(nt), [], 3 * nt, copies, arrivals)


def gather_d2d_rider(fulls, pieces=None):
    nt = len(fulls)
    pieces = pieces or [WHOLE] * nt

    def region(outs, t, slot, sel):
        return outs[t].at[slot, _half(fulls[t].shape[1], sel, fulls[t].dtype, pieces[t])]

    def both(outs, send, recv, mine):
        x, y, c = _place()
        res = []
        for t in range(nt):
            for k in (1, 2, 3):
                px, py = _partner(x, y, k)
                part = region(outs, t, 2 * px + py, c if mine else 1 - c)
                res.append(_remote(part, part, send, recv, 3 * t + k - 1, (x, y, 1 - c)))
        return res

    return _rider("sibling", fulls, range(nt), [], 3 * nt, lambda i, o, s, r: both(o, s, r, True),
                  lambda i, o, s, r: both(o, s, r, False))


def exchange_rider(grads):
    nt = len(grads)

    def both(ins, outs, send, recv):
        x, y, c = _place()
        return [_remote(ins[t].at[:, _half(grads[t].shape[1], 1 - c, grads[t].dtype)], outs[t], send, recv, t, (x, y, 1 - c))
                for t in range(nt)]

    fresh = [jax.ShapeDtypeStruct((N_CHIPS, g.shape[1] // 2, g.shape[2]), g.dtype) for g in grads]
    return _rider("sibling", grads, [], fresh, nt, both, both)


def scatter_rider(parts):
    nt = len(parts)

    def both(ins, outs, send, recv):
        x, y, c = _place()
        res = []
        for t in range(nt):
            for k in (1, 2, 3):
                px, py = _partner(x, y, k)
                res.append(_remote(ins[t].at[2 * px + py], outs[t].at[k - 1], send, recv, 3 * t + k - 1, (px, py, c)))
        return res

    fresh = [jax.ShapeDtypeStruct((3,) + p.shape[1:], p.dtype) for p in parts]
    return _rider("chips", parts, [], fresh, 3 * nt, both, both)


def broadcast_rider(bufs, items):
    def region(outs, item, sel):
        bi, lead = item
        ref = outs[bi]
        if lead == "chip":
            x, y, _ = _place()
            ref = ref.at[2 * x + y]
        elif lead is not None:
            ref = ref.at[lead]
        return ref.at[_half(ref.shape[0], sel, F32)]

    def both(outs, send, recv, mine):
        x, y, c = _place()
        res = []
        for i, item in enumerate(items):
            part = region(outs, item, c if mine else 1 - c)
            res.append(_remote(part, part, send, recv, i, (x, y, 1 - c)))
        return res

    return _rider("sibling", bufs, range(len(bufs)), [], len(items), lambda i, o, s, r: both(o, s, r, True),
                  lambda i, o, s, r: both(o, s, r, False))


def allcast_rider(buf):
    peers = [(k, flip) for k in range(N_CHIPS) for flip in (0, 1) if (k, flip) != (0, 0)]

    def both(outs, send, recv, mine):
        x, y, c = _place()
        res = []
        for i, (k, flip) in enumerate(peers):
            px, py = _partner(x, y, k)
            pc = 1 - c if flip else c
            slot, sel = (2 * x + y, c) if mine else (2 * px + py, pc)
            part = outs[0].at[slot, _half(buf.shape[1], sel, F32)]
            res.append(_remote(part, part, send, recv, i, (px, py, pc)))
        return res

    return _rider("everyone", [buf], [0], [], len(peers), lambda i, o, s, r: both(o, s, r, True),
                  lambda i, o, s, r: both(o, s, r, False))


def comm_call(riders, *, name):
    _, res = _call(None, riders=riders, name=name)
    return res


SEMS = pl.BlockSpec(memory_space=pltpu.SEMAPHORE)
SIDE_EFFECT = pltpu.SideEffectType.DATAFLOW_SIDE_EFFECTING


def _split_refs(riders, refs):
    views, p = [], 0
    for r in riders:
        bufs = refs[p:p + len(r.inputs) + len(r.fresh)]
        p += len(bufs)
        ins = bufs[:len(r.inputs)]
        views.append([ins, [ins[i] for i in r.aliased] + list(bufs[len(r.inputs):])])
    for view in views:
        view += [refs[p], refs[p + 1]]
        p += 2
    return views


def comm_start(riders, *, name):
    kind = _peer_kind(riders)
    bufs = [a for r in riders for a in r.inputs]
    fresh = [f for r in riders for f in r.fresh]
    n_buf, n_fresh = len(bufs), len(fresh)

    def body(*refs):
        ins, outs = refs[:n_buf], refs[n_buf:]
        through, land, sems = outs[:n_buf], outs[n_buf:n_buf + n_fresh], outs[n_buf + n_fresh:-1]
        _peer_barrier(kind)
        per_rider, pb, pf = [], 0, 0
        for r in riders:
            per_rider += list(through[pb:pb + len(r.inputs)]) + list(land[pf:pf + len(r.fresh)])
            pb, pf = pb + len(r.inputs), pf + len(r.fresh)
        for r, (r_ins, r_outs, send, recv) in zip(riders, _split_refs(riders, per_rider + list(sems))):
            r.start(r_ins, r_outs, send, recv)
        outs[-1][...] = jnp.zeros((8, LANES), F32)

    sem_shapes = [pltpu.SemaphoreType.DMA((r.nsem,)) for r in riders for _ in (0, 1)]
    res = pl.pallas_call(
        body, name=name, in_specs=[ANY] * n_buf,
        out_specs=[ANY] * (n_buf + n_fresh) + [SEMS] * len(sem_shapes) + [pl.BlockSpec(memory_space=pltpu.VMEM)],
        out_shape=[jax.ShapeDtypeStruct(a.shape, a.dtype) for a in bufs] + fresh + sem_shapes
        + [jax.ShapeDtypeStruct((8, LANES), F32)],
        input_output_aliases={i: i for i in range(n_buf)},
        compiler_params=pltpu.CompilerParams(has_side_effects=SIDE_EFFECT, collective_id=PEER_KINDS.index(kind)))(*bufs)
    return (riders, list(res[:n_buf + n_fresh]), list(res[n_buf + n_fresh:-1])), res[-1]


def comm_wait(state, after, *, name):
    riders, bufs, sems = state
    n_buf, n_sem = len(bufs), len(sems)
    n_in = sum(len(r.inputs) for r in riders)

    def body(*refs):
        held, sem_refs = refs[:n_buf], refs[n_buf:n_buf + n_sem]
        through, land = held[:n_in], held[n_in:]
        per_rider, pb, pf = [], 0, 0
        for r in riders:
            per_rider += list(through[pb:pb + len(r.inputs)]) + list(land[pf:pf + len(r.fresh)])
            pb, pf = pb + len(r.inputs), pf + len(r.fresh)
        for r, (r_ins, r_outs, send, recv) in zip(riders, _split_refs(riders, per_rider + list(sem_refs))):
            r.finish(r_ins, r_outs, send, recv)

    res = pl.pallas_call(
        body, name=name, in_specs=[ANY] * n_buf + [SEMS] * n_sem + [ANY], out_specs=[ANY] * n_buf,
        out_shape=[jax.ShapeDtypeStruct(a.shape, a.dtype) for a in bufs],
        input_output_aliases={i: i for i in range(n_buf)},
        compiler_params=pltpu.CompilerParams(has_side_effects=SIDE_EFFECT))(*bufs, *sems, after)
    through, land = list(res[:n_in]), list(res[n_in:])
    out, pb, pf = [], 0, 0
    for r in riders:
        r_ins, r_land = through[pb:pb + len(r.inputs)], land[pf:pf + len(r.fresh)]
        pb, pf = pb + len(r.inputs), pf + len(r.fresh)
        out.append([r_ins[i] for i in r.aliased] + r_land)
    return out


SLAB_ROWS = 192


def _pad_rows(a, rows=8):
    return jnp.pad(a, ((0, rows - a.shape[0]), (0, 0)))


def _pack_small(norm_grads, db_qkv, db_o, dsinks, db_sp, dln_g, dln_b, dw_sp, loss_part):
    parts = [
        jnp.concatenate(norm_grads, axis=0),
        _pad_rows(jnp.pad(db_qkv, ((0, 0), (0, 2 * D_MODEL - QKV_WIDTH))).reshape(2, D_MODEL)),
        _pad_rows(db_o),
        _pad_rows(jnp.pad(dsinks.reshape(1, N_Q_HEADS), ((0, 0), (0, D_MODEL - N_Q_HEADS)))),
        _pad_rows(db_sp.reshape(1, D_MODEL)),
        _pad_rows(jnp.concatenate([dln_g, dln_b, jnp.pad(loss_part[0:1], ((0, 0), (0, D_MODEL - LANES)))], axis=0)),
        dw_sp.reshape(SGU_CHUNK, D_MODEL),
    ]
    slab = jnp.concatenate(parts, axis=0)
    return jnp.pad(slab, ((0, SLAB_ROWS - slab.shape[0]), (0, 0))).reshape(N_CHIPS, SLAB_ROWS // N_CHIPS, D_MODEL)


def _unpack_small(slab, j):
    slab = slab.reshape(SLAB_ROWS, D_MODEL)
    norms = [slab[2 * i:2 * i + 2] for i in range(4)]
    db_qkv = slab[8:10].reshape(1, 2 * D_MODEL)[:, :QKV_WIDTH]
    db_o = slab[16:17]
    dsinks = slab[24:25, :N_Q_HEADS]
    db_sp = slab[32:33].reshape(SGU_GROUPS, SGU_CHUNK)
    width = D_MODEL // N_CHIPS
    dln_g = lax.dynamic_slice(slab[40:41], (0, j * width), (1, width))
    dln_b = lax.dynamic_slice(slab[41:42], (0, j * width), (1, width))
    dw_sp = slab[48:48 + SGU_CHUNK].reshape(SGU_GROUPS * SGU_CHUNK, SGU_CHUNK)
    return norms, db_qkv, db_o, dsinks, db_sp, dln_g, dln_b, dw_sp, slab[42, 0]


class _GradReduce:
    def __init__(self, c_arr, jc_arr, dest_shapes):
        self.c_arr, self.jc_arr, self.dest_shapes = c_arr, jc_arr, dest_shapes
        self.grad, self.sibling, self.pair, self.chips, self.dest = {}, {}, {}, {}, {}

    def exchange(self, tags):
        return exchange_rider([self.grad[t] for t in tags])

    def exchanged(self, tags, res):
        for t, r in zip(tags, res):
            self.sibling[t] = r
            self.pair[t] = pair_add(self.grad[t], r, self.c_arr, name=f"pair_add_{t}")

    def scatter(self, tags):
        return scatter_rider([self.pair[t] for t in tags])

    def scattered(self, tags, res, where):
        for t, r in zip(tags, res):
            name, lead = where[t]
            self.dest[name] = final_add(self.grad[t], self.sibling[t], r, self.jc_arr, dest_shape=self.dest_shapes[name],
                                        lead=lead, prev=self.dest.get(name), name=f"final_add_{t}")

    def broadcast(self, items):
        names = []
        for n, _ in items:
            if n not in names:
                names.append(n)
        return names, broadcast_rider([self.dest[n] for n in names], [(names.index(n), lead) for n, lead in items])

    def broadcasted(self, names, res):
        for n, r in zip(names, res):
            self.dest[n] = r


def kernel(x, norm_mix_pre, norm_mix_post, norm_ffn_pre, norm_ffn_post, attn_w_qkv, attn_b_qkv, attn_sinks, attn_w_o, attn_b_o, sgu_w_in, sgu_ln_g, sgu_ln_b, sgu_w_spatial, sgu_b_spatial, sgu_w_out, ffn_w_gate_up, ffn_w_down, loss_target, m_norm_mix_pre, m_norm_mix_post, m_norm_ffn_pre, m_norm_ffn_post, m_attn_w_qkv, m_attn_b_qkv, m_attn_sinks, m_attn_w_o, m_attn_b_o, m_sgu_w_in, m_sgu_ln_g, m_sgu_ln_b, m_sgu_w_spatial, m_sgu_b_spatial, m_sgu_w_out, m_ffn_w_gate_up, m_ffn_w_down, v_norm_mix_pre, v_norm_mix_post, v_norm_ffn_pre, v_norm_ffn_post, v_attn_w_qkv, v_attn_b_qkv, v_attn_sinks, v_attn_w_o, v_attn_b_o, v_sgu_w_in, v_sgu_ln_g, v_sgu_ln_b, v_sgu_w_spatial, v_sgu_b_spatial, v_sgu_w_out, v_ffn_w_gate_up, v_ffn_w_down):
    s = x.shape[1]
    x0 = x.reshape(s, D_MODEL)
    target = loss_target.reshape(s, D_MODEL)
    mx, my, mc = lax.axis_index("x"), lax.axis_index("y"), lax.axis_index("c")
    chip = 2 * mx + my
    chip_arr = jnp.reshape(chip, (1,)).astype(I32)
    c_arr = jnp.reshape(mc, (1,)).astype(I32)
    jc_arr = jnp.stack([chip, mc]).astype(I32)
    zero_bias = jnp.zeros((1, D_MODEL), F32)

    def gain(p, i):
        return p[i:i + 1]

    big = [attn_w_qkv, attn_w_o, sgu_w_in, sgu_w_out, ffn_w_gate_up, ffn_w_gate_up, ffn_w_down, ffn_w_down]
    layers = [0, 0, 0, 0, 0, 1, 0, 1]
    tags = ["qkv", "wo", "win", "wout", "wgu0", "wgu1", "wd0", "wd1"]
    full = {t: place_shard(w, l, chip_arr, BF16, name=f"place_{t}") for w, l, t in zip(big, layers, tags)
            if t not in ("wgu0", "wgu1")}
    ln_pack = _pad_rows(jnp.concatenate([sgu_ln_g, sgu_ln_b], axis=0), 16)[None]
    full["ln"] = place_shard(ln_pack, 0, chip_arr, F32, name="place_ln")

    def split(items):
        return [i if isinstance(i, str) else i[0] for i in items], [WHOLE if isinstance(i, str) else tuple(i[1:]) for i in items]

    def ici(*items):
        names, pieces = split(items)
        return gather_ici_rider([full[n] for n in names], pieces)

    def d2d(*items):
        names, pieces = split(items)
        return gather_d2d_rider([full[n] for n in names], pieces)

    def landed(items, res):
        for n, r in zip(split(items)[0], res):
            full[n] = r

    cos, sin = _rope_tables(s)
    sink_rows = jnp.broadcast_to(
        jnp.repeat(attn_sinks.reshape(N_KV_HEADS, GQA_GROUP), WINDOW, axis=1)[:, None, :], (N_KV_HEADS, 8, ROWS))
    w_sp = sgu_w_spatial.reshape(SGU_GROUPS, SGU_CHUNK, SGU_CHUNK)
    b_sp = jnp.broadcast_to(sgu_b_spatial.reshape(SGU_GROUPS, SGU_CHUNK)[:, :, None], (SGU_GROUPS, SGU_CHUNK, LANES))

    (h0, full["wgu0"]), (res,) = prenorm_and_place(x0, gain(norm_mix_pre, 0), ffn_w_gate_up, 0, chip_arr, name="prenorm_0",
                                                   riders=[ici("qkv", "ln")])
    landed(("qkv", "ln"), res)
    full["wgu1"], (res,) = place_shard(ffn_w_gate_up, 1, chip_arr, BF16, name="place_wgu1", riders=[d2d("qkv", "ln")])
    landed(("qkv", "ln"), res)
    ln_g = full["ln"][:, 0, :].reshape(1, D_MODEL)
    ln_b = full["ln"][:, 1, :].reshape(1, D_MODEL)

    def hosted(call, stages):
        outputs, results = call([{"ici": ici, "d2d": d2d}[kind](*items) for kind, items in stages])
        for (_, items), res in zip(stages, results):
            landed(items, res)
        return outputs

    qkv = hosted(lambda r: qkv_proj(h0, full["qkv"], attn_b_qkv, cos, sin, name="qkv_proj", riders=r),
                 [("ici", ("wo", ("wgu0", 0, 3, 8)))])
    o = hosted(lambda r: attn_fwd(qkv, sink_rows, name="attn_fwd", riders=r),
               [("d2d", ("wo",)), ("ici", (("wgu0", 3, 8, 8), ("wd0", 0, 2, 11)))])
    w_o = full["wo"].reshape(Q_WIDTH, D_MODEL)
    x1, h1, m0 = hosted(lambda r: proj_residual_norm(o, w_o, x0, attn_b_o, gain(norm_mix_post, 0), gain(norm_ffn_pre, 0),
                                                     name="attn_out_norm", riders=r),
                        [("d2d", ("wgu0",)), ("ici", (("wd0", 2, 11, 11),))])
    gu0, a0 = hosted(lambda r: ffn_up(h1, full["wgu0"], name="ffn_up_0", riders=r),
                     [("d2d", ("wd0",)), ("ici", ("win", "wout", ("wgu1", 0, 4, 8)))])
    w_d0 = full["wd0"].reshape(D_FF, D_MODEL)
    x2, h2, f0 = hosted(lambda r: proj_residual_norm(a0, w_d0, x1, zero_bias, gain(norm_ffn_post, 0), gain(norm_mix_pre, 1),
                                                     name="ffn_down_norm_0", riders=r),
                        [("d2d", ("win", "wout")), ("ici", (("wgu1", 4, 8, 8),))])
    w_in = full["win"]
    z, y = hosted(lambda r: sgu_in_fwd(h2, w_in, ln_g, ln_b, w_sp, b_sp, name="sgu_in_fwd", riders=r),
                  [("d2d", ("wgu1",)), ("ici", ("wd1",))])
    w_out = full["wout"].reshape(D_MODEL, D_MODEL)
    x3, h3, m1 = hosted(lambda r: proj_residual_norm(y, w_out, x2, zero_bias, gain(norm_mix_post, 1), gain(norm_ffn_pre, 1),
                                                     name="sgu_out_norm", riders=r),
                        [("d2d", ("wd1",))])
    w_qkv, w_gu0, w_gu1 = full["qkv"], full["wgu0"], full["wgu1"]
    w_d1 = full["wd1"].reshape(D_FF, D_MODEL)
    gu1, a1, dx4, df1, dg_fpost1, loss_part = ffn_fwd_loss_rows(
        h3, w_gu1, w_d1, x3, gain(norm_ffn_post, 1), target, name="ffn_fwd_loss_rows")

    red = _GradReduce(c_arr, jc_arr, {
        "qkv": attn_w_qkv.shape[1:], "wo": attn_w_o.shape[1:], "win": sgu_w_in.shape[1:], "wout": sgu_w_out.shape[1:],
        "wgu": ffn_w_gate_up.shape, "wd": ffn_w_down.shape, "slab": (N_CHIPS, SLAB_ROWS // N_CHIPS, D_MODEL)})
    where = {"qkv": ("qkv", None), "wo": ("wo", None), "win": ("win", None), "wout": ("wout", None), "wgu0": ("wgu", 0),
             "wgu1": ("wgu", 1), "wd0": ("wd", 0), "wd1": ("wd", 1), "small": ("slab", "chip")}

    dgu1, dx3, dm1, dg_fpre1, dg_mpost1, _ = ffn_bwd_rows(
        df1, w_d1, gu1, w_gu1, dx4, x3, gain(norm_ffn_pre, 1), m1, gain(norm_mix_post, 1), name="ffn_bwd_rows_1")
    red.grad["wd1"] = mm_tn(a1, df1, shard_major=False, tm=256, tn=D_MODEL, name="dw_down_1").reshape(
        N_CHIPS, D_FF // N_CHIPS, D_MODEL)
    red.grad["wgu1"], (res,) = mm_tn(h3, dgu1, shard_major=True, tm=512, tn=FF_HALF, name="dw_gate_up_1",
                                     riders=[red.exchange(["wd1"])])
    red.exchanged(["wd1"], res)
    dy, (res,) = mm_nt(dm1, w_out, out_dtype=F32, name="dy_sgu", riders=[red.exchange(["wgu1"])])
    red.exchanged(["wgu1"], res)
    red.grad["wout"] = mm_tn(y, dm1, shard_major=False, tm=512, tn=D_MODEL, name="dw_sgu_out").reshape(
        N_CHIPS, D_MODEL // N_CHIPS, D_MODEL)
    (dz, dw_sp, db_sp, dln_g, dln_b), (res_a, res_b) = sgu_bwd(
        z, dy, ln_g, ln_b, w_sp, b_sp, name="sgu_bwd", riders=[red.scatter(["wd1"]), red.exchange(["wout"])])
    red.scattered(["wd1"], res_a, where)
    red.exchanged(["wout"], res_b)
    names, rider = red.broadcast([("wd", 1)])
    red.grad["win"], (res_a, res_b) = mm_tn(h2, dz, shard_major=True, tm=D_MODEL, tn=2 * D_MODEL // N_CHIPS, name="dw_sgu_in",
                                            riders=[rider, red.scatter(["wout"])])
    red.broadcasted(names, res_a)
    red.scattered(["wout"], res_b, where)
    names, rider = red.broadcast([("wout", None)])
    (dx2, df0, dg_mpre1, dg_fpost0, _), (res_a, res_b) = dh_norm_bwd_pair(
        dz, w_in, dx3, x2, gain(norm_mix_pre, 1), f0, gain(norm_ffn_post, 0), name="dh_sgu_norm",
        riders=[red.exchange(["win"]), rider])
    red.exchanged(["win"], res_a)
    red.broadcasted(names, res_b)
    (dgu0, dx1, dm0, dg_fpre0, dg_mpost0, db_o), (res,) = ffn_bwd_rows(
        df0, w_d0, gu0, w_gu0, dx2, x1, gain(norm_ffn_pre, 0), m0, gain(norm_mix_post, 0), name="ffn_bwd_rows_0",
        riders=[red.scatter(["wgu1", "win"])])
    red.scattered(["wgu1", "win"], res, where)
    names, rider = red.broadcast([("wgu", 1), ("win", None)])
    dw_d0, (res,) = mm_tn(a0, df0, shard_major=False, tm=256, tn=D_MODEL, name="dw_down_0", riders=[rider])
    red.broadcasted(names, res)
    red.grad["wd0"] = dw_d0.reshape(N_CHIPS, D_FF // N_CHIPS, D_MODEL)
    do, (res,) = mm_nt(dm0, w_o, out_dtype=BF16, name="do_attn", riders=[red.exchange(["wd0"])])
    red.exchanged(["wd0"], res)
    red.grad["wgu0"], (res,) = mm_tn(h1, dgu0, shard_major=True, tm=512, tn=FF_HALF, name="dw_gate_up_0",
                                     riders=[red.scatter(["wd0"])])
    red.scattered(["wd0"], res, where)
    names, rider = red.broadcast([("wd", 0)])
    dw_o, (res_a, res_b) = mm_tn(o, dm0, shard_major=False, tm=512, tn=D_MODEL, name="dw_attn_out",
                                 riders=[red.exchange(["wgu0"]), rider])
    red.exchanged(["wgu0"], res_a)
    red.broadcasted(names, res_b)
    red.grad["wo"] = dw_o.reshape(N_CHIPS, Q_WIDTH // N_CHIPS, D_MODEL)
    (dq, dkc, dkp, dvc, dvp, dsink), (res_a, res_b) = attn_bwd(
        qkv, sink_rows, do, name="attn_bwd", riders=[red.scatter(["wgu0"]), red.exchange(["wo"])])
    red.scattered(["wgu0"], res_a, where)
    red.exchanged(["wo"], res_b)
    names, rider = red.broadcast([("wgu", 0)])
    (dqkv, db_qkv), (res_a, res_b) = rope_bwd(dq, dkc, dkp, dvc, dvp, cos, sin, name="rope_bwd",
                                              riders=[rider, red.scatter(["wo"])])
    red.broadcasted(names, res_a)
    red.scattered(["wo"], res_b, where)
    names, rider = red.broadcast([("wo", None)])
    red.grad["qkv"], (res,) = mm_tn(h0, dqkv, shard_major=True, tm=D_MODEL, tn=QKV_WIDTH // N_CHIPS, name="dw_qkv",
                                    riders=[rider])
    red.broadcasted(names, res)
    grad_x, dg_mpre0 = dh_norm_bwd_last(dqkv, w_qkv, dx1, x0, gain(norm_mix_pre, 0), name="dh_attn_norm_in")

    norm_grads = [jnp.concatenate(p, axis=0) for p in
                  ((dg_mpre0, dg_mpre1), (dg_mpost0, dg_mpost1), (dg_fpre0, dg_fpre1), (dg_fpost0, dg_fpost1))]
    red.grad["small"] = _pack_small(norm_grads, db_qkv, db_o, dsink[:, :, 0, 0], db_sp[:, :, 0], dln_g, dln_b, dw_sp,
                                    loss_part)
    def big_update(w, g, m, v, tag, after=None):
        return adamw(w, g.reshape(w.shape), m, v, name=f"adamw_{tag}", after=after)

    (res,) = comm_call([red.exchange(["qkv", "small"])], name="tail_1")
    red.exchanged(["qkv", "small"], res)
    state, token = comm_start([red.scatter(["qkv", "small"])], name="tail_2_start")
    upd_wgu = big_update(ffn_w_gate_up, red.dest["wgu"], m_ffn_w_gate_up, v_ffn_w_gate_up, "wgu", after=token)
    (res,) = comm_wait(state, upd_wgu[1], name="tail_2_wait")
    red.scattered(["qkv", "small"], res, where)
    names, rider = red.broadcast([("qkv", None)])
    state, token = comm_start([rider, allcast_rider(red.dest["slab"])], name="tail_3_start")
    upd_wd = big_update(ffn_w_down, red.dest["wd"], m_ffn_w_down, v_ffn_w_down, "wd", after=token)
    (res_a,), (slab_full,) = comm_wait(state, upd_wd[1], name="tail_3_wait")
    red.broadcasted(names, [res_a])
    g_qkv, g_wo, g_win, g_wout = (red.dest[n] for n in ("qkv", "wo", "win", "wout"))
    g_norms, g_bqkv, g_bo, g_sinks, g_bsp, g_lng, g_lnb, g_wsp, loss = _unpack_small(slab_full, chip)

    upd = {
        "attn_w_qkv": big_update(attn_w_qkv, g_qkv, m_attn_w_qkv, v_attn_w_qkv, "qkv"),
        "attn_w_o": big_update(attn_w_o, g_wo, m_attn_w_o, v_attn_w_o, "wo"),
        "sgu_w_in": big_update(sgu_w_in, g_win, m_sgu_w_in, v_sgu_w_in, "win"),
        "sgu_w_out": big_update(sgu_w_out, g_wout, m_sgu_w_out, v_sgu_w_out, "wout"),
        "ffn_w_gate_up": upd_wgu,
        "ffn_w_down": upd_wd,
    }
    small_names = ["norm_mix_pre", "norm_mix_post", "norm_ffn_pre", "norm_ffn_post", "attn_b_qkv", "attn_sinks", "attn_b_o",
                   "sgu_ln_g", "sgu_ln_b", "sgu_w_spatial", "sgu_b_spatial"]
    small_w = [norm_mix_pre, norm_mix_post, norm_ffn_pre, norm_ffn_post, attn_b_qkv, attn_sinks, attn_b_o, sgu_ln_g, sgu_ln_b,
               sgu_w_spatial, sgu_b_spatial]
    small_m = [m_norm_mix_pre, m_norm_mix_post, m_norm_ffn_pre, m_norm_ffn_post, m_attn_b_qkv, m_attn_sinks, m_attn_b_o,
               m_sgu_ln_g, m_sgu_ln_b, m_sgu_w_spatial, m_sgu_b_spatial]
    small_v = [v_norm_mix_pre, v_norm_mix_post, v_norm_ffn_pre, v_norm_ffn_post, v_attn_b_qkv, v_attn_sinks, v_attn_b_o,
               v_sgu_ln_g, v_sgu_ln_b, v_sgu_w_spatial, v_sgu_b_spatial]
    small_g = g_norms + [g_bqkv, g_sinks, g_bo, g_lng, g_lnb, g_wsp, g_bsp]

    def flat2(a):
        return a.reshape(-1, a.shape[-1])

    res = adamw_small([flat2(a) for a in small_w], [flat2(a) for a in small_g], [flat2(a) for a in small_m],
                      [flat2(a) for a in small_v], name="adamw_small")
    for i, nm in enumerate(small_names):
        upd[nm] = tuple(r[i].reshape(small_w[i].shape) for r in res)

    order = ["norm_mix_pre", "norm_mix_post", "norm_ffn_pre", "norm_ffn_post", "attn_w_qkv", "attn_b_qkv", "attn_sinks",
             "attn_w_o", "attn_b_o", "sgu_w_in", "sgu_ln_g", "sgu_ln_b", "sgu_w_spatial", "sgu_b_spatial", "sgu_w_out",
             "ffn_w_gate_up", "ffn_w_down"]
    outs = [loss, grad_x.reshape(1, s, D_MODEL)]
    for part in range(4):
        outs += [upd[nm][part] for nm in order]
    return tuple(outs)
```

```python
import types

import numpy as np
import jax
import jax.numpy as jnp
from jax import lax
from jax.experimental import pallas as pl
from jax.experimental.pallas import tpu as pltpu

F32 = jnp.float32
BF16 = jnp.bfloat16
I32 = jnp.int32

D_MODEL = 1024
HEAD_DIM = 64
N_Q_HEADS = 16
N_KV_HEADS = 4
GQA_GROUP = 4
WINDOW = 128
Q_WIDTH = 1024
KV_WIDTH = 256
QKV_WIDTH = 1536
ROPE_THETA = 10000.0
SGU_GROUPS = 8
SGU_CHUNK = 128
D_FF = 2816
FF_HALF = D_FF // 2
EPS = 1e-6
N_CHIPS = 4
LANES = 128

ADAM_LR = 0.001
ADAM_B1 = 0.9
ADAM_B2 = 0.999
ADAM_EPS = 1e-08
ADAM_WD = 0.01
ADAM_STEP = 10

VMEM_LIMIT = 52 * 1024 * 1024
BIG_VMEM_LIMIT = 62 * 1024 * 1024
SUB_ROWS = 256
MESH = pl.DeviceIdType.MESH
NEG = -1e30
NT_DIMS = (((1,), (1,)), ((), ()))
TN_DIMS = (((0,), (0,)), ((), ()))
NN_DIMS = (((1,), (0,)), ((), ()))
ANY = pl.BlockSpec(memory_space=pl.ANY)


def _row_tile(s, want):
    return want if s % want == 0 else s


PEER_KINDS = ("sibling", "chips", "sibling+chips", "everyone")


def _peer_kind(riders):
    kinds = {r.peers for r in riders}
    if not kinds:
        return None
    if "everyone" in kinds:
        return "everyone"
    return "sibling+chips" if len(kinds) == 2 else kinds.pop()


def _peer_barrier(kind):
    x, y, c = _place()
    chips = [(*_partner(x, y, k), c) for k in (1, 2, 3)]
    peers = {"sibling": [(x, y, 1 - c)], "chips": chips, "sibling+chips": [(x, y, 1 - c)] + chips,
             "everyone": [(x, y, 1 - c)] + chips + [(px, py, 1 - c) for px, py, _ in chips]}[kind]
    barrier = pltpu.get_barrier_semaphore()
    for dev in peers:
        pl.semaphore_signal(barrier, inc=1, device_id=dev, device_id_type=MESH)
    pl.semaphore_wait(barrier, len(peers))


def _call(body, *, name, grid=(), in_specs=(), out_specs=(), out_shape=(), scratch_shapes=(), operands=(), prefetch=(),
          aliases=None, riders=(), sem=None, vmem_limit=VMEM_LIMIT):
    n_pre, n_in, n_out, n_scr = len(prefetch), len(operands), len(out_shape), len(scratch_shapes)
    in_specs, out_specs, out_shape = list(in_specs), list(out_specs), list(out_shape)
    operands, scratch_shapes = list(operands), list(scratch_shapes)
    io_alias = {n_pre + i: o for i, o in (aliases or {}).items()}
    for r in riders:
        base_in, base_out = n_pre + len(operands), len(out_shape)
        operands += list(r.inputs)
        in_specs += [ANY] * len(r.inputs)
        for pos, i in enumerate(r.aliased):
            io_alias[base_in + i] = base_out + pos
            out_shape.append(jax.ShapeDtypeStruct(r.inputs[i].shape, r.inputs[i].dtype))
        out_shape += list(r.fresh)
        out_specs += [ANY] * (len(r.aliased) + len(r.fresh))
        scratch_shapes += [pltpu.SemaphoreType.DMA((r.nsem,)), pltpu.SemaphoreType.DMA((r.nsem,))]

    def wrapped(*refs):
        pre, p = refs[:n_pre], n_pre
        core_in, p = refs[p:p + n_in], p + n_in
        r_in = []
        for r in riders:
            r_in.append(refs[p:p + len(r.inputs)])
            p += len(r.inputs)
        core_out, p = refs[p:p + n_out], p + n_out
        r_out = []
        for r in riders:
            k = len(r.aliased) + len(r.fresh)
            r_out.append(refs[p:p + k])
            p += k
        core_scr, p = refs[p:p + n_scr], p + n_scr
        r_sem = [refs[p + 2 * i:p + 2 * i + 2] for i in range(len(riders))]

        def edge(at_last, fns):
            def run():
                if not at_last:
                    _peer_barrier(peer_kind)
                for i, r in enumerate(riders):
                    getattr(r, fns)(r_in[i], r_out[i], r_sem[i][0], r_sem[i][1])
            if not riders:
                return
            if not grid:
                run()
                return
            cond = None
            for d, n in enumerate(grid):
                c = pl.program_id(d) == (n - 1 if at_last else 0)
                cond = c if cond is None else jnp.logical_and(cond, c)
            pl.when(cond)(run)

        edge(False, "start")
        if body is not None:
            body(*pre, *core_in, *core_out, *core_scr)
        edge(True, "finish")

    if sem is None or riders:
        sem = ("arbitrary",) * len(grid)
    kwargs = dict(out_shape=out_shape, input_output_aliases=io_alias, name=name)
    peer_kind = _peer_kind(riders)
    collective = {} if peer_kind is None else {"collective_id": PEER_KINDS.index(peer_kind)}
    if grid:
        kwargs["compiler_params"] = pltpu.CompilerParams(dimension_semantics=sem, vmem_limit_bytes=vmem_limit, **collective)
    elif collective:
        kwargs["compiler_params"] = pltpu.CompilerParams(**collective)
    if n_pre:
        kwargs["grid_spec"] = pltpu.PrefetchScalarGridSpec(
            num_scalar_prefetch=n_pre, grid=grid, in_specs=in_specs, out_specs=out_specs, scratch_shapes=scratch_shapes)
    else:
        kwargs.update(grid=grid, in_specs=in_specs, out_specs=out_specs, scratch_shapes=scratch_shapes)
    res = pl.pallas_call(wrapped, **kwargs)(*prefetch, *operands)
    core, rest, rider_res = list(res[:n_out]), list(res[n_out:]), []
    for r in riders:
        k = len(r.aliased) + len(r.fresh)
        rider_res.append(rest[:k])
        rest = rest[k:]
    return core, rider_res


def _mm_call(*, grid, in_specs, out_spec, out_shape, dims, nk, kaxis, acc_shape, name, operands, riders=()):
    out_dtype = out_shape.dtype

    def body(a_ref, b_ref, o_ref, *scratch):
        p = lax.dot_general(a_ref[...].astype(BF16), b_ref[...].astype(BF16), dims, preferred_element_type=F32)
        if nk == 1:
            o_ref[...] = p.astype(out_dtype)
        else:
            acc = scratch[0]
            kk = pl.program_id(kaxis)

            @pl.when(kk == 0)
            def _():
                acc[...] = p

            @pl.when(kk > 0)
            def _():
                acc[...] += p

            @pl.when(kk == nk - 1)
            def _():
                o_ref[...] = acc[...].astype(out_dtype)

    sem = ["parallel"] * len(grid)
    if nk > 1:
        sem[kaxis] = "arbitrary"
    (out,), rider_res = _call(
        body, grid=grid, in_specs=in_specs, out_specs=[out_spec], out_shape=[out_shape],
        scratch_shapes=[pltpu.VMEM(acc_shape, F32)] if nk > 1 else [], operands=operands, name=name, riders=riders,
        sem=tuple(sem))
    return (out, rider_res) if riders else out


def mm_nt(a, w, *, out_dtype, name, tm=1024, riders=()):
    m, n = a.shape
    kout = w.shape[0]
    tm = _row_tile(m, tm)
    return _mm_call(grid=(m // tm,),
                    in_specs=[pl.BlockSpec((tm, n), lambda i: (i, 0)), pl.BlockSpec((kout, n), lambda i: (0, 0))],
                    out_spec=pl.BlockSpec((tm, kout), lambda i: (i, 0)),
                    out_shape=jax.ShapeDtypeStruct((m, kout), out_dtype), dims=NT_DIMS, nk=1, kaxis=0,
                    acc_shape=None, name=name, operands=(a, w), riders=riders)


def mm_tn(a, b, *, shard_major, name, tm, tn, tk=None, out_dtype=BF16, riders=()):
    s, m = a.shape
    tk = s if tk is None else _row_tile(s, tk)
    if b.ndim == 3:
        n = 2 * b.shape[2]
        b_spec = pl.BlockSpec((None, tk, tn), lambda j, i, kk: (j // 2, kk, j % 2))
    else:
        n = b.shape[1]
        b_spec = pl.BlockSpec((tk, tn), lambda j, i, kk: (kk, j))
    if shard_major:
        assert tn == n // N_CHIPS
        o_spec = pl.BlockSpec((None, tm, tn), lambda j, i, kk: (j, i, 0))
        o_shape = jax.ShapeDtypeStruct((N_CHIPS, m, tn), out_dtype)
    else:
        o_spec = pl.BlockSpec((tm, tn), lambda j, i, kk: (i, j))
        o_shape = jax.ShapeDtypeStruct((m, n), out_dtype)
    return _mm_call(grid=(n // tn, m // tm, s // tk),
                    in_specs=[pl.BlockSpec((tk, tm), lambda j, i, kk: (kk, i)), b_spec], out_spec=o_spec,
                    out_shape=o_shape, dims=TN_DIMS, nk=s // tk, kaxis=2, acc_shape=(tm, tn), name=name, operands=(a, b),
                    riders=riders)


def _rstd(x):
    return lax.rsqrt(jnp.mean(x * x, axis=-1, keepdims=True) + EPS)


def _rms_bwd(dy, x, g):
    r = _rstd(x)
    xhat = x * r
    gy = dy * g
    dx = r * (gy - xhat * jnp.mean(gy * xhat, axis=-1, keepdims=True))
    return dx, jnp.sum(dy * xhat, axis=0, keepdims=True)


def _accum(ref, val, first):
    @pl.when(first)
    def _():
        ref[...] = val

    @pl.when(jnp.logical_not(first))
    def _():
        ref[...] += val


def _row_spec(tm, width):
    return pl.BlockSpec((tm, width), lambda i: (i, 0))


def _vec_spec(width):
    return pl.BlockSpec((1, width), lambda i: (0, 0))


def _ret(core, rider_res, riders):
    core = core[0] if len(core) == 1 else core
    return (core, rider_res) if riders else core


def prenorm_and_place(x, g, w, layer, chip_arr, *, name, tm=256, riders=()):
    s = x.shape[0]
    tm = _row_tile(s, tm)
    steps = s // tm
    _, r, c = w.shape
    tr = r // steps
    assert tr * steps == r and tr % 16 == 0

    def body(chip_ref, x_ref, g_ref, w_ref, h_ref, o_ref):
        xv = x_ref[...]
        h_ref[...] = (xv * _rstd(xv) * g_ref[...]).astype(BF16)
        o_ref[...] = w_ref[...].astype(BF16)

    core, rr = _call(
        body, grid=(steps,), prefetch=(chip_arr,),
        in_specs=[pl.BlockSpec((tm, D_MODEL), lambda i, chip: (i, 0)), pl.BlockSpec((1, D_MODEL), lambda i, chip: (0, 0)),
                  pl.BlockSpec((None, tr, c), lambda i, chip: (layer, i, 0))],
        out_specs=[pl.BlockSpec((tm, D_MODEL), lambda i, chip: (i, 0)), pl.BlockSpec((None, tr, c), lambda i, chip: (chip[0], i, 0))],
        out_shape=[jax.ShapeDtypeStruct((s, D_MODEL), BF16), jax.ShapeDtypeStruct((N_CHIPS, r, c), BF16)],
        operands=(x, g, w), sem=("parallel",), name=name, riders=riders)
    return _ret(core, rr, riders)


def proj_residual_norm(a, w, x, bias, g_post, g_next, *, name, tm=512, sub=256, riders=()):
    s, k = a.shape
    tm = _row_tile(s, tm)
    sub = min(sub, tm)

    def body(a_ref, w_ref, x_ref, b_ref, gp_ref, gn_ref, xo_ref, h_ref, m_ref):
        for t in range(tm // sub):
            rows = slice(t * sub, (t + 1) * sub)
            mv = jnp.dot(a_ref[rows, :], w_ref[...], preferred_element_type=F32) + b_ref[...]
            m_ref[rows, :] = mv.astype(BF16)
            xn = x_ref[rows, :] + mv * _rstd(mv) * gp_ref[...]
            xo_ref[rows, :] = xn
            h_ref[rows, :] = (xn * _rstd(xn) * gn_ref[...]).astype(BF16)

    row, vec = _row_spec(tm, D_MODEL), _vec_spec(D_MODEL)
    core, rr = _call(
        body, grid=(s // tm,),
        in_specs=[_row_spec(tm, k), pl.BlockSpec((k, D_MODEL), lambda i: (0, 0)), row, vec, vec, vec], out_specs=[row, row, row],
        out_shape=[jax.ShapeDtypeStruct((s, D_MODEL), F32), jax.ShapeDtypeStruct((s, D_MODEL), BF16),
                   jax.ShapeDtypeStruct((s, D_MODEL), BF16)],
        operands=(a, w, x, bias, g_post, g_next), sem=("parallel",), name=name, riders=riders)
    return _ret(core, rr, riders)


def ffn_fwd_loss_rows(h, w_gu, w_d, x, g_post, target, *, name, tm=512, riders=()):
    s = x.shape[0]
    tm = _row_tile(s, tm)

    def body(h_ref, w0, w1, w2, w3, wd_ref, x_ref, g_ref, t_ref, d_ref, a_ref, dx_ref, df_ref, dg_ref, loss_ref):
        first = pl.program_id(0) == 0
        halves = [slice(half * FF_HALF, (half + 1) * FF_HALF) for half in (0, 1)]
        gain = g_ref[...]
        sub = min(SUB_ROWS, tm)
        sums = None
        for t in range(tm // sub):
            rows = slice(t * sub, (t + 1) * sub)
            hv = h_ref[rows, :]
            fv = None
            for cols, (wg_ref, wu_ref) in zip(halves, ((w0, w2), (w1, w3))):
                g = jnp.dot(hv, wg_ref[...], preferred_element_type=F32)
                u = jnp.dot(hv, wu_ref[...], preferred_element_type=F32)
                sig = _sigmoid(g)
                silu = g * sig
                d_ref[0, rows, cols] = (u * (sig + silu * (1.0 - sig))).astype(BF16)
                d_ref[1, rows, cols] = silu.astype(BF16)
                act = (silu * u).astype(BF16)
                a_ref[rows, cols] = act
                p = jnp.dot(act, wd_ref[cols, :], preferred_element_type=F32)
                fv = p if fv is None else fv + p
            err = x_ref[rows, :] + fv * _rstd(fv) * gain - t_ref[rows, :]
            dx = err * (1.0 / D_MODEL)
            dx_ref[rows, :] = dx
            df, dg = _rms_bwd(dx, fv, gain)
            df_ref[rows, :] = df.astype(BF16)
            part = (dg, jnp.sum(jnp.sum(err * err, axis=-1, keepdims=True), axis=0, keepdims=True) * (0.5 / D_MODEL))
            sums = part if sums is None else tuple(a + b for a, b in zip(sums, part))
        _accum(dg_ref, sums[0], first)
        _accum(loss_ref, jnp.broadcast_to(sums[1], (8, LANES)), first)

    def resident(shape, index):
        return pl.BlockSpec(shape, index, pipeline_mode=pl.Buffered(1))

    row, vec = _row_spec(tm, D_MODEL), _vec_spec(D_MODEL)
    shards = [resident((None, D_MODEL, FF_HALF), (lambda j: (lambda i: (j, 0, 0)))(j)) for j in range(N_CHIPS)]
    core, rr = _call(
        body, grid=(s // tm,),
        in_specs=[row] + shards + [resident((D_FF, D_MODEL), lambda i: (0, 0)), row, vec, row],
        out_specs=[pl.BlockSpec((2, tm, D_FF), lambda i: (0, i, 0)), _row_spec(tm, D_FF), row, row, vec,
                   pl.BlockSpec((8, LANES), lambda i: (0, 0))],
        out_shape=[jax.ShapeDtypeStruct((2, s, D_FF), BF16), jax.ShapeDtypeStruct((s, D_FF), BF16),
                   jax.ShapeDtypeStruct((s, D_MODEL), F32), jax.ShapeDtypeStruct((s, D_MODEL), BF16),
                   jax.ShapeDtypeStruct((1, D_MODEL), F32), jax.ShapeDtypeStruct((8, LANES), F32)],
        operands=(h, w_gu, w_gu, w_gu, w_gu, w_d, x, g_post, target), name=name, riders=riders, vmem_limit=BIG_VMEM_LIMIT)
    return _ret(core, rr, riders)


def dh_norm_bwd_pair(a, w, dres, x, g_pre, m, g_post, *, name, tm=512, sub=256, riders=()):
    _, kout, ns = w.shape
    planes = a.ndim == 3
    s = x.shape[0]
    tm = _row_tile(s, tm)
    sub = min(sub, tm)
    a_spec = pl.BlockSpec((2, tm, 2 * ns), lambda i: (0, i, 0)) if planes else pl.BlockSpec((tm, N_CHIPS * ns), lambda i: (i, 0))

    def body(a_ref, w0, w1, w2, w3, dres_ref, x_ref, gpre_ref, m_ref, gpost_ref, dx_ref, dm_ref, dgpre_ref, dgpost_ref, db_ref):
        first = pl.program_id(0) == 0
        sums = None
        for t in range(tm // sub):
            rows = slice(t * sub, (t + 1) * sub)
            dh = None
            for j, w_ref in enumerate((w0, w1, w2, w3)):
                a_j = a_ref[j // 2, rows, (j % 2) * ns:(j % 2 + 1) * ns] if planes else a_ref[rows, j * ns:(j + 1) * ns]
                p = lax.dot_general(a_j, w_ref[...], NT_DIMS, preferred_element_type=F32)
                dh = p if dh is None else dh + p
            d1, dgpre = _rms_bwd(dh, x_ref[rows, :], gpre_ref[...])
            dx = dres_ref[rows, :] + d1
            dx_ref[rows, :] = dx
            dm, dgpost = _rms_bwd(dx, m_ref[rows, :].astype(F32), gpost_ref[...])
            dm_ref[rows, :] = dm.astype(BF16)
            part = (dgpre, dgpost, jnp.sum(dm, axis=0, keepdims=True))
            sums = part if sums is None else tuple(u + v for u, v in zip(sums, part))
        _accum(dgpre_ref, sums[0], first)
        _accum(dgpost_ref, sums[1], first)
        _accum(db_ref, sums[2], first)

    def shard(j):
        return pl.BlockSpec((None, kout, ns), lambda i: (j, 0, 0))

    row, vec = _row_spec(tm, D_MODEL), _vec_spec(D_MODEL)
    vshape = jax.ShapeDtypeStruct((1, D_MODEL), F32)
    core, rr = _call(
        body, grid=(s // tm,), in_specs=[a_spec] + [shard(j) for j in range(N_CHIPS)] + [row, row, vec, row, vec],
        out_specs=[row, row, vec, vec, vec],
        out_shape=[jax.ShapeDtypeStruct((s, D_MODEL), F32), jax.ShapeDtypeStruct((s, D_MODEL), BF16), vshape, vshape, vshape],
        operands=(a, w, w, w, w, dres, x, g_pre, m, g_post), name=name, riders=riders)
    return _ret(core, rr, riders)


def ffn_bwd_rows(df, w_d, d_planes, w_gu, dres, x, g_pre, m, g_post, *, name, tm=512, riders=()):
    s = x.shape[0]
    tm = _row_tile(s, tm)

    def body(df_ref, wd_ref, d_ref, w0, w1, w2, w3, dres_ref, x_ref, gpre_ref, m_ref, gpost_ref,
             o_ref, dx_ref, dm_ref, dgpre_ref, dgpost_ref, db_ref):
        first = pl.program_id(0) == 0
        halves = [slice(half * FF_HALF, (half + 1) * FF_HALF) for half in (0, 1)]
        sub = min(SUB_ROWS, tm)
        sums = None
        for t in range(tm // sub):
            rows = slice(t * sub, (t + 1) * sub)
            dfv = df_ref[rows, :]
            dh = None
            for cols, (wg_ref, wu_ref) in zip(halves, ((w0, w2), (w1, w3))):
                da = lax.dot_general(dfv, wd_ref[cols, :], NT_DIMS, preferred_element_type=F32)
                dg = (da * d_ref[0, rows, cols].astype(F32)).astype(BF16)
                du = (da * d_ref[1, rows, cols].astype(F32)).astype(BF16)
                o_ref[0, rows, cols] = dg
                o_ref[1, rows, cols] = du
                p = lax.dot_general(dg, wg_ref[...], NT_DIMS, preferred_element_type=F32)
                p += lax.dot_general(du, wu_ref[...], NT_DIMS, preferred_element_type=F32)
                dh = p if dh is None else dh + p
            d1, dgpre = _rms_bwd(dh, x_ref[rows, :], gpre_ref[...])
            dx = dres_ref[rows, :] + d1
            dx_ref[rows, :] = dx
            dm, dgpost = _rms_bwd(dx, m_ref[rows, :].astype(F32), gpost_ref[...])
            dm_ref[rows, :] = dm.astype(BF16)
            part = (dgpre, dgpost, jnp.sum(dm, axis=0, keepdims=True))
            sums = part if sums is None else tuple(a + b for a, b in zip(sums, part))
        _accum(dgpre_ref, sums[0], first)
        _accum(dgpost_ref, sums[1], first)
        _accum(db_ref, sums[2], first)

    def resident(shape, index):
        return pl.BlockSpec(shape, index, pipeline_mode=pl.Buffered(1))

    planes = pl.BlockSpec((2, tm, D_FF), lambda i: (0, i, 0))
    row, vec = _row_spec(tm, D_MODEL), _vec_spec(D_MODEL)
    vshape = jax.ShapeDtypeStruct((1, D_MODEL), F32)
    shards = [resident((None, D_MODEL, FF_HALF), (lambda j: (lambda i: (j, 0, 0)))(j)) for j in range(N_CHIPS)]
    core, rr = _call(
        body, grid=(s // tm,),
        in_specs=[row, resident((D_FF, D_MODEL), lambda i: (0, 0)), planes] + shards + [row, row, vec, row, vec],
        out_specs=[planes, row, row, vec, vec, vec],
        out_shape=[jax.ShapeDtypeStruct((2, s, D_FF), BF16), jax.ShapeDtypeStruct((s, D_MODEL), F32),
                   jax.ShapeDtypeStruct((s, D_MODEL), BF16), vshape, vshape, vshape],
        operands=(df, w_d, d_planes, w_gu, w_gu, w_gu, w_gu, dres, x, g_pre, m, g_post), name=name, riders=riders,
        vmem_limit=BIG_VMEM_LIMIT)
    return _ret(core, rr, riders)


def dh_norm_bwd_last(a, w, dres, x, g_pre, *, name, tm=512, sub=256):
    _, kout, ns = w.shape
    s = x.shape[0]
    tm = _row_tile(s, tm)
    sub = min(sub, tm)

    def body(a_ref, w0, w1, w2, w3, dres_ref, x_ref, g_ref, dx_ref, dg_ref):
        total = None
        for t in range(tm // sub):
            rows = slice(t * sub, (t + 1) * sub)
            dh = None
            for j, w_ref in enumerate((w0, w1, w2, w3)):
                p = lax.dot_general(a_ref[rows, j * ns:(j + 1) * ns], w_ref[...], NT_DIMS, preferred_element_type=F32)
                dh = p if dh is None else dh + p
            d1, dg = _rms_bwd(dh, x_ref[rows, :], g_ref[...])
            dx_ref[rows, :] = dres_ref[rows, :] + d1
            total = dg if total is None else total + dg
        _accum(dg_ref, total, pl.program_id(0) == 0)

    def shard(j):
        return pl.BlockSpec((None, kout, ns), lambda i: (j, 0, 0))

    row, vec = _row_spec(tm, D_MODEL), _vec_spec(D_MODEL)
    (dx, dg), _ = _call(
        body, grid=(s // tm,), in_specs=[_row_spec(tm, N_CHIPS * ns)] + [shard(j) for j in range(N_CHIPS)] + [row, row, vec],
        out_specs=[row, vec], out_shape=[jax.ShapeDtypeStruct((s, D_MODEL), F32), jax.ShapeDtypeStruct((1, D_MODEL), F32)],
        operands=(a, w, w, w, w, dres, x, g_pre), name=name)
    return dx, dg


def _rope_tables(s):
    half = HEAD_DIM // 2
    inv_freq = np.float32(ROPE_THETA) ** (-(np.arange(half, dtype=np.float32) * np.float32(2.0)) / np.float32(HEAD_DIM))
    ang = np.arange(s, dtype=np.float32)[:, None] * inv_freq[None, :]
    cos, sin = np.cos(ang).astype(np.float32), np.sin(ang).astype(np.float32)
    return jnp.asarray(np.tile(cos, (1, 4))), jnp.asarray(np.concatenate([-sin, sin, -sin, sin], axis=1))


def _swap_halves(x):
    lane = lax.broadcasted_iota(I32, x.shape, 1)
    return jnp.where((lane & (HEAD_DIM - 1)) < HEAD_DIM // 2, pltpu.roll(x, LANES - 32, 1), pltpu.roll(x, 32, 1))


N_ROPE_BLOCKS = (Q_WIDTH + KV_WIDTH) // LANES


def qkv_proj(h, w, bias, cos, sin, *, name, tm=1024, riders=()):
    s, k = h.shape
    ns = w.shape[2]
    tm = _row_tile(s, tm)

    def body(h_ref, w_ref, b_ref, c_ref, s_ref, o_ref):
        j = pl.program_id(0)
        sub = min(256, tm)
        for t in range(tm // sub):
            rows = slice(t * sub, (t + 1) * sub)
            p = jnp.dot(h_ref[rows, :], w_ref[...], preferred_element_type=F32) + b_ref[...]
            cosv, sinv = c_ref[rows, :], s_ref[rows, :]
            for blk in range(ns // LANES):
                xb = p[:, blk * LANES:(blk + 1) * LANES]
                roped = xb * cosv + _swap_halves(xb) * sinv
                is_qk = j * (ns // LANES) + blk < N_ROPE_BLOCKS
                o_ref[rows, blk * LANES:(blk + 1) * LANES] = jnp.where(is_qk, roped, xb).astype(BF16)

    core, rr = _call(
        body, grid=(N_CHIPS, s // tm),
        in_specs=[pl.BlockSpec((tm, k), lambda j, i: (i, 0)), pl.BlockSpec((None, k, ns), lambda j, i: (j, 0, 0)),
                  pl.BlockSpec((1, ns), lambda j, i: (0, j)), pl.BlockSpec((tm, LANES), lambda j, i: (i, 0)),
                  pl.BlockSpec((tm, LANES), lambda j, i: (i, 0))],
        out_specs=[pl.BlockSpec((tm, ns), lambda j, i: (i, j))], out_shape=[jax.ShapeDtypeStruct((s, N_CHIPS * ns), BF16)],
        operands=(h, w, bias, cos, sin), sem=("parallel", "parallel"), name=name, riders=riders)
    return _ret(core, rr, riders)


def rope_bwd(dq, dkc, dkp, dvc, dvp, cos, sin, *, name, riders=()):
    s = dq.shape[0]
    tm = 2 * WINDOW if s % (2 * WINDOW) == 0 else WINDOW
    nb = s // tm

    def body(dq_ref, dkc_ref, dkp_ref, dkp_next_ref, dvc_ref, dvp_ref, dvp_next_ref, c_ref, s_ref, o_ref, db_ref):
        i = pl.program_id(0)
        has_next = (i < nb - 1).astype(F32)
        cosv, sinv = c_ref[...], s_ref[...]

        def shifted(ref, next_ref, cols):
            last = has_next * next_ref[:WINDOW, cols].astype(F32)
            return last if tm == WINDOW else jnp.concatenate([ref[WINDOW:, cols].astype(F32), last], axis=0)

        parts = []
        for blk in range(QKV_WIDTH // LANES):
            if blk < Q_WIDTH // LANES:
                g = dq_ref[:, blk * LANES:(blk + 1) * LANES].astype(F32)
            else:
                own, prv, nxt = (dkc_ref, dkp_ref, dkp_next_ref) if blk < N_ROPE_BLOCKS else (dvc_ref, dvp_ref, dvp_next_ref)
                cols = slice((blk % 2) * LANES, (blk % 2 + 1) * LANES)
                g = own[:, cols].astype(F32) + shifted(prv, nxt, cols)
            if blk < N_ROPE_BLOCKS:
                g = g * cosv + _swap_halves(g * sinv)
            o_ref[:, blk * LANES:(blk + 1) * LANES] = g.astype(BF16)
            parts.append(jnp.sum(g, axis=0, keepdims=True))
        sums = jnp.concatenate(parts, axis=1)
        _accum(db_ref, sums, i == 0)

    own_spec = _row_spec(tm, KV_WIDTH)
    next_spec = pl.BlockSpec((tm, KV_WIDTH), lambda i: (jnp.minimum(i + 1, nb - 1), 0))
    core, rr = _call(
        body, grid=(nb,),
        in_specs=[_row_spec(tm, Q_WIDTH), own_spec, own_spec, next_spec, own_spec, own_spec, next_spec,
                  _row_spec(tm, LANES), _row_spec(tm, LANES)],
        out_specs=[_row_spec(tm, QKV_WIDTH), _vec_spec(QKV_WIDTH)],
        out_shape=[jax.ShapeDtypeStruct((s, QKV_WIDTH), BF16), jax.ShapeDtypeStruct((1, QKV_WIDTH), F32)],
        operands=(dq, dkc, dkp, dkp, dvc, dvp, dvp, cos, sin), name=name, riders=riders)
    return _ret(core, rr, riders)


ROWS = GQA_GROUP * WINDOW


def _prev_slots():
    kpos = lax.broadcasted_iota(I32, (WINDOW, ROWS), 0)
    qpos = lax.broadcasted_iota(I32, (WINDOW, ROWS), 1) & (WINDOW - 1)
    return kpos > qpos


def _head_cols(ref, head):
    return ref[:, head * HEAD_DIM:(head + 1) * HEAD_DIM]


def _stack_heads(ref, h):
    return jnp.concatenate([_head_cols(ref, GQA_GROUP * h + g) for g in range(GQA_GROUP)], axis=0)


def _band(prev_ref, cur_ref, h):
    return jnp.concatenate([_head_cols(prev_ref, h), _head_cols(cur_ref, h)], axis=0)


def _pick(prev, band):
    return jnp.where(prev, band[:WINDOW], band[WINDOW:])


def _spread(prev, x):
    return jnp.concatenate([jnp.where(prev, x, 0.0), jnp.where(prev, 0.0, x)], axis=0).astype(BF16)


def _attn_probs(s_band, sink, prev, has_prev):
    scale = HEAD_DIM ** -0.5
    s = jnp.where(prev, jnp.where(has_prev, s_band[:WINDOW], NEG), s_band[WINDOW:]) * scale
    m = jnp.maximum(jnp.max(s, axis=0, keepdims=True), sink)
    e, es = jnp.exp(s - m), jnp.exp(sink - m)
    inv = 1.0 / (jnp.sum(e, axis=0, keepdims=True) + es)
    return e * inv, es * inv


def _attn_specs(nb):
    kcol, vcol = Q_WIDTH // KV_WIDTH, Q_WIDTH // KV_WIDTH + 1
    q_spec = pl.BlockSpec((WINDOW, Q_WIDTH), lambda n: (n, 0))
    return [q_spec,
            pl.BlockSpec((WINDOW, KV_WIDTH), lambda n: (n, kcol)),
            pl.BlockSpec((WINDOW, KV_WIDTH), lambda n: (jnp.maximum(n - 1, 0), kcol)),
            pl.BlockSpec((WINDOW, KV_WIDTH), lambda n: (n, vcol)),
            pl.BlockSpec((WINDOW, KV_WIDTH), lambda n: (jnp.maximum(n - 1, 0), vcol)),
            pl.BlockSpec((N_KV_HEADS, 8, ROWS), lambda n: (0, 0, 0))]


def attn_fwd(qkv, sink_rows, *, name, riders=()):
    s = qkv.shape[0]

    def body(q_ref, kc_ref, kp_ref, vc_ref, vp_ref, sink_ref, o_ref):
        prev = _prev_slots()
        has_prev = pl.program_id(0) > 0
        heads = range(N_KV_HEADS)
        s_bands = [lax.dot_general(_band(kp_ref, kc_ref, h), _stack_heads(q_ref, h), NT_DIMS, preferred_element_type=F32)
                   for h in heads]
        p_bands = [_spread(prev, _attn_probs(s_bands[h], sink_ref[h, 0:1, :], prev, has_prev)[0]) for h in heads]
        outs = [lax.dot_general(_band(vp_ref, vc_ref, h), p_bands[h], TN_DIMS, preferred_element_type=F32).T for h in heads]
        for h in heads:
            for g in range(GQA_GROUP):
                head = GQA_GROUP * h + g
                o_ref[:, head * HEAD_DIM:(head + 1) * HEAD_DIM] = outs[h][g * WINDOW:(g + 1) * WINDOW].astype(BF16)

    core, rr = _call(
        body, grid=(s // WINDOW,), in_specs=_attn_specs(s // WINDOW), out_specs=[pl.BlockSpec((WINDOW, Q_WIDTH), lambda n: (n, 0))],
        out_shape=[jax.ShapeDtypeStruct((s, Q_WIDTH), BF16)], operands=(qkv, qkv, qkv, qkv, qkv, sink_rows), sem=("parallel",),
        name=name, riders=riders)
    return _ret(core, rr, riders)


def attn_bwd(qkv, sink_rows, do, *, name, riders=()):
    s = qkv.shape[0]

    def body(q_ref, kc_ref, kp_ref, vc_ref, vp_ref, sink_ref, do_ref, dq_ref, dkc_ref, dkp_ref, dvc_ref, dvp_ref, dsink_ref):
        n = pl.program_id(0)
        prev = _prev_slots()
        scale = HEAD_DIM ** -0.5
        heads = range(N_KV_HEADS)
        qs, dos = [_stack_heads(q_ref, h) for h in heads], [_stack_heads(do_ref, h) for h in heads]
        kbands, vbands = [_band(kp_ref, kc_ref, h) for h in heads], [_band(vp_ref, vc_ref, h) for h in heads]
        s_bands = [lax.dot_general(kbands[h], qs[h], NT_DIMS, preferred_element_type=F32) for h in heads]
        dp_bands = [lax.dot_general(vbands[h], dos[h], NT_DIMS, preferred_element_type=F32) for h in heads]
        ds_bands, p_bands, parts = [], [], []
        for h in heads:
            p, ps = _attn_probs(s_bands[h], sink_ref[h, 0:1, :], prev, n > 0)
            dp = _pick(prev, dp_bands[h])
            delta = jnp.sum(p * dp, axis=0, keepdims=True)
            ds_bands.append(_spread(prev, p * (dp - delta) * scale))
            p_bands.append(_spread(prev, p))
            dsink = -(ps * delta)
            for g in range(GQA_GROUP):
                parts.append(jnp.broadcast_to(jnp.sum(dsink[:, g * WINDOW:(g + 1) * WINDOW], axis=1, keepdims=True), (8, LANES)))
        for h in heads:
            dk = jnp.dot(ds_bands[h], qs[h], preferred_element_type=F32).astype(BF16)
            dv = jnp.dot(p_bands[h], dos[h], preferred_element_type=F32).astype(BF16)
            dq = lax.dot_general(kbands[h], ds_bands[h], TN_DIMS, preferred_element_type=F32).T
            cols = slice(h * HEAD_DIM, (h + 1) * HEAD_DIM)
            dkp_ref[:, cols], dkc_ref[:, cols] = dk[:WINDOW], dk[WINDOW:]
            dvp_ref[:, cols], dvc_ref[:, cols] = dv[:WINDOW], dv[WINDOW:]
            for g in range(GQA_GROUP):
                head = GQA_GROUP * h + g
                dq_ref[:, head * HEAD_DIM:(head + 1) * HEAD_DIM] = dq[g * WINDOW:(g + 1) * WINDOW].astype(BF16)

        @pl.when(n == 0)
        def _():
            for i, part in enumerate(parts):
                dsink_ref[i // GQA_GROUP, i % GQA_GROUP] = part

        @pl.when(n > 0)
        def _():
            for i, part in enumerate(parts):
                dsink_ref[i // GQA_GROUP, i % GQA_GROUP] += part

    rows_q = pl.BlockSpec((WINDOW, Q_WIDTH), lambda n: (n, 0))
    rows_kv = pl.BlockSpec((WINDOW, KV_WIDTH), lambda n: (n, 0))
    kv_shape = jax.ShapeDtypeStruct((s, KV_WIDTH), BF16)
    core, rr = _call(
        body, grid=(s // WINDOW,), in_specs=_attn_specs(s // WINDOW) + [rows_q],
        out_specs=[rows_q, rows_kv, rows_kv, rows_kv, rows_kv,
                   pl.BlockSpec((N_KV_HEADS, GQA_GROUP, 8, LANES), lambda n: (0, 0, 0, 0))],
        out_shape=[jax.ShapeDtypeStruct((s, Q_WIDTH), BF16), kv_shape, kv_shape, kv_shape, kv_shape,
                   jax.ShapeDtypeStruct((N_KV_HEADS, GQA_GROUP, 8, LANES), F32)],
        operands=(qkv, qkv, qkv, qkv, qkv, sink_rows, do), sem=("arbitrary",), name=name, riders=riders)
    return _ret(core, rr, riders)


GELU_C = 0.7978845608028654
GELU_A = 0.044715


def _gelu(x):
    return 0.5 * x * (1.0 + jnp.tanh(x * (GELU_C + (GELU_C * GELU_A) * (x * x))))


def _gelu_and_grad(x):
    x2 = x * x
    t = jnp.tanh(x * (GELU_C + (GELU_C * GELU_A) * x2))
    half_x, one_t = 0.5 * x, 1.0 + t
    return half_x * one_t, 0.5 * one_t + half_x * (1.0 - t * t) * (GELU_C + (3.0 * GELU_C * GELU_A) * x2)


def _tril_bf16(w):
    row = lax.broadcasted_iota(I32, (SGU_CHUNK, SGU_CHUNK), 0)
    col = lax.broadcasted_iota(I32, (SGU_CHUNK, SGU_CHUNK), 1)
    return jnp.where(row >= col, w, 0.0).astype(BF16)


def _sgu_norm(vg, g, b):
    mu = jnp.mean(vg, axis=-1, keepdims=True)
    cen = vg - mu
    rstd = lax.rsqrt(jnp.mean(cen * cen, axis=-1, keepdims=True) + EPS)
    xhat = cen * rstd
    return xhat, rstd, xhat * g + b


def sgu_in_fwd(h, w_in, ln_g, ln_b, w_sp, b_sp, *, name, tm=512, riders=()):
    s, k = h.shape
    ns = w_in.shape[2]
    tm = _row_tile(s, tm)

    def body(h_ref, w0, w1, w2, w3, g_ref, b_ref, w_ref, bs_ref, z_ref, y_ref):
        hv = h_ref[...]
        zs = [jnp.dot(hv, w_ref_j[...], preferred_element_type=F32) for w_ref_j in (w0, w1, w2, w3)]
        for j, zj in enumerate(zs):
            z_ref[:, j * ns:(j + 1) * ns] = zj.astype(BF16)
        u = _gelu(jnp.concatenate(zs[:2], axis=1))
        _, _, vn = _sgu_norm(_gelu(jnp.concatenate(zs[2:], axis=1)), g_ref[...], b_ref[...])
        vn = vn.astype(BF16)
        for grp in range(SGU_GROUPS):
            w = _tril_bf16(w_ref[grp])
            cols = slice(grp * LANES, (grp + 1) * LANES)
            for ch in range(tm // SGU_CHUNK):
                rows = slice(ch * SGU_CHUNK, (ch + 1) * SGU_CHUNK)
                mixed = jnp.dot(w, vn[rows, cols], preferred_element_type=F32) + bs_ref[grp]
                y_ref[rows, cols] = (u[rows, cols] * mixed).astype(BF16)

    def shard(j):
        return pl.BlockSpec((None, k, ns), lambda i: (j, 0, 0))

    full3 = pl.BlockSpec((SGU_GROUPS, SGU_CHUNK, SGU_CHUNK), lambda i: (0, 0, 0))
    core, rr = _call(
        body, grid=(s // tm,),
        in_specs=[_row_spec(tm, k)] + [shard(j) for j in range(N_CHIPS)] + [_vec_spec(D_MODEL), _vec_spec(D_MODEL), full3, full3],
        out_specs=[_row_spec(tm, 2 * D_MODEL), _row_spec(tm, D_MODEL)],
        out_shape=[jax.ShapeDtypeStruct((s, 2 * D_MODEL), BF16), jax.ShapeDtypeStruct((s, D_MODEL), BF16)],
        operands=(h, w_in, w_in, w_in, w_in, ln_g, ln_b, w_sp, b_sp), sem=("parallel",), name=name, riders=riders)
    return _ret(core, rr, riders)


def sgu_bwd(z, dy, ln_g, ln_b, w_sp, b_sp, *, name, tm=256, riders=()):
    s = z.shape[0]
    tm = _row_tile(s, tm)

    def body(z_ref, dy_ref, g_ref, b_ref, w_ref, bs_ref, dz_ref, dw_ref, dbs_ref, dg_ref, db_ref, dvn_buf):
        first = pl.program_id(0) == 0
        u, u_grad = _gelu_and_grad(z_ref[:, :D_MODEL].astype(F32))
        vg, v_grad = _gelu_and_grad(z_ref[:, D_MODEL:].astype(F32))
        xhat, rstd, vn = _sgu_norm(vg, g_ref[...], b_ref[...])
        vn = vn.astype(BF16)
        dyv = dy_ref[...]
        dmixed = dyv * u
        dz_gate = dyv * u_grad
        row = lax.broadcasted_iota(I32, (SGU_CHUNK, SGU_CHUNK), 0)
        col = lax.broadcasted_iota(I32, (SGU_CHUNK, SGU_CHUNK), 1)
        dws, dbss = [], []
        for grp in range(SGU_GROUPS):
            w = _tril_bf16(w_ref[grp])
            cols = slice(grp * LANES, (grp + 1) * LANES)
            dw = jnp.zeros((SGU_CHUNK, SGU_CHUNK), F32)
            dbs = jnp.zeros((SGU_CHUNK, 1), F32)
            for ch in range(tm // SGU_CHUNK):
                rows = slice(ch * SGU_CHUNK, (ch + 1) * SGU_CHUNK)
                vblk = vn[rows, cols]
                mixed = jnp.dot(w, vblk, preferred_element_type=F32) + bs_ref[grp]
                dz_ref[rows, cols] = (dz_gate[rows, cols] * mixed).astype(BF16)
                dm = dmixed[rows, cols]
                dmb = dm.astype(BF16)
                dvn_buf[rows, cols] = lax.dot_general(w, dmb, TN_DIMS, preferred_element_type=F32)
                dw += lax.dot_general(dmb, vblk, NT_DIMS, preferred_element_type=F32)
                dbs += jnp.sum(dm, axis=-1, keepdims=True)
            dws.append(jnp.where(row >= col, dw, 0.0))
            dbss.append(jnp.broadcast_to(dbs, (SGU_CHUNK, SGU_CHUNK)))

        dvn = dvn_buf[...]
        dxhat = dvn * g_ref[...]
        dvg = rstd * (dxhat - jnp.mean(dxhat, axis=-1, keepdims=True) - xhat * jnp.mean(dxhat * xhat, axis=-1, keepdims=True))
        dz_ref[:, D_MODEL:] = (dvg * v_grad).astype(BF16)
        dlng, dlnb = jnp.sum(dvn * xhat, axis=0, keepdims=True), jnp.sum(dvn, axis=0, keepdims=True)

        @pl.when(first)
        def _():
            for grp in range(SGU_GROUPS):
                dw_ref[grp] = dws[grp]
                dbs_ref[grp] = dbss[grp]
            dg_ref[...] = dlng
            db_ref[...] = dlnb

        @pl.when(jnp.logical_not(first))
        def _():
            for grp in range(SGU_GROUPS):
                dw_ref[grp] += dws[grp]
                dbs_ref[grp] += dbss[grp]
            dg_ref[...] += dlng
            db_ref[...] += dlnb

    full3 = pl.BlockSpec((SGU_GROUPS, SGU_CHUNK, SGU_CHUNK), lambda i: (0, 0, 0))
    s3 = jax.ShapeDtypeStruct((SGU_GROUPS, SGU_CHUNK, SGU_CHUNK), F32)
    vshape = jax.ShapeDtypeStruct((1, D_MODEL), F32)
    core, rr = _call(
        body, grid=(s // tm,),
        in_specs=[_row_spec(tm, 2 * D_MODEL), _row_spec(tm, D_MODEL), _vec_spec(D_MODEL), _vec_spec(D_MODEL), full3, full3],
        out_specs=[_row_spec(tm, 2 * D_MODEL), full3, full3, _vec_spec(D_MODEL), _vec_spec(D_MODEL)],
        out_shape=[jax.ShapeDtypeStruct((s, 2 * D_MODEL), BF16), s3, s3, vshape, vshape],
        scratch_shapes=[pltpu.VMEM((tm, D_MODEL), F32)], operands=(z, dy, ln_g, ln_b, w_sp, b_sp), name=name, riders=riders)
    return _ret(core, rr, riders)


def _sigmoid(x):
    return 1.0 / (1.0 + jnp.exp(-x))


def ffn_up(h, w_gu, *, name, tm=512, riders=()):
    s = h.shape[0]
    tm = _row_tile(s, tm)

    def body(h_ref, wg_ref, wu_ref, d_ref, a_ref):
        hv = h_ref[...]
        sub = min(256, tm)
        for t in range(tm // sub):
            rows = slice(t * sub, (t + 1) * sub)
            g = jnp.dot(hv[rows], wg_ref[...], preferred_element_type=F32)
            u = jnp.dot(hv[rows], wu_ref[...], preferred_element_type=F32)
            sig = _sigmoid(g)
            silu = g * sig
            d_ref[0, rows, :] = (u * (sig + silu * (1.0 - sig))).astype(BF16)
            d_ref[1, rows, :] = silu.astype(BF16)
            a_ref[rows, :] = (silu * u).astype(BF16)

    core, rr = _call(
        body, grid=(2, s // tm),
        in_specs=[pl.BlockSpec((tm, D_MODEL), lambda j, i: (i, 0)),
                  pl.BlockSpec((None, D_MODEL, FF_HALF), lambda j, i: (j, 0, 0)),
                  pl.BlockSpec((None, D_MODEL, FF_HALF), lambda j, i: (j + 2, 0, 0))],
        out_specs=[pl.BlockSpec((2, tm, FF_HALF), lambda j, i: (0, i, j)), pl.BlockSpec((tm, FF_HALF), lambda j, i: (i, j))],
        out_shape=[jax.ShapeDtypeStruct((2, s, D_FF), BF16), jax.ShapeDtypeStruct((s, D_FF), BF16)],
        operands=(h, w_gu, w_gu), sem=("parallel", "parallel"), name=name, riders=riders)
    return _ret(core, rr, riders)


def _weight_tile(rows):
    for tr in (512, 352, 256, 128):
        if rows % tr == 0:
            return tr
    return rows


def place_shard(w, layer, chip_arr, dtype, *, name, riders=()):
    _, r, c = w.shape
    tr = _weight_tile(r)

    def body(chip_ref, w_ref, o_ref):
        o_ref[...] = w_ref[...].astype(dtype)

    core, rr = _call(
        body, grid=(r // tr,), prefetch=(chip_arr,),
        in_specs=[pl.BlockSpec((None, tr, c), lambda i, chip: (layer, i, 0))],
        out_specs=[pl.BlockSpec((None, tr, c), lambda i, chip: (chip[0], i, 0))],
        out_shape=[jax.ShapeDtypeStruct((N_CHIPS, r, c), dtype)], operands=(w,), sem=("parallel",), name=name, riders=riders)
    return _ret(core, rr, riders)


def _adamw_math(w, g, m, v):
    m = ADAM_B1 * m + (1.0 - ADAM_B1) * g
    v = ADAM_B2 * v + (1.0 - ADAM_B2) * (g * g)
    m_hat = m / (1.0 - ADAM_B1 ** ADAM_STEP)
    v_hat = v / (1.0 - ADAM_B2 ** ADAM_STEP)
    delta = -ADAM_LR * (m_hat / (jnp.sqrt(v_hat) + ADAM_EPS) + ADAM_WD * w)
    return delta, m, v


def adamw(w, g, m, v, *, name, after=None):
    nl, r, c = w.shape
    tr = _weight_tile(r)

    def body(w_ref, g_ref, m_ref, v_ref, *rest):
        go_ref, d_ref, mo_ref, vo_ref = rest[-4:]
        gv = g_ref[...]
        go_ref[...] = gv
        d_ref[...], mo_ref[...], vo_ref[...] = _adamw_math(w_ref[...], gv, m_ref[...], v_ref[...])

    spec = pl.BlockSpec((None, tr, c), lambda l, i: (l, i, 0))
    shape = jax.ShapeDtypeStruct(w.shape, F32)
    extra = [] if after is None else [after]
    outs, _ = _call(body, grid=(nl, r // tr), in_specs=[spec] * 4 + [ANY] * len(extra), out_specs=[spec] * 4,
                    out_shape=[shape] * 4, operands=(w, g, m, v, *extra), sem=("parallel", "parallel"), name=name)
    return outs


def adamw_small(ws, gs, ms, vs, *, name):
    n = len(ws)

    def body(*refs):
        ins, outs = refs[:4 * n], refs[4 * n:]
        for t in range(n):
            gv = ins[n + t][...]
            outs[t][...] = gv
            outs[n + t][...], outs[2 * n + t][...], outs[3 * n + t][...] = _adamw_math(
                ins[t][...], gv, ins[2 * n + t][...], ins[3 * n + t][...])

    shapes = [jax.ShapeDtypeStruct(w.shape, F32) for w in ws]
    res = pl.pallas_call(body, out_shape=shapes * 4, name=name)(*ws, *gs, *ms, *vs)
    return res[:n], res[n:2 * n], res[2 * n:3 * n], res[3 * n:]


def pair_add(g, r1, c_arr, *, name):
    _, rows, cdim = g.shape
    h = rows // 2

    def body(c_ref, g_ref, r_ref, o_ref):
        o_ref[...] = (g_ref[...].astype(F32) + r_ref[...].astype(F32)).astype(o_ref.dtype)

    (out,), _ = _call(
        body, grid=(N_CHIPS,), prefetch=(c_arr,),
        in_specs=[pl.BlockSpec((None, h, cdim), lambda s, c: (s, c[0], 0)), pl.BlockSpec((None, h, cdim), lambda s, c: (s, 0, 0))],
        out_specs=[pl.BlockSpec((None, h, cdim), lambda s, c: (s, 0, 0))],
        out_shape=[jax.ShapeDtypeStruct((N_CHIPS, h, cdim), g.dtype)], operands=(g, r1), sem=("parallel",), name=name)
    return out


def final_add(g, r1, r2, jc_arr, *, dest_shape, lead, prev, name):
    _, rows, cdim = g.shape
    h = rows // 2

    def body(jc_ref, g_ref, r1_ref, r2_ref, *rest):
        o_ref = rest[-1]
        acc = g_ref[...].astype(F32) + r1_ref[...].astype(F32)
        for k in range(3):
            acc = acc + r2_ref[k].astype(F32)
        o_ref[...] = acc

    if lead is None:
        o_spec = pl.BlockSpec((h, cdim), lambda i, jc: (jc[1], 0))
    elif lead == "chip":
        o_spec = pl.BlockSpec((None, h, cdim), lambda i, jc: (jc[0], jc[1], 0))
    else:
        o_spec = pl.BlockSpec((None, h, cdim), lambda i, jc: (lead, jc[1], 0))
    in_specs = [pl.BlockSpec((None, h, cdim), lambda i, jc: (jc[0], jc[1], 0)),
                pl.BlockSpec((None, h, cdim), lambda i, jc: (jc[0], 0, 0)),
                pl.BlockSpec((3, h, cdim), lambda i, jc: (0, 0, 0))]
    operands = [g, r1, r2]
    aliases = None
    if prev is not None:
        in_specs.append(ANY)
        operands.append(prev)
        aliases = {3: 0}
    (out,), _ = _call(body, grid=(1,), prefetch=(jc_arr,), in_specs=in_specs, out_specs=[o_spec],
                      out_shape=[jax.ShapeDtypeStruct(dest_shape, F32)], operands=operands, aliases=aliases, name=name)
    return out


def _place():
    return lax.axis_index("x"), lax.axis_index("y"), lax.axis_index("c")


def _partner(x, y, k):
    return (1 - x if k >> 1 else x), (1 - y if k & 1 else y)


WHOLE = (0, 1, 1)


def _half(rows, sel, dtype, piece=WHOLE):
    lo, hi, n = piece
    align = 16 if dtype == BF16 else 8
    step = rows // 2 // n
    assert rows // 2 == step * n and step % align == 0
    return pl.ds(pl.multiple_of(sel * (rows // 2) + lo * step, align), (hi - lo) * step)


def _rider(peers, inputs, aliased, fresh, nsem, copies, arrivals):
    def start(ins, outs, send, recv):
        for cp in copies(ins, outs, send, recv):
            cp.start()

    def finish(ins, outs, send, recv):
        for cp in arrivals(ins, outs, send, recv):
            cp.wait_recv()
        for cp in copies(ins, outs, send, recv):
            cp.wait_send()

    return types.SimpleNamespace(peers=peers, inputs=list(inputs), aliased=list(aliased), fresh=list(fresh), nsem=nsem,
                                 start=start, finish=finish)


def _remote(src, dst, send, recv, idx, dev):
    return pltpu.make_async_remote_copy(src_ref=src, dst_ref=dst, send_sem=send.at[idx], recv_sem=recv.at[idx],
                                        device_id=dev, device_id_type=MESH)


def gather_ici_rider(fulls, pieces=None):
    nt = len(fulls)
    pieces = pieces or [WHOLE] * nt

    def region(outs, t, slot, sel):
        return outs[t].at[slot, _half(fulls[t].shape[1], sel, fulls[t].dtype, pieces[t])]

    def copies(ins, outs, send, recv):
        x, y, c = _place()
        res = []
        for t in range(nt):
            for k in (1, 2, 3):
                px, py = _partner(x, y, k)
                mine = region(outs, t, 2 * x + y, c)
                res.append(_remote(mine, mine, send, recv, 3 * t + k - 1, (px, py, c)))
        return res

    def arrivals(ins, outs, send, recv):
        x, y, c = _place()
        res = []
        for t in range(nt):
            for k in (1, 2, 3):
                px, py = _partner(x, y, k)
                theirs = region(outs, t, 2 * px + py, c)
                res.append(_remote(theirs, theirs, send, recv, 3 * t + k - 1, (x, y, c)))
        return res

    return _rider("chips", fulls, range(nt), [], 3 * nt, copies, arrivals)


def gather_d2d_rider(fulls, pieces=None):
    nt = len(fulls)
    pieces = pieces or [WHOLE] * nt

    def region(outs, t, slot, sel):
        return outs[t].at[slot, _half(fulls[t].shape[1], sel, fulls[t].dtype, pieces[t])]

    def both(outs, send, recv, mine):
        x, y, c = _place()
        res = []
        for t in range(nt):
            for k in (1, 2, 3):
                px, py = _partner(x, y, k)
                part = region(outs, t, 2 * px + py, c if mine else 1 - c)
                res.append(_remote(part, part, send, recv, 3 * t + k - 1, (x, y, 1 - c)))
        return res

    return _rider("sibling", fulls, range(nt), [], 3 * nt, lambda i, o, s, r: both(o, s, r, True),
                  lambda i, o, s, r: both(o, s, r, False))


def exchange_rider(grads):
    nt = len(grads)

    def both(ins, outs, send, recv):
        x, y, c = _place()
        return [_remote(ins[t].at[:, _half(grads[t].shape[1], 1 - c, grads[t].dtype)], outs[t], send, recv, t, (x, y, 1 - c))
                for t in range(nt)]

    fresh = [jax.ShapeDtypeStruct((N_CHIPS, g.shape[1] // 2, g.shape[2]), g.dtype) for g in grads]
    return _rider("sibling", grads, [], fresh, nt, both, both)


def scatter_rider(parts):
    nt = len(parts)

    def both(ins, outs, send, recv):
        x, y, c = _place()
        res = []
        for t in range(nt):
            for k in (1, 2, 3):
                px, py = _partner(x, y, k)
                res.append(_remote(ins[t].at[2 * px + py], outs[t].at[k - 1], send, recv, 3 * t + k - 1, (px, py, c)))
        return res

    fresh = [jax.ShapeDtypeStruct((3,) + p.shape[1:], p.dtype) for p in parts]
    return _rider("chips", parts, [], fresh, 3 * nt, both, both)


def broadcast_rider(bufs, items):
    def region(outs, item, sel):
        bi, lead = item
        ref = outs[bi]
        if lead == "chip":
            x, y, _ = _place()
            ref = ref.at[2 * x + y]
        elif lead is not None:
            ref = ref.at[lead]
        return ref.at[_half(ref.shape[0], sel, F32)]

    def both(outs, send, recv, mine):
        x, y, c = _place()
        res = []
        for i, item in enumerate(items):
            part = region(outs, item, c if mine else 1 - c)
            res.append(_remote(part, part, send, recv, i, (x, y, 1 - c)))
        return res

    return _rider("sibling", bufs, range(len(bufs)), [], len(items), lambda i, o, s, r: both(o, s, r, True),
                  lambda i, o, s, r: both(o, s, r, False))


def allcast_rider(buf):
    peers = [(k, flip) for k in range(N_CHIPS) for flip in (0, 1) if (k, flip) != (0, 0)]

    def both(outs, send, recv, mine):
        x, y, c = _place()
        res = []
        for i, (k, flip) in enumerate(peers):
            px, py = _partner(x, y, k)
            pc = 1 - c if flip else c
            slot, sel = (2 * x + y, c) if mine else (2 * px + py, pc)
            part = outs[0].at[slot, _half(buf.shape[1], sel, F32)]
            res.append(_remote(part, part, send, recv, i, (px, py, pc)))
        return res

    return _rider("everyone", [buf], [0], [], len(peers), lambda i, o, s, r: both(o, s, r, True),
                  lambda i, o, s, r: both(o, s, r, False))


def comm_call(riders, *, name):
    _, res = _call(None, riders=riders, name=name)
    return res


SEMS = pl.BlockSpec(memory_space=pltpu.SEMAPHORE)
SIDE_EFFECT = pltpu.SideEffectType.DATAFLOW_SIDE_EFFECTING


def _split_refs(riders, refs):
    views, p = [], 0
    for r in riders:
        bufs = refs[p:p + len(r.inputs) + len(r.fresh)]
        p += len(bufs)
        ins = bufs[:len(r.inputs)]
        views.append([ins, [ins[i] for i in r.aliased] + list(bufs[len(r.inputs):])])
    for view in views:
        view += [refs[p], refs[p + 1]]
        p += 2
    return views


def comm_start(riders, *, name):
    kind = _peer_kind(riders)
    bufs = [a for r in riders for a in r.inputs]
    fresh = [f for r in riders for f in r.fresh]
    n_buf, n_fresh = len(bufs), len(fresh)

    def body(*refs):
        ins, outs = refs[:n_buf], refs[n_buf:]
        through, land, sems = outs[:n_buf], outs[n_buf:n_buf + n_fresh], outs[n_buf + n_fresh:-1]
        _peer_barrier(kind)
        per_rider, pb, pf = [], 0, 0
        for r in riders:
            per_rider += list(through[pb:pb + len(r.inputs)]) + list(land[pf:pf + len(r.fresh)])
            pb, pf = pb + len(r.inputs), pf + len(r.fresh)
        for r, (r_ins, r_outs, send, recv) in zip(riders, _split_refs(riders, per_rider + list(sems))):
            r.start(r_ins, r_outs, send, recv)
        outs[-1][...] = jnp.zeros((8, LANES), F32)

    sem_shapes = [pltpu.SemaphoreType.DMA((r.nsem,)) for r in riders for _ in (0, 1)]
    res = pl.pallas_call(
        body, name=name, in_specs=[ANY] * n_buf,
        out_specs=[ANY] * (n_buf + n_fresh) + [SEMS] * len(sem_shapes) + [pl.BlockSpec(memory_space=pltpu.VMEM)],
        out_shape=[jax.ShapeDtypeStruct(a.shape, a.dtype) for a in bufs] + fresh + sem_shapes
        + [jax.ShapeDtypeStruct((8, LANES), F32)],
        input_output_aliases={i: i for i in range(n_buf)},
        compiler_params=pltpu.CompilerParams(has_side_effects=SIDE_EFFECT, collective_id=PEER_KINDS.index(kind)))(*bufs)
    return (riders, list(res[:n_buf + n_fresh]), list(res[n_buf + n_fresh:-1])), res[-1]


def comm_wait(state, after, *, name):
    riders, bufs, sems = state
    n_buf, n_sem = len(bufs), len(sems)
    n_in = sum(len(r.inputs) for r in riders)

    def body(*refs):
        held, sem_refs = refs[:n_buf], refs[n_buf:n_buf + n_sem]
        through, land = held[:n_in], held[n_in:]
        per_rider, pb, pf = [], 0, 0
        for r in riders:
            per_rider += list(through[pb:pb + len(r.inputs)]) + list(land[pf:pf + len(r.fresh)])
            pb, pf = pb + len(r.inputs), pf + len(r.fresh)
        for r, (r_ins, r_outs, send, recv) in zip(riders, _split_refs(riders, per_rider + list(sem_refs))):
            r.finish(r_ins, r_outs, send, recv)

    res = pl.pallas_call(
        body, name=name, in_specs=[ANY] * n_buf + [SEMS] * n_sem + [ANY], out_specs=[ANY] * n_buf,
        out_shape=[jax.ShapeDtypeStruct(a.shape, a.dtype) for a in bufs],
        input_output_aliases={i: i for i in range(n_buf)},
        compiler_params=pltpu.CompilerParams(has_side_effects=SIDE_EFFECT))(*bufs, *sems, after)
    through, land = list(res[:n_in]), list(res[n_in:])
    out, pb, pf = [], 0, 0
    for r in riders:
        r_ins, r_land = through[pb:pb + len(r.inputs)], land[pf:pf + len(r.fresh)]
        pb, pf = pb + len(r.inputs), pf + len(r.fresh)
        out.append([r_ins[i] for i in r.aliased] + r_land)
    return out


SLAB_ROWS = 192


def _pad_rows(a, rows=8):
    return jnp.pad(a, ((0, rows - a.shape[0]), (0, 0)))


def _pack_small(norm_grads, db_qkv, db_o, dsinks, db_sp, dln_g, dln_b, dw_sp, loss_part):
    parts = [
        jnp.concatenate(norm_grads, axis=0),
        _pad_rows(jnp.pad(db_qkv, ((0, 0), (0, 2 * D_MODEL - QKV_WIDTH))).reshape(2, D_MODEL)),
        _pad_rows(db_o),
        _pad_rows(jnp.pad(dsinks.reshape(1, N_Q_HEADS), ((0, 0), (0, D_MODEL - N_Q_HEADS)))),
        _pad_rows(db_sp.reshape(1, D_MODEL)),
        _pad_rows(jnp.concatenate([dln_g, dln_b, jnp.pad(loss_part[0:1], ((0, 0), (0, D_MODEL - LANES)))], axis=0)),
        dw_sp.reshape(SGU_CHUNK, D_MODEL),
    ]
    slab = jnp.concatenate(parts, axis=0)
    return jnp.pad(slab, ((0, SLAB_ROWS - slab.shape[0]), (0, 0))).reshape(N_CHIPS, SLAB_ROWS // N_CHIPS, D_MODEL)


def _unpack_small(slab, j):
    slab = slab.reshape(SLAB_ROWS, D_MODEL)
    norms = [slab[2 * i:2 * i + 2] for i in range(4)]
    db_qkv = slab[8:10].reshape(1, 2 * D_MODEL)[:, :QKV_WIDTH]
    db_o = slab[16:17]
    dsinks = slab[24:25, :N_Q_HEADS]
    db_sp = slab[32:33].reshape(SGU_GROUPS, SGU_CHUNK)
    width = D_MODEL // N_CHIPS
    dln_g = lax.dynamic_slice(slab[40:41], (0, j * width), (1, width))
    dln_b = lax.dynamic_slice(slab[41:42], (0, j * width), (1, width))
    dw_sp = slab[48:48 + SGU_CHUNK].reshape(SGU_GROUPS * SGU_CHUNK, SGU_CHUNK)
    return norms, db_qkv, db_o, dsinks, db_sp, dln_g, dln_b, dw_sp, slab[42, 0]


class _GradReduce:
    def __init__(self, c_arr, jc_arr, dest_shapes):
        self.c_arr, self.jc_arr, self.dest_shapes = c_arr, jc_arr, dest_shapes
        self.grad, self.sibling, self.pair, self.chips, self.dest = {}, {}, {}, {}, {}

    def exchange(self, tags):
        return exchange_rider([self.grad[t] for t in tags])

    def exchanged(self, tags, res):
        for t, r in zip(tags, res):
            self.sibling[t] = r
            self.pair[t] = pair_add(self.grad[t], r, self.c_arr, name=f"pair_add_{t}")

    def scatter(self, tags):
        return scatter_rider([self.pair[t] for t in tags])

    def scattered(self, tags, res, where):
        for t, r in zip(tags, res):
            name, lead = where[t]
            self.dest[name] = final_add(self.grad[t], self.sibling[t], r, self.jc_arr, dest_shape=self.dest_shapes[name],
                                        lead=lead, prev=self.dest.get(name), name=f"final_add_{t}")

    def broadcast(self, items):
        names = []
        for n, _ in items:
            if n not in names:
                names.append(n)
        return names, broadcast_rider([self.dest[n] for n in names], [(names.index(n), lead) for n, lead in items])

    def broadcasted(self, names, res):
        for n, r in zip(names, res):
            self.dest[n] = r


def kernel(x, norm_mix_pre, norm_mix_post, norm_ffn_pre, norm_ffn_post, attn_w_qkv, attn_b_qkv, attn_sinks, attn_w_o, attn_b_o, sgu_w_in, sgu_ln_g, sgu_ln_b, sgu_w_spatial, sgu_b_spatial, sgu_w_out, ffn_w_gate_up, ffn_w_down, loss_target, m_norm_mix_pre, m_norm_mix_post, m_norm_ffn_pre, m_norm_ffn_post, m_attn_w_qkv, m_attn_b_qkv, m_attn_sinks, m_attn_w_o, m_attn_b_o, m_sgu_w_in, m_sgu_ln_g, m_sgu_ln_b, m_sgu_w_spatial, m_sgu_b_spatial, m_sgu_w_out, m_ffn_w_gate_up, m_ffn_w_down, v_norm_mix_pre, v_norm_mix_post, v_norm_ffn_pre, v_norm_ffn_post, v_attn_w_qkv, v_attn_b_qkv, v_attn_sinks, v_attn_w_o, v_attn_b_o, v_sgu_w_in, v_sgu_ln_g, v_sgu_ln_b, v_sgu_w_spatial, v_sgu_b_spatial, v_sgu_w_out, v_ffn_w_gate_up, v_ffn_w_down):
    s = x.shape[1]
    x0 = x.reshape(s, D_MODEL)
    target = loss_target.reshape(s, D_MODEL)
    mx, my, mc = lax.axis_index("x"), lax.axis_index("y"), lax.axis_index("c")
    chip = 2 * mx + my
    chip_arr = jnp.reshape(chip, (1,)).astype(I32)
    c_arr = jnp.reshape(mc, (1,)).astype(I32)
    jc_arr = jnp.stack([chip, mc]).astype(I32)
    zero_bias = jnp.zeros((1, D_MODEL), F32)

    def gain(p, i):
        return p[i:i + 1]

    big = [attn_w_qkv, attn_w_o, sgu_w_in, sgu_w_out, ffn_w_gate_up, ffn_w_gate_up, ffn_w_down, ffn_w_down]
    layers = [0, 0, 0, 0, 0, 1, 0, 1]
    tags = ["qkv", "wo", "win", "wout", "wgu0", "wgu1", "wd0", "wd1"]
    full = {t: place_shard(w, l, chip_arr, BF16, name=f"place_{t}") for w, l, t in zip(big, layers, tags)
            if t not in ("wgu0", "wgu1")}
    ln_pack = _pad_rows(jnp.concatenate([sgu_ln_g, sgu_ln_b], axis=0), 16)[None]
    full["ln"] = place_shard(ln_pack, 0, chip_arr, F32, name="place_ln")

    def split(items):
        return [i if isinstance(i, str) else i[0] for i in items], [WHOLE if isinstance(i, str) else tuple(i[1:]) for i in items]

    def ici(*items):
        names, pieces = split(items)
        return gather_ici_rider([full[n] for n in names], pieces)

    def d2d(*items):
        names, pieces = split(items)
        return gather_d2d_rider([full[n] for n in names], pieces)

    def landed(items, res):
        for n, r in zip(split(items)[0], res):
            full[n] = r

    cos, sin = _rope_tables(s)
    sink_rows = jnp.broadcast_to(
        jnp.repeat(attn_sinks.reshape(N_KV_HEADS, GQA_GROUP), WINDOW, axis=1)[:, None, :], (N_KV_HEADS, 8, ROWS))
    w_sp = sgu_w_spatial.reshape(SGU_GROUPS, SGU_CHUNK, SGU_CHUNK)
    b_sp = jnp.broadcast_to(sgu_b_spatial.reshape(SGU_GROUPS, SGU_CHUNK)[:, :, None], (SGU_GROUPS, SGU_CHUNK, LANES))

    (h0, full["wgu0"]), (res,) = prenorm_and_place(x0, gain(norm_mix_pre, 0), ffn_w_gate_up, 0, chip_arr, name="prenorm_0",
                                                   riders=[ici("qkv", "ln")])
    landed(("qkv", "ln"), res)
    full["wgu1"], (res,) = place_shard(ffn_w_gate_up, 1, chip_arr, BF16, name="place_wgu1", riders=[d2d("qkv", "ln")])
    landed(("qkv", "ln"), res)
    ln_g = full["ln"][:, 0, :].reshape(1, D_MODEL)
    ln_b = full["ln"][:, 1, :].reshape(1, D_MODEL)

    def hosted(call, stages):
        outputs, results = call([{"ici": ici, "d2d": d2d}[kind](*items) for kind, items in stages])
        for (_, items), res in zip(stages, results):
            landed(items, res)
        return outputs

    qkv = hosted(lambda r: qkv_proj(h0, full["qkv"], attn_b_qkv, cos, sin, name="qkv_proj", riders=r),
                 [("ici", ("wo", ("wgu0", 0, 3, 8)))])
    o = hosted(lambda r: attn_fwd(qkv, sink_rows, name="attn_fwd", riders=r),
               [("d2d", ("wo",)), ("ici", (("wgu0", 3, 8, 8), ("wd0", 0, 2, 11), ("win", 0, 2, 8)))])
    w_o = full["wo"].reshape(Q_WIDTH, D_MODEL)
    x1, h1, m0 = hosted(lambda r: proj_residual_norm(o, w_o, x0, attn_b_o, gain(norm_mix_post, 0), gain(norm_ffn_pre, 0),
                                                     name="attn_out_norm", riders=r),
                        [("d2d", ("wgu0",)), ("ici", (("wd0", 2, 11, 11),))])
    gu0, a0 = hosted(lambda r: ffn_up(h1, full["wgu0"], name="ffn_up_0", riders=r),
                     [("d2d", ("wd0",)), ("ici", (("win", 2, 8, 8), "wout", ("wgu1", 0, 4, 8)))])
    w_d0 = full["wd0"].reshape(D_FF, D_MODEL)
    x2, h2, f0 = hosted(lambda r: proj_residual_norm(a0, w_d0, x1, zero_bias, gain(norm_ffn_post, 0), gain(norm_mix_pre, 1),
                                                     name="ffn_down_norm_0", riders=r),
                        [("d2d", ("win", "wout")), ("ici", (("wgu1", 4, 8, 8),))])
    w_in = full["win"]
    z, y = hosted(lambda r: sgu_in_fwd(h2, w_in, ln_g, ln_b, w_sp, b_sp, name="sgu_in_fwd", riders=r),
                  [("d2d", ("wgu1",)), ("ici", ("wd1",))])
    w_out = full["wout"].reshape(D_MODEL, D_MODEL)
    x3, h3, m1 = hosted(lambda r: proj_residual_norm(y, w_out, x2, zero_bias, gain(norm_mix_post, 1), gain(norm_ffn_pre, 1),
                                                     name="sgu_out_norm", riders=r),
                        [("d2d", ("wd1",))])
    w_qkv, w_gu0, w_gu1 = full["qkv"], full["wgu0"], full["wgu1"]
    w_d1 = full["wd1"].reshape(D_FF, D_MODEL)
    gu1, a1, dx4, df1, dg_fpost1, loss_part = ffn_fwd_loss_rows(
        h3, w_gu1, w_d1, x3, gain(norm_ffn_post, 1), target, name="ffn_fwd_loss_rows")

    red = _GradReduce(c_arr, jc_arr, {
        "qkv": attn_w_qkv.shape[1:], "wo": attn_w_o.shape[1:], "win": sgu_w_in.shape[1:], "wout": sgu_w_out.shape[1:],
        "wgu": ffn_w_gate_up.shape, "wd": ffn_w_down.shape, "slab": (N_CHIPS, SLAB_ROWS // N_CHIPS, D_MODEL)})
    where = {"qkv": ("qkv", None), "wo": ("wo", None), "win": ("win", None), "wout": ("wout", None), "wgu0": ("wgu", 0),
             "wgu1": ("wgu", 1), "wd0": ("wd", 0), "wd1": ("wd", 1), "small": ("slab", "chip")}

    dgu1, dx3, dm1, dg_fpre1, dg_mpost1, _ = ffn_bwd_rows(
        df1, w_d1, gu1, w_gu1, dx4, x3, gain(norm_ffn_pre, 1), m1, gain(norm_mix_post, 1), name="ffn_bwd_rows_1")
    red.grad["wd1"] = mm_tn(a1, df1, shard_major=False, tm=256, tn=D_MODEL, name="dw_down_1").reshape(
        N_CHIPS, D_FF // N_CHIPS, D_MODEL)
    red.grad["wgu1"], (res,) = mm_tn(h3, dgu1, shard_major=True, tm=512, tn=FF_HALF, name="dw_gate_up_1",
                                     riders=[red.exchange(["wd1"])])
    red.exchanged(["wd1"], res)
    dy, (res,) = mm_nt(dm1, w_out, out_dtype=F32, name="dy_sgu", riders=[red.exchange(["wgu1"])])
    red.exchanged(["wgu1"], res)
    red.grad["wout"] = mm_tn(y, dm1, shard_major=False, tm=512, tn=D_MODEL, name="dw_sgu_out").reshape(
        N_CHIPS, D_MODEL // N_CHIPS, D_MODEL)
    (dz, dw_sp, db_sp, dln_g, dln_b), (res_a, res_b) = sgu_bwd(
        z, dy, ln_g, ln_b, w_sp, b_sp, name="sgu_bwd", riders=[red.scatter(["wd1"]), red.exchange(["wout"])])
    red.scattered(["wd1"], res_a, where)
    red.exchanged(["wout"], res_b)
    names, rider = red.broadcast([("wd", 1)])
    red.grad["win"], (res_a, res_b) = mm_tn(h2, dz, shard_major=True, tm=D_MODEL, tn=2 * D_MODEL // N_CHIPS, name="dw_sgu_in",
                                            riders=[rider, red.scatter(["wout"])])
    red.broadcasted(names, res_a)
    red.scattered(["wout"], res_b, where)
    names, rider = red.broadcast([("wout", None)])
    (dx2, df0, dg_mpre1, dg_fpost0, _), (res_a, res_b) = dh_norm_bwd_pair(
        dz, w_in, dx3, x2, gain(norm_mix_pre, 1), f0, gain(norm_ffn_post, 0), name="dh_sgu_norm",
        riders=[red.exchange(["win"]), rider])
    red.exchanged(["win"], res_a)
    red.broadcasted(names, res_b)
    (dgu0, dx1, dm0, dg_fpre0, dg_mpost0, db_o), (res,) = ffn_bwd_rows(
        df0, w_d0, gu0, w_gu0, dx2, x1, gain(norm_ffn_pre, 0), m0, gain(norm_mix_post, 0), name="ffn_bwd_rows_0",
        riders=[red.scatter(["wgu1", "win"])])
    red.scattered(["wgu1", "win"], res, where)
    names, rider = red.broadcast([("wgu", 1), ("win", None)])
    dw_d0, (res,) = mm_tn(a0, df0, shard_major=False, tm=256, tn=D_MODEL, name="dw_down_0", riders=[rider])
    red.broadcasted(names, res)
    red.grad["wd0"] = dw_d0.reshape(N_CHIPS, D_FF // N_CHIPS, D_MODEL)
    do, (res,) = mm_nt(dm0, w_o, out_dtype=BF16, name="do_attn", riders=[red.exchange(["wd0"])])
    red.exchanged(["wd0"], res)
    red.grad["wgu0"], (res,) = mm_tn(h1, dgu0, shard_major=True, tm=512, tn=FF_HALF, name="dw_gate_up_0",
                                     riders=[red.scatter(["wd0"])])
    red.scattered(["wd0"], res, where)
    names, rider = red.broadcast([("wd", 0)])
    dw_o, (res_a, res_b) = mm_tn(o, dm0, shard_major=False, tm=512, tn=D_MODEL, name="dw_attn_out",
                                 riders=[red.exchange(["wgu0"]), rider])
    red.exchanged(["wgu0"], res_a)
    red.broadcasted(names, res_b)
    red.grad["wo"] = dw_o.reshape(N_CHIPS, Q_WIDTH // N_CHIPS, D_MODEL)
    (dq, dkc, dkp, dvc, dvp, dsink), (res_a, res_b) = attn_bwd(
        qkv, sink_rows, do, name="attn_bwd", riders=[red.scatter(["wgu0"]), red.exchange(["wo"])])
    red.scattered(["wgu0"], res_a, where)
    red.exchanged(["wo"], res_b)
    names, rider = red.broadcast([("wgu", 0)])
    (dqkv, db_qkv), (res,) = rope_bwd(dq, dkc, dkp, dvc, dvp, cos, sin, name="rope_bwd", riders=[rider])
    red.broadcasted(names, res)
    red.grad["qkv"], (res,) = mm_tn(h0, dqkv, shard_major=True, tm=D_MODEL, tn=QKV_WIDTH // N_CHIPS, name="dw_qkv",
                                    riders=[red.scatter(["wo"])])
    red.scattered(["wo"], res, where)
    grad_x, dg_mpre0 = dh_norm_bwd_last(dqkv, w_qkv, dx1, x0, gain(norm_mix_pre, 0), name="dh_attn_norm_in")

    norm_grads = [jnp.concatenate(p, axis=0) for p in
                  ((dg_mpre0, dg_mpre1), (dg_mpost0, dg_mpost1), (dg_fpre0, dg_fpre1), (dg_fpost0, dg_fpost1))]
    red.grad["small"] = _pack_small(norm_grads, db_qkv, db_o, dsink[:, :, 0, 0], db_sp[:, :, 0], dln_g, dln_b, dw_sp,
                                    loss_part)
    def big_update(w, g, m, v, tag, after=None):
        return adamw(w, g.reshape(w.shape), m, v, name=f"adamw_{tag}", after=after)

    (res,) = comm_call([red.exchange(["qkv", "small"])], name="tail_1")
    red.exchanged(["qkv", "small"], res)
    state, token = comm_start([red.scatter(["qkv", "small"])], name="tail_2_start")
    upd_wgu = big_update(ffn_w_gate_up, red.dest["wgu"], m_ffn_w_gate_up, v_ffn_w_gate_up, "wgu", after=token)
    (res,) = comm_wait(state, upd_wgu[1], name="tail_2_wait")
    red.scattered(["qkv", "small"], res, where)
    names, rider = red.broadcast([("qkv", None), ("wo", None)])
    state, token = comm_start([rider, allcast_rider(red.dest["slab"])], name="tail_3_start")
    upd_wd = big_update(ffn_w_down, red.dest["wd"], m_ffn_w_down, v_ffn_w_down, "wd", after=token)
    res, (slab_full,) = comm_wait(state, upd_wd[1], name="tail_3_wait")
    red.broadcasted(names, res)
    g_qkv, g_wo, g_win, g_wout = (red.dest[n] for n in ("qkv", "wo", "win", "wout"))
    g_norms, g_bqkv, g_bo, g_sinks, g_bsp, g_lng, g_lnb, g_wsp, loss = _unpack_small(slab_full, chip)

    upd = {
        "attn_w_qkv": big_update(attn_w_qkv, g_qkv, m_attn_w_qkv, v_attn_w_qkv, "qkv"),
        "attn_w_o": big_update(attn_w_o, g_wo, m_attn_w_o, v_attn_w_o, "wo"),
        "sgu_w_in": big_update(sgu_w_in, g_win, m_sgu_w_in, v_sgu_w_in, "win"),
        "sgu_w_out": big_update(sgu_w_out, g_wout, m_sgu_w_out, v_sgu_w_out, "wout"),
        "ffn_w_gate_up": upd_wgu,
        "ffn_w_down": upd_wd,
    }
    small_names = ["norm_mix_pre", "norm_mix_post", "norm_ffn_pre", "norm_ffn_post", "attn_b_qkv", "attn_sinks", "attn_b_o",
                   "sgu_ln_g", "sgu_ln_b", "sgu_w_spatial", "sgu_b_spatial"]
    small_w = [norm_mix_pre, norm_mix_post, norm_ffn_pre, norm_ffn_post, attn_b_qkv, attn_sinks, attn_b_o, sgu_ln_g, sgu_ln_b,
               sgu_w_spatial, sgu_b_spatial]
    small_m = [m_norm_mix_pre, m_norm_mix_post, m_norm_ffn_pre, m_norm_ffn_post, m_attn_b_qkv, m_attn_sinks, m_attn_b_o,
               m_sgu_ln_g, m_sgu_ln_b, m_sgu_w_spatial, m_sgu_b_spatial]
    small_v = [v_norm_mix_pre, v_norm_mix_post, v_norm_ffn_pre, v_norm_ffn_post, v_attn_b_qkv, v_attn_sinks, v_attn_b_o,
               v_sgu_ln_g, v_sgu_ln_b, v_sgu_w_spatial, v_sgu_b_spatial]
    small_g = g_norms + [g_bqkv, g_sinks, g_bo, g_lng, g_lnb, g_wsp, g_bsp]

    def flat2(a):
        return a.reshape(-1, a.shape[-1])

    res = adamw_small([flat2(a) for a in small_w], [flat2(a) for a in small_g], [flat2(a) for a in small_m],
                      [flat2(a) for a in small_v], name="adamw_small")
    for i, nm in enumerate(small_names):
        upd[nm] = tuple(r[i].reshape(small_w[i].shape) for r in res)

    order = ["norm_mix_pre", "norm_mix_post", "norm_ffn_pre", "norm_ffn_post", "attn_w_qkv", "attn_b_qkv", "attn_sinks",
             "attn_w_o", "attn_b_o", "sgu_w_in", "sgu_ln_g", "sgu_ln_b", "sgu_w_spatial", "sgu_b_spatial", "sgu_w_out",
             "ffn_w_gate_up", "ffn_w_down"]
    outs = [loss, grad_x.reshape(1, s, D_MODEL)]
    for part in range(4):
        outs += [upd[nm][part] for nm in order]
    return tuple(outs)
```

```python
import types

import numpy as np
import jax
import jax.numpy as jnp
from jax import lax
from jax.experimental import pallas as pl
from jax.experimental.pallas import tpu as pltpu

F32 = jnp.float32
BF16 = jnp.bfloat16
I32 = jnp.int32

D_MODEL = 1024
HEAD_DIM = 64
N_Q_HEADS = 16
N_KV_HEADS = 4
GQA_GROUP = 4
WINDOW = 128
Q_WIDTH = 1024
KV_WIDTH = 256
QKV_WIDTH = 1536
ROPE_THETA = 10000.0
SGU_GROUPS = 8
SGU_CHUNK = 128
D_FF = 2816
FF_HALF = D_FF // 2
EPS = 1e-6
N_CHIPS = 4
LANES = 128

ADAM_LR = 0.001
ADAM_B1 = 0.9
ADAM_B2 = 0.999
ADAM_EPS = 1e-08
ADAM_WD = 0.01
ADAM_STEP = 10

VMEM_LIMIT = 52 * 1024 * 1024
BIG_VMEM_LIMIT = 62 * 1024 * 1024
SUB_ROWS = 256
MESH = pl.DeviceIdType.MESH
NEG = -1e30
NT_DIMS = (((1,), (1,)), ((), ()))
TN_DIMS = (((0,), (0,)), ((), ()))
NN_DIMS = (((1,), (0,)), ((), ()))
ANY = pl.BlockSpec(memory_space=pl.ANY)


def _row_tile(s, want):
    return want if s % want == 0 else s


PEER_KINDS = ("sibling", "chips", "sibling+chips", "everyone")


def _peer_kind(riders):
    kinds = {r.peers for r in riders}
    if not kinds:
        return None
    if "everyone" in kinds:
        return "everyone"
    return "sibling+chips" if len(kinds) == 2 else kinds.pop()


def _peer_barrier(kind):
    x, y, c = _place()
    chips = [(*_partner(x, y, k), c) for k in (1, 2, 3)]
    peers = {"sibling": [(x, y, 1 - c)], "chips": chips, "sibling+chips": [(x, y, 1 - c)] + chips,
             "everyone": [(x, y, 1 - c)] + chips + [(px, py, 1 - c) for px, py, _ in chips]}[kind]
    barrier = pltpu.get_barrier_semaphore()
    for dev in peers:
        pl.semaphore_signal(barrier, inc=1, device_id=dev, device_id_type=MESH)
    pl.semaphore_wait(barrier, len(peers))


def _call(body, *, name, grid=(), in_specs=(), out_specs=(), out_shape=(), scratch_shapes=(), operands=(), prefetch=(),
          aliases=None, riders=(), sem=None, vmem_limit=VMEM_LIMIT):
    n_pre, n_in, n_out, n_scr = len(prefetch), len(operands), len(out_shape), len(scratch_shapes)
    in_specs, out_specs, out_shape = list(in_specs), list(out_specs), list(out_shape)
    operands, scratch_shapes = list(operands), list(scratch_shapes)
    io_alias = {n_pre + i: o for i, o in (aliases or {}).items()}
    for r in riders:
        base_in, base_out = n_pre + len(operands), len(out_shape)
        operands += list(r.inputs)
        in_specs += [ANY] * len(r.inputs)
        for pos, i in enumerate(r.aliased):
            io_alias[base_in + i] = base_out + pos
            out_shape.append(jax.ShapeDtypeStruct(r.inputs[i].shape, r.inputs[i].dtype))
        out_shape += list(r.fresh)
        out_specs += [ANY] * (len(r.aliased) + len(r.fresh))
        scratch_shapes += [pltpu.SemaphoreType.DMA((r.nsem,)), pltpu.SemaphoreType.DMA((r.nsem,))]

    def wrapped(*refs):
        pre, p = refs[:n_pre], n_pre
        core_in, p = refs[p:p + n_in], p + n_in
        r_in = []
        for r in riders:
            r_in.append(refs[p:p + len(r.inputs)])
            p += len(r.inputs)
        core_out, p = refs[p:p + n_out], p + n_out
        r_out = []
        for r in riders:
            k = len(r.aliased) + len(r.fresh)
            r_out.append(refs[p:p + k])
            p += k
        core_scr, p = refs[p:p + n_scr], p + n_scr
        r_sem = [refs[p + 2 * i:p + 2 * i + 2] for i in range(len(riders))]

        def edge(at_last, fns):
            def run():
                if not at_last:
                    _peer_barrier(peer_kind)
                for i, r in enumerate(riders):
                    getattr(r, fns)(r_in[i], r_out[i], r_sem[i][0], r_sem[i][1])
            if not riders:
                return
            if not grid:
                run()
                return
            cond = None
            for d, n in enumerate(grid):
                c = pl.program_id(d) == (n - 1 if at_last else 0)
                cond = c if cond is None else jnp.logical_and(cond, c)
            pl.when(cond)(run)

        edge(False, "start")
        if body is not None:
            body(*pre, *core_in, *core_out, *core_scr)
        edge(True, "finish")

    if sem is None or riders:
        sem = ("arbitrary",) * len(grid)
    kwargs = dict(out_shape=out_shape, input_output_aliases=io_alias, name=name)
    peer_kind = _peer_kind(riders)
    collective = {} if peer_kind is None else {"collective_id": PEER_KINDS.index(peer_kind)}
    if grid:
        kwargs["compiler_params"] = pltpu.CompilerParams(dimension_semantics=sem, vmem_limit_bytes=vmem_limit, **collective)
    elif collective:
        kwargs["compiler_params"] = pltpu.CompilerParams(**collective)
    if n_pre:
        kwargs["grid_spec"] = pltpu.PrefetchScalarGridSpec(
            num_scalar_prefetch=n_pre, grid=grid, in_specs=in_specs, out_specs=out_specs, scratch_shapes=scratch_shapes)
    else:
        kwargs.update(grid=grid, in_specs=in_specs, out_specs=out_specs, scratch_shapes=scratch_shapes)
    res = pl.pallas_call(wrapped, **kwargs)(*prefetch, *operands)
    core, rest, rider_res = list(res[:n_out]), list(res[n_out:]), []
    for r in riders:
        k = len(r.aliased) + len(r.fresh)
        rider_res.append(rest[:k])
        rest = rest[k:]
    return core, rider_res


def _mm_call(*, grid, in_specs, out_spec, out_shape, dims, nk, kaxis, acc_shape, name, operands, riders=()):
    out_dtype = out_shape.dtype

    def body(a_ref, b_ref, o_ref, *scratch):
        p = lax.dot_general(a_ref[...].astype(BF16), b_ref[...].astype(BF16), dims, preferred_element_type=F32)
        if nk == 1:
            o_ref[...] = p.astype(out_dtype)
        else:
            acc = scratch[0]
            kk = pl.program_id(kaxis)

            @pl.when(kk == 0)
            def _():
                acc[...] = p

            @pl.when(kk > 0)
            def _():
                acc[...] += p

            @pl.when(kk == nk - 1)
            def _():
                o_ref[...] = acc[...].astype(out_dtype)

    sem = ["parallel"] * len(grid)
    if nk > 1:
        sem[kaxis] = "arbitrary"
    (out,), rider_res = _call(
        body, grid=grid, in_specs=in_specs, out_specs=[out_spec], out_shape=[out_shape],
        scratch_shapes=[pltpu.VMEM(acc_shape, F32)] if nk > 1 else [], operands=operands, name=name, riders=riders,
        sem=tuple(sem))
    return (out, rider_res) if riders else out


def mm_nt(a, w, *, out_dtype, name, tm=1024, riders=()):
    m, n = a.shape
    kout = w.shape[0]
    tm = _row_tile(m, tm)
    return _mm_call(grid=(m // tm,),
                    in_specs=[pl.BlockSpec((tm, n), lambda i: (i, 0)), pl.BlockSpec((kout, n), lambda i: (0, 0))],
                    out_spec=pl.BlockSpec((tm, kout), lambda i: (i, 0)),
                    out_shape=jax.ShapeDtypeStruct((m, kout), out_dtype), dims=NT_DIMS, nk=1, kaxis=0,
                    acc_shape=None, name=name, operands=(a, w), riders=riders)


def mm_tn(a, b, *, shard_major, name, tm, tn, tk=None, out_dtype=BF16, riders=()):
    s, m = a.shape
    tk = s if tk is None else _row_tile(s, tk)
    if b.ndim == 3:
        n = 2 * b.shape[2]
        b_spec = pl.BlockSpec((None, tk, tn), lambda j, i, kk: (j // 2, kk, j % 2))
    else:
        n = b.shape[1]
        b_spec = pl.BlockSpec((tk, tn), lambda j, i, kk: (kk, j))
    if shard_major:
        assert tn == n // N_CHIPS
        o_spec = pl.BlockSpec((None, tm, tn), lambda j, i, kk: (j, i, 0))
        o_shape = jax.ShapeDtypeStruct((N_CHIPS, m, tn), out_dtype)
    else:
        o_spec = pl.BlockSpec((tm, tn), lambda j, i, kk: (i, j))
        o_shape = jax.ShapeDtypeStruct((m, n), out_dtype)
    return _mm_call(grid=(n // tn, m // tm, s // tk),
                    in_specs=[pl.BlockSpec((tk, tm), lambda j, i, kk: (kk, i)), b_spec], out_spec=o_spec,
                    out_shape=o_shape, dims=TN_DIMS, nk=s // tk, kaxis=2, acc_shape=(tm, tn), name=name, operands=(a, b),
                    riders=riders)


def _rstd(x):
    return lax.rsqrt(jnp.mean(x * x, axis=-1, keepdims=True) + EPS)


def _rms_bwd(dy, x, g):
    r = _rstd(x)
    xhat = x * r
    gy = dy * g
    dx = r * (gy - xhat * jnp.mean(gy * xhat, axis=-1, keepdims=True))
    return dx, jnp.sum(dy * xhat, axis=0, keepdims=True)


def _accum(ref, val, first):
    @pl.when(first)
    def _():
        ref[...] = val

    @pl.when(jnp.logical_not(first))
    def _():
        ref[...] += val


def _row_spec(tm, width):
    return pl.BlockSpec((tm, width), lambda i: (i, 0))


def _vec_spec(width):
    return pl.BlockSpec((1, width), lambda i: (0, 0))


def _ret(core, rider_res, riders):
    core = core[0] if len(core) == 1 else core
    return (core, rider_res) if riders else core


def prenorm_and_place(x, g, w, layer, chip_arr, *, name, tm=256, riders=()):
    s = x.shape[0]
    tm = _row_tile(s, tm)
    steps = s // tm
    _, r, c = w.shape
    tr = r // steps
    assert tr * steps == r and tr % 16 == 0

    def body(chip_ref, x_ref, g_ref, w_ref, h_ref, o_ref):
        xv = x_ref[...]
        h_ref[...] = (xv * _rstd(xv) * g_ref[...]).astype(BF16)
        o_ref[...] = w_ref[...].astype(BF16)

    core, rr = _call(
        body, grid=(steps,), prefetch=(chip_arr,),
        in_specs=[pl.BlockSpec((tm, D_MODEL), lambda i, chip: (i, 0)), pl.BlockSpec((1, D_MODEL), lambda i, chip: (0, 0)),
                  pl.BlockSpec((None, tr, c), lambda i, chip: (layer, i, 0))],
        out_specs=[pl.BlockSpec((tm, D_MODEL), lambda i, chip: (i, 0)), pl.BlockSpec((None, tr, c), lambda i, chip: (chip[0], i, 0))],
        out_shape=[jax.ShapeDtypeStruct((s, D_MODEL), BF16), jax.ShapeDtypeStruct((N_CHIPS, r, c), BF16)],
        operands=(x, g, w), sem=("parallel",), name=name, riders=riders)
    return _ret(core, rr, riders)


def proj_residual_norm(a, w, x, bias, g_post, g_next, *, name, tm=512, sub=256, riders=()):
    s, k = a.shape
    tm = _row_tile(s, tm)
    sub = min(sub, tm)

    def body(a_ref, w_ref, x_ref, b_ref, gp_ref, gn_ref, xo_ref, h_ref, m_ref):
        for t in range(tm // sub):
            rows = slice(t * sub, (t + 1) * sub)
            mv = jnp.dot(a_ref[rows, :], w_ref[...], preferred_element_type=F32) + b_ref[...]
            m_ref[rows, :] = mv.astype(BF16)
            xn = x_ref[rows, :] + mv * _rstd(mv) * gp_ref[...]
            xo_ref[rows, :] = xn
            h_ref[rows, :] = (xn * _rstd(xn) * gn_ref[...]).astype(BF16)

    row, vec = _row_spec(tm, D_MODEL), _vec_spec(D_MODEL)
    core, rr = _call(
        body, grid=(s // tm,),
        in_specs=[_row_spec(tm, k), pl.BlockSpec((k, D_MODEL), lambda i: (0, 0)), row, vec, vec, vec], out_specs=[row, row, row],
        out_shape=[jax.ShapeDtypeStruct((s, D_MODEL), F32), jax.ShapeDtypeStruct((s, D_MODEL), BF16),
                   jax.ShapeDtypeStruct((s, D_MODEL), BF16)],
        operands=(a, w, x, bias, g_post, g_next), sem=("parallel",), name=name, riders=riders)
    return _ret(core, rr, riders)


def ffn_fwd_loss_rows(h, w_gu, w_d, x, g_post, target, *, name, tm=512, riders=()):
    s = x.shape[0]
    tm = _row_tile(s, tm)

    def body(h_ref, w0, w1, w2, w3, wd_ref, x_ref, g_ref, t_ref, d_ref, a_ref, dx_ref, df_ref, dg_ref, loss_ref):
        first = pl.program_id(0) == 0
        halves = [slice(half * FF_HALF, (half + 1) * FF_HALF) for half in (0, 1)]
        gain = g_ref[...]
        sub = min(SUB_ROWS, tm)
        sums = None
        for t in range(tm // sub):
            rows = slice(t * sub, (t + 1) * sub)
            hv = h_ref[rows, :]
            fv = None
            for cols, (wg_ref, wu_ref) in zip(halves, ((w0, w2), (w1, w3))):
                g = jnp.dot(hv, wg_ref[...], preferred_element_type=F32)
                u = jnp.dot(hv, wu_ref[...], preferred_element_type=F32)
                sig = _sigmoid(g)
                silu = g * sig
                d_ref[0, rows, cols] = (u * (sig + silu * (1.0 - sig))).astype(BF16)
                d_ref[1, rows, cols] = silu.astype(BF16)
                act = (silu * u).astype(BF16)
                a_ref[rows, cols] = act
                p = jnp.dot(act, wd_ref[cols, :], preferred_element_type=F32)
                fv = p if fv is None else fv + p
            err = x_ref[rows, :] + fv * _rstd(fv) * gain - t_ref[rows, :]
            dx = err * (1.0 / D_MODEL)
            dx_ref[rows, :] = dx
            df, dg = _rms_bwd(dx, fv, gain)
            df_ref[rows, :] = df.astype(BF16)
            part = (dg, jnp.sum(jnp.sum(err * err, axis=-1, keepdims=True), axis=0, keepdims=True) * (0.5 / D_MODEL))
            sums = part if sums is None else tuple(a + b for a, b in zip(sums, part))
        _accum(dg_ref, sums[0], first)
        _accum(loss_ref, jnp.broadcast_to(sums[1], (8, LANES)), first)

    def resident(shape, index):
        return pl.BlockSpec(shape, index, pipeline_mode=pl.Buffered(1))

    row, vec = _row_spec(tm, D_MODEL), _vec_spec(D_MODEL)
    shards = [resident((None, D_MODEL, FF_HALF), (lambda j: (lambda i: (j, 0, 0)))(j)) for j in range(N_CHIPS)]
    core, rr = _call(
        body, grid=(s // tm,),
        in_specs=[row] + shards + [resident((D_FF, D_MODEL), lambda i: (0, 0)), row, vec, row],
        out_specs=[pl.BlockSpec((2, tm, D_FF), lambda i: (0, i, 0)), _row_spec(tm, D_FF), row, row, vec,
                   pl.BlockSpec((8, LANES), lambda i: (0, 0))],
        out_shape=[jax.ShapeDtypeStruct((2, s, D_FF), BF16), jax.ShapeDtypeStruct((s, D_FF), BF16),
                   jax.ShapeDtypeStruct((s, D_MODEL), F32), jax.ShapeDtypeStruct((s, D_MODEL), BF16),
                   jax.ShapeDtypeStruct((1, D_MODEL), F32), jax.ShapeDtypeStruct((8, LANES), F32)],
        operands=(h, w_gu, w_gu, w_gu, w_gu, w_d, x, g_post, target), name=name, riders=riders, vmem_limit=BIG_VMEM_LIMIT)
    return _ret(core, rr, riders)


def dh_norm_bwd_pair(a, w, dres, x, g_pre, m, g_post, *, name, tm=512, sub=256, riders=()):
    _, kout, ns = w.shape
    planes = a.ndim == 3
    s = x.shape[0]
    tm = _row_tile(s, tm)
    sub = min(sub, tm)
    a_spec = pl.BlockSpec((2, tm, 2 * ns), lambda i: (0, i, 0)) if planes else pl.BlockSpec((tm, N_CHIPS * ns), lambda i: (i, 0))

    def body(a_ref, w0, w1, w2, w3, dres_ref, x_ref, gpre_ref, m_ref, gpost_ref, dx_ref, dm_ref, dgpre_ref, dgpost_ref, db_ref):
        first = pl.program_id(0) == 0
        sums = None
        for t in range(tm // sub):
            rows = slice(t * sub, (t + 1) * sub)
            dh = None
            for j, w_ref in enumerate((w0, w1, w2, w3)):
                a_j = a_ref[j // 2, rows, (j % 2) * ns:(j % 2 + 1) * ns] if planes else a_ref[rows, j * ns:(j + 1) * ns]
                p = lax.dot_general(a_j, w_ref[...], NT_DIMS, preferred_element_type=F32)
                dh = p if dh is None else dh + p
            d1, dgpre = _rms_bwd(dh, x_ref[rows, :], gpre_ref[...])
            dx = dres_ref[rows, :] + d1
            dx_ref[rows, :] = dx
            dm, dgpost = _rms_bwd(dx, m_ref[rows, :].astype(F32), gpost_ref[...])
            dm_ref[rows, :] = dm.astype(BF16)
            part = (dgpre, dgpost, jnp.sum(dm, axis=0, keepdims=True))
            sums = part if sums is None else tuple(u + v for u, v in zip(sums, part))
        _accum(dgpre_ref, sums[0], first)
        _accum(dgpost_ref, sums[1], first)
        _accum(db_ref, sums[2], first)

    def shard(j):
        return pl.BlockSpec((None, kout, ns), lambda i: (j, 0, 0))

    row, vec = _row_spec(tm, D_MODEL), _vec_spec(D_MODEL)
    vshape = jax.ShapeDtypeStruct((1, D_MODEL), F32)
    core, rr = _call(
        body, grid=(s // tm,), in_specs=[a_spec] + [shard(j) for j in range(N_CHIPS)] + [row, row, vec, row, vec],
        out_specs=[row, row, vec, vec, vec],
        out_shape=[jax.ShapeDtypeStruct((s, D_MODEL), F32), jax.ShapeDtypeStruct((s, D_MODEL), BF16), vshape, vshape, vshape],
        operands=(a, w, w, w, w, dres, x, g_pre, m, g_post), name=name, riders=riders)
    return _ret(core, rr, riders)


def ffn_bwd_rows(df, w_d, d_planes, w_gu, dres, x, g_pre, m, g_post, *, name, tm=512, riders=()):
    s = x.shape[0]
    tm = _row_tile(s, tm)

    def body(df_ref, wd_ref, d_ref, w0, w1, w2, w3, dres_ref, x_ref, gpre_ref, m_ref, gpost_ref,
             o_ref, dx_ref, dm_ref, dgpre_ref, dgpost_ref, db_ref):
        first = pl.program_id(0) == 0
        halves = [slice(half * FF_HALF, (half + 1) * FF_HALF) for half in (0, 1)]
        sub = min(SUB_ROWS, tm)
        sums = None
        for t in range(tm // sub):
            rows = slice(t * sub, (t + 1) * sub)
            dfv = df_ref[rows, :]
            dh = None
            for cols, (wg_ref, wu_ref) in zip(halves, ((w0, w2), (w1, w3))):
                da = lax.dot_general(dfv, wd_ref[cols, :], NT_DIMS, preferred_element_type=F32)
                dg = (da * d_ref[0, rows, cols].astype(F32)).astype(BF16)
                du = (da * d_ref[1, rows, cols].astype(F32)).astype(BF16)
                o_ref[0, rows, cols] = dg
                o_ref[1, rows, cols] = du
                p = lax.dot_general(dg, wg_ref[...], NT_DIMS, preferred_element_type=F32)
                p += lax.dot_general(du, wu_ref[...], NT_DIMS, preferred_element_type=F32)
                dh = p if dh is None else dh + p
            d1, dgpre = _rms_bwd(dh, x_ref[rows, :], gpre_ref[...])
            dx = dres_ref[rows, :] + d1
            dx_ref[rows, :] = dx
            dm, dgpost = _rms_bwd(dx, m_ref[rows, :].astype(F32), gpost_ref[...])
            dm_ref[rows, :] = dm.astype(BF16)
            part = (dgpre, dgpost, jnp.sum(dm, axis=0, keepdims=True))
            sums = part if sums is None else tuple(a + b for a, b in zip(sums, part))
        _accum(dgpre_ref, sums[0], first)
        _accum(dgpost_ref, sums[1], first)
        _accum(db_ref, sums[2], first)

    def resident(shape, index):
        return pl.BlockSpec(shape, index, pipeline_mode=pl.Buffered(1))

    planes = pl.BlockSpec((2, tm, D_FF), lambda i: (0, i, 0))
    row, vec = _row_spec(tm, D_MODEL), _vec_spec(D_MODEL)
    vshape = jax.ShapeDtypeStruct((1, D_MODEL), F32)
    shards = [resident((None, D_MODEL, FF_HALF), (lambda j: (lambda i: (j, 0, 0)))(j)) for j in range(N_CHIPS)]
    core, rr = _call(
        body, grid=(s // tm,),
        in_specs=[row, resident((D_FF, D_MODEL), lambda i: (0, 0)), planes] + shards + [row, row, vec, row, vec],
        out_specs=[planes, row, row, vec, vec, vec],
        out_shape=[jax.ShapeDtypeStruct((2, s, D_FF), BF16), jax.ShapeDtypeStruct((s, D_MODEL), F32),
                   jax.ShapeDtypeStruct((s, D_MODEL), BF16), vshape, vshape, vshape],
        operands=(df, w_d, d_planes, w_gu, w_gu, w_gu, w_gu, dres, x, g_pre, m, g_post), name=name, riders=riders,
        vmem_limit=BIG_VMEM_LIMIT)
    return _ret(core, rr, riders)


def dh_norm_bwd_last(a, w, dres, x, g_pre, *, name, tm=512, sub=256):
    _, kout, ns = w.shape
    s = x.shape[0]
    tm = _row_tile(s, tm)
    sub = min(sub, tm)

    def body(a_ref, w0, w1, w2, w3, dres_ref, x_ref, g_ref, dx_ref, dg_ref):
        total = None
        for t in range(tm // sub):
            rows = slice(t * sub, (t + 1) * sub)
            dh = None
            for j, w_ref in enumerate((w0, w1, w2, w3)):
                p = lax.dot_general(a_ref[rows, j * ns:(j + 1) * ns], w_ref[...], NT_DIMS, preferred_element_type=F32)
                dh = p if dh is None else dh + p
            d1, dg = _rms_bwd(dh, x_ref[rows, :], g_ref[...])
            dx_ref[rows, :] = dres_ref[rows, :] + d1
            total = dg if total is None else total + dg
        _accum(dg_ref, total, pl.program_id(0) == 0)

    def shard(j):
        return pl.BlockSpec((None, kout, ns), lambda i: (j, 0, 0))

    row, vec = _row_spec(tm, D_MODEL), _vec_spec(D_MODEL)
    (dx, dg), _ = _call(
        body, grid=(s // tm,), in_specs=[_row_spec(tm, N_CHIPS * ns)] + [shard(j) for j in range(N_CHIPS)] + [row, row, vec],
        out_specs=[row, vec], out_shape=[jax.ShapeDtypeStruct((s, D_MODEL), F32), jax.ShapeDtypeStruct((1, D_MODEL), F32)],
        operands=(a, w, w, w, w, dres, x, g_pre), name=name)
    return dx, dg


def _rope_tables(s):
    half = HEAD_DIM // 2
    inv_freq = np.float32(ROPE_THETA) ** (-(np.arange(half, dtype=np.float32) * np.float32(2.0)) / np.float32(HEAD_DIM))
    ang = np.arange(s, dtype=np.float32)[:, None] * inv_freq[None, :]
    cos, sin = np.cos(ang).astype(np.float32), np.sin(ang).astype(np.float32)
    return jnp.asarray(np.tile(cos, (1, 4))), jnp.asarray(np.concatenate([-sin, sin, -sin, sin], axis=1))


def _swap_halves(x):
    lane = lax.broadcasted_iota(I32, x.shape, 1)
    return jnp.where((lane & (HEAD_DIM - 1)) < HEAD_DIM // 2, pltpu.roll(x, LANES - 32, 1), pltpu.roll(x, 32, 1))


N_ROPE_BLOCKS = (Q_WIDTH + KV_WIDTH) // LANES


def qkv_proj(h, w, bias, cos, sin, casts, chip_arr, *, name, tm=1024, riders=()):
    s, k = h.shape
    ns = w.shape[2]
    tm = _row_tile(s, tm)
    nc = len(casts)

    def body(chip_ref, h_ref, w_ref, b_ref, c_ref, s_ref, *rest):
        cast_in, o_ref, cast_out = rest[:nc], rest[nc], rest[nc + 1:]
        j = pl.program_id(0)

        @pl.when(jnp.logical_and(j == 0, pl.program_id(1) == 0))
        def _():
            for src, dst in zip(cast_in, cast_out):
                dst[...] = src[...].astype(BF16)

        sub = min(256, tm)
        for t in range(tm // sub):
            rows = slice(t * sub, (t + 1) * sub)
            p = jnp.dot(h_ref[rows, :], w_ref[...], preferred_element_type=F32) + b_ref[...]
            cosv, sinv = c_ref[rows, :], s_ref[rows, :]
            for blk in range(ns // LANES):
                xb = p[:, blk * LANES:(blk + 1) * LANES]
                roped = xb * cosv + _swap_halves(xb) * sinv
                is_qk = j * (ns // LANES) + blk < N_ROPE_BLOCKS
                o_ref[rows, blk * LANES:(blk + 1) * LANES] = jnp.where(is_qk, roped, xb).astype(BF16)

    def cast_in_spec(wt, layer):
        return pl.BlockSpec((None,) + wt.shape[1:], lambda j, i, chip: (layer, 0, 0), pipeline_mode=pl.Buffered(1))

    def cast_out_spec(wt):
        return pl.BlockSpec((None,) + wt.shape[1:], lambda j, i, chip: (chip[0], 0, 0), pipeline_mode=pl.Buffered(1))

    core, rr = _call(
        body, grid=(N_CHIPS, s // tm), prefetch=(chip_arr,),
        in_specs=[pl.BlockSpec((tm, k), lambda j, i, chip: (i, 0)), pl.BlockSpec((None, k, ns), lambda j, i, chip: (j, 0, 0)),
                  pl.BlockSpec((1, ns), lambda j, i, chip: (0, j)), pl.BlockSpec((tm, LANES), lambda j, i, chip: (i, 0)),
                  pl.BlockSpec((tm, LANES), lambda j, i, chip: (i, 0))] + [cast_in_spec(wt, layer) for wt, layer in casts],
        out_specs=[pl.BlockSpec((tm, ns), lambda j, i, chip: (i, j))] + [cast_out_spec(wt) for wt, _ in casts],
        out_shape=[jax.ShapeDtypeStruct((s, N_CHIPS * ns), BF16)]
        + [jax.ShapeDtypeStruct((N_CHIPS,) + wt.shape[1:], BF16) for wt, _ in casts],
        operands=(h, w, bias, cos, sin, *[wt for wt, _ in casts]), sem=("arbitrary", "arbitrary"), name=name, riders=riders)
    return (core, rr) if riders else core


def rope_bwd(dq, dkc, dkp, dvc, dvp, cos, sin, *, name, riders=()):
    s = dq.shape[0]
    tm = 2 * WINDOW if s % (2 * WINDOW) == 0 else WINDOW
    nb = s // tm

    def body(dq_ref, dkc_ref, dkp_ref, dkp_next_ref, dvc_ref, dvp_ref, dvp_next_ref, c_ref, s_ref, o_ref, db_ref):
        i = pl.program_id(0)
        has_next = (i < nb - 1).astype(F32)
        cosv, sinv = c_ref[...], s_ref[...]

        def shifted(ref, next_ref, cols):
            last = has_next * next_ref[:WINDOW, cols].astype(F32)
            return last if tm == WINDOW else jnp.concatenate([ref[WINDOW:, cols].astype(F32), last], axis=0)

        parts = []
        for blk in range(QKV_WIDTH // LANES):
            if blk < Q_WIDTH // LANES:
                g = dq_ref[:, blk * LANES:(blk + 1) * LANES].astype(F32)
            else:
                own, prv, nxt = (dkc_ref, dkp_ref, dkp_next_ref) if blk < N_ROPE_BLOCKS else (dvc_ref, dvp_ref, dvp_next_ref)
                cols = slice((blk % 2) * LANES, (blk % 2 + 1) * LANES)
                g = own[:, cols].astype(F32) + shifted(prv, nxt, cols)
            if blk < N_ROPE_BLOCKS:
                g = g * cosv + _swap_halves(g * sinv)
            o_ref[:, blk * LANES:(blk + 1) * LANES] = g.astype(BF16)
            parts.append(jnp.sum(g, axis=0, keepdims=True))
        sums = jnp.concatenate(parts, axis=1)
        _accum(db_ref, sums, i == 0)

    own_spec = _row_spec(tm, KV_WIDTH)
    next_spec = pl.BlockSpec((tm, KV_WIDTH), lambda i: (jnp.minimum(i + 1, nb - 1), 0))
    core, rr = _call(
        body, grid=(nb,),
        in_specs=[_row_spec(tm, Q_WIDTH), own_spec, own_spec, next_spec, own_spec, own_spec, next_spec,
                  _row_spec(tm, LANES), _row_spec(tm, LANES)],
        out_specs=[_row_spec(tm, QKV_WIDTH), _vec_spec(QKV_WIDTH)],
        out_shape=[jax.ShapeDtypeStruct((s, QKV_WIDTH), BF16), jax.ShapeDtypeStruct((1, QKV_WIDTH), F32)],
        operands=(dq, dkc, dkp, dkp, dvc, dvp, dvp, cos, sin), name=name, riders=riders)
    return _ret(core, rr, riders)


ROWS = GQA_GROUP * WINDOW


def _prev_slots():
    kpos = lax.broadcasted_iota(I32, (WINDOW, ROWS), 0)
    qpos = lax.broadcasted_iota(I32, (WINDOW, ROWS), 1) & (WINDOW - 1)
    return kpos > qpos


def _head_cols(ref, head):
    return ref[:, head * HEAD_DIM:(head + 1) * HEAD_DIM]


def _stack_heads(ref, h):
    return jnp.concatenate([_head_cols(ref, GQA_GROUP * h + g) for g in range(GQA_GROUP)], axis=0)


def _band(prev_ref, cur_ref, h):
    return jnp.concatenate([_head_cols(prev_ref, h), _head_cols(cur_ref, h)], axis=0)


def _pick(prev, band):
    return jnp.where(prev, band[:WINDOW], band[WINDOW:])


def _spread(prev, x):
    return jnp.concatenate([jnp.where(prev, x, 0.0), jnp.where(prev, 0.0, x)], axis=0).astype(BF16)


def _attn_probs(s_band, sink, prev, has_prev):
    scale = HEAD_DIM ** -0.5
    s = jnp.where(prev, jnp.where(has_prev, s_band[:WINDOW], NEG), s_band[WINDOW:]) * scale
    m = jnp.maximum(jnp.max(s, axis=0, keepdims=True), sink)
    e, es = jnp.exp(s - m), jnp.exp(sink - m)
    inv = 1.0 / (jnp.sum(e, axis=0, keepdims=True) + es)
    return e * inv, es * inv


def _attn_specs(nb):
    kcol, vcol = Q_WIDTH // KV_WIDTH, Q_WIDTH // KV_WIDTH + 1
    q_spec = pl.BlockSpec((WINDOW, Q_WIDTH), lambda n: (n, 0))
    return [q_spec,
            pl.BlockSpec((WINDOW, KV_WIDTH), lambda n: (n, kcol)),
            pl.BlockSpec((WINDOW, KV_WIDTH), lambda n: (jnp.maximum(n - 1, 0), kcol)),
            pl.BlockSpec((WINDOW, KV_WIDTH), lambda n: (n, vcol)),
            pl.BlockSpec((WINDOW, KV_WIDTH), lambda n: (jnp.maximum(n - 1, 0), vcol)),
            pl.BlockSpec((N_KV_HEADS, 8, ROWS), lambda n: (0, 0, 0))]


def attn_fwd(qkv, sink_rows, *, name, riders=()):
    s = qkv.shape[0]

    def body(q_ref, kc_ref, kp_ref, vc_ref, vp_ref, sink_ref, o_ref):
        prev = _prev_slots()
        has_prev = pl.program_id(0) > 0
        heads = range(N_KV_HEADS)
        s_bands = [lax.dot_general(_band(kp_ref, kc_ref, h), _stack_heads(q_ref, h), NT_DIMS, preferred_element_type=F32)
                   for h in heads]
        p_bands = [_spread(prev, _attn_probs(s_bands[h], sink_ref[h, 0:1, :], prev, has_prev)[0]) for h in heads]
        outs = [lax.dot_general(_band(vp_ref, vc_ref, h), p_bands[h], TN_DIMS, preferred_element_type=F32).T for h in heads]
        for h in heads:
            for g in range(GQA_GROUP):
                head = GQA_GROUP * h + g
                o_ref[:, head * HEAD_DIM:(head + 1) * HEAD_DIM] = outs[h][g * WINDOW:(g + 1) * WINDOW].astype(BF16)

    core, rr = _call(
        body, grid=(s // WINDOW,), in_specs=_attn_specs(s // WINDOW), out_specs=[pl.BlockSpec((WINDOW, Q_WIDTH), lambda n: (n, 0))],
        out_shape=[jax.ShapeDtypeStruct((s, Q_WIDTH), BF16)], operands=(qkv, qkv, qkv, qkv, qkv, sink_rows), sem=("parallel",),
        name=name, riders=riders)
    return _ret(core, rr, riders)


def attn_bwd(qkv, sink_rows, do, *, name, riders=()):
    s = qkv.shape[0]

    def body(q_ref, kc_ref, kp_ref, vc_ref, vp_ref, sink_ref, do_ref, dq_ref, dkc_ref, dkp_ref, dvc_ref, dvp_ref, dsink_ref):
        n = pl.program_id(0)
        prev = _prev_slots()
        scale = HEAD_DIM ** -0.5
        heads = range(N_KV_HEADS)
        qs, dos = [_stack_heads(q_ref, h) for h in heads], [_stack_heads(do_ref, h) for h in heads]
        kbands, vbands = [_band(kp_ref, kc_ref, h) for h in heads], [_band(vp_ref, vc_ref, h) for h in heads]
        s_bands = [lax.dot_general(kbands[h], qs[h], NT_DIMS, preferred_element_type=F32) for h in heads]
        dp_bands = [lax.dot_general(vbands[h], dos[h], NT_DIMS, preferred_element_type=F32) for h in heads]
        ds_bands, p_bands, parts = [], [], []
        for h in heads:
            p, ps = _attn_probs(s_bands[h], sink_ref[h, 0:1, :], prev, n > 0)
            dp = _pick(prev, dp_bands[h])
            delta = jnp.sum(p * dp, axis=0, keepdims=True)
            ds_bands.append(_spread(prev, p * (dp - delta) * scale))
            p_bands.append(_spread(prev, p))
            dsink = -(ps * delta)
            for g in range(GQA_GROUP):
                parts.append(jnp.broadcast_to(jnp.sum(dsink[:, g * WINDOW:(g + 1) * WINDOW], axis=1, keepdims=True), (8, LANES)))
        for h in heads:
            dk = jnp.dot(ds_bands[h], qs[h], preferred_element_type=F32).astype(BF16)
            dv = jnp.dot(p_bands[h], dos[h], preferred_element_type=F32).astype(BF16)
            dq = lax.dot_general(kbands[h], ds_bands[h], TN_DIMS, preferred_element_type=F32).T
            cols = slice(h * HEAD_DIM, (h + 1) * HEAD_DIM)
            dkp_ref[:, cols], dkc_ref[:, cols] = dk[:WINDOW], dk[WINDOW:]
            dvp_ref[:, cols], dvc_ref[:, cols] = dv[:WINDOW], dv[WINDOW:]
            for g in range(GQA_GROUP):
                head = GQA_GROUP * h + g
                dq_ref[:, head * HEAD_DIM:(head + 1) * HEAD_DIM] = dq[g * WINDOW:(g + 1) * WINDOW].astype(BF16)

        @pl.when(n == 0)
        def _():
            for i, part in enumerate(parts):
                dsink_ref[i // GQA_GROUP, i % GQA_GROUP] = part

        @pl.when(n > 0)
        def _():
            for i, part in enumerate(parts):
                dsink_ref[i // GQA_GROUP, i % GQA_GROUP] += part

    rows_q = pl.BlockSpec((WINDOW, Q_WIDTH), lambda n: (n, 0))
    rows_kv = pl.BlockSpec((WINDOW, KV_WIDTH), lambda n: (n, 0))
    kv_shape = jax.ShapeDtypeStruct((s, KV_WIDTH), BF16)
    core, rr = _call(
        body, grid=(s // WINDOW,), in_specs=_attn_specs(s // WINDOW) + [rows_q],
        out_specs=[rows_q, rows_kv, rows_kv, rows_kv, rows_kv,
                   pl.BlockSpec((N_KV_HEADS, GQA_GROUP, 8, LANES), lambda n: (0, 0, 0, 0))],
        out_shape=[jax.ShapeDtypeStruct((s, Q_WIDTH), BF16), kv_shape, kv_shape, kv_shape, kv_shape,
                   jax.ShapeDtypeStruct((N_KV_HEADS, GQA_GROUP, 8, LANES), F32)],
        operands=(qkv, qkv, qkv, qkv, qkv, sink_rows, do), sem=("arbitrary",), name=name, riders=riders)
    return _ret(core, rr, riders)


GELU_C = 0.7978845608028654
GELU_A = 0.044715


def _gelu(x):
    return 0.5 * x * (1.0 + jnp.tanh(x * (GELU_C + (GELU_C * GELU_A) * (x * x))))


def _gelu_and_grad(x):
    x2 = x * x
    t = jnp.tanh(x * (GELU_C + (GELU_C * GELU_A) * x2))
    half_x, one_t = 0.5 * x, 1.0 + t
    return half_x * one_t, 0.5 * one_t + half_x * (1.0 - t * t) * (GELU_C + (3.0 * GELU_C * GELU_A) * x2)


def _tril_bf16(w):
    row = lax.broadcasted_iota(I32, (SGU_CHUNK, SGU_CHUNK), 0)
    col = lax.broadcasted_iota(I32, (SGU_CHUNK, SGU_CHUNK), 1)
    return jnp.where(row >= col, w, 0.0).astype(BF16)


def _sgu_norm(vg, g, b):
    mu = jnp.mean(vg, axis=-1, keepdims=True)
    cen = vg - mu
    rstd = lax.rsqrt(jnp.mean(cen * cen, axis=-1, keepdims=True) + EPS)
    xhat = cen * rstd
    return xhat, rstd, xhat * g + b


def sgu_in_fwd(h, w_in, ln_g, ln_b, w_sp, b_sp, *, name, tm=512, riders=()):
    s, k = h.shape
    ns = w_in.shape[2]
    tm = _row_tile(s, tm)

    def body(h_ref, w0, w1, w2, w3, g_ref, b_ref, w_ref, bs_ref, z_ref, y_ref):
        hv = h_ref[...]
        zs = [jnp.dot(hv, w_ref_j[...], preferred_element_type=F32) for w_ref_j in (w0, w1, w2, w3)]
        for j, zj in enumerate(zs):
            z_ref[:, j * ns:(j + 1) * ns] = zj.astype(BF16)
        u = _gelu(jnp.concatenate(zs[:2], axis=1))
        _, _, vn = _sgu_norm(_gelu(jnp.concatenate(zs[2:], axis=1)), g_ref[...], b_ref[...])
        vn = vn.astype(BF16)
        for grp in range(SGU_GROUPS):
            w = _tril_bf16(w_ref[grp])
            cols = slice(grp * LANES, (grp + 1) * LANES)
            for ch in range(tm // SGU_CHUNK):
                rows = slice(ch * SGU_CHUNK, (ch + 1) * SGU_CHUNK)
                mixed = jnp.dot(w, vn[rows, cols], preferred_element_type=F32) + bs_ref[grp]
                y_ref[rows, cols] = (u[rows, cols] * mixed).astype(BF16)

    def shard(j):
        return pl.BlockSpec((None, k, ns), lambda i: (j, 0, 0))

    full3 = pl.BlockSpec((SGU_GROUPS, SGU_CHUNK, SGU_CHUNK), lambda i: (0, 0, 0))
    core, rr = _call(
        body, grid=(s // tm,),
        in_specs=[_row_spec(tm, k)] + [shard(j) for j in range(N_CHIPS)] + [_vec_spec(D_MODEL), _vec_spec(D_MODEL), full3, full3],
        out_specs=[_row_spec(tm, 2 * D_MODEL), _row_spec(tm, D_MODEL)],
        out_shape=[jax.ShapeDtypeStruct((s, 2 * D_MODEL), BF16), jax.ShapeDtypeStruct((s, D_MODEL), BF16)],
        operands=(h, w_in, w_in, w_in, w_in, ln_g, ln_b, w_sp, b_sp), sem=("parallel",), name=name, riders=riders)
    return _ret(core, rr, riders)


def sgu_bwd(z, dy, ln_g, ln_b, w_sp, b_sp, *, name, tm=256, riders=()):
    s = z.shape[0]
    tm = _row_tile(s, tm)

    def body(z_ref, dy_ref, g_ref, b_ref, w_ref, bs_ref, dz_ref, dw_ref, dbs_ref, dg_ref, db_ref, dvn_buf):
        first = pl.program_id(0) == 0
        u, u_grad = _gelu_and_grad(z_ref[:, :D_MODEL].astype(F32))
        vg, v_grad = _gelu_and_grad(z_ref[:, D_MODEL:].astype(F32))
        xhat, rstd, vn = _sgu_norm(vg, g_ref[...], b_ref[...])
        vn = vn.astype(BF16)
        dyv = dy_ref[...]
        dmixed = dyv * u
        dz_gate = dyv * u_grad
        row = lax.broadcasted_iota(I32, (SGU_CHUNK, SGU_CHUNK), 0)
        col = lax.broadcasted_iota(I32, (SGU_CHUNK, SGU_CHUNK), 1)
        dws, dbss = [], []
        for grp in range(SGU_GROUPS):
            w = _tril_bf16(w_ref[grp])
            cols = slice(grp * LANES, (grp + 1) * LANES)
            dw = jnp.zeros((SGU_CHUNK, SGU_CHUNK), F32)
            dbs = jnp.zeros((SGU_CHUNK, 1), F32)
            for ch in range(tm // SGU_CHUNK):
                rows = slice(ch * SGU_CHUNK, (ch + 1) * SGU_CHUNK)
                vblk = vn[rows, cols]
                mixed = jnp.dot(w, vblk, preferred_element_type=F32) + bs_ref[grp]
                dz_ref[rows, cols] = (dz_gate[rows, cols] * mixed).astype(BF16)
                dm = dmixed[rows, cols]
                dmb = dm.astype(BF16)
                dvn_buf[rows, cols] = lax.dot_general(w, dmb, TN_DIMS, preferred_element_type=F32)
                dw += lax.dot_general(dmb, vblk, NT_DIMS, preferred_element_type=F32)
                dbs += jnp.sum(dm, axis=-1, keepdims=True)
            dws.append(jnp.where(row >= col, dw, 0.0))
            dbss.append(jnp.broadcast_to(dbs, (SGU_CHUNK, SGU_CHUNK)))

        dvn = dvn_buf[...]
        dxhat = dvn * g_ref[...]
        dvg = rstd * (dxhat - jnp.mean(dxhat, axis=-1, keepdims=True) - xhat * jnp.mean(dxhat * xhat, axis=-1, keepdims=True))
        dz_ref[:, D_MODEL:] = (dvg * v_grad).astype(BF16)
        dlng, dlnb = jnp.sum(dvn * xhat, axis=0, keepdims=True), jnp.sum(dvn, axis=0, keepdims=True)

        @pl.when(first)
        def _():
            for grp in range(SGU_GROUPS):
                dw_ref[grp] = dws[grp]
                dbs_ref[grp] = dbss[grp]
            dg_ref[...] = dlng
            db_ref[...] = dlnb

        @pl.when(jnp.logical_not(first))
        def _():
            for grp in range(SGU_GROUPS):
                dw_ref[grp] += dws[grp]
                dbs_ref[grp] += dbss[grp]
            dg_ref[...] += dlng
            db_ref[...] += dlnb

    full3 = pl.BlockSpec((SGU_GROUPS, SGU_CHUNK, SGU_CHUNK), lambda i: (0, 0, 0))
    s3 = jax.ShapeDtypeStruct((SGU_GROUPS, SGU_CHUNK, SGU_CHUNK), F32)
    vshape = jax.ShapeDtypeStruct((1, D_MODEL), F32)
    core, rr = _call(
        body, grid=(s // tm,),
        in_specs=[_row_spec(tm, 2 * D_MODEL), _row_spec(tm, D_MODEL), _vec_spec(D_MODEL), _vec_spec(D_MODEL), full3, full3],
        out_specs=[_row_spec(tm, 2 * D_MODEL), full3, full3, _vec_spec(D_MODEL), _vec_spec(D_MODEL)],
        out_shape=[jax.ShapeDtypeStruct((s, 2 * D_MODEL), BF16), s3, s3, vshape, vshape],
        scratch_shapes=[pltpu.VMEM((tm, D_MODEL), F32)], operands=(z, dy, ln_g, ln_b, w_sp, b_sp), name=name, riders=riders)
    return _ret(core, rr, riders)


def _sigmoid(x):
    return 1.0 / (1.0 + jnp.exp(-x))


def ffn_up(h, w_gu, *, name, tm=512, riders=()):
    s = h.shape[0]
    tm = _row_tile(s, tm)

    def body(h_ref, wg_ref, wu_ref, d_ref, a_ref):
        hv = h_ref[...]
        sub = min(256, tm)
        for t in range(tm // sub):
            rows = slice(t * sub, (t + 1) * sub)
            g = jnp.dot(hv[rows], wg_ref[...], preferred_element_type=F32)
            u = jnp.dot(hv[rows], wu_ref[...], preferred_element_type=F32)
            sig = _sigmoid(g)
            silu = g * sig
            d_ref[0, rows, :] = (u * (sig + silu * (1.0 - sig))).astype(BF16)
            d_ref[1, rows, :] = silu.astype(BF16)
            a_ref[rows, :] = (silu * u).astype(BF16)

    core, rr = _call(
        body, grid=(2, s // tm),
        in_specs=[pl.BlockSpec((tm, D_MODEL), lambda j, i: (i, 0)),
                  pl.BlockSpec((None, D_MODEL, FF_HALF), lambda j, i: (j, 0, 0)),
                  pl.BlockSpec((None, D_MODEL, FF_HALF), lambda j, i: (j + 2, 0, 0))],
        out_specs=[pl.BlockSpec((2, tm, FF_HALF), lambda j, i: (0, i, j)), pl.BlockSpec((tm, FF_HALF), lambda j, i: (i, j))],
        out_shape=[jax.ShapeDtypeStruct((2, s, D_FF), BF16), jax.ShapeDtypeStruct((s, D_FF), BF16)],
        operands=(h, w_gu, w_gu), sem=("parallel", "parallel"), name=name, riders=riders)
    return _ret(core, rr, riders)


def _weight_tile(rows):
    for tr in (512, 352, 256, 128):
        if rows % tr == 0:
            return tr
    return rows


def place_shard(w, layer, chip_arr, dtype, *, name, riders=()):
    _, r, c = w.shape
    tr = _weight_tile(r)

    def body(chip_ref, w_ref, o_ref):
        o_ref[...] = w_ref[...].astype(dtype)

    core, rr = _call(
        body, grid=(r // tr,), prefetch=(chip_arr,),
        in_specs=[pl.BlockSpec((None, tr, c), lambda i, chip: (layer, i, 0))],
        out_specs=[pl.BlockSpec((None, tr, c), lambda i, chip: (chip[0], i, 0))],
        out_shape=[jax.ShapeDtypeStruct((N_CHIPS, r, c), dtype)], operands=(w,), sem=("parallel",), name=name, riders=riders)
    return _ret(core, rr, riders)


def _adamw_math(w, g, m, v):
    m = ADAM_B1 * m + (1.0 - ADAM_B1) * g
    v = ADAM_B2 * v + (1.0 - ADAM_B2) * (g * g)
    m_hat = m / (1.0 - ADAM_B1 ** ADAM_STEP)
    v_hat = v / (1.0 - ADAM_B2 ** ADAM_STEP)
    delta = -ADAM_LR * (m_hat / (jnp.sqrt(v_hat) + ADAM_EPS) + ADAM_WD * w)
    return delta, m, v


def adamw(w, g, m, v, *, name, after=None):
    nl, r, c = w.shape
    tr = _weight_tile(r)

    def body(w_ref, g_ref, m_ref, v_ref, *rest):
        go_ref, d_ref, mo_ref, vo_ref = rest[-4:]
        gv = g_ref[...]
        go_ref[...] = gv
        d_ref[...], mo_ref[...], vo_ref[...] = _adamw_math(w_ref[...], gv, m_ref[...], v_ref[...])

    spec = pl.BlockSpec((None, tr, c), lambda l, i: (l, i, 0))
    shape = jax.ShapeDtypeStruct(w.shape, F32)
    extra = [] if after is None else [after]
    outs, _ = _call(body, grid=(nl, r // tr), in_specs=[spec] * 4 + [ANY] * len(extra), out_specs=[spec] * 4,
                    out_shape=[shape] * 4, operands=(w, g, m, v, *extra), sem=("parallel", "parallel"), name=name)
    return outs


def adamw_small(ws, gs, ms, vs, *, name):
    n = len(ws)

    def body(*refs):
        ins, outs = refs[:4 * n], refs[4 * n:]
        for t in range(n):
            gv = ins[n + t][...]
            outs[t][...] = gv
            outs[n + t][...], outs[2 * n + t][...], outs[3 * n + t][...] = _adamw_math(
                ins[t][...], gv, ins[2 * n + t][...], ins[3 * n + t][...])

    shapes = [jax.ShapeDtypeStruct(w.shape, F32) for w in ws]
    res = pl.pallas_call(body, out_shape=shapes * 4, name=name)(*ws, *gs, *ms, *vs)
    return res[:n], res[n:2 * n], res[2 * n:3 * n], res[3 * n:]


def pair_add(g, r1, c_arr, *, name):
    _, rows, cdim = g.shape
    h = rows // 2

    def body(c_ref, g_ref, r_ref, o_ref):
        o_ref[...] = (g_ref[...].astype(F32) + r_ref[...].astype(F32)).astype(o_ref.dtype)

    (out,), _ = _call(
        body, grid=(N_CHIPS,), prefetch=(c_arr,),
        in_specs=[pl.BlockSpec((None, h, cdim), lambda s, c: (s, c[0], 0)), pl.BlockSpec((None, h, cdim), lambda s, c: (s, 0, 0))],
        out_specs=[pl.BlockSpec((None, h, cdim), lambda s, c: (s, 0, 0))],
        out_shape=[jax.ShapeDtypeStruct((N_CHIPS, h, cdim), g.dtype)], operands=(g, r1), sem=("parallel",), name=name)
    return out


def final_add(g, r1, r2, jc_arr, *, dest_shape, lead, prev, name):
    _, rows, cdim = g.shape
    h = rows // 2

    def body(jc_ref, g_ref, r1_ref, r2_ref, *rest):
        o_ref = rest[-1]
        acc = g_ref[...].astype(F32) + r1_ref[...].astype(F32)
        for k in range(3):
            acc = acc + r2_ref[k].astype(F32)
        o_ref[...] = acc

    if lead is None:
        o_spec = pl.BlockSpec((h, cdim), lambda i, jc: (jc[1], 0))
    elif lead == "chip":
        o_spec = pl.BlockSpec((None, h, cdim), lambda i, jc: (jc[0], jc[1], 0))
    else:
        o_spec = pl.BlockSpec((None, h, cdim), lambda i, jc: (lead, jc[1], 0))
    in_specs = [pl.BlockSpec((None, h, cdim), lambda i, jc: (jc[0], jc[1], 0)),
                pl.BlockSpec((None, h, cdim), lambda i, jc: (jc[0], 0, 0)),
                pl.BlockSpec((3, h, cdim), lambda i, jc: (0, 0, 0))]
    operands = [g, r1, r2]
    aliases = None
    if prev is not None:
        in_specs.append(ANY)
        operands.append(prev)
        aliases = {3: 0}
    (out,), _ = _call(body, grid=(1,), prefetch=(jc_arr,), in_specs=in_specs, out_specs=[o_spec],
                      out_shape=[jax.ShapeDtypeStruct(dest_shape, F32)], operands=operands, aliases=aliases, name=name)
    return out


def _place():
    return lax.axis_index("x"), lax.axis_index("y"), lax.axis_index("c")


def _partner(x, y, k):
    return (1 - x if k >> 1 else x), (1 - y if k & 1 else y)


WHOLE = (0, 1, 1)


def _half(rows, sel, dtype, piece=WHOLE):
    lo, hi, n = piece
    align = 16 if dtype == BF16 else 8
    step = rows // 2 // n
    assert rows // 2 == step * n and step % align == 0
    return pl.ds(pl.multiple_of(sel * (rows // 2) + lo * step, align), (hi - lo) * step)


def _rider(peers, inputs, aliased, fresh, nsem, copies, arrivals):
    def start(ins, outs, send, recv):
        for cp in copies(ins, outs, send, recv):
            cp.start()

    def finish(ins, outs, send, recv):
        for cp in arrivals(ins, outs, send, recv):
            cp.wait_recv()
        for cp in copies(ins, outs, send, recv):
            cp.wait_send()

    return types.SimpleNamespace(peers=peers, inputs=list(inputs), aliased=list(aliased), fresh=list(fresh), nsem=nsem,
                                 start=start, finish=finish)


def _remote(src, dst, send, recv, idx, dev):
    return pltpu.make_async_remote_copy(src_ref=src, dst_ref=dst, send_sem=send.at[idx], recv_sem=recv.at[idx],
                                        device_id=dev, device_id_type=MESH)


def gather_ici_rider(fulls, pieces=None):
    nt = len(fulls)
    pieces = pieces or [WHOLE] * nt

    def region(outs, t, slot, sel):
        return outs[t].at[slot, _half(fulls[t].shape[1], sel, fulls[t].dtype, pieces[t])]

    def copies(ins, outs, send, recv):
        x, y, c = _place()
        res = []
        for t in range(nt):
            for k in (1, 2, 3):
                px, py = _partner(x, y, k)
                mine = region(outs, t, 2 * x + y, c)
                res.append(_remote(mine, mine, send, recv, 3 * t + k - 1, (px, py, c)))
        return res

    def arrivals(ins, outs, send, recv):
        x, y, c = _place()
        res = []
        for t in range(nt):
            for k in (1, 2, 3):
                px, py = _partner(x, y, k)
                theirs = region(outs, t, 2 * px + py, c)
                res.append(_remote(theirs, theirs, send, recv, 3 * t + k - 1, (x, y, c)))
        return res

    return _rider("chips", fulls, range(nt), [], 3 * nt, copies, arrivals)


def gather_d2d_rider(fulls, pieces=None):
    nt = len(fulls)
    pieces = pieces or [WHOLE] * nt

    def region(outs, t, slot, sel):
        return outs[t].at[slot, _half(fulls[t].shape[1], sel, fulls[t].dtype, pieces[t])]

    def both(outs, send, recv, mine):
        x, y, c = _place()
        res = []
        for t in range(nt):
            for k in (1, 2, 3):
                px, py = _partner(x, y, k)
                part = region(outs, t, 2 * px + py, c if mine else 1 - c)
                res.append(_remote(part, part, send, recv, 3 * t + k - 1, (x, y, 1 - c)))
        return res

    return _rider("sibling", fulls, range(nt), [], 3 * nt, lambda i, o, s, r: both(o, s, r, True),
                  lambda i, o, s, r: both(o, s, r, False))


def exchange_rider(grads):
    nt = len(grads)

    def both(ins, outs, send, recv):
        x, y, c = _place()
        return [_remote(ins[t].at[:, _half(grads[t].shape[1], 1 - c, grads[t].dtype)], outs[t], send, recv, t, (x, y, 1 - c))
                for t in range(nt)]

    fresh = [jax.ShapeDtypeStruct((N_CHIPS, g.shape[1] // 2, g.shape[2]), g.dtype) for g in grads]
    return _rider("sibling", grads, [], fresh, nt, both, both)


def scatter_rider(parts):
    nt = len(parts)

    def both(ins, outs, send, recv):
        x, y, c = _place()
        res = []
        for t in range(nt):
            for k in (1, 2, 3):
                px, py = _partner(x, y, k)
                res.append(_remote(ins[t].at[2 * px + py], outs[t].at[k - 1], send, recv, 3 * t + k - 1, (px, py, c)))
        return res

    fresh = [jax.ShapeDtypeStruct((3,) + p.shape[1:], p.dtype) for p in parts]
    return _rider("chips", parts, [], fresh, 3 * nt, both, both)


def broadcast_rider(bufs, items):
    def region(outs, item, sel):
        bi, lead = item
        ref = outs[bi]
        if lead == "chip":
            x, y, _ = _place()
            ref = ref.at[2 * x + y]
        elif lead is not None:
            ref = ref.at[lead]
        return ref.at[_half(ref.shape[0], sel, F32)]

    def both(outs, send, recv, mine):
        x, y, c = _place()
        res = []
        for i, item in enumerate(items):
            part = region(outs, item, c if mine else 1 - c)
            res.append(_remote(part, part, send, recv, i, (x, y, 1 - c)))
        return res

    return _rider("sibling", bufs, range(len(bufs)), [], len(items), lambda i, o, s, r: both(o, s, r, True),
                  lambda i, o, s, r: both(o, s, r, False))


def allcast_rider(buf):
    peers = [(k, flip) for k in range(N_CHIPS) for flip in (0, 1) if (k, flip) != (0, 0)]

    def both(outs, send, recv, mine):
        x, y, c = _place()
        res = []
        for i, (k, flip) in enumerate(peers):
            px, py = _partner(x, y, k)
            pc = 1 - c if flip else c
            slot, sel = (2 * x + y, c) if mine else (2 * px + py, pc)
            part = outs[0].at[slot, _half(buf.shape[1], sel, F32)]
            res.append(_remote(part, part, send, recv, i, (px, py, pc)))
        return res

    return _rider("everyone", [buf], [0], [], len(peers), lambda i, o, s, r: both(o, s, r, True),
                  lambda i, o, s, r: both(o, s, r, False))


def comm_call(riders, *, name):
    _, res = _call(None, riders=riders, name=name)
    return res


SEMS = pl.BlockSpec(memory_space=pltpu.SEMAPHORE)
SIDE_EFFECT = pltpu.SideEffectType.DATAFLOW_SIDE_EFFECTING


def _split_refs(riders, refs):
    views, p = [], 0
    for r in riders:
        bufs = refs[p:p + len(r.inputs) + len(r.fresh)]
        p += len(bufs)
        ins = bufs[:len(r.inputs)]
        views.append([ins, [ins[i] for i in r.aliased] + list(bufs[len(r.inputs):])])
    for view in views:
        view += [refs[p], refs[p + 1]]
        p += 2
    return views


def comm_start(riders, *, name):
    kind = _peer_kind(riders)
    bufs = [a for r in riders for a in r.inputs]
    fresh = [f for r in riders for f in r.fresh]
    n_buf, n_fresh = len(bufs), len(fresh)

    def body(*refs):
        ins, outs = refs[:n_buf], refs[n_buf:]
        through, land, sems = outs[:n_buf], outs[n_buf:n_buf + n_fresh], outs[n_buf + n_fresh:-1]
        _peer_barrier(kind)
        per_rider, pb, pf = [], 0, 0
        for r in riders:
            per_rider += list(through[pb:pb + len(r.inputs)]) + list(land[pf:pf + len(r.fresh)])
            pb, pf = pb + len(r.inputs), pf + len(r.fresh)
        for r, (r_ins, r_outs, send, recv) in zip(riders, _split_refs(riders, per_rider + list(sems))):
            r.start(r_ins, r_outs, send, recv)
        outs[-1][...] = jnp.zeros((8, LANES), F32)

    sem_shapes = [pltpu.SemaphoreType.DMA((r.nsem,)) for r in riders for _ in (0, 1)]
    res = pl.pallas_call(
        body, name=name, in_specs=[ANY] * n_buf,
        out_specs=[ANY] * (n_buf + n_fresh) + [SEMS] * len(sem_shapes) + [pl.BlockSpec(memory_space=pltpu.VMEM)],
        out_shape=[jax.ShapeDtypeStruct(a.shape, a.dtype) for a in bufs] + fresh + sem_shapes
        + [jax.ShapeDtypeStruct((8, LANES), F32)],
        input_output_aliases={i: i for i in range(n_buf)},
        compiler_params=pltpu.CompilerParams(has_side_effects=SIDE_EFFECT, collective_id=PEER_KINDS.index(kind)))(*bufs)
    return (riders, list(res[:n_buf + n_fresh]), list(res[n_buf + n_fresh:-1])), res[-1]


def comm_wait(state, after, *, name):
    riders, bufs, sems = state
    n_buf, n_sem = len(bufs), len(sems)
    n_in = sum(len(r.inputs) for r in riders)

    def body(*refs):
        held, sem_refs = refs[:n_buf], refs[n_buf:n_buf + n_sem]
        through, land = held[:n_in], held[n_in:]
        per_rider, pb, pf = [], 0, 0
        for r in riders:
            per_rider += list(through[pb:pb + len(r.inputs)]) + list(land[pf:pf + len(r.fresh)])
            pb, pf = pb + len(r.inputs), pf + len(r.fresh)
        for r, (r_ins, r_outs, send, recv) in zip(riders, _split_refs(riders, per_rider + list(sem_refs))):
            r.finish(r_ins, r_outs, send, recv)

    res = pl.pallas_call(
        body, name=name, in_specs=[ANY] * n_buf + [SEMS] * n_sem + [ANY], out_specs=[ANY] * n_buf,
        out_shape=[jax.ShapeDtypeStruct(a.shape, a.dtype) for a in bufs],
        input_output_aliases={i: i for i in range(n_buf)},
        compiler_params=pltpu.CompilerParams(has_side_effects=SIDE_EFFECT))(*bufs, *sems, after)
    through, land = list(res[:n_in]), list(res[n_in:])
    out, pb, pf = [], 0, 0
    for r in riders:
        r_ins, r_land = through[pb:pb + len(r.inputs)], land[pf:pf + len(r.fresh)]
        pb, pf = pb + len(r.inputs), pf + len(r.fresh)
        out.append([r_ins[i] for i in r.aliased] + r_land)
    return out


SLAB_ROWS = 192


def _pad_rows(a, rows=8):
    return jnp.pad(a, ((0, rows - a.shape[0]), (0, 0)))


def _pack_small(norm_grads, db_qkv, db_o, dsinks, db_sp, dln_g, dln_b, dw_sp, loss_part):
    parts = [
        jnp.concatenate(norm_grads, axis=0),
        _pad_rows(jnp.pad(db_qkv, ((0, 0), (0, 2 * D_MODEL - QKV_WIDTH))).reshape(2, D_MODEL)),
        _pad_rows(db_o),
        _pad_rows(jnp.pad(dsinks.reshape(1, N_Q_HEADS), ((0, 0), (0, D_MODEL - N_Q_HEADS)))),
        _pad_rows(db_sp.reshape(1, D_MODEL)),
        _pad_rows(jnp.concatenate([dln_g, dln_b, jnp.pad(loss_part[0:1], ((0, 0), (0, D_MODEL - LANES)))], axis=0)),
        dw_sp.reshape(SGU_CHUNK, D_MODEL),
    ]
    slab = jnp.concatenate(parts, axis=0)
    return jnp.pad(slab, ((0, SLAB_ROWS - slab.shape[0]), (0, 0))).reshape(N_CHIPS, SLAB_ROWS // N_CHIPS, D_MODEL)


def _unpack_small(slab, j):
    slab = slab.reshape(SLAB_ROWS, D_MODEL)
    norms = [slab[2 * i:2 * i + 2] for i in range(4)]
    db_qkv = slab[8:10].reshape(1, 2 * D_MODEL)[:, :QKV_WIDTH]
    db_o = slab[16:17]
    dsinks = slab[24:25, :N_Q_HEADS]
    db_sp = slab[32:33].reshape(SGU_GROUPS, SGU_CHUNK)
    width = D_MODEL // N_CHIPS
    dln_g = lax.dynamic_slice(slab[40:41], (0, j * width), (1, width))
    dln_b = lax.dynamic_slice(slab[41:42], (0, j * width), (1, width))
    dw_sp = slab[48:48 + SGU_CHUNK].reshape(SGU_GROUPS * SGU_CHUNK, SGU_CHUNK)
    return norms, db_qkv, db_o, dsinks, db_sp, dln_g, dln_b, dw_sp, slab[42, 0]


class _GradReduce:
    def __init__(self, c_arr, jc_arr, dest_shapes):
        self.c_arr, self.jc_arr, self.dest_shapes = c_arr, jc_arr, dest_shapes
        self.grad, self.sibling, self.pair, self.chips, self.dest = {}, {}, {}, {}, {}

    def exchange(self, tags):
        return exchange_rider([self.grad[t] for t in tags])

    def exchanged(self, tags, res):
        for t, r in zip(tags, res):
            self.sibling[t] = r
            self.pair[t] = pair_add(self.grad[t], r, self.c_arr, name=f"pair_add_{t}")

    def scatter(self, tags):
        return scatter_rider([self.pair[t] for t in tags])

    def scattered(self, tags, res, where):
        for t, r in zip(tags, res):
            name, lead = where[t]
            self.dest[name] = final_add(self.grad[t], self.sibling[t], r, self.jc_arr, dest_shape=self.dest_shapes[name],
                                        lead=lead, prev=self.dest.get(name), name=f"final_add_{t}")

    def broadcast(self, items):
        names = []
        for n, _ in items:
            if n not in names:
                names.append(n)
        return names, broadcast_rider([self.dest[n] for n in names], [(names.index(n), lead) for n, lead in items])

    def broadcasted(self, names, res):
        for n, r in zip(names, res):
            self.dest[n] = r


def kernel(x, norm_mix_pre, norm_mix_post, norm_ffn_pre, norm_ffn_post, attn_w_qkv, attn_b_qkv, attn_sinks, attn_w_o, attn_b_o, sgu_w_in, sgu_ln_g, sgu_ln_b, sgu_w_spatial, sgu_b_spatial, sgu_w_out, ffn_w_gate_up, ffn_w_down, loss_target, m_norm_mix_pre, m_norm_mix_post, m_norm_ffn_pre, m_norm_ffn_post, m_attn_w_qkv, m_attn_b_qkv, m_attn_sinks, m_attn_w_o, m_attn_b_o, m_sgu_w_in, m_sgu_ln_g, m_sgu_ln_b, m_sgu_w_spatial, m_sgu_b_spatial, m_sgu_w_out, m_ffn_w_gate_up, m_ffn_w_down, v_norm_mix_pre, v_norm_mix_post, v_norm_ffn_pre, v_norm_ffn_post, v_attn_w_qkv, v_attn_b_qkv, v_attn_sinks, v_attn_w_o, v_attn_b_o, v_sgu_w_in, v_sgu_ln_g, v_sgu_ln_b, v_sgu_w_spatial, v_sgu_b_spatial, v_sgu_w_out, v_ffn_w_gate_up, v_ffn_w_down):
    s = x.shape[1]
    x0 = x.reshape(s, D_MODEL)
    target = loss_target.reshape(s, D_MODEL)
    mx, my, mc = lax.axis_index("x"), lax.axis_index("y"), lax.axis_index("c")
    chip = 2 * mx + my
    chip_arr = jnp.reshape(chip, (1,)).astype(I32)
    c_arr = jnp.reshape(mc, (1,)).astype(I32)
    jc_arr = jnp.stack([chip, mc]).astype(I32)
    zero_bias = jnp.zeros((1, D_MODEL), F32)

    def gain(p, i):
        return p[i:i + 1]

    big = [attn_w_qkv, attn_w_o, sgu_w_in, sgu_w_out, ffn_w_gate_up, ffn_w_gate_up, ffn_w_down, ffn_w_down]
    layers = [0, 0, 0, 0, 0, 1, 0, 1]
    tags = ["qkv", "wo", "win", "wout", "wgu0", "wgu1", "wd0", "wd1"]
    full = {t: place_shard(w, l, chip_arr, BF16, name=f"place_{t}") for w, l, t in zip(big, layers, tags)
            if t in ("qkv", "wo")}
    ln_pack = _pad_rows(jnp.concatenate([sgu_ln_g, sgu_ln_b], axis=0), 16)[None]
    full["ln"] = place_shard(ln_pack, 0, chip_arr, F32, name="place_ln")

    def split(items):
        return [i if isinstance(i, str) else i[0] for i in items], [WHOLE if isinstance(i, str) else tuple(i[1:]) for i in items]

    def ici(*items):
        names, pieces = split(items)
        return gather_ici_rider([full[n] for n in names], pieces)

    def d2d(*items):
        names, pieces = split(items)
        return gather_d2d_rider([full[n] for n in names], pieces)

    def landed(items, res):
        for n, r in zip(split(items)[0], res):
            full[n] = r

    cos, sin = _rope_tables(s)
    sink_rows = jnp.broadcast_to(
        jnp.repeat(attn_sinks.reshape(N_KV_HEADS, GQA_GROUP), WINDOW, axis=1)[:, None, :], (N_KV_HEADS, 8, ROWS))
    w_sp = sgu_w_spatial.reshape(SGU_GROUPS, SGU_CHUNK, SGU_CHUNK)
    b_sp = jnp.broadcast_to(sgu_b_spatial.reshape(SGU_GROUPS, SGU_CHUNK)[:, :, None], (SGU_GROUPS, SGU_CHUNK, LANES))

    (h0, full["wgu0"]), (res,) = prenorm_and_place(x0, gain(norm_mix_pre, 0), ffn_w_gate_up, 0, chip_arr, name="prenorm_0",
                                                   riders=[ici("qkv", "ln")])
    landed(("qkv", "ln"), res)
    full["wgu1"], (res,) = place_shard(ffn_w_gate_up, 1, chip_arr, BF16, name="place_wgu1", riders=[d2d("qkv", "ln")])
    landed(("qkv", "ln"), res)
    ln_g = full["ln"][:, 0, :].reshape(1, D_MODEL)
    ln_b = full["ln"][:, 1, :].reshape(1, D_MODEL)

    def hosted(call, stages):
        outputs, results = call([{"ici": ici, "d2d": d2d}[kind](*items) for kind, items in stages])
        for (_, items), res in zip(stages, results):
            landed(items, res)
        return outputs

    casts = [(ffn_w_down, 0), (sgu_w_in, 0), (ffn_w_down, 1), (sgu_w_out, 0)]
    qkv, full["wd0"], full["win"], full["wd1"], full["wout"] = hosted(
        lambda r: qkv_proj(h0, full["qkv"], attn_b_qkv, cos, sin, casts, chip_arr, name="qkv_proj", riders=r),
        [("ici", ("wo", ("wgu0", 0, 3, 8)))])
    o = hosted(lambda r: attn_fwd(qkv, sink_rows, name="attn_fwd", riders=r),
               [("d2d", ("wo",)), ("ici", (("wgu0", 3, 8, 8), ("wd0", 0, 2, 11), ("win", 0, 2, 8)))])
    w_o = full["wo"].reshape(Q_WIDTH, D_MODEL)
    x1, h1, m0 = hosted(lambda r: proj_residual_norm(o, w_o, x0, attn_b_o, gain(norm_mix_post, 0), gain(norm_ffn_pre, 0),
                                                     name="attn_out_norm", riders=r),
                        [("d2d", ("wgu0",)), ("ici", (("wd0", 2, 11, 11),))])
    gu0, a0 = hosted(lambda r: ffn_up(h1, full["wgu0"], name="ffn_up_0", riders=r),
                     [("d2d", ("wd0",)), ("ici", (("win", 2, 8, 8), "wout", ("wgu1", 0, 4, 8)))])
    w_d0 = full["wd0"].reshape(D_FF, D_MODEL)
    x2, h2, f0 = hosted(lambda r: proj_residual_norm(a0, w_d0, x1, zero_bias, gain(norm_ffn_post, 0), gain(norm_mix_pre, 1),
                                                     name="ffn_down_norm_0", riders=r),
                        [("d2d", ("win", "wout")), ("ici", (("wgu1", 4, 8, 8),))])
    w_in = full["win"]
    z, y = hosted(lambda r: sgu_in_fwd(h2, w_in, ln_g, ln_b, w_sp, b_sp, name="sgu_in_fwd", riders=r),
                  [("d2d", ("wgu1",)), ("ici", ("wd1",))])
    w_out = full["wout"].reshape(D_MODEL, D_MODEL)
    x3, h3, m1 = hosted(lambda r: proj_residual_norm(y, w_out, x2, zero_bias, gain(norm_mix_post, 1), gain(norm_ffn_pre, 1),
                                                     name="sgu_out_norm", riders=r),
                        [("d2d", ("wd1",))])
    w_qkv, w_gu0, w_gu1 = full["qkv"], full["wgu0"], full["wgu1"]
    w_d1 = full["wd1"].reshape(D_FF, D_MODEL)
    gu1, a1, dx4, df1, dg_fpost1, loss_part = ffn_fwd_loss_rows(
        h3, w_gu1, w_d1, x3, gain(norm_ffn_post, 1), target, name="ffn_fwd_loss_rows")

    red = _GradReduce(c_arr, jc_arr, {
        "qkv": attn_w_qkv.shape[1:], "wo": attn_w_o.shape[1:], "win": sgu_w_in.shape[1:], "wout": sgu_w_out.shape[1:],
        "wgu": ffn_w_gate_up.shape, "wd": ffn_w_down.shape, "slab": (N_CHIPS, SLAB_ROWS // N_CHIPS, D_MODEL)})
    where = {"qkv": ("qkv", None), "wo": ("wo", None), "win": ("win", None), "wout": ("wout", None), "wgu0": ("wgu", 0),
             "wgu1": ("wgu", 1), "wd0": ("wd", 0), "wd1": ("wd", 1), "small": ("slab", "chip")}

    dgu1, dx3, dm1, dg_fpre1, dg_mpost1, _ = ffn_bwd_rows(
        df1, w_d1, gu1, w_gu1, dx4, x3, gain(norm_ffn_pre, 1), m1, gain(norm_mix_post, 1), name="ffn_bwd_rows_1")
    red.grad["wd1"] = mm_tn(a1, df1, shard_major=False, tm=256, tn=D_MODEL, name="dw_down_1").reshape(
        N_CHIPS, D_FF // N_CHIPS, D_MODEL)
    red.grad["wgu1"], (res,) = mm_tn(h3, dgu1, shard_major=True, tm=512, tn=FF_HALF, name="dw_gate_up_1",
                                     riders=[red.exchange(["wd1"])])
    red.exchanged(["wd1"], res)
    dy, (res,) = mm_nt(dm1, w_out, out_dtype=F32, name="dy_sgu", riders=[red.exchange(["wgu1"])])
    red.exchanged(["wgu1"], res)
    red.grad["wout"] = mm_tn(y, dm1, shard_major=False, tm=512, tn=D_MODEL, name="dw_sgu_out").reshape(
        N_CHIPS, D_MODEL // N_CHIPS, D_MODEL)
    (dz, dw_sp, db_sp, dln_g, dln_b), (res_a, res_b) = sgu_bwd(
        z, dy, ln_g, ln_b, w_sp, b_sp, name="sgu_bwd", riders=[red.scatter(["wd1"]), red.exchange(["wout"])])
    red.scattered(["wd1"], res_a, where)
    red.exchanged(["wout"], res_b)
    names, rider = red.broadcast([("wd", 1)])
    red.grad["win"], (res_a, res_b) = mm_tn(h2, dz, shard_major=True, tm=D_MODEL, tn=2 * D_MODEL // N_CHIPS, name="dw_sgu_in",
                                            riders=[rider, red.scatter(["wout"])])
    red.broadcasted(names, res_a)
    red.scattered(["wout"], res_b, where)
    names, rider = red.broadcast([("wout", None)])
    (dx2, df0, dg_mpre1, dg_fpost0, _), (res_a, res_b) = dh_norm_bwd_pair(
        dz, w_in, dx3, x2, gain(norm_mix_pre, 1), f0, gain(norm_ffn_post, 0), name="dh_sgu_norm",
        riders=[red.exchange(["win"]), rider])
    red.exchanged(["win"], res_a)
    red.broadcasted(names, res_b)
    (dgu0, dx1, dm0, dg_fpre0, dg_mpost0, db_o), (res,) = ffn_bwd_rows(
        df0, w_d0, gu0, w_gu0, dx2, x1, gain(norm_ffn_pre, 0), m0, gain(norm_mix_post, 0), name="ffn_bwd_rows_0",
        riders=[red.scatter(["wgu1", "win"])])
    red.scattered(["wgu1", "win"], res, where)
    names, rider = red.broadcast([("wgu", 1), ("win", None)])
    dw_d0, (res,) = mm_tn(a0, df0, shard_major=False, tm=256, tn=D_MODEL, name="dw_down_0", riders=[rider])
    red.broadcasted(names, res)
    red.grad["wd0"] = dw_d0.reshape(N_CHIPS, D_FF // N_CHIPS, D_MODEL)
    do, (res,) = mm_nt(dm0, w_o, out_dtype=BF16, name="do_attn", riders=[red.exchange(["wd0"])])
    red.exchanged(["wd0"], res)
    red.grad["wgu0"], (res,) = mm_tn(h1, dgu0, shard_major=True, tm=512, tn=FF_HALF, name="dw_gate_up_0",
                                     riders=[red.scatter(["wd0"])])
    red.scattered(["wd0"], res, where)
    names, rider = red.broadcast([("wd", 0)])
    dw_o, (res_a, res_b) = mm_tn(o, dm0, shard_major=False, tm=512, tn=D_MODEL, name="dw_attn_out",
                                 riders=[red.exchange(["wgu0"]), rider])
    red.exchanged(["wgu0"], res_a)
    red.broadcasted(names, res_b)
    red.grad["wo"] = dw_o.reshape(N_CHIPS, Q_WIDTH // N_CHIPS, D_MODEL)
    (dq, dkc, dkp, dvc, dvp, dsink), (res_a, res_b) = attn_bwd(
        qkv, sink_rows, do, name="attn_bwd", riders=[red.scatter(["wgu0"]), red.exchange(["wo"])])
    red.scattered(["wgu0"], res_a, where)
    red.exchanged(["wo"], res_b)
    names, rider = red.broadcast([("wgu", 0)])
    (dqkv, db_qkv), (res,) = rope_bwd(dq, dkc, dkp, dvc, dvp, cos, sin, name="rope_bwd", riders=[rider])
    red.broadcasted(names, res)
    red.grad["qkv"], (res,) = mm_tn(h0, dqkv, shard_major=True, tm=D_MODEL, tn=QKV_WIDTH // N_CHIPS, name="dw_qkv",
                                    riders=[red.scatter(["wo"])])
    red.scattered(["wo"], res, where)
    grad_x, dg_mpre0 = dh_norm_bwd_last(dqkv, w_qkv, dx1, x0, gain(norm_mix_pre, 0), name="dh_attn_norm_in")

    norm_grads = [jnp.concatenate(p, axis=0) for p in
                  ((dg_mpre0, dg_mpre1), (dg_mpost0, dg_mpost1), (dg_fpre0, dg_fpre1), (dg_fpost0, dg_fpost1))]
    red.grad["small"] = _pack_small(norm_grads, db_qkv, db_o, dsink[:, :, 0, 0], db_sp[:, :, 0], dln_g, dln_b, dw_sp,
                                    loss_part)
    def big_update(w, g, m, v, tag, after=None):
        return adamw(w, g.reshape(w.shape), m, v, name=f"adamw_{tag}", after=after)

    (res,) = comm_call([red.exchange(["qkv", "small"])], name="tail_1")
    red.exchanged(["qkv", "small"], res)
    state, token = comm_start([red.scatter(["qkv", "small"])], name="tail_2_start")
    upd_wgu = big_update(ffn_w_gate_up, red.dest["wgu"], m_ffn_w_gate_up, v_ffn_w_gate_up, "wgu", after=token)
    (res,) = comm_wait(state, upd_wgu[1], name="tail_2_wait")
    red.scattered(["qkv", "small"], res, where)
    names, rider = red.broadcast([("qkv", None), ("wo", None)])
    state, token = comm_start([rider, allcast_rider(red.dest["slab"])], name="tail_3_start")
    upd_wd = big_update(ffn_w_down, red.dest["wd"], m_ffn_w_down, v_ffn_w_down, "wd", after=token)
    res, (slab_full,) = comm_wait(state, upd_wd[1], name="tail_3_wait")
    red.broadcasted(names, res)
    g_qkv, g_wo, g_win, g_wout = (red.dest[n] for n in ("qkv", "wo", "win", "wout"))
    g_norms, g_bqkv, g_bo, g_sinks, g_bsp, g_lng, g_lnb, g_wsp, loss = _unpack_small(slab_full, chip)

    upd = {
        "attn_w_qkv": big_update(attn_w_qkv, g_qkv, m_attn_w_qkv, v_attn_w_qkv, "qkv"),
        "attn_w_o": big_update(attn_w_o, g_wo, m_attn_w_o, v_attn_w_o, "wo"),
        "sgu_w_in": big_update(sgu_w_in, g_win, m_sgu_w_in, v_sgu_w_in, "win"),
        "sgu_w_out": big_update(sgu_w_out, g_wout, m_sgu_w_out, v_sgu_w_out, "wout"),
        "ffn_w_gate_up": upd_wgu,
        "ffn_w_down": upd_wd,
    }
    small_names = ["norm_mix_pre", "norm_mix_post", "norm_ffn_pre", "norm_ffn_post", "attn_b_qkv", "attn_sinks", "attn_b_o",
                   "sgu_ln_g", "sgu_ln_b", "sgu_w_spatial", "sgu_b_spatial"]
    small_w = [norm_mix_pre, norm_mix_post, norm_ffn_pre, norm_ffn_post, attn_b_qkv, attn_sinks, attn_b_o, sgu_ln_g, sgu_ln_b,
               sgu_w_spatial, sgu_b_spatial]
    small_m = [m_norm_mix_pre, m_norm_mix_post, m_norm_ffn_pre, m_norm_ffn_post, m_attn_b_qkv, m_attn_sinks, m_attn_b_o,
               m_sgu_ln_g, m_sgu_ln_b, m_sgu_w_spatial, m_sgu_b_spatial]
    small_v = [v_norm_mix_pre, v_norm_mix_post, v_norm_ffn_pre, v_norm_ffn_post, v_attn_b_qkv, v_attn_sinks, v_attn_b_o,
               v_sgu_ln_g, v_sgu_ln_b, v_sgu_w_spatial, v_sgu_b_spatial]
    small_g = g_norms + [g_bqkv, g_sinks, g_bo, g_lng, g_lnb, g_wsp, g_bsp]

    def flat2(a):
        return a.reshape(-1, a.shape[-1])

    res = adamw_small([flat2(a) for a in small_w], [flat2(a) for a in small_g], [flat2(a) for a in small_m],
                      [flat2(a) for a in small_v], name="adamw_small")
    for i, nm in enumerate(small_names):
        upd[nm] = tuple(r[i].reshape(small_w[i].shape) for r in res)

    order = ["norm_mix_pre", "norm_mix_post", "norm_ffn_pre", "norm_ffn_post", "attn_w_qkv", "attn_b_qkv", "attn_sinks",
             "attn_w_o", "attn_b_o", "sgu_w_in", "sgu_ln_g", "sgu_ln_b", "sgu_w_spatial", "sgu_b_spatial", "sgu_w_out",
             "ffn_w_gate_up", "ffn_w_down"]
    outs = [loss, grad_x.reshape(1, s, D_MODEL)]
    for part in range(4):
        outs += [upd[nm][part] for nm in order]
    return tuple(outs)
```

```python
import types

import numpy as np
import jax
import jax.numpy as jnp
from jax import lax
from jax.experimental import pallas as pl
from jax.experimental.pallas import tpu as pltpu

F32 = jnp.float32
BF16 = jnp.bfloat16
I32 = jnp.int32

D_MODEL = 1024
HEAD_DIM = 64
N_Q_HEADS = 16
N_KV_HEADS = 4
GQA_GROUP = 4
WINDOW = 128
Q_WIDTH = 1024
KV_WIDTH = 256
QKV_WIDTH = 1536
ROPE_THETA = 10000.0
SGU_GROUPS = 8
SGU_CHUNK = 128
D_FF = 2816
FF_HALF = D_FF // 2
EPS = 1e-6
N_CHIPS = 4
LANES = 128

ADAM_LR = 0.001
ADAM_B1 = 0.9
ADAM_B2 = 0.999
ADAM_EPS = 1e-08
ADAM_WD = 0.01
ADAM_STEP = 10

VMEM_LIMIT = 52 * 1024 * 1024
BIG_VMEM_LIMIT = 62 * 1024 * 1024
SUB_ROWS = 256
MESH = pl.DeviceIdType.MESH
NEG = -1e30
NT_DIMS = (((1,), (1,)), ((), ()))
TN_DIMS = (((0,), (0,)), ((), ()))
NN_DIMS = (((1,), (0,)), ((), ()))
ANY = pl.BlockSpec(memory_space=pl.ANY)


def _row_tile(s, want):
    return want if s % want == 0 else s


PEER_KINDS = ("sibling", "chips", "sibling+chips", "everyone")


def _peer_kind(riders):
    kinds = {r.peers for r in riders}
    if not kinds:
        return None
    if "everyone" in kinds:
        return "everyone"
    return "sibling+chips" if len(kinds) == 2 else kinds.pop()


def _peer_barrier(kind):
    x, y, c = _place()
    chips = [(*_partner(x, y, k), c) for k in (1, 2, 3)]
    peers = {"sibling": [(x, y, 1 - c)], "chips": chips, "sibling+chips": [(x, y, 1 - c)] + chips,
             "everyone": [(x, y, 1 - c)] + chips + [(px, py, 1 - c) for px, py, _ in chips]}[kind]
    barrier = pltpu.get_barrier_semaphore()
    for dev in peers:
        pl.semaphore_signal(barrier, inc=1, device_id=dev, device_id_type=MESH)
    pl.semaphore_wait(barrier, len(peers))


def _call(body, *, name, grid=(), in_specs=(), out_specs=(), out_shape=(), scratch_shapes=(), operands=(), prefetch=(),
          aliases=None, riders=(), sem=None, vmem_limit=VMEM_LIMIT):
    n_pre, n_in, n_out, n_scr = len(prefetch), len(operands), len(out_shape), len(scratch_shapes)
    in_specs, out_specs, out_shape = list(in_specs), list(out_specs), list(out_shape)
    operands, scratch_shapes = list(operands), list(scratch_shapes)
    io_alias = {n_pre + i: o for i, o in (aliases or {}).items()}
    for r in riders:
        base_in, base_out = n_pre + len(operands), len(out_shape)
        operands += list(r.inputs)
        in_specs += [ANY] * len(r.inputs)
        for pos, i in enumerate(r.aliased):
            io_alias[base_in + i] = base_out + pos
            out_shape.append(jax.ShapeDtypeStruct(r.inputs[i].shape, r.inputs[i].dtype))
        out_shape += list(r.fresh)
        out_specs += [ANY] * (len(r.aliased) + len(r.fresh))
        scratch_shapes += [pltpu.SemaphoreType.DMA((r.nsem,)), pltpu.SemaphoreType.DMA((r.nsem,))]

    def wrapped(*refs):
        pre, p = refs[:n_pre], n_pre
        core_in, p = refs[p:p + n_in], p + n_in
        r_in = []
        for r in riders:
            r_in.append(refs[p:p + len(r.inputs)])
            p += len(r.inputs)
        core_out, p = refs[p:p + n_out], p + n_out
        r_out = []
        for r in riders:
            k = len(r.aliased) + len(r.fresh)
            r_out.append(refs[p:p + k])
            p += k
        core_scr, p = refs[p:p + n_scr], p + n_scr
        r_sem = [refs[p + 2 * i:p + 2 * i + 2] for i in range(len(riders))]

        def edge(at_last, fns):
            def run():
                if not at_last:
                    _peer_barrier(peer_kind)
                for i, r in enumerate(riders):
                    getattr(r, fns)(r_in[i], r_out[i], r_sem[i][0], r_sem[i][1])
            if not riders:
                return
            if not grid:
                run()
                return
            cond = None
            for d, n in enumerate(grid):
                c = pl.program_id(d) == (n - 1 if at_last else 0)
                cond = c if cond is None else jnp.logical_and(cond, c)
            pl.when(cond)(run)

        edge(False, "start")
        if body is not None:
            body(*pre, *core_in, *core_out, *core_scr)
        edge(True, "finish")

    if sem is None or riders:
        sem = ("arbitrary",) * len(grid)
    kwargs = dict(out_shape=out_shape, input_output_aliases=io_alias, name=name)
    peer_kind = _peer_kind(riders)
    collective = {} if peer_kind is None else {"collective_id": PEER_KINDS.index(peer_kind)}
    if grid:
        kwargs["compiler_params"] = pltpu.CompilerParams(dimension_semantics=sem, vmem_limit_bytes=vmem_limit, **collective)
    elif collective:
        kwargs["compiler_params"] = pltpu.CompilerParams(**collective)
    if n_pre:
        kwargs["grid_spec"] = pltpu.PrefetchScalarGridSpec(
            num_scalar_prefetch=n_pre, grid=grid, in_specs=in_specs, out_specs=out_specs, scratch_shapes=scratch_shapes)
    else:
        kwargs.update(grid=grid, in_specs=in_specs, out_specs=out_specs, scratch_shapes=scratch_shapes)
    res = pl.pallas_call(wrapped, **kwargs)(*prefetch, *operands)
    core, rest, rider_res = list(res[:n_out]), list(res[n_out:]), []
    for r in riders:
        k = len(r.aliased) + len(r.fresh)
        rider_res.append(rest[:k])
        rest = rest[k:]
    return core, rider_res


def _mm_call(*, grid, in_specs, out_spec, out_shape, dims, nk, kaxis, acc_shape, name, operands, riders=()):
    out_dtype = out_shape.dtype

    def body(a_ref, b_ref, o_ref, *scratch):
        p = lax.dot_general(a_ref[...].astype(BF16), b_ref[...].astype(BF16), dims, preferred_element_type=F32)
        if nk == 1:
            o_ref[...] = p.astype(out_dtype)
        else:
            acc = scratch[0]
            kk = pl.program_id(kaxis)

            @pl.when(kk == 0)
            def _():
                acc[...] = p

            @pl.when(kk > 0)
            def _():
                acc[...] += p

            @pl.when(kk == nk - 1)
            def _():
                o_ref[...] = acc[...].astype(out_dtype)

    sem = ["parallel"] * len(grid)
    if nk > 1:
        sem[kaxis] = "arbitrary"
    (out,), rider_res = _call(
        body, grid=grid, in_specs=in_specs, out_specs=[out_spec], out_shape=[out_shape],
        scratch_shapes=[pltpu.VMEM(acc_shape, F32)] if nk > 1 else [], operands=operands, name=name, riders=riders,
        sem=tuple(sem))
    return (out, rider_res) if riders else out


def mm_nt(a, w, *, out_dtype, name, tm=1024, riders=()):
    m, n = a.shape
    kout = w.shape[0]
    tm = _row_tile(m, tm)
    return _mm_call(grid=(m // tm,),
                    in_specs=[pl.BlockSpec((tm, n), lambda i: (i, 0)), pl.BlockSpec((kout, n), lambda i: (0, 0))],
                    out_spec=pl.BlockSpec((tm, kout), lambda i: (i, 0)),
                    out_shape=jax.ShapeDtypeStruct((m, kout), out_dtype), dims=NT_DIMS, nk=1, kaxis=0,
                    acc_shape=None, name=name, operands=(a, w), riders=riders)


def mm_tn(a, b, *, shard_major, name, tm, tn, tk=None, out_dtype=BF16, riders=()):
    s, m = a.shape
    tk = s if tk is None else _row_tile(s, tk)
    if b.ndim == 3:
        n = 2 * b.shape[2]
        b_spec = pl.BlockSpec((None, tk, tn), lambda j, i, kk: (j // 2, kk, j % 2))
    else:
        n = b.shape[1]
        b_spec = pl.BlockSpec((tk, tn), lambda j, i, kk: (kk, j))
    if shard_major:
        assert tn == n // N_CHIPS
        o_spec = pl.BlockSpec((None, tm, tn), lambda j, i, kk: (j, i, 0))
        o_shape = jax.ShapeDtypeStruct((N_CHIPS, m, tn), out_dtype)
    else:
        o_spec = pl.BlockSpec((tm, tn), lambda j, i, kk: (i, j))
        o_shape = jax.ShapeDtypeStruct((m, n), out_dtype)
    return _mm_call(grid=(n // tn, m // tm, s // tk),
                    in_specs=[pl.BlockSpec((tk, tm), lambda j, i, kk: (kk, i)), b_spec], out_spec=o_spec,
                    out_shape=o_shape, dims=TN_DIMS, nk=s // tk, kaxis=2, acc_shape=(tm, tn), name=name, operands=(a, b),
                    riders=riders)


def _rstd(x):
    return lax.rsqrt(jnp.mean(x * x, axis=-1, keepdims=True) + EPS)


def _rms_bwd(dy, x, g):
    r = _rstd(x)
    xhat = x * r
    gy = dy * g
    dx = r * (gy - xhat * jnp.mean(gy * xhat, axis=-1, keepdims=True))
    return dx, jnp.sum(dy * xhat, axis=0, keepdims=True)


def _accum(ref, val, first):
    @pl.when(first)
    def _():
        ref[...] = val

    @pl.when(jnp.logical_not(first))
    def _():
        ref[...] += val


def _row_spec(tm, width):
    return pl.BlockSpec((tm, width), lambda i: (i, 0))


def _vec_spec(width):
    return pl.BlockSpec((1, width), lambda i: (0, 0))


def _ret(core, rider_res, riders):
    core = core[0] if len(core) == 1 else core
    return (core, rider_res) if riders else core


def prenorm_and_place(x, g, w, layer, chip_arr, *, name, tm=256, riders=()):
    s = x.shape[0]
    tm = _row_tile(s, tm)
    steps = s // tm
    _, r, c = w.shape
    tr = r // steps
    assert tr * steps == r and tr % 16 == 0

    def body(chip_ref, x_ref, g_ref, w_ref, h_ref, o_ref):
        xv = x_ref[...]
        h_ref[...] = (xv * _rstd(xv) * g_ref[...]).astype(BF16)
        o_ref[...] = w_ref[...].astype(BF16)

    core, rr = _call(
        body, grid=(steps,), prefetch=(chip_arr,),
        in_specs=[pl.BlockSpec((tm, D_MODEL), lambda i, chip: (i, 0)), pl.BlockSpec((1, D_MODEL), lambda i, chip: (0, 0)),
                  pl.BlockSpec((None, tr, c), lambda i, chip: (layer, i, 0))],
        out_specs=[pl.BlockSpec((tm, D_MODEL), lambda i, chip: (i, 0)), pl.BlockSpec((None, tr, c), lambda i, chip: (chip[0], i, 0))],
        out_shape=[jax.ShapeDtypeStruct((s, D_MODEL), BF16), jax.ShapeDtypeStruct((N_CHIPS, r, c), BF16)],
        operands=(x, g, w), sem=("parallel",), name=name, riders=riders)
    return _ret(core, rr, riders)


def proj_residual_norm(a, w, x, bias, g_post, g_next, *, name, tm=512, sub=256, riders=()):
    s, k = a.shape
    tm = _row_tile(s, tm)
    sub = min(sub, tm)

    def body(a_ref, w_ref, x_ref, b_ref, gp_ref, gn_ref, xo_ref, h_ref, m_ref):
        for t in range(tm // sub):
            rows = slice(t * sub, (t + 1) * sub)
            mv = jnp.dot(a_ref[rows, :], w_ref[...], preferred_element_type=F32) + b_ref[...]
            m_ref[rows, :] = mv.astype(BF16)
            xn = x_ref[rows, :] + mv * _rstd(mv) * gp_ref[...]
            xo_ref[rows, :] = xn
            h_ref[rows, :] = (xn * _rstd(xn) * gn_ref[...]).astype(BF16)

    row, vec = _row_spec(tm, D_MODEL), _vec_spec(D_MODEL)
    core, rr = _call(
        body, grid=(s // tm,),
        in_specs=[_row_spec(tm, k), pl.BlockSpec((k, D_MODEL), lambda i: (0, 0)), row, vec, vec, vec], out_specs=[row, row, row],
        out_shape=[jax.ShapeDtypeStruct((s, D_MODEL), F32), jax.ShapeDtypeStruct((s, D_MODEL), BF16),
                   jax.ShapeDtypeStruct((s, D_MODEL), BF16)],
        operands=(a, w, x, bias, g_post, g_next), sem=("parallel",), name=name, riders=riders)
    return _ret(core, rr, riders)


def ffn_fwd_loss_rows(h, w_gu, w_d, x, g_post, target, *, name, tm=512, riders=()):
    s = x.shape[0]
    tm = _row_tile(s, tm)

    def body(h_ref, w0, w1, w2, w3, wd_ref, x_ref, g_ref, t_ref, d_ref, a_ref, dx_ref, df_ref, dg_ref, loss_ref):
        first = pl.program_id(0) == 0
        halves = [slice(half * FF_HALF, (half + 1) * FF_HALF) for half in (0, 1)]
        gain = g_ref[...]
        sub = min(SUB_ROWS, tm)
        sums = None
        for t in range(tm // sub):
            rows = slice(t * sub, (t + 1) * sub)
            hv = h_ref[rows, :]
            fv = None
            for cols, (wg_ref, wu_ref) in zip(halves, ((w0, w2), (w1, w3))):
                g = jnp.dot(hv, wg_ref[...], preferred_element_type=F32)
                u = jnp.dot(hv, wu_ref[...], preferred_element_type=F32)
                sig = _sigmoid(g)
                silu = g * sig
                d_ref[0, rows, cols] = (u * (sig + silu * (1.0 - sig))).astype(BF16)
                d_ref[1, rows, cols] = silu.astype(BF16)
                act = (silu * u).astype(BF16)
                a_ref[rows, cols] = act
                p = jnp.dot(act, wd_ref[cols, :], preferred_element_type=F32)
                fv = p if fv is None else fv + p
            err = x_ref[rows, :] + fv * _rstd(fv) * gain - t_ref[rows, :]
            dx = err * (1.0 / D_MODEL)
            dx_ref[rows, :] = dx
            df, dg = _rms_bwd(dx, fv, gain)
            df_ref[rows, :] = df.astype(BF16)
            part = (dg, jnp.sum(jnp.sum(err * err, axis=-1, keepdims=True), axis=0, keepdims=True) * (0.5 / D_MODEL))
            sums = part if sums is None else tuple(a + b for a, b in zip(sums, part))
        _accum(dg_ref, sums[0], first)
        _accum(loss_ref, jnp.broadcast_to(sums[1], (8, LANES)), first)

    def resident(shape, index):
        return pl.BlockSpec(shape, index, pipeline_mode=pl.Buffered(1))

    row, vec = _row_spec(tm, D_MODEL), _vec_spec(D_MODEL)
    shards = [resident((None, D_MODEL, FF_HALF), (lambda j: (lambda i: (j, 0, 0)))(j)) for j in range(N_CHIPS)]
    core, rr = _call(
        body, grid=(s // tm,),
        in_specs=[row] + shards + [resident((D_FF, D_MODEL), lambda i: (0, 0)), row, vec, row],
        out_specs=[pl.BlockSpec((2, tm, D_FF), lambda i: (0, i, 0)), _row_spec(tm, D_FF), row, row, vec,
                   pl.BlockSpec((8, LANES), lambda i: (0, 0))],
        out_shape=[jax.ShapeDtypeStruct((2, s, D_FF), BF16), jax.ShapeDtypeStruct((s, D_FF), BF16),
                   jax.ShapeDtypeStruct((s, D_MODEL), F32), jax.ShapeDtypeStruct((s, D_MODEL), BF16),
                   jax.ShapeDtypeStruct((1, D_MODEL), F32), jax.ShapeDtypeStruct((8, LANES), F32)],
        operands=(h, w_gu, w_gu, w_gu, w_gu, w_d, x, g_post, target), name=name, riders=riders, vmem_limit=BIG_VMEM_LIMIT)
    return _ret(core, rr, riders)


def dh_norm_bwd_pair(a, w, dres, x, g_pre, m, g_post, *, name, tm=512, sub=256, riders=()):
    _, kout, ns = w.shape
    planes = a.ndim == 3
    s = x.shape[0]
    tm = _row_tile(s, tm)
    sub = min(sub, tm)
    a_spec = pl.BlockSpec((2, tm, 2 * ns), lambda i: (0, i, 0)) if planes else pl.BlockSpec((tm, N_CHIPS * ns), lambda i: (i, 0))

    def body(a_ref, w0, w1, w2, w3, dres_ref, x_ref, gpre_ref, m_ref, gpost_ref, dx_ref, dm_ref, dgpre_ref, dgpost_ref, db_ref):
        first = pl.program_id(0) == 0
        sums = None
        for t in range(tm // sub):
            rows = slice(t * sub, (t + 1) * sub)
            dh = None
            for j, w_ref in enumerate((w0, w1, w2, w3)):
                a_j = a_ref[j // 2, rows, (j % 2) * ns:(j % 2 + 1) * ns] if planes else a_ref[rows, j * ns:(j + 1) * ns]
                p = lax.dot_general(a_j, w_ref[...], NT_DIMS, preferred_element_type=F32)
                dh = p if dh is None else dh + p
            d1, dgpre = _rms_bwd(dh, x_ref[rows, :], gpre_ref[...])
            dx = dres_ref[rows, :] + d1
            dx_ref[rows, :] = dx
            dm, dgpost = _rms_bwd(dx, m_ref[rows, :].astype(F32), gpost_ref[...])
            dm_ref[rows, :] = dm.astype(BF16)
            part = (dgpre, dgpost, jnp.sum(dm, axis=0, keepdims=True))
            sums = part if sums is None else tuple(u + v for u, v in zip(sums, part))
        _accum(dgpre_ref, sums[0], first)
        _accum(dgpost_ref, sums[1], first)
        _accum(db_ref, sums[2], first)

    def shard(j):
        return pl.BlockSpec((None, kout, ns), lambda i: (j, 0, 0))

    row, vec = _row_spec(tm, D_MODEL), _vec_spec(D_MODEL)
    vshape = jax.ShapeDtypeStruct((1, D_MODEL), F32)
    core, rr = _call(
        body, grid=(s // tm,), in_specs=[a_spec] + [shard(j) for j in range(N_CHIPS)] + [row, row, vec, row, vec],
        out_specs=[row, row, vec, vec, vec],
        out_shape=[jax.ShapeDtypeStruct((s, D_MODEL), F32), jax.ShapeDtypeStruct((s, D_MODEL), BF16), vshape, vshape, vshape],
        operands=(a, w, w, w, w, dres, x, g_pre, m, g_post), name=name, riders=riders)
    return _ret(core, rr, riders)


def ffn_bwd_rows(df, w_d, d_planes, w_gu, dres, x, g_pre, m, g_post, *, name, tm=512, riders=()):
    s = x.shape[0]
    tm = _row_tile(s, tm)

    def body(df_ref, wd_ref, d_ref, w0, w1, w2, w3, dres_ref, x_ref, gpre_ref, m_ref, gpost_ref,
             o_ref, dx_ref, dm_ref, dgpre_ref, dgpost_ref, db_ref):
        first = pl.program_id(0) == 0
        halves = [slice(half * FF_HALF, (half + 1) * FF_HALF) for half in (0, 1)]
        sub = min(SUB_ROWS, tm)
        sums = None
        for t in range(tm // sub):
            rows = slice(t * sub, (t + 1) * sub)
            dfv = df_ref[rows, :]
            dh = None
            for cols, (wg_ref, wu_ref) in zip(halves, ((w0, w2), (w1, w3))):
                da = lax.dot_general(dfv, wd_ref[cols, :], NT_DIMS, preferred_element_type=F32)
                dg = (da * d_ref[0, rows, cols].astype(F32)).astype(BF16)
                du = (da * d_ref[1, rows, cols].astype(F32)).astype(BF16)
                o_ref[0, rows, cols] = dg
                o_ref[1, rows, cols] = du
                p = lax.dot_general(dg, wg_ref[...], NT_DIMS, preferred_element_type=F32)
                p += lax.dot_general(du, wu_ref[...], NT_DIMS, preferred_element_type=F32)
                dh = p if dh is None else dh + p
            d1, dgpre = _rms_bwd(dh, x_ref[rows, :], gpre_ref[...])
            dx = dres_ref[rows, :] + d1
            dx_ref[rows, :] = dx
            dm, dgpost = _rms_bwd(dx, m_ref[rows, :].astype(F32), gpost_ref[...])
            dm_ref[rows, :] = dm.astype(BF16)
            part = (dgpre, dgpost, jnp.sum(dm, axis=0, keepdims=True))
            sums = part if sums is None else tuple(a + b for a, b in zip(sums, part))
        _accum(dgpre_ref, sums[0], first)
        _accum(dgpost_ref, sums[1], first)
        _accum(db_ref, sums[2], first)

    def resident(shape, index):
        return pl.BlockSpec(shape, index, pipeline_mode=pl.Buffered(1))

    planes = pl.BlockSpec((2, tm, D_FF), lambda i: (0, i, 0))
    row, vec = _row_spec(tm, D_MODEL), _vec_spec(D_MODEL)
    vshape = jax.ShapeDtypeStruct((1, D_MODEL), F32)
    shards = [resident((None, D_MODEL, FF_HALF), (lambda j: (lambda i: (j, 0, 0)))(j)) for j in range(N_CHIPS)]
    core, rr = _call(
        body, grid=(s // tm,),
        in_specs=[row, resident((D_FF, D_MODEL), lambda i: (0, 0)), planes] + shards + [row, row, vec, row, vec],
        out_specs=[planes, row, row, vec, vec, vec],
        out_shape=[jax.ShapeDtypeStruct((2, s, D_FF), BF16), jax.ShapeDtypeStruct((s, D_MODEL), F32),
                   jax.ShapeDtypeStruct((s, D_MODEL), BF16), vshape, vshape, vshape],
        operands=(df, w_d, d_planes, w_gu, w_gu, w_gu, w_gu, dres, x, g_pre, m, g_post), name=name, riders=riders,
        vmem_limit=BIG_VMEM_LIMIT)
    return _ret(core, rr, riders)


def dh_norm_bwd_last(a, w, dres, x, g_pre, *, name, tm=512, sub=256):
    _, kout, ns = w.shape
    s = x.shape[0]
    tm = _row_tile(s, tm)
    sub = min(sub, tm)

    def body(a_ref, w0, w1, w2, w3, dres_ref, x_ref, g_ref, dx_ref, dg_ref):
        total = None
        for t in range(tm // sub):
            rows = slice(t * sub, (t + 1) * sub)
            dh = None
            for j, w_ref in enumerate((w0, w1, w2, w3)):
                p = lax.dot_general(a_ref[rows, j * ns:(j + 1) * ns], w_ref[...], NT_DIMS, preferred_element_type=F32)
                dh = p if dh is None else dh + p
            d1, dg = _rms_bwd(dh, x_ref[rows, :], g_ref[...])
            dx_ref[rows, :] = dres_ref[rows, :] + d1
            total = dg if total is None else total + dg
        _accum(dg_ref, total, pl.program_id(0) == 0)

    def shard(j):
        return pl.BlockSpec((None, kout, ns), lambda i: (j, 0, 0))

    row, vec = _row_spec(tm, D_MODEL), _vec_spec(D_MODEL)
    (dx, dg), _ = _call(
        body, grid=(s // tm,), in_specs=[_row_spec(tm, N_CHIPS * ns)] + [shard(j) for j in range(N_CHIPS)] + [row, row, vec],
        out_specs=[row, vec], out_shape=[jax.ShapeDtypeStruct((s, D_MODEL), F32), jax.ShapeDtypeStruct((1, D_MODEL), F32)],
        operands=(a, w, w, w, w, dres, x, g_pre), name=name)
    return dx, dg


def _rope_tables(s):
    half = HEAD_DIM // 2
    inv_freq = np.float32(ROPE_THETA) ** (-(np.arange(half, dtype=np.float32) * np.float32(2.0)) / np.float32(HEAD_DIM))
    ang = np.arange(s, dtype=np.float32)[:, None] * inv_freq[None, :]
    cos, sin = np.cos(ang).astype(np.float32), np.sin(ang).astype(np.float32)
    return jnp.asarray(np.tile(cos, (1, 4))), jnp.asarray(np.concatenate([-sin, sin, -sin, sin], axis=1))


def _swap_halves(x):
    lane = lax.broadcasted_iota(I32, x.shape, 1)
    return jnp.where((lane & (HEAD_DIM - 1)) < HEAD_DIM // 2, pltpu.roll(x, LANES - 32, 1), pltpu.roll(x, 32, 1))


N_ROPE_BLOCKS = (Q_WIDTH + KV_WIDTH) // LANES


def qkv_proj(h, w, bias, cos, sin, casts, chip_arr, *, name, tm=1024, riders=()):
    s, k = h.shape
    ns = w.shape[2]
    tm = _row_tile(s, tm)
    nc = len(casts)

    def body(chip_ref, h_ref, w_ref, b_ref, c_ref, s_ref, *rest):
        cast_in, o_ref, cast_out = rest[:nc], rest[nc], rest[nc + 1:]
        j = pl.program_id(0)

        @pl.when(jnp.logical_and(j == 0, pl.program_id(1) == 0))
        def _():
            for src, dst in zip(cast_in, cast_out):
                dst[...] = src[...].astype(BF16)

        sub = min(256, tm)
        for t in range(tm // sub):
            rows = slice(t * sub, (t + 1) * sub)
            p = jnp.dot(h_ref[rows, :], w_ref[...], preferred_element_type=F32) + b_ref[...]
            cosv, sinv = c_ref[rows, :], s_ref[rows, :]
            for blk in range(ns // LANES):
                xb = p[:, blk * LANES:(blk + 1) * LANES]
                roped = xb * cosv + _swap_halves(xb) * sinv
                is_qk = j * (ns // LANES) + blk < N_ROPE_BLOCKS
                o_ref[rows, blk * LANES:(blk + 1) * LANES] = jnp.where(is_qk, roped, xb).astype(BF16)

    def cast_in_spec(wt, layer):
        return pl.BlockSpec((None,) + wt.shape[1:], lambda j, i, chip: (layer, 0, 0), pipeline_mode=pl.Buffered(1))

    def cast_out_spec(wt):
        return pl.BlockSpec((None,) + wt.shape[1:], lambda j, i, chip: (chip[0], 0, 0), pipeline_mode=pl.Buffered(1))

    core, rr = _call(
        body, grid=(N_CHIPS, s // tm), prefetch=(chip_arr,),
        in_specs=[pl.BlockSpec((tm, k), lambda j, i, chip: (i, 0)), pl.BlockSpec((None, k, ns), lambda j, i, chip: (j, 0, 0)),
                  pl.BlockSpec((1, ns), lambda j, i, chip: (0, j)), pl.BlockSpec((tm, LANES), lambda j, i, chip: (i, 0)),
                  pl.BlockSpec((tm, LANES), lambda j, i, chip: (i, 0))] + [cast_in_spec(wt, layer) for wt, layer in casts],
        out_specs=[pl.BlockSpec((tm, ns), lambda j, i, chip: (i, j))] + [cast_out_spec(wt) for wt, _ in casts],
        out_shape=[jax.ShapeDtypeStruct((s, N_CHIPS * ns), BF16)]
        + [jax.ShapeDtypeStruct((N_CHIPS,) + wt.shape[1:], BF16) for wt, _ in casts],
        operands=(h, w, bias, cos, sin, *[wt for wt, _ in casts]), sem=("arbitrary", "arbitrary"), name=name, riders=riders)
    return (core, rr) if riders else core


def rope_bwd(dq, dkc, dkp, dvc, dvp, cos, sin, *, name, riders=()):
    s = dq.shape[0]
    tm = 2 * WINDOW if s % (2 * WINDOW) == 0 else WINDOW
    nb = s // tm

    def body(dq_ref, dkc_ref, dkp_ref, dkp_next_ref, dvc_ref, dvp_ref, dvp_next_ref, c_ref, s_ref, o_ref, db_ref):
        i = pl.program_id(0)
        has_next = (i < nb - 1).astype(F32)
        cosv, sinv = c_ref[...], s_ref[...]

        def shifted(ref, next_ref, cols):
            last = has_next * next_ref[:WINDOW, cols].astype(F32)
            return last if tm == WINDOW else jnp.concatenate([ref[WINDOW:, cols].astype(F32), last], axis=0)

        parts = []
        for blk in range(QKV_WIDTH // LANES):
            if blk < Q_WIDTH // LANES:
                g = dq_ref[:, blk * LANES:(blk + 1) * LANES].astype(F32)
            else:
                own, prv, nxt = (dkc_ref, dkp_ref, dkp_next_ref) if blk < N_ROPE_BLOCKS else (dvc_ref, dvp_ref, dvp_next_ref)
                cols = slice((blk % 2) * LANES, (blk % 2 + 1) * LANES)
                g = own[:, cols].astype(F32) + shifted(prv, nxt, cols)
            if blk < N_ROPE_BLOCKS:
                g = g * cosv + _swap_halves(g * sinv)
            o_ref[:, blk * LANES:(blk + 1) * LANES] = g.astype(BF16)
            parts.append(jnp.sum(g, axis=0, keepdims=True))
        sums = jnp.concatenate(parts, axis=1)
        _accum(db_ref, sums, i == 0)

    own_spec = _row_spec(tm, KV_WIDTH)
    next_spec = pl.BlockSpec((tm, KV_WIDTH), lambda i: (jnp.minimum(i + 1, nb - 1), 0))
    core, rr = _call(
        body, grid=(nb,),
        in_specs=[_row_spec(tm, Q_WIDTH), own_spec, own_spec, next_spec, own_spec, own_spec, next_spec,
                  _row_spec(tm, LANES), _row_spec(tm, LANES)],
        out_specs=[_row_spec(tm, QKV_WIDTH), _vec_spec(QKV_WIDTH)],
        out_shape=[jax.ShapeDtypeStruct((s, QKV_WIDTH), BF16), jax.ShapeDtypeStruct((1, QKV_WIDTH), F32)],
        operands=(dq, dkc, dkp, dkp, dvc, dvp, dvp, cos, sin), name=name, riders=riders)
    return _ret(core, rr, riders)


ROWS = GQA_GROUP * WINDOW


def _prev_slots():
    kpos = lax.broadcasted_iota(I32, (WINDOW, ROWS), 0)
    qpos = lax.broadcasted_iota(I32, (WINDOW, ROWS), 1) & (WINDOW - 1)
    return kpos > qpos


def _head_cols(ref, head):
    return ref[:, head * HEAD_DIM:(head + 1) * HEAD_DIM]


def _stack_heads(ref, h):
    return jnp.concatenate([_head_cols(ref, GQA_GROUP * h + g) for g in range(GQA_GROUP)], axis=0)


def _band(prev_ref, cur_ref, h):
    return jnp.concatenate([_head_cols(prev_ref, h), _head_cols(cur_ref, h)], axis=0)


def _pick(prev, band):
    return jnp.where(prev, band[:WINDOW], band[WINDOW:])


def _spread(prev, x):
    return jnp.concatenate([jnp.where(prev, x, 0.0), jnp.where(prev, 0.0, x)], axis=0).astype(BF16)


def _attn_probs(s_band, sink, prev, has_prev):
    scale = HEAD_DIM ** -0.5
    s = jnp.where(prev, jnp.where(has_prev, s_band[:WINDOW], NEG), s_band[WINDOW:]) * scale
    m = jnp.maximum(jnp.max(s, axis=0, keepdims=True), sink)
    e, es = jnp.exp(s - m), jnp.exp(sink - m)
    inv = 1.0 / (jnp.sum(e, axis=0, keepdims=True) + es)
    return e * inv, es * inv


def _attn_specs(nb):
    kcol, vcol = Q_WIDTH // KV_WIDTH, Q_WIDTH // KV_WIDTH + 1
    q_spec = pl.BlockSpec((WINDOW, Q_WIDTH), lambda n: (n, 0))
    return [q_spec,
            pl.BlockSpec((WINDOW, KV_WIDTH), lambda n: (n, kcol)),
            pl.BlockSpec((WINDOW, KV_WIDTH), lambda n: (jnp.maximum(n - 1, 0), kcol)),
            pl.BlockSpec((WINDOW, KV_WIDTH), lambda n: (n, vcol)),
            pl.BlockSpec((WINDOW, KV_WIDTH), lambda n: (jnp.maximum(n - 1, 0), vcol)),
            pl.BlockSpec((N_KV_HEADS, 8, ROWS), lambda n: (0, 0, 0))]


def attn_fwd(qkv, sink_rows, *, name, riders=()):
    s = qkv.shape[0]

    def body(q_ref, kc_ref, kp_ref, vc_ref, vp_ref, sink_ref, o_ref):
        prev = _prev_slots()
        has_prev = pl.program_id(0) > 0
        heads = range(N_KV_HEADS)
        s_bands = [lax.dot_general(_band(kp_ref, kc_ref, h), _stack_heads(q_ref, h), NT_DIMS, preferred_element_type=F32)
                   for h in heads]
        p_bands = [_spread(prev, _attn_probs(s_bands[h], sink_ref[h, 0:1, :], prev, has_prev)[0]) for h in heads]
        outs = [lax.dot_general(_band(vp_ref, vc_ref, h), p_bands[h], TN_DIMS, preferred_element_type=F32).T for h in heads]
        for h in heads:
            for g in range(GQA_GROUP):
                head = GQA_GROUP * h + g
                o_ref[:, head * HEAD_DIM:(head + 1) * HEAD_DIM] = outs[h][g * WINDOW:(g + 1) * WINDOW].astype(BF16)

    core, rr = _call(
        body, grid=(s // WINDOW,), in_specs=_attn_specs(s // WINDOW), out_specs=[pl.BlockSpec((WINDOW, Q_WIDTH), lambda n: (n, 0))],
        out_shape=[jax.ShapeDtypeStruct((s, Q_WIDTH), BF16)], operands=(qkv, qkv, qkv, qkv, qkv, sink_rows), sem=("parallel",),
        name=name, riders=riders)
    return _ret(core, rr, riders)


def attn_bwd(qkv, sink_rows, do, *, name, riders=()):
    s = qkv.shape[0]

    def body(q_ref, kc_ref, kp_ref, vc_ref, vp_ref, sink_ref, do_ref, dq_ref, dkc_ref, dkp_ref, dvc_ref, dvp_ref, dsink_ref):
        n = pl.program_id(0)
        prev = _prev_slots()
        scale = HEAD_DIM ** -0.5
        heads = range(N_KV_HEADS)
        qs, dos = [_stack_heads(q_ref, h) for h in heads], [_stack_heads(do_ref, h) for h in heads]
        kbands, vbands = [_band(kp_ref, kc_ref, h) for h in heads], [_band(vp_ref, vc_ref, h) for h in heads]
        s_bands = [lax.dot_general(kbands[h], qs[h], NT_DIMS, preferred_element_type=F32) for h in heads]
        dp_bands = [lax.dot_general(vbands[h], dos[h], NT_DIMS, preferred_element_type=F32) for h in heads]
        ds_bands, p_bands, parts = [], [], []
        for h in heads:
            p, ps = _attn_probs(s_bands[h], sink_ref[h, 0:1, :], prev, n > 0)
            dp = _pick(prev, dp_bands[h])
            delta = jnp.sum(p * dp, axis=0, keepdims=True)
            ds_bands.append(_spread(prev, p * (dp - delta) * scale))
            p_bands.append(_spread(prev, p))
            dsink = -(ps * delta)
            for g in range(GQA_GROUP):
                parts.append(jnp.broadcast_to(jnp.sum(dsink[:, g * WINDOW:(g + 1) * WINDOW], axis=1, keepdims=True), (8, LANES)))
        for h in heads:
            dk = jnp.dot(ds_bands[h], qs[h], preferred_element_type=F32).astype(BF16)
            dv = jnp.dot(p_bands[h], dos[h], preferred_element_type=F32).astype(BF16)
            dq = lax.dot_general(kbands[h], ds_bands[h], TN_DIMS, preferred_element_type=F32).T
            cols = slice(h * HEAD_DIM, (h + 1) * HEAD_DIM)
            dkp_ref[:, cols], dkc_ref[:, cols] = dk[:WINDOW], dk[WINDOW:]
            dvp_ref[:, cols], dvc_ref[:, cols] = dv[:WINDOW], dv[WINDOW:]
            for g in range(GQA_GROUP):
                head = GQA_GROUP * h + g
                dq_ref[:, head * HEAD_DIM:(head + 1) * HEAD_DIM] = dq[g * WINDOW:(g + 1) * WINDOW].astype(BF16)

        @pl.when(n == 0)
        def _():
            for i, part in enumerate(parts):
                dsink_ref[i // GQA_GROUP, i % GQA_GROUP] = part

        @pl.when(n > 0)
        def _():
            for i, part in enumerate(parts):
                dsink_ref[i // GQA_GROUP, i % GQA_GROUP] += part

    rows_q = pl.BlockSpec((WINDOW, Q_WIDTH), lambda n: (n, 0))
    rows_kv = pl.BlockSpec((WINDOW, KV_WIDTH), lambda n: (n, 0))
    kv_shape = jax.ShapeDtypeStruct((s, KV_WIDTH), BF16)
    core, rr = _call(
        body, grid=(s // WINDOW,), in_specs=_attn_specs(s // WINDOW) + [rows_q],
        out_specs=[rows_q, rows_kv, rows_kv, rows_kv, rows_kv,
                   pl.BlockSpec((N_KV_HEADS, GQA_GROUP, 8, LANES), lambda n: (0, 0, 0, 0))],
        out_shape=[jax.ShapeDtypeStruct((s, Q_WIDTH), BF16), kv_shape, kv_shape, kv_shape, kv_shape,
                   jax.ShapeDtypeStruct((N_KV_HEADS, GQA_GROUP, 8, LANES), F32)],
        operands=(qkv, qkv, qkv, qkv, qkv, sink_rows, do), sem=("arbitrary",), name=name, riders=riders)
    return _ret(core, rr, riders)


GELU_C = 0.7978845608028654
GELU_A = 0.044715


def _gelu(x):
    return 0.5 * x * (1.0 + jnp.tanh(x * (GELU_C + (GELU_C * GELU_A) * (x * x))))


def _gelu_and_grad(x):
    x2 = x * x
    t = jnp.tanh(x * (GELU_C + (GELU_C * GELU_A) * x2))
    half_x, one_t = 0.5 * x, 1.0 + t
    return half_x * one_t, 0.5 * one_t + half_x * (1.0 - t * t) * (GELU_C + (3.0 * GELU_C * GELU_A) * x2)


def _tril_bf16(w):
    row = lax.broadcasted_iota(I32, (SGU_CHUNK, SGU_CHUNK), 0)
    col = lax.broadcasted_iota(I32, (SGU_CHUNK, SGU_CHUNK), 1)
    return jnp.where(row >= col, w, 0.0).astype(BF16)


def _sgu_norm(vg, g, b):
    mu = jnp.mean(vg, axis=-1, keepdims=True)
    cen = vg - mu
    rstd = lax.rsqrt(jnp.mean(cen * cen, axis=-1, keepdims=True) + EPS)
    xhat = cen * rstd
    return xhat, rstd, xhat * g + b


def sgu_in_fwd(h, w_in, ln_g, ln_b, w_sp, b_sp, *, name, tm=512, riders=()):
    s, k = h.shape
    ns = w_in.shape[2]
    tm = _row_tile(s, tm)

    def body(h_ref, w0, w1, w2, w3, g_ref, b_ref, w_ref, bs_ref, z_ref, y_ref):
        hv = h_ref[...]
        zs = [jnp.dot(hv, w_ref_j[...], preferred_element_type=F32) for w_ref_j in (w0, w1, w2, w3)]
        for j, zj in enumerate(zs):
            z_ref[:, j * ns:(j + 1) * ns] = zj.astype(BF16)
        u = _gelu(jnp.concatenate(zs[:2], axis=1))
        _, _, vn = _sgu_norm(_gelu(jnp.concatenate(zs[2:], axis=1)), g_ref[...], b_ref[...])
        vn = vn.astype(BF16)
        for grp in range(SGU_GROUPS):
            w = _tril_bf16(w_ref[grp])
            cols = slice(grp * LANES, (grp + 1) * LANES)
            for ch in range(tm // SGU_CHUNK):
                rows = slice(ch * SGU_CHUNK, (ch + 1) * SGU_CHUNK)
                mixed = jnp.dot(w, vn[rows, cols], preferred_element_type=F32) + bs_ref[grp]
                y_ref[rows, cols] = (u[rows, cols] * mixed).astype(BF16)

    def shard(j):
        return pl.BlockSpec((None, k, ns), lambda i: (j, 0, 0))

    full3 = pl.BlockSpec((SGU_GROUPS, SGU_CHUNK, SGU_CHUNK), lambda i: (0, 0, 0))
    core, rr = _call(
        body, grid=(s // tm,),
        in_specs=[_row_spec(tm, k)] + [shard(j) for j in range(N_CHIPS)] + [_vec_spec(D_MODEL), _vec_spec(D_MODEL), full3, full3],
        out_specs=[_row_spec(tm, 2 * D_MODEL), _row_spec(tm, D_MODEL)],
        out_shape=[jax.ShapeDtypeStruct((s, 2 * D_MODEL), BF16), jax.ShapeDtypeStruct((s, D_MODEL), BF16)],
        operands=(h, w_in, w_in, w_in, w_in, ln_g, ln_b, w_sp, b_sp), sem=("parallel",), name=name, riders=riders)
    return _ret(core, rr, riders)


def sgu_bwd(z, dy, ln_g, ln_b, w_sp, b_sp, *, name, tm=256, riders=()):
    s = z.shape[0]
    tm = _row_tile(s, tm)

    def body(z_ref, dy_ref, g_ref, b_ref, w_ref, bs_ref, dz_ref, dw_ref, dbs_ref, dg_ref, db_ref, dvn_buf):
        first = pl.program_id(0) == 0
        u, u_grad = _gelu_and_grad(z_ref[:, :D_MODEL].astype(F32))
        vg, v_grad = _gelu_and_grad(z_ref[:, D_MODEL:].astype(F32))
        xhat, rstd, vn = _sgu_norm(vg, g_ref[...], b_ref[...])
        vn = vn.astype(BF16)
        dyv = dy_ref[...]
        dmixed = dyv * u
        dz_gate = dyv * u_grad
        row = lax.broadcasted_iota(I32, (SGU_CHUNK, SGU_CHUNK), 0)
        col = lax.broadcasted_iota(I32, (SGU_CHUNK, SGU_CHUNK), 1)
        dws, dbss = [], []
        for grp in range(SGU_GROUPS):
            w = _tril_bf16(w_ref[grp])
            cols = slice(grp * LANES, (grp + 1) * LANES)
            dw = jnp.zeros((SGU_CHUNK, SGU_CHUNK), F32)
            dbs = jnp.zeros((SGU_CHUNK, 1), F32)
            for ch in range(tm // SGU_CHUNK):
                rows = slice(ch * SGU_CHUNK, (ch + 1) * SGU_CHUNK)
                vblk = vn[rows, cols]
                mixed = jnp.dot(w, vblk, preferred_element_type=F32) + bs_ref[grp]
                dz_ref[rows, cols] = (dz_gate[rows, cols] * mixed).astype(BF16)
                dm = dmixed[rows, cols]
                dmb = dm.astype(BF16)
                dvn_buf[rows, cols] = lax.dot_general(w, dmb, TN_DIMS, preferred_element_type=F32)
                dw += lax.dot_general(dmb, vblk, NT_DIMS, preferred_element_type=F32)
                dbs += jnp.sum(dm, axis=-1, keepdims=True)
            dws.append(jnp.where(row >= col, dw, 0.0))
            dbss.append(jnp.broadcast_to(dbs, (SGU_CHUNK, SGU_CHUNK)))

        dvn = dvn_buf[...]
        dxhat = dvn * g_ref[...]
        dvg = rstd * (dxhat - jnp.mean(dxhat, axis=-1, keepdims=True) - xhat * jnp.mean(dxhat * xhat, axis=-1, keepdims=True))
        dz_ref[:, D_MODEL:] = (dvg * v_grad).astype(BF16)
        dlng, dlnb = jnp.sum(dvn * xhat, axis=0, keepdims=True), jnp.sum(dvn, axis=0, keepdims=True)

        @pl.when(first)
        def _():
            for grp in range(SGU_GROUPS):
                dw_ref[grp] = dws[grp]
                dbs_ref[grp] = dbss[grp]
            dg_ref[...] = dlng
            db_ref[...] = dlnb

        @pl.when(jnp.logical_not(first))
        def _():
            for grp in range(SGU_GROUPS):
                dw_ref[grp] += dws[grp]
                dbs_ref[grp] += dbss[grp]
            dg_ref[...] += dlng
            db_ref[...] += dlnb

    full3 = pl.BlockSpec((SGU_GROUPS, SGU_CHUNK, SGU_CHUNK), lambda i: (0, 0, 0))
    s3 = jax.ShapeDtypeStruct((SGU_GROUPS, SGU_CHUNK, SGU_CHUNK), F32)
    vshape = jax.ShapeDtypeStruct((1, D_MODEL), F32)
    core, rr = _call(
        body, grid=(s // tm,),
        in_specs=[_row_spec(tm, 2 * D_MODEL), _row_spec(tm, D_MODEL), _vec_spec(D_MODEL), _vec_spec(D_MODEL), full3, full3],
        out_specs=[_row_spec(tm, 2 * D_MODEL), full3, full3, _vec_spec(D_MODEL), _vec_spec(D_MODEL)],
        out_shape=[jax.ShapeDtypeStruct((s, 2 * D_MODEL), BF16), s3, s3, vshape, vshape],
        scratch_shapes=[pltpu.VMEM((tm, D_MODEL), F32)], operands=(z, dy, ln_g, ln_b, w_sp, b_sp), name=name, riders=riders)
    return _ret(core, rr, riders)


def _sigmoid(x):
    return 1.0 / (1.0 + jnp.exp(-x))


def ffn_up(h, w_gu, *, name, tm=512, riders=()):
    s = h.shape[0]
    tm = _row_tile(s, tm)

    def body(h_ref, wg_ref, wu_ref, d_ref, a_ref):
        hv = h_ref[...]
        sub = min(256, tm)
        for t in range(tm // sub):
            rows = slice(t * sub, (t + 1) * sub)
            g = jnp.dot(hv[rows], wg_ref[...], preferred_element_type=F32)
            u = jnp.dot(hv[rows], wu_ref[...], preferred_element_type=F32)
            sig = _sigmoid(g)
            silu = g * sig
            d_ref[0, rows, :] = (u * (sig + silu * (1.0 - sig))).astype(BF16)
            d_ref[1, rows, :] = silu.astype(BF16)
            a_ref[rows, :] = (silu * u).astype(BF16)

    core, rr = _call(
        body, grid=(2, s // tm),
        in_specs=[pl.BlockSpec((tm, D_MODEL), lambda j, i: (i, 0)),
                  pl.BlockSpec((None, D_MODEL, FF_HALF), lambda j, i: (j, 0, 0)),
                  pl.BlockSpec((None, D_MODEL, FF_HALF), lambda j, i: (j + 2, 0, 0))],
        out_specs=[pl.BlockSpec((2, tm, FF_HALF), lambda j, i: (0, i, j)), pl.BlockSpec((tm, FF_HALF), lambda j, i: (i, j))],
        out_shape=[jax.ShapeDtypeStruct((2, s, D_FF), BF16), jax.ShapeDtypeStruct((s, D_FF), BF16)],
        operands=(h, w_gu, w_gu), sem=("parallel", "parallel"), name=name, riders=riders)
    return _ret(core, rr, riders)


def _weight_tile(rows):
    for tr in (512, 352, 256, 128):
        if rows % tr == 0:
            return tr
    return rows


def place_shard(w, layer, chip_arr, dtype, *, name, riders=()):
    _, r, c = w.shape
    tr = _weight_tile(r)

    def body(chip_ref, w_ref, o_ref):
        o_ref[...] = w_ref[...].astype(dtype)

    core, rr = _call(
        body, grid=(r // tr,), prefetch=(chip_arr,),
        in_specs=[pl.BlockSpec((None, tr, c), lambda i, chip: (layer, i, 0))],
        out_specs=[pl.BlockSpec((None, tr, c), lambda i, chip: (chip[0], i, 0))],
        out_shape=[jax.ShapeDtypeStruct((N_CHIPS, r, c), dtype)], operands=(w,), sem=("parallel",), name=name, riders=riders)
    return _ret(core, rr, riders)


def _adamw_math(w, g, m, v):
    m = ADAM_B1 * m + (1.0 - ADAM_B1) * g
    v = ADAM_B2 * v + (1.0 - ADAM_B2) * (g * g)
    m_hat = m / (1.0 - ADAM_B1 ** ADAM_STEP)
    v_hat = v / (1.0 - ADAM_B2 ** ADAM_STEP)
    delta = -ADAM_LR * (m_hat / (jnp.sqrt(v_hat) + ADAM_EPS) + ADAM_WD * w)
    return delta, m, v


def adamw(w, g, m, v, *, name, after=None):
    nl, r, c = w.shape
    tr = _weight_tile(r)

    def body(w_ref, g_ref, m_ref, v_ref, *rest):
        go_ref, d_ref, mo_ref, vo_ref = rest[-4:]
        gv = g_ref[...]
        go_ref[...] = gv
        d_ref[...], mo_ref[...], vo_ref[...] = _adamw_math(w_ref[...], gv, m_ref[...], v_ref[...])

    spec = pl.BlockSpec((None, tr, c), lambda l, i: (l, i, 0))
    shape = jax.ShapeDtypeStruct(w.shape, F32)
    extra = [] if after is None else [after]
    outs, _ = _call(body, grid=(nl, r // tr), in_specs=[spec] * 4 + [ANY] * len(extra), out_specs=[spec] * 4,
                    out_shape=[shape] * 4, operands=(w, g, m, v, *extra), sem=("parallel", "parallel"), name=name)
    return outs


def adamw_small(ws, gs, ms, vs, *, name):
    n = len(ws)

    def body(*refs):
        ins, outs = refs[:4 * n], refs[4 * n:]
        for t in range(n):
            gv = ins[n + t][...]
            outs[t][...] = gv
            outs[n + t][...], outs[2 * n + t][...], outs[3 * n + t][...] = _adamw_math(
                ins[t][...], gv, ins[2 * n + t][...], ins[3 * n + t][...])

    shapes = [jax.ShapeDtypeStruct(w.shape, F32) for w in ws]
    res = pl.pallas_call(body, out_shape=shapes * 4, name=name)(*ws, *gs, *ms, *vs)
    return res[:n], res[n:2 * n], res[2 * n:3 * n], res[3 * n:]


def pair_add(g, r1, c_arr, *, name):
    _, rows, cdim = g.shape
    h = rows // 2

    def body(c_ref, g_ref, r_ref, o_ref):
        o_ref[...] = (g_ref[...].astype(F32) + r_ref[...].astype(F32)).astype(o_ref.dtype)

    (out,), _ = _call(
        body, grid=(N_CHIPS,), prefetch=(c_arr,),
        in_specs=[pl.BlockSpec((None, h, cdim), lambda s, c: (s, c[0], 0)), pl.BlockSpec((None, h, cdim), lambda s, c: (s, 0, 0))],
        out_specs=[pl.BlockSpec((None, h, cdim), lambda s, c: (s, 0, 0))],
        out_shape=[jax.ShapeDtypeStruct((N_CHIPS, h, cdim), g.dtype)], operands=(g, r1), sem=("parallel",), name=name)
    return out


def final_add(g, r1, r2, jc_arr, *, dest_shape, lead, prev, name):
    _, rows, cdim = g.shape
    h = rows // 2

    def body(jc_ref, g_ref, r1_ref, r2_ref, *rest):
        o_ref = rest[-1]
        acc = g_ref[...].astype(F32) + r1_ref[...].astype(F32)
        for k in range(3):
            acc = acc + r2_ref[k].astype(F32)
        o_ref[...] = acc

    if lead is None:
        o_spec = pl.BlockSpec((h, cdim), lambda i, jc: (jc[1], 0))
    elif lead == "chip":
        o_spec = pl.BlockSpec((None, h, cdim), lambda i, jc: (jc[0], jc[1], 0))
    else:
        o_spec = pl.BlockSpec((None, h, cdim), lambda i, jc: (lead, jc[1], 0))
    in_specs = [pl.BlockSpec((None, h, cdim), lambda i, jc: (jc[0], jc[1], 0)),
                pl.BlockSpec((None, h, cdim), lambda i, jc: (jc[0], 0, 0)),
                pl.BlockSpec((3, h, cdim), lambda i, jc: (0, 0, 0))]
    operands = [g, r1, r2]
    aliases = None
    if prev is not None:
        in_specs.append(ANY)
        operands.append(prev)
        aliases = {3: 0}
    (out,), _ = _call(body, grid=(1,), prefetch=(jc_arr,), in_specs=in_specs, out_specs=[o_spec],
                      out_shape=[jax.ShapeDtypeStruct(dest_shape, F32)], operands=operands, aliases=aliases, name=name)
    return out


def _place():
    return lax.axis_index("x"), lax.axis_index("y"), lax.axis_index("c")


def _partner(x, y, k):
    return (1 - x if k >> 1 else x), (1 - y if k & 1 else y)


WHOLE = (0, 1, 1)


def _half(rows, sel, dtype, piece=WHOLE):
    lo, hi, n = piece
    align = 16 if dtype == BF16 else 8
    step = rows // 2 // n
    assert rows // 2 == step * n and step % align == 0
    return pl.ds(pl.multiple_of(sel * (rows // 2) + lo * step, align), (hi - lo) * step)


def _rider(peers, inputs, aliased, fresh, nsem, copies, arrivals):
    def start(ins, outs, send, recv):
        for cp in copies(ins, outs, send, recv):
            cp.start()

    def finish(ins, outs, send, recv):
        for cp in arrivals(ins, outs, send, recv):
            cp.wait_recv()
        for cp in copies(ins, outs, send, recv):
            cp.wait_send()

    return types.SimpleNamespace(peers=peers, inputs=list(inputs), aliased=list(aliased), fresh=list(fresh), nsem=nsem,
                                 start=start, finish=finish)


def _remote(src, dst, send, recv, idx, dev):
    return pltpu.make_async_remote_copy(src_ref=src, dst_ref=dst, send_sem=send.at[idx], recv_sem=recv.at[idx],
                                        device_id=dev, device_id_type=MESH)


def gather_ici_rider(fulls, pieces=None):
    nt = len(fulls)
    pieces = pieces or [WHOLE] * nt

    def region(outs, t, slot, sel):
        return outs[t].at[slot, _half(fulls[t].shape[1], sel, fulls[t].dtype, pieces[t])]

    def copies(ins, outs, send, recv):
        x, y, c = _place()
        res = []
        for t in range(nt):
            for k in (1, 2, 3):
                px, py = _partner(x, y, k)
                mine = region(outs, t, 2 * x + y, c)
                res.append(_remote(mine, mine, send, recv, 3 * t + k - 1, (px, py, c)))
        return res

    def arrivals(ins, outs, send, recv):
        x, y, c = _place()
        res = []
        for t in range(nt):
            for k in (1, 2, 3):
                px, py = _partner(x, y, k)
                theirs = region(outs, t, 2 * px + py, c)
                res.append(_remote(theirs, theirs, send, recv, 3 * t + k - 1, (x, y, c)))
        return res

    return _rider("chips", fulls, range(nt), [], 3 * nt, copies, arrivals)


def gather_d2d_rider(fulls, pieces=None):
    nt = len(fulls)
    pieces = pieces or [WHOLE] * nt

    def region(outs, t, slot, sel):
        return outs[t].at[slot, _half(fulls[t].shape[1], sel, fulls[t].dtype, pieces[t])]

    def both(outs, send, recv, mine):
        x, y, c = _place()
        res = []
        for t in range(nt):
            for k in (1, 2, 3):
                px, py = _partner(x, y, k)
                part = region(outs, t, 2 * px + py, c if mine else 1 - c)
                res.append(_remote(part, part, send, recv, 3 * t + k - 1, (x, y, 1 - c)))
        return res

    return _rider("sibling", fulls, range(nt), [], 3 * nt, lambda i, o, s, r: both(o, s, r, True),
                  lambda i, o, s, r: both(o, s, r, False))


def exchange_rider(grads):
    nt = len(grads)

    def both(ins, outs, send, recv):
        x, y, c = _place()
        return [_remote(ins[t].at[:, _half(grads[t].shape[1], 1 - c, grads[t].dtype)], outs[t], send, recv, t, (x, y, 1 - c))
                for t in range(nt)]

    fresh = [jax.ShapeDtypeStruct((N_CHIPS, g.shape[1] // 2, g.shape[2]), g.dtype) for g in grads]
    return _rider("sibling", grads, [], fresh, nt, both, both)


def scatter_rider(parts):
    nt = len(parts)

    def both(ins, outs, send, recv):
        x, y, c = _place()
        res = []
        for t in range(nt):
            for k in (1, 2, 3):
                px, py = _partner(x, y, k)
                res.append(_remote(ins[t].at[2 * px + py], outs[t].at[k - 1], send, recv, 3 * t + k - 1, (px, py, c)))
        return res

    fresh = [jax.ShapeDtypeStruct((3,) + p.shape[1:], p.dtype) for p in parts]
    return _rider("chips", parts, [], fresh, 3 * nt, both, both)


def broadcast_rider(bufs, items):
    def region(outs, item, sel):
        bi, lead = item
        ref = outs[bi]
        if lead == "chip":
            x, y, _ = _place()
            ref = ref.at[2 * x + y]
        elif lead is not None:
            ref = ref.at[lead]
        return ref.at[_half(ref.shape[0], sel, F32)]

    def both(outs, send, recv, mine):
        x, y, c = _place()
        res = []
        for i, item in enumerate(items):
            part = region(outs, item, c if mine else 1 - c)
            res.append(_remote(part, part, send, recv, i, (x, y, 1 - c)))
        return res

    return _rider("sibling", bufs, range(len(bufs)), [], len(items), lambda i, o, s, r: both(o, s, r, True),
                  lambda i, o, s, r: both(o, s, r, False))


def allcast_rider(buf):
    peers = [(k, flip) for k in range(N_CHIPS) for flip in (0, 1) if (k, flip) != (0, 0)]

    def both(outs, send, recv, mine):
        x, y, c = _place()
        res = []
        for i, (k, flip) in enumerate(peers):
            px, py = _partner(x, y, k)
            pc = 1 - c if flip else c
            slot, sel = (2 * x + y, c) if mine else (2 * px + py, pc)
            part = outs[0].at[slot, _half(buf.shape[1], sel, F32)]
            res.append(_remote(part, part, send, recv, i, (px, py, pc)))
        return res

    return _rider("everyone", [buf], [0], [], len(peers), lambda i, o, s, r: both(o, s, r, True),
                  lambda i, o, s, r: both(o, s, r, False))


def comm_call(riders, *, name):
    _, res = _call(None, riders=riders, name=name)
    return res


SEMS = pl.BlockSpec(memory_space=pltpu.SEMAPHORE)
SIDE_EFFECT = pltpu.SideEffectType.DATAFLOW_SIDE_EFFECTING


def _split_refs(riders, refs):
    views, p = [], 0
    for r in riders:
        bufs = refs[p:p + len(r.inputs) + len(r.fresh)]
        p += len(bufs)
        ins = bufs[:len(r.inputs)]
        views.append([ins, [ins[i] for i in r.aliased] + list(bufs[len(r.inputs):])])
    for view in views:
        view += [refs[p], refs[p + 1]]
        p += 2
    return views


def comm_start(riders, *, name):
    kind = _peer_kind(riders)
    bufs = [a for r in riders for a in r.inputs]
    fresh = [f for r in riders for f in r.fresh]
    n_buf, n_fresh = len(bufs), len(fresh)

    def body(*refs):
        ins, outs = refs[:n_buf], refs[n_buf:]
        through, land, sems = outs[:n_buf], outs[n_buf:n_buf + n_fresh], outs[n_buf + n_fresh:-1]
        _peer_barrier(kind)
        per_rider, pb, pf = [], 0, 0
        for r in riders:
            per_rider += list(through[pb:pb + len(r.inputs)]) + list(land[pf:pf + len(r.fresh)])
            pb, pf = pb + len(r.inputs), pf + len(r.fresh)
        for r, (r_ins, r_outs, send, recv) in zip(riders, _split_refs(riders, per_rider + list(sems))):
            r.start(r_ins, r_outs, send, recv)
        outs[-1][...] = jnp.zeros((8, LANES), F32)

    sem_shapes = [pltpu.SemaphoreType.DMA((r.nsem,)) for r in riders for _ in (0, 1)]
    res = pl.pallas_call(
        body, name=name, in_specs=[ANY] * n_buf,
        out_specs=[ANY] * (n_buf + n_fresh) + [SEMS] * len(sem_shapes) + [pl.BlockSpec(memory_space=pltpu.VMEM)],
        out_shape=[jax.ShapeDtypeStruct(a.shape, a.dtype) for a in bufs] + fresh + sem_shapes
        + [jax.ShapeDtypeStruct((8, LANES), F32)],
        input_output_aliases={i: i for i in range(n_buf)},
        compiler_params=pltpu.CompilerParams(has_side_effects=SIDE_EFFECT, collective_id=PEER_KINDS.index(kind)))(*bufs)
    return (riders, list(res[:n_buf + n_fresh]), list(res[n_buf + n_fresh:-1])), res[-1]


def comm_wait(state, after, *, name):
    riders, bufs, sems = state
    n_buf, n_sem = len(bufs), len(sems)
    n_in = sum(len(r.inputs) for r in riders)

    def body(*refs):
        held, sem_refs = refs[:n_buf], refs[n_buf:n_buf + n_sem]
        through, land = held[:n_in], held[n_in:]
        per_rider, pb, pf = [], 0, 0
        for r in riders:
            per_rider += list(through[pb:pb + len(r.inputs)]) + list(land[pf:pf + len(r.fresh)])
            pb, pf = pb + len(r.inputs), pf + len(r.fresh)
        for r, (r_ins, r_outs, send, recv) in zip(riders, _split_refs(riders, per_rider + list(sem_refs))):
            r.finish(r_ins, r_outs, send, recv)

    res = pl.pallas_call(
        body, name=name, in_specs=[ANY] * n_buf + [SEMS] * n_sem + [ANY], out_specs=[ANY] * n_buf,
        out_shape=[jax.ShapeDtypeStruct(a.shape, a.dtype) for a in bufs],
        input_output_aliases={i: i for i in range(n_buf)},
        compiler_params=pltpu.CompilerParams(has_side_effects=SIDE_EFFECT))(*bufs, *sems, after)
    through, land = list(res[:n_in]), list(res[n_in:])
    out, pb, pf = [], 0, 0
    for r in riders:
        r_ins, r_land = through[pb:pb + len(r.inputs)], land[pf:pf + len(r.fresh)]
        pb, pf = pb + len(r.inputs), pf + len(r.fresh)
        out.append([r_ins[i] for i in r.aliased] + r_land)
    return out


SLAB_ROWS = 192


def _pad_rows(a, rows=8):
    return jnp.pad(a, ((0, rows - a.shape[0]), (0, 0)))


def _pack_small(norm_grads, db_qkv, db_o, dsinks, db_sp, dln_g, dln_b, dw_sp, loss_part):
    parts = [
        jnp.concatenate(norm_grads, axis=0),
        _pad_rows(jnp.pad(db_qkv, ((0, 0), (0, 2 * D_MODEL - QKV_WIDTH))).reshape(2, D_MODEL)),
        _pad_rows(db_o),
        _pad_rows(jnp.pad(dsinks.reshape(1, N_Q_HEADS), ((0, 0), (0, D_MODEL - N_Q_HEADS)))),
        _pad_rows(db_sp.reshape(1, D_MODEL)),
        _pad_rows(jnp.concatenate([dln_g, dln_b, jnp.pad(loss_part[0:1], ((0, 0), (0, D_MODEL - LANES)))], axis=0)),
        dw_sp.reshape(SGU_CHUNK, D_MODEL),
    ]
    slab = jnp.concatenate(parts, axis=0)
    return jnp.pad(slab, ((0, SLAB_ROWS - slab.shape[0]), (0, 0))).reshape(N_CHIPS, SLAB_ROWS // N_CHIPS, D_MODEL)


def _unpack_small(slab, j):
    slab = slab.reshape(SLAB_ROWS, D_MODEL)
    norms = [slab[2 * i:2 * i + 2] for i in range(4)]
    db_qkv = slab[8:10].reshape(1, 2 * D_MODEL)[:, :QKV_WIDTH]
    db_o = slab[16:17]
    dsinks = slab[24:25, :N_Q_HEADS]
    db_sp = slab[32:33].reshape(SGU_GROUPS, SGU_CHUNK)
    width = D_MODEL // N_CHIPS
    dln_g = lax.dynamic_slice(slab[40:41], (0, j * width), (1, width))
    dln_b = lax.dynamic_slice(slab[41:42], (0, j * width), (1, width))
    dw_sp = slab[48:48 + SGU_CHUNK].reshape(SGU_GROUPS * SGU_CHUNK, SGU_CHUNK)
    return norms, db_qkv, db_o, dsinks, db_sp, dln_g, dln_b, dw_sp, slab[42, 0]


class _GradReduce:
    def __init__(self, c_arr, jc_arr, dest_shapes):
        self.c_arr, self.jc_arr, self.dest_shapes = c_arr, jc_arr, dest_shapes
        self.grad, self.sibling, self.pair, self.chips, self.dest = {}, {}, {}, {}, {}

    def exchange(self, tags):
        return exchange_rider([self.grad[t] for t in tags])

    def exchanged(self, tags, res):
        for t, r in zip(tags, res):
            self.sibling[t] = r
            self.pair[t] = pair_add(self.grad[t], r, self.c_arr, name=f"pair_add_{t}")

    def scatter(self, tags):
        return scatter_rider([self.pair[t] for t in tags])

    def scattered(self, tags, res, where):
        for t, r in zip(tags, res):
            name, lead = where[t]
            self.dest[name] = final_add(self.grad[t], self.sibling[t], r, self.jc_arr, dest_shape=self.dest_shapes[name],
                                        lead=lead, prev=self.dest.get(name), name=f"final_add_{t}")

    def broadcast(self, items):
        names = []
        for n, _ in items:
            if n not in names:
                names.append(n)
        return names, broadcast_rider([self.dest[n] for n in names], [(names.index(n), lead) for n, lead in items])

    def broadcasted(self, names, res):
        for n, r in zip(names, res):
            self.dest[n] = r


def kernel(x, norm_mix_pre, norm_mix_post, norm_ffn_pre, norm_ffn_post, attn_w_qkv, attn_b_qkv, attn_sinks, attn_w_o, attn_b_o, sgu_w_in, sgu_ln_g, sgu_ln_b, sgu_w_spatial, sgu_b_spatial, sgu_w_out, ffn_w_gate_up, ffn_w_down, loss_target, m_norm_mix_pre, m_norm_mix_post, m_norm_ffn_pre, m_norm_ffn_post, m_attn_w_qkv, m_attn_b_qkv, m_attn_sinks, m_attn_w_o, m_attn_b_o, m_sgu_w_in, m_sgu_ln_g, m_sgu_ln_b, m_sgu_w_spatial, m_sgu_b_spatial, m_sgu_w_out, m_ffn_w_gate_up, m_ffn_w_down, v_norm_mix_pre, v_norm_mix_post, v_norm_ffn_pre, v_norm_ffn_post, v_attn_w_qkv, v_attn_b_qkv, v_attn_sinks, v_attn_w_o, v_attn_b_o, v_sgu_w_in, v_sgu_ln_g, v_sgu_ln_b, v_sgu_w_spatial, v_sgu_b_spatial, v_sgu_w_out, v_ffn_w_gate_up, v_ffn_w_down):
    s = x.shape[1]
    x0 = x.reshape(s, D_MODEL)
    target = loss_target.reshape(s, D_MODEL)
    mx, my, mc = lax.axis_index("x"), lax.axis_index("y"), lax.axis_index("c")
    chip = 2 * mx + my
    chip_arr = jnp.reshape(chip, (1,)).astype(I32)
    c_arr = jnp.reshape(mc, (1,)).astype(I32)
    jc_arr = jnp.stack([chip, mc]).astype(I32)
    zero_bias = jnp.zeros((1, D_MODEL), F32)

    def gain(p, i):
        return p[i:i + 1]

    big = [attn_w_qkv, attn_w_o, sgu_w_in, sgu_w_out, ffn_w_gate_up, ffn_w_gate_up, ffn_w_down, ffn_w_down]
    layers = [0, 0, 0, 0, 0, 1, 0, 1]
    tags = ["qkv", "wo", "win", "wout", "wgu0", "wgu1", "wd0", "wd1"]
    full = {t: place_shard(w, l, chip_arr, BF16, name=f"place_{t}") for w, l, t in zip(big, layers, tags)
            if t == "qkv"}
    ln_pack = _pad_rows(jnp.concatenate([sgu_ln_g, sgu_ln_b], axis=0), 16)[None]
    full["ln"] = place_shard(ln_pack, 0, chip_arr, F32, name="place_ln")

    def split(items):
        return [i if isinstance(i, str) else i[0] for i in items], [WHOLE if isinstance(i, str) else tuple(i[1:]) for i in items]

    def ici(*items):
        names, pieces = split(items)
        return gather_ici_rider([full[n] for n in names], pieces)

    def d2d(*items):
        names, pieces = split(items)
        return gather_d2d_rider([full[n] for n in names], pieces)

    def landed(items, res):
        for n, r in zip(split(items)[0], res):
            full[n] = r

    cos, sin = _rope_tables(s)
    sink_rows = jnp.broadcast_to(
        jnp.repeat(attn_sinks.reshape(N_KV_HEADS, GQA_GROUP), WINDOW, axis=1)[:, None, :], (N_KV_HEADS, 8, ROWS))
    w_sp = sgu_w_spatial.reshape(SGU_GROUPS, SGU_CHUNK, SGU_CHUNK)
    b_sp = jnp.broadcast_to(sgu_b_spatial.reshape(SGU_GROUPS, SGU_CHUNK)[:, :, None], (SGU_GROUPS, SGU_CHUNK, LANES))

    (h0, full["wgu0"]), (res,) = prenorm_and_place(x0, gain(norm_mix_pre, 0), ffn_w_gate_up, 0, chip_arr, name="prenorm_0",
                                                   riders=[ici("qkv", "ln")])
    landed(("qkv", "ln"), res)
    full["wo"], (res,) = place_shard(attn_w_o, 0, chip_arr, BF16, name="place_wo", riders=[d2d("qkv", "ln")])
    landed(("qkv", "ln"), res)
    ln_g = full["ln"][:, 0, :].reshape(1, D_MODEL)
    ln_b = full["ln"][:, 1, :].reshape(1, D_MODEL)

    def hosted(call, stages):
        outputs, results = call([{"ici": ici, "d2d": d2d}[kind](*items) for kind, items in stages])
        for (_, items), res in zip(stages, results):
            landed(items, res)
        return outputs

    casts = [(ffn_w_down, 0), (sgu_w_in, 0), (ffn_w_down, 1), (sgu_w_out, 0), (ffn_w_gate_up, 1)]
    qkv, full["wd0"], full["win"], full["wd1"], full["wout"], full["wgu1"] = hosted(
        lambda r: qkv_proj(h0, full["qkv"], attn_b_qkv, cos, sin, casts, chip_arr, name="qkv_proj", riders=r),
        [("ici", ("wo", ("wgu0", 0, 3, 8)))])
    o = hosted(lambda r: attn_fwd(qkv, sink_rows, name="attn_fwd", riders=r),
               [("d2d", ("wo",)), ("ici", (("wgu0", 3, 8, 8), ("wd0", 0, 2, 11), ("win", 0, 2, 8)))])
    w_o = full["wo"].reshape(Q_WIDTH, D_MODEL)
    x1, h1, m0 = hosted(lambda r: proj_residual_norm(o, w_o, x0, attn_b_o, gain(norm_mix_post, 0), gain(norm_ffn_pre, 0),
                                                     name="attn_out_norm", riders=r),
                        [("d2d", ("wgu0",)), ("ici", (("wd0", 2, 11, 11),))])
    gu0, a0 = hosted(lambda r: ffn_up(h1, full["wgu0"], name="ffn_up_0", riders=r),
                     [("d2d", ("wd0",)), ("ici", (("win", 2, 8, 8), "wout", ("wgu1", 0, 4, 8)))])
    w_d0 = full["wd0"].reshape(D_FF, D_MODEL)
    x2, h2, f0 = hosted(lambda r: proj_residual_norm(a0, w_d0, x1, zero_bias, gain(norm_ffn_post, 0), gain(norm_mix_pre, 1),
                                                     name="ffn_down_norm_0", riders=r),
                        [("d2d", ("win", "wout")), ("ici", (("wgu1", 4, 8, 8),))])
    w_in = full["win"]
    z, y = hosted(lambda r: sgu_in_fwd(h2, w_in, ln_g, ln_b, w_sp, b_sp, name="sgu_in_fwd", riders=r),
                  [("d2d", ("wgu1",)), ("ici", ("wd1",))])
    w_out = full["wout"].reshape(D_MODEL, D_MODEL)
    x3, h3, m1 = hosted(lambda r: proj_residual_norm(y, w_out, x2, zero_bias, gain(norm_mix_post, 1), gain(norm_ffn_pre, 1),
                                                     name="sgu_out_norm", riders=r),
                        [("d2d", ("wd1",))])
    w_qkv, w_gu0, w_gu1 = full["qkv"], full["wgu0"], full["wgu1"]
    w_d1 = full["wd1"].reshape(D_FF, D_MODEL)
    gu1, a1, dx4, df1, dg_fpost1, loss_part = ffn_fwd_loss_rows(
        h3, w_gu1, w_d1, x3, gain(norm_ffn_post, 1), target, name="ffn_fwd_loss_rows")

    red = _GradReduce(c_arr, jc_arr, {
        "qkv": attn_w_qkv.shape[1:], "wo": attn_w_o.shape[1:], "win": sgu_w_in.shape[1:], "wout": sgu_w_out.shape[1:],
        "wgu": ffn_w_gate_up.shape, "wd": ffn_w_down.shape, "slab": (N_CHIPS, SLAB_ROWS // N_CHIPS, D_MODEL)})
    where = {"qkv": ("qkv", None), "wo": ("wo", None), "win": ("win", None), "wout": ("wout", None), "wgu0": ("wgu", 0),
             "wgu1": ("wgu", 1), "wd0": ("wd", 0), "wd1": ("wd", 1), "small": ("slab", "chip")}

    dgu1, dx3, dm1, dg_fpre1, dg_mpost1, _ = ffn_bwd_rows(
        df1, w_d1, gu1, w_gu1, dx4, x3, gain(norm_ffn_pre, 1), m1, gain(norm_mix_post, 1), name="ffn_bwd_rows_1")
    red.grad["wd1"] = mm_tn(a1, df1, shard_major=False, tm=256, tn=D_MODEL, name="dw_down_1").reshape(
        N_CHIPS, D_FF // N_CHIPS, D_MODEL)
    red.grad["wgu1"], (res,) = mm_tn(h3, dgu1, shard_major=True, tm=512, tn=FF_HALF, name="dw_gate_up_1",
                                     riders=[red.exchange(["wd1"])])
    red.exchanged(["wd1"], res)
    dy, (res,) = mm_nt(dm1, w_out, out_dtype=F32, name="dy_sgu", riders=[red.exchange(["wgu1"])])
    red.exchanged(["wgu1"], res)
    red.grad["wout"] = mm_tn(y, dm1, shard_major=False, tm=512, tn=D_MODEL, name="dw_sgu_out").reshape(
        N_CHIPS, D_MODEL // N_CHIPS, D_MODEL)
    (dz, dw_sp, db_sp, dln_g, dln_b), (res_a, res_b) = sgu_bwd(
        z, dy, ln_g, ln_b, w_sp, b_sp, name="sgu_bwd", riders=[red.scatter(["wd1"]), red.exchange(["wout"])])
    red.scattered(["wd1"], res_a, where)
    red.exchanged(["wout"], res_b)
    names, rider = red.broadcast([("wd", 1)])
    red.grad["win"], (res_a, res_b) = mm_tn(h2, dz, shard_major=True, tm=D_MODEL, tn=2 * D_MODEL // N_CHIPS, name="dw_sgu_in",
                                            riders=[rider, red.scatter(["wout"])])
    red.broadcasted(names, res_a)
    red.scattered(["wout"], res_b, where)
    names, rider = red.broadcast([("wout", None)])
    (dx2, df0, dg_mpre1, dg_fpost0, _), (res_a, res_b) = dh_norm_bwd_pair(
        dz, w_in, dx3, x2, gain(norm_mix_pre, 1), f0, gain(norm_ffn_post, 0), name="dh_sgu_norm",
        riders=[red.exchange(["win"]), rider])
    red.exchanged(["win"], res_a)
    red.broadcasted(names, res_b)
    (dgu0, dx1, dm0, dg_fpre0, dg_mpost0, db_o), (res,) = ffn_bwd_rows(
        df0, w_d0, gu0, w_gu0, dx2, x1, gain(norm_ffn_pre, 0), m0, gain(norm_mix_post, 0), name="ffn_bwd_rows_0",
        riders=[red.scatter(["wgu1", "win"])])
    red.scattered(["wgu1", "win"], res, where)
    names, rider = red.broadcast([("wgu", 1), ("win", None)])
    dw_d0, (res,) = mm_tn(a0, df0, shard_major=False, tm=256, tn=D_MODEL, name="dw_down_0", riders=[rider])
    red.broadcasted(names, res)
    red.grad["wd0"] = dw_d0.reshape(N_CHIPS, D_FF // N_CHIPS, D_MODEL)
    do, (res,) = mm_nt(dm0, w_o, out_dtype=BF16, name="do_attn", riders=[red.exchange(["wd0"])])
    red.exchanged(["wd0"], res)
    red.grad["wgu0"], (res,) = mm_tn(h1, dgu0, shard_major=True, tm=512, tn=FF_HALF, name="dw_gate_up_0",
                                     riders=[red.scatter(["wd0"])])
    red.scattered(["wd0"], res, where)
    names, rider = red.broadcast([("wd", 0)])
    dw_o, (res_a, res_b) = mm_tn(o, dm0, shard_major=False, tm=512, tn=D_MODEL, name="dw_attn_out",
                                 riders=[red.exchange(["wgu0"]), rider])
    red.exchanged(["wgu0"], res_a)
    red.broadcasted(names, res_b)
    red.grad["wo"] = dw_o.reshape(N_CHIPS, Q_WIDTH // N_CHIPS, D_MODEL)
    (dq, dkc, dkp, dvc, dvp, dsink), (res_a, res_b) = attn_bwd(
        qkv, sink_rows, do, name="attn_bwd", riders=[red.scatter(["wgu0"]), red.exchange(["wo"])])
    red.scattered(["wgu0"], res_a, where)
    red.exchanged(["wo"], res_b)
    names, rider = red.broadcast([("wgu", 0)])
    (dqkv, db_qkv), (res,) = rope_bwd(dq, dkc, dkp, dvc, dvp, cos, sin, name="rope_bwd", riders=[rider])
    red.broadcasted(names, res)
    red.grad["qkv"], (res,) = mm_tn(h0, dqkv, shard_major=True, tm=D_MODEL, tn=QKV_WIDTH // N_CHIPS, name="dw_qkv",
                                    riders=[red.scatter(["wo"])])
    red.scattered(["wo"], res, where)
    grad_x, dg_mpre0 = dh_norm_bwd_last(dqkv, w_qkv, dx1, x0, gain(norm_mix_pre, 0), name="dh_attn_norm_in")

    norm_grads = [jnp.concatenate(p, axis=0) for p in
                  ((dg_mpre0, dg_mpre1), (dg_mpost0, dg_mpost1), (dg_fpre0, dg_fpre1), (dg_fpost0, dg_fpost1))]
    red.grad["small"] = _pack_small(norm_grads, db_qkv, db_o, dsink[:, :, 0, 0], db_sp[:, :, 0], dln_g, dln_b, dw_sp,
                                    loss_part)
    def big_update(w, g, m, v, tag, after=None):
        return adamw(w, g.reshape(w.shape), m, v, name=f"adamw_{tag}", after=after)

    state, token = comm_start([red.exchange(["qkv", "small"])], name="tail_1_start")
    upd_win = big_update(sgu_w_in, red.dest["win"], m_sgu_w_in, v_sgu_w_in, "win", after=token)
    (res,) = comm_wait(state, upd_win[1], name="tail_1_wait")
    red.exchanged(["qkv", "small"], res)
    state, token = comm_start([red.scatter(["qkv", "small"])], name="tail_2_start")
    upd_wgu = big_update(ffn_w_gate_up, red.dest["wgu"], m_ffn_w_gate_up, v_ffn_w_gate_up, "wgu", after=token)
    (res,) = comm_wait(state, upd_wgu[1], name="tail_2_wait")
    red.scattered(["qkv", "small"], res, where)
    names, rider = red.broadcast([("qkv", None), ("wo", None)])
    state, token = comm_start([rider, allcast_rider(red.dest["slab"])], name="tail_3_start")
    upd_wd = big_update(ffn_w_down, red.dest["wd"], m_ffn_w_down, v_ffn_w_down, "wd", after=token)
    res, (slab_full,) = comm_wait(state, upd_wd[1], name="tail_3_wait")
    red.broadcasted(names, res)
    g_qkv, g_wo, g_wout = (red.dest[n] for n in ("qkv", "wo", "wout"))
    g_norms, g_bqkv, g_bo, g_sinks, g_bsp, g_lng, g_lnb, g_wsp, loss = _unpack_small(slab_full, chip)

    upd = {
        "attn_w_qkv": big_update(attn_w_qkv, g_qkv, m_attn_w_qkv, v_attn_w_qkv, "qkv"),
        "attn_w_o": big_update(attn_w_o, g_wo, m_attn_w_o, v_attn_w_o, "wo"),
        "sgu_w_in": upd_win,
        "sgu_w_out": big_update(sgu_w_out, g_wout, m_sgu_w_out, v_sgu_w_out, "wout"),
        "ffn_w_gate_up": upd_wgu,
        "ffn_w_down": upd_wd,
    }
    small_names = ["norm_mix_pre", "norm_mix_post", "norm_ffn_pre", "norm_ffn_post", "attn_b_qkv", "attn_sinks", "attn_b_o",
                   "sgu_ln_g", "sgu_ln_b", "sgu_w_spatial", "sgu_b_spatial"]
    small_w = [norm_mix_pre, norm_mix_post, norm_ffn_pre, norm_ffn_post, attn_b_qkv, attn_sinks, attn_b_o, sgu_ln_g, sgu_ln_b,
               sgu_w_spatial, sgu_b_spatial]
    small_m = [m_norm_mix_pre, m_norm_mix_post, m_norm_ffn_pre, m_norm_ffn_post, m_attn_b_qkv, m_attn_sinks, m_attn_b_o,
               m_sgu_ln_g, m_sgu_ln_b, m_sgu_w_spatial, m_sgu_b_spatial]
    small_v = [v_norm_mix_pre, v_norm_mix_post, v_norm_ffn_pre, v_norm_ffn_post, v_attn_b_qkv, v_attn_sinks, v_attn_b_o,
               v_sgu_ln_g, v_sgu_ln_b, v_sgu_w_spatial, v_sgu_b_spatial]
    small_g = g_norms + [g_bqkv, g_sinks, g_bo, g_lng, g_lnb, g_wsp, g_bsp]

    def flat2(a):
        return a.reshape(-1, a.shape[-1])

    res = adamw_small([flat2(a) for a in small_w], [flat2(a) for a in small_g], [flat2(a) for a in small_m],
                      [flat2(a) for a in small_v], name="adamw_small")
    for i, nm in enumerate(small_names):
        upd[nm] = tuple(r[i].reshape(small_w[i].shape) for r in res)

    order = ["norm_mix_pre", "norm_mix_post", "norm_ffn_pre", "norm_ffn_post", "attn_w_qkv", "attn_b_qkv", "attn_sinks",
             "attn_w_o", "attn_b_o", "sgu_w_in", "sgu_ln_g", "sgu_ln_b", "sgu_w_spatial", "sgu_b_spatial", "sgu_w_out",
             "ffn_w_gate_up", "ffn_w_down"]
    outs = [loss, grad_x.reshape(1, s, D_MODEL)]
    for part in range(4):
        outs += [upd[nm][part] for nm in order]
    return tuple(outs)
```

```python
import types

import numpy as np
import jax
import jax.numpy as jnp
from jax import lax
from jax.experimental import pallas as pl
from jax.experimental.pallas import tpu as pltpu

F32 = jnp.float32
BF16 = jnp.bfloat16
I32 = jnp.int32

D_MODEL = 1024
HEAD_DIM = 64
N_Q_HEADS = 16
N_KV_HEADS = 4
GQA_GROUP = 4
WINDOW = 128
Q_WIDTH = 1024
KV_WIDTH = 256
QKV_WIDTH = 1536
ROPE_THETA = 10000.0
SGU_GROUPS = 8
SGU_CHUNK = 128
D_FF = 2816
FF_HALF = D_FF // 2
EPS = 1e-6
N_CHIPS = 4
LANES = 128

ADAM_LR = 0.001
ADAM_B1 = 0.9
ADAM_B2 = 0.999
ADAM_EPS = 1e-08
ADAM_WD = 0.01
ADAM_STEP = 10

VMEM_LIMIT = 52 * 1024 * 1024
BIG_VMEM_LIMIT = 62 * 1024 * 1024
SUB_ROWS = 256
MESH = pl.DeviceIdType.MESH
NEG = -1e30
NT_DIMS = (((1,), (1,)), ((), ()))
TN_DIMS = (((0,), (0,)), ((), ()))
NN_DIMS = (((1,), (0,)), ((), ()))
ANY = pl.BlockSpec(memory_space=pl.ANY)


def _row_tile(s, want):
    return want if s % want == 0 else s


PEER_KINDS = ("sibling", "chips", "sibling+chips", "everyone")


def _peer_kind(riders):
    kinds = {r.peers for r in riders}
    if not kinds:
        return None
    if "everyone" in kinds:
        return "everyone"
    return "sibling+chips" if len(kinds) == 2 else kinds.pop()


def _peer_barrier(kind):
    x, y, c = _place()
    chips = [(*_partner(x, y, k), c) for k in (1, 2, 3)]
    peers = {"sibling": [(x, y, 1 - c)], "chips": chips, "sibling+chips": [(x, y, 1 - c)] + chips,
             "everyone": [(x, y, 1 - c)] + chips + [(px, py, 1 - c) for px, py, _ in chips]}[kind]
    barrier = pltpu.get_barrier_semaphore()
    for dev in peers:
        pl.semaphore_signal(barrier, inc=1, device_id=dev, device_id_type=MESH)
    pl.semaphore_wait(barrier, len(peers))


def _call(body, *, name, grid=(), in_specs=(), out_specs=(), out_shape=(), scratch_shapes=(), operands=(), prefetch=(),
          aliases=None, riders=(), sem=None, vmem_limit=VMEM_LIMIT):
    n_pre, n_in, n_out, n_scr = len(prefetch), len(operands), len(out_shape), len(scratch_shapes)
    in_specs, out_specs, out_shape = list(in_specs), list(out_specs), list(out_shape)
    operands, scratch_shapes = list(operands), list(scratch_shapes)
    io_alias = {n_pre + i: o for i, o in (aliases or {}).items()}
    for r in riders:
        base_in, base_out = n_pre + len(operands), len(out_shape)
        operands += list(r.inputs)
        in_specs += [ANY] * len(r.inputs)
        for pos, i in enumerate(r.aliased):
            io_alias[base_in + i] = base_out + pos
            out_shape.append(jax.ShapeDtypeStruct(r.inputs[i].shape, r.inputs[i].dtype))
        out_shape += list(r.fresh)
        out_specs += [ANY] * (len(r.aliased) + len(r.fresh))
        scratch_shapes += [pltpu.SemaphoreType.DMA((r.nsem,)), pltpu.SemaphoreType.DMA((r.nsem,))]

    def wrapped(*refs):
        pre, p = refs[:n_pre], n_pre
        core_in, p = refs[p:p + n_in], p + n_in
        r_in = []
        for r in riders:
            r_in.append(refs[p:p + len(r.inputs)])
            p += len(r.inputs)
        core_out, p = refs[p:p + n_out], p + n_out
        r_out = []
        for r in riders:
            k = len(r.aliased) + len(r.fresh)
            r_out.append(refs[p:p + k])
            p += k
        core_scr, p = refs[p:p + n_scr], p + n_scr
        r_sem = [refs[p + 2 * i:p + 2 * i + 2] for i in range(len(riders))]

        def edge(at_last, fns):
            def run():
                if not at_last:
                    _peer_barrier(peer_kind)
                for i, r in enumerate(riders):
                    getattr(r, fns)(r_in[i], r_out[i], r_sem[i][0], r_sem[i][1])
            if not riders:
                return
            if not grid:
                run()
                return
            cond = None
            for d, n in enumerate(grid):
                c = pl.program_id(d) == (n - 1 if at_last else 0)
                cond = c if cond is None else jnp.logical_and(cond, c)
            pl.when(cond)(run)

        edge(False, "start")
        if body is not None:
            body(*pre, *core_in, *core_out, *core_scr)
        edge(True, "finish")

    if sem is None or riders:
        sem = ("arbitrary",) * len(grid)
    kwargs = dict(out_shape=out_shape, input_output_aliases=io_alias, name=name)
    peer_kind = _peer_kind(riders)
    collective = {} if peer_kind is None else {"collective_id": PEER_KINDS.index(peer_kind)}
    if grid:
        kwargs["compiler_params"] = pltpu.CompilerParams(dimension_semantics=sem, vmem_limit_bytes=vmem_limit, **collective)
    elif collective:
        kwargs["compiler_params"] = pltpu.CompilerParams(**collective)
    if n_pre:
        kwargs["grid_spec"] = pltpu.PrefetchScalarGridSpec(
            num_scalar_prefetch=n_pre, grid=grid, in_specs=in_specs, out_specs=out_specs, scratch_shapes=scratch_shapes)
    else:
        kwargs.update(grid=grid, in_specs=in_specs, out_specs=out_specs, scratch_shapes=scratch_shapes)
    res = pl.pallas_call(wrapped, **kwargs)(*prefetch, *operands)
    core, rest, rider_res = list(res[:n_out]), list(res[n_out:]), []
    for r in riders:
        k = len(r.aliased) + len(r.fresh)
        rider_res.append(rest[:k])
        rest = rest[k:]
    return core, rider_res


def _mm_call(*, grid, in_specs, out_spec, out_shape, dims, nk, kaxis, acc_shape, name, operands, riders=()):
    out_dtype = out_shape.dtype

    def body(a_ref, b_ref, o_ref, *scratch):
        p = lax.dot_general(a_ref[...].astype(BF16), b_ref[...].astype(BF16), dims, preferred_element_type=F32)
        if nk == 1:
            o_ref[...] = p.astype(out_dtype)
        else:
            acc = scratch[0]
            kk = pl.program_id(kaxis)

            @pl.when(kk == 0)
            def _():
                acc[...] = p

            @pl.when(kk > 0)
            def _():
                acc[...] += p

            @pl.when(kk == nk - 1)
            def _():
                o_ref[...] = acc[...].astype(out_dtype)

    sem = ["parallel"] * len(grid)
    if nk > 1:
        sem[kaxis] = "arbitrary"
    (out,), rider_res = _call(
        body, grid=grid, in_specs=in_specs, out_specs=[out_spec], out_shape=[out_shape],
        scratch_shapes=[pltpu.VMEM(acc_shape, F32)] if nk > 1 else [], operands=operands, name=name, riders=riders,
        sem=tuple(sem))
    return (out, rider_res) if riders else out


def mm_nt(a, w, *, out_dtype, name, tm=1024, riders=()):
    m, n = a.shape
    kout = w.shape[0]
    tm = _row_tile(m, tm)
    return _mm_call(grid=(m // tm,),
                    in_specs=[pl.BlockSpec((tm, n), lambda i: (i, 0)), pl.BlockSpec((kout, n), lambda i: (0, 0))],
                    out_spec=pl.BlockSpec((tm, kout), lambda i: (i, 0)),
                    out_shape=jax.ShapeDtypeStruct((m, kout), out_dtype), dims=NT_DIMS, nk=1, kaxis=0,
                    acc_shape=None, name=name, operands=(a, w), riders=riders)


def mm_tn(a, b, *, shard_major, name, tm, tn, tk=None, out_dtype=BF16, riders=()):
    s, m = a.shape
    tk = s if tk is None else _row_tile(s, tk)
    if b.ndim == 3:
        n = 2 * b.shape[2]
        b_spec = pl.BlockSpec((None, tk, tn), lambda j, i, kk: (j // 2, kk, j % 2))
    else:
        n = b.shape[1]
        b_spec = pl.BlockSpec((tk, tn), lambda j, i, kk: (kk, j))
    if shard_major:
        assert tn == n // N_CHIPS
        o_spec = pl.BlockSpec((None, tm, tn), lambda j, i, kk: (j, i, 0))
        o_shape = jax.ShapeDtypeStruct((N_CHIPS, m, tn), out_dtype)
    else:
        o_spec = pl.BlockSpec((tm, tn), lambda j, i, kk: (i, j))
        o_shape = jax.ShapeDtypeStruct((m, n), out_dtype)
    return _mm_call(grid=(n // tn, m // tm, s // tk),
                    in_specs=[pl.BlockSpec((tk, tm), lambda j, i, kk: (kk, i)), b_spec], out_spec=o_spec,
                    out_shape=o_shape, dims=TN_DIMS, nk=s // tk, kaxis=2, acc_shape=(tm, tn), name=name, operands=(a, b),
                    riders=riders)


def _rstd(x):
    return lax.rsqrt(jnp.mean(x * x, axis=-1, keepdims=True) + EPS)


def _rms_bwd(dy, x, g):
    r = _rstd(x)
    xhat = x * r
    gy = dy * g
    dx = r * (gy - xhat * jnp.mean(gy * xhat, axis=-1, keepdims=True))
    return dx, jnp.sum(dy * xhat, axis=0, keepdims=True)


def _accum(ref, val, first):
    @pl.when(first)
    def _():
        ref[...] = val

    @pl.when(jnp.logical_not(first))
    def _():
        ref[...] += val


def _row_spec(tm, width):
    return pl.BlockSpec((tm, width), lambda i: (i, 0))


def _vec_spec(width):
    return pl.BlockSpec((1, width), lambda i: (0, 0))


def _ret(core, rider_res, riders):
    core = core[0] if len(core) == 1 else core
    return (core, rider_res) if riders else core


def prenorm_and_place(x, g, w, layer, chip_arr, *, name, tm=256, riders=()):
    s = x.shape[0]
    tm = _row_tile(s, tm)
    steps = s // tm
    _, r, c = w.shape
    tr = r // steps
    assert tr * steps == r and tr % 16 == 0

    def body(chip_ref, x_ref, g_ref, w_ref, h_ref, o_ref):
        xv = x_ref[...]
        h_ref[...] = (xv * _rstd(xv) * g_ref[...]).astype(BF16)
        o_ref[...] = w_ref[...].astype(BF16)

    core, rr = _call(
        body, grid=(steps,), prefetch=(chip_arr,),
        in_specs=[pl.BlockSpec((tm, D_MODEL), lambda i, chip: (i, 0)), pl.BlockSpec((1, D_MODEL), lambda i, chip: (0, 0)),
                  pl.BlockSpec((None, tr, c), lambda i, chip: (layer, i, 0))],
        out_specs=[pl.BlockSpec((tm, D_MODEL), lambda i, chip: (i, 0)), pl.BlockSpec((None, tr, c), lambda i, chip: (chip[0], i, 0))],
        out_shape=[jax.ShapeDtypeStruct((s, D_MODEL), BF16), jax.ShapeDtypeStruct((N_CHIPS, r, c), BF16)],
        operands=(x, g, w), sem=("parallel",), name=name, riders=riders)
    return _ret(core, rr, riders)


def proj_residual_norm(a, w, x, bias, g_post, g_next, *, name, tm=512, sub=256, riders=()):
    s, k = a.shape
    tm = _row_tile(s, tm)
    sub = min(sub, tm)

    def body(a_ref, w_ref, x_ref, b_ref, gp_ref, gn_ref, xo_ref, h_ref, m_ref):
        for t in range(tm // sub):
            rows = slice(t * sub, (t + 1) * sub)
            mv = jnp.dot(a_ref[rows, :], w_ref[...], preferred_element_type=F32) + b_ref[...]
            m_ref[rows, :] = mv.astype(BF16)
            xn = x_ref[rows, :] + mv * _rstd(mv) * gp_ref[...]
            xo_ref[rows, :] = xn
            h_ref[rows, :] = (xn * _rstd(xn) * gn_ref[...]).astype(BF16)

    row, vec = _row_spec(tm, D_MODEL), _vec_spec(D_MODEL)
    core, rr = _call(
        body, grid=(s // tm,),
        in_specs=[_row_spec(tm, k), pl.BlockSpec((k, D_MODEL), lambda i: (0, 0)), row, vec, vec, vec], out_specs=[row, row, row],
        out_shape=[jax.ShapeDtypeStruct((s, D_MODEL), F32), jax.ShapeDtypeStruct((s, D_MODEL), BF16),
                   jax.ShapeDtypeStruct((s, D_MODEL), BF16)],
        operands=(a, w, x, bias, g_post, g_next), sem=("parallel",), name=name, riders=riders)
    return _ret(core, rr, riders)


def ffn_fwd_loss_rows(h, w_gu, w_d, x, g_post, target, *, name, tm=512, riders=()):
    s = x.shape[0]
    tm = _row_tile(s, tm)

    def body(h_ref, w0, w1, w2, w3, wd_ref, x_ref, g_ref, t_ref, d_ref, a_ref, dx_ref, df_ref, dg_ref, loss_ref):
        first = pl.program_id(0) == 0
        halves = [slice(half * FF_HALF, (half + 1) * FF_HALF) for half in (0, 1)]
        gain = g_ref[...]
        sub = min(SUB_ROWS, tm)
        sums = None
        for t in range(tm // sub):
            rows = slice(t * sub, (t + 1) * sub)
            hv = h_ref[rows, :]
            fv = None
            for cols, (wg_ref, wu_ref) in zip(halves, ((w0, w2), (w1, w3))):
                g = jnp.dot(hv, wg_ref[...], preferred_element_type=F32)
                u = jnp.dot(hv, wu_ref[...], preferred_element_type=F32)
                sig = _sigmoid(g)
                silu = g * sig
                d_ref[0, rows, cols] = (u * (sig + silu * (1.0 - sig))).astype(BF16)
                d_ref[1, rows, cols] = silu.astype(BF16)
                act = (silu * u).astype(BF16)
                a_ref[rows, cols] = act
                p = jnp.dot(act, wd_ref[cols, :], preferred_element_type=F32)
                fv = p if fv is None else fv + p
            err = x_ref[rows, :] + fv * _rstd(fv) * gain - t_ref[rows, :]
            dx = err * (1.0 / D_MODEL)
            dx_ref[rows, :] = dx
            df, dg = _rms_bwd(dx, fv, gain)
            df_ref[rows, :] = df.astype(BF16)
            part = (dg, jnp.sum(jnp.sum(err * err, axis=-1, keepdims=True), axis=0, keepdims=True) * (0.5 / D_MODEL))
            sums = part if sums is None else tuple(a + b for a, b in zip(sums, part))
        _accum(dg_ref, sums[0], first)
        _accum(loss_ref, jnp.broadcast_to(sums[1], (8, LANES)), first)

    def resident(shape, index):
        return pl.BlockSpec(shape, index, pipeline_mode=pl.Buffered(1))

    row, vec = _row_spec(tm, D_MODEL), _vec_spec(D_MODEL)
    shards = [resident((None, D_MODEL, FF_HALF), (lambda j: (lambda i: (j, 0, 0)))(j)) for j in range(N_CHIPS)]
    core, rr = _call(
        body, grid=(s // tm,),
        in_specs=[row] + shards + [resident((D_FF, D_MODEL), lambda i: (0, 0)), row, vec, row],
        out_specs=[pl.BlockSpec((2, tm, D_FF), lambda i: (0, i, 0)), _row_spec(tm, D_FF), row, row, vec,
                   pl.BlockSpec((8, LANES), lambda i: (0, 0))],
        out_shape=[jax.ShapeDtypeStruct((2, s, D_FF), BF16), jax.ShapeDtypeStruct((s, D_FF), BF16),
                   jax.ShapeDtypeStruct((s, D_MODEL), F32), jax.ShapeDtypeStruct((s, D_MODEL), BF16),
                   jax.ShapeDtypeStruct((1, D_MODEL), F32), jax.ShapeDtypeStruct((8, LANES), F32)],
        operands=(h, w_gu, w_gu, w_gu, w_gu, w_d, x, g_post, target), name=name, riders=riders, vmem_limit=BIG_VMEM_LIMIT)
    return _ret(core, rr, riders)


def dh_norm_bwd_pair(a, w, dres, x, g_pre, m, g_post, *, name, tm=512, sub=256, riders=()):
    _, kout, ns = w.shape
    planes = a.ndim == 3
    s = x.shape[0]
    tm = _row_tile(s, tm)
    sub = min(sub, tm)
    a_spec = pl.BlockSpec((2, tm, 2 * ns), lambda i: (0, i, 0)) if planes else pl.BlockSpec((tm, N_CHIPS * ns), lambda i: (i, 0))

    def body(a_ref, w0, w1, w2, w3, dres_ref, x_ref, gpre_ref, m_ref, gpost_ref, dx_ref, dm_ref, dgpre_ref, dgpost_ref, db_ref):
        first = pl.program_id(0) == 0
        sums = None
        for t in range(tm // sub):
            rows = slice(t * sub, (t + 1) * sub)
            dh = None
            for j, w_ref in enumerate((w0, w1, w2, w3)):
                a_j = a_ref[j // 2, rows, (j % 2) * ns:(j % 2 + 1) * ns] if planes else a_ref[rows, j * ns:(j + 1) * ns]
                p = lax.dot_general(a_j, w_ref[...], NT_DIMS, preferred_element_type=F32)
                dh = p if dh is None else dh + p
            d1, dgpre = _rms_bwd(dh, x_ref[rows, :], gpre_ref[...])
            dx = dres_ref[rows, :] + d1
            dx_ref[rows, :] = dx
            dm, dgpost = _rms_bwd(dx, m_ref[rows, :].astype(F32), gpost_ref[...])
            dm_ref[rows, :] = dm.astype(BF16)
            part = (dgpre, dgpost, jnp.sum(dm, axis=0, keepdims=True))
            sums = part if sums is None else tuple(u + v for u, v in zip(sums, part))
        _accum(dgpre_ref, sums[0], first)
        _accum(dgpost_ref, sums[1], first)
        _accum(db_ref, sums[2], first)

    def shard(j):
        return pl.BlockSpec((None, kout, ns), lambda i: (j, 0, 0))

    row, vec = _row_spec(tm, D_MODEL), _vec_spec(D_MODEL)
    vshape = jax.ShapeDtypeStruct((1, D_MODEL), F32)
    core, rr = _call(
        body, grid=(s // tm,), in_specs=[a_spec] + [shard(j) for j in range(N_CHIPS)] + [row, row, vec, row, vec],
        out_specs=[row, row, vec, vec, vec],
        out_shape=[jax.ShapeDtypeStruct((s, D_MODEL), F32), jax.ShapeDtypeStruct((s, D_MODEL), BF16), vshape, vshape, vshape],
        operands=(a, w, w, w, w, dres, x, g_pre, m, g_post), name=name, riders=riders)
    return _ret(core, rr, riders)


def ffn_bwd_rows(df, w_d, d_planes, w_gu, dres, x, g_pre, m, g_post, *, name, tm=512, riders=()):
    s = x.shape[0]
    tm = _row_tile(s, tm)

    def body(df_ref, wd_ref, d_ref, w0, w1, w2, w3, dres_ref, x_ref, gpre_ref, m_ref, gpost_ref,
             o_ref, dx_ref, dm_ref, dgpre_ref, dgpost_ref, db_ref):
        first = pl.program_id(0) == 0
        halves = [slice(half * FF_HALF, (half + 1) * FF_HALF) for half in (0, 1)]
        sub = min(SUB_ROWS, tm)
        sums = None
        for t in range(tm // sub):
            rows = slice(t * sub, (t + 1) * sub)
            dfv = df_ref[rows, :]
            dh = None
            for cols, (wg_ref, wu_ref) in zip(halves, ((w0, w2), (w1, w3))):
                da = lax.dot_general(dfv, wd_ref[cols, :], NT_DIMS, preferred_element_type=F32)
                dg = (da * d_ref[0, rows, cols].astype(F32)).astype(BF16)
                du = (da * d_ref[1, rows, cols].astype(F32)).astype(BF16)
                o_ref[0, rows, cols] = dg
                o_ref[1, rows, cols] = du
                p = lax.dot_general(dg, wg_ref[...], NT_DIMS, preferred_element_type=F32)
                p += lax.dot_general(du, wu_ref[...], NT_DIMS, preferred_element_type=F32)
                dh = p if dh is None else dh + p
            d1, dgpre = _rms_bwd(dh, x_ref[rows, :], gpre_ref[...])
            dx = dres_ref[rows, :] + d1
            dx_ref[rows, :] = dx
            dm, dgpost = _rms_bwd(dx, m_ref[rows, :].astype(F32), gpost_ref[...])
            dm_ref[rows, :] = dm.astype(BF16)
            part = (dgpre, dgpost, jnp.sum(dm, axis=0, keepdims=True))
            sums = part if sums is None else tuple(a + b for a, b in zip(sums, part))
        _accum(dgpre_ref, sums[0], first)
        _accum(dgpost_ref, sums[1], first)
        _accum(db_ref, sums[2], first)

    def resident(shape, index):
        return pl.BlockSpec(shape, index, pipeline_mode=pl.Buffered(1))

    planes = pl.BlockSpec((2, tm, D_FF), lambda i: (0, i, 0))
    row, vec = _row_spec(tm, D_MODEL), _vec_spec(D_MODEL)
    vshape = jax.ShapeDtypeStruct((1, D_MODEL), F32)
    shards = [resident((None, D_MODEL, FF_HALF), (lambda j: (lambda i: (j, 0, 0)))(j)) for j in range(N_CHIPS)]
    core, rr = _call(
        body, grid=(s // tm,),
        in_specs=[row, resident((D_FF, D_MODEL), lambda i: (0, 0)), planes] + shards + [row, row, vec, row, vec],
        out_specs=[planes, row, row, vec, vec, vec],
        out_shape=[jax.ShapeDtypeStruct((2, s, D_FF), BF16), jax.ShapeDtypeStruct((s, D_MODEL), F32),
                   jax.ShapeDtypeStruct((s, D_MODEL), BF16), vshape, vshape, vshape],
        operands=(df, w_d, d_planes, w_gu, w_gu, w_gu, w_gu, dres, x, g_pre, m, g_post), name=name, riders=riders,
        vmem_limit=BIG_VMEM_LIMIT)
    return _ret(core, rr, riders)


def dh_norm_bwd_last(a, w, dres, x, g_pre, *, name, tm=512, sub=256):
    _, kout, ns = w.shape
    s = x.shape[0]
    tm = _row_tile(s, tm)
    sub = min(sub, tm)

    def body(a_ref, w0, w1, w2, w3, dres_ref, x_ref, g_ref, dx_ref, dg_ref):
        total = None
        for t in range(tm // sub):
            rows = slice(t * sub, (t + 1) * sub)
            dh = None
            for j, w_ref in enumerate((w0, w1, w2, w3)):
                p = lax.dot_general(a_ref[rows, j * ns:(j + 1) * ns], w_ref[...], NT_DIMS, preferred_element_type=F32)
                dh = p if dh is None else dh + p
            d1, dg = _rms_bwd(dh, x_ref[rows, :], g_ref[...])
            dx_ref[rows, :] = dres_ref[rows, :] + d1
            total = dg if total is None else total + dg
        _accum(dg_ref, total, pl.program_id(0) == 0)

    def shard(j):
        return pl.BlockSpec((None, kout, ns), lambda i: (j, 0, 0))

    row, vec = _row_spec(tm, D_MODEL), _vec_spec(D_MODEL)
    (dx, dg), _ = _call(
        body, grid=(s // tm,), in_specs=[_row_spec(tm, N_CHIPS * ns)] + [shard(j) for j in range(N_CHIPS)] + [row, row, vec],
        out_specs=[row, vec], out_shape=[jax.ShapeDtypeStruct((s, D_MODEL), F32), jax.ShapeDtypeStruct((1, D_MODEL), F32)],
        operands=(a, w, w, w, w, dres, x, g_pre), name=name)
    return dx, dg


def _rope_tables(s):
    half = HEAD_DIM // 2
    inv_freq = np.float32(ROPE_THETA) ** (-(np.arange(half, dtype=np.float32) * np.float32(2.0)) / np.float32(HEAD_DIM))
    ang = np.arange(s, dtype=np.float32)[:, None] * inv_freq[None, :]
    cos, sin = np.cos(ang).astype(np.float32), np.sin(ang).astype(np.float32)
    return jnp.asarray(np.tile(cos, (1, 4))), jnp.asarray(np.concatenate([-sin, sin, -sin, sin], axis=1))


def _swap_halves(x):
    lane = lax.broadcasted_iota(I32, x.shape, 1)
    return jnp.where((lane & (HEAD_DIM - 1)) < HEAD_DIM // 2, pltpu.roll(x, LANES - 32, 1), pltpu.roll(x, 32, 1))


N_ROPE_BLOCKS = (Q_WIDTH + KV_WIDTH) // LANES


def qkv_proj(h, w, bias, cos, sin, casts, chip_arr, *, name, tm=1024, riders=()):
    s, k = h.shape
    ns = w.shape[2]
    tm = _row_tile(s, tm)
    nc = len(casts)

    def body(chip_ref, h_ref, w_ref, b_ref, c_ref, s_ref, *rest):
        cast_in, o_ref, cast_out = rest[:nc], rest[nc], rest[nc + 1:]
        j = pl.program_id(0)

        @pl.when(jnp.logical_and(j == 0, pl.program_id(1) == 0))
        def _():
            for src, dst in zip(cast_in, cast_out):
                dst[...] = src[...].astype(BF16)

        sub = min(256, tm)
        for t in range(tm // sub):
            rows = slice(t * sub, (t + 1) * sub)
            p = jnp.dot(h_ref[rows, :], w_ref[...], preferred_element_type=F32) + b_ref[...]
            cosv, sinv = c_ref[rows, :], s_ref[rows, :]
            for blk in range(ns // LANES):
                xb = p[:, blk * LANES:(blk + 1) * LANES]
                roped = xb * cosv + _swap_halves(xb) * sinv
                is_qk = j * (ns // LANES) + blk < N_ROPE_BLOCKS
                o_ref[rows, blk * LANES:(blk + 1) * LANES] = jnp.where(is_qk, roped, xb).astype(BF16)

    def cast_in_spec(wt, layer):
        return pl.BlockSpec((None,) + wt.shape[1:], lambda j, i, chip: (layer, 0, 0), pipeline_mode=pl.Buffered(1))

    def cast_out_spec(wt):
        return pl.BlockSpec((None,) + wt.shape[1:], lambda j, i, chip: (chip[0], 0, 0), pipeline_mode=pl.Buffered(1))

    core, rr = _call(
        body, grid=(N_CHIPS, s // tm), prefetch=(chip_arr,),
        in_specs=[pl.BlockSpec((tm, k), lambda j, i, chip: (i, 0)), pl.BlockSpec((None, k, ns), lambda j, i, chip: (j, 0, 0)),
                  pl.BlockSpec((1, ns), lambda j, i, chip: (0, j)), pl.BlockSpec((tm, LANES), lambda j, i, chip: (i, 0)),
                  pl.BlockSpec((tm, LANES), lambda j, i, chip: (i, 0))] + [cast_in_spec(wt, layer) for wt, layer in casts],
        out_specs=[pl.BlockSpec((tm, ns), lambda j, i, chip: (i, j))] + [cast_out_spec(wt) for wt, _ in casts],
        out_shape=[jax.ShapeDtypeStruct((s, N_CHIPS * ns), BF16)]
        + [jax.ShapeDtypeStruct((N_CHIPS,) + wt.shape[1:], BF16) for wt, _ in casts],
        operands=(h, w, bias, cos, sin, *[wt for wt, _ in casts]), sem=("arbitrary", "arbitrary"), name=name, riders=riders)
    return (core, rr) if riders else core


def rope_bwd(dq, dkc, dkp, dvc, dvp, cos, sin, *, name, riders=()):
    s = dq.shape[0]
    tm = 2 * WINDOW if s % (2 * WINDOW) == 0 else WINDOW
    nb = s // tm

    def body(dq_ref, dkc_ref, dkp_ref, dkp_next_ref, dvc_ref, dvp_ref, dvp_next_ref, c_ref, s_ref, o_ref, db_ref):
        i = pl.program_id(0)
        has_next = (i < nb - 1).astype(F32)
        cosv, sinv = c_ref[...], s_ref[...]

        def shifted(ref, next_ref, cols):
            last = has_next * next_ref[:WINDOW, cols].astype(F32)
            return last if tm == WINDOW else jnp.concatenate([ref[WINDOW:, cols].astype(F32), last], axis=0)

        parts = []
        for blk in range(QKV_WIDTH // LANES):
            if blk < Q_WIDTH // LANES:
                g = dq_ref[:, blk * LANES:(blk + 1) * LANES].astype(F32)
            else:
                own, prv, nxt = (dkc_ref, dkp_ref, dkp_next_ref) if blk < N_ROPE_BLOCKS else (dvc_ref, dvp_ref, dvp_next_ref)
                cols = slice((blk % 2) * LANES, (blk % 2 + 1) * LANES)
                g = own[:, cols].astype(F32) + shifted(prv, nxt, cols)
            if blk < N_ROPE_BLOCKS:
                g = g * cosv + _swap_halves(g * sinv)
            o_ref[:, blk * LANES:(blk + 1) * LANES] = g.astype(BF16)
            parts.append(jnp.sum(g, axis=0, keepdims=True))
        sums = jnp.concatenate(parts, axis=1)
        _accum(db_ref, sums, i == 0)

    own_spec = _row_spec(tm, KV_WIDTH)
    next_spec = pl.BlockSpec((tm, KV_WIDTH), lambda i: (jnp.minimum(i + 1, nb - 1), 0))
    core, rr = _call(
        body, grid=(nb,),
        in_specs=[_row_spec(tm, Q_WIDTH), own_spec, own_spec, next_spec, own_spec, own_spec, next_spec,
                  _row_spec(tm, LANES), _row_spec(tm, LANES)],
        out_specs=[_row_spec(tm, QKV_WIDTH), _vec_spec(QKV_WIDTH)],
        out_shape=[jax.ShapeDtypeStruct((s, QKV_WIDTH), BF16), jax.ShapeDtypeStruct((1, QKV_WIDTH), F32)],
        operands=(dq, dkc, dkp, dkp, dvc, dvp, dvp, cos, sin), name=name, riders=riders)
    return _ret(core, rr, riders)


ROWS = GQA_GROUP * WINDOW


def _prev_slots():
    kpos = lax.broadcasted_iota(I32, (WINDOW, ROWS), 0)
    qpos = lax.broadcasted_iota(I32, (WINDOW, ROWS), 1) & (WINDOW - 1)
    return kpos > qpos


def _head_cols(ref, head):
    return ref[:, head * HEAD_DIM:(head + 1) * HEAD_DIM]


def _stack_heads(ref, h):
    return jnp.concatenate([_head_cols(ref, GQA_GROUP * h + g) for g in range(GQA_GROUP)], axis=0)


def _band(prev_ref, cur_ref, h):
    return jnp.concatenate([_head_cols(prev_ref, h), _head_cols(cur_ref, h)], axis=0)


def _pick(prev, band):
    return jnp.where(prev, band[:WINDOW], band[WINDOW:])


def _spread(prev, x):
    return jnp.concatenate([jnp.where(prev, x, 0.0), jnp.where(prev, 0.0, x)], axis=0).astype(BF16)


def _attn_probs(s_band, sink, prev, has_prev):
    scale = HEAD_DIM ** -0.5
    s = jnp.where(prev, jnp.where(has_prev, s_band[:WINDOW], NEG), s_band[WINDOW:]) * scale
    m = jnp.maximum(jnp.max(s, axis=0, keepdims=True), sink)
    e, es = jnp.exp(s - m), jnp.exp(sink - m)
    inv = 1.0 / (jnp.sum(e, axis=0, keepdims=True) + es)
    return e * inv, es * inv


def _attn_specs(nb):
    kcol, vcol = Q_WIDTH // KV_WIDTH, Q_WIDTH // KV_WIDTH + 1
    q_spec = pl.BlockSpec((WINDOW, Q_WIDTH), lambda n: (n, 0))
    return [q_spec,
            pl.BlockSpec((WINDOW, KV_WIDTH), lambda n: (n, kcol)),
            pl.BlockSpec((WINDOW, KV_WIDTH), lambda n: (jnp.maximum(n - 1, 0), kcol)),
            pl.BlockSpec((WINDOW, KV_WIDTH), lambda n: (n, vcol)),
            pl.BlockSpec((WINDOW, KV_WIDTH), lambda n: (jnp.maximum(n - 1, 0), vcol)),
            pl.BlockSpec((N_KV_HEADS, 8, ROWS), lambda n: (0, 0, 0))]


def attn_fwd(qkv, sink_rows, *, name, riders=()):
    s = qkv.shape[0]

    def body(q_ref, kc_ref, kp_ref, vc_ref, vp_ref, sink_ref, o_ref):
        prev = _prev_slots()
        has_prev = pl.program_id(0) > 0
        heads = range(N_KV_HEADS)
        s_bands = [lax.dot_general(_band(kp_ref, kc_ref, h), _stack_heads(q_ref, h), NT_DIMS, preferred_element_type=F32)
                   for h in heads]
        p_bands = [_spread(prev, _attn_probs(s_bands[h], sink_ref[h, 0:1, :], prev, has_prev)[0]) for h in heads]
        outs = [lax.dot_general(_band(vp_ref, vc_ref, h), p_bands[h], TN_DIMS, preferred_element_type=F32).T for h in heads]
        for h in heads:
            for g in range(GQA_GROUP):
                head = GQA_GROUP * h + g
                o_ref[:, head * HEAD_DIM:(head + 1) * HEAD_DIM] = outs[h][g * WINDOW:(g + 1) * WINDOW].astype(BF16)

    core, rr = _call(
        body, grid=(s // WINDOW,), in_specs=_attn_specs(s // WINDOW), out_specs=[pl.BlockSpec((WINDOW, Q_WIDTH), lambda n: (n, 0))],
        out_shape=[jax.ShapeDtypeStruct((s, Q_WIDTH), BF16)], operands=(qkv, qkv, qkv, qkv, qkv, sink_rows), sem=("parallel",),
        name=name, riders=riders)
    return _ret(core, rr, riders)


def attn_bwd(qkv, sink_rows, do, *, name, riders=()):
    s = qkv.shape[0]

    def body(q_ref, kc_ref, kp_ref, vc_ref, vp_ref, sink_ref, do_ref, dq_ref, dkc_ref, dkp_ref, dvc_ref, dvp_ref, dsink_ref):
        n = pl.program_id(0)
        prev = _prev_slots()
        scale = HEAD_DIM ** -0.5
        heads = range(N_KV_HEADS)
        qs, dos = [_stack_heads(q_ref, h) for h in heads], [_stack_heads(do_ref, h) for h in heads]
        kbands, vbands = [_band(kp_ref, kc_ref, h) for h in heads], [_band(vp_ref, vc_ref, h) for h in heads]
        s_bands = [lax.dot_general(kbands[h], qs[h], NT_DIMS, preferred_element_type=F32) for h in heads]
        dp_bands = [lax.dot_general(vbands[h], dos[h], NT_DIMS, preferred_element_type=F32) for h in heads]
        ds_bands, p_bands, parts = [], [], []
        for h in heads:
            p, ps = _attn_probs(s_bands[h], sink_ref[h, 0:1, :], prev, n > 0)
            dp = _pick(prev, dp_bands[h])
            delta = jnp.sum(p * dp, axis=0, keepdims=True)
            ds_bands.append(_spread(prev, p * (dp - delta) * scale))
            p_bands.append(_spread(prev, p))
            dsink = -(ps * delta)
            for g in range(GQA_GROUP):
                parts.append(jnp.broadcast_to(jnp.sum(dsink[:, g * WINDOW:(g + 1) * WINDOW], axis=1, keepdims=True), (8, LANES)))
        for h in heads:
            dk = jnp.dot(ds_bands[h], qs[h], preferred_element_type=F32).astype(BF16)
            dv = jnp.dot(p_bands[h], dos[h], preferred_element_type=F32).astype(BF16)
            dq = lax.dot_general(kbands[h], ds_bands[h], TN_DIMS, preferred_element_type=F32).T
            cols = slice(h * HEAD_DIM, (h + 1) * HEAD_DIM)
            dkp_ref[:, cols], dkc_ref[:, cols] = dk[:WINDOW], dk[WINDOW:]
            dvp_ref[:, cols], dvc_ref[:, cols] = dv[:WINDOW], dv[WINDOW:]
            for g in range(GQA_GROUP):
                head = GQA_GROUP * h + g
                dq_ref[:, head * HEAD_DIM:(head + 1) * HEAD_DIM] = dq[g * WINDOW:(g + 1) * WINDOW].astype(BF16)

        @pl.when(n == 0)
        def _():
            for i, part in enumerate(parts):
                dsink_ref[i // GQA_GROUP, i % GQA_GROUP] = part

        @pl.when(n > 0)
        def _():
            for i, part in enumerate(parts):
                dsink_ref[i // GQA_GROUP, i % GQA_GROUP] += part

    rows_q = pl.BlockSpec((WINDOW, Q_WIDTH), lambda n: (n, 0))
    rows_kv = pl.BlockSpec((WINDOW, KV_WIDTH), lambda n: (n, 0))
    kv_shape = jax.ShapeDtypeStruct((s, KV_WIDTH), BF16)
    core, rr = _call(
        body, grid=(s // WINDOW,), in_specs=_attn_specs(s // WINDOW) + [rows_q],
        out_specs=[rows_q, rows_kv, rows_kv, rows_kv, rows_kv,
                   pl.BlockSpec((N_KV_HEADS, GQA_GROUP, 8, LANES), lambda n: (0, 0, 0, 0))],
        out_shape=[jax.ShapeDtypeStruct((s, Q_WIDTH), BF16), kv_shape, kv_shape, kv_shape, kv_shape,
                   jax.ShapeDtypeStruct((N_KV_HEADS, GQA_GROUP, 8, LANES), F32)],
        operands=(qkv, qkv, qkv, qkv, qkv, sink_rows, do), sem=("arbitrary",), name=name, riders=riders)
    return _ret(core, rr, riders)


GELU_C = 0.7978845608028654
GELU_A = 0.044715


def _gelu_and_grad(x):
    x2 = x * x
    t = jnp.tanh(x * (GELU_C + (GELU_C * GELU_A) * x2))
    half_x, one_t = 0.5 * x, 1.0 + t
    return half_x * one_t, 0.5 * one_t + half_x * (1.0 - t * t) * (GELU_C + (3.0 * GELU_C * GELU_A) * x2)


def _tril_bf16(w):
    row = lax.broadcasted_iota(I32, (SGU_CHUNK, SGU_CHUNK), 0)
    col = lax.broadcasted_iota(I32, (SGU_CHUNK, SGU_CHUNK), 1)
    return jnp.where(row >= col, w, 0.0).astype(BF16)


def _sgu_norm(vg, g, b):
    mu = jnp.mean(vg, axis=-1, keepdims=True)
    cen = vg - mu
    rstd = lax.rsqrt(jnp.mean(cen * cen, axis=-1, keepdims=True) + EPS)
    xhat = cen * rstd
    return xhat, rstd, xhat * g + b


def sgu_in_fwd(h, w_in, ln_g, ln_b, w_sp, b_sp, *, name, tm=512, riders=()):
    s, k = h.shape
    tm = _row_tile(s, tm)

    def body(h_ref, w0, w1, w2, w3, g_ref, b_ref, w_ref, bs_ref, z_ref, y_ref):
        hv = h_ref[...]
        zs = [jnp.dot(hv, w_ref_j[...], preferred_element_type=F32) for w_ref_j in (w0, w1, w2, w3)]
        u, u_grad = _gelu_and_grad(jnp.concatenate(zs[:2], axis=1))
        vg, v_grad = _gelu_and_grad(jnp.concatenate(zs[2:], axis=1))
        for j, part in enumerate((u, u_grad, vg, v_grad)):
            z_ref[:, j * D_MODEL:(j + 1) * D_MODEL] = part.astype(BF16)
        _, _, vn = _sgu_norm(vg, g_ref[...], b_ref[...])
        vn = vn.astype(BF16)
        for grp in range(SGU_GROUPS):
            w = _tril_bf16(w_ref[grp])
            cols = slice(grp * LANES, (grp + 1) * LANES)
            for ch in range(tm // SGU_CHUNK):
                rows = slice(ch * SGU_CHUNK, (ch + 1) * SGU_CHUNK)
                mixed = jnp.dot(w, vn[rows, cols], preferred_element_type=F32) + bs_ref[grp]
                y_ref[rows, cols] = (u[rows, cols] * mixed).astype(BF16)

    def shard(j):
        return pl.BlockSpec((None, k, w_in.shape[2]), lambda i: (j, 0, 0))

    full3 = pl.BlockSpec((SGU_GROUPS, SGU_CHUNK, SGU_CHUNK), lambda i: (0, 0, 0))
    core, rr = _call(
        body, grid=(s // tm,),
        in_specs=[_row_spec(tm, k)] + [shard(j) for j in range(N_CHIPS)] + [_vec_spec(D_MODEL), _vec_spec(D_MODEL), full3, full3],
        out_specs=[_row_spec(tm, 4 * D_MODEL), _row_spec(tm, D_MODEL)],
        out_shape=[jax.ShapeDtypeStruct((s, 4 * D_MODEL), BF16), jax.ShapeDtypeStruct((s, D_MODEL), BF16)],
        operands=(h, w_in, w_in, w_in, w_in, ln_g, ln_b, w_sp, b_sp), sem=("parallel",), name=name, riders=riders)
    return _ret(core, rr, riders)


def sgu_bwd(z, dy, ln_g, ln_b, w_sp, b_sp, *, name, tm=256, riders=()):
    s = z.shape[0]
    tm = _row_tile(s, tm)

    def body(z_ref, dy_ref, g_ref, b_ref, w_ref, bs_ref, dz_ref, dw_ref, dbs_ref, dg_ref, db_ref, dvn_buf):
        first = pl.program_id(0) == 0
        u, u_grad, vg, v_grad = (z_ref[:, j * D_MODEL:(j + 1) * D_MODEL].astype(F32) for j in range(4))
        xhat, rstd, vn = _sgu_norm(vg, g_ref[...], b_ref[...])
        vn = vn.astype(BF16)
        dyv = dy_ref[...]
        dmixed = dyv * u
        dz_gate = dyv * u_grad
        row = lax.broadcasted_iota(I32, (SGU_CHUNK, SGU_CHUNK), 0)
        col = lax.broadcasted_iota(I32, (SGU_CHUNK, SGU_CHUNK), 1)
        dws, dbss = [], []
        for grp in range(SGU_GROUPS):
            w = _tril_bf16(w_ref[grp])
            cols = slice(grp * LANES, (grp + 1) * LANES)
            dw = jnp.zeros((SGU_CHUNK, SGU_CHUNK), F32)
            dbs = jnp.zeros((SGU_CHUNK, 1), F32)
            for ch in range(tm // SGU_CHUNK):
                rows = slice(ch * SGU_CHUNK, (ch + 1) * SGU_CHUNK)
                vblk = vn[rows, cols]
                mixed = jnp.dot(w, vblk, preferred_element_type=F32) + bs_ref[grp]
                dz_ref[rows, cols] = (dz_gate[rows, cols] * mixed).astype(BF16)
                dm = dmixed[rows, cols]
                dmb = dm.astype(BF16)
                dvn_buf[rows, cols] = lax.dot_general(w, dmb, TN_DIMS, preferred_element_type=F32)
                dw += lax.dot_general(dmb, vblk, NT_DIMS, preferred_element_type=F32)
                dbs += jnp.sum(dm, axis=-1, keepdims=True)
            dws.append(jnp.where(row >= col, dw, 0.0))
            dbss.append(jnp.broadcast_to(dbs, (SGU_CHUNK, SGU_CHUNK)))

        dvn = dvn_buf[...]
        dxhat = dvn * g_ref[...]
        dvg = rstd * (dxhat - jnp.mean(dxhat, axis=-1, keepdims=True) - xhat * jnp.mean(dxhat * xhat, axis=-1, keepdims=True))
        dz_ref[:, D_MODEL:] = (dvg * v_grad).astype(BF16)
        dlng, dlnb = jnp.sum(dvn * xhat, axis=0, keepdims=True), jnp.sum(dvn, axis=0, keepdims=True)

        @pl.when(first)
        def _():
            for grp in range(SGU_GROUPS):
                dw_ref[grp] = dws[grp]
                dbs_ref[grp] = dbss[grp]
            dg_ref[...] = dlng
            db_ref[...] = dlnb

        @pl.when(jnp.logical_not(first))
        def _():
            for grp in range(SGU_GROUPS):
                dw_ref[grp] += dws[grp]
                dbs_ref[grp] += dbss[grp]
            dg_ref[...] += dlng
            db_ref[...] += dlnb

    full3 = pl.BlockSpec((SGU_GROUPS, SGU_CHUNK, SGU_CHUNK), lambda i: (0, 0, 0))
    s3 = jax.ShapeDtypeStruct((SGU_GROUPS, SGU_CHUNK, SGU_CHUNK), F32)
    vshape = jax.ShapeDtypeStruct((1, D_MODEL), F32)
    core, rr = _call(
        body, grid=(s // tm,),
        in_specs=[_row_spec(tm, 4 * D_MODEL), _row_spec(tm, D_MODEL), _vec_spec(D_MODEL), _vec_spec(D_MODEL), full3, full3],
        out_specs=[_row_spec(tm, 2 * D_MODEL), full3, full3, _vec_spec(D_MODEL), _vec_spec(D_MODEL)],
        out_shape=[jax.ShapeDtypeStruct((s, 2 * D_MODEL), BF16), s3, s3, vshape, vshape],
        scratch_shapes=[pltpu.VMEM((tm, D_MODEL), F32)], operands=(z, dy, ln_g, ln_b, w_sp, b_sp), name=name, riders=riders)
    return _ret(core, rr, riders)


def _sigmoid(x):
    return 1.0 / (1.0 + jnp.exp(-x))


def ffn_up(h, w_gu, *, name, tm=512, riders=()):
    s = h.shape[0]
    tm = _row_tile(s, tm)

    def body(h_ref, wg_ref, wu_ref, d_ref, a_ref):
        hv = h_ref[...]
        sub = min(256, tm)
        for t in range(tm // sub):
            rows = slice(t * sub, (t + 1) * sub)
            g = jnp.dot(hv[rows], wg_ref[...], preferred_element_type=F32)
            u = jnp.dot(hv[rows], wu_ref[...], preferred_element_type=F32)
            sig = _sigmoid(g)
            silu = g * sig
            d_ref[0, rows, :] = (u * (sig + silu * (1.0 - sig))).astype(BF16)
            d_ref[1, rows, :] = silu.astype(BF16)
            a_ref[rows, :] = (silu * u).astype(BF16)

    core, rr = _call(
        body, grid=(2, s // tm),
        in_specs=[pl.BlockSpec((tm, D_MODEL), lambda j, i: (i, 0)),
                  pl.BlockSpec((None, D_MODEL, FF_HALF), lambda j, i: (j, 0, 0)),
                  pl.BlockSpec((None, D_MODEL, FF_HALF), lambda j, i: (j + 2, 0, 0))],
        out_specs=[pl.BlockSpec((2, tm, FF_HALF), lambda j, i: (0, i, j)), pl.BlockSpec((tm, FF_HALF), lambda j, i: (i, j))],
        out_shape=[jax.ShapeDtypeStruct((2, s, D_FF), BF16), jax.ShapeDtypeStruct((s, D_FF), BF16)],
        operands=(h, w_gu, w_gu), sem=("parallel", "parallel"), name=name, riders=riders)
    return _ret(core, rr, riders)


def _weight_tile(rows):
    for tr in (512, 352, 256, 128):
        if rows % tr == 0:
            return tr
    return rows


def place_shard(w, layer, chip_arr, dtype, *, name, riders=()):
    _, r, c = w.shape
    tr = _weight_tile(r)

    def body(chip_ref, w_ref, o_ref):
        o_ref[...] = w_ref[...].astype(dtype)

    core, rr = _call(
        body, grid=(r // tr,), prefetch=(chip_arr,),
        in_specs=[pl.BlockSpec((None, tr, c), lambda i, chip: (layer, i, 0))],
        out_specs=[pl.BlockSpec((None, tr, c), lambda i, chip: (chip[0], i, 0))],
        out_shape=[jax.ShapeDtypeStruct((N_CHIPS, r, c), dtype)], operands=(w,), sem=("parallel",), name=name, riders=riders)
    return _ret(core, rr, riders)


def _adamw_math(w, g, m, v):
    m = ADAM_B1 * m + (1.0 - ADAM_B1) * g
    v = ADAM_B2 * v + (1.0 - ADAM_B2) * (g * g)
    m_hat = m / (1.0 - ADAM_B1 ** ADAM_STEP)
    v_hat = v / (1.0 - ADAM_B2 ** ADAM_STEP)
    delta = -ADAM_LR * (m_hat / (jnp.sqrt(v_hat) + ADAM_EPS) + ADAM_WD * w)
    return delta, m, v


def adamw(w, g, m, v, *, name, after=None):
    nl, r, c = w.shape
    tr = _weight_tile(r)

    def body(w_ref, g_ref, m_ref, v_ref, *rest):
        go_ref, d_ref, mo_ref, vo_ref = rest[-4:]
        gv = g_ref[...]
        go_ref[...] = gv
        d_ref[...], mo_ref[...], vo_ref[...] = _adamw_math(w_ref[...], gv, m_ref[...], v_ref[...])

    spec = pl.BlockSpec((None, tr, c), lambda l, i: (l, i, 0))
    shape = jax.ShapeDtypeStruct(w.shape, F32)
    extra = [] if after is None else [after]
    outs, _ = _call(body, grid=(nl, r // tr), in_specs=[spec] * 4 + [ANY] * len(extra), out_specs=[spec] * 4,
                    out_shape=[shape] * 4, operands=(w, g, m, v, *extra), sem=("parallel", "parallel"), name=name)
    return outs


def adamw_small(ws, gs, ms, vs, *, name):
    n = len(ws)

    def body(*refs):
        ins, outs = refs[:4 * n], refs[4 * n:]
        for t in range(n):
            gv = ins[n + t][...]
            outs[t][...] = gv
            outs[n + t][...], outs[2 * n + t][...], outs[3 * n + t][...] = _adamw_math(
                ins[t][...], gv, ins[2 * n + t][...], ins[3 * n + t][...])

    shapes = [jax.ShapeDtypeStruct(w.shape, F32) for w in ws]
    res = pl.pallas_call(body, out_shape=shapes * 4, name=name)(*ws, *gs, *ms, *vs)
    return res[:n], res[n:2 * n], res[2 * n:3 * n], res[3 * n:]


def pair_add(g, r1, c_arr, *, name):
    _, rows, cdim = g.shape
    h = rows // 2

    def body(c_ref, g_ref, r_ref, o_ref):
        o_ref[...] = (g_ref[...].astype(F32) + r_ref[...].astype(F32)).astype(o_ref.dtype)

    (out,), _ = _call(
        body, grid=(N_CHIPS,), prefetch=(c_arr,),
        in_specs=[pl.BlockSpec((None, h, cdim), lambda s, c: (s, c[0], 0)), pl.BlockSpec((None, h, cdim), lambda s, c: (s, 0, 0))],
        out_specs=[pl.BlockSpec((None, h, cdim), lambda s, c: (s, 0, 0))],
        out_shape=[jax.ShapeDtypeStruct((N_CHIPS, h, cdim), g.dtype)], operands=(g, r1), sem=("parallel",), name=name)
    return out


def final_add(g, r1, r2, jc_arr, *, dest_shape, lead, prev, name):
    _, rows, cdim = g.shape
    h = rows // 2

    def body(jc_ref, g_ref, r1_ref, r2_ref, *rest):
        o_ref = rest[-1]
        acc = g_ref[...].astype(F32) + r1_ref[...].astype(F32)
        for k in range(3):
            acc = acc + r2_ref[k].astype(F32)
        o_ref[...] = acc

    if lead is None:
        o_spec = pl.BlockSpec((h, cdim), lambda i, jc: (jc[1], 0))
    elif lead == "chip":
        o_spec = pl.BlockSpec((None, h, cdim), lambda i, jc: (jc[0], jc[1], 0))
    else:
        o_spec = pl.BlockSpec((None, h, cdim), lambda i, jc: (lead, jc[1], 0))
    in_specs = [pl.BlockSpec((None, h, cdim), lambda i, jc: (jc[0], jc[1], 0)),
                pl.BlockSpec((None, h, cdim), lambda i, jc: (jc[0], 0, 0)),
                pl.BlockSpec((3, h, cdim), lambda i, jc: (0, 0, 0))]
    operands = [g, r1, r2]
    aliases = None
    if prev is not None:
        in_specs.append(ANY)
        operands.append(prev)
        aliases = {3: 0}
    (out,), _ = _call(body, grid=(1,), prefetch=(jc_arr,), in_specs=in_specs, out_specs=[o_spec],
                      out_shape=[jax.ShapeDtypeStruct(dest_shape, F32)], operands=operands, aliases=aliases, name=name)
    return out


def _place():
    return lax.axis_index("x"), lax.axis_index("y"), lax.axis_index("c")


def _partner(x, y, k):
    return (1 - x if k >> 1 else x), (1 - y if k & 1 else y)


WHOLE = (0, 1, 1)


def _half(rows, sel, dtype, piece=WHOLE):
    lo, hi, n = piece
    align = 16 if dtype == BF16 else 8
    step = rows // 2 // n
    assert rows // 2 == step * n and step % align == 0
    return pl.ds(pl.multiple_of(sel * (rows // 2) + lo * step, align), (hi - lo) * step)


def _rider(peers, inputs, aliased, fresh, nsem, copies, arrivals):
    def start(ins, outs, send, recv):
        for cp in copies(ins, outs, send, recv):
            cp.start()

    def finish(ins, outs, send, recv):
        for cp in arrivals(ins, outs, send, recv):
            cp.wait_recv()
        for cp in copies(ins, outs, send, recv):
            cp.wait_send()

    return types.SimpleNamespace(peers=peers, inputs=list(inputs), aliased=list(aliased), fresh=list(fresh), nsem=nsem,
                                 start=start, finish=finish)


def _remote(src, dst, send, recv, idx, dev):
    return pltpu.make_async_remote_copy(src_ref=src, dst_ref=dst, send_sem=send.at[idx], recv_sem=recv.at[idx],
                                        device_id=dev, device_id_type=MESH)


def gather_ici_rider(fulls, pieces=None):
    nt = len(fulls)
    pieces = pieces or [WHOLE] * nt

    def region(outs, t, slot, sel):
        return outs[t].at[slot, _half(fulls[t].shape[1], sel, fulls[t].dtype, pieces[t])]

    def copies(ins, outs, send, recv):
        x, y, c = _place()
        res = []
        for t in range(nt):
            for k in (1, 2, 3):
                px, py = _partner(x, y, k)
                mine = region(outs, t, 2 * x + y, c)
                res.append(_remote(mine, mine, send, recv, 3 * t + k - 1, (px, py, c)))
        return res

    def arrivals(ins, outs, send, recv):
        x, y, c = _place()
        res = []
        for t in range(nt):
            for k in (1, 2, 3):
                px, py = _partner(x, y, k)
                theirs = region(outs, t, 2 * px + py, c)
                res.append(_remote(theirs, theirs, send, recv, 3 * t + k - 1, (x, y, c)))
        return res

    return _rider("chips", fulls, range(nt), [], 3 * nt, copies, arrivals)


def gather_d2d_rider(fulls, pieces=None):
    nt = len(fulls)
    pieces = pieces or [WHOLE] * nt

    def region(outs, t, slot, sel):
        return outs[t].at[slot, _half(fulls[t].shape[1], sel, fulls[t].dtype, pieces[t])]

    def both(outs, send, recv, mine):
        x, y, c = _place()
        res = []
        for t in range(nt):
            for k in (1, 2, 3):
                px, py = _partner(x, y, k)
                part = region(outs, t, 2 * px + py, c if mine else 1 - c)
                res.append(_remote(part, part, send, recv, 3 * t + k - 1, (x, y, 1 - c)))
        return res

    return _rider("sibling", fulls, range(nt), [], 3 * nt, lambda i, o, s, r: both(o, s, r, True),
                  lambda i, o, s, r: both(o, s, r, False))


def exchange_rider(grads):
    nt = len(grads)

    def both(ins, outs, send, recv):
        x, y, c = _place()
        return [_remote(ins[t].at[:, _half(grads[t].shape[1], 1 - c, grads[t].dtype)], outs[t], send, recv, t, (x, y, 1 - c))
                for t in range(nt)]

    fresh = [jax.ShapeDtypeStruct((N_CHIPS, g.shape[1] // 2, g.shape[2]), g.dtype) for g in grads]
    return _rider("sibling", grads, [], fresh, nt, both, both)


def scatter_rider(parts):
    nt = len(parts)

    def both(ins, outs, send, recv):
        x, y, c = _place()
        res = []
        for t in range(nt):
            for k in (1, 2, 3):
                px, py = _partner(x, y, k)
                res.append(_remote(ins[t].at[2 * px + py], outs[t].at[k - 1], send, recv, 3 * t + k - 1, (px, py, c)))
        return res

    fresh = [jax.ShapeDtypeStruct((3,) + p.shape[1:], p.dtype) for p in parts]
    return _rider("chips", parts, [], fresh, 3 * nt, both, both)


def broadcast_rider(bufs, items):
    def region(outs, item, sel):
        bi, lead = item
        ref = outs[bi]
        if lead == "chip":
            x, y, _ = _place()
            ref = ref.at[2 * x + y]
        elif lead is not None:
            ref = ref.at[lead]
        return ref.at[_half(ref.shape[0], sel, F32)]

    def both(outs, send, recv, mine):
        x, y, c = _place()
        res = []
        for i, item in enumerate(items):
            part = region(outs, item, c if mine else 1 - c)
            res.append(_remote(part, part, send, recv, i, (x, y, 1 - c)))
        return res

    return _rider("sibling", bufs, range(len(bufs)), [], len(items), lambda i, o, s, r: both(o, s, r, True),
                  lambda i, o, s, r: both(o, s, r, False))


def allcast_rider(buf):
    peers = [(k, flip) for k in range(N_CHIPS) for flip in (0, 1) if (k, flip) != (0, 0)]

    def both(outs, send, recv, mine):
        x, y, c = _place()
        res = []
        for i, (k, flip) in enumerate(peers):
            px, py = _partner(x, y, k)
            pc = 1 - c if flip else c
            slot, sel = (2 * x + y, c) if mine else (2 * px + py, pc)
            part = outs[0].at[slot, _half(buf.shape[1], sel, F32)]
            res.append(_remote(part, part, send, recv, i, (px, py, pc)))
        return res

    return _rider("everyone", [buf], [0], [], len(peers), lambda i, o, s, r: both(o, s, r, True),
                  lambda i, o, s, r: both(o, s, r, False))


def comm_call(riders, *, name):
    _, res = _call(None, riders=riders, name=name)
    return res


SEMS = pl.BlockSpec(memory_space=pltpu.SEMAPHORE)
SIDE_EFFECT = pltpu.SideEffectType.DATAFLOW_SIDE_EFFECTING


def _split_refs(riders, refs):
    views, p = [], 0
    for r in riders:
        bufs = refs[p:p + len(r.inputs) + len(r.fresh)]
        p += len(bufs)
        ins = bufs[:len(r.inputs)]
        views.append([ins, [ins[i] for i in r.aliased] + list(bufs[len(r.inputs):])])
    for view in views:
        view += [refs[p], refs[p + 1]]
        p += 2
    return views


def comm_start(riders, *, name):
    kind = _peer_kind(riders)
    bufs = [a for r in riders for a in r.inputs]
    fresh = [f for r in riders for f in r.fresh]
    n_buf, n_fresh = len(bufs), len(fresh)

    def body(*refs):
        ins, outs = refs[:n_buf], refs[n_buf:]
        through, land, sems = outs[:n_buf], outs[n_buf:n_buf + n_fresh], outs[n_buf + n_fresh:-1]
        _peer_barrier(kind)
        per_rider, pb, pf = [], 0, 0
        for r in riders:
            per_rider += list(through[pb:pb + len(r.inputs)]) + list(land[pf:pf + len(r.fresh)])
            pb, pf = pb + len(r.inputs), pf + len(r.fresh)
        for r, (r_ins, r_outs, send, recv) in zip(riders, _split_refs(riders, per_rider + list(sems))):
            r.start(r_ins, r_outs, send, recv)
        outs[-1][...] = jnp.zeros((8, LANES), F32)

    sem_shapes = [pltpu.SemaphoreType.DMA((r.nsem,)) for r in riders for _ in (0, 1)]
    res = pl.pallas_call(
        body, name=name, in_specs=[ANY] * n_buf,
        out_specs=[ANY] * (n_buf + n_fresh) + [SEMS] * len(sem_shapes) + [pl.BlockSpec(memory_space=pltpu.VMEM)],
        out_shape=[jax.ShapeDtypeStruct(a.shape, a.dtype) for a in bufs] + fresh + sem_shapes
        + [jax.ShapeDtypeStruct((8, LANES), F32)],
        input_output_aliases={i: i for i in range(n_buf)},
        compiler_params=pltpu.CompilerParams(has_side_effects=SIDE_EFFECT, collective_id=PEER_KINDS.index(kind)))(*bufs)
    return (riders, list(res[:n_buf + n_fresh]), list(res[n_buf + n_fresh:-1])), res[-1]


def comm_wait(state, after, *, name):
    riders, bufs, sems = state
    n_buf, n_sem = len(bufs), len(sems)
    n_in = sum(len(r.inputs) for r in riders)

    def body(*refs):
        held, sem_refs = refs[:n_buf], refs[n_buf:n_buf + n_sem]
        through, land = held[:n_in], held[n_in:]
        per_rider, pb, pf = [], 0, 0
        for r in riders:
            per_rider += list(through[pb:pb + len(r.inputs)]) + list(land[pf:pf + len(r.fresh)])
            pb, pf = pb + len(r.inputs), pf + len(r.fresh)
        for r, (r_ins, r_outs, send, recv) in zip(riders, _split_refs(riders, per_rider + list(sem_refs))):
            r.finish(r_ins, r_outs, send, recv)

    res = pl.pallas_call(
        body, name=name, in_specs=[ANY] * n_buf + [SEMS] * n_sem + [ANY], out_specs=[ANY] * n_buf,
        out_shape=[jax.ShapeDtypeStruct(a.shape, a.dtype) for a in bufs],
        input_output_aliases={i: i for i in range(n_buf)},
        compiler_params=pltpu.CompilerParams(has_side_effects=SIDE_EFFECT))(*bufs, *sems, after)
    through, land = list(res[:n_in]), list(res[n_in:])
    out, pb, pf = [], 0, 0
    for r in riders:
        r_ins, r_land = through[pb:pb + len(r.inputs)], land[pf:pf + len(r.fresh)]
        pb, pf = pb + len(r.inputs), pf + len(r.fresh)
        out.append([r_ins[i] for i in r.aliased] + r_land)
    return out


SLAB_ROWS = 192


def _pad_rows(a, rows=8):
    return jnp.pad(a, ((0, rows - a.shape[0]), (0, 0)))


def _pack_small(norm_grads, db_qkv, db_o, dsinks, db_sp, dln_g, dln_b, dw_sp, loss_part):
    parts = [
        jnp.concatenate(norm_grads, axis=0),
        _pad_rows(jnp.pad(db_qkv, ((0, 0), (0, 2 * D_MODEL - QKV_WIDTH))).reshape(2, D_MODEL)),
        _pad_rows(db_o),
        _pad_rows(jnp.pad(dsinks.reshape(1, N_Q_HEADS), ((0, 0), (0, D_MODEL - N_Q_HEADS)))),
        _pad_rows(db_sp.reshape(1, D_MODEL)),
        _pad_rows(jnp.concatenate([dln_g, dln_b, jnp.pad(loss_part[0:1], ((0, 0), (0, D_MODEL - LANES)))], axis=0)),
        dw_sp.reshape(SGU_CHUNK, D_MODEL),
    ]
    slab = jnp.concatenate(parts, axis=0)
    return jnp.pad(slab, ((0, SLAB_ROWS - slab.shape[0]), (0, 0))).reshape(N_CHIPS, SLAB_ROWS // N_CHIPS, D_MODEL)


def _unpack_small(slab, j):
    slab = slab.reshape(SLAB_ROWS, D_MODEL)
    norms = [slab[2 * i:2 * i + 2] for i in range(4)]
    db_qkv = slab[8:10].reshape(1, 2 * D_MODEL)[:, :QKV_WIDTH]
    db_o = slab[16:17]
    dsinks = slab[24:25, :N_Q_HEADS]
    db_sp = slab[32:33].reshape(SGU_GROUPS, SGU_CHUNK)
    width = D_MODEL // N_CHIPS
    dln_g = lax.dynamic_slice(slab[40:41], (0, j * width), (1, width))
    dln_b = lax.dynamic_slice(slab[41:42], (0, j * width), (1, width))
    dw_sp = slab[48:48 + SGU_CHUNK].reshape(SGU_GROUPS * SGU_CHUNK, SGU_CHUNK)
    return norms, db_qkv, db_o, dsinks, db_sp, dln_g, dln_b, dw_sp, slab[42, 0]


class _GradReduce:
    def __init__(self, c_arr, jc_arr, dest_shapes):
        self.c_arr, self.jc_arr, self.dest_shapes = c_arr, jc_arr, dest_shapes
        self.grad, self.sibling, self.pair, self.chips, self.dest = {}, {}, {}, {}, {}

    def exchange(self, tags):
        return exchange_rider([self.grad[t] for t in tags])

    def exchanged(self, tags, res):
        for t, r in zip(tags, res):
            self.sibling[t] = r
            self.pair[t] = pair_add(self.grad[t], r, self.c_arr, name=f"pair_add_{t}")

    def scatter(self, tags):
        return scatter_rider([self.pair[t] for t in tags])

    def scattered(self, tags, res, where):
        for t, r in zip(tags, res):
            name, lead = where[t]
            self.dest[name] = final_add(self.grad[t], self.sibling[t], r, self.jc_arr, dest_shape=self.dest_shapes[name],
                                        lead=lead, prev=self.dest.get(name), name=f"final_add_{t}")

    def broadcast(self, items):
        names = []
        for n, _ in items:
            if n not in names:
                names.append(n)
        return names, broadcast_rider([self.dest[n] for n in names], [(names.index(n), lead) for n, lead in items])

    def broadcasted(self, names, res):
        for n, r in zip(names, res):
            self.dest[n] = r


def kernel(x, norm_mix_pre, norm_mix_post, norm_ffn_pre, norm_ffn_post, attn_w_qkv, attn_b_qkv, attn_sinks, attn_w_o, attn_b_o, sgu_w_in, sgu_ln_g, sgu_ln_b, sgu_w_spatial, sgu_b_spatial, sgu_w_out, ffn_w_gate_up, ffn_w_down, loss_target, m_norm_mix_pre, m_norm_mix_post, m_norm_ffn_pre, m_norm_ffn_post, m_attn_w_qkv, m_attn_b_qkv, m_attn_sinks, m_attn_w_o, m_attn_b_o, m_sgu_w_in, m_sgu_ln_g, m_sgu_ln_b, m_sgu_w_spatial, m_sgu_b_spatial, m_sgu_w_out, m_ffn_w_gate_up, m_ffn_w_down, v_norm_mix_pre, v_norm_mix_post, v_norm_ffn_pre, v_norm_ffn_post, v_attn_w_qkv, v_attn_b_qkv, v_attn_sinks, v_attn_w_o, v_attn_b_o, v_sgu_w_in, v_sgu_ln_g, v_sgu_ln_b, v_sgu_w_spatial, v_sgu_b_spatial, v_sgu_w_out, v_ffn_w_gate_up, v_ffn_w_down):
    s = x.shape[1]
    x0 = x.reshape(s, D_MODEL)
    target = loss_target.reshape(s, D_MODEL)
    mx, my, mc = lax.axis_index("x"), lax.axis_index("y"), lax.axis_index("c")
    chip = 2 * mx + my
    chip_arr = jnp.reshape(chip, (1,)).astype(I32)
    c_arr = jnp.reshape(mc, (1,)).astype(I32)
    jc_arr = jnp.stack([chip, mc]).astype(I32)
    zero_bias = jnp.zeros((1, D_MODEL), F32)

    def gain(p, i):
        return p[i:i + 1]

    big = [attn_w_qkv, attn_w_o, sgu_w_in, sgu_w_out, ffn_w_gate_up, ffn_w_gate_up, ffn_w_down, ffn_w_down]
    layers = [0, 0, 0, 0, 0, 1, 0, 1]
    tags = ["qkv", "wo", "win", "wout", "wgu0", "wgu1", "wd0", "wd1"]
    full = {t: place_shard(w, l, chip_arr, BF16, name=f"place_{t}") for w, l, t in zip(big, layers, tags)
            if t in ("qkv", "wo")}
    ln_pack = _pad_rows(jnp.concatenate([sgu_ln_g, sgu_ln_b], axis=0), 16)[None]
    full["ln"] = place_shard(ln_pack, 0, chip_arr, F32, name="place_ln")

    def split(items):
        return [i if isinstance(i, str) else i[0] for i in items], [WHOLE if isinstance(i, str) else tuple(i[1:]) for i in items]

    def ici(*items):
        names, pieces = split(items)
        return gather_ici_rider([full[n] for n in names], pieces)

    def d2d(*items):
        names, pieces = split(items)
        return gather_d2d_rider([full[n] for n in names], pieces)

    def landed(items, res):
        for n, r in zip(split(items)[0], res):
            full[n] = r

    cos, sin = _rope_tables(s)
    sink_rows = jnp.broadcast_to(
        jnp.repeat(attn_sinks.reshape(N_KV_HEADS, GQA_GROUP), WINDOW, axis=1)[:, None, :], (N_KV_HEADS, 8, ROWS))
    w_sp = sgu_w_spatial.reshape(SGU_GROUPS, SGU_CHUNK, SGU_CHUNK)
    b_sp = jnp.broadcast_to(sgu_b_spatial.reshape(SGU_GROUPS, SGU_CHUNK)[:, :, None], (SGU_GROUPS, SGU_CHUNK, LANES))

    (h0, full["wgu0"]), (res,) = prenorm_and_place(x0, gain(norm_mix_pre, 0), ffn_w_gate_up, 0, chip_arr, name="prenorm_0",
                                                   riders=[ici("qkv", "ln")])
    landed(("qkv", "ln"), res)
    full["wgu1"], (res,) = place_shard(ffn_w_gate_up, 1, chip_arr, BF16, name="place_wgu1", riders=[d2d("qkv", "ln")])
    landed(("qkv", "ln"), res)
    ln_g = full["ln"][:, 0, :].reshape(1, D_MODEL)
    ln_b = full["ln"][:, 1, :].reshape(1, D_MODEL)

    def hosted(call, stages):
        outputs, results = call([{"ici": ici, "d2d": d2d}[kind](*items) for kind, items in stages])
        for (_, items), res in zip(stages, results):
            landed(items, res)
        return outputs

    casts = [(ffn_w_down, 0), (sgu_w_in, 0), (ffn_w_down, 1), (sgu_w_out, 0)]
    qkv, full["wd0"], full["win"], full["wd1"], full["wout"] = hosted(
        lambda r: qkv_proj(h0, full["qkv"], attn_b_qkv, cos, sin, casts, chip_arr, name="qkv_proj", riders=r),
        [("ici", ("wo", ("wgu0", 0, 3, 8)))])
    o = hosted(lambda r: attn_fwd(qkv, sink_rows, name="attn_fwd", riders=r),
               [("d2d", ("wo",)), ("ici", (("wgu0", 3, 8, 8), ("wd0", 0, 2, 11), ("win", 0, 2, 8)))])
    w_o = full["wo"].reshape(Q_WIDTH, D_MODEL)
    x1, h1, m0 = hosted(lambda r: proj_residual_norm(o, w_o, x0, attn_b_o, gain(norm_mix_post, 0), gain(norm_ffn_pre, 0),
                                                     name="attn_out_norm", riders=r),
                        [("d2d", ("wgu0",)), ("ici", (("wd0", 2, 11, 11),))])
    gu0, a0 = hosted(lambda r: ffn_up(h1, full["wgu0"], name="ffn_up_0", riders=r),
                     [("d2d", ("wd0",)), ("ici", (("win", 2, 8, 8), "wout", ("wgu1", 0, 4, 8)))])
    w_d0 = full["wd0"].reshape(D_FF, D_MODEL)
    x2, h2, f0 = hosted(lambda r: proj_residual_norm(a0, w_d0, x1, zero_bias, gain(norm_ffn_post, 0), gain(norm_mix_pre, 1),
                                                     name="ffn_down_norm_0", riders=r),
                        [("d2d", ("win", "wout")), ("ici", (("wgu1", 4, 8, 8),))])
    w_in = full["win"]
    z, y = hosted(lambda r: sgu_in_fwd(h2, w_in, ln_g, ln_b, w_sp, b_sp, name="sgu_in_fwd", riders=r),
                  [("d2d", ("wgu1",)), ("ici", ("wd1",))])
    w_out = full["wout"].reshape(D_MODEL, D_MODEL)
    x3, h3, m1 = hosted(lambda r: proj_residual_norm(y, w_out, x2, zero_bias, gain(norm_mix_post, 1), gain(norm_ffn_pre, 1),
                                                     name="sgu_out_norm", riders=r),
                        [("d2d", ("wd1",))])
    w_qkv, w_gu0, w_gu1 = full["qkv"], full["wgu0"], full["wgu1"]
    w_d1 = full["wd1"].reshape(D_FF, D_MODEL)
    gu1, a1, dx4, df1, dg_fpost1, loss_part = ffn_fwd_loss_rows(
        h3, w_gu1, w_d1, x3, gain(norm_ffn_post, 1), target, name="ffn_fwd_loss_rows")

    red = _GradReduce(c_arr, jc_arr, {
        "qkv": attn_w_qkv.shape[1:], "wo": attn_w_o.shape[1:], "win": sgu_w_in.shape[1:], "wout": sgu_w_out.shape[1:],
        "wgu": ffn_w_gate_up.shape, "wd": ffn_w_down.shape, "slab": (N_CHIPS, SLAB_ROWS // N_CHIPS, D_MODEL)})
    where = {"qkv": ("qkv", None), "wo": ("wo", None), "win": ("win", None), "wout": ("wout", None), "wgu0": ("wgu", 0),
             "wgu1": ("wgu", 1), "wd0": ("wd", 0), "wd1": ("wd", 1), "small": ("slab", "chip")}

    dgu1, dx3, dm1, dg_fpre1, dg_mpost1, _ = ffn_bwd_rows(
        df1, w_d1, gu1, w_gu1, dx4, x3, gain(norm_ffn_pre, 1), m1, gain(norm_mix_post, 1), name="ffn_bwd_rows_1")
    red.grad["wd1"] = mm_tn(a1, df1, shard_major=False, tm=256, tn=D_MODEL, name="dw_down_1").reshape(
        N_CHIPS, D_FF // N_CHIPS, D_MODEL)
    red.grad["wgu1"], (res,) = mm_tn(h3, dgu1, shard_major=True, tm=512, tn=FF_HALF, name="dw_gate_up_1",
                                     riders=[red.exchange(["wd1"])])
    red.exchanged(["wd1"], res)
    dy, (res,) = mm_nt(dm1, w_out, out_dtype=F32, name="dy_sgu", riders=[red.exchange(["wgu1"])])
    red.exchanged(["wgu1"], res)
    red.grad["wout"] = mm_tn(y, dm1, shard_major=False, tm=512, tn=D_MODEL, name="dw_sgu_out").reshape(
        N_CHIPS, D_MODEL // N_CHIPS, D_MODEL)
    (dz, dw_sp, db_sp, dln_g, dln_b), (res_a, res_b) = sgu_bwd(
        z, dy, ln_g, ln_b, w_sp, b_sp, name="sgu_bwd", riders=[red.scatter(["wd1"]), red.exchange(["wout"])])
    red.scattered(["wd1"], res_a, where)
    red.exchanged(["wout"], res_b)
    names, rider = red.broadcast([("wd", 1)])
    red.grad["win"], (res_a, res_b) = mm_tn(h2, dz, shard_major=True, tm=D_MODEL, tn=2 * D_MODEL // N_CHIPS, name="dw_sgu_in",
                                            riders=[rider, red.scatter(["wout"])])
    red.broadcasted(names, res_a)
    red.scattered(["wout"], res_b, where)
    names, rider = red.broadcast([("wout", None)])
    (dx2, df0, dg_mpre1, dg_fpost0, _), (res_a, res_b) = dh_norm_bwd_pair(
        dz, w_in, dx3, x2, gain(norm_mix_pre, 1), f0, gain(norm_ffn_post, 0), name="dh_sgu_norm",
        riders=[red.exchange(["win"]), rider])
    red.exchanged(["win"], res_a)
    red.broadcasted(names, res_b)
    (dgu0, dx1, dm0, dg_fpre0, dg_mpost0, db_o), (res,) = ffn_bwd_rows(
        df0, w_d0, gu0, w_gu0, dx2, x1, gain(norm_ffn_pre, 0), m0, gain(norm_mix_post, 0), name="ffn_bwd_rows_0",
        riders=[red.scatter(["wgu1", "win"])])
    red.scattered(["wgu1", "win"], res, where)
    names, rider = red.broadcast([("wgu", 1), ("win", None)])
    dw_d0, (res,) = mm_tn(a0, df0, shard_major=False, tm=256, tn=D_MODEL, name="dw_down_0", riders=[rider])
    red.broadcasted(names, res)
    red.grad["wd0"] = dw_d0.reshape(N_CHIPS, D_FF // N_CHIPS, D_MODEL)
    do, (res,) = mm_nt(dm0, w_o, out_dtype=BF16, name="do_attn", riders=[red.exchange(["wd0"])])
    red.exchanged(["wd0"], res)
    red.grad["wgu0"], (res,) = mm_tn(h1, dgu0, shard_major=True, tm=512, tn=FF_HALF, name="dw_gate_up_0",
                                     riders=[red.scatter(["wd0"])])
    red.scattered(["wd0"], res, where)
    names, rider = red.broadcast([("wd", 0)])
    dw_o, (res_a, res_b) = mm_tn(o, dm0, shard_major=False, tm=512, tn=D_MODEL, name="dw_attn_out",
                                 riders=[red.exchange(["wgu0"]), rider])
    red.exchanged(["wgu0"], res_a)
    red.broadcasted(names, res_b)
    red.grad["wo"] = dw_o.reshape(N_CHIPS, Q_WIDTH // N_CHIPS, D_MODEL)
    (dq, dkc, dkp, dvc, dvp, dsink), (res_a, res_b) = attn_bwd(
        qkv, sink_rows, do, name="attn_bwd", riders=[red.scatter(["wgu0"]), red.exchange(["wo"])])
    red.scattered(["wgu0"], res_a, where)
    red.exchanged(["wo"], res_b)
    names, rider = red.broadcast([("wgu", 0)])
    (dqkv, db_qkv), (res,) = rope_bwd(dq, dkc, dkp, dvc, dvp, cos, sin, name="rope_bwd", riders=[rider])
    red.broadcasted(names, res)
    red.grad["qkv"], (res,) = mm_tn(h0, dqkv, shard_major=True, tm=D_MODEL, tn=QKV_WIDTH // N_CHIPS, name="dw_qkv",
                                    riders=[red.scatter(["wo"])])
    red.scattered(["wo"], res, where)
    grad_x, dg_mpre0 = dh_norm_bwd_last(dqkv, w_qkv, dx1, x0, gain(norm_mix_pre, 0), name="dh_attn_norm_in")

    norm_grads = [jnp.concatenate(p, axis=0) for p in
                  ((dg_mpre0, dg_mpre1), (dg_mpost0, dg_mpost1), (dg_fpre0, dg_fpre1), (dg_fpost0, dg_fpost1))]
    red.grad["small"] = _pack_small(norm_grads, db_qkv, db_o, dsink[:, :, 0, 0], db_sp[:, :, 0], dln_g, dln_b, dw_sp,
                                    loss_part)
    def big_update(w, g, m, v, tag, after=None):
        return adamw(w, g.reshape(w.shape), m, v, name=f"adamw_{tag}", after=after)

    (res,) = comm_call([red.exchange(["qkv", "small"])], name="tail_1")
    red.exchanged(["qkv", "small"], res)
    state, token = comm_start([red.scatter(["qkv", "small"])], name="tail_2_start")
    upd_wgu = big_update(ffn_w_gate_up, red.dest["wgu"], m_ffn_w_gate_up, v_ffn_w_gate_up, "wgu", after=token)
    (res,) = comm_wait(state, upd_wgu[1], name="tail_2_wait")
    red.scattered(["qkv", "small"], res, where)
    names, rider = red.broadcast([("qkv", None), ("wo", None)])
    state, token = comm_start([rider, allcast_rider(red.dest["slab"])], name="tail_3_start")
    upd_wd = big_update(ffn_w_down, red.dest["wd"], m_ffn_w_down, v_ffn_w_down, "wd", after=token)
    res, (slab_full,) = comm_wait(state, upd_wd[1], name="tail_3_wait")
    red.broadcasted(names, res)
    g_qkv, g_wo, g_win, g_wout = (red.dest[n] for n in ("qkv", "wo", "win", "wout"))
    g_norms, g_bqkv, g_bo, g_sinks, g_bsp, g_lng, g_lnb, g_wsp, loss = _unpack_small(slab_full, chip)

    upd = {
        "attn_w_qkv": big_update(attn_w_qkv, g_qkv, m_attn_w_qkv, v_attn_w_qkv, "qkv"),
        "attn_w_o": big_update(attn_w_o, g_wo, m_attn_w_o, v_attn_w_o, "wo"),
        "sgu_w_in": big_update(sgu_w_in, g_win, m_sgu_w_in, v_sgu_w_in, "win"),
        "sgu_w_out": big_update(sgu_w_out, g_wout, m_sgu_w_out, v_sgu_w_out, "wout"),
        "ffn_w_gate_up": upd_wgu,
        "ffn_w_down": upd_wd,
    }
    small_names = ["norm_mix_pre", "norm_mix_post", "norm_ffn_pre", "norm_ffn_post", "attn_b_qkv", "attn_sinks", "attn_b_o",
                   "sgu_ln_g", "sgu_ln_b", "sgu_w_spatial", "sgu_b_spatial"]
    small_w = [norm_mix_pre, norm_mix_post, norm_ffn_pre, norm_ffn_post, attn_b_qkv, attn_sinks, attn_b_o, sgu_ln_g, sgu_ln_b,
               sgu_w_spatial, sgu_b_spatial]
    small_m = [m_norm_mix_pre, m_norm_mix_post, m_norm_ffn_pre, m_norm_ffn_post, m_attn_b_qkv, m_attn_sinks, m_attn_b_o,
               m_sgu_ln_g, m_sgu_ln_b, m_sgu_w_spatial, m_sgu_b_spatial]
    small_v = [v_norm_mix_pre, v_norm_mix_post, v_norm_ffn_pre, v_norm_ffn_post, v_attn_b_qkv, v_attn_sinks, v_attn_b_o,
               v_sgu_ln_g, v_sgu_ln_b, v_sgu_w_spatial, v_sgu_b_spatial]
    small_g = g_norms + [g_bqkv, g_sinks, g_bo, g_lng, g_lnb, g_wsp, g_bsp]

    def flat2(a):
        return a.reshape(-1, a.shape[-1])

    res = adamw_small([flat2(a) for a in small_w], [flat2(a) for a in small_g], [flat2(a) for a in small_m],
                      [flat2(a) for a in small_v], name="adamw_small")
    for i, nm in enumerate(small_names):
        upd[nm] = tuple(r[i].reshape(small_w[i].shape) for r in res)

    order = ["norm_mix_pre", "norm_mix_post", "norm_ffn_pre", "norm_ffn_post", "attn_w_qkv", "attn_b_qkv", "attn_sinks",
             "attn_w_o", "attn_b_o", "sgu_w_in", "sgu_ln_g", "sgu_ln_b", "sgu_w_spatial", "sgu_b_spatial", "sgu_w_out",
             "ffn_w_gate_up", "ffn_w_down"]
    outs = [loss, grad_x.reshape(1, s, D_MODEL)]
    for part in range(4):
        outs += [upd[nm][part] for nm in order]
    return tuple(outs)
```

```python
import types

import numpy as np
import jax
import jax.numpy as jnp
from jax import lax
from jax.experimental import pallas as pl
from jax.experimental.pallas import tpu as pltpu

F32 = jnp.float32
BF16 = jnp.bfloat16
I32 = jnp.int32

D_MODEL = 1024
HEAD_DIM = 64
N_Q_HEADS = 16
N_KV_HEADS = 4
GQA_GROUP = 4
WINDOW = 128
Q_WIDTH = 1024
KV_WIDTH = 256
QKV_WIDTH = 1536
ROPE_THETA = 10000.0
SGU_GROUPS = 8
SGU_CHUNK = 128
D_FF = 2816
FF_HALF = D_FF // 2
EPS = 1e-6
N_CHIPS = 4
LANES = 128

ADAM_LR = 0.001
ADAM_B1 = 0.9
ADAM_B2 = 0.999
ADAM_EPS = 1e-08
ADAM_WD = 0.01
ADAM_STEP = 10

VMEM_LIMIT = 52 * 1024 * 1024
BIG_VMEM_LIMIT = 62 * 1024 * 1024
SUB_ROWS = 256
MESH = pl.DeviceIdType.MESH
NEG = -1e30
NT_DIMS = (((1,), (1,)), ((), ()))
TN_DIMS = (((0,), (0,)), ((), ()))
NN_DIMS = (((1,), (0,)), ((), ()))
ANY = pl.BlockSpec(memory_space=pl.ANY)


def _row_tile(s, want):
    return want if s % want == 0 else s


PEER_KINDS = ("sibling", "chips", "sibling+chips", "everyone")


def _peer_kind(riders):
    kinds = {r.peers for r in riders}
    if not kinds:
        return None
    if "everyone" in kinds:
        return "everyone"
    return "sibling+chips" if len(kinds) == 2 else kinds.pop()


def _peer_barrier(kind):
    x, y, c = _place()
    chips = [(*_partner(x, y, k), c) for k in (1, 2, 3)]
    peers = {"sibling": [(x, y, 1 - c)], "chips": chips, "sibling+chips": [(x, y, 1 - c)] + chips,
             "everyone": [(x, y, 1 - c)] + chips + [(px, py, 1 - c) for px, py, _ in chips]}[kind]
    barrier = pltpu.get_barrier_semaphore()
    for dev in peers:
        pl.semaphore_signal(barrier, inc=1, device_id=dev, device_id_type=MESH)
    pl.semaphore_wait(barrier, len(peers))


def _call(body, *, name, grid=(), in_specs=(), out_specs=(), out_shape=(), scratch_shapes=(), operands=(), prefetch=(),
          aliases=None, riders=(), sem=None, vmem_limit=VMEM_LIMIT):
    n_pre, n_in, n_out, n_scr = len(prefetch), len(operands), len(out_shape), len(scratch_shapes)
    in_specs, out_specs, out_shape = list(in_specs), list(out_specs), list(out_shape)
    operands, scratch_shapes = list(operands), list(scratch_shapes)
    io_alias = {n_pre + i: o for i, o in (aliases or {}).items()}
    for r in riders:
        base_in, base_out = n_pre + len(operands), len(out_shape)
        operands += list(r.inputs)
        in_specs += [ANY] * len(r.inputs)
        for pos, i in enumerate(r.aliased):
            io_alias[base_in + i] = base_out + pos
            out_shape.append(jax.ShapeDtypeStruct(r.inputs[i].shape, r.inputs[i].dtype))
        out_shape += list(r.fresh)
        out_specs += [ANY] * (len(r.aliased) + len(r.fresh))
        scratch_shapes += [pltpu.SemaphoreType.DMA((r.nsem,)), pltpu.SemaphoreType.DMA((r.nsem,))]

    def wrapped(*refs):
        pre, p = refs[:n_pre], n_pre
        core_in, p = refs[p:p + n_in], p + n_in
        r_in = []
        for r in riders:
            r_in.append(refs[p:p + len(r.inputs)])
            p += len(r.inputs)
        core_out, p = refs[p:p + n_out], p + n_out
        r_out = []
        for r in riders:
            k = len(r.aliased) + len(r.fresh)
            r_out.append(refs[p:p + k])
            p += k
        core_scr, p = refs[p:p + n_scr], p + n_scr
        r_sem = [refs[p + 2 * i:p + 2 * i + 2] for i in range(len(riders))]

        def edge(at_last, fns):
            def run():
                if not at_last:
                    _peer_barrier(peer_kind)
                for i, r in enumerate(riders):
                    getattr(r, fns)(r_in[i], r_out[i], r_sem[i][0], r_sem[i][1])
            if not riders:
                return
            if not grid:
                run()
                return
            cond = None
            for d, n in enumerate(grid):
                c = pl.program_id(d) == (n - 1 if at_last else 0)
                cond = c if cond is None else jnp.logical_and(cond, c)
            pl.when(cond)(run)

        edge(False, "start")
        if body is not None:
            body(*pre, *core_in, *core_out, *core_scr)
        edge(True, "finish")

    if sem is None or riders:
        sem = ("arbitrary",) * len(grid)
    kwargs = dict(out_shape=out_shape, input_output_aliases=io_alias, name=name)
    peer_kind = _peer_kind(riders)
    collective = {} if peer_kind is None else {"collective_id": PEER_KINDS.index(peer_kind)}
    if grid:
        kwargs["compiler_params"] = pltpu.CompilerParams(dimension_semantics=sem, vmem_limit_bytes=vmem_limit, **collective)
    elif collective:
        kwargs["compiler_params"] = pltpu.CompilerParams(**collective)
    if n_pre:
        kwargs["grid_spec"] = pltpu.PrefetchScalarGridSpec(
            num_scalar_prefetch=n_pre, grid=grid, in_specs=in_specs, out_specs=out_specs, scratch_shapes=scratch_shapes)
    else:
        kwargs.update(grid=grid, in_specs=in_specs, out_specs=out_specs, scratch_shapes=scratch_shapes)
    res = pl.pallas_call(wrapped, **kwargs)(*prefetch, *operands)
    core, rest, rider_res = list(res[:n_out]), list(res[n_out:]), []
    for r in riders:
        k = len(r.aliased) + len(r.fresh)
        rider_res.append(rest[:k])
        rest = rest[k:]
    return core, rider_res


def _mm_call(*, grid, in_specs, out_spec, out_shape, dims, nk, kaxis, acc_shape, name, operands, riders=()):
    out_dtype = out_shape.dtype

    def body(a_ref, b_ref, o_ref, *scratch):
        p = lax.dot_general(a_ref[...].astype(BF16), b_ref[...].astype(BF16), dims, preferred_element_type=F32)
        if nk == 1:
            o_ref[...] = p.astype(out_dtype)
        else:
            acc = scratch[0]
            kk = pl.program_id(kaxis)

            @pl.when(kk == 0)
            def _():
                acc[...] = p

            @pl.when(kk > 0)
            def _():
                acc[...] += p

            @pl.when(kk == nk - 1)
            def _():
                o_ref[...] = acc[...].astype(out_dtype)

    sem = ["parallel"] * len(grid)
    if nk > 1:
        sem[kaxis] = "arbitrary"
    (out,), rider_res = _call(
        body, grid=grid, in_specs=in_specs, out_specs=[out_spec], out_shape=[out_shape],
        scratch_shapes=[pltpu.VMEM(acc_shape, F32)] if nk > 1 else [], operands=operands, name=name, riders=riders,
        sem=tuple(sem))
    return (out, rider_res) if riders else out


def mm_nt(a, w, *, out_dtype, name, tm=1024, riders=()):
    m, n = a.shape
    kout = w.shape[0]
    tm = _row_tile(m, tm)
    return _mm_call(grid=(m // tm,),
                    in_specs=[pl.BlockSpec((tm, n), lambda i: (i, 0)), pl.BlockSpec((kout, n), lambda i: (0, 0))],
                    out_spec=pl.BlockSpec((tm, kout), lambda i: (i, 0)),
                    out_shape=jax.ShapeDtypeStruct((m, kout), out_dtype), dims=NT_DIMS, nk=1, kaxis=0,
                    acc_shape=None, name=name, operands=(a, w), riders=riders)


def mm_tn(a, b, *, shard_major, name, tm, tn, tk=None, out_dtype=BF16, riders=()):
    s, m = a.shape
    tk = s if tk is None else _row_tile(s, tk)
    if b.ndim == 3:
        n = 2 * b.shape[2]
        b_spec = pl.BlockSpec((None, tk, tn), lambda j, i, kk: (j // 2, kk, j % 2))
    else:
        n = b.shape[1]
        b_spec = pl.BlockSpec((tk, tn), lambda j, i, kk: (kk, j))
    if shard_major:
        assert tn == n // N_CHIPS
        o_spec = pl.BlockSpec((None, tm, tn), lambda j, i, kk: (j, i, 0))
        o_shape = jax.ShapeDtypeStruct((N_CHIPS, m, tn), out_dtype)
    else:
        o_spec = pl.BlockSpec((tm, tn), lambda j, i, kk: (i, j))
        o_shape = jax.ShapeDtypeStruct((m, n), out_dtype)
    return _mm_call(grid=(n // tn, m // tm, s // tk),
                    in_specs=[pl.BlockSpec((tk, tm), lambda j, i, kk: (kk, i)), b_spec], out_spec=o_spec,
                    out_shape=o_shape, dims=TN_DIMS, nk=s // tk, kaxis=2, acc_shape=(tm, tn), name=name, operands=(a, b),
                    riders=riders)


def _rstd(x):
    return lax.rsqrt(jnp.mean(x * x, axis=-1, keepdims=True) + EPS)


def _rms_bwd(dy, x, g):
    r = _rstd(x)
    xhat = x * r
    gy = dy * g
    dx = r * (gy - xhat * jnp.mean(gy * xhat, axis=-1, keepdims=True))
    return dx, jnp.sum(dy * xhat, axis=0, keepdims=True)


def _accum(ref, val, first):
    @pl.when(first)
    def _():
        ref[...] = val

    @pl.when(jnp.logical_not(first))
    def _():
        ref[...] += val


def _row_spec(tm, width):
    return pl.BlockSpec((tm, width), lambda i: (i, 0))


def _vec_spec(width):
    return pl.BlockSpec((1, width), lambda i: (0, 0))


def _ret(core, rider_res, riders):
    core = core[0] if len(core) == 1 else core
    return (core, rider_res) if riders else core


def prenorm_and_place(x, g, w, layer, chip_arr, *, name, tm=256, riders=()):
    s = x.shape[0]
    tm = _row_tile(s, tm)
    steps = s // tm
    _, r, c = w.shape
    tr = r // steps
    assert tr * steps == r and tr % 16 == 0

    def body(chip_ref, x_ref, g_ref, w_ref, h_ref, o_ref):
        xv = x_ref[...]
        h_ref[...] = (xv * _rstd(xv) * g_ref[...]).astype(BF16)
        o_ref[...] = w_ref[...].astype(BF16)

    core, rr = _call(
        body, grid=(steps,), prefetch=(chip_arr,),
        in_specs=[pl.BlockSpec((tm, D_MODEL), lambda i, chip: (i, 0)), pl.BlockSpec((1, D_MODEL), lambda i, chip: (0, 0)),
                  pl.BlockSpec((None, tr, c), lambda i, chip: (layer, i, 0))],
        out_specs=[pl.BlockSpec((tm, D_MODEL), lambda i, chip: (i, 0)), pl.BlockSpec((None, tr, c), lambda i, chip: (chip[0], i, 0))],
        out_shape=[jax.ShapeDtypeStruct((s, D_MODEL), BF16), jax.ShapeDtypeStruct((N_CHIPS, r, c), BF16)],
        operands=(x, g, w), sem=("parallel",), name=name, riders=riders)
    return _ret(core, rr, riders)


def proj_residual_norm(a, w, x, bias, g_post, g_next, *, name, tm=512, sub=256, riders=()):
    s, k = a.shape
    tm = _row_tile(s, tm)
    sub = min(sub, tm)

    def body(a_ref, w_ref, x_ref, b_ref, gp_ref, gn_ref, xo_ref, h_ref, m_ref):
        for t in range(tm // sub):
            rows = slice(t * sub, (t + 1) * sub)
            mv = jnp.dot(a_ref[rows, :], w_ref[...], preferred_element_type=F32) + b_ref[...]
            m_ref[rows, :] = mv.astype(BF16)
            xn = x_ref[rows, :] + mv * _rstd(mv) * gp_ref[...]
            xo_ref[rows, :] = xn
            h_ref[rows, :] = (xn * _rstd(xn) * gn_ref[...]).astype(BF16)

    row, vec = _row_spec(tm, D_MODEL), _vec_spec(D_MODEL)
    core, rr = _call(
        body, grid=(s // tm,),
        in_specs=[_row_spec(tm, k), pl.BlockSpec((k, D_MODEL), lambda i: (0, 0)), row, vec, vec, vec], out_specs=[row, row, row],
        out_shape=[jax.ShapeDtypeStruct((s, D_MODEL), F32), jax.ShapeDtypeStruct((s, D_MODEL), BF16),
                   jax.ShapeDtypeStruct((s, D_MODEL), BF16)],
        operands=(a, w, x, bias, g_post, g_next), sem=("parallel",), name=name, riders=riders)
    return _ret(core, rr, riders)


def proj_loss_head(a, w, x, g_post, target, *, name, tm=256, riders=()):
    s, k = a.shape
    tm = _row_tile(s, tm)

    def body(a_ref, w_ref, x_ref, g_ref, t_ref, dx_ref, df_ref, dg_ref, loss_ref):
        first = pl.program_id(0) == 0
        fv = jnp.dot(a_ref[...], w_ref[...], preferred_element_type=F32)
        g = g_ref[...]
        err = x_ref[...] + fv * _rstd(fv) * g - t_ref[...]
        dx = err * (1.0 / D_MODEL)
        dx_ref[...] = dx
        df, dg = _rms_bwd(dx, fv, g)
        df_ref[...] = df.astype(BF16)
        _accum(dg_ref, dg, first)
        part = jnp.sum(jnp.sum(err * err, axis=-1, keepdims=True), axis=0, keepdims=True) * (0.5 / D_MODEL)
        _accum(loss_ref, jnp.broadcast_to(part, (8, LANES)), first)

    row, vec = _row_spec(tm, D_MODEL), _vec_spec(D_MODEL)
    core, rr = _call(
        body, grid=(s // tm,), in_specs=[_row_spec(tm, k), pl.BlockSpec((k, D_MODEL), lambda i: (0, 0)), row, vec, row],
        out_specs=[row, row, vec, pl.BlockSpec((8, LANES), lambda i: (0, 0))],
        out_shape=[jax.ShapeDtypeStruct((s, D_MODEL), F32), jax.ShapeDtypeStruct((s, D_MODEL), BF16),
                   jax.ShapeDtypeStruct((1, D_MODEL), F32), jax.ShapeDtypeStruct((8, LANES), F32)],
        operands=(a, w, x, g_post, target), name=name, riders=riders)
    return _ret(core, rr, riders)


def ffn_fwd_loss_rows(h, w_gu, w_d, x, g_post, target, *, name, tm=512, riders=()):
    s = x.shape[0]
    tm = _row_tile(s, tm)

    def body(h_ref, w0, w1, w2, w3, wd_ref, x_ref, g_ref, t_ref, d_ref, a_ref, dx_ref, df_ref, dg_ref, loss_ref):
        first = pl.program_id(0) == 0
        halves = [slice(half * FF_HALF, (half + 1) * FF_HALF) for half in (0, 1)]
        gain = g_ref[...]
        sub = min(SUB_ROWS, tm)
        sums = None
        for t in range(tm // sub):
            rows = slice(t * sub, (t + 1) * sub)
            hv = h_ref[rows, :]
            fv = None
            for cols, (wg_ref, wu_ref) in zip(halves, ((w0, w2), (w1, w3))):
                g = jnp.dot(hv, wg_ref[...], preferred_element_type=F32)
                u = jnp.dot(hv, wu_ref[...], preferred_element_type=F32)
                sig = _sigmoid(g)
                silu = g * sig
                d_ref[0, rows, cols] = (u * (sig + silu * (1.0 - sig))).astype(BF16)
                d_ref[1, rows, cols] = silu.astype(BF16)
                act = (silu * u).astype(BF16)
                a_ref[rows, cols] = act
                p = jnp.dot(act, wd_ref[cols, :], preferred_element_type=F32)
                fv = p if fv is None else fv + p
            err = x_ref[rows, :] + fv * _rstd(fv) * gain - t_ref[rows, :]
            dx = err * (1.0 / D_MODEL)
            dx_ref[rows, :] = dx
            df, dg = _rms_bwd(dx, fv, gain)
            df_ref[rows, :] = df.astype(BF16)
            part = (dg, jnp.sum(jnp.sum(err * err, axis=-1, keepdims=True), axis=0, keepdims=True) * (0.5 / D_MODEL))
            sums = part if sums is None else tuple(a + b for a, b in zip(sums, part))
        _accum(dg_ref, sums[0], first)
        _accum(loss_ref, jnp.broadcast_to(sums[1], (8, LANES)), first)

    def resident(shape, index):
        return pl.BlockSpec(shape, index, pipeline_mode=pl.Buffered(1))

    row, vec = _row_spec(tm, D_MODEL), _vec_spec(D_MODEL)
    shards = [resident((None, D_MODEL, FF_HALF), (lambda j: (lambda i: (j, 0, 0)))(j)) for j in range(N_CHIPS)]
    core, rr = _call(
        body, grid=(s // tm,),
        in_specs=[row] + shards + [resident((D_FF, D_MODEL), lambda i: (0, 0)), row, vec, row],
        out_specs=[pl.BlockSpec((2, tm, D_FF), lambda i: (0, i, 0)), _row_spec(tm, D_FF), row, row, vec,
                   pl.BlockSpec((8, LANES), lambda i: (0, 0))],
        out_shape=[jax.ShapeDtypeStruct((2, s, D_FF), BF16), jax.ShapeDtypeStruct((s, D_FF), BF16),
                   jax.ShapeDtypeStruct((s, D_MODEL), F32), jax.ShapeDtypeStruct((s, D_MODEL), BF16),
                   jax.ShapeDtypeStruct((1, D_MODEL), F32), jax.ShapeDtypeStruct((8, LANES), F32)],
        operands=(h, w_gu, w_gu, w_gu, w_gu, w_d, x, g_post, target), name=name, riders=riders, vmem_limit=BIG_VMEM_LIMIT)
    return _ret(core, rr, riders)


def dh_norm_bwd_pair(a, w, dres, x, g_pre, m, g_post, *, name, tm=512, sub=256, riders=()):
    _, kout, ns = w.shape
    planes = a.ndim == 3
    s = x.shape[0]
    tm = _row_tile(s, tm)
    sub = min(sub, tm)
    a_spec = pl.BlockSpec((2, tm, 2 * ns), lambda i: (0, i, 0)) if planes else pl.BlockSpec((tm, N_CHIPS * ns), lambda i: (i, 0))

    def body(a_ref, w0, w1, w2, w3, dres_ref, x_ref, gpre_ref, m_ref, gpost_ref, dx_ref, dm_ref, dgpre_ref, dgpost_ref, db_ref):
        first = pl.program_id(0) == 0
        sums = None
        for t in range(tm // sub):
            rows = slice(t * sub, (t + 1) * sub)
            dh = None
            for j, w_ref in enumerate((w0, w1, w2, w3)):
                a_j = a_ref[j // 2, rows, (j % 2) * ns:(j % 2 + 1) * ns] if planes else a_ref[rows, j * ns:(j + 1) * ns]
                p = lax.dot_general(a_j, w_ref[...], NT_DIMS, preferred_element_type=F32)
                dh = p if dh is None else dh + p
            d1, dgpre = _rms_bwd(dh, x_ref[rows, :], gpre_ref[...])
            dx = dres_ref[rows, :] + d1
            dx_ref[rows, :] = dx
            dm, dgpost = _rms_bwd(dx, m_ref[rows, :].astype(F32), gpost_ref[...])
            dm_ref[rows, :] = dm.astype(BF16)
            part = (dgpre, dgpost, jnp.sum(dm, axis=0, keepdims=True))
            sums = part if sums is None else tuple(u + v for u, v in zip(sums, part))
        _accum(dgpre_ref, sums[0], first)
        _accum(dgpost_ref, sums[1], first)
        _accum(db_ref, sums[2], first)

    def shard(j):
        return pl.BlockSpec((None, kout, ns), lambda i: (j, 0, 0))

    row, vec = _row_spec(tm, D_MODEL), _vec_spec(D_MODEL)
    vshape = jax.ShapeDtypeStruct((1, D_MODEL), F32)
    core, rr = _call(
        body, grid=(s // tm,), in_specs=[a_spec] + [shard(j) for j in range(N_CHIPS)] + [row, row, vec, row, vec],
        out_specs=[row, row, vec, vec, vec],
        out_shape=[jax.ShapeDtypeStruct((s, D_MODEL), F32), jax.ShapeDtypeStruct((s, D_MODEL), BF16), vshape, vshape, vshape],
        operands=(a, w, w, w, w, dres, x, g_pre, m, g_post), name=name, riders=riders)
    return _ret(core, rr, riders)


def ffn_bwd_rows(df, w_d, d_planes, w_gu, dres, x, g_pre, m, g_post, *, name, tm=512, riders=()):
    s = x.shape[0]
    tm = _row_tile(s, tm)

    def body(df_ref, wd_ref, d_ref, w0, w1, w2, w3, dres_ref, x_ref, gpre_ref, m_ref, gpost_ref,
             o_ref, dx_ref, dm_ref, dgpre_ref, dgpost_ref, db_ref):
        first = pl.program_id(0) == 0
        halves = [slice(half * FF_HALF, (half + 1) * FF_HALF) for half in (0, 1)]
        sub = min(SUB_ROWS, tm)
        sums = None
        for t in range(tm // sub):
            rows = slice(t * sub, (t + 1) * sub)
            dfv = df_ref[rows, :]
            dh = None
            for cols, (wg_ref, wu_ref) in zip(halves, ((w0, w2), (w1, w3))):
                da = lax.dot_general(dfv, wd_ref[cols, :], NT_DIMS, preferred_element_type=F32)
                dg = (da * d_ref[0, rows, cols].astype(F32)).astype(BF16)
                du = (da * d_ref[1, rows, cols].astype(F32)).astype(BF16)
                o_ref[0, rows, cols] = dg
                o_ref[1, rows, cols] = du
                p = lax.dot_general(dg, wg_ref[...], NT_DIMS, preferred_element_type=F32)
                p += lax.dot_general(du, wu_ref[...], NT_DIMS, preferred_element_type=F32)
                dh = p if dh is None else dh + p
            d1, dgpre = _rms_bwd(dh, x_ref[rows, :], gpre_ref[...])
            dx = dres_ref[rows, :] + d1
            dx_ref[rows, :] = dx
            dm, dgpost = _rms_bwd(dx, m_ref[rows, :].astype(F32), gpost_ref[...])
            dm_ref[rows, :] = dm.astype(BF16)
            part = (dgpre, dgpost, jnp.sum(dm, axis=0, keepdims=True))
            sums = part if sums is None else tuple(a + b for a, b in zip(sums, part))
        _accum(dgpre_ref, sums[0], first)
        _accum(dgpost_ref, sums[1], first)
        _accum(db_ref, sums[2], first)

    def resident(shape, index):
        return pl.BlockSpec(shape, index, pipeline_mode=pl.Buffered(1))

    planes = pl.BlockSpec((2, tm, D_FF), lambda i: (0, i, 0))
    row, vec = _row_spec(tm, D_MODEL), _vec_spec(D_MODEL)
    vshape = jax.ShapeDtypeStruct((1, D_MODEL), F32)
    shards = [resident((None, D_MODEL, FF_HALF), (lambda j: (lambda i: (j, 0, 0)))(j)) for j in range(N_CHIPS)]
    core, rr = _call(
        body, grid=(s // tm,),
        in_specs=[row, resident((D_FF, D_MODEL), lambda i: (0, 0)), planes] + shards + [row, row, vec, row, vec],
        out_specs=[planes, row, row, vec, vec, vec],
        out_shape=[jax.ShapeDtypeStruct((2, s, D_FF), BF16), jax.ShapeDtypeStruct((s, D_MODEL), F32),
                   jax.ShapeDtypeStruct((s, D_MODEL), BF16), vshape, vshape, vshape],
        operands=(df, w_d, d_planes, w_gu, w_gu, w_gu, w_gu, dres, x, g_pre, m, g_post), name=name, riders=riders,
        vmem_limit=BIG_VMEM_LIMIT)
    return _ret(core, rr, riders)


def dh_norm_bwd_last(a, w, dres, x, g_pre, *, name, tm=512, sub=256):
    _, kout, ns = w.shape
    s = x.shape[0]
    tm = _row_tile(s, tm)
    sub = min(sub, tm)

    def body(a_ref, w0, w1, w2, w3, dres_ref, x_ref, g_ref, dx_ref, dg_ref):
        total = None
        for t in range(tm // sub):
            rows = slice(t * sub, (t + 1) * sub)
            dh = None
            for j, w_ref in enumerate((w0, w1, w2, w3)):
                p = lax.dot_general(a_ref[rows, j * ns:(j + 1) * ns], w_ref[...], NT_DIMS, preferred_element_type=F32)
                dh = p if dh is None else dh + p
            d1, dg = _rms_bwd(dh, x_ref[rows, :], g_ref[...])
            dx_ref[rows, :] = dres_ref[rows, :] + d1
            total = dg if total is None else total + dg
        _accum(dg_ref, total, pl.program_id(0) == 0)

    def shard(j):
        return pl.BlockSpec((None, kout, ns), lambda i: (j, 0, 0))

    row, vec = _row_spec(tm, D_MODEL), _vec_spec(D_MODEL)
    (dx, dg), _ = _call(
        body, grid=(s // tm,), in_specs=[_row_spec(tm, N_CHIPS * ns)] + [shard(j) for j in range(N_CHIPS)] + [row, row, vec],
        out_specs=[row, vec], out_shape=[jax.ShapeDtypeStruct((s, D_MODEL), F32), jax.ShapeDtypeStruct((1, D_MODEL), F32)],
        operands=(a, w, w, w, w, dres, x, g_pre), name=name)
    return dx, dg


def _rope_tables(s):
    half = HEAD_DIM // 2
    inv_freq = np.float32(ROPE_THETA) ** (-(np.arange(half, dtype=np.float32) * np.float32(2.0)) / np.float32(HEAD_DIM))
    ang = np.arange(s, dtype=np.float32)[:, None] * inv_freq[None, :]
    cos, sin = np.cos(ang).astype(np.float32), np.sin(ang).astype(np.float32)
    return jnp.asarray(np.tile(cos, (1, 4))), jnp.asarray(np.concatenate([-sin, sin, -sin, sin], axis=1))


def _swap_halves(x):
    lane = lax.broadcasted_iota(I32, x.shape, 1)
    return jnp.where((lane & (HEAD_DIM - 1)) < HEAD_DIM // 2, pltpu.roll(x, LANES - 32, 1), pltpu.roll(x, 32, 1))


N_ROPE_BLOCKS = (Q_WIDTH + KV_WIDTH) // LANES


def qkv_proj(h, w, bias, cos, sin, casts, chip_arr, *, name, tm=1024, riders=()):
    s, k = h.shape
    ns = w.shape[2]
    tm = _row_tile(s, tm)
    nc = len(casts)

    def body(chip_ref, h_ref, w_ref, b_ref, c_ref, s_ref, *rest):
        cast_in, o_ref, cast_out = rest[:nc], rest[nc], rest[nc + 1:]
        j = pl.program_id(0)

        @pl.when(jnp.logical_and(j == 0, pl.program_id(1) == 0))
        def _():
            for src, dst in zip(cast_in, cast_out):
                dst[...] = src[...].astype(BF16)

        sub = min(256, tm)
        for t in range(tm // sub):
            rows = slice(t * sub, (t + 1) * sub)
            p = jnp.dot(h_ref[rows, :], w_ref[...], preferred_element_type=F32) + b_ref[...]
            cosv, sinv = c_ref[rows, :], s_ref[rows, :]
            for blk in range(ns // LANES):
                xb = p[:, blk * LANES:(blk + 1) * LANES]
                roped = xb * cosv + _swap_halves(xb) * sinv
                is_qk = j * (ns // LANES) + blk < N_ROPE_BLOCKS
                o_ref[rows, blk * LANES:(blk + 1) * LANES] = jnp.where(is_qk, roped, xb).astype(BF16)

    def cast_in_spec(wt, layer):
        return pl.BlockSpec((None,) + wt.shape[1:], lambda j, i, chip: (layer, 0, 0), pipeline_mode=pl.Buffered(1))

    def cast_out_spec(wt):
        return pl.BlockSpec((None,) + wt.shape[1:], lambda j, i, chip: (chip[0], 0, 0), pipeline_mode=pl.Buffered(1))

    core, rr = _call(
        body, grid=(N_CHIPS, s // tm), prefetch=(chip_arr,),
        in_specs=[pl.BlockSpec((tm, k), lambda j, i, chip: (i, 0)), pl.BlockSpec((None, k, ns), lambda j, i, chip: (j, 0, 0)),
                  pl.BlockSpec((1, ns), lambda j, i, chip: (0, j)), pl.BlockSpec((tm, LANES), lambda j, i, chip: (i, 0)),
                  pl.BlockSpec((tm, LANES), lambda j, i, chip: (i, 0))] + [cast_in_spec(wt, layer) for wt, layer in casts],
        out_specs=[pl.BlockSpec((tm, ns), lambda j, i, chip: (i, j))] + [cast_out_spec(wt) for wt, _ in casts],
        out_shape=[jax.ShapeDtypeStruct((s, N_CHIPS * ns), BF16)]
        + [jax.ShapeDtypeStruct((N_CHIPS,) + wt.shape[1:], BF16) for wt, _ in casts],
        operands=(h, w, bias, cos, sin, *[wt for wt, _ in casts]), sem=("arbitrary", "arbitrary"), name=name, riders=riders)
    return (core, rr) if riders else core


def rope_bwd(dq, dkc, dkp, dvc, dvp, cos, sin, *, name, riders=()):
    s = dq.shape[0]
    tm = 2 * WINDOW if s % (2 * WINDOW) == 0 else WINDOW
    nb = s // tm

    def body(dq_ref, dkc_ref, dkp_ref, dkp_next_ref, dvc_ref, dvp_ref, dvp_next_ref, c_ref, s_ref, o_ref, db_ref):
        i = pl.program_id(0)
        has_next = (i < nb - 1).astype(F32)
        cosv, sinv = c_ref[...], s_ref[...]

        def shifted(ref, next_ref, cols):
            last = has_next * next_ref[:WINDOW, cols].astype(F32)
            return last if tm == WINDOW else jnp.concatenate([ref[WINDOW:, cols].astype(F32), last], axis=0)

        parts = []
        for blk in range(QKV_WIDTH // LANES):
            if blk < Q_WIDTH // LANES:
                g = dq_ref[:, blk * LANES:(blk + 1) * LANES].astype(F32)
            else:
                own, prv, nxt = (dkc_ref, dkp_ref, dkp_next_ref) if blk < N_ROPE_BLOCKS else (dvc_ref, dvp_ref, dvp_next_ref)
                cols = slice((blk % 2) * LANES, (blk % 2 + 1) * LANES)
                g = own[:, cols].astype(F32) + shifted(prv, nxt, cols)
            if blk < N_ROPE_BLOCKS:
                g = g * cosv + _swap_halves(g * sinv)
            o_ref[:, blk * LANES:(blk + 1) * LANES] = g.astype(BF16)
            parts.append(jnp.sum(g, axis=0, keepdims=True))
        sums = jnp.concatenate(parts, axis=1)
        _accum(db_ref, sums, i == 0)

    own_spec = _row_spec(tm, KV_WIDTH)
    next_spec = pl.BlockSpec((tm, KV_WIDTH), lambda i: (jnp.minimum(i + 1, nb - 1), 0))
    core, rr = _call(
        body, grid=(nb,),
        in_specs=[_row_spec(tm, Q_WIDTH), own_spec, own_spec, next_spec, own_spec, own_spec, next_spec,
                  _row_spec(tm, LANES), _row_spec(tm, LANES)],
        out_specs=[_row_spec(tm, QKV_WIDTH), _vec_spec(QKV_WIDTH)],
        out_shape=[jax.ShapeDtypeStruct((s, QKV_WIDTH), BF16), jax.ShapeDtypeStruct((1, QKV_WIDTH), F32)],
        operands=(dq, dkc, dkp, dkp, dvc, dvp, dvp, cos, sin), name=name, riders=riders)
    return _ret(core, rr, riders)


ROWS = GQA_GROUP * WINDOW


def _prev_slots():
    kpos = lax.broadcasted_iota(I32, (WINDOW, ROWS), 0)
    qpos = lax.broadcasted_iota(I32, (WINDOW, ROWS), 1) & (WINDOW - 1)
    return kpos > qpos


def _head_cols(ref, head):
    return ref[:, head * HEAD_DIM:(head + 1) * HEAD_DIM]


def _stack_heads(ref, h):
    return jnp.concatenate([_head_cols(ref, GQA_GROUP * h + g) for g in range(GQA_GROUP)], axis=0)


def _band(prev_ref, cur_ref, h):
    return jnp.concatenate([_head_cols(prev_ref, h), _head_cols(cur_ref, h)], axis=0)


def _pick(prev, band):
    return jnp.where(prev, band[:WINDOW], band[WINDOW:])


def _spread(prev, x):
    return jnp.concatenate([jnp.where(prev, x, 0.0), jnp.where(prev, 0.0, x)], axis=0).astype(BF16)


def _attn_probs(s_band, sink, prev, has_prev):
    scale = HEAD_DIM ** -0.5
    s = jnp.where(prev, jnp.where(has_prev, s_band[:WINDOW], NEG), s_band[WINDOW:]) * scale
    m = jnp.maximum(jnp.max(s, axis=0, keepdims=True), sink)
    e, es = jnp.exp(s - m), jnp.exp(sink - m)
    inv = 1.0 / (jnp.sum(e, axis=0, keepdims=True) + es)
    return e * inv, es * inv


def _attn_specs(nb):
    kcol, vcol = Q_WIDTH // KV_WIDTH, Q_WIDTH // KV_WIDTH + 1
    q_spec = pl.BlockSpec((WINDOW, Q_WIDTH), lambda n: (n, 0))
    return [q_spec,
            pl.BlockSpec((WINDOW, KV_WIDTH), lambda n: (n, kcol)),
            pl.BlockSpec((WINDOW, KV_WIDTH), lambda n: (jnp.maximum(n - 1, 0), kcol)),
            pl.BlockSpec((WINDOW, KV_WIDTH), lambda n: (n, vcol)),
            pl.BlockSpec((WINDOW, KV_WIDTH), lambda n: (jnp.maximum(n - 1, 0), vcol)),
            pl.BlockSpec((N_KV_HEADS, 8, ROWS), lambda n: (0, 0, 0))]


def attn_fwd(qkv, sink_rows, *, name, riders=()):
    s = qkv.shape[0]

    def body(q_ref, kc_ref, kp_ref, vc_ref, vp_ref, sink_ref, o_ref):
        prev = _prev_slots()
        has_prev = pl.program_id(0) > 0
        heads = range(N_KV_HEADS)
        s_bands = [lax.dot_general(_band(kp_ref, kc_ref, h), _stack_heads(q_ref, h), NT_DIMS, preferred_element_type=F32)
                   for h in heads]
        p_bands = [_spread(prev, _attn_probs(s_bands[h], sink_ref[h, 0:1, :], prev, has_prev)[0]) for h in heads]
        outs = [lax.dot_general(_band(vp_ref, vc_ref, h), p_bands[h], TN_DIMS, preferred_element_type=F32).T for h in heads]
        for h in heads:
            for g in range(GQA_GROUP):
                head = GQA_GROUP * h + g
                o_ref[:, head * HEAD_DIM:(head + 1) * HEAD_DIM] = outs[h][g * WINDOW:(g + 1) * WINDOW].astype(BF16)

    core, rr = _call(
        body, grid=(s // WINDOW,), in_specs=_attn_specs(s // WINDOW), out_specs=[pl.BlockSpec((WINDOW, Q_WIDTH), lambda n: (n, 0))],
        out_shape=[jax.ShapeDtypeStruct((s, Q_WIDTH), BF16)], operands=(qkv, qkv, qkv, qkv, qkv, sink_rows), sem=("parallel",),
        name=name, riders=riders)
    return _ret(core, rr, riders)


def attn_bwd(qkv, sink_rows, do, *, name, riders=()):
    s = qkv.shape[0]

    def body(q_ref, kc_ref, kp_ref, vc_ref, vp_ref, sink_ref, do_ref, dq_ref, dkc_ref, dkp_ref, dvc_ref, dvp_ref, dsink_ref):
        n = pl.program_id(0)
        prev = _prev_slots()
        scale = HEAD_DIM ** -0.5
        heads = range(N_KV_HEADS)
        qs, dos = [_stack_heads(q_ref, h) for h in heads], [_stack_heads(do_ref, h) for h in heads]
        kbands, vbands = [_band(kp_ref, kc_ref, h) for h in heads], [_band(vp_ref, vc_ref, h) for h in heads]
        s_bands = [lax.dot_general(kbands[h], qs[h], NT_DIMS, preferred_element_type=F32) for h in heads]
        dp_bands = [lax.dot_general(vbands[h], dos[h], NT_DIMS, preferred_element_type=F32) for h in heads]
        ds_bands, p_bands, parts = [], [], []
        for h in heads:
            p, ps = _attn_probs(s_bands[h], sink_ref[h, 0:1, :], prev, n > 0)
            dp = _pick(prev, dp_bands[h])
            delta = jnp.sum(p * dp, axis=0, keepdims=True)
            ds_bands.append(_spread(prev, p * (dp - delta) * scale))
            p_bands.append(_spread(prev, p))
            dsink = -(ps * delta)
            for g in range(GQA_GROUP):
                parts.append(jnp.broadcast_to(jnp.sum(dsink[:, g * WINDOW:(g + 1) * WINDOW], axis=1, keepdims=True), (8, LANES)))
        for h in heads:
            dk = jnp.dot(ds_bands[h], qs[h], preferred_element_type=F32).astype(BF16)
            dv = jnp.dot(p_bands[h], dos[h], preferred_element_type=F32).astype(BF16)
            dq = lax.dot_general(kbands[h], ds_bands[h], TN_DIMS, preferred_element_type=F32).T
            cols = slice(h * HEAD_DIM, (h + 1) * HEAD_DIM)
            dkp_ref[:, cols], dkc_ref[:, cols] = dk[:WINDOW], dk[WINDOW:]
            dvp_ref[:, cols], dvc_ref[:, cols] = dv[:WINDOW], dv[WINDOW:]
            for g in range(GQA_GROUP):
                head = GQA_GROUP * h + g
                dq_ref[:, head * HEAD_DIM:(head + 1) * HEAD_DIM] = dq[g * WINDOW:(g + 1) * WINDOW].astype(BF16)

        @pl.when(n == 0)
        def _():
            for i, part in enumerate(parts):
                dsink_ref[i // GQA_GROUP, i % GQA_GROUP] = part

        @pl.when(n > 0)
        def _():
            for i, part in enumerate(parts):
                dsink_ref[i // GQA_GROUP, i % GQA_GROUP] += part

    rows_q = pl.BlockSpec((WINDOW, Q_WIDTH), lambda n: (n, 0))
    rows_kv = pl.BlockSpec((WINDOW, KV_WIDTH), lambda n: (n, 0))
    kv_shape = jax.ShapeDtypeStruct((s, KV_WIDTH), BF16)
    core, rr = _call(
        body, grid=(s // WINDOW,), in_specs=_attn_specs(s // WINDOW) + [rows_q],
        out_specs=[rows_q, rows_kv, rows_kv, rows_kv, rows_kv,
                   pl.BlockSpec((N_KV_HEADS, GQA_GROUP, 8, LANES), lambda n: (0, 0, 0, 0))],
        out_shape=[jax.ShapeDtypeStruct((s, Q_WIDTH), BF16), kv_shape, kv_shape, kv_shape, kv_shape,
                   jax.ShapeDtypeStruct((N_KV_HEADS, GQA_GROUP, 8, LANES), F32)],
        operands=(qkv, qkv, qkv, qkv, qkv, sink_rows, do), sem=("arbitrary",), name=name, riders=riders)
    return _ret(core, rr, riders)


GELU_C = 0.7978845608028654
GELU_A = 0.044715


def _gelu(x):
    return 0.5 * x * (1.0 + jnp.tanh(x * (GELU_C + (GELU_C * GELU_A) * (x * x))))


def _gelu_and_grad(x):
    x2 = x * x
    t = jnp.tanh(x * (GELU_C + (GELU_C * GELU_A) * x2))
    half_x, one_t = 0.5 * x, 1.0 + t
    return half_x * one_t, 0.5 * one_t + half_x * (1.0 - t * t) * (GELU_C + (3.0 * GELU_C * GELU_A) * x2)


def _tril_bf16(w):
    row = lax.broadcasted_iota(I32, (SGU_CHUNK, SGU_CHUNK), 0)
    col = lax.broadcasted_iota(I32, (SGU_CHUNK, SGU_CHUNK), 1)
    return jnp.where(row >= col, w, 0.0).astype(BF16)


def _sgu_norm(vg, g, b):
    mu = jnp.mean(vg, axis=-1, keepdims=True)
    cen = vg - mu
    rstd = lax.rsqrt(jnp.mean(cen * cen, axis=-1, keepdims=True) + EPS)
    xhat = cen * rstd
    return xhat, rstd, xhat * g + b


def sgu_in_fwd(h, w_in, ln_g, ln_b, w_sp, b_sp, *, name, tm=512, riders=()):
    s, k = h.shape
    ns = w_in.shape[2]
    tm = _row_tile(s, tm)

    def body(h_ref, w0, w1, w2, w3, g_ref, b_ref, w_ref, bs_ref, z_ref, y_ref):
        hv = h_ref[...]
        zs = [jnp.dot(hv, w_ref_j[...], preferred_element_type=F32) for w_ref_j in (w0, w1, w2, w3)]
        for j, zj in enumerate(zs):
            z_ref[:, j * ns:(j + 1) * ns] = zj.astype(BF16)
        u = _gelu(jnp.concatenate(zs[:2], axis=1))
        _, _, vn = _sgu_norm(_gelu(jnp.concatenate(zs[2:], axis=1)), g_ref[...], b_ref[...])
        vn = vn.astype(BF16)
        for grp in range(SGU_GROUPS):
            w = _tril_bf16(w_ref[grp])
            cols = slice(grp * LANES, (grp + 1) * LANES)
            for ch in range(tm // SGU_CHUNK):
                rows = slice(ch * SGU_CHUNK, (ch + 1) * SGU_CHUNK)
                mixed = jnp.dot(w, vn[rows, cols], preferred_element_type=F32) + bs_ref[grp]
                y_ref[rows, cols] = (u[rows, cols] * mixed).astype(BF16)

    def shard(j):
        return pl.BlockSpec((None, k, ns), lambda i: (j, 0, 0))

    full3 = pl.BlockSpec((SGU_GROUPS, SGU_CHUNK, SGU_CHUNK), lambda i: (0, 0, 0))
    core, rr = _call(
        body, grid=(s // tm,),
        in_specs=[_row_spec(tm, k)] + [shard(j) for j in range(N_CHIPS)] + [_vec_spec(D_MODEL), _vec_spec(D_MODEL), full3, full3],
        out_specs=[_row_spec(tm, 2 * D_MODEL), _row_spec(tm, D_MODEL)],
        out_shape=[jax.ShapeDtypeStruct((s, 2 * D_MODEL), BF16), jax.ShapeDtypeStruct((s, D_MODEL), BF16)],
        operands=(h, w_in, w_in, w_in, w_in, ln_g, ln_b, w_sp, b_sp), sem=("parallel",), name=name, riders=riders)
    return _ret(core, rr, riders)


def sgu_bwd(z, dy, ln_g, ln_b, w_sp, b_sp, *, name, tm=256, riders=()):
    s = z.shape[0]
    tm = _row_tile(s, tm)

    def body(z_ref, dy_ref, g_ref, b_ref, w_ref, bs_ref, dz_ref, dw_ref, dbs_ref, dg_ref, db_ref, dvn_buf):
        first = pl.program_id(0) == 0
        u, u_grad = _gelu_and_grad(z_ref[:, :D_MODEL].astype(F32))
        vg, v_grad = _gelu_and_grad(z_ref[:, D_MODEL:].astype(F32))
        xhat, rstd, vn = _sgu_norm(vg, g_ref[...], b_ref[...])
        vn = vn.astype(BF16)
        dyv = dy_ref[...]
        dmixed = dyv * u
        dz_gate = dyv * u_grad
        row = lax.broadcasted_iota(I32, (SGU_CHUNK, SGU_CHUNK), 0)
        col = lax.broadcasted_iota(I32, (SGU_CHUNK, SGU_CHUNK), 1)
        dws, dbss = [], []
        for grp in range(SGU_GROUPS):
            w = _tril_bf16(w_ref[grp])
            cols = slice(grp * LANES, (grp + 1) * LANES)
            dw = jnp.zeros((SGU_CHUNK, SGU_CHUNK), F32)
            dbs = jnp.zeros((SGU_CHUNK, 1), F32)
            for ch in range(tm // SGU_CHUNK):
                rows = slice(ch * SGU_CHUNK, (ch + 1) * SGU_CHUNK)
                vblk = vn[rows, cols]
                mixed = jnp.dot(w, vblk, preferred_element_type=F32) + bs_ref[grp]
                dz_ref[rows, cols] = (dz_gate[rows, cols] * mixed).astype(BF16)
                dm = dmixed[rows, cols]
                dmb = dm.astype(BF16)
                dvn_buf[rows, cols] = lax.dot_general(w, dmb, TN_DIMS, preferred_element_type=F32)
                dw += lax.dot_general(dmb, vblk, NT_DIMS, preferred_element_type=F32)
                dbs += jnp.sum(dm, axis=-1, keepdims=True)
            dws.append(jnp.where(row >= col, dw, 0.0))
            dbss.append(jnp.broadcast_to(dbs, (SGU_CHUNK, SGU_CHUNK)))

        dvn = dvn_buf[...]
        dxhat = dvn * g_ref[...]
        dvg = rstd * (dxhat - jnp.mean(dxhat, axis=-1, keepdims=True) - xhat * jnp.mean(dxhat * xhat, axis=-1, keepdims=True))
        dz_ref[:, D_MODEL:] = (dvg * v_grad).astype(BF16)
        dlng, dlnb = jnp.sum(dvn * xhat, axis=0, keepdims=True), jnp.sum(dvn, axis=0, keepdims=True)

        @pl.when(first)
        def _():
            for grp in range(SGU_GROUPS):
                dw_ref[grp] = dws[grp]
                dbs_ref[grp] = dbss[grp]
            dg_ref[...] = dlng
            db_ref[...] = dlnb

        @pl.when(jnp.logical_not(first))
        def _():
            for grp in range(SGU_GROUPS):
                dw_ref[grp] += dws[grp]
                dbs_ref[grp] += dbss[grp]
            dg_ref[...] += dlng
            db_ref[...] += dlnb

    full3 = pl.BlockSpec((SGU_GROUPS, SGU_CHUNK, SGU_CHUNK), lambda i: (0, 0, 0))
    s3 = jax.ShapeDtypeStruct((SGU_GROUPS, SGU_CHUNK, SGU_CHUNK), F32)
    vshape = jax.ShapeDtypeStruct((1, D_MODEL), F32)
    core, rr = _call(
        body, grid=(s // tm,),
        in_specs=[_row_spec(tm, 2 * D_MODEL), _row_spec(tm, D_MODEL), _vec_spec(D_MODEL), _vec_spec(D_MODEL), full3, full3],
        out_specs=[_row_spec(tm, 2 * D_MODEL), full3, full3, _vec_spec(D_MODEL), _vec_spec(D_MODEL)],
        out_shape=[jax.ShapeDtypeStruct((s, 2 * D_MODEL), BF16), s3, s3, vshape, vshape],
        scratch_shapes=[pltpu.VMEM((tm, D_MODEL), F32)], operands=(z, dy, ln_g, ln_b, w_sp, b_sp), name=name, riders=riders)
    return _ret(core, rr, riders)


def _sigmoid(x):
    return 1.0 / (1.0 + jnp.exp(-x))


def ffn_up(h, w_gu, *, name, tm=512, riders=()):
    s = h.shape[0]
    tm = _row_tile(s, tm)

    def body(h_ref, wg_ref, wu_ref, d_ref, a_ref):
        hv = h_ref[...]
        sub = min(256, tm)
        for t in range(tm // sub):
            rows = slice(t * sub, (t + 1) * sub)
            g = jnp.dot(hv[rows], wg_ref[...], preferred_element_type=F32)
            u = jnp.dot(hv[rows], wu_ref[...], preferred_element_type=F32)
            sig = _sigmoid(g)
            silu = g * sig
            d_ref[0, rows, :] = (u * (sig + silu * (1.0 - sig))).astype(BF16)
            d_ref[1, rows, :] = silu.astype(BF16)
            a_ref[rows, :] = (silu * u).astype(BF16)

    core, rr = _call(
        body, grid=(2, s // tm),
        in_specs=[pl.BlockSpec((tm, D_MODEL), lambda j, i: (i, 0)),
                  pl.BlockSpec((None, D_MODEL, FF_HALF), lambda j, i: (j, 0, 0)),
                  pl.BlockSpec((None, D_MODEL, FF_HALF), lambda j, i: (j + 2, 0, 0))],
        out_specs=[pl.BlockSpec((2, tm, FF_HALF), lambda j, i: (0, i, j)), pl.BlockSpec((tm, FF_HALF), lambda j, i: (i, j))],
        out_shape=[jax.ShapeDtypeStruct((2, s, D_FF), BF16), jax.ShapeDtypeStruct((s, D_FF), BF16)],
        operands=(h, w_gu, w_gu), sem=("parallel", "parallel"), name=name, riders=riders)
    return _ret(core, rr, riders)


def _weight_tile(rows):
    for tr in (512, 352, 256, 128):
        if rows % tr == 0:
            return tr
    return rows


def place_shard(w, layer, chip_arr, dtype, *, name, riders=()):
    _, r, c = w.shape
    tr = _weight_tile(r)

    def body(chip_ref, w_ref, o_ref):
        o_ref[...] = w_ref[...].astype(dtype)

    core, rr = _call(
        body, grid=(r // tr,), prefetch=(chip_arr,),
        in_specs=[pl.BlockSpec((None, tr, c), lambda i, chip: (layer, i, 0))],
        out_specs=[pl.BlockSpec((None, tr, c), lambda i, chip: (chip[0], i, 0))],
        out_shape=[jax.ShapeDtypeStruct((N_CHIPS, r, c), dtype)], operands=(w,), sem=("parallel",), name=name, riders=riders)
    return _ret(core, rr, riders)


def _adamw_math(w, g, m, v):
    m = ADAM_B1 * m + (1.0 - ADAM_B1) * g
    v = ADAM_B2 * v + (1.0 - ADAM_B2) * (g * g)
    m_hat = m / (1.0 - ADAM_B1 ** ADAM_STEP)
    v_hat = v / (1.0 - ADAM_B2 ** ADAM_STEP)
    delta = -ADAM_LR * (m_hat / (jnp.sqrt(v_hat) + ADAM_EPS) + ADAM_WD * w)
    return delta, m, v


def adamw(w, g, m, v, *, name, after=None):
    nl, r, c = w.shape
    tr = _weight_tile(r)

    def body(w_ref, g_ref, m_ref, v_ref, *rest):
        go_ref, d_ref, mo_ref, vo_ref = rest[-4:]
        gv = g_ref[...]
        go_ref[...] = gv
        d_ref[...], mo_ref[...], vo_ref[...] = _adamw_math(w_ref[...], gv, m_ref[...], v_ref[...])

    spec = pl.BlockSpec((None, tr, c), lambda l, i: (l, i, 0))
    shape = jax.ShapeDtypeStruct(w.shape, F32)
    extra = [] if after is None else [after]
    outs, _ = _call(body, grid=(nl, r // tr), in_specs=[spec] * 4 + [ANY] * len(extra), out_specs=[spec] * 4,
                    out_shape=[shape] * 4, operands=(w, g, m, v, *extra), sem=("parallel", "parallel"), name=name)
    return outs


def adamw_small(ws, gs, ms, vs, *, name):
    n = len(ws)

    def body(*refs):
        ins, outs = refs[:4 * n], refs[4 * n:]
        for t in range(n):
            gv = ins[n + t][...]
            outs[t][...] = gv
            outs[n + t][...], outs[2 * n + t][...], outs[3 * n + t][...] = _adamw_math(
                ins[t][...], gv, ins[2 * n + t][...], ins[3 * n + t][...])

    shapes = [jax.ShapeDtypeStruct(w.shape, F32) for w in ws]
    res = pl.pallas_call(body, out_shape=shapes * 4, name=name)(*ws, *gs, *ms, *vs)
    return res[:n], res[n:2 * n], res[2 * n:3 * n], res[3 * n:]


def pair_add(g, r1, c_arr, *, name):
    _, rows, cdim = g.shape
    h = rows // 2

    def body(c_ref, g_ref, r_ref, o_ref):
        o_ref[...] = (g_ref[...].astype(F32) + r_ref[...].astype(F32)).astype(o_ref.dtype)

    (out,), _ = _call(
        body, grid=(N_CHIPS,), prefetch=(c_arr,),
        in_specs=[pl.BlockSpec((None, h, cdim), lambda s, c: (s, c[0], 0)), pl.BlockSpec((None, h, cdim), lambda s, c: (s, 0, 0))],
        out_specs=[pl.BlockSpec((None, h, cdim), lambda s, c: (s, 0, 0))],
        out_shape=[jax.ShapeDtypeStruct((N_CHIPS, h, cdim), g.dtype)], operands=(g, r1), sem=("parallel",), name=name)
    return out


def final_add(g, r1, r2, jc_arr, *, dest_shape, lead, prev, name):
    _, rows, cdim = g.shape
    h = rows // 2

    def body(jc_ref, g_ref, r1_ref, r2_ref, *rest):
        o_ref = rest[-1]
        acc = g_ref[...].astype(F32) + r1_ref[...].astype(F32)
        for k in range(3):
            acc = acc + r2_ref[k].astype(F32)
        o_ref[...] = acc

    if lead is None:
        o_spec = pl.BlockSpec((h, cdim), lambda i, jc: (jc[1], 0))
    elif lead == "chip":
        o_spec = pl.BlockSpec((None, h, cdim), lambda i, jc: (jc[0], jc[1], 0))
    else:
        o_spec = pl.BlockSpec((None, h, cdim), lambda i, jc: (lead, jc[1], 0))
    in_specs = [pl.BlockSpec((None, h, cdim), lambda i, jc: (jc[0], jc[1], 0)),
                pl.BlockSpec((None, h, cdim), lambda i, jc: (jc[0], 0, 0)),
                pl.BlockSpec((3, h, cdim), lambda i, jc: (0, 0, 0))]
    operands = [g, r1, r2]
    aliases = None
    if prev is not None:
        in_specs.append(ANY)
        operands.append(prev)
        aliases = {3: 0}
    (out,), _ = _call(body, grid=(1,), prefetch=(jc_arr,), in_specs=in_specs, out_specs=[o_spec],
                      out_shape=[jax.ShapeDtypeStruct(dest_shape, F32)], operands=operands, aliases=aliases, name=name)
    return out


def _place():
    return lax.axis_index("x"), lax.axis_index("y"), lax.axis_index("c")


def _partner(x, y, k):
    return (1 - x if k >> 1 else x), (1 - y if k & 1 else y)


WHOLE = (0, 1, 1)


def _half(rows, sel, dtype, piece=WHOLE):
    lo, hi, n = piece
    align = 16 if dtype == BF16 else 8
    step = rows // 2 // n
    assert rows // 2 == step * n and step % align == 0
    return pl.ds(pl.multiple_of(sel * (rows // 2) + lo * step, align), (hi - lo) * step)


def _rider(peers, inputs, aliased, fresh, nsem, copies, arrivals):
    def start(ins, outs, send, recv):
        for cp in copies(ins, outs, send, recv):
            cp.start()

    def finish(ins, outs, send, recv):
        for cp in arrivals(ins, outs, send, recv):
            cp.wait_recv()
        for cp in copies(ins, outs, send, recv):
            cp.wait_send()

    return types.SimpleNamespace(peers=peers, inputs=list(inputs), aliased=list(aliased), fresh=list(fresh), nsem=nsem,
                                 start=start, finish=finish)


def _remote(src, dst, send, recv, idx, dev):
    return pltpu.make_async_remote_copy(src_ref=src, dst_ref=dst, send_sem=send.at[idx], recv_sem=recv.at[idx],
                                        device_id=dev, device_id_type=MESH)


def gather_ici_rider(fulls, pieces=None):
    nt = len(fulls)
    pieces = pieces or [WHOLE] * nt

    def region(outs, t, slot, sel):
        return outs[t].at[slot, _half(fulls[t].shape[1], sel, fulls[t].dtype, pieces[t])]

    def copies(ins, outs, send, recv):
        x, y, c = _place()
        res = []
        for t in range(nt):
            for k in (1, 2, 3):
                px, py = _partner(x, y, k)
                mine = region(outs, t, 2 * x + y, c)
                res.append(_remote(mine, mine, send, recv, 3 * t + k - 1, (px, py, c)))
        return res

    def arrivals(ins, outs, send, recv):
        x, y, c = _place()
        res = []
        for t in range(nt):
            for k in (1, 2, 3):
                px, py = _partner(x, y, k)
                theirs = region(outs, t, 2 * px + py, c)
                res.append(_remote(theirs, theirs, send, recv, 3 * t + k - 1, (x, y, c)))
        return res

    return _rider("chips", fulls, range(nt), [], 3 * nt, copies, arrivals)


def gather_d2d_rider(fulls, pieces=None):
    nt = len(fulls)
    pieces = pieces or [WHOLE] * nt

    def region(outs, t, slot, sel):
        return outs[t].at[slot, _half(fulls[t].shape[1], sel, fulls[t].dtype, pieces[t])]

    def both(outs, send, recv, mine):
        x, y, c = _place()
        res = []
        for t in range(nt):
            for k in (1, 2, 3):
                px, py = _partner(x, y, k)
                part = region(outs, t, 2 * px + py, c if mine else 1 - c)
                res.append(_remote(part, part, send, recv, 3 * t + k - 1, (x, y, 1 - c)))
        return res

    return _rider("sibling", fulls, range(nt), [], 3 * nt, lambda i, o, s, r: both(o, s, r, True),
                  lambda i, o, s, r: both(o, s, r, False))


def exchange_rider(grads):
    nt = len(grads)

    def both(ins, outs, send, recv):
        x, y, c = _place()
        return [_remote(ins[t].at[:, _half(grads[t].shape[1], 1 - c, grads[t].dtype)], outs[t], send, recv, t, (x, y, 1 - c))
                for t in range(nt)]

    fresh = [jax.ShapeDtypeStruct((N_CHIPS, g.shape[1] // 2, g.shape[2]), g.dtype) for g in grads]
    return _rider("sibling", grads, [], fresh, nt, both, both)


def scatter_rider(parts):
    nt = len(parts)

    def both(ins, outs, send, recv):
        x, y, c = _place()
        res = []
        for t in range(nt):
            for k in (1, 2, 3):
                px, py = _partner(x, y, k)
                res.append(_remote(ins[t].at[2 * px + py], outs[t].at[k - 1], send, recv, 3 * t + k - 1, (px, py, c)))
        return res

    fresh = [jax.ShapeDtypeStruct((3,) + p.shape[1:], p.dtype) for p in parts]
    return _rider("chips", parts, [], fresh, 3 * nt, both, both)


def broadcast_rider(bufs, items):
    def region(outs, item, sel):
        bi, lead = item
        ref = outs[bi]
        if lead == "chip":
            x, y, _ = _place()
            ref = ref.at[2 * x + y]
        elif lead is not None:
            ref = ref.at[lead]
        return ref.at[_half(ref.shape[0], sel, F32)]

    def both(outs, send, recv, mine):
        x, y, c = _place()
        res = []
        for i, item in enumerate(items):
            part = region(outs, item, c if mine else 1 - c)
            res.append(_remote(part, part, send, recv, i, (x, y, 1 - c)))
        return res

    return _rider("sibling", bufs, range(len(bufs)), [], len(items), lambda i, o, s, r: both(o, s, r, True),
                  lambda i, o, s, r: both(o, s, r, False))


def allcast_rider(buf):
    peers = [(k, flip) for k in range(N_CHIPS) for flip in (0, 1) if (k, flip) != (0, 0)]

    def both(outs, send, recv, mine):
        x, y, c = _place()
        res = []
        for i, (k, flip) in enumerate(peers):
            px, py = _partner(x, y, k)
            pc = 1 - c if flip else c
            slot, sel = (2 * x + y, c) if mine else (2 * px + py, pc)
            part = outs[0].at[slot, _half(buf.shape[1], sel, F32)]
            res.append(_remote(part, part, send, recv, i, (px, py, pc)))
        return res

    return _rider("everyone", [buf], [0], [], len(peers), lambda i, o, s, r: both(o, s, r, True),
                  lambda i, o, s, r: both(o, s, r, False))


def comm_call(riders, *, name):
    _, res = _call(None, riders=riders, name=name)
    return res


SEMS = pl.BlockSpec(memory_space=pltpu.SEMAPHORE)
SIDE_EFFECT = pltpu.SideEffectType.DATAFLOW_SIDE_EFFECTING


def _split_refs(riders, refs):
    views, p = [], 0
    for r in riders:
        bufs = refs[p:p + len(r.inputs) + len(r.fresh)]
        p += len(bufs)
        ins = bufs[:len(r.inputs)]
        views.append([ins, [ins[i] for i in r.aliased] + list(bufs[len(r.inputs):])])
    for view in views:
        view += [refs[p], refs[p + 1]]
        p += 2
    return views


def comm_start(riders, *, name):
    kind = _peer_kind(riders)
    bufs = [a for r in riders for a in r.inputs]
    fresh = [f for r in riders for f in r.fresh]
    n_buf, n_fresh = len(bufs), len(fresh)

    def body(*refs):
        ins, outs = refs[:n_buf], refs[n_buf:]
        through, land, sems = outs[:n_buf], outs[n_buf:n_buf + n_fresh], outs[n_buf + n_fresh:-1]
        _peer_barrier(kind)
        per_rider, pb, pf = [], 0, 0
        for r in riders:
            per_rider += list(through[pb:pb + len(r.inputs)]) + list(land[pf:pf + len(r.fresh)])
            pb, pf = pb + len(r.inputs), pf + len(r.fresh)
        for r, (r_ins, r_outs, send, recv) in zip(riders, _split_refs(riders, per_rider + list(sems))):
            r.start(r_ins, r_outs, send, recv)
        outs[-1][...] = jnp.zeros((8, LANES), F32)

    sem_shapes = [pltpu.SemaphoreType.DMA((r.nsem,)) for r in riders for _ in (0, 1)]
    res = pl.pallas_call(
        body, name=name, in_specs=[ANY] * n_buf,
        out_specs=[ANY] * (n_buf + n_fresh) + [SEMS] * len(sem_shapes) + [pl.BlockSpec(memory_space=pltpu.VMEM)],
        out_shape=[jax.ShapeDtypeStruct(a.shape, a.dtype) for a in bufs] + fresh + sem_shapes
        + [jax.ShapeDtypeStruct((8, LANES), F32)],
        input_output_aliases={i: i for i in range(n_buf)},
        compiler_params=pltpu.CompilerParams(has_side_effects=SIDE_EFFECT, collective_id=PEER_KINDS.index(kind)))(*bufs)
    return (riders, list(res[:n_buf + n_fresh]), list(res[n_buf + n_fresh:-1])), res[-1]


def comm_wait(state, after, *, name):
    riders, bufs, sems = state
    n_buf, n_sem = len(bufs), len(sems)
    n_in = sum(len(r.inputs) for r in riders)

    def body(*refs):
        held, sem_refs = refs[:n_buf], refs[n_buf:n_buf + n_sem]
        through, land = held[:n_in], held[n_in:]
        per_rider, pb, pf = [], 0, 0
        for r in riders:
            per_rider += list(through[pb:pb + len(r.inputs)]) + list(land[pf:pf + len(r.fresh)])
            pb, pf = pb + len(r.inputs), pf + len(r.fresh)
        for r, (r_ins, r_outs, send, recv) in zip(riders, _split_refs(riders, per_rider + list(sem_refs))):
            r.finish(r_ins, r_outs, send, recv)

    res = pl.pallas_call(
        body, name=name, in_specs=[ANY] * n_buf + [SEMS] * n_sem + [ANY], out_specs=[ANY] * n_buf,
        out_shape=[jax.ShapeDtypeStruct(a.shape, a.dtype) for a in bufs],
        input_output_aliases={i: i for i in range(n_buf)},
        compiler_params=pltpu.CompilerParams(has_side_effects=SIDE_EFFECT))(*bufs, *sems, after)
    through, land = list(res[:n_in]), list(res[n_in:])
    out, pb, pf = [], 0, 0
    for r in riders:
        r_ins, r_land = through[pb:pb + len(r.inputs)], land[pf:pf + len(r.fresh)]
        pb, pf = pb + len(r.inputs), pf + len(r.fresh)
        out.append([r_ins[i] for i in r.aliased] + r_land)
    return out


SLAB_ROWS = 192


def _pad_rows(a, rows=8):
    return jnp.pad(a, ((0, rows - a.shape[0]), (0, 0)))


def _pack_small(norm_grads, db_qkv, db_o, dsinks, db_sp, dln_g, dln_b, dw_sp, loss_part):
    parts = [
        jnp.concatenate(norm_grads, axis=0),
        _pad_rows(jnp.pad(db_qkv, ((0, 0), (0, 2 * D_MODEL - QKV_WIDTH))).reshape(2, D_MODEL)),
        _pad_rows(db_o),
        _pad_rows(jnp.pad(dsinks.reshape(1, N_Q_HEADS), ((0, 0), (0, D_MODEL - N_Q_HEADS)))),
        _pad_rows(db_sp.reshape(1, D_MODEL)),
        _pad_rows(jnp.concatenate([dln_g, dln_b, jnp.pad(loss_part[0:1], ((0, 0), (0, D_MODEL - LANES)))], axis=0)),
        dw_sp.reshape(SGU_CHUNK, D_MODEL),
    ]
    slab = jnp.concatenate(parts, axis=0)
    return jnp.pad(slab, ((0, SLAB_ROWS - slab.shape[0]), (0, 0))).reshape(N_CHIPS, SLAB_ROWS // N_CHIPS, D_MODEL)


def _unpack_small(slab, j):
    slab = slab.reshape(SLAB_ROWS, D_MODEL)
    norms = [slab[2 * i:2 * i + 2] for i in range(4)]
    db_qkv = slab[8:10].reshape(1, 2 * D_MODEL)[:, :QKV_WIDTH]
    db_o = slab[16:17]
    dsinks = slab[24:25, :N_Q_HEADS]
    db_sp = slab[32:33].reshape(SGU_GROUPS, SGU_CHUNK)
    width = D_MODEL // N_CHIPS
    dln_g = lax.dynamic_slice(slab[40:41], (0, j * width), (1, width))
    dln_b = lax.dynamic_slice(slab[41:42], (0, j * width), (1, width))
    dw_sp = slab[48:48 + SGU_CHUNK].reshape(SGU_GROUPS * SGU_CHUNK, SGU_CHUNK)
    return norms, db_qkv, db_o, dsinks, db_sp, dln_g, dln_b, dw_sp, slab[42, 0]


class _GradReduce:
    def __init__(self, c_arr, jc_arr, dest_shapes):
        self.c_arr, self.jc_arr, self.dest_shapes = c_arr, jc_arr, dest_shapes
        self.grad, self.sibling, self.pair, self.chips, self.dest = {}, {}, {}, {}, {}

    def exchange(self, tags):
        return exchange_rider([self.grad[t] for t in tags])

    def exchanged(self, tags, res):
        for t, r in zip(tags, res):
            self.sibling[t] = r
            self.pair[t] = pair_add(self.grad[t], r, self.c_arr, name=f"pair_add_{t}")

    def scatter(self, tags):
        return scatter_rider([self.pair[t] for t in tags])

    def scattered(self, tags, res, where):
        for t, r in zip(tags, res):
            name, lead = where[t]
            self.dest[name] = final_add(self.grad[t], self.sibling[t], r, self.jc_arr, dest_shape=self.dest_shapes[name],
                                        lead=lead, prev=self.dest.get(name), name=f"final_add_{t}")

    def broadcast(self, items):
        names = []
        for n, _ in items:
            if n not in names:
                names.append(n)
        return names, broadcast_rider([self.dest[n] for n in names], [(names.index(n), lead) for n, lead in items])

    def broadcasted(self, names, res):
        for n, r in zip(names, res):
            self.dest[n] = r


def kernel(x, norm_mix_pre, norm_mix_post, norm_ffn_pre, norm_ffn_post, attn_w_qkv, attn_b_qkv, attn_sinks, attn_w_o, attn_b_o, sgu_w_in, sgu_ln_g, sgu_ln_b, sgu_w_spatial, sgu_b_spatial, sgu_w_out, ffn_w_gate_up, ffn_w_down, loss_target, m_norm_mix_pre, m_norm_mix_post, m_norm_ffn_pre, m_norm_ffn_post, m_attn_w_qkv, m_attn_b_qkv, m_attn_sinks, m_attn_w_o, m_attn_b_o, m_sgu_w_in, m_sgu_ln_g, m_sgu_ln_b, m_sgu_w_spatial, m_sgu_b_spatial, m_sgu_w_out, m_ffn_w_gate_up, m_ffn_w_down, v_norm_mix_pre, v_norm_mix_post, v_norm_ffn_pre, v_norm_ffn_post, v_attn_w_qkv, v_attn_b_qkv, v_attn_sinks, v_attn_w_o, v_attn_b_o, v_sgu_w_in, v_sgu_ln_g, v_sgu_ln_b, v_sgu_w_spatial, v_sgu_b_spatial, v_sgu_w_out, v_ffn_w_gate_up, v_ffn_w_down):
    s = x.shape[1]
    x0 = x.reshape(s, D_MODEL)
    target = loss_target.reshape(s, D_MODEL)
    mx, my, mc = lax.axis_index("x"), lax.axis_index("y"), lax.axis_index("c")
    chip = 2 * mx + my
    chip_arr = jnp.reshape(chip, (1,)).astype(I32)
    c_arr = jnp.reshape(mc, (1,)).astype(I32)
    jc_arr = jnp.stack([chip, mc]).astype(I32)
    zero_bias = jnp.zeros((1, D_MODEL), F32)

    def gain(p, i):
        return p[i:i + 1]

    big = [attn_w_qkv, attn_w_o, sgu_w_in, sgu_w_out, ffn_w_gate_up, ffn_w_gate_up, ffn_w_down, ffn_w_down]
    layers = [0, 0, 0, 0, 0, 1, 0, 1]
    tags = ["qkv", "wo", "win", "wout", "wgu0", "wgu1", "wd0", "wd1"]
    full = {t: place_shard(w, l, chip_arr, BF16, name=f"place_{t}") for w, l, t in zip(big, layers, tags)
            if t in ("qkv", "wo")}
    ln_pack = _pad_rows(jnp.concatenate([sgu_ln_g, sgu_ln_b], axis=0), 16)[None]
    full["ln"] = place_shard(ln_pack, 0, chip_arr, F32, name="place_ln")

    def split(items):
        return [i if isinstance(i, str) else i[0] for i in items], [WHOLE if isinstance(i, str) else tuple(i[1:]) for i in items]

    def ici(*items):
        names, pieces = split(items)
        return gather_ici_rider([full[n] for n in names], pieces)

    def d2d(*items):
        names, pieces = split(items)
        return gather_d2d_rider([full[n] for n in names], pieces)

    def landed(items, res):
        for n, r in zip(split(items)[0], res):
            full[n] = r

    cos, sin = _rope_tables(s)
    sink_rows = jnp.broadcast_to(
        jnp.repeat(attn_sinks.reshape(N_KV_HEADS, GQA_GROUP), WINDOW, axis=1)[:, None, :], (N_KV_HEADS, 8, ROWS))
    w_sp = sgu_w_spatial.reshape(SGU_GROUPS, SGU_CHUNK, SGU_CHUNK)
    b_sp = jnp.broadcast_to(sgu_b_spatial.reshape(SGU_GROUPS, SGU_CHUNK)[:, :, None], (SGU_GROUPS, SGU_CHUNK, LANES))

    (h0, full["wgu0"]), (res,) = prenorm_and_place(x0, gain(norm_mix_pre, 0), ffn_w_gate_up, 0, chip_arr, name="prenorm_0",
                                                   riders=[ici("qkv", "ln")])
    landed(("qkv", "ln"), res)
    full["wgu1"], (res,) = place_shard(ffn_w_gate_up, 1, chip_arr, BF16, name="place_wgu1", riders=[d2d("qkv", "ln")])
    landed(("qkv", "ln"), res)
    ln_g = full["ln"][:, 0, :].reshape(1, D_MODEL)
    ln_b = full["ln"][:, 1, :].reshape(1, D_MODEL)

    def hosted(call, stages):
        outputs, results = call([{"ici": ici, "d2d": d2d}[kind](*items) for kind, items in stages])
        for (_, items), res in zip(stages, results):
            landed(items, res)
        return outputs

    casts = [(ffn_w_down, 0), (sgu_w_in, 0), (ffn_w_down, 1), (sgu_w_out, 0)]
    qkv, full["wd0"], full["win"], full["wd1"], full["wout"] = hosted(
        lambda r: qkv_proj(h0, full["qkv"], attn_b_qkv, cos, sin, casts, chip_arr, name="qkv_proj", riders=r),
        [("ici", ("wo", ("wgu0", 0, 3, 8)))])
    o = hosted(lambda r: attn_fwd(qkv, sink_rows, name="attn_fwd", riders=r),
               [("d2d", ("wo",)), ("ici", (("wgu0", 3, 8, 8), ("wd0", 0, 2, 11), ("win", 0, 2, 8)))])
    w_o = full["wo"].reshape(Q_WIDTH, D_MODEL)
    x1, h1, m0 = hosted(lambda r: proj_residual_norm(o, w_o, x0, attn_b_o, gain(norm_mix_post, 0), gain(norm_ffn_pre, 0),
                                                     name="attn_out_norm", riders=r),
                        [("d2d", ("wgu0",)), ("ici", (("wd0", 2, 11, 11),))])
    gu0, a0 = hosted(lambda r: ffn_up(h1, full["wgu0"], name="ffn_up_0", riders=r),
                     [("d2d", ("wd0",)), ("ici", (("win", 2, 8, 8), "wout", ("wgu1", 0, 2, 8)))])
    w_d0 = full["wd0"].reshape(D_FF, D_MODEL)
    x2, h2, f0 = hosted(lambda r: proj_residual_norm(a0, w_d0, x1, zero_bias, gain(norm_ffn_post, 0), gain(norm_mix_pre, 1),
                                                     name="ffn_down_norm_0", riders=r),
                        [("d2d", ("win", "wout")), ("ici", (("wgu1", 2, 4, 8),))])
    w_in = full["win"]
    z, y = hosted(lambda r: sgu_in_fwd(h2, w_in, ln_g, ln_b, w_sp, b_sp, name="sgu_in_fwd", riders=r),
                  [("ici", (("wgu1", 4, 8, 8),))])
    w_out = full["wout"].reshape(D_MODEL, D_MODEL)
    x3, h3, m1 = hosted(lambda r: proj_residual_norm(y, w_out, x2, zero_bias, gain(norm_mix_post, 1), gain(norm_ffn_pre, 1),
                                                     name="sgu_out_norm", riders=r),
                        [("d2d", ("wgu1",))])
    w_qkv, w_gu0, w_gu1 = full["qkv"], full["wgu0"], full["wgu1"]
    gu1, a1 = hosted(lambda r: ffn_up(h3, w_gu1, name="ffn_up_1", riders=r), [("ici", ("wd1",))])
    (res,) = comm_call([d2d("wd1")], name="gather_wd1_sibling")
    landed(("wd1",), res)
    w_d1 = full["wd1"].reshape(D_FF, D_MODEL)
    dx4, df1, dg_fpost1, loss_part = proj_loss_head(a1, w_d1, x3, gain(norm_ffn_post, 1), target, name="ffn_down_loss")

    red = _GradReduce(c_arr, jc_arr, {
        "qkv": attn_w_qkv.shape[1:], "wo": attn_w_o.shape[1:], "win": sgu_w_in.shape[1:], "wout": sgu_w_out.shape[1:],
        "wgu": ffn_w_gate_up.shape, "wd": ffn_w_down.shape, "slab": (N_CHIPS, SLAB_ROWS // N_CHIPS, D_MODEL)})
    where = {"qkv": ("qkv", None), "wo": ("wo", None), "win": ("win", None), "wout": ("wout", None), "wgu0": ("wgu", 0),
             "wgu1": ("wgu", 1), "wd0": ("wd", 0), "wd1": ("wd", 1), "small": ("slab", "chip")}

    dgu1, dx3, dm1, dg_fpre1, dg_mpost1, _ = ffn_bwd_rows(
        df1, w_d1, gu1, w_gu1, dx4, x3, gain(norm_ffn_pre, 1), m1, gain(norm_mix_post, 1), name="ffn_bwd_rows_1")
    red.grad["wd1"] = mm_tn(a1, df1, shard_major=False, tm=256, tn=D_MODEL, name="dw_down_1").reshape(
        N_CHIPS, D_FF // N_CHIPS, D_MODEL)
    red.grad["wgu1"], (res,) = mm_tn(h3, dgu1, shard_major=True, tm=512, tn=FF_HALF, name="dw_gate_up_1",
                                     riders=[red.exchange(["wd1"])])
    red.exchanged(["wd1"], res)
    dy, (res,) = mm_nt(dm1, w_out, out_dtype=F32, name="dy_sgu", riders=[red.exchange(["wgu1"])])
    red.exchanged(["wgu1"], res)
    red.grad["wout"] = mm_tn(y, dm1, shard_major=False, tm=512, tn=D_MODEL, name="dw_sgu_out").reshape(
        N_CHIPS, D_MODEL // N_CHIPS, D_MODEL)
    (dz, dw_sp, db_sp, dln_g, dln_b), (res_a, res_b) = sgu_bwd(
        z, dy, ln_g, ln_b, w_sp, b_sp, name="sgu_bwd", riders=[red.scatter(["wd1"]), red.exchange(["wout"])])
    red.scattered(["wd1"], res_a, where)
    red.exchanged(["wout"], res_b)
    names, rider = red.broadcast([("wd", 1)])
    red.grad["win"], (res_a, res_b) = mm_tn(h2, dz, shard_major=True, tm=D_MODEL, tn=2 * D_MODEL // N_CHIPS, name="dw_sgu_in",
                                            riders=[rider, red.scatter(["wout"])])
    red.broadcasted(names, res_a)
    red.scattered(["wout"], res_b, where)
    names, rider = red.broadcast([("wout", None)])
    (dx2, df0, dg_mpre1, dg_fpost0, _), (res_a, res_b) = dh_norm_bwd_pair(
        dz, w_in, dx3, x2, gain(norm_mix_pre, 1), f0, gain(norm_ffn_post, 0), name="dh_sgu_norm",
        riders=[red.exchange(["win"]), rider])
    red.exchanged(["win"], res_a)
    red.broadcasted(names, res_b)
    (dgu0, dx1, dm0, dg_fpre0, dg_mpost0, db_o), (res,) = ffn_bwd_rows(
        df0, w_d0, gu0, w_gu0, dx2, x1, gain(norm_ffn_pre, 0), m0, gain(norm_mix_post, 0), name="ffn_bwd_rows_0",
        riders=[red.scatter(["wgu1", "win"])])
    red.scattered(["wgu1", "win"], res, where)
    names, rider = red.broadcast([("wgu", 1), ("win", None)])
    dw_d0, (res,) = mm_tn(a0, df0, shard_major=False, tm=256, tn=D_MODEL, name="dw_down_0", riders=[rider])
    red.broadcasted(names, res)
    red.grad["wd0"] = dw_d0.reshape(N_CHIPS, D_FF // N_CHIPS, D_MODEL)
    do, (res,) = mm_nt(dm0, w_o, out_dtype=BF16, name="do_attn", riders=[red.exchange(["wd0"])])
    red.exchanged(["wd0"], res)
    red.grad["wgu0"], (res,) = mm_tn(h1, dgu0, shard_major=True, tm=512, tn=FF_HALF, name="dw_gate_up_0",
                                     riders=[red.scatter(["wd0"])])
    red.scattered(["wd0"], res, where)
    names, rider = red.broadcast([("wd", 0)])
    dw_o, (res_a, res_b) = mm_tn(o, dm0, shard_major=False, tm=512, tn=D_MODEL, name="dw_attn_out",
                                 riders=[red.exchange(["wgu0"]), rider])
    red.exchanged(["wgu0"], res_a)
    red.broadcasted(names, res_b)
    red.grad["wo"] = dw_o.reshape(N_CHIPS, Q_WIDTH // N_CHIPS, D_MODEL)
    (dq, dkc, dkp, dvc, dvp, dsink), (res_a, res_b) = attn_bwd(
        qkv, sink_rows, do, name="attn_bwd", riders=[red.scatter(["wgu0"]), red.exchange(["wo"])])
    red.scattered(["wgu0"], res_a, where)
    red.exchanged(["wo"], res_b)
    names, rider = red.broadcast([("wgu", 0)])
    (dqkv, db_qkv), (res,) = rope_bwd(dq, dkc, dkp, dvc, dvp, cos, sin, name="rope_bwd", riders=[rider])
    red.broadcasted(names, res)
    red.grad["qkv"], (res,) = mm_tn(h0, dqkv, shard_major=True, tm=D_MODEL, tn=QKV_WIDTH // N_CHIPS, name="dw_qkv",
                                    riders=[red.scatter(["wo"])])
    red.scattered(["wo"], res, where)
    grad_x, dg_mpre0 = dh_norm_bwd_last(dqkv, w_qkv, dx1, x0, gain(norm_mix_pre, 0), name="dh_attn_norm_in")

    norm_grads = [jnp.concatenate(p, axis=0) for p in
                  ((dg_mpre0, dg_mpre1), (dg_mpost0, dg_mpost1), (dg_fpre0, dg_fpre1), (dg_fpost0, dg_fpost1))]
    red.grad["small"] = _pack_small(norm_grads, db_qkv, db_o, dsink[:, :, 0, 0], db_sp[:, :, 0], dln_g, dln_b, dw_sp,
                                    loss_part)
    def big_update(w, g, m, v, tag, after=None):
        return adamw(w, g.reshape(w.shape), m, v, name=f"adamw_{tag}", after=after)

    (res,) = comm_call([red.exchange(["qkv", "small"])], name="tail_1")
    red.exchanged(["qkv", "small"], res)
    state, token = comm_start([red.scatter(["qkv", "small"])], name="tail_2_start")
    upd_wgu = big_update(ffn_w_gate_up, red.dest["wgu"], m_ffn_w_gate_up, v_ffn_w_gate_up, "wgu", after=token)
    (res,) = comm_wait(state, upd_wgu[1], name="tail_2_wait")
    red.scattered(["qkv", "small"], res, where)
    names, rider = red.broadcast([("qkv", None), ("wo", None)])
    state, token = comm_start([rider, allcast_rider(red.dest["slab"])], name="tail_3_start")
    upd_wd = big_update(ffn_w_down, red.dest["wd"], m_ffn_w_down, v_ffn_w_down, "wd", after=token)
    res, (slab_full,) = comm_wait(state, upd_wd[1], name="tail_3_wait")
    red.broadcasted(names, res)
    g_qkv, g_wo, g_win, g_wout = (red.dest[n] for n in ("qkv", "wo", "win", "wout"))
    g_norms, g_bqkv, g_bo, g_sinks, g_bsp, g_lng, g_lnb, g_wsp, loss = _unpack_small(slab_full, chip)

    upd = {
        "attn_w_qkv": big_update(attn_w_qkv, g_qkv, m_attn_w_qkv, v_attn_w_qkv, "qkv"),
        "attn_w_o": big_update(attn_w_o, g_wo, m_attn_w_o, v_attn_w_o, "wo"),
        "sgu_w_in": big_update(sgu_w_in, g_win, m_sgu_w_in, v_sgu_w_in, "win"),
        "sgu_w_out": big_update(sgu_w_out, g_wout, m_sgu_w_out, v_sgu_w_out, "wout"),
        "ffn_w_gate_up": upd_wgu,
        "ffn_w_down": upd_wd,
    }
    small_names = ["norm_mix_pre", "norm_mix_post", "norm_ffn_pre", "norm_ffn_post", "attn_b_qkv", "attn_sinks", "attn_b_o",
                   "sgu_ln_g", "sgu_ln_b", "sgu_w_spatial", "sgu_b_spatial"]
    small_w = [norm_mix_pre, norm_mix_post, norm_ffn_pre, norm_ffn_post, attn_b_qkv, attn_sinks, attn_b_o, sgu_ln_g, sgu_ln_b,
               sgu_w_spatial, sgu_b_spatial]
    small_m = [m_norm_mix_pre, m_norm_mix_post, m_norm_ffn_pre, m_norm_ffn_post, m_attn_b_qkv, m_attn_sinks, m_attn_b_o,
               m_sgu_ln_g, m_sgu_ln_b, m_sgu_w_spatial, m_sgu_b_spatial]
    small_v = [v_norm_mix_pre, v_norm_mix_post, v_norm_ffn_pre, v_norm_ffn_post, v_attn_b_qkv, v_attn_sinks, v_attn_b_o,
               v_sgu_ln_g, v_sgu_ln_b, v_sgu_w_spatial, v_sgu_b_spatial]
    small_g = g_norms + [g_bqkv, g_sinks, g_bo, g_lng, g_lnb, g_wsp, g_bsp]

    def flat2(a):
        return a.reshape(-1, a.shape[-1])

    res = adamw_small([flat2(a) for a in small_w], [flat2(a) for a in small_g], [flat2(a) for a in small_m],
                      [flat2(a) for a in small_v], name="adamw_small")
    for i, nm in enumerate(small_names):
        upd[nm] = tuple(r[i].reshape(small_w[i].shape) for r in res)

    order = ["norm_mix_pre", "norm_mix_post", "norm_ffn_pre", "norm_ffn_post", "attn_w_qkv", "attn_b_qkv", "attn_sinks",
             "attn_w_o", "attn_b_o", "sgu_w_in", "sgu_ln_g", "sgu_ln_b", "sgu_w_spatial", "sgu_b_spatial", "sgu_w_out",
             "ffn_w_gate_up", "ffn_w_down"]
    outs = [loss, grad_x.reshape(1, s, D_MODEL)]
    for part in range(4):
        outs += [upd[nm][part] for nm in order]
    return tuple(outs)
```

```python
import types

import numpy as np
import jax
import jax.numpy as jnp
from jax import lax
from jax.experimental import pallas as pl
from jax.experimental.pallas import tpu as pltpu

F32 = jnp.float32
BF16 = jnp.bfloat16
I32 = jnp.int32

D_MODEL = 1024
HEAD_DIM = 64
N_Q_HEADS = 16
N_KV_HEADS = 4
GQA_GROUP = 4
WINDOW = 128
Q_WIDTH = 1024
KV_WIDTH = 256
QKV_WIDTH = 1536
ROPE_THETA = 10000.0
SGU_GROUPS = 8
SGU_CHUNK = 128
D_FF = 2816
FF_HALF = D_FF // 2
EPS = 1e-6
N_CHIPS = 4
LANES = 128
BF16_SUBLANES = 16

ADAM_LR = 0.001
ADAM_B1 = 0.9
ADAM_B2 = 0.999
ADAM_EPS = 1e-08
ADAM_WD = 0.01
ADAM_STEP = 10

VMEM_LIMIT = 52 * 1024 * 1024
BIG_VMEM_LIMIT = 62 * 1024 * 1024
SUB_ROWS = 256
MESH = pl.DeviceIdType.MESH
NEG = -1e30
NT_DIMS = (((1,), (1,)), ((), ()))
TN_DIMS = (((0,), (0,)), ((), ()))
NN_DIMS = (((1,), (0,)), ((), ()))
ANY = pl.BlockSpec(memory_space=pl.ANY)


def _row_tile(s, want):
    return want if s % want == 0 else s


PEER_KINDS = ("sibling", "chips", "sibling+chips", "everyone")


def _peer_kind(riders):
    kinds = {r.peers for r in riders}
    if not kinds:
        return None
    if "everyone" in kinds:
        return "everyone"
    return "sibling+chips" if len(kinds) == 2 else kinds.pop()


def _peer_barrier(kind):
    x, y, c = _place()
    chips = [(*_partner(x, y, k), c) for k in (1, 2, 3)]
    peers = {"sibling": [(x, y, 1 - c)], "chips": chips, "sibling+chips": [(x, y, 1 - c)] + chips,
             "everyone": [(x, y, 1 - c)] + chips + [(px, py, 1 - c) for px, py, _ in chips]}[kind]
    barrier = pltpu.get_barrier_semaphore()
    for dev in peers:
        pl.semaphore_signal(barrier, inc=1, device_id=dev, device_id_type=MESH)
    pl.semaphore_wait(barrier, len(peers))


def _call(body, *, name, grid=(), in_specs=(), out_specs=(), out_shape=(), scratch_shapes=(), operands=(), prefetch=(),
          aliases=None, riders=(), sem=None, vmem_limit=VMEM_LIMIT):
    n_pre, n_in, n_out, n_scr = len(prefetch), len(operands), len(out_shape), len(scratch_shapes)
    in_specs, out_specs, out_shape = list(in_specs), list(out_specs), list(out_shape)
    operands, scratch_shapes = list(operands), list(scratch_shapes)
    io_alias = {n_pre + i: o for i, o in (aliases or {}).items()}
    for r in riders:
        base_in, base_out = n_pre + len(operands), len(out_shape)
        operands += list(r.inputs)
        in_specs += [ANY] * len(r.inputs)
        for pos, i in enumerate(r.aliased):
            io_alias[base_in + i] = base_out + pos
            out_shape.append(jax.ShapeDtypeStruct(r.inputs[i].shape, r.inputs[i].dtype))
        out_shape += list(r.fresh)
        out_specs += [ANY] * (len(r.aliased) + len(r.fresh))
        scratch_shapes += [pltpu.SemaphoreType.DMA((r.nsem,)), pltpu.SemaphoreType.DMA((r.nsem,))]

    def wrapped(*refs):
        pre, p = refs[:n_pre], n_pre
        core_in, p = refs[p:p + n_in], p + n_in
        r_in = []
        for r in riders:
            r_in.append(refs[p:p + len(r.inputs)])
            p += len(r.inputs)
        core_out, p = refs[p:p + n_out], p + n_out
        r_out = []
        for r in riders:
            k = len(r.aliased) + len(r.fresh)
            r_out.append(refs[p:p + k])
            p += k
        core_scr, p = refs[p:p + n_scr], p + n_scr
        r_sem = [refs[p + 2 * i:p + 2 * i + 2] for i in range(len(riders))]

        def edge(at_last, fns):
            def run():
                if not at_last:
                    _peer_barrier(peer_kind)
                for i, r in enumerate(riders):
                    getattr(r, fns)(r_in[i], r_out[i], r_sem[i][0], r_sem[i][1])
            if not riders:
                return
            if not grid:
                run()
                return
            cond = None
            for d, n in enumerate(grid):
                c = pl.program_id(d) == (n - 1 if at_last else 0)
                cond = c if cond is None else jnp.logical_and(cond, c)
            pl.when(cond)(run)

        edge(False, "start")
        if body is not None:
            body(*pre, *core_in, *core_out, *core_scr)
        edge(True, "finish")

    if sem is None or riders:
        sem = ("arbitrary",) * len(grid)
    kwargs = dict(out_shape=out_shape, input_output_aliases=io_alias, name=name)
    peer_kind = _peer_kind(riders)
    collective = {} if peer_kind is None else {"collective_id": PEER_KINDS.index(peer_kind)}
    if grid:
        kwargs["compiler_params"] = pltpu.CompilerParams(dimension_semantics=sem, vmem_limit_bytes=vmem_limit, **collective)
    elif collective:
        kwargs["compiler_params"] = pltpu.CompilerParams(**collective)
    if n_pre:
        kwargs["grid_spec"] = pltpu.PrefetchScalarGridSpec(
            num_scalar_prefetch=n_pre, grid=grid, in_specs=in_specs, out_specs=out_specs, scratch_shapes=scratch_shapes)
    else:
        kwargs.update(grid=grid, in_specs=in_specs, out_specs=out_specs, scratch_shapes=scratch_shapes)
    res = pl.pallas_call(wrapped, **kwargs)(*prefetch, *operands)
    core, rest, rider_res = list(res[:n_out]), list(res[n_out:]), []
    for r in riders:
        k = len(r.aliased) + len(r.fresh)
        rider_res.append(rest[:k])
        rest = rest[k:]
    return core, rider_res


def _mm_call(*, grid, in_specs, out_spec, out_shape, dims, nk, kaxis, acc_shape, name, operands, riders=()):
    out_dtype = out_shape.dtype

    def body(a_ref, b_ref, o_ref, *scratch):
        p = lax.dot_general(a_ref[...].astype(BF16), b_ref[...].astype(BF16), dims, preferred_element_type=F32)
        if nk == 1:
            o_ref[...] = p.astype(out_dtype)
        else:
            acc = scratch[0]
            kk = pl.program_id(kaxis)

            @pl.when(kk == 0)
            def _():
                acc[...] = p

            @pl.when(kk > 0)
            def _():
                acc[...] += p

            @pl.when(kk == nk - 1)
            def _():
                o_ref[...] = acc[...].astype(out_dtype)

    sem = ["parallel"] * len(grid)
    if nk > 1:
        sem[kaxis] = "arbitrary"
    (out,), rider_res = _call(
        body, grid=grid, in_specs=in_specs, out_specs=[out_spec], out_shape=[out_shape],
        scratch_shapes=[pltpu.VMEM(acc_shape, F32)] if nk > 1 else [], operands=operands, name=name, riders=riders,
        sem=tuple(sem))
    return (out, rider_res) if riders else out


def mm_nt(a, w, *, out_dtype, name, tm=1024, riders=()):
    m, n = a.shape
    kout = w.shape[0]
    tm = _row_tile(m, tm)
    return _mm_call(grid=(m // tm,),
                    in_specs=[pl.BlockSpec((tm, n), lambda i: (i, 0)), pl.BlockSpec((kout, n), lambda i: (0, 0))],
                    out_spec=pl.BlockSpec((tm, kout), lambda i: (i, 0)),
                    out_shape=jax.ShapeDtypeStruct((m, kout), out_dtype), dims=NT_DIMS, nk=1, kaxis=0,
                    acc_shape=None, name=name, operands=(a, w), riders=riders)


def mm_tn(a, b, *, shard_major, name, tm, tn, tk=None, out_dtype=BF16, riders=()):
    s, m = a.shape
    tk = s if tk is None else _row_tile(s, tk)
    if b.ndim == 3:
        n = 2 * b.shape[2]
        b_spec = pl.BlockSpec((None, tk, tn), lambda j, i, kk: (j // 2, kk, j % 2))
    else:
        n = b.shape[1]
        b_spec = pl.BlockSpec((tk, tn), lambda j, i, kk: (kk, j))
    if shard_major:
        assert tn == n // N_CHIPS
        o_spec = pl.BlockSpec((None, tm, tn), lambda j, i, kk: (j, i, 0))
        o_shape = jax.ShapeDtypeStruct((N_CHIPS, m, tn), out_dtype)
    else:
        o_spec = pl.BlockSpec((tm, tn), lambda j, i, kk: (i, j))
        o_shape = jax.ShapeDtypeStruct((m, n), out_dtype)
    return _mm_call(grid=(n // tn, m // tm, s // tk),
                    in_specs=[pl.BlockSpec((tk, tm), lambda j, i, kk: (kk, i)), b_spec], out_spec=o_spec,
                    out_shape=o_shape, dims=TN_DIMS, nk=s // tk, kaxis=2, acc_shape=(tm, tn), name=name, operands=(a, b),
                    riders=riders)


def _rstd(x):
    return lax.rsqrt(jnp.mean(x * x, axis=-1, keepdims=True) + EPS)


def _rms_bwd(dy, x, g):
    r = _rstd(x)
    xhat = x * r
    gy = dy * g
    dx = r * (gy - xhat * jnp.mean(gy * xhat, axis=-1, keepdims=True))
    return dx, jnp.sum(dy * xhat, axis=0, keepdims=True)


def _accum(ref, val, first):
    @pl.when(first)
    def _():
        ref[...] = val

    @pl.when(jnp.logical_not(first))
    def _():
        ref[...] += val


def _row_spec(tm, width):
    return pl.BlockSpec((tm, width), lambda i: (i, 0))


def _vec_spec(width):
    return pl.BlockSpec((1, width), lambda i: (0, 0))


def _ret(core, rider_res, riders):
    core = core[0] if len(core) == 1 else core
    return (core, rider_res) if riders else core


def prenorm_and_place(x, g, w, layer, chip_arr, *, name, tm=256, riders=()):
    s = x.shape[0]
    tm = _row_tile(s, tm)
    steps = s // tm
    _, r, c = w.shape
    tr = r // steps
    assert tr * steps == r and tr % 16 == 0

    def body(chip_ref, x_ref, g_ref, w_ref, h_ref, o_ref):
        xv = x_ref[...]
        h_ref[...] = (xv * _rstd(xv) * g_ref[...]).astype(BF16)
        o_ref[...] = w_ref[...].astype(BF16)

    core, rr = _call(
        body, grid=(steps,), prefetch=(chip_arr,),
        in_specs=[pl.BlockSpec((tm, D_MODEL), lambda i, chip: (i, 0)), pl.BlockSpec((1, D_MODEL), lambda i, chip: (0, 0)),
                  pl.BlockSpec((None, tr, c), lambda i, chip: (layer, i, 0))],
        out_specs=[pl.BlockSpec((tm, D_MODEL), lambda i, chip: (i, 0)), pl.BlockSpec((None, tr, c), lambda i, chip: (chip[0], i, 0))],
        out_shape=[jax.ShapeDtypeStruct((s, D_MODEL), BF16), jax.ShapeDtypeStruct((N_CHIPS, r, c), BF16)],
        operands=(x, g, w), sem=("parallel",), name=name, riders=riders)
    return _ret(core, rr, riders)


def proj_residual_norm(a, w, x, bias, g_post, g_next, *, name, tm=512, sub=256, riders=()):
    s, k = a.shape
    tm = _row_tile(s, tm)
    sub = min(sub, tm)

    def body(a_ref, w_ref, x_ref, b_ref, gp_ref, gn_ref, xo_ref, h_ref, m_ref):
        for t in range(tm // sub):
            rows = slice(t * sub, (t + 1) * sub)
            mv = jnp.dot(a_ref[rows, :], w_ref[...], preferred_element_type=F32) + b_ref[...]
            m_ref[rows, :] = mv.astype(BF16)
            xn = x_ref[rows, :] + mv * _rstd(mv) * gp_ref[...]
            xo_ref[rows, :] = xn
            h_ref[rows, :] = (xn * _rstd(xn) * gn_ref[...]).astype(BF16)

    row, vec = _row_spec(tm, D_MODEL), _vec_spec(D_MODEL)
    core, rr = _call(
        body, grid=(s // tm,),
        in_specs=[_row_spec(tm, k), pl.BlockSpec((k, D_MODEL), lambda i: (0, 0)), row, vec, vec, vec], out_specs=[row, row, row],
        out_shape=[jax.ShapeDtypeStruct((s, D_MODEL), F32), jax.ShapeDtypeStruct((s, D_MODEL), BF16),
                   jax.ShapeDtypeStruct((s, D_MODEL), BF16)],
        operands=(a, w, x, bias, g_post, g_next), sem=("parallel",), name=name, riders=riders)
    return _ret(core, rr, riders)


def ffn_fwd_loss_rows(h, w_gu, w_d, x, g_post, target, *, name, tm=512, riders=()):
    s = x.shape[0]
    tm = _row_tile(s, tm)

    def body(h_ref, w0, w1, w2, w3, wd_ref, x_ref, g_ref, t_ref, d_ref, a_ref, dx_ref, df_ref, dg_ref, loss_ref):
        first = pl.program_id(0) == 0
        halves = [slice(half * FF_HALF, (half + 1) * FF_HALF) for half in (0, 1)]
        gain = g_ref[...]
        sub = min(SUB_ROWS, tm)
        sums = None
        for t in range(tm // sub):
            rows = slice(t * sub, (t + 1) * sub)
            hv = h_ref[rows, :]
            fv = None
            for cols, (wg_ref, wu_ref) in zip(halves, ((w0, w2), (w1, w3))):
                g = jnp.dot(hv, wg_ref[...], preferred_element_type=F32)
                u = jnp.dot(hv, wu_ref[...], preferred_element_type=F32)
                sig = _sigmoid(g)
                silu = g * sig
                d_ref[0, rows, cols] = (u * (sig + silu * (1.0 - sig))).astype(BF16)
                d_ref[1, rows, cols] = silu.astype(BF16)
                act = (silu * u).astype(BF16)
                a_ref[rows, cols] = act
                p = jnp.dot(act, wd_ref[cols, :], preferred_element_type=F32)
                fv = p if fv is None else fv + p
            err = x_ref[rows, :] + fv * _rstd(fv) * gain - t_ref[rows, :]
            dx = err * (1.0 / D_MODEL)
            dx_ref[rows, :] = dx
            df, dg = _rms_bwd(dx, fv, gain)
            df_ref[rows, :] = df.astype(BF16)
            part = (dg, jnp.sum(jnp.sum(err * err, axis=-1, keepdims=True), axis=0, keepdims=True) * (0.5 / D_MODEL))
            sums = part if sums is None else tuple(a + b for a, b in zip(sums, part))
        _accum(dg_ref, sums[0], first)
        _accum(loss_ref, jnp.broadcast_to(sums[1], (8, LANES)), first)

    def resident(shape, index):
        return pl.BlockSpec(shape, index, pipeline_mode=pl.Buffered(1))

    row, vec = _row_spec(tm, D_MODEL), _vec_spec(D_MODEL)
    shards = [resident((None, D_MODEL, FF_HALF), (lambda j: (lambda i: (j, 0, 0)))(j)) for j in range(N_CHIPS)]
    core, rr = _call(
        body, grid=(s // tm,),
        in_specs=[row] + shards + [resident((D_FF, D_MODEL), lambda i: (0, 0)), row, vec, row],
        out_specs=[pl.BlockSpec((2, tm, D_FF), lambda i: (0, i, 0)), _row_spec(tm, D_FF), row, row, vec,
                   pl.BlockSpec((8, LANES), lambda i: (0, 0))],
        out_shape=[jax.ShapeDtypeStruct((2, s, D_FF), BF16), jax.ShapeDtypeStruct((s, D_FF), BF16),
                   jax.ShapeDtypeStruct((s, D_MODEL), F32), jax.ShapeDtypeStruct((s, D_MODEL), BF16),
                   jax.ShapeDtypeStruct((1, D_MODEL), F32), jax.ShapeDtypeStruct((8, LANES), F32)],
        operands=(h, w_gu, w_gu, w_gu, w_gu, w_d, x, g_post, target), name=name, riders=riders, vmem_limit=BIG_VMEM_LIMIT)
    return _ret(core, rr, riders)


def dh_norm_bwd_pair(a, w, dres, x, g_pre, m, g_post, *, name, tm=512, sub=256, riders=()):
    _, kout, ns = w.shape
    planes = a.ndim == 3
    s = x.shape[0]
    tm = _row_tile(s, tm)
    sub = min(sub, tm)
    a_spec = pl.BlockSpec((2, tm, 2 * ns), lambda i: (0, i, 0)) if planes else pl.BlockSpec((tm, N_CHIPS * ns), lambda i: (i, 0))

    def body(a_ref, w0, w1, w2, w3, dres_ref, x_ref, gpre_ref, m_ref, gpost_ref, dx_ref, dm_ref, dgpre_ref, dgpost_ref, db_ref):
        first = pl.program_id(0) == 0
        sums = None
        for t in range(tm // sub):
            rows = slice(t * sub, (t + 1) * sub)
            dh = None
            for j, w_ref in enumerate((w0, w1, w2, w3)):
                a_j = a_ref[j // 2, rows, (j % 2) * ns:(j % 2 + 1) * ns] if planes else a_ref[rows, j * ns:(j + 1) * ns]
                p = lax.dot_general(a_j, w_ref[...], NT_DIMS, preferred_element_type=F32)
                dh = p if dh is None else dh + p
            d1, dgpre = _rms_bwd(dh, x_ref[rows, :], gpre_ref[...])
            dx = dres_ref[rows, :] + d1
            dx_ref[rows, :] = dx
            dm, dgpost = _rms_bwd(dx, m_ref[rows, :].astype(F32), gpost_ref[...])
            dm_ref[rows, :] = dm.astype(BF16)
            part = (dgpre, dgpost, jnp.sum(dm, axis=0, keepdims=True))
            sums = part if sums is None else tuple(u + v for u, v in zip(sums, part))
        _accum(dgpre_ref, sums[0], first)
        _accum(dgpost_ref, sums[1], first)
        _accum(db_ref, sums[2], first)

    def shard(j):
        return pl.BlockSpec((None, kout, ns), lambda i: (j, 0, 0))

    row, vec = _row_spec(tm, D_MODEL), _vec_spec(D_MODEL)
    vshape = jax.ShapeDtypeStruct((1, D_MODEL), F32)
    core, rr = _call(
        body, grid=(s // tm,), in_specs=[a_spec] + [shard(j) for j in range(N_CHIPS)] + [row, row, vec, row, vec],
        out_specs=[row, row, vec, vec, vec],
        out_shape=[jax.ShapeDtypeStruct((s, D_MODEL), F32), jax.ShapeDtypeStruct((s, D_MODEL), BF16), vshape, vshape, vshape],
        operands=(a, w, w, w, w, dres, x, g_pre, m, g_post), name=name, riders=riders)
    return _ret(core, rr, riders)


def ffn_bwd_rows(df, w_d, d_planes, w_gu, dres, x, g_pre, m, g_post, *, name, tm=512, riders=()):
    s = x.shape[0]
    tm = _row_tile(s, tm)

    def body(df_ref, wd_ref, d_ref, w0, w1, w2, w3, dres_ref, x_ref, gpre_ref, m_ref, gpost_ref,
             o_ref, dx_ref, dm_ref, dgpre_ref, dgpost_ref, db_ref):
        first = pl.program_id(0) == 0
        halves = [slice(half * FF_HALF, (half + 1) * FF_HALF) for half in (0, 1)]
        sub = min(SUB_ROWS, tm)
        sums = None
        for t in range(tm // sub):
            rows = slice(t * sub, (t + 1) * sub)
            dfv = df_ref[rows, :]
            dh = None
            for cols, (wg_ref, wu_ref) in zip(halves, ((w0, w2), (w1, w3))):
                da = lax.dot_general(dfv, wd_ref[cols, :], NT_DIMS, preferred_element_type=F32)
                dg = (da * d_ref[0, rows, cols].astype(F32)).astype(BF16)
                du = (da * d_ref[1, rows, cols].astype(F32)).astype(BF16)
                o_ref[0, rows, cols] = dg
                o_ref[1, rows, cols] = du
                p = lax.dot_general(dg, wg_ref[...], NT_DIMS, preferred_element_type=F32)
                p += lax.dot_general(du, wu_ref[...], NT_DIMS, preferred_element_type=F32)
                dh = p if dh is None else dh + p
            d1, dgpre = _rms_bwd(dh, x_ref[rows, :], gpre_ref[...])
            dx = dres_ref[rows, :] + d1
            dx_ref[rows, :] = dx
            dm, dgpost = _rms_bwd(dx, m_ref[rows, :].astype(F32), gpost_ref[...])
            dm_ref[rows, :] = dm.astype(BF16)
            part = (dgpre, dgpost, jnp.sum(dm, axis=0, keepdims=True))
            sums = part if sums is None else tuple(a + b for a, b in zip(sums, part))
        _accum(dgpre_ref, sums[0], first)
        _accum(dgpost_ref, sums[1], first)
        _accum(db_ref, sums[2], first)

    def resident(shape, index):
        return pl.BlockSpec(shape, index, pipeline_mode=pl.Buffered(1))

    planes = pl.BlockSpec((2, tm, D_FF), lambda i: (0, i, 0))
    row, vec = _row_spec(tm, D_MODEL), _vec_spec(D_MODEL)
    vshape = jax.ShapeDtypeStruct((1, D_MODEL), F32)
    shards = [resident((None, D_MODEL, FF_HALF), (lambda j: (lambda i: (j, 0, 0)))(j)) for j in range(N_CHIPS)]
    core, rr = _call(
        body, grid=(s // tm,),
        in_specs=[row, resident((D_FF, D_MODEL), lambda i: (0, 0)), planes] + shards + [row, row, vec, row, vec],
        out_specs=[planes, row, row, vec, vec, vec],
        out_shape=[jax.ShapeDtypeStruct((2, s, D_FF), BF16), jax.ShapeDtypeStruct((s, D_MODEL), F32),
                   jax.ShapeDtypeStruct((s, D_MODEL), BF16), vshape, vshape, vshape],
        operands=(df, w_d, d_planes, w_gu, w_gu, w_gu, w_gu, dres, x, g_pre, m, g_post), name=name, riders=riders,
        vmem_limit=BIG_VMEM_LIMIT)
    return _ret(core, rr, riders)


def dh_norm_bwd_last(a, w, dres, x, g_pre, *, name, tm=512, sub=256):
    _, kout, ns = w.shape
    s = x.shape[0]
    tm = _row_tile(s, tm)
    sub = min(sub, tm)

    def body(a_ref, w0, w1, w2, w3, dres_ref, x_ref, g_ref, dx_ref, dg_ref):
        total = None
        for t in range(tm // sub):
            rows = slice(t * sub, (t + 1) * sub)
            dh = None
            for j, w_ref in enumerate((w0, w1, w2, w3)):
                p = lax.dot_general(a_ref[rows, j * ns:(j + 1) * ns], w_ref[...], NT_DIMS, preferred_element_type=F32)
                dh = p if dh is None else dh + p
            d1, dg = _rms_bwd(dh, x_ref[rows, :], g_ref[...])
            dx_ref[rows, :] = dres_ref[rows, :] + d1
            total = dg if total is None else total + dg
        _accum(dg_ref, total, pl.program_id(0) == 0)

    def shard(j):
        return pl.BlockSpec((None, kout, ns), lambda i: (j, 0, 0))

    row, vec = _row_spec(tm, D_MODEL), _vec_spec(D_MODEL)
    (dx, dg), _ = _call(
        body, grid=(s // tm,), in_specs=[_row_spec(tm, N_CHIPS * ns)] + [shard(j) for j in range(N_CHIPS)] + [row, row, vec],
        out_specs=[row, vec], out_shape=[jax.ShapeDtypeStruct((s, D_MODEL), F32), jax.ShapeDtypeStruct((1, D_MODEL), F32)],
        operands=(a, w, w, w, w, dres, x, g_pre), name=name)
    return dx, dg


def _rope_tables(s):
    half = HEAD_DIM // 2
    inv_freq = np.float32(ROPE_THETA) ** (-(np.arange(half, dtype=np.float32) * np.float32(2.0)) / np.float32(HEAD_DIM))
    ang = np.arange(s, dtype=np.float32)[:, None] * inv_freq[None, :]
    cos, sin = np.cos(ang).astype(np.float32), np.sin(ang).astype(np.float32)
    return jnp.asarray(np.tile(cos, (1, 4))), jnp.asarray(np.concatenate([-sin, sin, -sin, sin], axis=1))


def _swap_halves(x):
    lane = lax.broadcasted_iota(I32, x.shape, 1)
    return jnp.where((lane & (HEAD_DIM - 1)) < HEAD_DIM // 2, pltpu.roll(x, LANES - 32, 1), pltpu.roll(x, 32, 1))


N_ROPE_BLOCKS = (Q_WIDTH + KV_WIDTH) // LANES


def qkv_proj(h, w, bias, cos, sin, casts, chip_arr, *, name, tm=1024, riders=()):
    s, k = h.shape
    ns = w.shape[2]
    tm = _row_tile(s, tm)
    nc = len(casts)

    def body(chip_ref, h_ref, w_ref, b_ref, c_ref, s_ref, *rest):
        cast_in, o_ref, cast_out = rest[:nc], rest[nc], rest[nc + 1:]
        j = pl.program_id(0)

        @pl.when(jnp.logical_and(j == 0, pl.program_id(1) == 0))
        def _():
            for src, dst in zip(cast_in, cast_out):
                dst[...] = src[...].astype(BF16)

        sub = min(256, tm)
        for t in range(tm // sub):
            rows = slice(t * sub, (t + 1) * sub)
            p = jnp.dot(h_ref[rows, :], w_ref[...], preferred_element_type=F32) + b_ref[...]
            cosv, sinv = c_ref[rows, :], s_ref[rows, :]
            for blk in range(ns // LANES):
                xb = p[:, blk * LANES:(blk + 1) * LANES]
                roped = xb * cosv + _swap_halves(xb) * sinv
                is_qk = j * (ns // LANES) + blk < N_ROPE_BLOCKS
                o_ref[rows, blk * LANES:(blk + 1) * LANES] = jnp.where(is_qk, roped, xb).astype(BF16)

    def cast_in_spec(wt, layer):
        return pl.BlockSpec((None,) + wt.shape[1:], lambda j, i, chip: (layer, 0, 0), pipeline_mode=pl.Buffered(1))

    def cast_out_spec(wt):
        return pl.BlockSpec((None,) + wt.shape[1:], lambda j, i, chip: (chip[0], 0, 0), pipeline_mode=pl.Buffered(1))

    core, rr = _call(
        body, grid=(N_CHIPS, s // tm), prefetch=(chip_arr,),
        in_specs=[pl.BlockSpec((tm, k), lambda j, i, chip: (i, 0)), pl.BlockSpec((None, k, ns), lambda j, i, chip: (j, 0, 0)),
                  pl.BlockSpec((1, ns), lambda j, i, chip: (0, j)), pl.BlockSpec((tm, LANES), lambda j, i, chip: (i, 0)),
                  pl.BlockSpec((tm, LANES), lambda j, i, chip: (i, 0))] + [cast_in_spec(wt, layer) for wt, layer in casts],
        out_specs=[pl.BlockSpec((tm, ns), lambda j, i, chip: (i, j))] + [cast_out_spec(wt) for wt, _ in casts],
        out_shape=[jax.ShapeDtypeStruct((s, N_CHIPS * ns), BF16)]
        + [jax.ShapeDtypeStruct((N_CHIPS,) + wt.shape[1:], BF16) for wt, _ in casts],
        operands=(h, w, bias, cos, sin, *[wt for wt, _ in casts]), sem=("arbitrary", "arbitrary"), name=name, riders=riders)
    return (core, rr) if riders else core


def rope_bwd(dq, dkc, dkp, dvc, dvp, cos, sin, *, name, riders=()):
    s = dq.shape[0]
    tm = 2 * WINDOW if s % (2 * WINDOW) == 0 else WINDOW
    nb = s // tm

    def body(dq_ref, dkc_ref, dkp_ref, dkp_next_ref, dvc_ref, dvp_ref, dvp_next_ref, c_ref, s_ref, o_ref, db_ref):
        i = pl.program_id(0)
        has_next = (i < nb - 1).astype(F32)
        cosv, sinv = c_ref[...], s_ref[...]

        def shifted(ref, next_ref, cols):
            last = has_next * next_ref[:WINDOW, cols].astype(F32)
            return last if tm == WINDOW else jnp.concatenate([ref[WINDOW:, cols].astype(F32), last], axis=0)

        parts = []
        for blk in range(QKV_WIDTH // LANES):
            if blk < Q_WIDTH // LANES:
                g = dq_ref[:, blk * LANES:(blk + 1) * LANES].astype(F32)
            else:
                own, prv, nxt = (dkc_ref, dkp_ref, dkp_next_ref) if blk < N_ROPE_BLOCKS else (dvc_ref, dvp_ref, dvp_next_ref)
                cols = slice((blk % 2) * LANES, (blk % 2 + 1) * LANES)
                g = own[:, cols].astype(F32) + shifted(prv, nxt, cols)
            if blk < N_ROPE_BLOCKS:
                g = g * cosv + _swap_halves(g * sinv)
            o_ref[:, blk * LANES:(blk + 1) * LANES] = g.astype(BF16)
            parts.append(jnp.sum(g, axis=0, keepdims=True))
        sums = jnp.concatenate(parts, axis=1)
        _accum(db_ref, sums, i == 0)

    own_spec = _row_spec(tm, KV_WIDTH)
    next_spec = pl.BlockSpec((tm, KV_WIDTH), lambda i: (jnp.minimum(i + 1, nb - 1), 0))
    core, rr = _call(
        body, grid=(nb,),
        in_specs=[_row_spec(tm, Q_WIDTH), own_spec, own_spec, next_spec, own_spec, own_spec, next_spec,
                  _row_spec(tm, LANES), _row_spec(tm, LANES)],
        out_specs=[_row_spec(tm, QKV_WIDTH), _vec_spec(QKV_WIDTH)],
        out_shape=[jax.ShapeDtypeStruct((s, QKV_WIDTH), BF16), jax.ShapeDtypeStruct((1, QKV_WIDTH), F32)],
        operands=(dq, dkc, dkp, dkp, dvc, dvp, dvp, cos, sin), name=name, riders=riders)
    return _ret(core, rr, riders)


ROWS = GQA_GROUP * WINDOW


def _prev_slots():
    kpos = lax.broadcasted_iota(I32, (WINDOW, ROWS), 0)
    qpos = lax.broadcasted_iota(I32, (WINDOW, ROWS), 1) & (WINDOW - 1)
    return kpos > qpos


def _head_cols(ref, head):
    return ref[:, head * HEAD_DIM:(head + 1) * HEAD_DIM]


def _stack_heads(ref, h):
    return jnp.concatenate([_head_cols(ref, GQA_GROUP * h + g) for g in range(GQA_GROUP)], axis=0)


def _band(prev_ref, cur_ref, h):
    return jnp.concatenate([_head_cols(prev_ref, h), _head_cols(cur_ref, h)], axis=0)


def _pick(prev, band):
    return jnp.where(prev, band[:WINDOW], band[WINDOW:])


def _spread(prev, x):
    return jnp.concatenate([jnp.where(prev, x, 0.0), jnp.where(prev, 0.0, x)], axis=0).astype(BF16)


def _attn_probs(s_band, sink, prev, has_prev):
    scale = HEAD_DIM ** -0.5
    s = jnp.where(prev, jnp.where(has_prev, s_band[:WINDOW], NEG), s_band[WINDOW:]) * scale
    m = jnp.maximum(jnp.max(s, axis=0, keepdims=True), sink)
    e, es = jnp.exp(s - m), jnp.exp(sink - m)
    inv = 1.0 / (jnp.sum(e, axis=0, keepdims=True) + es)
    return e * inv, es * inv


def _attn_specs(nb):
    kcol, vcol = Q_WIDTH // KV_WIDTH, Q_WIDTH // KV_WIDTH + 1
    q_spec = pl.BlockSpec((WINDOW, Q_WIDTH), lambda n: (n, 0))
    return [q_spec,
            pl.BlockSpec((WINDOW, KV_WIDTH), lambda n: (n, kcol)),
            pl.BlockSpec((WINDOW, KV_WIDTH), lambda n: (jnp.maximum(n - 1, 0), kcol)),
            pl.BlockSpec((WINDOW, KV_WIDTH), lambda n: (n, vcol)),
            pl.BlockSpec((WINDOW, KV_WIDTH), lambda n: (jnp.maximum(n - 1, 0), vcol)),
            pl.BlockSpec((N_KV_HEADS, 8, ROWS), lambda n: (0, 0, 0))]


def attn_fwd(qkv, sink_rows, *, name, riders=()):
    s = qkv.shape[0]

    def body(q_ref, kc_ref, kp_ref, vc_ref, vp_ref, sink_ref, o_ref):
        prev = _prev_slots()
        has_prev = pl.program_id(0) > 0
        heads = range(N_KV_HEADS)
        s_bands = [lax.dot_general(_band(kp_ref, kc_ref, h), _stack_heads(q_ref, h), NT_DIMS, preferred_element_type=F32)
                   for h in heads]
        p_bands = [_spread(prev, _attn_probs(s_bands[h], sink_ref[h, 0:1, :], prev, has_prev)[0]) for h in heads]
        outs = [lax.dot_general(_band(vp_ref, vc_ref, h), p_bands[h], TN_DIMS, preferred_element_type=F32).T for h in heads]
        for h in heads:
            for g in range(GQA_GROUP):
                head = GQA_GROUP * h + g
                o_ref[:, head * HEAD_DIM:(head + 1) * HEAD_DIM] = outs[h][g * WINDOW:(g + 1) * WINDOW].astype(BF16)

    core, rr = _call(
        body, grid=(s // WINDOW,), in_specs=_attn_specs(s // WINDOW), out_specs=[pl.BlockSpec((WINDOW, Q_WIDTH), lambda n: (n, 0))],
        out_shape=[jax.ShapeDtypeStruct((s, Q_WIDTH), BF16)], operands=(qkv, qkv, qkv, qkv, qkv, sink_rows), sem=("parallel",),
        name=name, riders=riders)
    return _ret(core, rr, riders)


def attn_bwd(qkv, sink_rows, do, *, name, riders=()):
    s = qkv.shape[0]

    def body(q_ref, kc_ref, kp_ref, vc_ref, vp_ref, sink_ref, do_ref, dq_ref, dkc_ref, dkp_ref, dvc_ref, dvp_ref, dsink_ref):
        n = pl.program_id(0)
        prev = _prev_slots()
        scale = HEAD_DIM ** -0.5
        heads = range(N_KV_HEADS)
        qs, dos = [_stack_heads(q_ref, h) for h in heads], [_stack_heads(do_ref, h) for h in heads]
        kbands, vbands = [_band(kp_ref, kc_ref, h) for h in heads], [_band(vp_ref, vc_ref, h) for h in heads]
        s_bands = [lax.dot_general(kbands[h], qs[h], NT_DIMS, preferred_element_type=F32) for h in heads]
        dp_bands = [lax.dot_general(vbands[h], dos[h], NT_DIMS, preferred_element_type=F32) for h in heads]
        ds_bands, p_bands, parts = [], [], []
        for h in heads:
            p, ps = _attn_probs(s_bands[h], sink_ref[h, 0:1, :], prev, n > 0)
            dp = _pick(prev, dp_bands[h])
            delta = jnp.sum(p * dp, axis=0, keepdims=True)
            ds_bands.append(_spread(prev, p * (dp - delta) * scale))
            p_bands.append(_spread(prev, p))
            dsink = -(ps * delta)
            for g in range(GQA_GROUP):
                parts.append(jnp.broadcast_to(jnp.sum(dsink[:, g * WINDOW:(g + 1) * WINDOW], axis=1, keepdims=True), (8, LANES)))
        for h in heads:
            dk = jnp.dot(ds_bands[h], qs[h], preferred_element_type=F32).astype(BF16)
            dv = jnp.dot(p_bands[h], dos[h], preferred_element_type=F32).astype(BF16)
            dq = lax.dot_general(kbands[h], ds_bands[h], TN_DIMS, preferred_element_type=F32).T
            cols = slice(h * HEAD_DIM, (h + 1) * HEAD_DIM)
            dkp_ref[:, cols], dkc_ref[:, cols] = dk[:WINDOW], dk[WINDOW:]
            dvp_ref[:, cols], dvc_ref[:, cols] = dv[:WINDOW], dv[WINDOW:]
            for g in range(GQA_GROUP):
                head = GQA_GROUP * h + g
                dq_ref[:, head * HEAD_DIM:(head + 1) * HEAD_DIM] = dq[g * WINDOW:(g + 1) * WINDOW].astype(BF16)

        @pl.when(n == 0)
        def _():
            for i, part in enumerate(parts):
                dsink_ref[i // GQA_GROUP, i % GQA_GROUP] = part

        @pl.when(n > 0)
        def _():
            for i, part in enumerate(parts):
                dsink_ref[i // GQA_GROUP, i % GQA_GROUP] += part

    rows_q = pl.BlockSpec((WINDOW, Q_WIDTH), lambda n: (n, 0))
    rows_kv = pl.BlockSpec((WINDOW, KV_WIDTH), lambda n: (n, 0))
    kv_shape = jax.ShapeDtypeStruct((s, KV_WIDTH), BF16)
    core, rr = _call(
        body, grid=(s // WINDOW,), in_specs=_attn_specs(s // WINDOW) + [rows_q],
        out_specs=[rows_q, rows_kv, rows_kv, rows_kv, rows_kv,
                   pl.BlockSpec((N_KV_HEADS, GQA_GROUP, 8, LANES), lambda n: (0, 0, 0, 0))],
        out_shape=[jax.ShapeDtypeStruct((s, Q_WIDTH), BF16), kv_shape, kv_shape, kv_shape, kv_shape,
                   jax.ShapeDtypeStruct((N_KV_HEADS, GQA_GROUP, 8, LANES), F32)],
        operands=(qkv, qkv, qkv, qkv, qkv, sink_rows, do), sem=("arbitrary",), name=name, riders=riders)
    return _ret(core, rr, riders)


GELU_C = 0.7978845608028654
GELU_A = 0.044715


def _gelu(x):
    return 0.5 * x * (1.0 + jnp.tanh(x * (GELU_C + (GELU_C * GELU_A) * (x * x))))


def _gelu_and_grad(x):
    x2 = x * x
    t = jnp.tanh(x * (GELU_C + (GELU_C * GELU_A) * x2))
    half_x, one_t = 0.5 * x, 1.0 + t
    return half_x * one_t, 0.5 * one_t + half_x * (1.0 - t * t) * (GELU_C + (3.0 * GELU_C * GELU_A) * x2)


def _tril_bf16(w):
    row = lax.broadcasted_iota(I32, (SGU_CHUNK, SGU_CHUNK), 0)
    col = lax.broadcasted_iota(I32, (SGU_CHUNK, SGU_CHUNK), 1)
    return jnp.where(row >= col, w, 0.0).astype(BF16)


def _sgu_norm(vg, g, b):
    mu = jnp.mean(vg, axis=-1, keepdims=True)
    cen = vg - mu
    rstd = lax.rsqrt(jnp.mean(cen * cen, axis=-1, keepdims=True) + EPS)
    xhat = cen * rstd
    return xhat, rstd, xhat * g + b


def sgu_in_fwd(h, w_in, ln_g, ln_b, w_sp, b_sp, *, name, tm=512, riders=()):
    s, k = h.shape
    ns = w_in.shape[2]
    tm = _row_tile(s, tm)

    def body(h_ref, w0, w1, w2, w3, g_ref, b_ref, w_ref, bs_ref, z_ref, y_ref):
        hv = h_ref[...]
        zs = [jnp.dot(hv, w_ref_j[...], preferred_element_type=F32) for w_ref_j in (w0, w1, w2, w3)]
        for j, zj in enumerate(zs):
            z_ref[:, j * ns:(j + 1) * ns] = zj.astype(BF16)
        u = _gelu(jnp.concatenate(zs[:2], axis=1))
        _, _, vn = _sgu_norm(_gelu(jnp.concatenate(zs[2:], axis=1)), g_ref[...], b_ref[...])
        vn = vn.astype(BF16)
        for grp in range(SGU_GROUPS):
            w = _tril_bf16(w_ref[grp])
            cols = slice(grp * LANES, (grp + 1) * LANES)
            for ch in range(tm // SGU_CHUNK):
                rows = slice(ch * SGU_CHUNK, (ch + 1) * SGU_CHUNK)
                mixed = jnp.dot(w, vn[rows, cols], preferred_element_type=F32) + bs_ref[grp]
                y_ref[rows, cols] = (u[rows, cols] * mixed).astype(BF16)

    def shard(j):
        return pl.BlockSpec((None, k, ns), lambda i: (j, 0, 0))

    full3 = pl.BlockSpec((SGU_GROUPS, SGU_CHUNK, SGU_CHUNK), lambda i: (0, 0, 0))
    core, rr = _call(
        body, grid=(s // tm,),
        in_specs=[_row_spec(tm, k)] + [shard(j) for j in range(N_CHIPS)] + [_vec_spec(D_MODEL), _vec_spec(D_MODEL), full3, full3],
        out_specs=[_row_spec(tm, 2 * D_MODEL), _row_spec(tm, D_MODEL)],
        out_shape=[jax.ShapeDtypeStruct((s, 2 * D_MODEL), BF16), jax.ShapeDtypeStruct((s, D_MODEL), BF16)],
        operands=(h, w_in, w_in, w_in, w_in, ln_g, ln_b, w_sp, b_sp), sem=("parallel",), name=name, riders=riders)
    return _ret(core, rr, riders)


def sgu_bwd(z, dy, ln_g, ln_b, w_sp, b_sp, *, name, tm=256, riders=()):
    s = z.shape[0]
    tm = _row_tile(s, tm)

    def body(z_ref, dy_ref, g_ref, b_ref, w_ref, bs_ref, dz_ref, dw_ref, dbs_ref, dg_ref, db_ref, dvn_buf):
        first = pl.program_id(0) == 0
        u, u_grad = _gelu_and_grad(z_ref[:, :D_MODEL].astype(F32))
        vg, v_grad = _gelu_and_grad(z_ref[:, D_MODEL:].astype(F32))
        xhat, rstd, vn = _sgu_norm(vg, g_ref[...], b_ref[...])
        vn = vn.astype(BF16)
        dyv = dy_ref[...]
        dmixed = dyv * u
        dz_gate = dyv * u_grad
        row = lax.broadcasted_iota(I32, (SGU_CHUNK, SGU_CHUNK), 0)
        col = lax.broadcasted_iota(I32, (SGU_CHUNK, SGU_CHUNK), 1)
        dws, dbss = [], []
        for grp in range(SGU_GROUPS):
            w = _tril_bf16(w_ref[grp])
            cols = slice(grp * LANES, (grp + 1) * LANES)
            dw = jnp.zeros((SGU_CHUNK, SGU_CHUNK), F32)
            dbs = jnp.zeros((SGU_CHUNK, 1), F32)
            for ch in range(tm // SGU_CHUNK):
                rows = slice(ch * SGU_CHUNK, (ch + 1) * SGU_CHUNK)
                vblk = vn[rows, cols]
                mixed = jnp.dot(w, vblk, preferred_element_type=F32) + bs_ref[grp]
                dz_ref[rows, cols] = (dz_gate[rows, cols] * mixed).astype(BF16)
                dm = dmixed[rows, cols]
                dmb = dm.astype(BF16)
                dvn_buf[rows, cols] = lax.dot_general(w, dmb, TN_DIMS, preferred_element_type=F32)
                dw += lax.dot_general(dmb, vblk, NT_DIMS, preferred_element_type=F32)
                dbs += jnp.sum(dm, axis=-1, keepdims=True)
            dws.append(jnp.where(row >= col, dw, 0.0))
            dbss.append(jnp.broadcast_to(dbs, (SGU_CHUNK, SGU_CHUNK)))

        dvn = dvn_buf[...]
        dxhat = dvn * g_ref[...]
        dvg = rstd * (dxhat - jnp.mean(dxhat, axis=-1, keepdims=True) - xhat * jnp.mean(dxhat * xhat, axis=-1, keepdims=True))
        dz_ref[:, D_MODEL:] = (dvg * v_grad).astype(BF16)
        dlng, dlnb = jnp.sum(dvn * xhat, axis=0, keepdims=True), jnp.sum(dvn, axis=0, keepdims=True)

        @pl.when(first)
        def _():
            for grp in range(SGU_GROUPS):
                dw_ref[grp] = dws[grp]
                dbs_ref[grp] = dbss[grp]
            dg_ref[...] = dlng
            db_ref[...] = dlnb

        @pl.when(jnp.logical_not(first))
        def _():
            for grp in range(SGU_GROUPS):
                dw_ref[grp] += dws[grp]
                dbs_ref[grp] += dbss[grp]
            dg_ref[...] += dlng
            db_ref[...] += dlnb

    full3 = pl.BlockSpec((SGU_GROUPS, SGU_CHUNK, SGU_CHUNK), lambda i: (0, 0, 0))
    s3 = jax.ShapeDtypeStruct((SGU_GROUPS, SGU_CHUNK, SGU_CHUNK), F32)
    vshape = jax.ShapeDtypeStruct((1, D_MODEL), F32)
    core, rr = _call(
        body, grid=(s // tm,),
        in_specs=[_row_spec(tm, 2 * D_MODEL), _row_spec(tm, D_MODEL), _vec_spec(D_MODEL), _vec_spec(D_MODEL), full3, full3],
        out_specs=[_row_spec(tm, 2 * D_MODEL), full3, full3, _vec_spec(D_MODEL), _vec_spec(D_MODEL)],
        out_shape=[jax.ShapeDtypeStruct((s, 2 * D_MODEL), BF16), s3, s3, vshape, vshape],
        scratch_shapes=[pltpu.VMEM((tm, D_MODEL), F32)], operands=(z, dy, ln_g, ln_b, w_sp, b_sp), name=name, riders=riders)
    return _ret(core, rr, riders)


def _sigmoid(x):
    return 1.0 / (1.0 + jnp.exp(-x))


def ffn_up(h, w_gu, *, name, tm=512, riders=()):
    s = h.shape[0]
    tm = _row_tile(s, tm)

    def body(h_ref, wg_ref, wu_ref, d_ref, a_ref):
        hv = h_ref[...]
        sub = min(256, tm)
        for t in range(tm // sub):
            rows = slice(t * sub, (t + 1) * sub)
            g = jnp.dot(hv[rows], wg_ref[...], preferred_element_type=F32)
            u = jnp.dot(hv[rows], wu_ref[...], preferred_element_type=F32)
            sig = _sigmoid(g)
            silu = g * sig
            d_ref[0, rows, :] = (u * (sig + silu * (1.0 - sig))).astype(BF16)
            d_ref[1, rows, :] = silu.astype(BF16)
            a_ref[rows, :] = (silu * u).astype(BF16)

    core, rr = _call(
        body, grid=(2, s // tm),
        in_specs=[pl.BlockSpec((tm, D_MODEL), lambda j, i: (i, 0)),
                  pl.BlockSpec((None, D_MODEL, FF_HALF), lambda j, i: (j, 0, 0)),
                  pl.BlockSpec((None, D_MODEL, FF_HALF), lambda j, i: (j + 2, 0, 0))],
        out_specs=[pl.BlockSpec((2, tm, FF_HALF), lambda j, i: (0, i, j)), pl.BlockSpec((tm, FF_HALF), lambda j, i: (i, j))],
        out_shape=[jax.ShapeDtypeStruct((2, s, D_FF), BF16), jax.ShapeDtypeStruct((s, D_FF), BF16)],
        operands=(h, w_gu, w_gu), sem=("parallel", "parallel"), name=name, riders=riders)
    return _ret(core, rr, riders)


def _weight_tile(rows):
    for tr in (512, 352, 256, 128):
        if rows % tr == 0:
            return tr
    return rows


def place_shard(w, layer, chip_arr, dtype, *, name, riders=()):
    _, r, c = w.shape
    tr = _weight_tile(r)

    def body(chip_ref, w_ref, o_ref):
        o_ref[...] = w_ref[...].astype(dtype)

    core, rr = _call(
        body, grid=(r // tr,), prefetch=(chip_arr,),
        in_specs=[pl.BlockSpec((None, tr, c), lambda i, chip: (layer, i, 0))],
        out_specs=[pl.BlockSpec((None, tr, c), lambda i, chip: (chip[0], i, 0))],
        out_shape=[jax.ShapeDtypeStruct((N_CHIPS, r, c), dtype)], operands=(w,), sem=("parallel",), name=name, riders=riders)
    return _ret(core, rr, riders)


def _adamw_math(w, g, m, v):
    m = ADAM_B1 * m + (1.0 - ADAM_B1) * g
    v = ADAM_B2 * v + (1.0 - ADAM_B2) * (g * g)
    m_hat = m / (1.0 - ADAM_B1 ** ADAM_STEP)
    v_hat = v / (1.0 - ADAM_B2 ** ADAM_STEP)
    delta = -ADAM_LR * (m_hat / (jnp.sqrt(v_hat) + ADAM_EPS) + ADAM_WD * w)
    return delta, m, v


def adamw(w, g, m, v, *, name, after=None):
    nl, r, c = w.shape
    tr = _weight_tile(r)

    def body(w_ref, g_ref, m_ref, v_ref, *rest):
        go_ref, d_ref, mo_ref, vo_ref = rest[-4:]
        gv = g_ref[...]
        go_ref[...] = gv
        d_ref[...], mo_ref[...], vo_ref[...] = _adamw_math(w_ref[...], gv, m_ref[...], v_ref[...])

    spec = pl.BlockSpec((None, tr, c), lambda l, i: (l, i, 0))
    shape = jax.ShapeDtypeStruct(w.shape, F32)
    extra = [] if after is None else [after]
    outs, _ = _call(body, grid=(nl, r // tr), in_specs=[spec] * 4 + [ANY] * len(extra), out_specs=[spec] * 4,
                    out_shape=[shape] * 4, operands=(w, g, m, v, *extra), sem=("parallel", "parallel"), name=name)
    return outs


def adamw_small(ws, gs, ms, vs, *, name):
    n = len(ws)

    def body(*refs):
        ins, outs = refs[:4 * n], refs[4 * n:]
        for t in range(n):
            gv = ins[n + t][...]
            outs[t][...] = gv
            outs[n + t][...], outs[2 * n + t][...], outs[3 * n + t][...] = _adamw_math(
                ins[t][...], gv, ins[2 * n + t][...], ins[3 * n + t][...])

    shapes = [jax.ShapeDtypeStruct(w.shape, F32) for w in ws]
    res = pl.pallas_call(body, out_shape=shapes * 4, name=name)(*ws, *gs, *ms, *vs)
    return res[:n], res[n:2 * n], res[2 * n:3 * n], res[3 * n:]


def pair_add(g, r1, c_arr, *, name):
    _, rows, cdim = g.shape
    h = rows // 2

    def body(c_ref, g_ref, r_ref, o_ref):
        o_ref[...] = (g_ref[...].astype(F32) + r_ref[...].astype(F32)).astype(o_ref.dtype)

    (out,), _ = _call(
        body, grid=(N_CHIPS,), prefetch=(c_arr,),
        in_specs=[pl.BlockSpec((None, h, cdim), lambda s, c: (s, c[0], 0)), pl.BlockSpec((None, h, cdim), lambda s, c: (s, 0, 0))],
        out_specs=[pl.BlockSpec((None, h, cdim), lambda s, c: (s, 0, 0))],
        out_shape=[jax.ShapeDtypeStruct((N_CHIPS, h, cdim), g.dtype)], operands=(g, r1), sem=("parallel",), name=name)
    return out


def final_add(g, r1, r2, jc_arr, *, dest_shape, lead, prev, name):
    _, rows, cdim = g.shape
    h = rows // 2
    n = next((k for k in (4, 2) if h % (k * BF16_SUBLANES) == 0), 1)
    hc = h // n

    def body(jc_ref, g_ref, r1_ref, r2_ref, *rest):
        o_ref = rest[-1]
        acc = g_ref[...].astype(F32) + r1_ref[...].astype(F32)
        for k in range(3):
            acc = acc + r2_ref[k].astype(F32)
        o_ref[...] = acc

    if lead is None:
        o_spec = pl.BlockSpec((hc, cdim), lambda i, jc: (jc[1] * n + i, 0))
    elif lead == "chip":
        o_spec = pl.BlockSpec((None, hc, cdim), lambda i, jc: (jc[0], jc[1] * n + i, 0))
    else:
        o_spec = pl.BlockSpec((None, hc, cdim), lambda i, jc: (lead, jc[1] * n + i, 0))
    in_specs = [pl.BlockSpec((None, hc, cdim), lambda i, jc: (jc[0], jc[1] * n + i, 0)),
                pl.BlockSpec((None, hc, cdim), lambda i, jc: (jc[0], i, 0)),
                pl.BlockSpec((3, hc, cdim), lambda i, jc: (0, i, 0))]
    operands = [g, r1, r2]
    aliases = None
    if prev is not None:
        in_specs.append(ANY)
        operands.append(prev)
        aliases = {3: 0}
    (out,), _ = _call(body, grid=(n,), prefetch=(jc_arr,), in_specs=in_specs, out_specs=[o_spec],
                      out_shape=[jax.ShapeDtypeStruct(dest_shape, F32)], operands=operands, aliases=aliases, name=name)
    return out


def _place():
    return lax.axis_index("x"), lax.axis_index("y"), lax.axis_index("c")


def _partner(x, y, k):
    return (1 - x if k >> 1 else x), (1 - y if k & 1 else y)


WHOLE = (0, 1, 1)


def _half(rows, sel, dtype, piece=WHOLE):
    lo, hi, n = piece
    align = 16 if dtype == BF16 else 8
    step = rows // 2 // n
    assert rows // 2 == step * n and step % align == 0
    return pl.ds(pl.multiple_of(sel * (rows // 2) + lo * step, align), (hi - lo) * step)


def _rider(peers, inputs, aliased, fresh, nsem, copies, arrivals):
    def start(ins, outs, send, recv):
        for cp in copies(ins, outs, send, recv):
            cp.start()

    def finish(ins, outs, send, recv):
        for cp in arrivals(ins, outs, send, recv):
            cp.wait_recv()
        for cp in copies(ins, outs, send, recv):
            cp.wait_send()

    return types.SimpleNamespace(peers=peers, inputs=list(inputs), aliased=list(aliased), fresh=list(fresh), nsem=nsem,
                                 start=start, finish=finish)


def _remote(src, dst, send, recv, idx, dev):
    return pltpu.make_async_remote_copy(src_ref=src, dst_ref=dst, send_sem=send.at[idx], recv_sem=recv.at[idx],
                                        device_id=dev, device_id_type=MESH)


def gather_ici_rider(fulls, pieces=None):
    nt = len(fulls)
    pieces = pieces or [WHOLE] * nt

    def region(outs, t, slot, sel):
        return outs[t].at[slot, _half(fulls[t].shape[1], sel, fulls[t].dtype, pieces[t])]

    def copies(ins, outs, send, recv):
        x, y, c = _place()
        res = []
        for t in range(nt):
            for k in (1, 2, 3):
                px, py = _partner(x, y, k)
                mine = region(outs, t, 2 * x + y, c)
                res.append(_remote(mine, mine, send, recv, 3 * t + k - 1, (px, py, c)))
        return res

    def arrivals(ins, outs, send, recv):
        x, y, c = _place()
        res = []
        for t in range(nt):
            for k in (1, 2, 3):
                px, py = _partner(x, y, k)
                theirs = region(outs, t, 2 * px + py, c)
                res.append(_remote(theirs, theirs, send, recv, 3 * t + k - 1, (x, y, c)))
        return res

    return _rider("chips", fulls, range(nt), [], 3 * nt, copies, arrivals)


def gather_d2d_rider(fulls, pieces=None):
    nt = len(fulls)
    pieces = pieces or [WHOLE] * nt

    def region(outs, t, slot, sel):
        return outs[t].at[slot, _half(fulls[t].shape[1], sel, fulls[t].dtype, pieces[t])]

    def both(outs, send, recv, mine):
        x, y, c = _place()
        res = []
        for t in range(nt):
            for k in (1, 2, 3):
                px, py = _partner(x, y, k)
                part = region(outs, t, 2 * px + py, c if mine else 1 - c)
                res.append(_remote(part, part, send, recv, 3 * t + k - 1, (x, y, 1 - c)))
        return res

    return _rider("sibling", fulls, range(nt), [], 3 * nt, lambda i, o, s, r: both(o, s, r, True),
                  lambda i, o, s, r: both(o, s, r, False))


def exchange_rider(grads):
    nt = len(grads)

    def both(ins, outs, send, recv):
        x, y, c = _place()
        return [_remote(ins[t].at[:, _half(grads[t].shape[1], 1 - c, grads[t].dtype)], outs[t], send, recv, t, (x, y, 1 - c))
                for t in range(nt)]

    fresh = [jax.ShapeDtypeStruct((N_CHIPS, g.shape[1] // 2, g.shape[2]), g.dtype) for g in grads]
    return _rider("sibling", grads, [], fresh, nt, both, both)


def scatter_rider(parts):
    nt = len(parts)

    def both(ins, outs, send, recv):
        x, y, c = _place()
        res = []
        for t in range(nt):
            for k in (1, 2, 3):
                px, py = _partner(x, y, k)
                res.append(_remote(ins[t].at[2 * px + py], outs[t].at[k - 1], send, recv, 3 * t + k - 1, (px, py, c)))
        return res

    fresh = [jax.ShapeDtypeStruct((3,) + p.shape[1:], p.dtype) for p in parts]
    return _rider("chips", parts, [], fresh, 3 * nt, both, both)


def broadcast_rider(bufs, items):
    def region(outs, item, sel):
        bi, lead = item
        ref = outs[bi]
        if lead == "chip":
            x, y, _ = _place()
            ref = ref.at[2 * x + y]
        elif lead is not None:
            ref = ref.at[lead]
        return ref.at[_half(ref.shape[0], sel, F32)]

    def both(outs, send, recv, mine):
        x, y, c = _place()
        res = []
        for i, item in enumerate(items):
            part = region(outs, item, c if mine else 1 - c)
            res.append(_remote(part, part, send, recv, i, (x, y, 1 - c)))
        return res

    return _rider("sibling", bufs, range(len(bufs)), [], len(items), lambda i, o, s, r: both(o, s, r, True),
                  lambda i, o, s, r: both(o, s, r, False))


def allcast_rider(buf):
    peers = [(k, flip) for k in range(N_CHIPS) for flip in (0, 1) if (k, flip) != (0, 0)]

    def both(outs, send, recv, mine):
        x, y, c = _place()
        res = []
        for i, (k, flip) in enumerate(peers):
            px, py = _partner(x, y, k)
            pc = 1 - c if flip else c
            slot, sel = (2 * x + y, c) if mine else (2 * px + py, pc)
            part = outs[0].at[slot, _half(buf.shape[1], sel, F32)]
            res.append(_remote(part, part, send, recv, i, (px, py, pc)))
        return res

    return _rider("everyone", [buf], [0], [], len(peers), lambda i, o, s, r: both(o, s, r, True),
                  lambda i, o, s, r: both(o, s, r, False))


def comm_call(riders, *, name):
    _, res = _call(None, riders=riders, name=name)
    return res


SEMS = pl.BlockSpec(memory_space=pltpu.SEMAPHORE)
SIDE_EFFECT = pltpu.SideEffectType.DATAFLOW_SIDE_EFFECTING


def _split_refs(riders, refs):
    views, p = [], 0
    for r in riders:
        bufs = refs[p:p + len(r.inputs) + len(r.fresh)]
        p += len(bufs)
        ins = bufs[:len(r.inputs)]
        views.append([ins, [ins[i] for i in r.aliased] + list(bufs[len(r.inputs):])])
    for view in views:
        view += [refs[p], refs[p + 1]]
        p += 2
    return views


def comm_start(riders, *, name):
    kind = _peer_kind(riders)
    bufs = [a for r in riders for a in r.inputs]
    fresh = [f for r in riders for f in r.fresh]
    n_buf, n_fresh = len(bufs), len(fresh)

    def body(*refs):
        ins, outs = refs[:n_buf], refs[n_buf:]
        through, land, sems = outs[:n_buf], outs[n_buf:n_buf + n_fresh], outs[n_buf + n_fresh:-1]
        _peer_barrier(kind)
        per_rider, pb, pf = [], 0, 0
        for r in riders:
            per_rider += list(through[pb:pb + len(r.inputs)]) + list(land[pf:pf + len(r.fresh)])
            pb, pf = pb + len(r.inputs), pf + len(r.fresh)
        for r, (r_ins, r_outs, send, recv) in zip(riders, _split_refs(riders, per_rider + list(sems))):
            r.start(r_ins, r_outs, send, recv)
        outs[-1][...] = jnp.zeros((8, LANES), F32)

    sem_shapes = [pltpu.SemaphoreType.DMA((r.nsem,)) for r in riders for _ in (0, 1)]
    res = pl.pallas_call(
        body, name=name, in_specs=[ANY] * n_buf,
        out_specs=[ANY] * (n_buf + n_fresh) + [SEMS] * len(sem_shapes) + [pl.BlockSpec(memory_space=pltpu.VMEM)],
        out_shape=[jax.ShapeDtypeStruct(a.shape, a.dtype) for a in bufs] + fresh + sem_shapes
        + [jax.ShapeDtypeStruct((8, LANES), F32)],
        input_output_aliases={i: i for i in range(n_buf)},
        compiler_params=pltpu.CompilerParams(has_side_effects=SIDE_EFFECT, collective_id=PEER_KINDS.index(kind)))(*bufs)
    return (riders, list(res[:n_buf + n_fresh]), list(res[n_buf + n_fresh:-1])), res[-1]


def comm_wait(state, after, *, name):
    riders, bufs, sems = state
    n_buf, n_sem = len(bufs), len(sems)
    n_in = sum(len(r.inputs) for r in riders)

    def body(*refs):
        held, sem_refs = refs[:n_buf], refs[n_buf:n_buf + n_sem]
        through, land = held[:n_in], held[n_in:]
        per_rider, pb, pf = [], 0, 0
        for r in riders:
            per_rider += list(through[pb:pb + len(r.inputs)]) + list(land[pf:pf + len(r.fresh)])
            pb, pf = pb + len(r.inputs), pf + len(r.fresh)
        for r, (r_ins, r_outs, send, recv) in zip(riders, _split_refs(riders, per_rider + list(sem_refs))):
            r.finish(r_ins, r_outs, send, recv)

    res = pl.pallas_call(
        body, name=name, in_specs=[ANY] * n_buf + [SEMS] * n_sem + [ANY], out_specs=[ANY] * n_buf,
        out_shape=[jax.ShapeDtypeStruct(a.shape, a.dtype) for a in bufs],
        input_output_aliases={i: i for i in range(n_buf)},
        compiler_params=pltpu.CompilerParams(has_side_effects=SIDE_EFFECT))(*bufs, *sems, after)
    through, land = list(res[:n_in]), list(res[n_in:])
    out, pb, pf = [], 0, 0
    for r in riders:
        r_ins, r_land = through[pb:pb + len(r.inputs)], land[pf:pf + len(r.fresh)]
        pb, pf = pb + len(r.inputs), pf + len(r.fresh)
        out.append([r_ins[i] for i in r.aliased] + r_land)
    return out


SLAB_ROWS = 192


def _pad_rows(a, rows=8):
    return jnp.pad(a, ((0, rows - a.shape[0]), (0, 0)))


def _pack_small(norm_grads, db_qkv, db_o, dsinks, db_sp, dln_g, dln_b, dw_sp, loss_part):
    parts = [
        jnp.concatenate(norm_grads, axis=0),
        _pad_rows(jnp.pad(db_qkv, ((0, 0), (0, 2 * D_MODEL - QKV_WIDTH))).reshape(2, D_MODEL)),
        _pad_rows(db_o),
        _pad_rows(jnp.pad(dsinks.reshape(1, N_Q_HEADS), ((0, 0), (0, D_MODEL - N_Q_HEADS)))),
        _pad_rows(db_sp.reshape(1, D_MODEL)),
        _pad_rows(jnp.concatenate([dln_g, dln_b, jnp.pad(loss_part[0:1], ((0, 0), (0, D_MODEL - LANES)))], axis=0)),
        dw_sp.reshape(SGU_CHUNK, D_MODEL),
    ]
    slab = jnp.concatenate(parts, axis=0)
    return jnp.pad(slab, ((0, SLAB_ROWS - slab.shape[0]), (0, 0))).reshape(N_CHIPS, SLAB_ROWS // N_CHIPS, D_MODEL)


def _unpack_small(slab, j):
    slab = slab.reshape(SLAB_ROWS, D_MODEL)
    norms = [slab[2 * i:2 * i + 2] for i in range(4)]
    db_qkv = slab[8:10].reshape(1, 2 * D_MODEL)[:, :QKV_WIDTH]
    db_o = slab[16:17]
    dsinks = slab[24:25, :N_Q_HEADS]
    db_sp = slab[32:33].reshape(SGU_GROUPS, SGU_CHUNK)
    width = D_MODEL // N_CHIPS
    dln_g = lax.dynamic_slice(slab[40:41], (0, j * width), (1, width))
    dln_b = lax.dynamic_slice(slab[41:42], (0, j * width), (1, width))
    dw_sp = slab[48:48 + SGU_CHUNK].reshape(SGU_GROUPS * SGU_CHUNK, SGU_CHUNK)
    return norms, db_qkv, db_o, dsinks, db_sp, dln_g, dln_b, dw_sp, slab[42, 0]


class _GradReduce:
    def __init__(self, c_arr, jc_arr, dest_shapes):
        self.c_arr, self.jc_arr, self.dest_shapes = c_arr, jc_arr, dest_shapes
        self.grad, self.sibling, self.pair, self.chips, self.dest = {}, {}, {}, {}, {}

    def exchange(self, tags):
        return exchange_rider([self.grad[t] for t in tags])

    def exchanged(self, tags, res):
        for t, r in zip(tags, res):
            self.sibling[t] = r
            self.pair[t] = pair_add(self.grad[t], r, self.c_arr, name=f"pair_add_{t}")

    def scatter(self, tags):
        return scatter_rider([self.pair[t] for t in tags])

    def scattered(self, tags, res, where):
        for t, r in zip(tags, res):
            name, lead = where[t]
            self.dest[name] = final_add(self.grad[t], self.sibling[t], r, self.jc_arr, dest_shape=self.dest_shapes[name],
                                        lead=lead, prev=self.dest.get(name), name=f"final_add_{t}")

    def broadcast(self, items):
        names = []
        for n, _ in items:
            if n not in names:
                names.append(n)
        return names, broadcast_rider([self.dest[n] for n in names], [(names.index(n), lead) for n, lead in items])

    def broadcasted(self, names, res):
        for n, r in zip(names, res):
            self.dest[n] = r


def kernel(x, norm_mix_pre, norm_mix_post, norm_ffn_pre, norm_ffn_post, attn_w_qkv, attn_b_qkv, attn_sinks, attn_w_o, attn_b_o, sgu_w_in, sgu_ln_g, sgu_ln_b, sgu_w_spatial, sgu_b_spatial, sgu_w_out, ffn_w_gate_up, ffn_w_down, loss_target, m_norm_mix_pre, m_norm_mix_post, m_norm_ffn_pre, m_norm_ffn_post, m_attn_w_qkv, m_attn_b_qkv, m_attn_sinks, m_attn_w_o, m_attn_b_o, m_sgu_w_in, m_sgu_ln_g, m_sgu_ln_b, m_sgu_w_spatial, m_sgu_b_spatial, m_sgu_w_out, m_ffn_w_gate_up, m_ffn_w_down, v_norm_mix_pre, v_norm_mix_post, v_norm_ffn_pre, v_norm_ffn_post, v_attn_w_qkv, v_attn_b_qkv, v_attn_sinks, v_attn_w_o, v_attn_b_o, v_sgu_w_in, v_sgu_ln_g, v_sgu_ln_b, v_sgu_w_spatial, v_sgu_b_spatial, v_sgu_w_out, v_ffn_w_gate_up, v_ffn_w_down):
    s = x.shape[1]
    x0 = x.reshape(s, D_MODEL)
    target = loss_target.reshape(s, D_MODEL)
    mx, my, mc = lax.axis_index("x"), lax.axis_index("y"), lax.axis_index("c")
    chip = 2 * mx + my
    chip_arr = jnp.reshape(chip, (1,)).astype(I32)
    c_arr = jnp.reshape(mc, (1,)).astype(I32)
    jc_arr = jnp.stack([chip, mc]).astype(I32)
    zero_bias = jnp.zeros((1, D_MODEL), F32)

    def gain(p, i):
        return p[i:i + 1]

    big = [attn_w_qkv, attn_w_o, sgu_w_in, sgu_w_out, ffn_w_gate_up, ffn_w_gate_up, ffn_w_down, ffn_w_down]
    layers = [0, 0, 0, 0, 0, 1, 0, 1]
    tags = ["qkv", "wo", "win", "wout", "wgu0", "wgu1", "wd0", "wd1"]
    full = {t: place_shard(w, l, chip_arr, BF16, name=f"place_{t}") for w, l, t in zip(big, layers, tags)
            if t in ("qkv", "wo")}
    ln_pack = _pad_rows(jnp.concatenate([sgu_ln_g, sgu_ln_b], axis=0), 16)[None]
    full["ln"] = place_shard(ln_pack, 0, chip_arr, F32, name="place_ln")

    def split(items):
        return [i if isinstance(i, str) else i[0] for i in items], [WHOLE if isinstance(i, str) else tuple(i[1:]) for i in items]

    def ici(*items):
        names, pieces = split(items)
        return gather_ici_rider([full[n] for n in names], pieces)

    def d2d(*items):
        names, pieces = split(items)
        return gather_d2d_rider([full[n] for n in names], pieces)

    def landed(items, res):
        for n, r in zip(split(items)[0], res):
            full[n] = r

    cos, sin = _rope_tables(s)
    sink_rows = jnp.broadcast_to(
        jnp.repeat(attn_sinks.reshape(N_KV_HEADS, GQA_GROUP), WINDOW, axis=1)[:, None, :], (N_KV_HEADS, 8, ROWS))
    w_sp = sgu_w_spatial.reshape(SGU_GROUPS, SGU_CHUNK, SGU_CHUNK)
    b_sp = jnp.broadcast_to(sgu_b_spatial.reshape(SGU_GROUPS, SGU_CHUNK)[:, :, None], (SGU_GROUPS, SGU_CHUNK, LANES))

    (h0, full["wgu0"]), (res,) = prenorm_and_place(x0, gain(norm_mix_pre, 0), ffn_w_gate_up, 0, chip_arr, name="prenorm_0",
                                                   riders=[ici("qkv", "ln")])
    landed(("qkv", "ln"), res)
    full["wgu1"], (res,) = place_shard(ffn_w_gate_up, 1, chip_arr, BF16, name="place_wgu1", riders=[d2d("qkv", "ln")])
    landed(("qkv", "ln"), res)
    ln_g = full["ln"][:, 0, :].reshape(1, D_MODEL)
    ln_b = full["ln"][:, 1, :].reshape(1, D_MODEL)

    def hosted(call, stages):
        outputs, results = call([{"ici": ici, "d2d": d2d}[kind](*items) for kind, items in stages])
        for (_, items), res in zip(stages, results):
            landed(items, res)
        return outputs

    casts = [(ffn_w_down, 0), (sgu_w_in, 0), (ffn_w_down, 1), (sgu_w_out, 0)]
    qkv, full["wd0"], full["win"], full["wd1"], full["wout"] = hosted(
        lambda r: qkv_proj(h0, full["qkv"], attn_b_qkv, cos, sin, casts, chip_arr, name="qkv_proj", riders=r),
        [("ici", ("wo", ("wgu0", 0, 3, 8)))])
    o = hosted(lambda r: attn_fwd(qkv, sink_rows, name="attn_fwd", riders=r),
               [("d2d", ("wo",)), ("ici", (("wgu0", 3, 8, 8), ("wd0", 0, 2, 11), ("win", 0, 2, 8)))])
    w_o = full["wo"].reshape(Q_WIDTH, D_MODEL)
    x1, h1, m0 = hosted(lambda r: proj_residual_norm(o, w_o, x0, attn_b_o, gain(norm_mix_post, 0), gain(norm_ffn_pre, 0),
                                                     name="attn_out_norm", riders=r),
                        [("d2d", ("wgu0",)), ("ici", (("wd0", 2, 11, 11),))])
    gu0, a0 = hosted(lambda r: ffn_up(h1, full["wgu0"], name="ffn_up_0", riders=r),
                     [("d2d", ("wd0",)), ("ici", (("win", 2, 8, 8), "wout", ("wgu1", 0, 4, 8)))])
    w_d0 = full["wd0"].reshape(D_FF, D_MODEL)
    x2, h2, f0 = hosted(lambda r: proj_residual_norm(a0, w_d0, x1, zero_bias, gain(norm_ffn_post, 0), gain(norm_mix_pre, 1),
                                                     name="ffn_down_norm_0", riders=r),
                        [("d2d", ("win", "wout")), ("ici", (("wgu1", 4, 8, 8),))])
    w_in = full["win"]
    z, y = hosted(lambda r: sgu_in_fwd(h2, w_in, ln_g, ln_b, w_sp, b_sp, name="sgu_in_fwd", riders=r),
                  [("d2d", ("wgu1",)), ("ici", ("wd1",))])
    w_out = full["wout"].reshape(D_MODEL, D_MODEL)
    x3, h3, m1 = hosted(lambda r: proj_residual_norm(y, w_out, x2, zero_bias, gain(norm_mix_post, 1), gain(norm_ffn_pre, 1),
                                                     name="sgu_out_norm", riders=r),
                        [("d2d", ("wd1",))])
    w_qkv, w_gu0, w_gu1 = full["qkv"], full["wgu0"], full["wgu1"]
    w_d1 = full["wd1"].reshape(D_FF, D_MODEL)
    gu1, a1, dx4, df1, dg_fpost1, loss_part = ffn_fwd_loss_rows(
        h3, w_gu1, w_d1, x3, gain(norm_ffn_post, 1), target, name="ffn_fwd_loss_rows")

    red = _GradReduce(c_arr, jc_arr, {
        "qkv": attn_w_qkv.shape[1:], "wo": attn_w_o.shape[1:], "win": sgu_w_in.shape[1:], "wout": sgu_w_out.shape[1:],
        "wgu": ffn_w_gate_up.shape, "wd": ffn_w_down.shape, "slab": (N_CHIPS, SLAB_ROWS // N_CHIPS, D_MODEL)})
    where = {"qkv": ("qkv", None), "wo": ("wo", None), "win": ("win", None), "wout": ("wout", None), "wgu0": ("wgu", 0),
             "wgu1": ("wgu", 1), "wd0": ("wd", 0), "wd1": ("wd", 1), "small": ("slab", "chip")}

    dgu1, dx3, dm1, dg_fpre1, dg_mpost1, _ = ffn_bwd_rows(
        df1, w_d1, gu1, w_gu1, dx4, x3, gain(norm_ffn_pre, 1), m1, gain(norm_mix_post, 1), name="ffn_bwd_rows_1")
    red.grad["wd1"] = mm_tn(a1, df1, shard_major=False, tm=256, tn=D_MODEL, name="dw_down_1").reshape(
        N_CHIPS, D_FF // N_CHIPS, D_MODEL)
    red.grad["wgu1"], (res,) = mm_tn(h3, dgu1, shard_major=True, tm=512, tn=FF_HALF, name="dw_gate_up_1",
                                     riders=[red.exchange(["wd1"])])
    red.exchanged(["wd1"], res)
    dy, (res,) = mm_nt(dm1, w_out, out_dtype=F32, name="dy_sgu", riders=[red.exchange(["wgu1"])])
    red.exchanged(["wgu1"], res)
    red.grad["wout"] = mm_tn(y, dm1, shard_major=False, tm=512, tn=D_MODEL, name="dw_sgu_out").reshape(
        N_CHIPS, D_MODEL // N_CHIPS, D_MODEL)
    (dz, dw_sp, db_sp, dln_g, dln_b), (res_a, res_b) = sgu_bwd(
        z, dy, ln_g, ln_b, w_sp, b_sp, name="sgu_bwd", riders=[red.scatter(["wd1"]), red.exchange(["wout"])])
    red.scattered(["wd1"], res_a, where)
    red.exchanged(["wout"], res_b)
    names, rider = red.broadcast([("wd", 1)])
    red.grad["win"], (res_a, res_b) = mm_tn(h2, dz, shard_major=True, tm=D_MODEL, tn=2 * D_MODEL // N_CHIPS, name="dw_sgu_in",
                                            riders=[rider, red.scatter(["wout"])])
    red.broadcasted(names, res_a)
    red.scattered(["wout"], res_b, where)
    names, rider = red.broadcast([("wout", None)])
    (dx2, df0, dg_mpre1, dg_fpost0, _), (res_a, res_b) = dh_norm_bwd_pair(
        dz, w_in, dx3, x2, gain(norm_mix_pre, 1), f0, gain(norm_ffn_post, 0), name="dh_sgu_norm",
        riders=[red.exchange(["win"]), rider])
    red.exchanged(["win"], res_a)
    red.broadcasted(names, res_b)
    (dgu0, dx1, dm0, dg_fpre0, dg_mpost0, db_o), (res,) = ffn_bwd_rows(
        df0, w_d0, gu0, w_gu0, dx2, x1, gain(norm_ffn_pre, 0), m0, gain(norm_mix_post, 0), name="ffn_bwd_rows_0",
        riders=[red.scatter(["wgu1", "win"])])
    red.scattered(["wgu1", "win"], res, where)
    names, rider = red.broadcast([("wgu", 1), ("win", None)])
    dw_d0, (res,) = mm_tn(a0, df0, shard_major=False, tm=256, tn=D_MODEL, name="dw_down_0", riders=[rider])
    red.broadcasted(names, res)
    red.grad["wd0"] = dw_d0.reshape(N_CHIPS, D_FF // N_CHIPS, D_MODEL)
    do, (res,) = mm_nt(dm0, w_o, out_dtype=BF16, name="do_attn", riders=[red.exchange(["wd0"])])
    red.exchanged(["wd0"], res)
    red.grad["wgu0"], (res,) = mm_tn(h1, dgu0, shard_major=True, tm=512, tn=FF_HALF, name="dw_gate_up_0",
                                     riders=[red.scatter(["wd0"])])
    red.scattered(["wd0"], res, where)
    names, rider = red.broadcast([("wd", 0)])
    dw_o, (res_a, res_b) = mm_tn(o, dm0, shard_major=False, tm=512, tn=D_MODEL, name="dw_attn_out",
                                 riders=[red.exchange(["wgu0"]), rider])
    red.exchanged(["wgu0"], res_a)
    red.broadcasted(names, res_b)
    red.grad["wo"] = dw_o.reshape(N_CHIPS, Q_WIDTH // N_CHIPS, D_MODEL)
    (dq, dkc, dkp, dvc, dvp, dsink), (res_a, res_b) = attn_bwd(
        qkv, sink_rows, do, name="attn_bwd", riders=[red.scatter(["wgu0"]), red.exchange(["wo"])])
    red.scattered(["wgu0"], res_a, where)
    red.exchanged(["wo"], res_b)
    names, rider = red.broadcast([("wgu", 0)])
    (dqkv, db_qkv), (res,) = rope_bwd(dq, dkc, dkp, dvc, dvp, cos, sin, name="rope_bwd", riders=[rider])
    red.broadcasted(names, res)
    red.grad["qkv"], (res,) = mm_tn(h0, dqkv, shard_major=True, tm=D_MODEL, tn=QKV_WIDTH // N_CHIPS, name="dw_qkv",
                                    riders=[red.scatter(["wo"])])
    red.scattered(["wo"], res, where)
    grad_x, dg_mpre0 = dh_norm_bwd_last(dqkv, w_qkv, dx1, x0, gain(norm_mix_pre, 0), name="dh_attn_norm_in")

    norm_grads = [jnp.concatenate(p, axis=0) for p in
                  ((dg_mpre0, dg_mpre1), (dg_mpost0, dg_mpost1), (dg_fpre0, dg_fpre1), (dg_fpost0, dg_fpost1))]
    red.grad["small"] = _pack_small(norm_grads, db_qkv, db_o, dsink[:, :, 0, 0], db_sp[:, :, 0], dln_g, dln_b, dw_sp,
                                    loss_part)
    def big_update(w, g, m, v, tag, after=None):
        return adamw(w, g.reshape(w.shape), m, v, name=f"adamw_{tag}", after=after)

    (res,) = comm_call([red.exchange(["qkv", "small"])], name="tail_1")
    red.exchanged(["qkv", "small"], res)
    state, token = comm_start([red.scatter(["qkv", "small"])], name="tail_2_start")
    upd_wgu = big_update(ffn_w_gate_up, red.dest["wgu"], m_ffn_w_gate_up, v_ffn_w_gate_up, "wgu", after=token)
    (res,) = comm_wait(state, upd_wgu[1], name="tail_2_wait")
    red.scattered(["qkv", "small"], res, where)
    names, rider = red.broadcast([("qkv", None), ("wo", None)])
    state, token = comm_start([rider, allcast_rider(red.dest["slab"])], name="tail_3_start")
    upd_wd = big_update(ffn_w_down, red.dest["wd"], m_ffn_w_down, v_ffn_w_down, "wd", after=token)
    res, (slab_full,) = comm_wait(state, upd_wd[1], name="tail_3_wait")
    red.broadcasted(names, res)
    g_qkv, g_wo, g_win, g_wout = (red.dest[n] for n in ("qkv", "wo", "win", "wout"))
    g_norms, g_bqkv, g_bo, g_sinks, g_bsp, g_lng, g_lnb, g_wsp, loss = _unpack_small(slab_full, chip)

    upd = {
        "attn_w_qkv": big_update(attn_w_qkv, g_qkv, m_attn_w_qkv, v_attn_w_qkv, "qkv"),
        "attn_w_o": big_update(attn_w_o, g_wo, m_attn_w_o, v_attn_w_o, "wo"),
        "sgu_w_in": big_update(sgu_w_in, g_win, m_sgu_w_in, v_sgu_w_in, "win"),
        "sgu_w_out": big_update(sgu_w_out, g_wout, m_sgu_w_out, v_sgu_w_out, "wout"),
        "ffn_w_gate_up": upd_wgu,
        "ffn_w_down": upd_wd,
    }
    small_names = ["norm_mix_pre", "norm_mix_post", "norm_ffn_pre", "norm_ffn_post", "attn_b_qkv", "attn_sinks", "attn_b_o",
                   "sgu_ln_g", "sgu_ln_b", "sgu_w_spatial", "sgu_b_spatial"]
    small_w = [norm_mix_pre, norm_mix_post, norm_ffn_pre, norm_ffn_post, attn_b_qkv, attn_sinks, attn_b_o, sgu_ln_g, sgu_ln_b,
               sgu_w_spatial, sgu_b_spatial]
    small_m = [m_norm_mix_pre, m_norm_mix_post, m_norm_ffn_pre, m_norm_ffn_post, m_attn_b_qkv, m_attn_sinks, m_attn_b_o,
               m_sgu_ln_g, m_sgu_ln_b, m_sgu_w_spatial, m_sgu_b_spatial]
    small_v = [v_norm_mix_pre, v_norm_mix_post, v_norm_ffn_pre, v_norm_ffn_post, v_attn_b_qkv, v_attn_sinks, v_attn_b_o,
               v_sgu_ln_g, v_sgu_ln_b, v_sgu_w_spatial, v_sgu_b_spatial]
    small_g = g_norms + [g_bqkv, g_sinks, g_bo, g_lng, g_lnb, g_wsp, g_bsp]

    def flat2(a):
        return a.reshape(-1, a.shape[-1])

    res = adamw_small([flat2(a) for a in small_w], [flat2(a) for a in small_g], [flat2(a) for a in small_m],
                      [flat2(a) for a in small_v], name="adamw_small")
    for i, nm in enumerate(small_names):
        upd[nm] = tuple(r[i].reshape(small_w[i].shape) for r in res)

    order = ["norm_mix_pre", "norm_mix_post", "norm_ffn_pre", "norm_ffn_post", "attn_w_qkv", "attn_b_qkv", "attn_sinks",
             "attn_w_o", "attn_b_o", "sgu_w_in", "sgu_ln_g", "sgu_ln_b", "sgu_w_spatial", "sgu_b_spatial", "sgu_w_out",
             "ffn_w_gate_up", "ffn_w_down"]
    outs = [loss, grad_x.reshape(1, s, D_MODEL)]
    for part in range(4):
        outs += [upd[nm][part] for nm in order]
    return tuple(outs)
```

```python
import types

import numpy as np
import jax
import jax.numpy as jnp
from jax import lax
from jax.experimental import pallas as pl
from jax.experimental.pallas import tpu as pltpu

F32 = jnp.float32
BF16 = jnp.bfloat16
I32 = jnp.int32

D_MODEL = 1024
HEAD_DIM = 64
N_Q_HEADS = 16
N_KV_HEADS = 4
GQA_GROUP = 4
WINDOW = 128
Q_WIDTH = 1024
KV_WIDTH = 256
QKV_WIDTH = 1536
ROPE_THETA = 10000.0
SGU_GROUPS = 8
SGU_CHUNK = 128
D_FF = 2816
FF_HALF = D_FF // 2
EPS = 1e-6
N_CHIPS = 4
LANES = 128

ADAM_LR = 0.001
ADAM_B1 = 0.9
ADAM_B2 = 0.999
ADAM_EPS = 1e-08
ADAM_WD = 0.01
ADAM_STEP = 10

VMEM_LIMIT = 52 * 1024 * 1024
BIG_VMEM_LIMIT = 62 * 1024 * 1024
SUB_ROWS = 256
MESH = pl.DeviceIdType.MESH
NEG = -1e30
NT_DIMS = (((1,), (1,)), ((), ()))
TN_DIMS = (((0,), (0,)), ((), ()))
NN_DIMS = (((1,), (0,)), ((), ()))
ANY = pl.BlockSpec(memory_space=pl.ANY)


def _row_tile(s, want):
    return want if s % want == 0 else s


PEER_KINDS = ("sibling", "chips", "sibling+chips", "everyone")


def _peer_kind(riders):
    kinds = {r.peers for r in riders}
    if not kinds:
        return None
    if "everyone" in kinds:
        return "everyone"
    return "sibling+chips" if len(kinds) == 2 else kinds.pop()


def _peer_barrier(kind):
    x, y, c = _place()
    chips = [(*_partner(x, y, k), c) for k in (1, 2, 3)]
    peers = {"sibling": [(x, y, 1 - c)], "chips": chips, "sibling+chips": [(x, y, 1 - c)] + chips,
             "everyone": [(x, y, 1 - c)] + chips + [(px, py, 1 - c) for px, py, _ in chips]}[kind]
    barrier = pltpu.get_barrier_semaphore()
    for dev in peers:
        pl.semaphore_signal(barrier, inc=1, device_id=dev, device_id_type=MESH)
    pl.semaphore_wait(barrier, len(peers))


def _call(body, *, name, grid=(), in_specs=(), out_specs=(), out_shape=(), scratch_shapes=(), operands=(), prefetch=(),
          aliases=None, riders=(), sem=None, vmem_limit=VMEM_LIMIT):
    n_pre, n_in, n_out, n_scr = len(prefetch), len(operands), len(out_shape), len(scratch_shapes)
    in_specs, out_specs, out_shape = list(in_specs), list(out_specs), list(out_shape)
    operands, scratch_shapes = list(operands), list(scratch_shapes)
    io_alias = {n_pre + i: o for i, o in (aliases or {}).items()}
    for r in riders:
        base_in, base_out = n_pre + len(operands), len(out_shape)
        operands += list(r.inputs)
        in_specs += [ANY] * len(r.inputs)
        for pos, i in enumerate(r.aliased):
            io_alias[base_in + i] = base_out + pos
            out_shape.append(jax.ShapeDtypeStruct(r.inputs[i].shape, r.inputs[i].dtype))
        out_shape += list(r.fresh)
        out_specs += [ANY] * (len(r.aliased) + len(r.fresh))
        scratch_shapes += [pltpu.SemaphoreType.DMA((r.nsem,)), pltpu.SemaphoreType.DMA((r.nsem,))]

    def wrapped(*refs):
        pre, p = refs[:n_pre], n_pre
        core_in, p = refs[p:p + n_in], p + n_in
        r_in = []
        for r in riders:
            r_in.append(refs[p:p + len(r.inputs)])
            p += len(r.inputs)
        core_out, p = refs[p:p + n_out], p + n_out
        r_out = []
        for r in riders:
            k = len(r.aliased) + len(r.fresh)
            r_out.append(refs[p:p + k])
            p += k
        core_scr, p = refs[p:p + n_scr], p + n_scr
        r_sem = [refs[p + 2 * i:p + 2 * i + 2] for i in range(len(riders))]

        def edge(at_last, fns):
            def run():
                if not at_last:
                    _peer_barrier(peer_kind)
                for i, r in enumerate(riders):
                    getattr(r, fns)(r_in[i], r_out[i], r_sem[i][0], r_sem[i][1])
            if not riders:
                return
            if not grid:
                run()
                return
            cond = None
            for d, n in enumerate(grid):
                c = pl.program_id(d) == (n - 1 if at_last else 0)
                cond = c if cond is None else jnp.logical_and(cond, c)
            pl.when(cond)(run)

        edge(False, "start")
        if body is not None:
            body(*pre, *core_in, *core_out, *core_scr)
        edge(True, "finish")

    if sem is None or riders:
        sem = ("arbitrary",) * len(grid)
    kwargs = dict(out_shape=out_shape, input_output_aliases=io_alias, name=name)
    peer_kind = _peer_kind(riders)
    collective = {} if peer_kind is None else {"collective_id": PEER_KINDS.index(peer_kind)}
    if grid:
        kwargs["compiler_params"] = pltpu.CompilerParams(dimension_semantics=sem, vmem_limit_bytes=vmem_limit, **collective)
    elif collective:
        kwargs["compiler_params"] = pltpu.CompilerParams(**collective)
    if n_pre:
        kwargs["grid_spec"] = pltpu.PrefetchScalarGridSpec(
            num_scalar_prefetch=n_pre, grid=grid, in_specs=in_specs, out_specs=out_specs, scratch_shapes=scratch_shapes)
    else:
        kwargs.update(grid=grid, in_specs=in_specs, out_specs=out_specs, scratch_shapes=scratch_shapes)
    res = pl.pallas_call(wrapped, **kwargs)(*prefetch, *operands)
    core, rest, rider_res = list(res[:n_out]), list(res[n_out:]), []
    for r in riders:
        k = len(r.aliased) + len(r.fresh)
        rider_res.append(rest[:k])
        rest = rest[k:]
    return core, rider_res


def _mm_call(*, grid, in_specs, out_spec, out_shape, dims, nk, kaxis, acc_shape, name, operands, riders=()):
    out_dtype = out_shape.dtype

    def body(a_ref, b_ref, o_ref, *scratch):
        p = lax.dot_general(a_ref[...].astype(BF16), b_ref[...].astype(BF16), dims, preferred_element_type=F32)
        if nk == 1:
            o_ref[...] = p.astype(out_dtype)
        else:
            acc = scratch[0]
            kk = pl.program_id(kaxis)

            @pl.when(kk == 0)
            def _():
                acc[...] = p

            @pl.when(kk > 0)
            def _():
                acc[...] += p

            @pl.when(kk == nk - 1)
            def _():
                o_ref[...] = acc[...].astype(out_dtype)

    sem = ["parallel"] * len(grid)
    if nk > 1:
        sem[kaxis] = "arbitrary"
    (out,), rider_res = _call(
        body, grid=grid, in_specs=in_specs, out_specs=[out_spec], out_shape=[out_shape],
        scratch_shapes=[pltpu.VMEM(acc_shape, F32)] if nk > 1 else [], operands=operands, name=name, riders=riders,
        sem=tuple(sem))
    return (out, rider_res) if riders else out


def mm_nt(a, w, *, out_dtype, name, tm=1024, riders=()):
    m, n = a.shape
    kout = w.shape[0]
    tm = _row_tile(m, tm)
    return _mm_call(grid=(m // tm,),
                    in_specs=[pl.BlockSpec((tm, n), lambda i: (i, 0)), pl.BlockSpec((kout, n), lambda i: (0, 0))],
                    out_spec=pl.BlockSpec((tm, kout), lambda i: (i, 0)),
                    out_shape=jax.ShapeDtypeStruct((m, kout), out_dtype), dims=NT_DIMS, nk=1, kaxis=0,
                    acc_shape=None, name=name, operands=(a, w), riders=riders)


def mm_tn(a, b, *, shard_major, name, tm, tn, tk=None, out_dtype=BF16, riders=()):
    s, m = a.shape
    tk = s if tk is None else _row_tile(s, tk)
    if b.ndim == 3:
        n = 2 * b.shape[2]
        b_spec = pl.BlockSpec((None, tk, tn), lambda j, i, kk: (j // 2, kk, j % 2))
    else:
        n = b.shape[1]
        b_spec = pl.BlockSpec((tk, tn), lambda j, i, kk: (kk, j))
    if shard_major:
        assert tn == n // N_CHIPS
        o_spec = pl.BlockSpec((None, tm, tn), lambda j, i, kk: (j, i, 0))
        o_shape = jax.ShapeDtypeStruct((N_CHIPS, m, tn), out_dtype)
    else:
        o_spec = pl.BlockSpec((tm, tn), lambda j, i, kk: (i, j))
        o_shape = jax.ShapeDtypeStruct((m, n), out_dtype)
    return _mm_call(grid=(n // tn, m // tm, s // tk),
                    in_specs=[pl.BlockSpec((tk, tm), lambda j, i, kk: (kk, i)), b_spec], out_spec=o_spec,
                    out_shape=o_shape, dims=TN_DIMS, nk=s // tk, kaxis=2, acc_shape=(tm, tn), name=name, operands=(a, b),
                    riders=riders)


def _rstd(x):
    return lax.rsqrt(jnp.mean(x * x, axis=-1, keepdims=True) + EPS)


def _rms_bwd(dy, x, g):
    r = _rstd(x)
    xhat = x * r
    gy = dy * g
    dx = r * (gy - xhat * jnp.mean(gy * xhat, axis=-1, keepdims=True))
    return dx, jnp.sum(dy * xhat, axis=0, keepdims=True)


def _accum(ref, val, first):
    @pl.when(first)
    def _():
        ref[...] = val

    @pl.when(jnp.logical_not(first))
    def _():
        ref[...] += val


def _row_spec(tm, width):
    return pl.BlockSpec((tm, width), lambda i: (i, 0))


def _vec_spec(width):
    return pl.BlockSpec((1, width), lambda i: (0, 0))


def _ret(core, rider_res, riders):
    core = core[0] if len(core) == 1 else core
    return (core, rider_res) if riders else core


def prenorm_and_place(x, g, w, layer, chip_arr, *, name, tm=256, riders=()):
    s = x.shape[0]
    tm = _row_tile(s, tm)
    steps = s // tm
    _, r, c = w.shape
    tr = r // steps
    assert tr * steps == r and tr % 16 == 0

    def body(chip_ref, x_ref, g_ref, w_ref, h_ref, o_ref):
        xv = x_ref[...]
        h_ref[...] = (xv * _rstd(xv) * g_ref[...]).astype(BF16)
        o_ref[...] = w_ref[...].astype(BF16)

    core, rr = _call(
        body, grid=(steps,), prefetch=(chip_arr,),
        in_specs=[pl.BlockSpec((tm, D_MODEL), lambda i, chip: (i, 0)), pl.BlockSpec((1, D_MODEL), lambda i, chip: (0, 0)),
                  pl.BlockSpec((None, tr, c), lambda i, chip: (layer, i, 0))],
        out_specs=[pl.BlockSpec((tm, D_MODEL), lambda i, chip: (i, 0)), pl.BlockSpec((None, tr, c), lambda i, chip: (chip[0], i, 0))],
        out_shape=[jax.ShapeDtypeStruct((s, D_MODEL), BF16), jax.ShapeDtypeStruct((N_CHIPS, r, c), BF16)],
        operands=(x, g, w), sem=("parallel",), name=name, riders=riders)
    return _ret(core, rr, riders)


def proj_residual_norm(a, w, x, bias, g_post, g_next, *, name, tm=512, sub=256, riders=()):
    s, k = a.shape
    tm = _row_tile(s, tm)
    sub = min(sub, tm)

    def body(a_ref, w_ref, x_ref, b_ref, gp_ref, gn_ref, xo_ref, h_ref, m_ref):
        for t in range(tm // sub):
            rows = slice(t * sub, (t + 1) * sub)
            mv = jnp.dot(a_ref[rows, :], w_ref[...], preferred_element_type=F32) + b_ref[...]
            m_ref[rows, :] = mv.astype(BF16)
            xn = x_ref[rows, :] + mv * _rstd(mv) * gp_ref[...]
            xo_ref[rows, :] = xn
            h_ref[rows, :] = (xn * _rstd(xn) * gn_ref[...]).astype(BF16)

    row, vec = _row_spec(tm, D_MODEL), _vec_spec(D_MODEL)
    core, rr = _call(
        body, grid=(s // tm,),
        in_specs=[_row_spec(tm, k), pl.BlockSpec((k, D_MODEL), lambda i: (0, 0)), row, vec, vec, vec], out_specs=[row, row, row],
        out_shape=[jax.ShapeDtypeStruct((s, D_MODEL), F32), jax.ShapeDtypeStruct((s, D_MODEL), BF16),
                   jax.ShapeDtypeStruct((s, D_MODEL), BF16)],
        operands=(a, w, x, bias, g_post, g_next), sem=("parallel",), name=name, riders=riders)
    return _ret(core, rr, riders)


def ffn_fwd_loss_rows(h, w_gu, w_d, x, g_post, target, *, name, tm=512, riders=()):
    s = x.shape[0]
    tm = _row_tile(s, tm)

    def body(h_ref, w0, w1, w2, w3, wd_ref, x_ref, g_ref, t_ref, d_ref, a_ref, dx_ref, df_ref, dg_ref, loss_ref):
        first = pl.program_id(0) == 0
        halves = [slice(half * FF_HALF, (half + 1) * FF_HALF) for half in (0, 1)]
        gain = g_ref[...]
        sub = min(SUB_ROWS, tm)
        sums = None
        for t in range(tm // sub):
            rows = slice(t * sub, (t + 1) * sub)
            hv = h_ref[rows, :]
            fv = None
            for cols, (wg_ref, wu_ref) in zip(halves, ((w0, w2), (w1, w3))):
                g = jnp.dot(hv, wg_ref[...], preferred_element_type=F32)
                u = jnp.dot(hv, wu_ref[...], preferred_element_type=F32)
                sig = _sigmoid(g)
                silu = g * sig
                d_ref[0, rows, cols] = (u * (sig + silu * (1.0 - sig))).astype(BF16)
                d_ref[1, rows, cols] = silu.astype(BF16)
                act = (silu * u).astype(BF16)
                a_ref[rows, cols] = act
                p = jnp.dot(act, wd_ref[cols, :], preferred_element_type=F32)
                fv = p if fv is None else fv + p
            err = x_ref[rows, :] + fv * _rstd(fv) * gain - t_ref[rows, :]
            dx = err * (1.0 / D_MODEL)
            dx_ref[rows, :] = dx
            df, dg = _rms_bwd(dx, fv, gain)
            df_ref[rows, :] = df.astype(BF16)
            part = (dg, jnp.sum(jnp.sum(err * err, axis=-1, keepdims=True), axis=0, keepdims=True) * (0.5 / D_MODEL))
            sums = part if sums is None else tuple(a + b for a, b in zip(sums, part))
        _accum(dg_ref, sums[0], first)
        _accum(loss_ref, jnp.broadcast_to(sums[1], (8, LANES)), first)

    def resident(shape, index):
        return pl.BlockSpec(shape, index, pipeline_mode=pl.Buffered(1))

    row, vec = _row_spec(tm, D_MODEL), _vec_spec(D_MODEL)
    shards = [resident((None, D_MODEL, FF_HALF), (lambda j: (lambda i: (j, 0, 0)))(j)) for j in range(N_CHIPS)]
    core, rr = _call(
        body, grid=(s // tm,),
        in_specs=[row] + shards + [resident((D_FF, D_MODEL), lambda i: (0, 0)), row, vec, row],
        out_specs=[pl.BlockSpec((2, tm, D_FF), lambda i: (0, i, 0)), _row_spec(tm, D_FF), row, row, vec,
                   pl.BlockSpec((8, LANES), lambda i: (0, 0))],
        out_shape=[jax.ShapeDtypeStruct((2, s, D_FF), BF16), jax.ShapeDtypeStruct((s, D_FF), BF16),
                   jax.ShapeDtypeStruct((s, D_MODEL), F32), jax.ShapeDtypeStruct((s, D_MODEL), BF16),
                   jax.ShapeDtypeStruct((1, D_MODEL), F32), jax.ShapeDtypeStruct((8, LANES), F32)],
        operands=(h, w_gu, w_gu, w_gu, w_gu, w_d, x, g_post, target), name=name, riders=riders, vmem_limit=BIG_VMEM_LIMIT)
    return _ret(core, rr, riders)


def dh_norm_bwd_pair(a, w, dres, x, g_pre, m, g_post, *, name, tm=512, sub=256, riders=()):
    _, kout, ns = w.shape
    planes = a.ndim == 3
    s = x.shape[0]
    tm = _row_tile(s, tm)
    sub = min(sub, tm)
    a_spec = pl.BlockSpec((2, tm, 2 * ns), lambda i: (0, i, 0)) if planes else pl.BlockSpec((tm, N_CHIPS * ns), lambda i: (i, 0))

    def body(a_ref, w0, w1, w2, w3, dres_ref, x_ref, gpre_ref, m_ref, gpost_ref, dx_ref, dm_ref, dgpre_ref, dgpost_ref, db_ref):
        first = pl.program_id(0) == 0
        sums = None
        for t in range(tm // sub):
            rows = slice(t * sub, (t + 1) * sub)
            dh = None
            for j, w_ref in enumerate((w0, w1, w2, w3)):
                a_j = a_ref[j // 2, rows, (j % 2) * ns:(j % 2 + 1) * ns] if planes else a_ref[rows, j * ns:(j + 1) * ns]
                p = lax.dot_general(a_j, w_ref[...], NT_DIMS, preferred_element_type=F32)
                dh = p if dh is None else dh + p
            d1, dgpre = _rms_bwd(dh, x_ref[rows, :], gpre_ref[...])
            dx = dres_ref[rows, :] + d1
            dx_ref[rows, :] = dx
            dm, dgpost = _rms_bwd(dx, m_ref[rows, :].astype(F32), gpost_ref[...])
            dm_ref[rows, :] = dm.astype(BF16)
            part = (dgpre, dgpost, jnp.sum(dm, axis=0, keepdims=True))
            sums = part if sums is None else tuple(u + v for u, v in zip(sums, part))
        _accum(dgpre_ref, sums[0], first)
        _accum(dgpost_ref, sums[1], first)
        _accum(db_ref, sums[2], first)

    def shard(j):
        return pl.BlockSpec((None, kout, ns), lambda i: (j, 0, 0))

    row, vec = _row_spec(tm, D_MODEL), _vec_spec(D_MODEL)
    vshape = jax.ShapeDtypeStruct((1, D_MODEL), F32)
    core, rr = _call(
        body, grid=(s // tm,), in_specs=[a_spec] + [shard(j) for j in range(N_CHIPS)] + [row, row, vec, row, vec],
        out_specs=[row, row, vec, vec, vec],
        out_shape=[jax.ShapeDtypeStruct((s, D_MODEL), F32), jax.ShapeDtypeStruct((s, D_MODEL), BF16), vshape, vshape, vshape],
        operands=(a, w, w, w, w, dres, x, g_pre, m, g_post), name=name, riders=riders)
    return _ret(core, rr, riders)


def ffn_bwd_rows(df, w_d, d_planes, w_gu, dres, x, g_pre, m, g_post, *, name, tm=512, riders=()):
    s = x.shape[0]
    tm = _row_tile(s, tm)

    def body(df_ref, wd_ref, d_ref, w0, w1, w2, w3, dres_ref, x_ref, gpre_ref, m_ref, gpost_ref,
             o_ref, dx_ref, dm_ref, dgpre_ref, dgpost_ref, db_ref):
        first = pl.program_id(0) == 0
        halves = [slice(half * FF_HALF, (half + 1) * FF_HALF) for half in (0, 1)]
        sub = min(SUB_ROWS, tm)
        sums = None
        for t in range(tm // sub):
            rows = slice(t * sub, (t + 1) * sub)
            dfv = df_ref[rows, :]
            dh = None
            for cols, (wg_ref, wu_ref) in zip(halves, ((w0, w2), (w1, w3))):
                da = lax.dot_general(dfv, wd_ref[cols, :], NT_DIMS, preferred_element_type=F32)
                dg = (da * d_ref[0, rows, cols].astype(F32)).astype(BF16)
                du = (da * d_ref[1, rows, cols].astype(F32)).astype(BF16)
                o_ref[0, rows, cols] = dg
                o_ref[1, rows, cols] = du
                p = lax.dot_general(dg, wg_ref[...], NT_DIMS, preferred_element_type=F32)
                p += lax.dot_general(du, wu_ref[...], NT_DIMS, preferred_element_type=F32)
                dh = p if dh is None else dh + p
            d1, dgpre = _rms_bwd(dh, x_ref[rows, :], gpre_ref[...])
            dx = dres_ref[rows, :] + d1
            dx_ref[rows, :] = dx
            dm, dgpost = _rms_bwd(dx, m_ref[rows, :].astype(F32), gpost_ref[...])
            dm_ref[rows, :] = dm.astype(BF16)
            part = (dgpre, dgpost, jnp.sum(dm, axis=0, keepdims=True))
            sums = part if sums is None else tuple(a + b for a, b in zip(sums, part))
        _accum(dgpre_ref, sums[0], first)
        _accum(dgpost_ref, sums[1], first)
        _accum(db_ref, sums[2], first)

    def resident(shape, index):
        return pl.BlockSpec(shape, index, pipeline_mode=pl.Buffered(1))

    planes = pl.BlockSpec((2, tm, D_FF), lambda i: (0, i, 0))
    row, vec = _row_spec(tm, D_MODEL), _vec_spec(D_MODEL)
    vshape = jax.ShapeDtypeStruct((1, D_MODEL), F32)
    shards = [resident((None, D_MODEL, FF_HALF), (lambda j: (lambda i: (j, 0, 0)))(j)) for j in range(N_CHIPS)]
    core, rr = _call(
        body, grid=(s // tm,),
        in_specs=[row, resident((D_FF, D_MODEL), lambda i: (0, 0)), planes] + shards + [row, row, vec, row, vec],
        out_specs=[planes, row, row, vec, vec, vec],
        out_shape=[jax.ShapeDtypeStruct((2, s, D_FF), BF16), jax.ShapeDtypeStruct((s, D_MODEL), F32),
                   jax.ShapeDtypeStruct((s, D_MODEL), BF16), vshape, vshape, vshape],
        operands=(df, w_d, d_planes, w_gu, w_gu, w_gu, w_gu, dres, x, g_pre, m, g_post), name=name, riders=riders,
        vmem_limit=BIG_VMEM_LIMIT)
    return _ret(core, rr, riders)


def dh_norm_bwd_last(a, w, dres, x, g_pre, *, name, tm=512, sub=256):
    _, kout, ns = w.shape
    s = x.shape[0]
    tm = _row_tile(s, tm)
    sub = min(sub, tm)

    def body(a_ref, w0, w1, w2, w3, dres_ref, x_ref, g_ref, dx_ref, dg_ref):
        total = None
        for t in range(tm // sub):
            rows = slice(t * sub, (t + 1) * sub)
            dh = None
            for j, w_ref in enumerate((w0, w1, w2, w3)):
                p = lax.dot_general(a_ref[rows, j * ns:(j + 1) * ns], w_ref[...], NT_DIMS, preferred_element_type=F32)
                dh = p if dh is None else dh + p
            d1, dg = _rms_bwd(dh, x_ref[rows, :], g_ref[...])
            dx_ref[rows, :] = dres_ref[rows, :] + d1
            total = dg if total is None else total + dg
        _accum(dg_ref, total, pl.program_id(0) == 0)

    def shard(j):
        return pl.BlockSpec((None, kout, ns), lambda i: (j, 0, 0))

    row, vec = _row_spec(tm, D_MODEL), _vec_spec(D_MODEL)
    (dx, dg), _ = _call(
        body, grid=(s // tm,), in_specs=[_row_spec(tm, N_CHIPS * ns)] + [shard(j) for j in range(N_CHIPS)] + [row, row, vec],
        out_specs=[row, vec], out_shape=[jax.ShapeDtypeStruct((s, D_MODEL), F32), jax.ShapeDtypeStruct((1, D_MODEL), F32)],
        operands=(a, w, w, w, w, dres, x, g_pre), name=name)
    return dx, dg


def _rope_tables(s):
    half = HEAD_DIM // 2
    inv_freq = np.float32(ROPE_THETA) ** (-(np.arange(half, dtype=np.float32) * np.float32(2.0)) / np.float32(HEAD_DIM))
    ang = np.arange(s, dtype=np.float32)[:, None] * inv_freq[None, :]
    cos, sin = np.cos(ang).astype(np.float32), np.sin(ang).astype(np.float32)
    return jnp.asarray(np.tile(cos, (1, 4))), jnp.asarray(np.concatenate([-sin, sin, -sin, sin], axis=1))


def _swap_halves(x):
    lane = lax.broadcasted_iota(I32, x.shape, 1)
    return jnp.where((lane & (HEAD_DIM - 1)) < HEAD_DIM // 2, pltpu.roll(x, LANES - 32, 1), pltpu.roll(x, 32, 1))


N_ROPE_BLOCKS = (Q_WIDTH + KV_WIDTH) // LANES


def qkv_proj(h, w, bias, cos, sin, casts, chip_arr, *, name, tm=1024, riders=()):
    s, k = h.shape
    ns = w.shape[2]
    tm = _row_tile(s, tm)
    nc = len(casts)

    def body(chip_ref, h_ref, w_ref, b_ref, c_ref, s_ref, *rest):
        cast_in, o_ref, cast_out = rest[:nc], rest[nc], rest[nc + 1:]
        j = pl.program_id(0)

        @pl.when(jnp.logical_and(j == 0, pl.program_id(1) == 0))
        def _():
            for src, dst in zip(cast_in, cast_out):
                dst[...] = src[...].astype(BF16)

        sub = min(256, tm)
        for t in range(tm // sub):
            rows = slice(t * sub, (t + 1) * sub)
            p = jnp.dot(h_ref[rows, :], w_ref[...], preferred_element_type=F32) + b_ref[...]
            cosv, sinv = c_ref[rows, :], s_ref[rows, :]
            for blk in range(ns // LANES):
                xb = p[:, blk * LANES:(blk + 1) * LANES]
                roped = xb * cosv + _swap_halves(xb) * sinv
                is_qk = j * (ns // LANES) + blk < N_ROPE_BLOCKS
                o_ref[rows, blk * LANES:(blk + 1) * LANES] = jnp.where(is_qk, roped, xb).astype(BF16)

    def cast_in_spec(wt, layer):
        return pl.BlockSpec((None,) + wt.shape[1:], lambda j, i, chip: (layer, 0, 0), pipeline_mode=pl.Buffered(1))

    def cast_out_spec(wt):
        return pl.BlockSpec((None,) + wt.shape[1:], lambda j, i, chip: (chip[0], 0, 0), pipeline_mode=pl.Buffered(1))

    core, rr = _call(
        body, grid=(N_CHIPS, s // tm), prefetch=(chip_arr,),
        in_specs=[pl.BlockSpec((tm, k), lambda j, i, chip: (i, 0)), pl.BlockSpec((None, k, ns), lambda j, i, chip: (j, 0, 0)),
                  pl.BlockSpec((1, ns), lambda j, i, chip: (0, j)), pl.BlockSpec((tm, LANES), lambda j, i, chip: (i, 0)),
                  pl.BlockSpec((tm, LANES), lambda j, i, chip: (i, 0))] + [cast_in_spec(wt, layer) for wt, layer in casts],
        out_specs=[pl.BlockSpec((tm, ns), lambda j, i, chip: (i, j))] + [cast_out_spec(wt) for wt, _ in casts],
        out_shape=[jax.ShapeDtypeStruct((s, N_CHIPS * ns), BF16)]
        + [jax.ShapeDtypeStruct((N_CHIPS,) + wt.shape[1:], BF16) for wt, _ in casts],
        operands=(h, w, bias, cos, sin, *[wt for wt, _ in casts]), sem=("arbitrary", "arbitrary"), name=name, riders=riders)
    return (core, rr) if riders else core


def rope_bwd(dq, dkc, dkp, dvc, dvp, cos, sin, *, name, riders=()):
    s = dq.shape[0]
    tm = 2 * WINDOW if s % (2 * WINDOW) == 0 else WINDOW
    nb = s // tm

    def body(dq_ref, dkc_ref, dkp_ref, dkp_next_ref, dvc_ref, dvp_ref, dvp_next_ref, c_ref, s_ref, o_ref, db_ref):
        i = pl.program_id(0)
        has_next = (i < nb - 1).astype(F32)
        cosv, sinv = c_ref[...], s_ref[...]

        def shifted(ref, next_ref, cols):
            last = has_next * next_ref[:WINDOW, cols].astype(F32)
            return last if tm == WINDOW else jnp.concatenate([ref[WINDOW:, cols].astype(F32), last], axis=0)

        parts = []
        for blk in range(QKV_WIDTH // LANES):
            if blk < Q_WIDTH // LANES:
                g = dq_ref[:, blk * LANES:(blk + 1) * LANES].astype(F32)
            else:
                own, prv, nxt = (dkc_ref, dkp_ref, dkp_next_ref) if blk < N_ROPE_BLOCKS else (dvc_ref, dvp_ref, dvp_next_ref)
                cols = slice((blk % 2) * LANES, (blk % 2 + 1) * LANES)
                g = own[:, cols].astype(F32) + shifted(prv, nxt, cols)
            if blk < N_ROPE_BLOCKS:
                g = g * cosv + _swap_halves(g * sinv)
            o_ref[:, blk * LANES:(blk + 1) * LANES] = g.astype(BF16)
            parts.append(jnp.sum(g, axis=0, keepdims=True))
        sums = jnp.concatenate(parts, axis=1)
        _accum(db_ref, sums, i == 0)

    own_spec = _row_spec(tm, KV_WIDTH)
    next_spec = pl.BlockSpec((tm, KV_WIDTH), lambda i: (jnp.minimum(i + 1, nb - 1), 0))
    core, rr = _call(
        body, grid=(nb,),
        in_specs=[_row_spec(tm, Q_WIDTH), own_spec, own_spec, next_spec, own_spec, own_spec, next_spec,
                  _row_spec(tm, LANES), _row_spec(tm, LANES)],
        out_specs=[_row_spec(tm, QKV_WIDTH), _vec_spec(QKV_WIDTH)],
        out_shape=[jax.ShapeDtypeStruct((s, QKV_WIDTH), BF16), jax.ShapeDtypeStruct((1, QKV_WIDTH), F32)],
        operands=(dq, dkc, dkp, dkp, dvc, dvp, dvp, cos, sin), name=name, riders=riders)
    return _ret(core, rr, riders)


ROWS = GQA_GROUP * WINDOW


def _prev_slots():
    kpos = lax.broadcasted_iota(I32, (WINDOW, ROWS), 0)
    qpos = lax.broadcasted_iota(I32, (WINDOW, ROWS), 1) & (WINDOW - 1)
    return kpos > qpos


def _head_cols(ref, head):
    return ref[:, head * HEAD_DIM:(head + 1) * HEAD_DIM]


def _stack_heads(ref, h):
    return jnp.concatenate([_head_cols(ref, GQA_GROUP * h + g) for g in range(GQA_GROUP)], axis=0)


def _band(prev_ref, cur_ref, h):
    return jnp.concatenate([_head_cols(prev_ref, h), _head_cols(cur_ref, h)], axis=0)


def _pick(prev, band):
    return jnp.where(prev, band[:WINDOW], band[WINDOW:])


def _spread(prev, x):
    return jnp.concatenate([jnp.where(prev, x, 0.0), jnp.where(prev, 0.0, x)], axis=0).astype(BF16)


def _attn_probs(s_band, sink, prev, has_prev):
    scale = HEAD_DIM ** -0.5
    s = jnp.where(prev, jnp.where(has_prev, s_band[:WINDOW], NEG), s_band[WINDOW:]) * scale
    m = jnp.maximum(jnp.max(s, axis=0, keepdims=True), sink)
    e, es = jnp.exp(s - m), jnp.exp(sink - m)
    inv = 1.0 / (jnp.sum(e, axis=0, keepdims=True) + es)
    return e * inv, es * inv


def _attn_specs(nb):
    kcol, vcol = Q_WIDTH // KV_WIDTH, Q_WIDTH // KV_WIDTH + 1
    q_spec = pl.BlockSpec((WINDOW, Q_WIDTH), lambda n: (n, 0))
    return [q_spec,
            pl.BlockSpec((WINDOW, KV_WIDTH), lambda n: (n, kcol)),
            pl.BlockSpec((WINDOW, KV_WIDTH), lambda n: (jnp.maximum(n - 1, 0), kcol)),
            pl.BlockSpec((WINDOW, KV_WIDTH), lambda n: (n, vcol)),
            pl.BlockSpec((WINDOW, KV_WIDTH), lambda n: (jnp.maximum(n - 1, 0), vcol)),
            pl.BlockSpec((N_KV_HEADS, 8, ROWS), lambda n: (0, 0, 0))]


def attn_fwd(qkv, sink_rows, *, name, riders=()):
    s = qkv.shape[0]

    def body(q_ref, kc_ref, kp_ref, vc_ref, vp_ref, sink_ref, o_ref):
        prev = _prev_slots()
        has_prev = pl.program_id(0) > 0
        heads = range(N_KV_HEADS)
        s_bands = [lax.dot_general(_band(kp_ref, kc_ref, h), _stack_heads(q_ref, h), NT_DIMS, preferred_element_type=F32)
                   for h in heads]
        p_bands = [_spread(prev, _attn_probs(s_bands[h], sink_ref[h, 0:1, :], prev, has_prev)[0]) for h in heads]
        outs = [lax.dot_general(_band(vp_ref, vc_ref, h), p_bands[h], TN_DIMS, preferred_element_type=F32).T for h in heads]
        for h in heads:
            for g in range(GQA_GROUP):
                head = GQA_GROUP * h + g
                o_ref[:, head * HEAD_DIM:(head + 1) * HEAD_DIM] = outs[h][g * WINDOW:(g + 1) * WINDOW].astype(BF16)

    core, rr = _call(
        body, grid=(s // WINDOW,), in_specs=_attn_specs(s // WINDOW), out_specs=[pl.BlockSpec((WINDOW, Q_WIDTH), lambda n: (n, 0))],
        out_shape=[jax.ShapeDtypeStruct((s, Q_WIDTH), BF16)], operands=(qkv, qkv, qkv, qkv, qkv, sink_rows), sem=("parallel",),
        name=name, riders=riders)
    return _ret(core, rr, riders)


def attn_bwd(qkv, sink_rows, do, *, name, riders=()):
    s = qkv.shape[0]

    def body(q_ref, kc_ref, kp_ref, vc_ref, vp_ref, sink_ref, do_ref, dq_ref, dkc_ref, dkp_ref, dvc_ref, dvp_ref, dsink_ref):
        n = pl.program_id(0)
        prev = _prev_slots()
        scale = HEAD_DIM ** -0.5
        heads = range(N_KV_HEADS)
        qs, dos = [_stack_heads(q_ref, h) for h in heads], [_stack_heads(do_ref, h) for h in heads]
        kbands, vbands = [_band(kp_ref, kc_ref, h) for h in heads], [_band(vp_ref, vc_ref, h) for h in heads]
        s_bands = [lax.dot_general(kbands[h], qs[h], NT_DIMS, preferred_element_type=F32) for h in heads]
        dp_bands = [lax.dot_general(vbands[h], dos[h], NT_DIMS, preferred_element_type=F32) for h in heads]
        ds_bands, p_bands, parts = [], [], []
        for h in heads:
            p, ps = _attn_probs(s_bands[h], sink_ref[h, 0:1, :], prev, n > 0)
            dp = _pick(prev, dp_bands[h])
            delta = jnp.sum(p * dp, axis=0, keepdims=True)
            ds_bands.append(_spread(prev, p * (dp - delta) * scale))
            p_bands.append(_spread(prev, p))
            dsink = -(ps * delta)
            for g in range(GQA_GROUP):
                parts.append(jnp.broadcast_to(jnp.sum(dsink[:, g * WINDOW:(g + 1) * WINDOW], axis=1, keepdims=True), (8, LANES)))
        for h in heads:
            dk = jnp.dot(ds_bands[h], qs[h], preferred_element_type=F32).astype(BF16)
            dv = jnp.dot(p_bands[h], dos[h], preferred_element_type=F32).astype(BF16)
            dq = lax.dot_general(kbands[h], ds_bands[h], TN_DIMS, preferred_element_type=F32).T
            cols = slice(h * HEAD_DIM, (h + 1) * HEAD_DIM)
            dkp_ref[:, cols], dkc_ref[:, cols] = dk[:WINDOW], dk[WINDOW:]
            dvp_ref[:, cols], dvc_ref[:, cols] = dv[:WINDOW], dv[WINDOW:]
            for g in range(GQA_GROUP):
                head = GQA_GROUP * h + g
                dq_ref[:, head * HEAD_DIM:(head + 1) * HEAD_DIM] = dq[g * WINDOW:(g + 1) * WINDOW].astype(BF16)

        @pl.when(n == 0)
        def _():
            for i, part in enumerate(parts):
                dsink_ref[i // GQA_GROUP, i % GQA_GROUP] = part

        @pl.when(n > 0)
        def _():
            for i, part in enumerate(parts):
                dsink_ref[i // GQA_GROUP, i % GQA_GROUP] += part

    rows_q = pl.BlockSpec((WINDOW, Q_WIDTH), lambda n: (n, 0))
    rows_kv = pl.BlockSpec((WINDOW, KV_WIDTH), lambda n: (n, 0))
    kv_shape = jax.ShapeDtypeStruct((s, KV_WIDTH), BF16)
    core, rr = _call(
        body, grid=(s // WINDOW,), in_specs=_attn_specs(s // WINDOW) + [rows_q],
        out_specs=[rows_q, rows_kv, rows_kv, rows_kv, rows_kv,
                   pl.BlockSpec((N_KV_HEADS, GQA_GROUP, 8, LANES), lambda n: (0, 0, 0, 0))],
        out_shape=[jax.ShapeDtypeStruct((s, Q_WIDTH), BF16), kv_shape, kv_shape, kv_shape, kv_shape,
                   jax.ShapeDtypeStruct((N_KV_HEADS, GQA_GROUP, 8, LANES), F32)],
        operands=(qkv, qkv, qkv, qkv, qkv, sink_rows, do), sem=("arbitrary",), name=name, riders=riders)
    return _ret(core, rr, riders)


GELU_C = 0.7978845608028654
GELU_A = 0.044715


def _gelu(x):
    return 0.5 * x * (1.0 + jnp.tanh(x * (GELU_C + (GELU_C * GELU_A) * (x * x))))


def _gelu_and_grad(x):
    x2 = x * x
    t = jnp.tanh(x * (GELU_C + (GELU_C * GELU_A) * x2))
    half_x, one_t = 0.5 * x, 1.0 + t
    return half_x * one_t, 0.5 * one_t + half_x * (1.0 - t * t) * (GELU_C + (3.0 * GELU_C * GELU_A) * x2)


def _tril_bf16(w):
    row = lax.broadcasted_iota(I32, (SGU_CHUNK, SGU_CHUNK), 0)
    col = lax.broadcasted_iota(I32, (SGU_CHUNK, SGU_CHUNK), 1)
    return jnp.where(row >= col, w, 0.0).astype(BF16)


def _sgu_norm(vg, g, b):
    mu = jnp.mean(vg, axis=-1, keepdims=True)
    cen = vg - mu
    rstd = lax.rsqrt(jnp.mean(cen * cen, axis=-1, keepdims=True) + EPS)
    xhat = cen * rstd
    return xhat, rstd, xhat * g + b


def sgu_in_fwd(h, w_in, ln_g, ln_b, w_sp, b_sp, *, name, tm=512, riders=()):
    s, k = h.shape
    ns = w_in.shape[2]
    tm = _row_tile(s, tm)

    def body(h_ref, w0, w1, w2, w3, g_ref, b_ref, w_ref, bs_ref, z_ref, y_ref):
        hv = h_ref[...]
        zs = [jnp.dot(hv, w_ref_j[...], preferred_element_type=F32) for w_ref_j in (w0, w1, w2, w3)]
        for j, zj in enumerate(zs):
            z_ref[:, j * ns:(j + 1) * ns] = zj.astype(BF16)
        u = _gelu(jnp.concatenate(zs[:2], axis=1))
        _, _, vn = _sgu_norm(_gelu(jnp.concatenate(zs[2:], axis=1)), g_ref[...], b_ref[...])
        vn = vn.astype(BF16)
        for grp in range(SGU_GROUPS):
            w = _tril_bf16(w_ref[grp])
            cols = slice(grp * LANES, (grp + 1) * LANES)
            for ch in range(tm // SGU_CHUNK):
                rows = slice(ch * SGU_CHUNK, (ch + 1) * SGU_CHUNK)
                mixed = jnp.dot(w, vn[rows, cols], preferred_element_type=F32) + bs_ref[grp]
                y_ref[rows, cols] = (u[rows, cols] * mixed).astype(BF16)

    def shard(j):
        return pl.BlockSpec((None, k, ns), lambda i: (j, 0, 0))

    full3 = pl.BlockSpec((SGU_GROUPS, SGU_CHUNK, SGU_CHUNK), lambda i: (0, 0, 0))
    core, rr = _call(
        body, grid=(s // tm,),
        in_specs=[_row_spec(tm, k)] + [shard(j) for j in range(N_CHIPS)] + [_vec_spec(D_MODEL), _vec_spec(D_MODEL), full3, full3],
        out_specs=[_row_spec(tm, 2 * D_MODEL), _row_spec(tm, D_MODEL)],
        out_shape=[jax.ShapeDtypeStruct((s, 2 * D_MODEL), BF16), jax.ShapeDtypeStruct((s, D_MODEL), BF16)],
        operands=(h, w_in, w_in, w_in, w_in, ln_g, ln_b, w_sp, b_sp), sem=("parallel",), name=name, riders=riders)
    return _ret(core, rr, riders)


def sgu_bwd(z, dy, ln_g, ln_b, w_sp, b_sp, *, name, tm=256, riders=()):
    s = z.shape[0]
    tm = _row_tile(s, tm)

    def body(z_ref, dy_ref, g_ref, b_ref, w_ref, bs_ref, dz_ref, dw_ref, dbs_ref, dg_ref, db_ref, dvn_buf):
        first = pl.program_id(0) == 0
        u, u_grad = _gelu_and_grad(z_ref[:, :D_MODEL].astype(F32))
        vg, v_grad = _gelu_and_grad(z_ref[:, D_MODEL:].astype(F32))
        xhat, rstd, vn = _sgu_norm(vg, g_ref[...], b_ref[...])
        vn = vn.astype(BF16)
        dyv = dy_ref[...]
        dmixed = dyv * u
        dz_gate = dyv * u_grad
        row = lax.broadcasted_iota(I32, (SGU_CHUNK, SGU_CHUNK), 0)
        col = lax.broadcasted_iota(I32, (SGU_CHUNK, SGU_CHUNK), 1)
        dws, dbss = [], []
        for grp in range(SGU_GROUPS):
            w = _tril_bf16(w_ref[grp])
            cols = slice(grp * LANES, (grp + 1) * LANES)
            dw = jnp.zeros((SGU_CHUNK, SGU_CHUNK), F32)
            dbs = jnp.zeros((SGU_CHUNK, 1), F32)
            for ch in range(tm // SGU_CHUNK):
                rows = slice(ch * SGU_CHUNK, (ch + 1) * SGU_CHUNK)
                vblk = vn[rows, cols]
                mixed = jnp.dot(w, vblk, preferred_element_type=F32) + bs_ref[grp]
                dz_ref[rows, cols] = (dz_gate[rows, cols] * mixed).astype(BF16)
                dm = dmixed[rows, cols]
                dmb = dm.astype(BF16)
                dvn_buf[rows, cols] = lax.dot_general(w, dmb, TN_DIMS, preferred_element_type=F32)
                dw += lax.dot_general(dmb, vblk, NT_DIMS, preferred_element_type=F32)
                dbs += jnp.sum(dm, axis=-1, keepdims=True)
            dws.append(jnp.where(row >= col, dw, 0.0))
            dbss.append(jnp.broadcast_to(dbs, (SGU_CHUNK, SGU_CHUNK)))

        dvn = dvn_buf[...]
        dxhat = dvn * g_ref[...]
        dvg = rstd * (dxhat - jnp.mean(dxhat, axis=-1, keepdims=True) - xhat * jnp.mean(dxhat * xhat, axis=-1, keepdims=True))
        dz_ref[:, D_MODEL:] = (dvg * v_grad).astype(BF16)
        dlng, dlnb = jnp.sum(dvn * xhat, axis=0, keepdims=True), jnp.sum(dvn, axis=0, keepdims=True)

        @pl.when(first)
        def _():
            for grp in range(SGU_GROUPS):
                dw_ref[grp] = dws[grp]
                dbs_ref[grp] = dbss[grp]
            dg_ref[...] = dlng
            db_ref[...] = dlnb

        @pl.when(jnp.logical_not(first))
        def _():
            for grp in range(SGU_GROUPS):
                dw_ref[grp] += dws[grp]
                dbs_ref[grp] += dbss[grp]
            dg_ref[...] += dlng
            db_ref[...] += dlnb

    full3 = pl.BlockSpec((SGU_GROUPS, SGU_CHUNK, SGU_CHUNK), lambda i: (0, 0, 0))
    s3 = jax.ShapeDtypeStruct((SGU_GROUPS, SGU_CHUNK, SGU_CHUNK), F32)
    vshape = jax.ShapeDtypeStruct((1, D_MODEL), F32)
    core, rr = _call(
        body, grid=(s // tm,),
        in_specs=[_row_spec(tm, 2 * D_MODEL), _row_spec(tm, D_MODEL), _vec_spec(D_MODEL), _vec_spec(D_MODEL), full3, full3],
        out_specs=[_row_spec(tm, 2 * D_MODEL), full3, full3, _vec_spec(D_MODEL), _vec_spec(D_MODEL)],
        out_shape=[jax.ShapeDtypeStruct((s, 2 * D_MODEL), BF16), s3, s3, vshape, vshape],
        scratch_shapes=[pltpu.VMEM((tm, D_MODEL), F32)], operands=(z, dy, ln_g, ln_b, w_sp, b_sp), name=name, riders=riders)
    return _ret(core, rr, riders)


def _sigmoid(x):
    return 1.0 / (1.0 + jnp.exp(-x))


def ffn_up(h, w_gu, *, name, tm=512, riders=()):
    s = h.shape[0]
    tm = _row_tile(s, tm)

    def body(h_ref, wg_ref, wu_ref, d_ref, a_ref):
        hv = h_ref[...]
        sub = min(256, tm)
        for t in range(tm // sub):
            rows = slice(t * sub, (t + 1) * sub)
            g = jnp.dot(hv[rows], wg_ref[...], preferred_element_type=F32)
            u = jnp.dot(hv[rows], wu_ref[...], preferred_element_type=F32)
            sig = _sigmoid(g)
            silu = g * sig
            d_ref[0, rows, :] = (u * (sig + silu * (1.0 - sig))).astype(BF16)
            d_ref[1, rows, :] = silu.astype(BF16)
            a_ref[rows, :] = (silu * u).astype(BF16)

    core, rr = _call(
        body, grid=(2, s // tm),
        in_specs=[pl.BlockSpec((tm, D_MODEL), lambda j, i: (i, 0)),
                  pl.BlockSpec((None, D_MODEL, FF_HALF), lambda j, i: (j, 0, 0)),
                  pl.BlockSpec((None, D_MODEL, FF_HALF), lambda j, i: (j + 2, 0, 0))],
        out_specs=[pl.BlockSpec((2, tm, FF_HALF), lambda j, i: (0, i, j)), pl.BlockSpec((tm, FF_HALF), lambda j, i: (i, j))],
        out_shape=[jax.ShapeDtypeStruct((2, s, D_FF), BF16), jax.ShapeDtypeStruct((s, D_FF), BF16)],
        operands=(h, w_gu, w_gu), sem=("parallel", "parallel"), name=name, riders=riders)
    return _ret(core, rr, riders)


def _weight_tile(rows):
    for tr in (512, 352, 256, 128):
        if rows % tr == 0:
            return tr
    return rows


def place_shard(w, layer, chip_arr, dtype, *, name, riders=()):
    _, r, c = w.shape
    tr = _weight_tile(r)

    def body(chip_ref, w_ref, o_ref):
        o_ref[...] = w_ref[...].astype(dtype)

    core, rr = _call(
        body, grid=(r // tr,), prefetch=(chip_arr,),
        in_specs=[pl.BlockSpec((None, tr, c), lambda i, chip: (layer, i, 0))],
        out_specs=[pl.BlockSpec((None, tr, c), lambda i, chip: (chip[0], i, 0))],
        out_shape=[jax.ShapeDtypeStruct((N_CHIPS, r, c), dtype)], operands=(w,), sem=("parallel",), name=name, riders=riders)
    return _ret(core, rr, riders)


def _adamw_math(w, g, m, v):
    m = ADAM_B1 * m + (1.0 - ADAM_B1) * g
    v = ADAM_B2 * v + (1.0 - ADAM_B2) * (g * g)
    m_hat = m / (1.0 - ADAM_B1 ** ADAM_STEP)
    v_hat = v / (1.0 - ADAM_B2 ** ADAM_STEP)
    delta = -ADAM_LR * (m_hat / (jnp.sqrt(v_hat) + ADAM_EPS) + ADAM_WD * w)
    return delta, m, v


def adamw(w, g, m, v, *, name, after=None):
    nl, r, c = w.shape
    tr = _weight_tile(r)

    def body(w_ref, g_ref, m_ref, v_ref, *rest):
        go_ref, d_ref, mo_ref, vo_ref = rest[-4:]
        gv = g_ref[...]
        go_ref[...] = gv
        d_ref[...], mo_ref[...], vo_ref[...] = _adamw_math(w_ref[...], gv, m_ref[...], v_ref[...])

    spec = pl.BlockSpec((None, tr, c), lambda l, i: (l, i, 0))
    shape = jax.ShapeDtypeStruct(w.shape, F32)
    extra = [] if after is None else [after]
    outs, _ = _call(body, grid=(nl, r // tr), in_specs=[spec] * 4 + [ANY] * len(extra), out_specs=[spec] * 4,
                    out_shape=[shape] * 4, operands=(w, g, m, v, *extra), sem=("parallel", "parallel"), name=name)
    return outs


def adamw_small(ws, gs, ms, vs, *, name):
    n = len(ws)

    def body(*refs):
        ins, outs = refs[:4 * n], refs[4 * n:]
        for t in range(n):
            gv = ins[n + t][...]
            outs[t][...] = gv
            outs[n + t][...], outs[2 * n + t][...], outs[3 * n + t][...] = _adamw_math(
                ins[t][...], gv, ins[2 * n + t][...], ins[3 * n + t][...])

    shapes = [jax.ShapeDtypeStruct(w.shape, F32) for w in ws]
    res = pl.pallas_call(body, out_shape=shapes * 4, name=name)(*ws, *gs, *ms, *vs)
    return res[:n], res[n:2 * n], res[2 * n:3 * n], res[3 * n:]


def pair_add(g, r1, c_arr, *, name):
    _, rows, cdim = g.shape
    h = rows // 2

    def body(c_ref, g_ref, r_ref, o_ref):
        o_ref[...] = (g_ref[...].astype(F32) + r_ref[...].astype(F32)).astype(o_ref.dtype)

    (out,), _ = _call(
        body, grid=(N_CHIPS,), prefetch=(c_arr,),
        in_specs=[pl.BlockSpec((None, h, cdim), lambda s, c: (s, c[0], 0)), pl.BlockSpec((None, h, cdim), lambda s, c: (s, 0, 0))],
        out_specs=[pl.BlockSpec((None, h, cdim), lambda s, c: (s, 0, 0))],
        out_shape=[jax.ShapeDtypeStruct((N_CHIPS, h, cdim), g.dtype)], operands=(g, r1), sem=("parallel",), name=name)
    return out


def final_add(g, r1, r2, jc_arr, *, dest_shape, lead, prev, name):
    _, rows, cdim = g.shape
    h = rows // 2

    def body(jc_ref, g_ref, r1_ref, r2_ref, *rest):
        o_ref = rest[-1]
        acc = g_ref[...].astype(F32) + r1_ref[...].astype(F32)
        for k in range(3):
            acc = acc + r2_ref[k].astype(F32)
        o_ref[...] = acc

    if lead is None:
        o_spec = pl.BlockSpec((h, cdim), lambda i, jc: (jc[1], 0))
    elif lead == "chip":
        o_spec = pl.BlockSpec((None, h, cdim), lambda i, jc: (jc[0], jc[1], 0))
    else:
        o_spec = pl.BlockSpec((None, h, cdim), lambda i, jc: (lead, jc[1], 0))
    in_specs = [pl.BlockSpec((None, h, cdim), lambda i, jc: (jc[0], jc[1], 0)),
                pl.BlockSpec((None, h, cdim), lambda i, jc: (jc[0], 0, 0)),
                pl.BlockSpec((3, h, cdim), lambda i, jc: (0, 0, 0))]
    operands = [g, r1, r2]
    aliases = None
    if prev is not None:
        in_specs.append(ANY)
        operands.append(prev)
        aliases = {3: 0}
    (out,), _ = _call(body, grid=(1,), prefetch=(jc_arr,), in_specs=in_specs, out_specs=[o_spec],
                      out_shape=[jax.ShapeDtypeStruct(dest_shape, F32)], operands=operands, aliases=aliases, name=name)
    return out


def _place():
    return lax.axis_index("x"), lax.axis_index("y"), lax.axis_index("c")


def _partner(x, y, k):
    return (1 - x if k >> 1 else x), (1 - y if k & 1 else y)


WHOLE = (0, 1, 1)


def _half(rows, sel, dtype, piece=WHOLE):
    lo, hi, n = piece
    align = 16 if dtype == BF16 else 8
    step = rows // 2 // n
    assert rows // 2 == step * n and step % align == 0
    return pl.ds(pl.multiple_of(sel * (rows // 2) + lo * step, align), (hi - lo) * step)


def _rider(peers, inputs, aliased, fresh, nsem, copies, arrivals):
    def start(ins, outs, send, recv):
        for cp in copies(ins, outs, send, recv):
            cp.start()

    def finish(ins, outs, send, recv):
        for cp in arrivals(ins, outs, send, recv):
            cp.wait_recv()
        for cp in copies(ins, outs, send, recv):
            cp.wait_send()

    return types.SimpleNamespace(peers=peers, inputs=list(inputs), aliased=list(aliased), fresh=list(fresh), nsem=nsem,
                                 start=start, finish=finish)


def _remote(src, dst, send, recv, idx, dev):
    return pltpu.make_async_remote_copy(src_ref=src, dst_ref=dst, send_sem=send.at[idx], recv_sem=recv.at[idx],
                                        device_id=dev, device_id_type=MESH)


def gather_ici_rider(fulls, pieces=None):
    nt = len(fulls)
    pieces = pieces or [WHOLE] * nt

    def region(outs, t, slot, sel):
        return outs[t].at[slot, _half(fulls[t].shape[1], sel, fulls[t].dtype, pieces[t])]

    def copies(ins, outs, send, recv):
        x, y, c = _place()
        res = []
        for t in range(nt):
            for k in (1, 2, 3):
                px, py = _partner(x, y, k)
                mine = region(outs, t, 2 * x + y, c)
                res.append(_remote(mine, mine, send, recv, 3 * t + k - 1, (px, py, c)))
        return res

    def arrivals(ins, outs, send, recv):
        x, y, c = _place()
        res = []
        for t in range(nt):
            for k in (1, 2, 3):
                px, py = _partner(x, y, k)
                theirs = region(outs, t, 2 * px + py, c)
                res.append(_remote(theirs, theirs, send, recv, 3 * t + k - 1, (x, y, c)))
        return res

    return _rider("chips", fulls, range(nt), [], 3 * nt, copies, arrivals)


def gather_d2d_rider(fulls, pieces=None):
    nt = len(fulls)
    pieces = pieces or [WHOLE] * nt

    def region(outs, t, slot, sel):
        return outs[t].at[slot, _half(fulls[t].shape[1], sel, fulls[t].dtype, pieces[t])]

    def both(outs, send, recv, mine):
        x, y, c = _place()
        res = []
        for t in range(nt):
            for k in (1, 2, 3):
                px, py = _partner(x, y, k)
                part = region(outs, t, 2 * px + py, c if mine else 1 - c)
                res.append(_remote(part, part, send, recv, 3 * t + k - 1, (x, y, 1 - c)))
        return res

    return _rider("sibling", fulls, range(nt), [], 3 * nt, lambda i, o, s, r: both(o, s, r, True),
                  lambda i, o, s, r: both(o, s, r, False))


def exchange_rider(grads):
    nt = len(grads)

    def both(ins, outs, send, recv):
        x, y, c = _place()
        return [_remote(ins[t].at[:, _half(grads[t].shape[1], 1 - c, grads[t].dtype)], outs[t], send, recv, t, (x, y, 1 - c))
                for t in range(nt)]

    fresh = [jax.ShapeDtypeStruct((N_CHIPS, g.shape[1] // 2, g.shape[2]), g.dtype) for g in grads]
    return _rider("sibling", grads, [], fresh, nt, both, both)


def scatter_rider(parts):
    nt = len(parts)

    def both(ins, outs, send, recv):
        x, y, c = _place()
        res = []
        for t in range(nt):
            for k in (1, 2, 3):
                px, py = _partner(x, y, k)
                res.append(_remote(ins[t].at[2 * px + py], outs[t].at[k - 1], send, recv, 3 * t + k - 1, (px, py, c)))
        return res

    fresh = [jax.ShapeDtypeStruct((3,) + p.shape[1:], p.dtype) for p in parts]
    return _rider("chips", parts, [], fresh, 3 * nt, both, both)


def broadcast_rider(bufs, items):
    def region(outs, item, sel):
        bi, lead = item
        ref = outs[bi]
        if lead == "chip":
            x, y, _ = _place()
            ref = ref.at[2 * x + y]
        elif lead is not None:
            ref = ref.at[lead]
        return ref.at[_half(ref.shape[0], sel, F32)]

    def both(outs, send, recv, mine):
        x, y, c = _place()
        res = []
        for i, item in enumerate(items):
            part = region(outs, item, c if mine else 1 - c)
            res.append(_remote(part, part, send, recv, i, (x, y, 1 - c)))
        return res

    return _rider("sibling", bufs, range(len(bufs)), [], len(items), lambda i, o, s, r: both(o, s, r, True),
                  lambda i, o, s, r: both(o, s, r, False))


def allcast_rider(buf):
    peers = [(k, flip) for k in range(N_CHIPS) for flip in (0, 1) if (k, flip) != (0, 0)]

    def both(outs, send, recv, mine):
        x, y, c = _place()
        res = []
        for i, (k, flip) in enumerate(peers):
            px, py = _partner(x, y, k)
            pc = 1 - c if flip else c
            slot, sel = (2 * x + y, c) if mine else (2 * px + py, pc)
            part = outs[0].at[slot, _half(buf.shape[1], sel, F32)]
            res.append(_remote(part, part, send, recv, i, (px, py, pc)))
        return res

    return _rider("everyone", [buf], [0], [], len(peers), lambda i, o, s, r: both(o, s, r, True),
                  lambda i, o, s, r: both(o, s, r, False))


def comm_call(riders, *, name):
    _, res = _call(None, riders=riders, name=name)
    return res


SEMS = pl.BlockSpec(memory_space=pltpu.SEMAPHORE)
SIDE_EFFECT = pltpu.SideEffectType.DATAFLOW_SIDE_EFFECTING


def _split_refs(riders, refs):
    views, p = [], 0
    for r in riders:
        bufs = refs[p:p + len(r.inputs) + len(r.fresh)]
        p += len(bufs)
        ins = bufs[:len(r.inputs)]
        views.append([ins, [ins[i] for i in r.aliased] + list(bufs[len(r.inputs):])])
    for view in views:
        view += [refs[p], refs[p + 1]]
        p += 2
    return views


def comm_start(riders, *, name):
    kind = _peer_kind(riders)
    bufs = [a for r in riders for a in r.inputs]
    fresh = [f for r in riders for f in r.fresh]
    n_buf, n_fresh = len(bufs), len(fresh)

    def body(*refs):
        ins, outs = refs[:n_buf], refs[n_buf:]
        through, land, sems = outs[:n_buf], outs[n_buf:n_buf + n_fresh], outs[n_buf + n_fresh:-1]
        _peer_barrier(kind)
        per_rider, pb, pf = [], 0, 0
        for r in riders:
            per_rider += list(through[pb:pb + len(r.inputs)]) + list(land[pf:pf + len(r.fresh)])
            pb, pf = pb + len(r.inputs), pf + len(r.fresh)
        for r, (r_ins, r_outs, send, recv) in zip(riders, _split_refs(riders, per_rider + list(sems))):
            r.start(r_ins, r_outs, send, recv)
        outs[-1][...] = jnp.zeros((8, LANES), F32)

    sem_shapes = [pltpu.SemaphoreType.DMA((r.nsem,)) for r in riders for _ in (0, 1)]
    res = pl.pallas_call(
        body, name=name, in_specs=[ANY] * n_buf,
        out_specs=[ANY] * (n_buf + n_fresh) + [SEMS] * len(sem_shapes) + [pl.BlockSpec(memory_space=pltpu.VMEM)],
        out_shape=[jax.ShapeDtypeStruct(a.shape, a.dtype) for a in bufs] + fresh + sem_shapes
        + [jax.ShapeDtypeStruct((8, LANES), F32)],
        input_output_aliases={i: i for i in range(n_buf)},
        compiler_params=pltpu.CompilerParams(has_side_effects=SIDE_EFFECT, collective_id=PEER_KINDS.index(kind)))(*bufs)
    return (riders, list(res[:n_buf + n_fresh]), list(res[n_buf + n_fresh:-1])), res[-1]


def comm_wait(state, after, *, name):
    riders, bufs, sems = state
    n_buf, n_sem = len(bufs), len(sems)
    n_in = sum(len(r.inputs) for r in riders)

    def body(*refs):
        held, sem_refs = refs[:n_buf], refs[n_buf:n_buf + n_sem]
        through, land = held[:n_in], held[n_in:]
        per_rider, pb, pf = [], 0, 0
        for r in riders:
            per_rider += list(through[pb:pb + len(r.inputs)]) + list(land[pf:pf + len(r.fresh)])
            pb, pf = pb + len(r.inputs), pf + len(r.fresh)
        for r, (r_ins, r_outs, send, recv) in zip(riders, _split_refs(riders, per_rider + list(sem_refs))):
            r.finish(r_ins, r_outs, send, recv)

    res = pl.pallas_call(
        body, name=name, in_specs=[ANY] * n_buf + [SEMS] * n_sem + [ANY], out_specs=[ANY] * n_buf,
        out_shape=[jax.ShapeDtypeStruct(a.shape, a.dtype) for a in bufs],
        input_output_aliases={i: i for i in range(n_buf)},
        compiler_params=pltpu.CompilerParams(has_side_effects=SIDE_EFFECT))(*bufs, *sems, after)
    through, land = list(res[:n_in]), list(res[n_in:])
    out, pb, pf = [], 0, 0
    for r in riders:
        r_ins, r_land = through[pb:pb + len(r.inputs)], land[pf:pf + len(r.fresh)]
        pb, pf = pb + len(r.inputs), pf + len(r.fresh)
        out.append([r_ins[i] for i in r.aliased] + r_land)
    return out


SLAB_ROWS = 192


def _pad_rows(a, rows=8):
    return jnp.pad(a, ((0, rows - a.shape[0]), (0, 0)))


def _pack_small(norm_grads, db_qkv, db_o, dsinks, db_sp, dln_g, dln_b, dw_sp, loss_part):
    parts = [
        jnp.concatenate(norm_grads, axis=0),
        _pad_rows(jnp.pad(db_qkv, ((0, 0), (0, 2 * D_MODEL - QKV_WIDTH))).reshape(2, D_MODEL)),
        _pad_rows(db_o),
        _pad_rows(jnp.pad(dsinks.reshape(1, N_Q_HEADS), ((0, 0), (0, D_MODEL - N_Q_HEADS)))),
        _pad_rows(db_sp.reshape(1, D_MODEL)),
        _pad_rows(jnp.concatenate([dln_g, dln_b, jnp.pad(loss_part[0:1], ((0, 0), (0, D_MODEL - LANES)))], axis=0)),
        dw_sp.reshape(SGU_CHUNK, D_MODEL),
    ]
    slab = jnp.concatenate(parts, axis=0)
    return jnp.pad(slab, ((0, SLAB_ROWS - slab.shape[0]), (0, 0))).reshape(N_CHIPS, SLAB_ROWS // N_CHIPS, D_MODEL)


def _unpack_small(slab, j):
    slab = slab.reshape(SLAB_ROWS, D_MODEL)
    norms = [slab[2 * i:2 * i + 2] for i in range(4)]
    db_qkv = slab[8:10].reshape(1, 2 * D_MODEL)[:, :QKV_WIDTH]
    db_o = slab[16:17]
    dsinks = slab[24:25, :N_Q_HEADS]
    db_sp = slab[32:33].reshape(SGU_GROUPS, SGU_CHUNK)
    width = D_MODEL // N_CHIPS
    dln_g = lax.dynamic_slice(slab[40:41], (0, j * width), (1, width))
    dln_b = lax.dynamic_slice(slab[41:42], (0, j * width), (1, width))
    dw_sp = slab[48:48 + SGU_CHUNK].reshape(SGU_GROUPS * SGU_CHUNK, SGU_CHUNK)
    return norms, db_qkv, db_o, dsinks, db_sp, dln_g, dln_b, dw_sp, slab[42, 0]


class _GradReduce:
    def __init__(self, c_arr, jc_arr, dest_shapes):
        self.c_arr, self.jc_arr, self.dest_shapes = c_arr, jc_arr, dest_shapes
        self.grad, self.sibling, self.pair, self.chips, self.dest = {}, {}, {}, {}, {}

    def exchange(self, tags):
        return exchange_rider([self.grad[t] for t in tags])

    def exchanged(self, tags, res):
        for t, r in zip(tags, res):
            self.sibling[t] = r
            self.pair[t] = pair_add(self.grad[t], r, self.c_arr, name=f"pair_add_{t}")

    def scatter(self, tags):
        return scatter_rider([self.pair[t] for t in tags])

    def scattered(self, tags, res, where):
        for t, r in zip(tags, res):
            name, lead = where[t]
            self.dest[name] = final_add(self.grad[t], self.sibling[t], r, self.jc_arr, dest_shape=self.dest_shapes[name],
                                        lead=lead, prev=self.dest.get(name), name=f"final_add_{t}")

    def broadcast(self, items):
        names = []
        for n, _ in items:
            if n not in names:
                names.append(n)
        return names, broadcast_rider([self.dest[n] for n in names], [(names.index(n), lead) for n, lead in items])

    def broadcasted(self, names, res):
        for n, r in zip(names, res):
            self.dest[n] = r


def kernel(x, norm_mix_pre, norm_mix_post, norm_ffn_pre, norm_ffn_post, attn_w_qkv, attn_b_qkv, attn_sinks, attn_w_o, attn_b_o, sgu_w_in, sgu_ln_g, sgu_ln_b, sgu_w_spatial, sgu_b_spatial, sgu_w_out, ffn_w_gate_up, ffn_w_down, loss_target, m_norm_mix_pre, m_norm_mix_post, m_norm_ffn_pre, m_norm_ffn_post, m_attn_w_qkv, m_attn_b_qkv, m_attn_sinks, m_attn_w_o, m_attn_b_o, m_sgu_w_in, m_sgu_ln_g, m_sgu_ln_b, m_sgu_w_spatial, m_sgu_b_spatial, m_sgu_w_out, m_ffn_w_gate_up, m_ffn_w_down, v_norm_mix_pre, v_norm_mix_post, v_norm_ffn_pre, v_norm_ffn_post, v_attn_w_qkv, v_attn_b_qkv, v_attn_sinks, v_attn_w_o, v_attn_b_o, v_sgu_w_in, v_sgu_ln_g, v_sgu_ln_b, v_sgu_w_spatial, v_sgu_b_spatial, v_sgu_w_out, v_ffn_w_gate_up, v_ffn_w_down):
    s = x.shape[1]
    x0 = x.reshape(s, D_MODEL)
    target = loss_target.reshape(s, D_MODEL)
    mx, my, mc = lax.axis_index("x"), lax.axis_index("y"), lax.axis_index("c")
    chip = 2 * mx + my
    chip_arr = jnp.reshape(chip, (1,)).astype(I32)
    c_arr = jnp.reshape(mc, (1,)).astype(I32)
    jc_arr = jnp.stack([chip, mc]).astype(I32)
    zero_bias = jnp.zeros((1, D_MODEL), F32)

    def gain(p, i):
        return p[i:i + 1]

    big = [attn_w_qkv, attn_w_o, sgu_w_in, sgu_w_out, ffn_w_gate_up, ffn_w_gate_up, ffn_w_down, ffn_w_down]
    layers = [0, 0, 0, 0, 0, 1, 0, 1]
    tags = ["qkv", "wo", "win", "wout", "wgu0", "wgu1", "wd0", "wd1"]
    full = {t: place_shard(w, l, chip_arr, BF16, name=f"place_{t}") for w, l, t in zip(big, layers, tags)
            if t in ("qkv", "wo")}
    ln_pack = _pad_rows(jnp.concatenate([sgu_ln_g, sgu_ln_b], axis=0), 16)[None]
    full["ln"] = place_shard(ln_pack, 0, chip_arr, F32, name="place_ln")

    def split(items):
        return [i if isinstance(i, str) else i[0] for i in items], [WHOLE if isinstance(i, str) else tuple(i[1:]) for i in items]

    def ici(*items):
        names, pieces = split(items)
        return gather_ici_rider([full[n] for n in names], pieces)

    def d2d(*items):
        names, pieces = split(items)
        return gather_d2d_rider([full[n] for n in names], pieces)

    def landed(items, res):
        for n, r in zip(split(items)[0], res):
            full[n] = r

    cos, sin = _rope_tables(s)
    sink_rows = jnp.broadcast_to(
        jnp.repeat(attn_sinks.reshape(N_KV_HEADS, GQA_GROUP), WINDOW, axis=1)[:, None, :], (N_KV_HEADS, 8, ROWS))
    w_sp = sgu_w_spatial.reshape(SGU_GROUPS, SGU_CHUNK, SGU_CHUNK)
    b_sp = jnp.broadcast_to(sgu_b_spatial.reshape(SGU_GROUPS, SGU_CHUNK)[:, :, None], (SGU_GROUPS, SGU_CHUNK, LANES))

    (h0, full["wgu0"]), (res,) = prenorm_and_place(x0, gain(norm_mix_pre, 0), ffn_w_gate_up, 0, chip_arr, name="prenorm_0",
                                                   riders=[ici("qkv", "ln")])
    landed(("qkv", "ln"), res)
    full["wgu1"], (res,) = place_shard(ffn_w_gate_up, 1, chip_arr, BF16, name="place_wgu1", riders=[d2d("qkv", "ln")])
    landed(("qkv", "ln"), res)
    ln_g = full["ln"][:, 0, :].reshape(1, D_MODEL)
    ln_b = full["ln"][:, 1, :].reshape(1, D_MODEL)

    def hosted(call, stages):
        outputs, results = call([{"ici": ici, "d2d": d2d}[kind](*items) for kind, items in stages])
        for (_, items), res in zip(stages, results):
            landed(items, res)
        return outputs

    casts = [(ffn_w_down, 0), (sgu_w_in, 0), (ffn_w_down, 1), (sgu_w_out, 0)]
    qkv, full["wd0"], full["win"], full["wd1"], full["wout"] = hosted(
        lambda r: qkv_proj(h0, full["qkv"], attn_b_qkv, cos, sin, casts, chip_arr, name="qkv_proj", riders=r),
        [("ici", ("wo", ("wgu0", 0, 3, 8)))])
    o = hosted(lambda r: attn_fwd(qkv, sink_rows, name="attn_fwd", riders=r),
               [("d2d", ("wo",)), ("ici", (("wgu0", 3, 8, 8), ("wd0", 0, 4, 11)))])
    w_o = full["wo"].reshape(Q_WIDTH, D_MODEL)
    x1, h1, m0 = hosted(lambda r: proj_residual_norm(o, w_o, x0, attn_b_o, gain(norm_mix_post, 0), gain(norm_ffn_pre, 0),
                                                     name="attn_out_norm", riders=r),
                        [("d2d", ("wgu0",)), ("ici", (("wd0", 4, 11, 11),))])
    gu0, a0 = hosted(lambda r: ffn_up(h1, full["wgu0"], name="ffn_up_0", riders=r),
                     [("d2d", ("wd0",)), ("ici", ("win", "wout", ("wgu1", 0, 4, 8)))])
    w_d0 = full["wd0"].reshape(D_FF, D_MODEL)
    x2, h2, f0 = hosted(lambda r: proj_residual_norm(a0, w_d0, x1, zero_bias, gain(norm_ffn_post, 0), gain(norm_mix_pre, 1),
                                                     name="ffn_down_norm_0", riders=r),
                        [("d2d", ("win", "wout")), ("ici", (("wgu1", 4, 8, 8),))])
    w_in = full["win"]
    z, y = hosted(lambda r: sgu_in_fwd(h2, w_in, ln_g, ln_b, w_sp, b_sp, name="sgu_in_fwd", riders=r),
                  [("d2d", ("wgu1",)), ("ici", ("wd1",))])
    w_out = full["wout"].reshape(D_MODEL, D_MODEL)
    x3, h3, m1 = hosted(lambda r: proj_residual_norm(y, w_out, x2, zero_bias, gain(norm_mix_post, 1), gain(norm_ffn_pre, 1),
                                                     name="sgu_out_norm", riders=r),
                        [("d2d", ("wd1",))])
    w_qkv, w_gu0, w_gu1 = full["qkv"], full["wgu0"], full["wgu1"]
    w_d1 = full["wd1"].reshape(D_FF, D_MODEL)
    gu1, a1, dx4, df1, dg_fpost1, loss_part = ffn_fwd_loss_rows(
        h3, w_gu1, w_d1, x3, gain(norm_ffn_post, 1), target, name="ffn_fwd_loss_rows")

    red = _GradReduce(c_arr, jc_arr, {
        "qkv": attn_w_qkv.shape[1:], "wo": attn_w_o.shape[1:], "win": sgu_w_in.shape[1:], "wout": sgu_w_out.shape[1:],
        "wgu": ffn_w_gate_up.shape, "wd": ffn_w_down.shape, "slab": (N_CHIPS, SLAB_ROWS // N_CHIPS, D_MODEL)})
    where = {"qkv": ("qkv", None), "wo": ("wo", None), "win": ("win", None), "wout": ("wout", None), "wgu0": ("wgu", 0),
             "wgu1": ("wgu", 1), "wd0": ("wd", 0), "wd1": ("wd", 1), "small": ("slab", "chip")}

    dgu1, dx3, dm1, dg_fpre1, dg_mpost1, _ = ffn_bwd_rows(
        df1, w_d1, gu1, w_gu1, dx4, x3, gain(norm_ffn_pre, 1), m1, gain(norm_mix_post, 1), name="ffn_bwd_rows_1")
    red.grad["wd1"] = mm_tn(a1, df1, shard_major=False, tm=256, tn=D_MODEL, name="dw_down_1").reshape(
        N_CHIPS, D_FF // N_CHIPS, D_MODEL)
    red.grad["wgu1"], (res,) = mm_tn(h3, dgu1, shard_major=True, tm=512, tn=FF_HALF, name="dw_gate_up_1",
                                     riders=[red.exchange(["wd1"])])
    red.exchanged(["wd1"], res)
    dy, (res,) = mm_nt(dm1, w_out, out_dtype=F32, name="dy_sgu", riders=[red.exchange(["wgu1"])])
    red.exchanged(["wgu1"], res)
    red.grad["wout"] = mm_tn(y, dm1, shard_major=False, tm=512, tn=D_MODEL, name="dw_sgu_out").reshape(
        N_CHIPS, D_MODEL // N_CHIPS, D_MODEL)
    (dz, dw_sp, db_sp, dln_g, dln_b), (res_a, res_b) = sgu_bwd(
        z, dy, ln_g, ln_b, w_sp, b_sp, name="sgu_bwd", riders=[red.scatter(["wd1"]), red.exchange(["wout"])])
    red.scattered(["wd1"], res_a, where)
    red.exchanged(["wout"], res_b)
    names, rider = red.broadcast([("wd", 1)])
    red.grad["win"], (res_a, res_b) = mm_tn(h2, dz, shard_major=True, tm=D_MODEL, tn=2 * D_MODEL // N_CHIPS, name="dw_sgu_in",
                                            riders=[rider, red.scatter(["wout"])])
    red.broadcasted(names, res_a)
    red.scattered(["wout"], res_b, where)
    names, rider = red.broadcast([("wout", None)])
    (dx2, df0, dg_mpre1, dg_fpost0, _), (res_a, res_b) = dh_norm_bwd_pair(
        dz, w_in, dx3, x2, gain(norm_mix_pre, 1), f0, gain(norm_ffn_post, 0), name="dh_sgu_norm",
        riders=[red.exchange(["win"]), rider])
    red.exchanged(["win"], res_a)
    red.broadcasted(names, res_b)
    (dgu0, dx1, dm0, dg_fpre0, dg_mpost0, db_o), (res,) = ffn_bwd_rows(
        df0, w_d0, gu0, w_gu0, dx2, x1, gain(norm_ffn_pre, 0), m0, gain(norm_mix_post, 0), name="ffn_bwd_rows_0",
        riders=[red.scatter(["wgu1", "win"])])
    red.scattered(["wgu1", "win"], res, where)
    names, rider = red.broadcast([("wgu", 1), ("win", None)])
    dw_d0, (res,) = mm_tn(a0, df0, shard_major=False, tm=256, tn=D_MODEL, name="dw_down_0", riders=[rider])
    red.broadcasted(names, res)
    red.grad["wd0"] = dw_d0.reshape(N_CHIPS, D_FF // N_CHIPS, D_MODEL)
    do, (res,) = mm_nt(dm0, w_o, out_dtype=BF16, name="do_attn", riders=[red.exchange(["wd0"])])
    red.exchanged(["wd0"], res)
    red.grad["wgu0"], (res,) = mm_tn(h1, dgu0, shard_major=True, tm=512, tn=FF_HALF, name="dw_gate_up_0",
                                     riders=[red.scatter(["wd0"])])
    red.scattered(["wd0"], res, where)
    names, rider = red.broadcast([("wd", 0)])
    dw_o, (res_a, res_b) = mm_tn(o, dm0, shard_major=False, tm=512, tn=D_MODEL, name="dw_attn_out",
                                 riders=[red.exchange(["wgu0"]), rider])
    red.exchanged(["wgu0"], res_a)
    red.broadcasted(names, res_b)
    red.grad["wo"] = dw_o.reshape(N_CHIPS, Q_WIDTH // N_CHIPS, D_MODEL)
    (dq, dkc, dkp, dvc, dvp, dsink), (res_a, res_b) = attn_bwd(
        qkv, sink_rows, do, name="attn_bwd", riders=[red.scatter(["wgu0"]), red.exchange(["wo"])])
    red.scattered(["wgu0"], res_a, where)
    red.exchanged(["wo"], res_b)
    names, rider = red.broadcast([("wgu", 0)])
    (dqkv, db_qkv), (res,) = rope_bwd(dq, dkc, dkp, dvc, dvp, cos, sin, name="rope_bwd", riders=[rider])
    red.broadcasted(names, res)
    red.grad["qkv"], (res,) = mm_tn(h0, dqkv, shard_major=True, tm=D_MODEL, tn=QKV_WIDTH // N_CHIPS, name="dw_qkv",
                                    riders=[red.scatter(["wo"])])
    red.scattered(["wo"], res, where)
    grad_x, dg_mpre0 = dh_norm_bwd_last(dqkv, w_qkv, dx1, x0, gain(norm_mix_pre, 0), name="dh_attn_norm_in")

    norm_grads = [jnp.concatenate(p, axis=0) for p in
                  ((dg_mpre0, dg_mpre1), (dg_mpost0, dg_mpost1), (dg_fpre0, dg_fpre1), (dg_fpost0, dg_fpost1))]
    red.grad["small"] = _pack_small(norm_grads, db_qkv, db_o, dsink[:, :, 0, 0], db_sp[:, :, 0], dln_g, dln_b, dw_sp,
                                    loss_part)
    def big_update(w, g, m, v, tag, after=None):
        return adamw(w, g.reshape(w.shape), m, v, name=f"adamw_{tag}", after=after)

    (res,) = comm_call([red.exchange(["qkv", "small"])], name="tail_1")
    red.exchanged(["qkv", "small"], res)
    state, token = comm_start([red.scatter(["qkv", "small"])], name="tail_2_start")
    upd_wgu = big_update(ffn_w_gate_up, red.dest["wgu"], m_ffn_w_gate_up, v_ffn_w_gate_up, "wgu", after=token)
    (res,) = comm_wait(state, upd_wgu[1], name="tail_2_wait")
    red.scattered(["qkv", "small"], res, where)
    names, rider = red.broadcast([("qkv", None), ("wo", None)])
    state, token = comm_start([rider, allcast_rider(red.dest["slab"])], name="tail_3_start")
    upd_wd = big_update(ffn_w_down, red.dest["wd"], m_ffn_w_down, v_ffn_w_down, "wd", after=token)
    res, (slab_full,) = comm_wait(state, upd_wd[1], name="tail_3_wait")
    red.broadcasted(names, res)
    g_qkv, g_wo, g_win, g_wout = (red.dest[n] for n in ("qkv", "wo", "win", "wout"))
    g_norms, g_bqkv, g_bo, g_sinks, g_bsp, g_lng, g_lnb, g_wsp, loss = _unpack_small(slab_full, chip)

    upd = {
        "attn_w_qkv": big_update(attn_w_qkv, g_qkv, m_attn_w_qkv, v_attn_w_qkv, "qkv"),
        "attn_w_o": big_update(attn_w_o, g_wo, m_attn_w_o, v_attn_w_o, "wo"),
        "sgu_w_in": big_update(sgu_w_in, g_win, m_sgu_w_in, v_sgu_w_in, "win"),
        "sgu_w_out": big_update(sgu_w_out, g_wout, m_sgu_w_out, v_sgu_w_out, "wout"),
        "ffn_w_gate_up": upd_wgu,
        "ffn_w_down": upd_wd,
    }
    small_names = ["norm_mix_pre", "norm_mix_post", "norm_ffn_pre", "norm_ffn_post", "attn_b_qkv", "attn_sinks", "attn_b_o",
                   "sgu_ln_g", "sgu_ln_b", "sgu_w_spatial", "sgu_b_spatial"]
    small_w = [norm_mix_pre, norm_mix_post, norm_ffn_pre, norm_ffn_post, attn_b_qkv, attn_sinks, attn_b_o, sgu_ln_g, sgu_ln_b,
               sgu_w_spatial, sgu_b_spatial]
    small_m = [m_norm_mix_pre, m_norm_mix_post, m_norm_ffn_pre, m_norm_ffn_post, m_attn_b_qkv, m_attn_sinks, m_attn_b_o,
               m_sgu_ln_g, m_sgu_ln_b, m_sgu_w_spatial, m_sgu_b_spatial]
    small_v = [v_norm_mix_pre, v_norm_mix_post, v_norm_ffn_pre, v_norm_ffn_post, v_attn_b_qkv, v_attn_sinks, v_attn_b_o,
               v_sgu_ln_g, v_sgu_ln_b, v_sgu_w_spatial, v_sgu_b_spatial]
    small_g = g_norms + [g_bqkv, g_sinks, g_bo, g_lng, g_lnb, g_wsp, g_bsp]

    def flat2(a):
        return a.reshape(-1, a.shape[-1])

    res = adamw_small([flat2(a) for a in small_w], [flat2(a) for a in small_g], [flat2(a) for a in small_m],
                      [flat2(a) for a in small_v], name="adamw_small")
    for i, nm in enumerate(small_names):
        upd[nm] = tuple(r[i].reshape(small_w[i].shape) for r in res)

    order = ["norm_mix_pre", "norm_mix_post", "norm_ffn_pre", "norm_ffn_post", "attn_w_qkv", "attn_b_qkv", "attn_sinks",
             "attn_w_o", "attn_b_o", "sgu_w_in", "sgu_ln_g", "sgu_ln_b", "sgu_w_spatial", "sgu_b_spatial", "sgu_w_out",
             "ffn_w_gate_up", "ffn_w_down"]
    outs = [loss, grad_x.reshape(1, s, D_MODEL)]
    for part in range(4):
        outs += [upd[nm][part] for nm in order]
    return tuple(outs)
```

```python
import types

import numpy as np
import jax
import jax.numpy as jnp
from jax import lax
from jax.experimental import pallas as pl
from jax.experimental.pallas import tpu as pltpu

F32 = jnp.float32
BF16 = jnp.bfloat16
I32 = jnp.int32

D_MODEL = 1024
HEAD_DIM = 64
N_Q_HEADS = 16
N_KV_HEADS = 4
GQA_GROUP = 4
WINDOW = 128
Q_WIDTH = 1024
KV_WIDTH = 256
QKV_WIDTH = 1536
ROPE_THETA = 10000.0
SGU_GROUPS = 8
SGU_CHUNK = 128
D_FF = 2816
FF_HALF = D_FF // 2
EPS = 1e-6
N_CHIPS = 4
LANES = 128

ADAM_LR = 0.001
ADAM_B1 = 0.9
ADAM_B2 = 0.999
ADAM_EPS = 1e-08
ADAM_WD = 0.01
ADAM_STEP = 10

VMEM_LIMIT = 52 * 1024 * 1024
BIG_VMEM_LIMIT = 62 * 1024 * 1024
SUB_ROWS = 256
MESH = pl.DeviceIdType.MESH
NEG = -1e30
NT_DIMS = (((1,), (1,)), ((), ()))
TN_DIMS = (((0,), (0,)), ((), ()))
NN_DIMS = (((1,), (0,)), ((), ()))
ANY = pl.BlockSpec(memory_space=pl.ANY)


def _row_tile(s, want):
    return want if s % want == 0 else s


PEER_KINDS = ("sibling", "chips", "sibling+chips", "everyone")


def _peer_kind(riders):
    kinds = {r.peers for r in riders}
    if not kinds:
        return None
    if "everyone" in kinds:
        return "everyone"
    return "sibling+chips" if len(kinds) == 2 else kinds.pop()


def _peer_barrier(kind):
    x, y, c = _place()
    chips = [(*_partner(x, y, k), c) for k in (1, 2, 3)]
    peers = {"sibling": [(x, y, 1 - c)], "chips": chips, "sibling+chips": [(x, y, 1 - c)] + chips,
             "everyone": [(x, y, 1 - c)] + chips + [(px, py, 1 - c) for px, py, _ in chips]}[kind]
    barrier = pltpu.get_barrier_semaphore()
    for dev in peers:
        pl.semaphore_signal(barrier, inc=1, device_id=dev, device_id_type=MESH)
    pl.semaphore_wait(barrier, len(peers))


def _call(body, *, name, grid=(), in_specs=(), out_specs=(), out_shape=(), scratch_shapes=(), operands=(), prefetch=(),
          aliases=None, riders=(), sem=None, vmem_limit=VMEM_LIMIT):
    n_pre, n_in, n_out, n_scr = len(prefetch), len(operands), len(out_shape), len(scratch_shapes)
    in_specs, out_specs, out_shape = list(in_specs), list(out_specs), list(out_shape)
    operands, scratch_shapes = list(operands), list(scratch_shapes)
    io_alias = {n_pre + i: o for i, o in (aliases or {}).items()}
    for r in riders:
        base_in, base_out = n_pre + len(operands), len(out_shape)
        operands += list(r.inputs)
        in_specs += [ANY] * len(r.inputs)
        for pos, i in enumerate(r.aliased):
            io_alias[base_in + i] = base_out + pos
            out_shape.append(jax.ShapeDtypeStruct(r.inputs[i].shape, r.inputs[i].dtype))
        out_shape += list(r.fresh)
        out_specs += [ANY] * (len(r.aliased) + len(r.fresh))
        scratch_shapes += [pltpu.SemaphoreType.DMA((r.nsem,)), pltpu.SemaphoreType.DMA((r.nsem,))]

    def wrapped(*refs):
        pre, p = refs[:n_pre], n_pre
        core_in, p = refs[p:p + n_in], p + n_in
        r_in = []
        for r in riders:
            r_in.append(refs[p:p + len(r.inputs)])
            p += len(r.inputs)
        core_out, p = refs[p:p + n_out], p + n_out
        r_out = []
        for r in riders:
            k = len(r.aliased) + len(r.fresh)
            r_out.append(refs[p:p + k])
            p += k
        core_scr, p = refs[p:p + n_scr], p + n_scr
        r_sem = [refs[p + 2 * i:p + 2 * i + 2] for i in range(len(riders))]

        def edge(at_last, fns):
            def run():
                if not at_last:
                    _peer_barrier(peer_kind)
                for i, r in enumerate(riders):
                    getattr(r, fns)(r_in[i], r_out[i], r_sem[i][0], r_sem[i][1])
            if not riders:
                return
            if not grid:
                run()
                return
            cond = None
            for d, n in enumerate(grid):
                c = pl.program_id(d) == (n - 1 if at_last else 0)
                cond = c if cond is None else jnp.logical_and(cond, c)
            pl.when(cond)(run)

        edge(False, "start")
        if body is not None:
            body(*pre, *core_in, *core_out, *core_scr)
        edge(True, "finish")

    if sem is None or riders:
        sem = ("arbitrary",) * len(grid)
    kwargs = dict(out_shape=out_shape, input_output_aliases=io_alias, name=name)
    peer_kind = _peer_kind(riders)
    collective = {} if peer_kind is None else {"collective_id": PEER_KINDS.index(peer_kind)}
    if grid:
        kwargs["compiler_params"] = pltpu.CompilerParams(dimension_semantics=sem, vmem_limit_bytes=vmem_limit, **collective)
    elif collective:
        kwargs["compiler_params"] = pltpu.CompilerParams(**collective)
    if n_pre:
        kwargs["grid_spec"] = pltpu.PrefetchScalarGridSpec(
            num_scalar_prefetch=n_pre, grid=grid, in_specs=in_specs, out_specs=out_specs, scratch_shapes=scratch_shapes)
    else:
        kwargs.update(grid=grid, in_specs=in_specs, out_specs=out_specs, scratch_shapes=scratch_shapes)
    res = pl.pallas_call(wrapped, **kwargs)(*prefetch, *operands)
    core, rest, rider_res = list(res[:n_out]), list(res[n_out:]), []
    for r in riders:
        k = len(r.aliased) + len(r.fresh)
        rider_res.append(rest[:k])
        rest = rest[k:]
    return core, rider_res


def _mm_call(*, grid, in_specs, out_spec, out_shape, dims, nk, kaxis, acc_shape, name, operands, riders=()):
    out_dtype = out_shape.dtype

    def body(a_ref, b_ref, o_ref, *scratch):
        p = lax.dot_general(a_ref[...].astype(BF16), b_ref[...].astype(BF16), dims, preferred_element_type=F32)
        if nk == 1:
            o_ref[...] = p.astype(out_dtype)
        else:
            acc = scratch[0]
            kk = pl.program_id(kaxis)

            @pl.when(kk == 0)
            def _():
                acc[...] = p

            @pl.when(kk > 0)
            def _():
                acc[...] += p

            @pl.when(kk == nk - 1)
            def _():
                o_ref[...] = acc[...].astype(out_dtype)

    sem = ["parallel"] * len(grid)
    if nk > 1:
        sem[kaxis] = "arbitrary"
    (out,), rider_res = _call(
        body, grid=grid, in_specs=in_specs, out_specs=[out_spec], out_shape=[out_shape],
        scratch_shapes=[pltpu.VMEM(acc_shape, F32)] if nk > 1 else [], operands=operands, name=name, riders=riders,
        sem=tuple(sem))
    return (out, rider_res) if riders else out


def mm_nt(a, w, *, out_dtype, name, tm=1024, riders=()):
    m, n = a.shape
    kout = w.shape[0]
    tm = _row_tile(m, tm)
    return _mm_call(grid=(m // tm,),
                    in_specs=[pl.BlockSpec((tm, n), lambda i: (i, 0)), pl.BlockSpec((kout, n), lambda i: (0, 0))],
                    out_spec=pl.BlockSpec((tm, kout), lambda i: (i, 0)),
                    out_shape=jax.ShapeDtypeStruct((m, kout), out_dtype), dims=NT_DIMS, nk=1, kaxis=0,
                    acc_shape=None, name=name, operands=(a, w), riders=riders)


def mm_tn(a, b, *, shard_major, name, tm, tn, tk=None, out_dtype=BF16, riders=()):
    s, m = a.shape
    tk = s if tk is None else _row_tile(s, tk)
    if b.ndim == 3:
        n = 2 * b.shape[2]
        b_spec = pl.BlockSpec((None, tk, tn), lambda j, i, kk: (j // 2, kk, j % 2))
    else:
        n = b.shape[1]
        b_spec = pl.BlockSpec((tk, tn), lambda j, i, kk: (kk, j))
    if shard_major:
        assert tn == n // N_CHIPS
        o_spec = pl.BlockSpec((None, tm, tn), lambda j, i, kk: (j, i, 0))
        o_shape = jax.ShapeDtypeStruct((N_CHIPS, m, tn), out_dtype)
    else:
        o_spec = pl.BlockSpec((tm, tn), lambda j, i, kk: (i, j))
        o_shape = jax.ShapeDtypeStruct((m, n), out_dtype)
    return _mm_call(grid=(n // tn, m // tm, s // tk),
                    in_specs=[pl.BlockSpec((tk, tm), lambda j, i, kk: (kk, i)), b_spec], out_spec=o_spec,
                    out_shape=o_shape, dims=TN_DIMS, nk=s // tk, kaxis=2, acc_shape=(tm, tn), name=name, operands=(a, b),
                    riders=riders)


def _rstd(x):
    return lax.rsqrt(jnp.mean(x * x, axis=-1, keepdims=True) + EPS)


def _rms_bwd(dy, x, g):
    r = _rstd(x)
    xhat = x * r
    gy = dy * g
    dx = r * (gy - xhat * jnp.mean(gy * xhat, axis=-1, keepdims=True))
    return dx, jnp.sum(dy * xhat, axis=0, keepdims=True)


def _accum(ref, val, first):
    @pl.when(first)
    def _():
        ref[...] = val

    @pl.when(jnp.logical_not(first))
    def _():
        ref[...] += val


def _row_spec(tm, width):
    return pl.BlockSpec((tm, width), lambda i: (i, 0))


def _vec_spec(width):
    return pl.BlockSpec((1, width), lambda i: (0, 0))


def _ret(core, rider_res, riders):
    core = core[0] if len(core) == 1 else core
    return (core, rider_res) if riders else core


def prenorm_and_place(x, g, w, layer, chip_arr, *, name, tm=256, riders=()):
    s = x.shape[0]
    tm = _row_tile(s, tm)
    steps = s // tm
    _, r, c = w.shape
    tr = r // steps
    assert tr * steps == r and tr % 16 == 0

    def body(chip_ref, x_ref, g_ref, w_ref, h_ref, o_ref):
        xv = x_ref[...]
        h_ref[...] = (xv * _rstd(xv) * g_ref[...]).astype(BF16)
        o_ref[...] = w_ref[...].astype(BF16)

    core, rr = _call(
        body, grid=(steps,), prefetch=(chip_arr,),
        in_specs=[pl.BlockSpec((tm, D_MODEL), lambda i, chip: (i, 0)), pl.BlockSpec((1, D_MODEL), lambda i, chip: (0, 0)),
                  pl.BlockSpec((None, tr, c), lambda i, chip: (layer, i, 0))],
        out_specs=[pl.BlockSpec((tm, D_MODEL), lambda i, chip: (i, 0)), pl.BlockSpec((None, tr, c), lambda i, chip: (chip[0], i, 0))],
        out_shape=[jax.ShapeDtypeStruct((s, D_MODEL), BF16), jax.ShapeDtypeStruct((N_CHIPS, r, c), BF16)],
        operands=(x, g, w), sem=("parallel",), name=name, riders=riders)
    return _ret(core, rr, riders)


def proj_residual_norm(a, w, x, bias, g_post, g_next, *, name, tm=512, sub=256, riders=()):
    s, k = a.shape
    tm = _row_tile(s, tm)
    sub = min(sub, tm)

    def body(a_ref, w_ref, x_ref, b_ref, gp_ref, gn_ref, xo_ref, h_ref, m_ref):
        for t in range(tm // sub):
            rows = slice(t * sub, (t + 1) * sub)
            mv = jnp.dot(a_ref[rows, :], w_ref[...], preferred_element_type=F32) + b_ref[...]
            m_ref[rows, :] = mv.astype(BF16)
            xn = x_ref[rows, :] + mv * _rstd(mv) * gp_ref[...]
            xo_ref[rows, :] = xn
            h_ref[rows, :] = (xn * _rstd(xn) * gn_ref[...]).astype(BF16)

    row, vec = _row_spec(tm, D_MODEL), _vec_spec(D_MODEL)
    core, rr = _call(
        body, grid=(s // tm,),
        in_specs=[_row_spec(tm, k), pl.BlockSpec((k, D_MODEL), lambda i: (0, 0)), row, vec, vec, vec], out_specs=[row, row, row],
        out_shape=[jax.ShapeDtypeStruct((s, D_MODEL), F32), jax.ShapeDtypeStruct((s, D_MODEL), BF16),
                   jax.ShapeDtypeStruct((s, D_MODEL), BF16)],
        operands=(a, w, x, bias, g_post, g_next), sem=("parallel",), name=name, riders=riders)
    return _ret(core, rr, riders)


def ffn_fwd_loss_rows(h, w_gu, w_d, x, g_post, target, *, name, tm=512, riders=()):
    s = x.shape[0]
    tm = _row_tile(s, tm)

    def body(h_ref, w0, w1, w2, w3, wd_ref, x_ref, g_ref, t_ref, d_ref, a_ref, dx_ref, df_ref, dg_ref, loss_ref):
        first = pl.program_id(0) == 0
        halves = [slice(half * FF_HALF, (half + 1) * FF_HALF) for half in (0, 1)]
        gain = g_ref[...]
        sub = min(SUB_ROWS, tm)
        sums = None
        for t in range(tm // sub):
            rows = slice(t * sub, (t + 1) * sub)
            hv = h_ref[rows, :]
            fv = None
            for cols, (wg_ref, wu_ref) in zip(halves, ((w0, w2), (w1, w3))):
                g = jnp.dot(hv, wg_ref[...], preferred_element_type=F32)
                u = jnp.dot(hv, wu_ref[...], preferred_element_type=F32)
                sig = _sigmoid(g)
                silu = g * sig
                d_ref[0, rows, cols] = (u * (sig + silu * (1.0 - sig))).astype(BF16)
                d_ref[1, rows, cols] = silu.astype(BF16)
                act = (silu * u).astype(BF16)
                a_ref[rows, cols] = act
                p = jnp.dot(act, wd_ref[cols, :], preferred_element_type=F32)
                fv = p if fv is None else fv + p
            err = x_ref[rows, :] + fv * _rstd(fv) * gain - t_ref[rows, :]
            dx = err * (1.0 / D_MODEL)
            dx_ref[rows, :] = dx
            df, dg = _rms_bwd(dx, fv, gain)
            df_ref[rows, :] = df.astype(BF16)
            part = (dg, jnp.sum(jnp.sum(err * err, axis=-1, keepdims=True), axis=0, keepdims=True) * (0.5 / D_MODEL))
            sums = part if sums is None else tuple(a + b for a, b in zip(sums, part))
        _accum(dg_ref, sums[0], first)
        _accum(loss_ref, jnp.broadcast_to(sums[1], (8, LANES)), first)

    def resident(shape, index):
        return pl.BlockSpec(shape, index, pipeline_mode=pl.Buffered(1))

    row, vec = _row_spec(tm, D_MODEL), _vec_spec(D_MODEL)
    shards = [resident((None, D_MODEL, FF_HALF), (lambda j: (lambda i: (j, 0, 0)))(j)) for j in range(N_CHIPS)]
    core, rr = _call(
        body, grid=(s // tm,),
        in_specs=[row] + shards + [resident((D_FF, D_MODEL), lambda i: (0, 0)), row, vec, row],
        out_specs=[pl.BlockSpec((2, tm, D_FF), lambda i: (0, i, 0)), _row_spec(tm, D_FF), row, row, vec,
                   pl.BlockSpec((8, LANES), lambda i: (0, 0))],
        out_shape=[jax.ShapeDtypeStruct((2, s, D_FF), BF16), jax.ShapeDtypeStruct((s, D_FF), BF16),
                   jax.ShapeDtypeStruct((s, D_MODEL), F32), jax.ShapeDtypeStruct((s, D_MODEL), BF16),
                   jax.ShapeDtypeStruct((1, D_MODEL), F32), jax.ShapeDtypeStruct((8, LANES), F32)],
        operands=(h, w_gu, w_gu, w_gu, w_gu, w_d, x, g_post, target), name=name, riders=riders, vmem_limit=BIG_VMEM_LIMIT)
    return _ret(core, rr, riders)


def dh_norm_bwd_pair(a, w, dres, x, g_pre, m, g_post, *, name, tm=512, sub=256, riders=()):
    _, kout, ns = w.shape
    planes = a.ndim == 3
    s = x.shape[0]
    tm = _row_tile(s, tm)
    sub = min(sub, tm)
    a_spec = pl.BlockSpec((2, tm, 2 * ns), lambda i: (0, i, 0)) if planes else pl.BlockSpec((tm, N_CHIPS * ns), lambda i: (i, 0))

    def body(a_ref, w0, w1, w2, w3, dres_ref, x_ref, gpre_ref, m_ref, gpost_ref, dx_ref, dm_ref, dgpre_ref, dgpost_ref, db_ref):
        first = pl.program_id(0) == 0
        sums = None
        for t in range(tm // sub):
            rows = slice(t * sub, (t + 1) * sub)
            dh = None
            for j, w_ref in enumerate((w0, w1, w2, w3)):
                a_j = a_ref[j // 2, rows, (j % 2) * ns:(j % 2 + 1) * ns] if planes else a_ref[rows, j * ns:(j + 1) * ns]
                p = lax.dot_general(a_j, w_ref[...], NT_DIMS, preferred_element_type=F32)
                dh = p if dh is None else dh + p
            d1, dgpre = _rms_bwd(dh, x_ref[rows, :], gpre_ref[...])
            dx = dres_ref[rows, :] + d1
            dx_ref[rows, :] = dx
            dm, dgpost = _rms_bwd(dx, m_ref[rows, :].astype(F32), gpost_ref[...])
            dm_ref[rows, :] = dm.astype(BF16)
            part = (dgpre, dgpost, jnp.sum(dm, axis=0, keepdims=True))
            sums = part if sums is None else tuple(u + v for u, v in zip(sums, part))
        _accum(dgpre_ref, sums[0], first)
        _accum(dgpost_ref, sums[1], first)
        _accum(db_ref, sums[2], first)

    def shard(j):
        return pl.BlockSpec((None, kout, ns), lambda i: (j, 0, 0))

    row, vec = _row_spec(tm, D_MODEL), _vec_spec(D_MODEL)
    vshape = jax.ShapeDtypeStruct((1, D_MODEL), F32)
    core, rr = _call(
        body, grid=(s // tm,), in_specs=[a_spec] + [shard(j) for j in range(N_CHIPS)] + [row, row, vec, row, vec],
        out_specs=[row, row, vec, vec, vec],
        out_shape=[jax.ShapeDtypeStruct((s, D_MODEL), F32), jax.ShapeDtypeStruct((s, D_MODEL), BF16), vshape, vshape, vshape],
        operands=(a, w, w, w, w, dres, x, g_pre, m, g_post), name=name, riders=riders)
    return _ret(core, rr, riders)


def ffn_bwd_rows(df, w_d, d_planes, w_gu, dres, x, g_pre, m, g_post, *, name, tm=512, riders=()):
    s = x.shape[0]
    tm = _row_tile(s, tm)

    def body(df_ref, wd_ref, d_ref, w0, w1, w2, w3, dres_ref, x_ref, gpre_ref, m_ref, gpost_ref,
             o_ref, dx_ref, dm_ref, dgpre_ref, dgpost_ref, db_ref):
        first = pl.program_id(0) == 0
        halves = [slice(half * FF_HALF, (half + 1) * FF_HALF) for half in (0, 1)]
        sub = min(SUB_ROWS, tm)
        sums = None
        for t in range(tm // sub):
            rows = slice(t * sub, (t + 1) * sub)
            dfv = df_ref[rows, :]
            dh = None
            for cols, (wg_ref, wu_ref) in zip(halves, ((w0, w2), (w1, w3))):
                da = lax.dot_general(dfv, wd_ref[cols, :], NT_DIMS, preferred_element_type=F32)
                dg = (da * d_ref[0, rows, cols].astype(F32)).astype(BF16)
                du = (da * d_ref[1, rows, cols].astype(F32)).astype(BF16)
                o_ref[0, rows, cols] = dg
                o_ref[1, rows, cols] = du
                p = lax.dot_general(dg, wg_ref[...], NT_DIMS, preferred_element_type=F32)
                p += lax.dot_general(du, wu_ref[...], NT_DIMS, preferred_element_type=F32)
                dh = p if dh is None else dh + p
            d1, dgpre = _rms_bwd(dh, x_ref[rows, :], gpre_ref[...])
            dx = dres_ref[rows, :] + d1
            dx_ref[rows, :] = dx
            dm, dgpost = _rms_bwd(dx, m_ref[rows, :].astype(F32), gpost_ref[...])
            dm_ref[rows, :] = dm.astype(BF16)
            part = (dgpre, dgpost, jnp.sum(dm, axis=0, keepdims=True))
            sums = part if sums is None else tuple(a + b for a, b in zip(sums, part))
        _accum(dgpre_ref, sums[0], first)
        _accum(dgpost_ref, sums[1], first)
        _accum(db_ref, sums[2], first)

    def resident(shape, index):
        return pl.BlockSpec(shape, index, pipeline_mode=pl.Buffered(1))

    planes = pl.BlockSpec((2, tm, D_FF), lambda i: (0, i, 0))
    row, vec = _row_spec(tm, D_MODEL), _vec_spec(D_MODEL)
    vshape = jax.ShapeDtypeStruct((1, D_MODEL), F32)
    shards = [resident((None, D_MODEL, FF_HALF), (lambda j: (lambda i: (j, 0, 0)))(j)) for j in range(N_CHIPS)]
    core, rr = _call(
        body, grid=(s // tm,),
        in_specs=[row, resident((D_FF, D_MODEL), lambda i: (0, 0)), planes] + shards + [row, row, vec, row, vec],
        out_specs=[planes, row, row, vec, vec, vec],
        out_shape=[jax.ShapeDtypeStruct((2, s, D_FF), BF16), jax.ShapeDtypeStruct((s, D_MODEL), F32),
                   jax.ShapeDtypeStruct((s, D_MODEL), BF16), vshape, vshape, vshape],
        operands=(df, w_d, d_planes, w_gu, w_gu, w_gu, w_gu, dres, x, g_pre, m, g_post), name=name, riders=riders,
        vmem_limit=BIG_VMEM_LIMIT)
    return _ret(core, rr, riders)


def dh_norm_bwd_last(a, w, dres, x, g_pre, *, name, tm=512, sub=256):
    _, kout, ns = w.shape
    s = x.shape[0]
    tm = _row_tile(s, tm)
    sub = min(sub, tm)

    def body(a_ref, w0, w1, w2, w3, dres_ref, x_ref, g_ref, dx_ref, dg_ref):
        total = None
        for t in range(tm // sub):
            rows = slice(t * sub, (t + 1) * sub)
            dh = None
            for j, w_ref in enumerate((w0, w1, w2, w3)):
                p = lax.dot_general(a_ref[rows, j * ns:(j + 1) * ns], w_ref[...], NT_DIMS, preferred_element_type=F32)
                dh = p if dh is None else dh + p
            d1, dg = _rms_bwd(dh, x_ref[rows, :], g_ref[...])
            dx_ref[rows, :] = dres_ref[rows, :] + d1
            total = dg if total is None else total + dg
        _accum(dg_ref, total, pl.program_id(0) == 0)

    def shard(j):
        return pl.BlockSpec((None, kout, ns), lambda i: (j, 0, 0))

    row, vec = _row_spec(tm, D_MODEL), _vec_spec(D_MODEL)
    (dx, dg), _ = _call(
        body, grid=(s // tm,), in_specs=[_row_spec(tm, N_CHIPS * ns)] + [shard(j) for j in range(N_CHIPS)] + [row, row, vec],
        out_specs=[row, vec], out_shape=[jax.ShapeDtypeStruct((s, D_MODEL), F32), jax.ShapeDtypeStruct((1, D_MODEL), F32)],
        operands=(a, w, w, w, w, dres, x, g_pre), name=name)
    return dx, dg


def _rope_tables(s):
    half = HEAD_DIM // 2
    inv_freq = np.float32(ROPE_THETA) ** (-(np.arange(half, dtype=np.float32) * np.float32(2.0)) / np.float32(HEAD_DIM))
    ang = np.arange(s, dtype=np.float32)[:, None] * inv_freq[None, :]
    cos, sin = np.cos(ang).astype(np.float32), np.sin(ang).astype(np.float32)
    return jnp.asarray(np.tile(cos, (1, 4))), jnp.asarray(np.concatenate([-sin, sin, -sin, sin], axis=1))


def _swap_halves(x):
    lane = lax.broadcasted_iota(I32, x.shape, 1)
    return jnp.where((lane & (HEAD_DIM - 1)) < HEAD_DIM // 2, pltpu.roll(x, LANES - 32, 1), pltpu.roll(x, 32, 1))


N_ROPE_BLOCKS = (Q_WIDTH + KV_WIDTH) // LANES


def qkv_proj(h, w, bias, cos, sin, casts, chip_arr, *, name, tm=1024, riders=()):
    s, k = h.shape
    ns = w.shape[2]
    tm = _row_tile(s, tm)
    nc = len(casts)

    def body(chip_ref, h_ref, w_ref, b_ref, c_ref, s_ref, *rest):
        cast_in, o_ref, cast_out = rest[:nc], rest[nc], rest[nc + 1:]
        j = pl.program_id(0)

        @pl.when(jnp.logical_and(j == 0, pl.program_id(1) == 0))
        def _():
            for src, dst in zip(cast_in, cast_out):
                dst[...] = src[...].astype(BF16)

        sub = min(256, tm)
        for t in range(tm // sub):
            rows = slice(t * sub, (t + 1) * sub)
            p = jnp.dot(h_ref[rows, :], w_ref[...], preferred_element_type=F32) + b_ref[...]
            cosv, sinv = c_ref[rows, :], s_ref[rows, :]
            for blk in range(ns // LANES):
                xb = p[:, blk * LANES:(blk + 1) * LANES]
                roped = xb * cosv + _swap_halves(xb) * sinv
                is_qk = j * (ns // LANES) + blk < N_ROPE_BLOCKS
                o_ref[rows, blk * LANES:(blk + 1) * LANES] = jnp.where(is_qk, roped, xb).astype(BF16)

    def cast_in_spec(wt, layer):
        return pl.BlockSpec((None,) + wt.shape[1:], lambda j, i, chip: (layer, 0, 0), pipeline_mode=pl.Buffered(1))

    def cast_out_spec(wt):
        return pl.BlockSpec((None,) + wt.shape[1:], lambda j, i, chip: (chip[0], 0, 0), pipeline_mode=pl.Buffered(1))

    core, rr = _call(
        body, grid=(N_CHIPS, s // tm), prefetch=(chip_arr,),
        in_specs=[pl.BlockSpec((tm, k), lambda j, i, chip: (i, 0)), pl.BlockSpec((None, k, ns), lambda j, i, chip: (j, 0, 0)),
                  pl.BlockSpec((1, ns), lambda j, i, chip: (0, j)), pl.BlockSpec((tm, LANES), lambda j, i, chip: (i, 0)),
                  pl.BlockSpec((tm, LANES), lambda j, i, chip: (i, 0))] + [cast_in_spec(wt, layer) for wt, layer in casts],
        out_specs=[pl.BlockSpec((tm, ns), lambda j, i, chip: (i, j))] + [cast_out_spec(wt) for wt, _ in casts],
        out_shape=[jax.ShapeDtypeStruct((s, N_CHIPS * ns), BF16)]
        + [jax.ShapeDtypeStruct((N_CHIPS,) + wt.shape[1:], BF16) for wt, _ in casts],
        operands=(h, w, bias, cos, sin, *[wt for wt, _ in casts]), sem=("arbitrary", "arbitrary"), name=name, riders=riders)
    return (core, rr) if riders else core


def rope_bwd(dq, dkc, dkp, dvc, dvp, cos, sin, *, name, riders=()):
    s = dq.shape[0]
    tm = 2 * WINDOW if s % (2 * WINDOW) == 0 else WINDOW
    nb = s // tm

    def body(dq_ref, dkc_ref, dkp_ref, dkp_next_ref, dvc_ref, dvp_ref, dvp_next_ref, c_ref, s_ref, o_ref, db_ref):
        i = pl.program_id(0)
        has_next = (i < nb - 1).astype(F32)
        cosv, sinv = c_ref[...], s_ref[...]

        def shifted(ref, next_ref, cols):
            last = has_next * next_ref[:WINDOW, cols].astype(F32)
            return last if tm == WINDOW else jnp.concatenate([ref[WINDOW:, cols].astype(F32), last], axis=0)

        parts = []
        for blk in range(QKV_WIDTH // LANES):
            if blk < Q_WIDTH // LANES:
                g = dq_ref[:, blk * LANES:(blk + 1) * LANES].astype(F32)
            else:
                own, prv, nxt = (dkc_ref, dkp_ref, dkp_next_ref) if blk < N_ROPE_BLOCKS else (dvc_ref, dvp_ref, dvp_next_ref)
                cols = slice((blk % 2) * LANES, (blk % 2 + 1) * LANES)
                g = own[:, cols].astype(F32) + shifted(prv, nxt, cols)
            if blk < N_ROPE_BLOCKS:
                g = g * cosv + _swap_halves(g * sinv)
            o_ref[:, blk * LANES:(blk + 1) * LANES] = g.astype(BF16)
            parts.append(jnp.sum(g, axis=0, keepdims=True))
        sums = jnp.concatenate(parts, axis=1)
        _accum(db_ref, sums, i == 0)

    own_spec = _row_spec(tm, KV_WIDTH)
    next_spec = pl.BlockSpec((tm, KV_WIDTH), lambda i: (jnp.minimum(i + 1, nb - 1), 0))
    core, rr = _call(
        body, grid=(nb,),
        in_specs=[_row_spec(tm, Q_WIDTH), own_spec, own_spec, next_spec, own_spec, own_spec, next_spec,
                  _row_spec(tm, LANES), _row_spec(tm, LANES)],
        out_specs=[_row_spec(tm, QKV_WIDTH), _vec_spec(QKV_WIDTH)],
        out_shape=[jax.ShapeDtypeStruct((s, QKV_WIDTH), BF16), jax.ShapeDtypeStruct((1, QKV_WIDTH), F32)],
        operands=(dq, dkc, dkp, dkp, dvc, dvp, dvp, cos, sin), name=name, riders=riders)
    return _ret(core, rr, riders)


ROWS = GQA_GROUP * WINDOW


def _prev_slots():
    kpos = lax.broadcasted_iota(I32, (WINDOW, ROWS), 0)
    qpos = lax.broadcasted_iota(I32, (WINDOW, ROWS), 1) & (WINDOW - 1)
    return kpos > qpos


def _head_cols(ref, head):
    return ref[:, head * HEAD_DIM:(head + 1) * HEAD_DIM]


def _stack_heads(ref, h):
    return jnp.concatenate([_head_cols(ref, GQA_GROUP * h + g) for g in range(GQA_GROUP)], axis=0)


def _band(prev_ref, cur_ref, h):
    return jnp.concatenate([_head_cols(prev_ref, h), _head_cols(cur_ref, h)], axis=0)


def _pick(prev, band):
    return jnp.where(prev, band[:WINDOW], band[WINDOW:])


def _spread(prev, x):
    return jnp.concatenate([jnp.where(prev, x, 0.0), jnp.where(prev, 0.0, x)], axis=0).astype(BF16)


def _attn_probs(s_band, sink, prev, has_prev):
    scale = HEAD_DIM ** -0.5
    s = jnp.where(prev, jnp.where(has_prev, s_band[:WINDOW], NEG), s_band[WINDOW:]) * scale
    m = jnp.maximum(jnp.max(s, axis=0, keepdims=True), sink)
    e, es = jnp.exp(s - m), jnp.exp(sink - m)
    inv = 1.0 / (jnp.sum(e, axis=0, keepdims=True) + es)
    return e * inv, es * inv


def _attn_specs(nb):
    kcol, vcol = Q_WIDTH // KV_WIDTH, Q_WIDTH // KV_WIDTH + 1
    q_spec = pl.BlockSpec((WINDOW, Q_WIDTH), lambda n: (n, 0))
    return [q_spec,
            pl.BlockSpec((WINDOW, KV_WIDTH), lambda n: (n, kcol)),
            pl.BlockSpec((WINDOW, KV_WIDTH), lambda n: (jnp.maximum(n - 1, 0), kcol)),
            pl.BlockSpec((WINDOW, KV_WIDTH), lambda n: (n, vcol)),
            pl.BlockSpec((WINDOW, KV_WIDTH), lambda n: (jnp.maximum(n - 1, 0), vcol)),
            pl.BlockSpec((N_KV_HEADS, 8, ROWS), lambda n: (0, 0, 0))]


def attn_fwd(qkv, sink_rows, *, name, riders=()):
    s = qkv.shape[0]

    def body(q_ref, kc_ref, kp_ref, vc_ref, vp_ref, sink_ref, o_ref):
        prev = _prev_slots()
        has_prev = pl.program_id(0) > 0
        heads = range(N_KV_HEADS)
        s_bands = [lax.dot_general(_band(kp_ref, kc_ref, h), _stack_heads(q_ref, h), NT_DIMS, preferred_element_type=F32)
                   for h in heads]
        p_bands = [_spread(prev, _attn_probs(s_bands[h], sink_ref[h, 0:1, :], prev, has_prev)[0]) for h in heads]
        outs = [lax.dot_general(_band(vp_ref, vc_ref, h), p_bands[h], TN_DIMS, preferred_element_type=F32).T for h in heads]
        for h in heads:
            for g in range(GQA_GROUP):
                head = GQA_GROUP * h + g
                o_ref[:, head * HEAD_DIM:(head + 1) * HEAD_DIM] = outs[h][g * WINDOW:(g + 1) * WINDOW].astype(BF16)

    core, rr = _call(
        body, grid=(s // WINDOW,), in_specs=_attn_specs(s // WINDOW), out_specs=[pl.BlockSpec((WINDOW, Q_WIDTH), lambda n: (n, 0))],
        out_shape=[jax.ShapeDtypeStruct((s, Q_WIDTH), BF16)], operands=(qkv, qkv, qkv, qkv, qkv, sink_rows), sem=("parallel",),
        name=name, riders=riders)
    return _ret(core, rr, riders)


def attn_bwd(qkv, sink_rows, do, *, name, riders=()):
    s = qkv.shape[0]

    def body(q_ref, kc_ref, kp_ref, vc_ref, vp_ref, sink_ref, do_ref, dq_ref, dkc_ref, dkp_ref, dvc_ref, dvp_ref, dsink_ref):
        n = pl.program_id(0)
        prev = _prev_slots()
        scale = HEAD_DIM ** -0.5
        heads = range(N_KV_HEADS)
        qs, dos = [_stack_heads(q_ref, h) for h in heads], [_stack_heads(do_ref, h) for h in heads]
        kbands, vbands = [_band(kp_ref, kc_ref, h) for h in heads], [_band(vp_ref, vc_ref, h) for h in heads]
        s_bands = [lax.dot_general(kbands[h], qs[h], NT_DIMS, preferred_element_type=F32) for h in heads]
        dp_bands = [lax.dot_general(vbands[h], dos[h], NT_DIMS, preferred_element_type=F32) for h in heads]
        ds_bands, p_bands, parts = [], [], []
        for h in heads:
            p, ps = _attn_probs(s_bands[h], sink_ref[h, 0:1, :], prev, n > 0)
            dp = _pick(prev, dp_bands[h])
            delta = jnp.sum(p * dp, axis=0, keepdims=True)
            ds_bands.append(_spread(prev, p * (dp - delta) * scale))
            p_bands.append(_spread(prev, p))
            dsink = -(ps * delta)
            for g in range(GQA_GROUP):
                parts.append(jnp.broadcast_to(jnp.sum(dsink[:, g * WINDOW:(g + 1) * WINDOW], axis=1, keepdims=True), (8, LANES)))
        for h in heads:
            dk = jnp.dot(ds_bands[h], qs[h], preferred_element_type=F32).astype(BF16)
            dv = jnp.dot(p_bands[h], dos[h], preferred_element_type=F32).astype(BF16)
            dq = lax.dot_general(kbands[h], ds_bands[h], TN_DIMS, preferred_element_type=F32).T
            cols = slice(h * HEAD_DIM, (h + 1) * HEAD_DIM)
            dkp_ref[:, cols], dkc_ref[:, cols] = dk[:WINDOW], dk[WINDOW:]
            dvp_ref[:, cols], dvc_ref[:, cols] = dv[:WINDOW], dv[WINDOW:]
            for g in range(GQA_GROUP):
                head = GQA_GROUP * h + g
                dq_ref[:, head * HEAD_DIM:(head + 1) * HEAD_DIM] = dq[g * WINDOW:(g + 1) * WINDOW].astype(BF16)

        @pl.when(n == 0)
        def _():
            for i, part in enumerate(parts):
                dsink_ref[i // GQA_GROUP, i % GQA_GROUP] = part

        @pl.when(n > 0)
        def _():
            for i, part in enumerate(parts):
                dsink_ref[i // GQA_GROUP, i % GQA_GROUP] += part

    rows_q = pl.BlockSpec((WINDOW, Q_WIDTH), lambda n: (n, 0))
    rows_kv = pl.BlockSpec((WINDOW, KV_WIDTH), lambda n: (n, 0))
    kv_shape = jax.ShapeDtypeStruct((s, KV_WIDTH), BF16)
    core, rr = _call(
        body, grid=(s // WINDOW,), in_specs=_attn_specs(s // WINDOW) + [rows_q],
        out_specs=[rows_q, rows_kv, rows_kv, rows_kv, rows_kv,
                   pl.BlockSpec((N_KV_HEADS, GQA_GROUP, 8, LANES), lambda n: (0, 0, 0, 0))],
        out_shape=[jax.ShapeDtypeStruct((s, Q_WIDTH), BF16), kv_shape, kv_shape, kv_shape, kv_shape,
                   jax.ShapeDtypeStruct((N_KV_HEADS, GQA_GROUP, 8, LANES), F32)],
        operands=(qkv, qkv, qkv, qkv, qkv, sink_rows, do), sem=("arbitrary",), name=name, riders=riders)
    return _ret(core, rr, riders)


GELU_C = 0.7978845608028654
GELU_A = 0.044715


def _gelu(x):
    return 0.5 * x * (1.0 + jnp.tanh(x * (GELU_C + (GELU_C * GELU_A) * (x * x))))


def _gelu_and_grad(x):
    x2 = x * x
    t = jnp.tanh(x * (GELU_C + (GELU_C * GELU_A) * x2))
    half_x, one_t = 0.5 * x, 1.0 + t
    return half_x * one_t, 0.5 * one_t + half_x * (1.0 - t * t) * (GELU_C + (3.0 * GELU_C * GELU_A) * x2)


def _tril_bf16(w):
    row = lax.broadcasted_iota(I32, (SGU_CHUNK, SGU_CHUNK), 0)
    col = lax.broadcasted_iota(I32, (SGU_CHUNK, SGU_CHUNK), 1)
    return jnp.where(row >= col, w, 0.0).astype(BF16)


def _sgu_norm(vg, g, b):
    mu = jnp.mean(vg, axis=-1, keepdims=True)
    cen = vg - mu
    rstd = lax.rsqrt(jnp.mean(cen * cen, axis=-1, keepdims=True) + EPS)
    xhat = cen * rstd
    return xhat, rstd, xhat * g + b


def sgu_in_fwd(h, w_in, ln_g, ln_b, w_sp, b_sp, *, name, tm=512, riders=()):
    s, k = h.shape
    ns = w_in.shape[2]
    tm = _row_tile(s, tm)

    def body(h_ref, w0, w1, w2, w3, g_ref, b_ref, w_ref, bs_ref, z_ref, y_ref):
        hv = h_ref[...]
        zs = [jnp.dot(hv, w_ref_j[...], preferred_element_type=F32) for w_ref_j in (w0, w1, w2, w3)]
        for j, zj in enumerate(zs):
            z_ref[:, j * ns:(j + 1) * ns] = zj.astype(BF16)
        u = _gelu(jnp.concatenate(zs[:2], axis=1))
        _, _, vn = _sgu_norm(_gelu(jnp.concatenate(zs[2:], axis=1)), g_ref[...], b_ref[...])
        vn = vn.astype(BF16)
        for grp in range(SGU_GROUPS):
            w = _tril_bf16(w_ref[grp])
            cols = slice(grp * LANES, (grp + 1) * LANES)
            for ch in range(tm // SGU_CHUNK):
                rows = slice(ch * SGU_CHUNK, (ch + 1) * SGU_CHUNK)
                mixed = jnp.dot(w, vn[rows, cols], preferred_element_type=F32) + bs_ref[grp]
                y_ref[rows, cols] = (u[rows, cols] * mixed).astype(BF16)

    def shard(j):
        return pl.BlockSpec((None, k, ns), lambda i: (j, 0, 0))

    full3 = pl.BlockSpec((SGU_GROUPS, SGU_CHUNK, SGU_CHUNK), lambda i: (0, 0, 0))
    core, rr = _call(
        body, grid=(s // tm,),
        in_specs=[_row_spec(tm, k)] + [shard(j) for j in range(N_CHIPS)] + [_vec_spec(D_MODEL), _vec_spec(D_MODEL), full3, full3],
        out_specs=[_row_spec(tm, 2 * D_MODEL), _row_spec(tm, D_MODEL)],
        out_shape=[jax.ShapeDtypeStruct((s, 2 * D_MODEL), BF16), jax.ShapeDtypeStruct((s, D_MODEL), BF16)],
        operands=(h, w_in, w_in, w_in, w_in, ln_g, ln_b, w_sp, b_sp), sem=("parallel",), name=name, riders=riders)
    return _ret(core, rr, riders)


def sgu_bwd(z, dy, ln_g, ln_b, w_sp, b_sp, *, name, tm=256, riders=()):
    s = z.shape[0]
    tm = _row_tile(s, tm)

    def body(z_ref, dy_ref, g_ref, b_ref, w_ref, bs_ref, dz_ref, dw_ref, dbs_ref, dg_ref, db_ref, dvn_buf):
        first = pl.program_id(0) == 0
        u, u_grad = _gelu_and_grad(z_ref[:, :D_MODEL].astype(F32))
        vg, v_grad = _gelu_and_grad(z_ref[:, D_MODEL:].astype(F32))
        xhat, rstd, vn = _sgu_norm(vg, g_ref[...], b_ref[...])
        vn = vn.astype(BF16)
        dyv = dy_ref[...]
        dmixed = dyv * u
        dz_gate = dyv * u_grad
        row = lax.broadcasted_iota(I32, (SGU_CHUNK, SGU_CHUNK), 0)
        col = lax.broadcasted_iota(I32, (SGU_CHUNK, SGU_CHUNK), 1)
        dws, dbss = [], []
        for grp in range(SGU_GROUPS):
            w = _tril_bf16(w_ref[grp])
            cols = slice(grp * LANES, (grp + 1) * LANES)
            dw = jnp.zeros((SGU_CHUNK, SGU_CHUNK), F32)
            dbs = jnp.zeros((SGU_CHUNK, 1), F32)
            for ch in range(tm // SGU_CHUNK):
                rows = slice(ch * SGU_CHUNK, (ch + 1) * SGU_CHUNK)
                vblk = vn[rows, cols]
                mixed = jnp.dot(w, vblk, preferred_element_type=F32) + bs_ref[grp]
                dz_ref[rows, cols] = (dz_gate[rows, cols] * mixed).astype(BF16)
                dm = dmixed[rows, cols]
                dmb = dm.astype(BF16)
                dvn_buf[rows, cols] = lax.dot_general(w, dmb, TN_DIMS, preferred_element_type=F32)
                dw += lax.dot_general(dmb, vblk, NT_DIMS, preferred_element_type=F32)
                dbs += jnp.sum(dm, axis=-1, keepdims=True)
            dws.append(jnp.where(row >= col, dw, 0.0))
            dbss.append(jnp.broadcast_to(dbs, (SGU_CHUNK, SGU_CHUNK)))

        dvn = dvn_buf[...]
        dxhat = dvn * g_ref[...]
        dvg = rstd * (dxhat - jnp.mean(dxhat, axis=-1, keepdims=True) - xhat * jnp.mean(dxhat * xhat, axis=-1, keepdims=True))
        dz_ref[:, D_MODEL:] = (dvg * v_grad).astype(BF16)
        dlng, dlnb = jnp.sum(dvn * xhat, axis=0, keepdims=True), jnp.sum(dvn, axis=0, keepdims=True)

        @pl.when(first)
        def _():
            for grp in range(SGU_GROUPS):
                dw_ref[grp] = dws[grp]
                dbs_ref[grp] = dbss[grp]
            dg_ref[...] = dlng
            db_ref[...] = dlnb

        @pl.when(jnp.logical_not(first))
        def _():
            for grp in range(SGU_GROUPS):
                dw_ref[grp] += dws[grp]
                dbs_ref[grp] += dbss[grp]
            dg_ref[...] += dlng
            db_ref[...] += dlnb

    full3 = pl.BlockSpec((SGU_GROUPS, SGU_CHUNK, SGU_CHUNK), lambda i: (0, 0, 0))
    s3 = jax.ShapeDtypeStruct((SGU_GROUPS, SGU_CHUNK, SGU_CHUNK), F32)
    vshape = jax.ShapeDtypeStruct((1, D_MODEL), F32)
    core, rr = _call(
        body, grid=(s // tm,),
        in_specs=[_row_spec(tm, 2 * D_MODEL), _row_spec(tm, D_MODEL), _vec_spec(D_MODEL), _vec_spec(D_MODEL), full3, full3],
        out_specs=[_row_spec(tm, 2 * D_MODEL), full3, full3, _vec_spec(D_MODEL), _vec_spec(D_MODEL)],
        out_shape=[jax.ShapeDtypeStruct((s, 2 * D_MODEL), BF16), s3, s3, vshape, vshape],
        scratch_shapes=[pltpu.VMEM((tm, D_MODEL), F32)], operands=(z, dy, ln_g, ln_b, w_sp, b_sp), name=name, riders=riders)
    return _ret(core, rr, riders)


def _sigmoid(x):
    return 1.0 / (1.0 + jnp.exp(-x))


def ffn_up(h, w_gu, *, name, tm=512, riders=()):
    s = h.shape[0]
    tm = _row_tile(s, tm)

    def body(h_ref, wg_ref, wu_ref, d_ref, a_ref):
        hv = h_ref[...]
        sub = min(256, tm)
        for t in range(tm // sub):
            rows = slice(t * sub, (t + 1) * sub)
            g = jnp.dot(hv[rows], wg_ref[...], preferred_element_type=F32)
            u = jnp.dot(hv[rows], wu_ref[...], preferred_element_type=F32)
            sig = _sigmoid(g)
            silu = g * sig
            d_ref[0, rows, :] = (u * (sig + silu * (1.0 - sig))).astype(BF16)
            d_ref[1, rows, :] = silu.astype(BF16)
            a_ref[rows, :] = (silu * u).astype(BF16)

    core, rr = _call(
        body, grid=(2, s // tm),
        in_specs=[pl.BlockSpec((tm, D_MODEL), lambda j, i: (i, 0)),
                  pl.BlockSpec((None, D_MODEL, FF_HALF), lambda j, i: (j, 0, 0)),
                  pl.BlockSpec((None, D_MODEL, FF_HALF), lambda j, i: (j + 2, 0, 0))],
        out_specs=[pl.BlockSpec((2, tm, FF_HALF), lambda j, i: (0, i, j)), pl.BlockSpec((tm, FF_HALF), lambda j, i: (i, j))],
        out_shape=[jax.ShapeDtypeStruct((2, s, D_FF), BF16), jax.ShapeDtypeStruct((s, D_FF), BF16)],
        operands=(h, w_gu, w_gu), sem=("parallel", "parallel"), name=name, riders=riders)
    return _ret(core, rr, riders)


def _weight_tile(rows):
    for tr in (512, 352, 256, 128):
        if rows % tr == 0:
            return tr
    return rows


def place_shard(w, layer, chip_arr, dtype, *, name, riders=()):
    _, r, c = w.shape
    tr = _weight_tile(r)

    def body(chip_ref, w_ref, o_ref):
        o_ref[...] = w_ref[...].astype(dtype)

    core, rr = _call(
        body, grid=(r // tr,), prefetch=(chip_arr,),
        in_specs=[pl.BlockSpec((None, tr, c), lambda i, chip: (layer, i, 0))],
        out_specs=[pl.BlockSpec((None, tr, c), lambda i, chip: (chip[0], i, 0))],
        out_shape=[jax.ShapeDtypeStruct((N_CHIPS, r, c), dtype)], operands=(w,), sem=("parallel",), name=name, riders=riders)
    return _ret(core, rr, riders)


def _adamw_math(w, g, m, v):
    m = ADAM_B1 * m + (1.0 - ADAM_B1) * g
    v = ADAM_B2 * v + (1.0 - ADAM_B2) * (g * g)
    m_hat = m / (1.0 - ADAM_B1 ** ADAM_STEP)
    v_hat = v / (1.0 - ADAM_B2 ** ADAM_STEP)
    delta = -ADAM_LR * (m_hat / (jnp.sqrt(v_hat) + ADAM_EPS) + ADAM_WD * w)
    return delta, m, v


def adamw(w, g, m, v, *, name, after=None):
    nl, r, c = w.shape
    tr = _weight_tile(r)

    def body(w_ref, g_ref, m_ref, v_ref, *rest):
        go_ref, d_ref, mo_ref, vo_ref = rest[-4:]
        gv = g_ref[...]
        go_ref[...] = gv
        d_ref[...], mo_ref[...], vo_ref[...] = _adamw_math(w_ref[...], gv, m_ref[...], v_ref[...])

    spec = pl.BlockSpec((None, tr, c), lambda l, i: (l, i, 0))
    shape = jax.ShapeDtypeStruct(w.shape, F32)
    extra = [] if after is None else [after]
    outs, _ = _call(body, grid=(nl, r // tr), in_specs=[spec] * 4 + [ANY] * len(extra), out_specs=[spec] * 4,
                    out_shape=[shape] * 4, operands=(w, g, m, v, *extra), sem=("parallel", "parallel"), name=name)
    return outs


def adamw_small(ws, gs, ms, vs, *, name):
    n = len(ws)

    def body(*refs):
        ins, outs = refs[:4 * n], refs[4 * n:]
        for t in range(n):
            gv = ins[n + t][...]
            outs[t][...] = gv
            outs[n + t][...], outs[2 * n + t][...], outs[3 * n + t][...] = _adamw_math(
                ins[t][...], gv, ins[2 * n + t][...], ins[3 * n + t][...])

    shapes = [jax.ShapeDtypeStruct(w.shape, F32) for w in ws]
    res = pl.pallas_call(body, out_shape=shapes * 4, name=name)(*ws, *gs, *ms, *vs)
    return res[:n], res[n:2 * n], res[2 * n:3 * n], res[3 * n:]


def pair_add(g, r1, c_arr, *, name):
    _, rows, cdim = g.shape
    h = rows // 2

    def body(c_ref, g_ref, r_ref, o_ref):
        o_ref[...] = (g_ref[...].astype(F32) + r_ref[...].astype(F32)).astype(o_ref.dtype)

    (out,), _ = _call(
        body, grid=(N_CHIPS,), prefetch=(c_arr,),
        in_specs=[pl.BlockSpec((None, h, cdim), lambda s, c: (s, c[0], 0)), pl.BlockSpec((None, h, cdim), lambda s, c: (s, 0, 0))],
        out_specs=[pl.BlockSpec((None, h, cdim), lambda s, c: (s, 0, 0))],
        out_shape=[jax.ShapeDtypeStruct((N_CHIPS, h, cdim), g.dtype)], operands=(g, r1), sem=("parallel",), name=name)
    return out


def final_add(g, r1, r2, jc_arr, *, dest_shape, lead, prev, name):
    _, rows, cdim = g.shape
    h = rows // 2

    def body(jc_ref, g_ref, r1_ref, r2_ref, *rest):
        o_ref = rest[-1]
        acc = g_ref[...].astype(F32) + r1_ref[...].astype(F32)
        for k in range(3):
            acc = acc + r2_ref[k].astype(F32)
        o_ref[...] = acc

    if lead is None:
        o_spec = pl.BlockSpec((h, cdim), lambda i, jc: (jc[1], 0))
    elif lead == "chip":
        o_spec = pl.BlockSpec((None, h, cdim), lambda i, jc: (jc[0], jc[1], 0))
    else:
        o_spec = pl.BlockSpec((None, h, cdim), lambda i, jc: (lead, jc[1], 0))
    in_specs = [pl.BlockSpec((None, h, cdim), lambda i, jc: (jc[0], jc[1], 0)),
                pl.BlockSpec((None, h, cdim), lambda i, jc: (jc[0], 0, 0)),
                pl.BlockSpec((3, h, cdim), lambda i, jc: (0, 0, 0))]
    operands = [g, r1, r2]
    aliases = None
    if prev is not None:
        in_specs.append(ANY)
        operands.append(prev)
        aliases = {3: 0}
    (out,), _ = _call(body, grid=(1,), prefetch=(jc_arr,), in_specs=in_specs, out_specs=[o_spec],
                      out_shape=[jax.ShapeDtypeStruct(dest_shape, F32)], operands=operands, aliases=aliases, name=name)
    return out


def _place():
    return lax.axis_index("x"), lax.axis_index("y"), lax.axis_index("c")


def _partner(x, y, k):
    return (1 - x if k >> 1 else x), (1 - y if k & 1 else y)


WHOLE = (0, 1, 1)


def _half(rows, sel, dtype, piece=WHOLE):
    lo, hi, n = piece
    align = 16 if dtype == BF16 else 8
    step = rows // 2 // n
    assert rows // 2 == step * n and step % align == 0
    return pl.ds(pl.multiple_of(sel * (rows // 2) + lo * step, align), (hi - lo) * step)


def _rider(peers, inputs, aliased, fresh, nsem, copies, arrivals):
    def start(ins, outs, send, recv):
        for cp in copies(ins, outs, send, recv):
            cp.start()

    def finish(ins, outs, send, recv):
        for cp in arrivals(ins, outs, send, recv):
            cp.wait_recv()
        for cp in copies(ins, outs, send, recv):
            cp.wait_send()

    return types.SimpleNamespace(peers=peers, inputs=list(inputs), aliased=list(aliased), fresh=list(fresh), nsem=nsem,
                                 start=start, finish=finish)


def _remote(src, dst, send, recv, idx, dev):
    return pltpu.make_async_remote_copy(src_ref=src, dst_ref=dst, send_sem=send.at[idx], recv_sem=recv.at[idx],
                                        device_id=dev, device_id_type=MESH)


def gather_ici_rider(fulls, pieces=None):
    nt = len(fulls)
    pieces = pieces or [WHOLE] * nt

    def region(outs, t, slot, sel):
        return outs[t].at[slot, _half(fulls[t].shape[1], sel, fulls[t].dtype, pieces[t])]

    def copies(ins, outs, send, recv):
        x, y, c = _place()
        res = []
        for t in range(nt):
            for k in (1, 2, 3):
                px, py = _partner(x, y, k)
                mine = region(outs, t, 2 * x + y, c)
                res.append(_remote(mine, mine, send, recv, 3 * t + k - 1, (px, py, c)))
        return res

    def arrivals(ins, outs, send, recv):
        x, y, c = _place()
        res = []
        for t in range(nt):
            for k in (1, 2, 3):
                px, py = _partner(x, y, k)
                theirs = region(outs, t, 2 * px + py, c)
                res.append(_remote(theirs, theirs, send, recv, 3 * t + k - 1, (x, y, c)))
        return res

    return _rider("chips", fulls, range(nt), [], 3 * nt, copies, arrivals)


def gather_d2d_rider(fulls, pieces=None):
    nt = len(fulls)
    pieces = pieces or [WHOLE] * nt

    def region(outs, t, slot, sel):
        return outs[t].at[slot, _half(fulls[t].shape[1], sel, fulls[t].dtype, pieces[t])]

    def both(outs, send, recv, mine):
        x, y, c = _place()
        res = []
        for t in range(nt):
            for k in (1, 2, 3):
                px, py = _partner(x, y, k)
                part = region(outs, t, 2 * px + py, c if mine else 1 - c)
                res.append(_remote(part, part, send, recv, 3 * t + k - 1, (x, y, 1 - c)))
        return res

    return _rider("sibling", fulls, range(nt), [], 3 * nt, lambda i, o, s, r: both(o, s, r, True),
                  lambda i, o, s, r: both(o, s, r, False))


def exchange_rider(grads):
    nt = len(grads)

    def both(ins, outs, send, recv):
        x, y, c = _place()
        return [_remote(ins[t].at[:, _half(grads[t].shape[1], 1 - c, grads[t].dtype)], outs[t], send, recv, t, (x, y, 1 - c))
                for t in range(nt)]

    fresh = [jax.ShapeDtypeStruct((N_CHIPS, g.shape[1] // 2, g.shape[2]), g.dtype) for g in grads]
    return _rider("sibling", grads, [], fresh, nt, both, both)


def scatter_rider(parts):
    nt = len(parts)

    def both(ins, outs, send, recv):
        x, y, c = _place()
        res = []
        for t in range(nt):
            for k in (1, 2, 3):
                px, py = _partner(x, y, k)
                res.append(_remote(ins[t].at[2 * px + py], outs[t].at[k - 1], send, recv, 3 * t + k - 1, (px, py, c)))
        return res

    fresh = [jax.ShapeDtypeStruct((3,) + p.shape[1:], p.dtype) for p in parts]
    return _rider("chips", parts, [], fresh, 3 * nt, both, both)


def broadcast_rider(bufs, items):
    def region(outs, item, sel):
        bi, lead = item
        ref = outs[bi]
        if lead == "chip":
            x, y, _ = _place()
            ref = ref.at[2 * x + y]
        elif lead is not None:
            ref = ref.at[lead]
        return ref.at[_half(ref.shape[0], sel, F32)]

    def both(outs, send, recv, mine):
        x, y, c = _place()
        res = []
        for i, item in enumerate(items):
            part = region(outs, item, c if mine else 1 - c)
            res.append(_remote(part, part, send, recv, i, (x, y, 1 - c)))
        return res

    return _rider("sibling", bufs, range(len(bufs)), [], len(items), lambda i, o, s, r: both(o, s, r, True),
                  lambda i, o, s, r: both(o, s, r, False))


def allcast_rider(buf):
    peers = [(k, flip) for k in range(N_CHIPS) for flip in (0, 1) if (k, flip) != (0, 0)]

    def both(outs, send, recv, mine):
        x, y, c = _place()
        res = []
        for i, (k, flip) in enumerate(peers):
            px, py = _partner(x, y, k)
            pc = 1 - c if flip else c
            slot, sel = (2 * x + y, c) if mine else (2 * px + py, pc)
            part = outs[0].at[slot, _half(buf.shape[1], sel, F32)]
            res.append(_remote(part, part, send, recv, i, (px, py, pc)))
        return res

    return _rider("everyone", [buf], [0], [], len(peers), lambda i, o, s, r: both(o, s, r, True),
                  lambda i, o, s, r: both(o, s, r, False))


def comm_call(riders, *, name):
    _, res = _call(None, riders=riders, name=name)
    return res


SEMS = pl.BlockSpec(memory_space=pltpu.SEMAPHORE)
SIDE_EFFECT = pltpu.SideEffectType.DATAFLOW_SIDE_EFFECTING


def _split_refs(riders, refs):
    views, p = [], 0
    for r in riders:
        bufs = refs[p:p + len(r.inputs) + len(r.fresh)]
        p += len(bufs)
        ins = bufs[:len(r.inputs)]
        views.append([ins, [ins[i] for i in r.aliased] + list(bufs[len(r.inputs):])])
    for view in views:
        view += [refs[p], refs[p + 1]]
        p += 2
    return views


def comm_start(riders, *, name):
    kind = _peer_kind(riders)
    bufs = [a for r in riders for a in r.inputs]
    fresh = [f for r in riders for f in r.fresh]
    n_buf, n_fresh = len(bufs), len(fresh)

    def body(*refs):
        ins, outs = refs[:n_buf], refs[n_buf:]
        through, land, sems = outs[:n_buf], outs[n_buf:n_buf + n_fresh], outs[n_buf + n_fresh:-1]
        _peer_barrier(kind)
        per_rider, pb, pf = [], 0, 0
        for r in riders:
            per_rider += list(through[pb:pb + len(r.inputs)]) + list(land[pf:pf + len(r.fresh)])
            pb, pf = pb + len(r.inputs), pf + len(r.fresh)
        for r, (r_ins, r_outs, send, recv) in zip(riders, _split_refs(riders, per_rider + list(sems))):
            r.start(r_ins, r_outs, send, recv)
        outs[-1][...] = jnp.zeros((8, LANES), F32)

    sem_shapes = [pltpu.SemaphoreType.DMA((r.nsem,)) for r in riders for _ in (0, 1)]
    res = pl.pallas_call(
        body, name=name, in_specs=[ANY] * n_buf,
        out_specs=[ANY] * (n_buf + n_fresh) + [SEMS] * len(sem_shapes) + [pl.BlockSpec(memory_space=pltpu.VMEM)],
        out_shape=[jax.ShapeDtypeStruct(a.shape, a.dtype) for a in bufs] + fresh + sem_shapes
        + [jax.ShapeDtypeStruct((8, LANES), F32)],
        input_output_aliases={i: i for i in range(n_buf)},
        compiler_params=pltpu.CompilerParams(has_side_effects=SIDE_EFFECT, collective_id=PEER_KINDS.index(kind)))(*bufs)
    return (riders, list(res[:n_buf + n_fresh]), list(res[n_buf + n_fresh:-1])), res[-1]


def comm_wait(state, after, *, name):
    riders, bufs, sems = state
    n_buf, n_sem = len(bufs), len(sems)
    n_in = sum(len(r.inputs) for r in riders)

    def body(*refs):
        held, sem_refs = refs[:n_buf], refs[n_buf:n_buf + n_sem]
        through, land = held[:n_in], held[n_in:]
        per_rider, pb, pf = [], 0, 0
        for r in riders:
            per_rider += list(through[pb:pb + len(r.inputs)]) + list(land[pf:pf + len(r.fresh)])
            pb, pf = pb + len(r.inputs), pf + len(r.fresh)
        for r, (r_ins, r_outs, send, recv) in zip(riders, _split_refs(riders, per_rider + list(sem_refs))):
            r.finish(r_ins, r_outs, send, recv)

    res = pl.pallas_call(
        body, name=name, in_specs=[ANY] * n_buf + [SEMS] * n_sem + [ANY], out_specs=[ANY] * n_buf,
        out_shape=[jax.ShapeDtypeStruct(a.shape, a.dtype) for a in bufs],
        input_output_aliases={i: i for i in range(n_buf)},
        compiler_params=pltpu.CompilerParams(has_side_effects=SIDE_EFFECT))(*bufs, *sems, after)
    through, land = list(res[:n_in]), list(res[n_in:])
    out, pb, pf = [], 0, 0
    for r in riders:
        r_ins, r_land = through[pb:pb + len(r.inputs)], land[pf:pf + len(r.fresh)]
        pb, pf = pb + len(r.inputs), pf + len(r.fresh)
        out.append([r_ins[i] for i in r.aliased] + r_land)
    return out


SLAB_ROWS = 192


def _pad_rows(a, rows=8):
    return jnp.pad(a, ((0, rows - a.shape[0]), (0, 0)))


def _pack_small(norm_grads, db_qkv, db_o, dsinks, db_sp, dln_g, dln_b, dw_sp, loss_part):
    parts = [
        jnp.concatenate(norm_grads, axis=0),
        _pad_rows(jnp.pad(db_qkv, ((0, 0), (0, 2 * D_MODEL - QKV_WIDTH))).reshape(2, D_MODEL)),
        _pad_rows(db_o),
        _pad_rows(jnp.pad(dsinks.reshape(1, N_Q_HEADS), ((0, 0), (0, D_MODEL - N_Q_HEADS)))),
        _pad_rows(db_sp.reshape(1, D_MODEL)),
        _pad_rows(jnp.concatenate([dln_g, dln_b, jnp.pad(loss_part[0:1], ((0, 0), (0, D_MODEL - LANES)))], axis=0)),
        dw_sp.reshape(SGU_CHUNK, D_MODEL),
    ]
    slab = jnp.concatenate(parts, axis=0)
    return jnp.pad(slab, ((0, SLAB_ROWS - slab.shape[0]), (0, 0))).reshape(N_CHIPS, SLAB_ROWS // N_CHIPS, D_MODEL)


def _unpack_small(slab, j):
    slab = slab.reshape(SLAB_ROWS, D_MODEL)
    norms = [slab[2 * i:2 * i + 2] for i in range(4)]
    db_qkv = slab[8:10].reshape(1, 2 * D_MODEL)[:, :QKV_WIDTH]
    db_o = slab[16:17]
    dsinks = slab[24:25, :N_Q_HEADS]
    db_sp = slab[32:33].reshape(SGU_GROUPS, SGU_CHUNK)
    width = D_MODEL // N_CHIPS
    dln_g = lax.dynamic_slice(slab[40:41], (0, j * width), (1, width))
    dln_b = lax.dynamic_slice(slab[41:42], (0, j * width), (1, width))
    dw_sp = slab[48:48 + SGU_CHUNK].reshape(SGU_GROUPS * SGU_CHUNK, SGU_CHUNK)
    return norms, db_qkv, db_o, dsinks, db_sp, dln_g, dln_b, dw_sp, slab[42, 0]


class _GradReduce:
    def __init__(self, c_arr, jc_arr, dest_shapes):
        self.c_arr, self.jc_arr, self.dest_shapes = c_arr, jc_arr, dest_shapes
        self.grad, self.sibling, self.pair, self.chips, self.dest = {}, {}, {}, {}, {}

    def exchange(self, tags):
        return exchange_rider([self.grad[t] for t in tags])

    def exchanged(self, tags, res):
        for t, r in zip(tags, res):
            self.sibling[t] = r
            self.pair[t] = pair_add(self.grad[t], r, self.c_arr, name=f"pair_add_{t}")

    def scatter(self, tags):
        return scatter_rider([self.pair[t] for t in tags])

    def scattered(self, tags, res, where):
        for t, r in zip(tags, res):
            name, lead = where[t]
            self.dest[name] = final_add(self.grad[t], self.sibling[t], r, self.jc_arr, dest_shape=self.dest_shapes[name],
                                        lead=lead, prev=self.dest.get(name), name=f"final_add_{t}")

    def broadcast(self, items):
        names = []
        for n, _ in items:
            if n not in names:
                names.append(n)
        return names, broadcast_rider([self.dest[n] for n in names], [(names.index(n), lead) for n, lead in items])

    def broadcasted(self, names, res):
        for n, r in zip(names, res):
            self.dest[n] = r


def kernel(x, norm_mix_pre, norm_mix_post, norm_ffn_pre, norm_ffn_post, attn_w_qkv, attn_b_qkv, attn_sinks, attn_w_o, attn_b_o, sgu_w_in, sgu_ln_g, sgu_ln_b, sgu_w_spatial, sgu_b_spatial, sgu_w_out, ffn_w_gate_up, ffn_w_down, loss_target, m_norm_mix_pre, m_norm_mix_post, m_norm_ffn_pre, m_norm_ffn_post, m_attn_w_qkv, m_attn_b_qkv, m_attn_sinks, m_attn_w_o, m_attn_b_o, m_sgu_w_in, m_sgu_ln_g, m_sgu_ln_b, m_sgu_w_spatial, m_sgu_b_spatial, m_sgu_w_out, m_ffn_w_gate_up, m_ffn_w_down, v_norm_mix_pre, v_norm_mix_post, v_norm_ffn_pre, v_norm_ffn_post, v_attn_w_qkv, v_attn_b_qkv, v_attn_sinks, v_attn_w_o, v_attn_b_o, v_sgu_w_in, v_sgu_ln_g, v_sgu_ln_b, v_sgu_w_spatial, v_sgu_b_spatial, v_sgu_w_out, v_ffn_w_gate_up, v_ffn_w_down):
    s = x.shape[1]
    x0 = x.reshape(s, D_MODEL)
    target = loss_target.reshape(s, D_MODEL)
    mx, my, mc = lax.axis_index("x"), lax.axis_index("y"), lax.axis_index("c")
    chip = 2 * mx + my
    chip_arr = jnp.reshape(chip, (1,)).astype(I32)
    c_arr = jnp.reshape(mc, (1,)).astype(I32)
    jc_arr = jnp.stack([chip, mc]).astype(I32)
    zero_bias = jnp.zeros((1, D_MODEL), F32)

    def gain(p, i):
        return p[i:i + 1]

    big = [attn_w_qkv, attn_w_o, sgu_w_in, sgu_w_out, ffn_w_gate_up, ffn_w_gate_up, ffn_w_down, ffn_w_down]
    layers = [0, 0, 0, 0, 0, 1, 0, 1]
    tags = ["qkv", "wo", "win", "wout", "wgu0", "wgu1", "wd0", "wd1"]
    full = {t: place_shard(w, l, chip_arr, BF16, name=f"place_{t}") for w, l, t in zip(big, layers, tags)
            if t in ("qkv", "wo")}
    ln_pack = _pad_rows(jnp.concatenate([sgu_ln_g, sgu_ln_b], axis=0), 16)[None]
    full["ln"] = place_shard(ln_pack, 0, chip_arr, F32, name="place_ln")

    def split(items):
        return [i if isinstance(i, str) else i[0] for i in items], [WHOLE if isinstance(i, str) else tuple(i[1:]) for i in items]

    def ici(*items):
        names, pieces = split(items)
        return gather_ici_rider([full[n] for n in names], pieces)

    def d2d(*items):
        names, pieces = split(items)
        return gather_d2d_rider([full[n] for n in names], pieces)

    def landed(items, res):
        for n, r in zip(split(items)[0], res):
            full[n] = r

    cos, sin = _rope_tables(s)
    sink_rows = jnp.broadcast_to(
        jnp.repeat(attn_sinks.reshape(N_KV_HEADS, GQA_GROUP), WINDOW, axis=1)[:, None, :], (N_KV_HEADS, 8, ROWS))
    w_sp = sgu_w_spatial.reshape(SGU_GROUPS, SGU_CHUNK, SGU_CHUNK)
    b_sp = jnp.broadcast_to(sgu_b_spatial.reshape(SGU_GROUPS, SGU_CHUNK)[:, :, None], (SGU_GROUPS, SGU_CHUNK, LANES))

    (h0, full["wgu0"]), (res,) = prenorm_and_place(x0, gain(norm_mix_pre, 0), ffn_w_gate_up, 0, chip_arr, name="prenorm_0",
                                                   riders=[ici("qkv", "ln")])
    landed(("qkv", "ln"), res)
    full["wgu1"], (res,) = place_shard(ffn_w_gate_up, 1, chip_arr, BF16, name="place_wgu1", riders=[d2d("qkv", "ln")])
    landed(("qkv", "ln"), res)
    ln_g = full["ln"][:, 0, :].reshape(1, D_MODEL)
    ln_b = full["ln"][:, 1, :].reshape(1, D_MODEL)

    def hosted(call, stages):
        outputs, results = call([{"ici": ici, "d2d": d2d}[kind](*items) for kind, items in stages])
        for (_, items), res in zip(stages, results):
            landed(items, res)
        return outputs

    casts = [(ffn_w_down, 0), (sgu_w_in, 0), (ffn_w_down, 1), (sgu_w_out, 0)]
    qkv, full["wd0"], full["win"], full["wd1"], full["wout"] = hosted(
        lambda r: qkv_proj(h0, full["qkv"], attn_b_qkv, cos, sin, casts, chip_arr, name="qkv_proj", riders=r),
        [("ici", ("wo", ("wgu0", 0, 3, 8)))])
    o = hosted(lambda r: attn_fwd(qkv, sink_rows, name="attn_fwd", riders=r),
               [("d2d", ("wo",)), ("ici", (("wgu0", 3, 8, 8), ("wd0", 0, 4, 11)))])
    w_o = full["wo"].reshape(Q_WIDTH, D_MODEL)
    x1, h1, m0 = hosted(lambda r: proj_residual_norm(o, w_o, x0, attn_b_o, gain(norm_mix_post, 0), gain(norm_ffn_pre, 0),
                                                     name="attn_out_norm", riders=r),
                        [("d2d", ("wgu0",)), ("ici", (("wd0", 4, 11, 11),))])
    gu0, a0 = hosted(lambda r: ffn_up(h1, full["wgu0"], name="ffn_up_0", riders=r),
                     [("d2d", ("wd0",)), ("ici", ("win", "wout", ("wgu1", 0, 7, 16)))])
    w_d0 = full["wd0"].reshape(D_FF, D_MODEL)
    x2, h2, f0 = hosted(lambda r: proj_residual_norm(a0, w_d0, x1, zero_bias, gain(norm_ffn_post, 0), gain(norm_mix_pre, 1),
                                                     name="ffn_down_norm_0", riders=r),
                        [("d2d", ("win", "wout")), ("ici", (("wgu1", 7, 16, 16),))])
    w_in = full["win"]
    z, y = hosted(lambda r: sgu_in_fwd(h2, w_in, ln_g, ln_b, w_sp, b_sp, name="sgu_in_fwd", riders=r),
                  [("d2d", ("wgu1",)), ("ici", ("wd1",))])
    w_out = full["wout"].reshape(D_MODEL, D_MODEL)
    x3, h3, m1 = hosted(lambda r: proj_residual_norm(y, w_out, x2, zero_bias, gain(norm_mix_post, 1), gain(norm_ffn_pre, 1),
                                                     name="sgu_out_norm", riders=r),
                        [("d2d", ("wd1",))])
    w_qkv, w_gu0, w_gu1 = full["qkv"], full["wgu0"], full["wgu1"]
    w_d1 = full["wd1"].reshape(D_FF, D_MODEL)
    gu1, a1, dx4, df1, dg_fpost1, loss_part = ffn_fwd_loss_rows(
        h3, w_gu1, w_d1, x3, gain(norm_ffn_post, 1), target, name="ffn_fwd_loss_rows")

    red = _GradReduce(c_arr, jc_arr, {
        "qkv": attn_w_qkv.shape[1:], "wo": attn_w_o.shape[1:], "win": sgu_w_in.shape[1:], "wout": sgu_w_out.shape[1:],
        "wgu": ffn_w_gate_up.shape, "wd": ffn_w_down.shape, "slab": (N_CHIPS, SLAB_ROWS // N_CHIPS, D_MODEL)})
    where = {"qkv": ("qkv", None), "wo": ("wo", None), "win": ("win", None), "wout": ("wout", None), "wgu0": ("wgu", 0),
             "wgu1": ("wgu", 1), "wd0": ("wd", 0), "wd1": ("wd", 1), "small": ("slab", "chip")}

    dgu1, dx3, dm1, dg_fpre1, dg_mpost1, _ = ffn_bwd_rows(
        df1, w_d1, gu1, w_gu1, dx4, x3, gain(norm_ffn_pre, 1), m1, gain(norm_mix_post, 1), name="ffn_bwd_rows_1")
    red.grad["wd1"] = mm_tn(a1, df1, shard_major=False, tm=256, tn=D_MODEL, name="dw_down_1").reshape(
        N_CHIPS, D_FF // N_CHIPS, D_MODEL)
    red.grad["wgu1"], (res,) = mm_tn(h3, dgu1, shard_major=True, tm=512, tn=FF_HALF, name="dw_gate_up_1",
                                     riders=[red.exchange(["wd1"])])
    red.exchanged(["wd1"], res)
    dy, (res,) = mm_nt(dm1, w_out, out_dtype=F32, name="dy_sgu", riders=[red.exchange(["wgu1"])])
    red.exchanged(["wgu1"], res)
    red.grad["wout"] = mm_tn(y, dm1, shard_major=False, tm=512, tn=D_MODEL, name="dw_sgu_out").reshape(
        N_CHIPS, D_MODEL // N_CHIPS, D_MODEL)
    (dz, dw_sp, db_sp, dln_g, dln_b), (res_a, res_b) = sgu_bwd(
        z, dy, ln_g, ln_b, w_sp, b_sp, name="sgu_bwd", riders=[red.scatter(["wd1"]), red.exchange(["wout"])])
    red.scattered(["wd1"], res_a, where)
    red.exchanged(["wout"], res_b)
    names, rider = red.broadcast([("wd", 1)])
    red.grad["win"], (res_a, res_b) = mm_tn(h2, dz, shard_major=True, tm=D_MODEL, tn=2 * D_MODEL // N_CHIPS, name="dw_sgu_in",
                                            riders=[rider, red.scatter(["wout"])])
    red.broadcasted(names, res_a)
    red.scattered(["wout"], res_b, where)
    names, rider = red.broadcast([("wout", None)])
    (dx2, df0, dg_mpre1, dg_fpost0, _), (res_a, res_b) = dh_norm_bwd_pair(
        dz, w_in, dx3, x2, gain(norm_mix_pre, 1), f0, gain(norm_ffn_post, 0), name="dh_sgu_norm",
        riders=[red.exchange(["win"]), rider])
    red.exchanged(["win"], res_a)
    red.broadcasted(names, res_b)
    (dgu0, dx1, dm0, dg_fpre0, dg_mpost0, db_o), (res,) = ffn_bwd_rows(
        df0, w_d0, gu0, w_gu0, dx2, x1, gain(norm_ffn_pre, 0), m0, gain(norm_mix_post, 0), name="ffn_bwd_rows_0",
        riders=[red.scatter(["wgu1", "win"])])
    red.scattered(["wgu1", "win"], res, where)
    names, rider = red.broadcast([("wgu", 1), ("win", None)])
    dw_d0, (res,) = mm_tn(a0, df0, shard_major=False, tm=256, tn=D_MODEL, name="dw_down_0", riders=[rider])
    red.broadcasted(names, res)
    red.grad["wd0"] = dw_d0.reshape(N_CHIPS, D_FF // N_CHIPS, D_MODEL)
    do, (res,) = mm_nt(dm0, w_o, out_dtype=BF16, name="do_attn", riders=[red.exchange(["wd0"])])
    red.exchanged(["wd0"], res)
    red.grad["wgu0"], (res,) = mm_tn(h1, dgu0, shard_major=True, tm=512, tn=FF_HALF, name="dw_gate_up_0",
                                     riders=[red.scatter(["wd0"])])
    red.scattered(["wd0"], res, where)
    names, rider = red.broadcast([("wd", 0)])
    dw_o, (res_a, res_b) = mm_tn(o, dm0, shard_major=False, tm=512, tn=D_MODEL, name="dw_attn_out",
                                 riders=[red.exchange(["wgu0"]), rider])
    red.exchanged(["wgu0"], res_a)
    red.broadcasted(names, res_b)
    red.grad["wo"] = dw_o.reshape(N_CHIPS, Q_WIDTH // N_CHIPS, D_MODEL)
    (dq, dkc, dkp, dvc, dvp, dsink), (res_a, res_b) = attn_bwd(
        qkv, sink_rows, do, name="attn_bwd", riders=[red.scatter(["wgu0"]), red.exchange(["wo"])])
    red.scattered(["wgu0"], res_a, where)
    red.exchanged(["wo"], res_b)
    names, rider = red.broadcast([("wgu", 0)])
    (dqkv, db_qkv), (res,) = rope_bwd(dq, dkc, dkp, dvc, dvp, cos, sin, name="rope_bwd", riders=[rider])
    red.broadcasted(names, res)
    red.grad["qkv"], (res,) = mm_tn(h0, dqkv, shard_major=True, tm=D_MODEL, tn=QKV_WIDTH // N_CHIPS, name="dw_qkv",
                                    riders=[red.scatter(["wo"])])
    red.scattered(["wo"], res, where)
    grad_x, dg_mpre0 = dh_norm_bwd_last(dqkv, w_qkv, dx1, x0, gain(norm_mix_pre, 0), name="dh_attn_norm_in")

    norm_grads = [jnp.concatenate(p, axis=0) for p in
                  ((dg_mpre0, dg_mpre1), (dg_mpost0, dg_mpost1), (dg_fpre0, dg_fpre1), (dg_fpost0, dg_fpost1))]
    red.grad["small"] = _pack_small(norm_grads, db_qkv, db_o, dsink[:, :, 0, 0], db_sp[:, :, 0], dln_g, dln_b, dw_sp,
                                    loss_part)
    def big_update(w, g, m, v, tag, after=None):
        return adamw(w, g.reshape(w.shape), m, v, name=f"adamw_{tag}", after=after)

    (res,) = comm_call([red.exchange(["qkv", "small"])], name="tail_1")
    red.exchanged(["qkv", "small"], res)
    state, token = comm_start([red.scatter(["qkv", "small"])], name="tail_2_start")
    upd_wgu = big_update(ffn_w_gate_up, red.dest["wgu"], m_ffn_w_gate_up, v_ffn_w_gate_up, "wgu", after=token)
    (res,) = comm_wait(state, upd_wgu[1], name="tail_2_wait")
    red.scattered(["qkv", "small"], res, where)
    names, rider = red.broadcast([("qkv", None), ("wo", None)])
    state, token = comm_start([rider, allcast_rider(red.dest["slab"])], name="tail_3_start")
    upd_wd = big_update(ffn_w_down, red.dest["wd"], m_ffn_w_down, v_ffn_w_down, "wd", after=token)
    res, (slab_full,) = comm_wait(state, upd_wd[1], name="tail_3_wait")
    red.broadcasted(names, res)
    g_qkv, g_wo, g_win, g_wout = (red.dest[n] for n in ("qkv", "wo", "win", "wout"))
    g_norms, g_bqkv, g_bo, g_sinks, g_bsp, g_lng, g_lnb, g_wsp, loss = _unpack_small(slab_full, chip)

    upd = {
        "attn_w_qkv": big_update(attn_w_qkv, g_qkv, m_attn_w_qkv, v_attn_w_qkv, "qkv"),
        "attn_w_o": big_update(attn_w_o, g_wo, m_attn_w_o, v_attn_w_o, "wo"),
        "sgu_w_in": big_update(sgu_w_in, g_win, m_sgu_w_in, v_sgu_w_in, "win"),
        "sgu_w_out": big_update(sgu_w_out, g_wout, m_sgu_w_out, v_sgu_w_out, "wout"),
        "ffn_w_gate_up": upd_wgu,
        "ffn_w_down": upd_wd,
    }
    small_names = ["norm_mix_pre", "norm_mix_post", "norm_ffn_pre", "norm_ffn_post", "attn_b_qkv", "attn_sinks", "attn_b_o",
                   "sgu_ln_g", "sgu_ln_b", "sgu_w_spatial", "sgu_b_spatial"]
    small_w = [norm_mix_pre, norm_mix_post, norm_ffn_pre, norm_ffn_post, attn_b_qkv, attn_sinks, attn_b_o, sgu_ln_g, sgu_ln_b,
               sgu_w_spatial, sgu_b_spatial]
    small_m = [m_norm_mix_pre, m_norm_mix_post, m_norm_ffn_pre, m_norm_ffn_post, m_attn_b_qkv, m_attn_sinks, m_attn_b_o,
               m_sgu_ln_g, m_sgu_ln_b, m_sgu_w_spatial, m_sgu_b_spatial]
    small_v = [v_norm_mix_pre, v_norm_mix_post, v_norm_ffn_pre, v_norm_ffn_post, v_attn_b_qkv, v_attn_sinks, v_attn_b_o,
               v_sgu_ln_g, v_sgu_ln_b, v_sgu_w_spatial, v_sgu_b_spatial]
    small_g = g_norms + [g_bqkv, g_sinks, g_bo, g_lng, g_lnb, g_wsp, g_bsp]

    def flat2(a):
        return a.reshape(-1, a.shape[-1])

    res = adamw_small([flat2(a) for a in small_w], [flat2(a) for a in small_g], [flat2(a) for a in small_m],
                      [flat2(a) for a in small_v], name="adamw_small")
    for i, nm in enumerate(small_names):
        upd[nm] = tuple(r[i].reshape(small_w[i].shape) for r in res)

    order = ["norm_mix_pre", "norm_mix_post", "norm_ffn_pre", "norm_ffn_post", "attn_w_qkv", "attn_b_qkv", "attn_sinks",
             "attn_w_o", "attn_b_o", "sgu_w_in", "sgu_ln_g", "sgu_ln_b", "sgu_w_spatial", "sgu_b_spatial", "sgu_w_out",
             "ffn_w_gate_up", "ffn_w_down"]
    outs = [loss, grad_x.reshape(1, s, D_MODEL)]
    for part in range(4):
        outs += [upd[nm][part] for nm in order]
    return tuple(outs)
```

```python
import types

import numpy as np
import jax
import jax.numpy as jnp
from jax import lax
from jax.experimental import pallas as pl
from jax.experimental.pallas import tpu as pltpu

F32 = jnp.float32
BF16 = jnp.bfloat16
I32 = jnp.int32

D_MODEL = 1024
HEAD_DIM = 64
N_Q_HEADS = 16
N_KV_HEADS = 4
GQA_GROUP = 4
WINDOW = 128
Q_WIDTH = 1024
KV_WIDTH = 256
QKV_WIDTH = 1536
ROPE_THETA = 10000.0
SGU_GROUPS = 8
SGU_CHUNK = 128
D_FF = 2816
FF_HALF = D_FF // 2
EPS = 1e-6
N_CHIPS = 4
LANES = 128

ADAM_LR = 0.001
ADAM_B1 = 0.9
ADAM_B2 = 0.999
ADAM_EPS = 1e-08
ADAM_WD = 0.01
ADAM_STEP = 10

VMEM_LIMIT = 52 * 1024 * 1024
BIG_VMEM_LIMIT = 62 * 1024 * 1024
SUB_ROWS = 256
MESH = pl.DeviceIdType.MESH
NEG = -1e30
NT_DIMS = (((1,), (1,)), ((), ()))
TN_DIMS = (((0,), (0,)), ((), ()))
NN_DIMS = (((1,), (0,)), ((), ()))
ANY = pl.BlockSpec(memory_space=pl.ANY)


def _row_tile(s, want):
    return want if s % want == 0 else s


PEER_KINDS = ("sibling", "chips", "sibling+chips", "everyone")


def _peer_kind(riders):
    kinds = {r.peers for r in riders}
    if not kinds:
        return None
    if "everyone" in kinds:
        return "everyone"
    return "sibling+chips" if len(kinds) == 2 else kinds.pop()


def _peer_barrier(kind):
    x, y, c = _place()
    chips = [(*_partner(x, y, k), c) for k in (1, 2, 3)]
    peers = {"sibling": [(x, y, 1 - c)], "chips": chips, "sibling+chips": [(x, y, 1 - c)] + chips,
             "everyone": [(x, y, 1 - c)] + chips + [(px, py, 1 - c) for px, py, _ in chips]}[kind]
    barrier = pltpu.get_barrier_semaphore()
    for dev in peers:
        pl.semaphore_signal(barrier, inc=1, device_id=dev, device_id_type=MESH)
    pl.semaphore_wait(barrier, len(peers))


def _call(body, *, name, grid=(), in_specs=(), out_specs=(), out_shape=(), scratch_shapes=(), operands=(), prefetch=(),
          aliases=None, riders=(), sem=None, vmem_limit=VMEM_LIMIT):
    n_pre, n_in, n_out, n_scr = len(prefetch), len(operands), len(out_shape), len(scratch_shapes)
    in_specs, out_specs, out_shape = list(in_specs), list(out_specs), list(out_shape)
    operands, scratch_shapes = list(operands), list(scratch_shapes)
    io_alias = {n_pre + i: o for i, o in (aliases or {}).items()}
    for r in riders:
        base_in, base_out = n_pre + len(operands), len(out_shape)
        operands += list(r.inputs)
        in_specs += [ANY] * len(r.inputs)
        for pos, i in enumerate(r.aliased):
            io_alias[base_in + i] = base_out + pos
            out_shape.append(jax.ShapeDtypeStruct(r.inputs[i].shape, r.inputs[i].dtype))
        out_shape += list(r.fresh)
        out_specs += [ANY] * (len(r.aliased) + len(r.fresh))
        scratch_shapes += [pltpu.SemaphoreType.DMA((r.nsem,)), pltpu.SemaphoreType.DMA((r.nsem,))]

    def wrapped(*refs):
        pre, p = refs[:n_pre], n_pre
        core_in, p = refs[p:p + n_in], p + n_in
        r_in = []
        for r in riders:
            r_in.append(refs[p:p + len(r.inputs)])
            p += len(r.inputs)
        core_out, p = refs[p:p + n_out], p + n_out
        r_out = []
        for r in riders:
            k = len(r.aliased) + len(r.fresh)
            r_out.append(refs[p:p + k])
            p += k
        core_scr, p = refs[p:p + n_scr], p + n_scr
        r_sem = [refs[p + 2 * i:p + 2 * i + 2] for i in range(len(riders))]

        def edge(at_last, fns):
            def run():
                if not at_last:
                    _peer_barrier(peer_kind)
                for i, r in enumerate(riders):
                    getattr(r, fns)(r_in[i], r_out[i], r_sem[i][0], r_sem[i][1])
            if not riders:
                return
            if not grid:
                run()
                return
            cond = None
            for d, n in enumerate(grid):
                c = pl.program_id(d) == (n - 1 if at_last else 0)
                cond = c if cond is None else jnp.logical_and(cond, c)
            pl.when(cond)(run)

        edge(False, "start")
        if body is not None:
            body(*pre, *core_in, *core_out, *core_scr)
        edge(True, "finish")

    if sem is None or riders:
        sem = ("arbitrary",) * len(grid)
    kwargs = dict(out_shape=out_shape, input_output_aliases=io_alias, name=name)
    peer_kind = _peer_kind(riders)
    collective = {} if peer_kind is None else {"collective_id": PEER_KINDS.index(peer_kind)}
    if grid:
        kwargs["compiler_params"] = pltpu.CompilerParams(dimension_semantics=sem, vmem_limit_bytes=vmem_limit, **collective)
    elif collective:
        kwargs["compiler_params"] = pltpu.CompilerParams(**collective)
    if n_pre:
        kwargs["grid_spec"] = pltpu.PrefetchScalarGridSpec(
            num_scalar_prefetch=n_pre, grid=grid, in_specs=in_specs, out_specs=out_specs, scratch_shapes=scratch_shapes)
    else:
        kwargs.update(grid=grid, in_specs=in_specs, out_specs=out_specs, scratch_shapes=scratch_shapes)
    res = pl.pallas_call(wrapped, **kwargs)(*prefetch, *operands)
    core, rest, rider_res = list(res[:n_out]), list(res[n_out:]), []
    for r in riders:
        k = len(r.aliased) + len(r.fresh)
        rider_res.append(rest[:k])
        rest = rest[k:]
    return core, rider_res


def _mm_call(*, grid, in_specs, out_spec, out_shape, dims, nk, kaxis, acc_shape, name, operands, riders=()):
    out_dtype = out_shape.dtype

    def body(a_ref, b_ref, o_ref, *scratch):
        p = lax.dot_general(a_ref[...].astype(BF16), b_ref[...].astype(BF16), dims, preferred_element_type=F32)
        if nk == 1:
            o_ref[...] = p.astype(out_dtype)
        else:
            acc = scratch[0]
            kk = pl.program_id(kaxis)

            @pl.when(kk == 0)
            def _():
                acc[...] = p

            @pl.when(kk > 0)
            def _():
                acc[...] += p

            @pl.when(kk == nk - 1)
            def _():
                o_ref[...] = acc[...].astype(out_dtype)

    sem = ["parallel"] * len(grid)
    if nk > 1:
        sem[kaxis] = "arbitrary"
    (out,), rider_res = _call(
        body, grid=grid, in_specs=in_specs, out_specs=[out_spec], out_shape=[out_shape],
        scratch_shapes=[pltpu.VMEM(acc_shape, F32)] if nk > 1 else [], operands=operands, name=name, riders=riders,
        sem=tuple(sem))
    return (out, rider_res) if riders else out


def mm_nt(a, w, *, out_dtype, name, tm=1024, riders=()):
    m, n = a.shape
    kout = w.shape[0]
    tm = _row_tile(m, tm)
    return _mm_call(grid=(m // tm,),
                    in_specs=[pl.BlockSpec((tm, n), lambda i: (i, 0)), pl.BlockSpec((kout, n), lambda i: (0, 0))],
                    out_spec=pl.BlockSpec((tm, kout), lambda i: (i, 0)),
                    out_shape=jax.ShapeDtypeStruct((m, kout), out_dtype), dims=NT_DIMS, nk=1, kaxis=0,
                    acc_shape=None, name=name, operands=(a, w), riders=riders)


def mm_tn(a, b, *, shard_major, name, tm, tn, tk=None, out_dtype=BF16, riders=()):
    s, m = a.shape
    tk = s if tk is None else _row_tile(s, tk)
    if b.ndim == 3:
        n = 2 * b.shape[2]
        b_spec = pl.BlockSpec((None, tk, tn), lambda j, i, kk: (j // 2, kk, j % 2))
    else:
        n = b.shape[1]
        b_spec = pl.BlockSpec((tk, tn), lambda j, i, kk: (kk, j))
    if shard_major:
        assert tn == n // N_CHIPS
        o_spec = pl.BlockSpec((None, tm, tn), lambda j, i, kk: (j, i, 0))
        o_shape = jax.ShapeDtypeStruct((N_CHIPS, m, tn), out_dtype)
    else:
        o_spec = pl.BlockSpec((tm, tn), lambda j, i, kk: (i, j))
        o_shape = jax.ShapeDtypeStruct((m, n), out_dtype)
    return _mm_call(grid=(n // tn, m // tm, s // tk),
                    in_specs=[pl.BlockSpec((tk, tm), lambda j, i, kk: (kk, i)), b_spec], out_spec=o_spec,
                    out_shape=o_shape, dims=TN_DIMS, nk=s // tk, kaxis=2, acc_shape=(tm, tn), name=name, operands=(a, b),
                    riders=riders)


def _rstd(x):
    return lax.rsqrt(jnp.mean(x * x, axis=-1, keepdims=True) + EPS)


def _rms_bwd(dy, x, g):
    r = _rstd(x)
    xhat = x * r
    gy = dy * g
    dx = r * (gy - xhat * jnp.mean(gy * xhat, axis=-1, keepdims=True))
    return dx, jnp.sum(dy * xhat, axis=0, keepdims=True)


def _accum(ref, val, first):
    @pl.when(first)
    def _():
        ref[...] = val

    @pl.when(jnp.logical_not(first))
    def _():
        ref[...] += val


def _row_spec(tm, width):
    return pl.BlockSpec((tm, width), lambda i: (i, 0))


def _vec_spec(width):
    return pl.BlockSpec((1, width), lambda i: (0, 0))


def _ret(core, rider_res, riders):
    core = core[0] if len(core) == 1 else core
    return (core, rider_res) if riders else core


def prenorm_and_place(x, g, w, layer, chip_arr, *, name, tm=256, riders=()):
    s = x.shape[0]
    tm = _row_tile(s, tm)
    steps = s // tm
    _, r, c = w.shape
    tr = r // steps
    assert tr * steps == r and tr % 16 == 0

    def body(chip_ref, x_ref, g_ref, w_ref, h_ref, o_ref):
        xv = x_ref[...]
        h_ref[...] = (xv * _rstd(xv) * g_ref[...]).astype(BF16)
        o_ref[...] = w_ref[...].astype(BF16)

    core, rr = _call(
        body, grid=(steps,), prefetch=(chip_arr,),
        in_specs=[pl.BlockSpec((tm, D_MODEL), lambda i, chip: (i, 0)), pl.BlockSpec((1, D_MODEL), lambda i, chip: (0, 0)),
                  pl.BlockSpec((None, tr, c), lambda i, chip: (layer, i, 0))],
        out_specs=[pl.BlockSpec((tm, D_MODEL), lambda i, chip: (i, 0)), pl.BlockSpec((None, tr, c), lambda i, chip: (chip[0], i, 0))],
        out_shape=[jax.ShapeDtypeStruct((s, D_MODEL), BF16), jax.ShapeDtypeStruct((N_CHIPS, r, c), BF16)],
        operands=(x, g, w), sem=("parallel",), name=name, riders=riders)
    return _ret(core, rr, riders)


def proj_residual_norm(a, w, x, bias, g_post, g_next, *, name, tm=512, sub=256, riders=()):
    s, k = a.shape
    tm = _row_tile(s, tm)
    sub = min(sub, tm)

    def body(a_ref, w_ref, x_ref, b_ref, gp_ref, gn_ref, xo_ref, h_ref, m_ref):
        for t in range(tm // sub):
            rows = slice(t * sub, (t + 1) * sub)
            mv = jnp.dot(a_ref[rows, :], w_ref[...], preferred_element_type=F32) + b_ref[...]
            m_ref[rows, :] = mv.astype(BF16)
            xn = x_ref[rows, :] + mv * _rstd(mv) * gp_ref[...]
            xo_ref[rows, :] = xn
            h_ref[rows, :] = (xn * _rstd(xn) * gn_ref[...]).astype(BF16)

    row, vec = _row_spec(tm, D_MODEL), _vec_spec(D_MODEL)
    core, rr = _call(
        body, grid=(s // tm,),
        in_specs=[_row_spec(tm, k), pl.BlockSpec((k, D_MODEL), lambda i: (0, 0)), row, vec, vec, vec], out_specs=[row, row, row],
        out_shape=[jax.ShapeDtypeStruct((s, D_MODEL), F32), jax.ShapeDtypeStruct((s, D_MODEL), BF16),
                   jax.ShapeDtypeStruct((s, D_MODEL), BF16)],
        operands=(a, w, x, bias, g_post, g_next), sem=("parallel",), name=name, riders=riders)
    return _ret(core, rr, riders)


def ffn_fwd_loss_rows(h, w_gu, w_d, x, g_post, target, *, name, tm=512, riders=()):
    s = x.shape[0]
    tm = _row_tile(s, tm)

    def body(h_ref, w0, w1, w2, w3, wd_ref, x_ref, g_ref, t_ref, d_ref, a_ref, dx_ref, df_ref, dg_ref, loss_ref):
        first = pl.program_id(0) == 0
        halves = [slice(half * FF_HALF, (half + 1) * FF_HALF) for half in (0, 1)]
        gain = g_ref[...]
        sub = min(SUB_ROWS, tm)
        sums = None
        for t in range(tm // sub):
            rows = slice(t * sub, (t + 1) * sub)
            hv = h_ref[rows, :]
            fv = None
            for cols, (wg_ref, wu_ref) in zip(halves, ((w0, w2), (w1, w3))):
                g = jnp.dot(hv, wg_ref[...], preferred_element_type=F32)
                u = jnp.dot(hv, wu_ref[...], preferred_element_type=F32)
                sig = _sigmoid(g)
                silu = g * sig
                d_ref[0, rows, cols] = (u * (sig + silu * (1.0 - sig))).astype(BF16)
                d_ref[1, rows, cols] = silu.astype(BF16)
                act = (silu * u).astype(BF16)
                a_ref[rows, cols] = act
                p = jnp.dot(act, wd_ref[cols, :], preferred_element_type=F32)
                fv = p if fv is None else fv + p
            err = x_ref[rows, :] + fv * _rstd(fv) * gain - t_ref[rows, :]
            dx = err * (1.0 / D_MODEL)
            dx_ref[rows, :] = dx
            df, dg = _rms_bwd(dx, fv, gain)
            df_ref[rows, :] = df.astype(BF16)
            part = (dg, jnp.sum(jnp.sum(err * err, axis=-1, keepdims=True), axis=0, keepdims=True) * (0.5 / D_MODEL))
            sums = part if sums is None else tuple(a + b for a, b in zip(sums, part))
        _accum(dg_ref, sums[0], first)
        _accum(loss_ref, jnp.broadcast_to(sums[1], (8, LANES)), first)

    def resident(shape, index):
        return pl.BlockSpec(shape, index, pipeline_mode=pl.Buffered(1))

    row, vec = _row_spec(tm, D_MODEL), _vec_spec(D_MODEL)
    shards = [resident((None, D_MODEL, FF_HALF), (lambda j: (lambda i: (j, 0, 0)))(j)) for j in range(N_CHIPS)]
    core, rr = _call(
        body, grid=(s // tm,),
        in_specs=[row] + shards + [resident((D_FF, D_MODEL), lambda i: (0, 0)), row, vec, row],
        out_specs=[pl.BlockSpec((2, tm, D_FF), lambda i: (0, i, 0)), _row_spec(tm, D_FF), row, row, vec,
                   pl.BlockSpec((8, LANES), lambda i: (0, 0))],
        out_shape=[jax.ShapeDtypeStruct((2, s, D_FF), BF16), jax.ShapeDtypeStruct((s, D_FF), BF16),
                   jax.ShapeDtypeStruct((s, D_MODEL), F32), jax.ShapeDtypeStruct((s, D_MODEL), BF16),
                   jax.ShapeDtypeStruct((1, D_MODEL), F32), jax.ShapeDtypeStruct((8, LANES), F32)],
        operands=(h, w_gu, w_gu, w_gu, w_gu, w_d, x, g_post, target), name=name, riders=riders, vmem_limit=BIG_VMEM_LIMIT)
    return _ret(core, rr, riders)


def dh_norm_bwd_pair(a, w, dres, x, g_pre, m, g_post, *, name, tm=512, sub=256, riders=()):
    _, kout, ns = w.shape
    planes = a.ndim == 3
    s = x.shape[0]
    tm = _row_tile(s, tm)
    sub = min(sub, tm)
    a_spec = pl.BlockSpec((2, tm, 2 * ns), lambda i: (0, i, 0)) if planes else pl.BlockSpec((tm, N_CHIPS * ns), lambda i: (i, 0))

    def body(a_ref, w0, w1, w2, w3, dres_ref, x_ref, gpre_ref, m_ref, gpost_ref, dx_ref, dm_ref, dgpre_ref, dgpost_ref, db_ref):
        first = pl.program_id(0) == 0
        sums = None
        for t in range(tm // sub):
            rows = slice(t * sub, (t + 1) * sub)
            dh = None
            for j, w_ref in enumerate((w0, w1, w2, w3)):
                a_j = a_ref[j // 2, rows, (j % 2) * ns:(j % 2 + 1) * ns] if planes else a_ref[rows, j * ns:(j + 1) * ns]
                p = lax.dot_general(a_j, w_ref[...], NT_DIMS, preferred_element_type=F32)
                dh = p if dh is None else dh + p
            d1, dgpre = _rms_bwd(dh, x_ref[rows, :], gpre_ref[...])
            dx = dres_ref[rows, :] + d1
            dx_ref[rows, :] = dx
            dm, dgpost = _rms_bwd(dx, m_ref[rows, :].astype(F32), gpost_ref[...])
            dm_ref[rows, :] = dm.astype(BF16)
            part = (dgpre, dgpost, jnp.sum(dm, axis=0, keepdims=True))
            sums = part if sums is None else tuple(u + v for u, v in zip(sums, part))
        _accum(dgpre_ref, sums[0], first)
        _accum(dgpost_ref, sums[1], first)
        _accum(db_ref, sums[2], first)

    def shard(j):
        return pl.BlockSpec((None, kout, ns), lambda i: (j, 0, 0))

    row, vec = _row_spec(tm, D_MODEL), _vec_spec(D_MODEL)
    vshape = jax.ShapeDtypeStruct((1, D_MODEL), F32)
    core, rr = _call(
        body, grid=(s // tm,), in_specs=[a_spec] + [shard(j) for j in range(N_CHIPS)] + [row, row, vec, row, vec],
        out_specs=[row, row, vec, vec, vec],
        out_shape=[jax.ShapeDtypeStruct((s, D_MODEL), F32), jax.ShapeDtypeStruct((s, D_MODEL), BF16), vshape, vshape, vshape],
        operands=(a, w, w, w, w, dres, x, g_pre, m, g_post), name=name, riders=riders)
    return _ret(core, rr, riders)


def ffn_bwd_rows(df, w_d, d_planes, w_gu, dres, x, g_pre, m, g_post, *, name, tm=512, riders=()):
    s = x.shape[0]
    tm = _row_tile(s, tm)

    def body(df_ref, wd_ref, d_ref, w0, w1, w2, w3, dres_ref, x_ref, gpre_ref, m_ref, gpost_ref,
             o_ref, dx_ref, dm_ref, dgpre_ref, dgpost_ref, db_ref):
        first = pl.program_id(0) == 0
        halves = [slice(half * FF_HALF, (half + 1) * FF_HALF) for half in (0, 1)]
        sub = min(SUB_ROWS, tm)
        sums = None
        for t in range(tm // sub):
            rows = slice(t * sub, (t + 1) * sub)
            dfv = df_ref[rows, :]
            dh = None
            for cols, (wg_ref, wu_ref) in zip(halves, ((w0, w2), (w1, w3))):
                da = lax.dot_general(dfv, wd_ref[cols, :], NT_DIMS, preferred_element_type=F32)
                dg = (da * d_ref[0, rows, cols].astype(F32)).astype(BF16)
                du = (da * d_ref[1, rows, cols].astype(F32)).astype(BF16)
                o_ref[0, rows, cols] = dg
                o_ref[1, rows, cols] = du
                p = lax.dot_general(dg, wg_ref[...], NT_DIMS, preferred_element_type=F32)
                p += lax.dot_general(du, wu_ref[...], NT_DIMS, preferred_element_type=F32)
                dh = p if dh is None else dh + p
            d1, dgpre = _rms_bwd(dh, x_ref[rows, :], gpre_ref[...])
            dx = dres_ref[rows, :] + d1
            dx_ref[rows, :] = dx
            dm, dgpost = _rms_bwd(dx, m_ref[rows, :].astype(F32), gpost_ref[...])
            dm_ref[rows, :] = dm.astype(BF16)
            part = (dgpre, dgpost, jnp.sum(dm, axis=0, keepdims=True))
            sums = part if sums is None else tuple(a + b for a, b in zip(sums, part))
        _accum(dgpre_ref, sums[0], first)
        _accum(dgpost_ref, sums[1], first)
        _accum(db_ref, sums[2], first)

    def resident(shape, index):
        return pl.BlockSpec(shape, index, pipeline_mode=pl.Buffered(1))

    planes = pl.BlockSpec((2, tm, D_FF), lambda i: (0, i, 0))
    row, vec = _row_spec(tm, D_MODEL), _vec_spec(D_MODEL)
    vshape = jax.ShapeDtypeStruct((1, D_MODEL), F32)
    shards = [resident((None, D_MODEL, FF_HALF), (lambda j: (lambda i: (j, 0, 0)))(j)) for j in range(N_CHIPS)]
    core, rr = _call(
        body, grid=(s // tm,),
        in_specs=[row, resident((D_FF, D_MODEL), lambda i: (0, 0)), planes] + shards + [row, row, vec, row, vec],
        out_specs=[planes, row, row, vec, vec, vec],
        out_shape=[jax.ShapeDtypeStruct((2, s, D_FF), BF16), jax.ShapeDtypeStruct((s, D_MODEL), F32),
                   jax.ShapeDtypeStruct((s, D_MODEL), BF16), vshape, vshape, vshape],
        operands=(df, w_d, d_planes, w_gu, w_gu, w_gu, w_gu, dres, x, g_pre, m, g_post), name=name, riders=riders,
        vmem_limit=BIG_VMEM_LIMIT)
    return _ret(core, rr, riders)


def dh_norm_bwd_last(a, w, dres, x, g_pre, *, name, tm=512, sub=256):
    _, kout, ns = w.shape
    s = x.shape[0]
    tm = _row_tile(s, tm)
    sub = min(sub, tm)

    def body(a_ref, w0, w1, w2, w3, dres_ref, x_ref, g_ref, dx_ref, dg_ref):
        total = None
        for t in range(tm // sub):
            rows = slice(t * sub, (t + 1) * sub)
            dh = None
            for j, w_ref in enumerate((w0, w1, w2, w3)):
                p = lax.dot_general(a_ref[rows, j * ns:(j + 1) * ns], w_ref[...], NT_DIMS, preferred_element_type=F32)
                dh = p if dh is None else dh + p
            d1, dg = _rms_bwd(dh, x_ref[rows, :], g_ref[...])
            dx_ref[rows, :] = dres_ref[rows, :] + d1
            total = dg if total is None else total + dg
        _accum(dg_ref, total, pl.program_id(0) == 0)

    def shard(j):
        return pl.BlockSpec((None, kout, ns), lambda i: (j, 0, 0))

    row, vec = _row_spec(tm, D_MODEL), _vec_spec(D_MODEL)
    (dx, dg), _ = _call(
        body, grid=(s // tm,), in_specs=[_row_spec(tm, N_CHIPS * ns)] + [shard(j) for j in range(N_CHIPS)] + [row, row, vec],
        out_specs=[row, vec], out_shape=[jax.ShapeDtypeStruct((s, D_MODEL), F32), jax.ShapeDtypeStruct((1, D_MODEL), F32)],
        operands=(a, w, w, w, w, dres, x, g_pre), name=name)
    return dx, dg


def _rope_tables(s):
    half = HEAD_DIM // 2
    inv_freq = np.float32(ROPE_THETA) ** (-(np.arange(half, dtype=np.float32) * np.float32(2.0)) / np.float32(HEAD_DIM))
    ang = np.arange(s, dtype=np.float32)[:, None] * inv_freq[None, :]
    cos, sin = np.cos(ang).astype(np.float32), np.sin(ang).astype(np.float32)
    return jnp.asarray(np.tile(cos, (1, 4))), jnp.asarray(np.concatenate([-sin, sin, -sin, sin], axis=1))


def _swap_halves(x):
    lane = lax.broadcasted_iota(I32, x.shape, 1)
    return jnp.where((lane & (HEAD_DIM - 1)) < HEAD_DIM // 2, pltpu.roll(x, LANES - 32, 1), pltpu.roll(x, 32, 1))


N_ROPE_BLOCKS = (Q_WIDTH + KV_WIDTH) // LANES


def qkv_proj(h, w, bias, cos, sin, casts, chip_arr, *, name, tm=1024, riders=()):
    s, k = h.shape
    ns = w.shape[2]
    tm = _row_tile(s, tm)
    nc = len(casts)

    def body(chip_ref, h_ref, w_ref, b_ref, c_ref, s_ref, *rest):
        cast_in, o_ref, cast_out = rest[:nc], rest[nc], rest[nc + 1:]
        j = pl.program_id(0)

        @pl.when(jnp.logical_and(j == 0, pl.program_id(1) == 0))
        def _():
            for src, dst in zip(cast_in, cast_out):
                dst[...] = src[...].astype(BF16)

        sub = min(256, tm)
        for t in range(tm // sub):
            rows = slice(t * sub, (t + 1) * sub)
            p = jnp.dot(h_ref[rows, :], w_ref[...], preferred_element_type=F32) + b_ref[...]
            cosv, sinv = c_ref[rows, :], s_ref[rows, :]
            for blk in range(ns // LANES):
                xb = p[:, blk * LANES:(blk + 1) * LANES]
                roped = xb * cosv + _swap_halves(xb) * sinv
                is_qk = j * (ns // LANES) + blk < N_ROPE_BLOCKS
                o_ref[rows, blk * LANES:(blk + 1) * LANES] = jnp.where(is_qk, roped, xb).astype(BF16)

    def cast_in_spec(wt, layer):
        return pl.BlockSpec((None,) + wt.shape[1:], lambda j, i, chip: (layer, 0, 0), pipeline_mode=pl.Buffered(1))

    def cast_out_spec(wt):
        return pl.BlockSpec((None,) + wt.shape[1:], lambda j, i, chip: (chip[0], 0, 0), pipeline_mode=pl.Buffered(1))

    core, rr = _call(
        body, grid=(N_CHIPS, s // tm), prefetch=(chip_arr,),
        in_specs=[pl.BlockSpec((tm, k), lambda j, i, chip: (i, 0)), pl.BlockSpec((None, k, ns), lambda j, i, chip: (j, 0, 0)),
                  pl.BlockSpec((1, ns), lambda j, i, chip: (0, j)), pl.BlockSpec((tm, LANES), lambda j, i, chip: (i, 0)),
                  pl.BlockSpec((tm, LANES), lambda j, i, chip: (i, 0))] + [cast_in_spec(wt, layer) for wt, layer in casts],
        out_specs=[pl.BlockSpec((tm, ns), lambda j, i, chip: (i, j))] + [cast_out_spec(wt) for wt, _ in casts],
        out_shape=[jax.ShapeDtypeStruct((s, N_CHIPS * ns), BF16)]
        + [jax.ShapeDtypeStruct((N_CHIPS,) + wt.shape[1:], BF16) for wt, _ in casts],
        operands=(h, w, bias, cos, sin, *[wt for wt, _ in casts]), sem=("arbitrary", "arbitrary"), name=name, riders=riders)
    return (core, rr) if riders else core


def rope_bwd(dq, dkc, dkp, dvc, dvp, cos, sin, *, name, riders=()):
    s = dq.shape[0]
    tm = 2 * WINDOW if s % (2 * WINDOW) == 0 else WINDOW
    nb = s // tm

    def body(dq_ref, dkc_ref, dkp_ref, dkp_next_ref, dvc_ref, dvp_ref, dvp_next_ref, c_ref, s_ref, o_ref, db_ref):
        i = pl.program_id(0)
        has_next = (i < nb - 1).astype(F32)
        cosv, sinv = c_ref[...], s_ref[...]

        def shifted(ref, next_ref, cols):
            last = has_next * next_ref[:WINDOW, cols].astype(F32)
            return last if tm == WINDOW else jnp.concatenate([ref[WINDOW:, cols].astype(F32), last], axis=0)

        parts = []
        for blk in range(QKV_WIDTH // LANES):
            if blk < Q_WIDTH // LANES:
                g = dq_ref[:, blk * LANES:(blk + 1) * LANES].astype(F32)
            else:
                own, prv, nxt = (dkc_ref, dkp_ref, dkp_next_ref) if blk < N_ROPE_BLOCKS else (dvc_ref, dvp_ref, dvp_next_ref)
                cols = slice((blk % 2) * LANES, (blk % 2 + 1) * LANES)
                g = own[:, cols].astype(F32) + shifted(prv, nxt, cols)
            if blk < N_ROPE_BLOCKS:
                g = g * cosv + _swap_halves(g * sinv)
            o_ref[:, blk * LANES:(blk + 1) * LANES] = g.astype(BF16)
            parts.append(jnp.sum(g, axis=0, keepdims=True))
        sums = jnp.concatenate(parts, axis=1)
        _accum(db_ref, sums, i == 0)

    own_spec = _row_spec(tm, KV_WIDTH)
    next_spec = pl.BlockSpec((tm, KV_WIDTH), lambda i: (jnp.minimum(i + 1, nb - 1), 0))
    core, rr = _call(
        body, grid=(nb,),
        in_specs=[_row_spec(tm, Q_WIDTH), own_spec, own_spec, next_spec, own_spec, own_spec, next_spec,
                  _row_spec(tm, LANES), _row_spec(tm, LANES)],
        out_specs=[_row_spec(tm, QKV_WIDTH), _vec_spec(QKV_WIDTH)],
        out_shape=[jax.ShapeDtypeStruct((s, QKV_WIDTH), BF16), jax.ShapeDtypeStruct((1, QKV_WIDTH), F32)],
        operands=(dq, dkc, dkp, dkp, dvc, dvp, dvp, cos, sin), name=name, riders=riders)
    return _ret(core, rr, riders)


ROWS = GQA_GROUP * WINDOW


def _prev_slots():
    kpos = lax.broadcasted_iota(I32, (WINDOW, ROWS), 0)
    qpos = lax.broadcasted_iota(I32, (WINDOW, ROWS), 1) & (WINDOW - 1)
    return kpos > qpos


def _head_cols(ref, head):
    return ref[:, head * HEAD_DIM:(head + 1) * HEAD_DIM]


def _stack_heads(ref, h):
    return jnp.concatenate([_head_cols(ref, GQA_GROUP * h + g) for g in range(GQA_GROUP)], axis=0)


def _band(prev_ref, cur_ref, h):
    return jnp.concatenate([_head_cols(prev_ref, h), _head_cols(cur_ref, h)], axis=0)


def _pick(prev, band):
    return jnp.where(prev, band[:WINDOW], band[WINDOW:])


def _spread(prev, x):
    return jnp.concatenate([jnp.where(prev, x, 0.0), jnp.where(prev, 0.0, x)], axis=0).astype(BF16)


def _attn_probs(s_band, sink, prev, has_prev):
    scale = HEAD_DIM ** -0.5
    s = jnp.where(prev, jnp.where(has_prev, s_band[:WINDOW], NEG), s_band[WINDOW:]) * scale
    m = jnp.maximum(jnp.max(s, axis=0, keepdims=True), sink)
    e, es = jnp.exp(s - m), jnp.exp(sink - m)
    inv = 1.0 / (jnp.sum(e, axis=0, keepdims=True) + es)
    return e * inv, es * inv


def _attn_specs(nb):
    kcol, vcol = Q_WIDTH // KV_WIDTH, Q_WIDTH // KV_WIDTH + 1
    q_spec = pl.BlockSpec((WINDOW, Q_WIDTH), lambda n: (n, 0))
    return [q_spec,
            pl.BlockSpec((WINDOW, KV_WIDTH), lambda n: (n, kcol)),
            pl.BlockSpec((WINDOW, KV_WIDTH), lambda n: (jnp.maximum(n - 1, 0), kcol)),
            pl.BlockSpec((WINDOW, KV_WIDTH), lambda n: (n, vcol)),
            pl.BlockSpec((WINDOW, KV_WIDTH), lambda n: (jnp.maximum(n - 1, 0), vcol)),
            pl.BlockSpec((N_KV_HEADS, 8, ROWS), lambda n: (0, 0, 0))]


def attn_fwd(qkv, sink_rows, *, name, riders=()):
    s = qkv.shape[0]

    def body(q_ref, kc_ref, kp_ref, vc_ref, vp_ref, sink_ref, o_ref):
        prev = _prev_slots()
        has_prev = pl.program_id(0) > 0
        heads = range(N_KV_HEADS)
        s_bands = [lax.dot_general(_band(kp_ref, kc_ref, h), _stack_heads(q_ref, h), NT_DIMS, preferred_element_type=F32)
                   for h in heads]
        p_bands = [_spread(prev, _attn_probs(s_bands[h], sink_ref[h, 0:1, :], prev, has_prev)[0]) for h in heads]
        outs = [lax.dot_general(_band(vp_ref, vc_ref, h), p_bands[h], TN_DIMS, preferred_element_type=F32).T for h in heads]
        for h in heads:
            for g in range(GQA_GROUP):
                head = GQA_GROUP * h + g
                o_ref[:, head * HEAD_DIM:(head + 1) * HEAD_DIM] = outs[h][g * WINDOW:(g + 1) * WINDOW].astype(BF16)

    core, rr = _call(
        body, grid=(s // WINDOW,), in_specs=_attn_specs(s // WINDOW), out_specs=[pl.BlockSpec((WINDOW, Q_WIDTH), lambda n: (n, 0))],
        out_shape=[jax.ShapeDtypeStruct((s, Q_WIDTH), BF16)], operands=(qkv, qkv, qkv, qkv, qkv, sink_rows), sem=("parallel",),
        name=name, riders=riders)
    return _ret(core, rr, riders)


def attn_bwd(qkv, sink_rows, do, *, name, riders=()):
    s = qkv.shape[0]

    def body(q_ref, kc_ref, kp_ref, vc_ref, vp_ref, sink_ref, do_ref, dq_ref, dkc_ref, dkp_ref, dvc_ref, dvp_ref, dsink_ref):
        n = pl.program_id(0)
        prev = _prev_slots()
        scale = HEAD_DIM ** -0.5
        heads = range(N_KV_HEADS)
        qs, dos = [_stack_heads(q_ref, h) for h in heads], [_stack_heads(do_ref, h) for h in heads]
        kbands, vbands = [_band(kp_ref, kc_ref, h) for h in heads], [_band(vp_ref, vc_ref, h) for h in heads]
        s_bands = [lax.dot_general(kbands[h], qs[h], NT_DIMS, preferred_element_type=F32) for h in heads]
        dp_bands = [lax.dot_general(vbands[h], dos[h], NT_DIMS, preferred_element_type=F32) for h in heads]
        ds_bands, p_bands, parts = [], [], []
        for h in heads:
            p, ps = _attn_probs(s_bands[h], sink_ref[h, 0:1, :], prev, n > 0)
            dp = _pick(prev, dp_bands[h])
            delta = jnp.sum(p * dp, axis=0, keepdims=True)
            ds_bands.append(_spread(prev, p * (dp - delta) * scale))
            p_bands.append(_spread(prev, p))
            dsink = -(ps * delta)
            for g in range(GQA_GROUP):
                parts.append(jnp.broadcast_to(jnp.sum(dsink[:, g * WINDOW:(g + 1) * WINDOW], axis=1, keepdims=True), (8, LANES)))
        for h in heads:
            dk = jnp.dot(ds_bands[h], qs[h], preferred_element_type=F32).astype(BF16)
            dv = jnp.dot(p_bands[h], dos[h], preferred_element_type=F32).astype(BF16)
            dq = lax.dot_general(kbands[h], ds_bands[h], TN_DIMS, preferred_element_type=F32).T
            cols = slice(h * HEAD_DIM, (h + 1) * HEAD_DIM)
            dkp_ref[:, cols], dkc_ref[:, cols] = dk[:WINDOW], dk[WINDOW:]
            dvp_ref[:, cols], dvc_ref[:, cols] = dv[:WINDOW], dv[WINDOW:]
            for g in range(GQA_GROUP):
                head = GQA_GROUP * h + g
                dq_ref[:, head * HEAD_DIM:(head + 1) * HEAD_DIM] = dq[g * WINDOW:(g + 1) * WINDOW].astype(BF16)

        @pl.when(n == 0)
        def _():
            for i, part in enumerate(parts):
                dsink_ref[i // GQA_GROUP, i % GQA_GROUP] = part

        @pl.when(n > 0)
        def _():
            for i, part in enumerate(parts):
                dsink_ref[i // GQA_GROUP, i % GQA_GROUP] += part

    rows_q = pl.BlockSpec((WINDOW, Q_WIDTH), lambda n: (n, 0))
    rows_kv = pl.BlockSpec((WINDOW, KV_WIDTH), lambda n: (n, 0))
    kv_shape = jax.ShapeDtypeStruct((s, KV_WIDTH), BF16)
    core, rr = _call(
        body, grid=(s // WINDOW,), in_specs=_attn_specs(s // WINDOW) + [rows_q],
        out_specs=[rows_q, rows_kv, rows_kv, rows_kv, rows_kv,
                   pl.BlockSpec((N_KV_HEADS, GQA_GROUP, 8, LANES), lambda n: (0, 0, 0, 0))],
        out_shape=[jax.ShapeDtypeStruct((s, Q_WIDTH), BF16), kv_shape, kv_shape, kv_shape, kv_shape,
                   jax.ShapeDtypeStruct((N_KV_HEADS, GQA_GROUP, 8, LANES), F32)],
        operands=(qkv, qkv, qkv, qkv, qkv, sink_rows, do), sem=("arbitrary",), name=name, riders=riders)
    return _ret(core, rr, riders)


GELU_C = 0.7978845608028654
GELU_A = 0.044715


def _gelu(x):
    return 0.5 * x * (1.0 + jnp.tanh(x * (GELU_C + (GELU_C * GELU_A) * (x * x))))


def _gelu_and_grad(x):
    x2 = x * x
    t = jnp.tanh(x * (GELU_C + (GELU_C * GELU_A) * x2))
    half_x, one_t = 0.5 * x, 1.0 + t
    return half_x * one_t, 0.5 * one_t + half_x * (1.0 - t * t) * (GELU_C + (3.0 * GELU_C * GELU_A) * x2)


def _tril_bf16(w):
    row = lax.broadcasted_iota(I32, (SGU_CHUNK, SGU_CHUNK), 0)
    col = lax.broadcasted_iota(I32, (SGU_CHUNK, SGU_CHUNK), 1)
    return jnp.where(row >= col, w, 0.0).astype(BF16)


def _sgu_norm(vg, g, b):
    mu = jnp.mean(vg, axis=-1, keepdims=True)
    cen = vg - mu
    rstd = lax.rsqrt(jnp.mean(cen * cen, axis=-1, keepdims=True) + EPS)
    xhat = cen * rstd
    return xhat, rstd, xhat * g + b


def sgu_in_fwd(h, w_in, ln_g, ln_b, w_sp, b_sp, *, name, tm=512, riders=()):
    s, k = h.shape
    ns = w_in.shape[2]
    tm = _row_tile(s, tm)

    def body(h_ref, w0, w1, w2, w3, g_ref, b_ref, w_ref, bs_ref, z_ref, y_ref):
        hv = h_ref[...]
        zs = [jnp.dot(hv, w_ref_j[...], preferred_element_type=F32) for w_ref_j in (w0, w1, w2, w3)]
        for j, zj in enumerate(zs):
            z_ref[:, j * ns:(j + 1) * ns] = zj.astype(BF16)
        u = _gelu(jnp.concatenate(zs[:2], axis=1))
        _, _, vn = _sgu_norm(_gelu(jnp.concatenate(zs[2:], axis=1)), g_ref[...], b_ref[...])
        vn = vn.astype(BF16)
        for grp in range(SGU_GROUPS):
            w = _tril_bf16(w_ref[grp])
            cols = slice(grp * LANES, (grp + 1) * LANES)
            for ch in range(tm // SGU_CHUNK):
                rows = slice(ch * SGU_CHUNK, (ch + 1) * SGU_CHUNK)
                mixed = jnp.dot(w, vn[rows, cols], preferred_element_type=F32) + bs_ref[grp]
                y_ref[rows, cols] = (u[rows, cols] * mixed).astype(BF16)

    def shard(j):
        return pl.BlockSpec((None, k, ns), lambda i: (j, 0, 0))

    full3 = pl.BlockSpec((SGU_GROUPS, SGU_CHUNK, SGU_CHUNK), lambda i: (0, 0, 0))
    core, rr = _call(
        body, grid=(s // tm,),
        in_specs=[_row_spec(tm, k)] + [shard(j) for j in range(N_CHIPS)] + [_vec_spec(D_MODEL), _vec_spec(D_MODEL), full3, full3],
        out_specs=[_row_spec(tm, 2 * D_MODEL), _row_spec(tm, D_MODEL)],
        out_shape=[jax.ShapeDtypeStruct((s, 2 * D_MODEL), BF16), jax.ShapeDtypeStruct((s, D_MODEL), BF16)],
        operands=(h, w_in, w_in, w_in, w_in, ln_g, ln_b, w_sp, b_sp), sem=("parallel",), name=name, riders=riders)
    return _ret(core, rr, riders)


def sgu_bwd(z, dy, ln_g, ln_b, w_sp, b_sp, *, name, tm=256, riders=()):
    s = z.shape[0]
    tm = _row_tile(s, tm)

    def body(z_ref, dy_ref, g_ref, b_ref, w_ref, bs_ref, dz_ref, dw_ref, dbs_ref, dg_ref, db_ref, dvn_buf):
        first = pl.program_id(0) == 0
        u, u_grad = _gelu_and_grad(z_ref[:, :D_MODEL].astype(F32))
        vg, v_grad = _gelu_and_grad(z_ref[:, D_MODEL:].astype(F32))
        xhat, rstd, vn = _sgu_norm(vg, g_ref[...], b_ref[...])
        vn = vn.astype(BF16)
        dyv = dy_ref[...]
        dmixed = dyv * u
        dz_gate = dyv * u_grad
        row = lax.broadcasted_iota(I32, (SGU_CHUNK, SGU_CHUNK), 0)
        col = lax.broadcasted_iota(I32, (SGU_CHUNK, SGU_CHUNK), 1)
        dws, dbss = [], []
        for grp in range(SGU_GROUPS):
            w = _tril_bf16(w_ref[grp])
            cols = slice(grp * LANES, (grp + 1) * LANES)
            dw = jnp.zeros((SGU_CHUNK, SGU_CHUNK), F32)
            dbs = jnp.zeros((SGU_CHUNK, 1), F32)
            for ch in range(tm // SGU_CHUNK):
                rows = slice(ch * SGU_CHUNK, (ch + 1) * SGU_CHUNK)
                vblk = vn[rows, cols]
                mixed = jnp.dot(w, vblk, preferred_element_type=F32) + bs_ref[grp]
                dz_ref[rows, cols] = (dz_gate[rows, cols] * mixed).astype(BF16)
                dm = dmixed[rows, cols]
                dmb = dm.astype(BF16)
                dvn_buf[rows, cols] = lax.dot_general(w, dmb, TN_DIMS, preferred_element_type=F32)
                dw += lax.dot_general(dmb, vblk, NT_DIMS, preferred_element_type=F32)
                dbs += jnp.sum(dm, axis=-1, keepdims=True)
            dws.append(jnp.where(row >= col, dw, 0.0))
            dbss.append(jnp.broadcast_to(dbs, (SGU_CHUNK, SGU_CHUNK)))

        dvn = dvn_buf[...]
        dxhat = dvn * g_ref[...]
        dvg = rstd * (dxhat - jnp.mean(dxhat, axis=-1, keepdims=True) - xhat * jnp.mean(dxhat * xhat, axis=-1, keepdims=True))
        dz_ref[:, D_MODEL:] = (dvg * v_grad).astype(BF16)
        dlng, dlnb = jnp.sum(dvn * xhat, axis=0, keepdims=True), jnp.sum(dvn, axis=0, keepdims=True)

        @pl.when(first)
        def _():
            for grp in range(SGU_GROUPS):
                dw_ref[grp] = dws[grp]
                dbs_ref[grp] = dbss[grp]
            dg_ref[...] = dlng
            db_ref[...] = dlnb

        @pl.when(jnp.logical_not(first))
        def _():
            for grp in range(SGU_GROUPS):
                dw_ref[grp] += dws[grp]
                dbs_ref[grp] += dbss[grp]
            dg_ref[...] += dlng
            db_ref[...] += dlnb

    full3 = pl.BlockSpec((SGU_GROUPS, SGU_CHUNK, SGU_CHUNK), lambda i: (0, 0, 0))
    s3 = jax.ShapeDtypeStruct((SGU_GROUPS, SGU_CHUNK, SGU_CHUNK), F32)
    vshape = jax.ShapeDtypeStruct((1, D_MODEL), F32)
    core, rr = _call(
        body, grid=(s // tm,),
        in_specs=[_row_spec(tm, 2 * D_MODEL), _row_spec(tm, D_MODEL), _vec_spec(D_MODEL), _vec_spec(D_MODEL), full3, full3],
        out_specs=[_row_spec(tm, 2 * D_MODEL), full3, full3, _vec_spec(D_MODEL), _vec_spec(D_MODEL)],
        out_shape=[jax.ShapeDtypeStruct((s, 2 * D_MODEL), BF16), s3, s3, vshape, vshape],
        scratch_shapes=[pltpu.VMEM((tm, D_MODEL), F32)], operands=(z, dy, ln_g, ln_b, w_sp, b_sp), name=name, riders=riders)
    return _ret(core, rr, riders)


def _sigmoid(x):
    return 1.0 / (1.0 + jnp.exp(-x))


def ffn_up(h, w_gu, *, name, tm=512, riders=()):
    s = h.shape[0]
    tm = _row_tile(s, tm)

    def body(h_ref, wg_ref, wu_ref, d_ref, a_ref):
        hv = h_ref[...]
        sub = min(256, tm)
        for t in range(tm // sub):
            rows = slice(t * sub, (t + 1) * sub)
            g = jnp.dot(hv[rows], wg_ref[...], preferred_element_type=F32)
            u = jnp.dot(hv[rows], wu_ref[...], preferred_element_type=F32)
            sig = _sigmoid(g)
            silu = g * sig
            d_ref[0, rows, :] = (u * (sig + silu * (1.0 - sig))).astype(BF16)
            d_ref[1, rows, :] = silu.astype(BF16)
            a_ref[rows, :] = (silu * u).astype(BF16)

    core, rr = _call(
        body, grid=(2, s // tm),
        in_specs=[pl.BlockSpec((tm, D_MODEL), lambda j, i: (i, 0)),
                  pl.BlockSpec((None, D_MODEL, FF_HALF), lambda j, i: (j, 0, 0)),
                  pl.BlockSpec((None, D_MODEL, FF_HALF), lambda j, i: (j + 2, 0, 0))],
        out_specs=[pl.BlockSpec((2, tm, FF_HALF), lambda j, i: (0, i, j)), pl.BlockSpec((tm, FF_HALF), lambda j, i: (i, j))],
        out_shape=[jax.ShapeDtypeStruct((2, s, D_FF), BF16), jax.ShapeDtypeStruct((s, D_FF), BF16)],
        operands=(h, w_gu, w_gu), sem=("parallel", "parallel"), name=name, riders=riders)
    return _ret(core, rr, riders)


def _weight_tile(rows):
    for tr in (512, 352, 256, 128):
        if rows % tr == 0:
            return tr
    return rows


def place_shard(w, layer, chip_arr, dtype, *, name, riders=()):
    _, r, c = w.shape
    tr = _weight_tile(r)

    def body(chip_ref, w_ref, o_ref):
        o_ref[...] = w_ref[...].astype(dtype)

    core, rr = _call(
        body, grid=(r // tr,), prefetch=(chip_arr,),
        in_specs=[pl.BlockSpec((None, tr, c), lambda i, chip: (layer, i, 0))],
        out_specs=[pl.BlockSpec((None, tr, c), lambda i, chip: (chip[0], i, 0))],
        out_shape=[jax.ShapeDtypeStruct((N_CHIPS, r, c), dtype)], operands=(w,), sem=("parallel",), name=name, riders=riders)
    return _ret(core, rr, riders)


def _adamw_math(w, g, m, v):
    m = ADAM_B1 * m + (1.0 - ADAM_B1) * g
    v = ADAM_B2 * v + (1.0 - ADAM_B2) * (g * g)
    m_hat = m / (1.0 - ADAM_B1 ** ADAM_STEP)
    v_hat = v / (1.0 - ADAM_B2 ** ADAM_STEP)
    delta = -ADAM_LR * (m_hat / (jnp.sqrt(v_hat) + ADAM_EPS) + ADAM_WD * w)
    return delta, m, v


def adamw(w, g, m, v, *, name, after=None):
    nl, r, c = w.shape
    tr = _weight_tile(r)

    def body(w_ref, g_ref, m_ref, v_ref, *rest):
        go_ref, d_ref, mo_ref, vo_ref = rest[-4:]
        gv = g_ref[...]
        go_ref[...] = gv
        d_ref[...], mo_ref[...], vo_ref[...] = _adamw_math(w_ref[...], gv, m_ref[...], v_ref[...])

    spec = pl.BlockSpec((None, tr, c), lambda l, i: (l, i, 0))
    shape = jax.ShapeDtypeStruct(w.shape, F32)
    extra = [] if after is None else [after]
    outs, _ = _call(body, grid=(nl, r // tr), in_specs=[spec] * 4 + [ANY] * len(extra), out_specs=[spec] * 4,
                    out_shape=[shape] * 4, operands=(w, g, m, v, *extra), sem=("parallel", "parallel"), name=name)
    return outs


def adamw_small(ws, gs, ms, vs, *, name):
    n = len(ws)

    def body(*refs):
        ins, outs = refs[:4 * n], refs[4 * n:]
        for t in range(n):
            gv = ins[n + t][...]
            outs[t][...] = gv
            outs[n + t][...], outs[2 * n + t][...], outs[3 * n + t][...] = _adamw_math(
                ins[t][...], gv, ins[2 * n + t][...], ins[3 * n + t][...])

    shapes = [jax.ShapeDtypeStruct(w.shape, F32) for w in ws]
    res = pl.pallas_call(body, out_shape=shapes * 4, name=name)(*ws, *gs, *ms, *vs)
    return res[:n], res[n:2 * n], res[2 * n:3 * n], res[3 * n:]


def pair_add(g, r1, c_arr, *, name):
    _, rows, cdim = g.shape
    h = rows // 2

    def body(c_ref, g_ref, r_ref, o_ref):
        o_ref[...] = (g_ref[...].astype(F32) + r_ref[...].astype(F32)).astype(o_ref.dtype)

    (out,), _ = _call(
        body, grid=(N_CHIPS,), prefetch=(c_arr,),
        in_specs=[pl.BlockSpec((None, h, cdim), lambda s, c: (s, c[0], 0)), pl.BlockSpec((None, h, cdim), lambda s, c: (s, 0, 0))],
        out_specs=[pl.BlockSpec((None, h, cdim), lambda s, c: (s, 0, 0))],
        out_shape=[jax.ShapeDtypeStruct((N_CHIPS, h, cdim), g.dtype)], operands=(g, r1), sem=("parallel",), name=name)
    return out


def final_add(g, r1, r2, jc_arr, *, dest_shape, lead, prev, name):
    _, rows, cdim = g.shape
    h = rows // 2

    def body(jc_ref, g_ref, r1_ref, r2_ref, *rest):
        o_ref = rest[-1]
        acc = g_ref[...].astype(F32) + r1_ref[...].astype(F32)
        for k in range(3):
            acc = acc + r2_ref[k].astype(F32)
        o_ref[...] = acc

    if lead is None:
        o_spec = pl.BlockSpec((h, cdim), lambda i, jc: (jc[1], 0))
    elif lead == "chip":
        o_spec = pl.BlockSpec((None, h, cdim), lambda i, jc: (jc[0], jc[1], 0))
    else:
        o_spec = pl.BlockSpec((None, h, cdim), lambda i, jc: (lead, jc[1], 0))
    in_specs = [pl.BlockSpec((None, h, cdim), lambda i, jc: (jc[0], jc[1], 0)),
                pl.BlockSpec((None, h, cdim), lambda i, jc: (jc[0], 0, 0)),
                pl.BlockSpec((3, h, cdim), lambda i, jc: (0, 0, 0))]
    operands = [g, r1, r2]
    aliases = None
    if prev is not None:
        in_specs.append(ANY)
        operands.append(prev)
        aliases = {3: 0}
    (out,), _ = _call(body, grid=(1,), prefetch=(jc_arr,), in_specs=in_specs, out_specs=[o_spec],
                      out_shape=[jax.ShapeDtypeStruct(dest_shape, F32)], operands=operands, aliases=aliases, name=name)
    return out


def _place():
    return lax.axis_index("x"), lax.axis_index("y"), lax.axis_index("c")


def _partner(x, y, k):
    return (1 - x if k >> 1 else x), (1 - y if k & 1 else y)


WHOLE = (0, 1, 1)


def _half(rows, sel, dtype, piece=WHOLE):
    lo, hi, n = piece
    align = 16 if dtype == BF16 else 8
    step = rows // 2 // n
    assert rows // 2 == step * n and step % align == 0
    return pl.ds(pl.multiple_of(sel * (rows // 2) + lo * step, align), (hi - lo) * step)


def _rider(peers, inputs, aliased, fresh, nsem, copies, arrivals):
    def start(ins, outs, send, recv):
        for cp in copies(ins, outs, send, recv):
            cp.start()

    def finish(ins, outs, send, recv):
        for cp in arrivals(ins, outs, send, recv):
            cp.wait_recv()
        for cp in copies(ins, outs, send, recv):
            cp.wait_send()

    return types.SimpleNamespace(peers=peers, inputs=list(inputs), aliased=list(aliased), fresh=list(fresh), nsem=nsem,
                                 start=start, finish=finish)


def _remote(src, dst, send, recv, idx, dev):
    return pltpu.make_async_remote_copy(src_ref=src, dst_ref=dst, send_sem=send.at[idx], recv_sem=recv.at[idx],
                                        device_id=dev, device_id_type=MESH)


def gather_ici_rider(fulls, pieces=None):
    nt = len(fulls)
    pieces = pieces or [WHOLE] * nt

    def region(outs, t, slot, sel):
        return outs[t].at[slot, _half(fulls[t].shape[1], sel, fulls[t].dtype, pieces[t])]

    def copies(ins, outs, send, recv):
        x, y, c = _place()
        res = []
        for t in range(nt):
            for k in (1, 2, 3):
                px, py = _partner(x, y, k)
                mine = region(outs, t, 2 * x + y, c)
                res.append(_remote(mine, mine, send, recv, 3 * t + k - 1, (px, py, c)))
        return res

    def arrivals(ins, outs, send, recv):
        x, y, c = _place()
        res = []
        for t in range(nt):
            for k in (1, 2, 3):
                px, py = _partner(x, y, k)
                theirs = region(outs, t, 2 * px + py, c)
                res.append(_remote(theirs, theirs, send, recv, 3 * t + k - 1, (x, y, c)))
        return res

    return _rider("chips", fulls, range(nt), [], 3 * nt, copies, arrivals)


def gather_d2d_rider(fulls, pieces=None):
    nt = len(fulls)
    pieces = pieces or [WHOLE] * nt

    def region(outs, t, slot, sel):
        return outs[t].at[slot, _half(fulls[t].shape[1], sel, fulls[t].dtype, pieces[t])]

    def both(outs, send, recv, mine):
        x, y, c = _place()
        res = []
        for t in range(nt):
            for k in (1, 2, 3):
                px, py = _partner(x, y, k)
                part = region(outs, t, 2 * px + py, c if mine else 1 - c)
                res.append(_remote(part, part, send, recv, 3 * t + k - 1, (x, y, 1 - c)))
        return res

    return _rider("sibling", fulls, range(nt), [], 3 * nt, lambda i, o, s, r: both(o, s, r, True),
                  lambda i, o, s, r: both(o, s, r, False))


def exchange_rider(grads):
    nt = len(grads)

    def both(ins, outs, send, recv):
        x, y, c = _place()
        return [_remote(ins[t].at[:, _half(grads[t].shape[1], 1 - c, grads[t].dtype)], outs[t], send, recv, t, (x, y, 1 - c))
                for t in range(nt)]

    fresh = [jax.ShapeDtypeStruct((N_CHIPS, g.shape[1] // 2, g.shape[2]), g.dtype) for g in grads]
    return _rider("sibling", grads, [], fresh, nt, both, both)


def scatter_rider(parts):
    nt = len(parts)

    def both(ins, outs, send, recv):
        x, y, c = _place()
        res = []
        for t in range(nt):
            for k in (1, 2, 3):
                px, py = _partner(x, y, k)
                res.append(_remote(ins[t].at[2 * px + py], outs[t].at[k - 1], send, recv, 3 * t + k - 1, (px, py, c)))
        return res

    fresh = [jax.ShapeDtypeStruct((3,) + p.shape[1:], p.dtype) for p in parts]
    return _rider("chips", parts, [], fresh, 3 * nt, both, both)


def broadcast_rider(bufs, items):
    def region(outs, item, sel):
        bi, lead = item
        ref = outs[bi]
        if lead == "chip":
            x, y, _ = _place()
            ref = ref.at[2 * x + y]
        elif lead is not None:
            ref = ref.at[lead]
        return ref.at[_half(ref.shape[0], sel, F32)]

    def both(outs, send, recv, mine):
        x, y, c = _place()
        res = []
        for i, item in enumerate(items):
            part = region(outs, item, c if mine else 1 - c)
            res.append(_remote(part, part, send, recv, i, (x, y, 1 - c)))
        return res

    return _rider("sibling", bufs, range(len(bufs)), [], len(items), lambda i, o, s, r: both(o, s, r, True),
                  lambda i, o, s, r: both(o, s, r, False))


def allcast_rider(buf):
    peers = [(k, flip) for k in range(N_CHIPS) for flip in (0, 1) if (k, flip) != (0, 0)]

    def both(outs, send, recv, mine):
        x, y, c = _place()
        res = []
        for i, (k, flip) in enumerate(peers):
            px, py = _partner(x, y, k)
            pc = 1 - c if flip else c
            slot, sel = (2 * x + y, c) if mine else (2 * px + py, pc)
            part = outs[0].at[slot, _half(buf.shape[1], sel, F32)]
            res.append(_remote(part, part, send, recv, i, (px, py, pc)))
        return res

    return _rider("everyone", [buf], [0], [], len(peers), lambda i, o, s, r: both(o, s, r, True),
                  lambda i, o, s, r: both(o, s, r, False))


def comm_call(riders, *, name):
    _, res = _call(None, riders=riders, name=name)
    return res


SEMS = pl.BlockSpec(memory_space=pltpu.SEMAPHORE)
SIDE_EFFECT = pltpu.SideEffectType.DATAFLOW_SIDE_EFFECTING


def _split_refs(riders, refs):
    views, p = [], 0
    for r in riders:
        bufs = refs[p:p + len(r.inputs) + len(r.fresh)]
        p += len(bufs)
        ins = bufs[:len(r.inputs)]
        views.append([ins, [ins[i] for i in r.aliased] + list(bufs[len(r.inputs):])])
    for view in views:
        view += [refs[p], refs[p + 1]]
        p += 2
    return views


def comm_start(riders, *, name):
    kind = _peer_kind(riders)
    bufs = [a for r in riders for a in r.inputs]
    fresh = [f for r in riders for f in r.fresh]
    n_buf, n_fresh = len(bufs), len(fresh)

    def body(*refs):
        ins, outs = refs[:n_buf], refs[n_buf:]
        through, land, sems = outs[:n_buf], outs[n_buf:n_buf + n_fresh], outs[n_buf + n_fresh:-1]
        _peer_barrier(kind)
        per_rider, pb, pf = [], 0, 0
        for r in riders:
            per_rider += list(through[pb:pb + len(r.inputs)]) + list(land[pf:pf + len(r.fresh)])
            pb, pf = pb + len(r.inputs), pf + len(r.fresh)
        for r, (r_ins, r_outs, send, recv) in zip(riders, _split_refs(riders, per_rider + list(sems))):
            r.start(r_ins, r_outs, send, recv)
        outs[-1][...] = jnp.zeros((8, LANES), F32)

    sem_shapes = [pltpu.SemaphoreType.DMA((r.nsem,)) for r in riders for _ in (0, 1)]
    res = pl.pallas_call(
        body, name=name, in_specs=[ANY] * n_buf,
        out_specs=[ANY] * (n_buf + n_fresh) + [SEMS] * len(sem_shapes) + [pl.BlockSpec(memory_space=pltpu.VMEM)],
        out_shape=[jax.ShapeDtypeStruct(a.shape, a.dtype) for a in bufs] + fresh + sem_shapes
        + [jax.ShapeDtypeStruct((8, LANES), F32)],
        input_output_aliases={i: i for i in range(n_buf)},
        compiler_params=pltpu.CompilerParams(has_side_effects=SIDE_EFFECT, collective_id=PEER_KINDS.index(kind)))(*bufs)
    return (riders, list(res[:n_buf + n_fresh]), list(res[n_buf + n_fresh:-1])), res[-1]


def comm_wait(state, after, *, name):
    riders, bufs, sems = state
    n_buf, n_sem = len(bufs), len(sems)
    n_in = sum(len(r.inputs) for r in riders)

    def body(*refs):
        held, sem_refs = refs[:n_buf], refs[n_buf:n_buf + n_sem]
        through, land = held[:n_in], held[n_in:]
        per_rider, pb, pf = [], 0, 0
        for r in riders:
            per_rider += list(through[pb:pb + len(r.inputs)]) + list(land[pf:pf + len(r.fresh)])
            pb, pf = pb + len(r.inputs), pf + len(r.fresh)
        for r, (r_ins, r_outs, send, recv) in zip(riders, _split_refs(riders, per_rider + list(sem_refs))):
            r.finish(r_ins, r_outs, send, recv)

    res = pl.pallas_call(
        body, name=name, in_specs=[ANY] * n_buf + [SEMS] * n_sem + [ANY], out_specs=[ANY] * n_buf,
        out_shape=[jax.ShapeDtypeStruct(a.shape, a.dtype) for a in bufs],
        input_output_aliases={i: i for i in range(n_buf)},
        compiler_params=pltpu.CompilerParams(has_side_effects=SIDE_EFFECT))(*bufs, *sems, after)
    through, land = list(res[:n_in]), list(res[n_in:])
    out, pb, pf = [], 0, 0
    for r in riders:
        r_ins, r_land = through[pb:pb + len(r.inputs)], land[pf:pf + len(r.fresh)]
        pb, pf = pb + len(r.inputs), pf + len(r.fresh)
        out.append([r_ins[i] for i in r.aliased] + r_land)
    return out


SLAB_ROWS = 192


def _pad_rows(a, rows=8):
    return jnp.pad(a, ((0, rows - a.shape[0]), (0, 0)))


def _pack_small(norm_grads, db_qkv, db_o, dsinks, db_sp, dln_g, dln_b, dw_sp, loss_part):
    parts = [
        jnp.concatenate(norm_grads, axis=0),
        _pad_rows(jnp.pad(db_qkv, ((0, 0), (0, 2 * D_MODEL - QKV_WIDTH))).reshape(2, D_MODEL)),
        _pad_rows(db_o),
        _pad_rows(jnp.pad(dsinks.reshape(1, N_Q_HEADS), ((0, 0), (0, D_MODEL - N_Q_HEADS)))),
        _pad_rows(db_sp.reshape(1, D_MODEL)),
        _pad_rows(jnp.concatenate([dln_g, dln_b, jnp.pad(loss_part[0:1], ((0, 0), (0, D_MODEL - LANES)))], axis=0)),
        dw_sp.reshape(SGU_CHUNK, D_MODEL),
    ]
    slab = jnp.concatenate(parts, axis=0)
    return jnp.pad(slab, ((0, SLAB_ROWS - slab.shape[0]), (0, 0))).reshape(N_CHIPS, SLAB_ROWS // N_CHIPS, D_MODEL)


def _unpack_small(slab, j):
    slab = slab.reshape(SLAB_ROWS, D_MODEL)
    norms = [slab[2 * i:2 * i + 2] for i in range(4)]
    db_qkv = slab[8:10].reshape(1, 2 * D_MODEL)[:, :QKV_WIDTH]
    db_o = slab[16:17]
    dsinks = slab[24:25, :N_Q_HEADS]
    db_sp = slab[32:33].reshape(SGU_GROUPS, SGU_CHUNK)
    width = D_MODEL // N_CHIPS
    dln_g = lax.dynamic_slice(slab[40:41], (0, j * width), (1, width))
    dln_b = lax.dynamic_slice(slab[41:42], (0, j * width), (1, width))
    dw_sp = slab[48:48 + SGU_CHUNK].reshape(SGU_GROUPS * SGU_CHUNK, SGU_CHUNK)
    return norms, db_qkv, db_o, dsinks, db_sp, dln_g, dln_b, dw_sp, slab[42, 0]


class _GradReduce:
    def __init__(self, c_arr, jc_arr, dest_shapes):
        self.c_arr, self.jc_arr, self.dest_shapes = c_arr, jc_arr, dest_shapes
        self.grad, self.sibling, self.pair, self.chips, self.dest = {}, {}, {}, {}, {}

    def exchange(self, tags):
        return exchange_rider([self.grad[t] for t in tags])

    def exchanged(self, tags, res):
        for t, r in zip(tags, res):
            self.sibling[t] = r
            self.pair[t] = pair_add(self.grad[t], r, self.c_arr, name=f"pair_add_{t}")

    def scatter(self, tags):
        return scatter_rider([self.pair[t] for t in tags])

    def scattered(self, tags, res, where):
        for t, r in zip(tags, res):
            name, lead = where[t]
            self.dest[name] = final_add(self.grad[t], self.sibling[t], r, self.jc_arr, dest_shape=self.dest_shapes[name],
                                        lead=lead, prev=self.dest.get(name), name=f"final_add_{t}")

    def broadcast(self, items):
        names = []
        for n, _ in items:
            if n not in names:
                names.append(n)
        return names, broadcast_rider([self.dest[n] for n in names], [(names.index(n), lead) for n, lead in items])

    def broadcasted(self, names, res):
        for n, r in zip(names, res):
            self.dest[n] = r


def kernel(x, norm_mix_pre, norm_mix_post, norm_ffn_pre, norm_ffn_post, attn_w_qkv, attn_b_qkv, attn_sinks, attn_w_o, attn_b_o, sgu_w_in, sgu_ln_g, sgu_ln_b, sgu_w_spatial, sgu_b_spatial, sgu_w_out, ffn_w_gate_up, ffn_w_down, loss_target, m_norm_mix_pre, m_norm_mix_post, m_norm_ffn_pre, m_norm_ffn_post, m_attn_w_qkv, m_attn_b_qkv, m_attn_sinks, m_attn_w_o, m_attn_b_o, m_sgu_w_in, m_sgu_ln_g, m_sgu_ln_b, m_sgu_w_spatial, m_sgu_b_spatial, m_sgu_w_out, m_ffn_w_gate_up, m_ffn_w_down, v_norm_mix_pre, v_norm_mix_post, v_norm_ffn_pre, v_norm_ffn_post, v_attn_w_qkv, v_attn_b_qkv, v_attn_sinks, v_attn_w_o, v_attn_b_o, v_sgu_w_in, v_sgu_ln_g, v_sgu_ln_b, v_sgu_w_spatial, v_sgu_b_spatial, v_sgu_w_out, v_ffn_w_gate_up, v_ffn_w_down):
    s = x.shape[1]
    x0 = x.reshape(s, D_MODEL)
    target = loss_target.reshape(s, D_MODEL)
    mx, my, mc = lax.axis_index("x"), lax.axis_index("y"), lax.axis_index("c")
    chip = 2 * mx + my
    chip_arr = jnp.reshape(chip, (1,)).astype(I32)
    c_arr = jnp.reshape(mc, (1,)).astype(I32)
    jc_arr = jnp.stack([chip, mc]).astype(I32)
    zero_bias = jnp.zeros((1, D_MODEL), F32)

    def gain(p, i):
        return p[i:i + 1]

    big = [attn_w_qkv, attn_w_o, sgu_w_in, sgu_w_out, ffn_w_gate_up, ffn_w_gate_up, ffn_w_down, ffn_w_down]
    layers = [0, 0, 0, 0, 0, 1, 0, 1]
    tags = ["qkv", "wo", "win", "wout", "wgu0", "wgu1", "wd0", "wd1"]
    full = {t: place_shard(w, l, chip_arr, BF16, name=f"place_{t}") for w, l, t in zip(big, layers, tags)
            if t in ("qkv", "wo")}
    ln_pack = _pad_rows(jnp.concatenate([sgu_ln_g, sgu_ln_b], axis=0), 16)[None]
    full["ln"] = place_shard(ln_pack, 0, chip_arr, F32, name="place_ln")

    def split(items):
        return [i if isinstance(i, str) else i[0] for i in items], [WHOLE if isinstance(i, str) else tuple(i[1:]) for i in items]

    def ici(*items):
        names, pieces = split(items)
        return gather_ici_rider([full[n] for n in names], pieces)

    def d2d(*items):
        names, pieces = split(items)
        return gather_d2d_rider([full[n] for n in names], pieces)

    def landed(items, res):
        for n, r in zip(split(items)[0], res):
            full[n] = r

    cos, sin = _rope_tables(s)
    sink_rows = jnp.broadcast_to(
        jnp.repeat(attn_sinks.reshape(N_KV_HEADS, GQA_GROUP), WINDOW, axis=1)[:, None, :], (N_KV_HEADS, 8, ROWS))
    w_sp = sgu_w_spatial.reshape(SGU_GROUPS, SGU_CHUNK, SGU_CHUNK)
    b_sp = jnp.broadcast_to(sgu_b_spatial.reshape(SGU_GROUPS, SGU_CHUNK)[:, :, None], (SGU_GROUPS, SGU_CHUNK, LANES))

    (h0, full["wgu0"]), (res,) = prenorm_and_place(x0, gain(norm_mix_pre, 0), ffn_w_gate_up, 0, chip_arr, name="prenorm_0",
                                                   riders=[ici("qkv", "ln")])
    landed(("qkv", "ln"), res)
    full["wgu1"], (res,) = place_shard(ffn_w_gate_up, 1, chip_arr, BF16, name="place_wgu1", riders=[d2d("qkv", "ln")])
    landed(("qkv", "ln"), res)
    ln_g = full["ln"][:, 0, :].reshape(1, D_MODEL)
    ln_b = full["ln"][:, 1, :].reshape(1, D_MODEL)

    def hosted(call, stages):
        outputs, results = call([{"ici": ici, "d2d": d2d}[kind](*items) for kind, items in stages])
        for (_, items), res in zip(stages, results):
            landed(items, res)
        return outputs

    casts = [(ffn_w_down, 0), (sgu_w_in, 0), (ffn_w_down, 1), (sgu_w_out, 0)]
    qkv, full["wd0"], full["win"], full["wd1"], full["wout"] = hosted(
        lambda r: qkv_proj(h0, full["qkv"], attn_b_qkv, cos, sin, casts, chip_arr, name="qkv_proj", riders=r),
        [("ici", ("wo", ("wgu0", 0, 7, 16)))])
    o = hosted(lambda r: attn_fwd(qkv, sink_rows, name="attn_fwd", riders=r),
               [("d2d", ("wo",)), ("ici", (("wgu0", 7, 16, 16), ("wd0", 0, 4, 11)))])
    w_o = full["wo"].reshape(Q_WIDTH, D_MODEL)
    x1, h1, m0 = hosted(lambda r: proj_residual_norm(o, w_o, x0, attn_b_o, gain(norm_mix_post, 0), gain(norm_ffn_pre, 0),
                                                     name="attn_out_norm", riders=r),
                        [("d2d", ("wgu0",)), ("ici", (("wd0", 4, 11, 11),))])
    gu0, a0 = hosted(lambda r: ffn_up(h1, full["wgu0"], name="ffn_up_0", riders=r),
                     [("d2d", ("wd0",)), ("ici", ("win", "wout", ("wgu1", 0, 4, 8)))])
    w_d0 = full["wd0"].reshape(D_FF, D_MODEL)
    x2, h2, f0 = hosted(lambda r: proj_residual_norm(a0, w_d0, x1, zero_bias, gain(norm_ffn_post, 0), gain(norm_mix_pre, 1),
                                                     name="ffn_down_norm_0", riders=r),
                        [("d2d", ("win", "wout")), ("ici", (("wgu1", 4, 8, 8),))])
    w_in = full["win"]
    z, y = hosted(lambda r: sgu_in_fwd(h2, w_in, ln_g, ln_b, w_sp, b_sp, name="sgu_in_fwd", riders=r),
                  [("d2d", ("wgu1",)), ("ici", ("wd1",))])
    w_out = full["wout"].reshape(D_MODEL, D_MODEL)
    x3, h3, m1 = hosted(lambda r: proj_residual_norm(y, w_out, x2, zero_bias, gain(norm_mix_post, 1), gain(norm_ffn_pre, 1),
                                                     name="sgu_out_norm", riders=r),
                        [("d2d", ("wd1",))])
    w_qkv, w_gu0, w_gu1 = full["qkv"], full["wgu0"], full["wgu1"]
    w_d1 = full["wd1"].reshape(D_FF, D_MODEL)
    gu1, a1, dx4, df1, dg_fpost1, loss_part = ffn_fwd_loss_rows(
        h3, w_gu1, w_d1, x3, gain(norm_ffn_post, 1), target, name="ffn_fwd_loss_rows")

    red = _GradReduce(c_arr, jc_arr, {
        "qkv": attn_w_qkv.shape[1:], "wo": attn_w_o.shape[1:], "win": sgu_w_in.shape[1:], "wout": sgu_w_out.shape[1:],
        "wgu": ffn_w_gate_up.shape, "wd": ffn_w_down.shape, "slab": (N_CHIPS, SLAB_ROWS // N_CHIPS, D_MODEL)})
    where = {"qkv": ("qkv", None), "wo": ("wo", None), "win": ("win", None), "wout": ("wout", None), "wgu0": ("wgu", 0),
             "wgu1": ("wgu", 1), "wd0": ("wd", 0), "wd1": ("wd", 1), "small": ("slab", "chip")}

    dgu1, dx3, dm1, dg_fpre1, dg_mpost1, _ = ffn_bwd_rows(
        df1, w_d1, gu1, w_gu1, dx4, x3, gain(norm_ffn_pre, 1), m1, gain(norm_mix_post, 1), name="ffn_bwd_rows_1")
    red.grad["wd1"] = mm_tn(a1, df1, shard_major=False, tm=256, tn=D_MODEL, name="dw_down_1").reshape(
        N_CHIPS, D_FF // N_CHIPS, D_MODEL)
    red.grad["wgu1"], (res,) = mm_tn(h3, dgu1, shard_major=True, tm=512, tn=FF_HALF, name="dw_gate_up_1",
                                     riders=[red.exchange(["wd1"])])
    red.exchanged(["wd1"], res)
    dy, (res,) = mm_nt(dm1, w_out, out_dtype=F32, name="dy_sgu", riders=[red.exchange(["wgu1"])])
    red.exchanged(["wgu1"], res)
    red.grad["wout"] = mm_tn(y, dm1, shard_major=False, tm=512, tn=D_MODEL, name="dw_sgu_out").reshape(
        N_CHIPS, D_MODEL // N_CHIPS, D_MODEL)
    (dz, dw_sp, db_sp, dln_g, dln_b), (res_a, res_b) = sgu_bwd(
        z, dy, ln_g, ln_b, w_sp, b_sp, name="sgu_bwd", riders=[red.scatter(["wd1"]), red.exchange(["wout"])])
    red.scattered(["wd1"], res_a, where)
    red.exchanged(["wout"], res_b)
    names, rider = red.broadcast([("wd", 1)])
    red.grad["win"], (res_a, res_b) = mm_tn(h2, dz, shard_major=True, tm=D_MODEL, tn=2 * D_MODEL // N_CHIPS, name="dw_sgu_in",
                                            riders=[rider, red.scatter(["wout"])])
    red.broadcasted(names, res_a)
    red.scattered(["wout"], res_b, where)
    names, rider = red.broadcast([("wout", None)])
    (dx2, df0, dg_mpre1, dg_fpost0, _), (res_a, res_b) = dh_norm_bwd_pair(
        dz, w_in, dx3, x2, gain(norm_mix_pre, 1), f0, gain(norm_ffn_post, 0), name="dh_sgu_norm",
        riders=[red.exchange(["win"]), rider])
    red.exchanged(["win"], res_a)
    red.broadcasted(names, res_b)
    (dgu0, dx1, dm0, dg_fpre0, dg_mpost0, db_o), (res,) = ffn_bwd_rows(
        df0, w_d0, gu0, w_gu0, dx2, x1, gain(norm_ffn_pre, 0), m0, gain(norm_mix_post, 0), name="ffn_bwd_rows_0",
        riders=[red.scatter(["wgu1", "win"])])
    red.scattered(["wgu1", "win"], res, where)
    names, rider = red.broadcast([("wgu", 1), ("win", None)])
    dw_d0, (res,) = mm_tn(a0, df0, shard_major=False, tm=256, tn=D_MODEL, name="dw_down_0", riders=[rider])
    red.broadcasted(names, res)
    red.grad["wd0"] = dw_d0.reshape(N_CHIPS, D_FF // N_CHIPS, D_MODEL)
    do, (res,) = mm_nt(dm0, w_o, out_dtype=BF16, name="do_attn", riders=[red.exchange(["wd0"])])
    red.exchanged(["wd0"], res)
    red.grad["wgu0"], (res,) = mm_tn(h1, dgu0, shard_major=True, tm=512, tn=FF_HALF, name="dw_gate_up_0",
                                     riders=[red.scatter(["wd0"])])
    red.scattered(["wd0"], res, where)
    names, rider = red.broadcast([("wd", 0)])
    dw_o, (res_a, res_b) = mm_tn(o, dm0, shard_major=False, tm=512, tn=D_MODEL, name="dw_attn_out",
                                 riders=[red.exchange(["wgu0"]), rider])
    red.exchanged(["wgu0"], res_a)
    red.broadcasted(names, res_b)
    red.grad["wo"] = dw_o.reshape(N_CHIPS, Q_WIDTH // N_CHIPS, D_MODEL)
    (dq, dkc, dkp, dvc, dvp, dsink), (res_a, res_b) = attn_bwd(
        qkv, sink_rows, do, name="attn_bwd", riders=[red.scatter(["wgu0"]), red.exchange(["wo"])])
    red.scattered(["wgu0"], res_a, where)
    red.exchanged(["wo"], res_b)
    names, rider = red.broadcast([("wgu", 0)])
    (dqkv, db_qkv), (res,) = rope_bwd(dq, dkc, dkp, dvc, dvp, cos, sin, name="rope_bwd", riders=[rider])
    red.broadcasted(names, res)
    red.grad["qkv"], (res,) = mm_tn(h0, dqkv, shard_major=True, tm=D_MODEL, tn=QKV_WIDTH // N_CHIPS, name="dw_qkv",
                                    riders=[red.scatter(["wo"])])
    red.scattered(["wo"], res, where)
    grad_x, dg_mpre0 = dh_norm_bwd_last(dqkv, w_qkv, dx1, x0, gain(norm_mix_pre, 0), name="dh_attn_norm_in")

    norm_grads = [jnp.concatenate(p, axis=0) for p in
                  ((dg_mpre0, dg_mpre1), (dg_mpost0, dg_mpost1), (dg_fpre0, dg_fpre1), (dg_fpost0, dg_fpost1))]
    red.grad["small"] = _pack_small(norm_grads, db_qkv, db_o, dsink[:, :, 0, 0], db_sp[:, :, 0], dln_g, dln_b, dw_sp,
                                    loss_part)
    def big_update(w, g, m, v, tag, after=None):
        return adamw(w, g.reshape(w.shape), m, v, name=f"adamw_{tag}", after=after)

    (res,) = comm_call([red.exchange(["qkv", "small"])], name="tail_1")
    red.exchanged(["qkv", "small"], res)
    state, token = comm_start([red.scatter(["qkv", "small"])], name="tail_2_start")
    upd_wgu = big_update(ffn_w_gate_up, red.dest["wgu"], m_ffn_w_gate_up, v_ffn_w_gate_up, "wgu", after=token)
    (res,) = comm_wait(state, upd_wgu[1], name="tail_2_wait")
    red.scattered(["qkv", "small"], res, where)
    names, rider = red.broadcast([("qkv", None), ("wo", None)])
    state, token = comm_start([rider, allcast_rider(red.dest["slab"])], name="tail_3_start")
    upd_wd = big_update(ffn_w_down, red.dest["wd"], m_ffn_w_down, v_ffn_w_down, "wd", after=token)
    res, (slab_full,) = comm_wait(state, upd_wd[1], name="tail_3_wait")
    red.broadcasted(names, res)
    g_qkv, g_wo, g_win, g_wout = (red.dest[n] for n in ("qkv", "wo", "win", "wout"))
    g_norms, g_bqkv, g_bo, g_sinks, g_bsp, g_lng, g_lnb, g_wsp, loss = _unpack_small(slab_full, chip)

    upd = {
        "attn_w_qkv": big_update(attn_w_qkv, g_qkv, m_attn_w_qkv, v_attn_w_qkv, "qkv"),
        "attn_w_o": big_update(attn_w_o, g_wo, m_attn_w_o, v_attn_w_o, "wo"),
        "sgu_w_in": big_update(sgu_w_in, g_win, m_sgu_w_in, v_sgu_w_in, "win"),
        "sgu_w_out": big_update(sgu_w_out, g_wout, m_sgu_w_out, v_sgu_w_out, "wout"),
        "ffn_w_gate_up": upd_wgu,
        "ffn_w_down": upd_wd,
    }
    small_names = ["norm_mix_pre", "norm_mix_post", "norm_ffn_pre", "norm_ffn_post", "attn_b_qkv", "attn_sinks", "attn_b_o",
                   "sgu_ln_g", "sgu_ln_b", "sgu_w_spatial", "sgu_b_spatial"]
    small_w = [norm_mix_pre, norm_mix_post, norm_ffn_pre, norm_ffn_post, attn_b_qkv, attn_sinks, attn_b_o, sgu_ln_g, sgu_ln_b,
               sgu_w_spatial, sgu_b_spatial]
    small_m = [m_norm_mix_pre, m_norm_mix_post, m_norm_ffn_pre, m_norm_ffn_post, m_attn_b_qkv, m_attn_sinks, m_attn_b_o,
               m_sgu_ln_g, m_sgu_ln_b, m_sgu_w_spatial, m_sgu_b_spatial]
    small_v = [v_norm_mix_pre, v_norm_mix_post, v_norm_ffn_pre, v_norm_ffn_post, v_attn_b_qkv, v_attn_sinks, v_attn_b_o,
               v_sgu_ln_g, v_sgu_ln_b, v_sgu_w_spatial, v_sgu_b_spatial]
    small_g = g_norms + [g_bqkv, g_sinks, g_bo, g_lng, g_lnb, g_wsp, g_bsp]

    def flat2(a):
        return a.reshape(-1, a.shape[-1])

    res = adamw_small([flat2(a) for a in small_w], [flat2(a) for a in small_g], [flat2(a) for a in small_m],
                      [flat2(a) for a in small_v], name="adamw_small")
    for i, nm in enumerate(small_names):
        upd[nm] = tuple(r[i].reshape(small_w[i].shape) for r in res)

    order = ["norm_mix_pre", "norm_mix_post", "norm_ffn_pre", "norm_ffn_post", "attn_w_qkv", "attn_b_qkv", "attn_sinks",
             "attn_w_o", "attn_b_o", "sgu_w_in", "sgu_ln_g", "sgu_ln_b", "sgu_w_spatial", "sgu_b_spatial", "sgu_w_out",
             "ffn_w_gate_up", "ffn_w_down"]
    outs = [loss, grad_x.reshape(1, s, D_MODEL)]
    for part in range(4):
        outs += [upd[nm][part] for nm in order]
    return tuple(outs)
```
